```python
import math
import jax, jax.numpy as jnp
from jax import lax
import numpy as np

D_MODEL = 1024
BATCH = 8
SEQ = 8192
DEPTH = 1

MIX_WIDTH = D_MODEL
HEAD_DIM = 64
ATTN_WIDTH = MIX_WIDTH // 2
CONV_WIDTH = MIX_WIDTH - ATTN_WIDTH
N_HEADS = ATTN_WIDTH // HEAD_DIM
N_KV_HEADS = 2
GROUP = N_HEADS // N_KV_HEADS
KV_WIDTH = N_KV_HEADS * HEAD_DIM
CONV_GROUPS = CONV_WIDTH // HEAD_DIM
CONV_K = 3
WINDOW = 128
BLOCK = 128
ROPE_THETA = 500000.0
ROT_DIM = HEAD_DIM // 4
D_FF = 4 * D_MODEL
NORM_EPS = 1e-6
NEG_INF = -1e30
SPLIT_SIZES = (ATTN_WIDTH, KV_WIDTH, KV_WIDTH, CONV_WIDTH, CONV_WIDTH, CONV_WIDTH)
IN_COLS = sum(SPLIT_SIZES)

kernel_name = "hymba_swa_sink_shortconv_sqrelu_sandwich"


def _rmsnorm(x, gain):
    x32 = x.astype(jnp.float32)
    y = x32 * lax.rsqrt(jnp.mean(x32 * x32, axis=-1, keepdims=True) + NORM_EPS)
    return (y * gain.astype(jnp.float32)).astype(x.dtype)


def _partial_rope(t, cos, sin):
    half = ROT_DIM // 2
    t1 = t[..., :half]
    t2 = t[..., half:ROT_DIM]
    rot = jnp.concatenate([t1 * cos - t2 * sin, t2 * cos + t1 * sin], axis=-1)
    return jnp.concatenate([rot, t[..., ROT_DIM:]], axis=-1)


def _rope_tables(seq, dtype):
    pos = jnp.arange(seq, dtype=jnp.float32)
    inv_freq = ROPE_THETA ** (-jnp.arange(0, ROT_DIM, 2, dtype=jnp.float32) / ROT_DIM)
    ang = pos[:, None] * inv_freq[None, :]
    cos = jnp.cos(ang)[None, :, None, :].astype(dtype)
    sin = jnp.sin(ang)[None, :, None, :].astype(dtype)
    return cos, sin


def _sliding_window_attention(q, k, v, sinks):
    b, s = q.shape[0], q.shape[1]
    nb = s // BLOCK
    qb = q.reshape(b, nb, BLOCK, N_KV_HEADS, GROUP, HEAD_DIM)

    def band(t):
        tp = jnp.pad(t, ((0, 0), (BLOCK, 0), (0, 0), (0, 0)))
        tp = tp.reshape(b, nb + 1, BLOCK, N_KV_HEADS, HEAD_DIM)
        return jnp.concatenate([tp[:, :-1], tp[:, 1:]], axis=2)

    kb, vb = band(k), band(v)
    scale = 1.0 / math.sqrt(HEAD_DIM)
    scores = jnp.einsum('bnqkgd,bnskd->bnkgqs', qb, kb).astype(jnp.float32) * scale

    qi = jnp.arange(BLOCK)[:, None]
    kj = jnp.arange(2 * BLOCK)[None, :]
    dist = BLOCK + qi - kj
    in_win = (dist >= 0) & (dist < WINDOW)
    key_abs = jnp.arange(nb)[:, None, None] * BLOCK - BLOCK + kj[None]
    valid = in_win[None] & (key_abs >= 0)
    scores = jnp.where(valid[None, :, None, None], scores, NEG_INF)

    sink = jnp.broadcast_to(sinks.astype(jnp.float32).reshape(1, 1, N_KV_HEADS, GROUP, 1, 1),
                            scores.shape[:-1] + (1,))
    probs = jax.nn.softmax(jnp.concatenate([scores, sink], axis=-1), axis=-1)[..., :-1]
    out = jnp.einsum('bnkgqs,bnskd->bnqkgd', probs.astype(v.dtype), vb)
    return out.reshape(b, s, N_HEADS * HEAD_DIM)


def _short_conv(u, w):
    s = u.shape[1]
    up = jnp.pad(u, ((0, 0), (CONV_K - 1, 0), (0, 0)))
    return sum(up[:, j:j + s, :] * w[j] for j in range(CONV_K))


def _fwd_setup_inputs(seed: int = 0) -> dict:
    key = jax.random.key(seed)
    ks = jax.random.split(key, 16)
    f32 = jnp.float32

    def gain(k, n):
        return 1.0 + 0.02 * jax.random.normal(k, (DEPTH, n), f32)

    return {
        "x": jax.random.normal(ks[0], (BATCH, SEQ, D_MODEL), f32),
        "pre_mix_norm": gain(ks[1], D_MODEL),
        "w_in": jax.random.normal(ks[2], (DEPTH, D_MODEL, IN_COLS), f32) * D_MODEL ** -0.5,
        "conv_w": jax.random.normal(ks[3], (DEPTH, CONV_K, CONV_WIDTH), f32) * CONV_K ** -0.5,
        "attn_sinks": 0.5 * jax.random.normal(ks[4], (DEPTH, N_HEADS), f32),
        "attn_group_norm": gain(ks[5], ATTN_WIDTH),
        "conv_group_norm": gain(ks[6], CONV_WIDTH),
        "w_out": jax.random.normal(ks[7], (DEPTH, MIX_WIDTH, D_MODEL), f32) * MIX_WIDTH ** -0.5,
        "post_mix_norm": gain(ks[8], D_MODEL),
        "pre_mlp_norm": gain(ks[9], D_MODEL),
        "w_up": jax.random.normal(ks[10], (DEPTH, D_MODEL, D_FF), f32) * D_MODEL ** -0.5,
        "w_down": jax.random.normal(ks[11], (DEPTH, D_FF, D_MODEL), f32) * D_FF ** -0.5,
        "post_mlp_norm": gain(ks[12], D_MODEL),
    }


def _fwd_reference(x, pre_mix_norm, w_in, conv_w, attn_sinks, attn_group_norm, conv_group_norm,
              w_out, post_mix_norm, pre_mlp_norm, w_up, w_down, post_mlp_norm):
    b, s, _ = x.shape
    cos, sin = _rope_tables(s, x.dtype)
    split_idx = list(np.cumsum(SPLIT_SIZES)[:-1])
    h = x
    for l in range(DEPTH):
        hn = _rmsnorm(h, pre_mix_norm[l])
        proj = jnp.einsum('bsd,de->bse', hn, w_in[l])
        q, k, v, gb, gc, xin = jnp.split(proj, split_idx, axis=-1)

        q = _partial_rope(q.reshape(b, s, N_HEADS, HEAD_DIM), cos, sin)
        k = _partial_rope(k.reshape(b, s, N_KV_HEADS, HEAD_DIM), cos, sin)
        v = v.reshape(b, s, N_KV_HEADS, HEAD_DIM)
        attn = _sliding_window_attention(q, k, v, attn_sinks[l])

        conv = gb * _short_conv(gc * xin, conv_w[l])

        mixed = jnp.concatenate([_rmsnorm(attn, attn_group_norm[l]),
                                 _rmsnorm(conv, conv_group_norm[l])], axis=-1)
        mix_out = jnp.einsum('bse,ed->bsd', mixed, w_out[l])
        h = h + _rmsnorm(mix_out, post_mix_norm[l])

        hn = _rmsnorm(h, pre_mlp_norm[l])
        up = jax.nn.relu(jnp.einsum('bsd,df->bsf', hn, w_up[l]))
        mlp_out = jnp.einsum('bsf,fd->bsd', up * up, w_down[l])
        h = h + _rmsnorm(mlp_out, post_mlp_norm[l])
    return h


import jax as _jax
import jax.numpy as _jnp

TWIN_FORMAT = 'train_step'
FWD_PARAMS = ['x', 'pre_mix_norm', 'w_in', 'conv_w', 'attn_sinks', 'attn_group_norm', 'conv_group_norm', 'w_out', 'post_mix_norm', 'pre_mlp_norm', 'w_up', 'w_down', 'post_mlp_norm']
TWIN_WEIGHTS = ['pre_mix_norm', 'w_in', 'conv_w', 'attn_sinks', 'attn_group_norm', 'conv_group_norm', 'w_out', 'post_mix_norm', 'pre_mlp_norm', 'w_up', 'w_down', 'post_mlp_norm']
TWIN_DIFF_INPUT = 'x'
TWIN_INPUTS = ['x', 'pre_mix_norm', 'w_in', 'conv_w', 'attn_sinks', 'attn_group_norm', 'conv_group_norm', 'w_out', 'post_mix_norm', 'pre_mlp_norm', 'w_up', 'w_down', 'post_mlp_norm', 'loss_target', 'm_pre_mix_norm', 'm_w_in', 'm_conv_w', 'm_attn_sinks', 'm_attn_group_norm', 'm_conv_group_norm', 'm_w_out', 'm_post_mix_norm', 'm_pre_mlp_norm', 'm_w_up', 'm_w_down', 'm_post_mlp_norm', 'v_pre_mix_norm', 'v_w_in', 'v_conv_w', 'v_attn_sinks', 'v_attn_group_norm', 'v_conv_group_norm', 'v_w_out', 'v_post_mix_norm', 'v_pre_mlp_norm', 'v_w_up', 'v_w_down', 'v_post_mlp_norm']
TWIN_OUTPUTS = ['loss', 'grad_x', 'grad_pre_mix_norm', 'grad_w_in', 'grad_conv_w', 'grad_attn_sinks', 'grad_attn_group_norm', 'grad_conv_group_norm', 'grad_w_out', 'grad_post_mix_norm', 'grad_pre_mlp_norm', 'grad_w_up', 'grad_w_down', 'grad_post_mlp_norm', 'delta_pre_mix_norm', 'delta_w_in', 'delta_conv_w', 'delta_attn_sinks', 'delta_attn_group_norm', 'delta_conv_group_norm', 'delta_w_out', 'delta_post_mix_norm', 'delta_pre_mlp_norm', 'delta_w_up', 'delta_w_down', 'delta_post_mlp_norm', 'new_m_pre_mix_norm', 'new_m_w_in', 'new_m_conv_w', 'new_m_attn_sinks', 'new_m_attn_group_norm', 'new_m_conv_group_norm', 'new_m_w_out', 'new_m_post_mix_norm', 'new_m_pre_mlp_norm', 'new_m_w_up', 'new_m_w_down', 'new_m_post_mlp_norm', 'new_v_pre_mix_norm', 'new_v_w_in', 'new_v_conv_w', 'new_v_attn_sinks', 'new_v_attn_group_norm', 'new_v_conv_group_norm', 'new_v_w_out', 'new_v_post_mix_norm', 'new_v_pre_mlp_norm', 'new_v_w_up', 'new_v_w_down', 'new_v_post_mlp_norm']
TWIN_LEAF_KINDS = {'loss': 'loss', 'grad_x': 'grad_x', 'grad_pre_mix_norm': 'grad_w', 'grad_w_in': 'grad_w', 'grad_conv_w': 'grad_w', 'grad_attn_sinks': 'grad_w', 'grad_attn_group_norm': 'grad_w', 'grad_conv_group_norm': 'grad_w', 'grad_w_out': 'grad_w', 'grad_post_mix_norm': 'grad_w', 'grad_pre_mlp_norm': 'grad_w', 'grad_w_up': 'grad_w', 'grad_w_down': 'grad_w', 'grad_post_mlp_norm': 'grad_w', 'delta_pre_mix_norm': 'delta_w', 'delta_w_in': 'delta_w', 'delta_conv_w': 'delta_w', 'delta_attn_sinks': 'delta_w', 'delta_attn_group_norm': 'delta_w', 'delta_conv_group_norm': 'delta_w', 'delta_w_out': 'delta_w', 'delta_post_mix_norm': 'delta_w', 'delta_pre_mlp_norm': 'delta_w', 'delta_w_up': 'delta_w', 'delta_w_down': 'delta_w', 'delta_post_mlp_norm': 'delta_w', 'new_m_pre_mix_norm': 'new_m', 'new_m_w_in': 'new_m', 'new_m_conv_w': 'new_m', 'new_m_attn_sinks': 'new_m', 'new_m_attn_group_norm': 'new_m', 'new_m_conv_group_norm': 'new_m', 'new_m_w_out': 'new_m', 'new_m_post_mix_norm': 'new_m', 'new_m_pre_mlp_norm': 'new_m', 'new_m_w_up': 'new_m', 'new_m_w_down': 'new_m', 'new_m_post_mlp_norm': 'new_m', 'new_v_pre_mix_norm': 'new_v', 'new_v_w_in': 'new_v', 'new_v_conv_w': 'new_v', 'new_v_attn_sinks': 'new_v', 'new_v_attn_group_norm': 'new_v', 'new_v_conv_group_norm': 'new_v', 'new_v_w_out': 'new_v', 'new_v_post_mix_norm': 'new_v', 'new_v_pre_mlp_norm': 'new_v', 'new_v_w_up': 'new_v', 'new_v_w_down': 'new_v', 'new_v_post_mlp_norm': 'new_v'}


def _forward(args):
    return _fwd_reference(*[args[k] for k in FWD_PARAMS])


def _output_shape():
    out = _jax.eval_shape(lambda: _forward(_fwd_setup_inputs(0)))
    return out.shape, out.dtype

N_MICROBATCH = 1
ADAM_LR = 0.001
ADAM_B1 = 0.9
ADAM_B2 = 0.999
ADAM_EPS = 1e-08
ADAM_WD = 0.01
ADAM_STEP = 10
PER_EXAMPLE_BATCH_AXIS = {'x': 0, 'loss_target': 0}
SHARED_INPUTS = []
_WEIGHT_DTYPES = {'pre_mix_norm': _jnp.float32, 'w_in': _jnp.float32, 'conv_w': _jnp.float32, 'attn_sinks': _jnp.float32, 'attn_group_norm': _jnp.float32, 'conv_group_norm': _jnp.float32, 'w_out': _jnp.float32, 'post_mix_norm': _jnp.float32, 'pre_mlp_norm': _jnp.float32, 'w_up': _jnp.float32, 'w_down': _jnp.float32, 'post_mlp_norm': _jnp.float32}
MOMENT_SCALE = {'pre_mix_norm': 1.352778e+00, 'w_in': 8.109687e-01, 'conv_w': 7.520126e-01, 'attn_sinks': 2.005270e-01, 'attn_group_norm': 1.476387e+00, 'conv_group_norm': 1.242251e+00, 'w_out': 1.101342e+00, 'post_mix_norm': 6.356116e+01, 'pre_mlp_norm': 1.243251e+00, 'w_up': 6.062290e-01, 'w_down': 1.255111e+00, 'post_mlp_norm': 6.617089e+01}


def _to_microbatches(a, axis):
    t = _jnp.moveaxis(a, axis, 0)
    t = t.reshape((N_MICROBATCH, t.shape[0] // N_MICROBATCH) + t.shape[1:])
    return _jnp.moveaxis(t, 1, axis + 1)


def setup_inputs(seed: int = 0) -> dict:
    inp = _fwd_setup_inputs(seed)
    key = _jax.random.fold_in(_jax.random.key(seed), 7919)
    shape, _ = _output_shape()
    out = dict(inp)
    out["loss_target"] = _jax.random.normal(_jax.random.fold_in(key, 0), shape, _jnp.float32)
    for i, name in enumerate(TWIN_WEIGHTS):
        w = inp[name].astype(_jnp.float32)
        if MOMENT_SCALE is None:
            s = _jnp.sqrt(_jnp.mean(_jnp.square(w)) + 1e-30)
        else:
            s = MOMENT_SCALE[name]
        km, kv = _jax.random.split(_jax.random.fold_in(key, i + 1))
        out[name] = w
        out["m_" + name] = s * _jax.random.normal(km, w.shape, _jnp.float32)
        out["v_" + name] = (s * s) * _jax.random.uniform(kv, w.shape, _jnp.float32, 0.5, 1.5)
    if N_MICROBATCH > 1:
        for name, axis in PER_EXAMPLE_BATCH_AXIS.items():
            out[name] = _to_microbatches(out[name], axis)
    return {'x': out['x'], 'pre_mix_norm': out['pre_mix_norm'], 'w_in': out['w_in'], 'conv_w': out['conv_w'], 'attn_sinks': out['attn_sinks'], 'attn_group_norm': out['attn_group_norm'], 'conv_group_norm': out['conv_group_norm'], 'w_out': out['w_out'], 'post_mix_norm': out['post_mix_norm'], 'pre_mlp_norm': out['pre_mlp_norm'], 'w_up': out['w_up'], 'w_down': out['w_down'], 'post_mlp_norm': out['post_mlp_norm'], 'loss_target': out['loss_target'], 'm_pre_mix_norm': out['m_pre_mix_norm'], 'm_w_in': out['m_w_in'], 'm_conv_w': out['m_conv_w'], 'm_attn_sinks': out['m_attn_sinks'], 'm_attn_group_norm': out['m_attn_group_norm'], 'm_conv_group_norm': out['m_conv_group_norm'], 'm_w_out': out['m_w_out'], 'm_post_mix_norm': out['m_post_mix_norm'], 'm_pre_mlp_norm': out['m_pre_mlp_norm'], 'm_w_up': out['m_w_up'], 'm_w_down': out['m_w_down'], 'm_post_mlp_norm': out['m_post_mlp_norm'], 'v_pre_mix_norm': out['v_pre_mix_norm'], 'v_w_in': out['v_w_in'], 'v_conv_w': out['v_conv_w'], 'v_attn_sinks': out['v_attn_sinks'], 'v_attn_group_norm': out['v_attn_group_norm'], 'v_conv_group_norm': out['v_conv_group_norm'], 'v_w_out': out['v_w_out'], 'v_post_mix_norm': out['v_post_mix_norm'], 'v_pre_mlp_norm': out['v_pre_mlp_norm'], 'v_w_up': out['v_w_up'], 'v_w_down': out['v_w_down'], 'v_post_mlp_norm': out['v_post_mlp_norm']}


def _loss(weights, diff, rest, loss_target):
    with _jax.named_scope("forward"):
        args = {**rest, TWIN_DIFF_INPUT: diff, **{k: w.astype(_WEIGHT_DTYPES[k]) for k, w in weights.items()}}
        y = _forward(args)
    with _jax.named_scope("loss_head"):
        err = _jnp.square(y.astype(_jnp.float32) - loss_target)
        return 0.5 * _jnp.sum(_jnp.mean(err, axis=-1)) if err.ndim else 0.5 * err


def _adamw(w, g, m, v):
    m = ADAM_B1 * m + (1.0 - ADAM_B1) * g
    v = ADAM_B2 * v + (1.0 - ADAM_B2) * _jnp.square(g)
    m_hat = m / (1.0 - ADAM_B1 ** ADAM_STEP)
    v_hat = v / (1.0 - ADAM_B2 ** ADAM_STEP)
    delta = -ADAM_LR * (m_hat / (_jnp.sqrt(v_hat) + ADAM_EPS) + ADAM_WD * w)
    return delta, m, v


def reference(x, pre_mix_norm, w_in, conv_w, attn_sinks, attn_group_norm, conv_group_norm, w_out, post_mix_norm, pre_mlp_norm, w_up, w_down, post_mlp_norm, loss_target, m_pre_mix_norm, m_w_in, m_conv_w, m_attn_sinks, m_attn_group_norm, m_conv_group_norm, m_w_out, m_post_mix_norm, m_pre_mlp_norm, m_w_up, m_w_down, m_post_mlp_norm, v_pre_mix_norm, v_w_in, v_conv_w, v_attn_sinks, v_attn_group_norm, v_conv_group_norm, v_w_out, v_post_mix_norm, v_pre_mlp_norm, v_w_up, v_w_down, v_post_mlp_norm):
    given = dict(x=x, pre_mix_norm=pre_mix_norm, w_in=w_in, conv_w=conv_w, attn_sinks=attn_sinks, attn_group_norm=attn_group_norm, conv_group_norm=conv_group_norm, w_out=w_out, post_mix_norm=post_mix_norm, pre_mlp_norm=pre_mlp_norm, w_up=w_up, w_down=w_down, post_mlp_norm=post_mlp_norm, loss_target=loss_target, m_pre_mix_norm=m_pre_mix_norm, m_w_in=m_w_in, m_conv_w=m_conv_w, m_attn_sinks=m_attn_sinks, m_attn_group_norm=m_attn_group_norm, m_conv_group_norm=m_conv_group_norm, m_w_out=m_w_out, m_post_mix_norm=m_post_mix_norm, m_pre_mlp_norm=m_pre_mlp_norm, m_w_up=m_w_up, m_w_down=m_w_down, m_post_mlp_norm=m_post_mlp_norm, v_pre_mix_norm=v_pre_mix_norm, v_w_in=v_w_in, v_conv_w=v_conv_w, v_attn_sinks=v_attn_sinks, v_attn_group_norm=v_attn_group_norm, v_conv_group_norm=v_conv_group_norm, v_w_out=v_w_out, v_post_mix_norm=v_post_mix_norm, v_pre_mlp_norm=v_pre_mlp_norm, v_w_up=v_w_up, v_w_down=v_w_down, v_post_mlp_norm=v_post_mlp_norm)
    weights = {n: given[n] for n in TWIN_WEIGHTS}
    shared = {n: given[n] for n in SHARED_INPUTS}
    per_example = {n: given[n] for n in ['x']}
    grad_fn = _jax.value_and_grad(_loss, argnums=(0, 1))

    def one_microbatch(ex, loss_target):
        ex = dict(ex)
        diff = ex.pop(TWIN_DIFF_INPUT)
        return grad_fn(weights, diff, {**shared, **ex}, loss_target)

    if N_MICROBATCH == 1:
        loss, (grad_w, grad_x) = one_microbatch(per_example, given["loss_target"])
    else:
        def body(carry, xs):
            loss_sum, grad_sum = carry
            l_k, (gw_k, gx_k) = one_microbatch(xs[0], xs[1])
            with _jax.named_scope("update"):
                return (loss_sum + l_k, _jax.tree.map(_jnp.add, grad_sum, gw_k)), gx_k

        init = (_jnp.zeros((), _jnp.float32), _jax.tree.map(_jnp.zeros_like, weights))
        (loss, grad_w), grad_x = _jax.lax.scan(body, init, (per_example, given["loss_target"]))
    with _jax.named_scope("update"):
        delta_w, new_m, new_v = {}, {}, {}
        for n in TWIN_WEIGHTS:
            delta_w[n], new_m[n], new_v[n] = _adamw(weights[n], grad_w[n], given["m_" + n], given["v_" + n])
    return (loss, grad_x, *[grad_w[n] for n in TWIN_WEIGHTS], *[delta_w[n] for n in TWIN_WEIGHTS],
            *[new_m[n] for n in TWIN_WEIGHTS], *[new_v[n] for n in TWIN_WEIGHTS])
```

```python
import functools

import jax
import jax.numpy as jnp
from jax import lax
from jax.experimental import pallas as pl
from jax.experimental.pallas import tpu as pltpu

F32 = jnp.float32
BF16 = jnp.bfloat16

D_MODEL = 1024
HEAD_DIM = 64
ATTN_W = 512
CONV_W = 512
N_HEADS = 8
N_KV = 2
GROUP = 4
KV_W = 128
QKV_W = ATTN_W + 2 * KV_W
GATES_W = 3 * CONV_W
IN_COLS = QKV_W + GATES_W
D_FF = 4096
FF_CHUNK = 512
N_FF_CHUNKS = D_FF // FF_CHUNK
BLOCK = 128
ROT_HALF = 8
ROPE_THETA = 500000.0
NORM_EPS = 1e-6
NEG_INF = -1e30
ATTN_SCALE = 0.125
N_DEV = 8
N_CHIPS = 4
IN_SHARD = IN_COLS // N_DEV

ADAM_LR = 0.001
ADAM_B1 = 0.9
ADAM_B2 = 0.999
ADAM_EPS = 1e-08
ADAM_WD = 0.01
ADAM_STEP = 10

V7X_VMEM_BYTES = 64 * 1024 * 1024
VMEM_LIMIT = V7X_VMEM_BYTES - 8 * 1024 * 1024

MESH = pl.DeviceIdType.MESH
HBM_SPEC = pl.BlockSpec(memory_space=pltpu.HBM)


def _params(*sem):
    return pltpu.CompilerParams(dimension_semantics=sem, vmem_limit_bytes=VMEM_LIMIT)


def _mm(a, b):
    return jnp.dot(a, b, preferred_element_type=F32)


def _mm_nt(a, b):
    return lax.dot_general(a, b, (((1,), (1,)), ((), ())), preferred_element_type=F32)


def _mm_tn(a, b):
    return lax.dot_general(a, b, (((0,), (0,)), ((), ())), preferred_element_type=F32)


def _inv_rms(x):
    return lax.rsqrt(jnp.mean(x * x, axis=-1, keepdims=True) + NORM_EPS)


def _rms_bwd(xhat, r, gain, dy):
    gy = dy * gain
    return r * (gy - xhat * jnp.mean(gy * xhat, axis=-1, keepdims=True)), dy * xhat


def _colsum(a):
    return jnp.sum(a, axis=0, keepdims=True)


def _full(shape):
    zeros = (0,) * len(shape)
    return pl.BlockSpec(shape, lambda *_: zeros)


def _resident(shape):
    zeros = (0,) * len(shape)
    return pl.BlockSpec(shape, lambda *_: zeros, pipeline_mode=pl.Buffered(1))


def _rope_tables(t):
    pos = jnp.arange(t, dtype=F32)
    inv_freq = ROPE_THETA ** (-jnp.arange(0, 2 * ROT_HALF, 2, dtype=F32) / (2 * ROT_HALF))
    ang = pos[:, None] * inv_freq[None, :]
    cos, sin = jnp.cos(ang), jnp.sin(ang)
    zeros8 = jnp.zeros((t, ROT_HALF), F32)
    rest = HEAD_DIM - 2 * ROT_HALF
    c_head = jnp.concatenate([cos, cos, jnp.ones((t, rest), F32)], axis=1)
    s1_head = jnp.concatenate([zeros8, sin, jnp.zeros((t, rest), F32)], axis=1)
    s2_head = jnp.concatenate([-sin, zeros8, jnp.zeros((t, rest), F32)], axis=1)
    two = lambda a: jnp.concatenate([a, a], axis=1)
    return two(c_head), two(s1_head), two(s2_head)


def _rope(v, c, s1, s2):
    return v * c + pltpu.roll(v, ROT_HALF, 1) * s1 + pltpu.roll(v, 128 - ROT_HALF, 1) * s2


def _rope_transpose(dv, c, s1, s2):
    return dv * c + pltpu.roll(dv * s1, 128 - ROT_HALF, 1) + pltpu.roll(dv * s2, ROT_HALF, 1)


def _shift_rows_down(u, prev, k):
    row = lax.broadcasted_iota(jnp.int32, u.shape, 0)
    out = pltpu.roll(u, k, 0)
    for r in range(k):
        out = jnp.where(row == r, prev[8 - k + r:8 - k + r + 1, :], out)
    return out


def _shift_rows_up(u, nxt, k):
    n = u.shape[0]
    row = lax.broadcasted_iota(jnp.int32, u.shape, 0)
    out = pltpu.roll(u, n - k, 0)
    for r in range(k):
        out = jnp.where(row == n - k + r, nxt[r:r + 1, :], out)
    return out


def _conv3(u, u1, u2, w):
    return (w[0:1, :] * u2 + w[1:2, :] * u1) + w[2:3, :] * u


def _in_proj_fwd(x, g1, w_in, conv_w, g_conv, rope, tm):
    t = x.shape[0]
    rc, rs1, rs2 = rope

    def body(x_ref, g1_ref, w_ref, cw_ref, gc_ref, c_ref, s1_ref, s2_ref, qkv_ref, gates_ref, mconv_ref, carry_ref):
        @pl.when(pl.program_id(0) == 0)
        def _():
            carry_ref[...] = jnp.zeros_like(carry_ref)

        xv = x_ref[...]
        hn = ((xv * _inv_rms(xv)) * g1_ref[...]).astype(BF16)
        proj = _mm(hn, w_ref[...])
        c, s1, s2 = c_ref[...], s1_ref[...], s2_ref[...]
        for ci in range((ATTN_W + KV_W) // 128):
            sl = slice(128 * ci, 128 * (ci + 1))
            qkv_ref[:, sl] = _rope(proj[:, sl], c, s1, s2).astype(BF16)
        qkv_ref[:, ATTN_W + KV_W:QKV_W] = proj[:, ATTN_W + KV_W:QKV_W].astype(BF16)
        gates = proj[:, QKV_W:]
        gates_ref[...] = gates
        gb, gcc, xin = gates[:, :CONV_W], gates[:, CONV_W:2 * CONV_W], gates[:, 2 * CONV_W:]
        u = gcc * xin
        prev = carry_ref[...]
        conv = gb * _conv3(u, _shift_rows_down(u, prev, 1), _shift_rows_down(u, prev, 2), cw_ref[...])
        carry_ref[...] = u[tm - 8:tm, :]
        mconv_ref[...] = ((conv * _inv_rms(conv)) * gc_ref[...]).astype(BF16)

    tile = lambda n: pl.BlockSpec((tm, n), lambda i: (i, 0))
    return pl.pallas_call(
        body, name="in_proj_fwd", grid=(t // tm,),
        in_specs=[tile(D_MODEL), _full((1, D_MODEL)), _full((D_MODEL, IN_COLS)), _full((3, CONV_W)), _full((1, CONV_W)),
                  tile(128), tile(128), tile(128)],
        out_specs=[tile(QKV_W), tile(GATES_W), tile(CONV_W)],
        out_shape=[jax.ShapeDtypeStruct((t, QKV_W), BF16), jax.ShapeDtypeStruct((t, GATES_W), F32),
                   jax.ShapeDtypeStruct((t, CONV_W), BF16)],
        scratch_shapes=[pltpu.VMEM((8, CONV_W), F32)],
        compiler_params=_params("arbitrary"),
    )(x, g1, w_in, conv_w, g_conv, rc, rs1, rs2)


def _attn_mask(j):
    row = lax.broadcasted_iota(jnp.int32, (BLOCK, 2 * BLOCK), 0)
    col = lax.broadcasted_iota(jnp.int32, (BLOCK, 2 * BLOCK), 1)
    return (col > row) & (col <= row + BLOCK) & ((col >= BLOCK) | (j > 0))


def _attn_probs(qh, kk, sink, valid):
    s = jnp.where(valid, _mm_nt(qh, kk) * ATTN_SCALE, NEG_INF)
    m = jnp.maximum(jnp.max(s, axis=-1, keepdims=True), sink)
    p = jnp.exp(s - m)
    psink = jnp.exp(sink - m)
    inv_l = 1.0 / (jnp.sum(p, axis=-1, keepdims=True) + psink)
    return p * inv_l, psink * inv_l


def _qkv_specs(order):
    prev = lambda i: jnp.maximum(order(i) - 1, 0)
    kcol, vcol = ATTN_W // KV_W, ATTN_W // KV_W + 1
    return [pl.BlockSpec((BLOCK, ATTN_W), lambda i: (order(i), 0)),
            pl.BlockSpec((BLOCK, KV_W), lambda i: (prev(i), kcol)), pl.BlockSpec((BLOCK, KV_W), lambda i: (order(i), kcol)),
            pl.BlockSpec((BLOCK, KV_W), lambda i: (prev(i), vcol)), pl.BlockSpec((BLOCK, KV_W), lambda i: (order(i), vcol))]


def _attn_fwd(qkv, sinks, g_attn):
    t = qkv.shape[0]

    def body(sink_ref, q_ref, kp_ref, kc_ref, vp_ref, vc_ref, ga_ref, attn_ref, mattn_ref):
        valid = _attn_mask(pl.program_id(0))
        kp, kc, vp, vc = kp_ref[...], kc_ref[...], vp_ref[...], vc_ref[...]
        outs = []
        for g in range(N_KV):
            gs = slice(HEAD_DIM * g, HEAD_DIM * (g + 1))
            kk = jnp.concatenate([kp[:, gs], kc[:, gs]], axis=0)
            vv = jnp.concatenate([vp[:, gs], vc[:, gs]], axis=0)
            for hh in range(GROUP):
                h = GROUP * g + hh
                probs, _ = _attn_probs(q_ref[:, HEAD_DIM * h:HEAD_DIM * (h + 1)], kk, sink_ref[0, h], valid)
                outs.append(_mm(probs.astype(BF16), vv))
        attn = jnp.concatenate(outs, axis=1)
        attn_ref[...] = attn
        mattn_ref[...] = ((attn * _inv_rms(attn)) * ga_ref[...]).astype(BF16)

    blk = pl.BlockSpec((BLOCK, ATTN_W), lambda j: (j, 0))
    return pl.pallas_call(
        body, name="attn_fwd", grid=(t // BLOCK,),
        in_specs=[pl.BlockSpec(memory_space=pltpu.SMEM)] + _qkv_specs(lambda j: j) + [_full((1, ATTN_W))],
        out_specs=[blk, blk],
        out_shape=[jax.ShapeDtypeStruct((t, ATTN_W), F32), jax.ShapeDtypeStruct((t, ATTN_W), BF16)],
        compiler_params=_params("arbitrary"),
    )(sinks, qkv, qkv, qkv, qkv, qkv, g_attn)


SMALL_ROWS = 8
ROW_LOSS, ROW_G2, ROW_G3, ROW_G4 = 0, 1, 2, 3


def _mid(mattn, mconv, x, target, g2, g3, g4, w_out, w_up, w_down, tm):
    t = x.shape[0]

    def body(ma_ref, mc_ref, x_ref, t_ref, g2_ref, g3_ref, g4_ref, wo_ref, wu_ref, wd_ref,
             act_ref, dup_ref, hn2_ref, dmo_ref, dmix_ref, dh_ref, dmixed_ref, small_ref, up_ref):
        @pl.when(pl.program_id(0) == 0)
        def _():
            small_ref[...] = jnp.zeros_like(small_ref)

        g2, g3, g4 = g2_ref[...], g3_ref[...], g4_ref[...]
        mix_out = _mm(ma_ref[...], wo_ref[0:ATTN_W, :]) + _mm(mc_ref[...], wo_ref[ATTN_W:, :])
        r2 = _inv_rms(mix_out)
        mo_hat = mix_out * r2
        h = x_ref[...] + mo_hat * g2
        r3 = _inv_rms(h)
        h_hat = h * r3
        hn2 = (h_hat * g3).astype(BF16)
        hn2_ref[...] = hn2
        mlp = jnp.zeros((tm, D_MODEL), F32)
        for j in range(N_FF_CHUNKS):
            up = jnp.maximum(_mm(hn2, wu_ref[j]), 0.0)
            up_ref[j] = up
            act = (up * up).astype(BF16)
            act_ref[:, FF_CHUNK * j:FF_CHUNK * (j + 1)] = act
            mlp = mlp + _mm(act, wd_ref[j])
        r4 = _inv_rms(mlp)
        ml_hat = mlp * r4
        err = (h + ml_hat * g4) - t_ref[...]
        d_out = err * (1.0 / D_MODEL)
        d_mlp, dg4 = _rms_bwd(ml_hat, r4, g4, d_out)
        dmo = d_mlp.astype(BF16)
        dmo_ref[...] = dmo
        dhn2 = jnp.zeros((tm, D_MODEL), F32)
        for j in range(N_FF_CHUNKS):
            dup = (_mm_nt(dmo, wd_ref[j]) * (2.0 * up_ref[j])).astype(BF16)
            dup_ref[:, FF_CHUNK * j:FF_CHUNK * (j + 1)] = dup
            dhn2 = dhn2 + _mm_nt(dup, wu_ref[j])
        dh_norm, dg3 = _rms_bwd(h_hat, r3, g3, dhn2)
        dh = d_out + dh_norm
        dh_ref[...] = dh
        d_mix, dg2 = _rms_bwd(mo_hat, r2, g2, dh)
        dmix = d_mix.astype(BF16)
        dmix_ref[...] = dmix
        dmixed_ref[...] = _mm_nt(dmix, wo_ref[...])
        small_ref[ROW_LOSS:ROW_LOSS + 1, :] += _colsum(err * err)
        small_ref[ROW_G2:ROW_G2 + 1, :] += _colsum(dg2)
        small_ref[ROW_G3:ROW_G3 + 1, :] += _colsum(dg3)
        small_ref[ROW_G4:ROW_G4 + 1, :] += _colsum(dg4)

    tile = lambda n: pl.BlockSpec((tm, n), lambda i: (i, 0))
    gain = _full((1, D_MODEL))
    return pl.pallas_call(
        body, name="mid_fwd_bwd", grid=(t // tm,),
        in_specs=[tile(ATTN_W), tile(CONV_W), tile(D_MODEL), tile(D_MODEL), gain, gain, gain,
                  _resident((D_MODEL, D_MODEL)), _resident((N_FF_CHUNKS, D_MODEL, FF_CHUNK)),
                  _resident((N_FF_CHUNKS, FF_CHUNK, D_MODEL))],
        out_specs=[tile(D_FF), tile(D_FF), tile(D_MODEL), tile(D_MODEL), tile(D_MODEL), tile(D_MODEL), tile(D_MODEL),
                   _full((SMALL_ROWS, D_MODEL))],
        out_shape=[jax.ShapeDtypeStruct((t, D_FF), BF16), jax.ShapeDtypeStruct((t, D_FF), BF16),
                   jax.ShapeDtypeStruct((t, D_MODEL), BF16), jax.ShapeDtypeStruct((t, D_MODEL), BF16),
                   jax.ShapeDtypeStruct((t, D_MODEL), BF16), jax.ShapeDtypeStruct((t, D_MODEL), F32),
                   jax.ShapeDtypeStruct((t, D_MODEL), F32), jax.ShapeDtypeStruct((SMALL_ROWS, D_MODEL), F32)],
        scratch_shapes=[pltpu.VMEM((N_FF_CHUNKS, tm, FF_CHUNK), F32)],
        compiler_params=_params("arbitrary"),
    )(mattn, mconv, x, target, g2, g3, g4, w_out, w_up, w_down)


def _dw_up(hn2, dup, tk):
    t = hn2.shape[0]

    def body(a_ref, b_ref, o_ref):
        @pl.when(pl.program_id(1) == 0)
        def _():
            o_ref[...] = jnp.zeros_like(o_ref)
        o_ref[...] += _mm_tn(a_ref[...], b_ref[...])

    return pl.pallas_call(
        body, name="dw_up", grid=(N_DEV, t // tk),
        in_specs=[pl.BlockSpec((tk, D_MODEL), lambda j, k: (k, 0)), pl.BlockSpec((tk, FF_CHUNK), lambda j, k: (k, j))],
        out_specs=pl.BlockSpec((None, None, D_MODEL, FF_CHUNK), lambda j, k: (j % 2, j // 2, 0, 0)),
        out_shape=jax.ShapeDtypeStruct((2, N_CHIPS, D_MODEL, FF_CHUNK), F32),
        compiler_params=_params("parallel", "arbitrary"),
    )(hn2, dup)


def _dw_down(act, dmo, tk):
    t = act.shape[0]

    def body(a_ref, b_ref, o_ref):
        @pl.when(pl.program_id(1) == 0)
        def _():
            o_ref[...] = jnp.zeros_like(o_ref)
        o_ref[...] += _mm_tn(a_ref[...], b_ref[...])

    return pl.pallas_call(
        body, name="dw_down", grid=(N_DEV, t // tk),
        in_specs=[pl.BlockSpec((tk, FF_CHUNK), lambda j, k: (k, j)), pl.BlockSpec((tk, D_MODEL), lambda j, k: (k, 0))],
        out_specs=pl.BlockSpec((None, None, FF_CHUNK, D_MODEL), lambda j, k: (j % 2, j // 2, 0, 0)),
        out_shape=jax.ShapeDtypeStruct((2, N_CHIPS, FF_CHUNK, D_MODEL), F32),
        compiler_params=_params("parallel", "arbitrary"),
    )(act, dmo)


def _dw_out(mattn, mconv, dmix, tk):
    t = dmix.shape[0]

    def body(ma_ref, mc_ref, b_ref, o_ref):
        @pl.when(pl.program_id(0) == 0)
        def _():
            o_ref[...] = jnp.zeros_like(o_ref)
        b = b_ref[...]
        o_ref[0:ATTN_W, :] += _mm_tn(ma_ref[...], b)
        o_ref[ATTN_W:, :] += _mm_tn(mc_ref[...], b)

    tile = lambda n: pl.BlockSpec((tk, n), lambda k: (k, 0))
    return pl.pallas_call(
        body, name="dw_out", grid=(t // tk,),
        in_specs=[tile(ATTN_W), tile(CONV_W), tile(D_MODEL)],
        out_specs=_full((D_MODEL, D_MODEL)),
        out_shape=jax.ShapeDtypeStruct((D_MODEL, D_MODEL), F32),
        compiler_params=_params("arbitrary"),
    )(mattn, mconv, dmix)


ROW_GATTN, ROW_GCONV, ROW_CW0 = 0, 1, 2


def _mix_bwd(dmixed, attn, gates, g_attn, g_conv, conv_w, tm):
    t = attn.shape[0]
    n = t // tm
    rev = lambda i: n - 1 - i

    def body(dm_ref, attn_ref, gates_ref, gprev_ref, ga_ref, gc_ref, cw_ref, dattn_ref, dgates_ref, small_ref, carry_ref):
        i = pl.program_id(0)

        @pl.when(i == 0)
        def _():
            small_ref[...] = jnp.zeros_like(small_ref)
            carry_ref[...] = jnp.zeros_like(carry_ref)

        dm = dm_ref[...]
        a = attn_ref[...]
        ra = _inv_rms(a)
        a_hat = a * ra
        dattn, dga = _rms_bwd(a_hat, ra, ga_ref[...], dm[:, :ATTN_W])
        dattn_ref[...] = dattn

        gates = gates_ref[...]
        gb, gcc, xin = gates[:, :CONV_W], gates[:, CONV_W:2 * CONV_W], gates[:, 2 * CONV_W:]
        u = gcc * xin
        gp = gprev_ref[...]
        uprev = jnp.where(rev(i) == 0, 0.0, gp[:, CONV_W:2 * CONV_W] * gp[:, 2 * CONV_W:])
        u1, u2 = _shift_rows_down(u, uprev, 1), _shift_rows_down(u, uprev, 2)
        w = cw_ref[...]
        c = _conv3(u, u1, u2, w)
        conv = gb * c
        rcv = _inv_rms(conv)
        c_hat = conv * rcv
        dconv, dgc = _rms_bwd(c_hat, rcv, gc_ref[...], dm[:, ATTN_W:])
        dc = dconv * gb
        nxt = carry_ref[...]
        du = (w[2:3, :] * dc + w[1:2, :] * _shift_rows_up(dc, nxt, 1)) + w[0:1, :] * _shift_rows_up(dc, nxt, 2)
        carry_ref[...] = dc[0:8, :]
        dgates_ref[:, :CONV_W] = (dconv * c).astype(BF16)
        dgates_ref[:, CONV_W:2 * CONV_W] = (du * xin).astype(BF16)
        dgates_ref[:, 2 * CONV_W:] = (du * gcc).astype(BF16)
        small_ref[ROW_GATTN:ROW_GATTN + 1, :] += _colsum(dga)
        small_ref[ROW_GCONV:ROW_GCONV + 1, :] += _colsum(dgc)
        small_ref[ROW_CW0:ROW_CW0 + 1, :] += _colsum(dc * u2)
        small_ref[ROW_CW0 + 1:ROW_CW0 + 2, :] += _colsum(dc * u1)
        small_ref[ROW_CW0 + 2:ROW_CW0 + 3, :] += _colsum(dc * u)

    tile = lambda w_: pl.BlockSpec((tm, w_), lambda i: (rev(i), 0))
    prev8 = pl.BlockSpec((8, GATES_W), lambda i: (jnp.maximum(rev(i) * (tm // 8) - 1, 0), 0))
    return pl.pallas_call(
        body, name="mix_bwd", grid=(n,),
        in_specs=[tile(D_MODEL), tile(ATTN_W), tile(GATES_W), prev8, _full((1, ATTN_W)), _full((1, CONV_W)),
                  _full((3, CONV_W))],
        out_specs=[tile(ATTN_W), tile(GATES_W), _full((SMALL_ROWS, CONV_W))],
        out_shape=[jax.ShapeDtypeStruct((t, ATTN_W), F32), jax.ShapeDtypeStruct((t, GATES_W), BF16),
                   jax.ShapeDtypeStruct((SMALL_ROWS, CONV_W), F32)],
        scratch_shapes=[pltpu.VMEM((8, CONV_W), F32)],
        compiler_params=_params("arbitrary"),
    )(dmixed, attn, gates, gates, g_attn, g_conv, conv_w)


def _attn_bwd(qkv, dattn, sinks, rope):
    t = qkv.shape[0]
    nb = t // BLOCK
    rev = lambda i: nb - 1 - i
    rc, rs1, rs2 = rope

    def body(sink_ref, q_ref, kp_ref, kc_ref, vp_ref, vc_ref, do_ref, c_ref, s1_ref, s2_ref,
             dqkv_ref, dsink_ref, ck_ref, cv_ref):
        i = pl.program_id(0)

        @pl.when(i == 0)
        def _():
            dsink_ref[...] = jnp.zeros_like(dsink_ref)
            ck_ref[...] = jnp.zeros_like(ck_ref)
            cv_ref[...] = jnp.zeros_like(cv_ref)

        valid = _attn_mask(rev(i))
        kp, kc, vp, vc = kp_ref[...], kc_ref[...], vp_ref[...], vc_ref[...]
        lane = lax.broadcasted_iota(jnp.int32, (1, 128), 1)
        dsink = jnp.zeros((1, 128), F32)
        dq_parts, dk_parts, dv_parts = [], [], []
        for g in range(N_KV):
            gs = slice(HEAD_DIM * g, HEAD_DIM * (g + 1))
            kk = jnp.concatenate([kp[:, gs], kc[:, gs]], axis=0)
            vv = jnp.concatenate([vp[:, gs], vc[:, gs]], axis=0)
            dkk = jnp.zeros((2 * BLOCK, HEAD_DIM), F32)
            dvv = jnp.zeros((2 * BLOCK, HEAD_DIM), F32)
            for hh in range(GROUP):
                h = GROUP * g + hh
                hs = slice(HEAD_DIM * h, HEAD_DIM * (h + 1))
                qh = q_ref[:, hs]
                probs, psink = _attn_probs(qh, kk, sink_ref[0, h], valid)
                dout = do_ref[:, hs].astype(BF16)
                dp = _mm_nt(dout, vv)
                delta = jnp.sum(probs * dp, axis=-1, keepdims=True)
                ds = (probs * (dp - delta) * ATTN_SCALE).astype(BF16)
                dsink = dsink + jnp.where(lane == h, -jnp.sum(psink * delta), 0.0)
                dq_parts.append(_mm(ds, kk))
                dkk = dkk + _mm_tn(ds, qh)
                dvv = dvv + _mm_tn(probs.astype(BF16), dout)
            dk_parts.append(dkk)
            dv_parts.append(dvv)
        dk2 = jnp.concatenate(dk_parts, axis=1)
        dv2 = jnp.concatenate(dv_parts, axis=1)
        dk = dk2[BLOCK:, :] + ck_ref[...]
        dv = dv2[BLOCK:, :] + cv_ref[...]
        ck_ref[...] = dk2[:BLOCK, :]
        cv_ref[...] = dv2[:BLOCK, :]
        c, s1, s2 = c_ref[...], s1_ref[...], s2_ref[...]
        for ci in range(ATTN_W // 128):
            dq2 = jnp.concatenate(dq_parts[2 * ci:2 * ci + 2], axis=1)
            dqkv_ref[:, 128 * ci:128 * (ci + 1)] = _rope_transpose(dq2, c, s1, s2).astype(BF16)
        dqkv_ref[:, ATTN_W:ATTN_W + KV_W] = _rope_transpose(dk, c, s1, s2).astype(BF16)
        dqkv_ref[:, ATTN_W + KV_W:] = dv.astype(BF16)
        dsink_ref[0:1, :] += dsink

    blk = lambda w_: pl.BlockSpec((BLOCK, w_), lambda i: (rev(i), 0))
    return pl.pallas_call(
        body, name="attn_bwd", grid=(nb,),
        in_specs=[pl.BlockSpec(memory_space=pltpu.SMEM)] + _qkv_specs(rev) + [blk(ATTN_W), blk(128), blk(128), blk(128)],
        out_specs=[blk(QKV_W), _full((8, 128))],
        out_shape=[jax.ShapeDtypeStruct((t, QKV_W), BF16), jax.ShapeDtypeStruct((8, 128), F32)],
        scratch_shapes=[pltpu.VMEM((BLOCK, KV_W), F32), pltpu.VMEM((BLOCK, KV_W), F32)],
        compiler_params=_params("arbitrary"),
    )(sinks, qkv, qkv, qkv, qkv, qkv, dattn, rc, rs1, rs2)


def _in_proj_bwd(dqkv, dgates, x, dh, g1, w_in, tm):
    t = x.shape[0]

    def body(dq_ref, dg_ref, x_ref, dh_ref, g1_ref, w_ref, dx_ref, dwa_ref, dwb_ref, dg1_ref):
        @pl.when(pl.program_id(0) == 0)
        def _():
            dwa_ref[...] = jnp.zeros_like(dwa_ref)
            dwb_ref[...] = jnp.zeros_like(dwb_ref)
            dg1_ref[...] = jnp.zeros_like(dg1_ref)

        dq, dg = dq_ref[...], dg_ref[...]
        dhn = _mm_nt(dq, w_ref[:, :QKV_W]) + _mm_nt(dg, w_ref[:, QKV_W:])
        xv = x_ref[...]
        r = _inv_rms(xv)
        x_hat = xv * r
        g1 = g1_ref[...]
        dx, dg1 = _rms_bwd(x_hat, r, g1, dhn)
        dx_ref[...] = dh_ref[...] + dx
        hn = (x_hat * g1).astype(BF16)
        dwa_ref[...] += _mm_tn(hn, dq)
        dwb_ref[...] += _mm_tn(hn, dg)
        dg1_ref[0:1, :] += _colsum(dg1)

    tile = lambda n: pl.BlockSpec((tm, n), lambda i: (i, 0))
    return pl.pallas_call(
        body, name="in_proj_bwd", grid=(t // tm,),
        in_specs=[tile(QKV_W), tile(GATES_W), tile(D_MODEL), tile(D_MODEL), _full((1, D_MODEL)),
                  _resident((D_MODEL, IN_COLS))],
        out_specs=[tile(D_MODEL), _full((D_MODEL, QKV_W)), _full((D_MODEL, GATES_W)), _full((SMALL_ROWS, D_MODEL))],
        out_shape=[jax.ShapeDtypeStruct((t, D_MODEL), F32), jax.ShapeDtypeStruct((D_MODEL, QKV_W), F32),
                   jax.ShapeDtypeStruct((D_MODEL, GATES_W), F32), jax.ShapeDtypeStruct((SMALL_ROWS, D_MODEL), F32)],
        compiler_params=_params("arbitrary"),
    )(dqkv, dgates, x, dh, g1, w_in)


def _mesh_pos():
    return lax.axis_index("x"), lax.axis_index("y"), lax.axis_index("c")


def _all_gather(shards, name):
    n = len(shards)

    def body(*refs):
        ins, outs = refs[:n], refs[n:2 * n]
        send_sems, recv_sems, local_sems = refs[2 * n:]
        x, y, c = _mesh_pos()
        me, sibling = (x, y, c), (x, y, 1 - c)
        chips = [(1 - x, y), (x, 1 - y), (1 - x, 1 - y)]

        def copy(i, k, block, to, src=None):
            dst = outs[i].at[4 * block[0] + 2 * block[1] + block[2]]
            return pltpu.make_async_remote_copy(
                src_ref=dst if src is None else src, dst_ref=dst, send_sem=send_sems.at[7 * i + k],
                recv_sem=recv_sems.at[7 * i + k], device_id=to, device_id_type=MESH)

        mine = [pltpu.make_async_copy(ins[i], outs[i].at[4 * x + 2 * y + c], local_sems.at[i]) for i in range(n)]
        for cp in mine:
            cp.start()
        first = []
        for i in range(n):
            first.append(copy(i, 0, me, sibling, src=ins[i]))
            first += [copy(i, 1 + j, me, (*chip, c), src=ins[i]) for j, chip in enumerate(chips)]
        for cp in first:
            cp.start()
        passed = []
        for j, chip in enumerate(chips):
            for i in range(n):
                copy(i, 1 + j, (*chip, c), me).wait_recv()
                cp = copy(i, 4 + j, (*chip, c), sibling)
                cp.start()
                passed.append(cp)
        for i in range(n):
            copy(i, 0, sibling, me).wait_recv()
            for j, chip in enumerate(chips):
                copy(i, 4 + j, (*chip, 1 - c), me).wait_recv()
        for cp in first + passed:
            cp.wait_send()
        for cp in mine:
            cp.wait()

    return pl.pallas_call(
        body, name=name,
        in_specs=[HBM_SPEC] * n, out_specs=[HBM_SPEC] * n,
        out_shape=[jax.ShapeDtypeStruct((N_DEV,) + s.shape, s.dtype) for s in shards],
        scratch_shapes=[pltpu.SemaphoreType.DMA((7 * n,)), pltpu.SemaphoreType.DMA((7 * n,)),
                        pltpu.SemaphoreType.DMA((n,))],
    )(*shards)


def _sibling_exchange(grads, name):
    n = len(grads)

    def body(*refs):
        ins, outs = refs[:n], refs[n:2 * n]
        send_sems, recv_sems = refs[2 * n:]
        x, y, c = _mesh_pos()
        copies = [pltpu.make_async_remote_copy(
            src_ref=ins[i].at[1 - c], dst_ref=outs[i], send_sem=send_sems.at[i], recv_sem=recv_sems.at[i],
            device_id=(x, y, 1 - c), device_id_type=MESH) for i in range(n)]
        for cp in copies:
            cp.start()
        for cp in copies:
            cp.wait()

    return pl.pallas_call(
        body, name=name,
        in_specs=[HBM_SPEC] * n, out_specs=[HBM_SPEC] * n,
        out_shape=[jax.ShapeDtypeStruct(g.shape[1:], g.dtype) for g in grads],
        scratch_shapes=[pltpu.SemaphoreType.DMA((n,)), pltpu.SemaphoreType.DMA((n,))],
    )(*grads)


def _chip_exchange(sums, name):
    n = len(sums)

    def body(*refs):
        ins, outs = refs[:n], refs[n:2 * n]
        send_sems, recv_sems = refs[2 * n:]
        x, y, c = _mesh_pos()
        chips = [(1 - x, y), (x, 1 - y), (1 - x, 1 - y)]
        copies = [pltpu.make_async_remote_copy(
            src_ref=ins[i].at[2 * chip[0] + chip[1]], dst_ref=outs[i].at[k], send_sem=send_sems.at[3 * i + k],
            recv_sem=recv_sems.at[3 * i + k], device_id=(*chip, c), device_id_type=MESH)
            for i in range(n) for k, chip in enumerate(chips)]
        for cp in copies:
            cp.start()
        for cp in copies:
            cp.wait()

    return pl.pallas_call(
        body, name=name,
        in_specs=[HBM_SPEC] * n, out_specs=[HBM_SPEC] * n,
        out_shape=[jax.ShapeDtypeStruct((3,) + s.shape[1:], s.dtype) for s in sums],
        scratch_shapes=[pltpu.SemaphoreType.DMA((3 * n,)), pltpu.SemaphoreType.DMA((3 * n,))],
    )(*sums)


def _pair_sum(grad, recv, pos, tr):
    _, _, rows, cols = grad.shape

    def body(pos_ref, g_ref, r_ref, sb_ref):
        sb_ref[...] = (g_ref[...] + r_ref[...]).astype(BF16)

    return pl.pallas_call(
        body, name="pair_sum",
        grid_spec=pltpu.PrefetchScalarGridSpec(
            num_scalar_prefetch=1, grid=(N_CHIPS, rows // tr),
            in_specs=[pl.BlockSpec((None, None, tr, cols), lambda p, i, pos: (pos[0], p, i, 0)),
                      pl.BlockSpec((None, tr, cols), lambda p, i, pos: (p, i, 0))],
            out_specs=pl.BlockSpec((None, tr, cols), lambda p, i, pos: (p, i, 0))),
        out_shape=jax.ShapeDtypeStruct((N_CHIPS, rows, cols), BF16),
        compiler_params=_params("parallel", "parallel"),
    )(pos, grad, recv)


def _adam_math(w, g, m, v):
    m = ADAM_B1 * m + (1.0 - ADAM_B1) * g
    v = ADAM_B2 * v + (1.0 - ADAM_B2) * (g * g)
    m_hat = m / (1.0 - ADAM_B1 ** ADAM_STEP)
    v_hat = v / (1.0 - ADAM_B2 ** ADAM_STEP)
    delta = -ADAM_LR * (m_hat / (jnp.sqrt(v_hat) + ADAM_EPS) + ADAM_WD * w)
    return delta, m, v


def _adamw_shard(w, m, v, grad, from_sibling, from_chips, pos, tr):
    rows, cols = w.shape

    def body(pos_ref, w_ref, m_ref, v_ref, own_ref, sib_ref, r_ref, g_ref, d_ref, nm_ref, nv_ref):
        g = own_ref[...] + sib_ref[...]
        for k in range(3):
            g = g + r_ref[k].astype(F32)
        g_ref[...] = g
        d_ref[...], nm_ref[...], nv_ref[...] = _adam_math(w_ref[...], g, m_ref[...], v_ref[...])

    tile = pl.BlockSpec((tr, cols), lambda i, pos: (i, 0))
    out = jax.ShapeDtypeStruct((rows, cols), F32)
    return pl.pallas_call(
        body, name="adamw_shard",
        grid_spec=pltpu.PrefetchScalarGridSpec(
            num_scalar_prefetch=1, grid=(rows // tr,),
            in_specs=[tile, tile, tile,
                      pl.BlockSpec((None, None, tr, cols), lambda i, pos: (pos[0], pos[1], i, 0)),
                      pl.BlockSpec((None, tr, cols), lambda i, pos: (pos[1], i, 0)),
                      pl.BlockSpec((3, tr, cols), lambda i, pos: (0, i, 0))],
            out_specs=[tile] * 4),
        out_shape=[out] * 4,
        compiler_params=_params("parallel"),
    )(pos, w, m, v, grad, from_sibling, from_chips)


def _sum_devices(gathered):
    _, rows, cols = gathered.shape

    def body(g_ref, o_ref):
        s = g_ref[0]
        for d in range(1, N_DEV):
            s = s + g_ref[d]
        o_ref[...] = s

    return pl.pallas_call(
        body, name="sum_devices", in_specs=[_full(gathered.shape)], out_specs=_full((rows, cols)), grid=(1,),
        out_shape=jax.ShapeDtypeStruct((rows, cols), F32),
    )(gathered)


def _adamw_small(w, g, m, v):
    def body(w_ref, g_ref, m_ref, v_ref, d_ref, nm_ref, nv_ref):
        d_ref[...], nm_ref[...], nv_ref[...] = _adam_math(w_ref[...], g_ref[...], m_ref[...], v_ref[...])

    spec = _full(w.shape)
    out = jax.ShapeDtypeStruct(w.shape, F32)
    return pl.pallas_call(
        body, name="adamw_small", grid=(1,), in_specs=[spec] * 4, out_specs=[spec] * 3, out_shape=[out] * 3,
    )(w, g, m, v)


TOKEN_TILE = 512
MID_TILE = 256
DW_TILE = 1024
ADAM_ROWS = 128


def _local_grads(x, target, g1, w_in, conv_w, sinks, g_attn, g_conv, w_out, g2, g3, w_up, w_down, g4):
    t = x.shape[0]
    tm = min(TOKEN_TILE, t)
    rope = _rope_tables(t)
    qkv, gates, mconv = _in_proj_fwd(x, g1, w_in, conv_w, g_conv, rope, tm)
    attn, mattn = _attn_fwd(qkv, sinks, g_attn)
    act, dup, hn2, dmo, dmix, dh, dmixed, small_mid = _mid(
        mattn, mconv, x, target, g2, g3, g4, w_out, w_up, w_down, min(MID_TILE, t))
    tk = min(DW_TILE, t)
    dw_up = _dw_up(hn2, dup, tk)
    dw_down = _dw_down(act, dmo, tk)
    dw_out = _dw_out(mattn, mconv, dmix, tk)
    dattn, dgates, small_mix = _mix_bwd(dmixed, attn, gates, g_attn, g_conv, conv_w, tm)
    dqkv, dsink = _attn_bwd(qkv, dattn, sinks, rope)
    grad_x, dwa, dwb, small_in = _in_proj_bwd(dqkv, dgates, x, dh, g1, w_in, tm)
    dw_in = jnp.concatenate([dwa, dwb], axis=1)
    return grad_x, dw_in, dw_out, dw_up, dw_down, (small_mid, small_mix, dsink, small_in)


def _by_dest(a, rows_major):
    r, c = a.shape
    if rows_major:
        return a.reshape(N_CHIPS, 2, r // N_DEV, c).transpose(1, 0, 2, 3)
    return a.reshape(r, N_CHIPS, 2, c // N_DEV).transpose(2, 1, 0, 3)


def _pack_small(small_mid, small_mix, dsink, small_in):
    z = lambda n: jnp.zeros((1, n), F32)
    rows = [
        small_mid[ROW_LOSS:ROW_LOSS + 1],
        small_in[0:1],
        small_mid[ROW_G2:ROW_G2 + 1],
        small_mid[ROW_G3:ROW_G3 + 1],
        small_mid[ROW_G4:ROW_G4 + 1],
        jnp.concatenate([small_mix[ROW_GATTN:ROW_GATTN + 1], small_mix[ROW_GCONV:ROW_GCONV + 1]], axis=1),
        jnp.concatenate([small_mix[ROW_CW0:ROW_CW0 + 1], small_mix[ROW_CW0 + 1:ROW_CW0 + 2]], axis=1),
        jnp.concatenate([small_mix[ROW_CW0 + 2:ROW_CW0 + 3], dsink[0:1, :], z(D_MODEL - CONV_W - 128)], axis=1),
    ]
    return jnp.concatenate(rows, axis=0)


def kernel(x, pre_mix_norm, w_in, conv_w, attn_sinks, attn_group_norm, conv_group_norm, w_out, post_mix_norm, pre_mlp_norm, w_up, w_down, post_mlp_norm, loss_target, m_pre_mix_norm, m_w_in, m_conv_w, m_attn_sinks, m_attn_group_norm, m_conv_group_norm, m_w_out, m_post_mix_norm, m_pre_mlp_norm, m_w_up, m_w_down, m_post_mlp_norm, v_pre_mix_norm, v_w_in, v_conv_w, v_attn_sinks, v_attn_group_norm, v_conv_group_norm, v_w_out, v_post_mix_norm, v_pre_mlp_norm, v_w_up, v_w_down, v_post_mlp_norm):
    xi, yi, ci = _mesh_pos()
    chip = 2 * xi + yi
    dev = 2 * chip + ci

    gw_in, gw_out, gw_up, gw_down, gconv = _all_gather(
        [w_in[0].astype(BF16), w_out[0].astype(BF16), w_up[0].astype(BF16), w_down[0].astype(BF16), conv_w[0]],
        "gather_weights")
    w_in_full = gw_in.transpose(1, 0, 2).reshape(D_MODEL, IN_COLS)
    w_out_full = gw_out.reshape(D_MODEL, D_MODEL)
    conv_full = gconv.transpose(1, 0, 2).reshape(3, CONV_W)

    grad_x, dw_in, dw_out, dw_up, dw_down, smalls = _local_grads(
        x[0], loss_target[0], pre_mix_norm, w_in_full, conv_full, attn_sinks, attn_group_norm, conv_group_norm,
        w_out_full, post_mix_norm, pre_mlp_norm, gw_up, gw_down, post_mlp_norm)

    grads = [_by_dest(dw_in, False), _by_dest(dw_out, True), dw_up, dw_down]
    from_sibling = _sibling_exchange(grads, "reduce_sibling")
    pos = jnp.stack([ci, chip]).astype(jnp.int32)
    summed = [_pair_sum(g, r, pos, ADAM_ROWS) for g, r in zip(grads, from_sibling)]
    from_chips = _chip_exchange(summed, "reduce_chips")

    small = _sum_devices(_all_gather([_pack_small(*smalls)], "gather_small")[0])
    loss = (0.5 / D_MODEL) * jnp.sum(small[0])

    big = {}
    for name, w, m, v, g, rs, rc in zip(
            ("w_in", "w_out", "w_up", "w_down"), (w_in, w_out, w_up, w_down), (m_w_in, m_w_out, m_w_up, m_w_down),
            (v_w_in, v_w_out, v_w_up, v_w_down), grads, from_sibling, from_chips):
        big[name] = [a[None] for a in _adamw_shard(w[0], m[0], v[0], g, rs, rc, pos, ADAM_ROWS)]

    conv_g = lax.dynamic_slice(
        jnp.stack([small[6, :CONV_W], small[6, CONV_W:], small[7, :CONV_W]]), (0, dev * (CONV_W // N_DEV)),
        (3, CONV_W // N_DEV))
    pad = lambda a, n: jnp.pad(a.reshape(1, -1), ((0, 0), (0, n - a.size)))
    small_names = ("pre_mix_norm", "post_mix_norm", "pre_mlp_norm", "post_mlp_norm")
    small_w = {"pre_mix_norm": (pre_mix_norm, m_pre_mix_norm, v_pre_mix_norm),
               "post_mix_norm": (post_mix_norm, m_post_mix_norm, v_post_mix_norm),
               "pre_mlp_norm": (pre_mlp_norm, m_pre_mlp_norm, v_pre_mlp_norm),
               "post_mlp_norm": (post_mlp_norm, m_post_mlp_norm, v_post_mlp_norm)}

    def pack(k):
        rows = [small_w[nm][k] for nm in small_names]
        rows.append(jnp.concatenate([(attn_group_norm, m_attn_group_norm, v_attn_group_norm)[k],
                                     (conv_group_norm, m_conv_group_norm, v_conv_group_norm)[k]], axis=1))
        rows.append(pad((conv_w, m_conv_w, v_conv_w)[k], D_MODEL))
        rows.append(pad((attn_sinks, m_attn_sinks, v_attn_sinks)[k], D_MODEL))
        rows.append(jnp.zeros((1, D_MODEL), F32))
        return jnp.concatenate(rows, axis=0)

    g_small = jnp.concatenate(
        [small[1:6], pad(conv_g, D_MODEL), pad(small[7, CONV_W:CONV_W + N_HEADS], D_MODEL), jnp.zeros((1, D_MODEL), F32)],
        axis=0)
    d_small, nm_small, nv_small = _adamw_small(pack(0), g_small, pack(1), pack(2))

    def unpack(a):
        nconv = 3 * CONV_W // N_DEV
        return {"pre_mix_norm": a[0:1], "post_mix_norm": a[1:2], "pre_mlp_norm": a[2:3], "post_mlp_norm": a[3:4],
                "attn_group_norm": a[4:5, :ATTN_W], "conv_group_norm": a[4:5, ATTN_W:],
                "conv_w": a[5, :nconv].reshape(1, 3, CONV_W // N_DEV), "attn_sinks": a[6:7, :N_HEADS]}

    order = ("pre_mix_norm", "w_in", "conv_w", "attn_sinks", "attn_group_norm", "conv_group_norm", "w_out",
             "post_mix_norm", "pre_mlp_norm", "w_up", "w_down", "post_mlp_norm")
    outs = []
    for k, a in enumerate((g_small, d_small, nm_small, nv_small)):
        sm = unpack(a)
        outs += [big[nm][k] if nm in big else sm[nm] for nm in order]
    return (loss, grad_x[None], *outs)
```

```python
import functools

import jax
import jax.numpy as jnp
import numpy as np
from jax import lax
from jax.experimental import pallas as pl
from jax.experimental.pallas import tpu as pltpu

F32 = jnp.float32
BF16 = jnp.bfloat16

D_MODEL = 1024
HEAD_DIM = 64
ATTN_W = 512
CONV_W = 512
N_HEADS = 8
N_KV = 2
GROUP = 4
KV_W = 128
QKV_W = ATTN_W + 2 * KV_W
GATES_W = 3 * CONV_W
IN_COLS = QKV_W + GATES_W
D_FF = 4096
FF_CHUNK = 512
N_FF_CHUNKS = D_FF // FF_CHUNK
BLOCK = 128
ROT_HALF = 8
ROPE_THETA = 500000.0
NORM_EPS = 1e-6
NEG_INF = -1e30
ATTN_SCALE = 0.125
N_DEV = 8
N_CHIPS = 4
IN_SHARD = IN_COLS // N_DEV

ADAM_LR = 0.001
ADAM_B1 = 0.9
ADAM_B2 = 0.999
ADAM_EPS = 1e-08
ADAM_WD = 0.01
ADAM_STEP = 10

V7X_VMEM_BYTES = 64 * 1024 * 1024
VMEM_LIMIT = V7X_VMEM_BYTES - 8 * 1024 * 1024

MESH = pl.DeviceIdType.MESH
HBM_SPEC = pl.BlockSpec(memory_space=pltpu.HBM)


def _params(*sem):
    return pltpu.CompilerParams(dimension_semantics=sem, vmem_limit_bytes=VMEM_LIMIT)


def _mm(a, b):
    return jnp.dot(a, b, preferred_element_type=F32)


def _mm_nt(a, b):
    return lax.dot_general(a, b, (((1,), (1,)), ((), ())), preferred_element_type=F32)


def _mm_tn(a, b):
    return lax.dot_general(a, b, (((0,), (0,)), ((), ())), preferred_element_type=F32)


def _inv_rms(x):
    return lax.rsqrt(jnp.mean(x * x, axis=-1, keepdims=True) + NORM_EPS)


def _rms_bwd(xhat, r, gain, dy):
    gy = dy * gain
    return r * (gy - xhat * jnp.mean(gy * xhat, axis=-1, keepdims=True)), dy * xhat


def _colsum(a):
    return jnp.sum(a, axis=0, keepdims=True)


def _full(shape):
    zeros = (0,) * len(shape)
    return pl.BlockSpec(shape, lambda *_: zeros)


def _resident(shape):
    zeros = (0,) * len(shape)
    return pl.BlockSpec(shape, lambda *_: zeros, pipeline_mode=pl.Buffered(1))


def _rope_tables(t):
    pos = np.arange(t, dtype=np.float32)
    inv_freq = (ROPE_THETA ** (-np.arange(0, 2 * ROT_HALF, 2, dtype=np.float64) / (2 * ROT_HALF))).astype(np.float32)
    ang = (pos[:, None] * inv_freq[None, :]).astype(np.float64)
    cos, sin = np.cos(ang).astype(np.float32), np.sin(ang).astype(np.float32)
    zeros8 = np.zeros((t, ROT_HALF), np.float32)
    rest = np.zeros((t, HEAD_DIM - 2 * ROT_HALF), np.float32)
    c_head = np.concatenate([cos, cos, rest + 1.0], axis=1)
    s1_head = np.concatenate([zeros8, sin, rest], axis=1)
    s2_head = np.concatenate([-sin, zeros8, rest], axis=1)
    two = lambda a: jnp.asarray(np.concatenate([a, a], axis=1))
    return two(c_head), two(s1_head), two(s2_head)


def _rope(v, c, s1, s2):
    return v * c + pltpu.roll(v, ROT_HALF, 1) * s1 + pltpu.roll(v, 128 - ROT_HALF, 1) * s2


def _rope_transpose(dv, c, s1, s2):
    return dv * c + pltpu.roll(dv * s1, 128 - ROT_HALF, 1) + pltpu.roll(dv * s2, ROT_HALF, 1)


def _shift_rows_down(u, prev, k):
    row = lax.broadcasted_iota(jnp.int32, u.shape, 0)
    out = pltpu.roll(u, k, 0)
    for r in range(k):
        out = jnp.where(row == r, prev[8 - k + r:8 - k + r + 1, :], out)
    return out


def _shift_rows_up(u, nxt, k):
    n = u.shape[0]
    row = lax.broadcasted_iota(jnp.int32, u.shape, 0)
    out = pltpu.roll(u, n - k, 0)
    for r in range(k):
        out = jnp.where(row == n - k + r, nxt[r:r + 1, :], out)
    return out


def _conv3(u, u1, u2, w):
    return (w[0:1, :] * u2 + w[1:2, :] * u1) + w[2:3, :] * u


def _in_proj_fwd(x, g1, w_in, conv_w, g_conv, rope, tm):
    t = x.shape[0]
    rc, rs1, rs2 = rope

    def body(x_ref, g1_ref, w_ref, cw_ref, gc_ref, c_ref, s1_ref, s2_ref, qkv_ref, gates_ref, mconv_ref, carry_ref):
        @pl.when(pl.program_id(0) == 0)
        def _():
            carry_ref[...] = jnp.zeros_like(carry_ref)

        xv = x_ref[...]
        hn = ((xv * _inv_rms(xv)) * g1_ref[...]).astype(BF16)
        proj = _mm(hn, w_ref[...])
        c, s1, s2 = c_ref[...], s1_ref[...], s2_ref[...]
        for ci in range((ATTN_W + KV_W) // 128):
            sl = slice(128 * ci, 128 * (ci + 1))
            qkv_ref[:, sl] = _rope(proj[:, sl], c, s1, s2).astype(BF16)
        qkv_ref[:, ATTN_W + KV_W:QKV_W] = proj[:, ATTN_W + KV_W:QKV_W].astype(BF16)
        gates = proj[:, QKV_W:]
        gates_ref[...] = gates
        gb, gcc, xin = gates[:, :CONV_W], gates[:, CONV_W:2 * CONV_W], gates[:, 2 * CONV_W:]
        u = gcc * xin
        prev = carry_ref[...]
        conv = gb * _conv3(u, _shift_rows_down(u, prev, 1), _shift_rows_down(u, prev, 2), cw_ref[...])
        carry_ref[...] = u[tm - 8:tm, :]
        mconv_ref[...] = ((conv * _inv_rms(conv)) * gc_ref[...]).astype(BF16)

    tile = lambda n: pl.BlockSpec((tm, n), lambda i: (i, 0))
    return pl.pallas_call(
        body, name="in_proj_fwd", grid=(t // tm,),
        in_specs=[tile(D_MODEL), _full((1, D_MODEL)), _full((D_MODEL, IN_COLS)), _full((3, CONV_W)), _full((1, CONV_W)),
                  tile(128), tile(128), tile(128)],
        out_specs=[tile(QKV_W), tile(GATES_W), tile(CONV_W)],
        out_shape=[jax.ShapeDtypeStruct((t, QKV_W), BF16), jax.ShapeDtypeStruct((t, GATES_W), F32),
                   jax.ShapeDtypeStruct((t, CONV_W), BF16)],
        scratch_shapes=[pltpu.VMEM((8, CONV_W), F32)],
        compiler_params=_params("arbitrary"),
    )(x, g1, w_in, conv_w, g_conv, rc, rs1, rs2)


GROUP_ROWS = GROUP * BLOCK


def _attn_mask(j):
    row = lax.broadcasted_iota(jnp.int32, (GROUP_ROWS, 2 * BLOCK), 0) & (BLOCK - 1)
    col = lax.broadcasted_iota(jnp.int32, (GROUP_ROWS, 2 * BLOCK), 1)
    return (col > row) & (col <= row + BLOCK) & ((col >= BLOCK) | (j > 0))


def _stack_heads(a, g):
    return jnp.concatenate([a[:, HEAD_DIM * (GROUP * g + hh):HEAD_DIM * (GROUP * g + hh + 1)] for hh in range(GROUP)], axis=0)


def _unstack_heads(a):
    return jnp.concatenate([a[BLOCK * hh:BLOCK * (hh + 1), :] for hh in range(GROUP)], axis=1)


def _group_sinks(sink_ref, g):
    head = lax.broadcasted_iota(jnp.int32, (GROUP_ROWS, 1), 0) // BLOCK
    out = jnp.full((GROUP_ROWS, 1), sink_ref[0, GROUP * g], F32)
    for hh in range(1, GROUP):
        out = jnp.where(head == hh, sink_ref[0, GROUP * g + hh], out)
    return out


def _attn_probs(qs, kk, sink, valid):
    s = jnp.where(valid, _mm_nt(qs, kk) * ATTN_SCALE, NEG_INF)
    m = jnp.maximum(jnp.max(s, axis=-1, keepdims=True), sink)
    p = jnp.exp(s - m)
    psink = jnp.exp(sink - m)
    inv_l = 1.0 / (jnp.sum(p, axis=-1, keepdims=True) + psink)
    return p * inv_l, psink * inv_l


def _qkv_specs(order):
    prev = lambda i: jnp.maximum(order(i) - 1, 0)
    kcol, vcol = ATTN_W // KV_W, ATTN_W // KV_W + 1
    return [pl.BlockSpec((BLOCK, ATTN_W), lambda i: (order(i), 0)),
            pl.BlockSpec((BLOCK, KV_W), lambda i: (prev(i), kcol)), pl.BlockSpec((BLOCK, KV_W), lambda i: (order(i), kcol)),
            pl.BlockSpec((BLOCK, KV_W), lambda i: (prev(i), vcol)), pl.BlockSpec((BLOCK, KV_W), lambda i: (order(i), vcol))]


def _attn_fwd(qkv, sinks, g_attn):
    t = qkv.shape[0]

    def body(sink_ref, q_ref, kp_ref, kc_ref, vp_ref, vc_ref, ga_ref, attn_ref, mattn_ref):
        valid = _attn_mask(pl.program_id(0))
        q, kp, kc, vp, vc = q_ref[...], kp_ref[...], kc_ref[...], vp_ref[...], vc_ref[...]
        outs = []
        for g in range(N_KV):
            gs = slice(HEAD_DIM * g, HEAD_DIM * (g + 1))
            kk = jnp.concatenate([kp[:, gs], kc[:, gs]], axis=0)
            vv = jnp.concatenate([vp[:, gs], vc[:, gs]], axis=0)
            probs, _ = _attn_probs(_stack_heads(q, g), kk, _group_sinks(sink_ref, g), valid)
            outs.append(_unstack_heads(_mm(probs.astype(BF16), vv)))
        attn = jnp.concatenate(outs, axis=1)
        attn_ref[...] = attn
        mattn_ref[...] = ((attn * _inv_rms(attn)) * ga_ref[...]).astype(BF16)

    blk = pl.BlockSpec((BLOCK, ATTN_W), lambda j: (j, 0))
    return pl.pallas_call(
        body, name="attn_fwd", grid=(t // BLOCK,),
        in_specs=[pl.BlockSpec(memory_space=pltpu.SMEM)] + _qkv_specs(lambda j: j) + [_full((1, ATTN_W))],
        out_specs=[blk, blk],
        out_shape=[jax.ShapeDtypeStruct((t, ATTN_W), F32), jax.ShapeDtypeStruct((t, ATTN_W), BF16)],
        compiler_params=_params("arbitrary"),
    )(sinks, qkv, qkv, qkv, qkv, qkv, g_attn)


SMALL_ROWS = 8
ROW_LOSS, ROW_G2, ROW_G3, ROW_G4 = 0, 1, 2, 3


def _mid(mattn, mconv, x, target, g2, g3, g4, w_out, w_up, w_down, tm):
    t = x.shape[0]

    def body(ma_ref, mc_ref, x_ref, t_ref, g2_ref, g3_ref, g4_ref, wo_ref, wu_ref, wd_ref,
             act_ref, dup_ref, hn2_ref, dmo_ref, dmix_ref, dh_ref, dmixed_ref, small_ref, up_ref):
        @pl.when(pl.program_id(0) == 0)
        def _():
            small_ref[...] = jnp.zeros_like(small_ref)

        g2, g3, g4 = g2_ref[...], g3_ref[...], g4_ref[...]
        mix_out = _mm(ma_ref[...], wo_ref[0:ATTN_W, :]) + _mm(mc_ref[...], wo_ref[ATTN_W:, :])
        r2 = _inv_rms(mix_out)
        mo_hat = mix_out * r2
        h = x_ref[...] + mo_hat * g2
        r3 = _inv_rms(h)
        h_hat = h * r3
        hn2 = (h_hat * g3).astype(BF16)
        hn2_ref[...] = hn2
        mlp = jnp.zeros((tm, D_MODEL), F32)
        for j in range(N_FF_CHUNKS):
            up = jnp.maximum(_mm(hn2, wu_ref[j]), 0.0)
            up_ref[j] = up
            act = (up * up).astype(BF16)
            act_ref[:, FF_CHUNK * j:FF_CHUNK * (j + 1)] = act
            mlp = mlp + _mm(act, wd_ref[j])
        r4 = _inv_rms(mlp)
        ml_hat = mlp * r4
        err = (h + ml_hat * g4) - t_ref[...]
        d_out = err * (1.0 / D_MODEL)
        d_mlp, dg4 = _rms_bwd(ml_hat, r4, g4, d_out)
        dmo = d_mlp.astype(BF16)
        dmo_ref[...] = dmo
        dhn2 = jnp.zeros((tm, D_MODEL), F32)
        for j in range(N_FF_CHUNKS):
            dup = (_mm_nt(dmo, wd_ref[j]) * (2.0 * up_ref[j])).astype(BF16)
            dup_ref[:, FF_CHUNK * j:FF_CHUNK * (j + 1)] = dup
            dhn2 = dhn2 + _mm_nt(dup, wu_ref[j])
        dh_norm, dg3 = _rms_bwd(h_hat, r3, g3, dhn2)
        dh = d_out + dh_norm
        dh_ref[...] = dh
        d_mix, dg2 = _rms_bwd(mo_hat, r2, g2, dh)
        dmix = d_mix.astype(BF16)
        dmix_ref[...] = dmix
        dmixed_ref[...] = _mm_nt(dmix, wo_ref[...])
        small_ref[ROW_LOSS:ROW_LOSS + 1, :] += _colsum(err * err)
        small_ref[ROW_G2:ROW_G2 + 1, :] += _colsum(dg2)
        small_ref[ROW_G3:ROW_G3 + 1, :] += _colsum(dg3)
        small_ref[ROW_G4:ROW_G4 + 1, :] += _colsum(dg4)

    tile = lambda n: pl.BlockSpec((tm, n), lambda i: (i, 0))
    gain = _full((1, D_MODEL))
    return pl.pallas_call(
        body, name="mid_fwd_bwd", grid=(t // tm,),
        in_specs=[tile(ATTN_W), tile(CONV_W), tile(D_MODEL), tile(D_MODEL), gain, gain, gain,
                  _resident((D_MODEL, D_MODEL)), _resident((N_FF_CHUNKS, D_MODEL, FF_CHUNK)),
                  _resident((N_FF_CHUNKS, FF_CHUNK, D_MODEL))],
        out_specs=[tile(D_FF), tile(D_FF), tile(D_MODEL), tile(D_MODEL), tile(D_MODEL), tile(D_MODEL), tile(D_MODEL),
                   _full((SMALL_ROWS, D_MODEL))],
        out_shape=[jax.ShapeDtypeStruct((t, D_FF), BF16), jax.ShapeDtypeStruct((t, D_FF), BF16),
                   jax.ShapeDtypeStruct((t, D_MODEL), BF16), jax.ShapeDtypeStruct((t, D_MODEL), BF16),
                   jax.ShapeDtypeStruct((t, D_MODEL), BF16), jax.ShapeDtypeStruct((t, D_MODEL), F32),
                   jax.ShapeDtypeStruct((t, D_MODEL), F32), jax.ShapeDtypeStruct((SMALL_ROWS, D_MODEL), F32)],
        scratch_shapes=[pltpu.VMEM((N_FF_CHUNKS, tm, FF_CHUNK), F32)],
        compiler_params=_params("arbitrary"),
    )(mattn, mconv, x, target, g2, g3, g4, w_out, w_up, w_down)


def _dw_up(hn2, dup, tk):
    t = hn2.shape[0]

    def body(a_ref, b_ref, o_ref):
        @pl.when(pl.program_id(1) == 0)
        def _():
            o_ref[...] = jnp.zeros_like(o_ref)
        o_ref[...] += _mm_tn(a_ref[...], b_ref[...])

    return pl.pallas_call(
        body, name="dw_up", grid=(N_DEV, t // tk),
        in_specs=[pl.BlockSpec((tk, D_MODEL), lambda j, k: (k, 0)), pl.BlockSpec((tk, FF_CHUNK), lambda j, k: (k, j))],
        out_specs=pl.BlockSpec((None, None, D_MODEL, FF_CHUNK), lambda j, k: (j % 2, j // 2, 0, 0)),
        out_shape=jax.ShapeDtypeStruct((2, N_CHIPS, D_MODEL, FF_CHUNK), F32),
        compiler_params=_params("parallel", "arbitrary"),
    )(hn2, dup)


def _dw_down(act, dmo, tk):
    t = act.shape[0]

    def body(a_ref, b_ref, o_ref):
        @pl.when(pl.program_id(1) == 0)
        def _():
            o_ref[...] = jnp.zeros_like(o_ref)
        o_ref[...] += _mm_tn(a_ref[...], b_ref[...])

    return pl.pallas_call(
        body, name="dw_down", grid=(N_DEV, t // tk),
        in_specs=[pl.BlockSpec((tk, FF_CHUNK), lambda j, k: (k, j)), pl.BlockSpec((tk, D_MODEL), lambda j, k: (k, 0))],
        out_specs=pl.BlockSpec((None, None, FF_CHUNK, D_MODEL), lambda j, k: (j % 2, j // 2, 0, 0)),
        out_shape=jax.ShapeDtypeStruct((2, N_CHIPS, FF_CHUNK, D_MODEL), F32),
        compiler_params=_params("parallel", "arbitrary"),
    )(act, dmo)


def _dw_out(mattn, mconv, dmix, tk):
    t = dmix.shape[0]

    def body(ma_ref, mc_ref, b_ref, o_ref):
        @pl.when(pl.program_id(0) == 0)
        def _():
            o_ref[...] = jnp.zeros_like(o_ref)
        b = b_ref[...]
        o_ref[0:ATTN_W, :] += _mm_tn(ma_ref[...], b)
        o_ref[ATTN_W:, :] += _mm_tn(mc_ref[...], b)

    tile = lambda n: pl.BlockSpec((tk, n), lambda k: (k, 0))
    return pl.pallas_call(
        body, name="dw_out", grid=(t // tk,),
        in_specs=[tile(ATTN_W), tile(CONV_W), tile(D_MODEL)],
        out_specs=_full((D_MODEL, D_MODEL)),
        out_shape=jax.ShapeDtypeStruct((D_MODEL, D_MODEL), F32),
        compiler_params=_params("arbitrary"),
    )(mattn, mconv, dmix)


ROW_GATTN, ROW_GCONV, ROW_CW0 = 0, 1, 2


def _mix_bwd(dmixed, attn, gates, g_attn, g_conv, conv_w, tm):
    t = attn.shape[0]
    n = t // tm
    rev = lambda i: n - 1 - i

    def body(dm_ref, attn_ref, gates_ref, gprev_ref, ga_ref, gc_ref, cw_ref, dattn_ref, dgates_ref, small_ref, carry_ref):
        i = pl.program_id(0)

        @pl.when(i == 0)
        def _():
            small_ref[...] = jnp.zeros_like(small_ref)
            carry_ref[...] = jnp.zeros_like(carry_ref)

        dm = dm_ref[...]
        a = attn_ref[...]
        ra = _inv_rms(a)
        a_hat = a * ra
        dattn, dga = _rms_bwd(a_hat, ra, ga_ref[...], dm[:, :ATTN_W])
        dattn_ref[...] = dattn

        gates = gates_ref[...]
        gb, gcc, xin = gates[:, :CONV_W], gates[:, CONV_W:2 * CONV_W], gates[:, 2 * CONV_W:]
        u = gcc * xin
        gp = gprev_ref[...]
        uprev = jnp.where(rev(i) == 0, 0.0, gp[:, CONV_W:2 * CONV_W] * gp[:, 2 * CONV_W:])
        u1, u2 = _shift_rows_down(u, uprev, 1), _shift_rows_down(u, uprev, 2)
        w = cw_ref[...]
        c = _conv3(u, u1, u2, w)
        conv = gb * c
        rcv = _inv_rms(conv)
        c_hat = conv * rcv
        dconv, dgc = _rms_bwd(c_hat, rcv, gc_ref[...], dm[:, ATTN_W:])
        dc = dconv * gb
        nxt = carry_ref[...]
        du = (w[2:3, :] * dc + w[1:2, :] * _shift_rows_up(dc, nxt, 1)) + w[0:1, :] * _shift_rows_up(dc, nxt, 2)
        carry_ref[...] = dc[0:8, :]
        dgates_ref[:, :CONV_W] = (dconv * c).astype(BF16)
        dgates_ref[:, CONV_W:2 * CONV_W] = (du * xin).astype(BF16)
        dgates_ref[:, 2 * CONV_W:] = (du * gcc).astype(BF16)
        small_ref[ROW_GATTN:ROW_GATTN + 1, :] += _colsum(dga)
        small_ref[ROW_GCONV:ROW_GCONV + 1, :] += _colsum(dgc)
        small_ref[ROW_CW0:ROW_CW0 + 1, :] += _colsum(dc * u2)
        small_ref[ROW_CW0 + 1:ROW_CW0 + 2, :] += _colsum(dc * u1)
        small_ref[ROW_CW0 + 2:ROW_CW0 + 3, :] += _colsum(dc * u)

    tile = lambda w_: pl.BlockSpec((tm, w_), lambda i: (rev(i), 0))
    prev8 = pl.BlockSpec((8, GATES_W), lambda i: (jnp.maximum(rev(i) * (tm // 8) - 1, 0), 0))
    return pl.pallas_call(
        body, name="mix_bwd", grid=(n,),
        in_specs=[tile(D_MODEL), tile(ATTN_W), tile(GATES_W), prev8, _full((1, ATTN_W)), _full((1, CONV_W)),
                  _full((3, CONV_W))],
        out_specs=[tile(ATTN_W), tile(GATES_W), _full((SMALL_ROWS, CONV_W))],
        out_shape=[jax.ShapeDtypeStruct((t, ATTN_W), F32), jax.ShapeDtypeStruct((t, GATES_W), BF16),
                   jax.ShapeDtypeStruct((SMALL_ROWS, CONV_W), F32)],
        scratch_shapes=[pltpu.VMEM((8, CONV_W), F32)],
        compiler_params=_params("arbitrary"),
    )(dmixed, attn, gates, gates, g_attn, g_conv, conv_w)


def _attn_bwd(qkv, dattn, sinks, rope):
    t = qkv.shape[0]
    nb = t // BLOCK
    rev = lambda i: nb - 1 - i
    rc, rs1, rs2 = rope

    def body(sink_ref, q_ref, kp_ref, kc_ref, vp_ref, vc_ref, do_ref, c_ref, s1_ref, s2_ref,
             dqkv_ref, dsink_ref, ck_ref, cv_ref):
        i = pl.program_id(0)

        @pl.when(i == 0)
        def _():
            dsink_ref[...] = jnp.zeros_like(dsink_ref)
            ck_ref[...] = jnp.zeros_like(ck_ref)
            cv_ref[...] = jnp.zeros_like(cv_ref)

        valid = _attn_mask(rev(i))
        q, kp, kc, vp, vc = q_ref[...], kp_ref[...], kc_ref[...], vp_ref[...], vc_ref[...]
        dout = do_ref[...].astype(BF16)
        lane = lax.broadcasted_iota(jnp.int32, (1, 128), 1)
        dsink = jnp.zeros((1, 128), F32)
        dq_parts, dk_parts, dv_parts = [], [], []
        for g in range(N_KV):
            gs = slice(HEAD_DIM * g, HEAD_DIM * (g + 1))
            kk = jnp.concatenate([kp[:, gs], kc[:, gs]], axis=0)
            vv = jnp.concatenate([vp[:, gs], vc[:, gs]], axis=0)
            qs, dos = _stack_heads(q, g), _stack_heads(dout, g)
            probs, psink = _attn_probs(qs, kk, _group_sinks(sink_ref, g), valid)
            dp = _mm_nt(dos, vv)
            delta = jnp.sum(probs * dp, axis=-1, keepdims=True)
            ds = (probs * (dp - delta) * ATTN_SCALE).astype(BF16)
            sink_terms = psink * delta
            for hh in range(GROUP):
                dsink = dsink + jnp.where(lane == GROUP * g + hh, -jnp.sum(sink_terms[BLOCK * hh:BLOCK * (hh + 1), :]), 0.0)
            dq_parts.append(_unstack_heads(_mm(ds, kk)))
            dk_parts.append(_mm_tn(ds, qs))
            dv_parts.append(_mm_tn(probs.astype(BF16), dos))
        dk2 = jnp.concatenate(dk_parts, axis=1)
        dv2 = jnp.concatenate(dv_parts, axis=1)
        dk = dk2[BLOCK:, :] + ck_ref[...]
        dv = dv2[BLOCK:, :] + cv_ref[...]
        ck_ref[...] = dk2[:BLOCK, :]
        cv_ref[...] = dv2[:BLOCK, :]
        c, s1, s2 = c_ref[...], s1_ref[...], s2_ref[...]
        dq = jnp.concatenate(dq_parts, axis=1)
        for ci in range(ATTN_W // 128):
            sl = slice(128 * ci, 128 * (ci + 1))
            dqkv_ref[:, sl] = _rope_transpose(dq[:, sl], c, s1, s2).astype(BF16)
        dqkv_ref[:, ATTN_W:ATTN_W + KV_W] = _rope_transpose(dk, c, s1, s2).astype(BF16)
        dqkv_ref[:, ATTN_W + KV_W:] = dv.astype(BF16)
        dsink_ref[0:1, :] += dsink

    blk = lambda w_: pl.BlockSpec((BLOCK, w_), lambda i: (rev(i), 0))
    return pl.pallas_call(
        body, name="attn_bwd", grid=(nb,),
        in_specs=[pl.BlockSpec(memory_space=pltpu.SMEM)] + _qkv_specs(rev) + [blk(ATTN_W), blk(128), blk(128), blk(128)],
        out_specs=[blk(QKV_W), _full((8, 128))],
        out_shape=[jax.ShapeDtypeStruct((t, QKV_W), BF16), jax.ShapeDtypeStruct((8, 128), F32)],
        scratch_shapes=[pltpu.VMEM((BLOCK, KV_W), F32), pltpu.VMEM((BLOCK, KV_W), F32)],
        compiler_params=_params("arbitrary"),
    )(sinks, qkv, qkv, qkv, qkv, qkv, dattn, rc, rs1, rs2)


def _in_proj_bwd(dqkv, dgates, x, dh, g1, w_in, tm):
    t = x.shape[0]

    def body(dq_ref, dg_ref, x_ref, dh_ref, g1_ref, w_ref, dx_ref, dwa_ref, dwb_ref, dg1_ref):
        @pl.when(pl.program_id(0) == 0)
        def _():
            dwa_ref[...] = jnp.zeros_like(dwa_ref)
            dwb_ref[...] = jnp.zeros_like(dwb_ref)
            dg1_ref[...] = jnp.zeros_like(dg1_ref)

        dq, dg = dq_ref[...], dg_ref[...]
        dhn = _mm_nt(dq, w_ref[:, :QKV_W]) + _mm_nt(dg, w_ref[:, QKV_W:])
        xv = x_ref[...]
        r = _inv_rms(xv)
        x_hat = xv * r
        g1 = g1_ref[...]
        dx, dg1 = _rms_bwd(x_hat, r, g1, dhn)
        dx_ref[...] = dh_ref[...] + dx
        hn = (x_hat * g1).astype(BF16)
        dwa_ref[...] += _mm_tn(hn, dq)
        dwb_ref[...] += _mm_tn(hn, dg)
        dg1_ref[0:1, :] += _colsum(dg1)

    tile = lambda n: pl.BlockSpec((tm, n), lambda i: (i, 0))
    return pl.pallas_call(
        body, name="in_proj_bwd", grid=(t // tm,),
        in_specs=[tile(QKV_W), tile(GATES_W), tile(D_MODEL), tile(D_MODEL), _full((1, D_MODEL)),
                  _resident((D_MODEL, IN_COLS))],
        out_specs=[tile(D_MODEL), _full((D_MODEL, QKV_W)), _full((D_MODEL, GATES_W)), _full((SMALL_ROWS, D_MODEL))],
        out_shape=[jax.ShapeDtypeStruct((t, D_MODEL), F32), jax.ShapeDtypeStruct((D_MODEL, QKV_W), F32),
                   jax.ShapeDtypeStruct((D_MODEL, GATES_W), F32), jax.ShapeDtypeStruct((SMALL_ROWS, D_MODEL), F32)],
        compiler_params=_params("arbitrary"),
    )(dqkv, dgates, x, dh, g1, w_in)


def _mesh_pos():
    return lax.axis_index("x"), lax.axis_index("y"), lax.axis_index("c")


def _all_gather(shards, name):
    n = len(shards)

    def body(*refs):
        ins, outs = refs[:n], refs[n:2 * n]
        send_sems, recv_sems, local_sems = refs[2 * n:]
        x, y, c = _mesh_pos()
        me, sibling = (x, y, c), (x, y, 1 - c)
        chips = [(1 - x, y), (x, 1 - y), (1 - x, 1 - y)]

        def copy(i, k, block, to, src=None):
            dst = outs[i].at[4 * block[0] + 2 * block[1] + block[2]]
            return pltpu.make_async_remote_copy(
                src_ref=dst if src is None else src, dst_ref=dst, send_sem=send_sems.at[7 * i + k],
                recv_sem=recv_sems.at[7 * i + k], device_id=to, device_id_type=MESH)

        mine = [pltpu.make_async_copy(ins[i], outs[i].at[4 * x + 2 * y + c], local_sems.at[i]) for i in range(n)]
        for cp in mine:
            cp.start()
        first = []
        for i in range(n):
            first.append(copy(i, 0, me, sibling, src=ins[i]))
            first += [copy(i, 1 + j, me, (*chip, c), src=ins[i]) for j, chip in enumerate(chips)]
        for cp in first:
            cp.start()
        passed = []
        for j, chip in enumerate(chips):
            for i in range(n):
                copy(i, 1 + j, (*chip, c), me).wait_recv()
                cp = copy(i, 4 + j, (*chip, c), sibling)
                cp.start()
                passed.append(cp)
        for i in range(n):
            copy(i, 0, sibling, me).wait_recv()
            for j, chip in enumerate(chips):
                copy(i, 4 + j, (*chip, 1 - c), me).wait_recv()
        for cp in first + passed:
            cp.wait_send()
        for cp in mine:
            cp.wait()

    return pl.pallas_call(
        body, name=name,
        in_specs=[HBM_SPEC] * n, out_specs=[HBM_SPEC] * n,
        out_shape=[jax.ShapeDtypeStruct((N_DEV,) + s.shape, s.dtype) for s in shards],
        scratch_shapes=[pltpu.SemaphoreType.DMA((7 * n,)), pltpu.SemaphoreType.DMA((7 * n,)),
                        pltpu.SemaphoreType.DMA((n,))],
    )(*shards)


def _sibling_exchange(grads, name):
    n = len(grads)

    def body(*refs):
        ins, outs = refs[:n], refs[n:2 * n]
        send_sems, recv_sems = refs[2 * n:]
        x, y, c = _mesh_pos()
        copies = [pltpu.make_async_remote_copy(
            src_ref=ins[i].at[1 - c], dst_ref=outs[i], send_sem=send_sems.at[i], recv_sem=recv_sems.at[i],
            device_id=(x, y, 1 - c), device_id_type=MESH) for i in range(n)]
        for cp in copies:
            cp.start()
        for cp in copies:
            cp.wait()

    return pl.pallas_call(
        body, name=name,
        in_specs=[HBM_SPEC] * n, out_specs=[HBM_SPEC] * n,
        out_shape=[jax.ShapeDtypeStruct(g.shape[1:], g.dtype) for g in grads],
        scratch_shapes=[pltpu.SemaphoreType.DMA((n,)), pltpu.SemaphoreType.DMA((n,))],
    )(*grads)


def _chip_exchange(sums, name):
    n = len(sums)

    def body(*refs):
        ins, outs = refs[:n], refs[n:2 * n]
        send_sems, recv_sems = refs[2 * n:]
        x, y, c = _mesh_pos()
        chips = [(1 - x, y), (x, 1 - y), (1 - x, 1 - y)]
        copies = [pltpu.make_async_remote_copy(
            src_ref=ins[i].at[2 * chip[0] + chip[1]], dst_ref=outs[i].at[k], send_sem=send_sems.at[3 * i + k],
            recv_sem=recv_sems.at[3 * i + k], device_id=(*chip, c), device_id_type=MESH)
            for i in range(n) for k, chip in enumerate(chips)]
        for cp in copies:
            cp.start()
        for cp in copies:
            cp.wait()

    return pl.pallas_call(
        body, name=name,
        in_specs=[HBM_SPEC] * n, out_specs=[HBM_SPEC] * n,
        out_shape=[jax.ShapeDtypeStruct((3,) + s.shape[1:], s.dtype) for s in sums],
        scratch_shapes=[pltpu.SemaphoreType.DMA((3 * n,)), pltpu.SemaphoreType.DMA((3 * n,))],
    )(*sums)


def _pair_sum(grad, recv, pos, tr):
    _, _, rows, cols = grad.shape

    def body(pos_ref, g_ref, r_ref, sb_ref):
        sb_ref[...] = (g_ref[...] + r_ref[...]).astype(BF16)

    return pl.pallas_call(
        body, name="pair_sum",
        grid_spec=pltpu.PrefetchScalarGridSpec(
            num_scalar_prefetch=1, grid=(N_CHIPS, rows // tr),
            in_specs=[pl.BlockSpec((None, None, tr, cols), lambda p, i, pos: (pos[0], p, i, 0)),
                      pl.BlockSpec((None, tr, cols), lambda p, i, pos: (p, i, 0))],
            out_specs=pl.BlockSpec((None, tr, cols), lambda p, i, pos: (p, i, 0))),
        out_shape=jax.ShapeDtypeStruct((N_CHIPS, rows, cols), BF16),
        compiler_params=_params("parallel", "parallel"),
    )(pos, grad, recv)


def _adam_math(w, g, m, v):
    m = ADAM_B1 * m + (1.0 - ADAM_B1) * g
    v = ADAM_B2 * v + (1.0 - ADAM_B2) * (g * g)
    m_hat = m / (1.0 - ADAM_B1 ** ADAM_STEP)
    v_hat = v / (1.0 - ADAM_B2 ** ADAM_STEP)
    delta = -ADAM_LR * (m_hat / (jnp.sqrt(v_hat) + ADAM_EPS) + ADAM_WD * w)
    return delta, m, v


def _adamw_shard(w, m, v, grad, from_sibling, from_chips, pos, tr):
    rows, cols = w.shape

    def body(pos_ref, w_ref, m_ref, v_ref, own_ref, sib_ref, r_ref, g_ref, d_ref, nm_ref, nv_ref):
        g = own_ref[...] + sib_ref[...]
        for k in range(3):
            g = g + r_ref[k].astype(F32)
        g_ref[...] = g
        d_ref[...], nm_ref[...], nv_ref[...] = _adam_math(w_ref[...], g, m_ref[...], v_ref[...])

    tile = pl.BlockSpec((tr, cols), lambda i, pos: (i, 0))
    out = jax.ShapeDtypeStruct((rows, cols), F32)
    return pl.pallas_call(
        body, name="adamw_shard",
        grid_spec=pltpu.PrefetchScalarGridSpec(
            num_scalar_prefetch=1, grid=(rows // tr,),
            in_specs=[tile, tile, tile,
                      pl.BlockSpec((None, None, tr, cols), lambda i, pos: (pos[0], pos[1], i, 0)),
                      pl.BlockSpec((None, tr, cols), lambda i, pos: (pos[1], i, 0)),
                      pl.BlockSpec((3, tr, cols), lambda i, pos: (0, i, 0))],
            out_specs=[tile] * 4),
        out_shape=[out] * 4,
        compiler_params=_params("parallel"),
    )(pos, w, m, v, grad, from_sibling, from_chips)


def _sum_devices(gathered):
    _, rows, cols = gathered.shape

    def body(g_ref, o_ref):
        s = g_ref[0]
        for d in range(1, N_DEV):
            s = s + g_ref[d]
        o_ref[...] = s

    return pl.pallas_call(
        body, name="sum_devices", in_specs=[_full(gathered.shape)], out_specs=_full((rows, cols)), grid=(1,),
        out_shape=jax.ShapeDtypeStruct((rows, cols), F32),
    )(gathered)


def _adamw_small(w, g, m, v):
    def body(w_ref, g_ref, m_ref, v_ref, d_ref, nm_ref, nv_ref):
        d_ref[...], nm_ref[...], nv_ref[...] = _adam_math(w_ref[...], g_ref[...], m_ref[...], v_ref[...])

    spec = _full(w.shape)
    out = jax.ShapeDtypeStruct(w.shape, F32)
    return pl.pallas_call(
        body, name="adamw_small", grid=(1,), in_specs=[spec] * 4, out_specs=[spec] * 3, out_shape=[out] * 3,
    )(w, g, m, v)


TOKEN_TILE = 512
MID_TILE = 256
DW_TILE = 1024
ADAM_ROWS = 128


def _local_grads(x, target, g1, w_in, conv_w, sinks, g_attn, g_conv, w_out, g2, g3, w_up, w_down, g4):
    t = x.shape[0]
    tm = min(TOKEN_TILE, t)
    rope = _rope_tables(t)
    qkv, gates, mconv = _in_proj_fwd(x, g1, w_in, conv_w, g_conv, rope, tm)
    attn, mattn = _attn_fwd(qkv, sinks, g_attn)
    act, dup, hn2, dmo, dmix, dh, dmixed, small_mid = _mid(
        mattn, mconv, x, target, g2, g3, g4, w_out, w_up, w_down, min(MID_TILE, t))
    tk = min(DW_TILE, t)
    dw_up = _dw_up(hn2, dup, tk)
    dw_down = _dw_down(act, dmo, tk)
    dw_out = _dw_out(mattn, mconv, dmix, tk)
    dattn, dgates, small_mix = _mix_bwd(dmixed, attn, gates, g_attn, g_conv, conv_w, tm)
    dqkv, dsink = _attn_bwd(qkv, dattn, sinks, rope)
    grad_x, dwa, dwb, small_in = _in_proj_bwd(dqkv, dgates, x, dh, g1, w_in, tm)
    dw_in = jnp.concatenate([dwa, dwb], axis=1)
    return grad_x, dw_in, dw_out, dw_up, dw_down, (small_mid, small_mix, dsink, small_in)


def _by_dest(a, rows_major):
    r, c = a.shape
    if rows_major:
        return a.reshape(N_CHIPS, 2, r // N_DEV, c).transpose(1, 0, 2, 3)
    return a.reshape(r, N_CHIPS, 2, c // N_DEV).transpose(2, 1, 0, 3)


def _pack_small(small_mid, small_mix, dsink, small_in):
    z = lambda n: jnp.zeros((1, n), F32)
    rows = [
        small_mid[ROW_LOSS:ROW_LOSS + 1],
        small_in[0:1],
        small_mid[ROW_G2:ROW_G2 + 1],
        small_mid[ROW_G3:ROW_G3 + 1],
        small_mid[ROW_G4:ROW_G4 + 1],
        jnp.concatenate([small_mix[ROW_GATTN:ROW_GATTN + 1], small_mix[ROW_GCONV:ROW_GCONV + 1]], axis=1),
        jnp.concatenate([small_mix[ROW_CW0:ROW_CW0 + 1], small_mix[ROW_CW0 + 1:ROW_CW0 + 2]], axis=1),
        jnp.concatenate([small_mix[ROW_CW0 + 2:ROW_CW0 + 3], dsink[0:1, :], z(D_MODEL - CONV_W - 128)], axis=1),
    ]
    return jnp.concatenate(rows, axis=0)


def kernel(x, pre_mix_norm, w_in, conv_w, attn_sinks, attn_group_norm, conv_group_norm, w_out, post_mix_norm, pre_mlp_norm, w_up, w_down, post_mlp_norm, loss_target, m_pre_mix_norm, m_w_in, m_conv_w, m_attn_sinks, m_attn_group_norm, m_conv_group_norm, m_w_out, m_post_mix_norm, m_pre_mlp_norm, m_w_up, m_w_down, m_post_mlp_norm, v_pre_mix_norm, v_w_in, v_conv_w, v_attn_sinks, v_attn_group_norm, v_conv_group_norm, v_w_out, v_post_mix_norm, v_pre_mlp_norm, v_w_up, v_w_down, v_post_mlp_norm):
    xi, yi, ci = _mesh_pos()
    chip = 2 * xi + yi
    dev = 2 * chip + ci

    gw_in, gw_out, gw_up, gw_down, gconv = _all_gather(
        [w_in[0].astype(BF16), w_out[0].astype(BF16), w_up[0].astype(BF16), w_down[0].astype(BF16), conv_w[0]],
        "gather_weights")
    w_in_full = gw_in.transpose(1, 0, 2).reshape(D_MODEL, IN_COLS)
    w_out_full = gw_out.reshape(D_MODEL, D_MODEL)
    conv_full = gconv.transpose(1, 0, 2).reshape(3, CONV_W)

    grad_x, dw_in, dw_out, dw_up, dw_down, smalls = _local_grads(
        x[0], loss_target[0], pre_mix_norm, w_in_full, conv_full, attn_sinks, attn_group_norm, conv_group_norm,
        w_out_full, post_mix_norm, pre_mlp_norm, gw_up, gw_down, post_mlp_norm)

    grads = [_by_dest(dw_in, False), _by_dest(dw_out, True), dw_up, dw_down]
    from_sibling = _sibling_exchange(grads, "reduce_sibling")
    pos = jnp.stack([ci, chip]).astype(jnp.int32)
    summed = [_pair_sum(g, r, pos, ADAM_ROWS) for g, r in zip(grads, from_sibling)]
    from_chips = _chip_exchange(summed, "reduce_chips")

    small = _sum_devices(_all_gather([_pack_small(*smalls)], "gather_small")[0])
    loss = (0.5 / D_MODEL) * jnp.sum(small[0])

    big = {}
    for name, w, m, v, g, rs, rc in zip(
            ("w_in", "w_out", "w_up", "w_down"), (w_in, w_out, w_up, w_down), (m_w_in, m_w_out, m_w_up, m_w_down),
            (v_w_in, v_w_out, v_w_up, v_w_down), grads, from_sibling, from_chips):
        big[name] = [a[None] for a in _adamw_shard(w[0], m[0], v[0], g, rs, rc, pos, ADAM_ROWS)]

    conv_g = lax.dynamic_slice(
        jnp.stack([small[6, :CONV_W], small[6, CONV_W:], small[7, :CONV_W]]), (0, dev * (CONV_W // N_DEV)),
        (3, CONV_W // N_DEV))
    pad = lambda a, n: jnp.pad(a.reshape(1, -1), ((0, 0), (0, n - a.size)))
    small_names = ("pre_mix_norm", "post_mix_norm", "pre_mlp_norm", "post_mlp_norm")
    small_w = {"pre_mix_norm": (pre_mix_norm, m_pre_mix_norm, v_pre_mix_norm),
               "post_mix_norm": (post_mix_norm, m_post_mix_norm, v_post_mix_norm),
               "pre_mlp_norm": (pre_mlp_norm, m_pre_mlp_norm, v_pre_mlp_norm),
               "post_mlp_norm": (post_mlp_norm, m_post_mlp_norm, v_post_mlp_norm)}

    def pack(k):
        rows = [small_w[nm][k] for nm in small_names]
        rows.append(jnp.concatenate([(attn_group_norm, m_attn_group_norm, v_attn_group_norm)[k],
                                     (conv_group_norm, m_conv_group_norm, v_conv_group_norm)[k]], axis=1))
        rows.append(pad((conv_w, m_conv_w, v_conv_w)[k], D_MODEL))
        rows.append(pad((attn_sinks, m_attn_sinks, v_attn_sinks)[k], D_MODEL))
        rows.append(jnp.zeros((1, D_MODEL), F32))
        return jnp.concatenate(rows, axis=0)

    g_small = jnp.concatenate(
        [small[1:6], pad(conv_g, D_MODEL), pad(small[7, CONV_W:CONV_W + N_HEADS], D_MODEL), jnp.zeros((1, D_MODEL), F32)],
        axis=0)
    d_small, nm_small, nv_small = _adamw_small(pack(0), g_small, pack(1), pack(2))

    def unpack(a):
        nconv = 3 * CONV_W // N_DEV
        return {"pre_mix_norm": a[0:1], "post_mix_norm": a[1:2], "pre_mlp_norm": a[2:3], "post_mlp_norm": a[3:4],
                "attn_group_norm": a[4:5, :ATTN_W], "conv_group_norm": a[4:5, ATTN_W:],
                "conv_w": a[5, :nconv].reshape(1, 3, CONV_W // N_DEV), "attn_sinks": a[6:7, :N_HEADS]}

    order = ("pre_mix_norm", "w_in", "conv_w", "attn_sinks", "attn_group_norm", "conv_group_norm", "w_out",
             "post_mix_norm", "pre_mlp_norm", "w_up", "w_down", "post_mlp_norm")
    outs = []
    for k, a in enumerate((g_small, d_small, nm_small, nv_small)):
        sm = unpack(a)
        outs += [big[nm][k] if nm in big else sm[nm] for nm in order]
    return (loss, grad_x[None], *outs)
```

```python
import functools

import jax
import jax.numpy as jnp
import numpy as np
from jax import lax
from jax.experimental import pallas as pl
from jax.experimental.pallas import tpu as pltpu

F32 = jnp.float32
BF16 = jnp.bfloat16

D_MODEL = 1024
HEAD_DIM = 64
ATTN_W = 512
CONV_W = 512
N_HEADS = 8
N_KV = 2
GROUP = 4
KV_W = 128
QKV_W = ATTN_W + 2 * KV_W
GATES_W = 3 * CONV_W
IN_COLS = QKV_W + GATES_W
D_FF = 4096
FF_CHUNK = 512
N_FF_CHUNKS = D_FF // FF_CHUNK
BLOCK = 128
ROT_HALF = 8
ROPE_THETA = 500000.0
NORM_EPS = 1e-6
NEG_INF = -1e30
ATTN_SCALE = 0.125
N_DEV = 8
N_CHIPS = 4
IN_SHARD = IN_COLS // N_DEV

ADAM_LR = 0.001
ADAM_B1 = 0.9
ADAM_B2 = 0.999
ADAM_EPS = 1e-08
ADAM_WD = 0.01
ADAM_STEP = 10

V7X_VMEM_BYTES = 64 * 1024 * 1024
VMEM_LIMIT = V7X_VMEM_BYTES - 8 * 1024 * 1024

MESH = pl.DeviceIdType.MESH
HBM_SPEC = pl.BlockSpec(memory_space=pltpu.HBM)


def _params(*sem):
    return pltpu.CompilerParams(dimension_semantics=sem, vmem_limit_bytes=VMEM_LIMIT)


def _mm(a, b):
    return jnp.dot(a, b, preferred_element_type=F32)


def _mm_nt(a, b):
    return lax.dot_general(a, b, (((1,), (1,)), ((), ())), preferred_element_type=F32)


def _mm_tn(a, b):
    return lax.dot_general(a, b, (((0,), (0,)), ((), ())), preferred_element_type=F32)


def _inv_rms(x):
    return lax.rsqrt(jnp.mean(x * x, axis=-1, keepdims=True) + NORM_EPS)


def _rms_bwd(xhat, r, gain, dy):
    gy = dy * gain
    return r * (gy - xhat * jnp.mean(gy * xhat, axis=-1, keepdims=True)), dy * xhat


def _colsum(a):
    return jnp.sum(a, axis=0, keepdims=True)


def _full(shape):
    zeros = (0,) * len(shape)
    return pl.BlockSpec(shape, lambda *_: zeros)


def _resident(shape):
    zeros = (0,) * len(shape)
    return pl.BlockSpec(shape, lambda *_: zeros, pipeline_mode=pl.Buffered(1))


def _rope_tables(t):
    pos = np.arange(t, dtype=np.float32)
    inv_freq = (ROPE_THETA ** (-np.arange(0, 2 * ROT_HALF, 2, dtype=np.float64) / (2 * ROT_HALF))).astype(np.float32)
    ang = (pos[:, None] * inv_freq[None, :]).astype(np.float64)
    cos, sin = np.cos(ang).astype(np.float32), np.sin(ang).astype(np.float32)
    zeros8 = np.zeros((t, ROT_HALF), np.float32)
    rest = np.zeros((t, HEAD_DIM - 2 * ROT_HALF), np.float32)
    c_head = np.concatenate([cos, cos, rest + 1.0], axis=1)
    s1_head = np.concatenate([zeros8, sin, rest], axis=1)
    s2_head = np.concatenate([-sin, zeros8, rest], axis=1)
    two = lambda a: jnp.asarray(np.concatenate([a, a], axis=1))
    return two(c_head), two(s1_head), two(s2_head)


def _rope(v, c, s1, s2):
    return v * c + pltpu.roll(v, ROT_HALF, 1) * s1 + pltpu.roll(v, 128 - ROT_HALF, 1) * s2


def _rope_transpose(dv, c, s1, s2):
    return dv * c + pltpu.roll(dv * s1, 128 - ROT_HALF, 1) + pltpu.roll(dv * s2, ROT_HALF, 1)


def _shift_rows_down(u, prev, k):
    row = lax.broadcasted_iota(jnp.int32, u.shape, 0)
    out = pltpu.roll(u, k, 0)
    for r in range(k):
        out = jnp.where(row == r, prev[8 - k + r:8 - k + r + 1, :], out)
    return out


def _shift_rows_up(u, nxt, k):
    n = u.shape[0]
    row = lax.broadcasted_iota(jnp.int32, u.shape, 0)
    out = pltpu.roll(u, n - k, 0)
    for r in range(k):
        out = jnp.where(row == n - k + r, nxt[r:r + 1, :], out)
    return out


def _conv3(u, u1, u2, w):
    return (w[0:1, :] * u2 + w[1:2, :] * u1) + w[2:3, :] * u


def _mesh_pos():
    return lax.axis_index("x"), lax.axis_index("y"), lax.axis_index("c")


def _slot(ref, pos):
    return ref.at[4 * pos[0] + 2 * pos[1] + pos[2]]


def _push(src, dst, sems, k, to):
    send_sems, recv_sems = sems
    return pltpu.make_async_remote_copy(src_ref=src, dst_ref=dst, send_sem=send_sems.at[k], recv_sem=recv_sems.at[k],
                                        device_id=to, device_id_type=MESH)


def _gather_near(first, last, shards, outs, sems, local_sems):
    x, y, c = _mesh_pos()
    me, peers = (x, y, c), [(x, y, 1 - c), (1 - x, y, c), (x, 1 - y, c)]
    n = len(shards)
    local = [pltpu.make_async_copy(shards[i], _slot(outs[i], me), local_sems.at[i]) for i in range(n)]
    sends = [_push(shards[i], _slot(outs[i], me), sems, 3 * i + k, peers[k]) for i in range(n) for k in range(3)]
    arrivals = [_push(shards[i], _slot(outs[i], peers[k]), sems, 3 * i + k, peers[k]) for i in range(n) for k in range(3)]

    @pl.when(first)
    def _():
        for cp in local + sends:
            cp.start()

    @pl.when(last)
    def _():
        for cp in sends:
            cp.wait_send()
        for cp in arrivals:
            cp.wait_recv()
        for cp in local:
            cp.wait()


def _gather_far(first, last, shards, ins, outs, sems):
    x, y, c = _mesh_pos()
    me, sibling = (x, y, c), (x, y, 1 - c)
    chips = [(1 - x, y), (x, 1 - y), (1 - x, 1 - y)]
    n = len(shards)
    diag_send = [_push(shards[i], _slot(outs[i], me), sems, 4 * i, (*chips[2], c)) for i in range(n)]
    diag_arrival = [_push(shards[i], _slot(outs[i], (*chips[2], c)), sems, 4 * i, (*chips[2], c)) for i in range(n)]
    passed = [[_push(_slot(ins[i], (*chips[j], c)), _slot(outs[i], (*chips[j], c)), sems, 4 * i + 1 + j, sibling)
               for i in range(n)] for j in range(3)]
    from_sibling = [_push(shards[i], _slot(outs[i], (*chips[j], 1 - c)), sems, 4 * i + 1 + j, sibling)
                    for i in range(n) for j in range(3)]

    @pl.when(first)
    def _():
        for cp in diag_send + passed[0] + passed[1]:
            cp.start()

    @pl.when(last)
    def _():
        for cp in diag_arrival:
            cp.wait_recv()
        for cp in passed[2]:
            cp.start()
        for cp in from_sibling:
            cp.wait_recv()
        for cp in diag_send + passed[0] + passed[1] + passed[2]:
            cp.wait_send()


def _in_proj_fwd(x, g1, w_in, conv_w, g_conv, rope, tm, shards):
    t = x.shape[0]
    rc, rs1, rs2 = rope
    n = len(shards)

    def body(*refs):
        x_ref, g1_ref, w_ref, cw_ref, gc_ref, c_ref, s1_ref, s2_ref = refs[:8]
        shard_refs = refs[8:8 + n]
        qkv_ref, gates_ref, mconv_ref = refs[8 + n:11 + n]
        gathered = refs[11 + n:11 + 2 * n]
        carry_ref = refs[11 + 2 * n]
        if n:
            step = pl.program_id(0)
            _gather_near(step == 0, step == pl.num_programs(0) - 1, shard_refs, gathered, refs[12 + 2 * n:14 + 2 * n],
                         refs[14 + 2 * n])

        @pl.when(pl.program_id(0) == 0)
        def _():
            carry_ref[...] = jnp.zeros_like(carry_ref)

        xv = x_ref[...]
        hn = ((xv * _inv_rms(xv)) * g1_ref[...]).astype(BF16)
        proj = _mm(hn, w_ref[...])
        c, s1, s2 = c_ref[...], s1_ref[...], s2_ref[...]
        for ci in range((ATTN_W + KV_W) // 128):
            sl = slice(128 * ci, 128 * (ci + 1))
            qkv_ref[:, sl] = _rope(proj[:, sl], c, s1, s2).astype(BF16)
        qkv_ref[:, ATTN_W + KV_W:QKV_W] = proj[:, ATTN_W + KV_W:QKV_W].astype(BF16)
        gates = proj[:, QKV_W:]
        gates_ref[...] = gates
        gb, gcc, xin = gates[:, :CONV_W], gates[:, CONV_W:2 * CONV_W], gates[:, 2 * CONV_W:]
        u = gcc * xin
        prev = carry_ref[...]
        conv = gb * _conv3(u, _shift_rows_down(u, prev, 1), _shift_rows_down(u, prev, 2), cw_ref[...])
        carry_ref[...] = u[tm - 8:tm, :]
        mconv_ref[...] = ((conv * _inv_rms(conv)) * gc_ref[...]).astype(BF16)

    tile = lambda w_: pl.BlockSpec((tm, w_), lambda i: (i, 0))
    comm_scratch = [pltpu.SemaphoreType.DMA((3 * n,)), pltpu.SemaphoreType.DMA((3 * n,)), pltpu.SemaphoreType.DMA((n,))]
    res = pl.pallas_call(
        body, name="in_proj_fwd", grid=(t // tm,),
        in_specs=[tile(D_MODEL), _full((1, D_MODEL)), _full((D_MODEL, IN_COLS)), _full((3, CONV_W)), _full((1, CONV_W)),
                  tile(128), tile(128), tile(128)] + [HBM_SPEC] * n,
        out_specs=[tile(QKV_W), tile(GATES_W), tile(CONV_W)] + [HBM_SPEC] * n,
        out_shape=[jax.ShapeDtypeStruct((t, QKV_W), BF16), jax.ShapeDtypeStruct((t, GATES_W), F32),
                   jax.ShapeDtypeStruct((t, CONV_W), BF16)]
        + [jax.ShapeDtypeStruct((N_DEV,) + s.shape, s.dtype) for s in shards],
        scratch_shapes=[pltpu.VMEM((8, CONV_W), F32)] + (comm_scratch if n else []),
        compiler_params=_params("arbitrary"),
    )(x, g1, w_in, conv_w, g_conv, rc, rs1, rs2, *shards)
    return res[0], res[1], res[2], list(res[3:])


GROUP_ROWS = GROUP * BLOCK


def _attn_mask(j):
    row = lax.broadcasted_iota(jnp.int32, (GROUP_ROWS, 2 * BLOCK), 0) & (BLOCK - 1)
    col = lax.broadcasted_iota(jnp.int32, (GROUP_ROWS, 2 * BLOCK), 1)
    return (col > row) & (col <= row + BLOCK) & ((col >= BLOCK) | (j > 0))


def _stack_heads(a, g):
    return jnp.concatenate([a[:, HEAD_DIM * (GROUP * g + hh):HEAD_DIM * (GROUP * g + hh + 1)] for hh in range(GROUP)], axis=0)


def _unstack_heads(a):
    return jnp.concatenate([a[BLOCK * hh:BLOCK * (hh + 1), :] for hh in range(GROUP)], axis=1)


def _group_sinks(sink_ref, g):
    head = lax.broadcasted_iota(jnp.int32, (GROUP_ROWS, 1), 0) // BLOCK
    out = jnp.full((GROUP_ROWS, 1), sink_ref[0, GROUP * g], F32)
    for hh in range(1, GROUP):
        out = jnp.where(head == hh, sink_ref[0, GROUP * g + hh], out)
    return out


def _attn_probs(qs, kk, sink, valid):
    s = jnp.where(valid, _mm_nt(qs, kk) * ATTN_SCALE, NEG_INF)
    m = jnp.maximum(jnp.max(s, axis=-1, keepdims=True), sink)
    p = jnp.exp(s - m)
    psink = jnp.exp(sink - m)
    inv_l = 1.0 / (jnp.sum(p, axis=-1, keepdims=True) + psink)
    return p * inv_l, psink * inv_l


def _qkv_specs(order):
    prev = lambda i: jnp.maximum(order(i) - 1, 0)
    kcol, vcol = ATTN_W // KV_W, ATTN_W // KV_W + 1
    return [pl.BlockSpec((BLOCK, ATTN_W), lambda i: (order(i), 0)),
            pl.BlockSpec((BLOCK, KV_W), lambda i: (prev(i), kcol)), pl.BlockSpec((BLOCK, KV_W), lambda i: (order(i), kcol)),
            pl.BlockSpec((BLOCK, KV_W), lambda i: (prev(i), vcol)), pl.BlockSpec((BLOCK, KV_W), lambda i: (order(i), vcol))]


def _attn_fwd(qkv, sinks, g_attn, shards, gathered):
    t = qkv.shape[0]
    n = len(shards)

    def body(*refs):
        sink_ref, q_ref, kp_ref, kc_ref, vp_ref, vc_ref, ga_ref = refs[:7]
        attn_ref, mattn_ref = refs[7 + 2 * n:9 + 2 * n]
        if n:
            step = pl.program_id(0)
            _gather_far(step == 0, step == pl.num_programs(0) - 1, refs[7:7 + n], refs[7 + n:7 + 2 * n],
                        refs[9 + 2 * n:9 + 3 * n], refs[9 + 3 * n:11 + 3 * n])
        valid = _attn_mask(pl.program_id(0))
        q, kp, kc, vp, vc = q_ref[...], kp_ref[...], kc_ref[...], vp_ref[...], vc_ref[...]
        outs = []
        for g in range(N_KV):
            gs = slice(HEAD_DIM * g, HEAD_DIM * (g + 1))
            kk = jnp.concatenate([kp[:, gs], kc[:, gs]], axis=0)
            vv = jnp.concatenate([vp[:, gs], vc[:, gs]], axis=0)
            probs, _ = _attn_probs(_stack_heads(q, g), kk, _group_sinks(sink_ref, g), valid)
            outs.append(_unstack_heads(_mm(probs.astype(BF16), vv)))
        attn = jnp.concatenate(outs, axis=1)
        attn_ref[...] = attn
        mattn_ref[...] = ((attn * _inv_rms(attn)) * ga_ref[...]).astype(BF16)

    blk = pl.BlockSpec((BLOCK, ATTN_W), lambda j: (j, 0))
    res = pl.pallas_call(
        body, name="attn_fwd", grid=(t // BLOCK,),
        in_specs=[pl.BlockSpec(memory_space=pltpu.SMEM)] + _qkv_specs(lambda j: j) + [_full((1, ATTN_W))]
        + [HBM_SPEC] * (2 * n),
        out_specs=[blk, blk] + [HBM_SPEC] * n,
        out_shape=[jax.ShapeDtypeStruct((t, ATTN_W), F32), jax.ShapeDtypeStruct((t, ATTN_W), BF16)]
        + [jax.ShapeDtypeStruct(g.shape, g.dtype) for g in gathered],
        input_output_aliases={7 + n + i: 2 + i for i in range(n)},
        scratch_shapes=[pltpu.SemaphoreType.DMA((4 * n,)), pltpu.SemaphoreType.DMA((4 * n,))] if n else [],
        compiler_params=_params("arbitrary"),
    )(sinks, qkv, qkv, qkv, qkv, qkv, g_attn, *shards, *gathered)
    return res[0], res[1], list(res[2:])


SMALL_ROWS = 8
ROW_LOSS, ROW_G2, ROW_G3, ROW_G4 = 0, 1, 2, 3


def _mid(mattn, mconv, x, target, g2, g3, g4, w_out, w_up, w_down, tm):
    t = x.shape[0]

    def body(ma_ref, mc_ref, x_ref, t_ref, g2_ref, g3_ref, g4_ref, wo_ref, wu_ref, wd_ref,
             act_ref, dup_ref, hn2_ref, dmo_ref, dmix_ref, dh_ref, dmixed_ref, small_ref, up_ref):
        @pl.when(pl.program_id(0) == 0)
        def _():
            small_ref[...] = jnp.zeros_like(small_ref)

        g2, g3, g4 = g2_ref[...], g3_ref[...], g4_ref[...]
        mix_out = _mm(ma_ref[...], wo_ref[0:ATTN_W, :]) + _mm(mc_ref[...], wo_ref[ATTN_W:, :])
        r2 = _inv_rms(mix_out)
        mo_hat = mix_out * r2
        h = x_ref[...] + mo_hat * g2
        r3 = _inv_rms(h)
        h_hat = h * r3
        hn2 = (h_hat * g3).astype(BF16)
        hn2_ref[...] = hn2
        mlp = jnp.zeros((tm, D_MODEL), F32)
        for j in range(N_FF_CHUNKS):
            up = jnp.maximum(_mm(hn2, wu_ref[j]), 0.0)
            up_ref[j] = up
            act = (up * up).astype(BF16)
            act_ref[:, FF_CHUNK * j:FF_CHUNK * (j + 1)] = act
            mlp = mlp + _mm(act, wd_ref[j])
        r4 = _inv_rms(mlp)
        ml_hat = mlp * r4
        err = (h + ml_hat * g4) - t_ref[...]
        d_out = err * (1.0 / D_MODEL)
        d_mlp, dg4 = _rms_bwd(ml_hat, r4, g4, d_out)
        dmo = d_mlp.astype(BF16)
        dmo_ref[...] = dmo
        dhn2 = jnp.zeros((tm, D_MODEL), F32)
        for j in range(N_FF_CHUNKS):
            dup = (_mm_nt(dmo, wd_ref[j]) * (2.0 * up_ref[j])).astype(BF16)
            dup_ref[:, FF_CHUNK * j:FF_CHUNK * (j + 1)] = dup
            dhn2 = dhn2 + _mm_nt(dup, wu_ref[j])
        dh_norm, dg3 = _rms_bwd(h_hat, r3, g3, dhn2)
        dh = d_out + dh_norm
        dh_ref[...] = dh
        d_mix, dg2 = _rms_bwd(mo_hat, r2, g2, dh)
        dmix = d_mix.astype(BF16)
        dmix_ref[...] = dmix
        dmixed_ref[...] = _mm_nt(dmix, wo_ref[...])
        small_ref[ROW_LOSS:ROW_LOSS + 1, :] += _colsum(err * err)
        small_ref[ROW_G2:ROW_G2 + 1, :] += _colsum(dg2)
        small_ref[ROW_G3:ROW_G3 + 1, :] += _colsum(dg3)
        small_ref[ROW_G4:ROW_G4 + 1, :] += _colsum(dg4)

    tile = lambda n: pl.BlockSpec((tm, n), lambda i: (i, 0))
    gain = _full((1, D_MODEL))
    return pl.pallas_call(
        body, name="mid_fwd_bwd", grid=(t // tm,),
        in_specs=[tile(ATTN_W), tile(CONV_W), tile(D_MODEL), tile(D_MODEL), gain, gain, gain,
                  _resident((D_MODEL, D_MODEL)), _resident((N_FF_CHUNKS, D_MODEL, FF_CHUNK)),
                  _resident((N_FF_CHUNKS, FF_CHUNK, D_MODEL))],
        out_specs=[tile(D_FF), tile(D_FF), tile(D_MODEL), tile(D_MODEL), tile(D_MODEL), tile(D_MODEL), tile(D_MODEL),
                   _full((SMALL_ROWS, D_MODEL))],
        out_shape=[jax.ShapeDtypeStruct((t, D_FF), BF16), jax.ShapeDtypeStruct((t, D_FF), BF16),
                   jax.ShapeDtypeStruct((t, D_MODEL), BF16), jax.ShapeDtypeStruct((t, D_MODEL), BF16),
                   jax.ShapeDtypeStruct((t, D_MODEL), BF16), jax.ShapeDtypeStruct((t, D_MODEL), F32),
                   jax.ShapeDtypeStruct((t, D_MODEL), F32), jax.ShapeDtypeStruct((SMALL_ROWS, D_MODEL), F32)],
        scratch_shapes=[pltpu.VMEM((N_FF_CHUNKS, tm, FF_CHUNK), F32)],
        compiler_params=_params("arbitrary"),
    )(mattn, mconv, x, target, g2, g3, g4, w_out, w_up, w_down)


def _device_pos(d):
    return d // 4, (d // 2) % 2, d % 2


def _dw_scatter(a, b, me, tk, a_chunked, name):
    t = a.shape[0]
    n_k = t // tk
    rows, cols = (FF_CHUNK, D_MODEL) if a_chunked else (D_MODEL, FF_CHUNK)
    dest = lambda s, me_ref: (s + me_ref[0] + 1) % N_DEV

    def body(me_ref, a_ref, b_ref, o_ref, recv_ref, send_buf, send_sems, recv_sems):
        s_now, k = pl.program_id(0), pl.program_id(1)
        x, y, c = _mesh_pos()
        mine = 4 * x + 2 * y + c

        def send(s):
            return _push(send_buf.at[s], recv_ref.at[mine], (send_sems, recv_sems), s,
                         _device_pos((mine + 1 + s) % N_DEV))

        def arrival(s):
            src = (mine + N_DEV - 1 - s) % N_DEV
            return _push(send_buf.at[s], recv_ref.at[src], (send_sems, recv_sems), s, _device_pos(src))

        @pl.when(k == 0)
        def _():
            o_ref[...] = jnp.zeros_like(o_ref)

        o_ref[...] += _mm_tn(a_ref[...], b_ref[...])
        for s in range(N_DEV - 1):
            @pl.when((s_now == s) & (k == n_k - 1))
            def _():
                send_buf[s] = o_ref[...].astype(BF16)
                send(s).start()

        @pl.when((s_now == N_DEV - 1) & (k == n_k - 1))
        def _():
            for s in range(N_DEV - 1):
                send(s).wait_send()
                arrival(s).wait_recv()

    chunk = pl.BlockSpec((tk, FF_CHUNK), lambda s, k, me_ref: (k, dest(s, me_ref)))
    whole = pl.BlockSpec((tk, D_MODEL), lambda s, k, me_ref: (k, 0))
    return pl.pallas_call(
        body, name=name,
        grid_spec=pltpu.PrefetchScalarGridSpec(
            num_scalar_prefetch=1, grid=(N_DEV, n_k),
            in_specs=[chunk, whole] if a_chunked else [whole, chunk],
            out_specs=[pl.BlockSpec((None, rows, cols), lambda s, k, me_ref: (dest(s, me_ref), 0, 0)), HBM_SPEC],
            scratch_shapes=[pltpu.VMEM((N_DEV - 1, rows, cols), BF16), pltpu.SemaphoreType.DMA((N_DEV - 1,)),
                            pltpu.SemaphoreType.DMA((N_DEV - 1,))]),
        out_shape=[jax.ShapeDtypeStruct((N_DEV, rows, cols), F32), jax.ShapeDtypeStruct((N_DEV, rows, cols), BF16)],
        compiler_params=_params("arbitrary", "arbitrary"),
    )(me, a, b)


def _dw_out(mattn, mconv, dmix, tk):
    t = dmix.shape[0]

    def body(ma_ref, mc_ref, b_ref, o_ref):
        @pl.when(pl.program_id(0) == 0)
        def _():
            o_ref[...] = jnp.zeros_like(o_ref)
        b = b_ref[...]
        o_ref[0:ATTN_W, :] += _mm_tn(ma_ref[...], b)
        o_ref[ATTN_W:, :] += _mm_tn(mc_ref[...], b)

    tile = lambda n: pl.BlockSpec((tk, n), lambda k: (k, 0))
    return pl.pallas_call(
        body, name="dw_out", grid=(t // tk,),
        in_specs=[tile(ATTN_W), tile(CONV_W), tile(D_MODEL)],
        out_specs=_full((D_MODEL, D_MODEL)),
        out_shape=jax.ShapeDtypeStruct((D_MODEL, D_MODEL), F32),
        compiler_params=_params("arbitrary"),
    )(mattn, mconv, dmix)


ROW_GATTN, ROW_GCONV, ROW_CW0 = 0, 1, 2


def _mix_bwd(dmixed, attn, gates, g_attn, g_conv, conv_w, tm):
    t = attn.shape[0]
    n = t // tm
    rev = lambda i: n - 1 - i

    def body(dm_ref, attn_ref, gates_ref, gprev_ref, ga_ref, gc_ref, cw_ref, dattn_ref, dgates_ref, small_ref, carry_ref):
        i = pl.program_id(0)

        @pl.when(i == 0)
        def _():
            small_ref[...] = jnp.zeros_like(small_ref)
            carry_ref[...] = jnp.zeros_like(carry_ref)

        dm = dm_ref[...]
        a = attn_ref[...]
        ra = _inv_rms(a)
        a_hat = a * ra
        dattn, dga = _rms_bwd(a_hat, ra, ga_ref[...], dm[:, :ATTN_W])
        dattn_ref[...] = dattn

        gates = gates_ref[...]
        gb, gcc, xin = gates[:, :CONV_W], gates[:, CONV_W:2 * CONV_W], gates[:, 2 * CONV_W:]
        u = gcc * xin
        gp = gprev_ref[...]
        uprev = jnp.where(rev(i) == 0, 0.0, gp[:, CONV_W:2 * CONV_W] * gp[:, 2 * CONV_W:])
        u1, u2 = _shift_rows_down(u, uprev, 1), _shift_rows_down(u, uprev, 2)
        w = cw_ref[...]
        c = _conv3(u, u1, u2, w)
        conv = gb * c
        rcv = _inv_rms(conv)
        c_hat = conv * rcv
        dconv, dgc = _rms_bwd(c_hat, rcv, gc_ref[...], dm[:, ATTN_W:])
        dc = dconv * gb
        nxt = carry_ref[...]
        du = (w[2:3, :] * dc + w[1:2, :] * _shift_rows_up(dc, nxt, 1)) + w[0:1, :] * _shift_rows_up(dc, nxt, 2)
        carry_ref[...] = dc[0:8, :]
        dgates_ref[:, :CONV_W] = (dconv * c).astype(BF16)
        dgates_ref[:, CONV_W:2 * CONV_W] = (du * xin).astype(BF16)
        dgates_ref[:, 2 * CONV_W:] = (du * gcc).astype(BF16)
        small_ref[ROW_GATTN:ROW_GATTN + 1, :] += _colsum(dga)
        small_ref[ROW_GCONV:ROW_GCONV + 1, :] += _colsum(dgc)
        small_ref[ROW_CW0:ROW_CW0 + 1, :] += _colsum(dc * u2)
        small_ref[ROW_CW0 + 1:ROW_CW0 + 2, :] += _colsum(dc * u1)
        small_ref[ROW_CW0 + 2:ROW_CW0 + 3, :] += _colsum(dc * u)

    tile = lambda w_: pl.BlockSpec((tm, w_), lambda i: (rev(i), 0))
    prev8 = pl.BlockSpec((8, GATES_W), lambda i: (jnp.maximum(rev(i) * (tm // 8) - 1, 0), 0))
    return pl.pallas_call(
        body, name="mix_bwd", grid=(n,),
        in_specs=[tile(D_MODEL), tile(ATTN_W), tile(GATES_W), prev8, _full((1, ATTN_W)), _full((1, CONV_W)),
                  _full((3, CONV_W))],
        out_specs=[tile(ATTN_W), tile(GATES_W), _full((SMALL_ROWS, CONV_W))],
        out_shape=[jax.ShapeDtypeStruct((t, ATTN_W), F32), jax.ShapeDtypeStruct((t, GATES_W), BF16),
                   jax.ShapeDtypeStruct((SMALL_ROWS, CONV_W), F32)],
        scratch_shapes=[pltpu.VMEM((8, CONV_W), F32)],
        compiler_params=_params("arbitrary"),
    )(dmixed, attn, gates, gates, g_attn, g_conv, conv_w)


def _attn_bwd(qkv, dattn, sinks, rope):
    t = qkv.shape[0]
    nb = t // BLOCK
    rev = lambda i: nb - 1 - i
    rc, rs1, rs2 = rope

    def body(sink_ref, q_ref, kp_ref, kc_ref, vp_ref, vc_ref, do_ref, c_ref, s1_ref, s2_ref,
             dqkv_ref, dsink_ref, ck_ref, cv_ref):
        i = pl.program_id(0)

        @pl.when(i == 0)
        def _():
            dsink_ref[...] = jnp.zeros_like(dsink_ref)
            ck_ref[...] = jnp.zeros_like(ck_ref)
            cv_ref[...] = jnp.zeros_like(cv_ref)

        valid = _attn_mask(rev(i))
        q, kp, kc, vp, vc = q_ref[...], kp_ref[...], kc_ref[...], vp_ref[...], vc_ref[...]
        dout = do_ref[...].astype(BF16)
        lane = lax.broadcasted_iota(jnp.int32, (1, 128), 1)
        dsink = jnp.zeros((1, 128), F32)
        dq_parts, dk_parts, dv_parts = [], [], []
        for g in range(N_KV):
            gs = slice(HEAD_DIM * g, HEAD_DIM * (g + 1))
            kk = jnp.concatenate([kp[:, gs], kc[:, gs]], axis=0)
            vv = jnp.concatenate([vp[:, gs], vc[:, gs]], axis=0)
            qs, dos = _stack_heads(q, g), _stack_heads(dout, g)
            probs, psink = _attn_probs(qs, kk, _group_sinks(sink_ref, g), valid)
            dp = _mm_nt(dos, vv)
            delta = jnp.sum(probs * dp, axis=-1, keepdims=True)
            ds = (probs * (dp - delta) * ATTN_SCALE).astype(BF16)
            sink_terms = psink * delta
            for hh in range(GROUP):
                dsink = dsink + jnp.where(lane == GROUP * g + hh, -jnp.sum(sink_terms[BLOCK * hh:BLOCK * (hh + 1), :]), 0.0)
            dq_parts.append(_unstack_heads(_mm(ds, kk)))
            dk_parts.append(_mm_tn(ds, qs))
            dv_parts.append(_mm_tn(probs.astype(BF16), dos))
        dk2 = jnp.concatenate(dk_parts, axis=1)
        dv2 = jnp.concatenate(dv_parts, axis=1)
        dk = dk2[BLOCK:, :] + ck_ref[...]
        dv = dv2[BLOCK:, :] + cv_ref[...]
        ck_ref[...] = dk2[:BLOCK, :]
        cv_ref[...] = dv2[:BLOCK, :]
        c, s1, s2 = c_ref[...], s1_ref[...], s2_ref[...]
        dq = jnp.concatenate(dq_parts, axis=1)
        for ci in range(ATTN_W // 128):
            sl = slice(128 * ci, 128 * (ci + 1))
            dqkv_ref[:, sl] = _rope_transpose(dq[:, sl], c, s1, s2).astype(BF16)
        dqkv_ref[:, ATTN_W:ATTN_W + KV_W] = _rope_transpose(dk, c, s1, s2).astype(BF16)
        dqkv_ref[:, ATTN_W + KV_W:] = dv.astype(BF16)
        dsink_ref[0:1, :] += dsink

    blk = lambda w_: pl.BlockSpec((BLOCK, w_), lambda i: (rev(i), 0))
    return pl.pallas_call(
        body, name="attn_bwd", grid=(nb,),
        in_specs=[pl.BlockSpec(memory_space=pltpu.SMEM)] + _qkv_specs(rev) + [blk(ATTN_W), blk(128), blk(128), blk(128)],
        out_specs=[blk(QKV_W), _full((8, 128))],
        out_shape=[jax.ShapeDtypeStruct((t, QKV_W), BF16), jax.ShapeDtypeStruct((8, 128), F32)],
        scratch_shapes=[pltpu.VMEM((BLOCK, KV_W), F32), pltpu.VMEM((BLOCK, KV_W), F32)],
        compiler_params=_params("arbitrary"),
    )(sinks, qkv, qkv, qkv, qkv, qkv, dattn, rc, rs1, rs2)


def _in_proj_bwd(dqkv, dgates, x, dh, g1, w_in, tm):
    t = x.shape[0]

    def body(dq_ref, dg_ref, x_ref, dh_ref, g1_ref, w_ref, dx_ref, dwa_ref, dwb_ref, dg1_ref):
        @pl.when(pl.program_id(0) == 0)
        def _():
            dwa_ref[...] = jnp.zeros_like(dwa_ref)
            dwb_ref[...] = jnp.zeros_like(dwb_ref)
            dg1_ref[...] = jnp.zeros_like(dg1_ref)

        dq, dg = dq_ref[...], dg_ref[...]
        dhn = _mm_nt(dq, w_ref[:, :QKV_W]) + _mm_nt(dg, w_ref[:, QKV_W:])
        xv = x_ref[...]
        r = _inv_rms(xv)
        x_hat = xv * r
        g1 = g1_ref[...]
        dx, dg1 = _rms_bwd(x_hat, r, g1, dhn)
        dx_ref[...] = dh_ref[...] + dx
        hn = (x_hat * g1).astype(BF16)
        dwa_ref[...] += _mm_tn(hn, dq)
        dwb_ref[...] += _mm_tn(hn, dg)
        dg1_ref[0:1, :] += _colsum(dg1)

    tile = lambda n: pl.BlockSpec((tm, n), lambda i: (i, 0))
    return pl.pallas_call(
        body, name="in_proj_bwd", grid=(t // tm,),
        in_specs=[tile(QKV_W), tile(GATES_W), tile(D_MODEL), tile(D_MODEL), _full((1, D_MODEL)),
                  _resident((D_MODEL, IN_COLS))],
        out_specs=[tile(D_MODEL), _full((D_MODEL, QKV_W)), _full((D_MODEL, GATES_W)), _full((SMALL_ROWS, D_MODEL))],
        out_shape=[jax.ShapeDtypeStruct((t, D_MODEL), F32), jax.ShapeDtypeStruct((D_MODEL, QKV_W), F32),
                   jax.ShapeDtypeStruct((D_MODEL, GATES_W), F32), jax.ShapeDtypeStruct((SMALL_ROWS, D_MODEL), F32)],
        compiler_params=_params("arbitrary"),
    )(dqkv, dgates, x, dh, g1, w_in)


def _all_gather(shards, name):
    n = len(shards)

    def body(*refs):
        ins, outs = refs[:n], refs[n:2 * n]
        send_sems, recv_sems, local_sems = refs[2 * n:]
        x, y, c = _mesh_pos()
        me, sibling = (x, y, c), (x, y, 1 - c)
        chips = [(1 - x, y), (x, 1 - y), (1 - x, 1 - y)]

        def copy(i, k, block, to, src=None):
            dst = outs[i].at[4 * block[0] + 2 * block[1] + block[2]]
            return pltpu.make_async_remote_copy(
                src_ref=dst if src is None else src, dst_ref=dst, send_sem=send_sems.at[7 * i + k],
                recv_sem=recv_sems.at[7 * i + k], device_id=to, device_id_type=MESH)

        mine = [pltpu.make_async_copy(ins[i], outs[i].at[4 * x + 2 * y + c], local_sems.at[i]) for i in range(n)]
        for cp in mine:
            cp.start()
        first = []
        for i in range(n):
            first.append(copy(i, 0, me, sibling, src=ins[i]))
            first += [copy(i, 1 + j, me, (*chip, c), src=ins[i]) for j, chip in enumerate(chips)]
        for cp in first:
            cp.start()
        passed = []
        for j, chip in enumerate(chips):
            for i in range(n):
                copy(i, 1 + j, (*chip, c), me).wait_recv()
                cp = copy(i, 4 + j, (*chip, c), sibling)
                cp.start()
                passed.append(cp)
        for i in range(n):
            copy(i, 0, sibling, me).wait_recv()
            for j, chip in enumerate(chips):
                copy(i, 4 + j, (*chip, 1 - c), me).wait_recv()
        for cp in first + passed:
            cp.wait_send()
        for cp in mine:
            cp.wait()

    return pl.pallas_call(
        body, name=name,
        in_specs=[HBM_SPEC] * n, out_specs=[HBM_SPEC] * n,
        out_shape=[jax.ShapeDtypeStruct((N_DEV,) + s.shape, s.dtype) for s in shards],
        scratch_shapes=[pltpu.SemaphoreType.DMA((7 * n,)), pltpu.SemaphoreType.DMA((7 * n,)),
                        pltpu.SemaphoreType.DMA((n,))],
    )(*shards)


def _sibling_exchange(grads, name):
    n = len(grads)

    def body(*refs):
        ins, outs = refs[:n], refs[n:2 * n]
        send_sems, recv_sems = refs[2 * n:]
        x, y, c = _mesh_pos()
        copies = [pltpu.make_async_remote_copy(
            src_ref=ins[i].at[1 - c], dst_ref=outs[i], send_sem=send_sems.at[i], recv_sem=recv_sems.at[i],
            device_id=(x, y, 1 - c), device_id_type=MESH) for i in range(n)]
        for cp in copies:
            cp.start()
        for cp in copies:
            cp.wait()

    return pl.pallas_call(
        body, name=name,
        in_specs=[HBM_SPEC] * n, out_specs=[HBM_SPEC] * n,
        out_shape=[jax.ShapeDtypeStruct(g.shape[1:], g.dtype) for g in grads],
        scratch_shapes=[pltpu.SemaphoreType.DMA((n,)), pltpu.SemaphoreType.DMA((n,))],
    )(*grads)


def _chip_exchange(sums, name):
    n = len(sums)

    def body(*refs):
        ins, outs = refs[:n], refs[n:2 * n]
        send_sems, recv_sems = refs[2 * n:]
        x, y, c = _mesh_pos()
        chips = [(1 - x, y), (x, 1 - y), (1 - x, 1 - y)]
        copies = [pltpu.make_async_remote_copy(
            src_ref=ins[i].at[2 * chip[0] + chip[1]], dst_ref=outs[i].at[k], send_sem=send_sems.at[3 * i + k],
            recv_sem=recv_sems.at[3 * i + k], device_id=(*chip, c), device_id_type=MESH)
            for i in range(n) for k, chip in enumerate(chips)]
        for cp in copies:
            cp.start()
        for cp in copies:
            cp.wait()

    return pl.pallas_call(
        body, name=name,
        in_specs=[HBM_SPEC] * n, out_specs=[HBM_SPEC] * n,
        out_shape=[jax.ShapeDtypeStruct((3,) + s.shape[1:], s.dtype) for s in sums],
        scratch_shapes=[pltpu.SemaphoreType.DMA((3 * n,)), pltpu.SemaphoreType.DMA((3 * n,))],
    )(*sums)


def _pair_sum(grad, recv, pos, tr):
    _, _, rows, cols = grad.shape

    def body(pos_ref, g_ref, r_ref, sb_ref):
        sb_ref[...] = (g_ref[...] + r_ref[...]).astype(BF16)

    return pl.pallas_call(
        body, name="pair_sum",
        grid_spec=pltpu.PrefetchScalarGridSpec(
            num_scalar_prefetch=1, grid=(N_CHIPS, rows // tr),
            in_specs=[pl.BlockSpec((None, None, tr, cols), lambda p, i, pos: (pos[0], p, i, 0)),
                      pl.BlockSpec((None, tr, cols), lambda p, i, pos: (p, i, 0))],
            out_specs=pl.BlockSpec((None, tr, cols), lambda p, i, pos: (p, i, 0))),
        out_shape=jax.ShapeDtypeStruct((N_CHIPS, rows, cols), BF16),
        compiler_params=_params("parallel", "parallel"),
    )(pos, grad, recv)


def _adam_math(w, g, m, v):
    m = ADAM_B1 * m + (1.0 - ADAM_B1) * g
    v = ADAM_B2 * v + (1.0 - ADAM_B2) * (g * g)
    m_hat = m / (1.0 - ADAM_B1 ** ADAM_STEP)
    v_hat = v / (1.0 - ADAM_B2 ** ADAM_STEP)
    delta = -ADAM_LR * (m_hat / (jnp.sqrt(v_hat) + ADAM_EPS) + ADAM_WD * w)
    return delta, m, v


def _adamw_shard(w, m, v, grad, from_sibling, from_chips, pos, tr):
    rows, cols = w.shape

    def body(pos_ref, w_ref, m_ref, v_ref, own_ref, sib_ref, r_ref, g_ref, d_ref, nm_ref, nv_ref):
        g = own_ref[...] + sib_ref[...]
        for k in range(3):
            g = g + r_ref[k].astype(F32)
        g_ref[...] = g
        d_ref[...], nm_ref[...], nv_ref[...] = _adam_math(w_ref[...], g, m_ref[...], v_ref[...])

    tile = pl.BlockSpec((tr, cols), lambda i, pos: (i, 0))
    out = jax.ShapeDtypeStruct((rows, cols), F32)
    return pl.pallas_call(
        body, name="adamw_shard",
        grid_spec=pltpu.PrefetchScalarGridSpec(
            num_scalar_prefetch=1, grid=(rows // tr,),
            in_specs=[tile, tile, tile,
                      pl.BlockSpec((None, None, tr, cols), lambda i, pos: (pos[0], pos[1], i, 0)),
                      pl.BlockSpec((None, tr, cols), lambda i, pos: (pos[1], i, 0)),
                      pl.BlockSpec((3, tr, cols), lambda i, pos: (0, i, 0))],
            out_specs=[tile] * 4),
        out_shape=[out] * 4,
        compiler_params=_params("parallel"),
    )(pos, w, m, v, grad, from_sibling, from_chips)


def _adamw_scattered(w, m, v, grad, recv, me, tr):
    rows, cols = w.shape
    n_peers = N_DEV - 1

    def body(me_ref, w_ref, m_ref, v_ref, own_ref, *rest):
        peers, (g_ref, d_ref, nm_ref, nv_ref) = rest[:n_peers], rest[n_peers:]
        g = own_ref[...]
        for r_ref in peers:
            g = g + r_ref[...].astype(F32)
        g_ref[...] = g
        d_ref[...], nm_ref[...], nv_ref[...] = _adam_math(w_ref[...], g, m_ref[...], v_ref[...])

    tile = pl.BlockSpec((tr, cols), lambda i, me_ref: (i, 0))
    behind = lambda s: pl.BlockSpec((None, tr, cols), lambda i, me_ref: ((me_ref[0] + N_DEV - 1 - s) % N_DEV, i, 0))
    out = jax.ShapeDtypeStruct((rows, cols), F32)
    return pl.pallas_call(
        body, name="adamw_scattered",
        grid_spec=pltpu.PrefetchScalarGridSpec(
            num_scalar_prefetch=1, grid=(rows // tr,),
            in_specs=[tile, tile, tile, pl.BlockSpec((None, tr, cols), lambda i, me_ref: (me_ref[0], i, 0))]
            + [behind(s) for s in range(n_peers)],
            out_specs=[tile] * 4),
        out_shape=[out] * 4,
        compiler_params=_params("parallel"),
    )(me, w, m, v, grad, *([recv] * n_peers))


def _sum_devices(gathered):
    _, rows, cols = gathered.shape

    def body(g_ref, o_ref):
        s = g_ref[0]
        for d in range(1, N_DEV):
            s = s + g_ref[d]
        o_ref[...] = s

    return pl.pallas_call(
        body, name="sum_devices", in_specs=[_full(gathered.shape)], out_specs=_full((rows, cols)), grid=(1,),
        out_shape=jax.ShapeDtypeStruct((rows, cols), F32),
    )(gathered)


def _adamw_small(w, g, m, v):
    def body(w_ref, g_ref, m_ref, v_ref, d_ref, nm_ref, nv_ref):
        d_ref[...], nm_ref[...], nv_ref[...] = _adam_math(w_ref[...], g_ref[...], m_ref[...], v_ref[...])

    spec = _full(w.shape)
    out = jax.ShapeDtypeStruct(w.shape, F32)
    return pl.pallas_call(
        body, name="adamw_small", grid=(1,), in_specs=[spec] * 4, out_specs=[spec] * 3, out_shape=[out] * 3,
    )(w, g, m, v)


TOKEN_TILE = 512
MID_TILE = 256
DW_TILE = 1024
ADAM_ROWS = 128


def _local_grads(x, target, g1, w_in, conv_w, sinks, g_attn, g_conv, g2, g3, g4, shards, me):
    t = x.shape[0]
    tm = min(TOKEN_TILE, t)
    rope = _rope_tables(t)
    qkv, gates, mconv, gathered = _in_proj_fwd(x, g1, w_in, conv_w, g_conv, rope, tm, shards)
    attn, mattn, (w_out, w_up, w_down) = _attn_fwd(qkv, sinks, g_attn, shards, gathered)
    act, dup, hn2, dmo, dmix, dh, dmixed, small_mid = _mid(
        mattn, mconv, x, target, g2, g3, g4, w_out.reshape(D_MODEL, D_MODEL), w_up, w_down, min(MID_TILE, t))
    tk = min(DW_TILE, t)
    dw_up = _dw_scatter(hn2, dup, me, tk, False, "dw_up")
    dw_down = _dw_scatter(act, dmo, me, tk, True, "dw_down")
    dw_out = _dw_out(mattn, mconv, dmix, tk)
    dattn, dgates, small_mix = _mix_bwd(dmixed, attn, gates, g_attn, g_conv, conv_w, tm)
    dqkv, dsink = _attn_bwd(qkv, dattn, sinks, rope)
    grad_x, dwa, dwb, small_in = _in_proj_bwd(dqkv, dgates, x, dh, g1, w_in, tm)
    dw_in = jnp.concatenate([dwa, dwb], axis=1)
    return grad_x, dw_in, dw_out, dw_up, dw_down, (small_mid, small_mix, dsink, small_in)


def _by_dest(a, rows_major):
    r, c = a.shape
    if rows_major:
        return a.reshape(N_CHIPS, 2, r // N_DEV, c).transpose(1, 0, 2, 3)
    return a.reshape(r, N_CHIPS, 2, c // N_DEV).transpose(2, 1, 0, 3)


def _pack_small(small_mid, small_mix, dsink, small_in):
    z = lambda n: jnp.zeros((1, n), F32)
    rows = [
        small_mid[ROW_LOSS:ROW_LOSS + 1],
        small_in[0:1],
        small_mid[ROW_G2:ROW_G2 + 1],
        small_mid[ROW_G3:ROW_G3 + 1],
        small_mid[ROW_G4:ROW_G4 + 1],
        jnp.concatenate([small_mix[ROW_GATTN:ROW_GATTN + 1], small_mix[ROW_GCONV:ROW_GCONV + 1]], axis=1),
        jnp.concatenate([small_mix[ROW_CW0:ROW_CW0 + 1], small_mix[ROW_CW0 + 1:ROW_CW0 + 2]], axis=1),
        jnp.concatenate([small_mix[ROW_CW0 + 2:ROW_CW0 + 3], dsink[0:1, :], z(D_MODEL - CONV_W - 128)], axis=1),
    ]
    return jnp.concatenate(rows, axis=0)


def kernel(x, pre_mix_norm, w_in, conv_w, attn_sinks, attn_group_norm, conv_group_norm, w_out, post_mix_norm, pre_mlp_norm, w_up, w_down, post_mlp_norm, loss_target, m_pre_mix_norm, m_w_in, m_conv_w, m_attn_sinks, m_attn_group_norm, m_conv_group_norm, m_w_out, m_post_mix_norm, m_pre_mlp_norm, m_w_up, m_w_down, m_post_mlp_norm, v_pre_mix_norm, v_w_in, v_conv_w, v_attn_sinks, v_attn_group_norm, v_conv_group_norm, v_w_out, v_post_mix_norm, v_pre_mlp_norm, v_w_up, v_w_down, v_post_mlp_norm):
    xi, yi, ci = _mesh_pos()
    chip = 2 * xi + yi
    dev = 2 * chip + ci

    me = jnp.stack([dev]).astype(jnp.int32)
    pos = jnp.stack([ci, chip]).astype(jnp.int32)

    gw_in, gconv = _all_gather([w_in[0].astype(BF16), conv_w[0]], "gather_w_in")
    w_in_full = gw_in.transpose(1, 0, 2).reshape(D_MODEL, IN_COLS)
    conv_full = gconv.transpose(1, 0, 2).reshape(3, CONV_W)
    shards = [w_out[0].astype(BF16), w_up[0].astype(BF16), w_down[0].astype(BF16)]

    grad_x, dw_in, dw_out, dw_up, dw_down, smalls = _local_grads(
        x[0], loss_target[0], pre_mix_norm, w_in_full, conv_full, attn_sinks, attn_group_norm, conv_group_norm,
        post_mix_norm, pre_mlp_norm, post_mlp_norm, shards, me)

    grads = [_by_dest(dw_in, False), _by_dest(dw_out, True)]
    from_sibling = _sibling_exchange(grads, "reduce_sibling")
    summed = [_pair_sum(g, r, pos, ADAM_ROWS) for g, r in zip(grads, from_sibling)]
    from_chips = _chip_exchange(summed, "reduce_chips")

    small = _sum_devices(_all_gather([_pack_small(*smalls)], "gather_small")[0])
    loss = (0.5 / D_MODEL) * jnp.sum(small[0])

    big = {}
    for name, w, m, v, g, rs, rc in zip(("w_in", "w_out"), (w_in, w_out), (m_w_in, m_w_out), (v_w_in, v_w_out), grads,
                                        from_sibling, from_chips):
        big[name] = [a[None] for a in _adamw_shard(w[0], m[0], v[0], g, rs, rc, pos, ADAM_ROWS)]
    for name, w, m, v, (g, recv) in zip(("w_up", "w_down"), (w_up, w_down), (m_w_up, m_w_down), (v_w_up, v_w_down),
                                        (dw_up, dw_down)):
        big[name] = [a[None] for a in _adamw_scattered(w[0], m[0], v[0], g, recv, me, ADAM_ROWS)]

    conv_g = lax.dynamic_slice(
        jnp.stack([small[6, :CONV_W], small[6, CONV_W:], small[7, :CONV_W]]), (0, dev * (CONV_W // N_DEV)),
        (3, CONV_W // N_DEV))
    pad = lambda a, n: jnp.pad(a.reshape(1, -1), ((0, 0), (0, n - a.size)))
    small_names = ("pre_mix_norm", "post_mix_norm", "pre_mlp_norm", "post_mlp_norm")
    small_w = {"pre_mix_norm": (pre_mix_norm, m_pre_mix_norm, v_pre_mix_norm),
               "post_mix_norm": (post_mix_norm, m_post_mix_norm, v_post_mix_norm),
               "pre_mlp_norm": (pre_mlp_norm, m_pre_mlp_norm, v_pre_mlp_norm),
               "post_mlp_norm": (post_mlp_norm, m_post_mlp_norm, v_post_mlp_norm)}

    def pack(k):
        rows = [small_w[nm][k] for nm in small_names]
        rows.append(jnp.concatenate([(attn_group_norm, m_attn_group_norm, v_attn_group_norm)[k],
                                     (conv_group_norm, m_conv_group_norm, v_conv_group_norm)[k]], axis=1))
        rows.append(pad((conv_w, m_conv_w, v_conv_w)[k], D_MODEL))
        rows.append(pad((attn_sinks, m_attn_sinks, v_attn_sinks)[k], D_MODEL))
        rows.append(jnp.zeros((1, D_MODEL), F32))
        return jnp.concatenate(rows, axis=0)

    g_small = jnp.concatenate(
        [small[1:6], pad(conv_g, D_MODEL), pad(small[7, CONV_W:CONV_W + N_HEADS], D_MODEL), jnp.zeros((1, D_MODEL), F32)],
        axis=0)
    d_small, nm_small, nv_small = _adamw_small(pack(0), g_small, pack(1), pack(2))

    def unpack(a):
        nconv = 3 * CONV_W // N_DEV
        return {"pre_mix_norm": a[0:1], "post_mix_norm": a[1:2], "pre_mlp_norm": a[2:3], "post_mlp_norm": a[3:4],
                "attn_group_norm": a[4:5, :ATTN_W], "conv_group_norm": a[4:5, ATTN_W:],
                "conv_w": a[5, :nconv].reshape(1, 3, CONV_W // N_DEV), "attn_sinks": a[6:7, :N_HEADS]}

    order = ("pre_mix_norm", "w_in", "conv_w", "attn_sinks", "attn_group_norm", "conv_group_norm", "w_out",
             "post_mix_norm", "pre_mlp_norm", "w_up", "w_down", "post_mlp_norm")
    outs = []
    for k, a in enumerate((g_small, d_small, nm_small, nv_small)):
        sm = unpack(a)
        outs += [big[nm][k] if nm in big else sm[nm] for nm in order]
    return (loss, grad_x[None], *outs)
```

```python
import functools

import jax
import jax.numpy as jnp
import numpy as np
from jax import lax
from jax.experimental import pallas as pl
from jax.experimental.pallas import tpu as pltpu

F32 = jnp.float32
BF16 = jnp.bfloat16

D_MODEL = 1024
HEAD_DIM = 64
ATTN_W = 512
CONV_W = 512
N_HEADS = 8
N_KV = 2
GROUP = 4
KV_W = 128
QKV_W = ATTN_W + 2 * KV_W
GATES_W = 3 * CONV_W
IN_COLS = QKV_W + GATES_W
D_FF = 4096
FF_CHUNK = 512
N_FF_CHUNKS = D_FF // FF_CHUNK
BLOCK = 128
ROT_HALF = 8
ROPE_THETA = 500000.0
NORM_EPS = 1e-6
NEG_INF = -1e30
ATTN_SCALE = 0.125
N_DEV = 8
N_CHIPS = 4
IN_SHARD = IN_COLS // N_DEV

ADAM_LR = 0.001
ADAM_B1 = 0.9
ADAM_B2 = 0.999
ADAM_EPS = 1e-08
ADAM_WD = 0.01
ADAM_STEP = 10

V7X_VMEM_BYTES = 64 * 1024 * 1024
VMEM_LIMIT = V7X_VMEM_BYTES - 2 * 1024 * 1024

MESH = pl.DeviceIdType.MESH
HBM_SPEC = pl.BlockSpec(memory_space=pltpu.HBM)


def _params(*sem):
    return pltpu.CompilerParams(dimension_semantics=sem, vmem_limit_bytes=VMEM_LIMIT)


def _mm(a, b):
    return jnp.dot(a, b, preferred_element_type=F32)


def _mm_nt(a, b):
    return lax.dot_general(a, b, (((1,), (1,)), ((), ())), preferred_element_type=F32)


def _mm_tn(a, b):
    return lax.dot_general(a, b, (((0,), (0,)), ((), ())), preferred_element_type=F32)


def _inv_rms(x):
    return lax.rsqrt(jnp.mean(x * x, axis=-1, keepdims=True) + NORM_EPS)


def _rms_bwd(xhat, r, gain, dy):
    gy = dy * gain
    return r * (gy - xhat * jnp.mean(gy * xhat, axis=-1, keepdims=True)), dy * xhat


def _colsum(a):
    return jnp.sum(a, axis=0, keepdims=True)


def _full(shape):
    zeros = (0,) * len(shape)
    return pl.BlockSpec(shape, lambda *_: zeros)


def _resident(shape):
    zeros = (0,) * len(shape)
    return pl.BlockSpec(shape, lambda *_: zeros, pipeline_mode=pl.Buffered(1))


def _rope_tables(t):
    pos = np.arange(t, dtype=np.float32)
    inv_freq = (ROPE_THETA ** (-np.arange(0, 2 * ROT_HALF, 2, dtype=np.float64) / (2 * ROT_HALF))).astype(np.float32)
    ang = (pos[:, None] * inv_freq[None, :]).astype(np.float64)
    cos, sin = np.cos(ang).astype(np.float32), np.sin(ang).astype(np.float32)
    zeros8 = np.zeros((t, ROT_HALF), np.float32)
    rest = np.zeros((t, HEAD_DIM - 2 * ROT_HALF), np.float32)
    c_head = np.concatenate([cos, cos, rest + 1.0], axis=1)
    s1_head = np.concatenate([zeros8, sin, rest], axis=1)
    s2_head = np.concatenate([-sin, zeros8, rest], axis=1)
    two = lambda a: jnp.asarray(np.concatenate([a, a], axis=1))
    return two(c_head), two(s1_head), two(s2_head)


def _rope(v, c, s1, s2):
    return v * c + pltpu.roll(v, ROT_HALF, 1) * s1 + pltpu.roll(v, 128 - ROT_HALF, 1) * s2


def _rope_transpose(dv, c, s1, s2):
    return dv * c + pltpu.roll(dv * s1, 128 - ROT_HALF, 1) + pltpu.roll(dv * s2, ROT_HALF, 1)


def _shift_rows_down(u, prev, k):
    row = lax.broadcasted_iota(jnp.int32, u.shape, 0)
    out = pltpu.roll(u, k, 0)
    for r in range(k):
        out = jnp.where(row == r, prev[8 - k + r:8 - k + r + 1, :], out)
    return out


def _shift_rows_up(u, nxt, k):
    n = u.shape[0]
    row = lax.broadcasted_iota(jnp.int32, u.shape, 0)
    out = pltpu.roll(u, n - k, 0)
    for r in range(k):
        out = jnp.where(row == n - k + r, nxt[r:r + 1, :], out)
    return out


def _conv3(u, u1, u2, w):
    return (w[0:1, :] * u2 + w[1:2, :] * u1) + w[2:3, :] * u


def _mesh_pos():
    return lax.axis_index("x"), lax.axis_index("y"), lax.axis_index("c")


def _slot(ref, pos):
    return ref.at[4 * pos[0] + 2 * pos[1] + pos[2]]


def _push(src, dst, sems, k, to):
    send_sems, recv_sems = sems
    return pltpu.make_async_remote_copy(src_ref=src, dst_ref=dst, send_sem=send_sems.at[k], recv_sem=recv_sems.at[k],
                                        device_id=to, device_id_type=MESH)


def _gather_near(first, last, shards, outs, sems, local_sems):
    x, y, c = _mesh_pos()
    me, peers = (x, y, c), [(x, y, 1 - c), (1 - x, y, c), (x, 1 - y, c)]
    n = len(shards)
    local = [pltpu.make_async_copy(shards[i], _slot(outs[i], me), local_sems.at[i]) for i in range(n)]
    sends = [_push(shards[i], _slot(outs[i], me), sems, 3 * i + k, peers[k]) for i in range(n) for k in range(3)]
    arrivals = [_push(shards[i], _slot(outs[i], peers[k]), sems, 3 * i + k, peers[k]) for i in range(n) for k in range(3)]

    @pl.when(first)
    def _():
        for cp in local + sends:
            cp.start()

    @pl.when(last)
    def _():
        for cp in sends:
            cp.wait_send()
        for cp in arrivals:
            cp.wait_recv()
        for cp in local:
            cp.wait()


def _gather_far(first, last, shards, ins, outs, sems):
    x, y, c = _mesh_pos()
    me, sibling = (x, y, c), (x, y, 1 - c)
    chips = [(1 - x, y), (x, 1 - y), (1 - x, 1 - y)]
    n = len(shards)
    diag_send = [_push(shards[i], _slot(outs[i], me), sems, 4 * i, (*chips[2], c)) for i in range(n)]
    diag_arrival = [_push(shards[i], _slot(outs[i], (*chips[2], c)), sems, 4 * i, (*chips[2], c)) for i in range(n)]
    passed = [[_push(_slot(ins[i], (*chips[j], c)), _slot(outs[i], (*chips[j], c)), sems, 4 * i + 1 + j, sibling)
               for i in range(n)] for j in range(3)]
    from_sibling = [_push(shards[i], _slot(outs[i], (*chips[j], 1 - c)), sems, 4 * i + 1 + j, sibling)
                    for i in range(n) for j in range(3)]

    @pl.when(first)
    def _():
        for cp in diag_send + passed[0] + passed[1]:
            cp.start()

    @pl.when(last)
    def _():
        for cp in diag_arrival:
            cp.wait_recv()
        for cp in passed[2]:
            cp.start()
        for cp in from_sibling:
            cp.wait_recv()
        for cp in diag_send + passed[0] + passed[1] + passed[2]:
            cp.wait_send()


def _in_proj_fwd(x, g1, w_in, conv_w, g_conv, rope, tm, shards):
    t = x.shape[0]
    rc, rs1, rs2 = rope
    n = len(shards)

    def body(*refs):
        x_ref, g1_ref, w_ref, cw_ref, gc_ref, c_ref, s1_ref, s2_ref = refs[:8]
        shard_refs = refs[8:8 + n]
        qkv_ref, gates_ref, mconv_ref = refs[8 + n:11 + n]
        gathered = refs[11 + n:11 + 2 * n]
        carry_ref = refs[11 + 2 * n]
        if n:
            step = pl.program_id(0)
            _gather_near(step == 0, step == pl.num_programs(0) - 1, shard_refs, gathered, refs[12 + 2 * n:14 + 2 * n],
                         refs[14 + 2 * n])

        @pl.when(pl.program_id(0) == 0)
        def _():
            carry_ref[...] = jnp.zeros_like(carry_ref)

        xv = x_ref[...]
        hn = ((xv * _inv_rms(xv)) * g1_ref[...]).astype(BF16)
        proj = _mm(hn, w_ref[...])
        c, s1, s2 = c_ref[...], s1_ref[...], s2_ref[...]
        for ci in range((ATTN_W + KV_W) // 128):
            sl = slice(128 * ci, 128 * (ci + 1))
            qkv_ref[:, sl] = _rope(proj[:, sl], c, s1, s2).astype(BF16)
        qkv_ref[:, ATTN_W + KV_W:QKV_W] = proj[:, ATTN_W + KV_W:QKV_W].astype(BF16)
        gates = proj[:, QKV_W:]
        gates_ref[...] = gates
        gb, gcc, xin = gates[:, :CONV_W], gates[:, CONV_W:2 * CONV_W], gates[:, 2 * CONV_W:]
        u = gcc * xin
        prev = carry_ref[...]
        conv = gb * _conv3(u, _shift_rows_down(u, prev, 1), _shift_rows_down(u, prev, 2), cw_ref[...])
        carry_ref[...] = u[tm - 8:tm, :]
        mconv_ref[...] = ((conv * _inv_rms(conv)) * gc_ref[...]).astype(BF16)

    tile = lambda w_: pl.BlockSpec((tm, w_), lambda i: (i, 0))
    comm_scratch = [pltpu.SemaphoreType.DMA((3 * n,)), pltpu.SemaphoreType.DMA((3 * n,)), pltpu.SemaphoreType.DMA((n,))]
    res = pl.pallas_call(
        body, name="in_proj_fwd", grid=(t // tm,),
        in_specs=[tile(D_MODEL), _full((1, D_MODEL)), _full((D_MODEL, IN_COLS)), _full((3, CONV_W)), _full((1, CONV_W)),
                  tile(128), tile(128), tile(128)] + [HBM_SPEC] * n,
        out_specs=[tile(QKV_W), tile(GATES_W), tile(CONV_W)] + [HBM_SPEC] * n,
        out_shape=[jax.ShapeDtypeStruct((t, QKV_W), BF16), jax.ShapeDtypeStruct((t, GATES_W), F32),
                   jax.ShapeDtypeStruct((t, CONV_W), BF16)]
        + [jax.ShapeDtypeStruct((N_DEV,) + s.shape, s.dtype) for s in shards],
        scratch_shapes=[pltpu.VMEM((8, CONV_W), F32)] + (comm_scratch if n else []),
        compiler_params=_params("arbitrary"),
    )(x, g1, w_in, conv_w, g_conv, rc, rs1, rs2, *shards)
    return res[0], res[1], res[2], list(res[3:])


GROUP_ROWS = GROUP * BLOCK


def _attn_mask(j):
    row = lax.broadcasted_iota(jnp.int32, (GROUP_ROWS, 2 * BLOCK), 0) & (BLOCK - 1)
    col = lax.broadcasted_iota(jnp.int32, (GROUP_ROWS, 2 * BLOCK), 1)
    return (col > row) & (col <= row + BLOCK) & ((col >= BLOCK) | (j > 0))


def _stack_heads(a, g):
    return jnp.concatenate([a[:, HEAD_DIM * (GROUP * g + hh):HEAD_DIM * (GROUP * g + hh + 1)] for hh in range(GROUP)], axis=0)


def _unstack_heads(a):
    return jnp.concatenate([a[BLOCK * hh:BLOCK * (hh + 1), :] for hh in range(GROUP)], axis=1)


def _group_sinks(sink_ref, g):
    head = lax.broadcasted_iota(jnp.int32, (GROUP_ROWS, 1), 0) // BLOCK
    out = jnp.full((GROUP_ROWS, 1), sink_ref[0, GROUP * g], F32)
    for hh in range(1, GROUP):
        out = jnp.where(head == hh, sink_ref[0, GROUP * g + hh], out)
    return out


def _attn_probs(qs, kk, sink, valid):
    s = jnp.where(valid, _mm_nt(qs, kk) * ATTN_SCALE, NEG_INF)
    m = jnp.maximum(jnp.max(s, axis=-1, keepdims=True), sink)
    p = jnp.exp(s - m)
    psink = jnp.exp(sink - m)
    inv_l = 1.0 / (jnp.sum(p, axis=-1, keepdims=True) + psink)
    return p * inv_l, psink * inv_l


def _qkv_specs(order):
    prev = lambda i: jnp.maximum(order(i) - 1, 0)
    kcol, vcol = ATTN_W // KV_W, ATTN_W // KV_W + 1
    return [pl.BlockSpec((BLOCK, ATTN_W), lambda i: (order(i), 0)),
            pl.BlockSpec((BLOCK, KV_W), lambda i: (prev(i), kcol)), pl.BlockSpec((BLOCK, KV_W), lambda i: (order(i), kcol)),
            pl.BlockSpec((BLOCK, KV_W), lambda i: (prev(i), vcol)), pl.BlockSpec((BLOCK, KV_W), lambda i: (order(i), vcol))]


def _attn_fwd(qkv, sinks, g_attn, shards, gathered):
    t = qkv.shape[0]
    n = len(shards)

    def body(*refs):
        sink_ref, q_ref, kp_ref, kc_ref, vp_ref, vc_ref, ga_ref = refs[:7]
        attn_ref, mattn_ref = refs[7 + 2 * n:9 + 2 * n]
        if n:
            step = pl.program_id(0)
            _gather_far(step == 0, step == pl.num_programs(0) - 1, refs[7:7 + n], refs[7 + n:7 + 2 * n],
                        refs[9 + 2 * n:9 + 3 * n], refs[9 + 3 * n:11 + 3 * n])
        valid = _attn_mask(pl.program_id(0))
        q, kp, kc, vp, vc = q_ref[...], kp_ref[...], kc_ref[...], vp_ref[...], vc_ref[...]
        outs = []
        for g in range(N_KV):
            gs = slice(HEAD_DIM * g, HEAD_DIM * (g + 1))
            kk = jnp.concatenate([kp[:, gs], kc[:, gs]], axis=0)
            vv = jnp.concatenate([vp[:, gs], vc[:, gs]], axis=0)
            probs, _ = _attn_probs(_stack_heads(q, g), kk, _group_sinks(sink_ref, g), valid)
            outs.append(_unstack_heads(_mm(probs.astype(BF16), vv)))
        attn = jnp.concatenate(outs, axis=1)
        attn_ref[...] = attn
        mattn_ref[...] = ((attn * _inv_rms(attn)) * ga_ref[...]).astype(BF16)

    blk = pl.BlockSpec((BLOCK, ATTN_W), lambda j: (j, 0))
    res = pl.pallas_call(
        body, name="attn_fwd", grid=(t // BLOCK,),
        in_specs=[pl.BlockSpec(memory_space=pltpu.SMEM)] + _qkv_specs(lambda j: j) + [_full((1, ATTN_W))]
        + [HBM_SPEC] * (2 * n),
        out_specs=[blk, blk] + [HBM_SPEC] * n,
        out_shape=[jax.ShapeDtypeStruct((t, ATTN_W), F32), jax.ShapeDtypeStruct((t, ATTN_W), BF16)]
        + [jax.ShapeDtypeStruct(g.shape, g.dtype) for g in gathered],
        input_output_aliases={7 + n + i: 2 + i for i in range(n)},
        scratch_shapes=[pltpu.SemaphoreType.DMA((4 * n,)), pltpu.SemaphoreType.DMA((4 * n,))] if n else [],
        compiler_params=_params("arbitrary"),
    )(sinks, qkv, qkv, qkv, qkv, qkv, g_attn, *shards, *gathered)
    return res[0], res[1], list(res[2:])


SMALL_ROWS = 8
ROW_LOSS, ROW_G2, ROW_G3, ROW_G4 = 0, 1, 2, 3


def _mid(mattn, mconv, x, target, g2, g3, g4, w_out, w_up, w_down, tm):
    t = x.shape[0]

    def body(ma_ref, mc_ref, x_ref, t_ref, g2_ref, g3_ref, g4_ref, wo_ref, wu_ref, wd_ref,
             actt_ref, dup_ref, hn2t_ref, dmo_ref, dmix_ref, dh_ref, dmixed_ref, small_ref, up_ref):
        @pl.when(pl.program_id(0) == 0)
        def _():
            small_ref[...] = jnp.zeros_like(small_ref)

        g2, g3, g4 = g2_ref[...], g3_ref[...], g4_ref[...]
        mix_out = _mm(ma_ref[...], wo_ref[0:ATTN_W, :]) + _mm(mc_ref[...], wo_ref[ATTN_W:, :])
        r2 = _inv_rms(mix_out)
        mo_hat = mix_out * r2
        h = x_ref[...] + mo_hat * g2
        r3 = _inv_rms(h)
        h_hat = h * r3
        hn2 = (h_hat * g3).astype(BF16)
        hn2t_ref[...] = hn2.T
        mlp = jnp.zeros((tm, D_MODEL), F32)
        for j in range(N_FF_CHUNKS):
            up = jnp.maximum(_mm(hn2, wu_ref[j]), 0.0)
            up_ref[j] = up.astype(BF16)
            act = (up * up).astype(BF16)
            actt_ref[FF_CHUNK * j:FF_CHUNK * (j + 1), :] = act.T
            mlp = mlp + _mm(act, wd_ref[j])
        r4 = _inv_rms(mlp)
        ml_hat = mlp * r4
        err = (h + ml_hat * g4) - t_ref[...]
        d_out = err * (1.0 / D_MODEL)
        d_mlp, dg4 = _rms_bwd(ml_hat, r4, g4, d_out)
        dmo = d_mlp.astype(BF16)
        dmo_ref[...] = dmo
        dhn2 = jnp.zeros((tm, D_MODEL), F32)
        for j in range(N_FF_CHUNKS):
            dup = (_mm_nt(dmo, wd_ref[j]) * (2.0 * up_ref[j].astype(F32))).astype(BF16)
            dup_ref[:, FF_CHUNK * j:FF_CHUNK * (j + 1)] = dup
            dhn2 = dhn2 + _mm_nt(dup, wu_ref[j])
        dh_norm, dg3 = _rms_bwd(h_hat, r3, g3, dhn2)
        dh = d_out + dh_norm
        dh_ref[...] = dh
        d_mix, dg2 = _rms_bwd(mo_hat, r2, g2, dh)
        dmix = d_mix.astype(BF16)
        dmix_ref[...] = dmix
        dmixed_ref[...] = _mm_nt(dmix, wo_ref[...])
        small_ref[ROW_LOSS:ROW_LOSS + 1, :] += _colsum(err * err)
        small_ref[ROW_G2:ROW_G2 + 1, :] += _colsum(dg2)
        small_ref[ROW_G3:ROW_G3 + 1, :] += _colsum(dg3)
        small_ref[ROW_G4:ROW_G4 + 1, :] += _colsum(dg4)

    tile = lambda n: pl.BlockSpec((tm, n), lambda i: (i, 0))
    cols = lambda n: pl.BlockSpec((n, tm), lambda i: (0, i))
    gain = _full((1, D_MODEL))
    return pl.pallas_call(
        body, name="mid_fwd_bwd", grid=(t // tm,),
        in_specs=[tile(ATTN_W), tile(CONV_W), tile(D_MODEL), tile(D_MODEL), gain, gain, gain,
                  _resident((D_MODEL, D_MODEL)), _resident((N_FF_CHUNKS, D_MODEL, FF_CHUNK)),
                  _resident((N_FF_CHUNKS, FF_CHUNK, D_MODEL))],
        out_specs=[cols(D_FF), tile(D_FF), cols(D_MODEL), tile(D_MODEL), tile(D_MODEL), tile(D_MODEL), tile(D_MODEL),
                   _full((SMALL_ROWS, D_MODEL))],
        out_shape=[jax.ShapeDtypeStruct((D_FF, t), BF16), jax.ShapeDtypeStruct((t, D_FF), BF16),
                   jax.ShapeDtypeStruct((D_MODEL, t), BF16), jax.ShapeDtypeStruct((t, D_MODEL), BF16),
                   jax.ShapeDtypeStruct((t, D_MODEL), BF16), jax.ShapeDtypeStruct((t, D_MODEL), F32),
                   jax.ShapeDtypeStruct((t, D_MODEL), F32), jax.ShapeDtypeStruct((SMALL_ROWS, D_MODEL), F32)],
        scratch_shapes=[pltpu.VMEM((N_FF_CHUNKS, tm, FF_CHUNK), BF16)],
        compiler_params=_params("arbitrary"),
    )(mattn, mconv, x, target, g2, g3, g4, w_out, w_up, w_down)


PEER_FLIPS = ((1, 1, 0), (1, 0, 0), (0, 1, 0), (1, 1, 1), (1, 0, 1), (0, 1, 1), (0, 0, 1))


def _peer_order(dev):
    masks = [4 * fx + 2 * fy + fc for fx, fy, fc in PEER_FLIPS] + [0]
    return jnp.bitwise_xor(dev, jnp.asarray(masks, jnp.int32)).astype(jnp.int32)


def _dw_scatter(at, b, order, tk, at_chunked, name):
    t = b.shape[0]
    n_k = t // tk
    rows, cols = (FF_CHUNK, D_MODEL) if at_chunked else (D_MODEL, FF_CHUNK)

    def body(order_ref, a_ref, b_ref, o_ref, recv_ref, send_buf, send_sems, recv_sems):
        s_now, k = pl.program_id(0), pl.program_id(1)
        x, y, c = _mesh_pos()
        mine = 4 * x + 2 * y + c

        def peer(s):
            fx, fy, fc = PEER_FLIPS[s]
            return (1 - x if fx else x, 1 - y if fy else y, 1 - c if fc else c)

        def send(s):
            return _push(send_buf.at[s], recv_ref.at[mine], (send_sems, recv_sems), s, peer(s))

        def arrival(s):
            return _push(send_buf.at[s], _slot(recv_ref, peer(s)), (send_sems, recv_sems), s, peer(s))

        @pl.when(k == 0)
        def _():
            o_ref[...] = jnp.zeros_like(o_ref)

        o_ref[...] += _mm(a_ref[...], b_ref[...])
        for s in range(N_DEV - 1):
            @pl.when((s_now == s) & (k == n_k - 1))
            def _():
                send_buf[s] = o_ref[...].astype(BF16)
                send(s).start()

        @pl.when((s_now == N_DEV - 1) & (k == n_k - 1))
        def _():
            for s in range(N_DEV - 1):
                send(s).wait_send()
                arrival(s).wait_recv()

    if at_chunked:
        in_specs = [pl.BlockSpec((FF_CHUNK, tk), lambda s, k, order_ref: (order_ref[s], k)),
                    pl.BlockSpec((tk, D_MODEL), lambda s, k, order_ref: (k, 0))]
    else:
        in_specs = [pl.BlockSpec((D_MODEL, tk), lambda s, k, order_ref: (0, k)),
                    pl.BlockSpec((tk, FF_CHUNK), lambda s, k, order_ref: (k, order_ref[s]))]
    return pl.pallas_call(
        body, name=name,
        grid_spec=pltpu.PrefetchScalarGridSpec(
            num_scalar_prefetch=1, grid=(N_DEV, n_k), in_specs=in_specs,
            out_specs=[pl.BlockSpec((None, rows, cols), lambda s, k, order_ref: (order_ref[s], 0, 0)), HBM_SPEC],
            scratch_shapes=[pltpu.VMEM((N_DEV - 1, rows, cols), BF16), pltpu.SemaphoreType.DMA((N_DEV - 1,)),
                            pltpu.SemaphoreType.DMA((N_DEV - 1,))]),
        out_shape=[jax.ShapeDtypeStruct((N_DEV, rows, cols), F32), jax.ShapeDtypeStruct((N_DEV, rows, cols), BF16)],
        compiler_params=_params("arbitrary", "arbitrary"),
    )(order, at, b)


def _dw_out(mattn, mconv, dmix, tk):
    t = dmix.shape[0]

    def body(ma_ref, mc_ref, b_ref, o_ref):
        @pl.when(pl.program_id(0) == 0)
        def _():
            o_ref[...] = jnp.zeros_like(o_ref)
        b = b_ref[...]
        o_ref[0:ATTN_W, :] += _mm_tn(ma_ref[...], b)
        o_ref[ATTN_W:, :] += _mm_tn(mc_ref[...], b)

    tile = lambda n: pl.BlockSpec((tk, n), lambda k: (k, 0))
    return pl.pallas_call(
        body, name="dw_out", grid=(t // tk,),
        in_specs=[tile(ATTN_W), tile(CONV_W), tile(D_MODEL)],
        out_specs=_full((D_MODEL, D_MODEL)),
        out_shape=jax.ShapeDtypeStruct((D_MODEL, D_MODEL), F32),
        compiler_params=_params("arbitrary"),
    )(mattn, mconv, dmix)


ROW_GATTN, ROW_GCONV, ROW_CW0 = 0, 1, 2


def _mix_bwd(dmixed, attn, gates, g_attn, g_conv, conv_w, tm):
    t = attn.shape[0]
    n = t // tm
    rev = lambda i: n - 1 - i

    def body(dm_ref, attn_ref, gates_ref, gprev_ref, ga_ref, gc_ref, cw_ref, dattn_ref, dgates_ref, small_ref, carry_ref):
        i = pl.program_id(0)

        @pl.when(i == 0)
        def _():
            small_ref[...] = jnp.zeros_like(small_ref)
            carry_ref[...] = jnp.zeros_like(carry_ref)

        dm = dm_ref[...]
        a = attn_ref[...]
        ra = _inv_rms(a)
        a_hat = a * ra
        dattn, dga = _rms_bwd(a_hat, ra, ga_ref[...], dm[:, :ATTN_W])
        dattn_ref[...] = dattn

        gates = gates_ref[...]
        gb, gcc, xin = gates[:, :CONV_W], gates[:, CONV_W:2 * CONV_W], gates[:, 2 * CONV_W:]
        u = gcc * xin
        gp = gprev_ref[...]
        uprev = jnp.where(rev(i) == 0, 0.0, gp[:, CONV_W:2 * CONV_W] * gp[:, 2 * CONV_W:])
        u1, u2 = _shift_rows_down(u, uprev, 1), _shift_rows_down(u, uprev, 2)
        w = cw_ref[...]
        c = _conv3(u, u1, u2, w)
        conv = gb * c
        rcv = _inv_rms(conv)
        c_hat = conv * rcv
        dconv, dgc = _rms_bwd(c_hat, rcv, gc_ref[...], dm[:, ATTN_W:])
        dc = dconv * gb
        nxt = carry_ref[...]
        du = (w[2:3, :] * dc + w[1:2, :] * _shift_rows_up(dc, nxt, 1)) + w[0:1, :] * _shift_rows_up(dc, nxt, 2)
        carry_ref[...] = dc[0:8, :]
        dgates_ref[:, :CONV_W] = (dconv * c).astype(BF16)
        dgates_ref[:, CONV_W:2 * CONV_W] = (du * xin).astype(BF16)
        dgates_ref[:, 2 * CONV_W:] = (du * gcc).astype(BF16)
        small_ref[ROW_GATTN:ROW_GATTN + 1, :] += _colsum(dga)
        small_ref[ROW_GCONV:ROW_GCONV + 1, :] += _colsum(dgc)
        small_ref[ROW_CW0:ROW_CW0 + 1, :] += _colsum(dc * u2)
        small_ref[ROW_CW0 + 1:ROW_CW0 + 2, :] += _colsum(dc * u1)
        small_ref[ROW_CW0 + 2:ROW_CW0 + 3, :] += _colsum(dc * u)

    tile = lambda w_: pl.BlockSpec((tm, w_), lambda i: (rev(i), 0))
    prev8 = pl.BlockSpec((8, GATES_W), lambda i: (jnp.maximum(rev(i) * (tm // 8) - 1, 0), 0))
    return pl.pallas_call(
        body, name="mix_bwd", grid=(n,),
        in_specs=[tile(D_MODEL), tile(ATTN_W), tile(GATES_W), prev8, _full((1, ATTN_W)), _full((1, CONV_W)),
                  _full((3, CONV_W))],
        out_specs=[tile(ATTN_W), tile(GATES_W), _full((SMALL_ROWS, CONV_W))],
        out_shape=[jax.ShapeDtypeStruct((t, ATTN_W), F32), jax.ShapeDtypeStruct((t, GATES_W), BF16),
                   jax.ShapeDtypeStruct((SMALL_ROWS, CONV_W), F32)],
        scratch_shapes=[pltpu.VMEM((8, CONV_W), F32)],
        compiler_params=_params("arbitrary"),
    )(dmixed, attn, gates, gates, g_attn, g_conv, conv_w)


def _attn_bwd(qkv, dattn, sinks, rope):
    t = qkv.shape[0]
    nb = t // BLOCK
    rev = lambda i: nb - 1 - i
    rc, rs1, rs2 = rope

    def body(sink_ref, q_ref, kp_ref, kc_ref, vp_ref, vc_ref, do_ref, c_ref, s1_ref, s2_ref,
             dqkv_ref, dsink_ref, ck_ref, cv_ref):
        i = pl.program_id(0)

        @pl.when(i == 0)
        def _():
            dsink_ref[...] = jnp.zeros_like(dsink_ref)
            ck_ref[...] = jnp.zeros_like(ck_ref)
            cv_ref[...] = jnp.zeros_like(cv_ref)

        valid = _attn_mask(rev(i))
        q, kp, kc, vp, vc = q_ref[...], kp_ref[...], kc_ref[...], vp_ref[...], vc_ref[...]
        dout = do_ref[...].astype(BF16)
        lane = lax.broadcasted_iota(jnp.int32, (1, 128), 1)
        dsink = jnp.zeros((1, 128), F32)
        dq_parts, dk_parts, dv_parts = [], [], []
        for g in range(N_KV):
            gs = slice(HEAD_DIM * g, HEAD_DIM * (g + 1))
            kk = jnp.concatenate([kp[:, gs], kc[:, gs]], axis=0)
            vv = jnp.concatenate([vp[:, gs], vc[:, gs]], axis=0)
            qs, dos = _stack_heads(q, g), _stack_heads(dout, g)
            probs, psink = _attn_probs(qs, kk, _group_sinks(sink_ref, g), valid)
            dp = _mm_nt(dos, vv)
            delta = jnp.sum(probs * dp, axis=-1, keepdims=True)
            ds = (probs * (dp - delta) * ATTN_SCALE).astype(BF16)
            sink_terms = psink * delta
            for hh in range(GROUP):
                dsink = dsink + jnp.where(lane == GROUP * g + hh, -jnp.sum(sink_terms[BLOCK * hh:BLOCK * (hh + 1), :]), 0.0)
            dq_parts.append(_unstack_heads(_mm(ds, kk)))
            dk_parts.append(_mm_tn(ds, qs))
            dv_parts.append(_mm_tn(probs.astype(BF16), dos))
        dk2 = jnp.concatenate(dk_parts, axis=1)
        dv2 = jnp.concatenate(dv_parts, axis=1)
        dk = dk2[BLOCK:, :] + ck_ref[...]
        dv = dv2[BLOCK:, :] + cv_ref[...]
        ck_ref[...] = dk2[:BLOCK, :]
        cv_ref[...] = dv2[:BLOCK, :]
        c, s1, s2 = c_ref[...], s1_ref[...], s2_ref[...]
        dq = jnp.concatenate(dq_parts, axis=1)
        for ci in range(ATTN_W // 128):
            sl = slice(128 * ci, 128 * (ci + 1))
            dqkv_ref[:, sl] = _rope_transpose(dq[:, sl], c, s1, s2).astype(BF16)
        dqkv_ref[:, ATTN_W:ATTN_W + KV_W] = _rope_transpose(dk, c, s1, s2).astype(BF16)
        dqkv_ref[:, ATTN_W + KV_W:] = dv.astype(BF16)
        dsink_ref[0:1, :] += dsink

    blk = lambda w_: pl.BlockSpec((BLOCK, w_), lambda i: (rev(i), 0))
    return pl.pallas_call(
        body, name="attn_bwd", grid=(nb,),
        in_specs=[pl.BlockSpec(memory_space=pltpu.SMEM)] + _qkv_specs(rev) + [blk(ATTN_W), blk(128), blk(128), blk(128)],
        out_specs=[blk(QKV_W), _full((8, 128))],
        out_shape=[jax.ShapeDtypeStruct((t, QKV_W), BF16), jax.ShapeDtypeStruct((8, 128), F32)],
        scratch_shapes=[pltpu.VMEM((BLOCK, KV_W), F32), pltpu.VMEM((BLOCK, KV_W), F32)],
        compiler_params=_params("arbitrary"),
    )(sinks, qkv, qkv, qkv, qkv, qkv, dattn, rc, rs1, rs2)


def _in_proj_bwd(dqkv, dgates, x, dh, g1, w_in, tm):
    t = x.shape[0]

    def body(dq_ref, dg_ref, x_ref, dh_ref, g1_ref, w_ref, dx_ref, dwa_ref, dwb_ref, dg1_ref):
        @pl.when(pl.program_id(0) == 0)
        def _():
            dwa_ref[...] = jnp.zeros_like(dwa_ref)
            dwb_ref[...] = jnp.zeros_like(dwb_ref)
            dg1_ref[...] = jnp.zeros_like(dg1_ref)

        dq, dg = dq_ref[...], dg_ref[...]
        dhn = _mm_nt(dq, w_ref[:, :QKV_W]) + _mm_nt(dg, w_ref[:, QKV_W:])
        xv = x_ref[...]
        r = _inv_rms(xv)
        x_hat = xv * r
        g1 = g1_ref[...]
        dx, dg1 = _rms_bwd(x_hat, r, g1, dhn)
        dx_ref[...] = dh_ref[...] + dx
        hn = (x_hat * g1).astype(BF16)
        dwa_ref[...] += _mm_tn(hn, dq)
        dwb_ref[...] += _mm_tn(hn, dg)
        dg1_ref[0:1, :] += _colsum(dg1)

    tile = lambda n: pl.BlockSpec((tm, n), lambda i: (i, 0))
    return pl.pallas_call(
        body, name="in_proj_bwd", grid=(t // tm,),
        in_specs=[tile(QKV_W), tile(GATES_W), tile(D_MODEL), tile(D_MODEL), _full((1, D_MODEL)),
                  _resident((D_MODEL, IN_COLS))],
        out_specs=[tile(D_MODEL), _full((D_MODEL, QKV_W)), _full((D_MODEL, GATES_W)), _full((SMALL_ROWS, D_MODEL))],
        out_shape=[jax.ShapeDtypeStruct((t, D_MODEL), F32), jax.ShapeDtypeStruct((D_MODEL, QKV_W), F32),
                   jax.ShapeDtypeStruct((D_MODEL, GATES_W), F32), jax.ShapeDtypeStruct((SMALL_ROWS, D_MODEL), F32)],
        compiler_params=_params("arbitrary"),
    )(dqkv, dgates, x, dh, g1, w_in)


def _all_gather(shards, name):
    n = len(shards)

    def body(*refs):
        ins, outs = refs[:n], refs[n:2 * n]
        send_sems, recv_sems, local_sems = refs[2 * n:]
        x, y, c = _mesh_pos()
        me, sibling = (x, y, c), (x, y, 1 - c)
        chips = [(1 - x, y), (x, 1 - y), (1 - x, 1 - y)]

        def copy(i, k, block, to, src=None):
            dst = outs[i].at[4 * block[0] + 2 * block[1] + block[2]]
            return pltpu.make_async_remote_copy(
                src_ref=dst if src is None else src, dst_ref=dst, send_sem=send_sems.at[7 * i + k],
                recv_sem=recv_sems.at[7 * i + k], device_id=to, device_id_type=MESH)

        mine = [pltpu.make_async_copy(ins[i], outs[i].at[4 * x + 2 * y + c], local_sems.at[i]) for i in range(n)]
        for cp in mine:
            cp.start()
        first = []
        for i in range(n):
            first.append(copy(i, 0, me, sibling, src=ins[i]))
            first += [copy(i, 1 + j, me, (*chip, c), src=ins[i]) for j, chip in enumerate(chips)]
        for cp in first:
            cp.start()
        passed = []
        for j, chip in enumerate(chips):
            for i in range(n):
                copy(i, 1 + j, (*chip, c), me).wait_recv()
                cp = copy(i, 4 + j, (*chip, c), sibling)
                cp.start()
                passed.append(cp)
        for i in range(n):
            copy(i, 0, sibling, me).wait_recv()
            for j, chip in enumerate(chips):
                copy(i, 4 + j, (*chip, 1 - c), me).wait_recv()
        for cp in first + passed:
            cp.wait_send()
        for cp in mine:
            cp.wait()

    return pl.pallas_call(
        body, name=name,
        in_specs=[HBM_SPEC] * n, out_specs=[HBM_SPEC] * n,
        out_shape=[jax.ShapeDtypeStruct((N_DEV,) + s.shape, s.dtype) for s in shards],
        scratch_shapes=[pltpu.SemaphoreType.DMA((7 * n,)), pltpu.SemaphoreType.DMA((7 * n,)),
                        pltpu.SemaphoreType.DMA((n,))],
    )(*shards)


def _sibling_exchange(grads, name):
    n = len(grads)

    def body(*refs):
        ins, outs = refs[:n], refs[n:2 * n]
        send_sems, recv_sems = refs[2 * n:]
        x, y, c = _mesh_pos()
        copies = [pltpu.make_async_remote_copy(
            src_ref=ins[i].at[1 - c], dst_ref=outs[i], send_sem=send_sems.at[i], recv_sem=recv_sems.at[i],
            device_id=(x, y, 1 - c), device_id_type=MESH) for i in range(n)]
        for cp in copies:
            cp.start()
        for cp in copies:
            cp.wait()

    return pl.pallas_call(
        body, name=name,
        in_specs=[HBM_SPEC] * n, out_specs=[HBM_SPEC] * n,
        out_shape=[jax.ShapeDtypeStruct(g.shape[1:], g.dtype) for g in grads],
        scratch_shapes=[pltpu.SemaphoreType.DMA((n,)), pltpu.SemaphoreType.DMA((n,))],
    )(*grads)


def _chip_exchange(sums, name):
    n = len(sums)

    def body(*refs):
        ins, outs = refs[:n], refs[n:2 * n]
        send_sems, recv_sems = refs[2 * n:]
        x, y, c = _mesh_pos()
        chips = [(1 - x, y), (x, 1 - y), (1 - x, 1 - y)]
        copies = [pltpu.make_async_remote_copy(
            src_ref=ins[i].at[2 * chip[0] + chip[1]], dst_ref=outs[i].at[k], send_sem=send_sems.at[3 * i + k],
            recv_sem=recv_sems.at[3 * i + k], device_id=(*chip, c), device_id_type=MESH)
            for i in range(n) for k, chip in enumerate(chips)]
        for cp in copies:
            cp.start()
        for cp in copies:
            cp.wait()

    return pl.pallas_call(
        body, name=name,
        in_specs=[HBM_SPEC] * n, out_specs=[HBM_SPEC] * n,
        out_shape=[jax.ShapeDtypeStruct((3,) + s.shape[1:], s.dtype) for s in sums],
        scratch_shapes=[pltpu.SemaphoreType.DMA((3 * n,)), pltpu.SemaphoreType.DMA((3 * n,))],
    )(*sums)


def _pair_sum(grad, recv, pos, tr):
    _, _, rows, cols = grad.shape

    def body(pos_ref, g_ref, r_ref, sb_ref):
        sb_ref[...] = (g_ref[...] + r_ref[...]).astype(BF16)

    return pl.pallas_call(
        body, name="pair_sum",
        grid_spec=pltpu.PrefetchScalarGridSpec(
            num_scalar_prefetch=1, grid=(N_CHIPS, rows // tr),
            in_specs=[pl.BlockSpec((None, None, tr, cols), lambda p, i, pos: (pos[0], p, i, 0)),
                      pl.BlockSpec((None, tr, cols), lambda p, i, pos: (p, i, 0))],
            out_specs=pl.BlockSpec((None, tr, cols), lambda p, i, pos: (p, i, 0))),
        out_shape=jax.ShapeDtypeStruct((N_CHIPS, rows, cols), BF16),
        compiler_params=_params("parallel", "parallel"),
    )(pos, grad, recv)


def _adam_math(w, g, m, v):
    m = ADAM_B1 * m + (1.0 - ADAM_B1) * g
    v = ADAM_B2 * v + (1.0 - ADAM_B2) * (g * g)
    m_hat = m / (1.0 - ADAM_B1 ** ADAM_STEP)
    v_hat = v / (1.0 - ADAM_B2 ** ADAM_STEP)
    delta = -ADAM_LR * (m_hat / (jnp.sqrt(v_hat) + ADAM_EPS) + ADAM_WD * w)
    return delta, m, v


def _adamw_shard(w, m, v, grad, from_sibling, from_chips, pos, tr):
    rows, cols = w.shape

    def body(pos_ref, w_ref, m_ref, v_ref, own_ref, sib_ref, r_ref, g_ref, d_ref, nm_ref, nv_ref):
        g = own_ref[...] + sib_ref[...]
        for k in range(3):
            g = g + r_ref[k].astype(F32)
        g_ref[...] = g
        d_ref[...], nm_ref[...], nv_ref[...] = _adam_math(w_ref[...], g, m_ref[...], v_ref[...])

    tile = pl.BlockSpec((tr, cols), lambda i, pos: (i, 0))
    out = jax.ShapeDtypeStruct((rows, cols), F32)
    return pl.pallas_call(
        body, name="adamw_shard",
        grid_spec=pltpu.PrefetchScalarGridSpec(
            num_scalar_prefetch=1, grid=(rows // tr,),
            in_specs=[tile, tile, tile,
                      pl.BlockSpec((None, None, tr, cols), lambda i, pos: (pos[0], pos[1], i, 0)),
                      pl.BlockSpec((None, tr, cols), lambda i, pos: (pos[1], i, 0)),
                      pl.BlockSpec((3, tr, cols), lambda i, pos: (0, i, 0))],
            out_specs=[tile] * 4),
        out_shape=[out] * 4,
        compiler_params=_params("parallel"),
    )(pos, w, m, v, grad, from_sibling, from_chips)


def _adamw_scattered(w, m, v, grad, recv, order, tr):
    rows, cols = w.shape
    n_peers = N_DEV - 1

    def body(order_ref, w_ref, m_ref, v_ref, own_ref, *rest):
        peers, (g_ref, d_ref, nm_ref, nv_ref) = rest[:n_peers], rest[n_peers:]
        g = own_ref[...]
        for r_ref in peers:
            g = g + r_ref[...].astype(F32)
        g_ref[...] = g
        d_ref[...], nm_ref[...], nv_ref[...] = _adam_math(w_ref[...], g, m_ref[...], v_ref[...])

    tile = pl.BlockSpec((tr, cols), lambda i, order_ref: (i, 0))
    slot = lambda s: pl.BlockSpec((None, tr, cols), lambda i, order_ref: (order_ref[s], i, 0))
    out = jax.ShapeDtypeStruct((rows, cols), F32)
    return pl.pallas_call(
        body, name="adamw_scattered",
        grid_spec=pltpu.PrefetchScalarGridSpec(
            num_scalar_prefetch=1, grid=(rows // tr,),
            in_specs=[tile, tile, tile, slot(n_peers)] + [slot(s) for s in range(n_peers)],
            out_specs=[tile] * 4),
        out_shape=[out] * 4,
        compiler_params=_params("parallel"),
    )(order, w, m, v, grad, *([recv] * n_peers))


def _sum_devices(gathered):
    _, rows, cols = gathered.shape

    def body(g_ref, o_ref):
        s = g_ref[0]
        for d in range(1, N_DEV):
            s = s + g_ref[d]
        o_ref[...] = s

    return pl.pallas_call(
        body, name="sum_devices", in_specs=[_full(gathered.shape)], out_specs=_full((rows, cols)), grid=(1,),
        out_shape=jax.ShapeDtypeStruct((rows, cols), F32),
    )(gathered)


def _adamw_small(w, g, m, v):
    def body(w_ref, g_ref, m_ref, v_ref, d_ref, nm_ref, nv_ref):
        d_ref[...], nm_ref[...], nv_ref[...] = _adam_math(w_ref[...], g_ref[...], m_ref[...], v_ref[...])

    spec = _full(w.shape)
    out = jax.ShapeDtypeStruct(w.shape, F32)
    return pl.pallas_call(
        body, name="adamw_small", grid=(1,), in_specs=[spec] * 4, out_specs=[spec] * 3, out_shape=[out] * 3,
    )(w, g, m, v)


TOKEN_TILE = 512
MID_TILE = 256
DW_TILE = 1024
ADAM_ROWS = 128


def _local_grads(x, target, g1, w_in, conv_w, sinks, g_attn, g_conv, g2, g3, g4, shards, order):
    t = x.shape[0]
    tm = min(TOKEN_TILE, t)
    rope = _rope_tables(t)
    qkv, gates, mconv, gathered = _in_proj_fwd(x, g1, w_in, conv_w, g_conv, rope, tm, shards)
    attn, mattn, (w_out, w_up, w_down) = _attn_fwd(qkv, sinks, g_attn, shards, gathered)
    actt, dup, hn2t, dmo, dmix, dh, dmixed, small_mid = _mid(
        mattn, mconv, x, target, g2, g3, g4, w_out.reshape(D_MODEL, D_MODEL), w_up, w_down, min(MID_TILE, t))
    tk = min(DW_TILE, t)
    dw_up = _dw_scatter(hn2t, dup, order, tk, False, "dw_up")
    dw_down = _dw_scatter(actt, dmo, order, tk, True, "dw_down")
    dw_out = _dw_out(mattn, mconv, dmix, tk)
    dattn, dgates, small_mix = _mix_bwd(dmixed, attn, gates, g_attn, g_conv, conv_w, tm)
    dqkv, dsink = _attn_bwd(qkv, dattn, sinks, rope)
    grad_x, dwa, dwb, small_in = _in_proj_bwd(dqkv, dgates, x, dh, g1, w_in, tm)
    dw_in = jnp.concatenate([dwa, dwb], axis=1)
    return grad_x, dw_in, dw_out, dw_up, dw_down, (small_mid, small_mix, dsink, small_in)


def _by_dest(a, rows_major):
    r, c = a.shape
    if rows_major:
        return a.reshape(N_CHIPS, 2, r // N_DEV, c).transpose(1, 0, 2, 3)
    return a.reshape(r, N_CHIPS, 2, c // N_DEV).transpose(2, 1, 0, 3)


def _pack_small(small_mid, small_mix, dsink, small_in):
    z = lambda n: jnp.zeros((1, n), F32)
    rows = [
        small_mid[ROW_LOSS:ROW_LOSS + 1],
        small_in[0:1],
        small_mid[ROW_G2:ROW_G2 + 1],
        small_mid[ROW_G3:ROW_G3 + 1],
        small_mid[ROW_G4:ROW_G4 + 1],
        jnp.concatenate([small_mix[ROW_GATTN:ROW_GATTN + 1], small_mix[ROW_GCONV:ROW_GCONV + 1]], axis=1),
        jnp.concatenate([small_mix[ROW_CW0:ROW_CW0 + 1], small_mix[ROW_CW0 + 1:ROW_CW0 + 2]], axis=1),
        jnp.concatenate([small_mix[ROW_CW0 + 2:ROW_CW0 + 3], dsink[0:1, :], z(D_MODEL - CONV_W - 128)], axis=1),
    ]
    return jnp.concatenate(rows, axis=0)


def kernel(x, pre_mix_norm, w_in, conv_w, attn_sinks, attn_group_norm, conv_group_norm, w_out, post_mix_norm, pre_mlp_norm, w_up, w_down, post_mlp_norm, loss_target, m_pre_mix_norm, m_w_in, m_conv_w, m_attn_sinks, m_attn_group_norm, m_conv_group_norm, m_w_out, m_post_mix_norm, m_pre_mlp_norm, m_w_up, m_w_down, m_post_mlp_norm, v_pre_mix_norm, v_w_in, v_conv_w, v_attn_sinks, v_attn_group_norm, v_conv_group_norm, v_w_out, v_post_mix_norm, v_pre_mlp_norm, v_w_up, v_w_down, v_post_mlp_norm):
    xi, yi, ci = _mesh_pos()
    chip = 2 * xi + yi
    dev = 2 * chip + ci

    order = _peer_order(dev)
    pos = jnp.stack([ci, chip]).astype(jnp.int32)

    gw_in, gconv = _all_gather([w_in[0].astype(BF16), conv_w[0]], "gather_w_in")
    w_in_full = gw_in.transpose(1, 0, 2).reshape(D_MODEL, IN_COLS)
    conv_full = gconv.transpose(1, 0, 2).reshape(3, CONV_W)
    shards = [w_out[0].astype(BF16), w_up[0].astype(BF16), w_down[0].astype(BF16)]

    grad_x, dw_in, dw_out, dw_up, dw_down, smalls = _local_grads(
        x[0], loss_target[0], pre_mix_norm, w_in_full, conv_full, attn_sinks, attn_group_norm, conv_group_norm,
        post_mix_norm, pre_mlp_norm, post_mlp_norm, shards, order)

    grads = [_by_dest(dw_in, False), _by_dest(dw_out, True)]
    from_sibling = _sibling_exchange(grads, "reduce_sibling")
    summed = [_pair_sum(g, r, pos, ADAM_ROWS) for g, r in zip(grads, from_sibling)]
    from_chips = _chip_exchange(summed, "reduce_chips")

    small = _sum_devices(_all_gather([_pack_small(*smalls)], "gather_small")[0])
    loss = (0.5 / D_MODEL) * jnp.sum(small[0])

    big = {}
    for name, w, m, v, g, rs, rc in zip(("w_in", "w_out"), (w_in, w_out), (m_w_in, m_w_out), (v_w_in, v_w_out), grads,
                                        from_sibling, from_chips):
        big[name] = [a[None] for a in _adamw_shard(w[0], m[0], v[0], g, rs, rc, pos, ADAM_ROWS)]
    for name, w, m, v, (g, recv) in zip(("w_up", "w_down"), (w_up, w_down), (m_w_up, m_w_down), (v_w_up, v_w_down),
                                        (dw_up, dw_down)):
        big[name] = [a[None] for a in _adamw_scattered(w[0], m[0], v[0], g, recv, order, ADAM_ROWS)]

    conv_g = lax.dynamic_slice(
        jnp.stack([small[6, :CONV_W], small[6, CONV_W:], small[7, :CONV_W]]), (0, dev * (CONV_W // N_DEV)),
        (3, CONV_W // N_DEV))
    pad = lambda a, n: jnp.pad(a.reshape(1, -1), ((0, 0), (0, n - a.size)))
    small_names = ("pre_mix_norm", "post_mix_norm", "pre_mlp_norm", "post_mlp_norm")
    small_w = {"pre_mix_norm": (pre_mix_norm, m_pre_mix_norm, v_pre_mix_norm),
               "post_mix_norm": (post_mix_norm, m_post_mix_norm, v_post_mix_norm),
               "pre_mlp_norm": (pre_mlp_norm, m_pre_mlp_norm, v_pre_mlp_norm),
               "post_mlp_norm": (post_mlp_norm, m_post_mlp_norm, v_post_mlp_norm)}

    def pack(k):
        rows = [small_w[nm][k] for nm in small_names]
        rows.append(jnp.concatenate([(attn_group_norm, m_attn_group_norm, v_attn_group_norm)[k],
                                     (conv_group_norm, m_conv_group_norm, v_conv_group_norm)[k]], axis=1))
        rows.append(pad((conv_w, m_conv_w, v_conv_w)[k], D_MODEL))
        rows.append(pad((attn_sinks, m_attn_sinks, v_attn_sinks)[k], D_MODEL))
        rows.append(jnp.zeros((1, D_MODEL), F32))
        return jnp.concatenate(rows, axis=0)

    g_small = jnp.concatenate(
        [small[1:6], pad(conv_g, D_MODEL), pad(small[7, CONV_W:CONV_W + N_HEADS], D_MODEL), jnp.zeros((1, D_MODEL), F32)],
        axis=0)
    d_small, nm_small, nv_small = _adamw_small(pack(0), g_small, pack(1), pack(2))

    def unpack(a):
        nconv = 3 * CONV_W // N_DEV
        return {"pre_mix_norm": a[0:1], "post_mix_norm": a[1:2], "pre_mlp_norm": a[2:3], "post_mlp_norm": a[3:4],
                "attn_group_norm": a[4:5, :ATTN_W], "conv_group_norm": a[4:5, ATTN_W:],
                "conv_w": a[5, :nconv].reshape(1, 3, CONV_W // N_DEV), "attn_sinks": a[6:7, :N_HEADS]}

    order = ("pre_mix_norm", "w_in", "conv_w", "attn_sinks", "attn_group_norm", "conv_group_norm", "w_out",
             "post_mix_norm", "pre_mlp_norm", "w_up", "w_down", "post_mlp_norm")
    outs = []
    for k, a in enumerate((g_small, d_small, nm_small, nv_small)):
        sm = unpack(a)
        outs += [big[nm][k] if nm in big else sm[nm] for nm in order]
    return (loss, grad_x[None], *outs)
```

```python
import functools

import jax
import jax.numpy as jnp
import numpy as np
from jax import lax
from jax.experimental import pallas as pl
from jax.experimental.pallas import tpu as pltpu

F32 = jnp.float32
BF16 = jnp.bfloat16

D_MODEL = 1024
HEAD_DIM = 64
ATTN_W = 512
CONV_W = 512
N_HEADS = 8
N_KV = 2
GROUP = 4
KV_W = 128
QKV_W = ATTN_W + 2 * KV_W
GATES_W = 3 * CONV_W
IN_COLS = QKV_W + GATES_W
D_FF = 4096
FF_CHUNK = 512
N_FF_CHUNKS = D_FF // FF_CHUNK
BLOCK = 128
ROT_HALF = 8
ROPE_THETA = 500000.0
NORM_EPS = 1e-6
NEG_INF = -1e30
ATTN_SCALE = 0.125
N_DEV = 8
N_CHIPS = 4
IN_SHARD = IN_COLS // N_DEV

ADAM_LR = 0.001
ADAM_B1 = 0.9
ADAM_B2 = 0.999
ADAM_EPS = 1e-08
ADAM_WD = 0.01
ADAM_STEP = 10

V7X_VMEM_BYTES = 64 * 1024 * 1024
VMEM_LIMIT = V7X_VMEM_BYTES - 2 * 1024 * 1024

MESH = pl.DeviceIdType.MESH
HBM_SPEC = pl.BlockSpec(memory_space=pltpu.HBM)


def _params(*sem):
    return pltpu.CompilerParams(dimension_semantics=sem, vmem_limit_bytes=VMEM_LIMIT)


def _mm(a, b):
    return jnp.dot(a, b, preferred_element_type=F32)


def _mm_nt(a, b):
    return lax.dot_general(a, b, (((1,), (1,)), ((), ())), preferred_element_type=F32)


def _mm_tn(a, b):
    return lax.dot_general(a, b, (((0,), (0,)), ((), ())), preferred_element_type=F32)


def _inv_rms(x):
    return lax.rsqrt(jnp.mean(x * x, axis=-1, keepdims=True) + NORM_EPS)


def _rms_bwd(xhat, r, gain, dy):
    gy = dy * gain
    return r * (gy - xhat * jnp.mean(gy * xhat, axis=-1, keepdims=True)), dy * xhat


def _colsum(a):
    return jnp.sum(a, axis=0, keepdims=True)


def _full(shape):
    zeros = (0,) * len(shape)
    return pl.BlockSpec(shape, lambda *_: zeros)


def _resident(shape):
    zeros = (0,) * len(shape)
    return pl.BlockSpec(shape, lambda *_: zeros, pipeline_mode=pl.Buffered(1))


def _rope_tables(t):
    pos = np.arange(t, dtype=np.float32)
    inv_freq = (ROPE_THETA ** (-np.arange(0, 2 * ROT_HALF, 2, dtype=np.float64) / (2 * ROT_HALF))).astype(np.float32)
    ang = (pos[:, None] * inv_freq[None, :]).astype(np.float64)
    cos, sin = np.cos(ang).astype(np.float32), np.sin(ang).astype(np.float32)
    zeros8 = np.zeros((t, ROT_HALF), np.float32)
    rest = np.zeros((t, HEAD_DIM - 2 * ROT_HALF), np.float32)
    c_head = np.concatenate([cos, cos, rest + 1.0], axis=1)
    s1_head = np.concatenate([zeros8, sin, rest], axis=1)
    s2_head = np.concatenate([-sin, zeros8, rest], axis=1)
    two = lambda a: jnp.asarray(np.concatenate([a, a], axis=1))
    return two(c_head), two(s1_head), two(s2_head)


def _rope(v, c, s1, s2):
    return v * c + pltpu.roll(v, ROT_HALF, 1) * s1 + pltpu.roll(v, 128 - ROT_HALF, 1) * s2


def _rope_transpose(dv, c, s1, s2):
    return dv * c + pltpu.roll(dv * s1, 128 - ROT_HALF, 1) + pltpu.roll(dv * s2, ROT_HALF, 1)


def _shift_rows_down(u, prev, k):
    row = lax.broadcasted_iota(jnp.int32, u.shape, 0)
    out = pltpu.roll(u, k, 0)
    for r in range(k):
        out = jnp.where(row == r, prev[8 - k + r:8 - k + r + 1, :], out)
    return out


def _shift_rows_up(u, nxt, k):
    n = u.shape[0]
    row = lax.broadcasted_iota(jnp.int32, u.shape, 0)
    out = pltpu.roll(u, n - k, 0)
    for r in range(k):
        out = jnp.where(row == n - k + r, nxt[r:r + 1, :], out)
    return out


def _conv3(u, u1, u2, w):
    return (w[0:1, :] * u2 + w[1:2, :] * u1) + w[2:3, :] * u


def _mesh_pos():
    return lax.axis_index("x"), lax.axis_index("y"), lax.axis_index("c")


def _slot(ref, pos):
    return ref.at[4 * pos[0] + 2 * pos[1] + pos[2]]


def _push(src, dst, sems, k, to):
    send_sems, recv_sems = sems
    return pltpu.make_async_remote_copy(src_ref=src, dst_ref=dst, send_sem=send_sems.at[k], recv_sem=recv_sems.at[k],
                                        device_id=to, device_id_type=MESH)


def _gather_near(first, last, shards, outs, sems, local_sems):
    x, y, c = _mesh_pos()
    me, peers = (x, y, c), [(x, y, 1 - c), (1 - x, y, c), (x, 1 - y, c)]
    n = len(shards)
    local = [pltpu.make_async_copy(shards[i], _slot(outs[i], me), local_sems.at[i]) for i in range(n)]
    sends = [_push(shards[i], _slot(outs[i], me), sems, 3 * i + k, peers[k]) for i in range(n) for k in range(3)]
    arrivals = [_push(shards[i], _slot(outs[i], peers[k]), sems, 3 * i + k, peers[k]) for i in range(n) for k in range(3)]

    @pl.when(first)
    def _():
        for cp in local + sends:
            cp.start()

    @pl.when(last)
    def _():
        for cp in sends:
            cp.wait_send()
        for cp in arrivals:
            cp.wait_recv()
        for cp in local:
            cp.wait()


def _gather_far(first, last, shards, ins, outs, sems):
    x, y, c = _mesh_pos()
    me, sibling = (x, y, c), (x, y, 1 - c)
    chips = [(1 - x, y), (x, 1 - y), (1 - x, 1 - y)]
    n = len(shards)
    diag_send = [_push(shards[i], _slot(outs[i], me), sems, 4 * i, (*chips[2], c)) for i in range(n)]
    diag_arrival = [_push(shards[i], _slot(outs[i], (*chips[2], c)), sems, 4 * i, (*chips[2], c)) for i in range(n)]
    passed = [[_push(_slot(ins[i], (*chips[j], c)), _slot(outs[i], (*chips[j], c)), sems, 4 * i + 1 + j, sibling)
               for i in range(n)] for j in range(3)]
    from_sibling = [_push(shards[i], _slot(outs[i], (*chips[j], 1 - c)), sems, 4 * i + 1 + j, sibling)
                    for i in range(n) for j in range(3)]

    @pl.when(first)
    def _():
        for cp in diag_send + passed[0] + passed[1]:
            cp.start()

    @pl.when(last)
    def _():
        for cp in diag_arrival:
            cp.wait_recv()
        for cp in passed[2]:
            cp.start()
        for cp in from_sibling:
            cp.wait_recv()
        for cp in diag_send + passed[0] + passed[1] + passed[2]:
            cp.wait_send()


def _in_proj_fwd(x, g1, w_in, conv_w, g_conv, rope, tm, shards):
    t = x.shape[0]
    rc, rs1, rs2 = rope
    n = len(shards)

    def body(*refs):
        x_ref, g1_ref, w_ref, cw_ref, gc_ref, c_ref, s1_ref, s2_ref = refs[:8]
        shard_refs = refs[8:8 + n]
        qkv_ref, gates_ref, mconv_ref = refs[8 + n:11 + n]
        gathered = refs[11 + n:11 + 2 * n]
        carry_ref = refs[11 + 2 * n]
        if n:
            step = pl.program_id(0)
            _gather_near(step == 0, step == pl.num_programs(0) - 1, shard_refs, gathered, refs[12 + 2 * n:14 + 2 * n],
                         refs[14 + 2 * n])

        @pl.when(pl.program_id(0) == 0)
        def _():
            carry_ref[...] = jnp.zeros_like(carry_ref)

        xv = x_ref[...]
        hn = ((xv * _inv_rms(xv)) * g1_ref[...]).astype(BF16)
        proj = _mm(hn, w_ref[...])
        c, s1, s2 = c_ref[...], s1_ref[...], s2_ref[...]
        for ci in range((ATTN_W + KV_W) // 128):
            sl = slice(128 * ci, 128 * (ci + 1))
            qkv_ref[:, sl] = _rope(proj[:, sl], c, s1, s2).astype(BF16)
        qkv_ref[:, ATTN_W + KV_W:QKV_W] = proj[:, ATTN_W + KV_W:QKV_W].astype(BF16)
        gates = proj[:, QKV_W:]
        gates_ref[...] = gates
        gb, gcc, xin = gates[:, :CONV_W], gates[:, CONV_W:2 * CONV_W], gates[:, 2 * CONV_W:]
        u = gcc * xin
        prev = carry_ref[...]
        conv = gb * _conv3(u, _shift_rows_down(u, prev, 1), _shift_rows_down(u, prev, 2), cw_ref[...])
        carry_ref[...] = u[tm - 8:tm, :]
        mconv_ref[...] = ((conv * _inv_rms(conv)) * gc_ref[...]).astype(BF16)

    tile = lambda w_: pl.BlockSpec((tm, w_), lambda i: (i, 0))
    comm_scratch = [pltpu.SemaphoreType.DMA((3 * n,)), pltpu.SemaphoreType.DMA((3 * n,)), pltpu.SemaphoreType.DMA((n,))]
    res = pl.pallas_call(
        body, name="in_proj_fwd", grid=(t // tm,),
        in_specs=[tile(D_MODEL), _full((1, D_MODEL)), _full((D_MODEL, IN_COLS)), _full((3, CONV_W)), _full((1, CONV_W)),
                  tile(128), tile(128), tile(128)] + [HBM_SPEC] * n,
        out_specs=[tile(QKV_W), tile(GATES_W), tile(CONV_W)] + [HBM_SPEC] * n,
        out_shape=[jax.ShapeDtypeStruct((t, QKV_W), BF16), jax.ShapeDtypeStruct((t, GATES_W), F32),
                   jax.ShapeDtypeStruct((t, CONV_W), BF16)]
        + [jax.ShapeDtypeStruct((N_DEV,) + s.shape, s.dtype) for s in shards],
        scratch_shapes=[pltpu.VMEM((8, CONV_W), F32)] + (comm_scratch if n else []),
        compiler_params=_params("arbitrary"),
    )(x, g1, w_in, conv_w, g_conv, rc, rs1, rs2, *shards)
    return res[0], res[1], res[2], list(res[3:])


GROUP_ROWS = GROUP * BLOCK


def _attn_mask(j):
    row = lax.broadcasted_iota(jnp.int32, (GROUP_ROWS, 2 * BLOCK), 0) & (BLOCK - 1)
    col = lax.broadcasted_iota(jnp.int32, (GROUP_ROWS, 2 * BLOCK), 1)
    return (col > row) & (col <= row + BLOCK) & ((col >= BLOCK) | (j > 0))


def _stack_heads(a, g):
    return jnp.concatenate([a[:, HEAD_DIM * (GROUP * g + hh):HEAD_DIM * (GROUP * g + hh + 1)] for hh in range(GROUP)], axis=0)


def _unstack_heads(a):
    return jnp.concatenate([a[BLOCK * hh:BLOCK * (hh + 1), :] for hh in range(GROUP)], axis=1)


def _group_sinks(sink_ref, g):
    head = lax.broadcasted_iota(jnp.int32, (GROUP_ROWS, 1), 0) // BLOCK
    out = jnp.full((GROUP_ROWS, 1), sink_ref[0, GROUP * g], F32)
    for hh in range(1, GROUP):
        out = jnp.where(head == hh, sink_ref[0, GROUP * g + hh], out)
    return out


def _attn_probs(qs, kk, sink, valid):
    s = jnp.where(valid, _mm_nt(qs, kk) * ATTN_SCALE, NEG_INF)
    m = jnp.maximum(jnp.max(s, axis=-1, keepdims=True), sink)
    p = jnp.exp(s - m)
    psink = jnp.exp(sink - m)
    inv_l = 1.0 / (jnp.sum(p, axis=-1, keepdims=True) + psink)
    return p * inv_l, psink * inv_l


def _qkv_specs(order):
    prev = lambda i: jnp.maximum(order(i) - 1, 0)
    kcol, vcol = ATTN_W // KV_W, ATTN_W // KV_W + 1
    return [pl.BlockSpec((BLOCK, ATTN_W), lambda i: (order(i), 0)),
            pl.BlockSpec((BLOCK, KV_W), lambda i: (prev(i), kcol)), pl.BlockSpec((BLOCK, KV_W), lambda i: (order(i), kcol)),
            pl.BlockSpec((BLOCK, KV_W), lambda i: (prev(i), vcol)), pl.BlockSpec((BLOCK, KV_W), lambda i: (order(i), vcol))]


def _attn_fwd(qkv, sinks, g_attn, shards, gathered):
    t = qkv.shape[0]
    n = len(shards)

    def body(*refs):
        sink_ref, q_ref, kp_ref, kc_ref, vp_ref, vc_ref, ga_ref = refs[:7]
        attn_ref, mattn_ref = refs[7 + 2 * n:9 + 2 * n]
        if n:
            step = pl.program_id(0)
            _gather_far(step == 0, step == pl.num_programs(0) - 1, refs[7:7 + n], refs[7 + n:7 + 2 * n],
                        refs[9 + 2 * n:9 + 3 * n], refs[9 + 3 * n:11 + 3 * n])
        valid = _attn_mask(pl.program_id(0))
        q, kp, kc, vp, vc = q_ref[...], kp_ref[...], kc_ref[...], vp_ref[...], vc_ref[...]
        outs = []
        for g in range(N_KV):
            gs = slice(HEAD_DIM * g, HEAD_DIM * (g + 1))
            kk = jnp.concatenate([kp[:, gs], kc[:, gs]], axis=0)
            vv = jnp.concatenate([vp[:, gs], vc[:, gs]], axis=0)
            probs, _ = _attn_probs(_stack_heads(q, g), kk, _group_sinks(sink_ref, g), valid)
            outs.append(_unstack_heads(_mm(probs.astype(BF16), vv)))
        attn = jnp.concatenate(outs, axis=1)
        attn_ref[...] = attn
        mattn_ref[...] = ((attn * _inv_rms(attn)) * ga_ref[...]).astype(BF16)

    blk = pl.BlockSpec((BLOCK, ATTN_W), lambda j: (j, 0))
    res = pl.pallas_call(
        body, name="attn_fwd", grid=(t // BLOCK,),
        in_specs=[pl.BlockSpec(memory_space=pltpu.SMEM)] + _qkv_specs(lambda j: j) + [_full((1, ATTN_W))]
        + [HBM_SPEC] * (2 * n),
        out_specs=[blk, blk] + [HBM_SPEC] * n,
        out_shape=[jax.ShapeDtypeStruct((t, ATTN_W), F32), jax.ShapeDtypeStruct((t, ATTN_W), BF16)]
        + [jax.ShapeDtypeStruct(g.shape, g.dtype) for g in gathered],
        input_output_aliases={7 + n + i: 2 + i for i in range(n)},
        scratch_shapes=[pltpu.SemaphoreType.DMA((4 * n,)), pltpu.SemaphoreType.DMA((4 * n,))] if n else [],
        compiler_params=_params("arbitrary"),
    )(sinks, qkv, qkv, qkv, qkv, qkv, g_attn, *shards, *gathered)
    return res[0], res[1], list(res[2:])


SMALL_ROWS = 8
ROW_LOSS, ROW_G2, ROW_G3, ROW_G4 = 0, 1, 2, 3


def _mid(mattn, mconv, x, target, g2, g3, g4, w_out, w_up, w_down, tm):
    t = x.shape[0]

    def body(ma_ref, mc_ref, x_ref, t_ref, g2_ref, g3_ref, g4_ref, wo_ref, wu_ref, wd_ref,
             actt_ref, dup_ref, hn2t_ref, dmo_ref, dmix_ref, dh_ref, dmixed_ref, small_ref, up_ref):
        @pl.when(pl.program_id(0) == 0)
        def _():
            small_ref[...] = jnp.zeros_like(small_ref)

        g2, g3, g4 = g2_ref[...], g3_ref[...], g4_ref[...]
        mix_out = _mm(ma_ref[...], wo_ref[0:ATTN_W, :]) + _mm(mc_ref[...], wo_ref[ATTN_W:, :])
        r2 = _inv_rms(mix_out)
        mo_hat = mix_out * r2
        h = x_ref[...] + mo_hat * g2
        r3 = _inv_rms(h)
        h_hat = h * r3
        hn2 = (h_hat * g3).astype(BF16)
        hn2t_ref[...] = hn2.T
        mlp = jnp.zeros((tm, D_MODEL), F32)
        for j in range(N_FF_CHUNKS):
            up = jnp.maximum(_mm(hn2, wu_ref[j]), 0.0)
            up_ref[j] = up.astype(BF16)
            act = (up * up).astype(BF16)
            actt_ref[FF_CHUNK * j:FF_CHUNK * (j + 1), :] = act.T
            mlp = mlp + _mm(act, wd_ref[j])
        r4 = _inv_rms(mlp)
        ml_hat = mlp * r4
        err = (h + ml_hat * g4) - t_ref[...]
        d_out = err * (1.0 / D_MODEL)
        d_mlp, dg4 = _rms_bwd(ml_hat, r4, g4, d_out)
        dmo = d_mlp.astype(BF16)
        dmo_ref[...] = dmo
        dhn2 = jnp.zeros((tm, D_MODEL), F32)
        for j in range(N_FF_CHUNKS):
            dup = (_mm_nt(dmo, wd_ref[j]) * (2.0 * up_ref[j].astype(F32))).astype(BF16)
            dup_ref[:, FF_CHUNK * j:FF_CHUNK * (j + 1)] = dup
            dhn2 = dhn2 + _mm_nt(dup, wu_ref[j])
        dh_norm, dg3 = _rms_bwd(h_hat, r3, g3, dhn2)
        dh = d_out + dh_norm
        dh_ref[...] = dh
        d_mix, dg2 = _rms_bwd(mo_hat, r2, g2, dh)
        dmix = d_mix.astype(BF16)
        dmix_ref[...] = dmix
        dmixed_ref[...] = _mm_nt(dmix, wo_ref[...])
        small_ref[ROW_LOSS:ROW_LOSS + 1, :] += _colsum(err * err)
        small_ref[ROW_G2:ROW_G2 + 1, :] += _colsum(dg2)
        small_ref[ROW_G3:ROW_G3 + 1, :] += _colsum(dg3)
        small_ref[ROW_G4:ROW_G4 + 1, :] += _colsum(dg4)

    tile = lambda n: pl.BlockSpec((tm, n), lambda i: (i, 0))
    cols = lambda n: pl.BlockSpec((n, tm), lambda i: (0, i))
    gain = _full((1, D_MODEL))
    return pl.pallas_call(
        body, name="mid_fwd_bwd", grid=(t // tm,),
        in_specs=[tile(ATTN_W), tile(CONV_W), tile(D_MODEL), tile(D_MODEL), gain, gain, gain,
                  _resident((D_MODEL, D_MODEL)), _resident((N_FF_CHUNKS, D_MODEL, FF_CHUNK)),
                  _resident((N_FF_CHUNKS, FF_CHUNK, D_MODEL))],
        out_specs=[cols(D_FF), tile(D_FF), cols(D_MODEL), tile(D_MODEL), tile(D_MODEL), tile(D_MODEL), tile(D_MODEL),
                   _full((SMALL_ROWS, D_MODEL))],
        out_shape=[jax.ShapeDtypeStruct((D_FF, t), BF16), jax.ShapeDtypeStruct((t, D_FF), BF16),
                   jax.ShapeDtypeStruct((D_MODEL, t), BF16), jax.ShapeDtypeStruct((t, D_MODEL), BF16),
                   jax.ShapeDtypeStruct((t, D_MODEL), BF16), jax.ShapeDtypeStruct((t, D_MODEL), F32),
                   jax.ShapeDtypeStruct((t, D_MODEL), F32), jax.ShapeDtypeStruct((SMALL_ROWS, D_MODEL), F32)],
        scratch_shapes=[pltpu.VMEM((N_FF_CHUNKS, tm, FF_CHUNK), BF16)],
        compiler_params=_params("arbitrary"),
    )(mattn, mconv, x, target, g2, g3, g4, w_out, w_up, w_down)


PEER_FLIPS = ((1, 1, 0), (1, 0, 0), (0, 1, 0), (1, 1, 1), (1, 0, 1), (0, 1, 1), (0, 0, 1))


def _peer_order(dev):
    masks = [4 * fx + 2 * fy + fc for fx, fy, fc in PEER_FLIPS] + [0]
    return jnp.bitwise_xor(dev, jnp.asarray(masks, jnp.int32)).astype(jnp.int32)


def _dw_scatter(at, b, order, tk, at_chunked, name):
    t = b.shape[0]
    n_k = t // tk
    rows, cols = (FF_CHUNK, D_MODEL) if at_chunked else (D_MODEL, FF_CHUNK)

    def body(order_ref, a_ref, b_ref, o_ref, recv_ref, send_buf, send_sems, recv_sems):
        s_now, k = pl.program_id(0), pl.program_id(1)
        x, y, c = _mesh_pos()
        mine = 4 * x + 2 * y + c

        def peer(s):
            fx, fy, fc = PEER_FLIPS[s]
            return (1 - x if fx else x, 1 - y if fy else y, 1 - c if fc else c)

        def send(s):
            return _push(send_buf.at[s], recv_ref.at[mine], (send_sems, recv_sems), s, peer(s))

        def arrival(s):
            return _push(send_buf.at[s], _slot(recv_ref, peer(s)), (send_sems, recv_sems), s, peer(s))

        @pl.when(k == 0)
        def _():
            o_ref[...] = jnp.zeros_like(o_ref)

        tokens = pl.ds(pl.multiple_of(k * tk, tk), tk)
        if at_chunked:
            o_ref[...] += _mm(a_ref[...], b_ref[tokens, :])
        else:
            o_ref[...] += _mm(a_ref[:, tokens], b_ref[...])
        for s in range(N_DEV - 1):
            @pl.when((s_now == s) & (k == n_k - 1))
            def _():
                send_buf[s] = o_ref[...].astype(BF16)
                send(s).start()

        @pl.when((s_now == N_DEV - 1) & (k == n_k - 1))
        def _():
            for s in range(N_DEV - 1):
                send(s).wait_send()
                arrival(s).wait_recv()

    if at_chunked:
        in_specs = [pl.BlockSpec((FF_CHUNK, tk), lambda s, k, order_ref: (order_ref[s], k)), _resident((t, D_MODEL))]
    else:
        in_specs = [_resident((D_MODEL, t)), pl.BlockSpec((tk, FF_CHUNK), lambda s, k, order_ref: (k, order_ref[s]))]
    return pl.pallas_call(
        body, name=name,
        grid_spec=pltpu.PrefetchScalarGridSpec(
            num_scalar_prefetch=1, grid=(N_DEV, n_k), in_specs=in_specs,
            out_specs=[pl.BlockSpec((None, rows, cols), lambda s, k, order_ref: (order_ref[s], 0, 0)), HBM_SPEC],
            scratch_shapes=[pltpu.VMEM((N_DEV - 1, rows, cols), BF16), pltpu.SemaphoreType.DMA((N_DEV - 1,)),
                            pltpu.SemaphoreType.DMA((N_DEV - 1,))]),
        out_shape=[jax.ShapeDtypeStruct((N_DEV, rows, cols), F32), jax.ShapeDtypeStruct((N_DEV, rows, cols), BF16)],
        compiler_params=_params("arbitrary", "arbitrary"),
    )(order, at, b)


def _dw_out(mattn, mconv, dmix, tk):
    t = dmix.shape[0]

    def body(ma_ref, mc_ref, b_ref, o_ref):
        @pl.when(pl.program_id(0) == 0)
        def _():
            o_ref[...] = jnp.zeros_like(o_ref)
        b = b_ref[...]
        o_ref[0:ATTN_W, :] += _mm_tn(ma_ref[...], b)
        o_ref[ATTN_W:, :] += _mm_tn(mc_ref[...], b)

    tile = lambda n: pl.BlockSpec((tk, n), lambda k: (k, 0))
    return pl.pallas_call(
        body, name="dw_out", grid=(t // tk,),
        in_specs=[tile(ATTN_W), tile(CONV_W), tile(D_MODEL)],
        out_specs=_full((D_MODEL, D_MODEL)),
        out_shape=jax.ShapeDtypeStruct((D_MODEL, D_MODEL), F32),
        compiler_params=_params("arbitrary"),
    )(mattn, mconv, dmix)


ROW_GATTN, ROW_GCONV, ROW_CW0 = 0, 1, 2


def _mix_bwd(dmixed, attn, gates, g_attn, g_conv, conv_w, tm):
    t = attn.shape[0]
    n = t // tm
    rev = lambda i: n - 1 - i

    def body(dm_ref, attn_ref, gates_ref, gprev_ref, ga_ref, gc_ref, cw_ref, dattn_ref, dgates_ref, small_ref, carry_ref):
        i = pl.program_id(0)

        @pl.when(i == 0)
        def _():
            small_ref[...] = jnp.zeros_like(small_ref)
            carry_ref[...] = jnp.zeros_like(carry_ref)

        dm = dm_ref[...]
        a = attn_ref[...]
        ra = _inv_rms(a)
        a_hat = a * ra
        dattn, dga = _rms_bwd(a_hat, ra, ga_ref[...], dm[:, :ATTN_W])
        dattn_ref[...] = dattn

        gates = gates_ref[...]
        gb, gcc, xin = gates[:, :CONV_W], gates[:, CONV_W:2 * CONV_W], gates[:, 2 * CONV_W:]
        u = gcc * xin
        gp = gprev_ref[...]
        uprev = jnp.where(rev(i) == 0, 0.0, gp[:, CONV_W:2 * CONV_W] * gp[:, 2 * CONV_W:])
        u1, u2 = _shift_rows_down(u, uprev, 1), _shift_rows_down(u, uprev, 2)
        w = cw_ref[...]
        c = _conv3(u, u1, u2, w)
        conv = gb * c
        rcv = _inv_rms(conv)
        c_hat = conv * rcv
        dconv, dgc = _rms_bwd(c_hat, rcv, gc_ref[...], dm[:, ATTN_W:])
        dc = dconv * gb
        nxt = carry_ref[...]
        du = (w[2:3, :] * dc + w[1:2, :] * _shift_rows_up(dc, nxt, 1)) + w[0:1, :] * _shift_rows_up(dc, nxt, 2)
        carry_ref[...] = dc[0:8, :]
        dgates_ref[:, :CONV_W] = (dconv * c).astype(BF16)
        dgates_ref[:, CONV_W:2 * CONV_W] = (du * xin).astype(BF16)
        dgates_ref[:, 2 * CONV_W:] = (du * gcc).astype(BF16)
        small_ref[ROW_GATTN:ROW_GATTN + 1, :] += _colsum(dga)
        small_ref[ROW_GCONV:ROW_GCONV + 1, :] += _colsum(dgc)
        small_ref[ROW_CW0:ROW_CW0 + 1, :] += _colsum(dc * u2)
        small_ref[ROW_CW0 + 1:ROW_CW0 + 2, :] += _colsum(dc * u1)
        small_ref[ROW_CW0 + 2:ROW_CW0 + 3, :] += _colsum(dc * u)

    tile = lambda w_: pl.BlockSpec((tm, w_), lambda i: (rev(i), 0))
    prev8 = pl.BlockSpec((8, GATES_W), lambda i: (jnp.maximum(rev(i) * (tm // 8) - 1, 0), 0))
    return pl.pallas_call(
        body, name="mix_bwd", grid=(n,),
        in_specs=[tile(D_MODEL), tile(ATTN_W), tile(GATES_W), prev8, _full((1, ATTN_W)), _full((1, CONV_W)),
                  _full((3, CONV_W))],
        out_specs=[tile(ATTN_W), tile(GATES_W), _full((SMALL_ROWS, CONV_W))],
        out_shape=[jax.ShapeDtypeStruct((t, ATTN_W), F32), jax.ShapeDtypeStruct((t, GATES_W), BF16),
                   jax.ShapeDtypeStruct((SMALL_ROWS, CONV_W), F32)],
        scratch_shapes=[pltpu.VMEM((8, CONV_W), F32)],
        compiler_params=_params("arbitrary"),
    )(dmixed, attn, gates, gates, g_attn, g_conv, conv_w)


def _attn_bwd(qkv, dattn, sinks, rope):
    t = qkv.shape[0]
    nb = t // BLOCK
    rev = lambda i: nb - 1 - i
    rc, rs1, rs2 = rope

    def body(sink_ref, q_ref, kp_ref, kc_ref, vp_ref, vc_ref, do_ref, c_ref, s1_ref, s2_ref,
             dqkv_ref, dsink_ref, ck_ref, cv_ref):
        i = pl.program_id(0)

        @pl.when(i == 0)
        def _():
            dsink_ref[...] = jnp.zeros_like(dsink_ref)
            ck_ref[...] = jnp.zeros_like(ck_ref)
            cv_ref[...] = jnp.zeros_like(cv_ref)

        valid = _attn_mask(rev(i))
        q, kp, kc, vp, vc = q_ref[...], kp_ref[...], kc_ref[...], vp_ref[...], vc_ref[...]
        dout = do_ref[...].astype(BF16)
        lane = lax.broadcasted_iota(jnp.int32, (1, 128), 1)
        dsink = jnp.zeros((1, 128), F32)
        dq_parts, dk_parts, dv_parts = [], [], []
        for g in range(N_KV):
            gs = slice(HEAD_DIM * g, HEAD_DIM * (g + 1))
            kk = jnp.concatenate([kp[:, gs], kc[:, gs]], axis=0)
            vv = jnp.concatenate([vp[:, gs], vc[:, gs]], axis=0)
            qs, dos = _stack_heads(q, g), _stack_heads(dout, g)
            probs, psink = _attn_probs(qs, kk, _group_sinks(sink_ref, g), valid)
            dp = _mm_nt(dos, vv)
            delta = jnp.sum(probs * dp, axis=-1, keepdims=True)
            ds = (probs * (dp - delta) * ATTN_SCALE).astype(BF16)
            sink_terms = psink * delta
            for hh in range(GROUP):
                dsink = dsink + jnp.where(lane == GROUP * g + hh, -jnp.sum(sink_terms[BLOCK * hh:BLOCK * (hh + 1), :]), 0.0)
            dq_parts.append(_unstack_heads(_mm(ds, kk)))
            dk_parts.append(_mm_tn(ds, qs))
            dv_parts.append(_mm_tn(probs.astype(BF16), dos))
        dk2 = jnp.concatenate(dk_parts, axis=1)
        dv2 = jnp.concatenate(dv_parts, axis=1)
        dk = dk2[BLOCK:, :] + ck_ref[...]
        dv = dv2[BLOCK:, :] + cv_ref[...]
        ck_ref[...] = dk2[:BLOCK, :]
        cv_ref[...] = dv2[:BLOCK, :]
        c, s1, s2 = c_ref[...], s1_ref[...], s2_ref[...]
        dq = jnp.concatenate(dq_parts, axis=1)
        for ci in range(ATTN_W // 128):
            sl = slice(128 * ci, 128 * (ci + 1))
            dqkv_ref[:, sl] = _rope_transpose(dq[:, sl], c, s1, s2).astype(BF16)
        dqkv_ref[:, ATTN_W:ATTN_W + KV_W] = _rope_transpose(dk, c, s1, s2).astype(BF16)
        dqkv_ref[:, ATTN_W + KV_W:] = dv.astype(BF16)
        dsink_ref[0:1, :] += dsink

    blk = lambda w_: pl.BlockSpec((BLOCK, w_), lambda i: (rev(i), 0))
    return pl.pallas_call(
        body, name="attn_bwd", grid=(nb,),
        in_specs=[pl.BlockSpec(memory_space=pltpu.SMEM)] + _qkv_specs(rev) + [blk(ATTN_W), blk(128), blk(128), blk(128)],
        out_specs=[blk(QKV_W), _full((8, 128))],
        out_shape=[jax.ShapeDtypeStruct((t, QKV_W), BF16), jax.ShapeDtypeStruct((8, 128), F32)],
        scratch_shapes=[pltpu.VMEM((BLOCK, KV_W), F32), pltpu.VMEM((BLOCK, KV_W), F32)],
        compiler_params=_params("arbitrary"),
    )(sinks, qkv, qkv, qkv, qkv, qkv, dattn, rc, rs1, rs2)


def _in_proj_bwd(dqkv, dgates, x, dh, g1, w_in, tm):
    t = x.shape[0]

    def body(dq_ref, dg_ref, x_ref, dh_ref, g1_ref, w_ref, dx_ref, dwa_ref, dwb_ref, dg1_ref):
        @pl.when(pl.program_id(0) == 0)
        def _():
            dwa_ref[...] = jnp.zeros_like(dwa_ref)
            dwb_ref[...] = jnp.zeros_like(dwb_ref)
            dg1_ref[...] = jnp.zeros_like(dg1_ref)

        dq, dg = dq_ref[...], dg_ref[...]
        dhn = _mm_nt(dq, w_ref[:, :QKV_W]) + _mm_nt(dg, w_ref[:, QKV_W:])
        xv = x_ref[...]
        r = _inv_rms(xv)
        x_hat = xv * r
        g1 = g1_ref[...]
        dx, dg1 = _rms_bwd(x_hat, r, g1, dhn)
        dx_ref[...] = dh_ref[...] + dx
        hn = (x_hat * g1).astype(BF16)
        dwa_ref[...] += _mm_tn(hn, dq)
        dwb_ref[...] += _mm_tn(hn, dg)
        dg1_ref[0:1, :] += _colsum(dg1)

    tile = lambda n: pl.BlockSpec((tm, n), lambda i: (i, 0))
    return pl.pallas_call(
        body, name="in_proj_bwd", grid=(t // tm,),
        in_specs=[tile(QKV_W), tile(GATES_W), tile(D_MODEL), tile(D_MODEL), _full((1, D_MODEL)),
                  _resident((D_MODEL, IN_COLS))],
        out_specs=[tile(D_MODEL), _full((D_MODEL, QKV_W)), _full((D_MODEL, GATES_W)), _full((SMALL_ROWS, D_MODEL))],
        out_shape=[jax.ShapeDtypeStruct((t, D_MODEL), F32), jax.ShapeDtypeStruct((D_MODEL, QKV_W), F32),
                   jax.ShapeDtypeStruct((D_MODEL, GATES_W), F32), jax.ShapeDtypeStruct((SMALL_ROWS, D_MODEL), F32)],
        compiler_params=_params("arbitrary"),
    )(dqkv, dgates, x, dh, g1, w_in)


def _all_gather(shards, name):
    n = len(shards)

    def body(*refs):
        ins, outs = refs[:n], refs[n:2 * n]
        send_sems, recv_sems, local_sems = refs[2 * n:]
        x, y, c = _mesh_pos()
        me, sibling = (x, y, c), (x, y, 1 - c)
        chips = [(1 - x, y), (x, 1 - y), (1 - x, 1 - y)]

        def copy(i, k, block, to, src=None):
            dst = outs[i].at[4 * block[0] + 2 * block[1] + block[2]]
            return pltpu.make_async_remote_copy(
                src_ref=dst if src is None else src, dst_ref=dst, send_sem=send_sems.at[7 * i + k],
                recv_sem=recv_sems.at[7 * i + k], device_id=to, device_id_type=MESH)

        mine = [pltpu.make_async_copy(ins[i], outs[i].at[4 * x + 2 * y + c], local_sems.at[i]) for i in range(n)]
        for cp in mine:
            cp.start()
        first = []
        for i in range(n):
            first.append(copy(i, 0, me, sibling, src=ins[i]))
            first += [copy(i, 1 + j, me, (*chip, c), src=ins[i]) for j, chip in enumerate(chips)]
        for cp in first:
            cp.start()
        passed = []
        for j, chip in enumerate(chips):
            for i in range(n):
                copy(i, 1 + j, (*chip, c), me).wait_recv()
                cp = copy(i, 4 + j, (*chip, c), sibling)
                cp.start()
                passed.append(cp)
        for i in range(n):
            copy(i, 0, sibling, me).wait_recv()
            for j, chip in enumerate(chips):
                copy(i, 4 + j, (*chip, 1 - c), me).wait_recv()
        for cp in first + passed:
            cp.wait_send()
        for cp in mine:
            cp.wait()

    return pl.pallas_call(
        body, name=name,
        in_specs=[HBM_SPEC] * n, out_specs=[HBM_SPEC] * n,
        out_shape=[jax.ShapeDtypeStruct((N_DEV,) + s.shape, s.dtype) for s in shards],
        scratch_shapes=[pltpu.SemaphoreType.DMA((7 * n,)), pltpu.SemaphoreType.DMA((7 * n,)),
                        pltpu.SemaphoreType.DMA((n,))],
    )(*shards)


def _sibling_exchange(grads, name):
    n = len(grads)

    def body(*refs):
        ins, outs = refs[:n], refs[n:2 * n]
        send_sems, recv_sems = refs[2 * n:]
        x, y, c = _mesh_pos()
        copies = [pltpu.make_async_remote_copy(
            src_ref=ins[i].at[1 - c], dst_ref=outs[i], send_sem=send_sems.at[i], recv_sem=recv_sems.at[i],
            device_id=(x, y, 1 - c), device_id_type=MESH) for i in range(n)]
        for cp in copies:
            cp.start()
        for cp in copies:
            cp.wait()

    return pl.pallas_call(
        body, name=name,
        in_specs=[HBM_SPEC] * n, out_specs=[HBM_SPEC] * n,
        out_shape=[jax.ShapeDtypeStruct(g.shape[1:], g.dtype) for g in grads],
        scratch_shapes=[pltpu.SemaphoreType.DMA((n,)), pltpu.SemaphoreType.DMA((n,))],
    )(*grads)


def _chip_exchange(sums, name):
    n = len(sums)

    def body(*refs):
        ins, outs = refs[:n], refs[n:2 * n]
        send_sems, recv_sems = refs[2 * n:]
        x, y, c = _mesh_pos()
        chips = [(1 - x, y), (x, 1 - y), (1 - x, 1 - y)]
        copies = [pltpu.make_async_remote_copy(
            src_ref=ins[i].at[2 * chip[0] + chip[1]], dst_ref=outs[i].at[k], send_sem=send_sems.at[3 * i + k],
            recv_sem=recv_sems.at[3 * i + k], device_id=(*chip, c), device_id_type=MESH)
            for i in range(n) for k, chip in enumerate(chips)]
        for cp in copies:
            cp.start()
        for cp in copies:
            cp.wait()

    return pl.pallas_call(
        body, name=name,
        in_specs=[HBM_SPEC] * n, out_specs=[HBM_SPEC] * n,
        out_shape=[jax.ShapeDtypeStruct((3,) + s.shape[1:], s.dtype) for s in sums],
        scratch_shapes=[pltpu.SemaphoreType.DMA((3 * n,)), pltpu.SemaphoreType.DMA((3 * n,))],
    )(*sums)


def _pair_sum(grad, recv, pos, tr):
    _, _, rows, cols = grad.shape

    def body(pos_ref, g_ref, r_ref, sb_ref):
        sb_ref[...] = (g_ref[...] + r_ref[...]).astype(BF16)

    return pl.pallas_call(
        body, name="pair_sum",
        grid_spec=pltpu.PrefetchScalarGridSpec(
            num_scalar_prefetch=1, grid=(N_CHIPS, rows // tr),
            in_specs=[pl.BlockSpec((None, None, tr, cols), lambda p, i, pos: (pos[0], p, i, 0)),
                      pl.BlockSpec((None, tr, cols), lambda p, i, pos: (p, i, 0))],
            out_specs=pl.BlockSpec((None, tr, cols), lambda p, i, pos: (p, i, 0))),
        out_shape=jax.ShapeDtypeStruct((N_CHIPS, rows, cols), BF16),
        compiler_params=_params("parallel", "parallel"),
    )(pos, grad, recv)


def _adam_math(w, g, m, v):
    m = ADAM_B1 * m + (1.0 - ADAM_B1) * g
    v = ADAM_B2 * v + (1.0 - ADAM_B2) * (g * g)
    m_hat = m / (1.0 - ADAM_B1 ** ADAM_STEP)
    v_hat = v / (1.0 - ADAM_B2 ** ADAM_STEP)
    delta = -ADAM_LR * (m_hat / (jnp.sqrt(v_hat) + ADAM_EPS) + ADAM_WD * w)
    return delta, m, v


def _adamw_shard(w, m, v, grad, from_sibling, from_chips, pos, tr):
    rows, cols = w.shape

    def body(pos_ref, w_ref, m_ref, v_ref, own_ref, sib_ref, r_ref, g_ref, d_ref, nm_ref, nv_ref):
        g = own_ref[...] + sib_ref[...]
        for k in range(3):
            g = g + r_ref[k].astype(F32)
        g_ref[...] = g
        d_ref[...], nm_ref[...], nv_ref[...] = _adam_math(w_ref[...], g, m_ref[...], v_ref[...])

    tile = pl.BlockSpec((tr, cols), lambda i, pos: (i, 0))
    out = jax.ShapeDtypeStruct((rows, cols), F32)
    return pl.pallas_call(
        body, name="adamw_shard",
        grid_spec=pltpu.PrefetchScalarGridSpec(
            num_scalar_prefetch=1, grid=(rows // tr,),
            in_specs=[tile, tile, tile,
                      pl.BlockSpec((None, None, tr, cols), lambda i, pos: (pos[0], pos[1], i, 0)),
                      pl.BlockSpec((None, tr, cols), lambda i, pos: (pos[1], i, 0)),
                      pl.BlockSpec((3, tr, cols), lambda i, pos: (0, i, 0))],
            out_specs=[tile] * 4),
        out_shape=[out] * 4,
        compiler_params=_params("parallel"),
    )(pos, w, m, v, grad, from_sibling, from_chips)


def _adamw_scattered(w, m, v, grad, recv, order, tr):
    rows, cols = w.shape
    n_peers = N_DEV - 1

    def body(order_ref, w_ref, m_ref, v_ref, own_ref, *rest):
        peers, (g_ref, d_ref, nm_ref, nv_ref) = rest[:n_peers], rest[n_peers:]
        g = own_ref[...]
        for r_ref in peers:
            g = g + r_ref[...].astype(F32)
        g_ref[...] = g
        d_ref[...], nm_ref[...], nv_ref[...] = _adam_math(w_ref[...], g, m_ref[...], v_ref[...])

    tile = pl.BlockSpec((tr, cols), lambda i, order_ref: (i, 0))
    slot = lambda s: pl.BlockSpec((None, tr, cols), lambda i, order_ref: (order_ref[s], i, 0))
    out = jax.ShapeDtypeStruct((rows, cols), F32)
    return pl.pallas_call(
        body, name="adamw_scattered",
        grid_spec=pltpu.PrefetchScalarGridSpec(
            num_scalar_prefetch=1, grid=(rows // tr,),
            in_specs=[tile, tile, tile, slot(n_peers)] + [slot(s) for s in range(n_peers)],
            out_specs=[tile] * 4),
        out_shape=[out] * 4,
        compiler_params=_params("parallel"),
    )(order, w, m, v, grad, *([recv] * n_peers))


def _sum_devices(gathered):
    _, rows, cols = gathered.shape

    def body(g_ref, o_ref):
        s = g_ref[0]
        for d in range(1, N_DEV):
            s = s + g_ref[d]
        o_ref[...] = s

    return pl.pallas_call(
        body, name="sum_devices", in_specs=[_full(gathered.shape)], out_specs=_full((rows, cols)), grid=(1,),
        out_shape=jax.ShapeDtypeStruct((rows, cols), F32),
    )(gathered)


def _adamw_small(w, g, m, v):
    def body(w_ref, g_ref, m_ref, v_ref, d_ref, nm_ref, nv_ref):
        d_ref[...], nm_ref[...], nv_ref[...] = _adam_math(w_ref[...], g_ref[...], m_ref[...], v_ref[...])

    spec = _full(w.shape)
    out = jax.ShapeDtypeStruct(w.shape, F32)
    return pl.pallas_call(
        body, name="adamw_small", grid=(1,), in_specs=[spec] * 4, out_specs=[spec] * 3, out_shape=[out] * 3,
    )(w, g, m, v)


TOKEN_TILE = 512
MID_TILE = 256
DW_TILE = 1024
ADAM_ROWS = 128


def _local_grads(x, target, g1, w_in, conv_w, sinks, g_attn, g_conv, g2, g3, g4, shards, order):
    t = x.shape[0]
    tm = min(TOKEN_TILE, t)
    rope = _rope_tables(t)
    qkv, gates, mconv, gathered = _in_proj_fwd(x, g1, w_in, conv_w, g_conv, rope, tm, shards)
    attn, mattn, (w_out, w_up, w_down) = _attn_fwd(qkv, sinks, g_attn, shards, gathered)
    actt, dup, hn2t, dmo, dmix, dh, dmixed, small_mid = _mid(
        mattn, mconv, x, target, g2, g3, g4, w_out.reshape(D_MODEL, D_MODEL), w_up, w_down, min(MID_TILE, t))
    tk = min(DW_TILE, t)
    dw_up = _dw_scatter(hn2t, dup, order, tk, False, "dw_up")
    dw_down = _dw_scatter(actt, dmo, order, tk, True, "dw_down")
    dw_out = _dw_out(mattn, mconv, dmix, tk)
    dattn, dgates, small_mix = _mix_bwd(dmixed, attn, gates, g_attn, g_conv, conv_w, tm)
    dqkv, dsink = _attn_bwd(qkv, dattn, sinks, rope)
    grad_x, dwa, dwb, small_in = _in_proj_bwd(dqkv, dgates, x, dh, g1, w_in, tm)
    dw_in = jnp.concatenate([dwa, dwb], axis=1)
    return grad_x, dw_in, dw_out, dw_up, dw_down, (small_mid, small_mix, dsink, small_in)


def _by_dest(a, rows_major):
    r, c = a.shape
    if rows_major:
        return a.reshape(N_CHIPS, 2, r // N_DEV, c).transpose(1, 0, 2, 3)
    return a.reshape(r, N_CHIPS, 2, c // N_DEV).transpose(2, 1, 0, 3)


def _pack_small(small_mid, small_mix, dsink, small_in):
    z = lambda n: jnp.zeros((1, n), F32)
    rows = [
        small_mid[ROW_LOSS:ROW_LOSS + 1],
        small_in[0:1],
        small_mid[ROW_G2:ROW_G2 + 1],
        small_mid[ROW_G3:ROW_G3 + 1],
        small_mid[ROW_G4:ROW_G4 + 1],
        jnp.concatenate([small_mix[ROW_GATTN:ROW_GATTN + 1], small_mix[ROW_GCONV:ROW_GCONV + 1]], axis=1),
        jnp.concatenate([small_mix[ROW_CW0:ROW_CW0 + 1], small_mix[ROW_CW0 + 1:ROW_CW0 + 2]], axis=1),
        jnp.concatenate([small_mix[ROW_CW0 + 2:ROW_CW0 + 3], dsink[0:1, :], z(D_MODEL - CONV_W - 128)], axis=1),
    ]
    return jnp.concatenate(rows, axis=0)


def kernel(x, pre_mix_norm, w_in, conv_w, attn_sinks, attn_group_norm, conv_group_norm, w_out, post_mix_norm, pre_mlp_norm, w_up, w_down, post_mlp_norm, loss_target, m_pre_mix_norm, m_w_in, m_conv_w, m_attn_sinks, m_attn_group_norm, m_conv_group_norm, m_w_out, m_post_mix_norm, m_pre_mlp_norm, m_w_up, m_w_down, m_post_mlp_norm, v_pre_mix_norm, v_w_in, v_conv_w, v_attn_sinks, v_attn_group_norm, v_conv_group_norm, v_w_out, v_post_mix_norm, v_pre_mlp_norm, v_w_up, v_w_down, v_post_mlp_norm):
    xi, yi, ci = _mesh_pos()
    chip = 2 * xi + yi
    dev = 2 * chip + ci

    order = _peer_order(dev)
    pos = jnp.stack([ci, chip]).astype(jnp.int32)

    gw_in, gconv = _all_gather([w_in[0].astype(BF16), conv_w[0]], "gather_w_in")
    w_in_full = gw_in.transpose(1, 0, 2).reshape(D_MODEL, IN_COLS)
    conv_full = gconv.transpose(1, 0, 2).reshape(3, CONV_W)
    shards = [w_out[0].astype(BF16), w_up[0].astype(BF16), w_down[0].astype(BF16)]

    grad_x, dw_in, dw_out, dw_up, dw_down, smalls = _local_grads(
        x[0], loss_target[0], pre_mix_norm, w_in_full, conv_full, attn_sinks, attn_group_norm, conv_group_norm,
        post_mix_norm, pre_mlp_norm, post_mlp_norm, shards, order)

    grads = [_by_dest(dw_in, False), _by_dest(dw_out, True)]
    from_sibling = _sibling_exchange(grads, "reduce_sibling")
    summed = [_pair_sum(g, r, pos, ADAM_ROWS) for g, r in zip(grads, from_sibling)]
    from_chips = _chip_exchange(summed, "reduce_chips")

    small = _sum_devices(_all_gather([_pack_small(*smalls)], "gather_small")[0])
    loss = (0.5 / D_MODEL) * jnp.sum(small[0])

    big = {}
    for name, w, m, v, g, rs, rc in zip(("w_in", "w_out"), (w_in, w_out), (m_w_in, m_w_out), (v_w_in, v_w_out), grads,
                                        from_sibling, from_chips):
        big[name] = [a[None] for a in _adamw_shard(w[0], m[0], v[0], g, rs, rc, pos, ADAM_ROWS)]
    for name, w, m, v, (g, recv) in zip(("w_up", "w_down"), (w_up, w_down), (m_w_up, m_w_down), (v_w_up, v_w_down),
                                        (dw_up, dw_down)):
        big[name] = [a[None] for a in _adamw_scattered(w[0], m[0], v[0], g, recv, order, ADAM_ROWS)]

    conv_g = lax.dynamic_slice(
        jnp.stack([small[6, :CONV_W], small[6, CONV_W:], small[7, :CONV_W]]), (0, dev * (CONV_W // N_DEV)),
        (3, CONV_W // N_DEV))
    pad = lambda a, n: jnp.pad(a.reshape(1, -1), ((0, 0), (0, n - a.size)))
    small_names = ("pre_mix_norm", "post_mix_norm", "pre_mlp_norm", "post_mlp_norm")
    small_w = {"pre_mix_norm": (pre_mix_norm, m_pre_mix_norm, v_pre_mix_norm),
               "post_mix_norm": (post_mix_norm, m_post_mix_norm, v_post_mix_norm),
               "pre_mlp_norm": (pre_mlp_norm, m_pre_mlp_norm, v_pre_mlp_norm),
               "post_mlp_norm": (post_mlp_norm, m_post_mlp_norm, v_post_mlp_norm)}

    def pack(k):
        rows = [small_w[nm][k] for nm in small_names]
        rows.append(jnp.concatenate([(attn_group_norm, m_attn_group_norm, v_attn_group_norm)[k],
                                     (conv_group_norm, m_conv_group_norm, v_conv_group_norm)[k]], axis=1))
        rows.append(pad((conv_w, m_conv_w, v_conv_w)[k], D_MODEL))
        rows.append(pad((attn_sinks, m_attn_sinks, v_attn_sinks)[k], D_MODEL))
        rows.append(jnp.zeros((1, D_MODEL), F32))
        return jnp.concatenate(rows, axis=0)

    g_small = jnp.concatenate(
        [small[1:6], pad(conv_g, D_MODEL), pad(small[7, CONV_W:CONV_W + N_HEADS], D_MODEL), jnp.zeros((1, D_MODEL), F32)],
        axis=0)
    d_small, nm_small, nv_small = _adamw_small(pack(0), g_small, pack(1), pack(2))

    def unpack(a):
        nconv = 3 * CONV_W // N_DEV
        return {"pre_mix_norm": a[0:1], "post_mix_norm": a[1:2], "pre_mlp_norm": a[2:3], "post_mlp_norm": a[3:4],
                "attn_group_norm": a[4:5, :ATTN_W], "conv_group_norm": a[4:5, ATTN_W:],
                "conv_w": a[5, :nconv].reshape(1, 3, CONV_W // N_DEV), "attn_sinks": a[6:7, :N_HEADS]}

    order = ("pre_mix_norm", "w_in", "conv_w", "attn_sinks", "attn_group_norm", "conv_group_norm", "w_out",
             "post_mix_norm", "pre_mlp_norm", "w_up", "w_down", "post_mlp_norm")
    outs = []
    for k, a in enumerate((g_small, d_small, nm_small, nv_small)):
        sm = unpack(a)
        outs += [big[nm][k] if nm in big else sm[nm] for nm in order]
    return (loss, grad_x[None], *outs)
```

```python
import functools

import jax
import jax.numpy as jnp
import numpy as np
from jax import lax
from jax.experimental import pallas as pl
from jax.experimental.pallas import tpu as pltpu

F32 = jnp.float32
BF16 = jnp.bfloat16

D_MODEL = 1024
HEAD_DIM = 64
ATTN_W = 512
CONV_W = 512
N_HEADS = 8
N_KV = 2
GROUP = 4
KV_W = 128
QKV_W = ATTN_W + 2 * KV_W
GATES_W = 3 * CONV_W
IN_COLS = QKV_W + GATES_W
D_FF = 4096
FF_CHUNK = 512
N_FF_CHUNKS = D_FF // FF_CHUNK
BLOCK = 128
ROT_HALF = 8
ROPE_THETA = 500000.0
NORM_EPS = 1e-6
NEG_INF = -1e30
ATTN_SCALE = 0.125
N_DEV = 8
N_CHIPS = 4
IN_SHARD = IN_COLS // N_DEV

ADAM_LR = 0.001
ADAM_B1 = 0.9
ADAM_B2 = 0.999
ADAM_EPS = 1e-08
ADAM_WD = 0.01
ADAM_STEP = 10

V7X_VMEM_BYTES = 64 * 1024 * 1024
VMEM_LIMIT = V7X_VMEM_BYTES - 2 * 1024 * 1024

MESH = pl.DeviceIdType.MESH
HBM_SPEC = pl.BlockSpec(memory_space=pltpu.HBM)


def _params(*sem):
    return pltpu.CompilerParams(dimension_semantics=sem, vmem_limit_bytes=VMEM_LIMIT)


def _mm(a, b):
    return jnp.dot(a, b, preferred_element_type=F32)


def _mm_nt(a, b):
    return lax.dot_general(a, b, (((1,), (1,)), ((), ())), preferred_element_type=F32)


def _mm_tn(a, b):
    return lax.dot_general(a, b, (((0,), (0,)), ((), ())), preferred_element_type=F32)


def _inv_rms(x):
    return lax.rsqrt(jnp.mean(x * x, axis=-1, keepdims=True) + NORM_EPS)


def _rms_bwd(xhat, r, gain, dy):
    gy = dy * gain
    return r * (gy - xhat * jnp.mean(gy * xhat, axis=-1, keepdims=True)), dy * xhat


def _colsum(a):
    return jnp.sum(a, axis=0, keepdims=True)


def _full(shape):
    zeros = (0,) * len(shape)
    return pl.BlockSpec(shape, lambda *_: zeros)


def _resident(shape):
    zeros = (0,) * len(shape)
    return pl.BlockSpec(shape, lambda *_: zeros, pipeline_mode=pl.Buffered(1))


def _rope_tables(t):
    pos = np.arange(t, dtype=np.float32)
    inv_freq = (ROPE_THETA ** (-np.arange(0, 2 * ROT_HALF, 2, dtype=np.float64) / (2 * ROT_HALF))).astype(np.float32)
    ang = (pos[:, None] * inv_freq[None, :]).astype(np.float64)
    cos, sin = np.cos(ang).astype(np.float32), np.sin(ang).astype(np.float32)
    zeros8 = np.zeros((t, ROT_HALF), np.float32)
    rest = np.zeros((t, HEAD_DIM - 2 * ROT_HALF), np.float32)
    c_head = np.concatenate([cos, cos, rest + 1.0], axis=1)
    s1_head = np.concatenate([zeros8, sin, rest], axis=1)
    s2_head = np.concatenate([-sin, zeros8, rest], axis=1)
    two = lambda a: jnp.asarray(np.concatenate([a, a], axis=1))
    return two(c_head), two(s1_head), two(s2_head)


def _rope(v, c, s1, s2):
    return v * c + pltpu.roll(v, ROT_HALF, 1) * s1 + pltpu.roll(v, 128 - ROT_HALF, 1) * s2


def _rope_transpose(dv, c, s1, s2):
    return dv * c + pltpu.roll(dv * s1, 128 - ROT_HALF, 1) + pltpu.roll(dv * s2, ROT_HALF, 1)


def _shift_rows_down(u, prev, k):
    row = lax.broadcasted_iota(jnp.int32, u.shape, 0)
    out = pltpu.roll(u, k, 0)
    for r in range(k):
        out = jnp.where(row == r, prev[8 - k + r:8 - k + r + 1, :], out)
    return out


def _shift_rows_up(u, nxt, k):
    n = u.shape[0]
    row = lax.broadcasted_iota(jnp.int32, u.shape, 0)
    out = pltpu.roll(u, n - k, 0)
    for r in range(k):
        out = jnp.where(row == n - k + r, nxt[r:r + 1, :], out)
    return out


def _conv3(u, u1, u2, w):
    return (w[0:1, :] * u2 + w[1:2, :] * u1) + w[2:3, :] * u


def _mesh_pos():
    return lax.axis_index("x"), lax.axis_index("y"), lax.axis_index("c")


def _slot(ref, pos):
    return ref.at[4 * pos[0] + 2 * pos[1] + pos[2]]


def _push(src, dst, sems, k, to):
    send_sems, recv_sems = sems
    return pltpu.make_async_remote_copy(src_ref=src, dst_ref=dst, send_sem=send_sems.at[k], recv_sem=recv_sems.at[k],
                                        device_id=to, device_id_type=MESH)


def _gather_near(first, last, shards, outs, sems, local_sems):
    x, y, c = _mesh_pos()
    me, peers = (x, y, c), [(x, y, 1 - c), (1 - x, y, c), (x, 1 - y, c)]
    n = len(shards)
    local = [pltpu.make_async_copy(shards[i], _slot(outs[i], me), local_sems.at[i]) for i in range(n)]
    sends = [_push(shards[i], _slot(outs[i], me), sems, 3 * i + k, peers[k]) for i in range(n) for k in range(3)]
    arrivals = [_push(shards[i], _slot(outs[i], peers[k]), sems, 3 * i + k, peers[k]) for i in range(n) for k in range(3)]

    @pl.when(first)
    def _():
        for cp in local + sends:
            cp.start()

    @pl.when(last)
    def _():
        for cp in sends:
            cp.wait_send()
        for cp in arrivals:
            cp.wait_recv()
        for cp in local:
            cp.wait()


def _gather_far(first, last, shards, ins, outs, sems):
    x, y, c = _mesh_pos()
    me, sibling = (x, y, c), (x, y, 1 - c)
    chips = [(1 - x, y), (x, 1 - y), (1 - x, 1 - y)]
    n = len(shards)
    diag_send = [_push(shards[i], _slot(outs[i], me), sems, 4 * i, (*chips[2], c)) for i in range(n)]
    diag_arrival = [_push(shards[i], _slot(outs[i], (*chips[2], c)), sems, 4 * i, (*chips[2], c)) for i in range(n)]
    passed = [[_push(_slot(ins[i], (*chips[j], c)), _slot(outs[i], (*chips[j], c)), sems, 4 * i + 1 + j, sibling)
               for i in range(n)] for j in range(3)]
    from_sibling = [_push(shards[i], _slot(outs[i], (*chips[j], 1 - c)), sems, 4 * i + 1 + j, sibling)
                    for i in range(n) for j in range(3)]

    @pl.when(first)
    def _():
        for cp in diag_send + passed[0] + passed[1]:
            cp.start()

    @pl.when(last)
    def _():
        for cp in diag_arrival:
            cp.wait_recv()
        for cp in passed[2]:
            cp.start()
        for cp in from_sibling:
            cp.wait_recv()
        for cp in diag_send + passed[0] + passed[1] + passed[2]:
            cp.wait_send()


def _in_proj_fwd(x, g1, w_in, conv_w, g_conv, rope, tm, shards):
    t = x.shape[0]
    rc, rs1, rs2 = rope
    n = len(shards)

    def body(*refs):
        x_ref, g1_ref, w_ref, cw_ref, gc_ref, c_ref, s1_ref, s2_ref = refs[:8]
        shard_refs = refs[8:8 + n]
        qkv_ref, gates_ref, mconv_ref = refs[8 + n:11 + n]
        gathered = refs[11 + n:11 + 2 * n]
        carry_ref = refs[11 + 2 * n]
        if n:
            step = pl.program_id(0)
            _gather_near(step == 0, step == pl.num_programs(0) - 1, shard_refs, gathered, refs[12 + 2 * n:14 + 2 * n],
                         refs[14 + 2 * n])

        @pl.when(pl.program_id(0) == 0)
        def _():
            carry_ref[...] = jnp.zeros_like(carry_ref)

        xv = x_ref[...]
        hn = ((xv * _inv_rms(xv)) * g1_ref[...]).astype(BF16)
        proj = _mm(hn, w_ref[...])
        c, s1, s2 = c_ref[...], s1_ref[...], s2_ref[...]
        for ci in range((ATTN_W + KV_W) // 128):
            sl = slice(128 * ci, 128 * (ci + 1))
            qkv_ref[:, sl] = _rope(proj[:, sl], c, s1, s2).astype(BF16)
        qkv_ref[:, ATTN_W + KV_W:QKV_W] = proj[:, ATTN_W + KV_W:QKV_W].astype(BF16)
        gates = proj[:, QKV_W:]
        gates_ref[...] = gates
        gb, gcc, xin = gates[:, :CONV_W], gates[:, CONV_W:2 * CONV_W], gates[:, 2 * CONV_W:]
        u = gcc * xin
        prev = carry_ref[...]
        conv = gb * _conv3(u, _shift_rows_down(u, prev, 1), _shift_rows_down(u, prev, 2), cw_ref[...])
        carry_ref[...] = u[tm - 8:tm, :]
        mconv_ref[...] = ((conv * _inv_rms(conv)) * gc_ref[...]).astype(BF16)

    tile = lambda w_: pl.BlockSpec((tm, w_), lambda i: (i, 0))
    comm_scratch = [pltpu.SemaphoreType.DMA((3 * n,)), pltpu.SemaphoreType.DMA((3 * n,)), pltpu.SemaphoreType.DMA((n,))]
    res = pl.pallas_call(
        body, name="in_proj_fwd", grid=(t // tm,),
        in_specs=[tile(D_MODEL), _full((1, D_MODEL)), _full((D_MODEL, IN_COLS)), _full((3, CONV_W)), _full((1, CONV_W)),
                  tile(128), tile(128), tile(128)] + [HBM_SPEC] * n,
        out_specs=[tile(QKV_W), tile(GATES_W), tile(CONV_W)] + [HBM_SPEC] * n,
        out_shape=[jax.ShapeDtypeStruct((t, QKV_W), BF16), jax.ShapeDtypeStruct((t, GATES_W), F32),
                   jax.ShapeDtypeStruct((t, CONV_W), BF16)]
        + [jax.ShapeDtypeStruct((N_DEV,) + s.shape, s.dtype) for s in shards],
        scratch_shapes=[pltpu.VMEM((8, CONV_W), F32)] + (comm_scratch if n else []),
        compiler_params=_params("arbitrary"),
    )(x, g1, w_in, conv_w, g_conv, rc, rs1, rs2, *shards)
    return res[0], res[1], res[2], list(res[3:])


GROUP_ROWS = GROUP * BLOCK


def _attn_mask(j):
    row = lax.broadcasted_iota(jnp.int32, (GROUP_ROWS, 2 * BLOCK), 0) & (BLOCK - 1)
    col = lax.broadcasted_iota(jnp.int32, (GROUP_ROWS, 2 * BLOCK), 1)
    return (col > row) & (col <= row + BLOCK) & ((col >= BLOCK) | (j > 0))


def _stack_heads(a, g):
    return jnp.concatenate([a[:, HEAD_DIM * (GROUP * g + hh):HEAD_DIM * (GROUP * g + hh + 1)] for hh in range(GROUP)], axis=0)


def _unstack_heads(a):
    return jnp.concatenate([a[BLOCK * hh:BLOCK * (hh + 1), :] for hh in range(GROUP)], axis=1)


def _group_sinks(sink_ref, g):
    head = lax.broadcasted_iota(jnp.int32, (GROUP_ROWS, 1), 0) // BLOCK
    out = jnp.full((GROUP_ROWS, 1), sink_ref[0, GROUP * g], F32)
    for hh in range(1, GROUP):
        out = jnp.where(head == hh, sink_ref[0, GROUP * g + hh], out)
    return out


def _attn_probs(qs, kk, sink, valid):
    s = jnp.where(valid, _mm_nt(qs, kk) * ATTN_SCALE, NEG_INF)
    m = jnp.maximum(jnp.max(s, axis=-1, keepdims=True), sink)
    p = jnp.exp(s - m)
    psink = jnp.exp(sink - m)
    inv_l = 1.0 / (jnp.sum(p, axis=-1, keepdims=True) + psink)
    return p * inv_l, psink * inv_l


def _qkv_specs(order):
    prev = lambda i: jnp.maximum(order(i) - 1, 0)
    kcol, vcol = ATTN_W // KV_W, ATTN_W // KV_W + 1
    return [pl.BlockSpec((BLOCK, ATTN_W), lambda i: (order(i), 0)),
            pl.BlockSpec((BLOCK, KV_W), lambda i: (prev(i), kcol)), pl.BlockSpec((BLOCK, KV_W), lambda i: (order(i), kcol)),
            pl.BlockSpec((BLOCK, KV_W), lambda i: (prev(i), vcol)), pl.BlockSpec((BLOCK, KV_W), lambda i: (order(i), vcol))]


def _attn_fwd(qkv, sinks, g_attn, shards, gathered):
    t = qkv.shape[0]
    n = len(shards)

    def body(*refs):
        sink_ref, q_ref, kp_ref, kc_ref, vp_ref, vc_ref, ga_ref = refs[:7]
        attn_ref, mattn_ref = refs[7 + 2 * n:9 + 2 * n]
        if n:
            step = pl.program_id(0)
            _gather_far(step == 0, step == pl.num_programs(0) - 1, refs[7:7 + n], refs[7 + n:7 + 2 * n],
                        refs[9 + 2 * n:9 + 3 * n], refs[9 + 3 * n:11 + 3 * n])
        valid = _attn_mask(pl.program_id(0))
        q, kp, kc, vp, vc = q_ref[...], kp_ref[...], kc_ref[...], vp_ref[...], vc_ref[...]
        outs = []
        for g in range(N_KV):
            gs = slice(HEAD_DIM * g, HEAD_DIM * (g + 1))
            kk = jnp.concatenate([kp[:, gs], kc[:, gs]], axis=0)
            vv = jnp.concatenate([vp[:, gs], vc[:, gs]], axis=0)
            probs, _ = _attn_probs(_stack_heads(q, g), kk, _group_sinks(sink_ref, g), valid)
            outs.append(_unstack_heads(_mm(probs.astype(BF16), vv)))
        attn = jnp.concatenate(outs, axis=1)
        attn_ref[...] = attn
        mattn_ref[...] = ((attn * _inv_rms(attn)) * ga_ref[...]).astype(BF16)

    blk = pl.BlockSpec((BLOCK, ATTN_W), lambda j: (j, 0))
    res = pl.pallas_call(
        body, name="attn_fwd", grid=(t // BLOCK,),
        in_specs=[pl.BlockSpec(memory_space=pltpu.SMEM)] + _qkv_specs(lambda j: j) + [_full((1, ATTN_W))]
        + [HBM_SPEC] * (2 * n),
        out_specs=[blk, blk] + [HBM_SPEC] * n,
        out_shape=[jax.ShapeDtypeStruct((t, ATTN_W), F32), jax.ShapeDtypeStruct((t, ATTN_W), BF16)]
        + [jax.ShapeDtypeStruct(g.shape, g.dtype) for g in gathered],
        input_output_aliases={7 + n + i: 2 + i for i in range(n)},
        scratch_shapes=[pltpu.SemaphoreType.DMA((4 * n,)), pltpu.SemaphoreType.DMA((4 * n,))] if n else [],
        compiler_params=_params("arbitrary"),
    )(sinks, qkv, qkv, qkv, qkv, qkv, g_attn, *shards, *gathered)
    return res[0], res[1], list(res[2:])


SMALL_ROWS = 8
ROW_LOSS, ROW_G2, ROW_G3, ROW_G4 = 0, 1, 2, 3


def _mid(mattn, mconv, x, target, g2, g3, g4, w_out, w_up, w_down, tm):
    t = x.shape[0]

    def body(ma_ref, mc_ref, x_ref, t_ref, g2_ref, g3_ref, g4_ref, wo_ref, wu_ref, wd_ref,
             actt_ref, dup_ref, hn2t_ref, dmo_ref, dmix_ref, dh_ref, dmixed_ref, small_ref, up_ref):
        @pl.when(pl.program_id(0) == 0)
        def _():
            small_ref[...] = jnp.zeros_like(small_ref)

        g2, g3, g4 = g2_ref[...], g3_ref[...], g4_ref[...]
        mix_out = _mm(ma_ref[...], wo_ref[0:ATTN_W, :]) + _mm(mc_ref[...], wo_ref[ATTN_W:, :])
        r2 = _inv_rms(mix_out)
        mo_hat = mix_out * r2
        h = x_ref[...] + mo_hat * g2
        r3 = _inv_rms(h)
        h_hat = h * r3
        hn2 = (h_hat * g3).astype(BF16)
        hn2t_ref[...] = hn2.T
        mlp = jnp.zeros((tm, D_MODEL), F32)
        for j in range(N_FF_CHUNKS):
            up = jnp.maximum(_mm(hn2, wu_ref[j]), 0.0)
            up_ref[j] = up.astype(BF16)
            act = (up * up).astype(BF16)
            actt_ref[FF_CHUNK * j:FF_CHUNK * (j + 1), :] = act.T
            mlp = mlp + _mm(act, wd_ref[j])
        r4 = _inv_rms(mlp)
        ml_hat = mlp * r4
        err = (h + ml_hat * g4) - t_ref[...]
        d_out = err * (1.0 / D_MODEL)
        d_mlp, dg4 = _rms_bwd(ml_hat, r4, g4, d_out)
        dmo = d_mlp.astype(BF16)
        dmo_ref[...] = dmo
        dhn2 = jnp.zeros((tm, D_MODEL), F32)
        for j in range(N_FF_CHUNKS):
            dup = (_mm_nt(dmo, wd_ref[j]) * (2.0 * up_ref[j].astype(F32))).astype(BF16)
            dup_ref[:, FF_CHUNK * j:FF_CHUNK * (j + 1)] = dup
            dhn2 = dhn2 + _mm_nt(dup, wu_ref[j])
        dh_norm, dg3 = _rms_bwd(h_hat, r3, g3, dhn2)
        dh = d_out + dh_norm
        dh_ref[...] = dh
        d_mix, dg2 = _rms_bwd(mo_hat, r2, g2, dh)
        dmix = d_mix.astype(BF16)
        dmix_ref[...] = dmix
        dmixed_ref[...] = _mm_nt(dmix, wo_ref[...])
        small_ref[ROW_LOSS:ROW_LOSS + 1, :] += _colsum(err * err)
        small_ref[ROW_G2:ROW_G2 + 1, :] += _colsum(dg2)
        small_ref[ROW_G3:ROW_G3 + 1, :] += _colsum(dg3)
        small_ref[ROW_G4:ROW_G4 + 1, :] += _colsum(dg4)

    tile = lambda n: pl.BlockSpec((tm, n), lambda i: (i, 0))
    cols = lambda n: pl.BlockSpec((n, tm), lambda i: (0, i))
    gain = _full((1, D_MODEL))
    return pl.pallas_call(
        body, name="mid_fwd_bwd", grid=(t // tm,),
        in_specs=[tile(ATTN_W), tile(CONV_W), tile(D_MODEL), tile(D_MODEL), gain, gain, gain,
                  _resident((D_MODEL, D_MODEL)), _resident((N_FF_CHUNKS, D_MODEL, FF_CHUNK)),
                  _resident((N_FF_CHUNKS, FF_CHUNK, D_MODEL))],
        out_specs=[cols(D_FF), tile(D_FF), cols(D_MODEL), tile(D_MODEL), tile(D_MODEL), tile(D_MODEL), tile(D_MODEL),
                   _full((SMALL_ROWS, D_MODEL))],
        out_shape=[jax.ShapeDtypeStruct((D_FF, t), BF16), jax.ShapeDtypeStruct((t, D_FF), BF16),
                   jax.ShapeDtypeStruct((D_MODEL, t), BF16), jax.ShapeDtypeStruct((t, D_MODEL), BF16),
                   jax.ShapeDtypeStruct((t, D_MODEL), BF16), jax.ShapeDtypeStruct((t, D_MODEL), F32),
                   jax.ShapeDtypeStruct((t, D_MODEL), F32), jax.ShapeDtypeStruct((SMALL_ROWS, D_MODEL), F32)],
        scratch_shapes=[pltpu.VMEM((N_FF_CHUNKS, tm, FF_CHUNK), BF16)],
        compiler_params=_params("arbitrary"),
    )(mattn, mconv, x, target, g2, g3, g4, w_out, w_up, w_down)


PEER_FLIPS = ((1, 1, 0), (1, 0, 0), (0, 1, 0), (1, 1, 1), (1, 0, 1), (0, 1, 1), (0, 0, 1))


def _peer_order(dev):
    masks = [4 * fx + 2 * fy + fc for fx, fy, fc in PEER_FLIPS] + [0]
    return jnp.bitwise_xor(dev, jnp.asarray(masks, jnp.int32)).astype(jnp.int32)


def _dw_scatter(at, b, order, tk, at_chunked, name):
    t = b.shape[0]
    n_k = t // tk
    rows, cols = (FF_CHUNK, D_MODEL) if at_chunked else (D_MODEL, FF_CHUNK)

    def body(order_ref, a_ref, b_ref, o_ref, recv_ref, send_buf, send_sems, recv_sems):
        s_now, k = pl.program_id(0), pl.program_id(1)
        x, y, c = _mesh_pos()
        mine = 4 * x + 2 * y + c

        def peer(s):
            fx, fy, fc = PEER_FLIPS[s]
            return (1 - x if fx else x, 1 - y if fy else y, 1 - c if fc else c)

        def send(s):
            return _push(send_buf.at[s], recv_ref.at[mine], (send_sems, recv_sems), s, peer(s))

        def arrival(s):
            return _push(send_buf.at[s], _slot(recv_ref, peer(s)), (send_sems, recv_sems), s, peer(s))

        @pl.when(k == 0)
        def _():
            o_ref[...] = jnp.zeros_like(o_ref)

        tokens = pl.ds(pl.multiple_of(k * tk, tk), tk)
        if at_chunked:
            o_ref[...] += _mm(a_ref[...], b_ref[tokens, :])
        else:
            o_ref[...] += _mm(a_ref[:, tokens], b_ref[...])
        for s in range(N_DEV - 1):
            @pl.when((s_now == s) & (k == n_k - 1))
            def _():
                send_buf[s] = o_ref[...].astype(BF16)
                send(s).start()

        @pl.when((s_now == N_DEV - 1) & (k == n_k - 1))
        def _():
            for s in range(N_DEV - 1):
                send(s).wait_send()
                arrival(s).wait_recv()

    if at_chunked:
        in_specs = [pl.BlockSpec((FF_CHUNK, tk), lambda s, k, order_ref: (order_ref[s], k)), _resident((t, D_MODEL))]
    else:
        in_specs = [_resident((D_MODEL, t)), pl.BlockSpec((tk, FF_CHUNK), lambda s, k, order_ref: (k, order_ref[s]))]
    return pl.pallas_call(
        body, name=name,
        grid_spec=pltpu.PrefetchScalarGridSpec(
            num_scalar_prefetch=1, grid=(N_DEV, n_k), in_specs=in_specs,
            out_specs=[pl.BlockSpec((None, rows, cols), lambda s, k, order_ref: (order_ref[s], 0, 0)), HBM_SPEC],
            scratch_shapes=[pltpu.VMEM((N_DEV - 1, rows, cols), BF16), pltpu.SemaphoreType.DMA((N_DEV - 1,)),
                            pltpu.SemaphoreType.DMA((N_DEV - 1,))]),
        out_shape=[jax.ShapeDtypeStruct((N_DEV, rows, cols), F32), jax.ShapeDtypeStruct((N_DEV, rows, cols), BF16)],
        compiler_params=_params("arbitrary", "arbitrary"),
    )(order, at, b)


def _dw_out(mattn, mconv, dmix, tk):
    t = dmix.shape[0]

    def body(ma_ref, mc_ref, b_ref, o_ref):
        @pl.when(pl.program_id(0) == 0)
        def _():
            o_ref[...] = jnp.zeros_like(o_ref)
        b = b_ref[...]
        o_ref[0:ATTN_W, :] += _mm_tn(ma_ref[...], b)
        o_ref[ATTN_W:, :] += _mm_tn(mc_ref[...], b)

    tile = lambda n: pl.BlockSpec((tk, n), lambda k: (k, 0))
    return pl.pallas_call(
        body, name="dw_out", grid=(t // tk,),
        in_specs=[tile(ATTN_W), tile(CONV_W), tile(D_MODEL)],
        out_specs=_full((D_MODEL, D_MODEL)),
        out_shape=jax.ShapeDtypeStruct((D_MODEL, D_MODEL), F32),
        compiler_params=_params("arbitrary"),
    )(mattn, mconv, dmix)


ROW_GATTN, ROW_GCONV, ROW_CW0 = 0, 1, 2


def _mix_bwd(dmixed, attn, gates, g_attn, g_conv, conv_w, tm):
    t = attn.shape[0]
    n = t // tm
    rev = lambda i: n - 1 - i

    def body(dm_ref, attn_ref, gates_ref, gprev_ref, ga_ref, gc_ref, cw_ref, dattn_ref, dgates_ref, small_ref, carry_ref):
        i = pl.program_id(0)

        @pl.when(i == 0)
        def _():
            small_ref[...] = jnp.zeros_like(small_ref)
            carry_ref[...] = jnp.zeros_like(carry_ref)

        dm = dm_ref[...]
        a = attn_ref[...]
        ra = _inv_rms(a)
        a_hat = a * ra
        dattn, dga = _rms_bwd(a_hat, ra, ga_ref[...], dm[:, :ATTN_W])
        dattn_ref[...] = dattn

        gates = gates_ref[...]
        gb, gcc, xin = gates[:, :CONV_W], gates[:, CONV_W:2 * CONV_W], gates[:, 2 * CONV_W:]
        u = gcc * xin
        gp = gprev_ref[...]
        uprev = jnp.where(rev(i) == 0, 0.0, gp[:, CONV_W:2 * CONV_W] * gp[:, 2 * CONV_W:])
        u1, u2 = _shift_rows_down(u, uprev, 1), _shift_rows_down(u, uprev, 2)
        w = cw_ref[...]
        c = _conv3(u, u1, u2, w)
        conv = gb * c
        rcv = _inv_rms(conv)
        c_hat = conv * rcv
        dconv, dgc = _rms_bwd(c_hat, rcv, gc_ref[...], dm[:, ATTN_W:])
        dc = dconv * gb
        nxt = carry_ref[...]
        du = (w[2:3, :] * dc + w[1:2, :] * _shift_rows_up(dc, nxt, 1)) + w[0:1, :] * _shift_rows_up(dc, nxt, 2)
        carry_ref[...] = dc[0:8, :]
        dgates_ref[:, :CONV_W] = (dconv * c).astype(BF16)
        dgates_ref[:, CONV_W:2 * CONV_W] = (du * xin).astype(BF16)
        dgates_ref[:, 2 * CONV_W:] = (du * gcc).astype(BF16)
        small_ref[ROW_GATTN:ROW_GATTN + 1, :] += _colsum(dga)
        small_ref[ROW_GCONV:ROW_GCONV + 1, :] += _colsum(dgc)
        small_ref[ROW_CW0:ROW_CW0 + 1, :] += _colsum(dc * u2)
        small_ref[ROW_CW0 + 1:ROW_CW0 + 2, :] += _colsum(dc * u1)
        small_ref[ROW_CW0 + 2:ROW_CW0 + 3, :] += _colsum(dc * u)

    tile = lambda w_: pl.BlockSpec((tm, w_), lambda i: (rev(i), 0))
    prev8 = pl.BlockSpec((8, GATES_W), lambda i: (jnp.maximum(rev(i) * (tm // 8) - 1, 0), 0))
    return pl.pallas_call(
        body, name="mix_bwd", grid=(n,),
        in_specs=[tile(D_MODEL), tile(ATTN_W), tile(GATES_W), prev8, _full((1, ATTN_W)), _full((1, CONV_W)),
                  _full((3, CONV_W))],
        out_specs=[tile(ATTN_W), tile(GATES_W), _full((SMALL_ROWS, CONV_W))],
        out_shape=[jax.ShapeDtypeStruct((t, ATTN_W), F32), jax.ShapeDtypeStruct((t, GATES_W), BF16),
                   jax.ShapeDtypeStruct((SMALL_ROWS, CONV_W), F32)],
        scratch_shapes=[pltpu.VMEM((8, CONV_W), F32)],
        compiler_params=_params("arbitrary"),
    )(dmixed, attn, gates, gates, g_attn, g_conv, conv_w)


def _attn_bwd(qkv, dattn, sinks, rope):
    t = qkv.shape[0]
    nb = t // BLOCK
    rev = lambda i: nb - 1 - i
    rc, rs1, rs2 = rope

    def body(sink_ref, q_ref, kp_ref, kc_ref, vp_ref, vc_ref, do_ref, c_ref, s1_ref, s2_ref,
             dqkv_ref, dsink_ref, ck_ref, cv_ref):
        i = pl.program_id(0)

        @pl.when(i == 0)
        def _():
            dsink_ref[...] = jnp.zeros_like(dsink_ref)
            ck_ref[...] = jnp.zeros_like(ck_ref)
            cv_ref[...] = jnp.zeros_like(cv_ref)

        valid = _attn_mask(rev(i))
        q, kp, kc, vp, vc = q_ref[...], kp_ref[...], kc_ref[...], vp_ref[...], vc_ref[...]
        dout = do_ref[...].astype(BF16)
        lane = lax.broadcasted_iota(jnp.int32, (1, 128), 1)
        dsink = jnp.zeros((1, 128), F32)
        dq_parts, dk_parts, dv_parts = [], [], []
        for g in range(N_KV):
            gs = slice(HEAD_DIM * g, HEAD_DIM * (g + 1))
            kk = jnp.concatenate([kp[:, gs], kc[:, gs]], axis=0)
            vv = jnp.concatenate([vp[:, gs], vc[:, gs]], axis=0)
            qs, dos = _stack_heads(q, g), _stack_heads(dout, g)
            probs, psink = _attn_probs(qs, kk, _group_sinks(sink_ref, g), valid)
            dp = _mm_nt(dos, vv)
            delta = jnp.sum(probs * dp, axis=-1, keepdims=True)
            ds = (probs * (dp - delta) * ATTN_SCALE).astype(BF16)
            sink_terms = psink * delta
            for hh in range(GROUP):
                dsink = dsink + jnp.where(lane == GROUP * g + hh, -jnp.sum(sink_terms[BLOCK * hh:BLOCK * (hh + 1), :]), 0.0)
            dq_parts.append(_unstack_heads(_mm(ds, kk)))
            dk_parts.append(_mm_tn(ds, qs))
            dv_parts.append(_mm_tn(probs.astype(BF16), dos))
        dk2 = jnp.concatenate(dk_parts, axis=1)
        dv2 = jnp.concatenate(dv_parts, axis=1)
        dk = dk2[BLOCK:, :] + ck_ref[...]
        dv = dv2[BLOCK:, :] + cv_ref[...]
        ck_ref[...] = dk2[:BLOCK, :]
        cv_ref[...] = dv2[:BLOCK, :]
        c, s1, s2 = c_ref[...], s1_ref[...], s2_ref[...]
        dq = jnp.concatenate(dq_parts, axis=1)
        for ci in range(ATTN_W // 128):
            sl = slice(128 * ci, 128 * (ci + 1))
            dqkv_ref[:, sl] = _rope_transpose(dq[:, sl], c, s1, s2).astype(BF16)
        dqkv_ref[:, ATTN_W:ATTN_W + KV_W] = _rope_transpose(dk, c, s1, s2).astype(BF16)
        dqkv_ref[:, ATTN_W + KV_W:] = dv.astype(BF16)
        dsink_ref[0:1, :] += dsink

    blk = lambda w_: pl.BlockSpec((BLOCK, w_), lambda i: (rev(i), 0))
    return pl.pallas_call(
        body, name="attn_bwd", grid=(nb,),
        in_specs=[pl.BlockSpec(memory_space=pltpu.SMEM)] + _qkv_specs(rev) + [blk(ATTN_W), blk(128), blk(128), blk(128)],
        out_specs=[blk(QKV_W), _full((8, 128))],
        out_shape=[jax.ShapeDtypeStruct((t, QKV_W), BF16), jax.ShapeDtypeStruct((8, 128), F32)],
        scratch_shapes=[pltpu.VMEM((BLOCK, KV_W), F32), pltpu.VMEM((BLOCK, KV_W), F32)],
        compiler_params=_params("arbitrary"),
    )(sinks, qkv, qkv, qkv, qkv, qkv, dattn, rc, rs1, rs2)


def _in_proj_bwd(dqkv, dgates, x, dh, g1, w_in, tm):
    t = x.shape[0]

    def body(dq_ref, dg_ref, x_ref, dh_ref, g1_ref, w_ref, dx_ref, dwa_ref, dwb_ref, dg1_ref):
        @pl.when(pl.program_id(0) == 0)
        def _():
            dwa_ref[...] = jnp.zeros_like(dwa_ref)
            dwb_ref[...] = jnp.zeros_like(dwb_ref)
            dg1_ref[...] = jnp.zeros_like(dg1_ref)

        dq, dg = dq_ref[...], dg_ref[...]
        dhn = _mm_nt(dq, w_ref[:, :QKV_W]) + _mm_nt(dg, w_ref[:, QKV_W:])
        xv = x_ref[...]
        r = _inv_rms(xv)
        x_hat = xv * r
        g1 = g1_ref[...]
        dx, dg1 = _rms_bwd(x_hat, r, g1, dhn)
        dx_ref[...] = dh_ref[...] + dx
        hn = (x_hat * g1).astype(BF16)
        dwa_ref[...] += _mm_tn(hn, dq)
        dwb_ref[...] += _mm_tn(hn, dg)
        dg1_ref[0:1, :] += _colsum(dg1)

    tile = lambda n: pl.BlockSpec((tm, n), lambda i: (i, 0))
    return pl.pallas_call(
        body, name="in_proj_bwd", grid=(t // tm,),
        in_specs=[tile(QKV_W), tile(GATES_W), tile(D_MODEL), tile(D_MODEL), _full((1, D_MODEL)),
                  _resident((D_MODEL, IN_COLS))],
        out_specs=[tile(D_MODEL), _full((D_MODEL, QKV_W)), _full((D_MODEL, GATES_W)), _full((SMALL_ROWS, D_MODEL))],
        out_shape=[jax.ShapeDtypeStruct((t, D_MODEL), F32), jax.ShapeDtypeStruct((D_MODEL, QKV_W), F32),
                   jax.ShapeDtypeStruct((D_MODEL, GATES_W), F32), jax.ShapeDtypeStruct((SMALL_ROWS, D_MODEL), F32)],
        compiler_params=_params("arbitrary"),
    )(dqkv, dgates, x, dh, g1, w_in)


def _all_gather(shards, name):
    n = len(shards)

    def body(*refs):
        ins, outs = refs[:n], refs[n:2 * n]
        send_sems, recv_sems, local_sems = refs[2 * n:]
        x, y, c = _mesh_pos()
        me, sibling = (x, y, c), (x, y, 1 - c)
        chips = [(1 - x, y), (x, 1 - y), (1 - x, 1 - y)]

        def copy(i, k, block, to, src=None):
            dst = outs[i].at[4 * block[0] + 2 * block[1] + block[2]]
            return pltpu.make_async_remote_copy(
                src_ref=dst if src is None else src, dst_ref=dst, send_sem=send_sems.at[7 * i + k],
                recv_sem=recv_sems.at[7 * i + k], device_id=to, device_id_type=MESH)

        mine = [pltpu.make_async_copy(ins[i], outs[i].at[4 * x + 2 * y + c], local_sems.at[i]) for i in range(n)]
        for cp in mine:
            cp.start()
        first = []
        for i in range(n):
            first.append(copy(i, 0, me, sibling, src=ins[i]))
            first += [copy(i, 1 + j, me, (*chip, c), src=ins[i]) for j, chip in enumerate(chips)]
        for cp in first:
            cp.start()
        passed = []
        for j, chip in enumerate(chips):
            for i in range(n):
                copy(i, 1 + j, (*chip, c), me).wait_recv()
                cp = copy(i, 4 + j, (*chip, c), sibling)
                cp.start()
                passed.append(cp)
        for i in range(n):
            copy(i, 0, sibling, me).wait_recv()
            for j, chip in enumerate(chips):
                copy(i, 4 + j, (*chip, 1 - c), me).wait_recv()
        for cp in first + passed:
            cp.wait_send()
        for cp in mine:
            cp.wait()

    return pl.pallas_call(
        body, name=name,
        in_specs=[HBM_SPEC] * n, out_specs=[HBM_SPEC] * n,
        out_shape=[jax.ShapeDtypeStruct((N_DEV,) + s.shape, s.dtype) for s in shards],
        scratch_shapes=[pltpu.SemaphoreType.DMA((7 * n,)), pltpu.SemaphoreType.DMA((7 * n,)),
                        pltpu.SemaphoreType.DMA((n,))],
    )(*shards)


def _sibling_exchange(grads, name):
    n = len(grads)

    def body(*refs):
        ins, outs = refs[:n], refs[n:2 * n]
        send_sems, recv_sems = refs[2 * n:]
        x, y, c = _mesh_pos()
        copies = [pltpu.make_async_remote_copy(
            src_ref=ins[i].at[1 - c], dst_ref=outs[i], send_sem=send_sems.at[i], recv_sem=recv_sems.at[i],
            device_id=(x, y, 1 - c), device_id_type=MESH) for i in range(n)]
        for cp in copies:
            cp.start()
        for cp in copies:
            cp.wait()

    return pl.pallas_call(
        body, name=name,
        in_specs=[HBM_SPEC] * n, out_specs=[HBM_SPEC] * n,
        out_shape=[jax.ShapeDtypeStruct(g.shape[1:], g.dtype) for g in grads],
        scratch_shapes=[pltpu.SemaphoreType.DMA((n,)), pltpu.SemaphoreType.DMA((n,))],
    )(*grads)


def _chip_exchange(sums, name):
    n = len(sums)

    def body(*refs):
        ins, outs = refs[:n], refs[n:2 * n]
        send_sems, recv_sems = refs[2 * n:]
        x, y, c = _mesh_pos()
        chips = [(1 - x, y), (x, 1 - y), (1 - x, 1 - y)]
        copies = [pltpu.make_async_remote_copy(
            src_ref=ins[i].at[2 * chip[0] + chip[1]], dst_ref=outs[i].at[k], send_sem=send_sems.at[3 * i + k],
            recv_sem=recv_sems.at[3 * i + k], device_id=(*chip, c), device_id_type=MESH)
            for i in range(n) for k, chip in enumerate(chips)]
        for cp in copies:
            cp.start()
        for cp in copies:
            cp.wait()

    return pl.pallas_call(
        body, name=name,
        in_specs=[HBM_SPEC] * n, out_specs=[HBM_SPEC] * n,
        out_shape=[jax.ShapeDtypeStruct((3,) + s.shape[1:], s.dtype) for s in sums],
        scratch_shapes=[pltpu.SemaphoreType.DMA((3 * n,)), pltpu.SemaphoreType.DMA((3 * n,))],
    )(*sums)


def _pair_sum(grad, recv, pos, tr):
    _, _, rows, cols = grad.shape

    def body(pos_ref, g_ref, r_ref, sb_ref):
        sb_ref[...] = (g_ref[...] + r_ref[...]).astype(BF16)

    return pl.pallas_call(
        body, name="pair_sum",
        grid_spec=pltpu.PrefetchScalarGridSpec(
            num_scalar_prefetch=1, grid=(N_CHIPS, rows // tr),
            in_specs=[pl.BlockSpec((None, None, tr, cols), lambda p, i, pos: (pos[0], p, i, 0)),
                      pl.BlockSpec((None, tr, cols), lambda p, i, pos: (p, i, 0))],
            out_specs=pl.BlockSpec((None, tr, cols), lambda p, i, pos: (p, i, 0))),
        out_shape=jax.ShapeDtypeStruct((N_CHIPS, rows, cols), BF16),
        compiler_params=_params("parallel", "parallel"),
    )(pos, grad, recv)


def _adam_math(w, g, m, v):
    m = ADAM_B1 * m + (1.0 - ADAM_B1) * g
    v = ADAM_B2 * v + (1.0 - ADAM_B2) * (g * g)
    m_hat = m / (1.0 - ADAM_B1 ** ADAM_STEP)
    v_hat = v / (1.0 - ADAM_B2 ** ADAM_STEP)
    delta = -ADAM_LR * (m_hat / (jnp.sqrt(v_hat) + ADAM_EPS) + ADAM_WD * w)
    return delta, m, v


def _adamw_shard(w, m, v, grad, from_sibling, from_chips, pos, tr):
    rows, cols = w.shape

    def body(pos_ref, w_ref, m_ref, v_ref, own_ref, sib_ref, r_ref, g_ref, d_ref, nm_ref, nv_ref):
        g = own_ref[...] + sib_ref[...]
        for k in range(3):
            g = g + r_ref[k].astype(F32)
        g_ref[...] = g
        d_ref[...], nm_ref[...], nv_ref[...] = _adam_math(w_ref[...], g, m_ref[...], v_ref[...])

    tile = pl.BlockSpec((tr, cols), lambda i, pos: (i, 0))
    out = jax.ShapeDtypeStruct((rows, cols), F32)
    return pl.pallas_call(
        body, name="adamw_shard",
        grid_spec=pltpu.PrefetchScalarGridSpec(
            num_scalar_prefetch=1, grid=(rows // tr,),
            in_specs=[tile, tile, tile,
                      pl.BlockSpec((None, None, tr, cols), lambda i, pos: (pos[0], pos[1], i, 0)),
                      pl.BlockSpec((None, tr, cols), lambda i, pos: (pos[1], i, 0)),
                      pl.BlockSpec((3, tr, cols), lambda i, pos: (0, i, 0))],
            out_specs=[tile] * 4),
        out_shape=[out] * 4,
        compiler_params=_params("parallel"),
    )(pos, w, m, v, grad, from_sibling, from_chips)


def _adamw_scattered(w, m, v, grad, recv, order, tr):
    rows, cols = w.shape
    n_peers = N_DEV - 1

    def body(order_ref, w_ref, m_ref, v_ref, own_ref, *rest):
        peers, (g_ref, d_ref, nm_ref, nv_ref) = rest[:n_peers], rest[n_peers:]
        g = own_ref[...]
        for r_ref in peers:
            g = g + r_ref[...].astype(F32)
        g_ref[...] = g
        d_ref[...], nm_ref[...], nv_ref[...] = _adam_math(w_ref[...], g, m_ref[...], v_ref[...])

    tile = pl.BlockSpec((tr, cols), lambda i, order_ref: (i, 0))
    slot = lambda s: pl.BlockSpec((None, tr, cols), lambda i, order_ref: (order_ref[s], i, 0))
    out = jax.ShapeDtypeStruct((rows, cols), F32)
    return pl.pallas_call(
        body, name="adamw_scattered",
        grid_spec=pltpu.PrefetchScalarGridSpec(
            num_scalar_prefetch=1, grid=(rows // tr,),
            in_specs=[tile, tile, tile, slot(n_peers)] + [slot(s) for s in range(n_peers)],
            out_specs=[tile] * 4),
        out_shape=[out] * 4,
        compiler_params=_params("parallel"),
    )(order, w, m, v, grad, *([recv] * n_peers))


def _sum_devices(gathered):
    _, rows, cols = gathered.shape

    def body(g_ref, o_ref):
        s = g_ref[0]
        for d in range(1, N_DEV):
            s = s + g_ref[d]
        o_ref[...] = s

    return pl.pallas_call(
        body, name="sum_devices", in_specs=[_full(gathered.shape)], out_specs=_full((rows, cols)), grid=(1,),
        out_shape=jax.ShapeDtypeStruct((rows, cols), F32),
    )(gathered)


def _adamw_small(w, g, m, v):
    def body(w_ref, g_ref, m_ref, v_ref, d_ref, nm_ref, nv_ref):
        d_ref[...], nm_ref[...], nv_ref[...] = _adam_math(w_ref[...], g_ref[...], m_ref[...], v_ref[...])

    spec = _full(w.shape)
    out = jax.ShapeDtypeStruct(w.shape, F32)
    return pl.pallas_call(
        body, name="adamw_small", grid=(1,), in_specs=[spec] * 4, out_specs=[spec] * 3, out_shape=[out] * 3,
    )(w, g, m, v)


TOKEN_TILE = 512
MID_TILE = 256
DW_TILE = 1024
DW_SCATTER_TILE = 8192
ADAM_ROWS = 128


def _local_grads(x, target, g1, w_in, conv_w, sinks, g_attn, g_conv, g2, g3, g4, shards, order):
    t = x.shape[0]
    tm = min(TOKEN_TILE, t)
    rope = _rope_tables(t)
    qkv, gates, mconv, gathered = _in_proj_fwd(x, g1, w_in, conv_w, g_conv, rope, tm, shards)
    attn, mattn, (w_out, w_up, w_down) = _attn_fwd(qkv, sinks, g_attn, shards, gathered)
    actt, dup, hn2t, dmo, dmix, dh, dmixed, small_mid = _mid(
        mattn, mconv, x, target, g2, g3, g4, w_out.reshape(D_MODEL, D_MODEL), w_up, w_down, min(MID_TILE, t))
    tk = min(DW_TILE, t)
    dw_up = _dw_scatter(hn2t, dup, order, min(DW_SCATTER_TILE, t), False, "dw_up")
    dw_down = _dw_scatter(actt, dmo, order, min(DW_SCATTER_TILE, t), True, "dw_down")
    dw_out = _dw_out(mattn, mconv, dmix, tk)
    dattn, dgates, small_mix = _mix_bwd(dmixed, attn, gates, g_attn, g_conv, conv_w, tm)
    dqkv, dsink = _attn_bwd(qkv, dattn, sinks, rope)
    grad_x, dwa, dwb, small_in = _in_proj_bwd(dqkv, dgates, x, dh, g1, w_in, tm)
    dw_in = jnp.concatenate([dwa, dwb], axis=1)
    return grad_x, dw_in, dw_out, dw_up, dw_down, (small_mid, small_mix, dsink, small_in)


def _by_dest(a, rows_major):
    r, c = a.shape
    if rows_major:
        return a.reshape(N_CHIPS, 2, r // N_DEV, c).transpose(1, 0, 2, 3)
    return a.reshape(r, N_CHIPS, 2, c // N_DEV).transpose(2, 1, 0, 3)


def _pack_small(small_mid, small_mix, dsink, small_in):
    z = lambda n: jnp.zeros((1, n), F32)
    rows = [
        small_mid[ROW_LOSS:ROW_LOSS + 1],
        small_in[0:1],
        small_mid[ROW_G2:ROW_G2 + 1],
        small_mid[ROW_G3:ROW_G3 + 1],
        small_mid[ROW_G4:ROW_G4 + 1],
        jnp.concatenate([small_mix[ROW_GATTN:ROW_GATTN + 1], small_mix[ROW_GCONV:ROW_GCONV + 1]], axis=1),
        jnp.concatenate([small_mix[ROW_CW0:ROW_CW0 + 1], small_mix[ROW_CW0 + 1:ROW_CW0 + 2]], axis=1),
        jnp.concatenate([small_mix[ROW_CW0 + 2:ROW_CW0 + 3], dsink[0:1, :], z(D_MODEL - CONV_W - 128)], axis=1),
    ]
    return jnp.concatenate(rows, axis=0)


def kernel(x, pre_mix_norm, w_in, conv_w, attn_sinks, attn_group_norm, conv_group_norm, w_out, post_mix_norm, pre_mlp_norm, w_up, w_down, post_mlp_norm, loss_target, m_pre_mix_norm, m_w_in, m_conv_w, m_attn_sinks, m_attn_group_norm, m_conv_group_norm, m_w_out, m_post_mix_norm, m_pre_mlp_norm, m_w_up, m_w_down, m_post_mlp_norm, v_pre_mix_norm, v_w_in, v_conv_w, v_attn_sinks, v_attn_group_norm, v_conv_group_norm, v_w_out, v_post_mix_norm, v_pre_mlp_norm, v_w_up, v_w_down, v_post_mlp_norm):
    xi, yi, ci = _mesh_pos()
    chip = 2 * xi + yi
    dev = 2 * chip + ci

    order = _peer_order(dev)
    pos = jnp.stack([ci, chip]).astype(jnp.int32)

    gw_in, gconv = _all_gather([w_in[0].astype(BF16), conv_w[0]], "gather_w_in")
    w_in_full = gw_in.transpose(1, 0, 2).reshape(D_MODEL, IN_COLS)
    conv_full = gconv.transpose(1, 0, 2).reshape(3, CONV_W)
    shards = [w_out[0].astype(BF16), w_up[0].astype(BF16), w_down[0].astype(BF16)]

    grad_x, dw_in, dw_out, dw_up, dw_down, smalls = _local_grads(
        x[0], loss_target[0], pre_mix_norm, w_in_full, conv_full, attn_sinks, attn_group_norm, conv_group_norm,
        post_mix_norm, pre_mlp_norm, post_mlp_norm, shards, order)

    grads = [_by_dest(dw_in, False), _by_dest(dw_out, True)]
    from_sibling = _sibling_exchange(grads, "reduce_sibling")
    summed = [_pair_sum(g, r, pos, ADAM_ROWS) for g, r in zip(grads, from_sibling)]
    from_chips = _chip_exchange(summed, "reduce_chips")

    small = _sum_devices(_all_gather([_pack_small(*smalls)], "gather_small")[0])
    loss = (0.5 / D_MODEL) * jnp.sum(small[0])

    big = {}
    for name, w, m, v, g, rs, rc in zip(("w_in", "w_out"), (w_in, w_out), (m_w_in, m_w_out), (v_w_in, v_w_out), grads,
                                        from_sibling, from_chips):
        big[name] = [a[None] for a in _adamw_shard(w[0], m[0], v[0], g, rs, rc, pos, ADAM_ROWS)]
    for name, w, m, v, (g, recv) in zip(("w_up", "w_down"), (w_up, w_down), (m_w_up, m_w_down), (v_w_up, v_w_down),
                                        (dw_up, dw_down)):
        big[name] = [a[None] for a in _adamw_scattered(w[0], m[0], v[0], g, recv, order, ADAM_ROWS)]

    conv_g = lax.dynamic_slice(
        jnp.stack([small[6, :CONV_W], small[6, CONV_W:], small[7, :CONV_W]]), (0, dev * (CONV_W // N_DEV)),
        (3, CONV_W // N_DEV))
    pad = lambda a, n: jnp.pad(a.reshape(1, -1), ((0, 0), (0, n - a.size)))
    small_names = ("pre_mix_norm", "post_mix_norm", "pre_mlp_norm", "post_mlp_norm")
    small_w = {"pre_mix_norm": (pre_mix_norm, m_pre_mix_norm, v_pre_mix_norm),
               "post_mix_norm": (post_mix_norm, m_post_mix_norm, v_post_mix_norm),
               "pre_mlp_norm": (pre_mlp_norm, m_pre_mlp_norm, v_pre_mlp_norm),
               "post_mlp_norm": (post_mlp_norm, m_post_mlp_norm, v_post_mlp_norm)}

    def pack(k):
        rows = [small_w[nm][k] for nm in small_names]
        rows.append(jnp.concatenate([(attn_group_norm, m_attn_group_norm, v_attn_group_norm)[k],
                                     (conv_group_norm, m_conv_group_norm, v_conv_group_norm)[k]], axis=1))
        rows.append(pad((conv_w, m_conv_w, v_conv_w)[k], D_MODEL))
        rows.append(pad((attn_sinks, m_attn_sinks, v_attn_sinks)[k], D_MODEL))
        rows.append(jnp.zeros((1, D_MODEL), F32))
        return jnp.concatenate(rows, axis=0)

    g_small = jnp.concatenate(
        [small[1:6], pad(conv_g, D_MODEL), pad(small[7, CONV_W:CONV_W + N_HEADS], D_MODEL), jnp.zeros((1, D_MODEL), F32)],
        axis=0)
    d_small, nm_small, nv_small = _adamw_small(pack(0), g_small, pack(1), pack(2))

    def unpack(a):
        nconv = 3 * CONV_W // N_DEV
        return {"pre_mix_norm": a[0:1], "post_mix_norm": a[1:2], "pre_mlp_norm": a[2:3], "post_mlp_norm": a[3:4],
                "attn_group_norm": a[4:5, :ATTN_W], "conv_group_norm": a[4:5, ATTN_W:],
                "conv_w": a[5, :nconv].reshape(1, 3, CONV_W // N_DEV), "attn_sinks": a[6:7, :N_HEADS]}

    order = ("pre_mix_norm", "w_in", "conv_w", "attn_sinks", "attn_group_norm", "conv_group_norm", "w_out",
             "post_mix_norm", "pre_mlp_norm", "w_up", "w_down", "post_mlp_norm")
    outs = []
    for k, a in enumerate((g_small, d_small, nm_small, nv_small)):
        sm = unpack(a)
        outs += [big[nm][k] if nm in big else sm[nm] for nm in order]
    return (loss, grad_x[None], *outs)
```

```python
import functools

import jax
import jax.numpy as jnp
import numpy as np
from jax import lax
from jax.experimental import pallas as pl
from jax.experimental.pallas import tpu as pltpu

F32 = jnp.float32
BF16 = jnp.bfloat16

D_MODEL = 1024
HEAD_DIM = 64
ATTN_W = 512
CONV_W = 512
N_HEADS = 8
N_KV = 2
GROUP = 4
KV_W = 128
QKV_W = ATTN_W + 2 * KV_W
GATES_W = 3 * CONV_W
IN_COLS = QKV_W + GATES_W
D_FF = 4096
FF_CHUNK = 512
N_FF_CHUNKS = D_FF // FF_CHUNK
BLOCK = 128
ROT_HALF = 8
ROPE_THETA = 500000.0
NORM_EPS = 1e-6
NEG_INF = -1e30
ATTN_SCALE = 0.125
N_DEV = 8
N_CHIPS = 4
IN_SHARD = IN_COLS // N_DEV

ADAM_LR = 0.001
ADAM_B1 = 0.9
ADAM_B2 = 0.999
ADAM_EPS = 1e-08
ADAM_WD = 0.01
ADAM_STEP = 10

V7X_VMEM_BYTES = 64 * 1024 * 1024
VMEM_LIMIT = V7X_VMEM_BYTES - 2 * 1024 * 1024

MESH = pl.DeviceIdType.MESH
HBM_SPEC = pl.BlockSpec(memory_space=pltpu.HBM)


def _params(*sem):
    return pltpu.CompilerParams(dimension_semantics=sem, vmem_limit_bytes=VMEM_LIMIT)


def _mm(a, b):
    return jnp.dot(a, b, preferred_element_type=F32)


def _mm_nt(a, b):
    return lax.dot_general(a, b, (((1,), (1,)), ((), ())), preferred_element_type=F32)


def _mm_tn(a, b):
    return lax.dot_general(a, b, (((0,), (0,)), ((), ())), preferred_element_type=F32)


def _inv_rms(x):
    return lax.rsqrt(jnp.mean(x * x, axis=-1, keepdims=True) + NORM_EPS)


def _rms_bwd(xhat, r, gain, dy):
    gy = dy * gain
    return r * (gy - xhat * jnp.mean(gy * xhat, axis=-1, keepdims=True)), dy * xhat


def _colsum(a):
    return jnp.sum(a, axis=0, keepdims=True)


def _full(shape):
    zeros = (0,) * len(shape)
    return pl.BlockSpec(shape, lambda *_: zeros)


def _resident(shape):
    zeros = (0,) * len(shape)
    return pl.BlockSpec(shape, lambda *_: zeros, pipeline_mode=pl.Buffered(1))


def _rope_tables(t):
    pos = np.arange(t, dtype=np.float32)
    inv_freq = (ROPE_THETA ** (-np.arange(0, 2 * ROT_HALF, 2, dtype=np.float64) / (2 * ROT_HALF))).astype(np.float32)
    ang = (pos[:, None] * inv_freq[None, :]).astype(np.float64)
    cos, sin = np.cos(ang).astype(np.float32), np.sin(ang).astype(np.float32)
    zeros8 = np.zeros((t, ROT_HALF), np.float32)
    rest = np.zeros((t, HEAD_DIM - 2 * ROT_HALF), np.float32)
    c_head = np.concatenate([cos, cos, rest + 1.0], axis=1)
    s1_head = np.concatenate([zeros8, sin, rest], axis=1)
    s2_head = np.concatenate([-sin, zeros8, rest], axis=1)
    two = lambda a: jnp.asarray(np.concatenate([a, a], axis=1))
    return two(c_head), two(s1_head), two(s2_head)


def _rope(v, c, s1, s2):
    return v * c + pltpu.roll(v, ROT_HALF, 1) * s1 + pltpu.roll(v, 128 - ROT_HALF, 1) * s2


def _rope_transpose(dv, c, s1, s2):
    return dv * c + pltpu.roll(dv * s1, 128 - ROT_HALF, 1) + pltpu.roll(dv * s2, ROT_HALF, 1)


def _shift_rows_down(u, prev, k):
    row = lax.broadcasted_iota(jnp.int32, u.shape, 0)
    out = pltpu.roll(u, k, 0)
    for r in range(k):
        out = jnp.where(row == r, prev[8 - k + r:8 - k + r + 1, :], out)
    return out


def _shift_rows_up(u, nxt, k):
    n = u.shape[0]
    row = lax.broadcasted_iota(jnp.int32, u.shape, 0)
    out = pltpu.roll(u, n - k, 0)
    for r in range(k):
        out = jnp.where(row == n - k + r, nxt[r:r + 1, :], out)
    return out


def _conv3(u, u1, u2, w):
    return (w[0:1, :] * u2 + w[1:2, :] * u1) + w[2:3, :] * u


def _mesh_pos():
    return lax.axis_index("x"), lax.axis_index("y"), lax.axis_index("c")


def _slot(ref, pos):
    return ref.at[4 * pos[0] + 2 * pos[1] + pos[2]]


def _push(src, dst, sems, k, to):
    send_sems, recv_sems = sems
    return pltpu.make_async_remote_copy(src_ref=src, dst_ref=dst, send_sem=send_sems.at[k], recv_sem=recv_sems.at[k],
                                        device_id=to, device_id_type=MESH)


def _gather_near(first, last, shards, outs, sems, local_sems):
    x, y, c = _mesh_pos()
    me, peers = (x, y, c), [(x, y, 1 - c), (1 - x, y, c), (x, 1 - y, c)]
    n = len(shards)
    local = [pltpu.make_async_copy(shards[i], _slot(outs[i], me), local_sems.at[i]) for i in range(n)]
    sends = [_push(shards[i], _slot(outs[i], me), sems, 3 * i + k, peers[k]) for i in range(n) for k in range(3)]
    arrivals = [_push(shards[i], _slot(outs[i], peers[k]), sems, 3 * i + k, peers[k]) for i in range(n) for k in range(3)]

    @pl.when(first)
    def _():
        for cp in local + sends:
            cp.start()

    @pl.when(last)
    def _():
        for cp in sends:
            cp.wait_send()
        for cp in arrivals:
            cp.wait_recv()
        for cp in local:
            cp.wait()


def _gather_far(first, last, shards, ins, outs, sems):
    x, y, c = _mesh_pos()
    me, sibling = (x, y, c), (x, y, 1 - c)
    chips = [(1 - x, y), (x, 1 - y), (1 - x, 1 - y)]
    n = len(shards)
    diag_send = [_push(shards[i], _slot(outs[i], me), sems, 4 * i, (*chips[2], c)) for i in range(n)]
    diag_arrival = [_push(shards[i], _slot(outs[i], (*chips[2], c)), sems, 4 * i, (*chips[2], c)) for i in range(n)]
    passed = [[_push(_slot(ins[i], (*chips[j], c)), _slot(outs[i], (*chips[j], c)), sems, 4 * i + 1 + j, sibling)
               for i in range(n)] for j in range(3)]
    from_sibling = [_push(shards[i], _slot(outs[i], (*chips[j], 1 - c)), sems, 4 * i + 1 + j, sibling)
                    for i in range(n) for j in range(3)]

    @pl.when(first)
    def _():
        for cp in diag_send + passed[0] + passed[1]:
            cp.start()

    @pl.when(last)
    def _():
        for cp in diag_arrival:
            cp.wait_recv()
        for cp in passed[2]:
            cp.start()
        for cp in from_sibling:
            cp.wait_recv()
        for cp in diag_send + passed[0] + passed[1] + passed[2]:
            cp.wait_send()


def _in_proj_fwd(x, g1, w_in, conv_w, g_conv, rope, tm, shards):
    t = x.shape[0]
    rc, rs1, rs2 = rope
    n = len(shards)

    def body(*refs):
        x_ref, g1_ref, w_ref, cw_ref, gc_ref, c_ref, s1_ref, s2_ref = refs[:8]
        shard_refs = refs[8:8 + n]
        qkv_ref, gates_ref, mconv_ref = refs[8 + n:11 + n]
        gathered = refs[11 + n:11 + 2 * n]
        carry_ref = refs[11 + 2 * n]
        if n:
            step = pl.program_id(0)
            _gather_near(step == 0, step == pl.num_programs(0) - 1, shard_refs, gathered, refs[12 + 2 * n:14 + 2 * n],
                         refs[14 + 2 * n])

        @pl.when(pl.program_id(0) == 0)
        def _():
            carry_ref[...] = jnp.zeros_like(carry_ref)

        xv = x_ref[...]
        hn = ((xv * _inv_rms(xv)) * g1_ref[...]).astype(BF16)
        proj = _mm(hn, w_ref[...])
        c, s1, s2 = c_ref[...], s1_ref[...], s2_ref[...]
        for ci in range((ATTN_W + KV_W) // 128):
            sl = slice(128 * ci, 128 * (ci + 1))
            qkv_ref[:, sl] = _rope(proj[:, sl], c, s1, s2).astype(BF16)
        qkv_ref[:, ATTN_W + KV_W:QKV_W] = proj[:, ATTN_W + KV_W:QKV_W].astype(BF16)
        gates = proj[:, QKV_W:]
        gates_ref[...] = gates
        gb, gcc, xin = gates[:, :CONV_W], gates[:, CONV_W:2 * CONV_W], gates[:, 2 * CONV_W:]
        u = gcc * xin
        prev = carry_ref[...]
        conv = gb * _conv3(u, _shift_rows_down(u, prev, 1), _shift_rows_down(u, prev, 2), cw_ref[...])
        carry_ref[...] = u[tm - 8:tm, :]
        mconv_ref[...] = ((conv * _inv_rms(conv)) * gc_ref[...]).astype(BF16)

    tile = lambda w_: pl.BlockSpec((tm, w_), lambda i: (i, 0))
    comm_scratch = [pltpu.SemaphoreType.DMA((3 * n,)), pltpu.SemaphoreType.DMA((3 * n,)), pltpu.SemaphoreType.DMA((n,))]
    res = pl.pallas_call(
        body, name="in_proj_fwd", grid=(t // tm,),
        in_specs=[tile(D_MODEL), _full((1, D_MODEL)), _full((D_MODEL, IN_COLS)), _full((3, CONV_W)), _full((1, CONV_W)),
                  tile(128), tile(128), tile(128)] + [HBM_SPEC] * n,
        out_specs=[tile(QKV_W), tile(GATES_W), tile(CONV_W)] + [HBM_SPEC] * n,
        out_shape=[jax.ShapeDtypeStruct((t, QKV_W), BF16), jax.ShapeDtypeStruct((t, GATES_W), F32),
                   jax.ShapeDtypeStruct((t, CONV_W), BF16)]
        + [jax.ShapeDtypeStruct((N_DEV,) + s.shape, s.dtype) for s in shards],
        scratch_shapes=[pltpu.VMEM((8, CONV_W), F32)] + (comm_scratch if n else []),
        compiler_params=_params("arbitrary"),
    )(x, g1, w_in, conv_w, g_conv, rc, rs1, rs2, *shards)
    return res[0], res[1], res[2], list(res[3:])


GROUP_ROWS = GROUP * BLOCK


def _attn_mask(has_prev):
    row = lax.broadcasted_iota(jnp.int32, (GROUP_ROWS, 2 * BLOCK), 0) & (BLOCK - 1)
    col = lax.broadcasted_iota(jnp.int32, (GROUP_ROWS, 2 * BLOCK), 1)
    band = (col > row) & (col <= row + BLOCK)
    return band if has_prev is True else band & ((col >= BLOCK) | has_prev)


def _stack_heads(a, g):
    return jnp.concatenate([a[:, HEAD_DIM * (GROUP * g + hh):HEAD_DIM * (GROUP * g + hh + 1)] for hh in range(GROUP)], axis=0)


def _unstack_heads(a):
    return jnp.concatenate([a[BLOCK * hh:BLOCK * (hh + 1), :] for hh in range(GROUP)], axis=1)


def _group_sinks(sink_ref, g):
    head = lax.broadcasted_iota(jnp.int32, (GROUP_ROWS, 1), 0) // BLOCK
    out = jnp.full((GROUP_ROWS, 1), sink_ref[0, GROUP * g], F32)
    for hh in range(1, GROUP):
        out = jnp.where(head == hh, sink_ref[0, GROUP * g + hh], out)
    return out


def _attn_probs(qs, kk, sink, valid):
    s = jnp.where(valid, _mm_nt(qs, kk) * ATTN_SCALE, NEG_INF)
    m = jnp.maximum(jnp.max(s, axis=-1, keepdims=True), sink)
    p = jnp.exp(s - m)
    psink = jnp.exp(sink - m)
    inv_l = 1.0 / (jnp.sum(p, axis=-1, keepdims=True) + psink)
    return p * inv_l, psink * inv_l


ATTN_STEP_BLOCKS = 4
ATTN_STEP = ATTN_STEP_BLOCKS * BLOCK
ATTN_KEYS = ATTN_STEP + BLOCK


def _qkv_specs(order):
    prev = lambda i: jnp.maximum(ATTN_STEP_BLOCKS * order(i) - 1, 0)
    kcol, vcol = ATTN_W // KV_W, ATTN_W // KV_W + 1
    return [pl.BlockSpec((ATTN_STEP, ATTN_W), lambda i: (order(i), 0)),
            pl.BlockSpec((BLOCK, KV_W), lambda i: (prev(i), kcol)), pl.BlockSpec((ATTN_STEP, KV_W), lambda i: (order(i), kcol)),
            pl.BlockSpec((BLOCK, KV_W), lambda i: (prev(i), vcol)), pl.BlockSpec((ATTN_STEP, KV_W), lambda i: (order(i), vcol))]


def _attn_fwd(qkv, sinks, g_attn, shards, gathered):
    t = qkv.shape[0]
    n = len(shards)

    def body(*refs):
        sink_ref, q_ref, kp_ref, kc_ref, vp_ref, vc_ref, ga_ref = refs[:7]
        attn_ref, mattn_ref = refs[7 + 2 * n:9 + 2 * n]
        step = pl.program_id(0)
        if n:
            _gather_far(step == 0, step == pl.num_programs(0) - 1, refs[7:7 + n], refs[7 + n:7 + 2 * n],
                        refs[9 + 2 * n:9 + 3 * n], refs[9 + 3 * n:11 + 3 * n])
        q = q_ref[...]
        keys = jnp.concatenate([kp_ref[...], kc_ref[...]], axis=0)
        vals = jnp.concatenate([vp_ref[...], vc_ref[...]], axis=0)
        sink = [_group_sinks(sink_ref, g) for g in range(N_KV)]
        gain = ga_ref[...]
        for b in range(ATTN_STEP_BLOCKS):
            rows, window = slice(BLOCK * b, BLOCK * (b + 1)), slice(BLOCK * b, BLOCK * (b + 2))
            valid = _attn_mask(True if b else step > 0)
            outs = []
            for g in range(N_KV):
                gs = slice(HEAD_DIM * g, HEAD_DIM * (g + 1))
                probs, _ = _attn_probs(_stack_heads(q[rows], g), keys[window, gs], sink[g], valid)
                outs.append(_unstack_heads(_mm(probs.astype(BF16), vals[window, gs])))
            attn = jnp.concatenate(outs, axis=1)
            attn_ref[rows, :] = attn
            mattn_ref[rows, :] = ((attn * _inv_rms(attn)) * gain).astype(BF16)

    blk = pl.BlockSpec((ATTN_STEP, ATTN_W), lambda j: (j, 0))
    res = pl.pallas_call(
        body, name="attn_fwd", grid=(t // ATTN_STEP,),
        in_specs=[pl.BlockSpec(memory_space=pltpu.SMEM)] + _qkv_specs(lambda j: j) + [_full((1, ATTN_W))]
        + [HBM_SPEC] * (2 * n),
        out_specs=[blk, blk] + [HBM_SPEC] * n,
        out_shape=[jax.ShapeDtypeStruct((t, ATTN_W), F32), jax.ShapeDtypeStruct((t, ATTN_W), BF16)]
        + [jax.ShapeDtypeStruct(g.shape, g.dtype) for g in gathered],
        input_output_aliases={7 + n + i: 2 + i for i in range(n)},
        scratch_shapes=[pltpu.SemaphoreType.DMA((4 * n,)), pltpu.SemaphoreType.DMA((4 * n,))] if n else [],
        compiler_params=_params("arbitrary"),
    )(sinks, qkv, qkv, qkv, qkv, qkv, g_attn, *shards, *gathered)
    return res[0], res[1], list(res[2:])


SMALL_ROWS = 8
ROW_LOSS, ROW_G2, ROW_G3, ROW_G4 = 0, 1, 2, 3


def _mid(mattn, mconv, x, target, g2, g3, g4, w_out, w_up, w_down, tm):
    t = x.shape[0]

    def body(ma_ref, mc_ref, x_ref, t_ref, g2_ref, g3_ref, g4_ref, wo_ref, wu_ref, wd_ref,
             actt_ref, dup_ref, hn2t_ref, dmo_ref, dmix_ref, dh_ref, dmixed_ref, small_ref, up_ref):
        @pl.when(pl.program_id(0) == 0)
        def _():
            small_ref[...] = jnp.zeros_like(small_ref)

        g2, g3, g4 = g2_ref[...], g3_ref[...], g4_ref[...]
        mix_out = _mm(ma_ref[...], wo_ref[0:ATTN_W, :]) + _mm(mc_ref[...], wo_ref[ATTN_W:, :])
        r2 = _inv_rms(mix_out)
        mo_hat = mix_out * r2
        h = x_ref[...] + mo_hat * g2
        r3 = _inv_rms(h)
        h_hat = h * r3
        hn2 = (h_hat * g3).astype(BF16)
        hn2t_ref[...] = hn2.T
        mlp = jnp.zeros((tm, D_MODEL), F32)
        for j in range(N_FF_CHUNKS):
            up = jnp.maximum(_mm(hn2, wu_ref[j]), 0.0)
            up_ref[j] = up.astype(BF16)
            act = (up * up).astype(BF16)
            actt_ref[FF_CHUNK * j:FF_CHUNK * (j + 1), :] = act.T
            mlp = mlp + _mm(act, wd_ref[j])
        r4 = _inv_rms(mlp)
        ml_hat = mlp * r4
        err = (h + ml_hat * g4) - t_ref[...]
        d_out = err * (1.0 / D_MODEL)
        d_mlp, dg4 = _rms_bwd(ml_hat, r4, g4, d_out)
        dmo = d_mlp.astype(BF16)
        dmo_ref[...] = dmo
        dhn2 = jnp.zeros((tm, D_MODEL), F32)
        for j in range(N_FF_CHUNKS):
            dup = (_mm_nt(dmo, wd_ref[j]) * (2.0 * up_ref[j].astype(F32))).astype(BF16)
            dup_ref[:, FF_CHUNK * j:FF_CHUNK * (j + 1)] = dup
            dhn2 = dhn2 + _mm_nt(dup, wu_ref[j])
        dh_norm, dg3 = _rms_bwd(h_hat, r3, g3, dhn2)
        dh = d_out + dh_norm
        dh_ref[...] = dh
        d_mix, dg2 = _rms_bwd(mo_hat, r2, g2, dh)
        dmix = d_mix.astype(BF16)
        dmix_ref[...] = dmix
        dmixed_ref[...] = _mm_nt(dmix, wo_ref[...])
        small_ref[ROW_LOSS:ROW_LOSS + 1, :] += _colsum(err * err)
        small_ref[ROW_G2:ROW_G2 + 1, :] += _colsum(dg2)
        small_ref[ROW_G3:ROW_G3 + 1, :] += _colsum(dg3)
        small_ref[ROW_G4:ROW_G4 + 1, :] += _colsum(dg4)

    tile = lambda n: pl.BlockSpec((tm, n), lambda i: (i, 0))
    cols = lambda n: pl.BlockSpec((n, tm), lambda i: (0, i))
    gain = _full((1, D_MODEL))
    return pl.pallas_call(
        body, name="mid_fwd_bwd", grid=(t // tm,),
        in_specs=[tile(ATTN_W), tile(CONV_W), tile(D_MODEL), tile(D_MODEL), gain, gain, gain,
                  _resident((D_MODEL, D_MODEL)), _resident((N_FF_CHUNKS, D_MODEL, FF_CHUNK)),
                  _resident((N_FF_CHUNKS, FF_CHUNK, D_MODEL))],
        out_specs=[cols(D_FF), tile(D_FF), cols(D_MODEL), tile(D_MODEL), tile(D_MODEL), tile(D_MODEL), tile(D_MODEL),
                   _full((SMALL_ROWS, D_MODEL))],
        out_shape=[jax.ShapeDtypeStruct((D_FF, t), BF16), jax.ShapeDtypeStruct((t, D_FF), BF16),
                   jax.ShapeDtypeStruct((D_MODEL, t), BF16), jax.ShapeDtypeStruct((t, D_MODEL), BF16),
                   jax.ShapeDtypeStruct((t, D_MODEL), BF16), jax.ShapeDtypeStruct((t, D_MODEL), F32),
                   jax.ShapeDtypeStruct((t, D_MODEL), F32), jax.ShapeDtypeStruct((SMALL_ROWS, D_MODEL), F32)],
        scratch_shapes=[pltpu.VMEM((N_FF_CHUNKS, tm, FF_CHUNK), BF16)],
        compiler_params=_params("arbitrary"),
    )(mattn, mconv, x, target, g2, g3, g4, w_out, w_up, w_down)


PEER_FLIPS = ((1, 1, 0), (1, 0, 0), (0, 1, 0), (1, 1, 1), (1, 0, 1), (0, 1, 1), (0, 0, 1))


def _peer_order(dev):
    masks = [4 * fx + 2 * fy + fc for fx, fy, fc in PEER_FLIPS] + [0]
    return jnp.bitwise_xor(dev, jnp.asarray(masks, jnp.int32)).astype(jnp.int32)


def _dw_scatter(at, b, order, tk, at_chunked, name):
    t = b.shape[0]
    n_k = t // tk
    rows, cols = (FF_CHUNK, D_MODEL) if at_chunked else (D_MODEL, FF_CHUNK)

    def body(order_ref, a_ref, b_ref, o_ref, recv_ref, send_buf, send_sems, recv_sems):
        s_now, k = pl.program_id(0), pl.program_id(1)
        x, y, c = _mesh_pos()
        mine = 4 * x + 2 * y + c

        def peer(s):
            fx, fy, fc = PEER_FLIPS[s]
            return (1 - x if fx else x, 1 - y if fy else y, 1 - c if fc else c)

        def send(s):
            return _push(send_buf.at[s], recv_ref.at[mine], (send_sems, recv_sems), s, peer(s))

        def arrival(s):
            return _push(send_buf.at[s], _slot(recv_ref, peer(s)), (send_sems, recv_sems), s, peer(s))

        @pl.when(k == 0)
        def _():
            o_ref[...] = jnp.zeros_like(o_ref)

        tokens = pl.ds(pl.multiple_of(k * tk, tk), tk)
        if at_chunked:
            o_ref[...] += _mm(a_ref[...], b_ref[tokens, :])
        else:
            o_ref[...] += _mm(a_ref[:, tokens], b_ref[...])
        for s in range(N_DEV - 1):
            @pl.when((s_now == s) & (k == n_k - 1))
            def _():
                send_buf[s] = o_ref[...].astype(BF16)
                send(s).start()

        @pl.when((s_now == N_DEV - 1) & (k == n_k - 1))
        def _():
            for s in range(N_DEV - 1):
                send(s).wait_send()
                arrival(s).wait_recv()

    if at_chunked:
        in_specs = [pl.BlockSpec((FF_CHUNK, tk), lambda s, k, order_ref: (order_ref[s], k)), _resident((t, D_MODEL))]
    else:
        in_specs = [_resident((D_MODEL, t)), pl.BlockSpec((tk, FF_CHUNK), lambda s, k, order_ref: (k, order_ref[s]))]
    return pl.pallas_call(
        body, name=name,
        grid_spec=pltpu.PrefetchScalarGridSpec(
            num_scalar_prefetch=1, grid=(N_DEV, n_k), in_specs=in_specs,
            out_specs=[pl.BlockSpec((None, rows, cols), lambda s, k, order_ref: (order_ref[s], 0, 0)), HBM_SPEC],
            scratch_shapes=[pltpu.VMEM((N_DEV - 1, rows, cols), BF16), pltpu.SemaphoreType.DMA((N_DEV - 1,)),
                            pltpu.SemaphoreType.DMA((N_DEV - 1,))]),
        out_shape=[jax.ShapeDtypeStruct((N_DEV, rows, cols), F32), jax.ShapeDtypeStruct((N_DEV, rows, cols), BF16)],
        compiler_params=_params("arbitrary", "arbitrary"),
    )(order, at, b)


def _dw_out(mattn, mconv, dmix, tk):
    t = dmix.shape[0]

    def body(ma_ref, mc_ref, b_ref, o_ref):
        @pl.when(pl.program_id(0) == 0)
        def _():
            o_ref[...] = jnp.zeros_like(o_ref)
        b = b_ref[...]
        o_ref[0:ATTN_W, :] += _mm_tn(ma_ref[...], b)
        o_ref[ATTN_W:, :] += _mm_tn(mc_ref[...], b)

    tile = lambda n: pl.BlockSpec((tk, n), lambda k: (k, 0))
    return pl.pallas_call(
        body, name="dw_out", grid=(t // tk,),
        in_specs=[tile(ATTN_W), tile(CONV_W), tile(D_MODEL)],
        out_specs=_full((D_MODEL, D_MODEL)),
        out_shape=jax.ShapeDtypeStruct((D_MODEL, D_MODEL), F32),
        compiler_params=_params("arbitrary"),
    )(mattn, mconv, dmix)


ROW_GATTN, ROW_GCONV, ROW_CW0 = 0, 1, 2


def _mix_bwd(dmixed, attn, gates, g_attn, g_conv, conv_w, tm):
    t = attn.shape[0]
    n = t // tm
    rev = lambda i: n - 1 - i

    def body(dm_ref, attn_ref, gates_ref, gprev_ref, ga_ref, gc_ref, cw_ref, dattn_ref, dgates_ref, small_ref, carry_ref):
        i = pl.program_id(0)

        @pl.when(i == 0)
        def _():
            small_ref[...] = jnp.zeros_like(small_ref)
            carry_ref[...] = jnp.zeros_like(carry_ref)

        dm = dm_ref[...]
        a = attn_ref[...]
        ra = _inv_rms(a)
        a_hat = a * ra
        dattn, dga = _rms_bwd(a_hat, ra, ga_ref[...], dm[:, :ATTN_W])
        dattn_ref[...] = dattn

        gates = gates_ref[...]
        gb, gcc, xin = gates[:, :CONV_W], gates[:, CONV_W:2 * CONV_W], gates[:, 2 * CONV_W:]
        u = gcc * xin
        gp = gprev_ref[...]
        uprev = jnp.where(rev(i) == 0, 0.0, gp[:, CONV_W:2 * CONV_W] * gp[:, 2 * CONV_W:])
        u1, u2 = _shift_rows_down(u, uprev, 1), _shift_rows_down(u, uprev, 2)
        w = cw_ref[...]
        c = _conv3(u, u1, u2, w)
        conv = gb * c
        rcv = _inv_rms(conv)
        c_hat = conv * rcv
        dconv, dgc = _rms_bwd(c_hat, rcv, gc_ref[...], dm[:, ATTN_W:])
        dc = dconv * gb
        nxt = carry_ref[...]
        du = (w[2:3, :] * dc + w[1:2, :] * _shift_rows_up(dc, nxt, 1)) + w[0:1, :] * _shift_rows_up(dc, nxt, 2)
        carry_ref[...] = dc[0:8, :]
        dgates_ref[:, :CONV_W] = (dconv * c).astype(BF16)
        dgates_ref[:, CONV_W:2 * CONV_W] = (du * xin).astype(BF16)
        dgates_ref[:, 2 * CONV_W:] = (du * gcc).astype(BF16)
        small_ref[ROW_GATTN:ROW_GATTN + 1, :] += _colsum(dga)
        small_ref[ROW_GCONV:ROW_GCONV + 1, :] += _colsum(dgc)
        small_ref[ROW_CW0:ROW_CW0 + 1, :] += _colsum(dc * u2)
        small_ref[ROW_CW0 + 1:ROW_CW0 + 2, :] += _colsum(dc * u1)
        small_ref[ROW_CW0 + 2:ROW_CW0 + 3, :] += _colsum(dc * u)

    tile = lambda w_: pl.BlockSpec((tm, w_), lambda i: (rev(i), 0))
    prev8 = pl.BlockSpec((8, GATES_W), lambda i: (jnp.maximum(rev(i) * (tm // 8) - 1, 0), 0))
    return pl.pallas_call(
        body, name="mix_bwd", grid=(n,),
        in_specs=[tile(D_MODEL), tile(ATTN_W), tile(GATES_W), prev8, _full((1, ATTN_W)), _full((1, CONV_W)),
                  _full((3, CONV_W))],
        out_specs=[tile(ATTN_W), tile(GATES_W), _full((SMALL_ROWS, CONV_W))],
        out_shape=[jax.ShapeDtypeStruct((t, ATTN_W), F32), jax.ShapeDtypeStruct((t, GATES_W), BF16),
                   jax.ShapeDtypeStruct((SMALL_ROWS, CONV_W), F32)],
        scratch_shapes=[pltpu.VMEM((8, CONV_W), F32)],
        compiler_params=_params("arbitrary"),
    )(dmixed, attn, gates, gates, g_attn, g_conv, conv_w)


def _attn_bwd(qkv, dattn, sinks, rope):
    t = qkv.shape[0]
    n_steps = t // ATTN_STEP
    rev = lambda i: n_steps - 1 - i
    rc, rs1, rs2 = rope

    def body(sink_ref, q_ref, kp_ref, kc_ref, vp_ref, vc_ref, do_ref, c_ref, s1_ref, s2_ref,
             dqkv_ref, dsink_ref, ck_ref, cv_ref, kacc_ref, vacc_ref):
        i = pl.program_id(0)

        @pl.when(i == 0)
        def _():
            dsink_ref[...] = jnp.zeros_like(dsink_ref)
            ck_ref[...] = jnp.zeros_like(ck_ref)
            cv_ref[...] = jnp.zeros_like(cv_ref)

        kacc_ref[...] = jnp.zeros_like(kacc_ref)
        vacc_ref[...] = jnp.zeros_like(vacc_ref)
        q = q_ref[...]
        dout = do_ref[...].astype(BF16)
        keys = jnp.concatenate([kp_ref[...], kc_ref[...]], axis=0)
        vals = jnp.concatenate([vp_ref[...], vc_ref[...]], axis=0)
        sink = [_group_sinks(sink_ref, g) for g in range(N_KV)]
        c, s1, s2 = c_ref[...], s1_ref[...], s2_ref[...]
        lane = lax.broadcasted_iota(jnp.int32, (1, 128), 1)
        dsink = jnp.zeros((1, 128), F32)
        for b in range(ATTN_STEP_BLOCKS):
            rows, window = slice(BLOCK * b, BLOCK * (b + 1)), slice(BLOCK * b, BLOCK * (b + 2))
            valid = _attn_mask(True if b else rev(i) > 0)
            dq_parts, dk_parts, dv_parts = [], [], []
            for g in range(N_KV):
                gs = slice(HEAD_DIM * g, HEAD_DIM * (g + 1))
                kk, vv = keys[window, gs], vals[window, gs]
                qs, dos = _stack_heads(q[rows], g), _stack_heads(dout[rows], g)
                probs, psink = _attn_probs(qs, kk, sink[g], valid)
                dp = _mm_nt(dos, vv)
                delta = jnp.sum(probs * dp, axis=-1, keepdims=True)
                ds = (probs * (dp - delta) * ATTN_SCALE).astype(BF16)
                sink_terms = psink * delta
                for hh in range(GROUP):
                    head_sum = jnp.sum(sink_terms[BLOCK * hh:BLOCK * (hh + 1), :])
                    dsink = dsink + jnp.where(lane == GROUP * g + hh, -head_sum, 0.0)
                dq_parts.append(_unstack_heads(_mm(ds, kk)))
                dk_parts.append(_mm_tn(ds, qs))
                dv_parts.append(_mm_tn(probs.astype(BF16), dos))
            kacc_ref[window, :] += jnp.concatenate(dk_parts, axis=1)
            vacc_ref[window, :] += jnp.concatenate(dv_parts, axis=1)
            dq = jnp.concatenate(dq_parts, axis=1)
            for ci in range(ATTN_W // 128):
                sl = slice(128 * ci, 128 * (ci + 1))
                dqkv_ref[rows, sl] = _rope_transpose(dq[:, sl], c[rows], s1[rows], s2[rows]).astype(BF16)
        kacc_ref[ATTN_STEP:, :] += ck_ref[...]
        vacc_ref[ATTN_STEP:, :] += cv_ref[...]
        ck_ref[...] = kacc_ref[:BLOCK, :]
        cv_ref[...] = vacc_ref[:BLOCK, :]
        dqkv_ref[:, ATTN_W:ATTN_W + KV_W] = _rope_transpose(kacc_ref[BLOCK:, :], c, s1, s2).astype(BF16)
        dqkv_ref[:, ATTN_W + KV_W:] = vacc_ref[BLOCK:, :].astype(BF16)
        dsink_ref[0:1, :] += dsink

    blk = lambda w_: pl.BlockSpec((ATTN_STEP, w_), lambda i: (rev(i), 0))
    return pl.pallas_call(
        body, name="attn_bwd", grid=(n_steps,),
        in_specs=[pl.BlockSpec(memory_space=pltpu.SMEM)] + _qkv_specs(rev) + [blk(ATTN_W), blk(128), blk(128), blk(128)],
        out_specs=[blk(QKV_W), _full((8, 128))],
        out_shape=[jax.ShapeDtypeStruct((t, QKV_W), BF16), jax.ShapeDtypeStruct((8, 128), F32)],
        scratch_shapes=[pltpu.VMEM((BLOCK, KV_W), F32), pltpu.VMEM((BLOCK, KV_W), F32),
                        pltpu.VMEM((ATTN_KEYS, KV_W), F32), pltpu.VMEM((ATTN_KEYS, KV_W), F32)],
        compiler_params=_params("arbitrary"),
    )(sinks, qkv, qkv, qkv, qkv, qkv, dattn, rc, rs1, rs2)


def _in_proj_bwd(dqkv, dgates, x, dh, g1, w_in, tm):
    t = x.shape[0]

    def body(dq_ref, dg_ref, x_ref, dh_ref, g1_ref, w_ref, dx_ref, dwa_ref, dwb_ref, dg1_ref):
        @pl.when(pl.program_id(0) == 0)
        def _():
            dwa_ref[...] = jnp.zeros_like(dwa_ref)
            dwb_ref[...] = jnp.zeros_like(dwb_ref)
            dg1_ref[...] = jnp.zeros_like(dg1_ref)

        dq, dg = dq_ref[...], dg_ref[...]
        dhn = _mm_nt(dq, w_ref[:, :QKV_W]) + _mm_nt(dg, w_ref[:, QKV_W:])
        xv = x_ref[...]
        r = _inv_rms(xv)
        x_hat = xv * r
        g1 = g1_ref[...]
        dx, dg1 = _rms_bwd(x_hat, r, g1, dhn)
        dx_ref[...] = dh_ref[...] + dx
        hn = (x_hat * g1).astype(BF16)
        dwa_ref[...] += _mm_tn(hn, dq)
        dwb_ref[...] += _mm_tn(hn, dg)
        dg1_ref[0:1, :] += _colsum(dg1)

    tile = lambda n: pl.BlockSpec((tm, n), lambda i: (i, 0))
    return pl.pallas_call(
        body, name="in_proj_bwd", grid=(t // tm,),
        in_specs=[tile(QKV_W), tile(GATES_W), tile(D_MODEL), tile(D_MODEL), _full((1, D_MODEL)),
                  _resident((D_MODEL, IN_COLS))],
        out_specs=[tile(D_MODEL), _full((D_MODEL, QKV_W)), _full((D_MODEL, GATES_W)), _full((SMALL_ROWS, D_MODEL))],
        out_shape=[jax.ShapeDtypeStruct((t, D_MODEL), F32), jax.ShapeDtypeStruct((D_MODEL, QKV_W), F32),
                   jax.ShapeDtypeStruct((D_MODEL, GATES_W), F32), jax.ShapeDtypeStruct((SMALL_ROWS, D_MODEL), F32)],
        compiler_params=_params("arbitrary"),
    )(dqkv, dgates, x, dh, g1, w_in)


def _all_gather(shards, name):
    n = len(shards)

    def body(*refs):
        ins, outs = refs[:n], refs[n:2 * n]
        send_sems, recv_sems, local_sems = refs[2 * n:]
        x, y, c = _mesh_pos()
        me, sibling = (x, y, c), (x, y, 1 - c)
        chips = [(1 - x, y), (x, 1 - y), (1 - x, 1 - y)]

        def copy(i, k, block, to, src=None):
            dst = outs[i].at[4 * block[0] + 2 * block[1] + block[2]]
            return pltpu.make_async_remote_copy(
                src_ref=dst if src is None else src, dst_ref=dst, send_sem=send_sems.at[7 * i + k],
                recv_sem=recv_sems.at[7 * i + k], device_id=to, device_id_type=MESH)

        mine = [pltpu.make_async_copy(ins[i], outs[i].at[4 * x + 2 * y + c], local_sems.at[i]) for i in range(n)]
        for cp in mine:
            cp.start()
        first = []
        for i in range(n):
            first.append(copy(i, 0, me, sibling, src=ins[i]))
            first += [copy(i, 1 + j, me, (*chip, c), src=ins[i]) for j, chip in enumerate(chips)]
        for cp in first:
            cp.start()
        passed = []
        for j, chip in enumerate(chips):
            for i in range(n):
                copy(i, 1 + j, (*chip, c), me).wait_recv()
                cp = copy(i, 4 + j, (*chip, c), sibling)
                cp.start()
                passed.append(cp)
        for i in range(n):
            copy(i, 0, sibling, me).wait_recv()
            for j, chip in enumerate(chips):
                copy(i, 4 + j, (*chip, 1 - c), me).wait_recv()
        for cp in first + passed:
            cp.wait_send()
        for cp in mine:
            cp.wait()

    return pl.pallas_call(
        body, name=name,
        in_specs=[HBM_SPEC] * n, out_specs=[HBM_SPEC] * n,
        out_shape=[jax.ShapeDtypeStruct((N_DEV,) + s.shape, s.dtype) for s in shards],
        scratch_shapes=[pltpu.SemaphoreType.DMA((7 * n,)), pltpu.SemaphoreType.DMA((7 * n,)),
                        pltpu.SemaphoreType.DMA((n,))],
    )(*shards)


def _sibling_exchange(grads, name):
    n = len(grads)

    def body(*refs):
        ins, outs = refs[:n], refs[n:2 * n]
        send_sems, recv_sems = refs[2 * n:]
        x, y, c = _mesh_pos()
        copies = [pltpu.make_async_remote_copy(
            src_ref=ins[i].at[1 - c], dst_ref=outs[i], send_sem=send_sems.at[i], recv_sem=recv_sems.at[i],
            device_id=(x, y, 1 - c), device_id_type=MESH) for i in range(n)]
        for cp in copies:
            cp.start()
        for cp in copies:
            cp.wait()

    return pl.pallas_call(
        body, name=name,
        in_specs=[HBM_SPEC] * n, out_specs=[HBM_SPEC] * n,
        out_shape=[jax.ShapeDtypeStruct(g.shape[1:], g.dtype) for g in grads],
        scratch_shapes=[pltpu.SemaphoreType.DMA((n,)), pltpu.SemaphoreType.DMA((n,))],
    )(*grads)


def _chip_exchange(sums, name):
    n = len(sums)

    def body(*refs):
        ins, outs = refs[:n], refs[n:2 * n]
        send_sems, recv_sems = refs[2 * n:]
        x, y, c = _mesh_pos()
        chips = [(1 - x, y), (x, 1 - y), (1 - x, 1 - y)]
        copies = [pltpu.make_async_remote_copy(
            src_ref=ins[i].at[2 * chip[0] + chip[1]], dst_ref=outs[i].at[k], send_sem=send_sems.at[3 * i + k],
            recv_sem=recv_sems.at[3 * i + k], device_id=(*chip, c), device_id_type=MESH)
            for i in range(n) for k, chip in enumerate(chips)]
        for cp in copies:
            cp.start()
        for cp in copies:
            cp.wait()

    return pl.pallas_call(
        body, name=name,
        in_specs=[HBM_SPEC] * n, out_specs=[HBM_SPEC] * n,
        out_shape=[jax.ShapeDtypeStruct((3,) + s.shape[1:], s.dtype) for s in sums],
        scratch_shapes=[pltpu.SemaphoreType.DMA((3 * n,)), pltpu.SemaphoreType.DMA((3 * n,))],
    )(*sums)


def _pair_sum(grad, recv, pos, tr):
    _, _, rows, cols = grad.shape

    def body(pos_ref, g_ref, r_ref, sb_ref):
        sb_ref[...] = (g_ref[...] + r_ref[...]).astype(BF16)

    return pl.pallas_call(
        body, name="pair_sum",
        grid_spec=pltpu.PrefetchScalarGridSpec(
            num_scalar_prefetch=1, grid=(N_CHIPS, rows // tr),
            in_specs=[pl.BlockSpec((None, None, tr, cols), lambda p, i, pos: (pos[0], p, i, 0)),
                      pl.BlockSpec((None, tr, cols), lambda p, i, pos: (p, i, 0))],
            out_specs=pl.BlockSpec((None, tr, cols), lambda p, i, pos: (p, i, 0))),
        out_shape=jax.ShapeDtypeStruct((N_CHIPS, rows, cols), BF16),
        compiler_params=_params("parallel", "parallel"),
    )(pos, grad, recv)


def _adam_math(w, g, m, v):
    m = ADAM_B1 * m + (1.0 - ADAM_B1) * g
    v = ADAM_B2 * v + (1.0 - ADAM_B2) * (g * g)
    m_hat = m / (1.0 - ADAM_B1 ** ADAM_STEP)
    v_hat = v / (1.0 - ADAM_B2 ** ADAM_STEP)
    delta = -ADAM_LR * (m_hat / (jnp.sqrt(v_hat) + ADAM_EPS) + ADAM_WD * w)
    return delta, m, v


def _adamw_shard(w, m, v, grad, from_sibling, from_chips, pos, tr):
    rows, cols = w.shape

    def body(pos_ref, w_ref, m_ref, v_ref, own_ref, sib_ref, r_ref, g_ref, d_ref, nm_ref, nv_ref):
        g = own_ref[...] + sib_ref[...]
        for k in range(3):
            g = g + r_ref[k].astype(F32)
        g_ref[...] = g
        d_ref[...], nm_ref[...], nv_ref[...] = _adam_math(w_ref[...], g, m_ref[...], v_ref[...])

    tile = pl.BlockSpec((tr, cols), lambda i, pos: (i, 0))
    out = jax.ShapeDtypeStruct((rows, cols), F32)
    return pl.pallas_call(
        body, name="adamw_shard",
        grid_spec=pltpu.PrefetchScalarGridSpec(
            num_scalar_prefetch=1, grid=(rows // tr,),
            in_specs=[tile, tile, tile,
                      pl.BlockSpec((None, None, tr, cols), lambda i, pos: (pos[0], pos[1], i, 0)),
                      pl.BlockSpec((None, tr, cols), lambda i, pos: (pos[1], i, 0)),
                      pl.BlockSpec((3, tr, cols), lambda i, pos: (0, i, 0))],
            out_specs=[tile] * 4),
        out_shape=[out] * 4,
        compiler_params=_params("parallel"),
    )(pos, w, m, v, grad, from_sibling, from_chips)


def _adamw_scattered(w, m, v, grad, recv, order, tr):
    rows, cols = w.shape
    n_peers = N_DEV - 1

    def body(order_ref, w_ref, m_ref, v_ref, own_ref, *rest):
        peers, (g_ref, d_ref, nm_ref, nv_ref) = rest[:n_peers], rest[n_peers:]
        g = own_ref[...]
        for r_ref in peers:
            g = g + r_ref[...].astype(F32)
        g_ref[...] = g
        d_ref[...], nm_ref[...], nv_ref[...] = _adam_math(w_ref[...], g, m_ref[...], v_ref[...])

    tile = pl.BlockSpec((tr, cols), lambda i, order_ref: (i, 0))
    slot = lambda s: pl.BlockSpec((None, tr, cols), lambda i, order_ref: (order_ref[s], i, 0))
    out = jax.ShapeDtypeStruct((rows, cols), F32)
    return pl.pallas_call(
        body, name="adamw_scattered",
        grid_spec=pltpu.PrefetchScalarGridSpec(
            num_scalar_prefetch=1, grid=(rows // tr,),
            in_specs=[tile, tile, tile, slot(n_peers)] + [slot(s) for s in range(n_peers)],
            out_specs=[tile] * 4),
        out_shape=[out] * 4,
        compiler_params=_params("parallel"),
    )(order, w, m, v, grad, *([recv] * n_peers))


def _sum_devices(gathered):
    _, rows, cols = gathered.shape

    def body(g_ref, o_ref):
        s = g_ref[0]
        for d in range(1, N_DEV):
            s = s + g_ref[d]
        o_ref[...] = s

    return pl.pallas_call(
        body, name="sum_devices", in_specs=[_full(gathered.shape)], out_specs=_full((rows, cols)), grid=(1,),
        out_shape=jax.ShapeDtypeStruct((rows, cols), F32),
    )(gathered)


def _adamw_small(w, g, m, v):
    def body(w_ref, g_ref, m_ref, v_ref, d_ref, nm_ref, nv_ref):
        d_ref[...], nm_ref[...], nv_ref[...] = _adam_math(w_ref[...], g_ref[...], m_ref[...], v_ref[...])

    spec = _full(w.shape)
    out = jax.ShapeDtypeStruct(w.shape, F32)
    return pl.pallas_call(
        body, name="adamw_small", grid=(1,), in_specs=[spec] * 4, out_specs=[spec] * 3, out_shape=[out] * 3,
    )(w, g, m, v)


TOKEN_TILE = 512
MID_TILE = 256
DW_TILE = 1024
DW_SCATTER_TILE = 8192
ADAM_ROWS = 128


def _local_grads(x, target, g1, w_in, conv_w, sinks, g_attn, g_conv, g2, g3, g4, shards, order):
    t = x.shape[0]
    tm = min(TOKEN_TILE, t)
    rope = _rope_tables(t)
    qkv, gates, mconv, gathered = _in_proj_fwd(x, g1, w_in, conv_w, g_conv, rope, tm, shards)
    attn, mattn, (w_out, w_up, w_down) = _attn_fwd(qkv, sinks, g_attn, shards, gathered)
    actt, dup, hn2t, dmo, dmix, dh, dmixed, small_mid = _mid(
        mattn, mconv, x, target, g2, g3, g4, w_out.reshape(D_MODEL, D_MODEL), w_up, w_down, min(MID_TILE, t))
    tk = min(DW_TILE, t)
    dw_up = _dw_scatter(hn2t, dup, order, min(DW_SCATTER_TILE, t), False, "dw_up")
    dw_down = _dw_scatter(actt, dmo, order, min(DW_SCATTER_TILE, t), True, "dw_down")
    dw_out = _dw_out(mattn, mconv, dmix, tk)
    dattn, dgates, small_mix = _mix_bwd(dmixed, attn, gates, g_attn, g_conv, conv_w, tm)
    dqkv, dsink = _attn_bwd(qkv, dattn, sinks, rope)
    grad_x, dwa, dwb, small_in = _in_proj_bwd(dqkv, dgates, x, dh, g1, w_in, tm)
    dw_in = jnp.concatenate([dwa, dwb], axis=1)
    return grad_x, dw_in, dw_out, dw_up, dw_down, (small_mid, small_mix, dsink, small_in)


def _by_dest(a, rows_major):
    r, c = a.shape
    if rows_major:
        return a.reshape(N_CHIPS, 2, r // N_DEV, c).transpose(1, 0, 2, 3)
    return a.reshape(r, N_CHIPS, 2, c // N_DEV).transpose(2, 1, 0, 3)


def _pack_small(small_mid, small_mix, dsink, small_in):
    z = lambda n: jnp.zeros((1, n), F32)
    rows = [
        small_mid[ROW_LOSS:ROW_LOSS + 1],
        small_in[0:1],
        small_mid[ROW_G2:ROW_G2 + 1],
        small_mid[ROW_G3:ROW_G3 + 1],
        small_mid[ROW_G4:ROW_G4 + 1],
        jnp.concatenate([small_mix[ROW_GATTN:ROW_GATTN + 1], small_mix[ROW_GCONV:ROW_GCONV + 1]], axis=1),
        jnp.concatenate([small_mix[ROW_CW0:ROW_CW0 + 1], small_mix[ROW_CW0 + 1:ROW_CW0 + 2]], axis=1),
        jnp.concatenate([small_mix[ROW_CW0 + 2:ROW_CW0 + 3], dsink[0:1, :], z(D_MODEL - CONV_W - 128)], axis=1),
    ]
    return jnp.concatenate(rows, axis=0)


def kernel(x, pre_mix_norm, w_in, conv_w, attn_sinks, attn_group_norm, conv_group_norm, w_out, post_mix_norm, pre_mlp_norm, w_up, w_down, post_mlp_norm, loss_target, m_pre_mix_norm, m_w_in, m_conv_w, m_attn_sinks, m_attn_group_norm, m_conv_group_norm, m_w_out, m_post_mix_norm, m_pre_mlp_norm, m_w_up, m_w_down, m_post_mlp_norm, v_pre_mix_norm, v_w_in, v_conv_w, v_attn_sinks, v_attn_group_norm, v_conv_group_norm, v_w_out, v_post_mix_norm, v_pre_mlp_norm, v_w_up, v_w_down, v_post_mlp_norm):
    xi, yi, ci = _mesh_pos()
    chip = 2 * xi + yi
    dev = 2 * chip + ci

    order = _peer_order(dev)
    pos = jnp.stack([ci, chip]).astype(jnp.int32)

    gw_in, gconv = _all_gather([w_in[0].astype(BF16), conv_w[0]], "gather_w_in")
    w_in_full = gw_in.transpose(1, 0, 2).reshape(D_MODEL, IN_COLS)
    conv_full = gconv.transpose(1, 0, 2).reshape(3, CONV_W)
    shards = [w_out[0].astype(BF16), w_up[0].astype(BF16), w_down[0].astype(BF16)]

    grad_x, dw_in, dw_out, dw_up, dw_down, smalls = _local_grads(
        x[0], loss_target[0], pre_mix_norm, w_in_full, conv_full, attn_sinks, attn_group_norm, conv_group_norm,
        post_mix_norm, pre_mlp_norm, post_mlp_norm, shards, order)

    grads = [_by_dest(dw_in, False), _by_dest(dw_out, True)]
    from_sibling = _sibling_exchange(grads, "reduce_sibling")
    summed = [_pair_sum(g, r, pos, ADAM_ROWS) for g, r in zip(grads, from_sibling)]
    from_chips = _chip_exchange(summed, "reduce_chips")

    small = _sum_devices(_all_gather([_pack_small(*smalls)], "gather_small")[0])
    loss = (0.5 / D_MODEL) * jnp.sum(small[0])

    big = {}
    for name, w, m, v, g, rs, rc in zip(("w_in", "w_out"), (w_in, w_out), (m_w_in, m_w_out), (v_w_in, v_w_out), grads,
                                        from_sibling, from_chips):
        big[name] = [a[None] for a in _adamw_shard(w[0], m[0], v[0], g, rs, rc, pos, ADAM_ROWS)]
    for name, w, m, v, (g, recv) in zip(("w_up", "w_down"), (w_up, w_down), (m_w_up, m_w_down), (v_w_up, v_w_down),
                                        (dw_up, dw_down)):
        big[name] = [a[None] for a in _adamw_scattered(w[0], m[0], v[0], g, recv, order, ADAM_ROWS)]

    conv_g = lax.dynamic_slice(
        jnp.stack([small[6, :CONV_W], small[6, CONV_W:], small[7, :CONV_W]]), (0, dev * (CONV_W // N_DEV)),
        (3, CONV_W // N_DEV))
    pad = lambda a, n: jnp.pad(a.reshape(1, -1), ((0, 0), (0, n - a.size)))
    small_names = ("pre_mix_norm", "post_mix_norm", "pre_mlp_norm", "post_mlp_norm")
    small_w = {"pre_mix_norm": (pre_mix_norm, m_pre_mix_norm, v_pre_mix_norm),
               "post_mix_norm": (post_mix_norm, m_post_mix_norm, v_post_mix_norm),
               "pre_mlp_norm": (pre_mlp_norm, m_pre_mlp_norm, v_pre_mlp_norm),
               "post_mlp_norm": (post_mlp_norm, m_post_mlp_norm, v_post_mlp_norm)}

    def pack(k):
        rows = [small_w[nm][k] for nm in small_names]
        rows.append(jnp.concatenate([(attn_group_norm, m_attn_group_norm, v_attn_group_norm)[k],
                                     (conv_group_norm, m_conv_group_norm, v_conv_group_norm)[k]], axis=1))
        rows.append(pad((conv_w, m_conv_w, v_conv_w)[k], D_MODEL))
        rows.append(pad((attn_sinks, m_attn_sinks, v_attn_sinks)[k], D_MODEL))
        rows.append(jnp.zeros((1, D_MODEL), F32))
        return jnp.concatenate(rows, axis=0)

    g_small = jnp.concatenate(
        [small[1:6], pad(conv_g, D_MODEL), pad(small[7, CONV_W:CONV_W + N_HEADS], D_MODEL), jnp.zeros((1, D_MODEL), F32)],
        axis=0)
    d_small, nm_small, nv_small = _adamw_small(pack(0), g_small, pack(1), pack(2))

    def unpack(a):
        nconv = 3 * CONV_W // N_DEV
        return {"pre_mix_norm": a[0:1], "post_mix_norm": a[1:2], "pre_mlp_norm": a[2:3], "post_mlp_norm": a[3:4],
                "attn_group_norm": a[4:5, :ATTN_W], "conv_group_norm": a[4:5, ATTN_W:],
                "conv_w": a[5, :nconv].reshape(1, 3, CONV_W // N_DEV), "attn_sinks": a[6:7, :N_HEADS]}

    order = ("pre_mix_norm", "w_in", "conv_w", "attn_sinks", "attn_group_norm", "conv_group_norm", "w_out",
             "post_mix_norm", "pre_mlp_norm", "w_up", "w_down", "post_mlp_norm")
    outs = []
    for k, a in enumerate((g_small, d_small, nm_small, nv_small)):
        sm = unpack(a)
        outs += [big[nm][k] if nm in big else sm[nm] for nm in order]
    return (loss, grad_x[None], *outs)
```

```python
import functools

import jax
import jax.numpy as jnp
import numpy as np
from jax import lax
from jax.experimental import pallas as pl
from jax.experimental.pallas import tpu as pltpu

F32 = jnp.float32
BF16 = jnp.bfloat16

D_MODEL = 1024
HEAD_DIM = 64
ATTN_W = 512
CONV_W = 512
N_HEADS = 8
N_KV = 2
GROUP = 4
KV_W = 128
QKV_W = ATTN_W + 2 * KV_W
GATES_W = 3 * CONV_W
IN_COLS = QKV_W + GATES_W
D_FF = 4096
FF_CHUNK = 512
N_FF_CHUNKS = D_FF // FF_CHUNK
BLOCK = 128
ROT_HALF = 8
ROPE_THETA = 500000.0
NORM_EPS = 1e-6
NEG_INF = -1e30
ATTN_SCALE = 0.125
N_DEV = 8
N_CHIPS = 4
IN_SHARD = IN_COLS // N_DEV

ADAM_LR = 0.001
ADAM_B1 = 0.9
ADAM_B2 = 0.999
ADAM_EPS = 1e-08
ADAM_WD = 0.01
ADAM_STEP = 10

V7X_VMEM_BYTES = 64 * 1024 * 1024
VMEM_LIMIT = V7X_VMEM_BYTES - 2 * 1024 * 1024

MESH = pl.DeviceIdType.MESH
HBM_SPEC = pl.BlockSpec(memory_space=pltpu.HBM)


def _params(*sem):
    return pltpu.CompilerParams(dimension_semantics=sem, vmem_limit_bytes=VMEM_LIMIT)


def _mm(a, b):
    return jnp.dot(a, b, preferred_element_type=F32)


def _mm_nt(a, b):
    return lax.dot_general(a, b, (((1,), (1,)), ((), ())), preferred_element_type=F32)


def _mm_tn(a, b):
    return lax.dot_general(a, b, (((0,), (0,)), ((), ())), preferred_element_type=F32)


def _inv_rms(x):
    return lax.rsqrt(jnp.mean(x * x, axis=-1, keepdims=True) + NORM_EPS)


def _rms_bwd(xhat, r, gain, dy):
    gy = dy * gain
    return r * (gy - xhat * jnp.mean(gy * xhat, axis=-1, keepdims=True)), dy * xhat


def _colsum(a):
    return jnp.sum(a, axis=0, keepdims=True)


def _full(shape):
    zeros = (0,) * len(shape)
    return pl.BlockSpec(shape, lambda *_: zeros)


def _resident(shape):
    zeros = (0,) * len(shape)
    return pl.BlockSpec(shape, lambda *_: zeros, pipeline_mode=pl.Buffered(1))


def _rope_tables(t):
    pos = np.arange(t, dtype=np.float32)
    inv_freq = (ROPE_THETA ** (-np.arange(0, 2 * ROT_HALF, 2, dtype=np.float64) / (2 * ROT_HALF))).astype(np.float32)
    ang = (pos[:, None] * inv_freq[None, :]).astype(np.float64)
    cos, sin = np.cos(ang).astype(np.float32), np.sin(ang).astype(np.float32)
    zeros8 = np.zeros((t, ROT_HALF), np.float32)
    rest = np.zeros((t, HEAD_DIM - 2 * ROT_HALF), np.float32)
    c_head = np.concatenate([cos, cos, rest + 1.0], axis=1)
    s1_head = np.concatenate([zeros8, sin, rest], axis=1)
    s2_head = np.concatenate([-sin, zeros8, rest], axis=1)
    two = lambda a: jnp.asarray(np.concatenate([a, a], axis=1))
    return two(c_head), two(s1_head), two(s2_head)


def _rope(v, c, s1, s2):
    return v * c + pltpu.roll(v, ROT_HALF, 1) * s1 + pltpu.roll(v, 128 - ROT_HALF, 1) * s2


def _rope_transpose(dv, c, s1, s2):
    return dv * c + pltpu.roll(dv * s1, 128 - ROT_HALF, 1) + pltpu.roll(dv * s2, ROT_HALF, 1)


def _shift_rows_down(u, prev, k):
    row = lax.broadcasted_iota(jnp.int32, u.shape, 0)
    out = pltpu.roll(u, k, 0)
    for r in range(k):
        out = jnp.where(row == r, prev[8 - k + r:8 - k + r + 1, :], out)
    return out


def _shift_rows_up(u, nxt, k):
    n = u.shape[0]
    row = lax.broadcasted_iota(jnp.int32, u.shape, 0)
    out = pltpu.roll(u, n - k, 0)
    for r in range(k):
        out = jnp.where(row == n - k + r, nxt[r:r + 1, :], out)
    return out


def _conv3(u, u1, u2, w):
    return (w[0:1, :] * u2 + w[1:2, :] * u1) + w[2:3, :] * u


def _mesh_pos():
    return lax.axis_index("x"), lax.axis_index("y"), lax.axis_index("c")


def _slot(ref, pos):
    return ref.at[4 * pos[0] + 2 * pos[1] + pos[2]]


def _push(src, dst, sems, k, to):
    send_sems, recv_sems = sems
    return pltpu.make_async_remote_copy(src_ref=src, dst_ref=dst, send_sem=send_sems.at[k], recv_sem=recv_sems.at[k],
                                        device_id=to, device_id_type=MESH)


def _gather_near(first, last, shards, outs, sems, local_sems):
    x, y, c = _mesh_pos()
    me, peers = (x, y, c), [(x, y, 1 - c), (1 - x, y, c), (x, 1 - y, c)]
    n = len(shards)
    local = [pltpu.make_async_copy(shards[i], _slot(outs[i], me), local_sems.at[i]) for i in range(n)]
    sends = [_push(shards[i], _slot(outs[i], me), sems, 3 * i + k, peers[k]) for i in range(n) for k in range(3)]
    arrivals = [_push(shards[i], _slot(outs[i], peers[k]), sems, 3 * i + k, peers[k]) for i in range(n) for k in range(3)]

    @pl.when(first)
    def _():
        for cp in local + sends:
            cp.start()

    @pl.when(last)
    def _():
        for cp in sends:
            cp.wait_send()
        for cp in arrivals:
            cp.wait_recv()
        for cp in local:
            cp.wait()


def _gather_far(first, last, shards, ins, outs, sems):
    x, y, c = _mesh_pos()
    me, sibling = (x, y, c), (x, y, 1 - c)
    chips = [(1 - x, y), (x, 1 - y), (1 - x, 1 - y)]
    n = len(shards)
    diag_send = [_push(shards[i], _slot(outs[i], me), sems, 4 * i, (*chips[2], c)) for i in range(n)]
    diag_arrival = [_push(shards[i], _slot(outs[i], (*chips[2], c)), sems, 4 * i, (*chips[2], c)) for i in range(n)]
    passed = [[_push(_slot(ins[i], (*chips[j], c)), _slot(outs[i], (*chips[j], c)), sems, 4 * i + 1 + j, sibling)
               for i in range(n)] for j in range(3)]
    from_sibling = [_push(shards[i], _slot(outs[i], (*chips[j], 1 - c)), sems, 4 * i + 1 + j, sibling)
                    for i in range(n) for j in range(3)]

    @pl.when(first)
    def _():
        for cp in diag_send + passed[0] + passed[1]:
            cp.start()

    @pl.when(last)
    def _():
        for cp in diag_arrival:
            cp.wait_recv()
        for cp in passed[2]:
            cp.start()
        for cp in from_sibling:
            cp.wait_recv()
        for cp in diag_send + passed[0] + passed[1] + passed[2]:
            cp.wait_send()


def _in_proj_fwd(x, g1, w_in, conv_w, g_conv, rope, tm, shards):
    t = x.shape[0]
    rc, rs1, rs2 = rope
    n = len(shards)

    def body(*refs):
        x_ref, g1_ref, w_ref, cw_ref, gc_ref, c_ref, s1_ref, s2_ref = refs[:8]
        shard_refs = refs[8:8 + n]
        qkv_ref, gates_ref, mconv_ref = refs[8 + n:11 + n]
        gathered = refs[11 + n:11 + 2 * n]
        carry_ref = refs[11 + 2 * n]
        if n:
            step = pl.program_id(0)
            _gather_near(step == 0, step == pl.num_programs(0) - 1, shard_refs, gathered, refs[12 + 2 * n:14 + 2 * n],
                         refs[14 + 2 * n])

        @pl.when(pl.program_id(0) == 0)
        def _():
            carry_ref[...] = jnp.zeros_like(carry_ref)

        xv = x_ref[...]
        hn = ((xv * _inv_rms(xv)) * g1_ref[...]).astype(BF16)
        proj = _mm(hn, w_ref[...])
        c, s1, s2 = c_ref[...], s1_ref[...], s2_ref[...]
        for ci in range((ATTN_W + KV_W) // 128):
            sl = slice(128 * ci, 128 * (ci + 1))
            qkv_ref[:, sl] = _rope(proj[:, sl], c, s1, s2).astype(BF16)
        qkv_ref[:, ATTN_W + KV_W:QKV_W] = proj[:, ATTN_W + KV_W:QKV_W].astype(BF16)
        gates = proj[:, QKV_W:]
        gates_ref[...] = gates
        gb, gcc, xin = gates[:, :CONV_W], gates[:, CONV_W:2 * CONV_W], gates[:, 2 * CONV_W:]
        u = gcc * xin
        prev = carry_ref[...]
        conv = gb * _conv3(u, _shift_rows_down(u, prev, 1), _shift_rows_down(u, prev, 2), cw_ref[...])
        carry_ref[...] = u[tm - 8:tm, :]
        mconv_ref[...] = ((conv * _inv_rms(conv)) * gc_ref[...]).astype(BF16)

    tile = lambda w_: pl.BlockSpec((tm, w_), lambda i: (i, 0))
    comm_scratch = [pltpu.SemaphoreType.DMA((3 * n,)), pltpu.SemaphoreType.DMA((3 * n,)), pltpu.SemaphoreType.DMA((n,))]
    res = pl.pallas_call(
        body, name="in_proj_fwd", grid=(t // tm,),
        in_specs=[tile(D_MODEL), _full((1, D_MODEL)), _full((D_MODEL, IN_COLS)), _full((3, CONV_W)), _full((1, CONV_W)),
                  tile(128), tile(128), tile(128)] + [HBM_SPEC] * n,
        out_specs=[tile(QKV_W), tile(GATES_W), tile(CONV_W)] + [HBM_SPEC] * n,
        out_shape=[jax.ShapeDtypeStruct((t, QKV_W), BF16), jax.ShapeDtypeStruct((t, GATES_W), F32),
                   jax.ShapeDtypeStruct((t, CONV_W), BF16)]
        + [jax.ShapeDtypeStruct((N_DEV,) + s.shape, s.dtype) for s in shards],
        scratch_shapes=[pltpu.VMEM((8, CONV_W), F32)] + (comm_scratch if n else []),
        compiler_params=_params("arbitrary"),
    )(x, g1, w_in, conv_w, g_conv, rc, rs1, rs2, *shards)
    return res[0], res[1], res[2], list(res[3:])


GROUP_ROWS = GROUP * BLOCK


def _attn_mask(has_prev):
    row = lax.broadcasted_iota(jnp.int32, (GROUP_ROWS, 2 * BLOCK), 0) & (BLOCK - 1)
    col = lax.broadcasted_iota(jnp.int32, (GROUP_ROWS, 2 * BLOCK), 1)
    band = (col > row) & (col <= row + BLOCK)
    return band if has_prev is True else band & ((col >= BLOCK) | has_prev)


def _stack_heads(a, g):
    return jnp.concatenate([a[:, HEAD_DIM * (GROUP * g + hh):HEAD_DIM * (GROUP * g + hh + 1)] for hh in range(GROUP)], axis=0)


def _unstack_heads(a):
    return jnp.concatenate([a[BLOCK * hh:BLOCK * (hh + 1), :] for hh in range(GROUP)], axis=1)


def _group_sinks(sink_ref, g):
    head = lax.broadcasted_iota(jnp.int32, (GROUP_ROWS, 1), 0) // BLOCK
    out = jnp.full((GROUP_ROWS, 1), sink_ref[0, GROUP * g], F32)
    for hh in range(1, GROUP):
        out = jnp.where(head == hh, sink_ref[0, GROUP * g + hh], out)
    return out


def _attn_probs(qs, kk, sink, valid):
    s = jnp.where(valid, _mm_nt(qs, kk) * ATTN_SCALE, NEG_INF)
    m = jnp.maximum(jnp.max(s, axis=-1, keepdims=True), sink)
    p = jnp.exp(s - m)
    psink = jnp.exp(sink - m)
    inv_l = 1.0 / (jnp.sum(p, axis=-1, keepdims=True) + psink)
    return p * inv_l, psink * inv_l


ATTN_STEP_BLOCKS = 4
ATTN_STEP = ATTN_STEP_BLOCKS * BLOCK
ATTN_KEYS = ATTN_STEP + BLOCK


def _qkv_specs(order):
    prev = lambda i: jnp.maximum(ATTN_STEP_BLOCKS * order(i) - 1, 0)
    kcol, vcol = ATTN_W // KV_W, ATTN_W // KV_W + 1
    return [pl.BlockSpec((ATTN_STEP, ATTN_W), lambda i: (order(i), 0)),
            pl.BlockSpec((BLOCK, KV_W), lambda i: (prev(i), kcol)), pl.BlockSpec((ATTN_STEP, KV_W), lambda i: (order(i), kcol)),
            pl.BlockSpec((BLOCK, KV_W), lambda i: (prev(i), vcol)), pl.BlockSpec((ATTN_STEP, KV_W), lambda i: (order(i), vcol))]


def _attn_fwd(qkv, sinks, g_attn, shards, gathered):
    t = qkv.shape[0]
    n = len(shards)

    def body(*refs):
        sink_ref, q_ref, kp_ref, kc_ref, vp_ref, vc_ref, ga_ref = refs[:7]
        attn_ref, mattn_ref = refs[7 + 2 * n:9 + 2 * n]
        step = pl.program_id(0)
        if n:
            _gather_far(step == 0, step == pl.num_programs(0) - 1, refs[7:7 + n], refs[7 + n:7 + 2 * n],
                        refs[9 + 2 * n:9 + 3 * n], refs[9 + 3 * n:11 + 3 * n])
        q = q_ref[...]
        keys = jnp.concatenate([kp_ref[...], kc_ref[...]], axis=0)
        vals = jnp.concatenate([vp_ref[...], vc_ref[...]], axis=0)
        sink = [_group_sinks(sink_ref, g) for g in range(N_KV)]
        gain = ga_ref[...]
        for b in range(ATTN_STEP_BLOCKS):
            rows, window = slice(BLOCK * b, BLOCK * (b + 1)), slice(BLOCK * b, BLOCK * (b + 2))
            valid = _attn_mask(True if b else step > 0)
            outs = []
            for g in range(N_KV):
                gs = slice(HEAD_DIM * g, HEAD_DIM * (g + 1))
                probs, _ = _attn_probs(_stack_heads(q[rows], g), keys[window, gs], sink[g], valid)
                outs.append(_unstack_heads(_mm(probs.astype(BF16), vals[window, gs])))
            attn = jnp.concatenate(outs, axis=1)
            attn_ref[rows, :] = attn
            mattn_ref[rows, :] = ((attn * _inv_rms(attn)) * gain).astype(BF16)

    blk = pl.BlockSpec((ATTN_STEP, ATTN_W), lambda j: (j, 0))
    res = pl.pallas_call(
        body, name="attn_fwd", grid=(t // ATTN_STEP,),
        in_specs=[pl.BlockSpec(memory_space=pltpu.SMEM)] + _qkv_specs(lambda j: j) + [_full((1, ATTN_W))]
        + [HBM_SPEC] * (2 * n),
        out_specs=[blk, blk] + [HBM_SPEC] * n,
        out_shape=[jax.ShapeDtypeStruct((t, ATTN_W), F32), jax.ShapeDtypeStruct((t, ATTN_W), BF16)]
        + [jax.ShapeDtypeStruct(g.shape, g.dtype) for g in gathered],
        input_output_aliases={7 + n + i: 2 + i for i in range(n)},
        scratch_shapes=[pltpu.SemaphoreType.DMA((4 * n,)), pltpu.SemaphoreType.DMA((4 * n,))] if n else [],
        compiler_params=_params("arbitrary"),
    )(sinks, qkv, qkv, qkv, qkv, qkv, g_attn, *shards, *gathered)
    return res[0], res[1], list(res[2:])


SMALL_ROWS = 8
ROW_LOSS, ROW_G2, ROW_G3, ROW_G4 = 0, 1, 2, 3


def _mid(mattn, mconv, x, target, g2, g3, g4, w_out, w_up, w_down, tm):
    t = x.shape[0]

    def body(ma_ref, mc_ref, x_ref, t_ref, g2_ref, g3_ref, g4_ref, wo_ref, wu_ref, wd_ref,
             actt_ref, dup_ref, hn2t_ref, dmo_ref, dmix_ref, dh_ref, dmixed_ref, small_ref, up_ref):
        @pl.when(pl.program_id(0) == 0)
        def _():
            small_ref[...] = jnp.zeros_like(small_ref)

        g2, g3, g4 = g2_ref[...], g3_ref[...], g4_ref[...]
        mix_out = _mm(ma_ref[...], wo_ref[0:ATTN_W, :]) + _mm(mc_ref[...], wo_ref[ATTN_W:, :])
        r2 = _inv_rms(mix_out)
        mo_hat = mix_out * r2
        h = x_ref[...] + mo_hat * g2
        r3 = _inv_rms(h)
        h_hat = h * r3
        hn2 = (h_hat * g3).astype(BF16)
        hn2t_ref[...] = hn2.T
        up = jnp.maximum(_mm(hn2, wu_ref[...]), 0.0)
        up_ref[...] = up.astype(BF16)
        act = (up * up).astype(BF16)
        actt_ref[...] = act.T
        mlp = _mm(act, wd_ref[...])
        r4 = _inv_rms(mlp)
        ml_hat = mlp * r4
        err = (h + ml_hat * g4) - t_ref[...]
        d_out = err * (1.0 / D_MODEL)
        d_mlp, dg4 = _rms_bwd(ml_hat, r4, g4, d_out)
        dmo = d_mlp.astype(BF16)
        dmo_ref[...] = dmo
        dup = (_mm_nt(dmo, wd_ref[...]) * (2.0 * up_ref[...].astype(F32))).astype(BF16)
        dup_ref[...] = dup
        dhn2 = _mm_nt(dup, wu_ref[...])
        dh_norm, dg3 = _rms_bwd(h_hat, r3, g3, dhn2)
        dh = d_out + dh_norm
        dh_ref[...] = dh
        d_mix, dg2 = _rms_bwd(mo_hat, r2, g2, dh)
        dmix = d_mix.astype(BF16)
        dmix_ref[...] = dmix
        dmixed_ref[...] = _mm_nt(dmix, wo_ref[...])
        small_ref[ROW_LOSS:ROW_LOSS + 1, :] += _colsum(err * err)
        small_ref[ROW_G2:ROW_G2 + 1, :] += _colsum(dg2)
        small_ref[ROW_G3:ROW_G3 + 1, :] += _colsum(dg3)
        small_ref[ROW_G4:ROW_G4 + 1, :] += _colsum(dg4)

    tile = lambda n: pl.BlockSpec((tm, n), lambda i: (i, 0))
    cols = lambda n: pl.BlockSpec((n, tm), lambda i: (0, i))
    gain = _full((1, D_MODEL))
    return pl.pallas_call(
        body, name="mid_fwd_bwd", grid=(t // tm,),
        in_specs=[tile(ATTN_W), tile(CONV_W), tile(D_MODEL), tile(D_MODEL), gain, gain, gain,
                  _resident((D_MODEL, D_MODEL)), _resident((D_MODEL, D_FF)), _resident((D_FF, D_MODEL))],
        out_specs=[cols(D_FF), tile(D_FF), cols(D_MODEL), tile(D_MODEL), tile(D_MODEL), tile(D_MODEL), tile(D_MODEL),
                   _full((SMALL_ROWS, D_MODEL))],
        out_shape=[jax.ShapeDtypeStruct((D_FF, t), BF16), jax.ShapeDtypeStruct((t, D_FF), BF16),
                   jax.ShapeDtypeStruct((D_MODEL, t), BF16), jax.ShapeDtypeStruct((t, D_MODEL), BF16),
                   jax.ShapeDtypeStruct((t, D_MODEL), BF16), jax.ShapeDtypeStruct((t, D_MODEL), F32),
                   jax.ShapeDtypeStruct((t, D_MODEL), F32), jax.ShapeDtypeStruct((SMALL_ROWS, D_MODEL), F32)],
        scratch_shapes=[pltpu.VMEM((tm, D_FF), BF16)],
        compiler_params=_params("arbitrary"),
    )(mattn, mconv, x, target, g2, g3, g4, w_out, w_up, w_down)


PEER_FLIPS = ((1, 1, 0), (1, 0, 0), (0, 1, 0), (1, 1, 1), (1, 0, 1), (0, 1, 1), (0, 0, 1))


def _peer_order(dev):
    masks = [4 * fx + 2 * fy + fc for fx, fy, fc in PEER_FLIPS] + [0]
    return jnp.bitwise_xor(dev, jnp.asarray(masks, jnp.int32)).astype(jnp.int32)


def _dw_scatter(at, b, order, tk, at_chunked, name):
    t = b.shape[0]
    n_k = t // tk
    rows, cols = (FF_CHUNK, D_MODEL) if at_chunked else (D_MODEL, FF_CHUNK)

    def body(order_ref, a_ref, b_ref, o_ref, recv_ref, send_buf, send_sems, recv_sems):
        s_now, k = pl.program_id(0), pl.program_id(1)
        x, y, c = _mesh_pos()
        mine = 4 * x + 2 * y + c

        def peer(s):
            fx, fy, fc = PEER_FLIPS[s]
            return (1 - x if fx else x, 1 - y if fy else y, 1 - c if fc else c)

        def send(s):
            return _push(send_buf.at[s], recv_ref.at[mine], (send_sems, recv_sems), s, peer(s))

        def arrival(s):
            return _push(send_buf.at[s], _slot(recv_ref, peer(s)), (send_sems, recv_sems), s, peer(s))

        @pl.when(k == 0)
        def _():
            o_ref[...] = jnp.zeros_like(o_ref)

        tokens = pl.ds(pl.multiple_of(k * tk, tk), tk)
        if at_chunked:
            o_ref[...] += _mm(a_ref[...], b_ref[tokens, :])
        else:
            o_ref[...] += _mm(a_ref[:, tokens], b_ref[...])
        for s in range(N_DEV - 1):
            @pl.when((s_now == s) & (k == n_k - 1))
            def _():
                send_buf[s] = o_ref[...].astype(BF16)
                send(s).start()

        @pl.when((s_now == N_DEV - 1) & (k == n_k - 1))
        def _():
            for s in range(N_DEV - 1):
                send(s).wait_send()
                arrival(s).wait_recv()

    if at_chunked:
        in_specs = [pl.BlockSpec((FF_CHUNK, tk), lambda s, k, order_ref: (order_ref[s], k)), _resident((t, D_MODEL))]
    else:
        in_specs = [_resident((D_MODEL, t)), pl.BlockSpec((tk, FF_CHUNK), lambda s, k, order_ref: (k, order_ref[s]))]
    return pl.pallas_call(
        body, name=name,
        grid_spec=pltpu.PrefetchScalarGridSpec(
            num_scalar_prefetch=1, grid=(N_DEV, n_k), in_specs=in_specs,
            out_specs=[pl.BlockSpec((None, rows, cols), lambda s, k, order_ref: (order_ref[s], 0, 0)), HBM_SPEC],
            scratch_shapes=[pltpu.VMEM((N_DEV - 1, rows, cols), BF16), pltpu.SemaphoreType.DMA((N_DEV - 1,)),
                            pltpu.SemaphoreType.DMA((N_DEV - 1,))]),
        out_shape=[jax.ShapeDtypeStruct((N_DEV, rows, cols), F32), jax.ShapeDtypeStruct((N_DEV, rows, cols), BF16)],
        compiler_params=_params("arbitrary", "arbitrary"),
    )(order, at, b)


def _dw_out(mattn, mconv, dmix, tk):
    t = dmix.shape[0]

    def body(ma_ref, mc_ref, b_ref, o_ref):
        @pl.when(pl.program_id(0) == 0)
        def _():
            o_ref[...] = jnp.zeros_like(o_ref)
        b = b_ref[...]
        o_ref[0:ATTN_W, :] += _mm_tn(ma_ref[...], b)
        o_ref[ATTN_W:, :] += _mm_tn(mc_ref[...], b)

    tile = lambda n: pl.BlockSpec((tk, n), lambda k: (k, 0))
    return pl.pallas_call(
        body, name="dw_out", grid=(t // tk,),
        in_specs=[tile(ATTN_W), tile(CONV_W), tile(D_MODEL)],
        out_specs=_full((D_MODEL, D_MODEL)),
        out_shape=jax.ShapeDtypeStruct((D_MODEL, D_MODEL), F32),
        compiler_params=_params("arbitrary"),
    )(mattn, mconv, dmix)


ROW_GATTN, ROW_GCONV, ROW_CW0 = 0, 1, 2


def _mix_bwd(dmixed, attn, gates, g_attn, g_conv, conv_w, tm):
    t = attn.shape[0]
    n = t // tm
    rev = lambda i: n - 1 - i

    def body(dm_ref, attn_ref, gates_ref, gprev_ref, ga_ref, gc_ref, cw_ref, dattn_ref, dgates_ref, small_ref, carry_ref):
        i = pl.program_id(0)

        @pl.when(i == 0)
        def _():
            small_ref[...] = jnp.zeros_like(small_ref)
            carry_ref[...] = jnp.zeros_like(carry_ref)

        dm = dm_ref[...]
        a = attn_ref[...]
        ra = _inv_rms(a)
        a_hat = a * ra
        dattn, dga = _rms_bwd(a_hat, ra, ga_ref[...], dm[:, :ATTN_W])
        dattn_ref[...] = dattn

        gates = gates_ref[...]
        gb, gcc, xin = gates[:, :CONV_W], gates[:, CONV_W:2 * CONV_W], gates[:, 2 * CONV_W:]
        u = gcc * xin
        gp = gprev_ref[...]
        uprev = jnp.where(rev(i) == 0, 0.0, gp[:, CONV_W:2 * CONV_W] * gp[:, 2 * CONV_W:])
        u1, u2 = _shift_rows_down(u, uprev, 1), _shift_rows_down(u, uprev, 2)
        w = cw_ref[...]
        c = _conv3(u, u1, u2, w)
        conv = gb * c
        rcv = _inv_rms(conv)
        c_hat = conv * rcv
        dconv, dgc = _rms_bwd(c_hat, rcv, gc_ref[...], dm[:, ATTN_W:])
        dc = dconv * gb
        nxt = carry_ref[...]
        du = (w[2:3, :] * dc + w[1:2, :] * _shift_rows_up(dc, nxt, 1)) + w[0:1, :] * _shift_rows_up(dc, nxt, 2)
        carry_ref[...] = dc[0:8, :]
        dgates_ref[:, :CONV_W] = (dconv * c).astype(BF16)
        dgates_ref[:, CONV_W:2 * CONV_W] = (du * xin).astype(BF16)
        dgates_ref[:, 2 * CONV_W:] = (du * gcc).astype(BF16)
        small_ref[ROW_GATTN:ROW_GATTN + 1, :] += _colsum(dga)
        small_ref[ROW_GCONV:ROW_GCONV + 1, :] += _colsum(dgc)
        small_ref[ROW_CW0:ROW_CW0 + 1, :] += _colsum(dc * u2)
        small_ref[ROW_CW0 + 1:ROW_CW0 + 2, :] += _colsum(dc * u1)
        small_ref[ROW_CW0 + 2:ROW_CW0 + 3, :] += _colsum(dc * u)

    tile = lambda w_: pl.BlockSpec((tm, w_), lambda i: (rev(i), 0))
    prev8 = pl.BlockSpec((8, GATES_W), lambda i: (jnp.maximum(rev(i) * (tm // 8) - 1, 0), 0))
    return pl.pallas_call(
        body, name="mix_bwd", grid=(n,),
        in_specs=[tile(D_MODEL), tile(ATTN_W), tile(GATES_W), prev8, _full((1, ATTN_W)), _full((1, CONV_W)),
                  _full((3, CONV_W))],
        out_specs=[tile(ATTN_W), tile(GATES_W), _full((SMALL_ROWS, CONV_W))],
        out_shape=[jax.ShapeDtypeStruct((t, ATTN_W), F32), jax.ShapeDtypeStruct((t, GATES_W), BF16),
                   jax.ShapeDtypeStruct((SMALL_ROWS, CONV_W), F32)],
        scratch_shapes=[pltpu.VMEM((8, CONV_W), F32)],
        compiler_params=_params("arbitrary"),
    )(dmixed, attn, gates, gates, g_attn, g_conv, conv_w)


def _attn_bwd(qkv, dattn, sinks, rope):
    t = qkv.shape[0]
    n_steps = t // ATTN_STEP
    rev = lambda i: n_steps - 1 - i
    rc, rs1, rs2 = rope

    def body(sink_ref, q_ref, kp_ref, kc_ref, vp_ref, vc_ref, do_ref, c_ref, s1_ref, s2_ref,
             dqkv_ref, dsink_ref, ck_ref, cv_ref, kacc_ref, vacc_ref):
        i = pl.program_id(0)

        @pl.when(i == 0)
        def _():
            dsink_ref[...] = jnp.zeros_like(dsink_ref)
            ck_ref[...] = jnp.zeros_like(ck_ref)
            cv_ref[...] = jnp.zeros_like(cv_ref)

        kacc_ref[...] = jnp.zeros_like(kacc_ref)
        vacc_ref[...] = jnp.zeros_like(vacc_ref)
        q = q_ref[...]
        dout = do_ref[...].astype(BF16)
        keys = jnp.concatenate([kp_ref[...], kc_ref[...]], axis=0)
        vals = jnp.concatenate([vp_ref[...], vc_ref[...]], axis=0)
        sink = [_group_sinks(sink_ref, g) for g in range(N_KV)]
        c, s1, s2 = c_ref[...], s1_ref[...], s2_ref[...]
        lane = lax.broadcasted_iota(jnp.int32, (1, 128), 1)
        dsink = jnp.zeros((1, 128), F32)
        for b in range(ATTN_STEP_BLOCKS):
            rows, window = slice(BLOCK * b, BLOCK * (b + 1)), slice(BLOCK * b, BLOCK * (b + 2))
            valid = _attn_mask(True if b else rev(i) > 0)
            dq_parts, dk_parts, dv_parts = [], [], []
            for g in range(N_KV):
                gs = slice(HEAD_DIM * g, HEAD_DIM * (g + 1))
                kk, vv = keys[window, gs], vals[window, gs]
                qs, dos = _stack_heads(q[rows], g), _stack_heads(dout[rows], g)
                probs, psink = _attn_probs(qs, kk, sink[g], valid)
                dp = _mm_nt(dos, vv)
                delta = jnp.sum(probs * dp, axis=-1, keepdims=True)
                ds = (probs * (dp - delta) * ATTN_SCALE).astype(BF16)
                sink_terms = psink * delta
                for hh in range(GROUP):
                    head_sum = jnp.sum(sink_terms[BLOCK * hh:BLOCK * (hh + 1), :])
                    dsink = dsink + jnp.where(lane == GROUP * g + hh, -head_sum, 0.0)
                dq_parts.append(_unstack_heads(_mm(ds, kk)))
                dk_parts.append(_mm_tn(ds, qs))
                dv_parts.append(_mm_tn(probs.astype(BF16), dos))
            kacc_ref[window, :] += jnp.concatenate(dk_parts, axis=1)
            vacc_ref[window, :] += jnp.concatenate(dv_parts, axis=1)
            dq = jnp.concatenate(dq_parts, axis=1)
            for ci in range(ATTN_W // 128):
                sl = slice(128 * ci, 128 * (ci + 1))
                dqkv_ref[rows, sl] = _rope_transpose(dq[:, sl], c[rows], s1[rows], s2[rows]).astype(BF16)
        kacc_ref[ATTN_STEP:, :] += ck_ref[...]
        vacc_ref[ATTN_STEP:, :] += cv_ref[...]
        ck_ref[...] = kacc_ref[:BLOCK, :]
        cv_ref[...] = vacc_ref[:BLOCK, :]
        dqkv_ref[:, ATTN_W:ATTN_W + KV_W] = _rope_transpose(kacc_ref[BLOCK:, :], c, s1, s2).astype(BF16)
        dqkv_ref[:, ATTN_W + KV_W:] = vacc_ref[BLOCK:, :].astype(BF16)
        dsink_ref[0:1, :] += dsink

    blk = lambda w_: pl.BlockSpec((ATTN_STEP, w_), lambda i: (rev(i), 0))
    return pl.pallas_call(
        body, name="attn_bwd", grid=(n_steps,),
        in_specs=[pl.BlockSpec(memory_space=pltpu.SMEM)] + _qkv_specs(rev) + [blk(ATTN_W), blk(128), blk(128), blk(128)],
        out_specs=[blk(QKV_W), _full((8, 128))],
        out_shape=[jax.ShapeDtypeStruct((t, QKV_W), BF16), jax.ShapeDtypeStruct((8, 128), F32)],
        scratch_shapes=[pltpu.VMEM((BLOCK, KV_W), F32), pltpu.VMEM((BLOCK, KV_W), F32),
                        pltpu.VMEM((ATTN_KEYS, KV_W), F32), pltpu.VMEM((ATTN_KEYS, KV_W), F32)],
        compiler_params=_params("arbitrary"),
    )(sinks, qkv, qkv, qkv, qkv, qkv, dattn, rc, rs1, rs2)


def _in_proj_bwd(dqkv, dgates, x, dh, g1, w_in, tm):
    t = x.shape[0]

    def body(dq_ref, dg_ref, x_ref, dh_ref, g1_ref, w_ref, dx_ref, dwa_ref, dwb_ref, dg1_ref):
        @pl.when(pl.program_id(0) == 0)
        def _():
            dwa_ref[...] = jnp.zeros_like(dwa_ref)
            dwb_ref[...] = jnp.zeros_like(dwb_ref)
            dg1_ref[...] = jnp.zeros_like(dg1_ref)

        dq, dg = dq_ref[...], dg_ref[...]
        dhn = _mm_nt(dq, w_ref[:, :QKV_W]) + _mm_nt(dg, w_ref[:, QKV_W:])
        xv = x_ref[...]
        r = _inv_rms(xv)
        x_hat = xv * r
        g1 = g1_ref[...]
        dx, dg1 = _rms_bwd(x_hat, r, g1, dhn)
        dx_ref[...] = dh_ref[...] + dx
        hn = (x_hat * g1).astype(BF16)
        dwa_ref[...] += _mm_tn(hn, dq)
        dwb_ref[...] += _mm_tn(hn, dg)
        dg1_ref[0:1, :] += _colsum(dg1)

    tile = lambda n: pl.BlockSpec((tm, n), lambda i: (i, 0))
    return pl.pallas_call(
        body, name="in_proj_bwd", grid=(t // tm,),
        in_specs=[tile(QKV_W), tile(GATES_W), tile(D_MODEL), tile(D_MODEL), _full((1, D_MODEL)),
                  _resident((D_MODEL, IN_COLS))],
        out_specs=[tile(D_MODEL), _full((D_MODEL, QKV_W)), _full((D_MODEL, GATES_W)), _full((SMALL_ROWS, D_MODEL))],
        out_shape=[jax.ShapeDtypeStruct((t, D_MODEL), F32), jax.ShapeDtypeStruct((D_MODEL, QKV_W), F32),
                   jax.ShapeDtypeStruct((D_MODEL, GATES_W), F32), jax.ShapeDtypeStruct((SMALL_ROWS, D_MODEL), F32)],
        compiler_params=_params("arbitrary"),
    )(dqkv, dgates, x, dh, g1, w_in)


def _all_gather(shards, name):
    n = len(shards)

    def body(*refs):
        ins, outs = refs[:n], refs[n:2 * n]
        send_sems, recv_sems, local_sems = refs[2 * n:]
        x, y, c = _mesh_pos()
        me, sibling = (x, y, c), (x, y, 1 - c)
        chips = [(1 - x, y), (x, 1 - y), (1 - x, 1 - y)]

        def copy(i, k, block, to, src=None):
            dst = outs[i].at[4 * block[0] + 2 * block[1] + block[2]]
            return pltpu.make_async_remote_copy(
                src_ref=dst if src is None else src, dst_ref=dst, send_sem=send_sems.at[7 * i + k],
                recv_sem=recv_sems.at[7 * i + k], device_id=to, device_id_type=MESH)

        mine = [pltpu.make_async_copy(ins[i], outs[i].at[4 * x + 2 * y + c], local_sems.at[i]) for i in range(n)]
        for cp in mine:
            cp.start()
        first = []
        for i in range(n):
            first.append(copy(i, 0, me, sibling, src=ins[i]))
            first += [copy(i, 1 + j, me, (*chip, c), src=ins[i]) for j, chip in enumerate(chips)]
        for cp in first:
            cp.start()
        passed = []
        for j, chip in enumerate(chips):
            for i in range(n):
                copy(i, 1 + j, (*chip, c), me).wait_recv()
                cp = copy(i, 4 + j, (*chip, c), sibling)
                cp.start()
                passed.append(cp)
        for i in range(n):
            copy(i, 0, sibling, me).wait_recv()
            for j, chip in enumerate(chips):
                copy(i, 4 + j, (*chip, 1 - c), me).wait_recv()
        for cp in first + passed:
            cp.wait_send()
        for cp in mine:
            cp.wait()

    return pl.pallas_call(
        body, name=name,
        in_specs=[HBM_SPEC] * n, out_specs=[HBM_SPEC] * n,
        out_shape=[jax.ShapeDtypeStruct((N_DEV,) + s.shape, s.dtype) for s in shards],
        scratch_shapes=[pltpu.SemaphoreType.DMA((7 * n,)), pltpu.SemaphoreType.DMA((7 * n,)),
                        pltpu.SemaphoreType.DMA((n,))],
    )(*shards)


def _sibling_exchange(grads, name):
    n = len(grads)

    def body(*refs):
        ins, outs = refs[:n], refs[n:2 * n]
        send_sems, recv_sems = refs[2 * n:]
        x, y, c = _mesh_pos()
        copies = [pltpu.make_async_remote_copy(
            src_ref=ins[i].at[1 - c], dst_ref=outs[i], send_sem=send_sems.at[i], recv_sem=recv_sems.at[i],
            device_id=(x, y, 1 - c), device_id_type=MESH) for i in range(n)]
        for cp in copies:
            cp.start()
        for cp in copies:
            cp.wait()

    return pl.pallas_call(
        body, name=name,
        in_specs=[HBM_SPEC] * n, out_specs=[HBM_SPEC] * n,
        out_shape=[jax.ShapeDtypeStruct(g.shape[1:], g.dtype) for g in grads],
        scratch_shapes=[pltpu.SemaphoreType.DMA((n,)), pltpu.SemaphoreType.DMA((n,))],
    )(*grads)


def _chip_exchange(sums, name):
    n = len(sums)

    def body(*refs):
        ins, outs = refs[:n], refs[n:2 * n]
        send_sems, recv_sems = refs[2 * n:]
        x, y, c = _mesh_pos()
        chips = [(1 - x, y), (x, 1 - y), (1 - x, 1 - y)]
        copies = [pltpu.make_async_remote_copy(
            src_ref=ins[i].at[2 * chip[0] + chip[1]], dst_ref=outs[i].at[k], send_sem=send_sems.at[3 * i + k],
            recv_sem=recv_sems.at[3 * i + k], device_id=(*chip, c), device_id_type=MESH)
            for i in range(n) for k, chip in enumerate(chips)]
        for cp in copies:
            cp.start()
        for cp in copies:
            cp.wait()

    return pl.pallas_call(
        body, name=name,
        in_specs=[HBM_SPEC] * n, out_specs=[HBM_SPEC] * n,
        out_shape=[jax.ShapeDtypeStruct((3,) + s.shape[1:], s.dtype) for s in sums],
        scratch_shapes=[pltpu.SemaphoreType.DMA((3 * n,)), pltpu.SemaphoreType.DMA((3 * n,))],
    )(*sums)


def _pair_sum(grad, recv, pos, tr):
    _, _, rows, cols = grad.shape

    def body(pos_ref, g_ref, r_ref, sb_ref):
        sb_ref[...] = (g_ref[...] + r_ref[...]).astype(BF16)

    return pl.pallas_call(
        body, name="pair_sum",
        grid_spec=pltpu.PrefetchScalarGridSpec(
            num_scalar_prefetch=1, grid=(N_CHIPS, rows // tr),
            in_specs=[pl.BlockSpec((None, None, tr, cols), lambda p, i, pos: (pos[0], p, i, 0)),
                      pl.BlockSpec((None, tr, cols), lambda p, i, pos: (p, i, 0))],
            out_specs=pl.BlockSpec((None, tr, cols), lambda p, i, pos: (p, i, 0))),
        out_shape=jax.ShapeDtypeStruct((N_CHIPS, rows, cols), BF16),
        compiler_params=_params("parallel", "parallel"),
    )(pos, grad, recv)


def _adam_math(w, g, m, v):
    m = ADAM_B1 * m + (1.0 - ADAM_B1) * g
    v = ADAM_B2 * v + (1.0 - ADAM_B2) * (g * g)
    m_hat = m / (1.0 - ADAM_B1 ** ADAM_STEP)
    v_hat = v / (1.0 - ADAM_B2 ** ADAM_STEP)
    delta = -ADAM_LR * (m_hat / (jnp.sqrt(v_hat) + ADAM_EPS) + ADAM_WD * w)
    return delta, m, v


def _adamw_shard(w, m, v, grad, from_sibling, from_chips, pos, tr):
    rows, cols = w.shape

    def body(pos_ref, w_ref, m_ref, v_ref, own_ref, sib_ref, r_ref, g_ref, d_ref, nm_ref, nv_ref):
        g = own_ref[...] + sib_ref[...]
        for k in range(3):
            g = g + r_ref[k].astype(F32)
        g_ref[...] = g
        d_ref[...], nm_ref[...], nv_ref[...] = _adam_math(w_ref[...], g, m_ref[...], v_ref[...])

    tile = pl.BlockSpec((tr, cols), lambda i, pos: (i, 0))
    out = jax.ShapeDtypeStruct((rows, cols), F32)
    return pl.pallas_call(
        body, name="adamw_shard",
        grid_spec=pltpu.PrefetchScalarGridSpec(
            num_scalar_prefetch=1, grid=(rows // tr,),
            in_specs=[tile, tile, tile,
                      pl.BlockSpec((None, None, tr, cols), lambda i, pos: (pos[0], pos[1], i, 0)),
                      pl.BlockSpec((None, tr, cols), lambda i, pos: (pos[1], i, 0)),
                      pl.BlockSpec((3, tr, cols), lambda i, pos: (0, i, 0))],
            out_specs=[tile] * 4),
        out_shape=[out] * 4,
        compiler_params=_params("parallel"),
    )(pos, w, m, v, grad, from_sibling, from_chips)


def _adamw_scattered(w, m, v, grad, recv, order, tr):
    rows, cols = w.shape
    n_peers = N_DEV - 1

    def body(order_ref, w_ref, m_ref, v_ref, own_ref, *rest):
        peers, (g_ref, d_ref, nm_ref, nv_ref) = rest[:n_peers], rest[n_peers:]
        g = own_ref[...]
        for r_ref in peers:
            g = g + r_ref[...].astype(F32)
        g_ref[...] = g
        d_ref[...], nm_ref[...], nv_ref[...] = _adam_math(w_ref[...], g, m_ref[...], v_ref[...])

    tile = pl.BlockSpec((tr, cols), lambda i, order_ref: (i, 0))
    slot = lambda s: pl.BlockSpec((None, tr, cols), lambda i, order_ref: (order_ref[s], i, 0))
    out = jax.ShapeDtypeStruct((rows, cols), F32)
    return pl.pallas_call(
        body, name="adamw_scattered",
        grid_spec=pltpu.PrefetchScalarGridSpec(
            num_scalar_prefetch=1, grid=(rows // tr,),
            in_specs=[tile, tile, tile, slot(n_peers)] + [slot(s) for s in range(n_peers)],
            out_specs=[tile] * 4),
        out_shape=[out] * 4,
        compiler_params=_params("parallel"),
    )(order, w, m, v, grad, *([recv] * n_peers))


def _sum_devices(gathered):
    _, rows, cols = gathered.shape

    def body(g_ref, o_ref):
        s = g_ref[0]
        for d in range(1, N_DEV):
            s = s + g_ref[d]
        o_ref[...] = s

    return pl.pallas_call(
        body, name="sum_devices", in_specs=[_full(gathered.shape)], out_specs=_full((rows, cols)), grid=(1,),
        out_shape=jax.ShapeDtypeStruct((rows, cols), F32),
    )(gathered)


def _adamw_small(w, g, m, v):
    def body(w_ref, g_ref, m_ref, v_ref, d_ref, nm_ref, nv_ref):
        d_ref[...], nm_ref[...], nv_ref[...] = _adam_math(w_ref[...], g_ref[...], m_ref[...], v_ref[...])

    spec = _full(w.shape)
    out = jax.ShapeDtypeStruct(w.shape, F32)
    return pl.pallas_call(
        body, name="adamw_small", grid=(1,), in_specs=[spec] * 4, out_specs=[spec] * 3, out_shape=[out] * 3,
    )(w, g, m, v)


TOKEN_TILE = 512
MID_TILE = 256
DW_TILE = 1024
DW_SCATTER_TILE = 8192
ADAM_ROWS = 128


def _local_grads(x, target, g1, w_in, conv_w, sinks, g_attn, g_conv, g2, g3, g4, shards, order):
    t = x.shape[0]
    tm = min(TOKEN_TILE, t)
    rope = _rope_tables(t)
    qkv, gates, mconv, gathered = _in_proj_fwd(x, g1, w_in, conv_w, g_conv, rope, tm, shards)
    attn, mattn, (w_out, w_up, w_down) = _attn_fwd(qkv, sinks, g_attn, shards, gathered)
    actt, dup, hn2t, dmo, dmix, dh, dmixed, small_mid = _mid(
        mattn, mconv, x, target, g2, g3, g4, w_out.reshape(D_MODEL, D_MODEL),
        w_up.transpose(1, 0, 2).reshape(D_MODEL, D_FF), w_down.reshape(D_FF, D_MODEL), min(MID_TILE, t))
    tk = min(DW_TILE, t)
    dw_up = _dw_scatter(hn2t, dup, order, min(DW_SCATTER_TILE, t), False, "dw_up")
    dw_down = _dw_scatter(actt, dmo, order, min(DW_SCATTER_TILE, t), True, "dw_down")
    dw_out = _dw_out(mattn, mconv, dmix, tk)
    dattn, dgates, small_mix = _mix_bwd(dmixed, attn, gates, g_attn, g_conv, conv_w, tm)
    dqkv, dsink = _attn_bwd(qkv, dattn, sinks, rope)
    grad_x, dwa, dwb, small_in = _in_proj_bwd(dqkv, dgates, x, dh, g1, w_in, tm)
    dw_in = jnp.concatenate([dwa, dwb], axis=1)
    return grad_x, dw_in, dw_out, dw_up, dw_down, (small_mid, small_mix, dsink, small_in)


def _by_dest(a, rows_major):
    r, c = a.shape
    if rows_major:
        return a.reshape(N_CHIPS, 2, r // N_DEV, c).transpose(1, 0, 2, 3)
    return a.reshape(r, N_CHIPS, 2, c // N_DEV).transpose(2, 1, 0, 3)


def _pack_small(small_mid, small_mix, dsink, small_in):
    z = lambda n: jnp.zeros((1, n), F32)
    rows = [
        small_mid[ROW_LOSS:ROW_LOSS + 1],
        small_in[0:1],
        small_mid[ROW_G2:ROW_G2 + 1],
        small_mid[ROW_G3:ROW_G3 + 1],
        small_mid[ROW_G4:ROW_G4 + 1],
        jnp.concatenate([small_mix[ROW_GATTN:ROW_GATTN + 1], small_mix[ROW_GCONV:ROW_GCONV + 1]], axis=1),
        jnp.concatenate([small_mix[ROW_CW0:ROW_CW0 + 1], small_mix[ROW_CW0 + 1:ROW_CW0 + 2]], axis=1),
        jnp.concatenate([small_mix[ROW_CW0 + 2:ROW_CW0 + 3], dsink[0:1, :], z(D_MODEL - CONV_W - 128)], axis=1),
    ]
    return jnp.concatenate(rows, axis=0)


def kernel(x, pre_mix_norm, w_in, conv_w, attn_sinks, attn_group_norm, conv_group_norm, w_out, post_mix_norm, pre_mlp_norm, w_up, w_down, post_mlp_norm, loss_target, m_pre_mix_norm, m_w_in, m_conv_w, m_attn_sinks, m_attn_group_norm, m_conv_group_norm, m_w_out, m_post_mix_norm, m_pre_mlp_norm, m_w_up, m_w_down, m_post_mlp_norm, v_pre_mix_norm, v_w_in, v_conv_w, v_attn_sinks, v_attn_group_norm, v_conv_group_norm, v_w_out, v_post_mix_norm, v_pre_mlp_norm, v_w_up, v_w_down, v_post_mlp_norm):
    xi, yi, ci = _mesh_pos()
    chip = 2 * xi + yi
    dev = 2 * chip + ci

    order = _peer_order(dev)
    pos = jnp.stack([ci, chip]).astype(jnp.int32)

    gw_in, gconv = _all_gather([w_in[0].astype(BF16), conv_w[0]], "gather_w_in")
    w_in_full = gw_in.transpose(1, 0, 2).reshape(D_MODEL, IN_COLS)
    conv_full = gconv.transpose(1, 0, 2).reshape(3, CONV_W)
    shards = [w_out[0].astype(BF16), w_up[0].astype(BF16), w_down[0].astype(BF16)]

    grad_x, dw_in, dw_out, dw_up, dw_down, smalls = _local_grads(
        x[0], loss_target[0], pre_mix_norm, w_in_full, conv_full, attn_sinks, attn_group_norm, conv_group_norm,
        post_mix_norm, pre_mlp_norm, post_mlp_norm, shards, order)

    grads = [_by_dest(dw_in, False), _by_dest(dw_out, True)]
    from_sibling = _sibling_exchange(grads, "reduce_sibling")
    summed = [_pair_sum(g, r, pos, ADAM_ROWS) for g, r in zip(grads, from_sibling)]
    from_chips = _chip_exchange(summed, "reduce_chips")

    small = _sum_devices(_all_gather([_pack_small(*smalls)], "gather_small")[0])
    loss = (0.5 / D_MODEL) * jnp.sum(small[0])

    big = {}
    for name, w, m, v, g, rs, rc in zip(("w_in", "w_out"), (w_in, w_out), (m_w_in, m_w_out), (v_w_in, v_w_out), grads,
                                        from_sibling, from_chips):
        big[name] = [a[None] for a in _adamw_shard(w[0], m[0], v[0], g, rs, rc, pos, ADAM_ROWS)]
    for name, w, m, v, (g, recv) in zip(("w_up", "w_down"), (w_up, w_down), (m_w_up, m_w_down), (v_w_up, v_w_down),
                                        (dw_up, dw_down)):
        big[name] = [a[None] for a in _adamw_scattered(w[0], m[0], v[0], g, recv, order, ADAM_ROWS)]

    conv_g = lax.dynamic_slice(
        jnp.stack([small[6, :CONV_W], small[6, CONV_W:], small[7, :CONV_W]]), (0, dev * (CONV_W // N_DEV)),
        (3, CONV_W // N_DEV))
    pad = lambda a, n: jnp.pad(a.reshape(1, -1), ((0, 0), (0, n - a.size)))
    small_names = ("pre_mix_norm", "post_mix_norm", "pre_mlp_norm", "post_mlp_norm")
    small_w = {"pre_mix_norm": (pre_mix_norm, m_pre_mix_norm, v_pre_mix_norm),
               "post_mix_norm": (post_mix_norm, m_post_mix_norm, v_post_mix_norm),
               "pre_mlp_norm": (pre_mlp_norm, m_pre_mlp_norm, v_pre_mlp_norm),
               "post_mlp_norm": (post_mlp_norm, m_post_mlp_norm, v_post_mlp_norm)}

    def pack(k):
        rows = [small_w[nm][k] for nm in small_names]
        rows.append(jnp.concatenate([(attn_group_norm, m_attn_group_norm, v_attn_group_norm)[k],
                                     (conv_group_norm, m_conv_group_norm, v_conv_group_norm)[k]], axis=1))
        rows.append(pad((conv_w, m_conv_w, v_conv_w)[k], D_MODEL))
        rows.append(pad((attn_sinks, m_attn_sinks, v_attn_sinks)[k], D_MODEL))
        rows.append(jnp.zeros((1, D_MODEL), F32))
        return jnp.concatenate(rows, axis=0)

    g_small = jnp.concatenate(
        [small[1:6], pad(conv_g, D_MODEL), pad(small[7, CONV_W:CONV_W + N_HEADS], D_MODEL), jnp.zeros((1, D_MODEL), F32)],
        axis=0)
    d_small, nm_small, nv_small = _adamw_small(pack(0), g_small, pack(1), pack(2))

    def unpack(a):
        nconv = 3 * CONV_W // N_DEV
        return {"pre_mix_norm": a[0:1], "post_mix_norm": a[1:2], "pre_mlp_norm": a[2:3], "post_mlp_norm": a[3:4],
                "attn_group_norm": a[4:5, :ATTN_W], "conv_group_norm": a[4:5, ATTN_W:],
                "conv_w": a[5, :nconv].reshape(1, 3, CONV_W // N_DEV), "attn_sinks": a[6:7, :N_HEADS]}

    order = ("pre_mix_norm", "w_in", "conv_w", "attn_sinks", "attn_group_norm", "conv_group_norm", "w_out",
             "post_mix_norm", "pre_mlp_norm", "w_up", "w_down", "post_mlp_norm")
    outs = []
    for k, a in enumerate((g_small, d_small, nm_small, nv_small)):
        sm = unpack(a)
        outs += [big[nm][k] if nm in big else sm[nm] for nm in order]
    return (loss, grad_x[None], *outs)
```

```python
import functools

import jax
import jax.numpy as jnp
import numpy as np
from jax import lax
from jax.experimental import pallas as pl
from jax.experimental.pallas import tpu as pltpu

F32 = jnp.float32
BF16 = jnp.bfloat16

D_MODEL = 1024
HEAD_DIM = 64
ATTN_W = 512
CONV_W = 512
N_HEADS = 8
N_KV = 2
GROUP = 4
KV_W = 128
QKV_W = ATTN_W + 2 * KV_W
GATES_W = 3 * CONV_W
IN_COLS = QKV_W + GATES_W
D_FF = 4096
FF_CHUNK = 512
N_FF_CHUNKS = D_FF // FF_CHUNK
BLOCK = 128
ROT_HALF = 8
ROPE_THETA = 500000.0
NORM_EPS = 1e-6
NEG_INF = -1e30
ATTN_SCALE = 0.125
N_DEV = 8
N_CHIPS = 4
IN_SHARD = IN_COLS // N_DEV

ADAM_LR = 0.001
ADAM_B1 = 0.9
ADAM_B2 = 0.999
ADAM_EPS = 1e-08
ADAM_WD = 0.01
ADAM_STEP = 10

V7X_VMEM_BYTES = 64 * 1024 * 1024
VMEM_LIMIT = V7X_VMEM_BYTES - 2 * 1024 * 1024

MESH = pl.DeviceIdType.MESH
HBM_SPEC = pl.BlockSpec(memory_space=pltpu.HBM)


def _params(*sem):
    return pltpu.CompilerParams(dimension_semantics=sem, vmem_limit_bytes=VMEM_LIMIT)


def _mm(a, b):
    return jnp.dot(a, b, preferred_element_type=F32)


def _mm_nt(a, b):
    return lax.dot_general(a, b, (((1,), (1,)), ((), ())), preferred_element_type=F32)


def _mm_tn(a, b):
    return lax.dot_general(a, b, (((0,), (0,)), ((), ())), preferred_element_type=F32)


def _inv_rms(x):
    return lax.rsqrt(jnp.mean(x * x, axis=-1, keepdims=True) + NORM_EPS)


def _rms_bwd(xhat, r, gain, dy):
    gy = dy * gain
    return r * (gy - xhat * jnp.mean(gy * xhat, axis=-1, keepdims=True)), dy * xhat


def _colsum(a):
    return jnp.sum(a, axis=0, keepdims=True)


def _full(shape):
    zeros = (0,) * len(shape)
    return pl.BlockSpec(shape, lambda *_: zeros)


def _resident(shape):
    zeros = (0,) * len(shape)
    return pl.BlockSpec(shape, lambda *_: zeros, pipeline_mode=pl.Buffered(1))


def _rope_tables(t):
    pos = np.arange(t, dtype=np.float32)
    inv_freq = (ROPE_THETA ** (-np.arange(0, 2 * ROT_HALF, 2, dtype=np.float64) / (2 * ROT_HALF))).astype(np.float32)
    ang = (pos[:, None] * inv_freq[None, :]).astype(np.float64)
    cos, sin = np.cos(ang).astype(np.float32), np.sin(ang).astype(np.float32)
    zeros8 = np.zeros((t, ROT_HALF), np.float32)
    rest = np.zeros((t, HEAD_DIM - 2 * ROT_HALF), np.float32)
    c_head = np.concatenate([cos, cos, rest + 1.0], axis=1)
    s1_head = np.concatenate([zeros8, sin, rest], axis=1)
    s2_head = np.concatenate([-sin, zeros8, rest], axis=1)
    two = lambda a: jnp.asarray(np.concatenate([a, a], axis=1))
    return two(c_head), two(s1_head), two(s2_head)


def _rope(v, c, s1, s2):
    return v * c + pltpu.roll(v, ROT_HALF, 1) * s1 + pltpu.roll(v, 128 - ROT_HALF, 1) * s2


def _rope_transpose(dv, c, s1, s2):
    return dv * c + pltpu.roll(dv * s1, 128 - ROT_HALF, 1) + pltpu.roll(dv * s2, ROT_HALF, 1)


def _shift_rows_down(u, prev, k):
    row = lax.broadcasted_iota(jnp.int32, u.shape, 0)
    out = pltpu.roll(u, k, 0)
    for r in range(k):
        out = jnp.where(row == r, prev[8 - k + r:8 - k + r + 1, :], out)
    return out


def _shift_rows_up(u, nxt, k):
    n = u.shape[0]
    row = lax.broadcasted_iota(jnp.int32, u.shape, 0)
    out = pltpu.roll(u, n - k, 0)
    for r in range(k):
        out = jnp.where(row == n - k + r, nxt[r:r + 1, :], out)
    return out


def _conv3(u, u1, u2, w):
    return (w[0:1, :] * u2 + w[1:2, :] * u1) + w[2:3, :] * u


def _mesh_pos():
    return lax.axis_index("x"), lax.axis_index("y"), lax.axis_index("c")


def _slot(ref, pos):
    dev = 4 * pos[0] + 2 * pos[1] + pos[2]
    if len(ref.shape) == 2:
        width = ref.shape[1] // N_DEV
        return ref.at[:, pl.ds(pl.multiple_of(dev * width, width), width)]
    return ref.at[dev]


def _gathered_shape(shard, by_cols):
    if by_cols:
        return jax.ShapeDtypeStruct((shard.shape[0], N_DEV * shard.shape[1]), shard.dtype)
    return jax.ShapeDtypeStruct((N_DEV,) + shard.shape, shard.dtype)


def _push(src, dst, sems, k, to):
    send_sems, recv_sems = sems
    return pltpu.make_async_remote_copy(src_ref=src, dst_ref=dst, send_sem=send_sems.at[k], recv_sem=recv_sems.at[k],
                                        device_id=to, device_id_type=MESH)


def _gather_near(first, last, shards, outs, sems, local_sems):
    x, y, c = _mesh_pos()
    me, peers = (x, y, c), [(x, y, 1 - c), (1 - x, y, c), (x, 1 - y, c)]
    n = len(shards)
    local = [pltpu.make_async_copy(shards[i], _slot(outs[i], me), local_sems.at[i]) for i in range(n)]
    sends = [_push(shards[i], _slot(outs[i], me), sems, 3 * i + k, peers[k]) for i in range(n) for k in range(3)]
    arrivals = [_push(shards[i], _slot(outs[i], peers[k]), sems, 3 * i + k, peers[k]) for i in range(n) for k in range(3)]

    @pl.when(first)
    def _():
        for cp in local + sends:
            cp.start()

    @pl.when(last)
    def _():
        for cp in sends:
            cp.wait_send()
        for cp in arrivals:
            cp.wait_recv()
        for cp in local:
            cp.wait()


def _gather_far(first, last, shards, ins, outs, sems):
    x, y, c = _mesh_pos()
    me, sibling = (x, y, c), (x, y, 1 - c)
    chips = [(1 - x, y), (x, 1 - y), (1 - x, 1 - y)]
    n = len(shards)
    diag_send = [_push(shards[i], _slot(outs[i], me), sems, 4 * i, (*chips[2], c)) for i in range(n)]
    diag_arrival = [_push(shards[i], _slot(outs[i], (*chips[2], c)), sems, 4 * i, (*chips[2], c)) for i in range(n)]
    passed = [[_push(_slot(ins[i], (*chips[j], c)), _slot(outs[i], (*chips[j], c)), sems, 4 * i + 1 + j, sibling)
               for i in range(n)] for j in range(3)]
    from_sibling = [_push(shards[i], _slot(outs[i], (*chips[j], 1 - c)), sems, 4 * i + 1 + j, sibling)
                    for i in range(n) for j in range(3)]

    @pl.when(first)
    def _():
        for cp in diag_send + passed[0] + passed[1]:
            cp.start()

    @pl.when(last)
    def _():
        for cp in diag_arrival:
            cp.wait_recv()
        for cp in passed[2]:
            cp.start()
        for cp in from_sibling:
            cp.wait_recv()
        for cp in diag_send + passed[0] + passed[1] + passed[2]:
            cp.wait_send()


def _in_proj_fwd(x, g1, w_in, conv_w, g_conv, rope, tm, shards, by_cols):
    t = x.shape[0]
    rc, rs1, rs2 = rope
    n = len(shards)

    def body(*refs):
        x_ref, g1_ref, w_ref, cw_ref, gc_ref, c_ref, s1_ref, s2_ref = refs[:8]
        shard_refs = refs[8:8 + n]
        qkv_ref, gates_ref, mconv_ref, w_full_ref = refs[8 + n:12 + n]
        gathered = refs[12 + n:12 + 2 * n]
        carry_ref = refs[12 + 2 * n]
        if n:
            step = pl.program_id(0)
            _gather_near(step == 0, step == pl.num_programs(0) - 1, shard_refs, gathered, refs[13 + 2 * n:15 + 2 * n],
                         refs[15 + 2 * n])

        @pl.when(pl.program_id(0) == 0)
        def _():
            carry_ref[...] = jnp.zeros_like(carry_ref)
            for d in range(N_DEV):
                w_full_ref[:, IN_SHARD * d:IN_SHARD * (d + 1)] = w_ref[d]

        xv = x_ref[...]
        hn = ((xv * _inv_rms(xv)) * g1_ref[...]).astype(BF16)
        proj = _mm(hn, w_full_ref[...])
        c, s1, s2 = c_ref[...], s1_ref[...], s2_ref[...]
        for ci in range((ATTN_W + KV_W) // 128):
            sl = slice(128 * ci, 128 * (ci + 1))
            qkv_ref[:, sl] = _rope(proj[:, sl], c, s1, s2).astype(BF16)
        qkv_ref[:, ATTN_W + KV_W:QKV_W] = proj[:, ATTN_W + KV_W:QKV_W].astype(BF16)
        gates = proj[:, QKV_W:]
        gates_ref[...] = gates
        gb, gcc, xin = gates[:, :CONV_W], gates[:, CONV_W:2 * CONV_W], gates[:, 2 * CONV_W:]
        u = gcc * xin
        prev = carry_ref[...]
        conv = gb * _conv3(u, _shift_rows_down(u, prev, 1), _shift_rows_down(u, prev, 2), cw_ref[...])
        carry_ref[...] = u[tm - 8:tm, :]
        mconv_ref[...] = ((conv * _inv_rms(conv)) * gc_ref[...]).astype(BF16)

    tile = lambda w_: pl.BlockSpec((tm, w_), lambda i: (i, 0))
    comm_scratch = [pltpu.SemaphoreType.DMA((3 * n,)), pltpu.SemaphoreType.DMA((3 * n,)), pltpu.SemaphoreType.DMA((n,))]
    res = pl.pallas_call(
        body, name="in_proj_fwd", grid=(t // tm,),
        in_specs=[tile(D_MODEL), _full((1, D_MODEL)), _resident((N_DEV, D_MODEL, IN_SHARD)), _full((3, CONV_W)),
                  _full((1, CONV_W)), tile(128), tile(128), tile(128)] + [HBM_SPEC] * n,
        out_specs=[tile(QKV_W), tile(GATES_W), tile(CONV_W), _full((D_MODEL, IN_COLS))] + [HBM_SPEC] * n,
        out_shape=[jax.ShapeDtypeStruct((t, QKV_W), BF16), jax.ShapeDtypeStruct((t, GATES_W), F32),
                   jax.ShapeDtypeStruct((t, CONV_W), BF16), jax.ShapeDtypeStruct((D_MODEL, IN_COLS), BF16)]
        + [_gathered_shape(s, cols) for s, cols in zip(shards, by_cols)],
        scratch_shapes=[pltpu.VMEM((8, CONV_W), F32)] + (comm_scratch if n else []),
        compiler_params=_params("arbitrary"),
    )(x, g1, w_in, conv_w, g_conv, rc, rs1, rs2, *shards)
    return res[0], res[1], res[2], res[3], list(res[4:])


GROUP_ROWS = GROUP * BLOCK


def _attn_mask(has_prev):
    row = lax.broadcasted_iota(jnp.int32, (GROUP_ROWS, 2 * BLOCK), 0) & (BLOCK - 1)
    col = lax.broadcasted_iota(jnp.int32, (GROUP_ROWS, 2 * BLOCK), 1)
    band = (col > row) & (col <= row + BLOCK)
    return band if has_prev is True else band & ((col >= BLOCK) | has_prev)


def _stack_heads(a, g):
    return jnp.concatenate([a[:, HEAD_DIM * (GROUP * g + hh):HEAD_DIM * (GROUP * g + hh + 1)] for hh in range(GROUP)], axis=0)


def _unstack_heads(a):
    return jnp.concatenate([a[BLOCK * hh:BLOCK * (hh + 1), :] for hh in range(GROUP)], axis=1)


def _group_sinks(sink_ref, g):
    head = lax.broadcasted_iota(jnp.int32, (GROUP_ROWS, 1), 0) // BLOCK
    out = jnp.full((GROUP_ROWS, 1), sink_ref[0, GROUP * g], F32)
    for hh in range(1, GROUP):
        out = jnp.where(head == hh, sink_ref[0, GROUP * g + hh], out)
    return out


def _attn_probs(qs, kk, sink, valid):
    s = jnp.where(valid, _mm_nt(qs, kk) * ATTN_SCALE, NEG_INF)
    m = jnp.maximum(jnp.max(s, axis=-1, keepdims=True), sink)
    p = jnp.exp(s - m)
    psink = jnp.exp(sink - m)
    inv_l = 1.0 / (jnp.sum(p, axis=-1, keepdims=True) + psink)
    return p * inv_l, psink * inv_l


ATTN_STEP_BLOCKS = 4
ATTN_STEP = ATTN_STEP_BLOCKS * BLOCK
ATTN_KEYS = ATTN_STEP + BLOCK


def _qkv_specs(order):
    prev = lambda i: jnp.maximum(ATTN_STEP_BLOCKS * order(i) - 1, 0)
    kcol, vcol = ATTN_W // KV_W, ATTN_W // KV_W + 1
    return [pl.BlockSpec((ATTN_STEP, ATTN_W), lambda i: (order(i), 0)),
            pl.BlockSpec((BLOCK, KV_W), lambda i: (prev(i), kcol)), pl.BlockSpec((ATTN_STEP, KV_W), lambda i: (order(i), kcol)),
            pl.BlockSpec((BLOCK, KV_W), lambda i: (prev(i), vcol)), pl.BlockSpec((ATTN_STEP, KV_W), lambda i: (order(i), vcol))]


def _attn_fwd(qkv, sinks, g_attn, shards, gathered):
    t = qkv.shape[0]
    n = len(shards)

    def body(*refs):
        sink_ref, q_ref, kp_ref, kc_ref, vp_ref, vc_ref, ga_ref = refs[:7]
        attn_ref, mattn_ref = refs[7 + 2 * n:9 + 2 * n]
        step = pl.program_id(0)
        if n:
            _gather_far(step == 0, step == pl.num_programs(0) - 1, refs[7:7 + n], refs[7 + n:7 + 2 * n],
                        refs[9 + 2 * n:9 + 3 * n], refs[9 + 3 * n:11 + 3 * n])
        q = q_ref[...]
        keys = jnp.concatenate([kp_ref[...], kc_ref[...]], axis=0)
        vals = jnp.concatenate([vp_ref[...], vc_ref[...]], axis=0)
        sink = [_group_sinks(sink_ref, g) for g in range(N_KV)]
        gain = ga_ref[...]
        for b in range(ATTN_STEP_BLOCKS):
            rows, window = slice(BLOCK * b, BLOCK * (b + 1)), slice(BLOCK * b, BLOCK * (b + 2))
            valid = _attn_mask(True if b else step > 0)
            outs = []
            for g in range(N_KV):
                gs = slice(HEAD_DIM * g, HEAD_DIM * (g + 1))
                probs, _ = _attn_probs(_stack_heads(q[rows], g), keys[window, gs], sink[g], valid)
                outs.append(_unstack_heads(_mm(probs.astype(BF16), vals[window, gs])))
            attn = jnp.concatenate(outs, axis=1)
            attn_ref[rows, :] = attn
            mattn_ref[rows, :] = ((attn * _inv_rms(attn)) * gain).astype(BF16)

    blk = pl.BlockSpec((ATTN_STEP, ATTN_W), lambda j: (j, 0))
    res = pl.pallas_call(
        body, name="attn_fwd", grid=(t // ATTN_STEP,),
        in_specs=[pl.BlockSpec(memory_space=pltpu.SMEM)] + _qkv_specs(lambda j: j) + [_full((1, ATTN_W))]
        + [HBM_SPEC] * (2 * n),
        out_specs=[blk, blk] + [HBM_SPEC] * n,
        out_shape=[jax.ShapeDtypeStruct((t, ATTN_W), F32), jax.ShapeDtypeStruct((t, ATTN_W), BF16)]
        + [jax.ShapeDtypeStruct(g.shape, g.dtype) for g in gathered],
        input_output_aliases={7 + n + i: 2 + i for i in range(n)},
        scratch_shapes=[pltpu.SemaphoreType.DMA((4 * n,)), pltpu.SemaphoreType.DMA((4 * n,))] if n else [],
        compiler_params=_params("arbitrary"),
    )(sinks, qkv, qkv, qkv, qkv, qkv, g_attn, *shards, *gathered)
    return res[0], res[1], list(res[2:])


SMALL_ROWS = 8
ROW_LOSS, ROW_G2, ROW_G3, ROW_G4 = 0, 1, 2, 3


def _mid(mattn, mconv, x, target, g2, g3, g4, w_out, w_up, w_down, tm):
    t = x.shape[0]

    def body(ma_ref, mc_ref, x_ref, t_ref, g2_ref, g3_ref, g4_ref, wo_ref, wu_ref, wd_ref,
             actt_ref, dup_ref, hn2t_ref, dmo_ref, dmix_ref, dh_ref, dmixed_ref, small_ref, up_ref):
        @pl.when(pl.program_id(0) == 0)
        def _():
            small_ref[...] = jnp.zeros_like(small_ref)

        g2, g3, g4 = g2_ref[...], g3_ref[...], g4_ref[...]
        mix_out = _mm(ma_ref[...], wo_ref[0:ATTN_W, :]) + _mm(mc_ref[...], wo_ref[ATTN_W:, :])
        r2 = _inv_rms(mix_out)
        mo_hat = mix_out * r2
        h = x_ref[...] + mo_hat * g2
        r3 = _inv_rms(h)
        h_hat = h * r3
        hn2 = (h_hat * g3).astype(BF16)
        hn2t_ref[...] = hn2.T
        up = jnp.maximum(_mm(hn2, wu_ref[...]), 0.0)
        up_ref[...] = up.astype(BF16)
        act = (up * up).astype(BF16)
        actt_ref[...] = act.T
        mlp = _mm(act, wd_ref[...])
        r4 = _inv_rms(mlp)
        ml_hat = mlp * r4
        err = (h + ml_hat * g4) - t_ref[...]
        d_out = err * (1.0 / D_MODEL)
        d_mlp, dg4 = _rms_bwd(ml_hat, r4, g4, d_out)
        dmo = d_mlp.astype(BF16)
        dmo_ref[...] = dmo
        dup = (_mm_nt(dmo, wd_ref[...]) * (2.0 * up_ref[...].astype(F32))).astype(BF16)
        dup_ref[...] = dup
        dhn2 = _mm_nt(dup, wu_ref[...])
        dh_norm, dg3 = _rms_bwd(h_hat, r3, g3, dhn2)
        dh = d_out + dh_norm
        dh_ref[...] = dh
        d_mix, dg2 = _rms_bwd(mo_hat, r2, g2, dh)
        dmix = d_mix.astype(BF16)
        dmix_ref[...] = dmix
        dmixed_ref[...] = _mm_nt(dmix, wo_ref[...])
        small_ref[ROW_LOSS:ROW_LOSS + 1, :] += _colsum(err * err)
        small_ref[ROW_G2:ROW_G2 + 1, :] += _colsum(dg2)
        small_ref[ROW_G3:ROW_G3 + 1, :] += _colsum(dg3)
        small_ref[ROW_G4:ROW_G4 + 1, :] += _colsum(dg4)

    tile = lambda n: pl.BlockSpec((tm, n), lambda i: (i, 0))
    cols = lambda n: pl.BlockSpec((n, tm), lambda i: (0, i))
    gain = _full((1, D_MODEL))
    return pl.pallas_call(
        body, name="mid_fwd_bwd", grid=(t // tm,),
        in_specs=[tile(ATTN_W), tile(CONV_W), tile(D_MODEL), tile(D_MODEL), gain, gain, gain,
                  _resident((D_MODEL, D_MODEL)), _resident((D_MODEL, D_FF)), _resident((D_FF, D_MODEL))],
        out_specs=[cols(D_FF), tile(D_FF), cols(D_MODEL), tile(D_MODEL), tile(D_MODEL), tile(D_MODEL), tile(D_MODEL),
                   _full((SMALL_ROWS, D_MODEL))],
        out_shape=[jax.ShapeDtypeStruct((D_FF, t), BF16), jax.ShapeDtypeStruct((t, D_FF), BF16),
                   jax.ShapeDtypeStruct((D_MODEL, t), BF16), jax.ShapeDtypeStruct((t, D_MODEL), BF16),
                   jax.ShapeDtypeStruct((t, D_MODEL), BF16), jax.ShapeDtypeStruct((t, D_MODEL), F32),
                   jax.ShapeDtypeStruct((t, D_MODEL), F32), jax.ShapeDtypeStruct((SMALL_ROWS, D_MODEL), F32)],
        scratch_shapes=[pltpu.VMEM((tm, D_FF), BF16)],
        compiler_params=_params("arbitrary"),
    )(mattn, mconv, x, target, g2, g3, g4, w_out, w_up, w_down)


CHIP_FLIPS = ((1, 1), (1, 0), (0, 1))


def _block_order(dev):
    chip_masks = [4 * fx + 2 * fy for fx, fy in CHIP_FLIPS]
    masks = [m + 1 for m in chip_masks] + [1] + chip_masks + [0]
    return jnp.bitwise_xor(dev, jnp.asarray(masks, jnp.int32)).astype(jnp.int32)


def _other_chips(x, y, c):
    return [(1 - x if fx else x, 1 - y if fy else y, c) for fx, fy in CHIP_FLIPS]


def _dw_pair_sums(at, b, order, at_chunked, name):
    t = b.shape[0]
    rows, cols = (FF_CHUNK, D_MODEL) if at_chunked else (D_MODEL, FF_CHUNK)
    n_far = len(CHIP_FLIPS)

    def body(order_ref, a_ref, b_ref, own_ref, from_sib_ref, pair_ref, send_buf, land_buf, send_sems, recv_sems):
        s_now = pl.program_id(0)
        x, y, c = _mesh_pos()
        sibling = (x, y, 1 - c)
        sems = (send_sems, recv_sems)

        def hand_over(k):
            dst = land_buf.at[k] if k < n_far else from_sib_ref
            return _push(send_buf.at[k], dst, sems, k, sibling)

        block = _mm(a_ref[...], b_ref[...])
        for k in range(n_far + 1):
            @pl.when(s_now == k)
            def _():
                send_buf[k] = block.astype(BF16)
                hand_over(k).start()

        for k in range(n_far):
            @pl.when(s_now == n_far + 1 + k)
            def _():
                hand_over(k).wait_recv()
                pair_ref[...] = (block + land_buf[k].astype(F32)).astype(BF16)

        @pl.when(s_now == N_DEV - 1)
        def _():
            own_ref[...] = block
            for k in range(n_far + 1):
                hand_over(k).wait_send()
            hand_over(n_far).wait_recv()

    if at_chunked:
        in_specs = [pl.BlockSpec((FF_CHUNK, t), lambda s, order_ref: (order_ref[s], 0)), _resident((t, D_MODEL))]
    else:
        in_specs = [_resident((D_MODEL, t)), pl.BlockSpec((t, FF_CHUNK), lambda s, order_ref: (0, order_ref[s]))]
    return pl.pallas_call(
        body, name=name,
        grid_spec=pltpu.PrefetchScalarGridSpec(
            num_scalar_prefetch=1, grid=(N_DEV,), in_specs=in_specs,
            out_specs=[pl.BlockSpec((rows, cols), lambda s, order_ref: (0, 0)), HBM_SPEC,
                       pl.BlockSpec((None, rows, cols), lambda s, order_ref: (jnp.clip(s - n_far - 1, 0, n_far - 1), 0, 0))],
            scratch_shapes=[pltpu.VMEM((n_far + 1, rows, cols), BF16), pltpu.VMEM((n_far, rows, cols), BF16),
                            pltpu.SemaphoreType.DMA((n_far + 1,)), pltpu.SemaphoreType.DMA((n_far + 1,))]),
        out_shape=[jax.ShapeDtypeStruct((rows, cols), F32), jax.ShapeDtypeStruct((rows, cols), BF16),
                   jax.ShapeDtypeStruct((n_far, rows, cols), BF16)],
        compiler_params=_params("arbitrary"),
    )(order, at, b)


def _chip_exchange_beside(first, last, sums, outs, sems):
    chips = _other_chips(*_mesh_pos())
    copies = [_push(sums[i].at[k], outs[i].at[k], sems, len(chips) * i + k, chip)
              for i in range(len(sums)) for k, chip in enumerate(chips)]

    @pl.when(first)
    def _():
        for cp in copies:
            cp.start()

    @pl.when(last)
    def _():
        for cp in copies:
            cp.wait()


def _dw_out(mattn, mconv, dmix, tk):
    t = dmix.shape[0]

    def body(ma_ref, mc_ref, b_ref, o_ref):
        @pl.when(pl.program_id(0) == 0)
        def _():
            o_ref[...] = jnp.zeros_like(o_ref)
        b = b_ref[...]
        o_ref[0:ATTN_W, :] += _mm_tn(ma_ref[...], b)
        o_ref[ATTN_W:, :] += _mm_tn(mc_ref[...], b)

    tile = lambda n: pl.BlockSpec((tk, n), lambda k: (k, 0))
    return pl.pallas_call(
        body, name="dw_out", grid=(t // tk,),
        in_specs=[tile(ATTN_W), tile(CONV_W), tile(D_MODEL)],
        out_specs=_full((D_MODEL, D_MODEL)),
        out_shape=jax.ShapeDtypeStruct((D_MODEL, D_MODEL), F32),
        compiler_params=_params("arbitrary"),
    )(mattn, mconv, dmix)


ROW_GATTN, ROW_GCONV, ROW_CW0 = 0, 1, 2


def _mix_bwd(dmixed, attn, gates, g_attn, g_conv, conv_w, tm, sums):
    t = attn.shape[0]
    n = t // tm
    rev = lambda i: n - 1 - i

    def body(dm_ref, attn_ref, gates_ref, gprev_ref, ga_ref, gc_ref, cw_ref, sums_ref, dattn_ref, dgates_ref, small_ref,
             arrived_ref, carry_ref, send_sems, recv_sems):
        i = pl.program_id(0)
        _chip_exchange_beside(i == 0, i == n - 1, [sums_ref], [arrived_ref], (send_sems, recv_sems))

        @pl.when(i == 0)
        def _():
            small_ref[...] = jnp.zeros_like(small_ref)
            carry_ref[...] = jnp.zeros_like(carry_ref)

        dm = dm_ref[...]
        a = attn_ref[...]
        ra = _inv_rms(a)
        a_hat = a * ra
        dattn, dga = _rms_bwd(a_hat, ra, ga_ref[...], dm[:, :ATTN_W])
        dattn_ref[...] = dattn

        gates = gates_ref[...]
        gb, gcc, xin = gates[:, :CONV_W], gates[:, CONV_W:2 * CONV_W], gates[:, 2 * CONV_W:]
        u = gcc * xin
        gp = gprev_ref[...]
        uprev = jnp.where(rev(i) == 0, 0.0, gp[:, CONV_W:2 * CONV_W] * gp[:, 2 * CONV_W:])
        u1, u2 = _shift_rows_down(u, uprev, 1), _shift_rows_down(u, uprev, 2)
        w = cw_ref[...]
        c = _conv3(u, u1, u2, w)
        conv = gb * c
        rcv = _inv_rms(conv)
        c_hat = conv * rcv
        dconv, dgc = _rms_bwd(c_hat, rcv, gc_ref[...], dm[:, ATTN_W:])
        dc = dconv * gb
        nxt = carry_ref[...]
        du = (w[2:3, :] * dc + w[1:2, :] * _shift_rows_up(dc, nxt, 1)) + w[0:1, :] * _shift_rows_up(dc, nxt, 2)
        carry_ref[...] = dc[0:8, :]
        dgates_ref[:, :CONV_W] = (dconv * c).astype(BF16)
        dgates_ref[:, CONV_W:2 * CONV_W] = (du * xin).astype(BF16)
        dgates_ref[:, 2 * CONV_W:] = (du * gcc).astype(BF16)
        small_ref[ROW_GATTN:ROW_GATTN + 1, :] += _colsum(dga)
        small_ref[ROW_GCONV:ROW_GCONV + 1, :] += _colsum(dgc)
        small_ref[ROW_CW0:ROW_CW0 + 1, :] += _colsum(dc * u2)
        small_ref[ROW_CW0 + 1:ROW_CW0 + 2, :] += _colsum(dc * u1)
        small_ref[ROW_CW0 + 2:ROW_CW0 + 3, :] += _colsum(dc * u)

    tile = lambda w_: pl.BlockSpec((tm, w_), lambda i: (rev(i), 0))
    prev8 = pl.BlockSpec((8, GATES_W), lambda i: (jnp.maximum(rev(i) * (tm // 8) - 1, 0), 0))
    return pl.pallas_call(
        body, name="mix_bwd", grid=(n,),
        in_specs=[tile(D_MODEL), tile(ATTN_W), tile(GATES_W), prev8, _full((1, ATTN_W)), _full((1, CONV_W)),
                  _full((3, CONV_W)), HBM_SPEC],
        out_specs=[tile(ATTN_W), tile(GATES_W), _full((SMALL_ROWS, CONV_W)), HBM_SPEC],
        out_shape=[jax.ShapeDtypeStruct((t, ATTN_W), F32), jax.ShapeDtypeStruct((t, GATES_W), BF16),
                   jax.ShapeDtypeStruct((SMALL_ROWS, CONV_W), F32), jax.ShapeDtypeStruct(sums.shape, sums.dtype)],
        scratch_shapes=[pltpu.VMEM((8, CONV_W), F32), pltpu.SemaphoreType.DMA((len(CHIP_FLIPS),)),
                        pltpu.SemaphoreType.DMA((len(CHIP_FLIPS),))],
        compiler_params=_params("arbitrary"),
    )(dmixed, attn, gates, gates, g_attn, g_conv, conv_w, sums)


def _attn_bwd(qkv, dattn, sinks, rope, sums):
    t = qkv.shape[0]
    n_steps = t // ATTN_STEP
    rev = lambda i: n_steps - 1 - i
    rc, rs1, rs2 = rope

    def body(sink_ref, q_ref, kp_ref, kc_ref, vp_ref, vc_ref, do_ref, c_ref, s1_ref, s2_ref, sums_ref,
             dqkv_ref, dsink_ref, arrived_ref, ck_ref, cv_ref, kacc_ref, vacc_ref, send_sems, recv_sems):
        i = pl.program_id(0)
        _chip_exchange_beside(i == 0, i == n_steps - 1, [sums_ref], [arrived_ref], (send_sems, recv_sems))

        @pl.when(i == 0)
        def _():
            dsink_ref[...] = jnp.zeros_like(dsink_ref)
            ck_ref[...] = jnp.zeros_like(ck_ref)
            cv_ref[...] = jnp.zeros_like(cv_ref)

        kacc_ref[...] = jnp.zeros_like(kacc_ref)
        vacc_ref[...] = jnp.zeros_like(vacc_ref)
        q = q_ref[...]
        dout = do_ref[...].astype(BF16)
        keys = jnp.concatenate([kp_ref[...], kc_ref[...]], axis=0)
        vals = jnp.concatenate([vp_ref[...], vc_ref[...]], axis=0)
        sink = [_group_sinks(sink_ref, g) for g in range(N_KV)]
        c, s1, s2 = c_ref[...], s1_ref[...], s2_ref[...]
        lane = lax.broadcasted_iota(jnp.int32, (1, 128), 1)
        dsink = jnp.zeros((1, 128), F32)
        for b in range(ATTN_STEP_BLOCKS):
            rows, window = slice(BLOCK * b, BLOCK * (b + 1)), slice(BLOCK * b, BLOCK * (b + 2))
            valid = _attn_mask(True if b else rev(i) > 0)
            dq_parts, dk_parts, dv_parts = [], [], []
            for g in range(N_KV):
                gs = slice(HEAD_DIM * g, HEAD_DIM * (g + 1))
                kk, vv = keys[window, gs], vals[window, gs]
                qs, dos = _stack_heads(q[rows], g), _stack_heads(dout[rows], g)
                probs, psink = _attn_probs(qs, kk, sink[g], valid)
                dp = _mm_nt(dos, vv)
                delta = jnp.sum(probs * dp, axis=-1, keepdims=True)
                ds = (probs * (dp - delta) * ATTN_SCALE).astype(BF16)
                sink_terms = psink * delta
                for hh in range(GROUP):
                    head_sum = jnp.sum(sink_terms[BLOCK * hh:BLOCK * (hh + 1), :])
                    dsink = dsink + jnp.where(lane == GROUP * g + hh, -head_sum, 0.0)
                dq_parts.append(_unstack_heads(_mm(ds, kk)))
                dk_parts.append(_mm_tn(ds, qs))
                dv_parts.append(_mm_tn(probs.astype(BF16), dos))
            kacc_ref[window, :] += jnp.concatenate(dk_parts, axis=1)
            vacc_ref[window, :] += jnp.concatenate(dv_parts, axis=1)
            dq = jnp.concatenate(dq_parts, axis=1)
            for ci in range(ATTN_W // 128):
                sl = slice(128 * ci, 128 * (ci + 1))
                dqkv_ref[rows, sl] = _rope_transpose(dq[:, sl], c[rows], s1[rows], s2[rows]).astype(BF16)
        kacc_ref[ATTN_STEP:, :] += ck_ref[...]
        vacc_ref[ATTN_STEP:, :] += cv_ref[...]
        ck_ref[...] = kacc_ref[:BLOCK, :]
        cv_ref[...] = vacc_ref[:BLOCK, :]
        dqkv_ref[:, ATTN_W:ATTN_W + KV_W] = _rope_transpose(kacc_ref[BLOCK:, :], c, s1, s2).astype(BF16)
        dqkv_ref[:, ATTN_W + KV_W:] = vacc_ref[BLOCK:, :].astype(BF16)
        dsink_ref[0:1, :] += dsink

    blk = lambda w_: pl.BlockSpec((ATTN_STEP, w_), lambda i: (rev(i), 0))
    return pl.pallas_call(
        body, name="attn_bwd", grid=(n_steps,),
        in_specs=[pl.BlockSpec(memory_space=pltpu.SMEM)] + _qkv_specs(rev) + [blk(ATTN_W), blk(128), blk(128), blk(128),
                                                                              HBM_SPEC],
        out_specs=[blk(QKV_W), _full((8, 128)), HBM_SPEC],
        out_shape=[jax.ShapeDtypeStruct((t, QKV_W), BF16), jax.ShapeDtypeStruct((8, 128), F32),
                   jax.ShapeDtypeStruct(sums.shape, sums.dtype)],
        scratch_shapes=[pltpu.VMEM((BLOCK, KV_W), F32), pltpu.VMEM((BLOCK, KV_W), F32),
                        pltpu.VMEM((ATTN_KEYS, KV_W), F32), pltpu.VMEM((ATTN_KEYS, KV_W), F32),
                        pltpu.SemaphoreType.DMA((len(CHIP_FLIPS),)), pltpu.SemaphoreType.DMA((len(CHIP_FLIPS),))],
        compiler_params=_params("arbitrary"),
    )(sinks, qkv, qkv, qkv, qkv, qkv, dattn, rc, rs1, rs2, sums)


def _in_proj_bwd(dqkv, dgates, x, dh, g1, w_in, tm):
    t = x.shape[0]

    def body(dq_ref, dg_ref, x_ref, dh_ref, g1_ref, w_ref, dx_ref, dwa_ref, dwb_ref, dg1_ref):
        @pl.when(pl.program_id(0) == 0)
        def _():
            dwa_ref[...] = jnp.zeros_like(dwa_ref)
            dwb_ref[...] = jnp.zeros_like(dwb_ref)
            dg1_ref[...] = jnp.zeros_like(dg1_ref)

        dq, dg = dq_ref[...], dg_ref[...]
        dhn = _mm_nt(dq, w_ref[:, :QKV_W]) + _mm_nt(dg, w_ref[:, QKV_W:])
        xv = x_ref[...]
        r = _inv_rms(xv)
        x_hat = xv * r
        g1 = g1_ref[...]
        dx, dg1 = _rms_bwd(x_hat, r, g1, dhn)
        dx_ref[...] = dh_ref[...] + dx
        hn = (x_hat * g1).astype(BF16)
        dwa_ref[...] += _mm_tn(hn, dq)
        dwb_ref[...] += _mm_tn(hn, dg)
        dg1_ref[0:1, :] += _colsum(dg1)

    tile = lambda n: pl.BlockSpec((tm, n), lambda i: (i, 0))
    return pl.pallas_call(
        body, name="in_proj_bwd", grid=(t // tm,),
        in_specs=[tile(QKV_W), tile(GATES_W), tile(D_MODEL), tile(D_MODEL), _full((1, D_MODEL)),
                  _resident((D_MODEL, IN_COLS))],
        out_specs=[tile(D_MODEL), _full((D_MODEL, QKV_W)), _full((D_MODEL, GATES_W)), _full((SMALL_ROWS, D_MODEL))],
        out_shape=[jax.ShapeDtypeStruct((t, D_MODEL), F32), jax.ShapeDtypeStruct((D_MODEL, QKV_W), F32),
                   jax.ShapeDtypeStruct((D_MODEL, GATES_W), F32), jax.ShapeDtypeStruct((SMALL_ROWS, D_MODEL), F32)],
        compiler_params=_params("arbitrary"),
    )(dqkv, dgates, x, dh, g1, w_in)


def _all_gather(shards, name):
    n = len(shards)

    def body(*refs):
        ins, outs = refs[:n], refs[n:2 * n]
        send_sems, recv_sems, local_sems = refs[2 * n:]
        x, y, c = _mesh_pos()
        me, sibling = (x, y, c), (x, y, 1 - c)
        chips = [(1 - x, y), (x, 1 - y), (1 - x, 1 - y)]

        def copy(i, k, block, to, src=None):
            dst = outs[i].at[4 * block[0] + 2 * block[1] + block[2]]
            return pltpu.make_async_remote_copy(
                src_ref=dst if src is None else src, dst_ref=dst, send_sem=send_sems.at[7 * i + k],
                recv_sem=recv_sems.at[7 * i + k], device_id=to, device_id_type=MESH)

        mine = [pltpu.make_async_copy(ins[i], outs[i].at[4 * x + 2 * y + c], local_sems.at[i]) for i in range(n)]
        for cp in mine:
            cp.start()
        first = []
        for i in range(n):
            first.append(copy(i, 0, me, sibling, src=ins[i]))
            first += [copy(i, 1 + j, me, (*chip, c), src=ins[i]) for j, chip in enumerate(chips)]
        for cp in first:
            cp.start()
        passed = []
        for j, chip in enumerate(chips):
            for i in range(n):
                copy(i, 1 + j, (*chip, c), me).wait_recv()
                cp = copy(i, 4 + j, (*chip, c), sibling)
                cp.start()
                passed.append(cp)
        for i in range(n):
            copy(i, 0, sibling, me).wait_recv()
            for j, chip in enumerate(chips):
                copy(i, 4 + j, (*chip, 1 - c), me).wait_recv()
        for cp in first + passed:
            cp.wait_send()
        for cp in mine:
            cp.wait()

    return pl.pallas_call(
        body, name=name,
        in_specs=[HBM_SPEC] * n, out_specs=[HBM_SPEC] * n,
        out_shape=[jax.ShapeDtypeStruct((N_DEV,) + s.shape, s.dtype) for s in shards],
        scratch_shapes=[pltpu.SemaphoreType.DMA((7 * n,)), pltpu.SemaphoreType.DMA((7 * n,)),
                        pltpu.SemaphoreType.DMA((n,))],
    )(*shards)


def _sibling_exchange(grads, name):
    n = len(grads)

    def body(*refs):
        ins, outs = refs[:n], refs[n:2 * n]
        send_sems, recv_sems = refs[2 * n:]
        x, y, c = _mesh_pos()
        copies = [pltpu.make_async_remote_copy(
            src_ref=ins[i].at[1 - c], dst_ref=outs[i], send_sem=send_sems.at[i], recv_sem=recv_sems.at[i],
            device_id=(x, y, 1 - c), device_id_type=MESH) for i in range(n)]
        for cp in copies:
            cp.start()
        for cp in copies:
            cp.wait()

    return pl.pallas_call(
        body, name=name,
        in_specs=[HBM_SPEC] * n, out_specs=[HBM_SPEC] * n,
        out_shape=[jax.ShapeDtypeStruct(g.shape[1:], g.dtype) for g in grads],
        scratch_shapes=[pltpu.SemaphoreType.DMA((n,)), pltpu.SemaphoreType.DMA((n,))],
    )(*grads)


def _chip_exchange(sums, name):
    n = len(sums)

    def body(*refs):
        ins, outs = refs[:n], refs[n:2 * n]
        send_sems, recv_sems = refs[2 * n:]
        x, y, c = _mesh_pos()
        chips = [(1 - x, y), (x, 1 - y), (1 - x, 1 - y)]
        copies = [pltpu.make_async_remote_copy(
            src_ref=ins[i].at[2 * chip[0] + chip[1]], dst_ref=outs[i].at[k], send_sem=send_sems.at[3 * i + k],
            recv_sem=recv_sems.at[3 * i + k], device_id=(*chip, c), device_id_type=MESH)
            for i in range(n) for k, chip in enumerate(chips)]
        for cp in copies:
            cp.start()
        for cp in copies:
            cp.wait()

    return pl.pallas_call(
        body, name=name,
        in_specs=[HBM_SPEC] * n, out_specs=[HBM_SPEC] * n,
        out_shape=[jax.ShapeDtypeStruct((3,) + s.shape[1:], s.dtype) for s in sums],
        scratch_shapes=[pltpu.SemaphoreType.DMA((3 * n,)), pltpu.SemaphoreType.DMA((3 * n,))],
    )(*sums)


def _pair_sum(grad, recv, pos, tr):
    _, _, rows, cols = grad.shape

    def body(pos_ref, g_ref, r_ref, sb_ref):
        sb_ref[...] = (g_ref[...] + r_ref[...]).astype(BF16)

    return pl.pallas_call(
        body, name="pair_sum",
        grid_spec=pltpu.PrefetchScalarGridSpec(
            num_scalar_prefetch=1, grid=(N_CHIPS, rows // tr),
            in_specs=[pl.BlockSpec((None, None, tr, cols), lambda p, i, pos: (pos[0], p, i, 0)),
                      pl.BlockSpec((None, tr, cols), lambda p, i, pos: (p, i, 0))],
            out_specs=pl.BlockSpec((None, tr, cols), lambda p, i, pos: (p, i, 0))),
        out_shape=jax.ShapeDtypeStruct((N_CHIPS, rows, cols), BF16),
        compiler_params=_params("parallel", "parallel"),
    )(pos, grad, recv)


def _adam_math(w, g, m, v):
    m = ADAM_B1 * m + (1.0 - ADAM_B1) * g
    v = ADAM_B2 * v + (1.0 - ADAM_B2) * (g * g)
    m_hat = m / (1.0 - ADAM_B1 ** ADAM_STEP)
    v_hat = v / (1.0 - ADAM_B2 ** ADAM_STEP)
    delta = -ADAM_LR * (m_hat / (jnp.sqrt(v_hat) + ADAM_EPS) + ADAM_WD * w)
    return delta, m, v


def _adamw_shard(w, m, v, grad, from_sibling, from_chips, pos, tr):
    rows, cols = w.shape

    def body(pos_ref, w_ref, m_ref, v_ref, own_ref, sib_ref, r_ref, g_ref, d_ref, nm_ref, nv_ref):
        g = own_ref[...] + sib_ref[...]
        for k in range(3):
            g = g + r_ref[k].astype(F32)
        g_ref[...] = g
        d_ref[...], nm_ref[...], nv_ref[...] = _adam_math(w_ref[...], g, m_ref[...], v_ref[...])

    tile = pl.BlockSpec((tr, cols), lambda i, pos: (i, 0))
    out = jax.ShapeDtypeStruct((rows, cols), F32)
    return pl.pallas_call(
        body, name="adamw_shard",
        grid_spec=pltpu.PrefetchScalarGridSpec(
            num_scalar_prefetch=1, grid=(rows // tr,),
            in_specs=[tile, tile, tile,
                      pl.BlockSpec((None, None, tr, cols), lambda i, pos: (pos[0], pos[1], i, 0)),
                      pl.BlockSpec((None, tr, cols), lambda i, pos: (pos[1], i, 0)),
                      pl.BlockSpec((3, tr, cols), lambda i, pos: (0, i, 0))],
            out_specs=[tile] * 4),
        out_shape=[out] * 4,
        compiler_params=_params("parallel"),
    )(pos, w, m, v, grad, from_sibling, from_chips)


def _adamw_reduced(w, m, v, own, from_sibling, from_chips, tr):
    rows, cols = w.shape

    def body(w_ref, m_ref, v_ref, own_ref, sib_ref, far_ref, g_ref, d_ref, nm_ref, nv_ref):
        g = own_ref[...] + sib_ref[...].astype(F32)
        for k in range(len(CHIP_FLIPS)):
            g = g + far_ref[k].astype(F32)
        g_ref[...] = g
        d_ref[...], nm_ref[...], nv_ref[...] = _adam_math(w_ref[...], g, m_ref[...], v_ref[...])

    tile = pl.BlockSpec((tr, cols), lambda i: (i, 0))
    out = jax.ShapeDtypeStruct((rows, cols), F32)
    return pl.pallas_call(
        body, name="adamw_reduced", grid=(rows // tr,),
        in_specs=[tile] * 5 + [pl.BlockSpec((len(CHIP_FLIPS), tr, cols), lambda i: (0, i, 0))],
        out_specs=[tile] * 4, out_shape=[out] * 4,
        compiler_params=_params("parallel"),
    )(w, m, v, own, from_sibling, from_chips)


def _sum_devices(gathered):
    _, rows, cols = gathered.shape

    def body(g_ref, o_ref):
        s = g_ref[0]
        for d in range(1, N_DEV):
            s = s + g_ref[d]
        o_ref[...] = s

    return pl.pallas_call(
        body, name="sum_devices", in_specs=[_full(gathered.shape)], out_specs=_full((rows, cols)), grid=(1,),
        out_shape=jax.ShapeDtypeStruct((rows, cols), F32),
    )(gathered)


def _adamw_small(w, g, m, v):
    def body(w_ref, g_ref, m_ref, v_ref, d_ref, nm_ref, nv_ref):
        d_ref[...], nm_ref[...], nv_ref[...] = _adam_math(w_ref[...], g_ref[...], m_ref[...], v_ref[...])

    spec = _full(w.shape)
    out = jax.ShapeDtypeStruct(w.shape, F32)
    return pl.pallas_call(
        body, name="adamw_small", grid=(1,), in_specs=[spec] * 4, out_specs=[spec] * 3, out_shape=[out] * 3,
    )(w, g, m, v)


TOKEN_TILE = 512
MID_TILE = 256
DW_TILE = 1024
ADAM_ROWS = 128


def _local_grads(x, target, g1, w_in_blocks, conv_w, sinks, g_attn, g_conv, g2, g3, g4, shards, order):
    t = x.shape[0]
    tm = min(TOKEN_TILE, t)
    rope = _rope_tables(t)
    qkv, gates, mconv, w_in, gathered = _in_proj_fwd(x, g1, w_in_blocks, conv_w, g_conv, rope, tm, shards,
                                                     (False, True, False))
    attn, mattn, (w_out, w_up, w_down) = _attn_fwd(qkv, sinks, g_attn, shards, gathered)
    actt, dup, hn2t, dmo, dmix, dh, dmixed, small_mid = _mid(
        mattn, mconv, x, target, g2, g3, g4, w_out.reshape(D_MODEL, D_MODEL),
        w_up, w_down.reshape(D_FF, D_MODEL), min(MID_TILE, t))
    tk = min(DW_TILE, t)
    up_own, up_sib, up_sums = _dw_pair_sums(hn2t, dup, order, False, "dw_up")
    down_own, down_sib, down_sums = _dw_pair_sums(actt, dmo, order, True, "dw_down")
    dw_out = _dw_out(mattn, mconv, dmix, tk)
    dattn, dgates, small_mix, up_far = _mix_bwd(dmixed, attn, gates, g_attn, g_conv, conv_w, tm, up_sums)
    dqkv, dsink, down_far = _attn_bwd(qkv, dattn, sinks, rope, down_sums)
    dw_up, dw_down = (up_own, up_sib, up_far), (down_own, down_sib, down_far)
    grad_x, dwa, dwb, small_in = _in_proj_bwd(dqkv, dgates, x, dh, g1, w_in, tm)
    dw_in = jnp.concatenate([dwa, dwb], axis=1)
    return grad_x, dw_in, dw_out, dw_up, dw_down, (small_mid, small_mix, dsink, small_in)


def _by_dest(a, rows_major):
    r, c = a.shape
    if rows_major:
        return a.reshape(N_CHIPS, 2, r // N_DEV, c).transpose(1, 0, 2, 3)
    return a.reshape(r, N_CHIPS, 2, c // N_DEV).transpose(2, 1, 0, 3)


def _pack_small(small_mid, small_mix, dsink, small_in):
    z = lambda n: jnp.zeros((1, n), F32)
    rows = [
        small_mid[ROW_LOSS:ROW_LOSS + 1],
        small_in[0:1],
        small_mid[ROW_G2:ROW_G2 + 1],
        small_mid[ROW_G3:ROW_G3 + 1],
        small_mid[ROW_G4:ROW_G4 + 1],
        jnp.concatenate([small_mix[ROW_GATTN:ROW_GATTN + 1], small_mix[ROW_GCONV:ROW_GCONV + 1]], axis=1),
        jnp.concatenate([small_mix[ROW_CW0:ROW_CW0 + 1], small_mix[ROW_CW0 + 1:ROW_CW0 + 2]], axis=1),
        jnp.concatenate([small_mix[ROW_CW0 + 2:ROW_CW0 + 3], dsink[0:1, :], z(D_MODEL - CONV_W - 128)], axis=1),
    ]
    return jnp.concatenate(rows, axis=0)


def kernel(x, pre_mix_norm, w_in, conv_w, attn_sinks, attn_group_norm, conv_group_norm, w_out, post_mix_norm, pre_mlp_norm, w_up, w_down, post_mlp_norm, loss_target, m_pre_mix_norm, m_w_in, m_conv_w, m_attn_sinks, m_attn_group_norm, m_conv_group_norm, m_w_out, m_post_mix_norm, m_pre_mlp_norm, m_w_up, m_w_down, m_post_mlp_norm, v_pre_mix_norm, v_w_in, v_conv_w, v_attn_sinks, v_attn_group_norm, v_conv_group_norm, v_w_out, v_post_mix_norm, v_pre_mlp_norm, v_w_up, v_w_down, v_post_mlp_norm):
    xi, yi, ci = _mesh_pos()
    chip = 2 * xi + yi
    dev = 2 * chip + ci

    order = _block_order(dev)
    pos = jnp.stack([ci, chip]).astype(jnp.int32)

    gw_in, gconv = _all_gather([w_in[0].astype(BF16), conv_w[0]], "gather_w_in")
    conv_full = gconv.transpose(1, 0, 2).reshape(3, CONV_W)
    shards = [w_out[0].astype(BF16), w_up[0].astype(BF16), w_down[0].astype(BF16)]

    grad_x, dw_in, dw_out, dw_up, dw_down, smalls = _local_grads(
        x[0], loss_target[0], pre_mix_norm, gw_in, conv_full, attn_sinks, attn_group_norm, conv_group_norm,
        post_mix_norm, pre_mlp_norm, post_mlp_norm, shards, order)

    grads = [_by_dest(dw_in, False), _by_dest(dw_out, True)]
    from_sibling = _sibling_exchange(grads, "reduce_sibling")
    summed = [_pair_sum(g, r, pos, ADAM_ROWS) for g, r in zip(grads, from_sibling)]
    from_chips = _chip_exchange(summed, "reduce_chips")

    small = _sum_devices(_all_gather([_pack_small(*smalls)], "gather_small")[0])
    loss = (0.5 / D_MODEL) * jnp.sum(small[0])

    big = {}
    for name, w, m, v, g, rs, rc in zip(("w_in", "w_out"), (w_in, w_out), (m_w_in, m_w_out), (v_w_in, v_w_out), grads,
                                        from_sibling, from_chips):
        big[name] = [a[None] for a in _adamw_shard(w[0], m[0], v[0], g, rs, rc, pos, ADAM_ROWS)]
    for name, w, m, v, (own, sib, far) in zip(("w_up", "w_down"), (w_up, w_down), (m_w_up, m_w_down),
                                              (v_w_up, v_w_down), (dw_up, dw_down)):
        big[name] = [a[None] for a in _adamw_reduced(w[0], m[0], v[0], own, sib, far, ADAM_ROWS)]

    conv_g = lax.dynamic_slice(
        jnp.stack([small[6, :CONV_W], small[6, CONV_W:], small[7, :CONV_W]]), (0, dev * (CONV_W // N_DEV)),
        (3, CONV_W // N_DEV))
    pad = lambda a, n: jnp.pad(a.reshape(1, -1), ((0, 0), (0, n - a.size)))
    small_names = ("pre_mix_norm", "post_mix_norm", "pre_mlp_norm", "post_mlp_norm")
    small_w = {"pre_mix_norm": (pre_mix_norm, m_pre_mix_norm, v_pre_mix_norm),
               "post_mix_norm": (post_mix_norm, m_post_mix_norm, v_post_mix_norm),
               "pre_mlp_norm": (pre_mlp_norm, m_pre_mlp_norm, v_pre_mlp_norm),
               "post_mlp_norm": (post_mlp_norm, m_post_mlp_norm, v_post_mlp_norm)}

    def pack(k):
        rows = [small_w[nm][k] for nm in small_names]
        rows.append(jnp.concatenate([(attn_group_norm, m_attn_group_norm, v_attn_group_norm)[k],
                                     (conv_group_norm, m_conv_group_norm, v_conv_group_norm)[k]], axis=1))
        rows.append(pad((conv_w, m_conv_w, v_conv_w)[k], D_MODEL))
        rows.append(pad((attn_sinks, m_attn_sinks, v_attn_sinks)[k], D_MODEL))
        rows.append(jnp.zeros((1, D_MODEL), F32))
        return jnp.concatenate(rows, axis=0)

    g_small = jnp.concatenate(
        [small[1:6], pad(conv_g, D_MODEL), pad(small[7, CONV_W:CONV_W + N_HEADS], D_MODEL), jnp.zeros((1, D_MODEL), F32)],
        axis=0)
    d_small, nm_small, nv_small = _adamw_small(pack(0), g_small, pack(1), pack(2))

    def unpack(a):
        nconv = 3 * CONV_W // N_DEV
        return {"pre_mix_norm": a[0:1], "post_mix_norm": a[1:2], "pre_mlp_norm": a[2:3], "post_mlp_norm": a[3:4],
                "attn_group_norm": a[4:5, :ATTN_W], "conv_group_norm": a[4:5, ATTN_W:],
                "conv_w": a[5, :nconv].reshape(1, 3, CONV_W // N_DEV), "attn_sinks": a[6:7, :N_HEADS]}

    order = ("pre_mix_norm", "w_in", "conv_w", "attn_sinks", "attn_group_norm", "conv_group_norm", "w_out",
             "post_mix_norm", "pre_mlp_norm", "w_up", "w_down", "post_mlp_norm")
    outs = []
    for k, a in enumerate((g_small, d_small, nm_small, nv_small)):
        sm = unpack(a)
        outs += [big[nm][k] if nm in big else sm[nm] for nm in order]
    return (loss, grad_x[None], *outs)
```

```python
import functools

import jax
import jax.numpy as jnp
import numpy as np
from jax import lax
from jax.experimental import pallas as pl
from jax.experimental.pallas import tpu as pltpu

F32 = jnp.float32
BF16 = jnp.bfloat16

D_MODEL = 1024
HEAD_DIM = 64
ATTN_W = 512
CONV_W = 512
N_HEADS = 8
N_KV = 2
GROUP = 4
KV_W = 128
QKV_W = ATTN_W + 2 * KV_W
GATES_W = 3 * CONV_W
IN_COLS = QKV_W + GATES_W
D_FF = 4096
FF_CHUNK = 512
N_FF_CHUNKS = D_FF // FF_CHUNK
BLOCK = 128
ROT_HALF = 8
ROPE_THETA = 500000.0
NORM_EPS = 1e-6
NEG_INF = -1e30
ATTN_SCALE = 0.125
N_DEV = 8
N_CHIPS = 4
IN_SHARD = IN_COLS // N_DEV

ADAM_LR = 0.001
ADAM_B1 = 0.9
ADAM_B2 = 0.999
ADAM_EPS = 1e-08
ADAM_WD = 0.01
ADAM_STEP = 10

V7X_VMEM_BYTES = 64 * 1024 * 1024
VMEM_LIMIT = V7X_VMEM_BYTES - 2 * 1024 * 1024

MESH = pl.DeviceIdType.MESH
HBM_SPEC = pl.BlockSpec(memory_space=pltpu.HBM)


def _params(*sem):
    return pltpu.CompilerParams(dimension_semantics=sem, vmem_limit_bytes=VMEM_LIMIT)


def _mm(a, b):
    return jnp.dot(a, b, preferred_element_type=F32)


def _mm_nt(a, b):
    return lax.dot_general(a, b, (((1,), (1,)), ((), ())), preferred_element_type=F32)


def _mm_tn(a, b):
    return lax.dot_general(a, b, (((0,), (0,)), ((), ())), preferred_element_type=F32)


def _inv_rms(x):
    return lax.rsqrt(jnp.mean(x * x, axis=-1, keepdims=True) + NORM_EPS)


def _rms_bwd(xhat, r, gain, dy):
    gy = dy * gain
    return r * (gy - xhat * jnp.mean(gy * xhat, axis=-1, keepdims=True)), dy * xhat


def _colsum(a):
    return jnp.sum(a, axis=0, keepdims=True)


def _full(shape):
    zeros = (0,) * len(shape)
    return pl.BlockSpec(shape, lambda *_: zeros)


def _resident(shape):
    zeros = (0,) * len(shape)
    return pl.BlockSpec(shape, lambda *_: zeros, pipeline_mode=pl.Buffered(1))


def _rope_tables(t):
    pos = np.arange(t, dtype=np.float32)
    inv_freq = (ROPE_THETA ** (-np.arange(0, 2 * ROT_HALF, 2, dtype=np.float64) / (2 * ROT_HALF))).astype(np.float32)
    ang = (pos[:, None] * inv_freq[None, :]).astype(np.float64)
    cos, sin = np.cos(ang).astype(np.float32), np.sin(ang).astype(np.float32)
    zeros8 = np.zeros((t, ROT_HALF), np.float32)
    rest = np.zeros((t, HEAD_DIM - 2 * ROT_HALF), np.float32)
    c_head = np.concatenate([cos, cos, rest + 1.0], axis=1)
    s1_head = np.concatenate([zeros8, sin, rest], axis=1)
    s2_head = np.concatenate([-sin, zeros8, rest], axis=1)
    two = lambda a: jnp.asarray(np.concatenate([a, a], axis=1))
    return two(c_head), two(s1_head), two(s2_head)


def _rope(v, c, s1, s2):
    return v * c + pltpu.roll(v, ROT_HALF, 1) * s1 + pltpu.roll(v, 128 - ROT_HALF, 1) * s2


def _rope_transpose(dv, c, s1, s2):
    return dv * c + pltpu.roll(dv * s1, 128 - ROT_HALF, 1) + pltpu.roll(dv * s2, ROT_HALF, 1)


def _shift_rows_down(u, prev, k):
    row = lax.broadcasted_iota(jnp.int32, u.shape, 0)
    out = pltpu.roll(u, k, 0)
    for r in range(k):
        out = jnp.where(row == r, prev[8 - k + r:8 - k + r + 1, :], out)
    return out


def _shift_rows_up(u, nxt, k):
    n = u.shape[0]
    row = lax.broadcasted_iota(jnp.int32, u.shape, 0)
    out = pltpu.roll(u, n - k, 0)
    for r in range(k):
        out = jnp.where(row == n - k + r, nxt[r:r + 1, :], out)
    return out


def _conv3(u, u1, u2, w):
    return (w[0:1, :] * u2 + w[1:2, :] * u1) + w[2:3, :] * u


def _mesh_pos():
    return lax.axis_index("x"), lax.axis_index("y"), lax.axis_index("c")


def _slot(ref, pos):
    dev = 4 * pos[0] + 2 * pos[1] + pos[2]
    if len(ref.shape) == 2:
        width = ref.shape[1] // N_DEV
        return ref.at[:, pl.ds(pl.multiple_of(dev * width, width), width)]
    return ref.at[dev]


def _gathered_shape(shard, by_cols):
    if by_cols:
        return jax.ShapeDtypeStruct((shard.shape[0], N_DEV * shard.shape[1]), shard.dtype)
    return jax.ShapeDtypeStruct((N_DEV,) + shard.shape, shard.dtype)


def _push(src, dst, sems, k, to):
    send_sems, recv_sems = sems
    return pltpu.make_async_remote_copy(src_ref=src, dst_ref=dst, send_sem=send_sems.at[k], recv_sem=recv_sems.at[k],
                                        device_id=to, device_id_type=MESH)


def _gather_near(first, last, shards, outs, sems, local_sems):
    x, y, c = _mesh_pos()
    me, peers = (x, y, c), [(x, y, 1 - c), (1 - x, y, c), (x, 1 - y, c)]
    n = len(shards)
    local = [pltpu.make_async_copy(shards[i], _slot(outs[i], me), local_sems.at[i]) for i in range(n)]
    sends = [_push(shards[i], _slot(outs[i], me), sems, 3 * i + k, peers[k]) for i in range(n) for k in range(3)]
    arrivals = [_push(shards[i], _slot(outs[i], peers[k]), sems, 3 * i + k, peers[k]) for i in range(n) for k in range(3)]

    @pl.when(first)
    def _():
        for cp in local + sends:
            cp.start()

    @pl.when(last)
    def _():
        for cp in sends:
            cp.wait_send()
        for cp in arrivals:
            cp.wait_recv()
        for cp in local:
            cp.wait()


def _gather_far(first, last, shards, ins, outs, sems):
    x, y, c = _mesh_pos()
    me, sibling = (x, y, c), (x, y, 1 - c)
    chips = [(1 - x, y), (x, 1 - y), (1 - x, 1 - y)]
    n = len(shards)
    diag_send = [_push(shards[i], _slot(outs[i], me), sems, 4 * i, (*chips[2], c)) for i in range(n)]
    diag_arrival = [_push(shards[i], _slot(outs[i], (*chips[2], c)), sems, 4 * i, (*chips[2], c)) for i in range(n)]
    passed = [[_push(_slot(ins[i], (*chips[j], c)), _slot(outs[i], (*chips[j], c)), sems, 4 * i + 1 + j, sibling)
               for i in range(n)] for j in range(3)]
    from_sibling = [_push(shards[i], _slot(outs[i], (*chips[j], 1 - c)), sems, 4 * i + 1 + j, sibling)
                    for i in range(n) for j in range(3)]

    @pl.when(first)
    def _():
        for cp in diag_send + passed[0] + passed[1]:
            cp.start()

    @pl.when(last)
    def _():
        for cp in diag_arrival:
            cp.wait_recv()
        for cp in passed[2]:
            cp.start()
        for cp in from_sibling:
            cp.wait_recv()
        for cp in diag_send + passed[0] + passed[1] + passed[2]:
            cp.wait_send()


def _in_proj_fwd(x, g1, w_in, conv_w, g_conv, rope, tm, shards, by_cols):
    t = x.shape[0]
    rc, rs1, rs2 = rope
    n = len(shards)

    def body(*refs):
        x_ref, g1_ref, w_ref, cw_ref, gc_ref, c_ref, s1_ref, s2_ref = refs[:8]
        shard_refs = refs[8:8 + n]
        qkv_ref, gates_ref, mconv_ref, w_full_ref = refs[8 + n:12 + n]
        gathered = refs[12 + n:12 + 2 * n]
        carry_ref = refs[12 + 2 * n]
        if n:
            step = pl.program_id(0)
            _gather_near(step == 0, step == pl.num_programs(0) - 1, shard_refs, gathered, refs[13 + 2 * n:15 + 2 * n],
                         refs[15 + 2 * n])

        @pl.when(pl.program_id(0) == 0)
        def _():
            carry_ref[...] = jnp.zeros_like(carry_ref)
            for d in range(N_DEV):
                w_full_ref[:, IN_SHARD * d:IN_SHARD * (d + 1)] = w_ref[d]

        xv = x_ref[...]
        hn = ((xv * _inv_rms(xv)) * g1_ref[...]).astype(BF16)
        proj = _mm(hn, w_full_ref[...])
        c, s1, s2 = c_ref[...], s1_ref[...], s2_ref[...]
        for ci in range((ATTN_W + KV_W) // 128):
            sl = slice(128 * ci, 128 * (ci + 1))
            qkv_ref[:, sl] = _rope(proj[:, sl], c, s1, s2).astype(BF16)
        qkv_ref[:, ATTN_W + KV_W:QKV_W] = proj[:, ATTN_W + KV_W:QKV_W].astype(BF16)
        gates = proj[:, QKV_W:]
        gates_ref[...] = gates
        gb, gcc, xin = gates[:, :CONV_W], gates[:, CONV_W:2 * CONV_W], gates[:, 2 * CONV_W:]
        u = gcc * xin
        prev = carry_ref[...]
        conv = gb * _conv3(u, _shift_rows_down(u, prev, 1), _shift_rows_down(u, prev, 2), cw_ref[...])
        carry_ref[...] = u[tm - 8:tm, :]
        mconv_ref[...] = ((conv * _inv_rms(conv)) * gc_ref[...]).astype(BF16)

    tile = lambda w_: pl.BlockSpec((tm, w_), lambda i: (i, 0))
    comm_scratch = [pltpu.SemaphoreType.DMA((3 * n,)), pltpu.SemaphoreType.DMA((3 * n,)), pltpu.SemaphoreType.DMA((n,))]
    res = pl.pallas_call(
        body, name="in_proj_fwd", grid=(t // tm,),
        in_specs=[tile(D_MODEL), _full((1, D_MODEL)), _resident((N_DEV, D_MODEL, IN_SHARD)), _full((3, CONV_W)),
                  _full((1, CONV_W)), tile(128), tile(128), tile(128)] + [HBM_SPEC] * n,
        out_specs=[tile(QKV_W), tile(GATES_W), tile(CONV_W), _full((D_MODEL, IN_COLS))] + [HBM_SPEC] * n,
        out_shape=[jax.ShapeDtypeStruct((t, QKV_W), BF16), jax.ShapeDtypeStruct((t, GATES_W), F32),
                   jax.ShapeDtypeStruct((t, CONV_W), BF16), jax.ShapeDtypeStruct((D_MODEL, IN_COLS), BF16)]
        + [_gathered_shape(s, cols) for s, cols in zip(shards, by_cols)],
        scratch_shapes=[pltpu.VMEM((8, CONV_W), F32)] + (comm_scratch if n else []),
        compiler_params=_params("arbitrary"),
    )(x, g1, w_in, conv_w, g_conv, rc, rs1, rs2, *shards)
    return res[0], res[1], res[2], res[3], list(res[4:])


GROUP_ROWS = GROUP * BLOCK


def _attn_mask(has_prev):
    row = lax.broadcasted_iota(jnp.int32, (GROUP_ROWS, 2 * BLOCK), 0) & (BLOCK - 1)
    col = lax.broadcasted_iota(jnp.int32, (GROUP_ROWS, 2 * BLOCK), 1)
    band = (col > row) & (col <= row + BLOCK)
    return band if has_prev is True else band & ((col >= BLOCK) | has_prev)


def _stack_heads(a, g):
    return jnp.concatenate([a[:, HEAD_DIM * (GROUP * g + hh):HEAD_DIM * (GROUP * g + hh + 1)] for hh in range(GROUP)], axis=0)


def _unstack_heads(a):
    return jnp.concatenate([a[BLOCK * hh:BLOCK * (hh + 1), :] for hh in range(GROUP)], axis=1)


def _group_sinks(sink_ref, g):
    head = lax.broadcasted_iota(jnp.int32, (GROUP_ROWS, 1), 0) // BLOCK
    out = jnp.full((GROUP_ROWS, 1), sink_ref[0, GROUP * g], F32)
    for hh in range(1, GROUP):
        out = jnp.where(head == hh, sink_ref[0, GROUP * g + hh], out)
    return out


def _attn_probs(qs, kk, sink, valid):
    s = jnp.where(valid, _mm_nt(qs, kk) * ATTN_SCALE, NEG_INF)
    m = jnp.maximum(jnp.max(s, axis=-1, keepdims=True), sink)
    p = jnp.exp(s - m)
    psink = jnp.exp(sink - m)
    inv_l = 1.0 / (jnp.sum(p, axis=-1, keepdims=True) + psink)
    return p * inv_l, psink * inv_l


ATTN_STEP_BLOCKS = 4
ATTN_STEP = ATTN_STEP_BLOCKS * BLOCK
ATTN_KEYS = ATTN_STEP + BLOCK


def _qkv_specs(order):
    prev = lambda i: jnp.maximum(ATTN_STEP_BLOCKS * order(i) - 1, 0)
    kcol, vcol = ATTN_W // KV_W, ATTN_W // KV_W + 1
    return [pl.BlockSpec((ATTN_STEP, ATTN_W), lambda i: (order(i), 0)),
            pl.BlockSpec((BLOCK, KV_W), lambda i: (prev(i), kcol)), pl.BlockSpec((ATTN_STEP, KV_W), lambda i: (order(i), kcol)),
            pl.BlockSpec((BLOCK, KV_W), lambda i: (prev(i), vcol)), pl.BlockSpec((ATTN_STEP, KV_W), lambda i: (order(i), vcol))]


def _attn_fwd(qkv, sinks, g_attn, shards, gathered):
    t = qkv.shape[0]
    n = len(shards)

    def body(*refs):
        sink_ref, q_ref, kp_ref, kc_ref, vp_ref, vc_ref, ga_ref = refs[:7]
        attn_ref, mattn_ref = refs[7 + 2 * n:9 + 2 * n]
        step = pl.program_id(0)
        if n:
            _gather_far(step == 0, step == pl.num_programs(0) - 1, refs[7:7 + n], refs[7 + n:7 + 2 * n],
                        refs[9 + 2 * n:9 + 3 * n], refs[9 + 3 * n:11 + 3 * n])
        q = q_ref[...]
        keys = jnp.concatenate([kp_ref[...], kc_ref[...]], axis=0)
        vals = jnp.concatenate([vp_ref[...], vc_ref[...]], axis=0)
        sink = [_group_sinks(sink_ref, g) for g in range(N_KV)]
        gain = ga_ref[...]
        for b in range(ATTN_STEP_BLOCKS):
            rows, window = slice(BLOCK * b, BLOCK * (b + 1)), slice(BLOCK * b, BLOCK * (b + 2))
            valid = _attn_mask(True if b else step > 0)
            outs = []
            for g in range(N_KV):
                gs = slice(HEAD_DIM * g, HEAD_DIM * (g + 1))
                probs, _ = _attn_probs(_stack_heads(q[rows], g), keys[window, gs], sink[g], valid)
                outs.append(_unstack_heads(_mm(probs.astype(BF16), vals[window, gs])))
            attn = jnp.concatenate(outs, axis=1)
            attn_ref[rows, :] = attn
            mattn_ref[rows, :] = ((attn * _inv_rms(attn)) * gain).astype(BF16)

    blk = pl.BlockSpec((ATTN_STEP, ATTN_W), lambda j: (j, 0))
    res = pl.pallas_call(
        body, name="attn_fwd", grid=(t // ATTN_STEP,),
        in_specs=[pl.BlockSpec(memory_space=pltpu.SMEM)] + _qkv_specs(lambda j: j) + [_full((1, ATTN_W))]
        + [HBM_SPEC] * (2 * n),
        out_specs=[blk, blk] + [HBM_SPEC] * n,
        out_shape=[jax.ShapeDtypeStruct((t, ATTN_W), F32), jax.ShapeDtypeStruct((t, ATTN_W), BF16)]
        + [jax.ShapeDtypeStruct(g.shape, g.dtype) for g in gathered],
        input_output_aliases={7 + n + i: 2 + i for i in range(n)},
        scratch_shapes=[pltpu.SemaphoreType.DMA((4 * n,)), pltpu.SemaphoreType.DMA((4 * n,))] if n else [],
        compiler_params=_params("arbitrary"),
    )(sinks, qkv, qkv, qkv, qkv, qkv, g_attn, *shards, *gathered)
    return res[0], res[1], list(res[2:])


SMALL_ROWS = 8
ROW_LOSS, ROW_G2, ROW_G3, ROW_G4 = 0, 1, 2, 3


def _mid(mattn, mconv, x, target, g2, g3, g4, w_out, w_up, w_down, tm):
    t = x.shape[0]

    def body(ma_ref, mc_ref, x_ref, t_ref, g2_ref, g3_ref, g4_ref, wo_ref, wu_ref, wd_ref,
             actt_ref, dup_ref, hn2t_ref, dmo_ref, dmix_ref, dh_ref, dmixed_ref, small_ref, up_ref):
        @pl.when(pl.program_id(0) == 0)
        def _():
            small_ref[...] = jnp.zeros_like(small_ref)

        g2, g3, g4 = g2_ref[...], g3_ref[...], g4_ref[...]
        mix_out = _mm(ma_ref[...], wo_ref[0:ATTN_W, :]) + _mm(mc_ref[...], wo_ref[ATTN_W:, :])
        r2 = _inv_rms(mix_out)
        mo_hat = mix_out * r2
        h = x_ref[...] + mo_hat * g2
        r3 = _inv_rms(h)
        h_hat = h * r3
        hn2 = (h_hat * g3).astype(BF16)
        hn2t_ref[...] = hn2.T
        up = jnp.maximum(_mm(hn2, wu_ref[...]), 0.0)
        up_ref[...] = up.astype(BF16)
        act = (up * up).astype(BF16)
        actt_ref[...] = act.T
        mlp = _mm(act, wd_ref[...])
        r4 = _inv_rms(mlp)
        ml_hat = mlp * r4
        err = (h + ml_hat * g4) - t_ref[...]
        d_out = err * (1.0 / D_MODEL)
        d_mlp, dg4 = _rms_bwd(ml_hat, r4, g4, d_out)
        dmo = d_mlp.astype(BF16)
        dmo_ref[...] = dmo
        dup = (_mm_nt(dmo, wd_ref[...]) * (2.0 * up_ref[...].astype(F32))).astype(BF16)
        dup_ref[...] = dup
        dhn2 = _mm_nt(dup, wu_ref[...])
        dh_norm, dg3 = _rms_bwd(h_hat, r3, g3, dhn2)
        dh = d_out + dh_norm
        dh_ref[...] = dh
        d_mix, dg2 = _rms_bwd(mo_hat, r2, g2, dh)
        dmix = d_mix.astype(BF16)
        dmix_ref[...] = dmix
        dmixed_ref[...] = _mm_nt(dmix, wo_ref[...])
        small_ref[ROW_LOSS:ROW_LOSS + 1, :] += _colsum(err * err)
        small_ref[ROW_G2:ROW_G2 + 1, :] += _colsum(dg2)
        small_ref[ROW_G3:ROW_G3 + 1, :] += _colsum(dg3)
        small_ref[ROW_G4:ROW_G4 + 1, :] += _colsum(dg4)

    tile = lambda n: pl.BlockSpec((tm, n), lambda i: (i, 0))
    cols = lambda n: pl.BlockSpec((n, tm), lambda i: (0, i))
    gain = _full((1, D_MODEL))
    return pl.pallas_call(
        body, name="mid_fwd_bwd", grid=(t // tm,),
        in_specs=[tile(ATTN_W), tile(CONV_W), tile(D_MODEL), tile(D_MODEL), gain, gain, gain,
                  _resident((D_MODEL, D_MODEL)), _resident((D_MODEL, D_FF)), _resident((D_FF, D_MODEL))],
        out_specs=[cols(D_FF), tile(D_FF), cols(D_MODEL), tile(D_MODEL), tile(D_MODEL), tile(D_MODEL), tile(D_MODEL),
                   _full((SMALL_ROWS, D_MODEL))],
        out_shape=[jax.ShapeDtypeStruct((D_FF, t), BF16), jax.ShapeDtypeStruct((t, D_FF), BF16),
                   jax.ShapeDtypeStruct((D_MODEL, t), BF16), jax.ShapeDtypeStruct((t, D_MODEL), BF16),
                   jax.ShapeDtypeStruct((t, D_MODEL), BF16), jax.ShapeDtypeStruct((t, D_MODEL), F32),
                   jax.ShapeDtypeStruct((t, D_MODEL), F32), jax.ShapeDtypeStruct((SMALL_ROWS, D_MODEL), F32)],
        scratch_shapes=[pltpu.VMEM((tm, D_FF), BF16)],
        compiler_params=_params("arbitrary"),
    )(mattn, mconv, x, target, g2, g3, g4, w_out, w_up, w_down)


CHIP_FLIPS = ((1, 1), (1, 0), (0, 1))


def _block_order(dev):
    chip_masks = [4 * fx + 2 * fy for fx, fy in CHIP_FLIPS]
    masks = [m + 1 for m in chip_masks] + [1] + chip_masks + [0]
    return jnp.bitwise_xor(dev, jnp.asarray(masks, jnp.int32)).astype(jnp.int32)


def _other_chips(x, y, c):
    return [(1 - x if fx else x, 1 - y if fy else y, c) for fx, fy in CHIP_FLIPS]


def _dw_pair_sums(operands, order, which, name):
    t = operands[-1].shape[0]
    n_far = len(CHIP_FLIPS)
    n_in = len(operands)
    out_chunk = D_MODEL // N_DEV
    if which == "up":
        rows, cols = D_MODEL, FF_CHUNK
        in_specs = [_resident((D_MODEL, t)), pl.BlockSpec((t, FF_CHUNK), lambda s, order_ref: (0, order_ref[s]))]
    elif which == "down":
        rows, cols = FF_CHUNK, D_MODEL
        in_specs = [pl.BlockSpec((FF_CHUNK, t), lambda s, order_ref: (order_ref[s], 0)), _resident((t, D_MODEL))]
    else:
        rows, cols = out_chunk, D_MODEL
        half = pl.BlockSpec((t, out_chunk), lambda s, order_ref: (0, order_ref[s] % (N_DEV // 2)))
        in_specs = [half, half, _resident((t, D_MODEL))]

    def body(order_ref, *refs):
        own_ref, from_sib_ref, pair_ref, send_buf, land_buf, send_sems, recv_sems = refs[n_in:]
        s_now = pl.program_id(0)
        x, y, c = _mesh_pos()
        sibling = (x, y, 1 - c)
        sems = (send_sems, recv_sems)

        def hand_over(k):
            dst = land_buf.at[k] if k < n_far else from_sib_ref
            return _push(send_buf.at[k], dst, sems, k, sibling)

        if which == "out":
            ma_ref, mc_ref, b_ref = refs[:n_in]
            block = lax.cond(order_ref[s_now] < N_DEV // 2, lambda: _mm_tn(ma_ref[...], b_ref[...]),
                             lambda: _mm_tn(mc_ref[...], b_ref[...]))
        else:
            block = _mm(refs[0][...], refs[1][...])
        for k in range(n_far + 1):
            @pl.when(s_now == k)
            def _():
                send_buf[k] = block.astype(BF16)
                hand_over(k).start()

        for k in range(n_far):
            @pl.when(s_now == n_far + 1 + k)
            def _():
                hand_over(k).wait_recv()
                pair_ref[...] = (block + land_buf[k].astype(F32)).astype(BF16)

        @pl.when(s_now == N_DEV - 1)
        def _():
            own_ref[...] = block
            for k in range(n_far + 1):
                hand_over(k).wait_send()
            hand_over(n_far).wait_recv()

    return pl.pallas_call(
        body, name=name,
        grid_spec=pltpu.PrefetchScalarGridSpec(
            num_scalar_prefetch=1, grid=(N_DEV,), in_specs=in_specs,
            out_specs=[pl.BlockSpec((rows, cols), lambda s, order_ref: (0, 0)), HBM_SPEC,
                       pl.BlockSpec((None, rows, cols), lambda s, order_ref: (jnp.clip(s - n_far - 1, 0, n_far - 1), 0, 0))],
            scratch_shapes=[pltpu.VMEM((n_far + 1, rows, cols), BF16), pltpu.VMEM((n_far, rows, cols), BF16),
                            pltpu.SemaphoreType.DMA((n_far + 1,)), pltpu.SemaphoreType.DMA((n_far + 1,))]),
        out_shape=[jax.ShapeDtypeStruct((rows, cols), F32), jax.ShapeDtypeStruct((rows, cols), BF16),
                   jax.ShapeDtypeStruct((n_far, rows, cols), BF16)],
        compiler_params=_params("arbitrary"),
    )(order, *operands)


def _chip_exchange_beside(first, last, sums, outs, sems):
    chips = _other_chips(*_mesh_pos())
    copies = [_push(sums[i].at[k], outs[i].at[k], sems, len(chips) * i + k, chip)
              for i in range(len(sums)) for k, chip in enumerate(chips)]

    @pl.when(first)
    def _():
        for cp in copies:
            cp.start()

    @pl.when(last)
    def _():
        for cp in copies:
            cp.wait()


ROW_GATTN, ROW_GCONV, ROW_CW0 = 0, 1, 2


def _mix_bwd(dmixed, attn, gates, g_attn, g_conv, conv_w, tm, sums):
    t = attn.shape[0]
    n = t // tm
    rev = lambda i: n - 1 - i

    def body(dm_ref, attn_ref, gates_ref, gprev_ref, ga_ref, gc_ref, cw_ref, sums_ref, dattn_ref, dgates_ref, small_ref,
             arrived_ref, carry_ref, send_sems, recv_sems):
        i = pl.program_id(0)
        _chip_exchange_beside(i == 0, i == n - 1, [sums_ref], [arrived_ref], (send_sems, recv_sems))

        @pl.when(i == 0)
        def _():
            small_ref[...] = jnp.zeros_like(small_ref)
            carry_ref[...] = jnp.zeros_like(carry_ref)

        dm = dm_ref[...]
        a = attn_ref[...]
        ra = _inv_rms(a)
        a_hat = a * ra
        dattn, dga = _rms_bwd(a_hat, ra, ga_ref[...], dm[:, :ATTN_W])
        dattn_ref[...] = dattn

        gates = gates_ref[...]
        gb, gcc, xin = gates[:, :CONV_W], gates[:, CONV_W:2 * CONV_W], gates[:, 2 * CONV_W:]
        u = gcc * xin
        gp = gprev_ref[...]
        uprev = jnp.where(rev(i) == 0, 0.0, gp[:, CONV_W:2 * CONV_W] * gp[:, 2 * CONV_W:])
        u1, u2 = _shift_rows_down(u, uprev, 1), _shift_rows_down(u, uprev, 2)
        w = cw_ref[...]
        c = _conv3(u, u1, u2, w)
        conv = gb * c
        rcv = _inv_rms(conv)
        c_hat = conv * rcv
        dconv, dgc = _rms_bwd(c_hat, rcv, gc_ref[...], dm[:, ATTN_W:])
        dc = dconv * gb
        nxt = carry_ref[...]
        du = (w[2:3, :] * dc + w[1:2, :] * _shift_rows_up(dc, nxt, 1)) + w[0:1, :] * _shift_rows_up(dc, nxt, 2)
        carry_ref[...] = dc[0:8, :]
        dgates_ref[:, :CONV_W] = (dconv * c).astype(BF16)
        dgates_ref[:, CONV_W:2 * CONV_W] = (du * xin).astype(BF16)
        dgates_ref[:, 2 * CONV_W:] = (du * gcc).astype(BF16)
        small_ref[ROW_GATTN:ROW_GATTN + 1, :] += _colsum(dga)
        small_ref[ROW_GCONV:ROW_GCONV + 1, :] += _colsum(dgc)
        small_ref[ROW_CW0:ROW_CW0 + 1, :] += _colsum(dc * u2)
        small_ref[ROW_CW0 + 1:ROW_CW0 + 2, :] += _colsum(dc * u1)
        small_ref[ROW_CW0 + 2:ROW_CW0 + 3, :] += _colsum(dc * u)

    tile = lambda w_: pl.BlockSpec((tm, w_), lambda i: (rev(i), 0))
    prev8 = pl.BlockSpec((8, GATES_W), lambda i: (jnp.maximum(rev(i) * (tm // 8) - 1, 0), 0))
    return pl.pallas_call(
        body, name="mix_bwd", grid=(n,),
        in_specs=[tile(D_MODEL), tile(ATTN_W), tile(GATES_W), prev8, _full((1, ATTN_W)), _full((1, CONV_W)),
                  _full((3, CONV_W)), HBM_SPEC],
        out_specs=[tile(ATTN_W), tile(GATES_W), _full((SMALL_ROWS, CONV_W)), HBM_SPEC],
        out_shape=[jax.ShapeDtypeStruct((t, ATTN_W), F32), jax.ShapeDtypeStruct((t, GATES_W), BF16),
                   jax.ShapeDtypeStruct((SMALL_ROWS, CONV_W), F32), jax.ShapeDtypeStruct(sums.shape, sums.dtype)],
        scratch_shapes=[pltpu.VMEM((8, CONV_W), F32), pltpu.SemaphoreType.DMA((len(CHIP_FLIPS),)),
                        pltpu.SemaphoreType.DMA((len(CHIP_FLIPS),))],
        compiler_params=_params("arbitrary"),
    )(dmixed, attn, gates, gates, g_attn, g_conv, conv_w, sums)


def _attn_bwd(qkv, dattn, sinks, rope, sums):
    t = qkv.shape[0]
    n_steps = t // ATTN_STEP
    rev = lambda i: n_steps - 1 - i
    rc, rs1, rs2 = rope

    def body(sink_ref, q_ref, kp_ref, kc_ref, vp_ref, vc_ref, do_ref, c_ref, s1_ref, s2_ref, sums_ref,
             dqkv_ref, dsink_ref, arrived_ref, ck_ref, cv_ref, kacc_ref, vacc_ref, send_sems, recv_sems):
        i = pl.program_id(0)
        _chip_exchange_beside(i == 0, i == n_steps - 1, [sums_ref], [arrived_ref], (send_sems, recv_sems))

        @pl.when(i == 0)
        def _():
            dsink_ref[...] = jnp.zeros_like(dsink_ref)
            ck_ref[...] = jnp.zeros_like(ck_ref)
            cv_ref[...] = jnp.zeros_like(cv_ref)

        kacc_ref[...] = jnp.zeros_like(kacc_ref)
        vacc_ref[...] = jnp.zeros_like(vacc_ref)
        q = q_ref[...]
        dout = do_ref[...].astype(BF16)
        keys = jnp.concatenate([kp_ref[...], kc_ref[...]], axis=0)
        vals = jnp.concatenate([vp_ref[...], vc_ref[...]], axis=0)
        sink = [_group_sinks(sink_ref, g) for g in range(N_KV)]
        c, s1, s2 = c_ref[...], s1_ref[...], s2_ref[...]
        lane = lax.broadcasted_iota(jnp.int32, (1, 128), 1)
        dsink = jnp.zeros((1, 128), F32)
        for b in range(ATTN_STEP_BLOCKS):
            rows, window = slice(BLOCK * b, BLOCK * (b + 1)), slice(BLOCK * b, BLOCK * (b + 2))
            valid = _attn_mask(True if b else rev(i) > 0)
            dq_parts, dk_parts, dv_parts = [], [], []
            for g in range(N_KV):
                gs = slice(HEAD_DIM * g, HEAD_DIM * (g + 1))
                kk, vv = keys[window, gs], vals[window, gs]
                qs, dos = _stack_heads(q[rows], g), _stack_heads(dout[rows], g)
                probs, psink = _attn_probs(qs, kk, sink[g], valid)
                dp = _mm_nt(dos, vv)
                delta = jnp.sum(probs * dp, axis=-1, keepdims=True)
                ds = (probs * (dp - delta) * ATTN_SCALE).astype(BF16)
                sink_terms = psink * delta
                for hh in range(GROUP):
                    head_sum = jnp.sum(sink_terms[BLOCK * hh:BLOCK * (hh + 1), :])
                    dsink = dsink + jnp.where(lane == GROUP * g + hh, -head_sum, 0.0)
                dq_parts.append(_unstack_heads(_mm(ds, kk)))
                dk_parts.append(_mm_tn(ds, qs))
                dv_parts.append(_mm_tn(probs.astype(BF16), dos))
            kacc_ref[window, :] += jnp.concatenate(dk_parts, axis=1)
            vacc_ref[window, :] += jnp.concatenate(dv_parts, axis=1)
            dq = jnp.concatenate(dq_parts, axis=1)
            for ci in range(ATTN_W // 128):
                sl = slice(128 * ci, 128 * (ci + 1))
                dqkv_ref[rows, sl] = _rope_transpose(dq[:, sl], c[rows], s1[rows], s2[rows]).astype(BF16)
        kacc_ref[ATTN_STEP:, :] += ck_ref[...]
        vacc_ref[ATTN_STEP:, :] += cv_ref[...]
        ck_ref[...] = kacc_ref[:BLOCK, :]
        cv_ref[...] = vacc_ref[:BLOCK, :]
        dqkv_ref[:, ATTN_W:ATTN_W + KV_W] = _rope_transpose(kacc_ref[BLOCK:, :], c, s1, s2).astype(BF16)
        dqkv_ref[:, ATTN_W + KV_W:] = vacc_ref[BLOCK:, :].astype(BF16)
        dsink_ref[0:1, :] += dsink

    blk = lambda w_: pl.BlockSpec((ATTN_STEP, w_), lambda i: (rev(i), 0))
    return pl.pallas_call(
        body, name="attn_bwd", grid=(n_steps,),
        in_specs=[pl.BlockSpec(memory_space=pltpu.SMEM)] + _qkv_specs(rev) + [blk(ATTN_W), blk(128), blk(128), blk(128),
                                                                              HBM_SPEC],
        out_specs=[blk(QKV_W), _full((8, 128)), HBM_SPEC],
        out_shape=[jax.ShapeDtypeStruct((t, QKV_W), BF16), jax.ShapeDtypeStruct((8, 128), F32),
                   jax.ShapeDtypeStruct(sums.shape, sums.dtype)],
        scratch_shapes=[pltpu.VMEM((BLOCK, KV_W), F32), pltpu.VMEM((BLOCK, KV_W), F32),
                        pltpu.VMEM((ATTN_KEYS, KV_W), F32), pltpu.VMEM((ATTN_KEYS, KV_W), F32),
                        pltpu.SemaphoreType.DMA((len(CHIP_FLIPS),)), pltpu.SemaphoreType.DMA((len(CHIP_FLIPS),))],
        compiler_params=_params("arbitrary"),
    )(sinks, qkv, qkv, qkv, qkv, qkv, dattn, rc, rs1, rs2, sums)


def _in_proj_bwd(dqkv, dgates, x, dh, g1, w_in, tm, out_sums):
    t = x.shape[0]
    n = t // tm
    n_far = len(CHIP_FLIPS)
    shard = (D_MODEL, IN_SHARD)

    def body(dq_ref, dg_ref, x_ref, dh_ref, g1_ref, w_ref, osums_ref,
             dx_ref, own_ref, sib_ref, far_ref, dg1_ref, oarrived_ref,
             acc_ref, send_buf, land_buf, pair_buf, d2d_send, d2d_recv, ici_send, ici_recv, o_send, o_recv):
        i = pl.program_id(0)
        x_pos, y_pos, c = _mesh_pos()
        my_chip = 2 * x_pos + y_pos
        sibling = (x_pos, y_pos, 1 - c)
        _chip_exchange_beside(i == 0, i == 2 * n - 1, [osums_ref], [oarrived_ref], (o_send, o_recv))

        def cols(d):
            return slice(IN_SHARD * d, IN_SHARD * (d + 1))

        def hand_over(chip):
            return _push(send_buf.at[chip], land_buf.at[chip], (d2d_send, d2d_recv), chip, sibling)

        def to_chip(chip, rel):
            return pltpu.make_async_remote_copy(
                src_ref=pair_buf.at[chip], dst_ref=far_ref.at[rel - 1], send_sem=ici_send.at[rel - 1],
                recv_sem=ici_recv.at[rel - 1], device_id=(chip // 2, chip % 2, c), device_id_type=MESH)

        @pl.when(i == 0)
        def _():
            acc_ref[...] = jnp.zeros_like(acc_ref)
            dg1_ref[...] = jnp.zeros_like(dg1_ref)

        xv = x_ref[...]
        r = _inv_rms(xv)
        x_hat = xv * r
        g1 = g1_ref[...]
        dq, dg = dq_ref[...], dg_ref[...]

        @pl.when(i < n)
        def _():
            hn = (x_hat * g1).astype(BF16)
            acc_ref[:, :QKV_W] += _mm_tn(hn, dq)
            acc_ref[:, QKV_W:] += _mm_tn(hn, dg)

        @pl.when(i == n - 1)
        def _():
            for d in range(N_DEV):
                @pl.when(d % 2 != c)
                def _():
                    send_buf[d // 2] = acc_ref[:, cols(d)].astype(BF16)
                    hand_over(d // 2).start()

        @pl.when(i == n)
        def _():
            for d in range(N_DEV):
                chip = d // 2

                @pl.when(d % 2 == c)
                def _():
                    hand_over(chip).wait_recv()

                    @pl.when(chip == my_chip)
                    def _():
                        own_ref[...] = acc_ref[:, cols(d)]
                        sib_ref[...] = land_buf[chip]

                    @pl.when(chip != my_chip)
                    def _():
                        pair_buf[chip] = (acc_ref[:, cols(d)] + land_buf[chip].astype(F32)).astype(BF16)
                        to_chip(chip, chip ^ my_chip).start()
            for chip in range(N_CHIPS):
                hand_over(chip).wait_send()

        @pl.when(i >= n)
        def _():
            dhn = _mm_nt(dq, w_ref[:, :QKV_W]) + _mm_nt(dg, w_ref[:, QKV_W:])
            dx, dg1 = _rms_bwd(x_hat, r, g1, dhn)
            dx_ref[...] = dh_ref[...] + dx
            dg1_ref[0:1, :] += _colsum(dg1)

        @pl.when(i == 2 * n - 1)
        def _():
            for rel in range(1, n_far + 1):
                to_chip(0, rel).wait()

    both = lambda w_: pl.BlockSpec((tm, w_), lambda i: (i % n, 0))
    second = pl.BlockSpec((tm, D_MODEL), lambda i: (jnp.maximum(i - n, 0), 0))
    whole = lambda dtype: jax.ShapeDtypeStruct(shard, dtype)
    sems = lambda k: pltpu.SemaphoreType.DMA((k,))
    res = pl.pallas_call(
        body, name="in_proj_bwd", grid=(2 * n,),
        in_specs=[both(QKV_W), both(GATES_W), both(D_MODEL), second, _full((1, D_MODEL)), _resident((D_MODEL, IN_COLS)),
                  HBM_SPEC],
        out_specs=[second, _full(shard), _full(shard), HBM_SPEC, _full((SMALL_ROWS, D_MODEL)), HBM_SPEC],
        out_shape=[jax.ShapeDtypeStruct((t, D_MODEL), F32), whole(F32), whole(BF16),
                   jax.ShapeDtypeStruct((n_far,) + shard, BF16), jax.ShapeDtypeStruct((SMALL_ROWS, D_MODEL), F32),
                   jax.ShapeDtypeStruct(out_sums.shape, out_sums.dtype)],
        scratch_shapes=[pltpu.VMEM((D_MODEL, IN_COLS), F32), pltpu.VMEM((N_CHIPS,) + shard, BF16),
                        pltpu.VMEM((N_CHIPS,) + shard, BF16), pltpu.VMEM((N_CHIPS,) + shard, BF16),
                        sems(N_CHIPS), sems(N_CHIPS), sems(n_far), sems(n_far), sems(n_far), sems(n_far)],
        compiler_params=_params("arbitrary"),
    )(dqkv, dgates, x, dh, g1, w_in, out_sums)
    return res[0], (res[1], res[2], res[3]), res[4], res[5]


def _all_gather(shards, name):
    n = len(shards)

    def body(*refs):
        ins, outs = refs[:n], refs[n:2 * n]
        send_sems, recv_sems, local_sems = refs[2 * n:]
        x, y, c = _mesh_pos()
        me, sibling = (x, y, c), (x, y, 1 - c)
        chips = [(1 - x, y), (x, 1 - y), (1 - x, 1 - y)]

        def copy(i, k, block, to, src=None):
            dst = outs[i].at[4 * block[0] + 2 * block[1] + block[2]]
            return pltpu.make_async_remote_copy(
                src_ref=dst if src is None else src, dst_ref=dst, send_sem=send_sems.at[7 * i + k],
                recv_sem=recv_sems.at[7 * i + k], device_id=to, device_id_type=MESH)

        mine = [pltpu.make_async_copy(ins[i], outs[i].at[4 * x + 2 * y + c], local_sems.at[i]) for i in range(n)]
        for cp in mine:
            cp.start()
        first = []
        for i in range(n):
            first.append(copy(i, 0, me, sibling, src=ins[i]))
            first += [copy(i, 1 + j, me, (*chip, c), src=ins[i]) for j, chip in enumerate(chips)]
        for cp in first:
            cp.start()
        passed = []
        for j, chip in enumerate(chips):
            for i in range(n):
                copy(i, 1 + j, (*chip, c), me).wait_recv()
                cp = copy(i, 4 + j, (*chip, c), sibling)
                cp.start()
                passed.append(cp)
        for i in range(n):
            copy(i, 0, sibling, me).wait_recv()
            for j, chip in enumerate(chips):
                copy(i, 4 + j, (*chip, 1 - c), me).wait_recv()
        for cp in first + passed:
            cp.wait_send()
        for cp in mine:
            cp.wait()

    return pl.pallas_call(
        body, name=name,
        in_specs=[HBM_SPEC] * n, out_specs=[HBM_SPEC] * n,
        out_shape=[jax.ShapeDtypeStruct((N_DEV,) + s.shape, s.dtype) for s in shards],
        scratch_shapes=[pltpu.SemaphoreType.DMA((7 * n,)), pltpu.SemaphoreType.DMA((7 * n,)),
                        pltpu.SemaphoreType.DMA((n,))],
    )(*shards)


def _adam_math(w, g, m, v):
    m = ADAM_B1 * m + (1.0 - ADAM_B1) * g
    v = ADAM_B2 * v + (1.0 - ADAM_B2) * (g * g)
    m_hat = m / (1.0 - ADAM_B1 ** ADAM_STEP)
    v_hat = v / (1.0 - ADAM_B2 ** ADAM_STEP)
    delta = -ADAM_LR * (m_hat / (jnp.sqrt(v_hat) + ADAM_EPS) + ADAM_WD * w)
    return delta, m, v


def _adamw_reduced(w, m, v, own, from_sibling, from_chips, tr):
    rows, cols = w.shape

    def body(w_ref, m_ref, v_ref, own_ref, sib_ref, far_ref, g_ref, d_ref, nm_ref, nv_ref):
        g = own_ref[...] + sib_ref[...].astype(F32)
        for k in range(len(CHIP_FLIPS)):
            g = g + far_ref[k].astype(F32)
        g_ref[...] = g
        d_ref[...], nm_ref[...], nv_ref[...] = _adam_math(w_ref[...], g, m_ref[...], v_ref[...])

    tile = pl.BlockSpec((tr, cols), lambda i: (i, 0))
    out = jax.ShapeDtypeStruct((rows, cols), F32)
    return pl.pallas_call(
        body, name="adamw_reduced", grid=(rows // tr,),
        in_specs=[tile] * 5 + [pl.BlockSpec((len(CHIP_FLIPS), tr, cols), lambda i: (0, i, 0))],
        out_specs=[tile] * 4, out_shape=[out] * 4,
        compiler_params=_params("parallel"),
    )(w, m, v, own, from_sibling, from_chips)


def _sum_devices(gathered):
    _, rows, cols = gathered.shape

    def body(g_ref, o_ref):
        s = g_ref[0]
        for d in range(1, N_DEV):
            s = s + g_ref[d]
        o_ref[...] = s

    return pl.pallas_call(
        body, name="sum_devices", in_specs=[_full(gathered.shape)], out_specs=_full((rows, cols)), grid=(1,),
        out_shape=jax.ShapeDtypeStruct((rows, cols), F32),
    )(gathered)


def _adamw_small(w, g, m, v):
    def body(w_ref, g_ref, m_ref, v_ref, d_ref, nm_ref, nv_ref):
        d_ref[...], nm_ref[...], nv_ref[...] = _adam_math(w_ref[...], g_ref[...], m_ref[...], v_ref[...])

    spec = _full(w.shape)
    out = jax.ShapeDtypeStruct(w.shape, F32)
    return pl.pallas_call(
        body, name="adamw_small", grid=(1,), in_specs=[spec] * 4, out_specs=[spec] * 3, out_shape=[out] * 3,
    )(w, g, m, v)


TOKEN_TILE = 512
MID_TILE = 256
ADAM_ROWS = 128


def _local_grads(x, target, g1, w_in_blocks, conv_w, sinks, g_attn, g_conv, g2, g3, g4, shards, order):
    t = x.shape[0]
    tm = min(TOKEN_TILE, t)
    rope = _rope_tables(t)
    qkv, gates, mconv, w_in, gathered = _in_proj_fwd(x, g1, w_in_blocks, conv_w, g_conv, rope, tm, shards,
                                                     (False, True, False))
    attn, mattn, (w_out, w_up, w_down) = _attn_fwd(qkv, sinks, g_attn, shards, gathered)
    actt, dup, hn2t, dmo, dmix, dh, dmixed, small_mid = _mid(
        mattn, mconv, x, target, g2, g3, g4, w_out.reshape(D_MODEL, D_MODEL),
        w_up, w_down.reshape(D_FF, D_MODEL), min(MID_TILE, t))
    up_own, up_sib, up_sums = _dw_pair_sums((hn2t, dup), order, "up", "dw_up")
    down_own, down_sib, down_sums = _dw_pair_sums((actt, dmo), order, "down", "dw_down")
    out_own, out_sib, out_sums = _dw_pair_sums((mattn, mconv, dmix), order, "out", "dw_out")
    dattn, dgates, small_mix, up_far = _mix_bwd(dmixed, attn, gates, g_attn, g_conv, conv_w, tm, up_sums)
    dqkv, dsink, down_far = _attn_bwd(qkv, dattn, sinks, rope, down_sums)
    grad_x, dw_in, small_in, out_far = _in_proj_bwd(dqkv, dgates, x, dh, g1, w_in, tm, out_sums)
    dw_out, dw_up, dw_down = (out_own, out_sib, out_far), (up_own, up_sib, up_far), (down_own, down_sib, down_far)
    return grad_x, dw_in, dw_out, dw_up, dw_down, (small_mid, small_mix, dsink, small_in)


def _pack_small(small_mid, small_mix, dsink, small_in):
    z = lambda n: jnp.zeros((1, n), F32)
    rows = [
        small_mid[ROW_LOSS:ROW_LOSS + 1],
        small_in[0:1],
        small_mid[ROW_G2:ROW_G2 + 1],
        small_mid[ROW_G3:ROW_G3 + 1],
        small_mid[ROW_G4:ROW_G4 + 1],
        jnp.concatenate([small_mix[ROW_GATTN:ROW_GATTN + 1], small_mix[ROW_GCONV:ROW_GCONV + 1]], axis=1),
        jnp.concatenate([small_mix[ROW_CW0:ROW_CW0 + 1], small_mix[ROW_CW0 + 1:ROW_CW0 + 2]], axis=1),
        jnp.concatenate([small_mix[ROW_CW0 + 2:ROW_CW0 + 3], dsink[0:1, :], z(D_MODEL - CONV_W - 128)], axis=1),
    ]
    return jnp.concatenate(rows, axis=0)


def kernel(x, pre_mix_norm, w_in, conv_w, attn_sinks, attn_group_norm, conv_group_norm, w_out, post_mix_norm, pre_mlp_norm, w_up, w_down, post_mlp_norm, loss_target, m_pre_mix_norm, m_w_in, m_conv_w, m_attn_sinks, m_attn_group_norm, m_conv_group_norm, m_w_out, m_post_mix_norm, m_pre_mlp_norm, m_w_up, m_w_down, m_post_mlp_norm, v_pre_mix_norm, v_w_in, v_conv_w, v_attn_sinks, v_attn_group_norm, v_conv_group_norm, v_w_out, v_post_mix_norm, v_pre_mlp_norm, v_w_up, v_w_down, v_post_mlp_norm):
    xi, yi, ci = _mesh_pos()
    chip = 2 * xi + yi
    dev = 2 * chip + ci

    order = _block_order(dev)

    gw_in, gconv = _all_gather([w_in[0].astype(BF16), conv_w[0]], "gather_w_in")
    conv_full = gconv.transpose(1, 0, 2).reshape(3, CONV_W)
    shards = [w_out[0].astype(BF16), w_up[0].astype(BF16), w_down[0].astype(BF16)]

    grad_x, dw_in, dw_out, dw_up, dw_down, smalls = _local_grads(
        x[0], loss_target[0], pre_mix_norm, gw_in, conv_full, attn_sinks, attn_group_norm, conv_group_norm,
        post_mix_norm, pre_mlp_norm, post_mlp_norm, shards, order)

    small = _sum_devices(_all_gather([_pack_small(*smalls)], "gather_small")[0])
    loss = (0.5 / D_MODEL) * jnp.sum(small[0])

    big = {}
    for name, w, m, v, (own, sib, far) in zip(
            ("w_in", "w_out", "w_up", "w_down"), (w_in, w_out, w_up, w_down), (m_w_in, m_w_out, m_w_up, m_w_down),
            (v_w_in, v_w_out, v_w_up, v_w_down), (dw_in, dw_out, dw_up, dw_down)):
        big[name] = [a[None] for a in _adamw_reduced(w[0], m[0], v[0], own, sib, far, ADAM_ROWS)]

    conv_g = lax.dynamic_slice(
        jnp.stack([small[6, :CONV_W], small[6, CONV_W:], small[7, :CONV_W]]), (0, dev * (CONV_W // N_DEV)),
        (3, CONV_W // N_DEV))
    pad = lambda a, n: jnp.pad(a.reshape(1, -1), ((0, 0), (0, n - a.size)))
    small_names = ("pre_mix_norm", "post_mix_norm", "pre_mlp_norm", "post_mlp_norm")
    small_w = {"pre_mix_norm": (pre_mix_norm, m_pre_mix_norm, v_pre_mix_norm),
               "post_mix_norm": (post_mix_norm, m_post_mix_norm, v_post_mix_norm),
               "pre_mlp_norm": (pre_mlp_norm, m_pre_mlp_norm, v_pre_mlp_norm),
               "post_mlp_norm": (post_mlp_norm, m_post_mlp_norm, v_post_mlp_norm)}

    def pack(k):
        rows = [small_w[nm][k] for nm in small_names]
        rows.append(jnp.concatenate([(attn_group_norm, m_attn_group_norm, v_attn_group_norm)[k],
                                     (conv_group_norm, m_conv_group_norm, v_conv_group_norm)[k]], axis=1))
        rows.append(pad((conv_w, m_conv_w, v_conv_w)[k], D_MODEL))
        rows.append(pad((attn_sinks, m_attn_sinks, v_attn_sinks)[k], D_MODEL))
        rows.append(jnp.zeros((1, D_MODEL), F32))
        return jnp.concatenate(rows, axis=0)

    g_small = jnp.concatenate(
        [small[1:6], pad(conv_g, D_MODEL), pad(small[7, CONV_W:CONV_W + N_HEADS], D_MODEL), jnp.zeros((1, D_MODEL), F32)],
        axis=0)
    d_small, nm_small, nv_small = _adamw_small(pack(0), g_small, pack(1), pack(2))

    def unpack(a):
        nconv = 3 * CONV_W // N_DEV
        return {"pre_mix_norm": a[0:1], "post_mix_norm": a[1:2], "pre_mlp_norm": a[2:3], "post_mlp_norm": a[3:4],
                "attn_group_norm": a[4:5, :ATTN_W], "conv_group_norm": a[4:5, ATTN_W:],
                "conv_w": a[5, :nconv].reshape(1, 3, CONV_W // N_DEV), "attn_sinks": a[6:7, :N_HEADS]}

    order = ("pre_mix_norm", "w_in", "conv_w", "attn_sinks", "attn_group_norm", "conv_group_norm", "w_out",
             "post_mix_norm", "pre_mlp_norm", "w_up", "w_down", "post_mlp_norm")
    outs = []
    for k, a in enumerate((g_small, d_small, nm_small, nv_small)):
        sm = unpack(a)
        outs += [big[nm][k] if nm in big else sm[nm] for nm in order]
    return (loss, grad_x[None], *outs)
```

```python
import functools

import jax
import jax.numpy as jnp
import numpy as np
from jax import lax
from jax.experimental import pallas as pl
from jax.experimental.pallas import tpu as pltpu

F32 = jnp.float32
BF16 = jnp.bfloat16

D_MODEL = 1024
HEAD_DIM = 64
ATTN_W = 512
CONV_W = 512
N_HEADS = 8
N_KV = 2
GROUP = 4
KV_W = 128
QKV_W = ATTN_W + 2 * KV_W
GATES_W = 3 * CONV_W
IN_COLS = QKV_W + GATES_W
D_FF = 4096
FF_CHUNK = 512
N_FF_CHUNKS = D_FF // FF_CHUNK
BLOCK = 128
ROT_HALF = 8
ROPE_THETA = 500000.0
NORM_EPS = 1e-6
NEG_INF = -1e30
ATTN_SCALE = 0.125
N_DEV = 8
N_CHIPS = 4
IN_SHARD = IN_COLS // N_DEV

ADAM_LR = 0.001
ADAM_B1 = 0.9
ADAM_B2 = 0.999
ADAM_EPS = 1e-08
ADAM_WD = 0.01
ADAM_STEP = 10

V7X_VMEM_BYTES = 64 * 1024 * 1024
VMEM_LIMIT = V7X_VMEM_BYTES - 2 * 1024 * 1024

MESH = pl.DeviceIdType.MESH
HBM_SPEC = pl.BlockSpec(memory_space=pltpu.HBM)


def _params(*sem):
    return pltpu.CompilerParams(dimension_semantics=sem, vmem_limit_bytes=VMEM_LIMIT)


def _mm(a, b):
    return jnp.dot(a, b, preferred_element_type=F32)


def _mm_nt(a, b):
    return lax.dot_general(a, b, (((1,), (1,)), ((), ())), preferred_element_type=F32)


def _mm_tn(a, b):
    return lax.dot_general(a, b, (((0,), (0,)), ((), ())), preferred_element_type=F32)


def _inv_rms(x):
    return lax.rsqrt(jnp.mean(x * x, axis=-1, keepdims=True) + NORM_EPS)


def _rms_bwd(xhat, r, gain, dy):
    gy = dy * gain
    return r * (gy - xhat * jnp.mean(gy * xhat, axis=-1, keepdims=True)), dy * xhat


def _colsum(a):
    return jnp.sum(a, axis=0, keepdims=True)


def _full(shape):
    zeros = (0,) * len(shape)
    return pl.BlockSpec(shape, lambda *_: zeros)


def _resident(shape):
    zeros = (0,) * len(shape)
    return pl.BlockSpec(shape, lambda *_: zeros, pipeline_mode=pl.Buffered(1))


def _rope_tables(t):
    pos = np.arange(t, dtype=np.float32)
    inv_freq = (ROPE_THETA ** (-np.arange(0, 2 * ROT_HALF, 2, dtype=np.float64) / (2 * ROT_HALF))).astype(np.float32)
    ang = (pos[:, None] * inv_freq[None, :]).astype(np.float64)
    cos, sin = np.cos(ang).astype(np.float32), np.sin(ang).astype(np.float32)
    zeros8 = np.zeros((t, ROT_HALF), np.float32)
    rest = np.zeros((t, HEAD_DIM - 2 * ROT_HALF), np.float32)
    c_head = np.concatenate([cos, cos, rest + 1.0], axis=1)
    s1_head = np.concatenate([zeros8, sin, rest], axis=1)
    s2_head = np.concatenate([-sin, zeros8, rest], axis=1)
    two = lambda a: jnp.asarray(np.concatenate([a, a], axis=1))
    return two(c_head), two(s1_head), two(s2_head)


def _rope(v, c, s1, s2):
    return v * c + pltpu.roll(v, ROT_HALF, 1) * s1 + pltpu.roll(v, 128 - ROT_HALF, 1) * s2


def _rope_transpose(dv, c, s1, s2):
    return dv * c + pltpu.roll(dv * s1, 128 - ROT_HALF, 1) + pltpu.roll(dv * s2, ROT_HALF, 1)


def _shift_rows_down(u, prev, k):
    row = lax.broadcasted_iota(jnp.int32, u.shape, 0)
    out = pltpu.roll(u, k, 0)
    for r in range(k):
        out = jnp.where(row == r, prev[8 - k + r:8 - k + r + 1, :], out)
    return out


def _shift_rows_up(u, nxt, k):
    n = u.shape[0]
    row = lax.broadcasted_iota(jnp.int32, u.shape, 0)
    out = pltpu.roll(u, n - k, 0)
    for r in range(k):
        out = jnp.where(row == n - k + r, nxt[r:r + 1, :], out)
    return out


def _conv3(u, u1, u2, w):
    return (w[0:1, :] * u2 + w[1:2, :] * u1) + w[2:3, :] * u


def _mesh_pos():
    return lax.axis_index("x"), lax.axis_index("y"), lax.axis_index("c")


def _slot(ref, pos):
    dev = 4 * pos[0] + 2 * pos[1] + pos[2]
    if len(ref.shape) == 2:
        width = ref.shape[1] // N_DEV
        return ref.at[:, pl.ds(pl.multiple_of(dev * width, width), width)]
    return ref.at[dev]


def _gathered_shape(shard, by_cols):
    if by_cols:
        return jax.ShapeDtypeStruct((shard.shape[0], N_DEV * shard.shape[1]), shard.dtype)
    return jax.ShapeDtypeStruct((N_DEV,) + shard.shape, shard.dtype)


def _push(src, dst, sems, k, to):
    send_sems, recv_sems = sems
    return pltpu.make_async_remote_copy(src_ref=src, dst_ref=dst, send_sem=send_sems.at[k], recv_sem=recv_sems.at[k],
                                        device_id=to, device_id_type=MESH)


def _gather_now(shards, outs, send_sems, recv_sems, local_sems):
    n = len(shards)
    x, y, c = _mesh_pos()
    me, sibling = (x, y, c), (x, y, 1 - c)
    chips = [(1 - x, y), (x, 1 - y), (1 - x, 1 - y)]

    def copy(i, k, block, to, src=None):
        dst = _slot(outs[i], block)
        return _push(dst if src is None else src, dst, (send_sems, recv_sems), 7 * i + k, to)

    mine = [pltpu.make_async_copy(shards[i], _slot(outs[i], me), local_sems.at[i]) for i in range(n)]
    for cp in mine:
        cp.start()
    first = []
    for i in range(n):
        first.append(copy(i, 0, me, sibling, src=shards[i]))
        first += [copy(i, 1 + j, me, (*chip, c), src=shards[i]) for j, chip in enumerate(chips)]
    for cp in first:
        cp.start()
    passed = []
    for j, chip in enumerate(chips):
        for i in range(n):
            copy(i, 1 + j, (*chip, c), me).wait_recv()
            cp = copy(i, 4 + j, (*chip, c), sibling)
            cp.start()
            passed.append(cp)
    for i in range(n):
        copy(i, 0, sibling, me).wait_recv()
        for j, chip in enumerate(chips):
            copy(i, 4 + j, (*chip, 1 - c), me).wait_recv()
    for cp in first + passed:
        cp.wait_send()
    for cp in mine:
        cp.wait()


def _gather_near(first, last, shards, outs, sems, local_sems):
    x, y, c = _mesh_pos()
    me, peers = (x, y, c), [(x, y, 1 - c), (1 - x, y, c), (x, 1 - y, c)]
    n = len(shards)
    local = [pltpu.make_async_copy(shards[i], _slot(outs[i], me), local_sems.at[i]) for i in range(n)]
    sends = [_push(shards[i], _slot(outs[i], me), sems, 3 * i + k, peers[k]) for i in range(n) for k in range(3)]
    arrivals = [_push(shards[i], _slot(outs[i], peers[k]), sems, 3 * i + k, peers[k]) for i in range(n) for k in range(3)]

    @pl.when(first)
    def _():
        for cp in local + sends:
            cp.start()

    @pl.when(last)
    def _():
        for cp in sends:
            cp.wait_send()
        for cp in arrivals:
            cp.wait_recv()
        for cp in local:
            cp.wait()


def _gather_far(first, last, shards, ins, outs, sems):
    x, y, c = _mesh_pos()
    me, sibling = (x, y, c), (x, y, 1 - c)
    chips = [(1 - x, y), (x, 1 - y), (1 - x, 1 - y)]
    n = len(shards)
    diag_send = [_push(shards[i], _slot(outs[i], me), sems, 4 * i, (*chips[2], c)) for i in range(n)]
    diag_arrival = [_push(shards[i], _slot(outs[i], (*chips[2], c)), sems, 4 * i, (*chips[2], c)) for i in range(n)]
    passed = [[_push(_slot(ins[i], (*chips[j], c)), _slot(outs[i], (*chips[j], c)), sems, 4 * i + 1 + j, sibling)
               for i in range(n)] for j in range(3)]
    from_sibling = [_push(shards[i], _slot(outs[i], (*chips[j], 1 - c)), sems, 4 * i + 1 + j, sibling)
                    for i in range(n) for j in range(3)]

    @pl.when(first)
    def _():
        for cp in diag_send + passed[0] + passed[1]:
            cp.start()

    @pl.when(last)
    def _():
        for cp in diag_arrival:
            cp.wait_recv()
        for cp in passed[2]:
            cp.start()
        for cp in from_sibling:
            cp.wait_recv()
        for cp in diag_send + passed[0] + passed[1] + passed[2]:
            cp.wait_send()


def _in_proj_fwd(x, g1, w_in, conv_w, g_conv, rope, tm, shards, by_cols):
    t = x.shape[0]
    rc, rs1, rs2 = rope
    n = len(shards)

    def body(*refs):
        x_ref, g1_ref, w_ref, cw_ref, gc_ref, c_ref, s1_ref, s2_ref = refs[:8]
        shard_refs = refs[8:8 + n]
        qkv_ref, gates_ref, mconv_ref, w_full_ref, cw_full_ref = refs[8 + n:13 + n]
        gathered = refs[13 + n:13 + 2 * n]
        carry_ref, w_land, cw_land = refs[13 + 2 * n:16 + 2 * n]
        now_sems = refs[16 + 2 * n:19 + 2 * n]
        step = pl.program_id(0)
        if n:
            _gather_near(step == 0, step == pl.num_programs(0) - 1, shard_refs, gathered, refs[19 + 2 * n:21 + 2 * n],
                         refs[21 + 2 * n])

        @pl.when(step == 0)
        def _():
            carry_ref[...] = jnp.zeros_like(carry_ref)
            _gather_now([w_ref, cw_ref], [w_land, cw_land], *now_sems)
            conv_shard = CONV_W // N_DEV
            for d in range(N_DEV):
                w_full_ref[:, IN_SHARD * d:IN_SHARD * (d + 1)] = w_land[d]
                cw_full_ref[:, conv_shard * d:conv_shard * (d + 1)] = cw_land[d]

        xv = x_ref[...]
        hn = ((xv * _inv_rms(xv)) * g1_ref[...]).astype(BF16)
        proj = _mm(hn, w_full_ref[...])
        c, s1, s2 = c_ref[...], s1_ref[...], s2_ref[...]
        for ci in range((ATTN_W + KV_W) // 128):
            sl = slice(128 * ci, 128 * (ci + 1))
            qkv_ref[:, sl] = _rope(proj[:, sl], c, s1, s2).astype(BF16)
        qkv_ref[:, ATTN_W + KV_W:QKV_W] = proj[:, ATTN_W + KV_W:QKV_W].astype(BF16)
        gates = proj[:, QKV_W:]
        gates_ref[...] = gates
        gb, gcc, xin = gates[:, :CONV_W], gates[:, CONV_W:2 * CONV_W], gates[:, 2 * CONV_W:]
        u = gcc * xin
        prev = carry_ref[...]
        conv = gb * _conv3(u, _shift_rows_down(u, prev, 1), _shift_rows_down(u, prev, 2), cw_full_ref[...])
        carry_ref[...] = u[tm - 8:tm, :]
        mconv_ref[...] = ((conv * _inv_rms(conv)) * gc_ref[...]).astype(BF16)

    tile = lambda w_: pl.BlockSpec((tm, w_), lambda i: (i, 0))
    sems = lambda k: pltpu.SemaphoreType.DMA((k,))
    res = pl.pallas_call(
        body, name="in_proj_fwd", grid=(t // tm,),
        in_specs=[tile(D_MODEL), _full((1, D_MODEL)), HBM_SPEC, HBM_SPEC, _full((1, CONV_W)), tile(128), tile(128),
                  tile(128)] + [HBM_SPEC] * n,
        out_specs=[tile(QKV_W), tile(GATES_W), tile(CONV_W), _full((D_MODEL, IN_COLS)), _full((3, CONV_W))]
        + [HBM_SPEC] * n,
        out_shape=[jax.ShapeDtypeStruct((t, QKV_W), BF16), jax.ShapeDtypeStruct((t, GATES_W), F32),
                   jax.ShapeDtypeStruct((t, CONV_W), BF16), jax.ShapeDtypeStruct((D_MODEL, IN_COLS), BF16),
                   jax.ShapeDtypeStruct((3, CONV_W), F32)]
        + [_gathered_shape(s, cols) for s, cols in zip(shards, by_cols)],
        scratch_shapes=[pltpu.VMEM((8, CONV_W), F32), pltpu.VMEM((N_DEV,) + w_in.shape, BF16),
                        pltpu.VMEM((N_DEV,) + conv_w.shape, F32), sems(14), sems(14), sems(2)]
        + ([sems(3 * n), sems(3 * n), sems(n)] if n else []),
        compiler_params=_params("arbitrary"),
    )(x, g1, w_in, conv_w, g_conv, rc, rs1, rs2, *shards)
    return res[0], res[1], res[2], res[3], res[4], list(res[5:])


GROUP_ROWS = GROUP * BLOCK


def _attn_mask(has_prev):
    row = lax.broadcasted_iota(jnp.int32, (GROUP_ROWS, 2 * BLOCK), 0) & (BLOCK - 1)
    col = lax.broadcasted_iota(jnp.int32, (GROUP_ROWS, 2 * BLOCK), 1)
    band = (col > row) & (col <= row + BLOCK)
    return band if has_prev is True else band & ((col >= BLOCK) | has_prev)


def _stack_heads(a, g):
    return jnp.concatenate([a[:, HEAD_DIM * (GROUP * g + hh):HEAD_DIM * (GROUP * g + hh + 1)] for hh in range(GROUP)], axis=0)


def _unstack_heads(a):
    return jnp.concatenate([a[BLOCK * hh:BLOCK * (hh + 1), :] for hh in range(GROUP)], axis=1)


def _group_sinks(sink_ref, g):
    head = lax.broadcasted_iota(jnp.int32, (GROUP_ROWS, 1), 0) // BLOCK
    out = jnp.full((GROUP_ROWS, 1), sink_ref[0, GROUP * g], F32)
    for hh in range(1, GROUP):
        out = jnp.where(head == hh, sink_ref[0, GROUP * g + hh], out)
    return out


def _attn_probs(qs, kk, sink, valid):
    s = jnp.where(valid, _mm_nt(qs, kk) * ATTN_SCALE, NEG_INF)
    m = jnp.maximum(jnp.max(s, axis=-1, keepdims=True), sink)
    p = jnp.exp(s - m)
    psink = jnp.exp(sink - m)
    inv_l = 1.0 / (jnp.sum(p, axis=-1, keepdims=True) + psink)
    return p * inv_l, psink * inv_l


ATTN_STEP_BLOCKS = 4
ATTN_STEP = ATTN_STEP_BLOCKS * BLOCK
ATTN_KEYS = ATTN_STEP + BLOCK


def _qkv_specs(order):
    prev = lambda i: jnp.maximum(ATTN_STEP_BLOCKS * order(i) - 1, 0)
    kcol, vcol = ATTN_W // KV_W, ATTN_W // KV_W + 1
    return [pl.BlockSpec((ATTN_STEP, ATTN_W), lambda i: (order(i), 0)),
            pl.BlockSpec((BLOCK, KV_W), lambda i: (prev(i), kcol)), pl.BlockSpec((ATTN_STEP, KV_W), lambda i: (order(i), kcol)),
            pl.BlockSpec((BLOCK, KV_W), lambda i: (prev(i), vcol)), pl.BlockSpec((ATTN_STEP, KV_W), lambda i: (order(i), vcol))]


def _attn_fwd(qkv, sinks, g_attn, shards, gathered):
    t = qkv.shape[0]
    n = len(shards)

    def body(*refs):
        sink_ref, q_ref, kp_ref, kc_ref, vp_ref, vc_ref, ga_ref = refs[:7]
        attn_ref, mattn_ref = refs[7 + 2 * n:9 + 2 * n]
        step = pl.program_id(0)
        if n:
            _gather_far(step == 0, step == pl.num_programs(0) - 1, refs[7:7 + n], refs[7 + n:7 + 2 * n],
                        refs[9 + 2 * n:9 + 3 * n], refs[9 + 3 * n:11 + 3 * n])
        q = q_ref[...]
        keys = jnp.concatenate([kp_ref[...], kc_ref[...]], axis=0)
        vals = jnp.concatenate([vp_ref[...], vc_ref[...]], axis=0)
        sink = [_group_sinks(sink_ref, g) for g in range(N_KV)]
        gain = ga_ref[...]
        for b in range(ATTN_STEP_BLOCKS):
            rows, window = slice(BLOCK * b, BLOCK * (b + 1)), slice(BLOCK * b, BLOCK * (b + 2))
            valid = _attn_mask(True if b else step > 0)
            outs = []
            for g in range(N_KV):
                gs = slice(HEAD_DIM * g, HEAD_DIM * (g + 1))
                probs, _ = _attn_probs(_stack_heads(q[rows], g), keys[window, gs], sink[g], valid)
                outs.append(_unstack_heads(_mm(probs.astype(BF16), vals[window, gs])))
            attn = jnp.concatenate(outs, axis=1)
            attn_ref[rows, :] = attn
            mattn_ref[rows, :] = ((attn * _inv_rms(attn)) * gain).astype(BF16)

    blk = pl.BlockSpec((ATTN_STEP, ATTN_W), lambda j: (j, 0))
    res = pl.pallas_call(
        body, name="attn_fwd", grid=(t // ATTN_STEP,),
        in_specs=[pl.BlockSpec(memory_space=pltpu.SMEM)] + _qkv_specs(lambda j: j) + [_full((1, ATTN_W))]
        + [HBM_SPEC] * (2 * n),
        out_specs=[blk, blk] + [HBM_SPEC] * n,
        out_shape=[jax.ShapeDtypeStruct((t, ATTN_W), F32), jax.ShapeDtypeStruct((t, ATTN_W), BF16)]
        + [jax.ShapeDtypeStruct(g.shape, g.dtype) for g in gathered],
        input_output_aliases={7 + n + i: 2 + i for i in range(n)},
        scratch_shapes=[pltpu.SemaphoreType.DMA((4 * n,)), pltpu.SemaphoreType.DMA((4 * n,))] if n else [],
        compiler_params=_params("arbitrary"),
    )(sinks, qkv, qkv, qkv, qkv, qkv, g_attn, *shards, *gathered)
    return res[0], res[1], list(res[2:])


SMALL_ROWS = 8
ROW_LOSS, ROW_G2, ROW_G3, ROW_G4 = 0, 1, 2, 3


def _mid(mattn, mconv, x, target, g2, g3, g4, w_out, w_up, w_down, tm):
    t = x.shape[0]

    def body(ma_ref, mc_ref, x_ref, t_ref, g2_ref, g3_ref, g4_ref, wo_ref, wu_ref, wd_ref,
             actt_ref, dup_ref, hn2t_ref, dmo_ref, dmix_ref, dh_ref, dmixed_ref, small_ref, up_ref):
        @pl.when(pl.program_id(0) == 0)
        def _():
            small_ref[...] = jnp.zeros_like(small_ref)

        g2, g3, g4 = g2_ref[...], g3_ref[...], g4_ref[...]
        mix_out = _mm(ma_ref[...], wo_ref[0:ATTN_W, :]) + _mm(mc_ref[...], wo_ref[ATTN_W:, :])
        r2 = _inv_rms(mix_out)
        mo_hat = mix_out * r2
        h = x_ref[...] + mo_hat * g2
        r3 = _inv_rms(h)
        h_hat = h * r3
        hn2 = (h_hat * g3).astype(BF16)
        hn2t_ref[...] = hn2.T
        up = jnp.maximum(_mm(hn2, wu_ref[...]), 0.0)
        up_ref[...] = up.astype(BF16)
        act = (up * up).astype(BF16)
        actt_ref[...] = act.T
        mlp = _mm(act, wd_ref[...])
        r4 = _inv_rms(mlp)
        ml_hat = mlp * r4
        err = (h + ml_hat * g4) - t_ref[...]
        d_out = err * (1.0 / D_MODEL)
        d_mlp, dg4 = _rms_bwd(ml_hat, r4, g4, d_out)
        dmo = d_mlp.astype(BF16)
        dmo_ref[...] = dmo
        dup = (_mm_nt(dmo, wd_ref[...]) * (2.0 * up_ref[...].astype(F32))).astype(BF16)
        dup_ref[...] = dup
        dhn2 = _mm_nt(dup, wu_ref[...])
        dh_norm, dg3 = _rms_bwd(h_hat, r3, g3, dhn2)
        dh = d_out + dh_norm
        dh_ref[...] = dh
        d_mix, dg2 = _rms_bwd(mo_hat, r2, g2, dh)
        dmix = d_mix.astype(BF16)
        dmix_ref[...] = dmix
        dmixed_ref[...] = _mm_nt(dmix, wo_ref[...])
        small_ref[ROW_LOSS:ROW_LOSS + 1, :] += _colsum(err * err)
        small_ref[ROW_G2:ROW_G2 + 1, :] += _colsum(dg2)
        small_ref[ROW_G3:ROW_G3 + 1, :] += _colsum(dg3)
        small_ref[ROW_G4:ROW_G4 + 1, :] += _colsum(dg4)

    tile = lambda n: pl.BlockSpec((tm, n), lambda i: (i, 0))
    cols = lambda n: pl.BlockSpec((n, tm), lambda i: (0, i))
    gain = _full((1, D_MODEL))
    return pl.pallas_call(
        body, name="mid_fwd_bwd", grid=(t // tm,),
        in_specs=[tile(ATTN_W), tile(CONV_W), tile(D_MODEL), tile(D_MODEL), gain, gain, gain,
                  _resident((D_MODEL, D_MODEL)), _resident((D_MODEL, D_FF)), _resident((D_FF, D_MODEL))],
        out_specs=[cols(D_FF), tile(D_FF), cols(D_MODEL), tile(D_MODEL), tile(D_MODEL), tile(D_MODEL), tile(D_MODEL),
                   _full((SMALL_ROWS, D_MODEL))],
        out_shape=[jax.ShapeDtypeStruct((D_FF, t), BF16), jax.ShapeDtypeStruct((t, D_FF), BF16),
                   jax.ShapeDtypeStruct((D_MODEL, t), BF16), jax.ShapeDtypeStruct((t, D_MODEL), BF16),
                   jax.ShapeDtypeStruct((t, D_MODEL), BF16), jax.ShapeDtypeStruct((t, D_MODEL), F32),
                   jax.ShapeDtypeStruct((t, D_MODEL), F32), jax.ShapeDtypeStruct((SMALL_ROWS, D_MODEL), F32)],
        scratch_shapes=[pltpu.VMEM((tm, D_FF), BF16)],
        compiler_params=_params("arbitrary"),
    )(mattn, mconv, x, target, g2, g3, g4, w_out, w_up, w_down)


CHIP_FLIPS = ((1, 1), (1, 0), (0, 1))


def _block_order(dev):
    chip_masks = [4 * fx + 2 * fy for fx, fy in CHIP_FLIPS]
    masks = [m + 1 for m in chip_masks] + [1] + chip_masks + [0]
    return jnp.bitwise_xor(dev, jnp.asarray(masks, jnp.int32)).astype(jnp.int32)


def _other_chips(x, y, c):
    return [(1 - x if fx else x, 1 - y if fy else y, c) for fx, fy in CHIP_FLIPS]


def _dw_pair_sums(operands, order, which, name):
    t = operands[-1].shape[0]
    n_far = len(CHIP_FLIPS)
    n_in = len(operands)
    out_chunk = D_MODEL // N_DEV
    if which == "up":
        rows, cols = D_MODEL, FF_CHUNK
        in_specs = [_resident((D_MODEL, t)), pl.BlockSpec((t, FF_CHUNK), lambda s, order_ref: (0, order_ref[s]))]
    elif which == "down":
        rows, cols = FF_CHUNK, D_MODEL
        in_specs = [pl.BlockSpec((FF_CHUNK, t), lambda s, order_ref: (order_ref[s], 0)), _resident((t, D_MODEL))]
    else:
        rows, cols = out_chunk, D_MODEL
        half = pl.BlockSpec((t, out_chunk), lambda s, order_ref: (0, order_ref[s] % (N_DEV // 2)))
        in_specs = [half, half, _resident((t, D_MODEL))]

    def body(order_ref, *refs):
        own_ref, from_sib_ref, pair_ref, send_buf, land_buf, send_sems, recv_sems = refs[n_in:]
        s_now = pl.program_id(0)
        x, y, c = _mesh_pos()
        sibling = (x, y, 1 - c)
        sems = (send_sems, recv_sems)

        def hand_over(k):
            dst = land_buf.at[k] if k < n_far else from_sib_ref
            return _push(send_buf.at[k], dst, sems, k, sibling)

        if which == "out":
            ma_ref, mc_ref, b_ref = refs[:n_in]
            block = lax.cond(order_ref[s_now] < N_DEV // 2, lambda: _mm_tn(ma_ref[...], b_ref[...]),
                             lambda: _mm_tn(mc_ref[...], b_ref[...]))
        else:
            block = _mm(refs[0][...], refs[1][...])
        for k in range(n_far + 1):
            @pl.when(s_now == k)
            def _():
                send_buf[k] = block.astype(BF16)
                hand_over(k).start()

        for k in range(n_far):
            @pl.when(s_now == n_far + 1 + k)
            def _():
                hand_over(k).wait_recv()
                pair_ref[...] = (block + land_buf[k].astype(F32)).astype(BF16)

        @pl.when(s_now == N_DEV - 1)
        def _():
            own_ref[...] = block
            for k in range(n_far + 1):
                hand_over(k).wait_send()
            hand_over(n_far).wait_recv()

    return pl.pallas_call(
        body, name=name,
        grid_spec=pltpu.PrefetchScalarGridSpec(
            num_scalar_prefetch=1, grid=(N_DEV,), in_specs=in_specs,
            out_specs=[pl.BlockSpec((rows, cols), lambda s, order_ref: (0, 0)), HBM_SPEC,
                       pl.BlockSpec((None, rows, cols), lambda s, order_ref: (jnp.clip(s - n_far - 1, 0, n_far - 1), 0, 0))],
            scratch_shapes=[pltpu.VMEM((n_far + 1, rows, cols), BF16), pltpu.VMEM((n_far, rows, cols), BF16),
                            pltpu.SemaphoreType.DMA((n_far + 1,)), pltpu.SemaphoreType.DMA((n_far + 1,))]),
        out_shape=[jax.ShapeDtypeStruct((rows, cols), F32), jax.ShapeDtypeStruct((rows, cols), BF16),
                   jax.ShapeDtypeStruct((n_far, rows, cols), BF16)],
        compiler_params=_params("arbitrary"),
    )(order, *operands)


def _chip_exchange_beside(first, last, sums, outs, sems):
    chips = _other_chips(*_mesh_pos())
    copies = [_push(sums[i].at[k], outs[i].at[k], sems, len(chips) * i + k, chip)
              for i in range(len(sums)) for k, chip in enumerate(chips)]

    @pl.when(first)
    def _():
        for cp in copies:
            cp.start()

    @pl.when(last)
    def _():
        for cp in copies:
            cp.wait()


ROW_GATTN, ROW_GCONV, ROW_CW0 = 0, 1, 2


def _mix_bwd(dmixed, attn, gates, g_attn, g_conv, conv_w, tm, sums):
    t = attn.shape[0]
    n = t // tm
    rev = lambda i: n - 1 - i

    def body(dm_ref, attn_ref, gates_ref, gprev_ref, ga_ref, gc_ref, cw_ref, sums_ref, dattn_ref, dgates_ref, small_ref,
             arrived_ref, carry_ref, send_sems, recv_sems):
        i = pl.program_id(0)
        _chip_exchange_beside(i == 0, i == n - 1, [sums_ref], [arrived_ref], (send_sems, recv_sems))

        @pl.when(i == 0)
        def _():
            small_ref[...] = jnp.zeros_like(small_ref)
            carry_ref[...] = jnp.zeros_like(carry_ref)

        dm = dm_ref[...]
        a = attn_ref[...]
        ra = _inv_rms(a)
        a_hat = a * ra
        dattn, dga = _rms_bwd(a_hat, ra, ga_ref[...], dm[:, :ATTN_W])
        dattn_ref[...] = dattn

        gates = gates_ref[...]
        gb, gcc, xin = gates[:, :CONV_W], gates[:, CONV_W:2 * CONV_W], gates[:, 2 * CONV_W:]
        u = gcc * xin
        gp = gprev_ref[...]
        uprev = jnp.where(rev(i) == 0, 0.0, gp[:, CONV_W:2 * CONV_W] * gp[:, 2 * CONV_W:])
        u1, u2 = _shift_rows_down(u, uprev, 1), _shift_rows_down(u, uprev, 2)
        w = cw_ref[...]
        c = _conv3(u, u1, u2, w)
        conv = gb * c
        rcv = _inv_rms(conv)
        c_hat = conv * rcv
        dconv, dgc = _rms_bwd(c_hat, rcv, gc_ref[...], dm[:, ATTN_W:])
        dc = dconv * gb
        nxt = carry_ref[...]
        du = (w[2:3, :] * dc + w[1:2, :] * _shift_rows_up(dc, nxt, 1)) + w[0:1, :] * _shift_rows_up(dc, nxt, 2)
        carry_ref[...] = dc[0:8, :]
        dgates_ref[:, :CONV_W] = (dconv * c).astype(BF16)
        dgates_ref[:, CONV_W:2 * CONV_W] = (du * xin).astype(BF16)
        dgates_ref[:, 2 * CONV_W:] = (du * gcc).astype(BF16)
        small_ref[ROW_GATTN:ROW_GATTN + 1, :] += _colsum(dga)
        small_ref[ROW_GCONV:ROW_GCONV + 1, :] += _colsum(dgc)
        small_ref[ROW_CW0:ROW_CW0 + 1, :] += _colsum(dc * u2)
        small_ref[ROW_CW0 + 1:ROW_CW0 + 2, :] += _colsum(dc * u1)
        small_ref[ROW_CW0 + 2:ROW_CW0 + 3, :] += _colsum(dc * u)

    tile = lambda w_: pl.BlockSpec((tm, w_), lambda i: (rev(i), 0))
    prev8 = pl.BlockSpec((8, GATES_W), lambda i: (jnp.maximum(rev(i) * (tm // 8) - 1, 0), 0))
    return pl.pallas_call(
        body, name="mix_bwd", grid=(n,),
        in_specs=[tile(D_MODEL), tile(ATTN_W), tile(GATES_W), prev8, _full((1, ATTN_W)), _full((1, CONV_W)),
                  _full((3, CONV_W)), HBM_SPEC],
        out_specs=[tile(ATTN_W), tile(GATES_W), _full((SMALL_ROWS, CONV_W)), HBM_SPEC],
        out_shape=[jax.ShapeDtypeStruct((t, ATTN_W), F32), jax.ShapeDtypeStruct((t, GATES_W), BF16),
                   jax.ShapeDtypeStruct((SMALL_ROWS, CONV_W), F32), jax.ShapeDtypeStruct(sums.shape, sums.dtype)],
        scratch_shapes=[pltpu.VMEM((8, CONV_W), F32), pltpu.SemaphoreType.DMA((len(CHIP_FLIPS),)),
                        pltpu.SemaphoreType.DMA((len(CHIP_FLIPS),))],
        compiler_params=_params("arbitrary"),
    )(dmixed, attn, gates, gates, g_attn, g_conv, conv_w, sums)


def _attn_bwd(qkv, dattn, sinks, rope, sums):
    t = qkv.shape[0]
    n_steps = t // ATTN_STEP
    rev = lambda i: n_steps - 1 - i
    rc, rs1, rs2 = rope

    def body(sink_ref, q_ref, kp_ref, kc_ref, vp_ref, vc_ref, do_ref, c_ref, s1_ref, s2_ref, sums_ref,
             dqkv_ref, dsink_ref, arrived_ref, ck_ref, cv_ref, kacc_ref, vacc_ref, send_sems, recv_sems):
        i = pl.program_id(0)
        _chip_exchange_beside(i == 0, i == n_steps - 1, [sums_ref], [arrived_ref], (send_sems, recv_sems))

        @pl.when(i == 0)
        def _():
            dsink_ref[...] = jnp.zeros_like(dsink_ref)
            ck_ref[...] = jnp.zeros_like(ck_ref)
            cv_ref[...] = jnp.zeros_like(cv_ref)

        kacc_ref[...] = jnp.zeros_like(kacc_ref)
        vacc_ref[...] = jnp.zeros_like(vacc_ref)
        q = q_ref[...]
        dout = do_ref[...].astype(BF16)
        keys = jnp.concatenate([kp_ref[...], kc_ref[...]], axis=0)
        vals = jnp.concatenate([vp_ref[...], vc_ref[...]], axis=0)
        sink = [_group_sinks(sink_ref, g) for g in range(N_KV)]
        c, s1, s2 = c_ref[...], s1_ref[...], s2_ref[...]
        lane = lax.broadcasted_iota(jnp.int32, (1, 128), 1)
        dsink = jnp.zeros((1, 128), F32)
        for b in range(ATTN_STEP_BLOCKS):
            rows, window = slice(BLOCK * b, BLOCK * (b + 1)), slice(BLOCK * b, BLOCK * (b + 2))
            valid = _attn_mask(True if b else rev(i) > 0)
            dq_parts, dk_parts, dv_parts = [], [], []
            for g in range(N_KV):
                gs = slice(HEAD_DIM * g, HEAD_DIM * (g + 1))
                kk, vv = keys[window, gs], vals[window, gs]
                qs, dos = _stack_heads(q[rows], g), _stack_heads(dout[rows], g)
                probs, psink = _attn_probs(qs, kk, sink[g], valid)
                dp = _mm_nt(dos, vv)
                delta = jnp.sum(probs * dp, axis=-1, keepdims=True)
                ds = (probs * (dp - delta) * ATTN_SCALE).astype(BF16)
                sink_terms = psink * delta
                for hh in range(GROUP):
                    head_sum = jnp.sum(sink_terms[BLOCK * hh:BLOCK * (hh + 1), :])
                    dsink = dsink + jnp.where(lane == GROUP * g + hh, -head_sum, 0.0)
                dq_parts.append(_unstack_heads(_mm(ds, kk)))
                dk_parts.append(_mm_tn(ds, qs))
                dv_parts.append(_mm_tn(probs.astype(BF16), dos))
            kacc_ref[window, :] += jnp.concatenate(dk_parts, axis=1)
            vacc_ref[window, :] += jnp.concatenate(dv_parts, axis=1)
            dq = jnp.concatenate(dq_parts, axis=1)
            for ci in range(ATTN_W // 128):
                sl = slice(128 * ci, 128 * (ci + 1))
                dqkv_ref[rows, sl] = _rope_transpose(dq[:, sl], c[rows], s1[rows], s2[rows]).astype(BF16)
        kacc_ref[ATTN_STEP:, :] += ck_ref[...]
        vacc_ref[ATTN_STEP:, :] += cv_ref[...]
        ck_ref[...] = kacc_ref[:BLOCK, :]
        cv_ref[...] = vacc_ref[:BLOCK, :]
        dqkv_ref[:, ATTN_W:ATTN_W + KV_W] = _rope_transpose(kacc_ref[BLOCK:, :], c, s1, s2).astype(BF16)
        dqkv_ref[:, ATTN_W + KV_W:] = vacc_ref[BLOCK:, :].astype(BF16)
        dsink_ref[0:1, :] += dsink

    blk = lambda w_: pl.BlockSpec((ATTN_STEP, w_), lambda i: (rev(i), 0))
    return pl.pallas_call(
        body, name="attn_bwd", grid=(n_steps,),
        in_specs=[pl.BlockSpec(memory_space=pltpu.SMEM)] + _qkv_specs(rev) + [blk(ATTN_W), blk(128), blk(128), blk(128),
                                                                              HBM_SPEC],
        out_specs=[blk(QKV_W), _full((8, 128)), HBM_SPEC],
        out_shape=[jax.ShapeDtypeStruct((t, QKV_W), BF16), jax.ShapeDtypeStruct((8, 128), F32),
                   jax.ShapeDtypeStruct(sums.shape, sums.dtype)],
        scratch_shapes=[pltpu.VMEM((BLOCK, KV_W), F32), pltpu.VMEM((BLOCK, KV_W), F32),
                        pltpu.VMEM((ATTN_KEYS, KV_W), F32), pltpu.VMEM((ATTN_KEYS, KV_W), F32),
                        pltpu.SemaphoreType.DMA((len(CHIP_FLIPS),)), pltpu.SemaphoreType.DMA((len(CHIP_FLIPS),))],
        compiler_params=_params("arbitrary"),
    )(sinks, qkv, qkv, qkv, qkv, qkv, dattn, rc, rs1, rs2, sums)


def _grad_x_tile(dq, dg, x_hat, r, g1, w_ref, dh):
    dhn = _mm_nt(dq, w_ref[:, :QKV_W]) + _mm_nt(dg, w_ref[:, QKV_W:])
    dx, dg1 = _rms_bwd(x_hat, r, g1, dhn)
    return dh + dx, _colsum(dg1)


def _in_proj_bwd(dqkv, dgates, x, dh, g1, w_in, tm, out_sums):
    t = x.shape[0]
    n = t // tm
    n_steps = 2 * n - 1
    n_far = len(CHIP_FLIPS)
    shard = (D_MODEL, IN_SHARD)

    def body(dq_ref, dg_ref, x_ref, dh_ref, g1_ref, w_ref, osums_ref,
             dx_ref, own_ref, sib_ref, far_ref, dg1_ref, oarrived_ref,
             acc_ref, send_buf, land_buf, pair_buf, d2d_send, d2d_recv, ici_send, ici_recv, o_send, o_recv):
        i = pl.program_id(0)
        x_pos, y_pos, c = _mesh_pos()
        my_chip = 2 * x_pos + y_pos
        sibling = (x_pos, y_pos, 1 - c)
        _chip_exchange_beside(i == 0, i == n_steps - 1, [osums_ref], [oarrived_ref], (o_send, o_recv))

        def cols(d):
            return slice(IN_SHARD * d, IN_SHARD * (d + 1))

        def hand_over(chip):
            return _push(send_buf.at[chip], land_buf.at[chip], (d2d_send, d2d_recv), chip, sibling)

        def to_chip(chip, rel):
            return pltpu.make_async_remote_copy(
                src_ref=pair_buf.at[chip], dst_ref=far_ref.at[rel - 1], send_sem=ici_send.at[rel - 1],
                recv_sem=ici_recv.at[rel - 1], device_id=(chip // 2, chip % 2, c), device_id_type=MESH)

        @pl.when(i == 0)
        def _():
            acc_ref[...] = jnp.zeros_like(acc_ref)
            dg1_ref[...] = jnp.zeros_like(dg1_ref)

        xv = x_ref[...]
        r = _inv_rms(xv)
        x_hat = xv * r
        g1 = g1_ref[...]
        dq, dg = dq_ref[...], dg_ref[...]

        @pl.when(i < n)
        def _():
            hn = (x_hat * g1).astype(BF16)
            acc_ref[:, :QKV_W] += _mm_tn(hn, dq)
            acc_ref[:, QKV_W:] += _mm_tn(hn, dg)

        @pl.when(i == n - 1)
        def _():
            for d in range(N_DEV):
                @pl.when(d % 2 != c)
                def _():
                    send_buf[d // 2] = acc_ref[:, cols(d)].astype(BF16)
                    hand_over(d // 2).start()
            for d in range(N_DEV):
                chip = d // 2

                @pl.when(d % 2 == c)
                def _():
                    hand_over(chip).wait_recv()

                    @pl.when(chip == my_chip)
                    def _():
                        own_ref[...] = acc_ref[:, cols(d)]
                        sib_ref[...] = land_buf[chip]

                    @pl.when(chip != my_chip)
                    def _():
                        pair_buf[chip] = (acc_ref[:, cols(d)] + land_buf[chip].astype(F32)).astype(BF16)
                        to_chip(chip, chip ^ my_chip).start()
            for chip in range(N_CHIPS):
                hand_over(chip).wait_send()

        @pl.when(i >= n)
        def _():
            dx_ref[...], dg1 = _grad_x_tile(dq, dg, x_hat, r, g1, w_ref, dh_ref[...])
            dg1_ref[0:1, :] += dg1

        @pl.when(i == n_steps - 1)
        def _():
            for rel in range(1, n_far + 1):
                to_chip(0, rel).wait()

    both = lambda w_: pl.BlockSpec((tm, w_), lambda i: (i % n, 0))
    second = pl.BlockSpec((tm, D_MODEL), lambda i: (jnp.maximum(i - n, 0), 0))
    whole = lambda dtype: jax.ShapeDtypeStruct(shard, dtype)
    sems = lambda k: pltpu.SemaphoreType.DMA((k,))
    res = pl.pallas_call(
        body, name="in_proj_bwd", grid=(n_steps,),
        in_specs=[both(QKV_W), both(GATES_W), both(D_MODEL), second, _full((1, D_MODEL)), _resident((D_MODEL, IN_COLS)),
                  HBM_SPEC],
        out_specs=[second, _full(shard), _full(shard), HBM_SPEC, _full((SMALL_ROWS, D_MODEL)), HBM_SPEC],
        out_shape=[jax.ShapeDtypeStruct((t, D_MODEL), F32), whole(F32), whole(BF16),
                   jax.ShapeDtypeStruct((n_far,) + shard, BF16), jax.ShapeDtypeStruct((SMALL_ROWS, D_MODEL), F32),
                   jax.ShapeDtypeStruct(out_sums.shape, out_sums.dtype)],
        scratch_shapes=[pltpu.VMEM((D_MODEL, IN_COLS), F32), pltpu.VMEM((N_CHIPS,) + shard, BF16),
                        pltpu.VMEM((N_CHIPS,) + shard, BF16), pltpu.VMEM((N_CHIPS,) + shard, BF16),
                        sems(N_CHIPS), sems(N_CHIPS), sems(n_far), sems(n_far), sems(n_far), sems(n_far)],
        compiler_params=_params("arbitrary"),
    )(dqkv, dgates, x, dh, g1, w_in, out_sums)
    return res[0], (res[1], res[2], res[3]), res[4], res[5]


def _in_proj_bwd_last(dqkv, dgates, x, dh, g1, w_in, tm, grad_x, dg1_rows):
    t = x.shape[0]
    last = t // tm - 1

    def body(dq_ref, dg_ref, x_ref, dh_ref, g1_ref, w_ref, gx_ref, rows_ref, dx_ref, dg1_ref):
        xv = x_ref[...]
        r = _inv_rms(xv)
        dx_ref[...], dg1 = _grad_x_tile(dq_ref[...], dg_ref[...], xv * r, r, g1_ref[...], w_ref, dh_ref[...])
        dg1_ref[...] = rows_ref[...]
        dg1_ref[0:1, :] += dg1

    tile = lambda w_: pl.BlockSpec((tm, w_), lambda i: (last, 0))
    return pl.pallas_call(
        body, name="in_proj_bwd_last", grid=(1,),
        in_specs=[tile(QKV_W), tile(GATES_W), tile(D_MODEL), tile(D_MODEL), _full((1, D_MODEL)),
                  _full((D_MODEL, IN_COLS)), pl.BlockSpec(memory_space=pl.ANY), _full((SMALL_ROWS, D_MODEL))],
        out_specs=[tile(D_MODEL), _full((SMALL_ROWS, D_MODEL))],
        out_shape=[jax.ShapeDtypeStruct((t, D_MODEL), F32), jax.ShapeDtypeStruct((SMALL_ROWS, D_MODEL), F32)],
        input_output_aliases={6: 0},
        compiler_params=_params("arbitrary"),
    )(dqkv, dgates, x, dh, g1, w_in, grad_x, dg1_rows)


def _all_gather(shards, name):
    n = len(shards)

    def body(*refs):
        _gather_now(refs[:n], refs[n:2 * n], *refs[2 * n:])

    return pl.pallas_call(
        body, name=name,
        in_specs=[HBM_SPEC] * n, out_specs=[HBM_SPEC] * n,
        out_shape=[jax.ShapeDtypeStruct((N_DEV,) + s.shape, s.dtype) for s in shards],
        scratch_shapes=[pltpu.SemaphoreType.DMA((7 * n,)), pltpu.SemaphoreType.DMA((7 * n,)),
                        pltpu.SemaphoreType.DMA((n,))],
    )(*shards)


def _adam_math(w, g, m, v):
    m = ADAM_B1 * m + (1.0 - ADAM_B1) * g
    v = ADAM_B2 * v + (1.0 - ADAM_B2) * (g * g)
    m_hat = m / (1.0 - ADAM_B1 ** ADAM_STEP)
    v_hat = v / (1.0 - ADAM_B2 ** ADAM_STEP)
    delta = -ADAM_LR * (m_hat / (jnp.sqrt(v_hat) + ADAM_EPS) + ADAM_WD * w)
    return delta, m, v


def _adamw_reduced(w, m, v, own, from_sibling, from_chips, tr):
    rows, cols = w.shape

    def body(w_ref, m_ref, v_ref, own_ref, sib_ref, far_ref, g_ref, d_ref, nm_ref, nv_ref):
        g = own_ref[...] + sib_ref[...].astype(F32)
        for k in range(len(CHIP_FLIPS)):
            g = g + far_ref[k].astype(F32)
        g_ref[...] = g
        d_ref[...], nm_ref[...], nv_ref[...] = _adam_math(w_ref[...], g, m_ref[...], v_ref[...])

    tile = pl.BlockSpec((tr, cols), lambda i: (i, 0))
    out = jax.ShapeDtypeStruct((rows, cols), F32)
    return pl.pallas_call(
        body, name="adamw_reduced", grid=(rows // tr,),
        in_specs=[tile] * 5 + [pl.BlockSpec((len(CHIP_FLIPS), tr, cols), lambda i: (0, i, 0))],
        out_specs=[tile] * 4, out_shape=[out] * 4,
        compiler_params=_params("parallel"),
    )(w, m, v, own, from_sibling, from_chips)


def _sum_devices(gathered):
    _, rows, cols = gathered.shape

    def body(g_ref, o_ref):
        s = g_ref[0]
        for d in range(1, N_DEV):
            s = s + g_ref[d]
        o_ref[...] = s

    return pl.pallas_call(
        body, name="sum_devices", in_specs=[_full(gathered.shape)], out_specs=_full((rows, cols)), grid=(1,),
        out_shape=jax.ShapeDtypeStruct((rows, cols), F32),
    )(gathered)


def _adamw_small(w, g, m, v):
    def body(w_ref, g_ref, m_ref, v_ref, d_ref, nm_ref, nv_ref):
        d_ref[...], nm_ref[...], nv_ref[...] = _adam_math(w_ref[...], g_ref[...], m_ref[...], v_ref[...])

    spec = _full(w.shape)
    out = jax.ShapeDtypeStruct(w.shape, F32)
    return pl.pallas_call(
        body, name="adamw_small", grid=(1,), in_specs=[spec] * 4, out_specs=[spec] * 3, out_shape=[out] * 3,
    )(w, g, m, v)


TOKEN_TILE = 512
MID_TILE = 256
ADAM_ROWS = 128


def _local_grads(x, target, g1, w_in_shard, conv_shard, sinks, g_attn, g_conv, g2, g3, g4, shards, order):
    t = x.shape[0]
    tm = min(TOKEN_TILE, t)
    rope = _rope_tables(t)
    qkv, gates, mconv, w_in, conv_w, gathered = _in_proj_fwd(x, g1, w_in_shard, conv_shard, g_conv, rope, tm, shards,
                                                             (False, True, False))
    attn, mattn, (w_out, w_up, w_down) = _attn_fwd(qkv, sinks, g_attn, shards, gathered)
    actt, dup, hn2t, dmo, dmix, dh, dmixed, small_mid = _mid(
        mattn, mconv, x, target, g2, g3, g4, w_out.reshape(D_MODEL, D_MODEL),
        w_up, w_down.reshape(D_FF, D_MODEL), min(MID_TILE, t))
    up_own, up_sib, up_sums = _dw_pair_sums((hn2t, dup), order, "up", "dw_up")
    down_own, down_sib, down_sums = _dw_pair_sums((actt, dmo), order, "down", "dw_down")
    out_own, out_sib, out_sums = _dw_pair_sums((mattn, mconv, dmix), order, "out", "dw_out")
    dattn, dgates, small_mix, up_far = _mix_bwd(dmixed, attn, gates, g_attn, g_conv, conv_w, tm, up_sums)
    dqkv, dsink, down_far = _attn_bwd(qkv, dattn, sinks, rope, down_sums)
    grad_x, dw_in, small_in, out_far = _in_proj_bwd(dqkv, dgates, x, dh, g1, w_in, tm, out_sums)
    grad_x, small_in = _in_proj_bwd_last(dqkv, dgates, x, dh, g1, w_in, tm, grad_x, small_in)
    dw_out, dw_up, dw_down = (out_own, out_sib, out_far), (up_own, up_sib, up_far), (down_own, down_sib, down_far)
    return grad_x, dw_in, dw_out, dw_up, dw_down, (small_mid, small_mix, dsink, small_in)


def _pack_small(small_mid, small_mix, dsink, small_in):
    z = lambda n: jnp.zeros((1, n), F32)
    rows = [
        small_mid[ROW_LOSS:ROW_LOSS + 1],
        small_in[0:1],
        small_mid[ROW_G2:ROW_G2 + 1],
        small_mid[ROW_G3:ROW_G3 + 1],
        small_mid[ROW_G4:ROW_G4 + 1],
        jnp.concatenate([small_mix[ROW_GATTN:ROW_GATTN + 1], small_mix[ROW_GCONV:ROW_GCONV + 1]], axis=1),
        jnp.concatenate([small_mix[ROW_CW0:ROW_CW0 + 1], small_mix[ROW_CW0 + 1:ROW_CW0 + 2]], axis=1),
        jnp.concatenate([small_mix[ROW_CW0 + 2:ROW_CW0 + 3], dsink[0:1, :], z(D_MODEL - CONV_W - 128)], axis=1),
    ]
    return jnp.concatenate(rows, axis=0)


def kernel(x, pre_mix_norm, w_in, conv_w, attn_sinks, attn_group_norm, conv_group_norm, w_out, post_mix_norm, pre_mlp_norm, w_up, w_down, post_mlp_norm, loss_target, m_pre_mix_norm, m_w_in, m_conv_w, m_attn_sinks, m_attn_group_norm, m_conv_group_norm, m_w_out, m_post_mix_norm, m_pre_mlp_norm, m_w_up, m_w_down, m_post_mlp_norm, v_pre_mix_norm, v_w_in, v_conv_w, v_attn_sinks, v_attn_group_norm, v_conv_group_norm, v_w_out, v_post_mix_norm, v_pre_mlp_norm, v_w_up, v_w_down, v_post_mlp_norm):
    xi, yi, ci = _mesh_pos()
    chip = 2 * xi + yi
    dev = 2 * chip + ci

    order = _block_order(dev)

    shards = [w_out[0].astype(BF16), w_up[0].astype(BF16), w_down[0].astype(BF16)]

    grad_x, dw_in, dw_out, dw_up, dw_down, smalls = _local_grads(
        x[0], loss_target[0], pre_mix_norm, w_in[0].astype(BF16), conv_w[0], attn_sinks, attn_group_norm, conv_group_norm,
        post_mix_norm, pre_mlp_norm, post_mlp_norm, shards, order)

    small = _sum_devices(_all_gather([_pack_small(*smalls)], "gather_small")[0])
    loss = (0.5 / D_MODEL) * jnp.sum(small[0])

    big = {}
    for name, w, m, v, (own, sib, far) in zip(
            ("w_in", "w_out", "w_up", "w_down"), (w_in, w_out, w_up, w_down), (m_w_in, m_w_out, m_w_up, m_w_down),
            (v_w_in, v_w_out, v_w_up, v_w_down), (dw_in, dw_out, dw_up, dw_down)):
        big[name] = [a[None] for a in _adamw_reduced(w[0], m[0], v[0], own, sib, far, ADAM_ROWS)]

    conv_g = lax.dynamic_slice(
        jnp.stack([small[6, :CONV_W], small[6, CONV_W:], small[7, :CONV_W]]), (0, dev * (CONV_W // N_DEV)),
        (3, CONV_W // N_DEV))
    pad = lambda a, n: jnp.pad(a.reshape(1, -1), ((0, 0), (0, n - a.size)))
    small_names = ("pre_mix_norm", "post_mix_norm", "pre_mlp_norm", "post_mlp_norm")
    small_w = {"pre_mix_norm": (pre_mix_norm, m_pre_mix_norm, v_pre_mix_norm),
               "post_mix_norm": (post_mix_norm, m_post_mix_norm, v_post_mix_norm),
               "pre_mlp_norm": (pre_mlp_norm, m_pre_mlp_norm, v_pre_mlp_norm),
               "post_mlp_norm": (post_mlp_norm, m_post_mlp_norm, v_post_mlp_norm)}

    def pack(k):
        rows = [small_w[nm][k] for nm in small_names]
        rows.append(jnp.concatenate([(attn_group_norm, m_attn_group_norm, v_attn_group_norm)[k],
                                     (conv_group_norm, m_conv_group_norm, v_conv_group_norm)[k]], axis=1))
        rows.append(pad((conv_w, m_conv_w, v_conv_w)[k], D_MODEL))
        rows.append(pad((attn_sinks, m_attn_sinks, v_attn_sinks)[k], D_MODEL))
        rows.append(jnp.zeros((1, D_MODEL), F32))
        return jnp.concatenate(rows, axis=0)

    g_small = jnp.concatenate(
        [small[1:6], pad(conv_g, D_MODEL), pad(small[7, CONV_W:CONV_W + N_HEADS], D_MODEL), jnp.zeros((1, D_MODEL), F32)],
        axis=0)
    d_small, nm_small, nv_small = _adamw_small(pack(0), g_small, pack(1), pack(2))

    def unpack(a):
        nconv = 3 * CONV_W // N_DEV
        return {"pre_mix_norm": a[0:1], "post_mix_norm": a[1:2], "pre_mlp_norm": a[2:3], "post_mlp_norm": a[3:4],
                "attn_group_norm": a[4:5, :ATTN_W], "conv_group_norm": a[4:5, ATTN_W:],
                "conv_w": a[5, :nconv].reshape(1, 3, CONV_W // N_DEV), "attn_sinks": a[6:7, :N_HEADS]}

    order = ("pre_mix_norm", "w_in", "conv_w", "attn_sinks", "attn_group_norm", "conv_group_norm", "w_out",
             "post_mix_norm", "pre_mlp_norm", "w_up", "w_down", "post_mlp_norm")
    outs = []
    for k, a in enumerate((g_small, d_small, nm_small, nv_small)):
        sm = unpack(a)
        outs += [big[nm][k] if nm in big else sm[nm] for nm in order]
    return (loss, grad_x[None], *outs)
```

```python
import functools

import jax
import jax.numpy as jnp
import numpy as np
from jax import lax
from jax.experimental import pallas as pl
from jax.experimental.pallas import tpu as pltpu

F32 = jnp.float32
BF16 = jnp.bfloat16

D_MODEL = 1024
HEAD_DIM = 64
ATTN_W = 512
CONV_W = 512
N_HEADS = 8
N_KV = 2
GROUP = 4
KV_W = 128
QKV_W = ATTN_W + 2 * KV_W
GATES_W = 3 * CONV_W
IN_COLS = QKV_W + GATES_W
D_FF = 4096
FF_CHUNK = 512
N_FF_CHUNKS = D_FF // FF_CHUNK
BLOCK = 128
ROT_HALF = 8
ROPE_THETA = 500000.0
NORM_EPS = 1e-6
NEG_INF = -1e30
ATTN_SCALE = 0.125
N_DEV = 8
N_CHIPS = 4
IN_SHARD = IN_COLS // N_DEV

ADAM_LR = 0.001
ADAM_B1 = 0.9
ADAM_B2 = 0.999
ADAM_EPS = 1e-08
ADAM_WD = 0.01
ADAM_STEP = 10

V7X_VMEM_BYTES = 64 * 1024 * 1024
VMEM_LIMIT = V7X_VMEM_BYTES - 2 * 1024 * 1024

MESH = pl.DeviceIdType.MESH
HBM_SPEC = pl.BlockSpec(memory_space=pltpu.HBM)


def _params(*sem):
    return pltpu.CompilerParams(dimension_semantics=sem, vmem_limit_bytes=VMEM_LIMIT)


def _mm(a, b):
    return jnp.dot(a, b, preferred_element_type=F32)


def _mm_nt(a, b):
    return lax.dot_general(a, b, (((1,), (1,)), ((), ())), preferred_element_type=F32)


def _mm_tn(a, b):
    return lax.dot_general(a, b, (((0,), (0,)), ((), ())), preferred_element_type=F32)


def _inv_rms(x):
    return lax.rsqrt(jnp.mean(x * x, axis=-1, keepdims=True) + NORM_EPS)


def _rms_bwd(xhat, r, gain, dy):
    gy = dy * gain
    return r * (gy - xhat * jnp.mean(gy * xhat, axis=-1, keepdims=True)), dy * xhat


def _colsum(a):
    return jnp.sum(a, axis=0, keepdims=True)


def _full(shape):
    zeros = (0,) * len(shape)
    return pl.BlockSpec(shape, lambda *_: zeros)


def _resident(shape):
    zeros = (0,) * len(shape)
    return pl.BlockSpec(shape, lambda *_: zeros, pipeline_mode=pl.Buffered(1))


def _rope_tables(t):
    pos = np.arange(t, dtype=np.float32)
    inv_freq = (ROPE_THETA ** (-np.arange(0, 2 * ROT_HALF, 2, dtype=np.float64) / (2 * ROT_HALF))).astype(np.float32)
    ang = (pos[:, None] * inv_freq[None, :]).astype(np.float64)
    cos, sin = np.cos(ang).astype(np.float32), np.sin(ang).astype(np.float32)
    zeros8 = np.zeros((t, ROT_HALF), np.float32)
    rest = np.zeros((t, HEAD_DIM - 2 * ROT_HALF), np.float32)
    c_head = np.concatenate([cos, cos, rest + 1.0], axis=1)
    s1_head = np.concatenate([zeros8, sin, rest], axis=1)
    s2_head = np.concatenate([-sin, zeros8, rest], axis=1)
    two = lambda a: jnp.asarray(np.concatenate([a, a], axis=1))
    return two(c_head), two(s1_head), two(s2_head)


def _rope(v, c, s1, s2):
    return v * c + pltpu.roll(v, ROT_HALF, 1) * s1 + pltpu.roll(v, 128 - ROT_HALF, 1) * s2


def _rope_transpose(dv, c, s1, s2):
    return dv * c + pltpu.roll(dv * s1, 128 - ROT_HALF, 1) + pltpu.roll(dv * s2, ROT_HALF, 1)


def _shift_rows_down(u, prev, k):
    row = lax.broadcasted_iota(jnp.int32, u.shape, 0)
    out = pltpu.roll(u, k, 0)
    for r in range(k):
        out = jnp.where(row == r, prev[8 - k + r:8 - k + r + 1, :], out)
    return out


def _shift_rows_up(u, nxt, k):
    n = u.shape[0]
    row = lax.broadcasted_iota(jnp.int32, u.shape, 0)
    out = pltpu.roll(u, n - k, 0)
    for r in range(k):
        out = jnp.where(row == n - k + r, nxt[r:r + 1, :], out)
    return out


def _conv3(u, u1, u2, w):
    return (w[0:1, :] * u2 + w[1:2, :] * u1) + w[2:3, :] * u


def _mesh_pos():
    return lax.axis_index("x"), lax.axis_index("y"), lax.axis_index("c")


def _slot(ref, pos):
    dev = 4 * pos[0] + 2 * pos[1] + pos[2]
    if len(ref.shape) == 2:
        width = ref.shape[1] // N_DEV
        return ref.at[:, pl.ds(pl.multiple_of(dev * width, width), width)]
    return ref.at[dev]


def _gathered_shape(shard, by_cols):
    if by_cols:
        return jax.ShapeDtypeStruct((shard.shape[0], N_DEV * shard.shape[1]), shard.dtype)
    return jax.ShapeDtypeStruct((N_DEV,) + shard.shape, shard.dtype)


def _push(src, dst, sems, k, to):
    send_sems, recv_sems = sems
    return pltpu.make_async_remote_copy(src_ref=src, dst_ref=dst, send_sem=send_sems.at[k], recv_sem=recv_sems.at[k],
                                        device_id=to, device_id_type=MESH)


def _gather_now(shards, outs, send_sems, recv_sems, local_sems, once_started=None):
    n = len(shards)
    x, y, c = _mesh_pos()
    me, sibling = (x, y, c), (x, y, 1 - c)
    chips = [(1 - x, y), (x, 1 - y), (1 - x, 1 - y)]

    def copy(i, k, block, to, src=None):
        dst = _slot(outs[i], block)
        return _push(dst if src is None else src, dst, (send_sems, recv_sems), 7 * i + k, to)

    mine = [pltpu.make_async_copy(shards[i], _slot(outs[i], me), local_sems.at[i]) for i in range(n)]
    for cp in mine:
        cp.start()
    first = []
    for i in range(n):
        first.append(copy(i, 0, me, sibling, src=shards[i]))
        first += [copy(i, 1 + j, me, (*chip, c), src=shards[i]) for j, chip in enumerate(chips)]
    for cp in first:
        cp.start()
    if once_started is not None:
        once_started()
    passed = []
    for j, chip in enumerate(chips):
        for i in range(n):
            copy(i, 1 + j, (*chip, c), me).wait_recv()
            cp = copy(i, 4 + j, (*chip, c), sibling)
            cp.start()
            passed.append(cp)
    for i in range(n):
        copy(i, 0, sibling, me).wait_recv()
        for j, chip in enumerate(chips):
            copy(i, 4 + j, (*chip, 1 - c), me).wait_recv()
    for cp in first + passed:
        cp.wait_send()
    for cp in mine:
        cp.wait()


def _gather_near(first, last, shards, outs, sems, local_sems):
    x, y, c = _mesh_pos()
    me, peers = (x, y, c), [(x, y, 1 - c), (1 - x, y, c), (x, 1 - y, c)]
    n = len(shards)
    local = [pltpu.make_async_copy(shards[i], _slot(outs[i], me), local_sems.at[i]) for i in range(n)]
    sends = [_push(shards[i], _slot(outs[i], me), sems, 3 * i + k, peers[k]) for i in range(n) for k in range(3)]
    arrivals = [_push(shards[i], _slot(outs[i], peers[k]), sems, 3 * i + k, peers[k]) for i in range(n) for k in range(3)]

    def start():
        for cp in local + sends:
            cp.start()

    if first is not None:
        pl.when(first)(start)

    @pl.when(last)
    def _():
        for cp in sends:
            cp.wait_send()
        for cp in arrivals:
            cp.wait_recv()
        for cp in local:
            cp.wait()

    return start


def _gather_far(first, last, shards, ins, outs, sems):
    x, y, c = _mesh_pos()
    me, sibling = (x, y, c), (x, y, 1 - c)
    chips = [(1 - x, y), (x, 1 - y), (1 - x, 1 - y)]
    n = len(shards)
    diag_send = [_push(shards[i], _slot(outs[i], me), sems, 4 * i, (*chips[2], c)) for i in range(n)]
    diag_arrival = [_push(shards[i], _slot(outs[i], (*chips[2], c)), sems, 4 * i, (*chips[2], c)) for i in range(n)]
    passed = [[_push(_slot(ins[i], (*chips[j], c)), _slot(outs[i], (*chips[j], c)), sems, 4 * i + 1 + j, sibling)
               for i in range(n)] for j in range(3)]
    from_sibling = [_push(shards[i], _slot(outs[i], (*chips[j], 1 - c)), sems, 4 * i + 1 + j, sibling)
                    for i in range(n) for j in range(3)]

    @pl.when(first)
    def _():
        for cp in diag_send + passed[0] + passed[1]:
            cp.start()

    @pl.when(last)
    def _():
        for cp in diag_arrival:
            cp.wait_recv()
        for cp in passed[2]:
            cp.start()
        for cp in from_sibling:
            cp.wait_recv()
        for cp in diag_send + passed[0] + passed[1] + passed[2]:
            cp.wait_send()


def _in_proj_fwd(x, g1, w_in, conv_w, g_conv, rope, tm, shards, by_cols):
    t = x.shape[0]
    rc, rs1, rs2 = rope
    n = len(shards)

    def body(*refs):
        x_ref, g1_ref, w_ref, cw_ref, gc_ref, c_ref, s1_ref, s2_ref = refs[:8]
        shard_refs = refs[8:8 + n]
        qkv_ref, gates_ref, mconv_ref, w_full_ref, cw_full_ref = refs[8 + n:13 + n]
        gathered = refs[13 + n:13 + 2 * n]
        carry_ref, w_land, cw_land = refs[13 + 2 * n:16 + 2 * n]
        now_sems = refs[16 + 2 * n:19 + 2 * n]
        step = pl.program_id(0)
        start_later_weights = _gather_near(None, step == pl.num_programs(0) - 1, shard_refs, gathered,
                                           refs[19 + 2 * n:21 + 2 * n], refs[21 + 2 * n]) if n else None

        @pl.when(step == 0)
        def _():
            carry_ref[...] = jnp.zeros_like(carry_ref)
            _gather_now([w_ref, cw_ref], [w_land, cw_land], *now_sems, once_started=start_later_weights)
            conv_shard = CONV_W // N_DEV
            for d in range(N_DEV):
                w_full_ref[:, IN_SHARD * d:IN_SHARD * (d + 1)] = w_land[d]
                cw_full_ref[:, conv_shard * d:conv_shard * (d + 1)] = cw_land[d]

        xv = x_ref[...]
        hn = ((xv * _inv_rms(xv)) * g1_ref[...]).astype(BF16)
        proj = _mm(hn, w_full_ref[...])
        c, s1, s2 = c_ref[...], s1_ref[...], s2_ref[...]
        for ci in range((ATTN_W + KV_W) // 128):
            sl = slice(128 * ci, 128 * (ci + 1))
            qkv_ref[:, sl] = _rope(proj[:, sl], c, s1, s2).astype(BF16)
        qkv_ref[:, ATTN_W + KV_W:QKV_W] = proj[:, ATTN_W + KV_W:QKV_W].astype(BF16)
        gates = proj[:, QKV_W:]
        gates_ref[...] = gates
        gb, gcc, xin = gates[:, :CONV_W], gates[:, CONV_W:2 * CONV_W], gates[:, 2 * CONV_W:]
        u = gcc * xin
        prev = carry_ref[...]
        conv = gb * _conv3(u, _shift_rows_down(u, prev, 1), _shift_rows_down(u, prev, 2), cw_full_ref[...])
        carry_ref[...] = u[tm - 8:tm, :]
        mconv_ref[...] = ((conv * _inv_rms(conv)) * gc_ref[...]).astype(BF16)

    tile = lambda w_: pl.BlockSpec((tm, w_), lambda i: (i, 0))
    sems = lambda k: pltpu.SemaphoreType.DMA((k,))
    res = pl.pallas_call(
        body, name="in_proj_fwd", grid=(t // tm,),
        in_specs=[tile(D_MODEL), _full((1, D_MODEL)), HBM_SPEC, HBM_SPEC, _full((1, CONV_W)), tile(128), tile(128),
                  tile(128)] + [HBM_SPEC] * n,
        out_specs=[tile(QKV_W), tile(GATES_W), tile(CONV_W), _full((D_MODEL, IN_COLS)), _full((3, CONV_W))]
        + [HBM_SPEC] * n,
        out_shape=[jax.ShapeDtypeStruct((t, QKV_W), BF16), jax.ShapeDtypeStruct((t, GATES_W), F32),
                   jax.ShapeDtypeStruct((t, CONV_W), BF16), jax.ShapeDtypeStruct((D_MODEL, IN_COLS), BF16),
                   jax.ShapeDtypeStruct((3, CONV_W), F32)]
        + [_gathered_shape(s, cols) for s, cols in zip(shards, by_cols)],
        scratch_shapes=[pltpu.VMEM((8, CONV_W), F32), pltpu.VMEM((N_DEV,) + w_in.shape, BF16),
                        pltpu.VMEM((N_DEV,) + conv_w.shape, F32), sems(14), sems(14), sems(2)]
        + ([sems(3 * n), sems(3 * n), sems(n)] if n else []),
        compiler_params=_params("arbitrary"),
    )(x, g1, w_in, conv_w, g_conv, rc, rs1, rs2, *shards)
    return res[0], res[1], res[2], res[3], res[4], list(res[5:])


GROUP_ROWS = GROUP * BLOCK


def _attn_mask(has_prev):
    row = lax.broadcasted_iota(jnp.int32, (GROUP_ROWS, 2 * BLOCK), 0) & (BLOCK - 1)
    col = lax.broadcasted_iota(jnp.int32, (GROUP_ROWS, 2 * BLOCK), 1)
    band = (col > row) & (col <= row + BLOCK)
    return band if has_prev is True else band & ((col >= BLOCK) | has_prev)


def _stack_heads(a, g):
    return jnp.concatenate([a[:, HEAD_DIM * (GROUP * g + hh):HEAD_DIM * (GROUP * g + hh + 1)] for hh in range(GROUP)], axis=0)


def _unstack_heads(a):
    return jnp.concatenate([a[BLOCK * hh:BLOCK * (hh + 1), :] for hh in range(GROUP)], axis=1)


def _group_sinks(sink_ref, g):
    head = lax.broadcasted_iota(jnp.int32, (GROUP_ROWS, 1), 0) // BLOCK
    out = jnp.full((GROUP_ROWS, 1), sink_ref[0, GROUP * g], F32)
    for hh in range(1, GROUP):
        out = jnp.where(head == hh, sink_ref[0, GROUP * g + hh], out)
    return out


def _attn_probs(qs, kk, sink, valid):
    s = jnp.where(valid, _mm_nt(qs, kk) * ATTN_SCALE, NEG_INF)
    m = jnp.maximum(jnp.max(s, axis=-1, keepdims=True), sink)
    p = jnp.exp(s - m)
    psink = jnp.exp(sink - m)
    inv_l = 1.0 / (jnp.sum(p, axis=-1, keepdims=True) + psink)
    return p * inv_l, psink * inv_l


ATTN_STEP_BLOCKS = 4
ATTN_STEP = ATTN_STEP_BLOCKS * BLOCK
ATTN_KEYS = ATTN_STEP + BLOCK


def _qkv_specs(order):
    prev = lambda i: jnp.maximum(ATTN_STEP_BLOCKS * order(i) - 1, 0)
    kcol, vcol = ATTN_W // KV_W, ATTN_W // KV_W + 1
    return [pl.BlockSpec((ATTN_STEP, ATTN_W), lambda i: (order(i), 0)),
            pl.BlockSpec((BLOCK, KV_W), lambda i: (prev(i), kcol)), pl.BlockSpec((ATTN_STEP, KV_W), lambda i: (order(i), kcol)),
            pl.BlockSpec((BLOCK, KV_W), lambda i: (prev(i), vcol)), pl.BlockSpec((ATTN_STEP, KV_W), lambda i: (order(i), vcol))]


def _attn_fwd(qkv, sinks, g_attn, shards, gathered):
    t = qkv.shape[0]
    n = len(shards)

    def body(*refs):
        sink_ref, q_ref, kp_ref, kc_ref, vp_ref, vc_ref, ga_ref = refs[:7]
        attn_ref, mattn_ref = refs[7 + 2 * n:9 + 2 * n]
        step = pl.program_id(0)
        if n:
            _gather_far(step == 0, step == pl.num_programs(0) - 1, refs[7:7 + n], refs[7 + n:7 + 2 * n],
                        refs[9 + 2 * n:9 + 3 * n], refs[9 + 3 * n:11 + 3 * n])
        q = q_ref[...]
        keys = jnp.concatenate([kp_ref[...], kc_ref[...]], axis=0)
        vals = jnp.concatenate([vp_ref[...], vc_ref[...]], axis=0)
        sink = [_group_sinks(sink_ref, g) for g in range(N_KV)]
        gain = ga_ref[...]
        for b in range(ATTN_STEP_BLOCKS):
            rows, window = slice(BLOCK * b, BLOCK * (b + 1)), slice(BLOCK * b, BLOCK * (b + 2))
            valid = _attn_mask(True if b else step > 0)
            outs = []
            for g in range(N_KV):
                gs = slice(HEAD_DIM * g, HEAD_DIM * (g + 1))
                probs, _ = _attn_probs(_stack_heads(q[rows], g), keys[window, gs], sink[g], valid)
                outs.append(_unstack_heads(_mm(probs.astype(BF16), vals[window, gs])))
            attn = jnp.concatenate(outs, axis=1)
            attn_ref[rows, :] = attn
            mattn_ref[rows, :] = ((attn * _inv_rms(attn)) * gain).astype(BF16)

    blk = pl.BlockSpec((ATTN_STEP, ATTN_W), lambda j: (j, 0))
    res = pl.pallas_call(
        body, name="attn_fwd", grid=(t // ATTN_STEP,),
        in_specs=[pl.BlockSpec(memory_space=pltpu.SMEM)] + _qkv_specs(lambda j: j) + [_full((1, ATTN_W))]
        + [HBM_SPEC] * (2 * n),
        out_specs=[blk, blk] + [HBM_SPEC] * n,
        out_shape=[jax.ShapeDtypeStruct((t, ATTN_W), F32), jax.ShapeDtypeStruct((t, ATTN_W), BF16)]
        + [jax.ShapeDtypeStruct(g.shape, g.dtype) for g in gathered],
        input_output_aliases={7 + n + i: 2 + i for i in range(n)},
        scratch_shapes=[pltpu.SemaphoreType.DMA((4 * n,)), pltpu.SemaphoreType.DMA((4 * n,))] if n else [],
        compiler_params=_params("arbitrary"),
    )(sinks, qkv, qkv, qkv, qkv, qkv, g_attn, *shards, *gathered)
    return res[0], res[1], list(res[2:])


SMALL_ROWS = 8
ROW_LOSS, ROW_G2, ROW_G3, ROW_G4 = 0, 1, 2, 3


def _mid(mattn, mconv, x, target, g2, g3, g4, w_out, w_up, w_down, tm):
    t = x.shape[0]

    def body(ma_ref, mc_ref, x_ref, t_ref, g2_ref, g3_ref, g4_ref, wo_ref, wu_ref, wd_ref,
             actt_ref, dup_ref, hn2t_ref, dmo_ref, dmix_ref, dh_ref, dmixed_ref, small_ref, up_ref):
        @pl.when(pl.program_id(0) == 0)
        def _():
            small_ref[...] = jnp.zeros_like(small_ref)

        g2, g3, g4 = g2_ref[...], g3_ref[...], g4_ref[...]
        mix_out = _mm(ma_ref[...], wo_ref[0:ATTN_W, :]) + _mm(mc_ref[...], wo_ref[ATTN_W:, :])
        r2 = _inv_rms(mix_out)
        mo_hat = mix_out * r2
        h = x_ref[...] + mo_hat * g2
        r3 = _inv_rms(h)
        h_hat = h * r3
        hn2 = (h_hat * g3).astype(BF16)
        hn2t_ref[...] = hn2.T
        up = jnp.maximum(_mm(hn2, wu_ref[...]), 0.0)
        up_ref[...] = up.astype(BF16)
        act = (up * up).astype(BF16)
        actt_ref[...] = act.T
        mlp = _mm(act, wd_ref[...])
        r4 = _inv_rms(mlp)
        ml_hat = mlp * r4
        err = (h + ml_hat * g4) - t_ref[...]
        d_out = err * (1.0 / D_MODEL)
        d_mlp, dg4 = _rms_bwd(ml_hat, r4, g4, d_out)
        dmo = d_mlp.astype(BF16)
        dmo_ref[...] = dmo
        dup = (_mm_nt(dmo, wd_ref[...]) * (2.0 * up_ref[...].astype(F32))).astype(BF16)
        dup_ref[...] = dup
        dhn2 = _mm_nt(dup, wu_ref[...])
        dh_norm, dg3 = _rms_bwd(h_hat, r3, g3, dhn2)
        dh = d_out + dh_norm
        dh_ref[...] = dh
        d_mix, dg2 = _rms_bwd(mo_hat, r2, g2, dh)
        dmix = d_mix.astype(BF16)
        dmix_ref[...] = dmix
        dmixed_ref[...] = _mm_nt(dmix, wo_ref[...])
        small_ref[ROW_LOSS:ROW_LOSS + 1, :] += _colsum(err * err)
        small_ref[ROW_G2:ROW_G2 + 1, :] += _colsum(dg2)
        small_ref[ROW_G3:ROW_G3 + 1, :] += _colsum(dg3)
        small_ref[ROW_G4:ROW_G4 + 1, :] += _colsum(dg4)

    tile = lambda n: pl.BlockSpec((tm, n), lambda i: (i, 0))
    cols = lambda n: pl.BlockSpec((n, tm), lambda i: (0, i))
    gain = _full((1, D_MODEL))
    return pl.pallas_call(
        body, name="mid_fwd_bwd", grid=(t // tm,),
        in_specs=[tile(ATTN_W), tile(CONV_W), tile(D_MODEL), tile(D_MODEL), gain, gain, gain,
                  _resident((D_MODEL, D_MODEL)), _resident((D_MODEL, D_FF)), _resident((D_FF, D_MODEL))],
        out_specs=[cols(D_FF), tile(D_FF), cols(D_MODEL), tile(D_MODEL), tile(D_MODEL), tile(D_MODEL), tile(D_MODEL),
                   _full((SMALL_ROWS, D_MODEL))],
        out_shape=[jax.ShapeDtypeStruct((D_FF, t), BF16), jax.ShapeDtypeStruct((t, D_FF), BF16),
                   jax.ShapeDtypeStruct((D_MODEL, t), BF16), jax.ShapeDtypeStruct((t, D_MODEL), BF16),
                   jax.ShapeDtypeStruct((t, D_MODEL), BF16), jax.ShapeDtypeStruct((t, D_MODEL), F32),
                   jax.ShapeDtypeStruct((t, D_MODEL), F32), jax.ShapeDtypeStruct((SMALL_ROWS, D_MODEL), F32)],
        scratch_shapes=[pltpu.VMEM((tm, D_FF), BF16)],
        compiler_params=_params("arbitrary"),
    )(mattn, mconv, x, target, g2, g3, g4, w_out, w_up, w_down)


CHIP_FLIPS = ((1, 1), (1, 0), (0, 1))


def _block_order(dev):
    chip_masks = [4 * fx + 2 * fy for fx, fy in CHIP_FLIPS]
    masks = [m + 1 for m in chip_masks] + [1] + chip_masks + [0]
    return jnp.bitwise_xor(dev, jnp.asarray(masks, jnp.int32)).astype(jnp.int32)


def _other_chips(x, y, c):
    return [(1 - x if fx else x, 1 - y if fy else y, c) for fx, fy in CHIP_FLIPS]


def _dw_pair_sums(operands, order, which, name):
    t = operands[-1].shape[0]
    n_far = len(CHIP_FLIPS)
    n_in = len(operands)
    out_chunk = D_MODEL // N_DEV
    if which == "up":
        rows, cols = D_MODEL, FF_CHUNK
        in_specs = [_resident((D_MODEL, t)), pl.BlockSpec((t, FF_CHUNK), lambda s, order_ref: (0, order_ref[s]))]
    elif which == "down":
        rows, cols = FF_CHUNK, D_MODEL
        in_specs = [pl.BlockSpec((FF_CHUNK, t), lambda s, order_ref: (order_ref[s], 0)), _resident((t, D_MODEL))]
    else:
        rows, cols = out_chunk, D_MODEL
        half = pl.BlockSpec((t, out_chunk), lambda s, order_ref: (0, order_ref[s] % (N_DEV // 2)))
        in_specs = [half, half, _resident((t, D_MODEL))]

    def body(order_ref, *refs):
        own_ref, from_sib_ref, pair_ref, send_buf, land_buf, send_sems, recv_sems = refs[n_in:]
        s_now = pl.program_id(0)
        x, y, c = _mesh_pos()
        sibling = (x, y, 1 - c)
        sems = (send_sems, recv_sems)

        def hand_over(k):
            dst = land_buf.at[k] if k < n_far else from_sib_ref
            return _push(send_buf.at[k], dst, sems, k, sibling)

        if which == "out":
            ma_ref, mc_ref, b_ref = refs[:n_in]
            block = lax.cond(order_ref[s_now] < N_DEV // 2, lambda: _mm_tn(ma_ref[...], b_ref[...]),
                             lambda: _mm_tn(mc_ref[...], b_ref[...]))
        else:
            block = _mm(refs[0][...], refs[1][...])
        for k in range(n_far + 1):
            @pl.when(s_now == k)
            def _():
                send_buf[k] = block.astype(BF16)
                hand_over(k).start()

        for k in range(n_far):
            @pl.when(s_now == n_far + 1 + k)
            def _():
                hand_over(k).wait_recv()
                pair_ref[...] = (block + land_buf[k].astype(F32)).astype(BF16)

        @pl.when(s_now == N_DEV - 1)
        def _():
            own_ref[...] = block
            for k in range(n_far + 1):
                hand_over(k).wait_send()
            hand_over(n_far).wait_recv()

    return pl.pallas_call(
        body, name=name,
        grid_spec=pltpu.PrefetchScalarGridSpec(
            num_scalar_prefetch=1, grid=(N_DEV,), in_specs=in_specs,
            out_specs=[pl.BlockSpec((rows, cols), lambda s, order_ref: (0, 0)), HBM_SPEC,
                       pl.BlockSpec((None, rows, cols), lambda s, order_ref: (jnp.clip(s - n_far - 1, 0, n_far - 1), 0, 0))],
            scratch_shapes=[pltpu.VMEM((n_far + 1, rows, cols), BF16), pltpu.VMEM((n_far, rows, cols), BF16),
                            pltpu.SemaphoreType.DMA((n_far + 1,)), pltpu.SemaphoreType.DMA((n_far + 1,))]),
        out_shape=[jax.ShapeDtypeStruct((rows, cols), F32), jax.ShapeDtypeStruct((rows, cols), BF16),
                   jax.ShapeDtypeStruct((n_far, rows, cols), BF16)],
        compiler_params=_params("arbitrary"),
    )(order, *operands)


def _chip_exchange_beside(first, last, sums, outs, sems):
    chips = _other_chips(*_mesh_pos())
    copies = [_push(sums[i].at[k], outs[i].at[k], sems, len(chips) * i + k, chip)
              for i in range(len(sums)) for k, chip in enumerate(chips)]

    @pl.when(first)
    def _():
        for cp in copies:
            cp.start()

    @pl.when(last)
    def _():
        for cp in copies:
            cp.wait()


ROW_GATTN, ROW_GCONV, ROW_CW0 = 0, 1, 2


def _mix_bwd(dmixed, attn, gates, g_attn, g_conv, conv_w, tm, sums):
    t = attn.shape[0]
    n = t // tm
    rev = lambda i: n - 1 - i

    def body(dm_ref, attn_ref, gates_ref, gprev_ref, ga_ref, gc_ref, cw_ref, sums_ref, dattn_ref, dgates_ref, small_ref,
             arrived_ref, carry_ref, send_sems, recv_sems):
        i = pl.program_id(0)
        _chip_exchange_beside(i == 0, i == n - 1, [sums_ref], [arrived_ref], (send_sems, recv_sems))

        @pl.when(i == 0)
        def _():
            small_ref[...] = jnp.zeros_like(small_ref)
            carry_ref[...] = jnp.zeros_like(carry_ref)

        dm = dm_ref[...]
        a = attn_ref[...]
        ra = _inv_rms(a)
        a_hat = a * ra
        dattn, dga = _rms_bwd(a_hat, ra, ga_ref[...], dm[:, :ATTN_W])
        dattn_ref[...] = dattn

        gates = gates_ref[...]
        gb, gcc, xin = gates[:, :CONV_W], gates[:, CONV_W:2 * CONV_W], gates[:, 2 * CONV_W:]
        u = gcc * xin
        gp = gprev_ref[...]
        uprev = jnp.where(rev(i) == 0, 0.0, gp[:, CONV_W:2 * CONV_W] * gp[:, 2 * CONV_W:])
        u1, u2 = _shift_rows_down(u, uprev, 1), _shift_rows_down(u, uprev, 2)
        w = cw_ref[...]
        c = _conv3(u, u1, u2, w)
        conv = gb * c
        rcv = _inv_rms(conv)
        c_hat = conv * rcv
        dconv, dgc = _rms_bwd(c_hat, rcv, gc_ref[...], dm[:, ATTN_W:])
        dc = dconv * gb
        nxt = carry_ref[...]
        du = (w[2:3, :] * dc + w[1:2, :] * _shift_rows_up(dc, nxt, 1)) + w[0:1, :] * _shift_rows_up(dc, nxt, 2)
        carry_ref[...] = dc[0:8, :]
        dgates_ref[:, :CONV_W] = (dconv * c).astype(BF16)
        dgates_ref[:, CONV_W:2 * CONV_W] = (du * xin).astype(BF16)
        dgates_ref[:, 2 * CONV_W:] = (du * gcc).astype(BF16)
        small_ref[ROW_GATTN:ROW_GATTN + 1, :] += _colsum(dga)
        small_ref[ROW_GCONV:ROW_GCONV + 1, :] += _colsum(dgc)
        small_ref[ROW_CW0:ROW_CW0 + 1, :] += _colsum(dc * u2)
        small_ref[ROW_CW0 + 1:ROW_CW0 + 2, :] += _colsum(dc * u1)
        small_ref[ROW_CW0 + 2:ROW_CW0 + 3, :] += _colsum(dc * u)

    tile = lambda w_: pl.BlockSpec((tm, w_), lambda i: (rev(i), 0))
    prev8 = pl.BlockSpec((8, GATES_W), lambda i: (jnp.maximum(rev(i) * (tm // 8) - 1, 0), 0))
    return pl.pallas_call(
        body, name="mix_bwd", grid=(n,),
        in_specs=[tile(D_MODEL), tile(ATTN_W), tile(GATES_W), prev8, _full((1, ATTN_W)), _full((1, CONV_W)),
                  _full((3, CONV_W)), HBM_SPEC],
        out_specs=[tile(ATTN_W), tile(GATES_W), _full((SMALL_ROWS, CONV_W)), HBM_SPEC],
        out_shape=[jax.ShapeDtypeStruct((t, ATTN_W), F32), jax.ShapeDtypeStruct((t, GATES_W), BF16),
                   jax.ShapeDtypeStruct((SMALL_ROWS, CONV_W), F32), jax.ShapeDtypeStruct(sums.shape, sums.dtype)],
        scratch_shapes=[pltpu.VMEM((8, CONV_W), F32), pltpu.SemaphoreType.DMA((len(CHIP_FLIPS),)),
                        pltpu.SemaphoreType.DMA((len(CHIP_FLIPS),))],
        compiler_params=_params("arbitrary"),
    )(dmixed, attn, gates, gates, g_attn, g_conv, conv_w, sums)


def _attn_bwd(qkv, dattn, sinks, rope, sums):
    t = qkv.shape[0]
    n_steps = t // ATTN_STEP
    rev = lambda i: n_steps - 1 - i
    rc, rs1, rs2 = rope

    def body(sink_ref, q_ref, kp_ref, kc_ref, vp_ref, vc_ref, do_ref, c_ref, s1_ref, s2_ref, sums_ref,
             dqkv_ref, dsink_ref, arrived_ref, ck_ref, cv_ref, kacc_ref, vacc_ref, send_sems, recv_sems):
        i = pl.program_id(0)
        _chip_exchange_beside(i == 0, i == n_steps - 1, [sums_ref], [arrived_ref], (send_sems, recv_sems))

        @pl.when(i == 0)
        def _():
            dsink_ref[...] = jnp.zeros_like(dsink_ref)
            ck_ref[...] = jnp.zeros_like(ck_ref)
            cv_ref[...] = jnp.zeros_like(cv_ref)

        kacc_ref[...] = jnp.zeros_like(kacc_ref)
        vacc_ref[...] = jnp.zeros_like(vacc_ref)
        q = q_ref[...]
        dout = do_ref[...].astype(BF16)
        keys = jnp.concatenate([kp_ref[...], kc_ref[...]], axis=0)
        vals = jnp.concatenate([vp_ref[...], vc_ref[...]], axis=0)
        sink = [_group_sinks(sink_ref, g) for g in range(N_KV)]
        c, s1, s2 = c_ref[...], s1_ref[...], s2_ref[...]
        lane = lax.broadcasted_iota(jnp.int32, (1, 128), 1)
        dsink = jnp.zeros((1, 128), F32)
        for b in range(ATTN_STEP_BLOCKS):
            rows, window = slice(BLOCK * b, BLOCK * (b + 1)), slice(BLOCK * b, BLOCK * (b + 2))
            valid = _attn_mask(True if b else rev(i) > 0)
            dq_parts, dk_parts, dv_parts = [], [], []
            for g in range(N_KV):
                gs = slice(HEAD_DIM * g, HEAD_DIM * (g + 1))
                kk, vv = keys[window, gs], vals[window, gs]
                qs, dos = _stack_heads(q[rows], g), _stack_heads(dout[rows], g)
                probs, psink = _attn_probs(qs, kk, sink[g], valid)
                dp = _mm_nt(dos, vv)
                delta = jnp.sum(probs * dp, axis=-1, keepdims=True)
                ds = (probs * (dp - delta) * ATTN_SCALE).astype(BF16)
                sink_terms = psink * delta
                for hh in range(GROUP):
                    head_sum = jnp.sum(sink_terms[BLOCK * hh:BLOCK * (hh + 1), :])
                    dsink = dsink + jnp.where(lane == GROUP * g + hh, -head_sum, 0.0)
                dq_parts.append(_unstack_heads(_mm(ds, kk)))
                dk_parts.append(_mm_tn(ds, qs))
                dv_parts.append(_mm_tn(probs.astype(BF16), dos))
            kacc_ref[window, :] += jnp.concatenate(dk_parts, axis=1)
            vacc_ref[window, :] += jnp.concatenate(dv_parts, axis=1)
            dq = jnp.concatenate(dq_parts, axis=1)
            for ci in range(ATTN_W // 128):
                sl = slice(128 * ci, 128 * (ci + 1))
                dqkv_ref[rows, sl] = _rope_transpose(dq[:, sl], c[rows], s1[rows], s2[rows]).astype(BF16)
        kacc_ref[ATTN_STEP:, :] += ck_ref[...]
        vacc_ref[ATTN_STEP:, :] += cv_ref[...]
        ck_ref[...] = kacc_ref[:BLOCK, :]
        cv_ref[...] = vacc_ref[:BLOCK, :]
        dqkv_ref[:, ATTN_W:ATTN_W + KV_W] = _rope_transpose(kacc_ref[BLOCK:, :], c, s1, s2).astype(BF16)
        dqkv_ref[:, ATTN_W + KV_W:] = vacc_ref[BLOCK:, :].astype(BF16)
        dsink_ref[0:1, :] += dsink

    blk = lambda w_: pl.BlockSpec((ATTN_STEP, w_), lambda i: (rev(i), 0))
    return pl.pallas_call(
        body, name="attn_bwd", grid=(n_steps,),
        in_specs=[pl.BlockSpec(memory_space=pltpu.SMEM)] + _qkv_specs(rev) + [blk(ATTN_W), blk(128), blk(128), blk(128),
                                                                              HBM_SPEC],
        out_specs=[blk(QKV_W), _full((8, 128)), HBM_SPEC],
        out_shape=[jax.ShapeDtypeStruct((t, QKV_W), BF16), jax.ShapeDtypeStruct((8, 128), F32),
                   jax.ShapeDtypeStruct(sums.shape, sums.dtype)],
        scratch_shapes=[pltpu.VMEM((BLOCK, KV_W), F32), pltpu.VMEM((BLOCK, KV_W), F32),
                        pltpu.VMEM((ATTN_KEYS, KV_W), F32), pltpu.VMEM((ATTN_KEYS, KV_W), F32),
                        pltpu.SemaphoreType.DMA((len(CHIP_FLIPS),)), pltpu.SemaphoreType.DMA((len(CHIP_FLIPS),))],
        compiler_params=_params("arbitrary"),
    )(sinks, qkv, qkv, qkv, qkv, qkv, dattn, rc, rs1, rs2, sums)


def _grad_x_tile(dq, dg, x_hat, r, g1, w_ref, dh):
    dhn = _mm_nt(dq, w_ref[:, :QKV_W]) + _mm_nt(dg, w_ref[:, QKV_W:])
    dx, dg1 = _rms_bwd(x_hat, r, g1, dhn)
    return dh + dx, _colsum(dg1)


def _in_proj_bwd(dqkv, dgates, x, dh, g1, w_in, tm, out_sums):
    t = x.shape[0]
    n = t // tm
    n_steps = 2 * n
    n_far = len(CHIP_FLIPS)
    shard = (D_MODEL, IN_SHARD)

    def body(dq_ref, dg_ref, x_ref, dh_ref, g1_ref, w_ref, osums_ref,
             dx_ref, own_ref, sib_ref, far_ref, dg1_ref, oarrived_ref,
             acc_ref, send_buf, land_buf, pair_buf, d2d_send, d2d_recv, ici_send, ici_recv, o_send, o_recv):
        i = pl.program_id(0)
        x_pos, y_pos, c = _mesh_pos()
        my_chip = 2 * x_pos + y_pos
        sibling = (x_pos, y_pos, 1 - c)
        _chip_exchange_beside(i == 0, i == n_steps - 1, [osums_ref], [oarrived_ref], (o_send, o_recv))

        def cols(d):
            return slice(IN_SHARD * d, IN_SHARD * (d + 1))

        def hand_over(chip):
            return _push(send_buf.at[chip], land_buf.at[chip], (d2d_send, d2d_recv), chip, sibling)

        def to_chip(chip, rel):
            return pltpu.make_async_remote_copy(
                src_ref=pair_buf.at[chip], dst_ref=far_ref.at[rel - 1], send_sem=ici_send.at[rel - 1],
                recv_sem=ici_recv.at[rel - 1], device_id=(chip // 2, chip % 2, c), device_id_type=MESH)

        @pl.when(i == 0)
        def _():
            acc_ref[...] = jnp.zeros_like(acc_ref)
            dg1_ref[...] = jnp.zeros_like(dg1_ref)

        xv = x_ref[...]
        r = _inv_rms(xv)
        x_hat = xv * r
        g1 = g1_ref[...]
        dq, dg = dq_ref[...], dg_ref[...]

        @pl.when(i < n)
        def _():
            hn = (x_hat * g1).astype(BF16)
            acc_ref[:, :QKV_W] += _mm_tn(hn, dq)
            acc_ref[:, QKV_W:] += _mm_tn(hn, dg)

        @pl.when(i == n - 1)
        def _():
            for d in range(N_DEV):
                @pl.when(d % 2 != c)
                def _():
                    send_buf[d // 2] = acc_ref[:, cols(d)].astype(BF16)
                    hand_over(d // 2).start()
            for d in range(N_DEV):
                chip = d // 2

                @pl.when(d % 2 == c)
                def _():
                    hand_over(chip).wait_recv()

                    @pl.when(chip == my_chip)
                    def _():
                        own_ref[...] = acc_ref[:, cols(d)]
                        sib_ref[...] = land_buf[chip]

                    @pl.when(chip != my_chip)
                    def _():
                        pair_buf[chip] = (acc_ref[:, cols(d)] + land_buf[chip].astype(F32)).astype(BF16)
                        to_chip(chip, chip ^ my_chip).start()
            for chip in range(N_CHIPS):
                hand_over(chip).wait_send()

        @pl.when(i >= n)
        def _():
            dx_ref[...], dg1 = _grad_x_tile(dq, dg, x_hat, r, g1, w_ref, dh_ref[...])
            dg1_ref[0:1, :] += dg1

        @pl.when(i == n_steps - 1)
        def _():
            for rel in range(1, n_far + 1):
                to_chip(0, rel).wait()

    both = lambda w_: pl.BlockSpec((tm, w_), lambda i: (i % n, 0))
    second = pl.BlockSpec((tm, D_MODEL), lambda i: (jnp.maximum(i - n, 0), 0))
    whole = lambda dtype: jax.ShapeDtypeStruct(shard, dtype)
    sems = lambda k: pltpu.SemaphoreType.DMA((k,))
    res = pl.pallas_call(
        body, name="in_proj_bwd", grid=(n_steps,),
        in_specs=[both(QKV_W), both(GATES_W), both(D_MODEL), second, _full((1, D_MODEL)), _resident((D_MODEL, IN_COLS)),
                  HBM_SPEC],
        out_specs=[second, _full(shard), _full(shard), HBM_SPEC, _full((SMALL_ROWS, D_MODEL)), HBM_SPEC],
        out_shape=[jax.ShapeDtypeStruct((t, D_MODEL), F32), whole(F32), whole(BF16),
                   jax.ShapeDtypeStruct((n_far,) + shard, BF16), jax.ShapeDtypeStruct((SMALL_ROWS, D_MODEL), F32),
                   jax.ShapeDtypeStruct(out_sums.shape, out_sums.dtype)],
        scratch_shapes=[pltpu.VMEM((D_MODEL, IN_COLS), F32), pltpu.VMEM((N_CHIPS,) + shard, BF16),
                        pltpu.VMEM((N_CHIPS,) + shard, BF16), pltpu.VMEM((N_CHIPS,) + shard, BF16),
                        sems(N_CHIPS), sems(N_CHIPS), sems(n_far), sems(n_far), sems(n_far), sems(n_far)],
        compiler_params=_params("arbitrary"),
    )(dqkv, dgates, x, dh, g1, w_in, out_sums)
    return res[0], (res[1], res[2], res[3]), res[4], res[5]


def _all_gather(shards, name):
    n = len(shards)

    def body(*refs):
        _gather_now(refs[:n], refs[n:2 * n], *refs[2 * n:])

    return pl.pallas_call(
        body, name=name,
        in_specs=[HBM_SPEC] * n, out_specs=[HBM_SPEC] * n,
        out_shape=[jax.ShapeDtypeStruct((N_DEV,) + s.shape, s.dtype) for s in shards],
        scratch_shapes=[pltpu.SemaphoreType.DMA((7 * n,)), pltpu.SemaphoreType.DMA((7 * n,)),
                        pltpu.SemaphoreType.DMA((n,))],
    )(*shards)


def _adam_math(w, g, m, v):
    m = ADAM_B1 * m + (1.0 - ADAM_B1) * g
    v = ADAM_B2 * v + (1.0 - ADAM_B2) * (g * g)
    m_hat = m / (1.0 - ADAM_B1 ** ADAM_STEP)
    v_hat = v / (1.0 - ADAM_B2 ** ADAM_STEP)
    delta = -ADAM_LR * (m_hat / (jnp.sqrt(v_hat) + ADAM_EPS) + ADAM_WD * w)
    return delta, m, v


def _adamw_reduced(w, m, v, own, from_sibling, from_chips, tr):
    rows, cols = w.shape

    def body(w_ref, m_ref, v_ref, own_ref, sib_ref, far_ref, g_ref, d_ref, nm_ref, nv_ref):
        g = own_ref[...] + sib_ref[...].astype(F32)
        for k in range(len(CHIP_FLIPS)):
            g = g + far_ref[k].astype(F32)
        g_ref[...] = g
        d_ref[...], nm_ref[...], nv_ref[...] = _adam_math(w_ref[...], g, m_ref[...], v_ref[...])

    tile = pl.BlockSpec((tr, cols), lambda i: (i, 0))
    out = jax.ShapeDtypeStruct((rows, cols), F32)
    return pl.pallas_call(
        body, name="adamw_reduced", grid=(rows // tr,),
        in_specs=[tile] * 5 + [pl.BlockSpec((len(CHIP_FLIPS), tr, cols), lambda i: (0, i, 0))],
        out_specs=[tile] * 4, out_shape=[out] * 4,
        compiler_params=_params("parallel"),
    )(w, m, v, own, from_sibling, from_chips)


def _sum_devices(gathered):
    _, rows, cols = gathered.shape

    def body(g_ref, o_ref):
        s = g_ref[0]
        for d in range(1, N_DEV):
            s = s + g_ref[d]
        o_ref[...] = s

    return pl.pallas_call(
        body, name="sum_devices", in_specs=[_full(gathered.shape)], out_specs=_full((rows, cols)), grid=(1,),
        out_shape=jax.ShapeDtypeStruct((rows, cols), F32),
    )(gathered)


def _adamw_small(w, g, m, v):
    def body(w_ref, g_ref, m_ref, v_ref, d_ref, nm_ref, nv_ref):
        d_ref[...], nm_ref[...], nv_ref[...] = _adam_math(w_ref[...], g_ref[...], m_ref[...], v_ref[...])

    spec = _full(w.shape)
    out = jax.ShapeDtypeStruct(w.shape, F32)
    return pl.pallas_call(
        body, name="adamw_small", grid=(1,), in_specs=[spec] * 4, out_specs=[spec] * 3, out_shape=[out] * 3,
    )(w, g, m, v)


TOKEN_TILE = 512
MID_TILE = 256
ADAM_ROWS = 128


def _local_grads(x, target, g1, w_in_shard, conv_shard, sinks, g_attn, g_conv, g2, g3, g4, shards, order):
    t = x.shape[0]
    tm = min(TOKEN_TILE, t)
    rope = _rope_tables(t)
    qkv, gates, mconv, w_in, conv_w, gathered = _in_proj_fwd(x, g1, w_in_shard, conv_shard, g_conv, rope, tm, shards,
                                                             (False, True, False))
    attn, mattn, (w_out, w_up, w_down) = _attn_fwd(qkv, sinks, g_attn, shards, gathered)
    actt, dup, hn2t, dmo, dmix, dh, dmixed, small_mid = _mid(
        mattn, mconv, x, target, g2, g3, g4, w_out.reshape(D_MODEL, D_MODEL),
        w_up, w_down.reshape(D_FF, D_MODEL), min(MID_TILE, t))
    up_own, up_sib, up_sums = _dw_pair_sums((hn2t, dup), order, "up", "dw_up")
    down_own, down_sib, down_sums = _dw_pair_sums((actt, dmo), order, "down", "dw_down")
    out_own, out_sib, out_sums = _dw_pair_sums((mattn, mconv, dmix), order, "out", "dw_out")
    dattn, dgates, small_mix, up_far = _mix_bwd(dmixed, attn, gates, g_attn, g_conv, conv_w, tm, up_sums)
    dqkv, dsink, down_far = _attn_bwd(qkv, dattn, sinks, rope, down_sums)
    grad_x, dw_in, small_in, out_far = _in_proj_bwd(dqkv, dgates, x, dh, g1, w_in, tm, out_sums)
    dw_out, dw_up, dw_down = (out_own, out_sib, out_far), (up_own, up_sib, up_far), (down_own, down_sib, down_far)
    return grad_x, dw_in, dw_out, dw_up, dw_down, (small_mid, small_mix, dsink, small_in)


def _pack_small(small_mid, small_mix, dsink, small_in):
    z = lambda n: jnp.zeros((1, n), F32)
    rows = [
        small_mid[ROW_LOSS:ROW_LOSS + 1],
        small_in[0:1],
        small_mid[ROW_G2:ROW_G2 + 1],
        small_mid[ROW_G3:ROW_G3 + 1],
        small_mid[ROW_G4:ROW_G4 + 1],
        jnp.concatenate([small_mix[ROW_GATTN:ROW_GATTN + 1], small_mix[ROW_GCONV:ROW_GCONV + 1]], axis=1),
        jnp.concatenate([small_mix[ROW_CW0:ROW_CW0 + 1], small_mix[ROW_CW0 + 1:ROW_CW0 + 2]], axis=1),
        jnp.concatenate([small_mix[ROW_CW0 + 2:ROW_CW0 + 3], dsink[0:1, :], z(D_MODEL - CONV_W - 128)], axis=1),
    ]
    return jnp.concatenate(rows, axis=0)


def kernel(x, pre_mix_norm, w_in, conv_w, attn_sinks, attn_group_norm, conv_group_norm, w_out, post_mix_norm, pre_mlp_norm, w_up, w_down, post_mlp_norm, loss_target, m_pre_mix_norm, m_w_in, m_conv_w, m_attn_sinks, m_attn_group_norm, m_conv_group_norm, m_w_out, m_post_mix_norm, m_pre_mlp_norm, m_w_up, m_w_down, m_post_mlp_norm, v_pre_mix_norm, v_w_in, v_conv_w, v_attn_sinks, v_attn_group_norm, v_conv_group_norm, v_w_out, v_post_mix_norm, v_pre_mlp_norm, v_w_up, v_w_down, v_post_mlp_norm):
    xi, yi, ci = _mesh_pos()
    chip = 2 * xi + yi
    dev = 2 * chip + ci

    order = _block_order(dev)

    shards = [w_out[0].astype(BF16), w_up[0].astype(BF16), w_down[0].astype(BF16)]

    grad_x, dw_in, dw_out, dw_up, dw_down, smalls = _local_grads(
        x[0], loss_target[0], pre_mix_norm, w_in[0].astype(BF16), conv_w[0], attn_sinks, attn_group_norm, conv_group_norm,
        post_mix_norm, pre_mlp_norm, post_mlp_norm, shards, order)

    small = _sum_devices(_all_gather([_pack_small(*smalls)], "gather_small")[0])
    loss = (0.5 / D_MODEL) * jnp.sum(small[0])

    big = {}
    for name, w, m, v, (own, sib, far) in zip(
            ("w_in", "w_out", "w_up", "w_down"), (w_in, w_out, w_up, w_down), (m_w_in, m_w_out, m_w_up, m_w_down),
            (v_w_in, v_w_out, v_w_up, v_w_down), (dw_in, dw_out, dw_up, dw_down)):
        big[name] = [a[None] for a in _adamw_reduced(w[0], m[0], v[0], own, sib, far, ADAM_ROWS)]

    conv_g = lax.dynamic_slice(
        jnp.stack([small[6, :CONV_W], small[6, CONV_W:], small[7, :CONV_W]]), (0, dev * (CONV_W // N_DEV)),
        (3, CONV_W // N_DEV))
    pad = lambda a, n: jnp.pad(a.reshape(1, -1), ((0, 0), (0, n - a.size)))
    small_names = ("pre_mix_norm", "post_mix_norm", "pre_mlp_norm", "post_mlp_norm")
    small_w = {"pre_mix_norm": (pre_mix_norm, m_pre_mix_norm, v_pre_mix_norm),
               "post_mix_norm": (post_mix_norm, m_post_mix_norm, v_post_mix_norm),
               "pre_mlp_norm": (pre_mlp_norm, m_pre_mlp_norm, v_pre_mlp_norm),
               "post_mlp_norm": (post_mlp_norm, m_post_mlp_norm, v_post_mlp_norm)}

    def pack(k):
        rows = [small_w[nm][k] for nm in small_names]
        rows.append(jnp.concatenate([(attn_group_norm, m_attn_group_norm, v_attn_group_norm)[k],
                                     (conv_group_norm, m_conv_group_norm, v_conv_group_norm)[k]], axis=1))
        rows.append(pad((conv_w, m_conv_w, v_conv_w)[k], D_MODEL))
        rows.append(pad((attn_sinks, m_attn_sinks, v_attn_sinks)[k], D_MODEL))
        rows.append(jnp.zeros((1, D_MODEL), F32))
        return jnp.concatenate(rows, axis=0)

    g_small = jnp.concatenate(
        [small[1:6], pad(conv_g, D_MODEL), pad(small[7, CONV_W:CONV_W + N_HEADS], D_MODEL), jnp.zeros((1, D_MODEL), F32)],
        axis=0)
    d_small, nm_small, nv_small = _adamw_small(pack(0), g_small, pack(1), pack(2))

    def unpack(a):
        nconv = 3 * CONV_W // N_DEV
        return {"pre_mix_norm": a[0:1], "post_mix_norm": a[1:2], "pre_mlp_norm": a[2:3], "post_mlp_norm": a[3:4],
                "attn_group_norm": a[4:5, :ATTN_W], "conv_group_norm": a[4:5, ATTN_W:],
                "conv_w": a[5, :nconv].reshape(1, 3, CONV_W // N_DEV), "attn_sinks": a[6:7, :N_HEADS]}

    order = ("pre_mix_norm", "w_in", "conv_w", "attn_sinks", "attn_group_norm", "conv_group_norm", "w_out",
             "post_mix_norm", "pre_mlp_norm", "w_up", "w_down", "post_mlp_norm")
    outs = []
    for k, a in enumerate((g_small, d_small, nm_small, nv_small)):
        sm = unpack(a)
        outs += [big[nm][k] if nm in big else sm[nm] for nm in order]
    return (loss, grad_x[None], *outs)
```

```python
import functools

import jax
import jax.numpy as jnp
import numpy as np
from jax import lax
from jax.experimental import pallas as pl
from jax.experimental.pallas import tpu as pltpu

F32 = jnp.float32
BF16 = jnp.bfloat16

D_MODEL = 1024
HEAD_DIM = 64
ATTN_W = 512
CONV_W = 512
N_HEADS = 8
N_KV = 2
GROUP = 4
KV_W = 128
QKV_W = ATTN_W + 2 * KV_W
GATES_W = 3 * CONV_W
IN_COLS = QKV_W + GATES_W
D_FF = 4096
FF_CHUNK = 512
N_FF_CHUNKS = D_FF // FF_CHUNK
BLOCK = 128
ROT_HALF = 8
ROPE_THETA = 500000.0
NORM_EPS = 1e-6
NEG_INF = -1e30
ATTN_SCALE = 0.125
N_DEV = 8
N_CHIPS = 4
IN_SHARD = IN_COLS // N_DEV

ADAM_LR = 0.001
ADAM_B1 = 0.9
ADAM_B2 = 0.999
ADAM_EPS = 1e-08
ADAM_WD = 0.01
ADAM_STEP = 10

V7X_VMEM_BYTES = 64 * 1024 * 1024
VMEM_LIMIT = V7X_VMEM_BYTES - 2 * 1024 * 1024

MESH = pl.DeviceIdType.MESH
HBM_SPEC = pl.BlockSpec(memory_space=pltpu.HBM)


def _params(*sem):
    return pltpu.CompilerParams(dimension_semantics=sem, vmem_limit_bytes=VMEM_LIMIT)


def _mm(a, b):
    return jnp.dot(a, b, preferred_element_type=F32)


def _mm_nt(a, b):
    return lax.dot_general(a, b, (((1,), (1,)), ((), ())), preferred_element_type=F32)


def _mm_tn(a, b):
    return lax.dot_general(a, b, (((0,), (0,)), ((), ())), preferred_element_type=F32)


def _inv_rms(x):
    return lax.rsqrt(jnp.mean(x * x, axis=-1, keepdims=True) + NORM_EPS)


def _rms_bwd(xhat, r, gain, dy):
    gy = dy * gain
    return r * (gy - xhat * jnp.mean(gy * xhat, axis=-1, keepdims=True)), dy * xhat


def _colsum(a):
    return jnp.sum(a, axis=0, keepdims=True)


def _full(shape):
    zeros = (0,) * len(shape)
    return pl.BlockSpec(shape, lambda *_: zeros)


def _resident(shape):
    zeros = (0,) * len(shape)
    return pl.BlockSpec(shape, lambda *_: zeros, pipeline_mode=pl.Buffered(1))


def _rope_tables(t):
    pos = np.arange(t, dtype=np.float32)
    inv_freq = (ROPE_THETA ** (-np.arange(0, 2 * ROT_HALF, 2, dtype=np.float64) / (2 * ROT_HALF))).astype(np.float32)
    ang = (pos[:, None] * inv_freq[None, :]).astype(np.float64)
    cos, sin = np.cos(ang).astype(np.float32), np.sin(ang).astype(np.float32)
    zeros8 = np.zeros((t, ROT_HALF), np.float32)
    rest = np.zeros((t, HEAD_DIM - 2 * ROT_HALF), np.float32)
    c_head = np.concatenate([cos, cos, rest + 1.0], axis=1)
    s1_head = np.concatenate([zeros8, sin, rest], axis=1)
    s2_head = np.concatenate([-sin, zeros8, rest], axis=1)
    two = lambda a: jnp.asarray(np.concatenate([a, a], axis=1))
    return two(c_head), two(s1_head), two(s2_head)


def _rope(v, c, s1, s2):
    return v * c + pltpu.roll(v, ROT_HALF, 1) * s1 + pltpu.roll(v, 128 - ROT_HALF, 1) * s2


def _rope_transpose(dv, c, s1, s2):
    return dv * c + pltpu.roll(dv * s1, 128 - ROT_HALF, 1) + pltpu.roll(dv * s2, ROT_HALF, 1)


def _shift_rows_down(u, prev, k):
    row = lax.broadcasted_iota(jnp.int32, u.shape, 0)
    out = pltpu.roll(u, k, 0)
    for r in range(k):
        out = jnp.where(row == r, prev[8 - k + r:8 - k + r + 1, :], out)
    return out


def _shift_rows_up(u, nxt, k):
    n = u.shape[0]
    row = lax.broadcasted_iota(jnp.int32, u.shape, 0)
    out = pltpu.roll(u, n - k, 0)
    for r in range(k):
        out = jnp.where(row == n - k + r, nxt[r:r + 1, :], out)
    return out


def _conv3(u, u1, u2, w):
    return (w[0:1, :] * u2 + w[1:2, :] * u1) + w[2:3, :] * u


def _mesh_pos():
    return lax.axis_index("x"), lax.axis_index("y"), lax.axis_index("c")


def _slot(ref, pos):
    dev = 4 * pos[0] + 2 * pos[1] + pos[2]
    if len(ref.shape) == 2:
        width = ref.shape[1] // N_DEV
        return ref.at[:, pl.ds(pl.multiple_of(dev * width, width), width)]
    return ref.at[dev]


def _gathered_shape(shard, by_cols):
    if by_cols:
        return jax.ShapeDtypeStruct((shard.shape[0], N_DEV * shard.shape[1]), shard.dtype)
    return jax.ShapeDtypeStruct((N_DEV,) + shard.shape, shard.dtype)


def _push(src, dst, sems, k, to):
    send_sems, recv_sems = sems
    return pltpu.make_async_remote_copy(src_ref=src, dst_ref=dst, send_sem=send_sems.at[k], recv_sem=recv_sems.at[k],
                                        device_id=to, device_id_type=MESH)


def _gather_now(shards, outs, send_sems, recv_sems, local_sems, once_started=None):
    n = len(shards)
    x, y, c = _mesh_pos()
    me, sibling = (x, y, c), (x, y, 1 - c)
    chips = [(1 - x, y), (x, 1 - y), (1 - x, 1 - y)]

    def copy(i, k, block, to, src=None):
        dst = _slot(outs[i], block)
        return _push(dst if src is None else src, dst, (send_sems, recv_sems), 7 * i + k, to)

    mine = [pltpu.make_async_copy(shards[i], _slot(outs[i], me), local_sems.at[i]) for i in range(n)]
    for cp in mine:
        cp.start()
    first = []
    for i in range(n):
        first.append(copy(i, 0, me, sibling, src=shards[i]))
        first += [copy(i, 1 + j, me, (*chip, c), src=shards[i]) for j, chip in enumerate(chips)]
    for cp in first:
        cp.start()
    if once_started is not None:
        once_started()
    passed = []
    for j, chip in enumerate(chips):
        for i in range(n):
            copy(i, 1 + j, (*chip, c), me).wait_recv()
            cp = copy(i, 4 + j, (*chip, c), sibling)
            cp.start()
            passed.append(cp)
    for i in range(n):
        copy(i, 0, sibling, me).wait_recv()
        for j, chip in enumerate(chips):
            copy(i, 4 + j, (*chip, 1 - c), me).wait_recv()
    for cp in first + passed:
        cp.wait_send()
    for cp in mine:
        cp.wait()


def _gather_near(first, last, shards, outs, sems, local_sems):
    x, y, c = _mesh_pos()
    me, peers = (x, y, c), [(x, y, 1 - c), (1 - x, y, c), (x, 1 - y, c)]
    n = len(shards)
    local = [pltpu.make_async_copy(shards[i], _slot(outs[i], me), local_sems.at[i]) for i in range(n)]
    sends = [_push(shards[i], _slot(outs[i], me), sems, 3 * i + k, peers[k]) for i in range(n) for k in range(3)]
    arrivals = [_push(shards[i], _slot(outs[i], peers[k]), sems, 3 * i + k, peers[k]) for i in range(n) for k in range(3)]

    def start():
        for cp in local + sends:
            cp.start()

    if first is not None:
        pl.when(first)(start)

    @pl.when(last)
    def _():
        for cp in sends:
            cp.wait_send()
        for cp in arrivals:
            cp.wait_recv()
        for cp in local:
            cp.wait()

    return start


def _gather_far(first, last, shards, ins, outs, sems):
    x, y, c = _mesh_pos()
    me, sibling = (x, y, c), (x, y, 1 - c)
    chips = [(1 - x, y), (x, 1 - y), (1 - x, 1 - y)]
    n = len(shards)
    diag_send = [_push(shards[i], _slot(outs[i], me), sems, 4 * i, (*chips[2], c)) for i in range(n)]
    diag_arrival = [_push(shards[i], _slot(outs[i], (*chips[2], c)), sems, 4 * i, (*chips[2], c)) for i in range(n)]
    passed = [[_push(_slot(ins[i], (*chips[j], c)), _slot(outs[i], (*chips[j], c)), sems, 4 * i + 1 + j, sibling)
               for i in range(n)] for j in range(3)]
    from_sibling = [_push(shards[i], _slot(outs[i], (*chips[j], 1 - c)), sems, 4 * i + 1 + j, sibling)
                    for i in range(n) for j in range(3)]

    @pl.when(first)
    def _():
        for cp in diag_send + passed[0] + passed[1]:
            cp.start()

    @pl.when(last)
    def _():
        for cp in diag_arrival:
            cp.wait_recv()
        for cp in passed[2]:
            cp.start()
        for cp in from_sibling:
            cp.wait_recv()
        for cp in diag_send + passed[0] + passed[1] + passed[2]:
            cp.wait_send()


def _in_proj_fwd(x, g1, w_in, conv_w, g_conv, rope, tm, shards, by_cols):
    t = x.shape[0]
    rc, rs1, rs2 = rope
    n = len(shards)

    def body(*refs):
        x_ref, g1_ref, w_ref, cw_ref, gc_ref, c_ref, s1_ref, s2_ref = refs[:8]
        shard_refs = refs[8:8 + n]
        qkv_ref, gates_ref, mconv_ref, w_full_ref, cw_full_ref = refs[8 + n:13 + n]
        gathered = refs[13 + n:13 + 2 * n]
        carry_ref, w_land, cw_land = refs[13 + 2 * n:16 + 2 * n]
        now_sems = refs[16 + 2 * n:19 + 2 * n]
        step = pl.program_id(0)
        start_later_weights = _gather_near(None, step == pl.num_programs(0) - 1, shard_refs, gathered,
                                           refs[19 + 2 * n:21 + 2 * n], refs[21 + 2 * n]) if n else None

        @pl.when(step == 0)
        def _():
            carry_ref[...] = jnp.zeros_like(carry_ref)
            _gather_now([w_ref, cw_ref], [w_land, cw_land], *now_sems, once_started=start_later_weights)
            conv_shard = CONV_W // N_DEV
            for d in range(N_DEV):
                w_full_ref[:, IN_SHARD * d:IN_SHARD * (d + 1)] = w_land[d]
                cw_full_ref[:, conv_shard * d:conv_shard * (d + 1)] = cw_land[d]

        xv = x_ref[...]
        hn = ((xv * _inv_rms(xv)) * g1_ref[...]).astype(BF16)
        proj = _mm(hn, w_full_ref[...])
        c, s1, s2 = c_ref[...], s1_ref[...], s2_ref[...]
        for ci in range((ATTN_W + KV_W) // 128):
            sl = slice(128 * ci, 128 * (ci + 1))
            qkv_ref[:, sl] = _rope(proj[:, sl], c, s1, s2).astype(BF16)
        qkv_ref[:, ATTN_W + KV_W:QKV_W] = proj[:, ATTN_W + KV_W:QKV_W].astype(BF16)
        gates = proj[:, QKV_W:]
        gates_ref[...] = gates
        gb, gcc, xin = gates[:, :CONV_W], gates[:, CONV_W:2 * CONV_W], gates[:, 2 * CONV_W:]
        u = gcc * xin
        prev = carry_ref[...]
        conv = gb * _conv3(u, _shift_rows_down(u, prev, 1), _shift_rows_down(u, prev, 2), cw_full_ref[...])
        carry_ref[...] = u[tm - 8:tm, :]
        mconv_ref[...] = ((conv * _inv_rms(conv)) * gc_ref[...]).astype(BF16)

    tile = lambda w_: pl.BlockSpec((tm, w_), lambda i: (i, 0))
    sems = lambda k: pltpu.SemaphoreType.DMA((k,))
    res = pl.pallas_call(
        body, name="in_proj_fwd", grid=(t // tm,),
        in_specs=[tile(D_MODEL), _full((1, D_MODEL)), HBM_SPEC, HBM_SPEC, _full((1, CONV_W)), tile(128), tile(128),
                  tile(128)] + [HBM_SPEC] * n,
        out_specs=[tile(QKV_W), tile(GATES_W), tile(CONV_W), _full((D_MODEL, IN_COLS)), _full((3, CONV_W))]
        + [HBM_SPEC] * n,
        out_shape=[jax.ShapeDtypeStruct((t, QKV_W), BF16), jax.ShapeDtypeStruct((t, GATES_W), F32),
                   jax.ShapeDtypeStruct((t, CONV_W), BF16), jax.ShapeDtypeStruct((D_MODEL, IN_COLS), BF16),
                   jax.ShapeDtypeStruct((3, CONV_W), F32)]
        + [_gathered_shape(s, cols) for s, cols in zip(shards, by_cols)],
        scratch_shapes=[pltpu.VMEM((8, CONV_W), F32), pltpu.VMEM((N_DEV,) + w_in.shape, BF16),
                        pltpu.VMEM((N_DEV,) + conv_w.shape, F32), sems(14), sems(14), sems(2)]
        + ([sems(3 * n), sems(3 * n), sems(n)] if n else []),
        compiler_params=_params("arbitrary"),
    )(x, g1, w_in, conv_w, g_conv, rc, rs1, rs2, *shards)
    return res[0], res[1], res[2], res[3], res[4], list(res[5:])


GROUP_COLS = GROUP * BLOCK
ATTN_STEP_BLOCKS = 4


def _attn_mask(has_prev):
    key = lax.broadcasted_iota(jnp.int32, (2 * BLOCK, GROUP_COLS), 0)
    query = lax.broadcasted_iota(jnp.int32, (2 * BLOCK, GROUP_COLS), 1) & (BLOCK - 1)
    band = (key > query) & (key <= query + BLOCK)
    return band if has_prev is True else band & ((key >= BLOCK) | has_prev)


def _heads_side_by_side(at, g, b):
    heads = [at[HEAD_DIM * (GROUP * g + hh):HEAD_DIM * (GROUP * g + hh + 1), BLOCK * b:BLOCK * (b + 1)] for hh in range(GROUP)]
    return jnp.concatenate(heads, axis=1)


def _to_token_rows(parts):
    rows = [jnp.concatenate([parts[b][g][:, BLOCK * hh:BLOCK * (hh + 1)] for b in range(ATTN_STEP_BLOCKS)], axis=1)
            for g in range(N_KV) for hh in range(GROUP)]
    return jnp.concatenate(rows, axis=0).T


def _group_sinks(sink_ref, g):
    head = lax.broadcasted_iota(jnp.int32, (1, GROUP_COLS), 1) // BLOCK
    out = jnp.full((1, GROUP_COLS), sink_ref[0, GROUP * g], F32)
    for hh in range(1, GROUP):
        out = jnp.where(head == hh, sink_ref[0, GROUP * g + hh], out)
    return out


def _attn_probs(qt, kk, sink, valid):
    s = jnp.where(valid, _mm(kk, qt) * ATTN_SCALE, NEG_INF)
    m = jnp.maximum(jnp.max(s, axis=0, keepdims=True), sink)
    p = jnp.exp(s - m)
    psink = jnp.exp(sink - m)
    inv_l = 1.0 / (jnp.sum(p, axis=0, keepdims=True) + psink)
    return p * inv_l, psink * inv_l


ATTN_STEP = ATTN_STEP_BLOCKS * BLOCK
ATTN_KEYS = ATTN_STEP + BLOCK


def _qkv_specs(order):
    prev = lambda i: jnp.maximum(ATTN_STEP_BLOCKS * order(i) - 1, 0)
    kcol, vcol = ATTN_W // KV_W, ATTN_W // KV_W + 1
    return [pl.BlockSpec((ATTN_STEP, ATTN_W), lambda i: (order(i), 0)),
            pl.BlockSpec((BLOCK, KV_W), lambda i: (prev(i), kcol)), pl.BlockSpec((ATTN_STEP, KV_W), lambda i: (order(i), kcol)),
            pl.BlockSpec((BLOCK, KV_W), lambda i: (prev(i), vcol)), pl.BlockSpec((ATTN_STEP, KV_W), lambda i: (order(i), vcol))]


def _attn_fwd(qkv, sinks, g_attn, shards, gathered):
    t = qkv.shape[0]
    n = len(shards)

    def body(*refs):
        sink_ref, q_ref, kp_ref, kc_ref, vp_ref, vc_ref, ga_ref = refs[:7]
        attn_ref, mattn_ref = refs[7 + 2 * n:9 + 2 * n]
        step = pl.program_id(0)
        if n:
            _gather_far(step == 0, step == pl.num_programs(0) - 1, refs[7:7 + n], refs[7 + n:7 + 2 * n],
                        refs[9 + 2 * n:9 + 3 * n], refs[9 + 3 * n:11 + 3 * n])
        qt = q_ref[...].T
        keys = jnp.concatenate([kp_ref[...], kc_ref[...]], axis=0)
        vals = jnp.concatenate([vp_ref[...], vc_ref[...]], axis=0)
        sink = [_group_sinks(sink_ref, g) for g in range(N_KV)]
        parts = []
        for b in range(ATTN_STEP_BLOCKS):
            window = slice(BLOCK * b, BLOCK * (b + 2))
            valid = _attn_mask(True if b else step > 0)
            parts.append([])
            for g in range(N_KV):
                gs = slice(HEAD_DIM * g, HEAD_DIM * (g + 1))
                probs, _ = _attn_probs(_heads_side_by_side(qt, g, b), keys[window, gs], sink[g], valid)
                parts[b].append(_mm_tn(vals[window, gs], probs.astype(BF16)))
        attn = _to_token_rows(parts)
        attn_ref[...] = attn
        mattn_ref[...] = ((attn * _inv_rms(attn)) * ga_ref[...]).astype(BF16)

    blk = pl.BlockSpec((ATTN_STEP, ATTN_W), lambda j: (j, 0))
    res = pl.pallas_call(
        body, name="attn_fwd", grid=(t // ATTN_STEP,),
        in_specs=[pl.BlockSpec(memory_space=pltpu.SMEM)] + _qkv_specs(lambda j: j) + [_full((1, ATTN_W))]
        + [HBM_SPEC] * (2 * n),
        out_specs=[blk, blk] + [HBM_SPEC] * n,
        out_shape=[jax.ShapeDtypeStruct((t, ATTN_W), F32), jax.ShapeDtypeStruct((t, ATTN_W), BF16)]
        + [jax.ShapeDtypeStruct(g.shape, g.dtype) for g in gathered],
        input_output_aliases={7 + n + i: 2 + i for i in range(n)},
        scratch_shapes=[pltpu.SemaphoreType.DMA((4 * n,)), pltpu.SemaphoreType.DMA((4 * n,))] if n else [],
        compiler_params=_params("arbitrary"),
    )(sinks, qkv, qkv, qkv, qkv, qkv, g_attn, *shards, *gathered)
    return res[0], res[1], list(res[2:])


SMALL_ROWS = 8
ROW_LOSS, ROW_G2, ROW_G3, ROW_G4 = 0, 1, 2, 3


def _mid(mattn, mconv, x, target, g2, g3, g4, w_out, w_up, w_down, tm):
    t = x.shape[0]

    def body(ma_ref, mc_ref, x_ref, t_ref, g2_ref, g3_ref, g4_ref, wo_ref, wu_ref, wd_ref,
             actt_ref, dup_ref, hn2t_ref, dmo_ref, dmix_ref, dh_ref, dmixed_ref, small_ref, up_ref):
        @pl.when(pl.program_id(0) == 0)
        def _():
            small_ref[...] = jnp.zeros_like(small_ref)

        g2, g3, g4 = g2_ref[...], g3_ref[...], g4_ref[...]
        mix_out = _mm(ma_ref[...], wo_ref[0:ATTN_W, :]) + _mm(mc_ref[...], wo_ref[ATTN_W:, :])
        r2 = _inv_rms(mix_out)
        mo_hat = mix_out * r2
        h = x_ref[...] + mo_hat * g2
        r3 = _inv_rms(h)
        h_hat = h * r3
        hn2 = (h_hat * g3).astype(BF16)
        hn2t_ref[...] = hn2.T
        up = jnp.maximum(_mm(hn2, wu_ref[...]), 0.0)
        up_ref[...] = up.astype(BF16)
        act = (up * up).astype(BF16)
        actt_ref[...] = act.T
        mlp = _mm(act, wd_ref[...])
        r4 = _inv_rms(mlp)
        ml_hat = mlp * r4
        err = (h + ml_hat * g4) - t_ref[...]
        d_out = err * (1.0 / D_MODEL)
        d_mlp, dg4 = _rms_bwd(ml_hat, r4, g4, d_out)
        dmo = d_mlp.astype(BF16)
        dmo_ref[...] = dmo
        dup = (_mm_nt(dmo, wd_ref[...]) * (2.0 * up_ref[...].astype(F32))).astype(BF16)
        dup_ref[...] = dup
        dhn2 = _mm_nt(dup, wu_ref[...])
        dh_norm, dg3 = _rms_bwd(h_hat, r3, g3, dhn2)
        dh = d_out + dh_norm
        dh_ref[...] = dh
        d_mix, dg2 = _rms_bwd(mo_hat, r2, g2, dh)
        dmix = d_mix.astype(BF16)
        dmix_ref[...] = dmix
        dmixed_ref[...] = _mm_nt(dmix, wo_ref[...])
        small_ref[ROW_LOSS:ROW_LOSS + 1, :] += _colsum(err * err)
        small_ref[ROW_G2:ROW_G2 + 1, :] += _colsum(dg2)
        small_ref[ROW_G3:ROW_G3 + 1, :] += _colsum(dg3)
        small_ref[ROW_G4:ROW_G4 + 1, :] += _colsum(dg4)

    tile = lambda n: pl.BlockSpec((tm, n), lambda i: (i, 0))
    cols = lambda n: pl.BlockSpec((n, tm), lambda i: (0, i))
    gain = _full((1, D_MODEL))
    return pl.pallas_call(
        body, name="mid_fwd_bwd", grid=(t // tm,),
        in_specs=[tile(ATTN_W), tile(CONV_W), tile(D_MODEL), tile(D_MODEL), gain, gain, gain,
                  _resident((D_MODEL, D_MODEL)), _resident((D_MODEL, D_FF)), _resident((D_FF, D_MODEL))],
        out_specs=[cols(D_FF), tile(D_FF), cols(D_MODEL), tile(D_MODEL), tile(D_MODEL), tile(D_MODEL), tile(D_MODEL),
                   _full((SMALL_ROWS, D_MODEL))],
        out_shape=[jax.ShapeDtypeStruct((D_FF, t), BF16), jax.ShapeDtypeStruct((t, D_FF), BF16),
                   jax.ShapeDtypeStruct((D_MODEL, t), BF16), jax.ShapeDtypeStruct((t, D_MODEL), BF16),
                   jax.ShapeDtypeStruct((t, D_MODEL), BF16), jax.ShapeDtypeStruct((t, D_MODEL), F32),
                   jax.ShapeDtypeStruct((t, D_MODEL), F32), jax.ShapeDtypeStruct((SMALL_ROWS, D_MODEL), F32)],
        scratch_shapes=[pltpu.VMEM((tm, D_FF), BF16)],
        compiler_params=_params("arbitrary"),
    )(mattn, mconv, x, target, g2, g3, g4, w_out, w_up, w_down)


CHIP_FLIPS = ((1, 1), (1, 0), (0, 1))


def _block_order(dev):
    chip_masks = [4 * fx + 2 * fy for fx, fy in CHIP_FLIPS]
    masks = [m + 1 for m in chip_masks] + [1] + chip_masks + [0]
    return jnp.bitwise_xor(dev, jnp.asarray(masks, jnp.int32)).astype(jnp.int32)


def _other_chips(x, y, c):
    return [(1 - x if fx else x, 1 - y if fy else y, c) for fx, fy in CHIP_FLIPS]


def _dw_pair_sums(operands, order, which, name):
    t = operands[-1].shape[0]
    n_far = len(CHIP_FLIPS)
    n_in = len(operands)
    out_chunk = D_MODEL // N_DEV
    if which == "up":
        rows, cols = D_MODEL, FF_CHUNK
        in_specs = [_resident((D_MODEL, t)), pl.BlockSpec((t, FF_CHUNK), lambda s, order_ref: (0, order_ref[s]))]
    elif which == "down":
        rows, cols = FF_CHUNK, D_MODEL
        in_specs = [pl.BlockSpec((FF_CHUNK, t), lambda s, order_ref: (order_ref[s], 0)), _resident((t, D_MODEL))]
    else:
        rows, cols = out_chunk, D_MODEL
        half = pl.BlockSpec((t, out_chunk), lambda s, order_ref: (0, order_ref[s] % (N_DEV // 2)))
        in_specs = [half, half, _resident((t, D_MODEL))]

    def body(order_ref, *refs):
        own_ref, from_sib_ref, pair_ref, send_buf, land_buf, send_sems, recv_sems = refs[n_in:]
        s_now = pl.program_id(0)
        x, y, c = _mesh_pos()
        sibling = (x, y, 1 - c)
        sems = (send_sems, recv_sems)

        def hand_over(k):
            dst = land_buf.at[k] if k < n_far else from_sib_ref
            return _push(send_buf.at[k], dst, sems, k, sibling)

        if which == "out":
            ma_ref, mc_ref, b_ref = refs[:n_in]
            block = lax.cond(order_ref[s_now] < N_DEV // 2, lambda: _mm_tn(ma_ref[...], b_ref[...]),
                             lambda: _mm_tn(mc_ref[...], b_ref[...]))
        else:
            block = _mm(refs[0][...], refs[1][...])
        for k in range(n_far + 1):
            @pl.when(s_now == k)
            def _():
                send_buf[k] = block.astype(BF16)
                hand_over(k).start()

        for k in range(n_far):
            @pl.when(s_now == n_far + 1 + k)
            def _():
                hand_over(k).wait_recv()
                pair_ref[...] = (block + land_buf[k].astype(F32)).astype(BF16)

        @pl.when(s_now == N_DEV - 1)
        def _():
            own_ref[...] = block
            for k in range(n_far + 1):
                hand_over(k).wait_send()
            hand_over(n_far).wait_recv()

    return pl.pallas_call(
        body, name=name,
        grid_spec=pltpu.PrefetchScalarGridSpec(
            num_scalar_prefetch=1, grid=(N_DEV,), in_specs=in_specs,
            out_specs=[pl.BlockSpec((rows, cols), lambda s, order_ref: (0, 0)), HBM_SPEC,
                       pl.BlockSpec((None, rows, cols), lambda s, order_ref: (jnp.clip(s - n_far - 1, 0, n_far - 1), 0, 0))],
            scratch_shapes=[pltpu.VMEM((n_far + 1, rows, cols), BF16), pltpu.VMEM((n_far, rows, cols), BF16),
                            pltpu.SemaphoreType.DMA((n_far + 1,)), pltpu.SemaphoreType.DMA((n_far + 1,))]),
        out_shape=[jax.ShapeDtypeStruct((rows, cols), F32), jax.ShapeDtypeStruct((rows, cols), BF16),
                   jax.ShapeDtypeStruct((n_far, rows, cols), BF16)],
        compiler_params=_params("arbitrary"),
    )(order, *operands)


def _chip_exchange_beside(first, last, sums, outs, sems):
    chips = _other_chips(*_mesh_pos())
    copies = [_push(sums[i].at[k], outs[i].at[k], sems, len(chips) * i + k, chip)
              for i in range(len(sums)) for k, chip in enumerate(chips)]

    @pl.when(first)
    def _():
        for cp in copies:
            cp.start()

    @pl.when(last)
    def _():
        for cp in copies:
            cp.wait()


ROW_GATTN, ROW_GCONV, ROW_CW0 = 0, 1, 2


def _mix_bwd(dmixed, attn, gates, g_attn, g_conv, conv_w, tm, sums):
    t = attn.shape[0]
    n = t // tm
    rev = lambda i: n - 1 - i

    def body(dm_ref, attn_ref, gates_ref, gprev_ref, ga_ref, gc_ref, cw_ref, sums_ref, dattn_ref, dgates_ref, small_ref,
             arrived_ref, carry_ref, send_sems, recv_sems):
        i = pl.program_id(0)
        _chip_exchange_beside(i == 0, i == n - 1, [sums_ref], [arrived_ref], (send_sems, recv_sems))

        @pl.when(i == 0)
        def _():
            small_ref[...] = jnp.zeros_like(small_ref)
            carry_ref[...] = jnp.zeros_like(carry_ref)

        dm = dm_ref[...]
        a = attn_ref[...]
        ra = _inv_rms(a)
        a_hat = a * ra
        dattn, dga = _rms_bwd(a_hat, ra, ga_ref[...], dm[:, :ATTN_W])
        dattn_ref[...] = dattn

        gates = gates_ref[...]
        gb, gcc, xin = gates[:, :CONV_W], gates[:, CONV_W:2 * CONV_W], gates[:, 2 * CONV_W:]
        u = gcc * xin
        gp = gprev_ref[...]
        uprev = jnp.where(rev(i) == 0, 0.0, gp[:, CONV_W:2 * CONV_W] * gp[:, 2 * CONV_W:])
        u1, u2 = _shift_rows_down(u, uprev, 1), _shift_rows_down(u, uprev, 2)
        w = cw_ref[...]
        c = _conv3(u, u1, u2, w)
        conv = gb * c
        rcv = _inv_rms(conv)
        c_hat = conv * rcv
        dconv, dgc = _rms_bwd(c_hat, rcv, gc_ref[...], dm[:, ATTN_W:])
        dc = dconv * gb
        nxt = carry_ref[...]
        du = (w[2:3, :] * dc + w[1:2, :] * _shift_rows_up(dc, nxt, 1)) + w[0:1, :] * _shift_rows_up(dc, nxt, 2)
        carry_ref[...] = dc[0:8, :]
        dgates_ref[:, :CONV_W] = (dconv * c).astype(BF16)
        dgates_ref[:, CONV_W:2 * CONV_W] = (du * xin).astype(BF16)
        dgates_ref[:, 2 * CONV_W:] = (du * gcc).astype(BF16)
        small_ref[ROW_GATTN:ROW_GATTN + 1, :] += _colsum(dga)
        small_ref[ROW_GCONV:ROW_GCONV + 1, :] += _colsum(dgc)
        small_ref[ROW_CW0:ROW_CW0 + 1, :] += _colsum(dc * u2)
        small_ref[ROW_CW0 + 1:ROW_CW0 + 2, :] += _colsum(dc * u1)
        small_ref[ROW_CW0 + 2:ROW_CW0 + 3, :] += _colsum(dc * u)

    tile = lambda w_: pl.BlockSpec((tm, w_), lambda i: (rev(i), 0))
    prev8 = pl.BlockSpec((8, GATES_W), lambda i: (jnp.maximum(rev(i) * (tm // 8) - 1, 0), 0))
    return pl.pallas_call(
        body, name="mix_bwd", grid=(n,),
        in_specs=[tile(D_MODEL), tile(ATTN_W), tile(GATES_W), prev8, _full((1, ATTN_W)), _full((1, CONV_W)),
                  _full((3, CONV_W)), HBM_SPEC],
        out_specs=[tile(ATTN_W), tile(GATES_W), _full((SMALL_ROWS, CONV_W)), HBM_SPEC],
        out_shape=[jax.ShapeDtypeStruct((t, ATTN_W), F32), jax.ShapeDtypeStruct((t, GATES_W), BF16),
                   jax.ShapeDtypeStruct((SMALL_ROWS, CONV_W), F32), jax.ShapeDtypeStruct(sums.shape, sums.dtype)],
        scratch_shapes=[pltpu.VMEM((8, CONV_W), F32), pltpu.SemaphoreType.DMA((len(CHIP_FLIPS),)),
                        pltpu.SemaphoreType.DMA((len(CHIP_FLIPS),))],
        compiler_params=_params("arbitrary"),
    )(dmixed, attn, gates, gates, g_attn, g_conv, conv_w, sums)


def _attn_bwd(qkv, dattn, sinks, rope, sums):
    t = qkv.shape[0]
    n_steps = t // ATTN_STEP
    rev = lambda i: n_steps - 1 - i
    rc, rs1, rs2 = rope

    def body(sink_ref, q_ref, kp_ref, kc_ref, vp_ref, vc_ref, do_ref, c_ref, s1_ref, s2_ref, sums_ref,
             dqkv_ref, dsink_ref, arrived_ref, ck_ref, cv_ref, kacc_ref, vacc_ref, send_sems, recv_sems):
        i = pl.program_id(0)
        _chip_exchange_beside(i == 0, i == n_steps - 1, [sums_ref], [arrived_ref], (send_sems, recv_sems))

        @pl.when(i == 0)
        def _():
            dsink_ref[...] = jnp.zeros_like(dsink_ref)
            ck_ref[...] = jnp.zeros_like(ck_ref)
            cv_ref[...] = jnp.zeros_like(cv_ref)

        kacc_ref[...] = jnp.zeros_like(kacc_ref)
        vacc_ref[...] = jnp.zeros_like(vacc_ref)
        qt = q_ref[...].T
        dot = do_ref[...].astype(BF16).T
        keys = jnp.concatenate([kp_ref[...], kc_ref[...]], axis=0)
        vals = jnp.concatenate([vp_ref[...], vc_ref[...]], axis=0)
        sink = [_group_sinks(sink_ref, g) for g in range(N_KV)]
        c, s1, s2 = c_ref[...], s1_ref[...], s2_ref[...]
        lane = lax.broadcasted_iota(jnp.int32, (1, 128), 1)
        dsink = jnp.zeros((1, 128), F32)
        dq_parts = []
        for b in range(ATTN_STEP_BLOCKS):
            window = slice(BLOCK * b, BLOCK * (b + 2))
            valid = _attn_mask(True if b else rev(i) > 0)
            dq_parts.append([])
            dk_parts, dv_parts = [], []
            for g in range(N_KV):
                gs = slice(HEAD_DIM * g, HEAD_DIM * (g + 1))
                kk, vv = keys[window, gs], vals[window, gs]
                qtg, dotg = _heads_side_by_side(qt, g, b), _heads_side_by_side(dot, g, b)
                probs, psink = _attn_probs(qtg, kk, sink[g], valid)
                dp = _mm(vv, dotg)
                delta = jnp.sum(probs * dp, axis=0, keepdims=True)
                ds = (probs * (dp - delta) * ATTN_SCALE).astype(BF16)
                sink_terms = psink * delta
                for hh in range(GROUP):
                    head_sum = jnp.sum(sink_terms[:, BLOCK * hh:BLOCK * (hh + 1)])
                    dsink = dsink + jnp.where(lane == GROUP * g + hh, -head_sum, 0.0)
                dq_parts[b].append(_mm_tn(kk, ds))
                dk_parts.append(_mm_nt(ds, qtg))
                dv_parts.append(_mm_nt(probs.astype(BF16), dotg))
            kacc_ref[window, :] += jnp.concatenate(dk_parts, axis=1)
            vacc_ref[window, :] += jnp.concatenate(dv_parts, axis=1)
        dq = _to_token_rows(dq_parts)
        for ci in range(ATTN_W // 128):
            sl = slice(128 * ci, 128 * (ci + 1))
            dqkv_ref[:, sl] = _rope_transpose(dq[:, sl], c, s1, s2).astype(BF16)
        kacc_ref[ATTN_STEP:, :] += ck_ref[...]
        vacc_ref[ATTN_STEP:, :] += cv_ref[...]
        ck_ref[...] = kacc_ref[:BLOCK, :]
        cv_ref[...] = vacc_ref[:BLOCK, :]
        dqkv_ref[:, ATTN_W:ATTN_W + KV_W] = _rope_transpose(kacc_ref[BLOCK:, :], c, s1, s2).astype(BF16)
        dqkv_ref[:, ATTN_W + KV_W:] = vacc_ref[BLOCK:, :].astype(BF16)
        dsink_ref[0:1, :] += dsink

    blk = lambda w_: pl.BlockSpec((ATTN_STEP, w_), lambda i: (rev(i), 0))
    return pl.pallas_call(
        body, name="attn_bwd", grid=(n_steps,),
        in_specs=[pl.BlockSpec(memory_space=pltpu.SMEM)] + _qkv_specs(rev) + [blk(ATTN_W), blk(128), blk(128), blk(128),
                                                                              HBM_SPEC],
        out_specs=[blk(QKV_W), _full((8, 128)), HBM_SPEC],
        out_shape=[jax.ShapeDtypeStruct((t, QKV_W), BF16), jax.ShapeDtypeStruct((8, 128), F32),
                   jax.ShapeDtypeStruct(sums.shape, sums.dtype)],
        scratch_shapes=[pltpu.VMEM((BLOCK, KV_W), F32), pltpu.VMEM((BLOCK, KV_W), F32),
                        pltpu.VMEM((ATTN_KEYS, KV_W), F32), pltpu.VMEM((ATTN_KEYS, KV_W), F32),
                        pltpu.SemaphoreType.DMA((len(CHIP_FLIPS),)), pltpu.SemaphoreType.DMA((len(CHIP_FLIPS),))],
        compiler_params=_params("arbitrary"),
    )(sinks, qkv, qkv, qkv, qkv, qkv, dattn, rc, rs1, rs2, sums)


def _grad_x_tile(dq, dg, x_hat, r, g1, w_ref, dh):
    dhn = _mm_nt(dq, w_ref[:, :QKV_W]) + _mm_nt(dg, w_ref[:, QKV_W:])
    dx, dg1 = _rms_bwd(x_hat, r, g1, dhn)
    return dh + dx, _colsum(dg1)


def _in_proj_bwd(dqkv, dgates, x, dh, g1, w_in, tm, out_sums):
    t = x.shape[0]
    n = t // tm
    n_steps = 2 * n
    n_far = len(CHIP_FLIPS)
    shard = (D_MODEL, IN_SHARD)

    def body(dq_ref, dg_ref, x_ref, dh_ref, g1_ref, w_ref, osums_ref,
             dx_ref, own_ref, sib_ref, far_ref, dg1_ref, oarrived_ref,
             acc_ref, send_buf, land_buf, pair_buf, d2d_send, d2d_recv, ici_send, ici_recv, o_send, o_recv):
        i = pl.program_id(0)
        x_pos, y_pos, c = _mesh_pos()
        my_chip = 2 * x_pos + y_pos
        sibling = (x_pos, y_pos, 1 - c)
        _chip_exchange_beside(i == 0, i == n_steps - 1, [osums_ref], [oarrived_ref], (o_send, o_recv))

        def cols(d):
            return slice(IN_SHARD * d, IN_SHARD * (d + 1))

        def hand_over(chip):
            return _push(send_buf.at[chip], land_buf.at[chip], (d2d_send, d2d_recv), chip, sibling)

        def to_chip(chip, rel):
            return pltpu.make_async_remote_copy(
                src_ref=pair_buf.at[chip], dst_ref=far_ref.at[rel - 1], send_sem=ici_send.at[rel - 1],
                recv_sem=ici_recv.at[rel - 1], device_id=(chip // 2, chip % 2, c), device_id_type=MESH)

        @pl.when(i == 0)
        def _():
            acc_ref[...] = jnp.zeros_like(acc_ref)
            dg1_ref[...] = jnp.zeros_like(dg1_ref)

        def normed_x():
            xv = x_ref[...]
            r = _inv_rms(xv)
            return xv * r, r

        @pl.when(i < n)
        def _():
            hn = (normed_x()[0] * g1_ref[...]).astype(BF16)
            acc_ref[:, :QKV_W] += _mm_tn(hn, dq_ref[...])
            acc_ref[:, QKV_W:] += _mm_tn(hn, dg_ref[...])

        @pl.when(i == n - 1)
        def _():
            for d in range(N_DEV):
                @pl.when(d % 2 != c)
                def _():
                    send_buf[d // 2] = acc_ref[:, cols(d)].astype(BF16)
                    hand_over(d // 2).start()
            for d in range(N_DEV):
                chip = d // 2

                @pl.when(d % 2 == c)
                def _():
                    hand_over(chip).wait_recv()

                    @pl.when(chip == my_chip)
                    def _():
                        own_ref[...] = acc_ref[:, cols(d)]
                        sib_ref[...] = land_buf[chip]

                    @pl.when(chip != my_chip)
                    def _():
                        pair_buf[chip] = (acc_ref[:, cols(d)] + land_buf[chip].astype(F32)).astype(BF16)
                        to_chip(chip, chip ^ my_chip).start()
            for chip in range(N_CHIPS):
                hand_over(chip).wait_send()

        @pl.when(i >= n)
        def _():
            x_hat, r = normed_x()
            dx_ref[...], dg1 = _grad_x_tile(dq_ref[...], dg_ref[...], x_hat, r, g1_ref[...], w_ref, dh_ref[...])
            dg1_ref[0:1, :] += dg1

        @pl.when(i == n_steps - 1)
        def _():
            for rel in range(1, n_far + 1):
                to_chip(0, rel).wait()

    both = lambda w_: pl.BlockSpec((tm, w_), lambda i: (i % n, 0))
    second = pl.BlockSpec((tm, D_MODEL), lambda i: (jnp.maximum(i - n, 0), 0))
    whole = lambda dtype: jax.ShapeDtypeStruct(shard, dtype)
    sems = lambda k: pltpu.SemaphoreType.DMA((k,))
    res = pl.pallas_call(
        body, name="in_proj_bwd", grid=(n_steps,),
        in_specs=[both(QKV_W), both(GATES_W), both(D_MODEL), second, _full((1, D_MODEL)), _resident((D_MODEL, IN_COLS)),
                  HBM_SPEC],
        out_specs=[second, _full(shard), _full(shard), HBM_SPEC, _full((SMALL_ROWS, D_MODEL)), HBM_SPEC],
        out_shape=[jax.ShapeDtypeStruct((t, D_MODEL), F32), whole(F32), whole(BF16),
                   jax.ShapeDtypeStruct((n_far,) + shard, BF16), jax.ShapeDtypeStruct((SMALL_ROWS, D_MODEL), F32),
                   jax.ShapeDtypeStruct(out_sums.shape, out_sums.dtype)],
        scratch_shapes=[pltpu.VMEM((D_MODEL, IN_COLS), F32), pltpu.VMEM((N_CHIPS,) + shard, BF16),
                        pltpu.VMEM((N_CHIPS,) + shard, BF16), pltpu.VMEM((N_CHIPS,) + shard, BF16),
                        sems(N_CHIPS), sems(N_CHIPS), sems(n_far), sems(n_far), sems(n_far), sems(n_far)],
        compiler_params=_params("arbitrary"),
    )(dqkv, dgates, x, dh, g1, w_in, out_sums)
    return res[0], (res[1], res[2], res[3]), res[4], res[5]


def _all_gather(shards, name):
    n = len(shards)

    def body(*refs):
        _gather_now(refs[:n], refs[n:2 * n], *refs[2 * n:])

    return pl.pallas_call(
        body, name=name,
        in_specs=[HBM_SPEC] * n, out_specs=[HBM_SPEC] * n,
        out_shape=[jax.ShapeDtypeStruct((N_DEV,) + s.shape, s.dtype) for s in shards],
        scratch_shapes=[pltpu.SemaphoreType.DMA((7 * n,)), pltpu.SemaphoreType.DMA((7 * n,)),
                        pltpu.SemaphoreType.DMA((n,))],
    )(*shards)


def _adam_math(w, g, m, v):
    m = ADAM_B1 * m + (1.0 - ADAM_B1) * g
    v = ADAM_B2 * v + (1.0 - ADAM_B2) * (g * g)
    m_hat = m / (1.0 - ADAM_B1 ** ADAM_STEP)
    v_hat = v / (1.0 - ADAM_B2 ** ADAM_STEP)
    delta = -ADAM_LR * (m_hat / (jnp.sqrt(v_hat) + ADAM_EPS) + ADAM_WD * w)
    return delta, m, v


def _adamw_reduced(w, m, v, own, from_sibling, from_chips, tr):
    rows, cols = w.shape

    def body(w_ref, m_ref, v_ref, own_ref, sib_ref, far_ref, g_ref, d_ref, nm_ref, nv_ref):
        g = own_ref[...] + sib_ref[...].astype(F32)
        for k in range(len(CHIP_FLIPS)):
            g = g + far_ref[k].astype(F32)
        g_ref[...] = g
        d_ref[...], nm_ref[...], nv_ref[...] = _adam_math(w_ref[...], g, m_ref[...], v_ref[...])

    tile = pl.BlockSpec((tr, cols), lambda i: (i, 0))
    out = jax.ShapeDtypeStruct((rows, cols), F32)
    return pl.pallas_call(
        body, name="adamw_reduced", grid=(rows // tr,),
        in_specs=[tile] * 5 + [pl.BlockSpec((len(CHIP_FLIPS), tr, cols), lambda i: (0, i, 0))],
        out_specs=[tile] * 4, out_shape=[out] * 4,
        compiler_params=_params("parallel"),
    )(w, m, v, own, from_sibling, from_chips)


def _sum_devices(gathered):
    _, rows, cols = gathered.shape

    def body(g_ref, o_ref):
        s = g_ref[0]
        for d in range(1, N_DEV):
            s = s + g_ref[d]
        o_ref[...] = s

    return pl.pallas_call(
        body, name="sum_devices", in_specs=[_full(gathered.shape)], out_specs=_full((rows, cols)), grid=(1,),
        out_shape=jax.ShapeDtypeStruct((rows, cols), F32),
    )(gathered)


def _adamw_small(w, g, m, v):
    def body(w_ref, g_ref, m_ref, v_ref, d_ref, nm_ref, nv_ref):
        d_ref[...], nm_ref[...], nv_ref[...] = _adam_math(w_ref[...], g_ref[...], m_ref[...], v_ref[...])

    spec = _full(w.shape)
    out = jax.ShapeDtypeStruct(w.shape, F32)
    return pl.pallas_call(
        body, name="adamw_small", grid=(1,), in_specs=[spec] * 4, out_specs=[spec] * 3, out_shape=[out] * 3,
    )(w, g, m, v)


TOKEN_TILE = 512
MID_TILE = 256
ADAM_ROWS = 128


def _local_grads(x, target, g1, w_in_shard, conv_shard, sinks, g_attn, g_conv, g2, g3, g4, shards, order):
    t = x.shape[0]
    tm = min(TOKEN_TILE, t)
    rope = _rope_tables(t)
    qkv, gates, mconv, w_in, conv_w, gathered = _in_proj_fwd(x, g1, w_in_shard, conv_shard, g_conv, rope, tm, shards,
                                                             (False, True, False))
    attn, mattn, (w_out, w_up, w_down) = _attn_fwd(qkv, sinks, g_attn, shards, gathered)
    actt, dup, hn2t, dmo, dmix, dh, dmixed, small_mid = _mid(
        mattn, mconv, x, target, g2, g3, g4, w_out.reshape(D_MODEL, D_MODEL),
        w_up, w_down.reshape(D_FF, D_MODEL), min(MID_TILE, t))
    up_own, up_sib, up_sums = _dw_pair_sums((hn2t, dup), order, "up", "dw_up")
    down_own, down_sib, down_sums = _dw_pair_sums((actt, dmo), order, "down", "dw_down")
    out_own, out_sib, out_sums = _dw_pair_sums((mattn, mconv, dmix), order, "out", "dw_out")
    dattn, dgates, small_mix, up_far = _mix_bwd(dmixed, attn, gates, g_attn, g_conv, conv_w, tm, up_sums)
    dqkv, dsink, down_far = _attn_bwd(qkv, dattn, sinks, rope, down_sums)
    grad_x, dw_in, small_in, out_far = _in_proj_bwd(dqkv, dgates, x, dh, g1, w_in, tm, out_sums)
    dw_out, dw_up, dw_down = (out_own, out_sib, out_far), (up_own, up_sib, up_far), (down_own, down_sib, down_far)
    return grad_x, dw_in, dw_out, dw_up, dw_down, (small_mid, small_mix, dsink, small_in)


def _pack_small(small_mid, small_mix, dsink, small_in):
    z = lambda n: jnp.zeros((1, n), F32)
    rows = [
        small_mid[ROW_LOSS:ROW_LOSS + 1],
        small_in[0:1],
        small_mid[ROW_G2:ROW_G2 + 1],
        small_mid[ROW_G3:ROW_G3 + 1],
        small_mid[ROW_G4:ROW_G4 + 1],
        jnp.concatenate([small_mix[ROW_GATTN:ROW_GATTN + 1], small_mix[ROW_GCONV:ROW_GCONV + 1]], axis=1),
        jnp.concatenate([small_mix[ROW_CW0:ROW_CW0 + 1], small_mix[ROW_CW0 + 1:ROW_CW0 + 2]], axis=1),
        jnp.concatenate([small_mix[ROW_CW0 + 2:ROW_CW0 + 3], dsink[0:1, :], z(D_MODEL - CONV_W - 128)], axis=1),
    ]
    return jnp.concatenate(rows, axis=0)


def kernel(x, pre_mix_norm, w_in, conv_w, attn_sinks, attn_group_norm, conv_group_norm, w_out, post_mix_norm, pre_mlp_norm, w_up, w_down, post_mlp_norm, loss_target, m_pre_mix_norm, m_w_in, m_conv_w, m_attn_sinks, m_attn_group_norm, m_conv_group_norm, m_w_out, m_post_mix_norm, m_pre_mlp_norm, m_w_up, m_w_down, m_post_mlp_norm, v_pre_mix_norm, v_w_in, v_conv_w, v_attn_sinks, v_attn_group_norm, v_conv_group_norm, v_w_out, v_post_mix_norm, v_pre_mlp_norm, v_w_up, v_w_down, v_post_mlp_norm):
    xi, yi, ci = _mesh_pos()
    chip = 2 * xi + yi
    dev = 2 * chip + ci

    order = _block_order(dev)

    shards = [w_out[0].astype(BF16), w_up[0].astype(BF16), w_down[0].astype(BF16)]

    grad_x, dw_in, dw_out, dw_up, dw_down, smalls = _local_grads(
        x[0], loss_target[0], pre_mix_norm, w_in[0].astype(BF16), conv_w[0], attn_sinks, attn_group_norm, conv_group_norm,
        post_mix_norm, pre_mlp_norm, post_mlp_norm, shards, order)

    small = _sum_devices(_all_gather([_pack_small(*smalls)], "gather_small")[0])
    loss = (0.5 / D_MODEL) * jnp.sum(small[0])

    big = {}
    for name, w, m, v, (own, sib, far) in zip(
            ("w_in", "w_out", "w_up", "w_down"), (w_in, w_out, w_up, w_down), (m_w_in, m_w_out, m_w_up, m_w_down),
            (v_w_in, v_w_out, v_w_up, v_w_down), (dw_in, dw_out, dw_up, dw_down)):
        big[name] = [a[None] for a in _adamw_reduced(w[0], m[0], v[0], own, sib, far, ADAM_ROWS)]

    conv_g = lax.dynamic_slice(
        jnp.stack([small[6, :CONV_W], small[6, CONV_W:], small[7, :CONV_W]]), (0, dev * (CONV_W // N_DEV)),
        (3, CONV_W // N_DEV))
    pad = lambda a, n: jnp.pad(a.reshape(1, -1), ((0, 0), (0, n - a.size)))
    small_names = ("pre_mix_norm", "post_mix_norm", "pre_mlp_norm", "post_mlp_norm")
    small_w = {"pre_mix_norm": (pre_mix_norm, m_pre_mix_norm, v_pre_mix_norm),
               "post_mix_norm": (post_mix_norm, m_post_mix_norm, v_post_mix_norm),
               "pre_mlp_norm": (pre_mlp_norm, m_pre_mlp_norm, v_pre_mlp_norm),
               "post_mlp_norm": (post_mlp_norm, m_post_mlp_norm, v_post_mlp_norm)}

    def pack(k):
        rows = [small_w[nm][k] for nm in small_names]
        rows.append(jnp.concatenate([(attn_group_norm, m_attn_group_norm, v_attn_group_norm)[k],
                                     (conv_group_norm, m_conv_group_norm, v_conv_group_norm)[k]], axis=1))
        rows.append(pad((conv_w, m_conv_w, v_conv_w)[k], D_MODEL))
        rows.append(pad((attn_sinks, m_attn_sinks, v_attn_sinks)[k], D_MODEL))
        rows.append(jnp.zeros((1, D_MODEL), F32))
        return jnp.concatenate(rows, axis=0)

    g_small = jnp.concatenate(
        [small[1:6], pad(conv_g, D_MODEL), pad(small[7, CONV_W:CONV_W + N_HEADS], D_MODEL), jnp.zeros((1, D_MODEL), F32)],
        axis=0)
    d_small, nm_small, nv_small = _adamw_small(pack(0), g_small, pack(1), pack(2))

    def unpack(a):
        nconv = 3 * CONV_W // N_DEV
        return {"pre_mix_norm": a[0:1], "post_mix_norm": a[1:2], "pre_mlp_norm": a[2:3], "post_mlp_norm": a[3:4],
                "attn_group_norm": a[4:5, :ATTN_W], "conv_group_norm": a[4:5, ATTN_W:],
                "conv_w": a[5, :nconv].reshape(1, 3, CONV_W // N_DEV), "attn_sinks": a[6:7, :N_HEADS]}

    order = ("pre_mix_norm", "w_in", "conv_w", "attn_sinks", "attn_group_norm", "conv_group_norm", "w_out",
             "post_mix_norm", "pre_mlp_norm", "w_up", "w_down", "post_mlp_norm")
    outs = []
    for k, a in enumerate((g_small, d_small, nm_small, nv_small)):
        sm = unpack(a)
        outs += [big[nm][k] if nm in big else sm[nm] for nm in order]
    return (loss, grad_x[None], *outs)
```

```python
import functools

import jax
import jax.numpy as jnp
import numpy as np
from jax import lax
from jax.experimental import pallas as pl
from jax.experimental.pallas import tpu as pltpu

F32 = jnp.float32
BF16 = jnp.bfloat16

D_MODEL = 1024
HEAD_DIM = 64
ATTN_W = 512
CONV_W = 512
N_HEADS = 8
N_KV = 2
GROUP = 4
KV_W = 128
QKV_W = ATTN_W + 2 * KV_W
GATES_W = 3 * CONV_W
IN_COLS = QKV_W + GATES_W
D_FF = 4096
FF_CHUNK = 512
N_FF_CHUNKS = D_FF // FF_CHUNK
BLOCK = 128
ROT_HALF = 8
ROPE_THETA = 500000.0
NORM_EPS = 1e-6
NEG_INF = -1e30
ATTN_SCALE = 0.125
N_DEV = 8
N_CHIPS = 4
IN_SHARD = IN_COLS // N_DEV

ADAM_LR = 0.001
ADAM_B1 = 0.9
ADAM_B2 = 0.999
ADAM_EPS = 1e-08
ADAM_WD = 0.01
ADAM_STEP = 10

V7X_VMEM_BYTES = 64 * 1024 * 1024
VMEM_LIMIT = V7X_VMEM_BYTES - 2 * 1024 * 1024

MESH = pl.DeviceIdType.MESH
HBM_SPEC = pl.BlockSpec(memory_space=pltpu.HBM)


def _params(*sem, barrier_id=None):
    return pltpu.CompilerParams(dimension_semantics=sem or None, vmem_limit_bytes=VMEM_LIMIT, collective_id=barrier_id)


def _mm(a, b):
    return jnp.dot(a, b, preferred_element_type=F32)


def _mm_nt(a, b):
    return lax.dot_general(a, b, (((1,), (1,)), ((), ())), preferred_element_type=F32)


def _mm_tn(a, b):
    return lax.dot_general(a, b, (((0,), (0,)), ((), ())), preferred_element_type=F32)


def _inv_rms(x):
    return lax.rsqrt(jnp.mean(x * x, axis=-1, keepdims=True) + NORM_EPS)


def _rms_bwd(xhat, r, gain, dy):
    gy = dy * gain
    return r * (gy - xhat * jnp.mean(gy * xhat, axis=-1, keepdims=True)), dy * xhat


def _colsum(a):
    return jnp.sum(a, axis=0, keepdims=True)


def _full(shape):
    zeros = (0,) * len(shape)
    return pl.BlockSpec(shape, lambda *_: zeros)


def _resident(shape):
    zeros = (0,) * len(shape)
    return pl.BlockSpec(shape, lambda *_: zeros, pipeline_mode=pl.Buffered(1))


def _rope_tables(t):
    pos = np.arange(t, dtype=np.float32)
    inv_freq = (ROPE_THETA ** (-np.arange(0, 2 * ROT_HALF, 2, dtype=np.float64) / (2 * ROT_HALF))).astype(np.float32)
    ang = (pos[:, None] * inv_freq[None, :]).astype(np.float64)
    cos, sin = np.cos(ang).astype(np.float32), np.sin(ang).astype(np.float32)
    zeros8 = np.zeros((t, ROT_HALF), np.float32)
    rest = np.zeros((t, HEAD_DIM - 2 * ROT_HALF), np.float32)
    c_head = np.concatenate([cos, cos, rest + 1.0], axis=1)
    s1_head = np.concatenate([zeros8, sin, rest], axis=1)
    s2_head = np.concatenate([-sin, zeros8, rest], axis=1)
    two = lambda a: jnp.asarray(np.concatenate([a, a], axis=1))
    return two(c_head), two(s1_head), two(s2_head)


def _rope(v, c, s1, s2):
    return v * c + pltpu.roll(v, ROT_HALF, 1) * s1 + pltpu.roll(v, 128 - ROT_HALF, 1) * s2


def _rope_transpose(dv, c, s1, s2):
    return dv * c + pltpu.roll(dv * s1, 128 - ROT_HALF, 1) + pltpu.roll(dv * s2, ROT_HALF, 1)


def _shift_rows_down(u, prev, k):
    row = lax.broadcasted_iota(jnp.int32, u.shape, 0)
    out = pltpu.roll(u, k, 0)
    for r in range(k):
        out = jnp.where(row == r, prev[8 - k + r:8 - k + r + 1, :], out)
    return out


def _shift_rows_up(u, nxt, k):
    n = u.shape[0]
    row = lax.broadcasted_iota(jnp.int32, u.shape, 0)
    out = pltpu.roll(u, n - k, 0)
    for r in range(k):
        out = jnp.where(row == n - k + r, nxt[r:r + 1, :], out)
    return out


def _conv3(u, u1, u2, w):
    return (w[0:1, :] * u2 + w[1:2, :] * u1) + w[2:3, :] * u


def _mesh_pos():
    return lax.axis_index("x"), lax.axis_index("y"), lax.axis_index("c")


def _slot(ref, pos):
    dev = 4 * pos[0] + 2 * pos[1] + pos[2]
    if len(ref.shape) == 2:
        width = ref.shape[1] // N_DEV
        return ref.at[:, pl.ds(pl.multiple_of(dev * width, width), width)]
    return ref.at[dev]


def _gathered_shape(shard, by_cols):
    if by_cols:
        return jax.ShapeDtypeStruct((shard.shape[0], N_DEV * shard.shape[1]), shard.dtype)
    return jax.ShapeDtypeStruct((N_DEV,) + shard.shape, shard.dtype)


def _enter_with(peers):
    barrier = pltpu.get_barrier_semaphore()
    for peer in peers:
        pl.semaphore_signal(barrier, inc=1, device_id=peer, device_id_type=MESH)
    pl.semaphore_wait(barrier, len(peers))


def _sibling_and_chips(x, y, c):
    return [(x, y, 1 - c), (1 - x, y, c), (x, 1 - y, c), (1 - x, 1 - y, c)]


def _push(src, dst, sems, k, to):
    send_sems, recv_sems = sems
    return pltpu.make_async_remote_copy(src_ref=src, dst_ref=dst, send_sem=send_sems.at[k], recv_sem=recv_sems.at[k],
                                        device_id=to, device_id_type=MESH)


def _gather_now(shards, outs, send_sems, recv_sems, local_sems, once_started=None):
    n = len(shards)
    x, y, c = _mesh_pos()
    me, sibling = (x, y, c), (x, y, 1 - c)
    chips = [(1 - x, y), (x, 1 - y), (1 - x, 1 - y)]

    def copy(i, k, block, to, src=None):
        dst = _slot(outs[i], block)
        return _push(dst if src is None else src, dst, (send_sems, recv_sems), 7 * i + k, to)

    mine = [pltpu.make_async_copy(shards[i], _slot(outs[i], me), local_sems.at[i]) for i in range(n)]
    for cp in mine:
        cp.start()
    first = []
    for i in range(n):
        first.append(copy(i, 0, me, sibling, src=shards[i]))
        first += [copy(i, 1 + j, me, (*chip, c), src=shards[i]) for j, chip in enumerate(chips)]
    for cp in first:
        cp.start()
    if once_started is not None:
        once_started()
    passed = []
    for j, chip in enumerate(chips):
        for i in range(n):
            copy(i, 1 + j, (*chip, c), me).wait_recv()
            cp = copy(i, 4 + j, (*chip, c), sibling)
            cp.start()
            passed.append(cp)
    for i in range(n):
        copy(i, 0, sibling, me).wait_recv()
        for j, chip in enumerate(chips):
            copy(i, 4 + j, (*chip, 1 - c), me).wait_recv()
    for cp in first + passed:
        cp.wait_send()
    for cp in mine:
        cp.wait()


def _gather_near(first, last, shards, outs, sems, local_sems):
    x, y, c = _mesh_pos()
    me, peers = (x, y, c), [(x, y, 1 - c), (1 - x, y, c), (x, 1 - y, c)]
    n = len(shards)
    local = [pltpu.make_async_copy(shards[i], _slot(outs[i], me), local_sems.at[i]) for i in range(n)]
    sends = [_push(shards[i], _slot(outs[i], me), sems, 3 * i + k, peers[k]) for i in range(n) for k in range(3)]
    arrivals = [_push(shards[i], _slot(outs[i], peers[k]), sems, 3 * i + k, peers[k]) for i in range(n) for k in range(3)]

    def start():
        for cp in local + sends:
            cp.start()

    if first is not None:
        pl.when(first)(start)

    @pl.when(last)
    def _():
        for cp in sends:
            cp.wait_send()
        for cp in arrivals:
            cp.wait_recv()
        for cp in local:
            cp.wait()

    return start


def _gather_far(first, last, shards, ins, outs, sems):
    x, y, c = _mesh_pos()
    me, sibling = (x, y, c), (x, y, 1 - c)
    chips = [(1 - x, y), (x, 1 - y), (1 - x, 1 - y)]
    n = len(shards)
    diag_send = [_push(shards[i], _slot(outs[i], me), sems, 4 * i, (*chips[2], c)) for i in range(n)]
    diag_arrival = [_push(shards[i], _slot(outs[i], (*chips[2], c)), sems, 4 * i, (*chips[2], c)) for i in range(n)]
    passed = [[_push(_slot(ins[i], (*chips[j], c)), _slot(outs[i], (*chips[j], c)), sems, 4 * i + 1 + j, sibling)
               for i in range(n)] for j in range(3)]
    from_sibling = [_push(shards[i], _slot(outs[i], (*chips[j], 1 - c)), sems, 4 * i + 1 + j, sibling)
                    for i in range(n) for j in range(3)]

    @pl.when(first)
    def _():
        for cp in diag_send + passed[0] + passed[1]:
            cp.start()

    @pl.when(last)
    def _():
        for cp in diag_arrival:
            cp.wait_recv()
        for cp in passed[2]:
            cp.start()
        for cp in from_sibling:
            cp.wait_recv()
        for cp in diag_send + passed[0] + passed[1] + passed[2]:
            cp.wait_send()


def _in_proj_fwd(x, g1, w_in, conv_w, g_conv, rope, tm, shards, by_cols):
    t = x.shape[0]
    rc, rs1, rs2 = rope
    n = len(shards)

    def body(*refs):
        x_ref, g1_ref, w_ref, cw_ref, gc_ref, c_ref, s1_ref, s2_ref = refs[:8]
        shard_refs = refs[8:8 + n]
        qkv_ref, gates_ref, mconv_ref, w_full_ref, cw_full_ref = refs[8 + n:13 + n]
        gathered = refs[13 + n:13 + 2 * n]
        carry_ref, w_land, cw_land = refs[13 + 2 * n:16 + 2 * n]
        now_sems = refs[16 + 2 * n:19 + 2 * n]
        step = pl.program_id(0)
        start_later_weights = _gather_near(None, step == pl.num_programs(0) - 1, shard_refs, gathered,
                                           refs[19 + 2 * n:21 + 2 * n], refs[21 + 2 * n]) if n else None

        @pl.when(step == 0)
        def _():
            carry_ref[...] = jnp.zeros_like(carry_ref)
            _enter_with(_sibling_and_chips(*_mesh_pos()))
            _gather_now([w_ref, cw_ref], [w_land, cw_land], *now_sems, once_started=start_later_weights)
            conv_shard = CONV_W // N_DEV
            for d in range(N_DEV):
                w_full_ref[:, IN_SHARD * d:IN_SHARD * (d + 1)] = w_land[d]
                cw_full_ref[:, conv_shard * d:conv_shard * (d + 1)] = cw_land[d]

        xv = x_ref[...]
        hn = ((xv * _inv_rms(xv)) * g1_ref[...]).astype(BF16)
        proj = _mm(hn, w_full_ref[...])
        c, s1, s2 = c_ref[...], s1_ref[...], s2_ref[...]
        for ci in range((ATTN_W + KV_W) // 128):
            sl = slice(128 * ci, 128 * (ci + 1))
            qkv_ref[:, sl] = _rope(proj[:, sl], c, s1, s2).astype(BF16)
        qkv_ref[:, ATTN_W + KV_W:QKV_W] = proj[:, ATTN_W + KV_W:QKV_W].astype(BF16)
        gates = proj[:, QKV_W:]
        gates_ref[...] = gates
        gb, gcc, xin = gates[:, :CONV_W], gates[:, CONV_W:2 * CONV_W], gates[:, 2 * CONV_W:]
        u = gcc * xin
        prev = carry_ref[...]
        conv = gb * _conv3(u, _shift_rows_down(u, prev, 1), _shift_rows_down(u, prev, 2), cw_full_ref[...])
        carry_ref[...] = u[tm - 8:tm, :]
        mconv_ref[...] = ((conv * _inv_rms(conv)) * gc_ref[...]).astype(BF16)

    tile = lambda w_: pl.BlockSpec((tm, w_), lambda i: (i, 0))
    sems = lambda k: pltpu.SemaphoreType.DMA((k,))
    res = pl.pallas_call(
        body, name="in_proj_fwd", grid=(t // tm,),
        in_specs=[tile(D_MODEL), _full((1, D_MODEL)), HBM_SPEC, HBM_SPEC, _full((1, CONV_W)), tile(128), tile(128),
                  tile(128)] + [HBM_SPEC] * n,
        out_specs=[tile(QKV_W), tile(GATES_W), tile(CONV_W), _full((D_MODEL, IN_COLS)), _full((3, CONV_W))]
        + [HBM_SPEC] * n,
        out_shape=[jax.ShapeDtypeStruct((t, QKV_W), BF16), jax.ShapeDtypeStruct((t, GATES_W), F32),
                   jax.ShapeDtypeStruct((t, CONV_W), BF16), jax.ShapeDtypeStruct((D_MODEL, IN_COLS), BF16),
                   jax.ShapeDtypeStruct((3, CONV_W), F32)]
        + [_gathered_shape(s, cols) for s, cols in zip(shards, by_cols)],
        scratch_shapes=[pltpu.VMEM((8, CONV_W), F32), pltpu.VMEM((N_DEV,) + w_in.shape, BF16),
                        pltpu.VMEM((N_DEV,) + conv_w.shape, F32), sems(14), sems(14), sems(2)]
        + ([sems(3 * n), sems(3 * n), sems(n)] if n else []),
        compiler_params=_params("arbitrary", barrier_id=0),
    )(x, g1, w_in, conv_w, g_conv, rc, rs1, rs2, *shards)
    return res[0], res[1], res[2], res[3], res[4], list(res[5:])


GROUP_COLS = GROUP * BLOCK
ATTN_STEP_BLOCKS = 4


def _attn_mask(has_prev):
    key = lax.broadcasted_iota(jnp.int32, (2 * BLOCK, GROUP_COLS), 0)
    query = lax.broadcasted_iota(jnp.int32, (2 * BLOCK, GROUP_COLS), 1) & (BLOCK - 1)
    band = (key > query) & (key <= query + BLOCK)
    return band if has_prev is True else band & ((key >= BLOCK) | has_prev)


def _heads_side_by_side(at, g, b):
    heads = [at[HEAD_DIM * (GROUP * g + hh):HEAD_DIM * (GROUP * g + hh + 1), BLOCK * b:BLOCK * (b + 1)] for hh in range(GROUP)]
    return jnp.concatenate(heads, axis=1)


def _to_token_rows(parts):
    rows = [jnp.concatenate([parts[b][g][:, BLOCK * hh:BLOCK * (hh + 1)] for b in range(ATTN_STEP_BLOCKS)], axis=1)
            for g in range(N_KV) for hh in range(GROUP)]
    return jnp.concatenate(rows, axis=0).T


def _group_sinks(sink_ref, g):
    head = lax.broadcasted_iota(jnp.int32, (1, GROUP_COLS), 1) // BLOCK
    out = jnp.full((1, GROUP_COLS), sink_ref[0, GROUP * g], F32)
    for hh in range(1, GROUP):
        out = jnp.where(head == hh, sink_ref[0, GROUP * g + hh], out)
    return out


def _attn_probs(qt, kk, sink, valid):
    s = jnp.where(valid, _mm(kk, qt) * ATTN_SCALE, NEG_INF)
    m = jnp.maximum(jnp.max(s, axis=0, keepdims=True), sink)
    p = jnp.exp(s - m)
    psink = jnp.exp(sink - m)
    inv_l = 1.0 / (jnp.sum(p, axis=0, keepdims=True) + psink)
    return p * inv_l, psink * inv_l


ATTN_STEP = ATTN_STEP_BLOCKS * BLOCK
ATTN_KEYS = ATTN_STEP + BLOCK


def _qkv_specs(order):
    prev = lambda i: jnp.maximum(ATTN_STEP_BLOCKS * order(i) - 1, 0)
    kcol, vcol = ATTN_W // KV_W, ATTN_W // KV_W + 1
    return [pl.BlockSpec((ATTN_STEP, ATTN_W), lambda i: (order(i), 0)),
            pl.BlockSpec((BLOCK, KV_W), lambda i: (prev(i), kcol)), pl.BlockSpec((ATTN_STEP, KV_W), lambda i: (order(i), kcol)),
            pl.BlockSpec((BLOCK, KV_W), lambda i: (prev(i), vcol)), pl.BlockSpec((ATTN_STEP, KV_W), lambda i: (order(i), vcol))]


def _attn_fwd(qkv, sinks, g_attn, shards, gathered):
    t = qkv.shape[0]
    n = len(shards)

    def body(*refs):
        sink_ref, q_ref, kp_ref, kc_ref, vp_ref, vc_ref, ga_ref = refs[:7]
        attn_ref, mattn_ref = refs[7 + 2 * n:9 + 2 * n]
        step = pl.program_id(0)
        if n:
            @pl.when(step == 0)
            def _():
                x, y, c = _mesh_pos()
                _enter_with([(x, y, 1 - c), (1 - x, 1 - y, c)])

            _gather_far(step == 0, step == pl.num_programs(0) - 1, refs[7:7 + n], refs[7 + n:7 + 2 * n],
                        refs[9 + 2 * n:9 + 3 * n], refs[9 + 3 * n:11 + 3 * n])
        qt = q_ref[...].T
        keys = jnp.concatenate([kp_ref[...], kc_ref[...]], axis=0)
        vals = jnp.concatenate([vp_ref[...], vc_ref[...]], axis=0)
        sink = [_group_sinks(sink_ref, g) for g in range(N_KV)]
        parts = []
        for b in range(ATTN_STEP_BLOCKS):
            window = slice(BLOCK * b, BLOCK * (b + 2))
            valid = _attn_mask(True if b else step > 0)
            parts.append([])
            for g in range(N_KV):
                gs = slice(HEAD_DIM * g, HEAD_DIM * (g + 1))
                probs, _ = _attn_probs(_heads_side_by_side(qt, g, b), keys[window, gs], sink[g], valid)
                parts[b].append(_mm_tn(vals[window, gs], probs.astype(BF16)))
        attn = _to_token_rows(parts)
        attn_ref[...] = attn
        mattn_ref[...] = ((attn * _inv_rms(attn)) * ga_ref[...]).astype(BF16)

    blk = pl.BlockSpec((ATTN_STEP, ATTN_W), lambda j: (j, 0))
    res = pl.pallas_call(
        body, name="attn_fwd", grid=(t // ATTN_STEP,),
        in_specs=[pl.BlockSpec(memory_space=pltpu.SMEM)] + _qkv_specs(lambda j: j) + [_full((1, ATTN_W))]
        + [HBM_SPEC] * (2 * n),
        out_specs=[blk, blk] + [HBM_SPEC] * n,
        out_shape=[jax.ShapeDtypeStruct((t, ATTN_W), F32), jax.ShapeDtypeStruct((t, ATTN_W), BF16)]
        + [jax.ShapeDtypeStruct(g.shape, g.dtype) for g in gathered],
        input_output_aliases={7 + n + i: 2 + i for i in range(n)},
        scratch_shapes=[pltpu.SemaphoreType.DMA((4 * n,)), pltpu.SemaphoreType.DMA((4 * n,))] if n else [],
        compiler_params=_params("arbitrary", barrier_id=1 if n else None),
    )(sinks, qkv, qkv, qkv, qkv, qkv, g_attn, *shards, *gathered)
    return res[0], res[1], list(res[2:])


SMALL_ROWS = 8
ROW_LOSS, ROW_G2, ROW_G3, ROW_G4 = 0, 1, 2, 3


def _mid(mattn, mconv, x, target, g2, g3, g4, w_out, w_up, w_down, tm):
    t = x.shape[0]

    def body(ma_ref, mc_ref, x_ref, t_ref, g2_ref, g3_ref, g4_ref, wo_ref, wu_ref, wd_ref,
             actt_ref, dup_ref, hn2t_ref, dmo_ref, dmix_ref, dh_ref, dmixed_ref, small_ref, up_ref):
        @pl.when(pl.program_id(0) == 0)
        def _():
            small_ref[...] = jnp.zeros_like(small_ref)

        g2, g3, g4 = g2_ref[...], g3_ref[...], g4_ref[...]
        mix_out = _mm(ma_ref[...], wo_ref[0:ATTN_W, :]) + _mm(mc_ref[...], wo_ref[ATTN_W:, :])
        r2 = _inv_rms(mix_out)
        mo_hat = mix_out * r2
        h = x_ref[...] + mo_hat * g2
        r3 = _inv_rms(h)
        h_hat = h * r3
        hn2 = (h_hat * g3).astype(BF16)
        hn2t_ref[...] = hn2.T
        up = jnp.maximum(_mm(hn2, wu_ref[...]), 0.0)
        up_ref[...] = up.astype(BF16)
        act = (up * up).astype(BF16)
        actt_ref[...] = act.T
        mlp = _mm(act, wd_ref[...])
        r4 = _inv_rms(mlp)
        ml_hat = mlp * r4
        err = (h + ml_hat * g4) - t_ref[...]
        d_out = err * (1.0 / D_MODEL)
        d_mlp, dg4 = _rms_bwd(ml_hat, r4, g4, d_out)
        dmo = d_mlp.astype(BF16)
        dmo_ref[...] = dmo
        dup = (_mm_nt(dmo, wd_ref[...]) * (2.0 * up_ref[...].astype(F32))).astype(BF16)
        dup_ref[...] = dup
        dhn2 = _mm_nt(dup, wu_ref[...])
        dh_norm, dg3 = _rms_bwd(h_hat, r3, g3, dhn2)
        dh = d_out + dh_norm
        dh_ref[...] = dh
        d_mix, dg2 = _rms_bwd(mo_hat, r2, g2, dh)
        dmix = d_mix.astype(BF16)
        dmix_ref[...] = dmix
        dmixed_ref[...] = _mm_nt(dmix, wo_ref[...])
        small_ref[ROW_LOSS:ROW_LOSS + 1, :] += _colsum(err * err)
        small_ref[ROW_G2:ROW_G2 + 1, :] += _colsum(dg2)
        small_ref[ROW_G3:ROW_G3 + 1, :] += _colsum(dg3)
        small_ref[ROW_G4:ROW_G4 + 1, :] += _colsum(dg4)

    tile = lambda n: pl.BlockSpec((tm, n), lambda i: (i, 0))
    cols = lambda n: pl.BlockSpec((n, tm), lambda i: (0, i))
    gain = _full((1, D_MODEL))
    return pl.pallas_call(
        body, name="mid_fwd_bwd", grid=(t // tm,),
        in_specs=[tile(ATTN_W), tile(CONV_W), tile(D_MODEL), tile(D_MODEL), gain, gain, gain,
                  _resident((D_MODEL, D_MODEL)), _resident((D_MODEL, D_FF)), _resident((D_FF, D_MODEL))],
        out_specs=[cols(D_FF), tile(D_FF), cols(D_MODEL), tile(D_MODEL), tile(D_MODEL), tile(D_MODEL), tile(D_MODEL),
                   _full((SMALL_ROWS, D_MODEL))],
        out_shape=[jax.ShapeDtypeStruct((D_FF, t), BF16), jax.ShapeDtypeStruct((t, D_FF), BF16),
                   jax.ShapeDtypeStruct((D_MODEL, t), BF16), jax.ShapeDtypeStruct((t, D_MODEL), BF16),
                   jax.ShapeDtypeStruct((t, D_MODEL), BF16), jax.ShapeDtypeStruct((t, D_MODEL), F32),
                   jax.ShapeDtypeStruct((t, D_MODEL), F32), jax.ShapeDtypeStruct((SMALL_ROWS, D_MODEL), F32)],
        scratch_shapes=[pltpu.VMEM((tm, D_FF), BF16)],
        compiler_params=_params("arbitrary"),
    )(mattn, mconv, x, target, g2, g3, g4, w_out, w_up, w_down)


CHIP_FLIPS = ((1, 1), (1, 0), (0, 1))


def _block_order(dev):
    chip_masks = [4 * fx + 2 * fy for fx, fy in CHIP_FLIPS]
    masks = [m + 1 for m in chip_masks] + [1] + chip_masks + [0]
    return jnp.bitwise_xor(dev, jnp.asarray(masks, jnp.int32)).astype(jnp.int32)


def _other_chips(x, y, c):
    return [(1 - x if fx else x, 1 - y if fy else y, c) for fx, fy in CHIP_FLIPS]


def _dw_pair_sums(operands, order, which, name, barrier_id):
    t = operands[-1].shape[0]
    n_far = len(CHIP_FLIPS)
    n_in = len(operands)
    out_chunk = D_MODEL // N_DEV
    if which == "up":
        rows, cols = D_MODEL, FF_CHUNK
        in_specs = [_resident((D_MODEL, t)), pl.BlockSpec((t, FF_CHUNK), lambda s, order_ref: (0, order_ref[s]))]
    elif which == "down":
        rows, cols = FF_CHUNK, D_MODEL
        in_specs = [pl.BlockSpec((FF_CHUNK, t), lambda s, order_ref: (order_ref[s], 0)), _resident((t, D_MODEL))]
    else:
        rows, cols = out_chunk, D_MODEL
        half = pl.BlockSpec((t, out_chunk), lambda s, order_ref: (0, order_ref[s] % (N_DEV // 2)))
        in_specs = [half, half, _resident((t, D_MODEL))]

    def body(order_ref, *refs):
        own_ref, from_sib_ref, pair_ref, send_buf, land_buf, send_sems, recv_sems = refs[n_in:]
        s_now = pl.program_id(0)
        x, y, c = _mesh_pos()
        sibling = (x, y, 1 - c)
        sems = (send_sems, recv_sems)

        @pl.when(s_now == 0)
        def _():
            _enter_with([sibling])

        def hand_over(k):
            dst = land_buf.at[k] if k < n_far else from_sib_ref
            return _push(send_buf.at[k], dst, sems, k, sibling)

        if which == "out":
            ma_ref, mc_ref, b_ref = refs[:n_in]
            block = lax.cond(order_ref[s_now] < N_DEV // 2, lambda: _mm_tn(ma_ref[...], b_ref[...]),
                             lambda: _mm_tn(mc_ref[...], b_ref[...]))
        else:
            block = _mm(refs[0][...], refs[1][...])
        for k in range(n_far + 1):
            @pl.when(s_now == k)
            def _():
                send_buf[k] = block.astype(BF16)
                hand_over(k).start()

        for k in range(n_far):
            @pl.when(s_now == n_far + 1 + k)
            def _():
                hand_over(k).wait_recv()
                pair_ref[...] = (block + land_buf[k].astype(F32)).astype(BF16)

        @pl.when(s_now == N_DEV - 1)
        def _():
            own_ref[...] = block
            for k in range(n_far + 1):
                hand_over(k).wait_send()
            hand_over(n_far).wait_recv()

    return pl.pallas_call(
        body, name=name,
        grid_spec=pltpu.PrefetchScalarGridSpec(
            num_scalar_prefetch=1, grid=(N_DEV,), in_specs=in_specs,
            out_specs=[pl.BlockSpec((rows, cols), lambda s, order_ref: (0, 0)), HBM_SPEC,
                       pl.BlockSpec((None, rows, cols), lambda s, order_ref: (jnp.clip(s - n_far - 1, 0, n_far - 1), 0, 0))],
            scratch_shapes=[pltpu.VMEM((n_far + 1, rows, cols), BF16), pltpu.VMEM((n_far, rows, cols), BF16),
                            pltpu.SemaphoreType.DMA((n_far + 1,)), pltpu.SemaphoreType.DMA((n_far + 1,))]),
        out_shape=[jax.ShapeDtypeStruct((rows, cols), F32), jax.ShapeDtypeStruct((rows, cols), BF16),
                   jax.ShapeDtypeStruct((n_far, rows, cols), BF16)],
        compiler_params=_params("arbitrary", barrier_id=barrier_id),
    )(order, *operands)


def _chip_exchange_beside(first, last, sums, outs, sems, enter=True):
    chips = _other_chips(*_mesh_pos())
    copies = [_push(sums[i].at[k], outs[i].at[k], sems, len(chips) * i + k, chip)
              for i in range(len(sums)) for k, chip in enumerate(chips)]

    @pl.when(first)
    def _():
        if enter:
            _enter_with(chips)
        for cp in copies:
            cp.start()

    @pl.when(last)
    def _():
        for cp in copies:
            cp.wait()


ROW_GATTN, ROW_GCONV, ROW_CW0 = 0, 1, 2


def _mix_bwd(dmixed, attn, gates, g_attn, g_conv, conv_w, tm, sums):
    t = attn.shape[0]
    n = t // tm
    rev = lambda i: n - 1 - i

    def body(dm_ref, attn_ref, gates_ref, gprev_ref, ga_ref, gc_ref, cw_ref, sums_ref, dattn_ref, dgates_ref, small_ref,
             arrived_ref, carry_ref, send_sems, recv_sems):
        i = pl.program_id(0)
        _chip_exchange_beside(i == 0, i == n - 1, [sums_ref], [arrived_ref], (send_sems, recv_sems))

        @pl.when(i == 0)
        def _():
            small_ref[...] = jnp.zeros_like(small_ref)
            carry_ref[...] = jnp.zeros_like(carry_ref)

        dm = dm_ref[...]
        a = attn_ref[...]
        ra = _inv_rms(a)
        a_hat = a * ra
        dattn, dga = _rms_bwd(a_hat, ra, ga_ref[...], dm[:, :ATTN_W])
        dattn_ref[...] = dattn

        gates = gates_ref[...]
        gb, gcc, xin = gates[:, :CONV_W], gates[:, CONV_W:2 * CONV_W], gates[:, 2 * CONV_W:]
        u = gcc * xin
        gp = gprev_ref[...]
        uprev = jnp.where(rev(i) == 0, 0.0, gp[:, CONV_W:2 * CONV_W] * gp[:, 2 * CONV_W:])
        u1, u2 = _shift_rows_down(u, uprev, 1), _shift_rows_down(u, uprev, 2)
        w = cw_ref[...]
        c = _conv3(u, u1, u2, w)
        conv = gb * c
        rcv = _inv_rms(conv)
        c_hat = conv * rcv
        dconv, dgc = _rms_bwd(c_hat, rcv, gc_ref[...], dm[:, ATTN_W:])
        dc = dconv * gb
        nxt = carry_ref[...]
        du = (w[2:3, :] * dc + w[1:2, :] * _shift_rows_up(dc, nxt, 1)) + w[0:1, :] * _shift_rows_up(dc, nxt, 2)
        carry_ref[...] = dc[0:8, :]
        dgates_ref[:, :CONV_W] = (dconv * c).astype(BF16)
        dgates_ref[:, CONV_W:2 * CONV_W] = (du * xin).astype(BF16)
        dgates_ref[:, 2 * CONV_W:] = (du * gcc).astype(BF16)
        small_ref[ROW_GATTN:ROW_GATTN + 1, :] += _colsum(dga)
        small_ref[ROW_GCONV:ROW_GCONV + 1, :] += _colsum(dgc)
        small_ref[ROW_CW0:ROW_CW0 + 1, :] += _colsum(dc * u2)
        small_ref[ROW_CW0 + 1:ROW_CW0 + 2, :] += _colsum(dc * u1)
        small_ref[ROW_CW0 + 2:ROW_CW0 + 3, :] += _colsum(dc * u)

    tile = lambda w_: pl.BlockSpec((tm, w_), lambda i: (rev(i), 0))
    prev8 = pl.BlockSpec((8, GATES_W), lambda i: (jnp.maximum(rev(i) * (tm // 8) - 1, 0), 0))
    return pl.pallas_call(
        body, name="mix_bwd", grid=(n,),
        in_specs=[tile(D_MODEL), tile(ATTN_W), tile(GATES_W), prev8, _full((1, ATTN_W)), _full((1, CONV_W)),
                  _full((3, CONV_W)), HBM_SPEC],
        out_specs=[tile(ATTN_W), tile(GATES_W), _full((SMALL_ROWS, CONV_W)), HBM_SPEC],
        out_shape=[jax.ShapeDtypeStruct((t, ATTN_W), F32), jax.ShapeDtypeStruct((t, GATES_W), BF16),
                   jax.ShapeDtypeStruct((SMALL_ROWS, CONV_W), F32), jax.ShapeDtypeStruct(sums.shape, sums.dtype)],
        scratch_shapes=[pltpu.VMEM((8, CONV_W), F32), pltpu.SemaphoreType.DMA((len(CHIP_FLIPS),)),
                        pltpu.SemaphoreType.DMA((len(CHIP_FLIPS),))],
        compiler_params=_params("arbitrary", barrier_id=5),
    )(dmixed, attn, gates, gates, g_attn, g_conv, conv_w, sums)


def _attn_bwd(qkv, dattn, sinks, rope, sums):
    t = qkv.shape[0]
    n_steps = t // ATTN_STEP
    rev = lambda i: n_steps - 1 - i
    rc, rs1, rs2 = rope

    def body(sink_ref, q_ref, kp_ref, kc_ref, vp_ref, vc_ref, do_ref, c_ref, s1_ref, s2_ref, sums_ref,
             dqkv_ref, dsink_ref, arrived_ref, ck_ref, cv_ref, kacc_ref, vacc_ref, send_sems, recv_sems):
        i = pl.program_id(0)
        _chip_exchange_beside(i == 0, i == n_steps - 1, [sums_ref], [arrived_ref], (send_sems, recv_sems))

        @pl.when(i == 0)
        def _():
            dsink_ref[...] = jnp.zeros_like(dsink_ref)
            ck_ref[...] = jnp.zeros_like(ck_ref)
            cv_ref[...] = jnp.zeros_like(cv_ref)

        kacc_ref[...] = jnp.zeros_like(kacc_ref)
        vacc_ref[...] = jnp.zeros_like(vacc_ref)
        qt = q_ref[...].T
        dot = do_ref[...].astype(BF16).T
        keys = jnp.concatenate([kp_ref[...], kc_ref[...]], axis=0)
        vals = jnp.concatenate([vp_ref[...], vc_ref[...]], axis=0)
        sink = [_group_sinks(sink_ref, g) for g in range(N_KV)]
        c, s1, s2 = c_ref[...], s1_ref[...], s2_ref[...]
        lane = lax.broadcasted_iota(jnp.int32, (1, 128), 1)
        dsink = jnp.zeros((1, 128), F32)
        dq_parts = []
        for b in range(ATTN_STEP_BLOCKS):
            window = slice(BLOCK * b, BLOCK * (b + 2))
            valid = _attn_mask(True if b else rev(i) > 0)
            dq_parts.append([])
            dk_parts, dv_parts = [], []
            for g in range(N_KV):
                gs = slice(HEAD_DIM * g, HEAD_DIM * (g + 1))
                kk, vv = keys[window, gs], vals[window, gs]
                qtg, dotg = _heads_side_by_side(qt, g, b), _heads_side_by_side(dot, g, b)
                probs, psink = _attn_probs(qtg, kk, sink[g], valid)
                dp = _mm(vv, dotg)
                delta = jnp.sum(probs * dp, axis=0, keepdims=True)
                ds = (probs * (dp - delta) * ATTN_SCALE).astype(BF16)
                sink_terms = psink * delta
                for hh in range(GROUP):
                    head_sum = jnp.sum(sink_terms[:, BLOCK * hh:BLOCK * (hh + 1)])
                    dsink = dsink + jnp.where(lane == GROUP * g + hh, -head_sum, 0.0)
                dq_parts[b].append(_mm_tn(kk, ds))
                dk_parts.append(_mm_nt(ds, qtg))
                dv_parts.append(_mm_nt(probs.astype(BF16), dotg))
            kacc_ref[window, :] += jnp.concatenate(dk_parts, axis=1)
            vacc_ref[window, :] += jnp.concatenate(dv_parts, axis=1)
        dq = _to_token_rows(dq_parts)
        for ci in range(ATTN_W // 128):
            sl = slice(128 * ci, 128 * (ci + 1))
            dqkv_ref[:, sl] = _rope_transpose(dq[:, sl], c, s1, s2).astype(BF16)
        kacc_ref[ATTN_STEP:, :] += ck_ref[...]
        vacc_ref[ATTN_STEP:, :] += cv_ref[...]
        ck_ref[...] = kacc_ref[:BLOCK, :]
        cv_ref[...] = vacc_ref[:BLOCK, :]
        dqkv_ref[:, ATTN_W:ATTN_W + KV_W] = _rope_transpose(kacc_ref[BLOCK:, :], c, s1, s2).astype(BF16)
        dqkv_ref[:, ATTN_W + KV_W:] = vacc_ref[BLOCK:, :].astype(BF16)
        dsink_ref[0:1, :] += dsink

    blk = lambda w_: pl.BlockSpec((ATTN_STEP, w_), lambda i: (rev(i), 0))
    return pl.pallas_call(
        body, name="attn_bwd", grid=(n_steps,),
        in_specs=[pl.BlockSpec(memory_space=pltpu.SMEM)] + _qkv_specs(rev) + [blk(ATTN_W), blk(128), blk(128), blk(128),
                                                                              HBM_SPEC],
        out_specs=[blk(QKV_W), _full((8, 128)), HBM_SPEC],
        out_shape=[jax.ShapeDtypeStruct((t, QKV_W), BF16), jax.ShapeDtypeStruct((8, 128), F32),
                   jax.ShapeDtypeStruct(sums.shape, sums.dtype)],
        scratch_shapes=[pltpu.VMEM((BLOCK, KV_W), F32), pltpu.VMEM((BLOCK, KV_W), F32),
                        pltpu.VMEM((ATTN_KEYS, KV_W), F32), pltpu.VMEM((ATTN_KEYS, KV_W), F32),
                        pltpu.SemaphoreType.DMA((len(CHIP_FLIPS),)), pltpu.SemaphoreType.DMA((len(CHIP_FLIPS),))],
        compiler_params=_params("arbitrary", barrier_id=6),
    )(sinks, qkv, qkv, qkv, qkv, qkv, dattn, rc, rs1, rs2, sums)


def _grad_x_tile(dq, dg, x_hat, r, g1, w_ref, dh):
    dhn = _mm_nt(dq, w_ref[:, :QKV_W]) + _mm_nt(dg, w_ref[:, QKV_W:])
    dx, dg1 = _rms_bwd(x_hat, r, g1, dhn)
    return dh + dx, _colsum(dg1)


def _in_proj_bwd(dqkv, dgates, x, dh, g1, w_in, tm, out_sums):
    t = x.shape[0]
    n = t // tm
    n_steps = 2 * n
    n_far = len(CHIP_FLIPS)
    shard = (D_MODEL, IN_SHARD)

    def body(dq_ref, dg_ref, x_ref, dh_ref, g1_ref, w_ref, osums_ref,
             dx_ref, own_ref, sib_ref, far_ref, dg1_ref, oarrived_ref,
             acc_ref, send_buf, land_buf, pair_buf, d2d_send, d2d_recv, ici_send, ici_recv, o_send, o_recv):
        i = pl.program_id(0)
        x_pos, y_pos, c = _mesh_pos()
        my_chip = 2 * x_pos + y_pos
        sibling = (x_pos, y_pos, 1 - c)
        @pl.when(i == 0)
        def _():
            _enter_with(_sibling_and_chips(x_pos, y_pos, c))

        _chip_exchange_beside(i == 0, i == n_steps - 1, [osums_ref], [oarrived_ref], (o_send, o_recv), enter=False)

        def cols(d):
            return slice(IN_SHARD * d, IN_SHARD * (d + 1))

        def hand_over(chip):
            return _push(send_buf.at[chip], land_buf.at[chip], (d2d_send, d2d_recv), chip, sibling)

        def to_chip(chip, rel):
            return pltpu.make_async_remote_copy(
                src_ref=pair_buf.at[chip], dst_ref=far_ref.at[rel - 1], send_sem=ici_send.at[rel - 1],
                recv_sem=ici_recv.at[rel - 1], device_id=(chip // 2, chip % 2, c), device_id_type=MESH)

        @pl.when(i == 0)
        def _():
            acc_ref[...] = jnp.zeros_like(acc_ref)
            dg1_ref[...] = jnp.zeros_like(dg1_ref)

        def normed_x():
            xv = x_ref[...]
            r = _inv_rms(xv)
            return xv * r, r

        @pl.when(i < n)
        def _():
            hn = (normed_x()[0] * g1_ref[...]).astype(BF16)
            acc_ref[:, :QKV_W] += _mm_tn(hn, dq_ref[...])
            acc_ref[:, QKV_W:] += _mm_tn(hn, dg_ref[...])

        @pl.when(i == n - 1)
        def _():
            for d in range(N_DEV):
                @pl.when(d % 2 != c)
                def _():
                    send_buf[d // 2] = acc_ref[:, cols(d)].astype(BF16)
                    hand_over(d // 2).start()
            for d in range(N_DEV):
                chip = d // 2

                @pl.when(d % 2 == c)
                def _():
                    hand_over(chip).wait_recv()

                    @pl.when(chip == my_chip)
                    def _():
                        own_ref[...] = acc_ref[:, cols(d)]
                        sib_ref[...] = land_buf[chip]

                    @pl.when(chip != my_chip)
                    def _():
                        pair_buf[chip] = (acc_ref[:, cols(d)] + land_buf[chip].astype(F32)).astype(BF16)
                        to_chip(chip, chip ^ my_chip).start()
            for chip in range(N_CHIPS):
                hand_over(chip).wait_send()

        @pl.when(i >= n)
        def _():
            x_hat, r = normed_x()
            dx_ref[...], dg1 = _grad_x_tile(dq_ref[...], dg_ref[...], x_hat, r, g1_ref[...], w_ref, dh_ref[...])
            dg1_ref[0:1, :] += dg1

        @pl.when(i == n_steps - 1)
        def _():
            for rel in range(1, n_far + 1):
                to_chip(0, rel).wait()

    both = lambda w_: pl.BlockSpec((tm, w_), lambda i: (i % n, 0))
    second = pl.BlockSpec((tm, D_MODEL), lambda i: (jnp.maximum(i - n, 0), 0))
    whole = lambda dtype: jax.ShapeDtypeStruct(shard, dtype)
    sems = lambda k: pltpu.SemaphoreType.DMA((k,))
    res = pl.pallas_call(
        body, name="in_proj_bwd", grid=(n_steps,),
        in_specs=[both(QKV_W), both(GATES_W), both(D_MODEL), second, _full((1, D_MODEL)), _resident((D_MODEL, IN_COLS)),
                  HBM_SPEC],
        out_specs=[second, _full(shard), _full(shard), HBM_SPEC, _full((SMALL_ROWS, D_MODEL)), HBM_SPEC],
        out_shape=[jax.ShapeDtypeStruct((t, D_MODEL), F32), whole(F32), whole(BF16),
                   jax.ShapeDtypeStruct((n_far,) + shard, BF16), jax.ShapeDtypeStruct((SMALL_ROWS, D_MODEL), F32),
                   jax.ShapeDtypeStruct(out_sums.shape, out_sums.dtype)],
        scratch_shapes=[pltpu.VMEM((D_MODEL, IN_COLS), F32), pltpu.VMEM((N_CHIPS,) + shard, BF16),
                        pltpu.VMEM((N_CHIPS,) + shard, BF16), pltpu.VMEM((N_CHIPS,) + shard, BF16),
                        sems(N_CHIPS), sems(N_CHIPS), sems(n_far), sems(n_far), sems(n_far), sems(n_far)],
        compiler_params=_params("arbitrary", barrier_id=7),
    )(dqkv, dgates, x, dh, g1, w_in, out_sums)
    return res[0], (res[1], res[2], res[3]), res[4], res[5]


def _all_gather(shards, name):
    n = len(shards)

    def body(*refs):
        _enter_with(_sibling_and_chips(*_mesh_pos()))
        _gather_now(refs[:n], refs[n:2 * n], *refs[2 * n:])

    return pl.pallas_call(
        body, name=name,
        in_specs=[HBM_SPEC] * n, out_specs=[HBM_SPEC] * n,
        out_shape=[jax.ShapeDtypeStruct((N_DEV,) + s.shape, s.dtype) for s in shards],
        scratch_shapes=[pltpu.SemaphoreType.DMA((7 * n,)), pltpu.SemaphoreType.DMA((7 * n,)),
                        pltpu.SemaphoreType.DMA((n,))],
        compiler_params=_params(barrier_id=8),
    )(*shards)


def _adam_math(w, g, m, v):
    m = ADAM_B1 * m + (1.0 - ADAM_B1) * g
    v = ADAM_B2 * v + (1.0 - ADAM_B2) * (g * g)
    m_hat = m / (1.0 - ADAM_B1 ** ADAM_STEP)
    v_hat = v / (1.0 - ADAM_B2 ** ADAM_STEP)
    delta = -ADAM_LR * (m_hat / (jnp.sqrt(v_hat) + ADAM_EPS) + ADAM_WD * w)
    return delta, m, v


def _adamw_reduced(w, m, v, own, from_sibling, from_chips, tr):
    rows, cols = w.shape

    def body(w_ref, m_ref, v_ref, own_ref, sib_ref, far_ref, g_ref, d_ref, nm_ref, nv_ref):
        g = own_ref[...] + sib_ref[...].astype(F32)
        for k in range(len(CHIP_FLIPS)):
            g = g + far_ref[k].astype(F32)
        g_ref[...] = g
        d_ref[...], nm_ref[...], nv_ref[...] = _adam_math(w_ref[...], g, m_ref[...], v_ref[...])

    tile = pl.BlockSpec((tr, cols), lambda i: (i, 0))
    out = jax.ShapeDtypeStruct((rows, cols), F32)
    return pl.pallas_call(
        body, name="adamw_reduced", grid=(rows // tr,),
        in_specs=[tile] * 5 + [pl.BlockSpec((len(CHIP_FLIPS), tr, cols), lambda i: (0, i, 0))],
        out_specs=[tile] * 4, out_shape=[out] * 4,
        compiler_params=_params("parallel"),
    )(w, m, v, own, from_sibling, from_chips)


def _sum_devices(gathered):
    _, rows, cols = gathered.shape

    def body(g_ref, o_ref):
        s = g_ref[0]
        for d in range(1, N_DEV):
            s = s + g_ref[d]
        o_ref[...] = s

    return pl.pallas_call(
        body, name="sum_devices", in_specs=[_full(gathered.shape)], out_specs=_full((rows, cols)), grid=(1,),
        out_shape=jax.ShapeDtypeStruct((rows, cols), F32),
    )(gathered)


def _adamw_small(w, g, m, v):
    def body(w_ref, g_ref, m_ref, v_ref, d_ref, nm_ref, nv_ref):
        d_ref[...], nm_ref[...], nv_ref[...] = _adam_math(w_ref[...], g_ref[...], m_ref[...], v_ref[...])

    spec = _full(w.shape)
    out = jax.ShapeDtypeStruct(w.shape, F32)
    return pl.pallas_call(
        body, name="adamw_small", grid=(1,), in_specs=[spec] * 4, out_specs=[spec] * 3, out_shape=[out] * 3,
    )(w, g, m, v)


TOKEN_TILE = 512
MID_TILE = 256
ADAM_ROWS = 128


def _local_grads(x, target, g1, w_in_shard, conv_shard, sinks, g_attn, g_conv, g2, g3, g4, shards, order):
    t = x.shape[0]
    tm = min(TOKEN_TILE, t)
    rope = _rope_tables(t)
    qkv, gates, mconv, w_in, conv_w, gathered = _in_proj_fwd(x, g1, w_in_shard, conv_shard, g_conv, rope, tm, shards,
                                                             (False, True, False))
    attn, mattn, (w_out, w_up, w_down) = _attn_fwd(qkv, sinks, g_attn, shards, gathered)
    actt, dup, hn2t, dmo, dmix, dh, dmixed, small_mid = _mid(
        mattn, mconv, x, target, g2, g3, g4, w_out.reshape(D_MODEL, D_MODEL),
        w_up, w_down.reshape(D_FF, D_MODEL), min(MID_TILE, t))
    up_own, up_sib, up_sums = _dw_pair_sums((hn2t, dup), order, "up", "dw_up", 2)
    down_own, down_sib, down_sums = _dw_pair_sums((actt, dmo), order, "down", "dw_down", 3)
    out_own, out_sib, out_sums = _dw_pair_sums((mattn, mconv, dmix), order, "out", "dw_out", 4)
    dattn, dgates, small_mix, up_far = _mix_bwd(dmixed, attn, gates, g_attn, g_conv, conv_w, tm, up_sums)
    dqkv, dsink, down_far = _attn_bwd(qkv, dattn, sinks, rope, down_sums)
    grad_x, dw_in, small_in, out_far = _in_proj_bwd(dqkv, dgates, x, dh, g1, w_in, tm, out_sums)
    dw_out, dw_up, dw_down = (out_own, out_sib, out_far), (up_own, up_sib, up_far), (down_own, down_sib, down_far)
    return grad_x, dw_in, dw_out, dw_up, dw_down, (small_mid, small_mix, dsink, small_in)


def _pack_small(small_mid, small_mix, dsink, small_in):
    z = lambda n: jnp.zeros((1, n), F32)
    rows = [
        small_mid[ROW_LOSS:ROW_LOSS + 1],
        small_in[0:1],
        small_mid[ROW_G2:ROW_G2 + 1],
        small_mid[ROW_G3:ROW_G3 + 1],
        small_mid[ROW_G4:ROW_G4 + 1],
        jnp.concatenate([small_mix[ROW_GATTN:ROW_GATTN + 1], small_mix[ROW_GCONV:ROW_GCONV + 1]], axis=1),
        jnp.concatenate([small_mix[ROW_CW0:ROW_CW0 + 1], small_mix[ROW_CW0 + 1:ROW_CW0 + 2]], axis=1),
        jnp.concatenate([small_mix[ROW_CW0 + 2:ROW_CW0 + 3], dsink[0:1, :], z(D_MODEL - CONV_W - 128)], axis=1),
    ]
    return jnp.concatenate(rows, axis=0)


def kernel(x, pre_mix_norm, w_in, conv_w, attn_sinks, attn_group_norm, conv_group_norm, w_out, post_mix_norm, pre_mlp_norm, w_up, w_down, post_mlp_norm, loss_target, m_pre_mix_norm, m_w_in, m_conv_w, m_attn_sinks, m_attn_group_norm, m_conv_group_norm, m_w_out, m_post_mix_norm, m_pre_mlp_norm, m_w_up, m_w_down, m_post_mlp_norm, v_pre_mix_norm, v_w_in, v_conv_w, v_attn_sinks, v_attn_group_norm, v_conv_group_norm, v_w_out, v_post_mix_norm, v_pre_mlp_norm, v_w_up, v_w_down, v_post_mlp_norm):
    xi, yi, ci = _mesh_pos()
    chip = 2 * xi + yi
    dev = 2 * chip + ci

    order = _block_order(dev)

    shards = [w_out[0].astype(BF16), w_up[0].astype(BF16), w_down[0].astype(BF16)]

    grad_x, dw_in, dw_out, dw_up, dw_down, smalls = _local_grads(
        x[0], loss_target[0], pre_mix_norm, w_in[0].astype(BF16), conv_w[0], attn_sinks, attn_group_norm, conv_group_norm,
        post_mix_norm, pre_mlp_norm, post_mlp_norm, shards, order)

    small = _sum_devices(_all_gather([_pack_small(*smalls)], "gather_small")[0])
    loss = (0.5 / D_MODEL) * jnp.sum(small[0])

    big = {}
    for name, w, m, v, (own, sib, far) in zip(
            ("w_in", "w_out", "w_up", "w_down"), (w_in, w_out, w_up, w_down), (m_w_in, m_w_out, m_w_up, m_w_down),
            (v_w_in, v_w_out, v_w_up, v_w_down), (dw_in, dw_out, dw_up, dw_down)):
        big[name] = [a[None] for a in _adamw_reduced(w[0], m[0], v[0], own, sib, far, ADAM_ROWS)]

    conv_g = lax.dynamic_slice(
        jnp.stack([small[6, :CONV_W], small[6, CONV_W:], small[7, :CONV_W]]), (0, dev * (CONV_W // N_DEV)),
        (3, CONV_W // N_DEV))
    pad = lambda a, n: jnp.pad(a.reshape(1, -1), ((0, 0), (0, n - a.size)))
    small_names = ("pre_mix_norm", "post_mix_norm", "pre_mlp_norm", "post_mlp_norm")
    small_w = {"pre_mix_norm": (pre_mix_norm, m_pre_mix_norm, v_pre_mix_norm),
               "post_mix_norm": (post_mix_norm, m_post_mix_norm, v_post_mix_norm),
               "pre_mlp_norm": (pre_mlp_norm, m_pre_mlp_norm, v_pre_mlp_norm),
               "post_mlp_norm": (post_mlp_norm, m_post_mlp_norm, v_post_mlp_norm)}

    def pack(k):
        rows = [small_w[nm][k] for nm in small_names]
        rows.append(jnp.concatenate([(attn_group_norm, m_attn_group_norm, v_attn_group_norm)[k],
                                     (conv_group_norm, m_conv_group_norm, v_conv_group_norm)[k]], axis=1))
        rows.append(pad((conv_w, m_conv_w, v_conv_w)[k], D_MODEL))
        rows.append(pad((attn_sinks, m_attn_sinks, v_attn_sinks)[k], D_MODEL))
        rows.append(jnp.zeros((1, D_MODEL), F32))
        return jnp.concatenate(rows, axis=0)

    g_small = jnp.concatenate(
        [small[1:6], pad(conv_g, D_MODEL), pad(small[7, CONV_W:CONV_W + N_HEADS], D_MODEL), jnp.zeros((1, D_MODEL), F32)],
        axis=0)
    d_small, nm_small, nv_small = _adamw_small(pack(0), g_small, pack(1), pack(2))

    def unpack(a):
        nconv = 3 * CONV_W // N_DEV
        return {"pre_mix_norm": a[0:1], "post_mix_norm": a[1:2], "pre_mlp_norm": a[2:3], "post_mlp_norm": a[3:4],
                "attn_group_norm": a[4:5, :ATTN_W], "conv_group_norm": a[4:5, ATTN_W:],
                "conv_w": a[5, :nconv].reshape(1, 3, CONV_W // N_DEV), "attn_sinks": a[6:7, :N_HEADS]}

    order = ("pre_mix_norm", "w_in", "conv_w", "attn_sinks", "attn_group_norm", "conv_group_norm", "w_out",
             "post_mix_norm", "pre_mlp_norm", "w_up", "w_down", "post_mlp_norm")
    outs = []
    for k, a in enumerate((g_small, d_small, nm_small, nv_small)):
        sm = unpack(a)
        outs += [big[nm][k] if nm in big else sm[nm] for nm in order]
    return (loss, grad_x[None], *outs)
```

```python
import functools

import jax
import jax.numpy as jnp
import numpy as np
from jax import lax
from jax.experimental import pallas as pl
from jax.experimental.pallas import tpu as pltpu

F32 = jnp.float32
BF16 = jnp.bfloat16

D_MODEL = 1024
HEAD_DIM = 64
ATTN_W = 512
CONV_W = 512
N_HEADS = 8
N_KV = 2
GROUP = 4
KV_W = 128
QKV_W = ATTN_W + 2 * KV_W
GATES_W = 3 * CONV_W
IN_COLS = QKV_W + GATES_W
D_FF = 4096
FF_CHUNK = 512
N_FF_CHUNKS = D_FF // FF_CHUNK
BLOCK = 128
ROT_HALF = 8
ROPE_THETA = 500000.0
NORM_EPS = 1e-6
NEG_INF = -1e30
ATTN_SCALE = 0.125
N_DEV = 8
N_CHIPS = 4
IN_SHARD = IN_COLS // N_DEV

ADAM_LR = 0.001
ADAM_B1 = 0.9
ADAM_B2 = 0.999
ADAM_EPS = 1e-08
ADAM_WD = 0.01
ADAM_STEP = 10

V7X_VMEM_BYTES = 64 * 1024 * 1024
VMEM_LIMIT = V7X_VMEM_BYTES - 2 * 1024 * 1024

MESH = pl.DeviceIdType.MESH
HBM_SPEC = pl.BlockSpec(memory_space=pltpu.HBM)


def _params(*sem, barrier_id=None):
    return pltpu.CompilerParams(dimension_semantics=sem or None, vmem_limit_bytes=VMEM_LIMIT, collective_id=barrier_id)


def _mm(a, b):
    return jnp.dot(a, b, preferred_element_type=F32)


def _mm_nt(a, b):
    return lax.dot_general(a, b, (((1,), (1,)), ((), ())), preferred_element_type=F32)


def _mm_tn(a, b):
    return lax.dot_general(a, b, (((0,), (0,)), ((), ())), preferred_element_type=F32)


def _inv_rms(x):
    return lax.rsqrt(jnp.mean(x * x, axis=-1, keepdims=True) + NORM_EPS)


def _rms_bwd(xhat, r, gain, dy):
    gy = dy * gain
    return r * (gy - xhat * jnp.mean(gy * xhat, axis=-1, keepdims=True)), dy * xhat


def _colsum(a):
    return jnp.sum(a, axis=0, keepdims=True)


def _full(shape):
    zeros = (0,) * len(shape)
    return pl.BlockSpec(shape, lambda *_: zeros)


def _resident(shape):
    zeros = (0,) * len(shape)
    return pl.BlockSpec(shape, lambda *_: zeros, pipeline_mode=pl.Buffered(1))


def _rope_tables(t):
    pos = np.arange(t, dtype=np.float32)
    inv_freq = (ROPE_THETA ** (-np.arange(0, 2 * ROT_HALF, 2, dtype=np.float64) / (2 * ROT_HALF))).astype(np.float32)
    ang = (pos[:, None] * inv_freq[None, :]).astype(np.float64)
    cos, sin = np.cos(ang).astype(np.float32), np.sin(ang).astype(np.float32)
    zeros8 = np.zeros((t, ROT_HALF), np.float32)
    rest = np.zeros((t, HEAD_DIM - 2 * ROT_HALF), np.float32)
    c_head = np.concatenate([cos, cos, rest + 1.0], axis=1)
    s1_head = np.concatenate([zeros8, sin, rest], axis=1)
    s2_head = np.concatenate([-sin, zeros8, rest], axis=1)
    two = lambda a: jnp.asarray(np.concatenate([a, a], axis=1))
    return two(c_head), two(s1_head), two(s2_head)


def _rope(v, c, s1, s2):
    return v * c + pltpu.roll(v, ROT_HALF, 1) * s1 + pltpu.roll(v, 128 - ROT_HALF, 1) * s2


def _rope_transpose(dv, c, s1, s2):
    return dv * c + pltpu.roll(dv * s1, 128 - ROT_HALF, 1) + pltpu.roll(dv * s2, ROT_HALF, 1)


def _shift_rows_down(u, prev, k):
    row = lax.broadcasted_iota(jnp.int32, u.shape, 0)
    out = pltpu.roll(u, k, 0)
    for r in range(k):
        out = jnp.where(row == r, prev[8 - k + r:8 - k + r + 1, :], out)
    return out


def _shift_rows_up(u, nxt, k):
    n = u.shape[0]
    row = lax.broadcasted_iota(jnp.int32, u.shape, 0)
    out = pltpu.roll(u, n - k, 0)
    for r in range(k):
        out = jnp.where(row == n - k + r, nxt[r:r + 1, :], out)
    return out


def _conv3(u, u1, u2, w):
    return (w[0:1, :] * u2 + w[1:2, :] * u1) + w[2:3, :] * u


def _mesh_pos():
    return lax.axis_index("x"), lax.axis_index("y"), lax.axis_index("c")


def _slot(ref, pos):
    dev = 4 * pos[0] + 2 * pos[1] + pos[2]
    if len(ref.shape) == 2:
        width = ref.shape[1] // N_DEV
        return ref.at[:, pl.ds(pl.multiple_of(dev * width, width), width)]
    return ref.at[dev]


def _gathered_shape(shard, by_cols):
    if by_cols:
        return jax.ShapeDtypeStruct((shard.shape[0], N_DEV * shard.shape[1]), shard.dtype)
    return jax.ShapeDtypeStruct((N_DEV,) + shard.shape, shard.dtype)


def _enter_with(peers):
    barrier = pltpu.get_barrier_semaphore()
    for peer in peers:
        pl.semaphore_signal(barrier, inc=1, device_id=peer, device_id_type=MESH)
    pl.semaphore_wait(barrier, len(peers))


def _sibling_and_chips(x, y, c):
    return [(x, y, 1 - c), (1 - x, y, c), (x, 1 - y, c), (1 - x, 1 - y, c)]


def _push(src, dst, sems, k, to):
    send_sems, recv_sems = sems
    return pltpu.make_async_remote_copy(src_ref=src, dst_ref=dst, send_sem=send_sems.at[k], recv_sem=recv_sems.at[k],
                                        device_id=to, device_id_type=MESH)


def _gather_steps(shards, outs, send_sems, recv_sems, local_sems):
    n = len(shards)
    x, y, c = _mesh_pos()
    me, sibling = (x, y, c), (x, y, 1 - c)
    chips = [(1 - x, y), (x, 1 - y), (1 - x, 1 - y)]

    def copy(i, k, block, to, src=None):
        dst = _slot(outs[i], block)
        return _push(dst if src is None else src, dst, (send_sems, recv_sems), 7 * i + k, to)

    mine = [pltpu.make_async_copy(shards[i], _slot(outs[i], me), local_sems.at[i]) for i in range(n)]
    first = []
    for i in range(n):
        first.append(copy(i, 0, me, sibling, src=shards[i]))
        first += [copy(i, 1 + j, me, (*chip, c), src=shards[i]) for j, chip in enumerate(chips)]

    def start():
        for cp in mine + first:
            cp.start()

    def finish():
        passed = []
        for j, chip in enumerate(chips):
            for i in range(n):
                copy(i, 1 + j, (*chip, c), me).wait_recv()
                cp = copy(i, 4 + j, (*chip, c), sibling)
                cp.start()
                passed.append(cp)
        for i in range(n):
            copy(i, 0, sibling, me).wait_recv()
            for j, chip in enumerate(chips):
                copy(i, 4 + j, (*chip, 1 - c), me).wait_recv()
        for cp in first + passed:
            cp.wait_send()
        for cp in mine:
            cp.wait()

    return start, finish


def _gather_near(first, last, shards, outs, sems, local_sems):
    x, y, c = _mesh_pos()
    me, peers = (x, y, c), [(x, y, 1 - c), (1 - x, y, c), (x, 1 - y, c)]
    n = len(shards)
    local = [pltpu.make_async_copy(shards[i], _slot(outs[i], me), local_sems.at[i]) for i in range(n)]
    sends = [_push(shards[i], _slot(outs[i], me), sems, 3 * i + k, peers[k]) for i in range(n) for k in range(3)]
    arrivals = [_push(shards[i], _slot(outs[i], peers[k]), sems, 3 * i + k, peers[k]) for i in range(n) for k in range(3)]

    def start():
        for cp in local + sends:
            cp.start()

    if first is not None:
        pl.when(first)(start)

    @pl.when(last)
    def _():
        for cp in sends:
            cp.wait_send()
        for cp in arrivals:
            cp.wait_recv()
        for cp in local:
            cp.wait()

    return start


def _gather_far(first, last, shards, ins, outs, sems):
    x, y, c = _mesh_pos()
    me, sibling = (x, y, c), (x, y, 1 - c)
    chips = [(1 - x, y), (x, 1 - y), (1 - x, 1 - y)]
    n = len(shards)
    diag_send = [_push(shards[i], _slot(outs[i], me), sems, 4 * i, (*chips[2], c)) for i in range(n)]
    diag_arrival = [_push(shards[i], _slot(outs[i], (*chips[2], c)), sems, 4 * i, (*chips[2], c)) for i in range(n)]
    passed = [[_push(_slot(ins[i], (*chips[j], c)), _slot(outs[i], (*chips[j], c)), sems, 4 * i + 1 + j, sibling)
               for i in range(n)] for j in range(3)]
    from_sibling = [_push(shards[i], _slot(outs[i], (*chips[j], 1 - c)), sems, 4 * i + 1 + j, sibling)
                    for i in range(n) for j in range(3)]

    @pl.when(first)
    def _():
        for cp in diag_send + passed[0] + passed[1]:
            cp.start()

    @pl.when(last)
    def _():
        for cp in diag_arrival:
            cp.wait_recv()
        for cp in passed[2]:
            cp.start()
        for cp in from_sibling:
            cp.wait_recv()
        for cp in diag_send + passed[0] + passed[1] + passed[2]:
            cp.wait_send()


def _in_proj_fwd(x, g1, w_in, conv_w, g_conv, rope, tm, shards, by_cols):
    t = x.shape[0]
    rc, rs1, rs2 = rope
    n = len(shards)
    n_tiles = t // tm

    def body(*refs):
        x_ref, g1_ref, w_ref, cw_ref, gc_ref, c_ref, s1_ref, s2_ref = refs[:8]
        shard_refs = refs[8:8 + n]
        qkv_ref, gates_ref, mconv_ref, w_full_ref, cw_full_ref = refs[8 + n:13 + n]
        gathered = refs[13 + n:13 + 2 * n]
        carry_ref, w_land, cw_land, hn_ref = refs[13 + 2 * n:17 + 2 * n]
        now_sems = refs[17 + 2 * n:20 + 2 * n]
        step = pl.program_id(0)
        start_later_weights = _gather_near(None, step == 2 * n_tiles - 1, shard_refs, gathered,
                                           refs[20 + 2 * n:22 + 2 * n], refs[22 + 2 * n]) if n else None
        start_w_in, finish_w_in = _gather_steps([w_ref, cw_ref], [w_land, cw_land], *now_sems)

        @pl.when(step == 0)
        def _():
            carry_ref[...] = jnp.zeros_like(carry_ref)
            _enter_with(_sibling_and_chips(*_mesh_pos()))
            start_w_in()
            if start_later_weights is not None:
                start_later_weights()

        @pl.when(step < n_tiles)
        def _():
            xv = x_ref[...]
            hn_ref[step] = ((xv * _inv_rms(xv)) * g1_ref[...]).astype(BF16)

        @pl.when(step == n_tiles)
        def _():
            finish_w_in()
            conv_shard = CONV_W // N_DEV
            for d in range(N_DEV):
                w_full_ref[:, IN_SHARD * d:IN_SHARD * (d + 1)] = w_land[d]
                cw_full_ref[:, conv_shard * d:conv_shard * (d + 1)] = cw_land[d]

        @pl.when(step >= n_tiles)
        def _():
            proj = _mm(hn_ref[step - n_tiles], w_full_ref[...])
            c, s1, s2 = c_ref[...], s1_ref[...], s2_ref[...]
            for ci in range((ATTN_W + KV_W) // 128):
                sl = slice(128 * ci, 128 * (ci + 1))
                qkv_ref[:, sl] = _rope(proj[:, sl], c, s1, s2).astype(BF16)
            qkv_ref[:, ATTN_W + KV_W:QKV_W] = proj[:, ATTN_W + KV_W:QKV_W].astype(BF16)
            gates = proj[:, QKV_W:]
            gates_ref[...] = gates
            gb, gcc, xin = gates[:, :CONV_W], gates[:, CONV_W:2 * CONV_W], gates[:, 2 * CONV_W:]
            u = gcc * xin
            prev = carry_ref[...]
            conv = gb * _conv3(u, _shift_rows_down(u, prev, 1), _shift_rows_down(u, prev, 2), cw_full_ref[...])
            carry_ref[...] = u[tm - 8:tm, :]
            mconv_ref[...] = ((conv * _inv_rms(conv)) * gc_ref[...]).astype(BF16)

    first_pass = pl.BlockSpec((tm, D_MODEL), lambda i: (jnp.minimum(i, n_tiles - 1), 0))
    tile = lambda w_: pl.BlockSpec((tm, w_), lambda i: (jnp.maximum(i - n_tiles, 0), 0))
    sems = lambda k: pltpu.SemaphoreType.DMA((k,))
    res = pl.pallas_call(
        body, name="in_proj_fwd", grid=(2 * n_tiles,),
        in_specs=[first_pass, _full((1, D_MODEL)), HBM_SPEC, HBM_SPEC, _full((1, CONV_W)), tile(128), tile(128),
                  tile(128)] + [HBM_SPEC] * n,
        out_specs=[tile(QKV_W), tile(GATES_W), tile(CONV_W), _full((D_MODEL, IN_COLS)), _full((3, CONV_W))]
        + [HBM_SPEC] * n,
        out_shape=[jax.ShapeDtypeStruct((t, QKV_W), BF16), jax.ShapeDtypeStruct((t, GATES_W), F32),
                   jax.ShapeDtypeStruct((t, CONV_W), BF16), jax.ShapeDtypeStruct((D_MODEL, IN_COLS), BF16),
                   jax.ShapeDtypeStruct((3, CONV_W), F32)]
        + [_gathered_shape(s, cols) for s, cols in zip(shards, by_cols)],
        scratch_shapes=[pltpu.VMEM((8, CONV_W), F32), pltpu.VMEM((N_DEV,) + w_in.shape, BF16),
                        pltpu.VMEM((N_DEV,) + conv_w.shape, F32), pltpu.VMEM((n_tiles, tm, D_MODEL), BF16),
                        sems(14), sems(14), sems(2)]
        + ([sems(3 * n), sems(3 * n), sems(n)] if n else []),
        compiler_params=_params("arbitrary", barrier_id=0),
    )(x, g1, w_in, conv_w, g_conv, rc, rs1, rs2, *shards)
    return res[0], res[1], res[2], res[3], res[4], list(res[5:])


GROUP_COLS = GROUP * BLOCK
ATTN_STEP_BLOCKS = 4


def _attn_masks(has_prev):
    key = lax.broadcasted_iota(jnp.int32, (2 * BLOCK, GROUP_COLS), 0)
    query = lax.broadcasted_iota(jnp.int32, (2 * BLOCK, GROUP_COLS), 1) & (BLOCK - 1)
    band = (key > query) & (key <= query + BLOCK)
    return [band & ((key >= BLOCK) | has_prev)] + [band] * (ATTN_STEP_BLOCKS - 1)


def _heads_side_by_side(at, g, b):
    heads = [at[HEAD_DIM * (GROUP * g + hh):HEAD_DIM * (GROUP * g + hh + 1), BLOCK * b:BLOCK * (b + 1)] for hh in range(GROUP)]
    return jnp.concatenate(heads, axis=1)


def _to_token_rows(parts):
    rows = [jnp.concatenate([parts[b][g][:, BLOCK * hh:BLOCK * (hh + 1)] for b in range(ATTN_STEP_BLOCKS)], axis=1)
            for g in range(N_KV) for hh in range(GROUP)]
    return jnp.concatenate(rows, axis=0).T


def _group_sinks(sink_ref, g):
    head = lax.broadcasted_iota(jnp.int32, (1, GROUP_COLS), 1) // BLOCK
    out = jnp.full((1, GROUP_COLS), sink_ref[0, GROUP * g], F32)
    for hh in range(1, GROUP):
        out = jnp.where(head == hh, sink_ref[0, GROUP * g + hh], out)
    return out


def _attn_probs(qt, kk, sink, valid):
    s = jnp.where(valid, _mm(kk, qt), NEG_INF)
    m = jnp.maximum(jnp.max(s, axis=0, keepdims=True), sink)
    p = jnp.exp(s - m)
    psink = jnp.exp(sink - m)
    inv_l = 1.0 / (jnp.sum(p, axis=0, keepdims=True) + psink)
    return p * inv_l, psink * inv_l


ATTN_STEP = ATTN_STEP_BLOCKS * BLOCK
ATTN_KEYS = ATTN_STEP + BLOCK


def _qkv_specs(order):
    prev = lambda i: jnp.maximum(ATTN_STEP_BLOCKS * order(i) - 1, 0)
    kcol, vcol = ATTN_W // KV_W, ATTN_W // KV_W + 1
    return [pl.BlockSpec((ATTN_STEP, ATTN_W), lambda i: (order(i), 0)),
            pl.BlockSpec((BLOCK, KV_W), lambda i: (prev(i), kcol)), pl.BlockSpec((ATTN_STEP, KV_W), lambda i: (order(i), kcol)),
            pl.BlockSpec((BLOCK, KV_W), lambda i: (prev(i), vcol)), pl.BlockSpec((ATTN_STEP, KV_W), lambda i: (order(i), vcol))]


def _attn_fwd(qkv, sinks, g_attn, shards, gathered):
    t = qkv.shape[0]
    n = len(shards)

    def body(*refs):
        sink_ref, q_ref, kp_ref, kc_ref, vp_ref, vc_ref, ga_ref = refs[:7]
        attn_ref, mattn_ref = refs[7 + 2 * n:9 + 2 * n]
        step = pl.program_id(0)
        if n:
            @pl.when(step == 0)
            def _():
                x, y, c = _mesh_pos()
                _enter_with([(x, y, 1 - c), (1 - x, 1 - y, c)])

            _gather_far(step == 0, step == pl.num_programs(0) - 1, refs[7:7 + n], refs[7 + n:7 + 2 * n],
                        refs[9 + 2 * n:9 + 3 * n], refs[9 + 3 * n:11 + 3 * n])
        qt = (q_ref[...] * ATTN_SCALE).T
        keys = jnp.concatenate([kp_ref[...], kc_ref[...]], axis=0)
        vals = jnp.concatenate([vp_ref[...], vc_ref[...]], axis=0)
        sink = [_group_sinks(sink_ref, g) for g in range(N_KV)]
        masks = _attn_masks(step > 0)
        parts = []
        for b in range(ATTN_STEP_BLOCKS):
            window = slice(BLOCK * b, BLOCK * (b + 2))
            valid = masks[b]
            parts.append([])
            for g in range(N_KV):
                gs = slice(HEAD_DIM * g, HEAD_DIM * (g + 1))
                probs, _ = _attn_probs(_heads_side_by_side(qt, g, b), keys[window, gs], sink[g], valid)
                parts[b].append(_mm_tn(vals[window, gs], probs.astype(BF16)))
        attn = _to_token_rows(parts)
        attn_ref[...] = attn
        mattn_ref[...] = ((attn * _inv_rms(attn)) * ga_ref[...]).astype(BF16)

    blk = pl.BlockSpec((ATTN_STEP, ATTN_W), lambda j: (j, 0))
    res = pl.pallas_call(
        body, name="attn_fwd", grid=(t // ATTN_STEP,),
        in_specs=[pl.BlockSpec(memory_space=pltpu.SMEM)] + _qkv_specs(lambda j: j) + [_full((1, ATTN_W))]
        + [HBM_SPEC] * (2 * n),
        out_specs=[blk, blk] + [HBM_SPEC] * n,
        out_shape=[jax.ShapeDtypeStruct((t, ATTN_W), F32), jax.ShapeDtypeStruct((t, ATTN_W), BF16)]
        + [jax.ShapeDtypeStruct(g.shape, g.dtype) for g in gathered],
        input_output_aliases={7 + n + i: 2 + i for i in range(n)},
        scratch_shapes=[pltpu.SemaphoreType.DMA((4 * n,)), pltpu.SemaphoreType.DMA((4 * n,))] if n else [],
        compiler_params=_params("arbitrary", barrier_id=1 if n else None),
    )(sinks, qkv, qkv, qkv, qkv, qkv, g_attn, *shards, *gathered)
    return res[0], res[1], list(res[2:])


SMALL_ROWS = 8
ROW_LOSS, ROW_G2, ROW_G3, ROW_G4 = 0, 1, 2, 3


def _mid(mattn, mconv, x, target, g2, g3, g4, w_out, w_up, w_down, tm):
    t = x.shape[0]

    def body(ma_ref, mc_ref, x_ref, t_ref, g2_ref, g3_ref, g4_ref, wo_ref, wu_ref, wd_ref,
             actt_ref, dup_ref, hn2t_ref, dmo_ref, dmix_ref, dh_ref, dmixed_ref, small_ref, up_ref):
        @pl.when(pl.program_id(0) == 0)
        def _():
            small_ref[...] = jnp.zeros_like(small_ref)

        g2, g3, g4 = g2_ref[...], g3_ref[...], g4_ref[...]
        mix_out = _mm(ma_ref[...], wo_ref[0:ATTN_W, :]) + _mm(mc_ref[...], wo_ref[ATTN_W:, :])
        r2 = _inv_rms(mix_out)
        mo_hat = mix_out * r2
        h = x_ref[...] + mo_hat * g2
        r3 = _inv_rms(h)
        h_hat = h * r3
        hn2 = (h_hat * g3).astype(BF16)
        hn2t_ref[...] = hn2.T
        up = jnp.maximum(_mm(hn2, wu_ref[...]), 0.0)
        up_ref[...] = up.astype(BF16)
        act = (up * up).astype(BF16)
        actt_ref[...] = act.T
        mlp = _mm(act, wd_ref[...])
        r4 = _inv_rms(mlp)
        ml_hat = mlp * r4
        err = (h + ml_hat * g4) - t_ref[...]
        d_out = err * (1.0 / D_MODEL)
        d_mlp, dg4 = _rms_bwd(ml_hat, r4, g4, d_out)
        dmo = d_mlp.astype(BF16)
        dmo_ref[...] = dmo
        dup = (_mm_nt(dmo, wd_ref[...]) * (2.0 * up_ref[...].astype(F32))).astype(BF16)
        dup_ref[...] = dup
        dhn2 = _mm_nt(dup, wu_ref[...])
        dh_norm, dg3 = _rms_bwd(h_hat, r3, g3, dhn2)
        dh = d_out + dh_norm
        dh_ref[...] = dh
        d_mix, dg2 = _rms_bwd(mo_hat, r2, g2, dh)
        dmix = d_mix.astype(BF16)
        dmix_ref[...] = dmix
        dmixed_ref[...] = _mm_nt(dmix, wo_ref[...])
        small_ref[ROW_LOSS:ROW_LOSS + 1, :] += _colsum(err * err)
        small_ref[ROW_G2:ROW_G2 + 1, :] += _colsum(dg2)
        small_ref[ROW_G3:ROW_G3 + 1, :] += _colsum(dg3)
        small_ref[ROW_G4:ROW_G4 + 1, :] += _colsum(dg4)

    tile = lambda n: pl.BlockSpec((tm, n), lambda i: (i, 0))
    cols = lambda n: pl.BlockSpec((n, tm), lambda i: (0, i))
    gain = _full((1, D_MODEL))
    return pl.pallas_call(
        body, name="mid_fwd_bwd", grid=(t // tm,),
        in_specs=[tile(ATTN_W), tile(CONV_W), tile(D_MODEL), tile(D_MODEL), gain, gain, gain,
                  _resident((D_MODEL, D_MODEL)), _resident((D_MODEL, D_FF)), _resident((D_FF, D_MODEL))],
        out_specs=[cols(D_FF), tile(D_FF), cols(D_MODEL), tile(D_MODEL), tile(D_MODEL), tile(D_MODEL), tile(D_MODEL),
                   _full((SMALL_ROWS, D_MODEL))],
        out_shape=[jax.ShapeDtypeStruct((D_FF, t), BF16), jax.ShapeDtypeStruct((t, D_FF), BF16),
                   jax.ShapeDtypeStruct((D_MODEL, t), BF16), jax.ShapeDtypeStruct((t, D_MODEL), BF16),
                   jax.ShapeDtypeStruct((t, D_MODEL), BF16), jax.ShapeDtypeStruct((t, D_MODEL), F32),
                   jax.ShapeDtypeStruct((t, D_MODEL), F32), jax.ShapeDtypeStruct((SMALL_ROWS, D_MODEL), F32)],
        scratch_shapes=[pltpu.VMEM((tm, D_FF), BF16)],
        compiler_params=_params("arbitrary"),
    )(mattn, mconv, x, target, g2, g3, g4, w_out, w_up, w_down)


CHIP_FLIPS = ((1, 1), (1, 0), (0, 1))


def _block_order(dev):
    chip_masks = [4 * fx + 2 * fy for fx, fy in CHIP_FLIPS]
    masks = [m + 1 for m in chip_masks] + [1] + chip_masks + [0]
    return jnp.bitwise_xor(dev, jnp.asarray(masks, jnp.int32)).astype(jnp.int32)


def _other_chips(x, y, c):
    return [(1 - x if fx else x, 1 - y if fy else y, c) for fx, fy in CHIP_FLIPS]


def _dw_pair_sums(operands, order, which, name, barrier_id):
    t = operands[-1].shape[0]
    n_far = len(CHIP_FLIPS)
    n_in = len(operands)
    out_chunk = D_MODEL // N_DEV
    if which == "up":
        rows, cols = D_MODEL, FF_CHUNK
        in_specs = [_resident((D_MODEL, t)), pl.BlockSpec((t, FF_CHUNK), lambda s, order_ref: (0, order_ref[s]))]
    elif which == "down":
        rows, cols = FF_CHUNK, D_MODEL
        in_specs = [pl.BlockSpec((FF_CHUNK, t), lambda s, order_ref: (order_ref[s], 0)), _resident((t, D_MODEL))]
    else:
        rows, cols = out_chunk, D_MODEL
        half = pl.BlockSpec((t, out_chunk), lambda s, order_ref: (0, order_ref[s] % (N_DEV // 2)))
        in_specs = [half, half, _resident((t, D_MODEL))]

    def body(order_ref, *refs):
        own_ref, from_sib_ref, pair_ref, send_buf, land_buf, send_sems, recv_sems = refs[n_in:]
        s_now = pl.program_id(0)
        x, y, c = _mesh_pos()
        sibling = (x, y, 1 - c)
        sems = (send_sems, recv_sems)

        @pl.when(s_now == 0)
        def _():
            _enter_with([sibling])

        def hand_over(k):
            dst = land_buf.at[k] if k < n_far else from_sib_ref
            return _push(send_buf.at[k], dst, sems, k, sibling)

        if which == "out":
            ma_ref, mc_ref, b_ref = refs[:n_in]
            block = lax.cond(order_ref[s_now] < N_DEV // 2, lambda: _mm_tn(ma_ref[...], b_ref[...]),
                             lambda: _mm_tn(mc_ref[...], b_ref[...]))
        else:
            block = _mm(refs[0][...], refs[1][...])
        for k in range(n_far + 1):
            @pl.when(s_now == k)
            def _():
                send_buf[k] = block.astype(BF16)
                hand_over(k).start()

        for k in range(n_far):
            @pl.when(s_now == n_far + 1 + k)
            def _():
                hand_over(k).wait_recv()
                pair_ref[...] = (block + land_buf[k].astype(F32)).astype(BF16)

        @pl.when(s_now == N_DEV - 1)
        def _():
            own_ref[...] = block
            for k in range(n_far + 1):
                hand_over(k).wait_send()
            hand_over(n_far).wait_recv()

    return pl.pallas_call(
        body, name=name,
        grid_spec=pltpu.PrefetchScalarGridSpec(
            num_scalar_prefetch=1, grid=(N_DEV,), in_specs=in_specs,
            out_specs=[pl.BlockSpec((rows, cols), lambda s, order_ref: (0, 0)), HBM_SPEC,
                       pl.BlockSpec((None, rows, cols), lambda s, order_ref: (jnp.clip(s - n_far - 1, 0, n_far - 1), 0, 0))],
            scratch_shapes=[pltpu.VMEM((n_far + 1, rows, cols), BF16), pltpu.VMEM((n_far, rows, cols), BF16),
                            pltpu.SemaphoreType.DMA((n_far + 1,)), pltpu.SemaphoreType.DMA((n_far + 1,))]),
        out_shape=[jax.ShapeDtypeStruct((rows, cols), F32), jax.ShapeDtypeStruct((rows, cols), BF16),
                   jax.ShapeDtypeStruct((n_far, rows, cols), BF16)],
        compiler_params=_params("arbitrary", barrier_id=barrier_id),
    )(order, *operands)


def _chip_exchange_beside(first, last, sums, outs, sems, enter=True):
    chips = _other_chips(*_mesh_pos())
    copies = [_push(sums[i].at[k], outs[i].at[k], sems, len(chips) * i + k, chip)
              for i in range(len(sums)) for k, chip in enumerate(chips)]

    @pl.when(first)
    def _():
        if enter:
            _enter_with(chips)
        for cp in copies:
            cp.start()

    @pl.when(last)
    def _():
        for cp in copies:
            cp.wait()


ROW_GATTN, ROW_GCONV, ROW_CW0 = 0, 1, 2


def _mix_bwd(dmixed, attn, gates, g_attn, g_conv, conv_w, tm, sums):
    t = attn.shape[0]
    n = t // tm
    rev = lambda i: n - 1 - i

    def body(dm_ref, attn_ref, gates_ref, gprev_ref, ga_ref, gc_ref, cw_ref, sums_ref, dattn_ref, dgates_ref, small_ref,
             arrived_ref, carry_ref, send_sems, recv_sems):
        i = pl.program_id(0)
        _chip_exchange_beside(i == 0, i == n - 1, [sums_ref], [arrived_ref], (send_sems, recv_sems))

        @pl.when(i == 0)
        def _():
            small_ref[...] = jnp.zeros_like(small_ref)
            carry_ref[...] = jnp.zeros_like(carry_ref)

        dm = dm_ref[...]
        a = attn_ref[...]
        ra = _inv_rms(a)
        a_hat = a * ra
        dattn, dga = _rms_bwd(a_hat, ra, ga_ref[...], dm[:, :ATTN_W])
        dattn_ref[...] = dattn

        gates = gates_ref[...]
        gb, gcc, xin = gates[:, :CONV_W], gates[:, CONV_W:2 * CONV_W], gates[:, 2 * CONV_W:]
        u = gcc * xin
        gp = gprev_ref[...]
        uprev = jnp.where(rev(i) == 0, 0.0, gp[:, CONV_W:2 * CONV_W] * gp[:, 2 * CONV_W:])
        u1, u2 = _shift_rows_down(u, uprev, 1), _shift_rows_down(u, uprev, 2)
        w = cw_ref[...]
        c = _conv3(u, u1, u2, w)
        conv = gb * c
        rcv = _inv_rms(conv)
        c_hat = conv * rcv
        dconv, dgc = _rms_bwd(c_hat, rcv, gc_ref[...], dm[:, ATTN_W:])
        dc = dconv * gb
        nxt = carry_ref[...]
        du = (w[2:3, :] * dc + w[1:2, :] * _shift_rows_up(dc, nxt, 1)) + w[0:1, :] * _shift_rows_up(dc, nxt, 2)
        carry_ref[...] = dc[0:8, :]
        dgates_ref[:, :CONV_W] = (dconv * c).astype(BF16)
        dgates_ref[:, CONV_W:2 * CONV_W] = (du * xin).astype(BF16)
        dgates_ref[:, 2 * CONV_W:] = (du * gcc).astype(BF16)
        small_ref[ROW_GATTN:ROW_GATTN + 1, :] += _colsum(dga)
        small_ref[ROW_GCONV:ROW_GCONV + 1, :] += _colsum(dgc)
        small_ref[ROW_CW0:ROW_CW0 + 1, :] += _colsum(dc * u2)
        small_ref[ROW_CW0 + 1:ROW_CW0 + 2, :] += _colsum(dc * u1)
        small_ref[ROW_CW0 + 2:ROW_CW0 + 3, :] += _colsum(dc * u)

    tile = lambda w_: pl.BlockSpec((tm, w_), lambda i: (rev(i), 0))
    prev8 = pl.BlockSpec((8, GATES_W), lambda i: (jnp.maximum(rev(i) * (tm // 8) - 1, 0), 0))
    return pl.pallas_call(
        body, name="mix_bwd", grid=(n,),
        in_specs=[tile(D_MODEL), tile(ATTN_W), tile(GATES_W), prev8, _full((1, ATTN_W)), _full((1, CONV_W)),
                  _full((3, CONV_W)), HBM_SPEC],
        out_specs=[tile(ATTN_W), tile(GATES_W), _full((SMALL_ROWS, CONV_W)), HBM_SPEC],
        out_shape=[jax.ShapeDtypeStruct((t, ATTN_W), F32), jax.ShapeDtypeStruct((t, GATES_W), BF16),
                   jax.ShapeDtypeStruct((SMALL_ROWS, CONV_W), F32), jax.ShapeDtypeStruct(sums.shape, sums.dtype)],
        scratch_shapes=[pltpu.VMEM((8, CONV_W), F32), pltpu.SemaphoreType.DMA((len(CHIP_FLIPS),)),
                        pltpu.SemaphoreType.DMA((len(CHIP_FLIPS),))],
        compiler_params=_params("arbitrary", barrier_id=5),
    )(dmixed, attn, gates, gates, g_attn, g_conv, conv_w, sums)


def _attn_bwd(qkv, dattn, sinks, rope, sums):
    t = qkv.shape[0]
    n_steps = t // ATTN_STEP
    rev = lambda i: n_steps - 1 - i
    rc, rs1, rs2 = rope

    def body(sink_ref, q_ref, kp_ref, kc_ref, vp_ref, vc_ref, do_ref, c_ref, s1_ref, s2_ref, sums_ref,
             dqkv_ref, dsink_ref, arrived_ref, ck_ref, cv_ref, kacc_ref, vacc_ref, send_sems, recv_sems):
        i = pl.program_id(0)
        _chip_exchange_beside(i == 0, i == n_steps - 1, [sums_ref], [arrived_ref], (send_sems, recv_sems))

        @pl.when(i == 0)
        def _():
            dsink_ref[...] = jnp.zeros_like(dsink_ref)
            ck_ref[...] = jnp.zeros_like(ck_ref)
            cv_ref[...] = jnp.zeros_like(cv_ref)

        kacc_ref[...] = jnp.zeros_like(kacc_ref)
        vacc_ref[...] = jnp.zeros_like(vacc_ref)
        qt = (q_ref[...] * ATTN_SCALE).T
        dot = do_ref[...].astype(BF16).T
        keys = jnp.concatenate([kp_ref[...], kc_ref[...]], axis=0)
        vals = jnp.concatenate([vp_ref[...], vc_ref[...]], axis=0)
        sink = [_group_sinks(sink_ref, g) for g in range(N_KV)]
        c, s1, s2 = c_ref[...], s1_ref[...], s2_ref[...]
        lane = lax.broadcasted_iota(jnp.int32, (1, 128), 1)
        dsink = jnp.zeros((1, 128), F32)
        masks = _attn_masks(rev(i) > 0)
        dq_parts = []
        for b in range(ATTN_STEP_BLOCKS):
            window = slice(BLOCK * b, BLOCK * (b + 2))
            valid = masks[b]
            dq_parts.append([])
            dk_parts, dv_parts = [], []
            for g in range(N_KV):
                gs = slice(HEAD_DIM * g, HEAD_DIM * (g + 1))
                kk, vv = keys[window, gs], vals[window, gs]
                qtg, dotg = _heads_side_by_side(qt, g, b), _heads_side_by_side(dot, g, b)
                probs, psink = _attn_probs(qtg, kk, sink[g], valid)
                dp = _mm(vv, dotg)
                delta = jnp.sum(probs * dp, axis=0, keepdims=True)
                ds = (probs * (dp - delta)).astype(BF16)
                sink_terms = psink * delta
                for hh in range(GROUP):
                    head_sum = jnp.sum(sink_terms[:, BLOCK * hh:BLOCK * (hh + 1)])
                    dsink = dsink + jnp.where(lane == GROUP * g + hh, -head_sum, 0.0)
                dq_parts[b].append(_mm_tn(kk * ATTN_SCALE, ds))
                dk_parts.append(_mm_nt(ds, qtg))
                dv_parts.append(_mm_nt(probs.astype(BF16), dotg))
            kacc_ref[window, :] += jnp.concatenate(dk_parts, axis=1)
            vacc_ref[window, :] += jnp.concatenate(dv_parts, axis=1)
        dq = _to_token_rows(dq_parts)
        for ci in range(ATTN_W // 128):
            sl = slice(128 * ci, 128 * (ci + 1))
            dqkv_ref[:, sl] = _rope_transpose(dq[:, sl], c, s1, s2).astype(BF16)
        kacc_ref[ATTN_STEP:, :] += ck_ref[...]
        vacc_ref[ATTN_STEP:, :] += cv_ref[...]
        ck_ref[...] = kacc_ref[:BLOCK, :]
        cv_ref[...] = vacc_ref[:BLOCK, :]
        dqkv_ref[:, ATTN_W:ATTN_W + KV_W] = _rope_transpose(kacc_ref[BLOCK:, :], c, s1, s2).astype(BF16)
        dqkv_ref[:, ATTN_W + KV_W:] = vacc_ref[BLOCK:, :].astype(BF16)
        dsink_ref[0:1, :] += dsink

    blk = lambda w_: pl.BlockSpec((ATTN_STEP, w_), lambda i: (rev(i), 0))
    return pl.pallas_call(
        body, name="attn_bwd", grid=(n_steps,),
        in_specs=[pl.BlockSpec(memory_space=pltpu.SMEM)] + _qkv_specs(rev) + [blk(ATTN_W), blk(128), blk(128), blk(128),
                                                                              HBM_SPEC],
        out_specs=[blk(QKV_W), _full((8, 128)), HBM_SPEC],
        out_shape=[jax.ShapeDtypeStruct((t, QKV_W), BF16), jax.ShapeDtypeStruct((8, 128), F32),
                   jax.ShapeDtypeStruct(sums.shape, sums.dtype)],
        scratch_shapes=[pltpu.VMEM((BLOCK, KV_W), F32), pltpu.VMEM((BLOCK, KV_W), F32),
                        pltpu.VMEM((ATTN_KEYS, KV_W), F32), pltpu.VMEM((ATTN_KEYS, KV_W), F32),
                        pltpu.SemaphoreType.DMA((len(CHIP_FLIPS),)), pltpu.SemaphoreType.DMA((len(CHIP_FLIPS),))],
        compiler_params=_params("arbitrary", barrier_id=6),
    )(sinks, qkv, qkv, qkv, qkv, qkv, dattn, rc, rs1, rs2, sums)


def _grad_x_tile(dq, dg, x_hat, r, g1, w_ref, dh):
    dhn = _mm_nt(dq, w_ref[:, :QKV_W]) + _mm_nt(dg, w_ref[:, QKV_W:])
    dx, dg1 = _rms_bwd(x_hat, r, g1, dhn)
    return dh + dx, _colsum(dg1)


def _in_proj_bwd(dqkv, dgates, x, dh, g1, w_in, tm, out_sums):
    t = x.shape[0]
    n = t // tm
    n_steps = 2 * n
    n_far = len(CHIP_FLIPS)
    shard = (D_MODEL, IN_SHARD)

    def body(dq_ref, dg_ref, x_ref, dh_ref, g1_ref, w_ref, osums_ref,
             dx_ref, own_ref, sib_ref, far_ref, dg1_ref, oarrived_ref,
             acc_ref, send_buf, land_buf, pair_buf, d2d_send, d2d_recv, ici_send, ici_recv, o_send, o_recv):
        i = pl.program_id(0)
        x_pos, y_pos, c = _mesh_pos()
        my_chip = 2 * x_pos + y_pos
        sibling = (x_pos, y_pos, 1 - c)
        @pl.when(i == 0)
        def _():
            _enter_with(_sibling_and_chips(x_pos, y_pos, c))

        _chip_exchange_beside(i == 0, i == n_steps - 1, [osums_ref], [oarrived_ref], (o_send, o_recv), enter=False)

        def cols(d):
            return slice(IN_SHARD * d, IN_SHARD * (d + 1))

        def hand_over(chip):
            return _push(send_buf.at[chip], land_buf.at[chip], (d2d_send, d2d_recv), chip, sibling)

        def to_chip(chip, rel):
            return pltpu.make_async_remote_copy(
                src_ref=pair_buf.at[chip], dst_ref=far_ref.at[rel - 1], send_sem=ici_send.at[rel - 1],
                recv_sem=ici_recv.at[rel - 1], device_id=(chip // 2, chip % 2, c), device_id_type=MESH)

        @pl.when(i == 0)
        def _():
            acc_ref[...] = jnp.zeros_like(acc_ref)
            dg1_ref[...] = jnp.zeros_like(dg1_ref)

        def normed_x():
            xv = x_ref[...]
            r = _inv_rms(xv)
            return xv * r, r

        @pl.when(i < n)
        def _():
            hn = (normed_x()[0] * g1_ref[...]).astype(BF16)
            acc_ref[:, :QKV_W] += _mm_tn(hn, dq_ref[...])
            acc_ref[:, QKV_W:] += _mm_tn(hn, dg_ref[...])

        @pl.when(i == n - 1)
        def _():
            for d in range(N_DEV):
                @pl.when(d % 2 != c)
                def _():
                    send_buf[d // 2] = acc_ref[:, cols(d)].astype(BF16)
                    hand_over(d // 2).start()
            for d in range(N_DEV):
                chip = d // 2

                @pl.when(d % 2 == c)
                def _():
                    hand_over(chip).wait_recv()

                    @pl.when(chip == my_chip)
                    def _():
                        own_ref[...] = acc_ref[:, cols(d)]
                        sib_ref[...] = land_buf[chip]

                    @pl.when(chip != my_chip)
                    def _():
                        pair_buf[chip] = (acc_ref[:, cols(d)] + land_buf[chip].astype(F32)).astype(BF16)
                        to_chip(chip, chip ^ my_chip).start()
            for chip in range(N_CHIPS):
                hand_over(chip).wait_send()

        @pl.when(i >= n)
        def _():
            x_hat, r = normed_x()
            dx_ref[...], dg1 = _grad_x_tile(dq_ref[...], dg_ref[...], x_hat, r, g1_ref[...], w_ref, dh_ref[...])
            dg1_ref[0:1, :] += dg1

        @pl.when(i == n_steps - 1)
        def _():
            for rel in range(1, n_far + 1):
                to_chip(0, rel).wait()

    both = lambda w_: pl.BlockSpec((tm, w_), lambda i: (i % n, 0))
    second = pl.BlockSpec((tm, D_MODEL), lambda i: (jnp.maximum(i - n, 0), 0))
    whole = lambda dtype: jax.ShapeDtypeStruct(shard, dtype)
    sems = lambda k: pltpu.SemaphoreType.DMA((k,))
    res = pl.pallas_call(
        body, name="in_proj_bwd", grid=(n_steps,),
        in_specs=[both(QKV_W), both(GATES_W), both(D_MODEL), second, _full((1, D_MODEL)), _resident((D_MODEL, IN_COLS)),
                  HBM_SPEC],
        out_specs=[second, _full(shard), _full(shard), HBM_SPEC, _full((SMALL_ROWS, D_MODEL)), HBM_SPEC],
        out_shape=[jax.ShapeDtypeStruct((t, D_MODEL), F32), whole(F32), whole(BF16),
                   jax.ShapeDtypeStruct((n_far,) + shard, BF16), jax.ShapeDtypeStruct((SMALL_ROWS, D_MODEL), F32),
                   jax.ShapeDtypeStruct(out_sums.shape, out_sums.dtype)],
        scratch_shapes=[pltpu.VMEM((D_MODEL, IN_COLS), F32), pltpu.VMEM((N_CHIPS,) + shard, BF16),
                        pltpu.VMEM((N_CHIPS,) + shard, BF16), pltpu.VMEM((N_CHIPS,) + shard, BF16),
                        sems(N_CHIPS), sems(N_CHIPS), sems(n_far), sems(n_far), sems(n_far), sems(n_far)],
        compiler_params=_params("arbitrary", barrier_id=7),
    )(dqkv, dgates, x, dh, g1, w_in, out_sums)
    return res[0], (res[1], res[2], res[3]), res[4], res[5]


def _all_gather(shards, name):
    n = len(shards)

    def body(*refs):
        _enter_with(_sibling_and_chips(*_mesh_pos()))
        start, finish = _gather_steps(refs[:n], refs[n:2 * n], *refs[2 * n:])
        start()
        finish()

    return pl.pallas_call(
        body, name=name,
        in_specs=[HBM_SPEC] * n, out_specs=[HBM_SPEC] * n,
        out_shape=[jax.ShapeDtypeStruct((N_DEV,) + s.shape, s.dtype) for s in shards],
        scratch_shapes=[pltpu.SemaphoreType.DMA((7 * n,)), pltpu.SemaphoreType.DMA((7 * n,)),
                        pltpu.SemaphoreType.DMA((n,))],
        compiler_params=_params(barrier_id=8),
    )(*shards)


def _adam_math(w, g, m, v):
    m = ADAM_B1 * m + (1.0 - ADAM_B1) * g
    v = ADAM_B2 * v + (1.0 - ADAM_B2) * (g * g)
    m_hat = m / (1.0 - ADAM_B1 ** ADAM_STEP)
    v_hat = v / (1.0 - ADAM_B2 ** ADAM_STEP)
    delta = -ADAM_LR * (m_hat / (jnp.sqrt(v_hat) + ADAM_EPS) + ADAM_WD * w)
    return delta, m, v


def _adamw_reduced(w, m, v, own, from_sibling, from_chips, tr):
    rows, cols = w.shape

    def body(w_ref, m_ref, v_ref, own_ref, sib_ref, far_ref, g_ref, d_ref, nm_ref, nv_ref):
        g = own_ref[...] + sib_ref[...].astype(F32)
        for k in range(len(CHIP_FLIPS)):
            g = g + far_ref[k].astype(F32)
        g_ref[...] = g
        d_ref[...], nm_ref[...], nv_ref[...] = _adam_math(w_ref[...], g, m_ref[...], v_ref[...])

    tile = pl.BlockSpec((tr, cols), lambda i: (i, 0))
    out = jax.ShapeDtypeStruct((rows, cols), F32)
    return pl.pallas_call(
        body, name="adamw_reduced", grid=(rows // tr,),
        in_specs=[tile] * 5 + [pl.BlockSpec((len(CHIP_FLIPS), tr, cols), lambda i: (0, i, 0))],
        out_specs=[tile] * 4, out_shape=[out] * 4,
        compiler_params=_params("parallel"),
    )(w, m, v, own, from_sibling, from_chips)


def _sum_devices(gathered):
    _, rows, cols = gathered.shape

    def body(g_ref, o_ref):
        s = g_ref[0]
        for d in range(1, N_DEV):
            s = s + g_ref[d]
        o_ref[...] = s

    return pl.pallas_call(
        body, name="sum_devices", in_specs=[_full(gathered.shape)], out_specs=_full((rows, cols)), grid=(1,),
        out_shape=jax.ShapeDtypeStruct((rows, cols), F32),
    )(gathered)


def _adamw_small(w, g, m, v):
    def body(w_ref, g_ref, m_ref, v_ref, d_ref, nm_ref, nv_ref):
        d_ref[...], nm_ref[...], nv_ref[...] = _adam_math(w_ref[...], g_ref[...], m_ref[...], v_ref[...])

    spec = _full(w.shape)
    out = jax.ShapeDtypeStruct(w.shape, F32)
    return pl.pallas_call(
        body, name="adamw_small", grid=(1,), in_specs=[spec] * 4, out_specs=[spec] * 3, out_shape=[out] * 3,
    )(w, g, m, v)


TOKEN_TILE = 512
MID_TILE = 256
ADAM_ROWS = 128


def _local_grads(x, target, g1, w_in_shard, conv_shard, sinks, g_attn, g_conv, g2, g3, g4, shards, order):
    t = x.shape[0]
    tm = min(TOKEN_TILE, t)
    rope = _rope_tables(t)
    qkv, gates, mconv, w_in, conv_w, gathered = _in_proj_fwd(x, g1, w_in_shard, conv_shard, g_conv, rope, tm, shards,
                                                             (False, True, False))
    attn, mattn, (w_out, w_up, w_down) = _attn_fwd(qkv, sinks, g_attn, shards, gathered)
    actt, dup, hn2t, dmo, dmix, dh, dmixed, small_mid = _mid(
        mattn, mconv, x, target, g2, g3, g4, w_out.reshape(D_MODEL, D_MODEL),
        w_up, w_down.reshape(D_FF, D_MODEL), min(MID_TILE, t))
    up_own, up_sib, up_sums = _dw_pair_sums((hn2t, dup), order, "up", "dw_up", 2)
    down_own, down_sib, down_sums = _dw_pair_sums((actt, dmo), order, "down", "dw_down", 3)
    out_own, out_sib, out_sums = _dw_pair_sums((mattn, mconv, dmix), order, "out", "dw_out", 4)
    dattn, dgates, small_mix, up_far = _mix_bwd(dmixed, attn, gates, g_attn, g_conv, conv_w, tm, up_sums)
    dqkv, dsink, down_far = _attn_bwd(qkv, dattn, sinks, rope, down_sums)
    grad_x, dw_in, small_in, out_far = _in_proj_bwd(dqkv, dgates, x, dh, g1, w_in, tm, out_sums)
    dw_out, dw_up, dw_down = (out_own, out_sib, out_far), (up_own, up_sib, up_far), (down_own, down_sib, down_far)
    return grad_x, dw_in, dw_out, dw_up, dw_down, (small_mid, small_mix, dsink, small_in)


def _pack_small(small_mid, small_mix, dsink, small_in):
    z = lambda n: jnp.zeros((1, n), F32)
    rows = [
        small_mid[ROW_LOSS:ROW_LOSS + 1],
        small_in[0:1],
        small_mid[ROW_G2:ROW_G2 + 1],
        small_mid[ROW_G3:ROW_G3 + 1],
        small_mid[ROW_G4:ROW_G4 + 1],
        jnp.concatenate([small_mix[ROW_GATTN:ROW_GATTN + 1], small_mix[ROW_GCONV:ROW_GCONV + 1]], axis=1),
        jnp.concatenate([small_mix[ROW_CW0:ROW_CW0 + 1], small_mix[ROW_CW0 + 1:ROW_CW0 + 2]], axis=1),
        jnp.concatenate([small_mix[ROW_CW0 + 2:ROW_CW0 + 3], dsink[0:1, :], z(D_MODEL - CONV_W - 128)], axis=1),
    ]
    return jnp.concatenate(rows, axis=0)


def kernel(x, pre_mix_norm, w_in, conv_w, attn_sinks, attn_group_norm, conv_group_norm, w_out, post_mix_norm, pre_mlp_norm, w_up, w_down, post_mlp_norm, loss_target, m_pre_mix_norm, m_w_in, m_conv_w, m_attn_sinks, m_attn_group_norm, m_conv_group_norm, m_w_out, m_post_mix_norm, m_pre_mlp_norm, m_w_up, m_w_down, m_post_mlp_norm, v_pre_mix_norm, v_w_in, v_conv_w, v_attn_sinks, v_attn_group_norm, v_conv_group_norm, v_w_out, v_post_mix_norm, v_pre_mlp_norm, v_w_up, v_w_down, v_post_mlp_norm):
    xi, yi, ci = _mesh_pos()
    chip = 2 * xi + yi
    dev = 2 * chip + ci

    order = _block_order(dev)

    shards = [w_out[0].astype(BF16), w_up[0].astype(BF16), w_down[0].astype(BF16)]

    grad_x, dw_in, dw_out, dw_up, dw_down, smalls = _local_grads(
        x[0], loss_target[0], pre_mix_norm, w_in[0].astype(BF16), conv_w[0], attn_sinks, attn_group_norm, conv_group_norm,
        post_mix_norm, pre_mlp_norm, post_mlp_norm, shards, order)

    small = _sum_devices(_all_gather([_pack_small(*smalls)], "gather_small")[0])
    loss = (0.5 / D_MODEL) * jnp.sum(small[0])

    big = {}
    for name, w, m, v, (own, sib, far) in zip(
            ("w_in", "w_out", "w_up", "w_down"), (w_in, w_out, w_up, w_down), (m_w_in, m_w_out, m_w_up, m_w_down),
            (v_w_in, v_w_out, v_w_up, v_w_down), (dw_in, dw_out, dw_up, dw_down)):
        big[name] = [a[None] for a in _adamw_reduced(w[0], m[0], v[0], own, sib, far, ADAM_ROWS)]

    conv_g = lax.dynamic_slice(
        jnp.stack([small[6, :CONV_W], small[6, CONV_W:], small[7, :CONV_W]]), (0, dev * (CONV_W // N_DEV)),
        (3, CONV_W // N_DEV))
    pad = lambda a, n: jnp.pad(a.reshape(1, -1), ((0, 0), (0, n - a.size)))
    small_names = ("pre_mix_norm", "post_mix_norm", "pre_mlp_norm", "post_mlp_norm")
    small_w = {"pre_mix_norm": (pre_mix_norm, m_pre_mix_norm, v_pre_mix_norm),
               "post_mix_norm": (post_mix_norm, m_post_mix_norm, v_post_mix_norm),
               "pre_mlp_norm": (pre_mlp_norm, m_pre_mlp_norm, v_pre_mlp_norm),
               "post_mlp_norm": (post_mlp_norm, m_post_mlp_norm, v_post_mlp_norm)}

    def pack(k):
        rows = [small_w[nm][k] for nm in small_names]
        rows.append(jnp.concatenate([(attn_group_norm, m_attn_group_norm, v_attn_group_norm)[k],
                                     (conv_group_norm, m_conv_group_norm, v_conv_group_norm)[k]], axis=1))
        rows.append(pad((conv_w, m_conv_w, v_conv_w)[k], D_MODEL))
        rows.append(pad((attn_sinks, m_attn_sinks, v_attn_sinks)[k], D_MODEL))
        rows.append(jnp.zeros((1, D_MODEL), F32))
        return jnp.concatenate(rows, axis=0)

    g_small = jnp.concatenate(
        [small[1:6], pad(conv_g, D_MODEL), pad(small[7, CONV_W:CONV_W + N_HEADS], D_MODEL), jnp.zeros((1, D_MODEL), F32)],
        axis=0)
    d_small, nm_small, nv_small = _adamw_small(pack(0), g_small, pack(1), pack(2))

    def unpack(a):
        nconv = 3 * CONV_W // N_DEV
        return {"pre_mix_norm": a[0:1], "post_mix_norm": a[1:2], "pre_mlp_norm": a[2:3], "post_mlp_norm": a[3:4],
                "attn_group_norm": a[4:5, :ATTN_W], "conv_group_norm": a[4:5, ATTN_W:],
                "conv_w": a[5, :nconv].reshape(1, 3, CONV_W // N_DEV), "attn_sinks": a[6:7, :N_HEADS]}

    order = ("pre_mix_norm", "w_in", "conv_w", "attn_sinks", "attn_group_norm", "conv_group_norm", "w_out",
             "post_mix_norm", "pre_mlp_norm", "w_up", "w_down", "post_mlp_norm")
    outs = []
    for k, a in enumerate((g_small, d_small, nm_small, nv_small)):
        sm = unpack(a)
        outs += [big[nm][k] if nm in big else sm[nm] for nm in order]
    return (loss, grad_x[None], *outs)
```

```python
import functools

import jax
import jax.numpy as jnp
import numpy as np
from jax import lax
from jax.experimental import pallas as pl
from jax.experimental.pallas import tpu as pltpu

F32 = jnp.float32
BF16 = jnp.bfloat16

D_MODEL = 1024
HEAD_DIM = 64
ATTN_W = 512
CONV_W = 512
N_HEADS = 8
N_KV = 2
GROUP = 4
KV_W = 128
QKV_W = ATTN_W + 2 * KV_W
GATES_W = 3 * CONV_W
IN_COLS = QKV_W + GATES_W
D_FF = 4096
FF_CHUNK = 512
N_FF_CHUNKS = D_FF // FF_CHUNK
BLOCK = 128
ROT_HALF = 8
ROPE_THETA = 500000.0
NORM_EPS = 1e-6
NEG_INF = -1e30
ATTN_SCALE = 0.125
N_DEV = 8
N_CHIPS = 4
IN_SHARD = IN_COLS // N_DEV

ADAM_LR = 0.001
ADAM_B1 = 0.9
ADAM_B2 = 0.999
ADAM_EPS = 1e-08
ADAM_WD = 0.01
ADAM_STEP = 10

V7X_VMEM_BYTES = 64 * 1024 * 1024
VMEM_LIMIT = V7X_VMEM_BYTES - 2 * 1024 * 1024

MESH = pl.DeviceIdType.MESH
HBM_SPEC = pl.BlockSpec(memory_space=pltpu.HBM)


def _params(*sem, barrier_id=None):
    return pltpu.CompilerParams(dimension_semantics=sem or None, vmem_limit_bytes=VMEM_LIMIT, collective_id=barrier_id)


def _mm(a, b):
    return jnp.dot(a, b, preferred_element_type=F32)


def _mm_nt(a, b):
    return lax.dot_general(a, b, (((1,), (1,)), ((), ())), preferred_element_type=F32)


def _mm_tn(a, b):
    return lax.dot_general(a, b, (((0,), (0,)), ((), ())), preferred_element_type=F32)


def _inv_rms(x):
    return lax.rsqrt(jnp.mean(x * x, axis=-1, keepdims=True) + NORM_EPS)


def _rms_bwd(xhat, r, gain, dy):
    gy = dy * gain
    return r * (gy - xhat * jnp.mean(gy * xhat, axis=-1, keepdims=True)), dy * xhat


def _colsum(a):
    return jnp.sum(a, axis=0, keepdims=True)


def _full(shape):
    zeros = (0,) * len(shape)
    return pl.BlockSpec(shape, lambda *_: zeros)


def _resident(shape):
    zeros = (0,) * len(shape)
    return pl.BlockSpec(shape, lambda *_: zeros, pipeline_mode=pl.Buffered(1))


def _rope_tables(t):
    pos = np.arange(t, dtype=np.float32)
    inv_freq = (ROPE_THETA ** (-np.arange(0, 2 * ROT_HALF, 2, dtype=np.float64) / (2 * ROT_HALF))).astype(np.float32)
    ang = (pos[:, None] * inv_freq[None, :]).astype(np.float64)
    cos, sin = np.cos(ang).astype(np.float32), np.sin(ang).astype(np.float32)
    zeros8 = np.zeros((t, ROT_HALF), np.float32)
    rest = np.zeros((t, HEAD_DIM - 2 * ROT_HALF), np.float32)
    c_head = np.concatenate([cos, cos, rest + 1.0], axis=1)
    s1_head = np.concatenate([zeros8, sin, rest], axis=1)
    s2_head = np.concatenate([-sin, zeros8, rest], axis=1)
    two = lambda a: jnp.asarray(np.concatenate([a, a], axis=1))
    return two(c_head), two(s1_head), two(s2_head)


def _rope(v, c, s1, s2):
    return v * c + pltpu.roll(v, ROT_HALF, 1) * s1 + pltpu.roll(v, 128 - ROT_HALF, 1) * s2


def _rope_transpose(dv, c, s1, s2):
    return dv * c + pltpu.roll(dv * s1, 128 - ROT_HALF, 1) + pltpu.roll(dv * s2, ROT_HALF, 1)


def _shift_rows_down(u, prev, k):
    row = lax.broadcasted_iota(jnp.int32, u.shape, 0)
    out = pltpu.roll(u, k, 0)
    for r in range(k):
        out = jnp.where(row == r, prev[8 - k + r:8 - k + r + 1, :], out)
    return out


def _shift_rows_up(u, nxt, k):
    n = u.shape[0]
    row = lax.broadcasted_iota(jnp.int32, u.shape, 0)
    out = pltpu.roll(u, n - k, 0)
    for r in range(k):
        out = jnp.where(row == n - k + r, nxt[r:r + 1, :], out)
    return out


def _conv3(u, u1, u2, w):
    return (w[0:1, :] * u2 + w[1:2, :] * u1) + w[2:3, :] * u


def _mesh_pos():
    return lax.axis_index("x"), lax.axis_index("y"), lax.axis_index("c")


def _slot(ref, pos):
    dev = 4 * pos[0] + 2 * pos[1] + pos[2]
    if len(ref.shape) == 2:
        width = ref.shape[1] // N_DEV
        return ref.at[:, pl.ds(pl.multiple_of(dev * width, width), width)]
    return ref.at[dev]


def _gathered_shape(shard, by_cols):
    if by_cols:
        return jax.ShapeDtypeStruct((shard.shape[0], N_DEV * shard.shape[1]), shard.dtype)
    return jax.ShapeDtypeStruct((N_DEV,) + shard.shape, shard.dtype)


def _enter_with(peers):
    barrier = pltpu.get_barrier_semaphore()
    for peer in peers:
        pl.semaphore_signal(barrier, inc=1, device_id=peer, device_id_type=MESH)
    pl.semaphore_wait(barrier, len(peers))


def _sibling_and_chips(x, y, c):
    return [(x, y, 1 - c), (1 - x, y, c), (x, 1 - y, c), (1 - x, 1 - y, c)]


def _push(src, dst, sems, k, to):
    send_sems, recv_sems = sems
    return pltpu.make_async_remote_copy(src_ref=src, dst_ref=dst, send_sem=send_sems.at[k], recv_sem=recv_sems.at[k],
                                        device_id=to, device_id_type=MESH)


def _gather_steps(shards, outs, send_sems, recv_sems, local_sems):
    n = len(shards)
    x, y, c = _mesh_pos()
    me, sibling = (x, y, c), (x, y, 1 - c)
    chips = [(1 - x, y), (x, 1 - y), (1 - x, 1 - y)]

    def copy(i, k, block, to, src=None):
        dst = _slot(outs[i], block)
        return _push(dst if src is None else src, dst, (send_sems, recv_sems), 7 * i + k, to)

    mine = [pltpu.make_async_copy(shards[i], _slot(outs[i], me), local_sems.at[i]) for i in range(n)]
    first = []
    for i in range(n):
        first.append(copy(i, 0, me, sibling, src=shards[i]))
        first += [copy(i, 1 + j, me, (*chip, c), src=shards[i]) for j, chip in enumerate(chips)]

    def start():
        for cp in mine + first:
            cp.start()

    def finish():
        passed = []
        for j, chip in enumerate(chips):
            for i in range(n):
                copy(i, 1 + j, (*chip, c), me).wait_recv()
                cp = copy(i, 4 + j, (*chip, c), sibling)
                cp.start()
                passed.append(cp)
        for i in range(n):
            copy(i, 0, sibling, me).wait_recv()
            for j, chip in enumerate(chips):
                copy(i, 4 + j, (*chip, 1 - c), me).wait_recv()
        for cp in first + passed:
            cp.wait_send()
        for cp in mine:
            cp.wait()

    return start, finish


def _gather_near(first, last, shards, outs, sems, local_sems):
    x, y, c = _mesh_pos()
    me, peers = (x, y, c), [(x, y, 1 - c), (1 - x, y, c), (x, 1 - y, c)]
    n = len(shards)
    local = [pltpu.make_async_copy(shards[i], _slot(outs[i], me), local_sems.at[i]) for i in range(n)]
    sends = [_push(shards[i], _slot(outs[i], me), sems, 3 * i + k, peers[k]) for i in range(n) for k in range(3)]
    arrivals = [_push(shards[i], _slot(outs[i], peers[k]), sems, 3 * i + k, peers[k]) for i in range(n) for k in range(3)]

    def start():
        for cp in local + sends:
            cp.start()

    if first is not None:
        pl.when(first)(start)

    @pl.when(last)
    def _():
        for cp in sends:
            cp.wait_send()
        for cp in arrivals:
            cp.wait_recv()
        for cp in local:
            cp.wait()

    return start


def _gather_far(first, last, shards, ins, outs, sems):
    x, y, c = _mesh_pos()
    me, sibling = (x, y, c), (x, y, 1 - c)
    chips = [(1 - x, y), (x, 1 - y), (1 - x, 1 - y)]
    n = len(shards)
    diag_send = [_push(shards[i], _slot(outs[i], me), sems, 4 * i, (*chips[2], c)) for i in range(n)]
    diag_arrival = [_push(shards[i], _slot(outs[i], (*chips[2], c)), sems, 4 * i, (*chips[2], c)) for i in range(n)]
    passed = [[_push(_slot(ins[i], (*chips[j], c)), _slot(outs[i], (*chips[j], c)), sems, 4 * i + 1 + j, sibling)
               for i in range(n)] for j in range(3)]
    from_sibling = [_push(shards[i], _slot(outs[i], (*chips[j], 1 - c)), sems, 4 * i + 1 + j, sibling)
                    for i in range(n) for j in range(3)]

    @pl.when(first)
    def _():
        for cp in diag_send + passed[0] + passed[1]:
            cp.start()

    @pl.when(last)
    def _():
        for cp in diag_arrival:
            cp.wait_recv()
        for cp in passed[2]:
            cp.start()
        for cp in from_sibling:
            cp.wait_recv()
        for cp in diag_send + passed[0] + passed[1] + passed[2]:
            cp.wait_send()


def _in_proj_fwd(x, g1, w_in, conv_w, g_conv, rope, tm, shards, by_cols):
    t = x.shape[0]
    rc, rs1, rs2 = rope
    n = len(shards)
    n_tiles = t // tm

    def body(*refs):
        x_ref, g1_ref, w_ref, cw_ref, gc_ref, c_ref, s1_ref, s2_ref = refs[:8]
        shard_refs = refs[8:8 + n]
        qkv_ref, gates_ref, mconv_ref, w_full_ref, cw_full_ref = refs[8 + n:13 + n]
        gathered = refs[13 + n:13 + 2 * n]
        carry_ref, w_land, cw_land, hn_ref = refs[13 + 2 * n:17 + 2 * n]
        now_sems = refs[17 + 2 * n:20 + 2 * n]
        step = pl.program_id(0)
        start_later_weights = _gather_near(None, step == 2 * n_tiles - 1, shard_refs, gathered,
                                           refs[20 + 2 * n:22 + 2 * n], refs[22 + 2 * n]) if n else None
        start_w_in, finish_w_in = _gather_steps([w_ref, cw_ref], [w_land, cw_land], *now_sems)

        @pl.when(step == 0)
        def _():
            carry_ref[...] = jnp.zeros_like(carry_ref)
            _enter_with(_sibling_and_chips(*_mesh_pos()))
            start_w_in()
            if start_later_weights is not None:
                start_later_weights()

        @pl.when(step < n_tiles)
        def _():
            xv = x_ref[...]
            hn_ref[step] = ((xv * _inv_rms(xv)) * g1_ref[...]).astype(BF16)

        @pl.when(step == n_tiles)
        def _():
            finish_w_in()
            conv_shard = CONV_W // N_DEV
            for d in range(N_DEV):
                w_full_ref[:, IN_SHARD * d:IN_SHARD * (d + 1)] = w_land[d]
                cw_full_ref[:, conv_shard * d:conv_shard * (d + 1)] = cw_land[d]

        @pl.when(step >= n_tiles)
        def _():
            proj = _mm(hn_ref[step - n_tiles], w_full_ref[...])
            c, s1, s2 = c_ref[...], s1_ref[...], s2_ref[...]
            for ci in range((ATTN_W + KV_W) // 128):
                sl = slice(128 * ci, 128 * (ci + 1))
                qkv_ref[:, sl] = _rope(proj[:, sl], c, s1, s2).astype(BF16)
            qkv_ref[:, ATTN_W + KV_W:QKV_W] = proj[:, ATTN_W + KV_W:QKV_W].astype(BF16)
            gates = proj[:, QKV_W:]
            gates_ref[...] = gates
            gb, gcc, xin = gates[:, :CONV_W], gates[:, CONV_W:2 * CONV_W], gates[:, 2 * CONV_W:]
            u = gcc * xin
            prev = carry_ref[...]
            conv = gb * _conv3(u, _shift_rows_down(u, prev, 1), _shift_rows_down(u, prev, 2), cw_full_ref[...])
            carry_ref[...] = u[tm - 8:tm, :]
            mconv_ref[...] = ((conv * _inv_rms(conv)) * gc_ref[...]).astype(BF16)

    first_pass = pl.BlockSpec((tm, D_MODEL), lambda i: (jnp.minimum(i, n_tiles - 1), 0))
    tile = lambda w_: pl.BlockSpec((tm, w_), lambda i: (jnp.maximum(i - n_tiles, 0), 0))
    sems = lambda k: pltpu.SemaphoreType.DMA((k,))
    res = pl.pallas_call(
        body, name="in_proj_fwd", grid=(2 * n_tiles,),
        in_specs=[first_pass, _full((1, D_MODEL)), HBM_SPEC, HBM_SPEC, _full((1, CONV_W)), tile(128), tile(128),
                  tile(128)] + [HBM_SPEC] * n,
        out_specs=[tile(QKV_W), tile(GATES_W), tile(CONV_W), _full((D_MODEL, IN_COLS)), _full((3, CONV_W))]
        + [HBM_SPEC] * n,
        out_shape=[jax.ShapeDtypeStruct((t, QKV_W), BF16), jax.ShapeDtypeStruct((t, GATES_W), F32),
                   jax.ShapeDtypeStruct((t, CONV_W), BF16), jax.ShapeDtypeStruct((D_MODEL, IN_COLS), BF16),
                   jax.ShapeDtypeStruct((3, CONV_W), F32)]
        + [_gathered_shape(s, cols) for s, cols in zip(shards, by_cols)],
        scratch_shapes=[pltpu.VMEM((8, CONV_W), F32), pltpu.VMEM((N_DEV,) + w_in.shape, BF16),
                        pltpu.VMEM((N_DEV,) + conv_w.shape, F32), pltpu.VMEM((n_tiles, tm, D_MODEL), BF16),
                        sems(14), sems(14), sems(2)]
        + ([sems(3 * n), sems(3 * n), sems(n)] if n else []),
        compiler_params=_params("arbitrary", barrier_id=0),
    )(x, g1, w_in, conv_w, g_conv, rc, rs1, rs2, *shards)
    return res[0], res[1], res[2], res[3], res[4], list(res[5:])


GROUP_COLS = GROUP * BLOCK
ATTN_STEP_BLOCKS = 4


def _attn_masks(has_prev):
    key = lax.broadcasted_iota(jnp.int32, (2 * BLOCK, GROUP_COLS), 0)
    query = lax.broadcasted_iota(jnp.int32, (2 * BLOCK, GROUP_COLS), 1) & (BLOCK - 1)
    band = (key > query) & (key <= query + BLOCK)
    return [band & ((key >= BLOCK) | has_prev)] + [band] * (ATTN_STEP_BLOCKS - 1)


def _heads_side_by_side(at, g, b):
    heads = [at[HEAD_DIM * (GROUP * g + hh):HEAD_DIM * (GROUP * g + hh + 1), BLOCK * b:BLOCK * (b + 1)] for hh in range(GROUP)]
    return jnp.concatenate(heads, axis=1)


def _to_token_rows(parts):
    rows = [jnp.concatenate([parts[b][g][:, BLOCK * hh:BLOCK * (hh + 1)] for b in range(ATTN_STEP_BLOCKS)], axis=1)
            for g in range(N_KV) for hh in range(GROUP)]
    return jnp.concatenate(rows, axis=0).T


def _group_sinks(sink_ref, g):
    head = lax.broadcasted_iota(jnp.int32, (1, GROUP_COLS), 1) // BLOCK
    out = jnp.full((1, GROUP_COLS), sink_ref[0, GROUP * g], F32)
    for hh in range(1, GROUP):
        out = jnp.where(head == hh, sink_ref[0, GROUP * g + hh], out)
    return out


def _attn_probs(qt, kk, sink, valid):
    s = jnp.where(valid, _mm(kk, qt), NEG_INF)
    m = jnp.maximum(jnp.max(s, axis=0, keepdims=True), sink)
    p = jnp.exp(s - m)
    psink = jnp.exp(sink - m)
    inv_l = 1.0 / (jnp.sum(p, axis=0, keepdims=True) + psink)
    return p * inv_l, psink * inv_l


ATTN_STEP = ATTN_STEP_BLOCKS * BLOCK
ATTN_KEYS = ATTN_STEP + BLOCK


def _qkv_specs(order):
    prev = lambda i: jnp.maximum(ATTN_STEP_BLOCKS * order(i) - 1, 0)
    kcol, vcol = ATTN_W // KV_W, ATTN_W // KV_W + 1
    return [pl.BlockSpec((ATTN_STEP, ATTN_W), lambda i: (order(i), 0)),
            pl.BlockSpec((BLOCK, KV_W), lambda i: (prev(i), kcol)), pl.BlockSpec((ATTN_STEP, KV_W), lambda i: (order(i), kcol)),
            pl.BlockSpec((BLOCK, KV_W), lambda i: (prev(i), vcol)), pl.BlockSpec((ATTN_STEP, KV_W), lambda i: (order(i), vcol))]


def _attn_fwd(qkv, sinks, g_attn, shards, gathered):
    t = qkv.shape[0]
    n = len(shards)

    def body(*refs):
        sink_ref, q_ref, kp_ref, kc_ref, vp_ref, vc_ref, ga_ref = refs[:7]
        attn_ref, mattn_ref = refs[7 + 2 * n:9 + 2 * n]
        step = pl.program_id(0)
        if n:
            @pl.when(step == 0)
            def _():
                x, y, c = _mesh_pos()
                _enter_with([(x, y, 1 - c), (1 - x, 1 - y, c)])

            _gather_far(step == 0, step == pl.num_programs(0) - 1, refs[7:7 + n], refs[7 + n:7 + 2 * n],
                        refs[9 + 2 * n:9 + 3 * n], refs[9 + 3 * n:11 + 3 * n])
        qt = (q_ref[...] * ATTN_SCALE).T
        keys = jnp.concatenate([kp_ref[...], kc_ref[...]], axis=0)
        vals = jnp.concatenate([vp_ref[...], vc_ref[...]], axis=0)
        sink = [_group_sinks(sink_ref, g) for g in range(N_KV)]
        masks = _attn_masks(step > 0)
        parts = []
        for b in range(ATTN_STEP_BLOCKS):
            window = slice(BLOCK * b, BLOCK * (b + 2))
            valid = masks[b]
            parts.append([])
            for g in range(N_KV):
                gs = slice(HEAD_DIM * g, HEAD_DIM * (g + 1))
                probs, _ = _attn_probs(_heads_side_by_side(qt, g, b), keys[window, gs], sink[g], valid)
                parts[b].append(_mm_tn(vals[window, gs], probs.astype(BF16)))
        attn = _to_token_rows(parts)
        attn_ref[...] = attn
        mattn_ref[...] = ((attn * _inv_rms(attn)) * ga_ref[...]).astype(BF16)

    blk = pl.BlockSpec((ATTN_STEP, ATTN_W), lambda j: (j, 0))
    res = pl.pallas_call(
        body, name="attn_fwd", grid=(t // ATTN_STEP,),
        in_specs=[pl.BlockSpec(memory_space=pltpu.SMEM)] + _qkv_specs(lambda j: j) + [_full((1, ATTN_W))]
        + [HBM_SPEC] * (2 * n),
        out_specs=[blk, blk] + [HBM_SPEC] * n,
        out_shape=[jax.ShapeDtypeStruct((t, ATTN_W), F32), jax.ShapeDtypeStruct((t, ATTN_W), BF16)]
        + [jax.ShapeDtypeStruct(g.shape, g.dtype) for g in gathered],
        input_output_aliases={7 + n + i: 2 + i for i in range(n)},
        scratch_shapes=[pltpu.SemaphoreType.DMA((4 * n,)), pltpu.SemaphoreType.DMA((4 * n,))] if n else [],
        compiler_params=_params("arbitrary", barrier_id=1 if n else None),
    )(sinks, qkv, qkv, qkv, qkv, qkv, g_attn, *shards, *gathered)
    return res[0], res[1], list(res[2:])


SMALL_ROWS = 8
ROW_LOSS, ROW_G2, ROW_G3, ROW_G4 = 0, 1, 2, 3


def _mid(mattn, mconv, x, target, g2, g3, g4, w_out, w_up, w_down, tm):
    t = x.shape[0]

    def body(ma_ref, mc_ref, x_ref, t_ref, g2_ref, g3_ref, g4_ref, wo_ref, wu_ref, wd_ref,
             act_ref, dup_ref, hn2t_ref, dmo_ref, dmix_ref, dh_ref, dmixed_ref, small_ref, up_ref):
        @pl.when(pl.program_id(0) == 0)
        def _():
            small_ref[...] = jnp.zeros_like(small_ref)

        g2, g3, g4 = g2_ref[...], g3_ref[...], g4_ref[...]
        mix_out = _mm(ma_ref[...], wo_ref[0:ATTN_W, :]) + _mm(mc_ref[...], wo_ref[ATTN_W:, :])
        r2 = _inv_rms(mix_out)
        mo_hat = mix_out * r2
        h = x_ref[...] + mo_hat * g2
        r3 = _inv_rms(h)
        h_hat = h * r3
        hn2 = (h_hat * g3).astype(BF16)
        hn2t_ref[...] = hn2.T
        up = jnp.maximum(_mm(hn2, wu_ref[...]), 0.0)
        up_ref[...] = up.astype(BF16)
        act = (up * up).astype(BF16)
        act_ref[...] = act
        mlp = _mm(act, wd_ref[...])
        r4 = _inv_rms(mlp)
        ml_hat = mlp * r4
        err = (h + ml_hat * g4) - t_ref[...]
        d_out = err * (1.0 / D_MODEL)
        d_mlp, dg4 = _rms_bwd(ml_hat, r4, g4, d_out)
        dmo = d_mlp.astype(BF16)
        dmo_ref[...] = dmo
        dup = (_mm_nt(dmo, wd_ref[...]) * (2.0 * up_ref[...].astype(F32))).astype(BF16)
        dup_ref[...] = dup
        dhn2 = _mm_nt(dup, wu_ref[...])
        dh_norm, dg3 = _rms_bwd(h_hat, r3, g3, dhn2)
        dh = d_out + dh_norm
        dh_ref[...] = dh
        d_mix, dg2 = _rms_bwd(mo_hat, r2, g2, dh)
        dmix = d_mix.astype(BF16)
        dmix_ref[...] = dmix
        dmixed_ref[...] = _mm_nt(dmix, wo_ref[...])
        small_ref[ROW_LOSS:ROW_LOSS + 1, :] += _colsum(err * err)
        small_ref[ROW_G2:ROW_G2 + 1, :] += _colsum(dg2)
        small_ref[ROW_G3:ROW_G3 + 1, :] += _colsum(dg3)
        small_ref[ROW_G4:ROW_G4 + 1, :] += _colsum(dg4)

    tile = lambda n: pl.BlockSpec((tm, n), lambda i: (i, 0))
    cols = lambda n: pl.BlockSpec((n, tm), lambda i: (0, i))
    gain = _full((1, D_MODEL))
    return pl.pallas_call(
        body, name="mid_fwd_bwd", grid=(t // tm,),
        in_specs=[tile(ATTN_W), tile(CONV_W), tile(D_MODEL), tile(D_MODEL), gain, gain, gain,
                  _resident((D_MODEL, D_MODEL)), _resident((D_MODEL, D_FF)), _resident((D_FF, D_MODEL))],
        out_specs=[tile(D_FF), tile(D_FF), cols(D_MODEL), tile(D_MODEL), tile(D_MODEL), tile(D_MODEL), tile(D_MODEL),
                   _full((SMALL_ROWS, D_MODEL))],
        out_shape=[jax.ShapeDtypeStruct((t, D_FF), BF16), jax.ShapeDtypeStruct((t, D_FF), BF16),
                   jax.ShapeDtypeStruct((D_MODEL, t), BF16), jax.ShapeDtypeStruct((t, D_MODEL), BF16),
                   jax.ShapeDtypeStruct((t, D_MODEL), BF16), jax.ShapeDtypeStruct((t, D_MODEL), F32),
                   jax.ShapeDtypeStruct((t, D_MODEL), F32), jax.ShapeDtypeStruct((SMALL_ROWS, D_MODEL), F32)],
        scratch_shapes=[pltpu.VMEM((tm, D_FF), BF16)],
        compiler_params=_params("arbitrary"),
    )(mattn, mconv, x, target, g2, g3, g4, w_out, w_up, w_down)


CHIP_FLIPS = ((1, 1), (1, 0), (0, 1))


def _block_order(dev):
    chip_masks = [4 * fx + 2 * fy for fx, fy in CHIP_FLIPS]
    masks = [m + 1 for m in chip_masks] + [1] + chip_masks + [0]
    return jnp.bitwise_xor(dev, jnp.asarray(masks, jnp.int32)).astype(jnp.int32)


def _other_chips(x, y, c):
    return [(1 - x if fx else x, 1 - y if fy else y, c) for fx, fy in CHIP_FLIPS]


def _dw_pair_sums(operands, order, which, name, barrier_id, ride=None):
    t = operands[-1].shape[0]
    n_far = len(CHIP_FLIPS)
    n_in = len(operands)
    n_ride = 0 if ride is None else 1
    out_chunk = D_MODEL // N_DEV
    if which == "up":
        rows, cols = D_MODEL, FF_CHUNK
        in_specs = [_resident((D_MODEL, t)), pl.BlockSpec((t, FF_CHUNK), lambda s, order_ref: (0, order_ref[s]))]
    elif which == "down":
        rows, cols = FF_CHUNK, D_MODEL
        in_specs = [pl.BlockSpec((t, FF_CHUNK), lambda s, order_ref: (0, order_ref[s])), _resident((t, D_MODEL))]
    else:
        rows, cols = out_chunk, D_MODEL
        half = pl.BlockSpec((t, out_chunk), lambda s, order_ref: (0, order_ref[s] % (N_DEV // 2)))
        in_specs = [half, half, _resident((t, D_MODEL))]

    def body(order_ref, *refs):
        own_ref, from_sib_ref, pair_ref = refs[n_in + n_ride:n_in + n_ride + 3]
        send_buf, land_buf, send_sems, recv_sems = refs[n_in + 2 * n_ride + 3:n_in + 2 * n_ride + 7]
        s_now = pl.program_id(0)
        x, y, c = _mesh_pos()
        sibling = (x, y, 1 - c)
        sems = (send_sems, recv_sems)

        @pl.when(s_now == 0)
        def _():
            _enter_with([sibling] + (_other_chips(x, y, c) if n_ride else []))

        if n_ride:
            _chip_exchange_beside(s_now == 0, s_now == N_DEV - 1, [refs[n_in]], [refs[n_in + 3 + n_ride]],
                                  refs[n_in + 2 * n_ride + 7:], enter=False)

        def hand_over(k):
            dst = land_buf.at[k] if k < n_far else from_sib_ref
            return _push(send_buf.at[k], dst, sems, k, sibling)

        if which == "out":
            ma_ref, mc_ref, b_ref = refs[:n_in]
            block = lax.cond(order_ref[s_now] < N_DEV // 2, lambda: _mm_tn(ma_ref[...], b_ref[...]),
                             lambda: _mm_tn(mc_ref[...], b_ref[...]))
        elif which == "down":
            block = _mm_tn(refs[0][...], refs[1][...])
        else:
            block = _mm(refs[0][...], refs[1][...])
        for k in range(n_far + 1):
            @pl.when(s_now == k)
            def _():
                send_buf[k] = block.astype(BF16)
                hand_over(k).start()

        for k in range(n_far):
            @pl.when(s_now == n_far + 1 + k)
            def _():
                hand_over(k).wait_recv()
                pair_ref[...] = (block + land_buf[k].astype(F32)).astype(BF16)

        @pl.when(s_now == N_DEV - 1)
        def _():
            own_ref[...] = block
            for k in range(n_far + 1):
                hand_over(k).wait_send()
            hand_over(n_far).wait_recv()

    rides = [] if ride is None else [ride]
    sems = lambda k: pltpu.SemaphoreType.DMA((k,))
    return pl.pallas_call(
        body, name=name,
        grid_spec=pltpu.PrefetchScalarGridSpec(
            num_scalar_prefetch=1, grid=(N_DEV,), in_specs=in_specs + [HBM_SPEC] * n_ride,
            out_specs=[pl.BlockSpec((rows, cols), lambda s, order_ref: (0, 0)), HBM_SPEC,
                       pl.BlockSpec((None, rows, cols), lambda s, order_ref: (jnp.clip(s - n_far - 1, 0, n_far - 1), 0, 0))]
            + [HBM_SPEC] * n_ride,
            scratch_shapes=[pltpu.VMEM((n_far + 1, rows, cols), BF16), pltpu.VMEM((n_far, rows, cols), BF16),
                            sems(n_far + 1), sems(n_far + 1)] + [sems(n_far), sems(n_far)] * n_ride),
        out_shape=[jax.ShapeDtypeStruct((rows, cols), F32), jax.ShapeDtypeStruct((rows, cols), BF16),
                   jax.ShapeDtypeStruct((n_far, rows, cols), BF16)]
        + [jax.ShapeDtypeStruct(r.shape, r.dtype) for r in rides],
        compiler_params=_params("arbitrary", barrier_id=barrier_id),
    )(order, *operands, *rides)


def _chip_exchange_beside(first, last, sums, outs, sems, enter=True):
    chips = _other_chips(*_mesh_pos())
    copies = [_push(sums[i].at[k], outs[i].at[k], sems, len(chips) * i + k, chip)
              for i in range(len(sums)) for k, chip in enumerate(chips)]

    @pl.when(first)
    def _():
        if enter:
            _enter_with(chips)
        for cp in copies:
            cp.start()

    @pl.when(last)
    def _():
        for cp in copies:
            cp.wait()


ROW_GATTN, ROW_GCONV, ROW_CW0 = 0, 1, 2


def _mix_bwd(dmixed, attn, gates, g_attn, g_conv, conv_w, tm):
    t = attn.shape[0]
    n = t // tm
    rev = lambda i: n - 1 - i

    def body(dm_ref, attn_ref, gates_ref, gprev_ref, ga_ref, gc_ref, cw_ref, dattn_ref, dgates_ref, small_ref, carry_ref):
        i = pl.program_id(0)

        @pl.when(i == 0)
        def _():
            small_ref[...] = jnp.zeros_like(small_ref)
            carry_ref[...] = jnp.zeros_like(carry_ref)

        dm = dm_ref[...]
        a = attn_ref[...]
        ra = _inv_rms(a)
        a_hat = a * ra
        dattn, dga = _rms_bwd(a_hat, ra, ga_ref[...], dm[:, :ATTN_W])
        dattn_ref[...] = dattn

        gates = gates_ref[...]
        gb, gcc, xin = gates[:, :CONV_W], gates[:, CONV_W:2 * CONV_W], gates[:, 2 * CONV_W:]
        u = gcc * xin
        gp = gprev_ref[...]
        uprev = jnp.where(rev(i) == 0, 0.0, gp[:, CONV_W:2 * CONV_W] * gp[:, 2 * CONV_W:])
        u1, u2 = _shift_rows_down(u, uprev, 1), _shift_rows_down(u, uprev, 2)
        w = cw_ref[...]
        c = _conv3(u, u1, u2, w)
        conv = gb * c
        rcv = _inv_rms(conv)
        c_hat = conv * rcv
        dconv, dgc = _rms_bwd(c_hat, rcv, gc_ref[...], dm[:, ATTN_W:])
        dc = dconv * gb
        nxt = carry_ref[...]
        du = (w[2:3, :] * dc + w[1:2, :] * _shift_rows_up(dc, nxt, 1)) + w[0:1, :] * _shift_rows_up(dc, nxt, 2)
        carry_ref[...] = dc[0:8, :]
        dgates_ref[:, :CONV_W] = (dconv * c).astype(BF16)
        dgates_ref[:, CONV_W:2 * CONV_W] = (du * xin).astype(BF16)
        dgates_ref[:, 2 * CONV_W:] = (du * gcc).astype(BF16)
        small_ref[ROW_GATTN:ROW_GATTN + 1, :] += _colsum(dga)
        small_ref[ROW_GCONV:ROW_GCONV + 1, :] += _colsum(dgc)
        small_ref[ROW_CW0:ROW_CW0 + 1, :] += _colsum(dc * u2)
        small_ref[ROW_CW0 + 1:ROW_CW0 + 2, :] += _colsum(dc * u1)
        small_ref[ROW_CW0 + 2:ROW_CW0 + 3, :] += _colsum(dc * u)

    tile = lambda w_: pl.BlockSpec((tm, w_), lambda i: (rev(i), 0))
    prev8 = pl.BlockSpec((8, GATES_W), lambda i: (jnp.maximum(rev(i) * (tm // 8) - 1, 0), 0))
    return pl.pallas_call(
        body, name="mix_bwd", grid=(n,),
        in_specs=[tile(D_MODEL), tile(ATTN_W), tile(GATES_W), prev8, _full((1, ATTN_W)), _full((1, CONV_W)),
                  _full((3, CONV_W))],
        out_specs=[tile(ATTN_W), tile(GATES_W), _full((SMALL_ROWS, CONV_W))],
        out_shape=[jax.ShapeDtypeStruct((t, ATTN_W), F32), jax.ShapeDtypeStruct((t, GATES_W), BF16),
                   jax.ShapeDtypeStruct((SMALL_ROWS, CONV_W), F32)],
        scratch_shapes=[pltpu.VMEM((8, CONV_W), F32)],
        compiler_params=_params("arbitrary"),
    )(dmixed, attn, gates, gates, g_attn, g_conv, conv_w)


def _attn_bwd(qkv, dattn, sinks, rope, sums):
    t = qkv.shape[0]
    n_steps = t // ATTN_STEP
    rev = lambda i: n_steps - 1 - i
    rc, rs1, rs2 = rope

    def body(sink_ref, q_ref, kp_ref, kc_ref, vp_ref, vc_ref, do_ref, c_ref, s1_ref, s2_ref, sums_ref,
             dqkv_ref, dsink_ref, arrived_ref, ck_ref, cv_ref, kacc_ref, vacc_ref, send_sems, recv_sems):
        i = pl.program_id(0)
        _chip_exchange_beside(i == 0, i == n_steps - 1, [sums_ref], [arrived_ref], (send_sems, recv_sems))

        @pl.when(i == 0)
        def _():
            dsink_ref[...] = jnp.zeros_like(dsink_ref)
            ck_ref[...] = jnp.zeros_like(ck_ref)
            cv_ref[...] = jnp.zeros_like(cv_ref)

        kacc_ref[...] = jnp.zeros_like(kacc_ref)
        vacc_ref[...] = jnp.zeros_like(vacc_ref)
        qt = (q_ref[...] * ATTN_SCALE).T
        dot = do_ref[...].astype(BF16).T
        keys = jnp.concatenate([kp_ref[...], kc_ref[...]], axis=0)
        vals = jnp.concatenate([vp_ref[...], vc_ref[...]], axis=0)
        sink = [_group_sinks(sink_ref, g) for g in range(N_KV)]
        c, s1, s2 = c_ref[...], s1_ref[...], s2_ref[...]
        lane = lax.broadcasted_iota(jnp.int32, (1, 128), 1)
        dsink = jnp.zeros((1, 128), F32)
        masks = _attn_masks(rev(i) > 0)
        dq_parts = []
        for b in range(ATTN_STEP_BLOCKS):
            window = slice(BLOCK * b, BLOCK * (b + 2))
            valid = masks[b]
            dq_parts.append([])
            dk_parts, dv_parts = [], []
            for g in range(N_KV):
                gs = slice(HEAD_DIM * g, HEAD_DIM * (g + 1))
                kk, vv = keys[window, gs], vals[window, gs]
                qtg, dotg = _heads_side_by_side(qt, g, b), _heads_side_by_side(dot, g, b)
                probs, psink = _attn_probs(qtg, kk, sink[g], valid)
                dp = _mm(vv, dotg)
                delta = jnp.sum(probs * dp, axis=0, keepdims=True)
                ds = (probs * (dp - delta)).astype(BF16)
                sink_terms = psink * delta
                for hh in range(GROUP):
                    head_sum = jnp.sum(sink_terms[:, BLOCK * hh:BLOCK * (hh + 1)])
                    dsink = dsink + jnp.where(lane == GROUP * g + hh, -head_sum, 0.0)
                dq_parts[b].append(_mm_tn(kk * ATTN_SCALE, ds))
                dk_parts.append(_mm_nt(ds, qtg))
                dv_parts.append(_mm_nt(probs.astype(BF16), dotg))
            kacc_ref[window, :] += jnp.concatenate(dk_parts, axis=1)
            vacc_ref[window, :] += jnp.concatenate(dv_parts, axis=1)
        dq = _to_token_rows(dq_parts)
        for ci in range(ATTN_W // 128):
            sl = slice(128 * ci, 128 * (ci + 1))
            dqkv_ref[:, sl] = _rope_transpose(dq[:, sl], c, s1, s2).astype(BF16)
        kacc_ref[ATTN_STEP:, :] += ck_ref[...]
        vacc_ref[ATTN_STEP:, :] += cv_ref[...]
        ck_ref[...] = kacc_ref[:BLOCK, :]
        cv_ref[...] = vacc_ref[:BLOCK, :]
        dqkv_ref[:, ATTN_W:ATTN_W + KV_W] = _rope_transpose(kacc_ref[BLOCK:, :], c, s1, s2).astype(BF16)
        dqkv_ref[:, ATTN_W + KV_W:] = vacc_ref[BLOCK:, :].astype(BF16)
        dsink_ref[0:1, :] += dsink

    blk = lambda w_: pl.BlockSpec((ATTN_STEP, w_), lambda i: (rev(i), 0))
    return pl.pallas_call(
        body, name="attn_bwd", grid=(n_steps,),
        in_specs=[pl.BlockSpec(memory_space=pltpu.SMEM)] + _qkv_specs(rev) + [blk(ATTN_W), blk(128), blk(128), blk(128),
                                                                              HBM_SPEC],
        out_specs=[blk(QKV_W), _full((8, 128)), HBM_SPEC],
        out_shape=[jax.ShapeDtypeStruct((t, QKV_W), BF16), jax.ShapeDtypeStruct((8, 128), F32),
                   jax.ShapeDtypeStruct(sums.shape, sums.dtype)],
        scratch_shapes=[pltpu.VMEM((BLOCK, KV_W), F32), pltpu.VMEM((BLOCK, KV_W), F32),
                        pltpu.VMEM((ATTN_KEYS, KV_W), F32), pltpu.VMEM((ATTN_KEYS, KV_W), F32),
                        pltpu.SemaphoreType.DMA((len(CHIP_FLIPS),)), pltpu.SemaphoreType.DMA((len(CHIP_FLIPS),))],
        compiler_params=_params("arbitrary", barrier_id=6),
    )(sinks, qkv, qkv, qkv, qkv, qkv, dattn, rc, rs1, rs2, sums)


def _grad_x_tile(dq, dg, x_hat, r, g1, w_ref, dh):
    dhn = _mm_nt(dq, w_ref[:, :QKV_W]) + _mm_nt(dg, w_ref[:, QKV_W:])
    dx, dg1 = _rms_bwd(x_hat, r, g1, dhn)
    return dh + dx, _colsum(dg1)


def _in_proj_bwd(dqkv, dgates, x, dh, g1, w_in, tm, out_sums):
    t = x.shape[0]
    n = t // tm
    n_steps = 2 * n
    n_far = len(CHIP_FLIPS)
    shard = (D_MODEL, IN_SHARD)

    def body(dq_ref, dg_ref, x_ref, dh_ref, g1_ref, w_ref, osums_ref,
             dx_ref, own_ref, sib_ref, far_ref, dg1_ref, oarrived_ref,
             acc_ref, send_buf, land_buf, pair_buf, d2d_send, d2d_recv, ici_send, ici_recv, o_send, o_recv):
        i = pl.program_id(0)
        x_pos, y_pos, c = _mesh_pos()
        my_chip = 2 * x_pos + y_pos
        sibling = (x_pos, y_pos, 1 - c)
        @pl.when(i == 0)
        def _():
            _enter_with(_sibling_and_chips(x_pos, y_pos, c))

        _chip_exchange_beside(i == 0, i == n_steps - 1, [osums_ref], [oarrived_ref], (o_send, o_recv), enter=False)

        def cols(d):
            return slice(IN_SHARD * d, IN_SHARD * (d + 1))

        def hand_over(chip):
            return _push(send_buf.at[chip], land_buf.at[chip], (d2d_send, d2d_recv), chip, sibling)

        def to_chip(chip, rel):
            return pltpu.make_async_remote_copy(
                src_ref=pair_buf.at[chip], dst_ref=far_ref.at[rel - 1], send_sem=ici_send.at[rel - 1],
                recv_sem=ici_recv.at[rel - 1], device_id=(chip // 2, chip % 2, c), device_id_type=MESH)

        @pl.when(i == 0)
        def _():
            acc_ref[...] = jnp.zeros_like(acc_ref)
            dg1_ref[...] = jnp.zeros_like(dg1_ref)

        def normed_x():
            xv = x_ref[...]
            r = _inv_rms(xv)
            return xv * r, r

        @pl.when(i < n)
        def _():
            hn = (normed_x()[0] * g1_ref[...]).astype(BF16)
            acc_ref[:, :QKV_W] += _mm_tn(hn, dq_ref[...])
            acc_ref[:, QKV_W:] += _mm_tn(hn, dg_ref[...])

        @pl.when(i == n - 1)
        def _():
            for d in range(N_DEV):
                @pl.when(d % 2 != c)
                def _():
                    send_buf[d // 2] = acc_ref[:, cols(d)].astype(BF16)
                    hand_over(d // 2).start()
            for d in range(N_DEV):
                chip = d // 2

                @pl.when(d % 2 == c)
                def _():
                    hand_over(chip).wait_recv()

                    @pl.when(chip == my_chip)
                    def _():
                        own_ref[...] = acc_ref[:, cols(d)]
                        sib_ref[...] = land_buf[chip]

                    @pl.when(chip != my_chip)
                    def _():
                        pair_buf[chip] = (acc_ref[:, cols(d)] + land_buf[chip].astype(F32)).astype(BF16)
                        to_chip(chip, chip ^ my_chip).start()
            for chip in range(N_CHIPS):
                hand_over(chip).wait_send()

        @pl.when(i >= n)
        def _():
            x_hat, r = normed_x()
            dx_ref[...], dg1 = _grad_x_tile(dq_ref[...], dg_ref[...], x_hat, r, g1_ref[...], w_ref, dh_ref[...])
            dg1_ref[0:1, :] += dg1

        @pl.when(i == n_steps - 1)
        def _():
            for rel in range(1, n_far + 1):
                to_chip(0, rel).wait()

    both = lambda w_: pl.BlockSpec((tm, w_), lambda i: (i % n, 0))
    second = pl.BlockSpec((tm, D_MODEL), lambda i: (jnp.maximum(i - n, 0), 0))
    whole = lambda dtype: jax.ShapeDtypeStruct(shard, dtype)
    sems = lambda k: pltpu.SemaphoreType.DMA((k,))
    res = pl.pallas_call(
        body, name="in_proj_bwd", grid=(n_steps,),
        in_specs=[both(QKV_W), both(GATES_W), both(D_MODEL), second, _full((1, D_MODEL)), _resident((D_MODEL, IN_COLS)),
                  HBM_SPEC],
        out_specs=[second, _full(shard), _full(shard), HBM_SPEC, _full((SMALL_ROWS, D_MODEL)), HBM_SPEC],
        out_shape=[jax.ShapeDtypeStruct((t, D_MODEL), F32), whole(F32), whole(BF16),
                   jax.ShapeDtypeStruct((n_far,) + shard, BF16), jax.ShapeDtypeStruct((SMALL_ROWS, D_MODEL), F32),
                   jax.ShapeDtypeStruct(out_sums.shape, out_sums.dtype)],
        scratch_shapes=[pltpu.VMEM((D_MODEL, IN_COLS), F32), pltpu.VMEM((N_CHIPS,) + shard, BF16),
                        pltpu.VMEM((N_CHIPS,) + shard, BF16), pltpu.VMEM((N_CHIPS,) + shard, BF16),
                        sems(N_CHIPS), sems(N_CHIPS), sems(n_far), sems(n_far), sems(n_far), sems(n_far)],
        compiler_params=_params("arbitrary", barrier_id=7),
    )(dqkv, dgates, x, dh, g1, w_in, out_sums)
    return res[0], (res[1], res[2], res[3]), res[4], res[5]


def _all_gather(shards, name):
    n = len(shards)

    def body(*refs):
        _enter_with(_sibling_and_chips(*_mesh_pos()))
        start, finish = _gather_steps(refs[:n], refs[n:2 * n], *refs[2 * n:])
        start()
        finish()

    return pl.pallas_call(
        body, name=name,
        in_specs=[HBM_SPEC] * n, out_specs=[HBM_SPEC] * n,
        out_shape=[jax.ShapeDtypeStruct((N_DEV,) + s.shape, s.dtype) for s in shards],
        scratch_shapes=[pltpu.SemaphoreType.DMA((7 * n,)), pltpu.SemaphoreType.DMA((7 * n,)),
                        pltpu.SemaphoreType.DMA((n,))],
        compiler_params=_params(barrier_id=8),
    )(*shards)


def _adam_math(w, g, m, v):
    m = ADAM_B1 * m + (1.0 - ADAM_B1) * g
    v = ADAM_B2 * v + (1.0 - ADAM_B2) * (g * g)
    m_hat = m / (1.0 - ADAM_B1 ** ADAM_STEP)
    v_hat = v / (1.0 - ADAM_B2 ** ADAM_STEP)
    delta = -ADAM_LR * (m_hat / (jnp.sqrt(v_hat) + ADAM_EPS) + ADAM_WD * w)
    return delta, m, v


def _adamw_reduced(w, m, v, own, from_sibling, from_chips, tr):
    rows, cols = w.shape

    def body(w_ref, m_ref, v_ref, own_ref, sib_ref, far_ref, g_ref, d_ref, nm_ref, nv_ref):
        g = own_ref[...] + sib_ref[...].astype(F32)
        for k in range(len(CHIP_FLIPS)):
            g = g + far_ref[k].astype(F32)
        g_ref[...] = g
        d_ref[...], nm_ref[...], nv_ref[...] = _adam_math(w_ref[...], g, m_ref[...], v_ref[...])

    tile = pl.BlockSpec((tr, cols), lambda i: (i, 0))
    out = jax.ShapeDtypeStruct((rows, cols), F32)
    return pl.pallas_call(
        body, name="adamw_reduced", grid=(rows // tr,),
        in_specs=[tile] * 5 + [pl.BlockSpec((len(CHIP_FLIPS), tr, cols), lambda i: (0, i, 0))],
        out_specs=[tile] * 4, out_shape=[out] * 4,
        compiler_params=_params("parallel"),
    )(w, m, v, own, from_sibling, from_chips)


def _sum_devices(gathered):
    _, rows, cols = gathered.shape

    def body(g_ref, o_ref):
        s = g_ref[0]
        for d in range(1, N_DEV):
            s = s + g_ref[d]
        o_ref[...] = s

    return pl.pallas_call(
        body, name="sum_devices", in_specs=[_full(gathered.shape)], out_specs=_full((rows, cols)), grid=(1,),
        out_shape=jax.ShapeDtypeStruct((rows, cols), F32),
    )(gathered)


def _adamw_small(w, g, m, v):
    def body(w_ref, g_ref, m_ref, v_ref, d_ref, nm_ref, nv_ref):
        d_ref[...], nm_ref[...], nv_ref[...] = _adam_math(w_ref[...], g_ref[...], m_ref[...], v_ref[...])

    spec = _full(w.shape)
    out = jax.ShapeDtypeStruct(w.shape, F32)
    return pl.pallas_call(
        body, name="adamw_small", grid=(1,), in_specs=[spec] * 4, out_specs=[spec] * 3, out_shape=[out] * 3,
    )(w, g, m, v)


TOKEN_TILE = 512
MID_TILE = 256
ADAM_ROWS = 128


def _local_grads(x, target, g1, w_in_shard, conv_shard, sinks, g_attn, g_conv, g2, g3, g4, shards, order):
    t = x.shape[0]
    tm = min(TOKEN_TILE, t)
    rope = _rope_tables(t)
    qkv, gates, mconv, w_in, conv_w, gathered = _in_proj_fwd(x, g1, w_in_shard, conv_shard, g_conv, rope, tm, shards,
                                                             (False, True, False))
    attn, mattn, (w_out, w_up, w_down) = _attn_fwd(qkv, sinks, g_attn, shards, gathered)
    act, dup, hn2t, dmo, dmix, dh, dmixed, small_mid = _mid(
        mattn, mconv, x, target, g2, g3, g4, w_out.reshape(D_MODEL, D_MODEL),
        w_up, w_down.reshape(D_FF, D_MODEL), min(MID_TILE, t))
    up_own, up_sib, up_sums = _dw_pair_sums((hn2t, dup), order, "up", "dw_up", 2)
    down_own, down_sib, down_sums, up_far = _dw_pair_sums((act, dmo), order, "down", "dw_down", 3, ride=up_sums)
    out_own, out_sib, out_sums = _dw_pair_sums((mattn, mconv, dmix), order, "out", "dw_out", 4)
    dattn, dgates, small_mix = _mix_bwd(dmixed, attn, gates, g_attn, g_conv, conv_w, tm)
    dqkv, dsink, down_far = _attn_bwd(qkv, dattn, sinks, rope, down_sums)
    grad_x, dw_in, small_in, out_far = _in_proj_bwd(dqkv, dgates, x, dh, g1, w_in, tm, out_sums)
    dw_out, dw_up, dw_down = (out_own, out_sib, out_far), (up_own, up_sib, up_far), (down_own, down_sib, down_far)
    return grad_x, dw_in, dw_out, dw_up, dw_down, (small_mid, small_mix, dsink, small_in)


def _pack_small(small_mid, small_mix, dsink, small_in):
    z = lambda n: jnp.zeros((1, n), F32)
    rows = [
        small_mid[ROW_LOSS:ROW_LOSS + 1],
        small_in[0:1],
        small_mid[ROW_G2:ROW_G2 + 1],
        small_mid[ROW_G3:ROW_G3 + 1],
        small_mid[ROW_G4:ROW_G4 + 1],
        jnp.concatenate([small_mix[ROW_GATTN:ROW_GATTN + 1], small_mix[ROW_GCONV:ROW_GCONV + 1]], axis=1),
        jnp.concatenate([small_mix[ROW_CW0:ROW_CW0 + 1], small_mix[ROW_CW0 + 1:ROW_CW0 + 2]], axis=1),
        jnp.concatenate([small_mix[ROW_CW0 + 2:ROW_CW0 + 3], dsink[0:1, :], z(D_MODEL - CONV_W - 128)], axis=1),
    ]
    return jnp.concatenate(rows, axis=0)


def kernel(x, pre_mix_norm, w_in, conv_w, attn_sinks, attn_group_norm, conv_group_norm, w_out, post_mix_norm, pre_mlp_norm, w_up, w_down, post_mlp_norm, loss_target, m_pre_mix_norm, m_w_in, m_conv_w, m_attn_sinks, m_attn_group_norm, m_conv_group_norm, m_w_out, m_post_mix_norm, m_pre_mlp_norm, m_w_up, m_w_down, m_post_mlp_norm, v_pre_mix_norm, v_w_in, v_conv_w, v_attn_sinks, v_attn_group_norm, v_conv_group_norm, v_w_out, v_post_mix_norm, v_pre_mlp_norm, v_w_up, v_w_down, v_post_mlp_norm):
    xi, yi, ci = _mesh_pos()
    chip = 2 * xi + yi
    dev = 2 * chip + ci

    order = _block_order(dev)

    shards = [w_out[0].astype(BF16), w_up[0].astype(BF16), w_down[0].astype(BF16)]

    grad_x, dw_in, dw_out, dw_up, dw_down, smalls = _local_grads(
        x[0], loss_target[0], pre_mix_norm, w_in[0].astype(BF16), conv_w[0], attn_sinks, attn_group_norm, conv_group_norm,
        post_mix_norm, pre_mlp_norm, post_mlp_norm, shards, order)

    small = _sum_devices(_all_gather([_pack_small(*smalls)], "gather_small")[0])
    loss = (0.5 / D_MODEL) * jnp.sum(small[0])

    big = {}
    for name, w, m, v, (own, sib, far) in zip(
            ("w_in", "w_out", "w_up", "w_down"), (w_in, w_out, w_up, w_down), (m_w_in, m_w_out, m_w_up, m_w_down),
            (v_w_in, v_w_out, v_w_up, v_w_down), (dw_in, dw_out, dw_up, dw_down)):
        big[name] = [a[None] for a in _adamw_reduced(w[0], m[0], v[0], own, sib, far, ADAM_ROWS)]

    conv_g = lax.dynamic_slice(
        jnp.stack([small[6, :CONV_W], small[6, CONV_W:], small[7, :CONV_W]]), (0, dev * (CONV_W // N_DEV)),
        (3, CONV_W // N_DEV))
    pad = lambda a, n: jnp.pad(a.reshape(1, -1), ((0, 0), (0, n - a.size)))
    small_names = ("pre_mix_norm", "post_mix_norm", "pre_mlp_norm", "post_mlp_norm")
    small_w = {"pre_mix_norm": (pre_mix_norm, m_pre_mix_norm, v_pre_mix_norm),
               "post_mix_norm": (post_mix_norm, m_post_mix_norm, v_post_mix_norm),
               "pre_mlp_norm": (pre_mlp_norm, m_pre_mlp_norm, v_pre_mlp_norm),
               "post_mlp_norm": (post_mlp_norm, m_post_mlp_norm, v_post_mlp_norm)}

    def pack(k):
        rows = [small_w[nm][k] for nm in small_names]
        rows.append(jnp.concatenate([(attn_group_norm, m_attn_group_norm, v_attn_group_norm)[k],
                                     (conv_group_norm, m_conv_group_norm, v_conv_group_norm)[k]], axis=1))
        rows.append(pad((conv_w, m_conv_w, v_conv_w)[k], D_MODEL))
        rows.append(pad((attn_sinks, m_attn_sinks, v_attn_sinks)[k], D_MODEL))
        rows.append(jnp.zeros((1, D_MODEL), F32))
        return jnp.concatenate(rows, axis=0)

    g_small = jnp.concatenate(
        [small[1:6], pad(conv_g, D_MODEL), pad(small[7, CONV_W:CONV_W + N_HEADS], D_MODEL), jnp.zeros((1, D_MODEL), F32)],
        axis=0)
    d_small, nm_small, nv_small = _adamw_small(pack(0), g_small, pack(1), pack(2))

    def unpack(a):
        nconv = 3 * CONV_W // N_DEV
        return {"pre_mix_norm": a[0:1], "post_mix_norm": a[1:2], "pre_mlp_norm": a[2:3], "post_mlp_norm": a[3:4],
                "attn_group_norm": a[4:5, :ATTN_W], "conv_group_norm": a[4:5, ATTN_W:],
                "conv_w": a[5, :nconv].reshape(1, 3, CONV_W // N_DEV), "attn_sinks": a[6:7, :N_HEADS]}

    order = ("pre_mix_norm", "w_in", "conv_w", "attn_sinks", "attn_group_norm", "conv_group_norm", "w_out",
             "post_mix_norm", "pre_mlp_norm", "w_up", "w_down", "post_mlp_norm")
    outs = []
    for k, a in enumerate((g_small, d_small, nm_small, nv_small)):
        sm = unpack(a)
        outs += [big[nm][k] if nm in big else sm[nm] for nm in order]
    return (loss, grad_x[None], *outs)
```

```python
import functools

import jax
import jax.numpy as jnp
import numpy as np
from jax import lax
from jax.experimental import pallas as pl
from jax.experimental.pallas import tpu as pltpu

F32 = jnp.float32
BF16 = jnp.bfloat16

D_MODEL = 1024
HEAD_DIM = 64
ATTN_W = 512
CONV_W = 512
N_HEADS = 8
N_KV = 2
GROUP = 4
KV_W = 128
QKV_W = ATTN_W + 2 * KV_W
GATES_W = 3 * CONV_W
IN_COLS = QKV_W + GATES_W
D_FF = 4096
FF_CHUNK = 512
N_FF_CHUNKS = D_FF // FF_CHUNK
BLOCK = 128
ROT_HALF = 8
ROPE_THETA = 500000.0
NORM_EPS = 1e-6
NEG_INF = -1e30
ATTN_SCALE = 0.125
N_DEV = 8
N_CHIPS = 4
IN_SHARD = IN_COLS // N_DEV

ADAM_LR = 0.001
ADAM_B1 = 0.9
ADAM_B2 = 0.999
ADAM_EPS = 1e-08
ADAM_WD = 0.01
ADAM_STEP = 10

V7X_VMEM_BYTES = 64 * 1024 * 1024
VMEM_LIMIT = V7X_VMEM_BYTES - 2 * 1024 * 1024

MESH = pl.DeviceIdType.MESH
HBM_SPEC = pl.BlockSpec(memory_space=pltpu.HBM)


def _params(*sem, barrier_id=None):
    return pltpu.CompilerParams(dimension_semantics=sem or None, vmem_limit_bytes=VMEM_LIMIT, collective_id=barrier_id)


def _mm(a, b):
    return jnp.dot(a, b, preferred_element_type=F32)


def _mm_nt(a, b):
    return lax.dot_general(a, b, (((1,), (1,)), ((), ())), preferred_element_type=F32)


def _mm_tn(a, b):
    return lax.dot_general(a, b, (((0,), (0,)), ((), ())), preferred_element_type=F32)


def _inv_rms(x):
    return lax.rsqrt(jnp.mean(x * x, axis=-1, keepdims=True) + NORM_EPS)


def _rms_bwd(xhat, r, gain, dy):
    gy = dy * gain
    return r * (gy - xhat * jnp.mean(gy * xhat, axis=-1, keepdims=True)), dy * xhat


def _colsum(a):
    return jnp.sum(a, axis=0, keepdims=True)


def _full(shape):
    zeros = (0,) * len(shape)
    return pl.BlockSpec(shape, lambda *_: zeros)


def _resident(shape):
    zeros = (0,) * len(shape)
    return pl.BlockSpec(shape, lambda *_: zeros, pipeline_mode=pl.Buffered(1))


def _rope_tables(t):
    pos = np.arange(t, dtype=np.float32)
    inv_freq = (ROPE_THETA ** (-np.arange(0, 2 * ROT_HALF, 2, dtype=np.float64) / (2 * ROT_HALF))).astype(np.float32)
    ang = (pos[:, None] * inv_freq[None, :]).astype(np.float64)
    cos, sin = np.cos(ang).astype(np.float32), np.sin(ang).astype(np.float32)
    zeros8 = np.zeros((t, ROT_HALF), np.float32)
    rest = np.zeros((t, HEAD_DIM - 2 * ROT_HALF), np.float32)
    c_head = np.concatenate([cos, cos, rest + 1.0], axis=1)
    s1_head = np.concatenate([zeros8, sin, rest], axis=1)
    s2_head = np.concatenate([-sin, zeros8, rest], axis=1)
    two = lambda a: jnp.asarray(np.concatenate([a, a], axis=1))
    return two(c_head), two(s1_head), two(s2_head)


def _rope(v, c, s1, s2):
    return v * c + pltpu.roll(v, ROT_HALF, 1) * s1 + pltpu.roll(v, 128 - ROT_HALF, 1) * s2


def _rope_transpose(dv, c, s1, s2):
    return dv * c + pltpu.roll(dv * s1, 128 - ROT_HALF, 1) + pltpu.roll(dv * s2, ROT_HALF, 1)


def _shift_rows_down(u, prev, k):
    row = lax.broadcasted_iota(jnp.int32, u.shape, 0)
    out = pltpu.roll(u, k, 0)
    for r in range(k):
        out = jnp.where(row == r, prev[8 - k + r:8 - k + r + 1, :], out)
    return out


def _shift_rows_up(u, nxt, k):
    n = u.shape[0]
    row = lax.broadcasted_iota(jnp.int32, u.shape, 0)
    out = pltpu.roll(u, n - k, 0)
    for r in range(k):
        out = jnp.where(row == n - k + r, nxt[r:r + 1, :], out)
    return out


def _conv3(u, u1, u2, w):
    return (w[0:1, :] * u2 + w[1:2, :] * u1) + w[2:3, :] * u


def _mesh_pos():
    return lax.axis_index("x"), lax.axis_index("y"), lax.axis_index("c")


def _slot(ref, pos):
    dev = 4 * pos[0] + 2 * pos[1] + pos[2]
    if len(ref.shape) == 2:
        width = ref.shape[1] // N_DEV
        return ref.at[:, pl.ds(pl.multiple_of(dev * width, width), width)]
    return ref.at[dev]


def _gathered_shape(shard, by_cols):
    if by_cols:
        return jax.ShapeDtypeStruct((shard.shape[0], N_DEV * shard.shape[1]), shard.dtype)
    return jax.ShapeDtypeStruct((N_DEV,) + shard.shape, shard.dtype)


def _enter_with(peers):
    barrier = pltpu.get_barrier_semaphore()
    for peer in peers:
        pl.semaphore_signal(barrier, inc=1, device_id=peer, device_id_type=MESH)
    pl.semaphore_wait(barrier, len(peers))


def _sibling_and_chips(x, y, c):
    return [(x, y, 1 - c), (1 - x, y, c), (x, 1 - y, c), (1 - x, 1 - y, c)]


def _push(src, dst, sems, k, to):
    send_sems, recv_sems = sems
    return pltpu.make_async_remote_copy(src_ref=src, dst_ref=dst, send_sem=send_sems.at[k], recv_sem=recv_sems.at[k],
                                        device_id=to, device_id_type=MESH)


def _gather_steps(shards, outs, send_sems, recv_sems, local_sems):
    n = len(shards)
    x, y, c = _mesh_pos()
    me, sibling = (x, y, c), (x, y, 1 - c)
    chips = [(1 - x, y), (x, 1 - y), (1 - x, 1 - y)]

    def copy(i, k, block, to, src=None):
        dst = _slot(outs[i], block)
        return _push(dst if src is None else src, dst, (send_sems, recv_sems), 7 * i + k, to)

    mine = [pltpu.make_async_copy(shards[i], _slot(outs[i], me), local_sems.at[i]) for i in range(n)]
    first = []
    for i in range(n):
        first.append(copy(i, 0, me, sibling, src=shards[i]))
        first += [copy(i, 1 + j, me, (*chip, c), src=shards[i]) for j, chip in enumerate(chips)]

    def start():
        for cp in mine + first:
            cp.start()

    def finish():
        passed = []
        for j, chip in enumerate(chips):
            for i in range(n):
                copy(i, 1 + j, (*chip, c), me).wait_recv()
                cp = copy(i, 4 + j, (*chip, c), sibling)
                cp.start()
                passed.append(cp)
        for i in range(n):
            copy(i, 0, sibling, me).wait_recv()
            for j, chip in enumerate(chips):
                copy(i, 4 + j, (*chip, 1 - c), me).wait_recv()
        for cp in first + passed:
            cp.wait_send()
        for cp in mine:
            cp.wait()

    return start, finish


def _gather_near(first, last, shards, outs, sems, local_sems):
    x, y, c = _mesh_pos()
    me, peers = (x, y, c), [(x, y, 1 - c), (1 - x, y, c), (x, 1 - y, c)]
    n = len(shards)
    local = [pltpu.make_async_copy(shards[i], _slot(outs[i], me), local_sems.at[i]) for i in range(n)]
    sends = [_push(shards[i], _slot(outs[i], me), sems, 3 * i + k, peers[k]) for i in range(n) for k in range(3)]
    arrivals = [_push(shards[i], _slot(outs[i], peers[k]), sems, 3 * i + k, peers[k]) for i in range(n) for k in range(3)]

    def start():
        for cp in local + sends:
            cp.start()

    if first is not None:
        pl.when(first)(start)

    @pl.when(last)
    def _():
        for cp in sends:
            cp.wait_send()
        for cp in arrivals:
            cp.wait_recv()
        for cp in local:
            cp.wait()

    return start


def _gather_far(first, last, shards, ins, outs, sems):
    x, y, c = _mesh_pos()
    me, sibling = (x, y, c), (x, y, 1 - c)
    chips = [(1 - x, y), (x, 1 - y), (1 - x, 1 - y)]
    n = len(shards)
    diag_send = [_push(shards[i], _slot(outs[i], me), sems, 4 * i, (*chips[2], c)) for i in range(n)]
    diag_arrival = [_push(shards[i], _slot(outs[i], (*chips[2], c)), sems, 4 * i, (*chips[2], c)) for i in range(n)]
    passed = [[_push(_slot(ins[i], (*chips[j], c)), _slot(outs[i], (*chips[j], c)), sems, 4 * i + 1 + j, sibling)
               for i in range(n)] for j in range(3)]
    from_sibling = [_push(shards[i], _slot(outs[i], (*chips[j], 1 - c)), sems, 4 * i + 1 + j, sibling)
                    for i in range(n) for j in range(3)]

    @pl.when(first)
    def _():
        for cp in diag_send + passed[0] + passed[1]:
            cp.start()

    @pl.when(last)
    def _():
        for cp in diag_arrival:
            cp.wait_recv()
        for cp in passed[2]:
            cp.start()
        for cp in from_sibling:
            cp.wait_recv()
        for cp in diag_send + passed[0] + passed[1] + passed[2]:
            cp.wait_send()


def _in_proj_fwd(x, g1, w_in, conv_w, g_conv, rope, tm, shards, by_cols):
    t = x.shape[0]
    rc, rs1, rs2 = rope
    n = len(shards)
    n_tiles = t // tm

    def body(*refs):
        x_ref, g1_ref, w_ref, cw_ref, gc_ref, c_ref, s1_ref, s2_ref = refs[:8]
        shard_refs = refs[8:8 + n]
        qkv_ref, gates_ref, mconv_ref, w_full_ref, cw_full_ref = refs[8 + n:13 + n]
        gathered = refs[13 + n:13 + 2 * n]
        carry_ref, w_land, cw_land, hn_ref = refs[13 + 2 * n:17 + 2 * n]
        now_sems = refs[17 + 2 * n:20 + 2 * n]
        step = pl.program_id(0)
        start_later_weights = _gather_near(None, step == 2 * n_tiles - 1, shard_refs, gathered,
                                           refs[20 + 2 * n:22 + 2 * n], refs[22 + 2 * n]) if n else None
        start_w_in, finish_w_in = _gather_steps([w_ref, cw_ref], [w_land, cw_land], *now_sems)

        @pl.when(step == 0)
        def _():
            carry_ref[...] = jnp.zeros_like(carry_ref)
            _enter_with(_sibling_and_chips(*_mesh_pos()))
            start_w_in()
            if start_later_weights is not None:
                start_later_weights()

        @pl.when(step < n_tiles)
        def _():
            xv = x_ref[...]
            hn_ref[step] = ((xv * _inv_rms(xv)) * g1_ref[...]).astype(BF16)

        @pl.when(step == n_tiles)
        def _():
            finish_w_in()
            conv_shard = CONV_W // N_DEV
            for d in range(N_DEV):
                w_full_ref[:, IN_SHARD * d:IN_SHARD * (d + 1)] = w_land[d]
                cw_full_ref[:, conv_shard * d:conv_shard * (d + 1)] = cw_land[d]

        @pl.when(step >= n_tiles)
        def _():
            proj = _mm(hn_ref[step - n_tiles], w_full_ref[...])
            c, s1, s2 = c_ref[...], s1_ref[...], s2_ref[...]
            for ci in range((ATTN_W + KV_W) // 128):
                sl = slice(128 * ci, 128 * (ci + 1))
                qkv_ref[:, sl] = _rope(proj[:, sl], c, s1, s2).astype(BF16)
            qkv_ref[:, ATTN_W + KV_W:QKV_W] = proj[:, ATTN_W + KV_W:QKV_W].astype(BF16)
            gates = proj[:, QKV_W:]
            gates_ref[...] = gates
            gb, gcc, xin = gates[:, :CONV_W], gates[:, CONV_W:2 * CONV_W], gates[:, 2 * CONV_W:]
            u = gcc * xin
            prev = carry_ref[...]
            conv = gb * _conv3(u, _shift_rows_down(u, prev, 1), _shift_rows_down(u, prev, 2), cw_full_ref[...])
            carry_ref[...] = u[tm - 8:tm, :]
            mconv_ref[...] = ((conv * _inv_rms(conv)) * gc_ref[...]).astype(BF16)

    first_pass = pl.BlockSpec((tm, D_MODEL), lambda i: (jnp.minimum(i, n_tiles - 1), 0))
    tile = lambda w_: pl.BlockSpec((tm, w_), lambda i: (jnp.maximum(i - n_tiles, 0), 0))
    sems = lambda k: pltpu.SemaphoreType.DMA((k,))
    res = pl.pallas_call(
        body, name="in_proj_fwd", grid=(2 * n_tiles,),
        in_specs=[first_pass, _full((1, D_MODEL)), HBM_SPEC, HBM_SPEC, _full((1, CONV_W)), tile(128), tile(128),
                  tile(128)] + [HBM_SPEC] * n,
        out_specs=[tile(QKV_W), tile(GATES_W), tile(CONV_W), _full((D_MODEL, IN_COLS)), _full((3, CONV_W))]
        + [HBM_SPEC] * n,
        out_shape=[jax.ShapeDtypeStruct((t, QKV_W), BF16), jax.ShapeDtypeStruct((t, GATES_W), F32),
                   jax.ShapeDtypeStruct((t, CONV_W), BF16), jax.ShapeDtypeStruct((D_MODEL, IN_COLS), BF16),
                   jax.ShapeDtypeStruct((3, CONV_W), F32)]
        + [_gathered_shape(s, cols) for s, cols in zip(shards, by_cols)],
        scratch_shapes=[pltpu.VMEM((8, CONV_W), F32), pltpu.VMEM((N_DEV,) + w_in.shape, BF16),
                        pltpu.VMEM((N_DEV,) + conv_w.shape, F32), pltpu.VMEM((n_tiles, tm, D_MODEL), BF16),
                        sems(14), sems(14), sems(2)]
        + ([sems(3 * n), sems(3 * n), sems(n)] if n else []),
        compiler_params=_params("arbitrary", barrier_id=0),
    )(x, g1, w_in, conv_w, g_conv, rc, rs1, rs2, *shards)
    return res[0], res[1], res[2], res[3], res[4], list(res[5:])


GROUP_COLS = GROUP * BLOCK
ATTN_STEP_BLOCKS = 4


def _attn_masks(has_prev):
    key = lax.broadcasted_iota(jnp.int32, (2 * BLOCK, GROUP_COLS), 0)
    query = lax.broadcasted_iota(jnp.int32, (2 * BLOCK, GROUP_COLS), 1) & (BLOCK - 1)
    band = (key > query) & (key <= query + BLOCK)
    return [band & ((key >= BLOCK) | has_prev)] + [band] * (ATTN_STEP_BLOCKS - 1)


def _heads_side_by_side(at, g, b):
    heads = [at[HEAD_DIM * (GROUP * g + hh):HEAD_DIM * (GROUP * g + hh + 1), BLOCK * b:BLOCK * (b + 1)] for hh in range(GROUP)]
    return jnp.concatenate(heads, axis=1)


def _to_token_rows(parts):
    rows = [jnp.concatenate([parts[b][g][:, BLOCK * hh:BLOCK * (hh + 1)] for b in range(ATTN_STEP_BLOCKS)], axis=1)
            for g in range(N_KV) for hh in range(GROUP)]
    return jnp.concatenate(rows, axis=0).T


def _group_sinks(sink_ref, g):
    head = lax.broadcasted_iota(jnp.int32, (1, GROUP_COLS), 1) // BLOCK
    out = jnp.full((1, GROUP_COLS), sink_ref[0, GROUP * g], F32)
    for hh in range(1, GROUP):
        out = jnp.where(head == hh, sink_ref[0, GROUP * g + hh], out)
    return out


def _attn_probs(qt, kk, sink, valid):
    s = jnp.where(valid, _mm(kk, qt), NEG_INF)
    m = jnp.maximum(jnp.max(s, axis=0, keepdims=True), sink)
    p = jnp.exp(s - m)
    psink = jnp.exp(sink - m)
    inv_l = 1.0 / (jnp.sum(p, axis=0, keepdims=True) + psink)
    return p * inv_l, psink * inv_l


ATTN_STEP = ATTN_STEP_BLOCKS * BLOCK
ATTN_KEYS = ATTN_STEP + BLOCK


def _qkv_specs(order):
    prev = lambda i: jnp.maximum(ATTN_STEP_BLOCKS * order(i) - 1, 0)
    kcol, vcol = ATTN_W // KV_W, ATTN_W // KV_W + 1
    return [pl.BlockSpec((ATTN_STEP, ATTN_W), lambda i: (order(i), 0)),
            pl.BlockSpec((BLOCK, KV_W), lambda i: (prev(i), kcol)), pl.BlockSpec((ATTN_STEP, KV_W), lambda i: (order(i), kcol)),
            pl.BlockSpec((BLOCK, KV_W), lambda i: (prev(i), vcol)), pl.BlockSpec((ATTN_STEP, KV_W), lambda i: (order(i), vcol))]


def _attn_fwd(qkv, sinks, g_attn, shards, gathered):
    t = qkv.shape[0]
    n = len(shards)

    def body(*refs):
        sink_ref, q_ref, kp_ref, kc_ref, vp_ref, vc_ref, ga_ref = refs[:7]
        attn_ref, mattn_ref = refs[7 + 2 * n:9 + 2 * n]
        step = pl.program_id(0)
        if n:
            @pl.when(step == 0)
            def _():
                x, y, c = _mesh_pos()
                _enter_with([(x, y, 1 - c), (1 - x, 1 - y, c)])

            _gather_far(step == 0, step == pl.num_programs(0) - 1, refs[7:7 + n], refs[7 + n:7 + 2 * n],
                        refs[9 + 2 * n:9 + 3 * n], refs[9 + 3 * n:11 + 3 * n])
        qt = (q_ref[...] * ATTN_SCALE).T
        keys = jnp.concatenate([kp_ref[...], kc_ref[...]], axis=0)
        vals = jnp.concatenate([vp_ref[...], vc_ref[...]], axis=0)
        sink = [_group_sinks(sink_ref, g) for g in range(N_KV)]
        masks = _attn_masks(step > 0)
        parts = []
        for b in range(ATTN_STEP_BLOCKS):
            window = slice(BLOCK * b, BLOCK * (b + 2))
            valid = masks[b]
            parts.append([])
            for g in range(N_KV):
                gs = slice(HEAD_DIM * g, HEAD_DIM * (g + 1))
                probs, _ = _attn_probs(_heads_side_by_side(qt, g, b), keys[window, gs], sink[g], valid)
                parts[b].append(_mm_tn(vals[window, gs], probs.astype(BF16)))
        attn = _to_token_rows(parts)
        attn_ref[...] = attn
        mattn_ref[...] = ((attn * _inv_rms(attn)) * ga_ref[...]).astype(BF16)

    blk = pl.BlockSpec((ATTN_STEP, ATTN_W), lambda j: (j, 0))
    res = pl.pallas_call(
        body, name="attn_fwd", grid=(t // ATTN_STEP,),
        in_specs=[pl.BlockSpec(memory_space=pltpu.SMEM)] + _qkv_specs(lambda j: j) + [_full((1, ATTN_W))]
        + [HBM_SPEC] * (2 * n),
        out_specs=[blk, blk] + [HBM_SPEC] * n,
        out_shape=[jax.ShapeDtypeStruct((t, ATTN_W), F32), jax.ShapeDtypeStruct((t, ATTN_W), BF16)]
        + [jax.ShapeDtypeStruct(g.shape, g.dtype) for g in gathered],
        input_output_aliases={7 + n + i: 2 + i for i in range(n)},
        scratch_shapes=[pltpu.SemaphoreType.DMA((4 * n,)), pltpu.SemaphoreType.DMA((4 * n,))] if n else [],
        compiler_params=_params("arbitrary", barrier_id=1 if n else None),
    )(sinks, qkv, qkv, qkv, qkv, qkv, g_attn, *shards, *gathered)
    return res[0], res[1], list(res[2:])


SMALL_ROWS = 8
ROW_LOSS, ROW_G2, ROW_G3, ROW_G4 = 0, 1, 2, 3


def _mid(mattn, mconv, x, target, g2, g3, g4, w_out, w_up, w_down, tm):
    t = x.shape[0]

    def body(ma_ref, mc_ref, x_ref, t_ref, g2_ref, g3_ref, g4_ref, wo_ref, wu_ref, wd_ref,
             act_ref, dup_ref, hn2t_ref, dmo_ref, dmix_ref, dh_ref, dmixed_ref, small_ref, up_ref):
        @pl.when(pl.program_id(0) == 0)
        def _():
            small_ref[...] = jnp.zeros_like(small_ref)

        g2, g3, g4 = g2_ref[...], g3_ref[...], g4_ref[...]
        mix_out = _mm(ma_ref[...], wo_ref[0:ATTN_W, :]) + _mm(mc_ref[...], wo_ref[ATTN_W:, :])
        r2 = _inv_rms(mix_out)
        mo_hat = mix_out * r2
        h = x_ref[...] + mo_hat * g2
        r3 = _inv_rms(h)
        h_hat = h * r3
        hn2 = (h_hat * g3).astype(BF16)
        hn2t_ref[...] = hn2.T
        for j in range(MID_CHUNKS):
            cols_j = slice(MID_CHUNK * j, MID_CHUNK * (j + 1))
            up = jnp.maximum(_mm(hn2, wu_ref[:, cols_j]), 0.0)
            up_ref[:, cols_j] = up.astype(BF16)
            act_ref[:, cols_j] = (up * up).astype(BF16)
        mlp = _mm(act_ref[...], wd_ref[...])
        r4 = _inv_rms(mlp)
        ml_hat = mlp * r4
        err = (h + ml_hat * g4) - t_ref[...]
        d_out = err * (1.0 / D_MODEL)
        d_mlp, dg4 = _rms_bwd(ml_hat, r4, g4, d_out)
        dmo = d_mlp.astype(BF16)
        dmo_ref[...] = dmo
        for j in range(MID_CHUNKS):
            cols_j = slice(MID_CHUNK * j, MID_CHUNK * (j + 1))
            dact = _mm_nt(dmo, wd_ref[cols_j, :])
            dup_ref[:, cols_j] = (dact * (2.0 * up_ref[:, cols_j].astype(F32))).astype(BF16)
        dhn2 = _mm_nt(dup_ref[...], wu_ref[...])
        dh_norm, dg3 = _rms_bwd(h_hat, r3, g3, dhn2)
        dh = d_out + dh_norm
        dh_ref[...] = dh
        d_mix, dg2 = _rms_bwd(mo_hat, r2, g2, dh)
        dmix = d_mix.astype(BF16)
        dmix_ref[...] = dmix
        dmixed_ref[...] = _mm_nt(dmix, wo_ref[...])
        small_ref[ROW_LOSS:ROW_LOSS + 1, :] += _colsum(err * err)
        small_ref[ROW_G2:ROW_G2 + 1, :] += _colsum(dg2)
        small_ref[ROW_G3:ROW_G3 + 1, :] += _colsum(dg3)
        small_ref[ROW_G4:ROW_G4 + 1, :] += _colsum(dg4)

    tile = lambda n: pl.BlockSpec((tm, n), lambda i: (i, 0))
    cols = lambda n: pl.BlockSpec((n, tm), lambda i: (0, i))
    gain = _full((1, D_MODEL))
    return pl.pallas_call(
        body, name="mid_fwd_bwd", grid=(t // tm,),
        in_specs=[tile(ATTN_W), tile(CONV_W), tile(D_MODEL), tile(D_MODEL), gain, gain, gain,
                  _resident((D_MODEL, D_MODEL)), _resident((D_MODEL, D_FF)), _resident((D_FF, D_MODEL))],
        out_specs=[tile(D_FF), tile(D_FF), cols(D_MODEL), tile(D_MODEL), tile(D_MODEL), tile(D_MODEL), tile(D_MODEL),
                   _full((SMALL_ROWS, D_MODEL))],
        out_shape=[jax.ShapeDtypeStruct((t, D_FF), BF16), jax.ShapeDtypeStruct((t, D_FF), BF16),
                   jax.ShapeDtypeStruct((D_MODEL, t), BF16), jax.ShapeDtypeStruct((t, D_MODEL), BF16),
                   jax.ShapeDtypeStruct((t, D_MODEL), BF16), jax.ShapeDtypeStruct((t, D_MODEL), F32),
                   jax.ShapeDtypeStruct((t, D_MODEL), F32), jax.ShapeDtypeStruct((SMALL_ROWS, D_MODEL), F32)],
        scratch_shapes=[pltpu.VMEM((tm, D_FF), BF16)],
        compiler_params=_params("arbitrary"),
    )(mattn, mconv, x, target, g2, g3, g4, w_out, w_up, w_down)


CHIP_FLIPS = ((1, 1), (1, 0), (0, 1))


def _block_order(dev):
    chip_masks = [4 * fx + 2 * fy for fx, fy in CHIP_FLIPS]
    masks = [m + 1 for m in chip_masks] + [1] + chip_masks + [0]
    return jnp.bitwise_xor(dev, jnp.asarray(masks, jnp.int32)).astype(jnp.int32)


def _other_chips(x, y, c):
    return [(1 - x if fx else x, 1 - y if fy else y, c) for fx, fy in CHIP_FLIPS]


def _dw_pair_sums(operands, order, which, name, barrier_id, ride=None):
    t = operands[-1].shape[0]
    n_far = len(CHIP_FLIPS)
    n_in = len(operands)
    n_ride = 0 if ride is None else 1
    out_chunk = D_MODEL // N_DEV
    if which == "up":
        rows, cols = D_MODEL, FF_CHUNK
        in_specs = [_resident((D_MODEL, t)), pl.BlockSpec((t, FF_CHUNK), lambda s, order_ref: (0, order_ref[s]))]
    elif which == "down":
        rows, cols = FF_CHUNK, D_MODEL
        in_specs = [pl.BlockSpec((t, FF_CHUNK), lambda s, order_ref: (0, order_ref[s])), _resident((t, D_MODEL))]
    else:
        rows, cols = out_chunk, D_MODEL
        half = pl.BlockSpec((t, out_chunk), lambda s, order_ref: (0, order_ref[s] % (N_DEV // 2)))
        in_specs = [half, half, _resident((t, D_MODEL))]

    def body(order_ref, *refs):
        own_ref, from_sib_ref, pair_ref = refs[n_in + n_ride:n_in + n_ride + 3]
        send_buf, land_buf, send_sems, recv_sems = refs[n_in + 2 * n_ride + 3:n_in + 2 * n_ride + 7]
        s_now = pl.program_id(0)
        x, y, c = _mesh_pos()
        sibling = (x, y, 1 - c)
        sems = (send_sems, recv_sems)

        @pl.when(s_now == 0)
        def _():
            _enter_with([sibling] + (_other_chips(x, y, c) if n_ride else []))

        if n_ride:
            _chip_exchange_beside(s_now == 0, s_now == N_DEV - 1, [refs[n_in]], [refs[n_in + 3 + n_ride]],
                                  refs[n_in + 2 * n_ride + 7:], enter=False)

        def hand_over(k):
            dst = land_buf.at[k] if k < n_far else from_sib_ref
            return _push(send_buf.at[k], dst, sems, k, sibling)

        if which == "out":
            ma_ref, mc_ref, b_ref = refs[:n_in]
            block = lax.cond(order_ref[s_now] < N_DEV // 2, lambda: _mm_tn(ma_ref[...], b_ref[...]),
                             lambda: _mm_tn(mc_ref[...], b_ref[...]))
        elif which == "down":
            block = _mm_tn(refs[0][...], refs[1][...])
        else:
            block = _mm(refs[0][...], refs[1][...])
        for k in range(n_far + 1):
            @pl.when(s_now == k)
            def _():
                send_buf[k] = block.astype(BF16)
                hand_over(k).start()

        for k in range(n_far):
            @pl.when(s_now == n_far + 1 + k)
            def _():
                hand_over(k).wait_recv()
                pair_ref[...] = (block + land_buf[k].astype(F32)).astype(BF16)

        @pl.when(s_now == N_DEV - 1)
        def _():
            own_ref[...] = block
            for k in range(n_far + 1):
                hand_over(k).wait_send()
            hand_over(n_far).wait_recv()

    rides = [] if ride is None else [ride]
    sems = lambda k: pltpu.SemaphoreType.DMA((k,))
    return pl.pallas_call(
        body, name=name,
        grid_spec=pltpu.PrefetchScalarGridSpec(
            num_scalar_prefetch=1, grid=(N_DEV,), in_specs=in_specs + [HBM_SPEC] * n_ride,
            out_specs=[pl.BlockSpec((rows, cols), lambda s, order_ref: (0, 0)), HBM_SPEC,
                       pl.BlockSpec((None, rows, cols), lambda s, order_ref: (jnp.clip(s - n_far - 1, 0, n_far - 1), 0, 0))]
            + [HBM_SPEC] * n_ride,
            scratch_shapes=[pltpu.VMEM((n_far + 1, rows, cols), BF16), pltpu.VMEM((n_far, rows, cols), BF16),
                            sems(n_far + 1), sems(n_far + 1)] + [sems(n_far), sems(n_far)] * n_ride),
        out_shape=[jax.ShapeDtypeStruct((rows, cols), F32), jax.ShapeDtypeStruct((rows, cols), BF16),
                   jax.ShapeDtypeStruct((n_far, rows, cols), BF16)]
        + [jax.ShapeDtypeStruct(r.shape, r.dtype) for r in rides],
        compiler_params=_params("arbitrary", barrier_id=barrier_id),
    )(order, *operands, *rides)


def _chip_exchange_beside(first, last, sums, outs, sems, enter=True):
    chips = _other_chips(*_mesh_pos())
    copies = [_push(sums[i].at[k], outs[i].at[k], sems, len(chips) * i + k, chip)
              for i in range(len(sums)) for k, chip in enumerate(chips)]

    @pl.when(first)
    def _():
        if enter:
            _enter_with(chips)
        for cp in copies:
            cp.start()

    @pl.when(last)
    def _():
        for cp in copies:
            cp.wait()


ROW_GCONV, ROW_CW0 = 1, 2


def _conv_bwd(dmixed, gates, g_conv, conv_w, tm):
    t = gates.shape[0]
    n = t // tm
    rev = lambda i: n - 1 - i

    def body(dm_ref, gates_ref, gprev_ref, gc_ref, cw_ref, dgates_ref, small_ref, carry_ref):
        i = pl.program_id(0)

        @pl.when(i == 0)
        def _():
            small_ref[...] = jnp.zeros_like(small_ref)
            carry_ref[...] = jnp.zeros_like(carry_ref)

        gates = gates_ref[...]
        gb, gcc, xin = gates[:, :CONV_W], gates[:, CONV_W:2 * CONV_W], gates[:, 2 * CONV_W:]
        u = gcc * xin
        gp = gprev_ref[...]
        uprev = jnp.where(rev(i) == 0, 0.0, gp[:, CONV_W:2 * CONV_W] * gp[:, 2 * CONV_W:])
        u1, u2 = _shift_rows_down(u, uprev, 1), _shift_rows_down(u, uprev, 2)
        w = cw_ref[...]
        c = _conv3(u, u1, u2, w)
        conv = gb * c
        rcv = _inv_rms(conv)
        c_hat = conv * rcv
        dconv, dgc = _rms_bwd(c_hat, rcv, gc_ref[...], dm_ref[...])
        dc = dconv * gb
        nxt = carry_ref[...]
        du = (w[2:3, :] * dc + w[1:2, :] * _shift_rows_up(dc, nxt, 1)) + w[0:1, :] * _shift_rows_up(dc, nxt, 2)
        carry_ref[...] = dc[0:8, :]
        dgates_ref[:, :CONV_W] = (dconv * c).astype(BF16)
        dgates_ref[:, CONV_W:2 * CONV_W] = (du * xin).astype(BF16)
        dgates_ref[:, 2 * CONV_W:] = (du * gcc).astype(BF16)
        small_ref[ROW_GCONV:ROW_GCONV + 1, :] += _colsum(dgc)
        small_ref[ROW_CW0:ROW_CW0 + 1, :] += _colsum(dc * u2)
        small_ref[ROW_CW0 + 1:ROW_CW0 + 2, :] += _colsum(dc * u1)
        small_ref[ROW_CW0 + 2:ROW_CW0 + 3, :] += _colsum(dc * u)

    tile = lambda w_: pl.BlockSpec((tm, w_), lambda i: (rev(i), 0))
    prev8 = pl.BlockSpec((8, GATES_W), lambda i: (jnp.maximum(rev(i) * (tm // 8) - 1, 0), 0))
    conv_half = pl.BlockSpec((tm, CONV_W), lambda i: (rev(i), ATTN_W // CONV_W))
    return pl.pallas_call(
        body, name="conv_bwd", grid=(n,),
        in_specs=[conv_half, tile(GATES_W), prev8, _full((1, CONV_W)), _full((3, CONV_W))],
        out_specs=[tile(GATES_W), _full((SMALL_ROWS, CONV_W))],
        out_shape=[jax.ShapeDtypeStruct((t, GATES_W), BF16), jax.ShapeDtypeStruct((SMALL_ROWS, CONV_W), F32)],
        scratch_shapes=[pltpu.VMEM((8, CONV_W), F32)],
        compiler_params=_params("arbitrary"),
    )(dmixed, gates, gates, g_conv, conv_w)


def _attn_bwd(qkv, dmixed, attn, g_attn, sinks, rope, sums):
    t = qkv.shape[0]
    n_steps = t // ATTN_STEP
    rev = lambda i: n_steps - 1 - i
    rc, rs1, rs2 = rope

    def body(sink_ref, q_ref, kp_ref, kc_ref, vp_ref, vc_ref, dm_ref, attn_ref, ga_ref, c_ref, s1_ref, s2_ref, sums_ref,
             dqkv_ref, dsink_ref, dgain_ref, arrived_ref, ck_ref, cv_ref, kacc_ref, vacc_ref, send_sems, recv_sems):
        i = pl.program_id(0)
        _chip_exchange_beside(i == 0, i == n_steps - 1, [sums_ref], [arrived_ref], (send_sems, recv_sems))

        @pl.when(i == 0)
        def _():
            dsink_ref[...] = jnp.zeros_like(dsink_ref)
            dgain_ref[...] = jnp.zeros_like(dgain_ref)
            ck_ref[...] = jnp.zeros_like(ck_ref)
            cv_ref[...] = jnp.zeros_like(cv_ref)

        kacc_ref[...] = jnp.zeros_like(kacc_ref)
        vacc_ref[...] = jnp.zeros_like(vacc_ref)
        a = attn_ref[...]
        ra = _inv_rms(a)
        dattn, dgain = _rms_bwd(a * ra, ra, ga_ref[...], dm_ref[...])
        dgain_ref[0:1, :] += _colsum(dgain)
        qt = (q_ref[...] * ATTN_SCALE).T
        dot = dattn.astype(BF16).T
        keys = jnp.concatenate([kp_ref[...], kc_ref[...]], axis=0)
        vals = jnp.concatenate([vp_ref[...], vc_ref[...]], axis=0)
        sink = [_group_sinks(sink_ref, g) for g in range(N_KV)]
        c, s1, s2 = c_ref[...], s1_ref[...], s2_ref[...]
        lane = lax.broadcasted_iota(jnp.int32, (1, 128), 1)
        dsink = jnp.zeros((1, 128), F32)
        masks = _attn_masks(rev(i) > 0)
        dq_parts = []
        for b in range(ATTN_STEP_BLOCKS):
            window = slice(BLOCK * b, BLOCK * (b + 2))
            valid = masks[b]
            dq_parts.append([])
            dk_parts, dv_parts = [], []
            for g in range(N_KV):
                gs = slice(HEAD_DIM * g, HEAD_DIM * (g + 1))
                kk, vv = keys[window, gs], vals[window, gs]
                qtg, dotg = _heads_side_by_side(qt, g, b), _heads_side_by_side(dot, g, b)
                probs, psink = _attn_probs(qtg, kk, sink[g], valid)
                dp = _mm(vv, dotg)
                delta = jnp.sum(probs * dp, axis=0, keepdims=True)
                ds = (probs * (dp - delta)).astype(BF16)
                sink_terms = psink * delta
                for hh in range(GROUP):
                    head_sum = jnp.sum(sink_terms[:, BLOCK * hh:BLOCK * (hh + 1)])
                    dsink = dsink + jnp.where(lane == GROUP * g + hh, -head_sum, 0.0)
                dq_parts[b].append(_mm_tn(kk * ATTN_SCALE, ds))
                dk_parts.append(_mm_nt(ds, qtg))
                dv_parts.append(_mm_nt(probs.astype(BF16), dotg))
            kacc_ref[window, :] += jnp.concatenate(dk_parts, axis=1)
            vacc_ref[window, :] += jnp.concatenate(dv_parts, axis=1)
        dq = _to_token_rows(dq_parts)
        for ci in range(ATTN_W // 128):
            sl = slice(128 * ci, 128 * (ci + 1))
            dqkv_ref[:, sl] = _rope_transpose(dq[:, sl], c, s1, s2).astype(BF16)
        kacc_ref[ATTN_STEP:, :] += ck_ref[...]
        vacc_ref[ATTN_STEP:, :] += cv_ref[...]
        ck_ref[...] = kacc_ref[:BLOCK, :]
        cv_ref[...] = vacc_ref[:BLOCK, :]
        dqkv_ref[:, ATTN_W:ATTN_W + KV_W] = _rope_transpose(kacc_ref[BLOCK:, :], c, s1, s2).astype(BF16)
        dqkv_ref[:, ATTN_W + KV_W:] = vacc_ref[BLOCK:, :].astype(BF16)
        dsink_ref[0:1, :] += dsink

    blk = lambda w_: pl.BlockSpec((ATTN_STEP, w_), lambda i: (rev(i), 0))
    return pl.pallas_call(
        body, name="attn_bwd", grid=(n_steps,),
        in_specs=[pl.BlockSpec(memory_space=pltpu.SMEM)] + _qkv_specs(rev)
        + [blk(ATTN_W), blk(ATTN_W), _full((1, ATTN_W)), blk(128), blk(128), blk(128), HBM_SPEC],
        out_specs=[blk(QKV_W), _full((8, 128)), _full((SMALL_ROWS, ATTN_W)), HBM_SPEC],
        out_shape=[jax.ShapeDtypeStruct((t, QKV_W), BF16), jax.ShapeDtypeStruct((8, 128), F32),
                   jax.ShapeDtypeStruct((SMALL_ROWS, ATTN_W), F32), jax.ShapeDtypeStruct(sums.shape, sums.dtype)],
        scratch_shapes=[pltpu.VMEM((BLOCK, KV_W), F32), pltpu.VMEM((BLOCK, KV_W), F32),
                        pltpu.VMEM((ATTN_KEYS, KV_W), F32), pltpu.VMEM((ATTN_KEYS, KV_W), F32),
                        pltpu.SemaphoreType.DMA((len(CHIP_FLIPS),)), pltpu.SemaphoreType.DMA((len(CHIP_FLIPS),))],
        compiler_params=_params("arbitrary", barrier_id=6),
    )(sinks, qkv, qkv, qkv, qkv, qkv, dmixed, attn, g_attn, rc, rs1, rs2, sums)


def _grad_x_tile(dq, dg, x_hat, r, g1, w_ref, dh):
    dhn = _mm_nt(dq, w_ref[:, :QKV_W]) + _mm_nt(dg, w_ref[:, QKV_W:])
    dx, dg1 = _rms_bwd(x_hat, r, g1, dhn)
    return dh + dx, _colsum(dg1)


def _in_proj_bwd(dqkv, dgates, x, dh, g1, w_in, tm, out_sums):
    t = x.shape[0]
    n = t // tm
    n_steps = 2 * n
    n_far = len(CHIP_FLIPS)
    shard = (D_MODEL, IN_SHARD)

    def body(dq_ref, dg_ref, x_ref, dh_ref, g1_ref, w_ref, osums_ref,
             dx_ref, own_ref, sib_ref, far_ref, dg1_ref, oarrived_ref,
             acc_ref, send_buf, land_buf, pair_buf, d2d_send, d2d_recv, ici_send, ici_recv, o_send, o_recv):
        i = pl.program_id(0)
        x_pos, y_pos, c = _mesh_pos()
        my_chip = 2 * x_pos + y_pos
        sibling = (x_pos, y_pos, 1 - c)
        @pl.when(i == 0)
        def _():
            _enter_with(_sibling_and_chips(x_pos, y_pos, c))

        _chip_exchange_beside(i == 0, i == n_steps - 1, [osums_ref], [oarrived_ref], (o_send, o_recv), enter=False)

        def cols(d):
            return slice(IN_SHARD * d, IN_SHARD * (d + 1))

        def hand_over(chip):
            return _push(send_buf.at[chip], land_buf.at[chip], (d2d_send, d2d_recv), chip, sibling)

        def to_chip(chip, rel):
            return pltpu.make_async_remote_copy(
                src_ref=pair_buf.at[chip], dst_ref=far_ref.at[rel - 1], send_sem=ici_send.at[rel - 1],
                recv_sem=ici_recv.at[rel - 1], device_id=(chip // 2, chip % 2, c), device_id_type=MESH)

        @pl.when(i == 0)
        def _():
            acc_ref[...] = jnp.zeros_like(acc_ref)
            dg1_ref[...] = jnp.zeros_like(dg1_ref)

        def normed_x():
            xv = x_ref[...]
            r = _inv_rms(xv)
            return xv * r, r

        @pl.when(i < n)
        def _():
            hn = (normed_x()[0] * g1_ref[...]).astype(BF16)
            acc_ref[:, :QKV_W] += _mm_tn(hn, dq_ref[...])
            acc_ref[:, QKV_W:] += _mm_tn(hn, dg_ref[...])

        @pl.when(i == n - 1)
        def _():
            for d in range(N_DEV):
                @pl.when(d % 2 != c)
                def _():
                    send_buf[d // 2] = acc_ref[:, cols(d)].astype(BF16)
                    hand_over(d // 2).start()
            for d in range(N_DEV):
                chip = d // 2

                @pl.when(d % 2 == c)
                def _():
                    hand_over(chip).wait_recv()

                    @pl.when(chip == my_chip)
                    def _():
                        own_ref[...] = acc_ref[:, cols(d)]
                        sib_ref[...] = land_buf[chip]

                    @pl.when(chip != my_chip)
                    def _():
                        pair_buf[chip] = (acc_ref[:, cols(d)] + land_buf[chip].astype(F32)).astype(BF16)
                        to_chip(chip, chip ^ my_chip).start()
            for chip in range(N_CHIPS):
                hand_over(chip).wait_send()

        @pl.when(i >= n)
        def _():
            x_hat, r = normed_x()
            dx_ref[...], dg1 = _grad_x_tile(dq_ref[...], dg_ref[...], x_hat, r, g1_ref[...], w_ref, dh_ref[...])
            dg1_ref[0:1, :] += dg1

        @pl.when(i == n_steps - 1)
        def _():
            for rel in range(1, n_far + 1):
                to_chip(0, rel).wait()

    both = lambda w_: pl.BlockSpec((tm, w_), lambda i: (i % n, 0))
    second = pl.BlockSpec((tm, D_MODEL), lambda i: (jnp.maximum(i - n, 0), 0))
    whole = lambda dtype: jax.ShapeDtypeStruct(shard, dtype)
    sems = lambda k: pltpu.SemaphoreType.DMA((k,))
    res = pl.pallas_call(
        body, name="in_proj_bwd", grid=(n_steps,),
        in_specs=[both(QKV_W), both(GATES_W), both(D_MODEL), second, _full((1, D_MODEL)), _resident((D_MODEL, IN_COLS)),
                  HBM_SPEC],
        out_specs=[second, _full(shard), _full(shard), HBM_SPEC, _full((SMALL_ROWS, D_MODEL)), HBM_SPEC],
        out_shape=[jax.ShapeDtypeStruct((t, D_MODEL), F32), whole(F32), whole(BF16),
                   jax.ShapeDtypeStruct((n_far,) + shard, BF16), jax.ShapeDtypeStruct((SMALL_ROWS, D_MODEL), F32),
                   jax.ShapeDtypeStruct(out_sums.shape, out_sums.dtype)],
        scratch_shapes=[pltpu.VMEM((D_MODEL, IN_COLS), F32), pltpu.VMEM((N_CHIPS,) + shard, BF16),
                        pltpu.VMEM((N_CHIPS,) + shard, BF16), pltpu.VMEM((N_CHIPS,) + shard, BF16),
                        sems(N_CHIPS), sems(N_CHIPS), sems(n_far), sems(n_far), sems(n_far), sems(n_far)],
        compiler_params=_params("arbitrary", barrier_id=7),
    )(dqkv, dgates, x, dh, g1, w_in, out_sums)
    return res[0], (res[1], res[2], res[3]), res[4], res[5]


def _all_gather(shards, name):
    n = len(shards)

    def body(*refs):
        _enter_with(_sibling_and_chips(*_mesh_pos()))
        start, finish = _gather_steps(refs[:n], refs[n:2 * n], *refs[2 * n:])
        start()
        finish()

    return pl.pallas_call(
        body, name=name,
        in_specs=[HBM_SPEC] * n, out_specs=[HBM_SPEC] * n,
        out_shape=[jax.ShapeDtypeStruct((N_DEV,) + s.shape, s.dtype) for s in shards],
        scratch_shapes=[pltpu.SemaphoreType.DMA((7 * n,)), pltpu.SemaphoreType.DMA((7 * n,)),
                        pltpu.SemaphoreType.DMA((n,))],
        compiler_params=_params(barrier_id=8),
    )(*shards)


def _adam_math(w, g, m, v):
    m = ADAM_B1 * m + (1.0 - ADAM_B1) * g
    v = ADAM_B2 * v + (1.0 - ADAM_B2) * (g * g)
    m_hat = m / (1.0 - ADAM_B1 ** ADAM_STEP)
    v_hat = v / (1.0 - ADAM_B2 ** ADAM_STEP)
    delta = -ADAM_LR * (m_hat / (jnp.sqrt(v_hat) + ADAM_EPS) + ADAM_WD * w)
    return delta, m, v


def _adamw_reduced(w, m, v, own, from_sibling, from_chips, tr):
    rows, cols = w.shape

    def body(w_ref, m_ref, v_ref, own_ref, sib_ref, far_ref, g_ref, d_ref, nm_ref, nv_ref):
        g = own_ref[...] + sib_ref[...].astype(F32)
        for k in range(len(CHIP_FLIPS)):
            g = g + far_ref[k].astype(F32)
        g_ref[...] = g
        d_ref[...], nm_ref[...], nv_ref[...] = _adam_math(w_ref[...], g, m_ref[...], v_ref[...])

    tile = pl.BlockSpec((tr, cols), lambda i: (i, 0))
    out = jax.ShapeDtypeStruct((rows, cols), F32)
    return pl.pallas_call(
        body, name="adamw_reduced", grid=(rows // tr,),
        in_specs=[tile] * 5 + [pl.BlockSpec((len(CHIP_FLIPS), tr, cols), lambda i: (0, i, 0))],
        out_specs=[tile] * 4, out_shape=[out] * 4,
        compiler_params=_params("parallel"),
    )(w, m, v, own, from_sibling, from_chips)


def _sum_devices(gathered):
    _, rows, cols = gathered.shape

    def body(g_ref, o_ref):
        s = g_ref[0]
        for d in range(1, N_DEV):
            s = s + g_ref[d]
        o_ref[...] = s

    return pl.pallas_call(
        body, name="sum_devices", in_specs=[_full(gathered.shape)], out_specs=_full((rows, cols)), grid=(1,),
        out_shape=jax.ShapeDtypeStruct((rows, cols), F32),
    )(gathered)


def _adamw_small(w, g, m, v):
    def body(w_ref, g_ref, m_ref, v_ref, d_ref, nm_ref, nv_ref):
        d_ref[...], nm_ref[...], nv_ref[...] = _adam_math(w_ref[...], g_ref[...], m_ref[...], v_ref[...])

    spec = _full(w.shape)
    out = jax.ShapeDtypeStruct(w.shape, F32)
    return pl.pallas_call(
        body, name="adamw_small", grid=(1,), in_specs=[spec] * 4, out_specs=[spec] * 3, out_shape=[out] * 3,
    )(w, g, m, v)


TOKEN_TILE = 512
MID_TILE = 256
MID_CHUNK = 1024
MID_CHUNKS = D_FF // MID_CHUNK
ADAM_ROWS = 128


def _local_grads(x, target, g1, w_in_shard, conv_shard, sinks, g_attn, g_conv, g2, g3, g4, shards, order):
    t = x.shape[0]
    tm = min(TOKEN_TILE, t)
    rope = _rope_tables(t)
    qkv, gates, mconv, w_in, conv_w, gathered = _in_proj_fwd(x, g1, w_in_shard, conv_shard, g_conv, rope, tm, shards,
                                                             (False, True, False))
    attn, mattn, (w_out, w_up, w_down) = _attn_fwd(qkv, sinks, g_attn, shards, gathered)
    act, dup, hn2t, dmo, dmix, dh, dmixed, small_mid = _mid(
        mattn, mconv, x, target, g2, g3, g4, w_out.reshape(D_MODEL, D_MODEL),
        w_up, w_down.reshape(D_FF, D_MODEL), min(MID_TILE, t))
    up_own, up_sib, up_sums = _dw_pair_sums((hn2t, dup), order, "up", "dw_up", 2)
    down_own, down_sib, down_sums, up_far = _dw_pair_sums((act, dmo), order, "down", "dw_down", 3, ride=up_sums)
    out_own, out_sib, out_sums = _dw_pair_sums((mattn, mconv, dmix), order, "out", "dw_out", 4)
    dgates, small_conv = _conv_bwd(dmixed, gates, g_conv, conv_w, tm)
    dqkv, dsink, dg_attn, down_far = _attn_bwd(qkv, dmixed, attn, g_attn, sinks, rope, down_sums)
    grad_x, dw_in, small_in, out_far = _in_proj_bwd(dqkv, dgates, x, dh, g1, w_in, tm, out_sums)
    dw_out, dw_up, dw_down = (out_own, out_sib, out_far), (up_own, up_sib, up_far), (down_own, down_sib, down_far)
    return grad_x, dw_in, dw_out, dw_up, dw_down, (small_mid, small_conv, dg_attn, dsink, small_in)


def _pack_small(small_mid, small_mix, dg_attn, dsink, small_in):
    z = lambda n: jnp.zeros((1, n), F32)
    rows = [
        small_mid[ROW_LOSS:ROW_LOSS + 1],
        small_in[0:1],
        small_mid[ROW_G2:ROW_G2 + 1],
        small_mid[ROW_G3:ROW_G3 + 1],
        small_mid[ROW_G4:ROW_G4 + 1],
        jnp.concatenate([dg_attn[0:1], small_mix[ROW_GCONV:ROW_GCONV + 1]], axis=1),
        jnp.concatenate([small_mix[ROW_CW0:ROW_CW0 + 1], small_mix[ROW_CW0 + 1:ROW_CW0 + 2]], axis=1),
        jnp.concatenate([small_mix[ROW_CW0 + 2:ROW_CW0 + 3], dsink[0:1, :], z(D_MODEL - CONV_W - 128)], axis=1),
    ]
    return jnp.concatenate(rows, axis=0)


def kernel(x, pre_mix_norm, w_in, conv_w, attn_sinks, attn_group_norm, conv_group_norm, w_out, post_mix_norm, pre_mlp_norm, w_up, w_down, post_mlp_norm, loss_target, m_pre_mix_norm, m_w_in, m_conv_w, m_attn_sinks, m_attn_group_norm, m_conv_group_norm, m_w_out, m_post_mix_norm, m_pre_mlp_norm, m_w_up, m_w_down, m_post_mlp_norm, v_pre_mix_norm, v_w_in, v_conv_w, v_attn_sinks, v_attn_group_norm, v_conv_group_norm, v_w_out, v_post_mix_norm, v_pre_mlp_norm, v_w_up, v_w_down, v_post_mlp_norm):
    xi, yi, ci = _mesh_pos()
    chip = 2 * xi + yi
    dev = 2 * chip + ci

    order = _block_order(dev)

    shards = [w_out[0].astype(BF16), w_up[0].astype(BF16), w_down[0].astype(BF16)]

    grad_x, dw_in, dw_out, dw_up, dw_down, smalls = _local_grads(
        x[0], loss_target[0], pre_mix_norm, w_in[0].astype(BF16), conv_w[0], attn_sinks, attn_group_norm, conv_group_norm,
        post_mix_norm, pre_mlp_norm, post_mlp_norm, shards, order)

    small = _sum_devices(_all_gather([_pack_small(*smalls)], "gather_small")[0])
    loss = (0.5 / D_MODEL) * jnp.sum(small[0])

    big = {}
    for name, w, m, v, (own, sib, far) in zip(
            ("w_in", "w_out", "w_up", "w_down"), (w_in, w_out, w_up, w_down), (m_w_in, m_w_out, m_w_up, m_w_down),
            (v_w_in, v_w_out, v_w_up, v_w_down), (dw_in, dw_out, dw_up, dw_down)):
        big[name] = [a[None] for a in _adamw_reduced(w[0], m[0], v[0], own, sib, far, ADAM_ROWS)]

    conv_g = lax.dynamic_slice(
        jnp.stack([small[6, :CONV_W], small[6, CONV_W:], small[7, :CONV_W]]), (0, dev * (CONV_W // N_DEV)),
        (3, CONV_W // N_DEV))
    pad = lambda a, n: jnp.pad(a.reshape(1, -1), ((0, 0), (0, n - a.size)))
    small_names = ("pre_mix_norm", "post_mix_norm", "pre_mlp_norm", "post_mlp_norm")
    small_w = {"pre_mix_norm": (pre_mix_norm, m_pre_mix_norm, v_pre_mix_norm),
               "post_mix_norm": (post_mix_norm, m_post_mix_norm, v_post_mix_norm),
               "pre_mlp_norm": (pre_mlp_norm, m_pre_mlp_norm, v_pre_mlp_norm),
               "post_mlp_norm": (post_mlp_norm, m_post_mlp_norm, v_post_mlp_norm)}

    def pack(k):
        rows = [small_w[nm][k] for nm in small_names]
        rows.append(jnp.concatenate([(attn_group_norm, m_attn_group_norm, v_attn_group_norm)[k],
                                     (conv_group_norm, m_conv_group_norm, v_conv_group_norm)[k]], axis=1))
        rows.append(pad((conv_w, m_conv_w, v_conv_w)[k], D_MODEL))
        rows.append(pad((attn_sinks, m_attn_sinks, v_attn_sinks)[k], D_MODEL))
        rows.append(jnp.zeros((1, D_MODEL), F32))
        return jnp.concatenate(rows, axis=0)

    g_small = jnp.concatenate(
        [small[1:6], pad(conv_g, D_MODEL), pad(small[7, CONV_W:CONV_W + N_HEADS], D_MODEL), jnp.zeros((1, D_MODEL), F32)],
        axis=0)
    d_small, nm_small, nv_small = _adamw_small(pack(0), g_small, pack(1), pack(2))

    def unpack(a):
        nconv = 3 * CONV_W // N_DEV
        return {"pre_mix_norm": a[0:1], "post_mix_norm": a[1:2], "pre_mlp_norm": a[2:3], "post_mlp_norm": a[3:4],
                "attn_group_norm": a[4:5, :ATTN_W], "conv_group_norm": a[4:5, ATTN_W:],
                "conv_w": a[5, :nconv].reshape(1, 3, CONV_W // N_DEV), "attn_sinks": a[6:7, :N_HEADS]}

    order = ("pre_mix_norm", "w_in", "conv_w", "attn_sinks", "attn_group_norm", "conv_group_norm", "w_out",
             "post_mix_norm", "pre_mlp_norm", "w_up", "w_down", "post_mlp_norm")
    outs = []
    for k, a in enumerate((g_small, d_small, nm_small, nv_small)):
        sm = unpack(a)
        outs += [big[nm][k] if nm in big else sm[nm] for nm in order]
    return (loss, grad_x[None], *outs)
```

```python
import functools

import jax
import jax.numpy as jnp
import numpy as np
from jax import lax
from jax.experimental import pallas as pl
from jax.experimental.pallas import tpu as pltpu

F32 = jnp.float32
BF16 = jnp.bfloat16

D_MODEL = 1024
HEAD_DIM = 64
ATTN_W = 512
CONV_W = 512
N_HEADS = 8
N_KV = 2
GROUP = 4
KV_W = 128
QKV_W = ATTN_W + 2 * KV_W
GATES_W = 3 * CONV_W
IN_COLS = QKV_W + GATES_W
D_FF = 4096
FF_CHUNK = 512
N_FF_CHUNKS = D_FF // FF_CHUNK
BLOCK = 128
ROT_HALF = 8
ROPE_THETA = 500000.0
NORM_EPS = 1e-6
NEG_INF = -1e30
ATTN_SCALE = 0.125
N_DEV = 8
N_CHIPS = 4
IN_SHARD = IN_COLS // N_DEV

ADAM_LR = 0.001
ADAM_B1 = 0.9
ADAM_B2 = 0.999
ADAM_EPS = 1e-08
ADAM_WD = 0.01
ADAM_STEP = 10

V7X_VMEM_BYTES = 64 * 1024 * 1024
VMEM_LIMIT = V7X_VMEM_BYTES - 2 * 1024 * 1024

MESH = pl.DeviceIdType.MESH
HBM_SPEC = pl.BlockSpec(memory_space=pltpu.HBM)


def _params(*sem, barrier_id=None):
    return pltpu.CompilerParams(dimension_semantics=sem or None, vmem_limit_bytes=VMEM_LIMIT, collective_id=barrier_id)


def _mm(a, b):
    return jnp.dot(a, b, preferred_element_type=F32)


def _mm_nt(a, b):
    return lax.dot_general(a, b, (((1,), (1,)), ((), ())), preferred_element_type=F32)


def _mm_tn(a, b):
    return lax.dot_general(a, b, (((0,), (0,)), ((), ())), preferred_element_type=F32)


def _inv_rms(x):
    return lax.rsqrt(jnp.mean(x * x, axis=-1, keepdims=True) + NORM_EPS)


def _rms_bwd(xhat, r, gain, dy):
    gy = dy * gain
    return r * (gy - xhat * jnp.mean(gy * xhat, axis=-1, keepdims=True)), dy * xhat


def _colsum(a):
    return jnp.sum(a, axis=0, keepdims=True)


def _full(shape):
    zeros = (0,) * len(shape)
    return pl.BlockSpec(shape, lambda *_: zeros)


def _resident(shape):
    zeros = (0,) * len(shape)
    return pl.BlockSpec(shape, lambda *_: zeros, pipeline_mode=pl.Buffered(1))


def _rope_tables(t):
    pos = np.arange(t, dtype=np.float32)
    inv_freq = (ROPE_THETA ** (-np.arange(0, 2 * ROT_HALF, 2, dtype=np.float64) / (2 * ROT_HALF))).astype(np.float32)
    ang = (pos[:, None] * inv_freq[None, :]).astype(np.float64)
    cos, sin = np.cos(ang).astype(np.float32), np.sin(ang).astype(np.float32)
    zeros8 = np.zeros((t, ROT_HALF), np.float32)
    rest = np.zeros((t, HEAD_DIM - 2 * ROT_HALF), np.float32)
    c_head = np.concatenate([cos, cos, rest + 1.0], axis=1)
    s1_head = np.concatenate([zeros8, sin, rest], axis=1)
    s2_head = np.concatenate([-sin, zeros8, rest], axis=1)
    two = lambda a: jnp.asarray(np.concatenate([a, a], axis=1))
    return two(c_head), two(s1_head), two(s2_head)


def _rope(v, c, s1, s2):
    return v * c + pltpu.roll(v, ROT_HALF, 1) * s1 + pltpu.roll(v, 128 - ROT_HALF, 1) * s2


def _rope_transpose(dv, c, s1, s2):
    return dv * c + pltpu.roll(dv * s1, 128 - ROT_HALF, 1) + pltpu.roll(dv * s2, ROT_HALF, 1)


def _shift_rows_down(u, prev, k):
    row = lax.broadcasted_iota(jnp.int32, u.shape, 0)
    out = pltpu.roll(u, k, 0)
    for r in range(k):
        out = jnp.where(row == r, prev[8 - k + r:8 - k + r + 1, :], out)
    return out


def _shift_rows_up(u, nxt, k):
    n = u.shape[0]
    row = lax.broadcasted_iota(jnp.int32, u.shape, 0)
    out = pltpu.roll(u, n - k, 0)
    for r in range(k):
        out = jnp.where(row == n - k + r, nxt[r:r + 1, :], out)
    return out


def _conv3(u, u1, u2, w):
    return (w[0:1, :] * u2 + w[1:2, :] * u1) + w[2:3, :] * u


def _mesh_pos():
    return lax.axis_index("x"), lax.axis_index("y"), lax.axis_index("c")


def _slot(ref, pos):
    dev = 4 * pos[0] + 2 * pos[1] + pos[2]
    if len(ref.shape) == 2:
        width = ref.shape[1] // N_DEV
        return ref.at[:, pl.ds(pl.multiple_of(dev * width, width), width)]
    return ref.at[dev]


def _gathered_shape(shard, by_cols):
    if by_cols:
        return jax.ShapeDtypeStruct((shard.shape[0], N_DEV * shard.shape[1]), shard.dtype)
    return jax.ShapeDtypeStruct((N_DEV,) + shard.shape, shard.dtype)


def _enter_with(peers):
    barrier = pltpu.get_barrier_semaphore()
    for peer in peers:
        pl.semaphore_signal(barrier, inc=1, device_id=peer, device_id_type=MESH)
    pl.semaphore_wait(barrier, len(peers))


def _sibling_and_chips(x, y, c):
    return [(x, y, 1 - c), (1 - x, y, c), (x, 1 - y, c), (1 - x, 1 - y, c)]


def _push(src, dst, sems, k, to):
    send_sems, recv_sems = sems
    return pltpu.make_async_remote_copy(src_ref=src, dst_ref=dst, send_sem=send_sems.at[k], recv_sem=recv_sems.at[k],
                                        device_id=to, device_id_type=MESH)


def _gather_steps(shards, outs, send_sems, recv_sems, local_sems):
    n = len(shards)
    x, y, c = _mesh_pos()
    me, sibling = (x, y, c), (x, y, 1 - c)
    chips = [(1 - x, y), (x, 1 - y), (1 - x, 1 - y)]

    def copy(i, k, block, to, src=None):
        dst = _slot(outs[i], block)
        return _push(dst if src is None else src, dst, (send_sems, recv_sems), 7 * i + k, to)

    mine = [pltpu.make_async_copy(shards[i], _slot(outs[i], me), local_sems.at[i]) for i in range(n)]
    first = []
    for i in range(n):
        first.append(copy(i, 0, me, sibling, src=shards[i]))
        first += [copy(i, 1 + j, me, (*chip, c), src=shards[i]) for j, chip in enumerate(chips)]

    def start():
        for cp in mine + first:
            cp.start()

    def finish():
        passed = []
        for j, chip in enumerate(chips):
            for i in range(n):
                copy(i, 1 + j, (*chip, c), me).wait_recv()
                cp = copy(i, 4 + j, (*chip, c), sibling)
                cp.start()
                passed.append(cp)
        for i in range(n):
            copy(i, 0, sibling, me).wait_recv()
            for j, chip in enumerate(chips):
                copy(i, 4 + j, (*chip, 1 - c), me).wait_recv()
        for cp in first + passed:
            cp.wait_send()
        for cp in mine:
            cp.wait()

    return start, finish


def _gather_near(first, last, shards, outs, sems, local_sems):
    x, y, c = _mesh_pos()
    me, peers = (x, y, c), [(x, y, 1 - c), (1 - x, y, c), (x, 1 - y, c)]
    n = len(shards)
    local = [pltpu.make_async_copy(shards[i], _slot(outs[i], me), local_sems.at[i]) for i in range(n)]
    sends = [_push(shards[i], _slot(outs[i], me), sems, 3 * i + k, peers[k]) for i in range(n) for k in range(3)]
    arrivals = [_push(shards[i], _slot(outs[i], peers[k]), sems, 3 * i + k, peers[k]) for i in range(n) for k in range(3)]

    def start():
        for cp in local + sends:
            cp.start()

    if first is not None:
        pl.when(first)(start)

    @pl.when(last)
    def _():
        for cp in sends:
            cp.wait_send()
        for cp in arrivals:
            cp.wait_recv()
        for cp in local:
            cp.wait()

    return start


def _gather_far(first, last, shards, ins, outs, sems):
    x, y, c = _mesh_pos()
    me, sibling = (x, y, c), (x, y, 1 - c)
    chips = [(1 - x, y), (x, 1 - y), (1 - x, 1 - y)]
    n = len(shards)
    diag_send = [_push(shards[i], _slot(outs[i], me), sems, 4 * i, (*chips[2], c)) for i in range(n)]
    diag_arrival = [_push(shards[i], _slot(outs[i], (*chips[2], c)), sems, 4 * i, (*chips[2], c)) for i in range(n)]
    passed = [[_push(_slot(ins[i], (*chips[j], c)), _slot(outs[i], (*chips[j], c)), sems, 4 * i + 1 + j, sibling)
               for i in range(n)] for j in range(3)]
    from_sibling = [_push(shards[i], _slot(outs[i], (*chips[j], 1 - c)), sems, 4 * i + 1 + j, sibling)
                    for i in range(n) for j in range(3)]

    @pl.when(first)
    def _():
        for cp in diag_send + passed[0] + passed[1]:
            cp.start()

    @pl.when(last)
    def _():
        for cp in diag_arrival:
            cp.wait_recv()
        for cp in passed[2]:
            cp.start()
        for cp in from_sibling:
            cp.wait_recv()
        for cp in diag_send + passed[0] + passed[1] + passed[2]:
            cp.wait_send()


def _in_proj_fwd(x, g1, w_in, conv_w, g_conv, rope, tm, shards, by_cols):
    t = x.shape[0]
    rc, rs1, rs2 = rope
    n = len(shards)
    n_tiles = t // tm

    def body(*refs):
        x_ref, g1_ref, w_ref, cw_ref, gc_ref, c_ref, s1_ref, s2_ref = refs[:8]
        shard_refs = refs[8:8 + n]
        qkv_ref, gates_ref, mconv_ref, w_full_ref, cw_full_ref = refs[8 + n:13 + n]
        gathered = refs[13 + n:13 + 2 * n]
        carry_ref, w_land, cw_land, hn_ref = refs[13 + 2 * n:17 + 2 * n]
        now_sems = refs[17 + 2 * n:20 + 2 * n]
        step = pl.program_id(0)
        start_later_weights = _gather_near(None, step == 2 * n_tiles - 1, shard_refs, gathered,
                                           refs[20 + 2 * n:22 + 2 * n], refs[22 + 2 * n]) if n else None
        start_w_in, finish_w_in = _gather_steps([w_ref, cw_ref], [w_land, cw_land], *now_sems)

        @pl.when(step == 0)
        def _():
            carry_ref[...] = jnp.zeros_like(carry_ref)
            _enter_with(_sibling_and_chips(*_mesh_pos()))
            start_w_in()
            if start_later_weights is not None:
                start_later_weights()

        @pl.when(step < n_tiles)
        def _():
            xv = x_ref[...]
            hn_ref[step] = ((xv * _inv_rms(xv)) * g1_ref[...]).astype(BF16)

        @pl.when(step == n_tiles)
        def _():
            finish_w_in()
            conv_shard = CONV_W // N_DEV
            for d in range(N_DEV):
                w_full_ref[:, IN_SHARD * d:IN_SHARD * (d + 1)] = w_land[d]
                cw_full_ref[:, conv_shard * d:conv_shard * (d + 1)] = cw_land[d]

        @pl.when(step >= n_tiles)
        def _():
            proj = _mm(hn_ref[step - n_tiles], w_full_ref[...])
            c, s1, s2 = c_ref[...], s1_ref[...], s2_ref[...]
            for ci in range((ATTN_W + KV_W) // 128):
                sl = slice(128 * ci, 128 * (ci + 1))
                qkv_ref[:, sl] = _rope(proj[:, sl], c, s1, s2).astype(BF16)
            qkv_ref[:, ATTN_W + KV_W:QKV_W] = proj[:, ATTN_W + KV_W:QKV_W].astype(BF16)
            gates = proj[:, QKV_W:]
            gates_ref[...] = gates
            gb, gcc, xin = gates[:, :CONV_W], gates[:, CONV_W:2 * CONV_W], gates[:, 2 * CONV_W:]
            u = gcc * xin
            prev = carry_ref[...]
            conv = gb * _conv3(u, _shift_rows_down(u, prev, 1), _shift_rows_down(u, prev, 2), cw_full_ref[...])
            carry_ref[...] = u[tm - 8:tm, :]
            mconv_ref[...] = ((conv * _inv_rms(conv)) * gc_ref[...]).astype(BF16)

    first_pass = pl.BlockSpec((tm, D_MODEL), lambda i: (jnp.minimum(i, n_tiles - 1), 0))
    tile = lambda w_: pl.BlockSpec((tm, w_), lambda i: (jnp.maximum(i - n_tiles, 0), 0))
    sems = lambda k: pltpu.SemaphoreType.DMA((k,))
    res = pl.pallas_call(
        body, name="in_proj_fwd", grid=(2 * n_tiles,),
        in_specs=[first_pass, _full((1, D_MODEL)), HBM_SPEC, HBM_SPEC, _full((1, CONV_W)), tile(128), tile(128),
                  tile(128)] + [HBM_SPEC] * n,
        out_specs=[tile(QKV_W), tile(GATES_W), tile(CONV_W), _full((D_MODEL, IN_COLS)), _full((3, CONV_W))]
        + [HBM_SPEC] * n,
        out_shape=[jax.ShapeDtypeStruct((t, QKV_W), BF16), jax.ShapeDtypeStruct((t, GATES_W), F32),
                   jax.ShapeDtypeStruct((t, CONV_W), BF16), jax.ShapeDtypeStruct((D_MODEL, IN_COLS), BF16),
                   jax.ShapeDtypeStruct((3, CONV_W), F32)]
        + [_gathered_shape(s, cols) for s, cols in zip(shards, by_cols)],
        scratch_shapes=[pltpu.VMEM((8, CONV_W), F32), pltpu.VMEM((N_DEV,) + w_in.shape, BF16),
                        pltpu.VMEM((N_DEV,) + conv_w.shape, F32), pltpu.VMEM((n_tiles, tm, D_MODEL), BF16),
                        sems(14), sems(14), sems(2)]
        + ([sems(3 * n), sems(3 * n), sems(n)] if n else []),
        compiler_params=_params("arbitrary", barrier_id=0),
    )(x, g1, w_in, conv_w, g_conv, rc, rs1, rs2, *shards)
    return res[0], res[1], res[2], res[3], res[4], list(res[5:])


GROUP_COLS = GROUP * BLOCK
ATTN_STEP_BLOCKS = 4


def _attn_masks(has_prev):
    key = lax.broadcasted_iota(jnp.int32, (2 * BLOCK, GROUP_COLS), 0)
    query = lax.broadcasted_iota(jnp.int32, (2 * BLOCK, GROUP_COLS), 1) & (BLOCK - 1)
    band = (key > query) & (key <= query + BLOCK)
    return [band & ((key >= BLOCK) | has_prev)] + [band] * (ATTN_STEP_BLOCKS - 1)


def _heads_side_by_side(at, g, b):
    heads = [at[HEAD_DIM * (GROUP * g + hh):HEAD_DIM * (GROUP * g + hh + 1), BLOCK * b:BLOCK * (b + 1)] for hh in range(GROUP)]
    return jnp.concatenate(heads, axis=1)


def _to_token_rows(parts):
    rows = [jnp.concatenate([parts[b][g][:, BLOCK * hh:BLOCK * (hh + 1)] for b in range(ATTN_STEP_BLOCKS)], axis=1)
            for g in range(N_KV) for hh in range(GROUP)]
    return jnp.concatenate(rows, axis=0).T


def _group_sinks(sink_ref, g):
    head = lax.broadcasted_iota(jnp.int32, (1, GROUP_COLS), 1) // BLOCK
    out = jnp.full((1, GROUP_COLS), sink_ref[0, GROUP * g], F32)
    for hh in range(1, GROUP):
        out = jnp.where(head == hh, sink_ref[0, GROUP * g + hh], out)
    return out


def _attn_probs(qt, kk, sink, valid):
    s = jnp.where(valid, _mm(kk, qt), NEG_INF)
    m = jnp.maximum(jnp.max(s, axis=0, keepdims=True), sink)
    p = jnp.exp(s - m)
    psink = jnp.exp(sink - m)
    inv_l = 1.0 / (jnp.sum(p, axis=0, keepdims=True) + psink)
    return p * inv_l, psink * inv_l


ATTN_STEP = ATTN_STEP_BLOCKS * BLOCK
ATTN_KEYS = ATTN_STEP + BLOCK


def _qkv_specs(order):
    prev = lambda i: jnp.maximum(ATTN_STEP_BLOCKS * order(i) - 1, 0)
    kcol, vcol = ATTN_W // KV_W, ATTN_W // KV_W + 1
    return [pl.BlockSpec((ATTN_STEP, ATTN_W), lambda i: (order(i), 0)),
            pl.BlockSpec((BLOCK, KV_W), lambda i: (prev(i), kcol)), pl.BlockSpec((ATTN_STEP, KV_W), lambda i: (order(i), kcol)),
            pl.BlockSpec((BLOCK, KV_W), lambda i: (prev(i), vcol)), pl.BlockSpec((ATTN_STEP, KV_W), lambda i: (order(i), vcol))]


def _attn_fwd(qkv, sinks, g_attn, shards, gathered):
    t = qkv.shape[0]
    n = len(shards)

    def body(*refs):
        sink_ref, q_ref, kp_ref, kc_ref, vp_ref, vc_ref, ga_ref = refs[:7]
        attn_ref, mattn_ref = refs[7 + 2 * n:9 + 2 * n]
        step = pl.program_id(0)
        if n:
            @pl.when(step == 0)
            def _():
                x, y, c = _mesh_pos()
                _enter_with([(x, y, 1 - c), (1 - x, 1 - y, c)])

            _gather_far(step == 0, step == pl.num_programs(0) - 1, refs[7:7 + n], refs[7 + n:7 + 2 * n],
                        refs[9 + 2 * n:9 + 3 * n], refs[9 + 3 * n:11 + 3 * n])
        qt = (q_ref[...] * ATTN_SCALE).T
        keys = jnp.concatenate([kp_ref[...], kc_ref[...]], axis=0)
        vals = jnp.concatenate([vp_ref[...], vc_ref[...]], axis=0)
        sink = [_group_sinks(sink_ref, g) for g in range(N_KV)]
        masks = _attn_masks(step > 0)
        parts = []
        for b in range(ATTN_STEP_BLOCKS):
            window = slice(BLOCK * b, BLOCK * (b + 2))
            valid = masks[b]
            parts.append([])
            for g in range(N_KV):
                gs = slice(HEAD_DIM * g, HEAD_DIM * (g + 1))
                probs, _ = _attn_probs(_heads_side_by_side(qt, g, b), keys[window, gs], sink[g], valid)
                parts[b].append(_mm_tn(vals[window, gs], probs.astype(BF16)))
        attn = _to_token_rows(parts)
        attn_ref[...] = attn
        mattn_ref[...] = ((attn * _inv_rms(attn)) * ga_ref[...]).astype(BF16)

    blk = pl.BlockSpec((ATTN_STEP, ATTN_W), lambda j: (j, 0))
    res = pl.pallas_call(
        body, name="attn_fwd", grid=(t // ATTN_STEP,),
        in_specs=[pl.BlockSpec(memory_space=pltpu.SMEM)] + _qkv_specs(lambda j: j) + [_full((1, ATTN_W))]
        + [HBM_SPEC] * (2 * n),
        out_specs=[blk, blk] + [HBM_SPEC] * n,
        out_shape=[jax.ShapeDtypeStruct((t, ATTN_W), F32), jax.ShapeDtypeStruct((t, ATTN_W), BF16)]
        + [jax.ShapeDtypeStruct(g.shape, g.dtype) for g in gathered],
        input_output_aliases={7 + n + i: 2 + i for i in range(n)},
        scratch_shapes=[pltpu.SemaphoreType.DMA((4 * n,)), pltpu.SemaphoreType.DMA((4 * n,))] if n else [],
        compiler_params=_params("arbitrary", barrier_id=1 if n else None),
    )(sinks, qkv, qkv, qkv, qkv, qkv, g_attn, *shards, *gathered)
    return res[0], res[1], list(res[2:])


SMALL_ROWS = 8
ROW_LOSS, ROW_G2, ROW_G3, ROW_G4 = 0, 1, 2, 3


def _mid(mattn, mconv, x, target, g2, g3, g4, w_out, w_up, w_down, tm):
    t = x.shape[0]

    def body(ma_ref, mc_ref, x_ref, t_ref, g2_ref, g3_ref, g4_ref, wo_ref, wu_ref, wd_ref,
             act_ref, dup_ref, hn2t_ref, dmo_ref, dmix_ref, dh_ref, dmixed_ref, small_ref, up_ref):
        @pl.when(pl.program_id(0) == 0)
        def _():
            small_ref[...] = jnp.zeros_like(small_ref)

        g2, g3, g4 = g2_ref[...], g3_ref[...], g4_ref[...]
        mix_out = _mm(ma_ref[...], wo_ref[0:ATTN_W, :]) + _mm(mc_ref[...], wo_ref[ATTN_W:, :])
        r2 = _inv_rms(mix_out)
        mo_hat = mix_out * r2
        h = x_ref[...] + mo_hat * g2
        r3 = _inv_rms(h)
        h_hat = h * r3
        hn2 = (h_hat * g3).astype(BF16)
        hn2t_ref[...] = hn2.T
        for j in range(MID_CHUNKS):
            cols_j = slice(MID_CHUNK * j, MID_CHUNK * (j + 1))
            up = jnp.maximum(_mm(hn2, wu_ref[:, cols_j]), 0.0)
            up_ref[:, cols_j] = up.astype(BF16)
            act_ref[:, cols_j] = (up * up).astype(BF16)
        mlp = _mm(act_ref[...], wd_ref[...])
        r4 = _inv_rms(mlp)
        ml_hat = mlp * r4
        err = (h + ml_hat * g4) - t_ref[...]
        d_out = err * (1.0 / D_MODEL)
        d_mlp, dg4 = _rms_bwd(ml_hat, r4, g4, d_out)
        dmo = d_mlp.astype(BF16)
        dmo_ref[...] = dmo
        for j in range(MID_CHUNKS):
            cols_j = slice(MID_CHUNK * j, MID_CHUNK * (j + 1))
            dact = _mm_nt(dmo, wd_ref[cols_j, :])
            dup_ref[:, cols_j] = (dact * (2.0 * up_ref[:, cols_j].astype(F32))).astype(BF16)
        dhn2 = _mm_nt(dup_ref[...], wu_ref[...])
        dh_norm, dg3 = _rms_bwd(h_hat, r3, g3, dhn2)
        dh = d_out + dh_norm
        dh_ref[...] = dh
        d_mix, dg2 = _rms_bwd(mo_hat, r2, g2, dh)
        dmix = d_mix.astype(BF16)
        dmix_ref[...] = dmix
        dmixed_ref[...] = _mm_nt(dmix, wo_ref[...])
        small_ref[ROW_LOSS:ROW_LOSS + 1, :] += _colsum(err * err)
        small_ref[ROW_G2:ROW_G2 + 1, :] += _colsum(dg2)
        small_ref[ROW_G3:ROW_G3 + 1, :] += _colsum(dg3)
        small_ref[ROW_G4:ROW_G4 + 1, :] += _colsum(dg4)

    tile = lambda n: pl.BlockSpec((tm, n), lambda i: (i, 0))
    cols = lambda n: pl.BlockSpec((n, tm), lambda i: (0, i))
    gain = _full((1, D_MODEL))
    return pl.pallas_call(
        body, name="mid_fwd_bwd", grid=(t // tm,),
        in_specs=[tile(ATTN_W), tile(CONV_W), tile(D_MODEL), tile(D_MODEL), gain, gain, gain,
                  _resident((D_MODEL, D_MODEL)), _resident((D_MODEL, D_FF)), _resident((D_FF, D_MODEL))],
        out_specs=[tile(D_FF), tile(D_FF), cols(D_MODEL), tile(D_MODEL), tile(D_MODEL), tile(D_MODEL), tile(D_MODEL),
                   _full((SMALL_ROWS, D_MODEL))],
        out_shape=[jax.ShapeDtypeStruct((t, D_FF), BF16), jax.ShapeDtypeStruct((t, D_FF), BF16),
                   jax.ShapeDtypeStruct((D_MODEL, t), BF16), jax.ShapeDtypeStruct((t, D_MODEL), BF16),
                   jax.ShapeDtypeStruct((t, D_MODEL), BF16), jax.ShapeDtypeStruct((t, D_MODEL), F32),
                   jax.ShapeDtypeStruct((t, D_MODEL), F32), jax.ShapeDtypeStruct((SMALL_ROWS, D_MODEL), F32)],
        scratch_shapes=[pltpu.VMEM((tm, D_FF), BF16)],
        compiler_params=_params("arbitrary"),
    )(mattn, mconv, x, target, g2, g3, g4, w_out, w_up, w_down)


CHIP_FLIPS = ((1, 1), (1, 0), (0, 1))


def _block_order(dev):
    chip_masks = [4 * fx + 2 * fy for fx, fy in CHIP_FLIPS]
    masks = [m + 1 for m in chip_masks] + [1] + chip_masks + [0]
    return jnp.bitwise_xor(dev, jnp.asarray(masks, jnp.int32)).astype(jnp.int32)


def _other_chips(x, y, c):
    return [(1 - x if fx else x, 1 - y if fy else y, c) for fx, fy in CHIP_FLIPS]


def _dw_pair_sums(operands, order, which, name, barrier_id, ride=None):
    t = operands[-1].shape[0]
    n_far = len(CHIP_FLIPS)
    n_in = len(operands)
    n_ride = 0 if ride is None else 1
    out_chunk = D_MODEL // N_DEV
    if which == "up":
        rows, cols = D_MODEL, FF_CHUNK
        in_specs = [_resident((D_MODEL, t)), pl.BlockSpec((t, FF_CHUNK), lambda s, order_ref: (0, order_ref[s]))]
    elif which == "down":
        rows, cols = FF_CHUNK, D_MODEL
        in_specs = [pl.BlockSpec((t, FF_CHUNK), lambda s, order_ref: (0, order_ref[s])), _resident((t, D_MODEL))]
    else:
        rows, cols = out_chunk, D_MODEL
        half = pl.BlockSpec((t, out_chunk), lambda s, order_ref: (0, order_ref[s] % (N_DEV // 2)))
        in_specs = [half, half, _resident((t, D_MODEL))]

    def body(order_ref, *refs):
        own_ref, from_sib_ref, pair_ref = refs[n_in + n_ride:n_in + n_ride + 3]
        send_buf, land_buf, send_sems, recv_sems = refs[n_in + 2 * n_ride + 3:n_in + 2 * n_ride + 7]
        s_now = pl.program_id(0)
        x, y, c = _mesh_pos()
        sibling = (x, y, 1 - c)
        sems = (send_sems, recv_sems)

        @pl.when(s_now == 0)
        def _():
            _enter_with([sibling] + (_other_chips(x, y, c) if n_ride else []))

        if n_ride:
            _chip_exchange_beside(s_now == 0, s_now == N_DEV - 1, [refs[n_in]], [refs[n_in + 3 + n_ride]],
                                  refs[n_in + 2 * n_ride + 7:], enter=False)

        def hand_over(k):
            dst = land_buf.at[k] if k < n_far else from_sib_ref
            return _push(send_buf.at[k], dst, sems, k, sibling)

        if which == "out":
            ma_ref, mc_ref, b_ref = refs[:n_in]
            block = lax.cond(order_ref[s_now] < N_DEV // 2, lambda: _mm_tn(ma_ref[...], b_ref[...]),
                             lambda: _mm_tn(mc_ref[...], b_ref[...]))
        elif which == "down":
            block = _mm_tn(refs[0][...], refs[1][...])
        else:
            block = _mm(refs[0][...], refs[1][...])
        for k in range(n_far + 1):
            @pl.when(s_now == k)
            def _():
                send_buf[k] = block.astype(BF16)
                hand_over(k).start()

        for k in range(n_far):
            @pl.when(s_now == n_far + 1 + k)
            def _():
                hand_over(k).wait_recv()
                pair_ref[...] = (block + land_buf[k].astype(F32)).astype(BF16)

        @pl.when(s_now == N_DEV - 1)
        def _():
            own_ref[...] = block
            for k in range(n_far + 1):
                hand_over(k).wait_send()
            hand_over(n_far).wait_recv()

    rides = [] if ride is None else [ride]
    sems = lambda k: pltpu.SemaphoreType.DMA((k,))
    return pl.pallas_call(
        body, name=name,
        grid_spec=pltpu.PrefetchScalarGridSpec(
            num_scalar_prefetch=1, grid=(N_DEV,), in_specs=in_specs + [HBM_SPEC] * n_ride,
            out_specs=[pl.BlockSpec((rows, cols), lambda s, order_ref: (0, 0)), HBM_SPEC,
                       pl.BlockSpec((None, rows, cols), lambda s, order_ref: (jnp.clip(s - n_far - 1, 0, n_far - 1), 0, 0))]
            + [HBM_SPEC] * n_ride,
            scratch_shapes=[pltpu.VMEM((n_far + 1, rows, cols), BF16), pltpu.VMEM((n_far, rows, cols), BF16),
                            sems(n_far + 1), sems(n_far + 1)] + [sems(n_far), sems(n_far)] * n_ride),
        out_shape=[jax.ShapeDtypeStruct((rows, cols), F32), jax.ShapeDtypeStruct((rows, cols), BF16),
                   jax.ShapeDtypeStruct((n_far, rows, cols), BF16)]
        + [jax.ShapeDtypeStruct(r.shape, r.dtype) for r in rides],
        compiler_params=_params("arbitrary", barrier_id=barrier_id),
    )(order, *operands, *rides)


def _chip_exchange_beside(first, last, sums, outs, sems, enter=True):
    chips = _other_chips(*_mesh_pos())
    copies = [_push(sums[i].at[k], outs[i].at[k], sems, len(chips) * i + k, chip)
              for i in range(len(sums)) for k, chip in enumerate(chips)]

    @pl.when(first)
    def _():
        if enter:
            _enter_with(chips)
        for cp in copies:
            cp.start()

    @pl.when(last)
    def _():
        for cp in copies:
            cp.wait()


ROW_GCONV, ROW_CW0 = 1, 2


def _conv_bwd(dmixed, gates, g_conv, conv_w, tm):
    t = gates.shape[0]
    n = t // tm
    rev = lambda i: n - 1 - i

    def body(dm_ref, gates_ref, gprev_ref, gc_ref, cw_ref, dgates_ref, small_ref, carry_ref):
        i = pl.program_id(0)

        @pl.when(i == 0)
        def _():
            small_ref[...] = jnp.zeros_like(small_ref)
            carry_ref[...] = jnp.zeros_like(carry_ref)

        gates = gates_ref[...]
        gb, gcc, xin = gates[:, :CONV_W], gates[:, CONV_W:2 * CONV_W], gates[:, 2 * CONV_W:]
        u = gcc * xin
        gp = gprev_ref[...]
        uprev = jnp.where(rev(i) == 0, 0.0, gp[:, CONV_W:2 * CONV_W] * gp[:, 2 * CONV_W:])
        u1, u2 = _shift_rows_down(u, uprev, 1), _shift_rows_down(u, uprev, 2)
        w = cw_ref[...]
        c = _conv3(u, u1, u2, w)
        conv = gb * c
        rcv = _inv_rms(conv)
        c_hat = conv * rcv
        dconv, dgc = _rms_bwd(c_hat, rcv, gc_ref[...], dm_ref[...])
        dc = dconv * gb
        nxt = carry_ref[...]
        du = (w[2:3, :] * dc + w[1:2, :] * _shift_rows_up(dc, nxt, 1)) + w[0:1, :] * _shift_rows_up(dc, nxt, 2)
        carry_ref[...] = dc[0:8, :]
        dgates_ref[:, :CONV_W] = (dconv * c).astype(BF16)
        dgates_ref[:, CONV_W:2 * CONV_W] = (du * xin).astype(BF16)
        dgates_ref[:, 2 * CONV_W:] = (du * gcc).astype(BF16)
        small_ref[ROW_GCONV:ROW_GCONV + 1, :] += _colsum(dgc)
        small_ref[ROW_CW0:ROW_CW0 + 1, :] += _colsum(dc * u2)
        small_ref[ROW_CW0 + 1:ROW_CW0 + 2, :] += _colsum(dc * u1)
        small_ref[ROW_CW0 + 2:ROW_CW0 + 3, :] += _colsum(dc * u)

    tile = lambda w_: pl.BlockSpec((tm, w_), lambda i: (rev(i), 0))
    prev8 = pl.BlockSpec((8, GATES_W), lambda i: (jnp.maximum(rev(i) * (tm // 8) - 1, 0), 0))
    conv_half = pl.BlockSpec((tm, CONV_W), lambda i: (rev(i), ATTN_W // CONV_W))
    return pl.pallas_call(
        body, name="conv_bwd", grid=(n,),
        in_specs=[conv_half, tile(GATES_W), prev8, _full((1, CONV_W)), _full((3, CONV_W))],
        out_specs=[tile(GATES_W), _full((SMALL_ROWS, CONV_W))],
        out_shape=[jax.ShapeDtypeStruct((t, GATES_W), BF16), jax.ShapeDtypeStruct((SMALL_ROWS, CONV_W), F32)],
        scratch_shapes=[pltpu.VMEM((8, CONV_W), F32)],
        compiler_params=_params("arbitrary"),
    )(dmixed, gates, gates, g_conv, conv_w)


def _attn_bwd(qkv, dmixed, attn, g_attn, sinks, rope, sums):
    t = qkv.shape[0]
    n_steps = t // ATTN_STEP
    rev = lambda i: n_steps - 1 - i
    rc, rs1, rs2 = rope

    def body(sink_ref, q_ref, kp_ref, kc_ref, vp_ref, vc_ref, dm_ref, attn_ref, ga_ref, c_ref, s1_ref, s2_ref, sums_ref,
             dqkv_ref, dsink_ref, dgain_ref, arrived_ref, ck_ref, cv_ref, kacc_ref, vacc_ref, send_sems, recv_sems):
        i = pl.program_id(0)
        _chip_exchange_beside(i == 0, i == n_steps - 1, [sums_ref], [arrived_ref], (send_sems, recv_sems))

        @pl.when(i == 0)
        def _():
            dsink_ref[...] = jnp.zeros_like(dsink_ref)
            dgain_ref[...] = jnp.zeros_like(dgain_ref)
            ck_ref[...] = jnp.zeros_like(ck_ref)
            cv_ref[...] = jnp.zeros_like(cv_ref)

        kacc_ref[...] = jnp.zeros_like(kacc_ref)
        vacc_ref[...] = jnp.zeros_like(vacc_ref)
        a = attn_ref[...]
        ra = _inv_rms(a)
        dattn, dgain = _rms_bwd(a * ra, ra, ga_ref[...], dm_ref[...])
        dgain_ref[0:1, :] += _colsum(dgain)
        qt = (q_ref[...] * ATTN_SCALE).T
        dot = dattn.astype(BF16).T
        keys = jnp.concatenate([kp_ref[...], kc_ref[...]], axis=0)
        vals = jnp.concatenate([vp_ref[...], vc_ref[...]], axis=0)
        sink = [_group_sinks(sink_ref, g) for g in range(N_KV)]
        c, s1, s2 = c_ref[...], s1_ref[...], s2_ref[...]
        lane = lax.broadcasted_iota(jnp.int32, (1, 128), 1)
        dsink = jnp.zeros((1, 128), F32)
        masks = _attn_masks(rev(i) > 0)
        dq_parts = []
        for b in range(ATTN_STEP_BLOCKS):
            window = slice(BLOCK * b, BLOCK * (b + 2))
            valid = masks[b]
            dq_parts.append([])
            dk_parts, dv_parts = [], []
            for g in range(N_KV):
                gs = slice(HEAD_DIM * g, HEAD_DIM * (g + 1))
                kk, vv = keys[window, gs], vals[window, gs]
                qtg, dotg = _heads_side_by_side(qt, g, b), _heads_side_by_side(dot, g, b)
                probs, psink = _attn_probs(qtg, kk, sink[g], valid)
                dp = _mm(vv, dotg)
                delta = jnp.sum(probs * dp, axis=0, keepdims=True)
                ds = (probs * (dp - delta)).astype(BF16)
                sink_terms = psink * delta
                for hh in range(GROUP):
                    head_sum = jnp.sum(sink_terms[:, BLOCK * hh:BLOCK * (hh + 1)])
                    dsink = dsink + jnp.where(lane == GROUP * g + hh, -head_sum, 0.0)
                dq_parts[b].append(_mm_tn(kk * ATTN_SCALE, ds))
                dk_parts.append(_mm_nt(ds, qtg))
                dv_parts.append(_mm_nt(probs.astype(BF16), dotg))
            kacc_ref[window, :] += jnp.concatenate(dk_parts, axis=1)
            vacc_ref[window, :] += jnp.concatenate(dv_parts, axis=1)
        dq = _to_token_rows(dq_parts)
        for ci in range(ATTN_W // 128):
            sl = slice(128 * ci, 128 * (ci + 1))
            dqkv_ref[:, sl] = _rope_transpose(dq[:, sl], c, s1, s2).astype(BF16)
        kacc_ref[ATTN_STEP:, :] += ck_ref[...]
        vacc_ref[ATTN_STEP:, :] += cv_ref[...]
        ck_ref[...] = kacc_ref[:BLOCK, :]
        cv_ref[...] = vacc_ref[:BLOCK, :]
        dqkv_ref[:, ATTN_W:ATTN_W + KV_W] = _rope_transpose(kacc_ref[BLOCK:, :], c, s1, s2).astype(BF16)
        dqkv_ref[:, ATTN_W + KV_W:] = vacc_ref[BLOCK:, :].astype(BF16)
        dsink_ref[0:1, :] += dsink

    blk = lambda w_: pl.BlockSpec((ATTN_STEP, w_), lambda i: (rev(i), 0))
    return pl.pallas_call(
        body, name="attn_bwd", grid=(n_steps,),
        in_specs=[pl.BlockSpec(memory_space=pltpu.SMEM)] + _qkv_specs(rev)
        + [blk(ATTN_W), blk(ATTN_W), _full((1, ATTN_W)), blk(128), blk(128), blk(128), HBM_SPEC],
        out_specs=[blk(QKV_W), _full((8, 128)), _full((SMALL_ROWS, ATTN_W)), HBM_SPEC],
        out_shape=[jax.ShapeDtypeStruct((t, QKV_W), BF16), jax.ShapeDtypeStruct((8, 128), F32),
                   jax.ShapeDtypeStruct((SMALL_ROWS, ATTN_W), F32), jax.ShapeDtypeStruct(sums.shape, sums.dtype)],
        scratch_shapes=[pltpu.VMEM((BLOCK, KV_W), F32), pltpu.VMEM((BLOCK, KV_W), F32),
                        pltpu.VMEM((ATTN_KEYS, KV_W), F32), pltpu.VMEM((ATTN_KEYS, KV_W), F32),
                        pltpu.SemaphoreType.DMA((len(CHIP_FLIPS),)), pltpu.SemaphoreType.DMA((len(CHIP_FLIPS),))],
        compiler_params=_params("arbitrary", barrier_id=6),
    )(sinks, qkv, qkv, qkv, qkv, qkv, dmixed, attn, g_attn, rc, rs1, rs2, sums)


def _grad_x_tile(dq, dg, x_hat, r, g1, w_ref, dh):
    dhn = _mm_nt(dq, w_ref[:, :QKV_W]) + _mm_nt(dg, w_ref[:, QKV_W:])
    dx, dg1 = _rms_bwd(x_hat, r, g1, dhn)
    return dh + dx, _colsum(dg1)


def _in_proj_bwd(dqkv, dgates, x, dh, g1, w_in, tm, out_sums):
    t = x.shape[0]
    n = t // tm
    n_cover = (n + 1) // 2
    n_steps = n + n_cover
    n_far = len(CHIP_FLIPS)
    shard = (D_MODEL, IN_SHARD)

    def body(dq_ref, dg_ref, x_ref, dh_ref, g1_ref, w_ref, osums_ref,
             dx_ref, own_ref, sib_ref, far_ref, dg1_ref, oarrived_ref,
             acc_ref, send_buf, land_buf, pair_buf, d2d_send, d2d_recv, ici_send, ici_recv, o_send, o_recv):
        i = pl.program_id(0)
        x_pos, y_pos, c = _mesh_pos()
        my_chip = 2 * x_pos + y_pos
        sibling = (x_pos, y_pos, 1 - c)
        @pl.when(i == 0)
        def _():
            _enter_with(_sibling_and_chips(x_pos, y_pos, c))

        _chip_exchange_beside(i == 0, i == n_steps - 1, [osums_ref], [oarrived_ref], (o_send, o_recv), enter=False)

        def cols(d):
            return slice(IN_SHARD * d, IN_SHARD * (d + 1))

        def hand_over(chip):
            return _push(send_buf.at[chip], land_buf.at[chip], (d2d_send, d2d_recv), chip, sibling)

        def to_chip(chip, rel):
            return pltpu.make_async_remote_copy(
                src_ref=pair_buf.at[chip], dst_ref=far_ref.at[rel - 1], send_sem=ici_send.at[rel - 1],
                recv_sem=ici_recv.at[rel - 1], device_id=(chip // 2, chip % 2, c), device_id_type=MESH)

        @pl.when(i == 0)
        def _():
            acc_ref[...] = jnp.zeros_like(acc_ref)
            dg1_ref[...] = jnp.zeros_like(dg1_ref)

        def normed_x():
            xv = x_ref[...]
            r = _inv_rms(xv)
            return xv * r, r

        @pl.when(i < n)
        def _():
            hn = (normed_x()[0] * g1_ref[...]).astype(BF16)
            acc_ref[:, :QKV_W] += _mm_tn(hn, dq_ref[...])
            acc_ref[:, QKV_W:] += _mm_tn(hn, dg_ref[...])

        @pl.when(i == n - 1)
        def _():
            for d in range(N_DEV):
                @pl.when(d % 2 != c)
                def _():
                    send_buf[d // 2] = acc_ref[:, cols(d)].astype(BF16)
                    hand_over(d // 2).start()
            for d in range(N_DEV):
                chip = d // 2

                @pl.when(d % 2 == c)
                def _():
                    hand_over(chip).wait_recv()

                    @pl.when(chip == my_chip)
                    def _():
                        own_ref[...] = acc_ref[:, cols(d)]
                        sib_ref[...] = land_buf[chip]

                    @pl.when(chip != my_chip)
                    def _():
                        pair_buf[chip] = (acc_ref[:, cols(d)] + land_buf[chip].astype(F32)).astype(BF16)
                        to_chip(chip, chip ^ my_chip).start()
            for chip in range(N_CHIPS):
                hand_over(chip).wait_send()

        @pl.when(i >= n)
        def _():
            x_hat, r = normed_x()
            dx_ref[...], dg1 = _grad_x_tile(dq_ref[...], dg_ref[...], x_hat, r, g1_ref[...], w_ref, dh_ref[...])
            dg1_ref[0:1, :] += dg1

        @pl.when(i == n_steps - 1)
        def _():
            for rel in range(1, n_far + 1):
                to_chip(0, rel).wait()

    both = lambda w_: pl.BlockSpec((tm, w_), lambda i: (i % n, 0))
    second = pl.BlockSpec((tm, D_MODEL), lambda i: (jnp.maximum(i - n, 0), 0))
    whole = lambda dtype: jax.ShapeDtypeStruct(shard, dtype)
    sems = lambda k: pltpu.SemaphoreType.DMA((k,))
    res = pl.pallas_call(
        body, name="in_proj_bwd", grid=(n_steps,),
        in_specs=[both(QKV_W), both(GATES_W), both(D_MODEL), second, _full((1, D_MODEL)), _resident((D_MODEL, IN_COLS)),
                  HBM_SPEC],
        out_specs=[second, _full(shard), _full(shard), HBM_SPEC, _full((SMALL_ROWS, D_MODEL)), HBM_SPEC],
        out_shape=[jax.ShapeDtypeStruct((n_cover * tm, D_MODEL), F32), whole(F32), whole(BF16),
                   jax.ShapeDtypeStruct((n_far,) + shard, BF16), jax.ShapeDtypeStruct((SMALL_ROWS, D_MODEL), F32),
                   jax.ShapeDtypeStruct(out_sums.shape, out_sums.dtype)],
        scratch_shapes=[pltpu.VMEM((D_MODEL, IN_COLS), F32), pltpu.VMEM((N_CHIPS,) + shard, BF16),
                        pltpu.VMEM((N_CHIPS,) + shard, BF16), pltpu.VMEM((N_CHIPS,) + shard, BF16),
                        sems(N_CHIPS), sems(N_CHIPS), sems(n_far), sems(n_far), sems(n_far), sems(n_far)],
        compiler_params=_params("arbitrary", barrier_id=7),
    )(dqkv, dgates, x, dh, g1, w_in, out_sums)
    return res[0], (res[1], res[2], res[3]), res[4], res[5]


def _grad_x_rest(dqkv, dgates, x, dh, g1, w_in, tm, head, dg1_rows):
    t = x.shape[0]
    first = head.shape[0] // tm
    n_rest = t // tm - first
    if n_rest == 0:
        return head, dg1_rows

    def body(dq_ref, dg_ref, x_ref, dh_ref, g1_ref, w_ref, head_ref, rows_ref, gx_ref, dg1_ref, stage, sems):
        j = pl.program_id(0)
        head_copy = pltpu.make_async_copy(head_ref, gx_ref.at[pl.ds(0, first * tm), :], sems.at[2])

        def tile_out(step):
            rows = pl.ds(pl.multiple_of((step + first) * tm, tm), tm)
            return pltpu.make_async_copy(stage.at[step % 2], gx_ref.at[rows, :], sems.at[step % 2])

        @pl.when(j == 0)
        def _():
            head_copy.start()
            dg1_ref[...] = rows_ref[...]

        @pl.when(j >= 2)
        def _():
            tile_out(j - 2).wait()

        xv = x_ref[...]
        r = _inv_rms(xv)
        dx, dg1 = _grad_x_tile(dq_ref[...], dg_ref[...], xv * r, r, g1_ref[...], w_ref, dh_ref[...])
        stage[j % 2] = dx
        dg1_ref[0:1, :] += dg1
        tile_out(j).start()

        @pl.when(j == n_rest - 1)
        def _():
            tile_out(j).wait()
            if n_rest >= 2:
                tile_out(j - 1).wait()
            head_copy.wait()

    tile = lambda w_: pl.BlockSpec((tm, w_), lambda j: (j + first, 0))
    return pl.pallas_call(
        body, name="grad_x_rest", grid=(n_rest,),
        in_specs=[tile(QKV_W), tile(GATES_W), tile(D_MODEL), tile(D_MODEL), _full((1, D_MODEL)),
                  _resident((D_MODEL, IN_COLS)), HBM_SPEC, _full((SMALL_ROWS, D_MODEL))],
        out_specs=[HBM_SPEC, _full((SMALL_ROWS, D_MODEL))],
        out_shape=[jax.ShapeDtypeStruct((t, D_MODEL), F32), jax.ShapeDtypeStruct((SMALL_ROWS, D_MODEL), F32)],
        scratch_shapes=[pltpu.VMEM((2, tm, D_MODEL), F32), pltpu.SemaphoreType.DMA((3,))],
        compiler_params=_params("arbitrary"),
    )(dqkv, dgates, x, dh, g1, w_in, head, dg1_rows)


def _all_gather(shards, name):
    n = len(shards)

    def body(*refs):
        _enter_with(_sibling_and_chips(*_mesh_pos()))
        start, finish = _gather_steps(refs[:n], refs[n:2 * n], *refs[2 * n:])
        start()
        finish()

    return pl.pallas_call(
        body, name=name,
        in_specs=[HBM_SPEC] * n, out_specs=[HBM_SPEC] * n,
        out_shape=[jax.ShapeDtypeStruct((N_DEV,) + s.shape, s.dtype) for s in shards],
        scratch_shapes=[pltpu.SemaphoreType.DMA((7 * n,)), pltpu.SemaphoreType.DMA((7 * n,)),
                        pltpu.SemaphoreType.DMA((n,))],
        compiler_params=_params(barrier_id=8),
    )(*shards)


def _adam_math(w, g, m, v):
    m = ADAM_B1 * m + (1.0 - ADAM_B1) * g
    v = ADAM_B2 * v + (1.0 - ADAM_B2) * (g * g)
    m_hat = m / (1.0 - ADAM_B1 ** ADAM_STEP)
    v_hat = v / (1.0 - ADAM_B2 ** ADAM_STEP)
    delta = -ADAM_LR * (m_hat / (jnp.sqrt(v_hat) + ADAM_EPS) + ADAM_WD * w)
    return delta, m, v


def _adamw_reduced(w, m, v, own, from_sibling, from_chips, tr):
    rows, cols = w.shape

    def body(w_ref, m_ref, v_ref, own_ref, sib_ref, far_ref, g_ref, d_ref, nm_ref, nv_ref):
        g = own_ref[...] + sib_ref[...].astype(F32)
        for k in range(len(CHIP_FLIPS)):
            g = g + far_ref[k].astype(F32)
        g_ref[...] = g
        d_ref[...], nm_ref[...], nv_ref[...] = _adam_math(w_ref[...], g, m_ref[...], v_ref[...])

    tile = pl.BlockSpec((tr, cols), lambda i: (i, 0))
    out = jax.ShapeDtypeStruct((rows, cols), F32)
    return pl.pallas_call(
        body, name="adamw_reduced", grid=(rows // tr,),
        in_specs=[tile] * 5 + [pl.BlockSpec((len(CHIP_FLIPS), tr, cols), lambda i: (0, i, 0))],
        out_specs=[tile] * 4, out_shape=[out] * 4,
        compiler_params=_params("parallel"),
    )(w, m, v, own, from_sibling, from_chips)


def _sum_devices(gathered):
    _, rows, cols = gathered.shape

    def body(g_ref, o_ref):
        s = g_ref[0]
        for d in range(1, N_DEV):
            s = s + g_ref[d]
        o_ref[...] = s

    return pl.pallas_call(
        body, name="sum_devices", in_specs=[_full(gathered.shape)], out_specs=_full((rows, cols)), grid=(1,),
        out_shape=jax.ShapeDtypeStruct((rows, cols), F32),
    )(gathered)


def _adamw_small(w, g, m, v):
    def body(w_ref, g_ref, m_ref, v_ref, d_ref, nm_ref, nv_ref):
        d_ref[...], nm_ref[...], nv_ref[...] = _adam_math(w_ref[...], g_ref[...], m_ref[...], v_ref[...])

    spec = _full(w.shape)
    out = jax.ShapeDtypeStruct(w.shape, F32)
    return pl.pallas_call(
        body, name="adamw_small", grid=(1,), in_specs=[spec] * 4, out_specs=[spec] * 3, out_shape=[out] * 3,
    )(w, g, m, v)


TOKEN_TILE = 512
MID_TILE = 256
MID_CHUNK = 1024
MID_CHUNKS = D_FF // MID_CHUNK
ADAM_ROWS = 128


def _local_grads(x, target, g1, w_in_shard, conv_shard, sinks, g_attn, g_conv, g2, g3, g4, shards, order):
    t = x.shape[0]
    tm = min(TOKEN_TILE, t)
    rope = _rope_tables(t)
    qkv, gates, mconv, w_in, conv_w, gathered = _in_proj_fwd(x, g1, w_in_shard, conv_shard, g_conv, rope, tm, shards,
                                                             (False, True, False))
    attn, mattn, (w_out, w_up, w_down) = _attn_fwd(qkv, sinks, g_attn, shards, gathered)
    act, dup, hn2t, dmo, dmix, dh, dmixed, small_mid = _mid(
        mattn, mconv, x, target, g2, g3, g4, w_out.reshape(D_MODEL, D_MODEL),
        w_up, w_down.reshape(D_FF, D_MODEL), min(MID_TILE, t))
    up_own, up_sib, up_sums = _dw_pair_sums((hn2t, dup), order, "up", "dw_up", 2)
    down_own, down_sib, down_sums, up_far = _dw_pair_sums((act, dmo), order, "down", "dw_down", 3, ride=up_sums)
    out_own, out_sib, out_sums = _dw_pair_sums((mattn, mconv, dmix), order, "out", "dw_out", 4)
    dgates, small_conv = _conv_bwd(dmixed, gates, g_conv, conv_w, tm)
    dqkv, dsink, dg_attn, down_far = _attn_bwd(qkv, dmixed, attn, g_attn, sinks, rope, down_sums)
    grad_x_head, dw_in, small_in, out_far = _in_proj_bwd(dqkv, dgates, x, dh, g1, w_in, tm, out_sums)
    grad_x, small_in = _grad_x_rest(dqkv, dgates, x, dh, g1, w_in, tm, grad_x_head, small_in)
    dw_out, dw_up, dw_down = (out_own, out_sib, out_far), (up_own, up_sib, up_far), (down_own, down_sib, down_far)
    return grad_x, dw_in, dw_out, dw_up, dw_down, (small_mid, small_conv, dg_attn, dsink, small_in)


def _pack_small(small_mid, small_mix, dg_attn, dsink, small_in):
    z = lambda n: jnp.zeros((1, n), F32)
    rows = [
        small_mid[ROW_LOSS:ROW_LOSS + 1],
        small_in[0:1],
        small_mid[ROW_G2:ROW_G2 + 1],
        small_mid[ROW_G3:ROW_G3 + 1],
        small_mid[ROW_G4:ROW_G4 + 1],
        jnp.concatenate([dg_attn[0:1], small_mix[ROW_GCONV:ROW_GCONV + 1]], axis=1),
        jnp.concatenate([small_mix[ROW_CW0:ROW_CW0 + 1], small_mix[ROW_CW0 + 1:ROW_CW0 + 2]], axis=1),
        jnp.concatenate([small_mix[ROW_CW0 + 2:ROW_CW0 + 3], dsink[0:1, :], z(D_MODEL - CONV_W - 128)], axis=1),
    ]
    return jnp.concatenate(rows, axis=0)


def kernel(x, pre_mix_norm, w_in, conv_w, attn_sinks, attn_group_norm, conv_group_norm, w_out, post_mix_norm, pre_mlp_norm, w_up, w_down, post_mlp_norm, loss_target, m_pre_mix_norm, m_w_in, m_conv_w, m_attn_sinks, m_attn_group_norm, m_conv_group_norm, m_w_out, m_post_mix_norm, m_pre_mlp_norm, m_w_up, m_w_down, m_post_mlp_norm, v_pre_mix_norm, v_w_in, v_conv_w, v_attn_sinks, v_attn_group_norm, v_conv_group_norm, v_w_out, v_post_mix_norm, v_pre_mlp_norm, v_w_up, v_w_down, v_post_mlp_norm):
    xi, yi, ci = _mesh_pos()
    chip = 2 * xi + yi
    dev = 2 * chip + ci

    order = _block_order(dev)

    shards = [w_out[0].astype(BF16), w_up[0].astype(BF16), w_down[0].astype(BF16)]

    grad_x, dw_in, dw_out, dw_up, dw_down, smalls = _local_grads(
        x[0], loss_target[0], pre_mix_norm, w_in[0].astype(BF16), conv_w[0], attn_sinks, attn_group_norm, conv_group_norm,
        post_mix_norm, pre_mlp_norm, post_mlp_norm, shards, order)

    small = _sum_devices(_all_gather([_pack_small(*smalls)], "gather_small")[0])
    loss = (0.5 / D_MODEL) * jnp.sum(small[0])

    big = {}
    for name, w, m, v, (own, sib, far) in zip(
            ("w_in", "w_out", "w_up", "w_down"), (w_in, w_out, w_up, w_down), (m_w_in, m_w_out, m_w_up, m_w_down),
            (v_w_in, v_w_out, v_w_up, v_w_down), (dw_in, dw_out, dw_up, dw_down)):
        big[name] = [a[None] for a in _adamw_reduced(w[0], m[0], v[0], own, sib, far, ADAM_ROWS)]

    conv_g = lax.dynamic_slice(
        jnp.stack([small[6, :CONV_W], small[6, CONV_W:], small[7, :CONV_W]]), (0, dev * (CONV_W // N_DEV)),
        (3, CONV_W // N_DEV))
    pad = lambda a, n: jnp.pad(a.reshape(1, -1), ((0, 0), (0, n - a.size)))
    small_names = ("pre_mix_norm", "post_mix_norm", "pre_mlp_norm", "post_mlp_norm")
    small_w = {"pre_mix_norm": (pre_mix_norm, m_pre_mix_norm, v_pre_mix_norm),
               "post_mix_norm": (post_mix_norm, m_post_mix_norm, v_post_mix_norm),
               "pre_mlp_norm": (pre_mlp_norm, m_pre_mlp_norm, v_pre_mlp_norm),
               "post_mlp_norm": (post_mlp_norm, m_post_mlp_norm, v_post_mlp_norm)}

    def pack(k):
        rows = [small_w[nm][k] for nm in small_names]
        rows.append(jnp.concatenate([(attn_group_norm, m_attn_group_norm, v_attn_group_norm)[k],
                                     (conv_group_norm, m_conv_group_norm, v_conv_group_norm)[k]], axis=1))
        rows.append(pad((conv_w, m_conv_w, v_conv_w)[k], D_MODEL))
        rows.append(pad((attn_sinks, m_attn_sinks, v_attn_sinks)[k], D_MODEL))
        rows.append(jnp.zeros((1, D_MODEL), F32))
        return jnp.concatenate(rows, axis=0)

    g_small = jnp.concatenate(
        [small[1:6], pad(conv_g, D_MODEL), pad(small[7, CONV_W:CONV_W + N_HEADS], D_MODEL), jnp.zeros((1, D_MODEL), F32)],
        axis=0)
    d_small, nm_small, nv_small = _adamw_small(pack(0), g_small, pack(1), pack(2))

    def unpack(a):
        nconv = 3 * CONV_W // N_DEV
        return {"pre_mix_norm": a[0:1], "post_mix_norm": a[1:2], "pre_mlp_norm": a[2:3], "post_mlp_norm": a[3:4],
                "attn_group_norm": a[4:5, :ATTN_W], "conv_group_norm": a[4:5, ATTN_W:],
                "conv_w": a[5, :nconv].reshape(1, 3, CONV_W // N_DEV), "attn_sinks": a[6:7, :N_HEADS]}

    order = ("pre_mix_norm", "w_in", "conv_w", "attn_sinks", "attn_group_norm", "conv_group_norm", "w_out",
             "post_mix_norm", "pre_mlp_norm", "w_up", "w_down", "post_mlp_norm")
    outs = []
    for k, a in enumerate((g_small, d_small, nm_small, nv_small)):
        sm = unpack(a)
        outs += [big[nm][k] if nm in big else sm[nm] for nm in order]
    return (loss, grad_x[None], *outs)
```

```python
import functools

import jax
import jax.numpy as jnp
import numpy as np
from jax import lax
from jax.experimental import pallas as pl
from jax.experimental.pallas import tpu as pltpu

F32 = jnp.float32
BF16 = jnp.bfloat16

D_MODEL = 1024
HEAD_DIM = 64
ATTN_W = 512
CONV_W = 512
N_HEADS = 8
N_KV = 2
GROUP = 4
KV_W = 128
QKV_W = ATTN_W + 2 * KV_W
GATES_W = 3 * CONV_W
IN_COLS = QKV_W + GATES_W
D_FF = 4096
FF_CHUNK = 512
N_FF_CHUNKS = D_FF // FF_CHUNK
BLOCK = 128
ROT_HALF = 8
ROPE_THETA = 500000.0
NORM_EPS = 1e-6
NEG_INF = -1e30
ATTN_SCALE = 0.125
N_DEV = 8
N_CHIPS = 4
IN_SHARD = IN_COLS // N_DEV

ADAM_LR = 0.001
ADAM_B1 = 0.9
ADAM_B2 = 0.999
ADAM_EPS = 1e-08
ADAM_WD = 0.01
ADAM_STEP = 10

V7X_VMEM_BYTES = 64 * 1024 * 1024
VMEM_LIMIT = V7X_VMEM_BYTES - 2 * 1024 * 1024

MESH = pl.DeviceIdType.MESH
HBM_SPEC = pl.BlockSpec(memory_space=pltpu.HBM)


def _params(*sem, barrier_id=None):
    return pltpu.CompilerParams(dimension_semantics=sem or None, vmem_limit_bytes=VMEM_LIMIT, collective_id=barrier_id)


def _mm(a, b):
    return jnp.dot(a, b, preferred_element_type=F32)


def _mm_nt(a, b):
    return lax.dot_general(a, b, (((1,), (1,)), ((), ())), preferred_element_type=F32)


def _mm_tn(a, b):
    return lax.dot_general(a, b, (((0,), (0,)), ((), ())), preferred_element_type=F32)


def _inv_rms(x):
    return lax.rsqrt(jnp.mean(x * x, axis=-1, keepdims=True) + NORM_EPS)


def _rms_bwd(xhat, r, gain, dy):
    gy = dy * gain
    return r * (gy - xhat * jnp.mean(gy * xhat, axis=-1, keepdims=True)), dy * xhat


def _colsum(a):
    return jnp.sum(a, axis=0, keepdims=True)


def _full(shape):
    zeros = (0,) * len(shape)
    return pl.BlockSpec(shape, lambda *_: zeros)


def _resident(shape):
    zeros = (0,) * len(shape)
    return pl.BlockSpec(shape, lambda *_: zeros, pipeline_mode=pl.Buffered(1))


def _rope_tables(t):
    pos = np.arange(t, dtype=np.float32)
    inv_freq = (ROPE_THETA ** (-np.arange(0, 2 * ROT_HALF, 2, dtype=np.float64) / (2 * ROT_HALF))).astype(np.float32)
    ang = (pos[:, None] * inv_freq[None, :]).astype(np.float64)
    cos, sin = np.cos(ang).astype(np.float32), np.sin(ang).astype(np.float32)
    zeros8 = np.zeros((t, ROT_HALF), np.float32)
    rest = np.zeros((t, HEAD_DIM - 2 * ROT_HALF), np.float32)
    c_head = np.concatenate([cos, cos, rest + 1.0], axis=1)
    s1_head = np.concatenate([zeros8, sin, rest], axis=1)
    s2_head = np.concatenate([-sin, zeros8, rest], axis=1)
    two = lambda a: jnp.asarray(np.concatenate([a, a], axis=1))
    return two(c_head), two(s1_head), two(s2_head)


def _rope(v, c, s1, s2):
    return v * c + pltpu.roll(v, ROT_HALF, 1) * s1 + pltpu.roll(v, 128 - ROT_HALF, 1) * s2


def _rope_transpose(dv, c, s1, s2):
    return dv * c + pltpu.roll(dv * s1, 128 - ROT_HALF, 1) + pltpu.roll(dv * s2, ROT_HALF, 1)


def _shift_rows_down(u, prev, k):
    row = lax.broadcasted_iota(jnp.int32, u.shape, 0)
    out = pltpu.roll(u, k, 0)
    for r in range(k):
        out = jnp.where(row == r, prev[8 - k + r:8 - k + r + 1, :], out)
    return out


def _shift_rows_up(u, nxt, k):
    n = u.shape[0]
    row = lax.broadcasted_iota(jnp.int32, u.shape, 0)
    out = pltpu.roll(u, n - k, 0)
    for r in range(k):
        out = jnp.where(row == n - k + r, nxt[r:r + 1, :], out)
    return out


def _conv3(u, u1, u2, w):
    return (w[0:1, :] * u2 + w[1:2, :] * u1) + w[2:3, :] * u


def _mesh_pos():
    return lax.axis_index("x"), lax.axis_index("y"), lax.axis_index("c")


def _slot(ref, pos):
    dev = 4 * pos[0] + 2 * pos[1] + pos[2]
    if len(ref.shape) == 2:
        width = ref.shape[1] // N_DEV
        return ref.at[:, pl.ds(pl.multiple_of(dev * width, width), width)]
    return ref.at[dev]


def _gathered_shape(shard, by_cols):
    if by_cols:
        return jax.ShapeDtypeStruct((shard.shape[0], N_DEV * shard.shape[1]), shard.dtype)
    return jax.ShapeDtypeStruct((N_DEV,) + shard.shape, shard.dtype)


def _enter_with(peers):
    barrier = pltpu.get_barrier_semaphore()
    for peer in peers:
        pl.semaphore_signal(barrier, inc=1, device_id=peer, device_id_type=MESH)
    pl.semaphore_wait(barrier, len(peers))


def _sibling_and_chips(x, y, c):
    return [(x, y, 1 - c), (1 - x, y, c), (x, 1 - y, c), (1 - x, 1 - y, c)]


def _push(src, dst, sems, k, to):
    send_sems, recv_sems = sems
    return pltpu.make_async_remote_copy(src_ref=src, dst_ref=dst, send_sem=send_sems.at[k], recv_sem=recv_sems.at[k],
                                        device_id=to, device_id_type=MESH)


def _gather_steps(shards, outs, send_sems, recv_sems, local_sems):
    n = len(shards)
    x, y, c = _mesh_pos()
    me, sibling = (x, y, c), (x, y, 1 - c)
    chips = [(1 - x, y), (x, 1 - y), (1 - x, 1 - y)]

    def copy(i, k, block, to, src=None):
        dst = _slot(outs[i], block)
        return _push(dst if src is None else src, dst, (send_sems, recv_sems), 7 * i + k, to)

    mine = [pltpu.make_async_copy(shards[i], _slot(outs[i], me), local_sems.at[i]) for i in range(n)]
    first = []
    for i in range(n):
        first.append(copy(i, 0, me, sibling, src=shards[i]))
        first += [copy(i, 1 + j, me, (*chip, c), src=shards[i]) for j, chip in enumerate(chips)]

    def start():
        for cp in mine + first:
            cp.start()

    def finish():
        passed = []
        for j, chip in enumerate(chips):
            for i in range(n):
                copy(i, 1 + j, (*chip, c), me).wait_recv()
                cp = copy(i, 4 + j, (*chip, c), sibling)
                cp.start()
                passed.append(cp)
        for i in range(n):
            copy(i, 0, sibling, me).wait_recv()
            for j, chip in enumerate(chips):
                copy(i, 4 + j, (*chip, 1 - c), me).wait_recv()
        for cp in first + passed:
            cp.wait_send()
        for cp in mine:
            cp.wait()

    return start, finish


def _gather_near(first, last, shards, outs, sems, local_sems):
    x, y, c = _mesh_pos()
    me, peers = (x, y, c), [(x, y, 1 - c), (1 - x, y, c), (x, 1 - y, c)]
    n = len(shards)
    local = [pltpu.make_async_copy(shards[i], _slot(outs[i], me), local_sems.at[i]) for i in range(n)]
    sends = [_push(shards[i], _slot(outs[i], me), sems, 3 * i + k, peers[k]) for i in range(n) for k in range(3)]
    arrivals = [_push(shards[i], _slot(outs[i], peers[k]), sems, 3 * i + k, peers[k]) for i in range(n) for k in range(3)]

    def start():
        for cp in local + sends:
            cp.start()

    if first is not None:
        pl.when(first)(start)

    @pl.when(last)
    def _():
        for cp in sends:
            cp.wait_send()
        for cp in arrivals:
            cp.wait_recv()
        for cp in local:
            cp.wait()

    return start


def _gather_far(first, last, shards, ins, outs, sems):
    x, y, c = _mesh_pos()
    me, sibling = (x, y, c), (x, y, 1 - c)
    chips = [(1 - x, y), (x, 1 - y), (1 - x, 1 - y)]
    n = len(shards)
    diag_send = [_push(shards[i], _slot(outs[i], me), sems, 4 * i, (*chips[2], c)) for i in range(n)]
    diag_arrival = [_push(shards[i], _slot(outs[i], (*chips[2], c)), sems, 4 * i, (*chips[2], c)) for i in range(n)]
    passed = [[_push(_slot(ins[i], (*chips[j], c)), _slot(outs[i], (*chips[j], c)), sems, 4 * i + 1 + j, sibling)
               for i in range(n)] for j in range(3)]
    from_sibling = [_push(shards[i], _slot(outs[i], (*chips[j], 1 - c)), sems, 4 * i + 1 + j, sibling)
                    for i in range(n) for j in range(3)]

    @pl.when(first)
    def _():
        for cp in diag_send + passed[0] + passed[1]:
            cp.start()

    @pl.when(last)
    def _():
        for cp in diag_arrival:
            cp.wait_recv()
        for cp in passed[2]:
            cp.start()
        for cp in from_sibling:
            cp.wait_recv()
        for cp in diag_send + passed[0] + passed[1] + passed[2]:
            cp.wait_send()


def _in_proj_fwd(x, g1, w_in, conv_w, g_conv, rope, tm, shards, by_cols):
    t = x.shape[0]
    rc, rs1, rs2 = rope
    n = len(shards)
    n_tiles = t // tm

    def body(*refs):
        x_ref, g1_ref, w_ref, cw_ref, gc_ref, c_ref, s1_ref, s2_ref = refs[:8]
        shard_refs = refs[8:8 + n]
        qkv_ref, gates_ref, mconv_ref, w_full_ref, cw_full_ref = refs[8 + n:13 + n]
        gathered = refs[13 + n:13 + 2 * n]
        carry_ref, w_land, cw_land, hn_ref = refs[13 + 2 * n:17 + 2 * n]
        now_sems = refs[17 + 2 * n:20 + 2 * n]
        step = pl.program_id(0)
        start_later_weights = _gather_near(None, step == 2 * n_tiles - 1, shard_refs, gathered,
                                           refs[20 + 2 * n:22 + 2 * n], refs[22 + 2 * n]) if n else None
        start_w_in, finish_w_in = _gather_steps([w_ref, cw_ref], [w_land, cw_land], *now_sems)

        @pl.when(step == 0)
        def _():
            carry_ref[...] = jnp.zeros_like(carry_ref)
            _enter_with(_sibling_and_chips(*_mesh_pos()))
            start_w_in()
            if start_later_weights is not None:
                start_later_weights()

        @pl.when(step < n_tiles)
        def _():
            xv = x_ref[...]
            hn_ref[step] = ((xv * _inv_rms(xv)) * g1_ref[...]).astype(BF16)

        @pl.when(step == n_tiles)
        def _():
            finish_w_in()
            conv_shard = CONV_W // N_DEV
            for d in range(N_DEV):
                w_full_ref[:, IN_SHARD * d:IN_SHARD * (d + 1)] = w_land[d]
                cw_full_ref[:, conv_shard * d:conv_shard * (d + 1)] = cw_land[d]

        @pl.when(step >= n_tiles)
        def _():
            proj = _mm(hn_ref[step - n_tiles], w_full_ref[...])
            c, s1, s2 = c_ref[...], s1_ref[...], s2_ref[...]
            for ci in range((ATTN_W + KV_W) // 128):
                sl = slice(128 * ci, 128 * (ci + 1))
                qkv_ref[:, sl] = _rope(proj[:, sl], c, s1, s2).astype(BF16)
            qkv_ref[:, ATTN_W + KV_W:QKV_W] = proj[:, ATTN_W + KV_W:QKV_W].astype(BF16)
            gates = proj[:, QKV_W:]
            gates_ref[...] = gates
            gb, gcc, xin = gates[:, :CONV_W], gates[:, CONV_W:2 * CONV_W], gates[:, 2 * CONV_W:]
            u = gcc * xin
            prev = carry_ref[...]
            conv = gb * _conv3(u, _shift_rows_down(u, prev, 1), _shift_rows_down(u, prev, 2), cw_full_ref[...])
            carry_ref[...] = u[tm - 8:tm, :]
            mconv_ref[...] = ((conv * _inv_rms(conv)) * gc_ref[...]).astype(BF16)

    first_pass = pl.BlockSpec((tm, D_MODEL), lambda i: (jnp.minimum(i, n_tiles - 1), 0))
    tile = lambda w_: pl.BlockSpec((tm, w_), lambda i: (jnp.maximum(i - n_tiles, 0), 0))
    sems = lambda k: pltpu.SemaphoreType.DMA((k,))
    res = pl.pallas_call(
        body, name="in_proj_fwd", grid=(2 * n_tiles,),
        in_specs=[first_pass, _full((1, D_MODEL)), HBM_SPEC, HBM_SPEC, _full((1, CONV_W)), tile(128), tile(128),
                  tile(128)] + [HBM_SPEC] * n,
        out_specs=[tile(QKV_W), tile(GATES_W), tile(CONV_W), _full((D_MODEL, IN_COLS)), _full((3, CONV_W))]
        + [HBM_SPEC] * n,
        out_shape=[jax.ShapeDtypeStruct((t, QKV_W), BF16), jax.ShapeDtypeStruct((t, GATES_W), F32),
                   jax.ShapeDtypeStruct((t, CONV_W), BF16), jax.ShapeDtypeStruct((D_MODEL, IN_COLS), BF16),
                   jax.ShapeDtypeStruct((3, CONV_W), F32)]
        + [_gathered_shape(s, cols) for s, cols in zip(shards, by_cols)],
        scratch_shapes=[pltpu.VMEM((8, CONV_W), F32), pltpu.VMEM((N_DEV,) + w_in.shape, BF16),
                        pltpu.VMEM((N_DEV,) + conv_w.shape, F32), pltpu.VMEM((n_tiles, tm, D_MODEL), BF16),
                        sems(14), sems(14), sems(2)]
        + ([sems(3 * n), sems(3 * n), sems(n)] if n else []),
        compiler_params=_params("arbitrary", barrier_id=0),
    )(x, g1, w_in, conv_w, g_conv, rc, rs1, rs2, *shards)
    return res[0], res[1], res[2], res[3], res[4], list(res[5:])


GROUP_COLS = GROUP * BLOCK
ATTN_STEP_BLOCKS = 4


def _attn_masks(has_prev):
    key = lax.broadcasted_iota(jnp.int32, (2 * BLOCK, GROUP_COLS), 0)
    query = lax.broadcasted_iota(jnp.int32, (2 * BLOCK, GROUP_COLS), 1) & (BLOCK - 1)
    band = (key > query) & (key <= query + BLOCK)
    return [band & ((key >= BLOCK) | has_prev)] + [band] * (ATTN_STEP_BLOCKS - 1)


def _heads_side_by_side(at, g, b):
    heads = [at[HEAD_DIM * (GROUP * g + hh):HEAD_DIM * (GROUP * g + hh + 1), BLOCK * b:BLOCK * (b + 1)] for hh in range(GROUP)]
    return jnp.concatenate(heads, axis=1)


def _to_token_rows(parts):
    rows = [jnp.concatenate([parts[b][g][:, BLOCK * hh:BLOCK * (hh + 1)] for b in range(ATTN_STEP_BLOCKS)], axis=1)
            for g in range(N_KV) for hh in range(GROUP)]
    return jnp.concatenate(rows, axis=0).T


def _group_sinks(sink_ref, g):
    head = lax.broadcasted_iota(jnp.int32, (1, GROUP_COLS), 1) // BLOCK
    out = jnp.full((1, GROUP_COLS), sink_ref[0, GROUP * g], F32)
    for hh in range(1, GROUP):
        out = jnp.where(head == hh, sink_ref[0, GROUP * g + hh], out)
    return out


def _attn_probs(qt, kk, sink, valid):
    s = jnp.where(valid, _mm(kk, qt), NEG_INF)
    m = jnp.maximum(jnp.max(s, axis=0, keepdims=True), sink)
    p = jnp.exp(s - m)
    psink = jnp.exp(sink - m)
    inv_l = 1.0 / (jnp.sum(p, axis=0, keepdims=True) + psink)
    return p * inv_l, psink * inv_l


ATTN_STEP = ATTN_STEP_BLOCKS * BLOCK
ATTN_KEYS = ATTN_STEP + BLOCK


def _qkv_specs(order):
    prev = lambda i: jnp.maximum(ATTN_STEP_BLOCKS * order(i) - 1, 0)
    kcol, vcol = ATTN_W // KV_W, ATTN_W // KV_W + 1
    return [pl.BlockSpec((ATTN_STEP, ATTN_W), lambda i: (order(i), 0)),
            pl.BlockSpec((BLOCK, KV_W), lambda i: (prev(i), kcol)), pl.BlockSpec((ATTN_STEP, KV_W), lambda i: (order(i), kcol)),
            pl.BlockSpec((BLOCK, KV_W), lambda i: (prev(i), vcol)), pl.BlockSpec((ATTN_STEP, KV_W), lambda i: (order(i), vcol))]


def _attn_fwd(qkv, sinks, g_attn, shards, gathered):
    t = qkv.shape[0]
    n = len(shards)

    def body(*refs):
        sink_ref, q_ref, kp_ref, kc_ref, vp_ref, vc_ref, ga_ref = refs[:7]
        attn_ref, mattn_ref = refs[7 + 2 * n:9 + 2 * n]
        step = pl.program_id(0)
        if n:
            @pl.when(step == 0)
            def _():
                x, y, c = _mesh_pos()
                _enter_with([(x, y, 1 - c), (1 - x, 1 - y, c)])

            _gather_far(step == 0, step == pl.num_programs(0) - 1, refs[7:7 + n], refs[7 + n:7 + 2 * n],
                        refs[9 + 2 * n:9 + 3 * n], refs[9 + 3 * n:11 + 3 * n])
        qt = (q_ref[...] * ATTN_SCALE).T
        keys = jnp.concatenate([kp_ref[...], kc_ref[...]], axis=0)
        vals = jnp.concatenate([vp_ref[...], vc_ref[...]], axis=0)
        sink = [_group_sinks(sink_ref, g) for g in range(N_KV)]
        masks = _attn_masks(step > 0)
        parts = []
        for b in range(ATTN_STEP_BLOCKS):
            window = slice(BLOCK * b, BLOCK * (b + 2))
            valid = masks[b]
            parts.append([])
            for g in range(N_KV):
                gs = slice(HEAD_DIM * g, HEAD_DIM * (g + 1))
                probs, _ = _attn_probs(_heads_side_by_side(qt, g, b), keys[window, gs], sink[g], valid)
                parts[b].append(_mm_tn(vals[window, gs], probs.astype(BF16)))
        attn = _to_token_rows(parts)
        attn_ref[...] = attn
        mattn_ref[...] = ((attn * _inv_rms(attn)) * ga_ref[...]).astype(BF16)

    blk = pl.BlockSpec((ATTN_STEP, ATTN_W), lambda j: (j, 0))
    res = pl.pallas_call(
        body, name="attn_fwd", grid=(t // ATTN_STEP,),
        in_specs=[pl.BlockSpec(memory_space=pltpu.SMEM)] + _qkv_specs(lambda j: j) + [_full((1, ATTN_W))]
        + [HBM_SPEC] * (2 * n),
        out_specs=[blk, blk] + [HBM_SPEC] * n,
        out_shape=[jax.ShapeDtypeStruct((t, ATTN_W), F32), jax.ShapeDtypeStruct((t, ATTN_W), BF16)]
        + [jax.ShapeDtypeStruct(g.shape, g.dtype) for g in gathered],
        input_output_aliases={7 + n + i: 2 + i for i in range(n)},
        scratch_shapes=[pltpu.SemaphoreType.DMA((4 * n,)), pltpu.SemaphoreType.DMA((4 * n,))] if n else [],
        compiler_params=_params("arbitrary", barrier_id=1 if n else None),
    )(sinks, qkv, qkv, qkv, qkv, qkv, g_attn, *shards, *gathered)
    return res[0], res[1], list(res[2:])


SMALL_ROWS = 8
ROW_LOSS, ROW_G2, ROW_G3, ROW_G4 = 0, 1, 2, 3


def _mid(mattn, mconv, x, target, g2, g3, g4, w_out, w_up, w_down, tm):
    t = x.shape[0]

    def body(ma_ref, mc_ref, x_ref, t_ref, g2_ref, g3_ref, g4_ref, wo_ref, wu_ref, wd_ref,
             act_ref, dup_ref, hn2t_ref, dmo_ref, dmix_ref, dh_ref, dmixed_ref, small_ref, up_ref):
        @pl.when(pl.program_id(0) == 0)
        def _():
            small_ref[...] = jnp.zeros_like(small_ref)

        g2, g3, g4 = g2_ref[...], g3_ref[...], g4_ref[...]
        mix_out = _mm(ma_ref[...], wo_ref[0:ATTN_W, :]) + _mm(mc_ref[...], wo_ref[ATTN_W:, :])
        r2 = _inv_rms(mix_out)
        mo_hat = mix_out * r2
        h = x_ref[...] + mo_hat * g2
        r3 = _inv_rms(h)
        h_hat = h * r3
        hn2 = (h_hat * g3).astype(BF16)
        hn2t_ref[...] = hn2.T
        for j in range(MID_CHUNKS):
            cols_j = slice(MID_CHUNK * j, MID_CHUNK * (j + 1))
            up = jnp.maximum(_mm(hn2, wu_ref[:, cols_j]), 0.0)
            up_ref[:, cols_j] = up.astype(BF16)
            act_ref[:, cols_j] = (up * up).astype(BF16)
        mlp = _mm(act_ref[...], wd_ref[...])
        r4 = _inv_rms(mlp)
        ml_hat = mlp * r4
        err = (h + ml_hat * g4) - t_ref[...]
        d_out = err * (1.0 / D_MODEL)
        d_mlp, dg4 = _rms_bwd(ml_hat, r4, g4, d_out)
        dmo = d_mlp.astype(BF16)
        dmo_ref[...] = dmo
        for j in range(MID_CHUNKS):
            cols_j = slice(MID_CHUNK * j, MID_CHUNK * (j + 1))
            dact = _mm_nt(dmo, wd_ref[cols_j, :])
            dup_ref[:, cols_j] = (dact * (2.0 * up_ref[:, cols_j].astype(F32))).astype(BF16)
        dhn2 = _mm_nt(dup_ref[...], wu_ref[...])
        dh_norm, dg3 = _rms_bwd(h_hat, r3, g3, dhn2)
        dh = d_out + dh_norm
        dh_ref[...] = dh
        d_mix, dg2 = _rms_bwd(mo_hat, r2, g2, dh)
        dmix = d_mix.astype(BF16)
        dmix_ref[...] = dmix
        dmixed_ref[...] = _mm_nt(dmix, wo_ref[...])
        small_ref[ROW_LOSS:ROW_LOSS + 1, :] += _colsum(err * err)
        small_ref[ROW_G2:ROW_G2 + 1, :] += _colsum(dg2)
        small_ref[ROW_G3:ROW_G3 + 1, :] += _colsum(dg3)
        small_ref[ROW_G4:ROW_G4 + 1, :] += _colsum(dg4)

    tile = lambda n: pl.BlockSpec((tm, n), lambda i: (i, 0))
    cols = lambda n: pl.BlockSpec((n, tm), lambda i: (0, i))
    gain = _full((1, D_MODEL))
    return pl.pallas_call(
        body, name="mid_fwd_bwd", grid=(t // tm,),
        in_specs=[tile(ATTN_W), tile(CONV_W), tile(D_MODEL), tile(D_MODEL), gain, gain, gain,
                  _resident((D_MODEL, D_MODEL)), _resident((D_MODEL, D_FF)), _resident((D_FF, D_MODEL))],
        out_specs=[tile(D_FF), tile(D_FF), cols(D_MODEL), tile(D_MODEL), tile(D_MODEL), tile(D_MODEL), tile(D_MODEL),
                   _full((SMALL_ROWS, D_MODEL))],
        out_shape=[jax.ShapeDtypeStruct((t, D_FF), BF16), jax.ShapeDtypeStruct((t, D_FF), BF16),
                   jax.ShapeDtypeStruct((D_MODEL, t), BF16), jax.ShapeDtypeStruct((t, D_MODEL), BF16),
                   jax.ShapeDtypeStruct((t, D_MODEL), BF16), jax.ShapeDtypeStruct((t, D_MODEL), F32),
                   jax.ShapeDtypeStruct((t, D_MODEL), F32), jax.ShapeDtypeStruct((SMALL_ROWS, D_MODEL), F32)],
        scratch_shapes=[pltpu.VMEM((tm, D_FF), BF16)],
        compiler_params=_params("arbitrary"),
    )(mattn, mconv, x, target, g2, g3, g4, w_out, w_up, w_down)


CHIP_FLIPS = ((1, 1), (1, 0), (0, 1))


def _block_order(dev):
    chip_masks = [4 * fx + 2 * fy for fx, fy in CHIP_FLIPS]
    masks = [m + 1 for m in chip_masks] + [1] + chip_masks + [0]
    return jnp.bitwise_xor(dev, jnp.asarray(masks, jnp.int32)).astype(jnp.int32)


def _other_chips(x, y, c):
    return [(1 - x if fx else x, 1 - y if fy else y, c) for fx, fy in CHIP_FLIPS]


def _dw_pair_sums(operands, order, which, name, barrier_id, ride=None):
    t = operands[-1].shape[0]
    n_far = len(CHIP_FLIPS)
    n_in = len(operands)
    n_ride = 0 if ride is None else 1
    out_chunk = D_MODEL // N_DEV
    if which == "up":
        rows, cols = D_MODEL, FF_CHUNK
        in_specs = [_resident((D_MODEL, t)), pl.BlockSpec((t, FF_CHUNK), lambda s, order_ref: (0, order_ref[s]))]
    elif which == "down":
        rows, cols = FF_CHUNK, D_MODEL
        in_specs = [pl.BlockSpec((t, FF_CHUNK), lambda s, order_ref: (0, order_ref[s])), _resident((t, D_MODEL))]
    else:
        rows, cols = out_chunk, D_MODEL
        half = pl.BlockSpec((t, out_chunk), lambda s, order_ref: (0, order_ref[s] % (N_DEV // 2)))
        in_specs = [half, half, _resident((t, D_MODEL))]

    def body(order_ref, *refs):
        own_ref, from_sib_ref, pair_ref = refs[n_in + n_ride:n_in + n_ride + 3]
        send_buf, land_buf, send_sems, recv_sems = refs[n_in + 2 * n_ride + 3:n_in + 2 * n_ride + 7]
        s_now = pl.program_id(0)
        x, y, c = _mesh_pos()
        sibling = (x, y, 1 - c)
        sems = (send_sems, recv_sems)

        @pl.when(s_now == 0)
        def _():
            _enter_with([sibling] + (_other_chips(x, y, c) if n_ride else []))

        if n_ride:
            _chip_exchange_beside(s_now == 0, s_now == N_DEV - 1, [refs[n_in]], [refs[n_in + 3 + n_ride]],
                                  refs[n_in + 2 * n_ride + 7:], enter=False)

        def hand_over(k):
            dst = land_buf.at[k] if k < n_far else from_sib_ref
            return _push(send_buf.at[k], dst, sems, k, sibling)

        if which == "out":
            ma_ref, mc_ref, b_ref = refs[:n_in]
            block = lax.cond(order_ref[s_now] < N_DEV // 2, lambda: _mm_tn(ma_ref[...], b_ref[...]),
                             lambda: _mm_tn(mc_ref[...], b_ref[...]))
        elif which == "down":
            block = _mm_tn(refs[0][...], refs[1][...])
        else:
            block = _mm(refs[0][...], refs[1][...])
        for k in range(n_far + 1):
            @pl.when(s_now == k)
            def _():
                send_buf[k] = block.astype(BF16)
                hand_over(k).start()

        for k in range(n_far):
            @pl.when(s_now == n_far + 1 + k)
            def _():
                hand_over(k).wait_recv()
                pair_ref[...] = (block + land_buf[k].astype(F32)).astype(BF16)

        @pl.when(s_now == N_DEV - 1)
        def _():
            own_ref[...] = block
            for k in range(n_far + 1):
                hand_over(k).wait_send()
            hand_over(n_far).wait_recv()

    rides = [] if ride is None else [ride]
    sems = lambda k: pltpu.SemaphoreType.DMA((k,))
    return pl.pallas_call(
        body, name=name,
        grid_spec=pltpu.PrefetchScalarGridSpec(
            num_scalar_prefetch=1, grid=(N_DEV,), in_specs=in_specs + [HBM_SPEC] * n_ride,
            out_specs=[pl.BlockSpec((rows, cols), lambda s, order_ref: (0, 0)), HBM_SPEC,
                       pl.BlockSpec((None, rows, cols), lambda s, order_ref: (jnp.clip(s - n_far - 1, 0, n_far - 1), 0, 0))]
            + [HBM_SPEC] * n_ride,
            scratch_shapes=[pltpu.VMEM((n_far + 1, rows, cols), BF16), pltpu.VMEM((n_far, rows, cols), BF16),
                            sems(n_far + 1), sems(n_far + 1)] + [sems(n_far), sems(n_far)] * n_ride),
        out_shape=[jax.ShapeDtypeStruct((rows, cols), F32), jax.ShapeDtypeStruct((rows, cols), BF16),
                   jax.ShapeDtypeStruct((n_far, rows, cols), BF16)]
        + [jax.ShapeDtypeStruct(r.shape, r.dtype) for r in rides],
        compiler_params=_params("arbitrary", barrier_id=barrier_id),
    )(order, *operands, *rides)


def _chip_exchange_beside(first, last, sums, outs, sems, enter=True):
    chips = _other_chips(*_mesh_pos())
    copies = [_push(sums[i].at[k], outs[i].at[k], sems, len(chips) * i + k, chip)
              for i in range(len(sums)) for k, chip in enumerate(chips)]

    @pl.when(first)
    def _():
        if enter:
            _enter_with(chips)
        for cp in copies:
            cp.start()

    @pl.when(last)
    def _():
        for cp in copies:
            cp.wait()


ROW_GCONV, ROW_CW0 = 1, 2


def _conv_bwd(dmixed, gates, g_conv, conv_w, tm):
    t = gates.shape[0]
    n = t // tm
    rev = lambda i: n - 1 - i

    def body(dm_ref, gates_ref, gprev_ref, gc_ref, cw_ref, dgates_ref, small_ref, carry_ref):
        i = pl.program_id(0)

        @pl.when(i == 0)
        def _():
            small_ref[...] = jnp.zeros_like(small_ref)
            carry_ref[...] = jnp.zeros_like(carry_ref)

        gates = gates_ref[...]
        gb, gcc, xin = gates[:, :CONV_W], gates[:, CONV_W:2 * CONV_W], gates[:, 2 * CONV_W:]
        u = gcc * xin
        gp = gprev_ref[...]
        uprev = jnp.where(rev(i) == 0, 0.0, gp[:, CONV_W:2 * CONV_W] * gp[:, 2 * CONV_W:])
        u1, u2 = _shift_rows_down(u, uprev, 1), _shift_rows_down(u, uprev, 2)
        w = cw_ref[...]
        c = _conv3(u, u1, u2, w)
        conv = gb * c
        rcv = _inv_rms(conv)
        c_hat = conv * rcv
        dconv, dgc = _rms_bwd(c_hat, rcv, gc_ref[...], dm_ref[...])
        dc = dconv * gb
        nxt = carry_ref[...]
        du = (w[2:3, :] * dc + w[1:2, :] * _shift_rows_up(dc, nxt, 1)) + w[0:1, :] * _shift_rows_up(dc, nxt, 2)
        carry_ref[...] = dc[0:8, :]
        dgates_ref[:, :CONV_W] = (dconv * c).astype(BF16)
        dgates_ref[:, CONV_W:2 * CONV_W] = (du * xin).astype(BF16)
        dgates_ref[:, 2 * CONV_W:] = (du * gcc).astype(BF16)
        small_ref[ROW_GCONV:ROW_GCONV + 1, :] += _colsum(dgc)
        small_ref[ROW_CW0:ROW_CW0 + 1, :] += _colsum(dc * u2)
        small_ref[ROW_CW0 + 1:ROW_CW0 + 2, :] += _colsum(dc * u1)
        small_ref[ROW_CW0 + 2:ROW_CW0 + 3, :] += _colsum(dc * u)

    tile = lambda w_: pl.BlockSpec((tm, w_), lambda i: (rev(i), 0))
    prev8 = pl.BlockSpec((8, GATES_W), lambda i: (jnp.maximum(rev(i) * (tm // 8) - 1, 0), 0))
    conv_half = pl.BlockSpec((tm, CONV_W), lambda i: (rev(i), ATTN_W // CONV_W))
    return pl.pallas_call(
        body, name="conv_bwd", grid=(n,),
        in_specs=[conv_half, tile(GATES_W), prev8, _full((1, CONV_W)), _full((3, CONV_W))],
        out_specs=[tile(GATES_W), _full((SMALL_ROWS, CONV_W))],
        out_shape=[jax.ShapeDtypeStruct((t, GATES_W), BF16), jax.ShapeDtypeStruct((SMALL_ROWS, CONV_W), F32)],
        scratch_shapes=[pltpu.VMEM((8, CONV_W), F32)],
        compiler_params=_params("arbitrary"),
    )(dmixed, gates, gates, g_conv, conv_w)


def _attn_bwd(qkv, dmixed, attn, g_attn, sinks, rope, sums):
    t = qkv.shape[0]
    n_steps = t // ATTN_STEP
    rev = lambda i: n_steps - 1 - i
    rc, rs1, rs2 = rope

    def body(sink_ref, q_ref, kp_ref, kc_ref, vp_ref, vc_ref, dm_ref, attn_ref, ga_ref, c_ref, s1_ref, s2_ref, sums_ref,
             dqkv_ref, dsink_ref, dgain_ref, arrived_ref, ck_ref, cv_ref, kacc_ref, vacc_ref, send_sems, recv_sems):
        i = pl.program_id(0)
        _chip_exchange_beside(i == 0, i == n_steps - 1, [sums_ref], [arrived_ref], (send_sems, recv_sems))

        @pl.when(i == 0)
        def _():
            dsink_ref[...] = jnp.zeros_like(dsink_ref)
            dgain_ref[...] = jnp.zeros_like(dgain_ref)
            ck_ref[...] = jnp.zeros_like(ck_ref)
            cv_ref[...] = jnp.zeros_like(cv_ref)

        kacc_ref[...] = jnp.zeros_like(kacc_ref)
        vacc_ref[...] = jnp.zeros_like(vacc_ref)
        a = attn_ref[...]
        ra = _inv_rms(a)
        dattn, dgain = _rms_bwd(a * ra, ra, ga_ref[...], dm_ref[...])
        dgain_ref[0:1, :] += _colsum(dgain)
        qt = (q_ref[...] * ATTN_SCALE).T
        dot = dattn.astype(BF16).T
        keys = jnp.concatenate([kp_ref[...], kc_ref[...]], axis=0)
        vals = jnp.concatenate([vp_ref[...], vc_ref[...]], axis=0)
        sink = [_group_sinks(sink_ref, g) for g in range(N_KV)]
        c, s1, s2 = c_ref[...], s1_ref[...], s2_ref[...]
        lane = lax.broadcasted_iota(jnp.int32, (1, 128), 1)
        dsink = jnp.zeros((1, 128), F32)
        masks = _attn_masks(rev(i) > 0)
        dq_parts = []
        for b in range(ATTN_STEP_BLOCKS):
            window = slice(BLOCK * b, BLOCK * (b + 2))
            valid = masks[b]
            dq_parts.append([])
            dk_parts, dv_parts = [], []
            for g in range(N_KV):
                gs = slice(HEAD_DIM * g, HEAD_DIM * (g + 1))
                kk, vv = keys[window, gs], vals[window, gs]
                qtg, dotg = _heads_side_by_side(qt, g, b), _heads_side_by_side(dot, g, b)
                probs, psink = _attn_probs(qtg, kk, sink[g], valid)
                dp = _mm(vv, dotg)
                delta = jnp.sum(probs * dp, axis=0, keepdims=True)
                ds = (probs * (dp - delta)).astype(BF16)
                sink_terms = psink * delta
                for hh in range(GROUP):
                    head_sum = jnp.sum(sink_terms[:, BLOCK * hh:BLOCK * (hh + 1)])
                    dsink = dsink + jnp.where(lane == GROUP * g + hh, -head_sum, 0.0)
                dq_parts[b].append(_mm_tn(kk * ATTN_SCALE, ds))
                dk_parts.append(_mm_nt(ds, qtg))
                dv_parts.append(_mm_nt(probs.astype(BF16), dotg))
            kacc_ref[window, :] += jnp.concatenate(dk_parts, axis=1)
            vacc_ref[window, :] += jnp.concatenate(dv_parts, axis=1)
        dq = _to_token_rows(dq_parts)
        for ci in range(ATTN_W // 128):
            sl = slice(128 * ci, 128 * (ci + 1))
            dqkv_ref[:, sl] = _rope_transpose(dq[:, sl], c, s1, s2).astype(BF16)
        kacc_ref[ATTN_STEP:, :] += ck_ref[...]
        vacc_ref[ATTN_STEP:, :] += cv_ref[...]
        ck_ref[...] = kacc_ref[:BLOCK, :]
        cv_ref[...] = vacc_ref[:BLOCK, :]
        dqkv_ref[:, ATTN_W:ATTN_W + KV_W] = _rope_transpose(kacc_ref[BLOCK:, :], c, s1, s2).astype(BF16)
        dqkv_ref[:, ATTN_W + KV_W:] = vacc_ref[BLOCK:, :].astype(BF16)
        dsink_ref[0:1, :] += dsink

    blk = lambda w_: pl.BlockSpec((ATTN_STEP, w_), lambda i: (rev(i), 0))
    return pl.pallas_call(
        body, name="attn_bwd", grid=(n_steps,),
        in_specs=[pl.BlockSpec(memory_space=pltpu.SMEM)] + _qkv_specs(rev)
        + [blk(ATTN_W), blk(ATTN_W), _full((1, ATTN_W)), blk(128), blk(128), blk(128), HBM_SPEC],
        out_specs=[blk(QKV_W), _full((8, 128)), _full((SMALL_ROWS, ATTN_W)), HBM_SPEC],
        out_shape=[jax.ShapeDtypeStruct((t, QKV_W), BF16), jax.ShapeDtypeStruct((8, 128), F32),
                   jax.ShapeDtypeStruct((SMALL_ROWS, ATTN_W), F32), jax.ShapeDtypeStruct(sums.shape, sums.dtype)],
        scratch_shapes=[pltpu.VMEM((BLOCK, KV_W), F32), pltpu.VMEM((BLOCK, KV_W), F32),
                        pltpu.VMEM((ATTN_KEYS, KV_W), F32), pltpu.VMEM((ATTN_KEYS, KV_W), F32),
                        pltpu.SemaphoreType.DMA((len(CHIP_FLIPS),)), pltpu.SemaphoreType.DMA((len(CHIP_FLIPS),))],
        compiler_params=_params("arbitrary", barrier_id=6),
    )(sinks, qkv, qkv, qkv, qkv, qkv, dmixed, attn, g_attn, rc, rs1, rs2, sums)


def _grad_x_tile(dq, dg, x_hat, r, g1, w_ref, dh):
    dhn = _mm_nt(dq, w_ref[:, :QKV_W]) + _mm_nt(dg, w_ref[:, QKV_W:])
    dx, dg1 = _rms_bwd(x_hat, r, g1, dhn)
    return dh + dx, _colsum(dg1)


def _in_proj_bwd(dqkv, dgates, x, dh, g1, w_in, tm, out_sums):
    t = x.shape[0]
    n = t // tm
    n_cover = max(n // 2, 1)
    n_steps = n + n_cover
    n_far = len(CHIP_FLIPS)
    shard = (D_MODEL, IN_SHARD)

    def body(dq_ref, dg_ref, x_ref, dh_ref, g1_ref, w_ref, osums_ref,
             dx_ref, own_ref, sib_ref, far_ref, dg1_ref, oarrived_ref,
             acc_ref, send_buf, land_buf, pair_buf, d2d_send, d2d_recv, ici_send, ici_recv, o_send, o_recv):
        i = pl.program_id(0)
        x_pos, y_pos, c = _mesh_pos()
        my_chip = 2 * x_pos + y_pos
        sibling = (x_pos, y_pos, 1 - c)
        @pl.when(i == 0)
        def _():
            _enter_with(_sibling_and_chips(x_pos, y_pos, c))

        _chip_exchange_beside(i == 0, i == n_steps - 1, [osums_ref], [oarrived_ref], (o_send, o_recv), enter=False)

        def cols(d):
            return slice(IN_SHARD * d, IN_SHARD * (d + 1))

        def hand_over(chip):
            return _push(send_buf.at[chip], land_buf.at[chip], (d2d_send, d2d_recv), chip, sibling)

        def to_chip(chip, rel):
            return pltpu.make_async_remote_copy(
                src_ref=pair_buf.at[chip], dst_ref=far_ref.at[rel - 1], send_sem=ici_send.at[rel - 1],
                recv_sem=ici_recv.at[rel - 1], device_id=(chip // 2, chip % 2, c), device_id_type=MESH)

        @pl.when(i == 0)
        def _():
            acc_ref[...] = jnp.zeros_like(acc_ref)
            dg1_ref[...] = jnp.zeros_like(dg1_ref)

        def normed_x():
            xv = x_ref[...]
            r = _inv_rms(xv)
            return xv * r, r

        @pl.when(i < n)
        def _():
            hn = (normed_x()[0] * g1_ref[...]).astype(BF16)
            acc_ref[:, :QKV_W] += _mm_tn(hn, dq_ref[...])
            acc_ref[:, QKV_W:] += _mm_tn(hn, dg_ref[...])

        @pl.when(i == n - 1)
        def _():
            for d in range(N_DEV):
                @pl.when(d % 2 != c)
                def _():
                    send_buf[d // 2] = acc_ref[:, cols(d)].astype(BF16)
                    hand_over(d // 2).start()
            for d in range(N_DEV):
                chip = d // 2

                @pl.when(d % 2 == c)
                def _():
                    hand_over(chip).wait_recv()

                    @pl.when(chip == my_chip)
                    def _():
                        own_ref[...] = acc_ref[:, cols(d)]
                        sib_ref[...] = land_buf[chip]

                    @pl.when(chip != my_chip)
                    def _():
                        pair_buf[chip] = (acc_ref[:, cols(d)] + land_buf[chip].astype(F32)).astype(BF16)
                        to_chip(chip, chip ^ my_chip).start()
            for chip in range(N_CHIPS):
                hand_over(chip).wait_send()

        @pl.when(i >= n)
        def _():
            x_hat, r = normed_x()
            dx_ref[...], dg1 = _grad_x_tile(dq_ref[...], dg_ref[...], x_hat, r, g1_ref[...], w_ref, dh_ref[...])
            dg1_ref[0:1, :] += dg1

        @pl.when(i == n_steps - 1)
        def _():
            for rel in range(1, n_far + 1):
                to_chip(0, rel).wait()

    both = lambda w_: pl.BlockSpec((tm, w_), lambda i: (i % n, 0))
    second = pl.BlockSpec((tm, D_MODEL), lambda i: (jnp.maximum(i - n, 0), 0))
    whole = lambda dtype: jax.ShapeDtypeStruct(shard, dtype)
    sems = lambda k: pltpu.SemaphoreType.DMA((k,))
    res = pl.pallas_call(
        body, name="in_proj_bwd", grid=(n_steps,),
        in_specs=[both(QKV_W), both(GATES_W), both(D_MODEL), second, _full((1, D_MODEL)), _resident((D_MODEL, IN_COLS)),
                  HBM_SPEC],
        out_specs=[second, _full(shard), _full(shard), HBM_SPEC, _full((SMALL_ROWS, D_MODEL)), HBM_SPEC],
        out_shape=[jax.ShapeDtypeStruct((n_cover * tm, D_MODEL), F32), whole(F32), whole(BF16),
                   jax.ShapeDtypeStruct((n_far,) + shard, BF16), jax.ShapeDtypeStruct((SMALL_ROWS, D_MODEL), F32),
                   jax.ShapeDtypeStruct(out_sums.shape, out_sums.dtype)],
        scratch_shapes=[pltpu.VMEM((D_MODEL, IN_COLS), F32), pltpu.VMEM((N_CHIPS,) + shard, BF16),
                        pltpu.VMEM((N_CHIPS,) + shard, BF16), pltpu.VMEM((N_CHIPS,) + shard, BF16),
                        sems(N_CHIPS), sems(N_CHIPS), sems(n_far), sems(n_far), sems(n_far), sems(n_far)],
        compiler_params=_params("arbitrary", barrier_id=7),
    )(dqkv, dgates, x, dh, g1, w_in, out_sums)
    return res[0], (res[1], res[2], res[3]), res[4], res[5]


def _grad_x_rest(dqkv, dgates, x, dh, g1, w_in, tm, head, dg1_rows):
    t = x.shape[0]
    first = head.shape[0] // tm
    n_rest = t // tm - first
    if n_rest == 0:
        return head, dg1_rows
    assert first <= n_rest

    def body(dq_ref, dg_ref, x_ref, dh_ref, g1_ref, w_ref, head_ref, rows_ref, gx_ref, dg1_ref, stage, sems):
        j = pl.program_id(0)

        def tile_out(step, kind):
            row0 = (step + first) * tm if kind == 0 else step * tm
            slot = 2 * kind + step % 2
            return pltpu.make_async_copy(stage.at[slot], gx_ref.at[pl.ds(pl.multiple_of(row0, tm), tm), :], sems.at[slot])

        @pl.when(j == 0)
        def _():
            dg1_ref[...] = rows_ref[...]

        @pl.when(j >= 2)
        def _():
            tile_out(j - 2, 0).wait()

        @pl.when((j >= 2) & (j - 2 < first))
        def _():
            tile_out(j - 2, 1).wait()

        @pl.when(j < first)
        def _():
            stage[2 + j % 2] = head_ref[...]
            tile_out(j, 1).start()

        xv = x_ref[...]
        r = _inv_rms(xv)
        dx, dg1 = _grad_x_tile(dq_ref[...], dg_ref[...], xv * r, r, g1_ref[...], w_ref, dh_ref[...])
        stage[j % 2] = dx
        dg1_ref[0:1, :] += dg1
        tile_out(j, 0).start()

        @pl.when(j == n_rest - 1)
        def _():
            for back in range(min(2, n_rest)):
                tile_out(j - back, 0).wait()

                @pl.when(j - back < first)
                def _():
                    tile_out(j - back, 1).wait()

    tile = lambda w_: pl.BlockSpec((tm, w_), lambda j: (j + first, 0))
    head_tile = pl.BlockSpec((tm, D_MODEL), lambda j: (jnp.minimum(j, first - 1), 0))
    return pl.pallas_call(
        body, name="grad_x_rest", grid=(n_rest,),
        in_specs=[tile(QKV_W), tile(GATES_W), tile(D_MODEL), tile(D_MODEL), _full((1, D_MODEL)),
                  _resident((D_MODEL, IN_COLS)), head_tile, _full((SMALL_ROWS, D_MODEL))],
        out_specs=[HBM_SPEC, _full((SMALL_ROWS, D_MODEL))],
        out_shape=[jax.ShapeDtypeStruct((t, D_MODEL), F32), jax.ShapeDtypeStruct((SMALL_ROWS, D_MODEL), F32)],
        scratch_shapes=[pltpu.VMEM((4, tm, D_MODEL), F32), pltpu.SemaphoreType.DMA((4,))],
        compiler_params=_params("arbitrary"),
    )(dqkv, dgates, x, dh, g1, w_in, head, dg1_rows)


def _all_gather(shards, name):
    n = len(shards)

    def body(*refs):
        _enter_with(_sibling_and_chips(*_mesh_pos()))
        start, finish = _gather_steps(refs[:n], refs[n:2 * n], *refs[2 * n:])
        start()
        finish()

    return pl.pallas_call(
        body, name=name,
        in_specs=[HBM_SPEC] * n, out_specs=[HBM_SPEC] * n,
        out_shape=[jax.ShapeDtypeStruct((N_DEV,) + s.shape, s.dtype) for s in shards],
        scratch_shapes=[pltpu.SemaphoreType.DMA((7 * n,)), pltpu.SemaphoreType.DMA((7 * n,)),
                        pltpu.SemaphoreType.DMA((n,))],
        compiler_params=_params(barrier_id=8),
    )(*shards)


def _adam_math(w, g, m, v):
    m = ADAM_B1 * m + (1.0 - ADAM_B1) * g
    v = ADAM_B2 * v + (1.0 - ADAM_B2) * (g * g)
    m_hat = m / (1.0 - ADAM_B1 ** ADAM_STEP)
    v_hat = v / (1.0 - ADAM_B2 ** ADAM_STEP)
    delta = -ADAM_LR * (m_hat / (jnp.sqrt(v_hat) + ADAM_EPS) + ADAM_WD * w)
    return delta, m, v


def _adamw_reduced(w, m, v, own, from_sibling, from_chips, tr):
    rows, cols = w.shape

    def body(w_ref, m_ref, v_ref, own_ref, sib_ref, far_ref, g_ref, d_ref, nm_ref, nv_ref):
        g = own_ref[...] + sib_ref[...].astype(F32)
        for k in range(len(CHIP_FLIPS)):
            g = g + far_ref[k].astype(F32)
        g_ref[...] = g
        d_ref[...], nm_ref[...], nv_ref[...] = _adam_math(w_ref[...], g, m_ref[...], v_ref[...])

    tile = pl.BlockSpec((tr, cols), lambda i: (i, 0))
    out = jax.ShapeDtypeStruct((rows, cols), F32)
    return pl.pallas_call(
        body, name="adamw_reduced", grid=(rows // tr,),
        in_specs=[tile] * 5 + [pl.BlockSpec((len(CHIP_FLIPS), tr, cols), lambda i: (0, i, 0))],
        out_specs=[tile] * 4, out_shape=[out] * 4,
        compiler_params=_params("parallel"),
    )(w, m, v, own, from_sibling, from_chips)


def _sum_devices(gathered):
    _, rows, cols = gathered.shape

    def body(g_ref, o_ref):
        s = g_ref[0]
        for d in range(1, N_DEV):
            s = s + g_ref[d]
        o_ref[...] = s

    return pl.pallas_call(
        body, name="sum_devices", in_specs=[_full(gathered.shape)], out_specs=_full((rows, cols)), grid=(1,),
        out_shape=jax.ShapeDtypeStruct((rows, cols), F32),
    )(gathered)


def _adamw_small(w, g, m, v):
    def body(w_ref, g_ref, m_ref, v_ref, d_ref, nm_ref, nv_ref):
        d_ref[...], nm_ref[...], nv_ref[...] = _adam_math(w_ref[...], g_ref[...], m_ref[...], v_ref[...])

    spec = _full(w.shape)
    out = jax.ShapeDtypeStruct(w.shape, F32)
    return pl.pallas_call(
        body, name="adamw_small", grid=(1,), in_specs=[spec] * 4, out_specs=[spec] * 3, out_shape=[out] * 3,
    )(w, g, m, v)


TOKEN_TILE = 512
MID_TILE = 256
MID_CHUNK = 1024
MID_CHUNKS = D_FF // MID_CHUNK
ADAM_ROWS = 128


def _local_grads(x, target, g1, w_in_shard, conv_shard, sinks, g_attn, g_conv, g2, g3, g4, shards, order):
    t = x.shape[0]
    tm = min(TOKEN_TILE, t)
    rope = _rope_tables(t)
    qkv, gates, mconv, w_in, conv_w, gathered = _in_proj_fwd(x, g1, w_in_shard, conv_shard, g_conv, rope, tm, shards,
                                                             (False, True, False))
    attn, mattn, (w_out, w_up, w_down) = _attn_fwd(qkv, sinks, g_attn, shards, gathered)
    act, dup, hn2t, dmo, dmix, dh, dmixed, small_mid = _mid(
        mattn, mconv, x, target, g2, g3, g4, w_out.reshape(D_MODEL, D_MODEL),
        w_up, w_down.reshape(D_FF, D_MODEL), min(MID_TILE, t))
    up_own, up_sib, up_sums = _dw_pair_sums((hn2t, dup), order, "up", "dw_up", 2)
    down_own, down_sib, down_sums, up_far = _dw_pair_sums((act, dmo), order, "down", "dw_down", 3, ride=up_sums)
    out_own, out_sib, out_sums = _dw_pair_sums((mattn, mconv, dmix), order, "out", "dw_out", 4)
    dgates, small_conv = _conv_bwd(dmixed, gates, g_conv, conv_w, tm)
    dqkv, dsink, dg_attn, down_far = _attn_bwd(qkv, dmixed, attn, g_attn, sinks, rope, down_sums)
    grad_x_head, dw_in, small_in, out_far = _in_proj_bwd(dqkv, dgates, x, dh, g1, w_in, tm, out_sums)
    grad_x, small_in = _grad_x_rest(dqkv, dgates, x, dh, g1, w_in, tm, grad_x_head, small_in)
    dw_out, dw_up, dw_down = (out_own, out_sib, out_far), (up_own, up_sib, up_far), (down_own, down_sib, down_far)
    return grad_x, dw_in, dw_out, dw_up, dw_down, (small_mid, small_conv, dg_attn, dsink, small_in)


def _pack_small(small_mid, small_mix, dg_attn, dsink, small_in):
    z = lambda n: jnp.zeros((1, n), F32)
    rows = [
        small_mid[ROW_LOSS:ROW_LOSS + 1],
        small_in[0:1],
        small_mid[ROW_G2:ROW_G2 + 1],
        small_mid[ROW_G3:ROW_G3 + 1],
        small_mid[ROW_G4:ROW_G4 + 1],
        jnp.concatenate([dg_attn[0:1], small_mix[ROW_GCONV:ROW_GCONV + 1]], axis=1),
        jnp.concatenate([small_mix[ROW_CW0:ROW_CW0 + 1], small_mix[ROW_CW0 + 1:ROW_CW0 + 2]], axis=1),
        jnp.concatenate([small_mix[ROW_CW0 + 2:ROW_CW0 + 3], dsink[0:1, :], z(D_MODEL - CONV_W - 128)], axis=1),
    ]
    return jnp.concatenate(rows, axis=0)


def kernel(x, pre_mix_norm, w_in, conv_w, attn_sinks, attn_group_norm, conv_group_norm, w_out, post_mix_norm, pre_mlp_norm, w_up, w_down, post_mlp_norm, loss_target, m_pre_mix_norm, m_w_in, m_conv_w, m_attn_sinks, m_attn_group_norm, m_conv_group_norm, m_w_out, m_post_mix_norm, m_pre_mlp_norm, m_w_up, m_w_down, m_post_mlp_norm, v_pre_mix_norm, v_w_in, v_conv_w, v_attn_sinks, v_attn_group_norm, v_conv_group_norm, v_w_out, v_post_mix_norm, v_pre_mlp_norm, v_w_up, v_w_down, v_post_mlp_norm):
    xi, yi, ci = _mesh_pos()
    chip = 2 * xi + yi
    dev = 2 * chip + ci

    order = _block_order(dev)

    shards = [w_out[0].astype(BF16), w_up[0].astype(BF16), w_down[0].astype(BF16)]

    grad_x, dw_in, dw_out, dw_up, dw_down, smalls = _local_grads(
        x[0], loss_target[0], pre_mix_norm, w_in[0].astype(BF16), conv_w[0], attn_sinks, attn_group_norm, conv_group_norm,
        post_mix_norm, pre_mlp_norm, post_mlp_norm, shards, order)

    small = _sum_devices(_all_gather([_pack_small(*smalls)], "gather_small")[0])
    loss = (0.5 / D_MODEL) * jnp.sum(small[0])

    big = {}
    for name, w, m, v, (own, sib, far) in zip(
            ("w_in", "w_out", "w_up", "w_down"), (w_in, w_out, w_up, w_down), (m_w_in, m_w_out, m_w_up, m_w_down),
            (v_w_in, v_w_out, v_w_up, v_w_down), (dw_in, dw_out, dw_up, dw_down)):
        big[name] = [a[None] for a in _adamw_reduced(w[0], m[0], v[0], own, sib, far, ADAM_ROWS)]

    conv_g = lax.dynamic_slice(
        jnp.stack([small[6, :CONV_W], small[6, CONV_W:], small[7, :CONV_W]]), (0, dev * (CONV_W // N_DEV)),
        (3, CONV_W // N_DEV))
    pad = lambda a, n: jnp.pad(a.reshape(1, -1), ((0, 0), (0, n - a.size)))
    small_names = ("pre_mix_norm", "post_mix_norm", "pre_mlp_norm", "post_mlp_norm")
    small_w = {"pre_mix_norm": (pre_mix_norm, m_pre_mix_norm, v_pre_mix_norm),
               "post_mix_norm": (post_mix_norm, m_post_mix_norm, v_post_mix_norm),
               "pre_mlp_norm": (pre_mlp_norm, m_pre_mlp_norm, v_pre_mlp_norm),
               "post_mlp_norm": (post_mlp_norm, m_post_mlp_norm, v_post_mlp_norm)}

    def pack(k):
        rows = [small_w[nm][k] for nm in small_names]
        rows.append(jnp.concatenate([(attn_group_norm, m_attn_group_norm, v_attn_group_norm)[k],
                                     (conv_group_norm, m_conv_group_norm, v_conv_group_norm)[k]], axis=1))
        rows.append(pad((conv_w, m_conv_w, v_conv_w)[k], D_MODEL))
        rows.append(pad((attn_sinks, m_attn_sinks, v_attn_sinks)[k], D_MODEL))
        rows.append(jnp.zeros((1, D_MODEL), F32))
        return jnp.concatenate(rows, axis=0)

    g_small = jnp.concatenate(
        [small[1:6], pad(conv_g, D_MODEL), pad(small[7, CONV_W:CONV_W + N_HEADS], D_MODEL), jnp.zeros((1, D_MODEL), F32)],
        axis=0)
    d_small, nm_small, nv_small = _adamw_small(pack(0), g_small, pack(1), pack(2))

    def unpack(a):
        nconv = 3 * CONV_W // N_DEV
        return {"pre_mix_norm": a[0:1], "post_mix_norm": a[1:2], "pre_mlp_norm": a[2:3], "post_mlp_norm": a[3:4],
                "attn_group_norm": a[4:5, :ATTN_W], "conv_group_norm": a[4:5, ATTN_W:],
                "conv_w": a[5, :nconv].reshape(1, 3, CONV_W // N_DEV), "attn_sinks": a[6:7, :N_HEADS]}

    order = ("pre_mix_norm", "w_in", "conv_w", "attn_sinks", "attn_group_norm", "conv_group_norm", "w_out",
             "post_mix_norm", "pre_mlp_norm", "w_up", "w_down", "post_mlp_norm")
    outs = []
    for k, a in enumerate((g_small, d_small, nm_small, nv_small)):
        sm = unpack(a)
        outs += [big[nm][k] if nm in big else sm[nm] for nm in order]
    return (loss, grad_x[None], *outs)
```

```python
import functools

import jax
import jax.numpy as jnp
import numpy as np
from jax import lax
from jax.experimental import pallas as pl
from jax.experimental.pallas import tpu as pltpu

F32 = jnp.float32
BF16 = jnp.bfloat16

D_MODEL = 1024
HEAD_DIM = 64
ATTN_W = 512
CONV_W = 512
N_HEADS = 8
N_KV = 2
GROUP = 4
KV_W = 128
QKV_W = ATTN_W + 2 * KV_W
GATES_W = 3 * CONV_W
IN_COLS = QKV_W + GATES_W
D_FF = 4096
FF_CHUNK = 512
N_FF_CHUNKS = D_FF // FF_CHUNK
BLOCK = 128
ROT_HALF = 8
ROPE_THETA = 500000.0
NORM_EPS = 1e-6
NEG_INF = -1e30
ATTN_SCALE = 0.125
N_DEV = 8
N_CHIPS = 4
IN_SHARD = IN_COLS // N_DEV

ADAM_LR = 0.001
ADAM_B1 = 0.9
ADAM_B2 = 0.999
ADAM_EPS = 1e-08
ADAM_WD = 0.01
ADAM_STEP = 10

V7X_VMEM_BYTES = 64 * 1024 * 1024
VMEM_LIMIT = V7X_VMEM_BYTES - 2 * 1024 * 1024

MESH = pl.DeviceIdType.MESH
HBM_SPEC = pl.BlockSpec(memory_space=pltpu.HBM)


def _params(*sem, barrier_id=None):
    return pltpu.CompilerParams(dimension_semantics=sem or None, vmem_limit_bytes=VMEM_LIMIT, collective_id=barrier_id)


def _mm(a, b):
    return jnp.dot(a, b, preferred_element_type=F32)


def _mm_nt(a, b):
    return lax.dot_general(a, b, (((1,), (1,)), ((), ())), preferred_element_type=F32)


def _mm_tn(a, b):
    return lax.dot_general(a, b, (((0,), (0,)), ((), ())), preferred_element_type=F32)


def _inv_rms(x):
    return lax.rsqrt(jnp.mean(x * x, axis=-1, keepdims=True) + NORM_EPS)


def _rms_bwd(xhat, r, gain, dy):
    gy = dy * gain
    return r * (gy - xhat * jnp.mean(gy * xhat, axis=-1, keepdims=True)), dy * xhat


def _colsum(a):
    return jnp.sum(a, axis=0, keepdims=True)


def _full(shape):
    zeros = (0,) * len(shape)
    return pl.BlockSpec(shape, lambda *_: zeros)


def _resident(shape):
    zeros = (0,) * len(shape)
    return pl.BlockSpec(shape, lambda *_: zeros, pipeline_mode=pl.Buffered(1))


def _rope_tables(t):
    pos = np.arange(t, dtype=np.float32)
    inv_freq = (ROPE_THETA ** (-np.arange(0, 2 * ROT_HALF, 2, dtype=np.float64) / (2 * ROT_HALF))).astype(np.float32)
    ang = (pos[:, None] * inv_freq[None, :]).astype(np.float64)
    cos, sin = np.cos(ang).astype(np.float32), np.sin(ang).astype(np.float32)
    zeros8 = np.zeros((t, ROT_HALF), np.float32)
    rest = np.zeros((t, HEAD_DIM - 2 * ROT_HALF), np.float32)
    c_head = np.concatenate([cos, cos, rest + 1.0], axis=1)
    s1_head = np.concatenate([zeros8, sin, rest], axis=1)
    s2_head = np.concatenate([-sin, zeros8, rest], axis=1)
    two = lambda a: jnp.asarray(np.concatenate([a, a], axis=1))
    return two(c_head), two(s1_head), two(s2_head)


def _rope(v, c, s1, s2):
    return v * c + pltpu.roll(v, ROT_HALF, 1) * s1 + pltpu.roll(v, 128 - ROT_HALF, 1) * s2


def _rope_transpose(dv, c, s1, s2):
    return dv * c + pltpu.roll(dv * s1, 128 - ROT_HALF, 1) + pltpu.roll(dv * s2, ROT_HALF, 1)


def _shift_rows_down(u, prev, k):
    row = lax.broadcasted_iota(jnp.int32, u.shape, 0)
    out = pltpu.roll(u, k, 0)
    for r in range(k):
        out = jnp.where(row == r, prev[8 - k + r:8 - k + r + 1, :], out)
    return out


def _shift_rows_up(u, nxt, k):
    n = u.shape[0]
    row = lax.broadcasted_iota(jnp.int32, u.shape, 0)
    out = pltpu.roll(u, n - k, 0)
    for r in range(k):
        out = jnp.where(row == n - k + r, nxt[r:r + 1, :], out)
    return out


def _conv3(u, u1, u2, w):
    return (w[0:1, :] * u2 + w[1:2, :] * u1) + w[2:3, :] * u


def _mesh_pos():
    return lax.axis_index("x"), lax.axis_index("y"), lax.axis_index("c")


def _slot(ref, pos):
    dev = 4 * pos[0] + 2 * pos[1] + pos[2]
    if len(ref.shape) == 2:
        width = ref.shape[1] // N_DEV
        return ref.at[:, pl.ds(pl.multiple_of(dev * width, width), width)]
    return ref.at[dev]


def _gathered_shape(shard, by_cols):
    if by_cols:
        return jax.ShapeDtypeStruct((shard.shape[0], N_DEV * shard.shape[1]), shard.dtype)
    return jax.ShapeDtypeStruct((N_DEV,) + shard.shape, shard.dtype)


def _enter_with(peers):
    barrier = pltpu.get_barrier_semaphore()
    for peer in peers:
        pl.semaphore_signal(barrier, inc=1, device_id=peer, device_id_type=MESH)
    pl.semaphore_wait(barrier, len(peers))


def _sibling_and_chips(x, y, c):
    return [(x, y, 1 - c), (1 - x, y, c), (x, 1 - y, c), (1 - x, 1 - y, c)]


def _push(src, dst, sems, k, to):
    send_sems, recv_sems = sems
    return pltpu.make_async_remote_copy(src_ref=src, dst_ref=dst, send_sem=send_sems.at[k], recv_sem=recv_sems.at[k],
                                        device_id=to, device_id_type=MESH)


def _gather_steps(shards, outs, send_sems, recv_sems, local_sems):
    n = len(shards)
    x, y, c = _mesh_pos()
    me, sibling = (x, y, c), (x, y, 1 - c)
    chips = [(1 - x, y), (x, 1 - y), (1 - x, 1 - y)]

    def copy(i, k, block, to, src=None):
        dst = _slot(outs[i], block)
        return _push(dst if src is None else src, dst, (send_sems, recv_sems), 7 * i + k, to)

    mine = [pltpu.make_async_copy(shards[i], _slot(outs[i], me), local_sems.at[i]) for i in range(n)]
    first = []
    for i in range(n):
        first.append(copy(i, 0, me, sibling, src=shards[i]))
        first += [copy(i, 1 + j, me, (*chip, c), src=shards[i]) for j, chip in enumerate(chips)]

    def start():
        for cp in mine + first:
            cp.start()

    def finish():
        passed = []
        for j, chip in enumerate(chips):
            for i in range(n):
                copy(i, 1 + j, (*chip, c), me).wait_recv()
                cp = copy(i, 4 + j, (*chip, c), sibling)
                cp.start()
                passed.append(cp)
        for i in range(n):
            copy(i, 0, sibling, me).wait_recv()
            for j, chip in enumerate(chips):
                copy(i, 4 + j, (*chip, 1 - c), me).wait_recv()
        for cp in first + passed:
            cp.wait_send()
        for cp in mine:
            cp.wait()

    return start, finish


def _gather_near(first, last, shards, outs, sems, local_sems):
    x, y, c = _mesh_pos()
    me, peers = (x, y, c), [(x, y, 1 - c), (1 - x, y, c), (x, 1 - y, c)]
    n = len(shards)
    local = [pltpu.make_async_copy(shards[i], _slot(outs[i], me), local_sems.at[i]) for i in range(n)]
    sends = [_push(shards[i], _slot(outs[i], me), sems, 3 * i + k, peers[k]) for i in range(n) for k in range(3)]
    arrivals = [_push(shards[i], _slot(outs[i], peers[k]), sems, 3 * i + k, peers[k]) for i in range(n) for k in range(3)]

    def start():
        for cp in local + sends:
            cp.start()

    if first is not None:
        pl.when(first)(start)

    @pl.when(last)
    def _():
        for cp in sends:
            cp.wait_send()
        for cp in arrivals:
            cp.wait_recv()
        for cp in local:
            cp.wait()

    return start


def _gather_far(first, last, shards, ins, outs, sems):
    x, y, c = _mesh_pos()
    me, sibling = (x, y, c), (x, y, 1 - c)
    chips = [(1 - x, y), (x, 1 - y), (1 - x, 1 - y)]
    n = len(shards)
    diag_send = [_push(shards[i], _slot(outs[i], me), sems, 4 * i, (*chips[2], c)) for i in range(n)]
    diag_arrival = [_push(shards[i], _slot(outs[i], (*chips[2], c)), sems, 4 * i, (*chips[2], c)) for i in range(n)]
    passed = [[_push(_slot(ins[i], (*chips[j], c)), _slot(outs[i], (*chips[j], c)), sems, 4 * i + 1 + j, sibling)
               for i in range(n)] for j in range(3)]
    from_sibling = [_push(shards[i], _slot(outs[i], (*chips[j], 1 - c)), sems, 4 * i + 1 + j, sibling)
                    for i in range(n) for j in range(3)]

    @pl.when(first)
    def _():
        for cp in diag_send + passed[0] + passed[1]:
            cp.start()

    @pl.when(last)
    def _():
        for cp in diag_arrival:
            cp.wait_recv()
        for cp in passed[2]:
            cp.start()
        for cp in from_sibling:
            cp.wait_recv()
        for cp in diag_send + passed[0] + passed[1] + passed[2]:
            cp.wait_send()


def _in_proj_fwd(x, g1, w_in, conv_w, g_conv, rope, tm, shards, by_cols):
    t = x.shape[0]
    rc, rs1, rs2 = rope
    n = len(shards)
    n_tiles = t // tm

    def body(*refs):
        x_ref, g1_ref, w_ref, cw_ref, gc_ref, c_ref, s1_ref, s2_ref = refs[:8]
        shard_refs = refs[8:8 + n]
        qkv_ref, gates_ref, mconv_ref, w_full_ref, cw_full_ref = refs[8 + n:13 + n]
        gathered = refs[13 + n:13 + 2 * n]
        carry_ref, w_land, cw_land, hn_ref = refs[13 + 2 * n:17 + 2 * n]
        now_sems = refs[17 + 2 * n:20 + 2 * n]
        step = pl.program_id(0)
        start_later_weights = _gather_near(None, step == 2 * n_tiles - 1, shard_refs, gathered,
                                           refs[20 + 2 * n:22 + 2 * n], refs[22 + 2 * n]) if n else None
        start_w_in, finish_w_in = _gather_steps([w_ref, cw_ref], [w_land, cw_land], *now_sems)

        @pl.when(step == 0)
        def _():
            carry_ref[...] = jnp.zeros_like(carry_ref)
            _enter_with(_sibling_and_chips(*_mesh_pos()))
            start_w_in()
            if start_later_weights is not None:
                start_later_weights()

        @pl.when(step < n_tiles)
        def _():
            xv = x_ref[...]
            hn_ref[step] = ((xv * _inv_rms(xv)) * g1_ref[...]).astype(BF16)

        @pl.when(step == n_tiles)
        def _():
            finish_w_in()
            conv_shard = CONV_W // N_DEV
            for d in range(N_DEV):
                w_full_ref[:, IN_SHARD * d:IN_SHARD * (d + 1)] = w_land[d]
                cw_full_ref[:, conv_shard * d:conv_shard * (d + 1)] = cw_land[d]

        @pl.when(step >= n_tiles)
        def _():
            proj = _mm(hn_ref[step - n_tiles], w_full_ref[...])
            c, s1, s2 = c_ref[...], s1_ref[...], s2_ref[...]
            for ci in range((ATTN_W + KV_W) // 128):
                sl = slice(128 * ci, 128 * (ci + 1))
                qkv_ref[:, sl] = _rope(proj[:, sl], c, s1, s2).astype(BF16)
            qkv_ref[:, ATTN_W + KV_W:QKV_W] = proj[:, ATTN_W + KV_W:QKV_W].astype(BF16)
            gates = proj[:, QKV_W:]
            gates_ref[...] = gates
            gb, gcc, xin = gates[:, :CONV_W], gates[:, CONV_W:2 * CONV_W], gates[:, 2 * CONV_W:]
            u = gcc * xin
            prev = carry_ref[...]
            conv = gb * _conv3(u, _shift_rows_down(u, prev, 1), _shift_rows_down(u, prev, 2), cw_full_ref[...])
            carry_ref[...] = u[tm - 8:tm, :]
            mconv_ref[...] = ((conv * _inv_rms(conv)) * gc_ref[...]).astype(BF16)

    first_pass = pl.BlockSpec((tm, D_MODEL), lambda i: (jnp.minimum(i, n_tiles - 1), 0))
    tile = lambda w_: pl.BlockSpec((tm, w_), lambda i: (jnp.maximum(i - n_tiles, 0), 0))
    sems = lambda k: pltpu.SemaphoreType.DMA((k,))
    res = pl.pallas_call(
        body, name="in_proj_fwd", grid=(2 * n_tiles,),
        in_specs=[first_pass, _full((1, D_MODEL)), HBM_SPEC, HBM_SPEC, _full((1, CONV_W)), tile(128), tile(128),
                  tile(128)] + [HBM_SPEC] * n,
        out_specs=[tile(QKV_W), tile(GATES_W), tile(CONV_W), _full((D_MODEL, IN_COLS)), _full((3, CONV_W))]
        + [HBM_SPEC] * n,
        out_shape=[jax.ShapeDtypeStruct((t, QKV_W), BF16), jax.ShapeDtypeStruct((t, GATES_W), F32),
                   jax.ShapeDtypeStruct((t, CONV_W), BF16), jax.ShapeDtypeStruct((D_MODEL, IN_COLS), BF16),
                   jax.ShapeDtypeStruct((3, CONV_W), F32)]
        + [_gathered_shape(s, cols) for s, cols in zip(shards, by_cols)],
        scratch_shapes=[pltpu.VMEM((8, CONV_W), F32), pltpu.VMEM((N_DEV,) + w_in.shape, BF16),
                        pltpu.VMEM((N_DEV,) + conv_w.shape, F32), pltpu.VMEM((n_tiles, tm, D_MODEL), BF16),
                        sems(14), sems(14), sems(2)]
        + ([sems(3 * n), sems(3 * n), sems(n)] if n else []),
        compiler_params=_params("arbitrary", barrier_id=0),
    )(x, g1, w_in, conv_w, g_conv, rc, rs1, rs2, *shards)
    return res[0], res[1], res[2], res[3], res[4], list(res[5:])


GROUP_COLS = GROUP * BLOCK
ATTN_STEP_BLOCKS = 4


def _attn_masks(has_prev):
    key = lax.broadcasted_iota(jnp.int32, (2 * BLOCK, GROUP_COLS), 0)
    query = lax.broadcasted_iota(jnp.int32, (2 * BLOCK, GROUP_COLS), 1) & (BLOCK - 1)
    band = (key > query) & (key <= query + BLOCK)
    return [band & ((key >= BLOCK) | has_prev)] + [band] * (ATTN_STEP_BLOCKS - 1)


def _heads_side_by_side(at, g, b):
    heads = [at[HEAD_DIM * (GROUP * g + hh):HEAD_DIM * (GROUP * g + hh + 1), BLOCK * b:BLOCK * (b + 1)] for hh in range(GROUP)]
    return jnp.concatenate(heads, axis=1)


def _to_token_rows(parts):
    rows = [jnp.concatenate([parts[b][g][:, BLOCK * hh:BLOCK * (hh + 1)] for b in range(ATTN_STEP_BLOCKS)], axis=1)
            for g in range(N_KV) for hh in range(GROUP)]
    return jnp.concatenate(rows, axis=0).T


def _group_sinks(sink_ref, g):
    head = lax.broadcasted_iota(jnp.int32, (1, GROUP_COLS), 1) // BLOCK
    out = jnp.full((1, GROUP_COLS), sink_ref[0, GROUP * g], F32)
    for hh in range(1, GROUP):
        out = jnp.where(head == hh, sink_ref[0, GROUP * g + hh], out)
    return out


def _attn_probs(qt, kk, sink, valid):
    s = jnp.where(valid, _mm(kk, qt), NEG_INF)
    m = jnp.maximum(jnp.max(s, axis=0, keepdims=True), sink)
    p = jnp.exp(s - m)
    psink = jnp.exp(sink - m)
    inv_l = 1.0 / (jnp.sum(p, axis=0, keepdims=True) + psink)
    return p * inv_l, psink * inv_l


ATTN_STEP = ATTN_STEP_BLOCKS * BLOCK
ATTN_KEYS = ATTN_STEP + BLOCK


def _qkv_specs(order):
    prev = lambda i: jnp.maximum(ATTN_STEP_BLOCKS * order(i) - 1, 0)
    kcol, vcol = ATTN_W // KV_W, ATTN_W // KV_W + 1
    return [pl.BlockSpec((ATTN_STEP, ATTN_W), lambda i: (order(i), 0)),
            pl.BlockSpec((BLOCK, KV_W), lambda i: (prev(i), kcol)), pl.BlockSpec((ATTN_STEP, KV_W), lambda i: (order(i), kcol)),
            pl.BlockSpec((BLOCK, KV_W), lambda i: (prev(i), vcol)), pl.BlockSpec((ATTN_STEP, KV_W), lambda i: (order(i), vcol))]


def _attn_fwd(qkv, sinks, g_attn, shards, gathered):
    t = qkv.shape[0]
    n = len(shards)

    def body(*refs):
        sink_ref, q_ref, kp_ref, kc_ref, vp_ref, vc_ref, ga_ref = refs[:7]
        attn_ref, mattn_ref = refs[7 + 2 * n:9 + 2 * n]
        step = pl.program_id(0)
        if n:
            @pl.when(step == 0)
            def _():
                x, y, c = _mesh_pos()
                _enter_with([(x, y, 1 - c), (1 - x, 1 - y, c)])

            _gather_far(step == 0, step == pl.num_programs(0) - 1, refs[7:7 + n], refs[7 + n:7 + 2 * n],
                        refs[9 + 2 * n:9 + 3 * n], refs[9 + 3 * n:11 + 3 * n])
        qt = (q_ref[...] * ATTN_SCALE).T
        keys = jnp.concatenate([kp_ref[...], kc_ref[...]], axis=0)
        vals = jnp.concatenate([vp_ref[...], vc_ref[...]], axis=0)
        sink = [_group_sinks(sink_ref, g) for g in range(N_KV)]
        masks = _attn_masks(step > 0)
        parts = []
        for b in range(ATTN_STEP_BLOCKS):
            window = slice(BLOCK * b, BLOCK * (b + 2))
            valid = masks[b]
            parts.append([])
            for g in range(N_KV):
                gs = slice(HEAD_DIM * g, HEAD_DIM * (g + 1))
                probs, _ = _attn_probs(_heads_side_by_side(qt, g, b), keys[window, gs], sink[g], valid)
                parts[b].append(_mm_tn(vals[window, gs], probs.astype(BF16)))
        attn = _to_token_rows(parts)
        attn_ref[...] = attn
        mattn_ref[...] = ((attn * _inv_rms(attn)) * ga_ref[...]).astype(BF16)

    blk = pl.BlockSpec((ATTN_STEP, ATTN_W), lambda j: (j, 0))
    res = pl.pallas_call(
        body, name="attn_fwd", grid=(t // ATTN_STEP,),
        in_specs=[pl.BlockSpec(memory_space=pltpu.SMEM)] + _qkv_specs(lambda j: j) + [_full((1, ATTN_W))]
        + [HBM_SPEC] * (2 * n),
        out_specs=[blk, blk] + [HBM_SPEC] * n,
        out_shape=[jax.ShapeDtypeStruct((t, ATTN_W), F32), jax.ShapeDtypeStruct((t, ATTN_W), BF16)]
        + [jax.ShapeDtypeStruct(g.shape, g.dtype) for g in gathered],
        input_output_aliases={7 + n + i: 2 + i for i in range(n)},
        scratch_shapes=[pltpu.SemaphoreType.DMA((4 * n,)), pltpu.SemaphoreType.DMA((4 * n,))] if n else [],
        compiler_params=_params("arbitrary", barrier_id=1 if n else None),
    )(sinks, qkv, qkv, qkv, qkv, qkv, g_attn, *shards, *gathered)
    return res[0], res[1], list(res[2:])


SMALL_ROWS = 8
ROW_LOSS, ROW_G2, ROW_G3, ROW_G4 = 0, 1, 2, 3


def _mid(mattn, mconv, x, target, g2, g3, g4, w_out, w_up, w_down, tm):
    t = x.shape[0]

    def body(ma_ref, mc_ref, x_ref, t_ref, g2_ref, g3_ref, g4_ref, wo_ref, wu_ref, wd_ref,
             act_ref, dup_ref, hn2t_ref, dmo_ref, dmix_ref, dh_ref, dmixed_ref, small_ref, up_ref):
        @pl.when(pl.program_id(0) == 0)
        def _():
            small_ref[...] = jnp.zeros_like(small_ref)

        g2, g3, g4 = g2_ref[...], g3_ref[...], g4_ref[...]
        mix_out = _mm(ma_ref[...], wo_ref[0:ATTN_W, :]) + _mm(mc_ref[...], wo_ref[ATTN_W:, :])
        r2 = _inv_rms(mix_out)
        mo_hat = mix_out * r2
        h = x_ref[...] + mo_hat * g2
        r3 = _inv_rms(h)
        h_hat = h * r3
        hn2 = (h_hat * g3).astype(BF16)
        hn2t_ref[...] = hn2.T
        for j in range(MID_CHUNKS):
            cols_j = slice(MID_CHUNK * j, MID_CHUNK * (j + 1))
            up = jnp.maximum(_mm(hn2, wu_ref[:, cols_j]), 0.0)
            up_ref[:, cols_j] = up.astype(BF16)
            act_ref[:, cols_j] = (up * up).astype(BF16)
        mlp = _mm(act_ref[...], wd_ref[...])
        r4 = _inv_rms(mlp)
        ml_hat = mlp * r4
        err = (h + ml_hat * g4) - t_ref[...]
        d_out = err * (1.0 / D_MODEL)
        d_mlp, dg4 = _rms_bwd(ml_hat, r4, g4, d_out)
        dmo = d_mlp.astype(BF16)
        dmo_ref[...] = dmo
        for j in range(MID_CHUNKS):
            cols_j = slice(MID_CHUNK * j, MID_CHUNK * (j + 1))
            dact = _mm_nt(dmo, wd_ref[cols_j, :])
            dup_ref[:, cols_j] = (dact * (2.0 * up_ref[:, cols_j].astype(F32))).astype(BF16)
        dhn2 = _mm_nt(dup_ref[...], wu_ref[...])
        dh_norm, dg3 = _rms_bwd(h_hat, r3, g3, dhn2)
        dh = d_out + dh_norm
        dh_ref[...] = dh
        d_mix, dg2 = _rms_bwd(mo_hat, r2, g2, dh)
        dmix = d_mix.astype(BF16)
        dmix_ref[...] = dmix
        dmixed_ref[...] = _mm_nt(dmix, wo_ref[...])
        small_ref[ROW_LOSS:ROW_LOSS + 1, :] += _colsum(err * err)
        small_ref[ROW_G2:ROW_G2 + 1, :] += _colsum(dg2)
        small_ref[ROW_G3:ROW_G3 + 1, :] += _colsum(dg3)
        small_ref[ROW_G4:ROW_G4 + 1, :] += _colsum(dg4)

    tile = lambda n: pl.BlockSpec((tm, n), lambda i: (i, 0))
    cols = lambda n: pl.BlockSpec((n, tm), lambda i: (0, i))
    gain = _full((1, D_MODEL))
    return pl.pallas_call(
        body, name="mid_fwd_bwd", grid=(t // tm,),
        in_specs=[tile(ATTN_W), tile(CONV_W), tile(D_MODEL), tile(D_MODEL), gain, gain, gain,
                  _resident((D_MODEL, D_MODEL)), _resident((D_MODEL, D_FF)), _resident((D_FF, D_MODEL))],
        out_specs=[tile(D_FF), tile(D_FF), cols(D_MODEL), tile(D_MODEL), tile(D_MODEL), tile(D_MODEL), tile(D_MODEL),
                   _full((SMALL_ROWS, D_MODEL))],
        out_shape=[jax.ShapeDtypeStruct((t, D_FF), BF16), jax.ShapeDtypeStruct((t, D_FF), BF16),
                   jax.ShapeDtypeStruct((D_MODEL, t), BF16), jax.ShapeDtypeStruct((t, D_MODEL), BF16),
                   jax.ShapeDtypeStruct((t, D_MODEL), BF16), jax.ShapeDtypeStruct((t, D_MODEL), F32),
                   jax.ShapeDtypeStruct((t, D_MODEL), F32), jax.ShapeDtypeStruct((SMALL_ROWS, D_MODEL), F32)],
        scratch_shapes=[pltpu.VMEM((tm, D_FF), BF16)],
        compiler_params=_params("arbitrary"),
    )(mattn, mconv, x, target, g2, g3, g4, w_out, w_up, w_down)


CHIP_FLIPS = ((1, 1), (1, 0), (0, 1))


def _block_order(dev):
    chip_masks = [4 * fx + 2 * fy for fx, fy in CHIP_FLIPS]
    masks = [m + 1 for m in chip_masks] + [1] + chip_masks + [0]
    return jnp.bitwise_xor(dev, jnp.asarray(masks, jnp.int32)).astype(jnp.int32)


def _other_chips(x, y, c):
    return [(1 - x if fx else x, 1 - y if fy else y, c) for fx, fy in CHIP_FLIPS]


def _dw_pair_sums(operands, order, which, name, barrier_id, ride=None):
    t = operands[-1].shape[0]
    n_far = len(CHIP_FLIPS)
    n_in = len(operands)
    n_ride = 0 if ride is None else 1
    out_chunk = D_MODEL // N_DEV
    if which == "up":
        rows, cols = D_MODEL, FF_CHUNK
        in_specs = [_resident((D_MODEL, t)), pl.BlockSpec((t, FF_CHUNK), lambda s, order_ref: (0, order_ref[s]))]
    elif which == "down":
        rows, cols = FF_CHUNK, D_MODEL
        in_specs = [pl.BlockSpec((t, FF_CHUNK), lambda s, order_ref: (0, order_ref[s])), _resident((t, D_MODEL))]
    else:
        rows, cols = out_chunk, D_MODEL
        half = pl.BlockSpec((t, out_chunk), lambda s, order_ref: (0, order_ref[s] % (N_DEV // 2)))
        in_specs = [half, half, _resident((t, D_MODEL))]

    def body(order_ref, *refs):
        own_ref, from_sib_ref, pair_ref = refs[n_in + n_ride:n_in + n_ride + 3]
        send_buf, land_buf, send_sems, recv_sems = refs[n_in + 2 * n_ride + 3:n_in + 2 * n_ride + 7]
        s_now = pl.program_id(0)
        x, y, c = _mesh_pos()
        sibling = (x, y, 1 - c)
        sems = (send_sems, recv_sems)

        @pl.when(s_now == 0)
        def _():
            _enter_with([sibling] + (_other_chips(x, y, c) if n_ride else []))

        if n_ride:
            _chip_exchange_beside(s_now == 0, s_now == N_DEV - 1, [refs[n_in]], [refs[n_in + 3 + n_ride]],
                                  refs[n_in + 2 * n_ride + 7:], enter=False)

        def hand_over(k):
            dst = land_buf.at[k] if k < n_far else from_sib_ref
            return _push(send_buf.at[k], dst, sems, k, sibling)

        if which == "out":
            ma_ref, mc_ref, b_ref = refs[:n_in]
            block = lax.cond(order_ref[s_now] < N_DEV // 2, lambda: _mm_tn(ma_ref[...], b_ref[...]),
                             lambda: _mm_tn(mc_ref[...], b_ref[...]))
        elif which == "down":
            block = _mm_tn(refs[0][...], refs[1][...])
        else:
            block = _mm(refs[0][...], refs[1][...])
        for k in range(n_far + 1):
            @pl.when(s_now == k)
            def _():
                send_buf[k] = block.astype(BF16)
                hand_over(k).start()

        for k in range(n_far):
            @pl.when(s_now == n_far + 1 + k)
            def _():
                hand_over(k).wait_recv()
                pair_ref[...] = (block + land_buf[k].astype(F32)).astype(BF16)

        @pl.when(s_now == N_DEV - 1)
        def _():
            own_ref[...] = block
            for k in range(n_far + 1):
                hand_over(k).wait_send()
            hand_over(n_far).wait_recv()

    rides = [] if ride is None else [ride]
    sems = lambda k: pltpu.SemaphoreType.DMA((k,))
    return pl.pallas_call(
        body, name=name,
        grid_spec=pltpu.PrefetchScalarGridSpec(
            num_scalar_prefetch=1, grid=(N_DEV,), in_specs=in_specs + [HBM_SPEC] * n_ride,
            out_specs=[pl.BlockSpec((rows, cols), lambda s, order_ref: (0, 0)), HBM_SPEC,
                       pl.BlockSpec((None, rows, cols), lambda s, order_ref: (jnp.clip(s - n_far - 1, 0, n_far - 1), 0, 0))]
            + [HBM_SPEC] * n_ride,
            scratch_shapes=[pltpu.VMEM((n_far + 1, rows, cols), BF16), pltpu.VMEM((n_far, rows, cols), BF16),
                            sems(n_far + 1), sems(n_far + 1)] + [sems(n_far), sems(n_far)] * n_ride),
        out_shape=[jax.ShapeDtypeStruct((rows, cols), F32), jax.ShapeDtypeStruct((rows, cols), BF16),
                   jax.ShapeDtypeStruct((n_far, rows, cols), BF16)]
        + [jax.ShapeDtypeStruct(r.shape, r.dtype) for r in rides],
        compiler_params=_params("arbitrary", barrier_id=barrier_id),
    )(order, *operands, *rides)


def _chip_exchange_beside(first, last, sums, outs, sems, enter=True):
    chips = _other_chips(*_mesh_pos())
    copies = [_push(sums[i].at[k], outs[i].at[k], sems, len(chips) * i + k, chip)
              for i in range(len(sums)) for k, chip in enumerate(chips)]

    @pl.when(first)
    def _():
        if enter:
            _enter_with(chips)
        for cp in copies:
            cp.start()

    @pl.when(last)
    def _():
        for cp in copies:
            cp.wait()


ROW_GCONV, ROW_CW0 = 1, 2


def _conv_bwd(dmixed, gates, g_conv, conv_w, tm):
    t = gates.shape[0]
    n = t // tm
    rev = lambda i: n - 1 - i

    def body(dm_ref, gates_ref, gprev_ref, gc_ref, cw_ref, dgates_ref, small_ref, carry_ref):
        i = pl.program_id(0)

        @pl.when(i == 0)
        def _():
            small_ref[...] = jnp.zeros_like(small_ref)
            carry_ref[...] = jnp.zeros_like(carry_ref)

        gates = gates_ref[...]
        gb, gcc, xin = gates[:, :CONV_W], gates[:, CONV_W:2 * CONV_W], gates[:, 2 * CONV_W:]
        u = gcc * xin
        gp = gprev_ref[...]
        uprev = jnp.where(rev(i) == 0, 0.0, gp[:, CONV_W:2 * CONV_W] * gp[:, 2 * CONV_W:])
        u1, u2 = _shift_rows_down(u, uprev, 1), _shift_rows_down(u, uprev, 2)
        w = cw_ref[...]
        c = _conv3(u, u1, u2, w)
        conv = gb * c
        rcv = _inv_rms(conv)
        c_hat = conv * rcv
        dconv, dgc = _rms_bwd(c_hat, rcv, gc_ref[...], dm_ref[...])
        dc = dconv * gb
        nxt = carry_ref[...]
        du = (w[2:3, :] * dc + w[1:2, :] * _shift_rows_up(dc, nxt, 1)) + w[0:1, :] * _shift_rows_up(dc, nxt, 2)
        carry_ref[...] = dc[0:8, :]
        dgates_ref[:, :CONV_W] = (dconv * c).astype(BF16)
        dgates_ref[:, CONV_W:2 * CONV_W] = (du * xin).astype(BF16)
        dgates_ref[:, 2 * CONV_W:] = (du * gcc).astype(BF16)
        small_ref[ROW_GCONV:ROW_GCONV + 1, :] += _colsum(dgc)
        small_ref[ROW_CW0:ROW_CW0 + 1, :] += _colsum(dc * u2)
        small_ref[ROW_CW0 + 1:ROW_CW0 + 2, :] += _colsum(dc * u1)
        small_ref[ROW_CW0 + 2:ROW_CW0 + 3, :] += _colsum(dc * u)

    tile = lambda w_: pl.BlockSpec((tm, w_), lambda i: (rev(i), 0))
    prev8 = pl.BlockSpec((8, GATES_W), lambda i: (jnp.maximum(rev(i) * (tm // 8) - 1, 0), 0))
    conv_half = pl.BlockSpec((tm, CONV_W), lambda i: (rev(i), ATTN_W // CONV_W))
    return pl.pallas_call(
        body, name="conv_bwd", grid=(n,),
        in_specs=[conv_half, tile(GATES_W), prev8, _full((1, CONV_W)), _full((3, CONV_W))],
        out_specs=[tile(GATES_W), _full((SMALL_ROWS, CONV_W))],
        out_shape=[jax.ShapeDtypeStruct((t, GATES_W), BF16), jax.ShapeDtypeStruct((SMALL_ROWS, CONV_W), F32)],
        scratch_shapes=[pltpu.VMEM((8, CONV_W), F32)],
        compiler_params=_params("arbitrary"),
    )(dmixed, gates, gates, g_conv, conv_w)


def _attn_bwd(qkv, dmixed, attn, g_attn, sinks, rope, sums):
    t = qkv.shape[0]
    n_steps = t // ATTN_STEP
    rev = lambda i: n_steps - 1 - i
    rc, rs1, rs2 = rope

    def body(sink_ref, q_ref, kp_ref, kc_ref, vp_ref, vc_ref, dm_ref, attn_ref, ga_ref, c_ref, s1_ref, s2_ref, sums_ref,
             dqkv_ref, dsink_ref, dgain_ref, arrived_ref, ck_ref, cv_ref, kacc_ref, vacc_ref, send_sems, recv_sems):
        i = pl.program_id(0)
        _chip_exchange_beside(i == 0, i == n_steps - 1, [sums_ref], [arrived_ref], (send_sems, recv_sems))

        @pl.when(i == 0)
        def _():
            dsink_ref[...] = jnp.zeros_like(dsink_ref)
            dgain_ref[...] = jnp.zeros_like(dgain_ref)
            ck_ref[...] = jnp.zeros_like(ck_ref)
            cv_ref[...] = jnp.zeros_like(cv_ref)

        kacc_ref[...] = jnp.zeros_like(kacc_ref)
        vacc_ref[...] = jnp.zeros_like(vacc_ref)
        a = attn_ref[...]
        ra = _inv_rms(a)
        dattn, dgain = _rms_bwd(a * ra, ra, ga_ref[...], dm_ref[...])
        dgain_ref[0:1, :] += _colsum(dgain)
        qt = (q_ref[...] * ATTN_SCALE).T
        dot = dattn.astype(BF16).T
        keys = jnp.concatenate([kp_ref[...], kc_ref[...]], axis=0)
        vals = jnp.concatenate([vp_ref[...], vc_ref[...]], axis=0)
        sink = [_group_sinks(sink_ref, g) for g in range(N_KV)]
        c, s1, s2 = c_ref[...], s1_ref[...], s2_ref[...]
        lane = lax.broadcasted_iota(jnp.int32, (1, 128), 1)
        dsink = jnp.zeros((1, 128), F32)
        masks = _attn_masks(rev(i) > 0)
        dq_parts = []
        for b in range(ATTN_STEP_BLOCKS):
            window = slice(BLOCK * b, BLOCK * (b + 2))
            valid = masks[b]
            dq_parts.append([])
            dk_parts, dv_parts = [], []
            for g in range(N_KV):
                gs = slice(HEAD_DIM * g, HEAD_DIM * (g + 1))
                kk, vv = keys[window, gs], vals[window, gs]
                qtg, dotg = _heads_side_by_side(qt, g, b), _heads_side_by_side(dot, g, b)
                probs, psink = _attn_probs(qtg, kk, sink[g], valid)
                dp = _mm(vv, dotg)
                delta = jnp.sum(probs * dp, axis=0, keepdims=True)
                ds = (probs * (dp - delta)).astype(BF16)
                sink_terms = psink * delta
                for hh in range(GROUP):
                    head_sum = jnp.sum(sink_terms[:, BLOCK * hh:BLOCK * (hh + 1)])
                    dsink = dsink + jnp.where(lane == GROUP * g + hh, -head_sum, 0.0)
                dq_parts[b].append(_mm_tn(kk * ATTN_SCALE, ds))
                dk_parts.append(_mm_nt(ds, qtg))
                dv_parts.append(_mm_nt(probs.astype(BF16), dotg))
            kacc_ref[window, :] += jnp.concatenate(dk_parts, axis=1)
            vacc_ref[window, :] += jnp.concatenate(dv_parts, axis=1)
        dq = _to_token_rows(dq_parts)
        for ci in range(ATTN_W // 128):
            sl = slice(128 * ci, 128 * (ci + 1))
            dqkv_ref[:, sl] = _rope_transpose(dq[:, sl], c, s1, s2).astype(BF16)
        kacc_ref[ATTN_STEP:, :] += ck_ref[...]
        vacc_ref[ATTN_STEP:, :] += cv_ref[...]
        ck_ref[...] = kacc_ref[:BLOCK, :]
        cv_ref[...] = vacc_ref[:BLOCK, :]
        dqkv_ref[:, ATTN_W:ATTN_W + KV_W] = _rope_transpose(kacc_ref[BLOCK:, :], c, s1, s2).astype(BF16)
        dqkv_ref[:, ATTN_W + KV_W:] = vacc_ref[BLOCK:, :].astype(BF16)
        dsink_ref[0:1, :] += dsink

    blk = lambda w_: pl.BlockSpec((ATTN_STEP, w_), lambda i: (rev(i), 0))
    return pl.pallas_call(
        body, name="attn_bwd", grid=(n_steps,),
        in_specs=[pl.BlockSpec(memory_space=pltpu.SMEM)] + _qkv_specs(rev)
        + [blk(ATTN_W), blk(ATTN_W), _full((1, ATTN_W)), blk(128), blk(128), blk(128), HBM_SPEC],
        out_specs=[blk(QKV_W), _full((8, 128)), _full((SMALL_ROWS, ATTN_W)), HBM_SPEC],
        out_shape=[jax.ShapeDtypeStruct((t, QKV_W), BF16), jax.ShapeDtypeStruct((8, 128), F32),
                   jax.ShapeDtypeStruct((SMALL_ROWS, ATTN_W), F32), jax.ShapeDtypeStruct(sums.shape, sums.dtype)],
        scratch_shapes=[pltpu.VMEM((BLOCK, KV_W), F32), pltpu.VMEM((BLOCK, KV_W), F32),
                        pltpu.VMEM((ATTN_KEYS, KV_W), F32), pltpu.VMEM((ATTN_KEYS, KV_W), F32),
                        pltpu.SemaphoreType.DMA((len(CHIP_FLIPS),)), pltpu.SemaphoreType.DMA((len(CHIP_FLIPS),))],
        compiler_params=_params("arbitrary", barrier_id=6),
    )(sinks, qkv, qkv, qkv, qkv, qkv, dmixed, attn, g_attn, rc, rs1, rs2, sums)


def _grad_x_tile(dq, dg, x_hat, r, g1, w_ref, dh):
    dhn = _mm_nt(dq, w_ref[:, :QKV_W]) + _mm_nt(dg, w_ref[:, QKV_W:])
    dx, dg1 = _rms_bwd(x_hat, r, g1, dhn)
    return dh + dx, _colsum(dg1)


def _in_proj_bwd(dqkv, dgates, x, dh, g1, w_in, tm, out_sums):
    t = x.shape[0]
    n = t // tm
    n_cover = max(n // 2, 1)
    n_steps = n + n_cover
    n_far = len(CHIP_FLIPS)
    shard = (D_MODEL, IN_SHARD)

    def body(dq_ref, dg_ref, x_ref, dh_ref, g1_ref, w_ref, osums_ref,
             dx_ref, own_ref, sib_ref, far_ref, dg1_ref, oarrived_ref,
             acc_ref, send_buf, land_buf, pair_buf, d2d_send, d2d_recv, ici_send, ici_recv, o_send, o_recv):
        i = pl.program_id(0)
        x_pos, y_pos, c = _mesh_pos()
        my_chip = 2 * x_pos + y_pos
        sibling = (x_pos, y_pos, 1 - c)
        @pl.when(i == 0)
        def _():
            _enter_with(_sibling_and_chips(x_pos, y_pos, c))

        _chip_exchange_beside(i == 0, i == n_steps - 1, [osums_ref], [oarrived_ref], (o_send, o_recv), enter=False)

        def cols(d):
            return slice(IN_SHARD * d, IN_SHARD * (d + 1))

        def hand_over(chip):
            return _push(send_buf.at[chip], land_buf.at[chip], (d2d_send, d2d_recv), chip, sibling)

        def to_chip(chip, rel):
            return pltpu.make_async_remote_copy(
                src_ref=pair_buf.at[chip], dst_ref=far_ref.at[rel - 1], send_sem=ici_send.at[rel - 1],
                recv_sem=ici_recv.at[rel - 1], device_id=(chip // 2, chip % 2, c), device_id_type=MESH)

        @pl.when(i == 0)
        def _():
            acc_ref[...] = jnp.zeros_like(acc_ref)
            dg1_ref[...] = jnp.zeros_like(dg1_ref)

        def normed_x():
            xv = x_ref[...]
            r = _inv_rms(xv)
            return xv * r, r

        @pl.when(i < n)
        def _():
            hn = (normed_x()[0] * g1_ref[...]).astype(BF16)
            acc_ref[:, :QKV_W] += _mm_tn(hn, dq_ref[...])
            acc_ref[:, QKV_W:] += _mm_tn(hn, dg_ref[...])

        @pl.when(i == n - 1)
        def _():
            for d in range(N_DEV):
                @pl.when(d % 2 != c)
                def _():
                    send_buf[d // 2] = acc_ref[:, cols(d)].astype(BF16)
                    hand_over(d // 2).start()
            for d in range(N_DEV):
                chip = d // 2

                @pl.when(d % 2 == c)
                def _():
                    hand_over(chip).wait_recv()

                    @pl.when(chip == my_chip)
                    def _():
                        own_ref[...] = acc_ref[:, cols(d)]
                        sib_ref[...] = land_buf[chip]

                    @pl.when(chip != my_chip)
                    def _():
                        pair_buf[chip] = (acc_ref[:, cols(d)] + land_buf[chip].astype(F32)).astype(BF16)
                        to_chip(chip, chip ^ my_chip).start()
            for chip in range(N_CHIPS):
                hand_over(chip).wait_send()

        @pl.when(i >= n)
        def _():
            x_hat, r = normed_x()
            dx_ref[...], dg1 = _grad_x_tile(dq_ref[...], dg_ref[...], x_hat, r, g1_ref[...], w_ref, dh_ref[...])
            dg1_ref[0:1, :] += dg1

        @pl.when(i == n_steps - 1)
        def _():
            for rel in range(1, n_far + 1):
                to_chip(0, rel).wait()

    both = lambda w_: pl.BlockSpec((tm, w_), lambda i: (i % n, 0))
    second = pl.BlockSpec((tm, D_MODEL), lambda i: (jnp.maximum(i - n, 0), 0))
    whole = lambda dtype: jax.ShapeDtypeStruct(shard, dtype)
    sems = lambda k: pltpu.SemaphoreType.DMA((k,))
    res = pl.pallas_call(
        body, name="in_proj_bwd", grid=(n_steps,),
        in_specs=[both(QKV_W), both(GATES_W), both(D_MODEL), second, _full((1, D_MODEL)), _resident((D_MODEL, IN_COLS)),
                  HBM_SPEC],
        out_specs=[second, _full(shard), _full(shard), HBM_SPEC, _full((SMALL_ROWS, D_MODEL)), HBM_SPEC],
        out_shape=[jax.ShapeDtypeStruct((n_cover * tm, D_MODEL), F32), whole(F32), whole(BF16),
                   jax.ShapeDtypeStruct((n_far,) + shard, BF16), jax.ShapeDtypeStruct((SMALL_ROWS, D_MODEL), F32),
                   jax.ShapeDtypeStruct(out_sums.shape, out_sums.dtype)],
        scratch_shapes=[pltpu.VMEM((D_MODEL, IN_COLS), F32), pltpu.VMEM((N_CHIPS,) + shard, BF16),
                        pltpu.VMEM((N_CHIPS,) + shard, BF16), pltpu.VMEM((N_CHIPS,) + shard, BF16),
                        sems(N_CHIPS), sems(N_CHIPS), sems(n_far), sems(n_far), sems(n_far), sems(n_far)],
        compiler_params=_params("arbitrary", barrier_id=7),
    )(dqkv, dgates, x, dh, g1, w_in, out_sums)
    return res[0], (res[1], res[2], res[3]), res[4], res[5]


def _grad_x_rest(dqkv, dgates, x, dh, g1, w_in, tm, head, dg1_rows):
    t = x.shape[0]
    first = head.shape[0] // tm
    n_rest = t // tm - first
    if n_rest == 0:
        return head, dg1_rows
    assert first <= n_rest

    def body(dq_ref, dg_ref, x_ref, dh_ref, g1_ref, w_ref, head_ref, rows_ref, gx_ref, dg1_ref, stage, sems):
        j = pl.program_id(0)

        def tile_out(step, kind):
            row0 = (step + first) * tm if kind == 0 else step * tm
            slot = 2 * kind + step % 2
            return pltpu.make_async_copy(stage.at[slot], gx_ref.at[pl.ds(pl.multiple_of(row0, tm), tm), :], sems.at[slot])

        @pl.when(j == 0)
        def _():
            dg1_ref[...] = rows_ref[...]

        @pl.when(j >= 2)
        def _():
            tile_out(j - 2, 0).wait()

        @pl.when((j >= 2) & (j - 2 < first))
        def _():
            tile_out(j - 2, 1).wait()

        @pl.when(j < first)
        def _():
            stage[2 + j % 2] = head_ref[...]
            tile_out(j, 1).start()

        xv = x_ref[...]
        r = _inv_rms(xv)
        dx, dg1 = _grad_x_tile(dq_ref[...], dg_ref[...], xv * r, r, g1_ref[...], w_ref, dh_ref[...])
        stage[j % 2] = dx
        dg1_ref[0:1, :] += dg1
        tile_out(j, 0).start()

        @pl.when(j == n_rest - 1)
        def _():
            for back in range(min(2, n_rest)):
                tile_out(j - back, 0).wait()

                @pl.when(j - back < first)
                def _():
                    tile_out(j - back, 1).wait()

    tile = lambda w_: pl.BlockSpec((tm, w_), lambda j: (j + first, 0))
    head_tile = pl.BlockSpec((tm, D_MODEL), lambda j: (jnp.minimum(j, first - 1), 0))
    return pl.pallas_call(
        body, name="grad_x_rest", grid=(n_rest,),
        in_specs=[tile(QKV_W), tile(GATES_W), tile(D_MODEL), tile(D_MODEL), _full((1, D_MODEL)),
                  _resident((D_MODEL, IN_COLS)), head_tile, _full((SMALL_ROWS, D_MODEL))],
        out_specs=[HBM_SPEC, _full((SMALL_ROWS, D_MODEL))],
        out_shape=[jax.ShapeDtypeStruct((t, D_MODEL), F32), jax.ShapeDtypeStruct((SMALL_ROWS, D_MODEL), F32)],
        scratch_shapes=[pltpu.VMEM((4, tm, D_MODEL), F32), pltpu.SemaphoreType.DMA((4,))],
        compiler_params=_params("arbitrary"),
    )(dqkv, dgates, x, dh, g1, w_in, head, dg1_rows)


def _all_gather(shards, name):
    n = len(shards)

    def body(*refs):
        _enter_with(_sibling_and_chips(*_mesh_pos()))
        start, finish = _gather_steps(refs[:n], refs[n:2 * n], *refs[2 * n:])
        start()
        finish()

    return pl.pallas_call(
        body, name=name,
        in_specs=[HBM_SPEC] * n, out_specs=[HBM_SPEC] * n,
        out_shape=[jax.ShapeDtypeStruct((N_DEV,) + s.shape, s.dtype) for s in shards],
        scratch_shapes=[pltpu.SemaphoreType.DMA((7 * n,)), pltpu.SemaphoreType.DMA((7 * n,)),
                        pltpu.SemaphoreType.DMA((n,))],
        compiler_params=_params(barrier_id=8),
    )(*shards)


def _adam_math(w, g, m, v):
    m = ADAM_B1 * m + (1.0 - ADAM_B1) * g
    v = ADAM_B2 * v + (1.0 - ADAM_B2) * (g * g)
    m_hat = m / (1.0 - ADAM_B1 ** ADAM_STEP)
    v_hat = v / (1.0 - ADAM_B2 ** ADAM_STEP)
    delta = -ADAM_LR * (m_hat / (jnp.sqrt(v_hat) + ADAM_EPS) + ADAM_WD * w)
    return delta, m, v


def _adamw_reduced(w, m, v, own, from_sibling, from_chips, tr):
    rows, cols = w.shape

    def body(w_ref, m_ref, v_ref, own_ref, sib_ref, far_ref, g_ref, d_ref, nm_ref, nv_ref):
        g = own_ref[...] + sib_ref[...].astype(F32)
        for k in range(len(CHIP_FLIPS)):
            g = g + far_ref[k].astype(F32)
        g_ref[...] = g
        d_ref[...], nm_ref[...], nv_ref[...] = _adam_math(w_ref[...], g, m_ref[...], v_ref[...])

    tile = pl.BlockSpec((tr, cols), lambda i: (i, 0))
    out = jax.ShapeDtypeStruct((rows, cols), F32)
    return pl.pallas_call(
        body, name="adamw_reduced", grid=(rows // tr,),
        in_specs=[tile] * 5 + [pl.BlockSpec((len(CHIP_FLIPS), tr, cols), lambda i: (0, i, 0))],
        out_specs=[tile] * 4, out_shape=[out] * 4,
        compiler_params=_params("parallel"),
    )(w, m, v, own, from_sibling, from_chips)


SMALL_PARAMS = ("pre_mix_norm", "post_mix_norm", "pre_mlp_norm", "post_mlp_norm", "attn_group_norm", "conv_group_norm",
                "conv_w", "attn_sinks")


def _small_tail(gathered, dev, weights, first_moments, second_moments):
    n = len(SMALL_PARAMS)
    conv_shard = CONV_W // N_DEV

    def body(dev_ref, mid_ref, conv_ref, gain_ref, sink_ref, in_ref, *refs):
        w_refs, m_refs, v_refs = refs[:n], refs[n:2 * n], refs[2 * n:3 * n]
        loss_ref, outs = refs[3 * n], refs[3 * n + 1:]

        def total(ref):
            acc = ref[0]
            for d in range(1, N_DEV):
                acc = acc + ref[d]
            return acc

        mid, conv, gain, sink, inp = total(mid_ref), total(conv_ref), total(gain_ref), total(sink_ref), total(in_ref)
        loss_ref[...] = (0.5 / D_MODEL) * jnp.sum(mid[ROW_LOSS:ROW_LOSS + 1, :], axis=1, keepdims=True)
        conv_rows = conv[ROW_CW0:ROW_CW0 + 3, :]
        conv_g = jnp.zeros((3, conv_shard), F32)
        for d in range(N_DEV):
            conv_g = conv_g + jnp.where(dev_ref[0] == d, conv_rows[:, conv_shard * d:conv_shard * (d + 1)], 0.0)
        grads = [inp[0:1, :], mid[ROW_G2:ROW_G2 + 1, :], mid[ROW_G3:ROW_G3 + 1, :], mid[ROW_G4:ROW_G4 + 1, :],
                 gain[0:1, :], conv[ROW_GCONV:ROW_GCONV + 1, :], conv_g, sink[0:1, :N_HEADS]]
        for i, g in enumerate(grads):
            delta, new_m, new_v = _adam_math(w_refs[i][...], g, m_refs[i][...], v_refs[i][...])
            outs[i][...], outs[n + i][...], outs[2 * n + i][...], outs[3 * n + i][...] = g, delta, new_m, new_v

    params = list(weights) + list(first_moments) + list(second_moments)
    shapes = [jax.ShapeDtypeStruct(w.shape, F32) for w in weights]
    res = pl.pallas_call(
        body, name="small_tail", grid=(1,),
        in_specs=[pl.BlockSpec(memory_space=pltpu.SMEM)] + [_full(g.shape) for g in gathered] + [_full(p.shape) for p in params],
        out_specs=[_full((1, 1))] + [_full(sh.shape) for sh in shapes] * 4,
        out_shape=[jax.ShapeDtypeStruct((1, 1), F32)] + shapes * 4,
    )(dev, *gathered, *params)
    return res[0], [res[1 + k * n:1 + (k + 1) * n] for k in range(4)]


TOKEN_TILE = 512
MID_TILE = 256
MID_CHUNK = 1024
MID_CHUNKS = D_FF // MID_CHUNK
ADAM_ROWS = 128


def _local_grads(x, target, g1, w_in_shard, conv_shard, sinks, g_attn, g_conv, g2, g3, g4, shards, order):
    t = x.shape[0]
    tm = min(TOKEN_TILE, t)
    rope = _rope_tables(t)
    qkv, gates, mconv, w_in, conv_w, gathered = _in_proj_fwd(x, g1, w_in_shard, conv_shard, g_conv, rope, tm, shards,
                                                             (False, True, False))
    attn, mattn, (w_out, w_up, w_down) = _attn_fwd(qkv, sinks, g_attn, shards, gathered)
    act, dup, hn2t, dmo, dmix, dh, dmixed, small_mid = _mid(
        mattn, mconv, x, target, g2, g3, g4, w_out.reshape(D_MODEL, D_MODEL),
        w_up, w_down.reshape(D_FF, D_MODEL), min(MID_TILE, t))
    up_own, up_sib, up_sums = _dw_pair_sums((hn2t, dup), order, "up", "dw_up", 2)
    down_own, down_sib, down_sums, up_far = _dw_pair_sums((act, dmo), order, "down", "dw_down", 3, ride=up_sums)
    out_own, out_sib, out_sums = _dw_pair_sums((mattn, mconv, dmix), order, "out", "dw_out", 4)
    dgates, small_conv = _conv_bwd(dmixed, gates, g_conv, conv_w, tm)
    dqkv, dsink, dg_attn, down_far = _attn_bwd(qkv, dmixed, attn, g_attn, sinks, rope, down_sums)
    grad_x_head, dw_in, small_in, out_far = _in_proj_bwd(dqkv, dgates, x, dh, g1, w_in, tm, out_sums)
    grad_x, small_in = _grad_x_rest(dqkv, dgates, x, dh, g1, w_in, tm, grad_x_head, small_in)
    dw_out, dw_up, dw_down = (out_own, out_sib, out_far), (up_own, up_sib, up_far), (down_own, down_sib, down_far)
    return grad_x, dw_in, dw_out, dw_up, dw_down, (small_mid, small_conv, dg_attn, dsink, small_in)


def kernel(x, pre_mix_norm, w_in, conv_w, attn_sinks, attn_group_norm, conv_group_norm, w_out, post_mix_norm, pre_mlp_norm, w_up, w_down, post_mlp_norm, loss_target, m_pre_mix_norm, m_w_in, m_conv_w, m_attn_sinks, m_attn_group_norm, m_conv_group_norm, m_w_out, m_post_mix_norm, m_pre_mlp_norm, m_w_up, m_w_down, m_post_mlp_norm, v_pre_mix_norm, v_w_in, v_conv_w, v_attn_sinks, v_attn_group_norm, v_conv_group_norm, v_w_out, v_post_mix_norm, v_pre_mlp_norm, v_w_up, v_w_down, v_post_mlp_norm):
    xi, yi, ci = _mesh_pos()
    chip = 2 * xi + yi
    dev = 2 * chip + ci

    order = _block_order(dev)

    shards = [w_out[0].astype(BF16), w_up[0].astype(BF16), w_down[0].astype(BF16)]

    grad_x, dw_in, dw_out, dw_up, dw_down, smalls = _local_grads(
        x[0], loss_target[0], pre_mix_norm, w_in[0].astype(BF16), conv_w[0], attn_sinks, attn_group_norm, conv_group_norm,
        post_mix_norm, pre_mlp_norm, post_mlp_norm, shards, order)

    big = {}
    for name, w, m, v, (own, sib, far) in zip(
            ("w_in", "w_out", "w_up", "w_down"), (w_in, w_out, w_up, w_down), (m_w_in, m_w_out, m_w_up, m_w_down),
            (v_w_in, v_w_out, v_w_up, v_w_down), (dw_in, dw_out, dw_up, dw_down)):
        big[name] = [a[None] for a in _adamw_reduced(w[0], m[0], v[0], own, sib, far, ADAM_ROWS)]

    flat = lambda a: a.reshape(-1, a.shape[-1])
    loss, small = _small_tail(
        _all_gather(list(smalls), "gather_small"), dev.reshape(1).astype(jnp.int32),
        [flat(a) for a in (pre_mix_norm, post_mix_norm, pre_mlp_norm, post_mlp_norm, attn_group_norm, conv_group_norm,
                           conv_w, attn_sinks)],
        [flat(a) for a in (m_pre_mix_norm, m_post_mix_norm, m_pre_mlp_norm, m_post_mlp_norm, m_attn_group_norm,
                           m_conv_group_norm, m_conv_w, m_attn_sinks)],
        [flat(a) for a in (v_pre_mix_norm, v_post_mix_norm, v_pre_mlp_norm, v_post_mlp_norm, v_attn_group_norm,
                           v_conv_group_norm, v_conv_w, v_attn_sinks)])

    order = ("pre_mix_norm", "w_in", "conv_w", "attn_sinks", "attn_group_norm", "conv_group_norm", "w_out",
             "post_mix_norm", "pre_mlp_norm", "w_up", "w_down", "post_mlp_norm")
    shape_of = {"conv_w": conv_w.shape}
    outs = []
    for k in range(4):
        by_name = dict(zip(SMALL_PARAMS, small[k]))
        outs += [big[nm][k] if nm in big else by_name[nm].reshape(shape_of.get(nm, by_name[nm].shape)) for nm in order]
    loss = loss.reshape(())
    return (loss, grad_x[None], *outs)
```

```python
import functools

import jax
import jax.numpy as jnp
import numpy as np
from jax import lax
from jax.experimental import pallas as pl
from jax.experimental.pallas import tpu as pltpu

F32 = jnp.float32
BF16 = jnp.bfloat16

D_MODEL = 1024
HEAD_DIM = 64
ATTN_W = 512
CONV_W = 512
N_HEADS = 8
N_KV = 2
GROUP = 4
KV_W = 128
QKV_W = ATTN_W + 2 * KV_W
GATES_W = 3 * CONV_W
IN_COLS = QKV_W + GATES_W
D_FF = 4096
FF_CHUNK = 512
N_FF_CHUNKS = D_FF // FF_CHUNK
BLOCK = 128
ROT_HALF = 8
ROPE_THETA = 500000.0
NORM_EPS = 1e-6
NEG_INF = -1e30
ATTN_SCALE = 0.125
N_DEV = 8
N_CHIPS = 4
IN_SHARD = IN_COLS // N_DEV

ADAM_LR = 0.001
ADAM_B1 = 0.9
ADAM_B2 = 0.999
ADAM_EPS = 1e-08
ADAM_WD = 0.01
ADAM_STEP = 10

V7X_VMEM_BYTES = 64 * 1024 * 1024
VMEM_LIMIT = V7X_VMEM_BYTES - 2 * 1024 * 1024

MESH = pl.DeviceIdType.MESH
HBM_SPEC = pl.BlockSpec(memory_space=pltpu.HBM)


def _params(*sem, barrier_id=None):
    return pltpu.CompilerParams(dimension_semantics=sem or None, vmem_limit_bytes=VMEM_LIMIT, collective_id=barrier_id)


def _mm(a, b):
    return jnp.dot(a, b, preferred_element_type=F32)


def _mm_nt(a, b):
    return lax.dot_general(a, b, (((1,), (1,)), ((), ())), preferred_element_type=F32)


def _mm_tn(a, b):
    return lax.dot_general(a, b, (((0,), (0,)), ((), ())), preferred_element_type=F32)


def _inv_rms(x):
    return lax.rsqrt(jnp.mean(x * x, axis=-1, keepdims=True) + NORM_EPS)


def _rms_bwd(xhat, r, gain, dy):
    gy = dy * gain
    return r * (gy - xhat * jnp.mean(gy * xhat, axis=-1, keepdims=True)), dy * xhat


def _colsum(a):
    return jnp.sum(a, axis=0, keepdims=True)


def _full(shape):
    zeros = (0,) * len(shape)
    return pl.BlockSpec(shape, lambda *_: zeros)


def _resident(shape):
    zeros = (0,) * len(shape)
    return pl.BlockSpec(shape, lambda *_: zeros, pipeline_mode=pl.Buffered(1))


def _rope_tables(t):
    pos = np.arange(t, dtype=np.float32)
    inv_freq = (ROPE_THETA ** (-np.arange(0, 2 * ROT_HALF, 2, dtype=np.float64) / (2 * ROT_HALF))).astype(np.float32)
    ang = (pos[:, None] * inv_freq[None, :]).astype(np.float64)
    cos, sin = np.cos(ang).astype(np.float32), np.sin(ang).astype(np.float32)
    zeros8 = np.zeros((t, ROT_HALF), np.float32)
    rest = np.zeros((t, HEAD_DIM - 2 * ROT_HALF), np.float32)
    c_head = np.concatenate([cos, cos, rest + 1.0], axis=1)
    s1_head = np.concatenate([zeros8, sin, rest], axis=1)
    s2_head = np.concatenate([-sin, zeros8, rest], axis=1)
    two = lambda a: jnp.asarray(np.concatenate([a, a], axis=1))
    return two(c_head), two(s1_head), two(s2_head)


def _rope(v, c, s1, s2):
    return v * c + pltpu.roll(v, ROT_HALF, 1) * s1 + pltpu.roll(v, 128 - ROT_HALF, 1) * s2


def _rope_transpose(dv, c, s1, s2):
    return dv * c + pltpu.roll(dv * s1, 128 - ROT_HALF, 1) + pltpu.roll(dv * s2, ROT_HALF, 1)


def _shift_rows_down(u, prev, k):
    row = lax.broadcasted_iota(jnp.int32, u.shape, 0)
    out = pltpu.roll(u, k, 0)
    for r in range(k):
        out = jnp.where(row == r, prev[8 - k + r:8 - k + r + 1, :], out)
    return out


def _shift_rows_up(u, nxt, k):
    n = u.shape[0]
    row = lax.broadcasted_iota(jnp.int32, u.shape, 0)
    out = pltpu.roll(u, n - k, 0)
    for r in range(k):
        out = jnp.where(row == n - k + r, nxt[r:r + 1, :], out)
    return out


def _conv3(u, u1, u2, w):
    return (w[0:1, :] * u2 + w[1:2, :] * u1) + w[2:3, :] * u


def _mesh_pos():
    return lax.axis_index("x"), lax.axis_index("y"), lax.axis_index("c")


def _slot(ref, pos):
    dev = 4 * pos[0] + 2 * pos[1] + pos[2]
    if len(ref.shape) == 2:
        width = ref.shape[1] // N_DEV
        return ref.at[:, pl.ds(pl.multiple_of(dev * width, width), width)]
    return ref.at[dev]


def _gathered_shape(shard, by_cols):
    if by_cols:
        return jax.ShapeDtypeStruct((shard.shape[0], N_DEV * shard.shape[1]), shard.dtype)
    return jax.ShapeDtypeStruct((N_DEV,) + shard.shape, shard.dtype)


def _enter_with(peers):
    barrier = pltpu.get_barrier_semaphore()
    for peer in peers:
        pl.semaphore_signal(barrier, inc=1, device_id=peer, device_id_type=MESH)
    pl.semaphore_wait(barrier, len(peers))


def _sibling_and_chips(x, y, c):
    return [(x, y, 1 - c), (1 - x, y, c), (x, 1 - y, c), (1 - x, 1 - y, c)]


def _push(src, dst, sems, k, to):
    send_sems, recv_sems = sems
    return pltpu.make_async_remote_copy(src_ref=src, dst_ref=dst, send_sem=send_sems.at[k], recv_sem=recv_sems.at[k],
                                        device_id=to, device_id_type=MESH)


def _gather_steps(shards, outs, send_sems, recv_sems, local_sems):
    n = len(shards)
    x, y, c = _mesh_pos()
    me, sibling = (x, y, c), (x, y, 1 - c)
    chips = [(1 - x, y), (x, 1 - y), (1 - x, 1 - y)]

    def copy(i, k, block, to, src=None):
        dst = _slot(outs[i], block)
        return _push(dst if src is None else src, dst, (send_sems, recv_sems), 7 * i + k, to)

    mine = [pltpu.make_async_copy(shards[i], _slot(outs[i], me), local_sems.at[i]) for i in range(n)]
    first = []
    for i in range(n):
        first.append(copy(i, 0, me, sibling, src=shards[i]))
        first += [copy(i, 1 + j, me, (*chip, c), src=shards[i]) for j, chip in enumerate(chips)]

    def start():
        for cp in mine + first:
            cp.start()

    def finish():
        passed = []
        for j, chip in enumerate(chips):
            for i in range(n):
                copy(i, 1 + j, (*chip, c), me).wait_recv()
                cp = copy(i, 4 + j, (*chip, c), sibling)
                cp.start()
                passed.append(cp)
        for i in range(n):
            copy(i, 0, sibling, me).wait_recv()
            for j, chip in enumerate(chips):
                copy(i, 4 + j, (*chip, 1 - c), me).wait_recv()
        for cp in first + passed:
            cp.wait_send()
        for cp in mine:
            cp.wait()

    return start, finish


def _gather_near(first, last, shards, outs, sems, local_sems):
    x, y, c = _mesh_pos()
    me, peers = (x, y, c), [(x, y, 1 - c), (1 - x, y, c), (x, 1 - y, c)]
    n = len(shards)
    local = [pltpu.make_async_copy(shards[i], _slot(outs[i], me), local_sems.at[i]) for i in range(n)]
    sends = [_push(shards[i], _slot(outs[i], me), sems, 3 * i + k, peers[k]) for i in range(n) for k in range(3)]
    arrivals = [_push(shards[i], _slot(outs[i], peers[k]), sems, 3 * i + k, peers[k]) for i in range(n) for k in range(3)]

    def start():
        for cp in local + sends:
            cp.start()

    if first is not None:
        pl.when(first)(start)

    @pl.when(last)
    def _():
        for cp in sends:
            cp.wait_send()
        for cp in arrivals:
            cp.wait_recv()
        for cp in local:
            cp.wait()

    return start


def _relay_route(x, y, c):
    south = c == 0
    via = (jnp.where(south, 1 - x, x), jnp.where(south, y, 1 - y))
    to = (jnp.where(south, x, 1 - x), jnp.where(south, 1 - y, y))
    return via, to


def _gather_far(first, last, shards, ins, outs, sems):
    x, y, c = _mesh_pos()
    sibling = (x, y, 1 - c)
    chips = [(1 - x, y), (x, 1 - y), (1 - x, 1 - y)]
    via, to = _relay_route(x, y, c)
    n = len(shards)
    diag_send = [_push(_slot(ins[i], (*via, c)), _slot(outs[i], (*via, c)), sems, 4 * i, (*to, c)) for i in range(n)]
    diag_arrival = [_push(shards[i], _slot(outs[i], (*chips[2], c)), sems, 4 * i, (*to, c)) for i in range(n)]
    passed = [[_push(_slot(ins[i], (*chips[j], c)), _slot(outs[i], (*chips[j], c)), sems, 4 * i + 1 + j, sibling)
               for i in range(n)] for j in range(3)]
    from_sibling = [_push(shards[i], _slot(outs[i], (*chips[j], 1 - c)), sems, 4 * i + 1 + j, sibling)
                    for i in range(n) for j in range(3)]

    @pl.when(first)
    def _():
        for cp in diag_send + passed[0] + passed[1]:
            cp.start()

    @pl.when(last)
    def _():
        for cp in diag_arrival:
            cp.wait_recv()
        for cp in passed[2]:
            cp.start()
        for cp in from_sibling:
            cp.wait_recv()
        for cp in diag_send + passed[0] + passed[1] + passed[2]:
            cp.wait_send()


def _in_proj_fwd(x, g1, w_in, conv_w, g_conv, rope, tm, shards, by_cols):
    t = x.shape[0]
    rc, rs1, rs2 = rope
    n = len(shards)
    n_tiles = t // tm

    def body(*refs):
        x_ref, g1_ref, w_ref, cw_ref, gc_ref, c_ref, s1_ref, s2_ref = refs[:8]
        shard_refs = refs[8:8 + n]
        qkv_ref, gates_ref, mconv_ref, w_full_ref, cw_full_ref = refs[8 + n:13 + n]
        gathered = refs[13 + n:13 + 2 * n]
        carry_ref, w_land, cw_land, hn_ref = refs[13 + 2 * n:17 + 2 * n]
        now_sems = refs[17 + 2 * n:20 + 2 * n]
        step = pl.program_id(0)
        start_later_weights = _gather_near(None, step == 2 * n_tiles - 1, shard_refs, gathered,
                                           refs[20 + 2 * n:22 + 2 * n], refs[22 + 2 * n]) if n else None
        start_w_in, finish_w_in = _gather_steps([w_ref, cw_ref], [w_land, cw_land], *now_sems)

        @pl.when(step == 0)
        def _():
            carry_ref[...] = jnp.zeros_like(carry_ref)
            _enter_with(_sibling_and_chips(*_mesh_pos()))
            start_w_in()
            if start_later_weights is not None:
                start_later_weights()

        @pl.when(step < n_tiles)
        def _():
            xv = x_ref[...]
            hn_ref[step] = ((xv * _inv_rms(xv)) * g1_ref[...]).astype(BF16)

        @pl.when(step == n_tiles)
        def _():
            finish_w_in()
            conv_shard = CONV_W // N_DEV
            for d in range(N_DEV):
                w_full_ref[:, IN_SHARD * d:IN_SHARD * (d + 1)] = w_land[d]
                cw_full_ref[:, conv_shard * d:conv_shard * (d + 1)] = cw_land[d]

        @pl.when(step >= n_tiles)
        def _():
            proj = _mm(hn_ref[step - n_tiles], w_full_ref[...])
            c, s1, s2 = c_ref[...], s1_ref[...], s2_ref[...]
            for ci in range((ATTN_W + KV_W) // 128):
                sl = slice(128 * ci, 128 * (ci + 1))
                qkv_ref[:, sl] = _rope(proj[:, sl], c, s1, s2).astype(BF16)
            qkv_ref[:, ATTN_W + KV_W:QKV_W] = proj[:, ATTN_W + KV_W:QKV_W].astype(BF16)
            gates = proj[:, QKV_W:]
            gates_ref[...] = gates
            gb, gcc, xin = gates[:, :CONV_W], gates[:, CONV_W:2 * CONV_W], gates[:, 2 * CONV_W:]
            u = gcc * xin
            prev = carry_ref[...]
            conv = gb * _conv3(u, _shift_rows_down(u, prev, 1), _shift_rows_down(u, prev, 2), cw_full_ref[...])
            carry_ref[...] = u[tm - 8:tm, :]
            mconv_ref[...] = ((conv * _inv_rms(conv)) * gc_ref[...]).astype(BF16)

    first_pass = pl.BlockSpec((tm, D_MODEL), lambda i: (jnp.minimum(i, n_tiles - 1), 0))
    tile = lambda w_: pl.BlockSpec((tm, w_), lambda i: (jnp.maximum(i - n_tiles, 0), 0))
    sems = lambda k: pltpu.SemaphoreType.DMA((k,))
    res = pl.pallas_call(
        body, name="in_proj_fwd", grid=(2 * n_tiles,),
        in_specs=[first_pass, _full((1, D_MODEL)), HBM_SPEC, HBM_SPEC, _full((1, CONV_W)), tile(128), tile(128),
                  tile(128)] + [HBM_SPEC] * n,
        out_specs=[tile(QKV_W), tile(GATES_W), tile(CONV_W), _full((D_MODEL, IN_COLS)), _full((3, CONV_W))]
        + [HBM_SPEC] * n,
        out_shape=[jax.ShapeDtypeStruct((t, QKV_W), BF16), jax.ShapeDtypeStruct((t, GATES_W), F32),
                   jax.ShapeDtypeStruct((t, CONV_W), BF16), jax.ShapeDtypeStruct((D_MODEL, IN_COLS), BF16),
                   jax.ShapeDtypeStruct((3, CONV_W), F32)]
        + [_gathered_shape(s, cols) for s, cols in zip(shards, by_cols)],
        scratch_shapes=[pltpu.VMEM((8, CONV_W), F32), pltpu.VMEM((N_DEV,) + w_in.shape, BF16),
                        pltpu.VMEM((N_DEV,) + conv_w.shape, F32), pltpu.VMEM((n_tiles, tm, D_MODEL), BF16),
                        sems(14), sems(14), sems(2)]
        + ([sems(3 * n), sems(3 * n), sems(n)] if n else []),
        compiler_params=_params("arbitrary", barrier_id=0),
    )(x, g1, w_in, conv_w, g_conv, rc, rs1, rs2, *shards)
    return res[0], res[1], res[2], res[3], res[4], list(res[5:])


GROUP_COLS = GROUP * BLOCK
ATTN_STEP_BLOCKS = 4


def _attn_masks(has_prev):
    key = lax.broadcasted_iota(jnp.int32, (2 * BLOCK, GROUP_COLS), 0)
    query = lax.broadcasted_iota(jnp.int32, (2 * BLOCK, GROUP_COLS), 1) & (BLOCK - 1)
    band = (key > query) & (key <= query + BLOCK)
    return [band & ((key >= BLOCK) | has_prev)] + [band] * (ATTN_STEP_BLOCKS - 1)


def _heads_side_by_side(at, g, b):
    heads = [at[HEAD_DIM * (GROUP * g + hh):HEAD_DIM * (GROUP * g + hh + 1), BLOCK * b:BLOCK * (b + 1)] for hh in range(GROUP)]
    return jnp.concatenate(heads, axis=1)


def _to_token_rows(parts):
    rows = [jnp.concatenate([parts[b][g][:, BLOCK * hh:BLOCK * (hh + 1)] for b in range(ATTN_STEP_BLOCKS)], axis=1)
            for g in range(N_KV) for hh in range(GROUP)]
    return jnp.concatenate(rows, axis=0).T


def _group_sinks(sink_ref, g):
    head = lax.broadcasted_iota(jnp.int32, (1, GROUP_COLS), 1) // BLOCK
    out = jnp.full((1, GROUP_COLS), sink_ref[0, GROUP * g], F32)
    for hh in range(1, GROUP):
        out = jnp.where(head == hh, sink_ref[0, GROUP * g + hh], out)
    return out


def _attn_probs(qt, kk, sink, valid):
    s = jnp.where(valid, _mm(kk, qt), NEG_INF)
    m = jnp.maximum(jnp.max(s, axis=0, keepdims=True), sink)
    p = jnp.exp(s - m)
    psink = jnp.exp(sink - m)
    inv_l = 1.0 / (jnp.sum(p, axis=0, keepdims=True) + psink)
    return p * inv_l, psink * inv_l


ATTN_STEP = ATTN_STEP_BLOCKS * BLOCK
ATTN_KEYS = ATTN_STEP + BLOCK


def _qkv_specs(order):
    prev = lambda i: jnp.maximum(ATTN_STEP_BLOCKS * order(i) - 1, 0)
    kcol, vcol = ATTN_W // KV_W, ATTN_W // KV_W + 1
    return [pl.BlockSpec((ATTN_STEP, ATTN_W), lambda i: (order(i), 0)),
            pl.BlockSpec((BLOCK, KV_W), lambda i: (prev(i), kcol)), pl.BlockSpec((ATTN_STEP, KV_W), lambda i: (order(i), kcol)),
            pl.BlockSpec((BLOCK, KV_W), lambda i: (prev(i), vcol)), pl.BlockSpec((ATTN_STEP, KV_W), lambda i: (order(i), vcol))]


def _attn_fwd(qkv, sinks, g_attn, shards, gathered):
    t = qkv.shape[0]
    n = len(shards)

    def body(*refs):
        sink_ref, q_ref, kp_ref, kc_ref, vp_ref, vc_ref, ga_ref = refs[:7]
        attn_ref, mattn_ref = refs[7 + 2 * n:9 + 2 * n]
        step = pl.program_id(0)
        if n:
            @pl.when(step == 0)
            def _():
                x, y, c = _mesh_pos()
                _enter_with([(x, y, 1 - c), (*_relay_route(x, y, c)[1], c)])

            _gather_far(step == 0, step == pl.num_programs(0) - 1, refs[7:7 + n], refs[7 + n:7 + 2 * n],
                        refs[9 + 2 * n:9 + 3 * n], refs[9 + 3 * n:11 + 3 * n])
        qt = (q_ref[...] * ATTN_SCALE).T
        keys = jnp.concatenate([kp_ref[...], kc_ref[...]], axis=0)
        vals = jnp.concatenate([vp_ref[...], vc_ref[...]], axis=0)
        sink = [_group_sinks(sink_ref, g) for g in range(N_KV)]
        masks = _attn_masks(step > 0)
        parts = []
        for b in range(ATTN_STEP_BLOCKS):
            window = slice(BLOCK * b, BLOCK * (b + 2))
            valid = masks[b]
            parts.append([])
            for g in range(N_KV):
                gs = slice(HEAD_DIM * g, HEAD_DIM * (g + 1))
                probs, _ = _attn_probs(_heads_side_by_side(qt, g, b), keys[window, gs], sink[g], valid)
                parts[b].append(_mm_tn(vals[window, gs], probs.astype(BF16)))
        attn = _to_token_rows(parts)
        attn_ref[...] = attn
        mattn_ref[...] = ((attn * _inv_rms(attn)) * ga_ref[...]).astype(BF16)

    blk = pl.BlockSpec((ATTN_STEP, ATTN_W), lambda j: (j, 0))
    res = pl.pallas_call(
        body, name="attn_fwd", grid=(t // ATTN_STEP,),
        in_specs=[pl.BlockSpec(memory_space=pltpu.SMEM)] + _qkv_specs(lambda j: j) + [_full((1, ATTN_W))]
        + [HBM_SPEC] * (2 * n),
        out_specs=[blk, blk] + [HBM_SPEC] * n,
        out_shape=[jax.ShapeDtypeStruct((t, ATTN_W), F32), jax.ShapeDtypeStruct((t, ATTN_W), BF16)]
        + [jax.ShapeDtypeStruct(g.shape, g.dtype) for g in gathered],
        input_output_aliases={7 + n + i: 2 + i for i in range(n)},
        scratch_shapes=[pltpu.SemaphoreType.DMA((4 * n,)), pltpu.SemaphoreType.DMA((4 * n,))] if n else [],
        compiler_params=_params("arbitrary", barrier_id=1 if n else None),
    )(sinks, qkv, qkv, qkv, qkv, qkv, g_attn, *shards, *gathered)
    return res[0], res[1], list(res[2:])


SMALL_ROWS = 8
ROW_LOSS, ROW_G2, ROW_G3, ROW_G4 = 0, 1, 2, 3


def _mid(mattn, mconv, x, target, g2, g3, g4, w_out, w_up, w_down, tm):
    t = x.shape[0]

    def body(ma_ref, mc_ref, x_ref, t_ref, g2_ref, g3_ref, g4_ref, wo_ref, wu_ref, wd_ref,
             act_ref, dup_ref, hn2t_ref, dmo_ref, dmix_ref, dh_ref, dmixed_ref, small_ref, up_ref):
        @pl.when(pl.program_id(0) == 0)
        def _():
            small_ref[...] = jnp.zeros_like(small_ref)

        g2, g3, g4 = g2_ref[...], g3_ref[...], g4_ref[...]
        mix_out = _mm(ma_ref[...], wo_ref[0:ATTN_W, :]) + _mm(mc_ref[...], wo_ref[ATTN_W:, :])
        r2 = _inv_rms(mix_out)
        mo_hat = mix_out * r2
        h = x_ref[...] + mo_hat * g2
        r3 = _inv_rms(h)
        h_hat = h * r3
        hn2 = (h_hat * g3).astype(BF16)
        hn2t_ref[...] = hn2.T
        for j in range(MID_CHUNKS):
            cols_j = slice(MID_CHUNK * j, MID_CHUNK * (j + 1))
            up = jnp.maximum(_mm(hn2, wu_ref[:, cols_j]), 0.0)
            up_ref[:, cols_j] = up.astype(BF16)
            act_ref[:, cols_j] = (up * up).astype(BF16)
        mlp = _mm(act_ref[...], wd_ref[...])
        r4 = _inv_rms(mlp)
        ml_hat = mlp * r4
        err = (h + ml_hat * g4) - t_ref[...]
        d_out = err * (1.0 / D_MODEL)
        d_mlp, dg4 = _rms_bwd(ml_hat, r4, g4, d_out)
        dmo = d_mlp.astype(BF16)
        dmo_ref[...] = dmo
        for j in range(MID_CHUNKS):
            cols_j = slice(MID_CHUNK * j, MID_CHUNK * (j + 1))
            dact = _mm_nt(dmo, wd_ref[cols_j, :])
            dup_ref[:, cols_j] = (dact * (2.0 * up_ref[:, cols_j].astype(F32))).astype(BF16)
        dhn2 = _mm_nt(dup_ref[...], wu_ref[...])
        dh_norm, dg3 = _rms_bwd(h_hat, r3, g3, dhn2)
        dh = d_out + dh_norm
        dh_ref[...] = dh
        d_mix, dg2 = _rms_bwd(mo_hat, r2, g2, dh)
        dmix = d_mix.astype(BF16)
        dmix_ref[...] = dmix
        dmixed_ref[...] = _mm_nt(dmix, wo_ref[...])
        small_ref[ROW_LOSS:ROW_LOSS + 1, :] += _colsum(err * err)
        small_ref[ROW_G2:ROW_G2 + 1, :] += _colsum(dg2)
        small_ref[ROW_G3:ROW_G3 + 1, :] += _colsum(dg3)
        small_ref[ROW_G4:ROW_G4 + 1, :] += _colsum(dg4)

    tile = lambda n: pl.BlockSpec((tm, n), lambda i: (i, 0))
    cols = lambda n: pl.BlockSpec((n, tm), lambda i: (0, i))
    gain = _full((1, D_MODEL))
    return pl.pallas_call(
        body, name="mid_fwd_bwd", grid=(t // tm,),
        in_specs=[tile(ATTN_W), tile(CONV_W), tile(D_MODEL), tile(D_MODEL), gain, gain, gain,
                  _resident((D_MODEL, D_MODEL)), _resident((D_MODEL, D_FF)), _resident((D_FF, D_MODEL))],
        out_specs=[tile(D_FF), tile(D_FF), cols(D_MODEL), tile(D_MODEL), tile(D_MODEL), tile(D_MODEL), tile(D_MODEL),
                   _full((SMALL_ROWS, D_MODEL))],
        out_shape=[jax.ShapeDtypeStruct((t, D_FF), BF16), jax.ShapeDtypeStruct((t, D_FF), BF16),
                   jax.ShapeDtypeStruct((D_MODEL, t), BF16), jax.ShapeDtypeStruct((t, D_MODEL), BF16),
                   jax.ShapeDtypeStruct((t, D_MODEL), BF16), jax.ShapeDtypeStruct((t, D_MODEL), F32),
                   jax.ShapeDtypeStruct((t, D_MODEL), F32), jax.ShapeDtypeStruct((SMALL_ROWS, D_MODEL), F32)],
        scratch_shapes=[pltpu.VMEM((tm, D_FF), BF16)],
        compiler_params=_params("arbitrary"),
    )(mattn, mconv, x, target, g2, g3, g4, w_out, w_up, w_down)


CHIP_FLIPS = ((1, 1), (1, 0), (0, 1))


def _block_order(dev):
    chip_masks = [4 * fx + 2 * fy for fx, fy in CHIP_FLIPS]
    masks = [m + 1 for m in chip_masks] + [1] + chip_masks + [0]
    return jnp.bitwise_xor(dev, jnp.asarray(masks, jnp.int32)).astype(jnp.int32)


def _other_chips(x, y, c):
    return [(1 - x if fx else x, 1 - y if fy else y, c) for fx, fy in CHIP_FLIPS]


def _dw_pair_sums(operands, order, which, name, barrier_id, ride=None):
    t = operands[-1].shape[0]
    n_far = len(CHIP_FLIPS)
    n_in = len(operands)
    n_ride = 0 if ride is None else 1
    out_chunk = D_MODEL // N_DEV
    if which == "up":
        rows, cols = D_MODEL, FF_CHUNK
        in_specs = [_resident((D_MODEL, t)), pl.BlockSpec((t, FF_CHUNK), lambda s, order_ref: (0, order_ref[s]))]
    elif which == "down":
        rows, cols = FF_CHUNK, D_MODEL
        in_specs = [pl.BlockSpec((t, FF_CHUNK), lambda s, order_ref: (0, order_ref[s])), _resident((t, D_MODEL))]
    else:
        rows, cols = out_chunk, D_MODEL
        half = pl.BlockSpec((t, out_chunk), lambda s, order_ref: (0, order_ref[s] % (N_DEV // 2)))
        in_specs = [half, half, _resident((t, D_MODEL))]

    def body(order_ref, *refs):
        own_ref, from_sib_ref, pair_ref = refs[n_in + n_ride:n_in + n_ride + 3]
        send_buf, land_buf, send_sems, recv_sems = refs[n_in + 2 * n_ride + 3:n_in + 2 * n_ride + 7]
        s_now = pl.program_id(0)
        x, y, c = _mesh_pos()
        sibling = (x, y, 1 - c)
        sems = (send_sems, recv_sems)

        @pl.when(s_now == 0)
        def _():
            _enter_with([sibling] + (_other_chips(x, y, c) if n_ride else []))

        if n_ride:
            _chip_exchange_beside(s_now == 0, s_now == N_DEV - 1, [refs[n_in]], [refs[n_in + 3 + n_ride]],
                                  refs[n_in + 2 * n_ride + 7:], enter=False)

        def hand_over(k):
            dst = land_buf.at[k] if k < n_far else from_sib_ref
            return _push(send_buf.at[k], dst, sems, k, sibling)

        if which == "out":
            ma_ref, mc_ref, b_ref = refs[:n_in]
            block = lax.cond(order_ref[s_now] < N_DEV // 2, lambda: _mm_tn(ma_ref[...], b_ref[...]),
                             lambda: _mm_tn(mc_ref[...], b_ref[...]))
        elif which == "down":
            block = _mm_tn(refs[0][...], refs[1][...])
        else:
            block = _mm(refs[0][...], refs[1][...])
        for k in range(n_far + 1):
            @pl.when(s_now == k)
            def _():
                send_buf[k] = block.astype(BF16)
                hand_over(k).start()

        for k in range(n_far):
            @pl.when(s_now == n_far + 1 + k)
            def _():
                hand_over(k).wait_recv()
                pair_ref[...] = (block + land_buf[k].astype(F32)).astype(BF16)

        @pl.when(s_now == N_DEV - 1)
        def _():
            own_ref[...] = block
            for k in range(n_far + 1):
                hand_over(k).wait_send()
            hand_over(n_far).wait_recv()

    rides = [] if ride is None else [ride]
    sems = lambda k: pltpu.SemaphoreType.DMA((k,))
    return pl.pallas_call(
        body, name=name,
        grid_spec=pltpu.PrefetchScalarGridSpec(
            num_scalar_prefetch=1, grid=(N_DEV,), in_specs=in_specs + [HBM_SPEC] * n_ride,
            out_specs=[pl.BlockSpec((rows, cols), lambda s, order_ref: (0, 0)), HBM_SPEC,
                       pl.BlockSpec((None, rows, cols), lambda s, order_ref: (jnp.clip(s - n_far - 1, 0, n_far - 1), 0, 0))]
            + [HBM_SPEC] * n_ride,
            scratch_shapes=[pltpu.VMEM((n_far + 1, rows, cols), BF16), pltpu.VMEM((n_far, rows, cols), BF16),
                            sems(n_far + 1), sems(n_far + 1)] + [sems(n_far), sems(n_far)] * n_ride),
        out_shape=[jax.ShapeDtypeStruct((rows, cols), F32), jax.ShapeDtypeStruct((rows, cols), BF16),
                   jax.ShapeDtypeStruct((n_far, rows, cols), BF16)]
        + [jax.ShapeDtypeStruct(r.shape, r.dtype) for r in rides],
        compiler_params=_params("arbitrary", barrier_id=barrier_id),
    )(order, *operands, *rides)


def _chip_exchange_beside(first, last, sums, outs, sems, enter=True):
    chips = _other_chips(*_mesh_pos())
    copies = [_push(sums[i].at[k], outs[i].at[k], sems, len(chips) * i + k, chip)
              for i in range(len(sums)) for k, chip in enumerate(chips)]

    @pl.when(first)
    def _():
        if enter:
            _enter_with(chips)
        for cp in copies:
            cp.start()

    @pl.when(last)
    def _():
        for cp in copies:
            cp.wait()


ROW_GCONV, ROW_CW0 = 1, 2


def _conv_bwd(dmixed, gates, g_conv, conv_w, tm):
    t = gates.shape[0]
    n = t // tm
    rev = lambda i: n - 1 - i

    def body(dm_ref, gates_ref, gprev_ref, gc_ref, cw_ref, dgates_ref, small_ref, carry_ref):
        i = pl.program_id(0)

        @pl.when(i == 0)
        def _():
            small_ref[...] = jnp.zeros_like(small_ref)
            carry_ref[...] = jnp.zeros_like(carry_ref)

        gates = gates_ref[...]
        gb, gcc, xin = gates[:, :CONV_W], gates[:, CONV_W:2 * CONV_W], gates[:, 2 * CONV_W:]
        u = gcc * xin
        gp = gprev_ref[...]
        uprev = jnp.where(rev(i) == 0, 0.0, gp[:, CONV_W:2 * CONV_W] * gp[:, 2 * CONV_W:])
        u1, u2 = _shift_rows_down(u, uprev, 1), _shift_rows_down(u, uprev, 2)
        w = cw_ref[...]
        c = _conv3(u, u1, u2, w)
        conv = gb * c
        rcv = _inv_rms(conv)
        c_hat = conv * rcv
        dconv, dgc = _rms_bwd(c_hat, rcv, gc_ref[...], dm_ref[...])
        dc = dconv * gb
        nxt = carry_ref[...]
        du = (w[2:3, :] * dc + w[1:2, :] * _shift_rows_up(dc, nxt, 1)) + w[0:1, :] * _shift_rows_up(dc, nxt, 2)
        carry_ref[...] = dc[0:8, :]
        dgates_ref[:, :CONV_W] = (dconv * c).astype(BF16)
        dgates_ref[:, CONV_W:2 * CONV_W] = (du * xin).astype(BF16)
        dgates_ref[:, 2 * CONV_W:] = (du * gcc).astype(BF16)
        small_ref[ROW_GCONV:ROW_GCONV + 1, :] += _colsum(dgc)
        small_ref[ROW_CW0:ROW_CW0 + 1, :] += _colsum(dc * u2)
        small_ref[ROW_CW0 + 1:ROW_CW0 + 2, :] += _colsum(dc * u1)
        small_ref[ROW_CW0 + 2:ROW_CW0 + 3, :] += _colsum(dc * u)

    tile = lambda w_: pl.BlockSpec((tm, w_), lambda i: (rev(i), 0))
    prev8 = pl.BlockSpec((8, GATES_W), lambda i: (jnp.maximum(rev(i) * (tm // 8) - 1, 0), 0))
    conv_half = pl.BlockSpec((tm, CONV_W), lambda i: (rev(i), ATTN_W // CONV_W))
    return pl.pallas_call(
        body, name="conv_bwd", grid=(n,),
        in_specs=[conv_half, tile(GATES_W), prev8, _full((1, CONV_W)), _full((3, CONV_W))],
        out_specs=[tile(GATES_W), _full((SMALL_ROWS, CONV_W))],
        out_shape=[jax.ShapeDtypeStruct((t, GATES_W), BF16), jax.ShapeDtypeStruct((SMALL_ROWS, CONV_W), F32)],
        scratch_shapes=[pltpu.VMEM((8, CONV_W), F32)],
        compiler_params=_params("arbitrary"),
    )(dmixed, gates, gates, g_conv, conv_w)


def _attn_bwd(qkv, dmixed, attn, g_attn, sinks, rope, sums):
    t = qkv.shape[0]
    n_steps = t // ATTN_STEP
    rev = lambda i: n_steps - 1 - i
    rc, rs1, rs2 = rope

    def body(sink_ref, q_ref, kp_ref, kc_ref, vp_ref, vc_ref, dm_ref, attn_ref, ga_ref, c_ref, s1_ref, s2_ref, sums_ref,
             dqkv_ref, dsink_ref, dgain_ref, arrived_ref, ck_ref, cv_ref, kacc_ref, vacc_ref, send_sems, recv_sems):
        i = pl.program_id(0)
        _chip_exchange_beside(i == 0, i == n_steps - 1, [sums_ref], [arrived_ref], (send_sems, recv_sems))

        @pl.when(i == 0)
        def _():
            dsink_ref[...] = jnp.zeros_like(dsink_ref)
            dgain_ref[...] = jnp.zeros_like(dgain_ref)
            ck_ref[...] = jnp.zeros_like(ck_ref)
            cv_ref[...] = jnp.zeros_like(cv_ref)

        kacc_ref[...] = jnp.zeros_like(kacc_ref)
        vacc_ref[...] = jnp.zeros_like(vacc_ref)
        a = attn_ref[...]
        ra = _inv_rms(a)
        dattn, dgain = _rms_bwd(a * ra, ra, ga_ref[...], dm_ref[...])
        dgain_ref[0:1, :] += _colsum(dgain)
        qt = (q_ref[...] * ATTN_SCALE).T
        dot = dattn.astype(BF16).T
        keys = jnp.concatenate([kp_ref[...], kc_ref[...]], axis=0)
        vals = jnp.concatenate([vp_ref[...], vc_ref[...]], axis=0)
        sink = [_group_sinks(sink_ref, g) for g in range(N_KV)]
        c, s1, s2 = c_ref[...], s1_ref[...], s2_ref[...]
        lane = lax.broadcasted_iota(jnp.int32, (1, 128), 1)
        dsink = jnp.zeros((1, 128), F32)
        masks = _attn_masks(rev(i) > 0)
        dq_parts = []
        for b in range(ATTN_STEP_BLOCKS):
            window = slice(BLOCK * b, BLOCK * (b + 2))
            valid = masks[b]
            dq_parts.append([])
            dk_parts, dv_parts = [], []
            for g in range(N_KV):
                gs = slice(HEAD_DIM * g, HEAD_DIM * (g + 1))
                kk, vv = keys[window, gs], vals[window, gs]
                qtg, dotg = _heads_side_by_side(qt, g, b), _heads_side_by_side(dot, g, b)
                probs, psink = _attn_probs(qtg, kk, sink[g], valid)
                dp = _mm(vv, dotg)
                delta = jnp.sum(probs * dp, axis=0, keepdims=True)
                ds = (probs * (dp - delta)).astype(BF16)
                sink_terms = psink * delta
                for hh in range(GROUP):
                    head_sum = jnp.sum(sink_terms[:, BLOCK * hh:BLOCK * (hh + 1)])
                    dsink = dsink + jnp.where(lane == GROUP * g + hh, -head_sum, 0.0)
                dq_parts[b].append(_mm_tn(kk * ATTN_SCALE, ds))
                dk_parts.append(_mm_nt(ds, qtg))
                dv_parts.append(_mm_nt(probs.astype(BF16), dotg))
            kacc_ref[window, :] += jnp.concatenate(dk_parts, axis=1)
            vacc_ref[window, :] += jnp.concatenate(dv_parts, axis=1)
        dq = _to_token_rows(dq_parts)
        for ci in range(ATTN_W // 128):
            sl = slice(128 * ci, 128 * (ci + 1))
            dqkv_ref[:, sl] = _rope_transpose(dq[:, sl], c, s1, s2).astype(BF16)
        kacc_ref[ATTN_STEP:, :] += ck_ref[...]
        vacc_ref[ATTN_STEP:, :] += cv_ref[...]
        ck_ref[...] = kacc_ref[:BLOCK, :]
        cv_ref[...] = vacc_ref[:BLOCK, :]
        dqkv_ref[:, ATTN_W:ATTN_W + KV_W] = _rope_transpose(kacc_ref[BLOCK:, :], c, s1, s2).astype(BF16)
        dqkv_ref[:, ATTN_W + KV_W:] = vacc_ref[BLOCK:, :].astype(BF16)
        dsink_ref[0:1, :] += dsink

    blk = lambda w_: pl.BlockSpec((ATTN_STEP, w_), lambda i: (rev(i), 0))
    return pl.pallas_call(
        body, name="attn_bwd", grid=(n_steps,),
        in_specs=[pl.BlockSpec(memory_space=pltpu.SMEM)] + _qkv_specs(rev)
        + [blk(ATTN_W), blk(ATTN_W), _full((1, ATTN_W)), blk(128), blk(128), blk(128), HBM_SPEC],
        out_specs=[blk(QKV_W), _full((8, 128)), _full((SMALL_ROWS, ATTN_W)), HBM_SPEC],
        out_shape=[jax.ShapeDtypeStruct((t, QKV_W), BF16), jax.ShapeDtypeStruct((8, 128), F32),
                   jax.ShapeDtypeStruct((SMALL_ROWS, ATTN_W), F32), jax.ShapeDtypeStruct(sums.shape, sums.dtype)],
        scratch_shapes=[pltpu.VMEM((BLOCK, KV_W), F32), pltpu.VMEM((BLOCK, KV_W), F32),
                        pltpu.VMEM((ATTN_KEYS, KV_W), F32), pltpu.VMEM((ATTN_KEYS, KV_W), F32),
                        pltpu.SemaphoreType.DMA((len(CHIP_FLIPS),)), pltpu.SemaphoreType.DMA((len(CHIP_FLIPS),))],
        compiler_params=_params("arbitrary", barrier_id=6),
    )(sinks, qkv, qkv, qkv, qkv, qkv, dmixed, attn, g_attn, rc, rs1, rs2, sums)


def _grad_x_tile(dq, dg, x_hat, r, g1, w_ref, dh):
    dhn = _mm_nt(dq, w_ref[:, :QKV_W]) + _mm_nt(dg, w_ref[:, QKV_W:])
    dx, dg1 = _rms_bwd(x_hat, r, g1, dhn)
    return dh + dx, _colsum(dg1)


def _in_proj_bwd(dqkv, dgates, x, dh, g1, w_in, tm, out_sums):
    t = x.shape[0]
    n = t // tm
    n_cover = max(n // 2, 1)
    n_steps = n + n_cover
    n_far = len(CHIP_FLIPS)
    shard = (D_MODEL, IN_SHARD)

    def body(dq_ref, dg_ref, x_ref, dh_ref, g1_ref, w_ref, osums_ref,
             dx_ref, own_ref, sib_ref, far_ref, dg1_ref, oarrived_ref,
             acc_ref, send_buf, land_buf, pair_buf, d2d_send, d2d_recv, ici_send, ici_recv, o_send, o_recv):
        i = pl.program_id(0)
        x_pos, y_pos, c = _mesh_pos()
        my_chip = 2 * x_pos + y_pos
        sibling = (x_pos, y_pos, 1 - c)
        @pl.when(i == 0)
        def _():
            _enter_with(_sibling_and_chips(x_pos, y_pos, c))

        _chip_exchange_beside(i == 0, i == n_steps - 1, [osums_ref], [oarrived_ref], (o_send, o_recv), enter=False)

        def cols(d):
            return slice(IN_SHARD * d, IN_SHARD * (d + 1))

        def hand_over(chip):
            return _push(send_buf.at[chip], land_buf.at[chip], (d2d_send, d2d_recv), chip, sibling)

        def to_chip(chip, rel):
            return pltpu.make_async_remote_copy(
                src_ref=pair_buf.at[chip], dst_ref=far_ref.at[rel - 1], send_sem=ici_send.at[rel - 1],
                recv_sem=ici_recv.at[rel - 1], device_id=(chip // 2, chip % 2, c), device_id_type=MESH)

        @pl.when(i == 0)
        def _():
            acc_ref[...] = jnp.zeros_like(acc_ref)
            dg1_ref[...] = jnp.zeros_like(dg1_ref)

        def normed_x():
            xv = x_ref[...]
            r = _inv_rms(xv)
            return xv * r, r

        @pl.when(i < n)
        def _():
            hn = (normed_x()[0] * g1_ref[...]).astype(BF16)
            acc_ref[:, :QKV_W] += _mm_tn(hn, dq_ref[...])
            acc_ref[:, QKV_W:] += _mm_tn(hn, dg_ref[...])

        @pl.when(i == n - 1)
        def _():
            for d in range(N_DEV):
                @pl.when(d % 2 != c)
                def _():
                    send_buf[d // 2] = acc_ref[:, cols(d)].astype(BF16)
                    hand_over(d // 2).start()
            for d in range(N_DEV):
                chip = d // 2

                @pl.when(d % 2 == c)
                def _():
                    hand_over(chip).wait_recv()

                    @pl.when(chip == my_chip)
                    def _():
                        own_ref[...] = acc_ref[:, cols(d)]
                        sib_ref[...] = land_buf[chip]

                    @pl.when(chip != my_chip)
                    def _():
                        pair_buf[chip] = (acc_ref[:, cols(d)] + land_buf[chip].astype(F32)).astype(BF16)
                        to_chip(chip, chip ^ my_chip).start()
            for chip in range(N_CHIPS):
                hand_over(chip).wait_send()

        @pl.when(i >= n)
        def _():
            x_hat, r = normed_x()
            dx_ref[...], dg1 = _grad_x_tile(dq_ref[...], dg_ref[...], x_hat, r, g1_ref[...], w_ref, dh_ref[...])
            dg1_ref[0:1, :] += dg1

        @pl.when(i == n_steps - 1)
        def _():
            for rel in range(1, n_far + 1):
                to_chip(0, rel).wait()

    both = lambda w_: pl.BlockSpec((tm, w_), lambda i: (i % n, 0))
    second = pl.BlockSpec((tm, D_MODEL), lambda i: (jnp.maximum(i - n, 0), 0))
    whole = lambda dtype: jax.ShapeDtypeStruct(shard, dtype)
    sems = lambda k: pltpu.SemaphoreType.DMA((k,))
    res = pl.pallas_call(
        body, name="in_proj_bwd", grid=(n_steps,),
        in_specs=[both(QKV_W), both(GATES_W), both(D_MODEL), second, _full((1, D_MODEL)), _resident((D_MODEL, IN_COLS)),
                  HBM_SPEC],
        out_specs=[second, _full(shard), _full(shard), HBM_SPEC, _full((SMALL_ROWS, D_MODEL)), HBM_SPEC],
        out_shape=[jax.ShapeDtypeStruct((n_cover * tm, D_MODEL), F32), whole(F32), whole(BF16),
                   jax.ShapeDtypeStruct((n_far,) + shard, BF16), jax.ShapeDtypeStruct((SMALL_ROWS, D_MODEL), F32),
                   jax.ShapeDtypeStruct(out_sums.shape, out_sums.dtype)],
        scratch_shapes=[pltpu.VMEM((D_MODEL, IN_COLS), F32), pltpu.VMEM((N_CHIPS,) + shard, BF16),
                        pltpu.VMEM((N_CHIPS,) + shard, BF16), pltpu.VMEM((N_CHIPS,) + shard, BF16),
                        sems(N_CHIPS), sems(N_CHIPS), sems(n_far), sems(n_far), sems(n_far), sems(n_far)],
        compiler_params=_params("arbitrary", barrier_id=7),
    )(dqkv, dgates, x, dh, g1, w_in, out_sums)
    return res[0], (res[1], res[2], res[3]), res[4], res[5]


def _grad_x_rest(dqkv, dgates, x, dh, g1, w_in, tm, head, dg1_rows):
    t = x.shape[0]
    first = head.shape[0] // tm
    n_rest = t // tm - first
    if n_rest == 0:
        return head, dg1_rows
    assert first <= n_rest

    def body(dq_ref, dg_ref, x_ref, dh_ref, g1_ref, w_ref, head_ref, rows_ref, gx_ref, dg1_ref, stage, sems):
        j = pl.program_id(0)

        def tile_out(step, kind):
            row0 = (step + first) * tm if kind == 0 else step * tm
            slot = 2 * kind + step % 2
            return pltpu.make_async_copy(stage.at[slot], gx_ref.at[pl.ds(pl.multiple_of(row0, tm), tm), :], sems.at[slot])

        @pl.when(j == 0)
        def _():
            dg1_ref[...] = rows_ref[...]

        @pl.when(j >= 2)
        def _():
            tile_out(j - 2, 0).wait()

        @pl.when((j >= 2) & (j - 2 < first))
        def _():
            tile_out(j - 2, 1).wait()

        @pl.when(j < first)
        def _():
            stage[2 + j % 2] = head_ref[...]
            tile_out(j, 1).start()

        xv = x_ref[...]
        r = _inv_rms(xv)
        dx, dg1 = _grad_x_tile(dq_ref[...], dg_ref[...], xv * r, r, g1_ref[...], w_ref, dh_ref[...])
        stage[j % 2] = dx
        dg1_ref[0:1, :] += dg1
        tile_out(j, 0).start()

        @pl.when(j == n_rest - 1)
        def _():
            for back in range(min(2, n_rest)):
                tile_out(j - back, 0).wait()

                @pl.when(j - back < first)
                def _():
                    tile_out(j - back, 1).wait()

    tile = lambda w_: pl.BlockSpec((tm, w_), lambda j: (j + first, 0))
    head_tile = pl.BlockSpec((tm, D_MODEL), lambda j: (jnp.minimum(j, first - 1), 0))
    return pl.pallas_call(
        body, name="grad_x_rest", grid=(n_rest,),
        in_specs=[tile(QKV_W), tile(GATES_W), tile(D_MODEL), tile(D_MODEL), _full((1, D_MODEL)),
                  _resident((D_MODEL, IN_COLS)), head_tile, _full((SMALL_ROWS, D_MODEL))],
        out_specs=[HBM_SPEC, _full((SMALL_ROWS, D_MODEL))],
        out_shape=[jax.ShapeDtypeStruct((t, D_MODEL), F32), jax.ShapeDtypeStruct((SMALL_ROWS, D_MODEL), F32)],
        scratch_shapes=[pltpu.VMEM((4, tm, D_MODEL), F32), pltpu.SemaphoreType.DMA((4,))],
        compiler_params=_params("arbitrary"),
    )(dqkv, dgates, x, dh, g1, w_in, head, dg1_rows)


def _all_gather(shards, name):
    n = len(shards)

    def body(*refs):
        _enter_with(_sibling_and_chips(*_mesh_pos()))
        start, finish = _gather_steps(refs[:n], refs[n:2 * n], *refs[2 * n:])
        start()
        finish()

    return pl.pallas_call(
        body, name=name,
        in_specs=[HBM_SPEC] * n, out_specs=[HBM_SPEC] * n,
        out_shape=[jax.ShapeDtypeStruct((N_DEV,) + s.shape, s.dtype) for s in shards],
        scratch_shapes=[pltpu.SemaphoreType.DMA((7 * n,)), pltpu.SemaphoreType.DMA((7 * n,)),
                        pltpu.SemaphoreType.DMA((n,))],
        compiler_params=_params(barrier_id=8),
    )(*shards)


def _adam_math(w, g, m, v):
    m = ADAM_B1 * m + (1.0 - ADAM_B1) * g
    v = ADAM_B2 * v + (1.0 - ADAM_B2) * (g * g)
    m_hat = m / (1.0 - ADAM_B1 ** ADAM_STEP)
    v_hat = v / (1.0 - ADAM_B2 ** ADAM_STEP)
    delta = -ADAM_LR * (m_hat / (jnp.sqrt(v_hat) + ADAM_EPS) + ADAM_WD * w)
    return delta, m, v


def _adamw_reduced(w, m, v, own, from_sibling, from_chips, tr):
    rows, cols = w.shape

    def body(w_ref, m_ref, v_ref, own_ref, sib_ref, far_ref, g_ref, d_ref, nm_ref, nv_ref):
        g = own_ref[...] + sib_ref[...].astype(F32)
        for k in range(len(CHIP_FLIPS)):
            g = g + far_ref[k].astype(F32)
        g_ref[...] = g
        d_ref[...], nm_ref[...], nv_ref[...] = _adam_math(w_ref[...], g, m_ref[...], v_ref[...])

    tile = pl.BlockSpec((tr, cols), lambda i: (i, 0))
    out = jax.ShapeDtypeStruct((rows, cols), F32)
    return pl.pallas_call(
        body, name="adamw_reduced", grid=(rows // tr,),
        in_specs=[tile] * 5 + [pl.BlockSpec((len(CHIP_FLIPS), tr, cols), lambda i: (0, i, 0))],
        out_specs=[tile] * 4, out_shape=[out] * 4,
        compiler_params=_params("parallel"),
    )(w, m, v, own, from_sibling, from_chips)


SMALL_PARAMS = ("pre_mix_norm", "post_mix_norm", "pre_mlp_norm", "post_mlp_norm", "attn_group_norm", "conv_group_norm",
                "conv_w", "attn_sinks")


def _small_tail(gathered, dev, weights, first_moments, second_moments):
    n = len(SMALL_PARAMS)
    conv_shard = CONV_W // N_DEV

    def body(dev_ref, mid_ref, conv_ref, gain_ref, sink_ref, in_ref, *refs):
        w_refs, m_refs, v_refs = refs[:n], refs[n:2 * n], refs[2 * n:3 * n]
        loss_ref, outs = refs[3 * n], refs[3 * n + 1:]

        def total(ref):
            acc = ref[0]
            for d in range(1, N_DEV):
                acc = acc + ref[d]
            return acc

        mid, conv, gain, sink, inp = total(mid_ref), total(conv_ref), total(gain_ref), total(sink_ref), total(in_ref)
        loss_ref[...] = (0.5 / D_MODEL) * jnp.sum(mid[ROW_LOSS:ROW_LOSS + 1, :], axis=1, keepdims=True)
        conv_rows = conv[ROW_CW0:ROW_CW0 + 3, :]
        conv_g = jnp.zeros((3, conv_shard), F32)
        for d in range(N_DEV):
            conv_g = conv_g + jnp.where(dev_ref[0] == d, conv_rows[:, conv_shard * d:conv_shard * (d + 1)], 0.0)
        grads = [inp[0:1, :], mid[ROW_G2:ROW_G2 + 1, :], mid[ROW_G3:ROW_G3 + 1, :], mid[ROW_G4:ROW_G4 + 1, :],
                 gain[0:1, :], conv[ROW_GCONV:ROW_GCONV + 1, :], conv_g, sink[0:1, :N_HEADS]]
        for i, g in enumerate(grads):
            delta, new_m, new_v = _adam_math(w_refs[i][...], g, m_refs[i][...], v_refs[i][...])
            outs[i][...], outs[n + i][...], outs[2 * n + i][...], outs[3 * n + i][...] = g, delta, new_m, new_v

    params = list(weights) + list(first_moments) + list(second_moments)
    shapes = [jax.ShapeDtypeStruct(w.shape, F32) for w in weights]
    res = pl.pallas_call(
        body, name="small_tail", grid=(1,),
        in_specs=[pl.BlockSpec(memory_space=pltpu.SMEM)] + [_full(g.shape) for g in gathered] + [_full(p.shape) for p in params],
        out_specs=[_full((1, 1))] + [_full(sh.shape) for sh in shapes] * 4,
        out_shape=[jax.ShapeDtypeStruct((1, 1), F32)] + shapes * 4,
    )(dev, *gathered, *params)
    return res[0], [res[1 + k * n:1 + (k + 1) * n] for k in range(4)]


TOKEN_TILE = 512
MID_TILE = 256
MID_CHUNK = 1024
MID_CHUNKS = D_FF // MID_CHUNK
ADAM_ROWS = 128


def _local_grads(x, target, g1, w_in_shard, conv_shard, sinks, g_attn, g_conv, g2, g3, g4, shards, order):
    t = x.shape[0]
    tm = min(TOKEN_TILE, t)
    rope = _rope_tables(t)
    qkv, gates, mconv, w_in, conv_w, gathered = _in_proj_fwd(x, g1, w_in_shard, conv_shard, g_conv, rope, tm, shards,
                                                             (False, True, False))
    attn, mattn, (w_out, w_up, w_down) = _attn_fwd(qkv, sinks, g_attn, shards, gathered)
    act, dup, hn2t, dmo, dmix, dh, dmixed, small_mid = _mid(
        mattn, mconv, x, target, g2, g3, g4, w_out.reshape(D_MODEL, D_MODEL),
        w_up, w_down.reshape(D_FF, D_MODEL), min(MID_TILE, t))
    up_own, up_sib, up_sums = _dw_pair_sums((hn2t, dup), order, "up", "dw_up", 2)
    down_own, down_sib, down_sums, up_far = _dw_pair_sums((act, dmo), order, "down", "dw_down", 3, ride=up_sums)
    out_own, out_sib, out_sums = _dw_pair_sums((mattn, mconv, dmix), order, "out", "dw_out", 4)
    dgates, small_conv = _conv_bwd(dmixed, gates, g_conv, conv_w, tm)
    dqkv, dsink, dg_attn, down_far = _attn_bwd(qkv, dmixed, attn, g_attn, sinks, rope, down_sums)
    grad_x_head, dw_in, small_in, out_far = _in_proj_bwd(dqkv, dgates, x, dh, g1, w_in, tm, out_sums)
    grad_x, small_in = _grad_x_rest(dqkv, dgates, x, dh, g1, w_in, tm, grad_x_head, small_in)
    dw_out, dw_up, dw_down = (out_own, out_sib, out_far), (up_own, up_sib, up_far), (down_own, down_sib, down_far)
    return grad_x, dw_in, dw_out, dw_up, dw_down, (small_mid, small_conv, dg_attn, dsink, small_in)


def kernel(x, pre_mix_norm, w_in, conv_w, attn_sinks, attn_group_norm, conv_group_norm, w_out, post_mix_norm, pre_mlp_norm, w_up, w_down, post_mlp_norm, loss_target, m_pre_mix_norm, m_w_in, m_conv_w, m_attn_sinks, m_attn_group_norm, m_conv_group_norm, m_w_out, m_post_mix_norm, m_pre_mlp_norm, m_w_up, m_w_down, m_post_mlp_norm, v_pre_mix_norm, v_w_in, v_conv_w, v_attn_sinks, v_attn_group_norm, v_conv_group_norm, v_w_out, v_post_mix_norm, v_pre_mlp_norm, v_w_up, v_w_down, v_post_mlp_norm):
    xi, yi, ci = _mesh_pos()
    chip = 2 * xi + yi
    dev = 2 * chip + ci

    order = _block_order(dev)

    shards = [w_out[0].astype(BF16), w_up[0].astype(BF16), w_down[0].astype(BF16)]

    grad_x, dw_in, dw_out, dw_up, dw_down, smalls = _local_grads(
        x[0], loss_target[0], pre_mix_norm, w_in[0].astype(BF16), conv_w[0], attn_sinks, attn_group_norm, conv_group_norm,
        post_mix_norm, pre_mlp_norm, post_mlp_norm, shards, order)

    big = {}
    for name, w, m, v, (own, sib, far) in zip(
            ("w_in", "w_out", "w_up", "w_down"), (w_in, w_out, w_up, w_down), (m_w_in, m_w_out, m_w_up, m_w_down),
            (v_w_in, v_w_out, v_w_up, v_w_down), (dw_in, dw_out, dw_up, dw_down)):
        big[name] = [a[None] for a in _adamw_reduced(w[0], m[0], v[0], own, sib, far, ADAM_ROWS)]

    flat = lambda a: a.reshape(-1, a.shape[-1])
    loss, small = _small_tail(
        _all_gather(list(smalls), "gather_small"), dev.reshape(1).astype(jnp.int32),
        [flat(a) for a in (pre_mix_norm, post_mix_norm, pre_mlp_norm, post_mlp_norm, attn_group_norm, conv_group_norm,
                           conv_w, attn_sinks)],
        [flat(a) for a in (m_pre_mix_norm, m_post_mix_norm, m_pre_mlp_norm, m_post_mlp_norm, m_attn_group_norm,
                           m_conv_group_norm, m_conv_w, m_attn_sinks)],
        [flat(a) for a in (v_pre_mix_norm, v_post_mix_norm, v_pre_mlp_norm, v_post_mlp_norm, v_attn_group_norm,
                           v_conv_group_norm, v_conv_w, v_attn_sinks)])

    order = ("pre_mix_norm", "w_in", "conv_w", "attn_sinks", "attn_group_norm", "conv_group_norm", "w_out",
             "post_mix_norm", "pre_mlp_norm", "w_up", "w_down", "post_mlp_norm")
    shape_of = {"conv_w": conv_w.shape}
    outs = []
    for k in range(4):
        by_name = dict(zip(SMALL_PARAMS, small[k]))
        outs += [big[nm][k] if nm in big else by_name[nm].reshape(shape_of.get(nm, by_name[nm].shape)) for nm in order]
    loss = loss.reshape(())
    return (loss, grad_x[None], *outs)
```

```python
import functools

import jax
import jax.numpy as jnp
import numpy as np
from jax import lax
from jax.experimental import pallas as pl
from jax.experimental.pallas import tpu as pltpu

F32 = jnp.float32
BF16 = jnp.bfloat16

D_MODEL = 1024
HEAD_DIM = 64
ATTN_W = 512
CONV_W = 512
N_HEADS = 8
N_KV = 2
GROUP = 4
KV_W = 128
QKV_W = ATTN_W + 2 * KV_W
GATES_W = 3 * CONV_W
IN_COLS = QKV_W + GATES_W
D_FF = 4096
FF_CHUNK = 512
N_FF_CHUNKS = D_FF // FF_CHUNK
BLOCK = 128
ROT_HALF = 8
ROPE_THETA = 500000.0
NORM_EPS = 1e-6
NEG_INF = -1e30
ATTN_SCALE = 0.125
N_DEV = 8
N_CHIPS = 4
IN_SHARD = IN_COLS // N_DEV

ADAM_LR = 0.001
ADAM_B1 = 0.9
ADAM_B2 = 0.999
ADAM_EPS = 1e-08
ADAM_WD = 0.01
ADAM_STEP = 10

V7X_VMEM_BYTES = 64 * 1024 * 1024
VMEM_LIMIT = V7X_VMEM_BYTES - 2 * 1024 * 1024

MESH = pl.DeviceIdType.MESH
HBM_SPEC = pl.BlockSpec(memory_space=pltpu.HBM)


def _params(*sem, barrier_id=None):
    return pltpu.CompilerParams(dimension_semantics=sem or None, vmem_limit_bytes=VMEM_LIMIT, collective_id=barrier_id)


def _mm(a, b):
    return jnp.dot(a, b, preferred_element_type=F32)


def _mm_nt(a, b):
    return lax.dot_general(a, b, (((1,), (1,)), ((), ())), preferred_element_type=F32)


def _mm_tn(a, b):
    return lax.dot_general(a, b, (((0,), (0,)), ((), ())), preferred_element_type=F32)


def _inv_rms(x):
    return lax.rsqrt(jnp.mean(x * x, axis=-1, keepdims=True) + NORM_EPS)


def _rms_bwd(xhat, r, gain, dy):
    gy = dy * gain
    return r * (gy - xhat * jnp.mean(gy * xhat, axis=-1, keepdims=True)), dy * xhat


def _colsum(a):
    return jnp.sum(a, axis=0, keepdims=True)


def _full(shape):
    zeros = (0,) * len(shape)
    return pl.BlockSpec(shape, lambda *_: zeros)


def _resident(shape):
    zeros = (0,) * len(shape)
    return pl.BlockSpec(shape, lambda *_: zeros, pipeline_mode=pl.Buffered(1))


def _rope_tables(t):
    pos = np.arange(t, dtype=np.float32)
    inv_freq = (ROPE_THETA ** (-np.arange(0, 2 * ROT_HALF, 2, dtype=np.float64) / (2 * ROT_HALF))).astype(np.float32)
    ang = (pos[:, None] * inv_freq[None, :]).astype(np.float64)
    cos, sin = np.cos(ang).astype(np.float32), np.sin(ang).astype(np.float32)
    zeros8 = np.zeros((t, ROT_HALF), np.float32)
    rest = np.zeros((t, HEAD_DIM - 2 * ROT_HALF), np.float32)
    c_head = np.concatenate([cos, cos, rest + 1.0], axis=1)
    s1_head = np.concatenate([zeros8, sin, rest], axis=1)
    s2_head = np.concatenate([-sin, zeros8, rest], axis=1)
    two = lambda a: jnp.asarray(np.concatenate([a, a], axis=1))
    return two(c_head), two(s1_head), two(s2_head)


def _rope(v, c, s1, s2):
    return v * c + pltpu.roll(v, ROT_HALF, 1) * s1 + pltpu.roll(v, 128 - ROT_HALF, 1) * s2


def _rope_transpose(dv, c, s1, s2):
    return dv * c + pltpu.roll(dv * s1, 128 - ROT_HALF, 1) + pltpu.roll(dv * s2, ROT_HALF, 1)


def _shift_rows_down(u, prev, k):
    row = lax.broadcasted_iota(jnp.int32, u.shape, 0)
    out = pltpu.roll(u, k, 0)
    for r in range(k):
        out = jnp.where(row == r, prev[8 - k + r:8 - k + r + 1, :], out)
    return out


def _shift_rows_up(u, nxt, k):
    n = u.shape[0]
    row = lax.broadcasted_iota(jnp.int32, u.shape, 0)
    out = pltpu.roll(u, n - k, 0)
    for r in range(k):
        out = jnp.where(row == n - k + r, nxt[r:r + 1, :], out)
    return out


def _conv3(u, u1, u2, w):
    return (w[0:1, :] * u2 + w[1:2, :] * u1) + w[2:3, :] * u


def _mesh_pos():
    return lax.axis_index("x"), lax.axis_index("y"), lax.axis_index("c")


def _slot(ref, pos):
    dev = 4 * pos[0] + 2 * pos[1] + pos[2]
    if len(ref.shape) == 2:
        width = ref.shape[1] // N_DEV
        return ref.at[:, pl.ds(pl.multiple_of(dev * width, width), width)]
    return ref.at[dev]


def _gathered_shape(shard, by_cols):
    if by_cols:
        return jax.ShapeDtypeStruct((shard.shape[0], N_DEV * shard.shape[1]), shard.dtype)
    return jax.ShapeDtypeStruct((N_DEV,) + shard.shape, shard.dtype)


def _enter_with(peers):
    barrier = pltpu.get_barrier_semaphore()
    for peer in peers:
        pl.semaphore_signal(barrier, inc=1, device_id=peer, device_id_type=MESH)
    pl.semaphore_wait(barrier, len(peers))


def _sibling_and_chips(x, y, c):
    return [(x, y, 1 - c), (1 - x, y, c), (x, 1 - y, c), (1 - x, 1 - y, c)]


def _push(src, dst, sems, k, to):
    send_sems, recv_sems = sems
    return pltpu.make_async_remote_copy(src_ref=src, dst_ref=dst, send_sem=send_sems.at[k], recv_sem=recv_sems.at[k],
                                        device_id=to, device_id_type=MESH)


def _gather_steps(shards, outs, send_sems, recv_sems, local_sems):
    n = len(shards)
    x, y, c = _mesh_pos()
    me, sibling = (x, y, c), (x, y, 1 - c)
    chips = [(1 - x, y), (x, 1 - y), (1 - x, 1 - y)]

    def copy(i, k, block, to, src=None):
        dst = _slot(outs[i], block)
        return _push(dst if src is None else src, dst, (send_sems, recv_sems), 7 * i + k, to)

    mine = [pltpu.make_async_copy(shards[i], _slot(outs[i], me), local_sems.at[i]) for i in range(n)]
    first = []
    for i in range(n):
        first.append(copy(i, 0, me, sibling, src=shards[i]))
        first += [copy(i, 1 + j, me, (*chip, c), src=shards[i]) for j, chip in enumerate(chips)]

    def start():
        for cp in mine + first:
            cp.start()

    def finish():
        passed = []
        for j, chip in enumerate(chips):
            for i in range(n):
                copy(i, 1 + j, (*chip, c), me).wait_recv()
                cp = copy(i, 4 + j, (*chip, c), sibling)
                cp.start()
                passed.append(cp)
        for i in range(n):
            copy(i, 0, sibling, me).wait_recv()
            for j, chip in enumerate(chips):
                copy(i, 4 + j, (*chip, 1 - c), me).wait_recv()
        for cp in first + passed:
            cp.wait_send()
        for cp in mine:
            cp.wait()

    return start, finish


def _gather_near(first, last, shards, outs, sems, local_sems):
    x, y, c = _mesh_pos()
    me, peers = (x, y, c), [(x, y, 1 - c), (1 - x, y, c), (x, 1 - y, c)]
    n = len(shards)
    local = [pltpu.make_async_copy(shards[i], _slot(outs[i], me), local_sems.at[i]) for i in range(n)]
    sends = [_push(shards[i], _slot(outs[i], me), sems, 3 * i + k, peers[k]) for i in range(n) for k in range(3)]
    arrivals = [_push(shards[i], _slot(outs[i], peers[k]), sems, 3 * i + k, peers[k]) for i in range(n) for k in range(3)]

    def start():
        for cp in local + sends:
            cp.start()

    if first is not None:
        pl.when(first)(start)

    @pl.when(last)
    def _():
        for cp in sends:
            cp.wait_send()
        for cp in arrivals:
            cp.wait_recv()
        for cp in local:
            cp.wait()

    return start


def _relay_route(x, y, c):
    south = c == 0
    via = (jnp.where(south, 1 - x, x), jnp.where(south, y, 1 - y))
    to = (jnp.where(south, x, 1 - x), jnp.where(south, 1 - y, y))
    return via, to


def _gather_far(first, middle, last, shards, ins, outs, sems, local_sems, n_late):
    x, y, c = _mesh_pos()
    me, sibling = (x, y, c), (x, y, 1 - c)
    chips = [(1 - x, y), (x, 1 - y), (1 - x, 1 - y)]
    near = [sibling, (*chips[0], c), (*chips[1], c)]
    via, to = _relay_route(x, y, c)
    n = len(shards)
    early, late = range(n - n_late), range(n - n_late, n)

    def relay(i):
        return _push(_slot(ins[i], (*via, c)), _slot(outs[i], (*via, c)), sems, 7 * i + 3, (*to, c))

    def diag_arrival(i):
        return _push(shards[i], _slot(outs[i], (*chips[2], c)), sems, 7 * i + 3, (*to, c))

    def passed(i, j):
        return _push(_slot(ins[i], (*chips[j], c)), _slot(outs[i], (*chips[j], c)), sems, 7 * i + 4 + j, sibling)

    def from_sibling(i, j):
        return _push(shards[i], _slot(outs[i], (*chips[j], 1 - c)), sems, 7 * i + 4 + j, sibling)

    local = [pltpu.make_async_copy(shards[i], _slot(outs[i], me), local_sems.at[i - (n - n_late)]) for i in late]
    near_sends = [_push(shards[i], _slot(outs[i], me), sems, 7 * i + k, near[k]) for i in late for k in range(3)]
    near_arrivals = [_push(shards[i], _slot(outs[i], near[k]), sems, 7 * i + k, near[k]) for i in late for k in range(3)]

    def pass_on(i):
        for cp in (relay(i), passed(i, 0), passed(i, 1)):
            cp.start()

    @pl.when(first)
    def _():
        for i in early:
            pass_on(i)
        for cp in local + near_sends:
            cp.start()

    @pl.when(middle)
    def _():
        for cp in near_arrivals:
            cp.wait_recv()
        for i in late:
            pass_on(i)

    @pl.when(last)
    def _():
        for i in range(n):
            diag_arrival(i).wait_recv()
            passed(i, 2).start()
        for i in range(n):
            for j in range(3):
                from_sibling(i, j).wait_recv()
        for i in range(n):
            for cp in (relay(i), passed(i, 0), passed(i, 1), passed(i, 2)):
                cp.wait_send()
        for cp in near_sends:
            cp.wait_send()
        for cp in local:
            cp.wait()


def _in_proj_fwd(x, g1, w_in, conv_w, g_conv, rope, tm, shards, by_cols):
    t = x.shape[0]
    rc, rs1, rs2 = rope
    n = len(shards)
    n_tiles = t // tm

    def body(*refs):
        x_ref, g1_ref, w_ref, cw_ref, gc_ref, c_ref, s1_ref, s2_ref = refs[:8]
        shard_refs = refs[8:8 + n]
        qkv_ref, gates_ref, mconv_ref, w_full_ref, cw_full_ref = refs[8 + n:13 + n]
        gathered = refs[13 + n:13 + 2 * n]
        carry_ref, w_land, cw_land, hn_ref = refs[13 + 2 * n:17 + 2 * n]
        now_sems = refs[17 + 2 * n:20 + 2 * n]
        step = pl.program_id(0)
        start_later_weights = _gather_near(None, step == 2 * n_tiles - 1, shard_refs, gathered,
                                           refs[20 + 2 * n:22 + 2 * n], refs[22 + 2 * n]) if n else None
        start_w_in, finish_w_in = _gather_steps([w_ref, cw_ref], [w_land, cw_land], *now_sems)

        @pl.when(step == 0)
        def _():
            carry_ref[...] = jnp.zeros_like(carry_ref)
            _enter_with(_sibling_and_chips(*_mesh_pos()))
            start_w_in()
            if start_later_weights is not None:
                start_later_weights()

        @pl.when(step < n_tiles)
        def _():
            xv = x_ref[...]
            hn_ref[step] = ((xv * _inv_rms(xv)) * g1_ref[...]).astype(BF16)

        @pl.when(step == n_tiles)
        def _():
            finish_w_in()
            conv_shard = CONV_W // N_DEV
            for d in range(N_DEV):
                w_full_ref[:, IN_SHARD * d:IN_SHARD * (d + 1)] = w_land[d]
                cw_full_ref[:, conv_shard * d:conv_shard * (d + 1)] = cw_land[d]

        @pl.when(step >= n_tiles)
        def _():
            proj = _mm(hn_ref[step - n_tiles], w_full_ref[...])
            c, s1, s2 = c_ref[...], s1_ref[...], s2_ref[...]
            for ci in range((ATTN_W + KV_W) // 128):
                sl = slice(128 * ci, 128 * (ci + 1))
                qkv_ref[:, sl] = _rope(proj[:, sl], c, s1, s2).astype(BF16)
            qkv_ref[:, ATTN_W + KV_W:QKV_W] = proj[:, ATTN_W + KV_W:QKV_W].astype(BF16)
            gates = proj[:, QKV_W:]
            gates_ref[...] = gates
            gb, gcc, xin = gates[:, :CONV_W], gates[:, CONV_W:2 * CONV_W], gates[:, 2 * CONV_W:]
            u = gcc * xin
            prev = carry_ref[...]
            conv = gb * _conv3(u, _shift_rows_down(u, prev, 1), _shift_rows_down(u, prev, 2), cw_full_ref[...])
            carry_ref[...] = u[tm - 8:tm, :]
            mconv_ref[...] = ((conv * _inv_rms(conv)) * gc_ref[...]).astype(BF16)

    first_pass = pl.BlockSpec((tm, D_MODEL), lambda i: (jnp.minimum(i, n_tiles - 1), 0))
    tile = lambda w_: pl.BlockSpec((tm, w_), lambda i: (jnp.maximum(i - n_tiles, 0), 0))
    sems = lambda k: pltpu.SemaphoreType.DMA((k,))
    res = pl.pallas_call(
        body, name="in_proj_fwd", grid=(2 * n_tiles,),
        in_specs=[first_pass, _full((1, D_MODEL)), HBM_SPEC, HBM_SPEC, _full((1, CONV_W)), tile(128), tile(128),
                  tile(128)] + [HBM_SPEC] * n,
        out_specs=[tile(QKV_W), tile(GATES_W), tile(CONV_W), _full((D_MODEL, IN_COLS)), _full((3, CONV_W))]
        + [HBM_SPEC] * n,
        out_shape=[jax.ShapeDtypeStruct((t, QKV_W), BF16), jax.ShapeDtypeStruct((t, GATES_W), F32),
                   jax.ShapeDtypeStruct((t, CONV_W), BF16), jax.ShapeDtypeStruct((D_MODEL, IN_COLS), BF16),
                   jax.ShapeDtypeStruct((3, CONV_W), F32)]
        + [_gathered_shape(s, cols) for s, cols in zip(shards, by_cols)],
        scratch_shapes=[pltpu.VMEM((8, CONV_W), F32), pltpu.VMEM((N_DEV,) + w_in.shape, BF16),
                        pltpu.VMEM((N_DEV,) + conv_w.shape, F32), pltpu.VMEM((n_tiles, tm, D_MODEL), BF16),
                        sems(14), sems(14), sems(2)]
        + ([sems(3 * n), sems(3 * n), sems(n)] if n else []),
        compiler_params=_params("arbitrary", barrier_id=0),
    )(x, g1, w_in, conv_w, g_conv, rc, rs1, rs2, *shards)
    return res[0], res[1], res[2], res[3], res[4], list(res[5:])


GROUP_COLS = GROUP * BLOCK
ATTN_STEP_BLOCKS = 4


def _attn_masks(has_prev):
    key = lax.broadcasted_iota(jnp.int32, (2 * BLOCK, GROUP_COLS), 0)
    query = lax.broadcasted_iota(jnp.int32, (2 * BLOCK, GROUP_COLS), 1) & (BLOCK - 1)
    band = (key > query) & (key <= query + BLOCK)
    return [band & ((key >= BLOCK) | has_prev)] + [band] * (ATTN_STEP_BLOCKS - 1)


def _heads_side_by_side(at, g, b):
    heads = [at[HEAD_DIM * (GROUP * g + hh):HEAD_DIM * (GROUP * g + hh + 1), BLOCK * b:BLOCK * (b + 1)] for hh in range(GROUP)]
    return jnp.concatenate(heads, axis=1)


def _to_token_rows(parts):
    rows = [jnp.concatenate([parts[b][g][:, BLOCK * hh:BLOCK * (hh + 1)] for b in range(ATTN_STEP_BLOCKS)], axis=1)
            for g in range(N_KV) for hh in range(GROUP)]
    return jnp.concatenate(rows, axis=0).T


def _group_sinks(sink_ref, g):
    head = lax.broadcasted_iota(jnp.int32, (1, GROUP_COLS), 1) // BLOCK
    out = jnp.full((1, GROUP_COLS), sink_ref[0, GROUP * g], F32)
    for hh in range(1, GROUP):
        out = jnp.where(head == hh, sink_ref[0, GROUP * g + hh], out)
    return out


def _attn_probs(qt, kk, sink, valid):
    s = jnp.where(valid, _mm(kk, qt), NEG_INF)
    m = jnp.maximum(jnp.max(s, axis=0, keepdims=True), sink)
    p = jnp.exp(s - m)
    psink = jnp.exp(sink - m)
    inv_l = 1.0 / (jnp.sum(p, axis=0, keepdims=True) + psink)
    return p * inv_l, psink * inv_l


ATTN_STEP = ATTN_STEP_BLOCKS * BLOCK
ATTN_KEYS = ATTN_STEP + BLOCK


def _qkv_specs(order):
    prev = lambda i: jnp.maximum(ATTN_STEP_BLOCKS * order(i) - 1, 0)
    kcol, vcol = ATTN_W // KV_W, ATTN_W // KV_W + 1
    return [pl.BlockSpec((ATTN_STEP, ATTN_W), lambda i: (order(i), 0)),
            pl.BlockSpec((BLOCK, KV_W), lambda i: (prev(i), kcol)), pl.BlockSpec((ATTN_STEP, KV_W), lambda i: (order(i), kcol)),
            pl.BlockSpec((BLOCK, KV_W), lambda i: (prev(i), vcol)), pl.BlockSpec((ATTN_STEP, KV_W), lambda i: (order(i), vcol))]


def _attn_fwd(qkv, sinks, g_attn, shards, gathered, late_by_cols):
    t = qkv.shape[0]
    n, n_early = len(shards), len(gathered)
    n_late = n - n_early
    n_steps = t // ATTN_STEP

    def body(*refs):
        sink_ref, q_ref, kp_ref, kc_ref, vp_ref, vc_ref, ga_ref = refs[:7]
        attn_ref, mattn_ref = refs[7 + n + n_early:9 + n + n_early]
        step = pl.program_id(0)
        if n:
            @pl.when(step == 0)
            def _():
                x, y, c = _mesh_pos()
                _enter_with([(x, y, 1 - c), (1 - x, y, c), (x, 1 - y, c)])

            outs = refs[9 + n + n_early:9 + 2 * n + n_early]
            landed = list(refs[7 + n:7 + n + n_early]) + list(outs[n_early:])
            scratch = refs[9 + 2 * n + n_early:]
            _gather_far(step == 0, step == n_steps // 2, step == n_steps - 1, refs[7:7 + n], landed, outs, scratch[:2],
                        scratch[2], n_late)
        qt = (q_ref[...] * ATTN_SCALE).T
        keys = jnp.concatenate([kp_ref[...], kc_ref[...]], axis=0)
        vals = jnp.concatenate([vp_ref[...], vc_ref[...]], axis=0)
        sink = [_group_sinks(sink_ref, g) for g in range(N_KV)]
        masks = _attn_masks(step > 0)
        parts = []
        for b in range(ATTN_STEP_BLOCKS):
            window = slice(BLOCK * b, BLOCK * (b + 2))
            valid = masks[b]
            parts.append([])
            for g in range(N_KV):
                gs = slice(HEAD_DIM * g, HEAD_DIM * (g + 1))
                probs, _ = _attn_probs(_heads_side_by_side(qt, g, b), keys[window, gs], sink[g], valid)
                parts[b].append(_mm_tn(vals[window, gs], probs.astype(BF16)))
        attn = _to_token_rows(parts)
        attn_ref[...] = attn
        mattn_ref[...] = ((attn * _inv_rms(attn)) * ga_ref[...]).astype(BF16)

    blk = pl.BlockSpec((ATTN_STEP, ATTN_W), lambda j: (j, 0))
    res = pl.pallas_call(
        body, name="attn_fwd", grid=(t // ATTN_STEP,),
        in_specs=[pl.BlockSpec(memory_space=pltpu.SMEM)] + _qkv_specs(lambda j: j) + [_full((1, ATTN_W))]
        + [HBM_SPEC] * (n + n_early),
        out_specs=[blk, blk] + [HBM_SPEC] * n,
        out_shape=[jax.ShapeDtypeStruct((t, ATTN_W), F32), jax.ShapeDtypeStruct((t, ATTN_W), BF16)]
        + [jax.ShapeDtypeStruct(g.shape, g.dtype) for g in gathered]
        + [_gathered_shape(s, cols) for s, cols in zip(shards[n_early:], late_by_cols)],
        input_output_aliases={7 + n + i: 2 + i for i in range(n_early)},
        scratch_shapes=[pltpu.SemaphoreType.DMA((7 * n,)), pltpu.SemaphoreType.DMA((7 * n,)),
                        pltpu.SemaphoreType.DMA((max(n_late, 1),))] if n else [],
        compiler_params=_params("arbitrary", barrier_id=1 if n else None),
    )(sinks, qkv, qkv, qkv, qkv, qkv, g_attn, *shards, *gathered)
    return res[0], res[1], list(res[2:])


SMALL_ROWS = 8
ROW_LOSS, ROW_G2, ROW_G3, ROW_G4 = 0, 1, 2, 3


def _mid(mattn, mconv, x, target, g2, g3, g4, w_out, w_up, w_down, tm):
    t = x.shape[0]

    def body(ma_ref, mc_ref, x_ref, t_ref, g2_ref, g3_ref, g4_ref, wo_ref, wu_ref, wd_ref,
             act_ref, dup_ref, hn2t_ref, dmo_ref, dmix_ref, dh_ref, dmixed_ref, small_ref, up_ref):
        @pl.when(pl.program_id(0) == 0)
        def _():
            small_ref[...] = jnp.zeros_like(small_ref)

        g2, g3, g4 = g2_ref[...], g3_ref[...], g4_ref[...]
        mix_out = _mm(ma_ref[...], wo_ref[0:ATTN_W, :]) + _mm(mc_ref[...], wo_ref[ATTN_W:, :])
        r2 = _inv_rms(mix_out)
        mo_hat = mix_out * r2
        h = x_ref[...] + mo_hat * g2
        r3 = _inv_rms(h)
        h_hat = h * r3
        hn2 = (h_hat * g3).astype(BF16)
        hn2t_ref[...] = hn2.T
        for j in range(MID_CHUNKS):
            cols_j = slice(MID_CHUNK * j, MID_CHUNK * (j + 1))
            up = jnp.maximum(_mm(hn2, wu_ref[:, cols_j]), 0.0)
            up_ref[:, cols_j] = up.astype(BF16)
            act_ref[:, cols_j] = (up * up).astype(BF16)
        mlp = _mm(act_ref[...], wd_ref[...])
        r4 = _inv_rms(mlp)
        ml_hat = mlp * r4
        err = (h + ml_hat * g4) - t_ref[...]
        d_out = err * (1.0 / D_MODEL)
        d_mlp, dg4 = _rms_bwd(ml_hat, r4, g4, d_out)
        dmo = d_mlp.astype(BF16)
        dmo_ref[...] = dmo
        for j in range(MID_CHUNKS):
            cols_j = slice(MID_CHUNK * j, MID_CHUNK * (j + 1))
            dact = _mm_nt(dmo, wd_ref[cols_j, :])
            dup_ref[:, cols_j] = (dact * (2.0 * up_ref[:, cols_j].astype(F32))).astype(BF16)
        dhn2 = _mm_nt(dup_ref[...], wu_ref[...])
        dh_norm, dg3 = _rms_bwd(h_hat, r3, g3, dhn2)
        dh = d_out + dh_norm
        dh_ref[...] = dh
        d_mix, dg2 = _rms_bwd(mo_hat, r2, g2, dh)
        dmix = d_mix.astype(BF16)
        dmix_ref[...] = dmix
        dmixed_ref[...] = _mm_nt(dmix, wo_ref[...])
        small_ref[ROW_LOSS:ROW_LOSS + 1, :] += _colsum(err * err)
        small_ref[ROW_G2:ROW_G2 + 1, :] += _colsum(dg2)
        small_ref[ROW_G3:ROW_G3 + 1, :] += _colsum(dg3)
        small_ref[ROW_G4:ROW_G4 + 1, :] += _colsum(dg4)

    tile = lambda n: pl.BlockSpec((tm, n), lambda i: (i, 0))
    cols = lambda n: pl.BlockSpec((n, tm), lambda i: (0, i))
    gain = _full((1, D_MODEL))
    return pl.pallas_call(
        body, name="mid_fwd_bwd", grid=(t // tm,),
        in_specs=[tile(ATTN_W), tile(CONV_W), tile(D_MODEL), tile(D_MODEL), gain, gain, gain,
                  _resident((D_MODEL, D_MODEL)), _resident((D_MODEL, D_FF)), _resident((D_FF, D_MODEL))],
        out_specs=[tile(D_FF), tile(D_FF), cols(D_MODEL), tile(D_MODEL), tile(D_MODEL), tile(D_MODEL), tile(D_MODEL),
                   _full((SMALL_ROWS, D_MODEL))],
        out_shape=[jax.ShapeDtypeStruct((t, D_FF), BF16), jax.ShapeDtypeStruct((t, D_FF), BF16),
                   jax.ShapeDtypeStruct((D_MODEL, t), BF16), jax.ShapeDtypeStruct((t, D_MODEL), BF16),
                   jax.ShapeDtypeStruct((t, D_MODEL), BF16), jax.ShapeDtypeStruct((t, D_MODEL), F32),
                   jax.ShapeDtypeStruct((t, D_MODEL), F32), jax.ShapeDtypeStruct((SMALL_ROWS, D_MODEL), F32)],
        scratch_shapes=[pltpu.VMEM((tm, D_FF), BF16)],
        compiler_params=_params("arbitrary"),
    )(mattn, mconv, x, target, g2, g3, g4, w_out, w_up, w_down)


CHIP_FLIPS = ((1, 1), (1, 0), (0, 1))


def _block_order(dev):
    chip_masks = [4 * fx + 2 * fy for fx, fy in CHIP_FLIPS]
    masks = [m + 1 for m in chip_masks] + [1] + chip_masks + [0]
    return jnp.bitwise_xor(dev, jnp.asarray(masks, jnp.int32)).astype(jnp.int32)


def _other_chips(x, y, c):
    return [(1 - x if fx else x, 1 - y if fy else y, c) for fx, fy in CHIP_FLIPS]


def _dw_pair_sums(operands, order, which, name, barrier_id, ride=None):
    t = operands[-1].shape[0]
    n_far = len(CHIP_FLIPS)
    n_in = len(operands)
    n_ride = 0 if ride is None else 1
    out_chunk = D_MODEL // N_DEV
    if which == "up":
        rows, cols = D_MODEL, FF_CHUNK
        in_specs = [_resident((D_MODEL, t)), pl.BlockSpec((t, FF_CHUNK), lambda s, order_ref: (0, order_ref[s]))]
    elif which == "down":
        rows, cols = FF_CHUNK, D_MODEL
        in_specs = [pl.BlockSpec((t, FF_CHUNK), lambda s, order_ref: (0, order_ref[s])), _resident((t, D_MODEL))]
    else:
        rows, cols = out_chunk, D_MODEL
        half = pl.BlockSpec((t, out_chunk), lambda s, order_ref: (0, order_ref[s] % (N_DEV // 2)))
        in_specs = [half, half, _resident((t, D_MODEL))]

    def body(order_ref, *refs):
        own_ref, from_sib_ref, pair_ref = refs[n_in + n_ride:n_in + n_ride + 3]
        send_buf, land_buf, send_sems, recv_sems = refs[n_in + 2 * n_ride + 3:n_in + 2 * n_ride + 7]
        s_now = pl.program_id(0)
        x, y, c = _mesh_pos()
        sibling = (x, y, 1 - c)
        sems = (send_sems, recv_sems)

        @pl.when(s_now == 0)
        def _():
            _enter_with([sibling] + (_other_chips(x, y, c) if n_ride else []))

        if n_ride:
            _chip_exchange_beside(s_now == 0, s_now == N_DEV - 1, [refs[n_in]], [refs[n_in + 3 + n_ride]],
                                  refs[n_in + 2 * n_ride + 7:], enter=False)

        def hand_over(k):
            dst = land_buf.at[k] if k < n_far else from_sib_ref
            return _push(send_buf.at[k], dst, sems, k, sibling)

        if which == "out":
            ma_ref, mc_ref, b_ref = refs[:n_in]
            block = lax.cond(order_ref[s_now] < N_DEV // 2, lambda: _mm_tn(ma_ref[...], b_ref[...]),
                             lambda: _mm_tn(mc_ref[...], b_ref[...]))
        elif which == "down":
            block = _mm_tn(refs[0][...], refs[1][...])
        else:
            block = _mm(refs[0][...], refs[1][...])
        for k in range(n_far + 1):
            @pl.when(s_now == k)
            def _():
                send_buf[k] = block.astype(BF16)
                hand_over(k).start()

        for k in range(n_far):
            @pl.when(s_now == n_far + 1 + k)
            def _():
                hand_over(k).wait_recv()
                pair_ref[...] = (block + land_buf[k].astype(F32)).astype(BF16)

        @pl.when(s_now == N_DEV - 1)
        def _():
            own_ref[...] = block
            for k in range(n_far + 1):
                hand_over(k).wait_send()
            hand_over(n_far).wait_recv()

    rides = [] if ride is None else [ride]
    sems = lambda k: pltpu.SemaphoreType.DMA((k,))
    return pl.pallas_call(
        body, name=name,
        grid_spec=pltpu.PrefetchScalarGridSpec(
            num_scalar_prefetch=1, grid=(N_DEV,), in_specs=in_specs + [HBM_SPEC] * n_ride,
            out_specs=[pl.BlockSpec((rows, cols), lambda s, order_ref: (0, 0)), HBM_SPEC,
                       pl.BlockSpec((None, rows, cols), lambda s, order_ref: (jnp.clip(s - n_far - 1, 0, n_far - 1), 0, 0))]
            + [HBM_SPEC] * n_ride,
            scratch_shapes=[pltpu.VMEM((n_far + 1, rows, cols), BF16), pltpu.VMEM((n_far, rows, cols), BF16),
                            sems(n_far + 1), sems(n_far + 1)] + [sems(n_far), sems(n_far)] * n_ride),
        out_shape=[jax.ShapeDtypeStruct((rows, cols), F32), jax.ShapeDtypeStruct((rows, cols), BF16),
                   jax.ShapeDtypeStruct((n_far, rows, cols), BF16)]
        + [jax.ShapeDtypeStruct(r.shape, r.dtype) for r in rides],
        compiler_params=_params("arbitrary", barrier_id=barrier_id),
    )(order, *operands, *rides)


def _chip_exchange_beside(first, last, sums, outs, sems, enter=True):
    chips = _other_chips(*_mesh_pos())
    copies = [_push(sums[i].at[k], outs[i].at[k], sems, len(chips) * i + k, chip)
              for i in range(len(sums)) for k, chip in enumerate(chips)]

    @pl.when(first)
    def _():
        if enter:
            _enter_with(chips)
        for cp in copies:
            cp.start()

    @pl.when(last)
    def _():
        for cp in copies:
            cp.wait()


ROW_GCONV, ROW_CW0 = 1, 2


def _conv_bwd(dmixed, gates, g_conv, conv_w, tm):
    t = gates.shape[0]
    n = t // tm
    rev = lambda i: n - 1 - i

    def body(dm_ref, gates_ref, gprev_ref, gc_ref, cw_ref, dgates_ref, small_ref, carry_ref):
        i = pl.program_id(0)

        @pl.when(i == 0)
        def _():
            small_ref[...] = jnp.zeros_like(small_ref)
            carry_ref[...] = jnp.zeros_like(carry_ref)

        gates = gates_ref[...]
        gb, gcc, xin = gates[:, :CONV_W], gates[:, CONV_W:2 * CONV_W], gates[:, 2 * CONV_W:]
        u = gcc * xin
        gp = gprev_ref[...]
        uprev = jnp.where(rev(i) == 0, 0.0, gp[:, CONV_W:2 * CONV_W] * gp[:, 2 * CONV_W:])
        u1, u2 = _shift_rows_down(u, uprev, 1), _shift_rows_down(u, uprev, 2)
        w = cw_ref[...]
        c = _conv3(u, u1, u2, w)
        conv = gb * c
        rcv = _inv_rms(conv)
        c_hat = conv * rcv
        dconv, dgc = _rms_bwd(c_hat, rcv, gc_ref[...], dm_ref[...])
        dc = dconv * gb
        nxt = carry_ref[...]
        du = (w[2:3, :] * dc + w[1:2, :] * _shift_rows_up(dc, nxt, 1)) + w[0:1, :] * _shift_rows_up(dc, nxt, 2)
        carry_ref[...] = dc[0:8, :]
        dgates_ref[:, :CONV_W] = (dconv * c).astype(BF16)
        dgates_ref[:, CONV_W:2 * CONV_W] = (du * xin).astype(BF16)
        dgates_ref[:, 2 * CONV_W:] = (du * gcc).astype(BF16)
        small_ref[ROW_GCONV:ROW_GCONV + 1, :] += _colsum(dgc)
        small_ref[ROW_CW0:ROW_CW0 + 1, :] += _colsum(dc * u2)
        small_ref[ROW_CW0 + 1:ROW_CW0 + 2, :] += _colsum(dc * u1)
        small_ref[ROW_CW0 + 2:ROW_CW0 + 3, :] += _colsum(dc * u)

    tile = lambda w_: pl.BlockSpec((tm, w_), lambda i: (rev(i), 0))
    prev8 = pl.BlockSpec((8, GATES_W), lambda i: (jnp.maximum(rev(i) * (tm // 8) - 1, 0), 0))
    conv_half = pl.BlockSpec((tm, CONV_W), lambda i: (rev(i), ATTN_W // CONV_W))
    return pl.pallas_call(
        body, name="conv_bwd", grid=(n,),
        in_specs=[conv_half, tile(GATES_W), prev8, _full((1, CONV_W)), _full((3, CONV_W))],
        out_specs=[tile(GATES_W), _full((SMALL_ROWS, CONV_W))],
        out_shape=[jax.ShapeDtypeStruct((t, GATES_W), BF16), jax.ShapeDtypeStruct((SMALL_ROWS, CONV_W), F32)],
        scratch_shapes=[pltpu.VMEM((8, CONV_W), F32)],
        compiler_params=_params("arbitrary"),
    )(dmixed, gates, gates, g_conv, conv_w)


def _attn_bwd(qkv, dmixed, attn, g_attn, sinks, rope, sums):
    t = qkv.shape[0]
    n_steps = t // ATTN_STEP
    rev = lambda i: n_steps - 1 - i
    rc, rs1, rs2 = rope

    def body(sink_ref, q_ref, kp_ref, kc_ref, vp_ref, vc_ref, dm_ref, attn_ref, ga_ref, c_ref, s1_ref, s2_ref, sums_ref,
             dqkv_ref, dsink_ref, dgain_ref, arrived_ref, ck_ref, cv_ref, kacc_ref, vacc_ref, send_sems, recv_sems):
        i = pl.program_id(0)
        _chip_exchange_beside(i == 0, i == n_steps - 1, [sums_ref], [arrived_ref], (send_sems, recv_sems))

        @pl.when(i == 0)
        def _():
            dsink_ref[...] = jnp.zeros_like(dsink_ref)
            dgain_ref[...] = jnp.zeros_like(dgain_ref)
            ck_ref[...] = jnp.zeros_like(ck_ref)
            cv_ref[...] = jnp.zeros_like(cv_ref)

        kacc_ref[...] = jnp.zeros_like(kacc_ref)
        vacc_ref[...] = jnp.zeros_like(vacc_ref)
        a = attn_ref[...]
        ra = _inv_rms(a)
        dattn, dgain = _rms_bwd(a * ra, ra, ga_ref[...], dm_ref[...])
        dgain_ref[0:1, :] += _colsum(dgain)
        qt = (q_ref[...] * ATTN_SCALE).T
        dot = dattn.astype(BF16).T
        keys = jnp.concatenate([kp_ref[...], kc_ref[...]], axis=0)
        vals = jnp.concatenate([vp_ref[...], vc_ref[...]], axis=0)
        sink = [_group_sinks(sink_ref, g) for g in range(N_KV)]
        c, s1, s2 = c_ref[...], s1_ref[...], s2_ref[...]
        lane = lax.broadcasted_iota(jnp.int32, (1, 128), 1)
        dsink = jnp.zeros((1, 128), F32)
        masks = _attn_masks(rev(i) > 0)
        dq_parts = []
        for b in range(ATTN_STEP_BLOCKS):
            window = slice(BLOCK * b, BLOCK * (b + 2))
            valid = masks[b]
            dq_parts.append([])
            dk_parts, dv_parts = [], []
            for g in range(N_KV):
                gs = slice(HEAD_DIM * g, HEAD_DIM * (g + 1))
                kk, vv = keys[window, gs], vals[window, gs]
                qtg, dotg = _heads_side_by_side(qt, g, b), _heads_side_by_side(dot, g, b)
                probs, psink = _attn_probs(qtg, kk, sink[g], valid)
                dp = _mm(vv, dotg)
                delta = jnp.sum(probs * dp, axis=0, keepdims=True)
                ds = (probs * (dp - delta)).astype(BF16)
                sink_terms = psink * delta
                for hh in range(GROUP):
                    head_sum = jnp.sum(sink_terms[:, BLOCK * hh:BLOCK * (hh + 1)])
                    dsink = dsink + jnp.where(lane == GROUP * g + hh, -head_sum, 0.0)
                dq_parts[b].append(_mm_tn(kk * ATTN_SCALE, ds))
                dk_parts.append(_mm_nt(ds, qtg))
                dv_parts.append(_mm_nt(probs.astype(BF16), dotg))
            kacc_ref[window, :] += jnp.concatenate(dk_parts, axis=1)
            vacc_ref[window, :] += jnp.concatenate(dv_parts, axis=1)
        dq = _to_token_rows(dq_parts)
        for ci in range(ATTN_W // 128):
            sl = slice(128 * ci, 128 * (ci + 1))
            dqkv_ref[:, sl] = _rope_transpose(dq[:, sl], c, s1, s2).astype(BF16)
        kacc_ref[ATTN_STEP:, :] += ck_ref[...]
        vacc_ref[ATTN_STEP:, :] += cv_ref[...]
        ck_ref[...] = kacc_ref[:BLOCK, :]
        cv_ref[...] = vacc_ref[:BLOCK, :]
        dqkv_ref[:, ATTN_W:ATTN_W + KV_W] = _rope_transpose(kacc_ref[BLOCK:, :], c, s1, s2).astype(BF16)
        dqkv_ref[:, ATTN_W + KV_W:] = vacc_ref[BLOCK:, :].astype(BF16)
        dsink_ref[0:1, :] += dsink

    blk = lambda w_: pl.BlockSpec((ATTN_STEP, w_), lambda i: (rev(i), 0))
    return pl.pallas_call(
        body, name="attn_bwd", grid=(n_steps,),
        in_specs=[pl.BlockSpec(memory_space=pltpu.SMEM)] + _qkv_specs(rev)
        + [blk(ATTN_W), blk(ATTN_W), _full((1, ATTN_W)), blk(128), blk(128), blk(128), HBM_SPEC],
        out_specs=[blk(QKV_W), _full((8, 128)), _full((SMALL_ROWS, ATTN_W)), HBM_SPEC],
        out_shape=[jax.ShapeDtypeStruct((t, QKV_W), BF16), jax.ShapeDtypeStruct((8, 128), F32),
                   jax.ShapeDtypeStruct((SMALL_ROWS, ATTN_W), F32), jax.ShapeDtypeStruct(sums.shape, sums.dtype)],
        scratch_shapes=[pltpu.VMEM((BLOCK, KV_W), F32), pltpu.VMEM((BLOCK, KV_W), F32),
                        pltpu.VMEM((ATTN_KEYS, KV_W), F32), pltpu.VMEM((ATTN_KEYS, KV_W), F32),
                        pltpu.SemaphoreType.DMA((len(CHIP_FLIPS),)), pltpu.SemaphoreType.DMA((len(CHIP_FLIPS),))],
        compiler_params=_params("arbitrary", barrier_id=6),
    )(sinks, qkv, qkv, qkv, qkv, qkv, dmixed, attn, g_attn, rc, rs1, rs2, sums)


def _grad_x_tile(dq, dg, x_hat, r, g1, w_ref, dh):
    dhn = _mm_nt(dq, w_ref[:, :QKV_W]) + _mm_nt(dg, w_ref[:, QKV_W:])
    dx, dg1 = _rms_bwd(x_hat, r, g1, dhn)
    return dh + dx, _colsum(dg1)


def _in_proj_bwd(dqkv, dgates, x, dh, g1, w_in, tm, out_sums):
    t = x.shape[0]
    n = t // tm
    n_cover = max(n // 2, 1)
    n_steps = n + n_cover
    n_far = len(CHIP_FLIPS)
    shard = (D_MODEL, IN_SHARD)

    def body(dq_ref, dg_ref, x_ref, dh_ref, g1_ref, w_ref, osums_ref,
             dx_ref, own_ref, sib_ref, far_ref, dg1_ref, oarrived_ref,
             acc_ref, send_buf, land_buf, pair_buf, d2d_send, d2d_recv, ici_send, ici_recv, o_send, o_recv):
        i = pl.program_id(0)
        x_pos, y_pos, c = _mesh_pos()
        my_chip = 2 * x_pos + y_pos
        sibling = (x_pos, y_pos, 1 - c)
        @pl.when(i == 0)
        def _():
            _enter_with(_sibling_and_chips(x_pos, y_pos, c))

        _chip_exchange_beside(i == 0, i == n_steps - 1, [osums_ref], [oarrived_ref], (o_send, o_recv), enter=False)

        def cols(d):
            return slice(IN_SHARD * d, IN_SHARD * (d + 1))

        def hand_over(chip):
            return _push(send_buf.at[chip], land_buf.at[chip], (d2d_send, d2d_recv), chip, sibling)

        def to_chip(chip, rel):
            return pltpu.make_async_remote_copy(
                src_ref=pair_buf.at[chip], dst_ref=far_ref.at[rel - 1], send_sem=ici_send.at[rel - 1],
                recv_sem=ici_recv.at[rel - 1], device_id=(chip // 2, chip % 2, c), device_id_type=MESH)

        @pl.when(i == 0)
        def _():
            acc_ref[...] = jnp.zeros_like(acc_ref)
            dg1_ref[...] = jnp.zeros_like(dg1_ref)

        def normed_x():
            xv = x_ref[...]
            r = _inv_rms(xv)
            return xv * r, r

        @pl.when(i < n)
        def _():
            hn = (normed_x()[0] * g1_ref[...]).astype(BF16)
            acc_ref[:, :QKV_W] += _mm_tn(hn, dq_ref[...])
            acc_ref[:, QKV_W:] += _mm_tn(hn, dg_ref[...])

        @pl.when(i == n - 1)
        def _():
            for d in range(N_DEV):
                @pl.when(d % 2 != c)
                def _():
                    send_buf[d // 2] = acc_ref[:, cols(d)].astype(BF16)
                    hand_over(d // 2).start()
            for d in range(N_DEV):
                chip = d // 2

                @pl.when(d % 2 == c)
                def _():
                    hand_over(chip).wait_recv()

                    @pl.when(chip == my_chip)
                    def _():
                        own_ref[...] = acc_ref[:, cols(d)]
                        sib_ref[...] = land_buf[chip]

                    @pl.when(chip != my_chip)
                    def _():
                        pair_buf[chip] = (acc_ref[:, cols(d)] + land_buf[chip].astype(F32)).astype(BF16)
                        to_chip(chip, chip ^ my_chip).start()
            for chip in range(N_CHIPS):
                hand_over(chip).wait_send()

        @pl.when(i >= n)
        def _():
            x_hat, r = normed_x()
            dx_ref[...], dg1 = _grad_x_tile(dq_ref[...], dg_ref[...], x_hat, r, g1_ref[...], w_ref, dh_ref[...])
            dg1_ref[0:1, :] += dg1

        @pl.when(i == n_steps - 1)
        def _():
            for rel in range(1, n_far + 1):
                to_chip(0, rel).wait()

    both = lambda w_: pl.BlockSpec((tm, w_), lambda i: (i % n, 0))
    second = pl.BlockSpec((tm, D_MODEL), lambda i: (jnp.maximum(i - n, 0), 0))
    whole = lambda dtype: jax.ShapeDtypeStruct(shard, dtype)
    sems = lambda k: pltpu.SemaphoreType.DMA((k,))
    res = pl.pallas_call(
        body, name="in_proj_bwd", grid=(n_steps,),
        in_specs=[both(QKV_W), both(GATES_W), both(D_MODEL), second, _full((1, D_MODEL)), _resident((D_MODEL, IN_COLS)),
                  HBM_SPEC],
        out_specs=[second, _full(shard), _full(shard), HBM_SPEC, _full((SMALL_ROWS, D_MODEL)), HBM_SPEC],
        out_shape=[jax.ShapeDtypeStruct((n_cover * tm, D_MODEL), F32), whole(F32), whole(BF16),
                   jax.ShapeDtypeStruct((n_far,) + shard, BF16), jax.ShapeDtypeStruct((SMALL_ROWS, D_MODEL), F32),
                   jax.ShapeDtypeStruct(out_sums.shape, out_sums.dtype)],
        scratch_shapes=[pltpu.VMEM((D_MODEL, IN_COLS), F32), pltpu.VMEM((N_CHIPS,) + shard, BF16),
                        pltpu.VMEM((N_CHIPS,) + shard, BF16), pltpu.VMEM((N_CHIPS,) + shard, BF16),
                        sems(N_CHIPS), sems(N_CHIPS), sems(n_far), sems(n_far), sems(n_far), sems(n_far)],
        compiler_params=_params("arbitrary", barrier_id=7),
    )(dqkv, dgates, x, dh, g1, w_in, out_sums)
    return res[0], (res[1], res[2], res[3]), res[4], res[5]


def _grad_x_rest(dqkv, dgates, x, dh, g1, w_in, tm, head, dg1_rows):
    t = x.shape[0]
    first = head.shape[0] // tm
    n_rest = t // tm - first
    if n_rest == 0:
        return head, dg1_rows
    assert first <= n_rest

    def body(dq_ref, dg_ref, x_ref, dh_ref, g1_ref, w_ref, head_ref, rows_ref, gx_ref, dg1_ref, stage, sems):
        j = pl.program_id(0)

        def tile_out(step, kind):
            row0 = (step + first) * tm if kind == 0 else step * tm
            slot = 2 * kind + step % 2
            return pltpu.make_async_copy(stage.at[slot], gx_ref.at[pl.ds(pl.multiple_of(row0, tm), tm), :], sems.at[slot])

        @pl.when(j == 0)
        def _():
            dg1_ref[...] = rows_ref[...]

        @pl.when(j >= 2)
        def _():
            tile_out(j - 2, 0).wait()

        @pl.when((j >= 2) & (j - 2 < first))
        def _():
            tile_out(j - 2, 1).wait()

        @pl.when(j < first)
        def _():
            stage[2 + j % 2] = head_ref[...]
            tile_out(j, 1).start()

        xv = x_ref[...]
        r = _inv_rms(xv)
        dx, dg1 = _grad_x_tile(dq_ref[...], dg_ref[...], xv * r, r, g1_ref[...], w_ref, dh_ref[...])
        stage[j % 2] = dx
        dg1_ref[0:1, :] += dg1
        tile_out(j, 0).start()

        @pl.when(j == n_rest - 1)
        def _():
            for back in range(min(2, n_rest)):
                tile_out(j - back, 0).wait()

                @pl.when(j - back < first)
                def _():
                    tile_out(j - back, 1).wait()

    tile = lambda w_: pl.BlockSpec((tm, w_), lambda j: (j + first, 0))
    head_tile = pl.BlockSpec((tm, D_MODEL), lambda j: (jnp.minimum(j, first - 1), 0))
    return pl.pallas_call(
        body, name="grad_x_rest", grid=(n_rest,),
        in_specs=[tile(QKV_W), tile(GATES_W), tile(D_MODEL), tile(D_MODEL), _full((1, D_MODEL)),
                  _resident((D_MODEL, IN_COLS)), head_tile, _full((SMALL_ROWS, D_MODEL))],
        out_specs=[HBM_SPEC, _full((SMALL_ROWS, D_MODEL))],
        out_shape=[jax.ShapeDtypeStruct((t, D_MODEL), F32), jax.ShapeDtypeStruct((SMALL_ROWS, D_MODEL), F32)],
        scratch_shapes=[pltpu.VMEM((4, tm, D_MODEL), F32), pltpu.SemaphoreType.DMA((4,))],
        compiler_params=_params("arbitrary"),
    )(dqkv, dgates, x, dh, g1, w_in, head, dg1_rows)


def _all_gather(shards, name):
    n = len(shards)

    def body(*refs):
        _enter_with(_sibling_and_chips(*_mesh_pos()))
        start, finish = _gather_steps(refs[:n], refs[n:2 * n], *refs[2 * n:])
        start()
        finish()

    return pl.pallas_call(
        body, name=name,
        in_specs=[HBM_SPEC] * n, out_specs=[HBM_SPEC] * n,
        out_shape=[jax.ShapeDtypeStruct((N_DEV,) + s.shape, s.dtype) for s in shards],
        scratch_shapes=[pltpu.SemaphoreType.DMA((7 * n,)), pltpu.SemaphoreType.DMA((7 * n,)),
                        pltpu.SemaphoreType.DMA((n,))],
        compiler_params=_params(barrier_id=8),
    )(*shards)


def _adam_math(w, g, m, v):
    m = ADAM_B1 * m + (1.0 - ADAM_B1) * g
    v = ADAM_B2 * v + (1.0 - ADAM_B2) * (g * g)
    m_hat = m / (1.0 - ADAM_B1 ** ADAM_STEP)
    v_hat = v / (1.0 - ADAM_B2 ** ADAM_STEP)
    delta = -ADAM_LR * (m_hat / (jnp.sqrt(v_hat) + ADAM_EPS) + ADAM_WD * w)
    return delta, m, v


def _adamw_reduced(w, m, v, own, from_sibling, from_chips, tr):
    rows, cols = w.shape

    def body(w_ref, m_ref, v_ref, own_ref, sib_ref, far_ref, g_ref, d_ref, nm_ref, nv_ref):
        g = own_ref[...] + sib_ref[...].astype(F32)
        for k in range(len(CHIP_FLIPS)):
            g = g + far_ref[k].astype(F32)
        g_ref[...] = g
        d_ref[...], nm_ref[...], nv_ref[...] = _adam_math(w_ref[...], g, m_ref[...], v_ref[...])

    tile = pl.BlockSpec((tr, cols), lambda i: (i, 0))
    out = jax.ShapeDtypeStruct((rows, cols), F32)
    return pl.pallas_call(
        body, name="adamw_reduced", grid=(rows // tr,),
        in_specs=[tile] * 5 + [pl.BlockSpec((len(CHIP_FLIPS), tr, cols), lambda i: (0, i, 0))],
        out_specs=[tile] * 4, out_shape=[out] * 4,
        compiler_params=_params("parallel"),
    )(w, m, v, own, from_sibling, from_chips)


SMALL_PARAMS = ("pre_mix_norm", "post_mix_norm", "pre_mlp_norm", "post_mlp_norm", "attn_group_norm", "conv_group_norm",
                "conv_w", "attn_sinks")


def _small_tail(gathered, dev, weights, first_moments, second_moments):
    n = len(SMALL_PARAMS)
    conv_shard = CONV_W // N_DEV

    def body(dev_ref, mid_ref, conv_ref, gain_ref, sink_ref, in_ref, *refs):
        w_refs, m_refs, v_refs = refs[:n], refs[n:2 * n], refs[2 * n:3 * n]
        loss_ref, outs = refs[3 * n], refs[3 * n + 1:]

        def total(ref):
            acc = ref[0]
            for d in range(1, N_DEV):
                acc = acc + ref[d]
            return acc

        mid, conv, gain, sink, inp = total(mid_ref), total(conv_ref), total(gain_ref), total(sink_ref), total(in_ref)
        loss_ref[...] = (0.5 / D_MODEL) * jnp.sum(mid[ROW_LOSS:ROW_LOSS + 1, :], axis=1, keepdims=True)
        conv_rows = conv[ROW_CW0:ROW_CW0 + 3, :]
        conv_g = jnp.zeros((3, conv_shard), F32)
        for d in range(N_DEV):
            conv_g = conv_g + jnp.where(dev_ref[0] == d, conv_rows[:, conv_shard * d:conv_shard * (d + 1)], 0.0)
        grads = [inp[0:1, :], mid[ROW_G2:ROW_G2 + 1, :], mid[ROW_G3:ROW_G3 + 1, :], mid[ROW_G4:ROW_G4 + 1, :],
                 gain[0:1, :], conv[ROW_GCONV:ROW_GCONV + 1, :], conv_g, sink[0:1, :N_HEADS]]
        for i, g in enumerate(grads):
            delta, new_m, new_v = _adam_math(w_refs[i][...], g, m_refs[i][...], v_refs[i][...])
            outs[i][...], outs[n + i][...], outs[2 * n + i][...], outs[3 * n + i][...] = g, delta, new_m, new_v

    params = list(weights) + list(first_moments) + list(second_moments)
    shapes = [jax.ShapeDtypeStruct(w.shape, F32) for w in weights]
    res = pl.pallas_call(
        body, name="small_tail", grid=(1,),
        in_specs=[pl.BlockSpec(memory_space=pltpu.SMEM)] + [_full(g.shape) for g in gathered] + [_full(p.shape) for p in params],
        out_specs=[_full((1, 1))] + [_full(sh.shape) for sh in shapes] * 4,
        out_shape=[jax.ShapeDtypeStruct((1, 1), F32)] + shapes * 4,
    )(dev, *gathered, *params)
    return res[0], [res[1 + k * n:1 + (k + 1) * n] for k in range(4)]


TOKEN_TILE = 512
MID_TILE = 256
MID_CHUNK = 1024
MID_CHUNKS = D_FF // MID_CHUNK
ADAM_ROWS = 128


def _local_grads(x, target, g1, w_in_shard, conv_shard, sinks, g_attn, g_conv, g2, g3, g4, shards, order):
    t = x.shape[0]
    tm = min(TOKEN_TILE, t)
    rope = _rope_tables(t)
    qkv, gates, mconv, w_in, conv_w, gathered = _in_proj_fwd(x, g1, w_in_shard, conv_shard, g_conv, rope, tm, shards[:2],
                                                             (False, True))
    attn, mattn, (w_out, w_up, w_down) = _attn_fwd(qkv, sinks, g_attn, shards, gathered, (False,))
    act, dup, hn2t, dmo, dmix, dh, dmixed, small_mid = _mid(
        mattn, mconv, x, target, g2, g3, g4, w_out.reshape(D_MODEL, D_MODEL),
        w_up, w_down.reshape(D_FF, D_MODEL), min(MID_TILE, t))
    up_own, up_sib, up_sums = _dw_pair_sums((hn2t, dup), order, "up", "dw_up", 2)
    down_own, down_sib, down_sums, up_far = _dw_pair_sums((act, dmo), order, "down", "dw_down", 3, ride=up_sums)
    out_own, out_sib, out_sums = _dw_pair_sums((mattn, mconv, dmix), order, "out", "dw_out", 4)
    dgates, small_conv = _conv_bwd(dmixed, gates, g_conv, conv_w, tm)
    dqkv, dsink, dg_attn, down_far = _attn_bwd(qkv, dmixed, attn, g_attn, sinks, rope, down_sums)
    grad_x_head, dw_in, small_in, out_far = _in_proj_bwd(dqkv, dgates, x, dh, g1, w_in, tm, out_sums)
    grad_x, small_in = _grad_x_rest(dqkv, dgates, x, dh, g1, w_in, tm, grad_x_head, small_in)
    dw_out, dw_up, dw_down = (out_own, out_sib, out_far), (up_own, up_sib, up_far), (down_own, down_sib, down_far)
    return grad_x, dw_in, dw_out, dw_up, dw_down, (small_mid, small_conv, dg_attn, dsink, small_in)


def kernel(x, pre_mix_norm, w_in, conv_w, attn_sinks, attn_group_norm, conv_group_norm, w_out, post_mix_norm, pre_mlp_norm, w_up, w_down, post_mlp_norm, loss_target, m_pre_mix_norm, m_w_in, m_conv_w, m_attn_sinks, m_attn_group_norm, m_conv_group_norm, m_w_out, m_post_mix_norm, m_pre_mlp_norm, m_w_up, m_w_down, m_post_mlp_norm, v_pre_mix_norm, v_w_in, v_conv_w, v_attn_sinks, v_attn_group_norm, v_conv_group_norm, v_w_out, v_post_mix_norm, v_pre_mlp_norm, v_w_up, v_w_down, v_post_mlp_norm):
    xi, yi, ci = _mesh_pos()
    chip = 2 * xi + yi
    dev = 2 * chip + ci

    order = _block_order(dev)

    shards = [w_out[0].astype(BF16), w_up[0].astype(BF16), w_down[0].astype(BF16)]

    grad_x, dw_in, dw_out, dw_up, dw_down, smalls = _local_grads(
        x[0], loss_target[0], pre_mix_norm, w_in[0].astype(BF16), conv_w[0], attn_sinks, attn_group_norm, conv_group_norm,
        post_mix_norm, pre_mlp_norm, post_mlp_norm, shards, order)

    big = {}
    for name, w, m, v, (own, sib, far) in zip(
            ("w_in", "w_out", "w_up", "w_down"), (w_in, w_out, w_up, w_down), (m_w_in, m_w_out, m_w_up, m_w_down),
            (v_w_in, v_w_out, v_w_up, v_w_down), (dw_in, dw_out, dw_up, dw_down)):
        big[name] = [a[None] for a in _adamw_reduced(w[0], m[0], v[0], own, sib, far, ADAM_ROWS)]

    flat = lambda a: a.reshape(-1, a.shape[-1])
    loss, small = _small_tail(
        _all_gather(list(smalls), "gather_small"), dev.reshape(1).astype(jnp.int32),
        [flat(a) for a in (pre_mix_norm, post_mix_norm, pre_mlp_norm, post_mlp_norm, attn_group_norm, conv_group_norm,
                           conv_w, attn_sinks)],
        [flat(a) for a in (m_pre_mix_norm, m_post_mix_norm, m_pre_mlp_norm, m_post_mlp_norm, m_attn_group_norm,
                           m_conv_group_norm, m_conv_w, m_attn_sinks)],
        [flat(a) for a in (v_pre_mix_norm, v_post_mix_norm, v_pre_mlp_norm, v_post_mlp_norm, v_attn_group_norm,
                           v_conv_group_norm, v_conv_w, v_attn_sinks)])

    order = ("pre_mix_norm", "w_in", "conv_w", "attn_sinks", "attn_group_norm", "conv_group_norm", "w_out",
             "post_mix_norm", "pre_mlp_norm", "w_up", "w_down", "post_mlp_norm")
    shape_of = {"conv_w": conv_w.shape}
    outs = []
    for k in range(4):
        by_name = dict(zip(SMALL_PARAMS, small[k]))
        outs += [big[nm][k] if nm in big else by_name[nm].reshape(shape_of.get(nm, by_name[nm].shape)) for nm in order]
    loss = loss.reshape(())
    return (loss, grad_x[None], *outs)
```

```python
import jax
import jax.numpy as jnp
import numpy as np
from jax import lax
from jax.experimental import pallas as pl
from jax.experimental.pallas import tpu as pltpu

F32 = jnp.float32
BF16 = jnp.bfloat16

D_MODEL = 1024
HEAD_DIM = 64
ATTN_W = 512
CONV_W = 512
N_HEADS = 8
N_KV = 2
GROUP = 4
KV_W = 128
QKV_W = ATTN_W + 2 * KV_W
GATES_W = 3 * CONV_W
IN_COLS = QKV_W + GATES_W
D_FF = 4096
FF_CHUNK = 512
BLOCK = 128
ROT_HALF = 8
ROPE_THETA = 500000.0
NORM_EPS = 1e-6
NEG_INF = -1e30
ATTN_SCALE = 0.125
N_DEV = 8
N_CHIPS = 4
IN_SHARD = IN_COLS // N_DEV

ADAM_LR = 0.001
ADAM_B1 = 0.9
ADAM_B2 = 0.999
ADAM_EPS = 1e-08
ADAM_WD = 0.01
ADAM_STEP = 10

V7X_VMEM_BYTES = 64 * 1024 * 1024
VMEM_LIMIT = V7X_VMEM_BYTES - 2 * 1024 * 1024

MESH = pl.DeviceIdType.MESH
HBM_SPEC = pl.BlockSpec(memory_space=pltpu.HBM)


def _params(*sem, barrier_id=None):
    return pltpu.CompilerParams(dimension_semantics=sem or None, vmem_limit_bytes=VMEM_LIMIT, collective_id=barrier_id)


def _mm(a, b):
    return jnp.dot(a, b, preferred_element_type=F32)


def _mm_nt(a, b):
    return lax.dot_general(a, b, (((1,), (1,)), ((), ())), preferred_element_type=F32)


def _mm_tn(a, b):
    return lax.dot_general(a, b, (((0,), (0,)), ((), ())), preferred_element_type=F32)


def _inv_rms(x):
    return lax.rsqrt(jnp.mean(x * x, axis=-1, keepdims=True) + NORM_EPS)


def _rms_bwd(xhat, r, gain, dy):
    gy = dy * gain
    return r * (gy - xhat * jnp.mean(gy * xhat, axis=-1, keepdims=True)), dy * xhat


def _colsum(a):
    return jnp.sum(a, axis=0, keepdims=True)


def _full(shape):
    zeros = (0,) * len(shape)
    return pl.BlockSpec(shape, lambda *_: zeros)


def _resident(shape):
    zeros = (0,) * len(shape)
    return pl.BlockSpec(shape, lambda *_: zeros, pipeline_mode=pl.Buffered(1))


def _rope_tables(t):
    pos = np.arange(t, dtype=np.float32)
    inv_freq = (ROPE_THETA ** (-np.arange(0, 2 * ROT_HALF, 2, dtype=np.float64) / (2 * ROT_HALF))).astype(np.float32)
    ang = (pos[:, None] * inv_freq[None, :]).astype(np.float64)
    cos, sin = np.cos(ang).astype(np.float32), np.sin(ang).astype(np.float32)
    zeros8 = np.zeros((t, ROT_HALF), np.float32)
    rest = np.zeros((t, HEAD_DIM - 2 * ROT_HALF), np.float32)
    c_head = np.concatenate([cos, cos, rest + 1.0], axis=1)
    s1_head = np.concatenate([zeros8, sin, rest], axis=1)
    s2_head = np.concatenate([-sin, zeros8, rest], axis=1)
    two = lambda a: jnp.asarray(np.concatenate([a, a], axis=1))
    return two(c_head), two(s1_head), two(s2_head)


def _rope(v, c, s1, s2):
    return v * c + pltpu.roll(v, ROT_HALF, 1) * s1 + pltpu.roll(v, 128 - ROT_HALF, 1) * s2


def _rope_transpose(dv, c, s1, s2):
    return dv * c + pltpu.roll(dv * s1, 128 - ROT_HALF, 1) + pltpu.roll(dv * s2, ROT_HALF, 1)


def _shift_rows_down(u, prev, k):
    row = lax.broadcasted_iota(jnp.int32, u.shape, 0)
    out = pltpu.roll(u, k, 0)
    for r in range(k):
        out = jnp.where(row == r, prev[8 - k + r:8 - k + r + 1, :], out)
    return out


def _shift_rows_up(u, nxt, k):
    n = u.shape[0]
    row = lax.broadcasted_iota(jnp.int32, u.shape, 0)
    out = pltpu.roll(u, n - k, 0)
    for r in range(k):
        out = jnp.where(row == n - k + r, nxt[r:r + 1, :], out)
    return out


def _conv3(u, u1, u2, w):
    return (w[0:1, :] * u2 + w[1:2, :] * u1) + w[2:3, :] * u


def _mesh_pos():
    return lax.axis_index("x"), lax.axis_index("y"), lax.axis_index("c")


def _slot(ref, pos):
    dev = 4 * pos[0] + 2 * pos[1] + pos[2]
    if len(ref.shape) == 2:
        width = ref.shape[1] // N_DEV
        return ref.at[:, pl.ds(pl.multiple_of(dev * width, width), width)]
    return ref.at[dev]


def _gathered_shape(shard, by_cols):
    if by_cols:
        return jax.ShapeDtypeStruct((shard.shape[0], N_DEV * shard.shape[1]), shard.dtype)
    return jax.ShapeDtypeStruct((N_DEV,) + shard.shape, shard.dtype)


def _enter_with(peers):
    barrier = pltpu.get_barrier_semaphore()
    for peer in peers:
        pl.semaphore_signal(barrier, inc=1, device_id=peer, device_id_type=MESH)
    pl.semaphore_wait(barrier, len(peers))


def _sibling_and_chips(x, y, c):
    return [(x, y, 1 - c), (1 - x, y, c), (x, 1 - y, c), (1 - x, 1 - y, c)]


def _push(src, dst, sems, k, to):
    send_sems, recv_sems = sems
    return pltpu.make_async_remote_copy(src_ref=src, dst_ref=dst, send_sem=send_sems.at[k], recv_sem=recv_sems.at[k],
                                        device_id=to, device_id_type=MESH)


def _gather_steps(shards, outs, send_sems, recv_sems, local_sems):
    n = len(shards)
    x, y, c = _mesh_pos()
    me, sibling = (x, y, c), (x, y, 1 - c)
    chips = [(1 - x, y), (x, 1 - y), (1 - x, 1 - y)]

    def copy(i, k, block, to, src=None):
        dst = _slot(outs[i], block)
        return _push(dst if src is None else src, dst, (send_sems, recv_sems), 7 * i + k, to)

    mine = [pltpu.make_async_copy(shards[i], _slot(outs[i], me), local_sems.at[i]) for i in range(n)]
    first = []
    for i in range(n):
        first.append(copy(i, 0, me, sibling, src=shards[i]))
        first += [copy(i, 1 + j, me, (*chip, c), src=shards[i]) for j, chip in enumerate(chips)]

    def start():
        for cp in mine + first:
            cp.start()

    def finish():
        passed = []
        for j, chip in enumerate(chips):
            for i in range(n):
                copy(i, 1 + j, (*chip, c), me).wait_recv()
                cp = copy(i, 4 + j, (*chip, c), sibling)
                cp.start()
                passed.append(cp)
        for i in range(n):
            copy(i, 0, sibling, me).wait_recv()
            for j, chip in enumerate(chips):
                copy(i, 4 + j, (*chip, 1 - c), me).wait_recv()
        for cp in first + passed:
            cp.wait_send()
        for cp in mine:
            cp.wait()

    return start, finish


def _gather_near(first, last, shards, outs, sems, local_sems):
    x, y, c = _mesh_pos()
    me, peers = (x, y, c), [(x, y, 1 - c), (1 - x, y, c), (x, 1 - y, c)]
    n = len(shards)
    local = [pltpu.make_async_copy(shards[i], _slot(outs[i], me), local_sems.at[i]) for i in range(n)]
    sends = [_push(shards[i], _slot(outs[i], me), sems, 3 * i + k, peers[k]) for i in range(n) for k in range(3)]
    arrivals = [_push(shards[i], _slot(outs[i], peers[k]), sems, 3 * i + k, peers[k]) for i in range(n) for k in range(3)]

    def start():
        for cp in local + sends:
            cp.start()

    if first is not None:
        pl.when(first)(start)

    @pl.when(last)
    def _():
        for cp in sends:
            cp.wait_send()
        for cp in arrivals:
            cp.wait_recv()
        for cp in local:
            cp.wait()

    return start


def _relay_route(x, y, c):
    south = c == 0
    via = (jnp.where(south, 1 - x, x), jnp.where(south, y, 1 - y))
    to = (jnp.where(south, x, 1 - x), jnp.where(south, 1 - y, y))
    return via, to


def _gather_far(first, middle, last, shards, ins, outs, sems):
    x, y, c = _mesh_pos()
    sibling = (x, y, 1 - c)
    chips = [(1 - x, y), (x, 1 - y), (1 - x, 1 - y)]
    via, to = _relay_route(x, y, c)
    n = len(shards)
    diag_send = [_push(_slot(ins[i], (*via, c)), _slot(outs[i], (*via, c)), sems, 4 * i, (*to, c)) for i in range(n)]
    diag_arrival = [_push(shards[i], _slot(outs[i], (*chips[2], c)), sems, 4 * i, (*to, c)) for i in range(n)]
    passed = [[_push(_slot(ins[i], (*chips[j], c)), _slot(outs[i], (*chips[j], c)), sems, 4 * i + 1 + j, sibling)
               for i in range(n)] for j in range(3)]
    from_sibling = [_push(shards[i], _slot(outs[i], (*chips[j], 1 - c)), sems, 4 * i + 1 + j, sibling)
                    for i in range(n) for j in range(3)]

    @pl.when(first)
    def _():
        for cp in diag_send + passed[0] + passed[1]:
            cp.start()

    @pl.when(middle)
    def _():
        for cp in diag_arrival:
            cp.wait_recv()
        for cp in passed[2]:
            cp.start()

    @pl.when(last)
    def _():
        for cp in from_sibling:
            cp.wait_recv()
        for cp in diag_send + passed[0] + passed[1] + passed[2]:
            cp.wait_send()


def _in_proj_fwd(x, g1, w_in, conv_w, g_conv, rope, tm, shards, by_cols):
    t = x.shape[0]
    rc, rs1, rs2 = rope
    n = len(shards)
    n_tiles = t // tm

    def body(*refs):
        x_ref, g1_ref, w_ref, cw_ref, gc_ref, c_ref, s1_ref, s2_ref = refs[:8]
        shard_refs = refs[8:8 + n]
        qkv_ref, gates_ref, mconv_ref, w_full_ref, cw_full_ref = refs[8 + n:13 + n]
        gathered = refs[13 + n:13 + 2 * n]
        carry_ref, w_land, cw_land, hn_ref = refs[13 + 2 * n:17 + 2 * n]
        now_sems = refs[17 + 2 * n:20 + 2 * n]
        step = pl.program_id(0)
        start_later_weights = _gather_near(None, step == 2 * n_tiles - 1, shard_refs, gathered,
                                           refs[20 + 2 * n:22 + 2 * n], refs[22 + 2 * n]) if n else None
        start_w_in, finish_w_in = _gather_steps([w_ref, cw_ref], [w_land, cw_land], *now_sems)

        @pl.when(step == 0)
        def _():
            carry_ref[...] = jnp.zeros_like(carry_ref)
            _enter_with(_sibling_and_chips(*_mesh_pos()))
            start_w_in()
            if start_later_weights is not None:
                start_later_weights()

        @pl.when(step < n_tiles)
        def _():
            xv = x_ref[...]
            hn_ref[step] = ((xv * _inv_rms(xv)) * g1_ref[...]).astype(BF16)

        @pl.when(step == n_tiles)
        def _():
            finish_w_in()
            conv_shard = CONV_W // N_DEV
            for d in range(N_DEV):
                w_full_ref[:, IN_SHARD * d:IN_SHARD * (d + 1)] = w_land[d]
                cw_full_ref[:, conv_shard * d:conv_shard * (d + 1)] = cw_land[d]

        @pl.when(step >= n_tiles)
        def _():
            proj = _mm(hn_ref[step - n_tiles], w_full_ref[...])
            c, s1, s2 = c_ref[...], s1_ref[...], s2_ref[...]
            for ci in range((ATTN_W + KV_W) // 128):
                sl = slice(128 * ci, 128 * (ci + 1))
                qkv_ref[:, sl] = _rope(proj[:, sl], c, s1, s2).astype(BF16)
            qkv_ref[:, ATTN_W + KV_W:QKV_W] = proj[:, ATTN_W + KV_W:QKV_W].astype(BF16)
            gates = proj[:, QKV_W:]
            gates_ref[...] = gates
            gb, gcc, xin = gates[:, :CONV_W], gates[:, CONV_W:2 * CONV_W], gates[:, 2 * CONV_W:]
            u = gcc * xin
            prev = carry_ref[...]
            conv = gb * _conv3(u, _shift_rows_down(u, prev, 1), _shift_rows_down(u, prev, 2), cw_full_ref[...])
            carry_ref[...] = u[tm - 8:tm, :]
            mconv_ref[...] = ((conv * _inv_rms(conv)) * gc_ref[...]).astype(BF16)

    first_pass = pl.BlockSpec((tm, D_MODEL), lambda i: (jnp.minimum(i, n_tiles - 1), 0))
    tile = lambda w_: pl.BlockSpec((tm, w_), lambda i: (jnp.maximum(i - n_tiles, 0), 0))
    sems = lambda k: pltpu.SemaphoreType.DMA((k,))
    res = pl.pallas_call(
        body, name="in_proj_fwd", grid=(2 * n_tiles,),
        in_specs=[first_pass, _full((1, D_MODEL)), HBM_SPEC, HBM_SPEC, _full((1, CONV_W)), tile(128), tile(128),
                  tile(128)] + [HBM_SPEC] * n,
        out_specs=[tile(QKV_W), tile(GATES_W), tile(CONV_W), _full((D_MODEL, IN_COLS)), _full((3, CONV_W))]
        + [HBM_SPEC] * n,
        out_shape=[jax.ShapeDtypeStruct((t, QKV_W), BF16), jax.ShapeDtypeStruct((t, GATES_W), F32),
                   jax.ShapeDtypeStruct((t, CONV_W), BF16), jax.ShapeDtypeStruct((D_MODEL, IN_COLS), BF16),
                   jax.ShapeDtypeStruct((3, CONV_W), F32)]
        + [_gathered_shape(s, cols) for s, cols in zip(shards, by_cols)],
        scratch_shapes=[pltpu.VMEM((8, CONV_W), F32), pltpu.VMEM((N_DEV,) + w_in.shape, BF16),
                        pltpu.VMEM((N_DEV,) + conv_w.shape, F32), pltpu.VMEM((n_tiles, tm, D_MODEL), BF16),
                        sems(14), sems(14), sems(2)]
        + ([sems(3 * n), sems(3 * n), sems(n)] if n else []),
        compiler_params=_params("arbitrary", barrier_id=0),
    )(x, g1, w_in, conv_w, g_conv, rc, rs1, rs2, *shards)
    return res[0], res[1], res[2], res[3], res[4], list(res[5:])


GROUP_COLS = GROUP * BLOCK
ATTN_STEP_BLOCKS = 4


def _attn_masks(has_prev):
    key = lax.broadcasted_iota(jnp.int32, (2 * BLOCK, GROUP_COLS), 0)
    query = lax.broadcasted_iota(jnp.int32, (2 * BLOCK, GROUP_COLS), 1) & (BLOCK - 1)
    band = (key > query) & (key <= query + BLOCK)
    return [band & ((key >= BLOCK) | has_prev)] + [band] * (ATTN_STEP_BLOCKS - 1)


def _heads_side_by_side(at, g, b):
    heads = [at[HEAD_DIM * (GROUP * g + hh):HEAD_DIM * (GROUP * g + hh + 1), BLOCK * b:BLOCK * (b + 1)] for hh in range(GROUP)]
    return jnp.concatenate(heads, axis=1)


def _to_token_rows(parts):
    rows = [jnp.concatenate([parts[b][g][:, BLOCK * hh:BLOCK * (hh + 1)] for b in range(ATTN_STEP_BLOCKS)], axis=1)
            for g in range(N_KV) for hh in range(GROUP)]
    return jnp.concatenate(rows, axis=0).T


def _group_sinks(sink_ref, g):
    head = lax.broadcasted_iota(jnp.int32, (1, GROUP_COLS), 1) // BLOCK
    out = jnp.full((1, GROUP_COLS), sink_ref[0, GROUP * g], F32)
    for hh in range(1, GROUP):
        out = jnp.where(head == hh, sink_ref[0, GROUP * g + hh], out)
    return out


def _attn_probs(qt, kk, sink, valid):
    s = jnp.where(valid, _mm(kk, qt), NEG_INF)
    m = jnp.maximum(jnp.max(s, axis=0, keepdims=True), sink)
    p = jnp.exp(s - m)
    psink = jnp.exp(sink - m)
    inv_l = 1.0 / (jnp.sum(p, axis=0, keepdims=True) + psink)
    return p * inv_l, psink * inv_l


ATTN_STEP = ATTN_STEP_BLOCKS * BLOCK
ATTN_KEYS = ATTN_STEP + BLOCK


def _qkv_specs(order):
    prev = lambda i: jnp.maximum(ATTN_STEP_BLOCKS * order(i) - 1, 0)
    kcol, vcol = ATTN_W // KV_W, ATTN_W // KV_W + 1
    return [pl.BlockSpec((ATTN_STEP, ATTN_W), lambda i: (order(i), 0)),
            pl.BlockSpec((BLOCK, KV_W), lambda i: (prev(i), kcol)), pl.BlockSpec((ATTN_STEP, KV_W), lambda i: (order(i), kcol)),
            pl.BlockSpec((BLOCK, KV_W), lambda i: (prev(i), vcol)), pl.BlockSpec((ATTN_STEP, KV_W), lambda i: (order(i), vcol))]


def _attn_fwd(qkv, sinks, g_attn, shards, gathered):
    t = qkv.shape[0]
    n = len(shards)

    def body(*refs):
        sink_ref, q_ref, kp_ref, kc_ref, vp_ref, vc_ref, ga_ref = refs[:7]
        attn_ref, mattn_ref = refs[7 + 2 * n:9 + 2 * n]
        step = pl.program_id(0)
        if n:
            @pl.when(step == 0)
            def _():
                x, y, c = _mesh_pos()
                _enter_with([(x, y, 1 - c), (*_relay_route(x, y, c)[1], c)])

            n_steps = t // ATTN_STEP
            _gather_far(step == 0, step == n_steps // 2, step == n_steps - 1, refs[7:7 + n], refs[7 + n:7 + 2 * n],
                        refs[9 + 2 * n:9 + 3 * n], refs[9 + 3 * n:11 + 3 * n])
        qt = (q_ref[...] * ATTN_SCALE).T
        keys = jnp.concatenate([kp_ref[...], kc_ref[...]], axis=0)
        vals = jnp.concatenate([vp_ref[...], vc_ref[...]], axis=0)
        sink = [_group_sinks(sink_ref, g) for g in range(N_KV)]
        masks = _attn_masks(step > 0)
        parts = []
        for b in range(ATTN_STEP_BLOCKS):
            window = slice(BLOCK * b, BLOCK * (b + 2))
            valid = masks[b]
            parts.append([])
            for g in range(N_KV):
                gs = slice(HEAD_DIM * g, HEAD_DIM * (g + 1))
                probs, _ = _attn_probs(_heads_side_by_side(qt, g, b), keys[window, gs], sink[g], valid)
                parts[b].append(_mm_tn(vals[window, gs], probs.astype(BF16)))
        attn = _to_token_rows(parts)
        attn_ref[...] = attn
        mattn_ref[...] = ((attn * _inv_rms(attn)) * ga_ref[...]).astype(BF16)

    blk = pl.BlockSpec((ATTN_STEP, ATTN_W), lambda j: (j, 0))
    res = pl.pallas_call(
        body, name="attn_fwd", grid=(t // ATTN_STEP,),
        in_specs=[pl.BlockSpec(memory_space=pltpu.SMEM)] + _qkv_specs(lambda j: j) + [_full((1, ATTN_W))]
        + [HBM_SPEC] * (2 * n),
        out_specs=[blk, blk] + [HBM_SPEC] * n,
        out_shape=[jax.ShapeDtypeStruct((t, ATTN_W), F32), jax.ShapeDtypeStruct((t, ATTN_W), BF16)]
        + [jax.ShapeDtypeStruct(g.shape, g.dtype) for g in gathered],
        input_output_aliases={7 + n + i: 2 + i for i in range(n)},
        scratch_shapes=[pltpu.SemaphoreType.DMA((4 * n,)), pltpu.SemaphoreType.DMA((4 * n,))] if n else [],
        compiler_params=_params("arbitrary", barrier_id=1 if n else None),
    )(sinks, qkv, qkv, qkv, qkv, qkv, g_attn, *shards, *gathered)
    return res[0], res[1], list(res[2:])


SMALL_ROWS = 8
ROW_LOSS, ROW_G2, ROW_G3, ROW_G4 = 0, 1, 2, 3


def _mid(mattn, mconv, x, target, g2, g3, g4, w_out, w_up, w_down, tm):
    t = x.shape[0]

    def body(ma_ref, mc_ref, x_ref, t_ref, g2_ref, g3_ref, g4_ref, wo_ref, wu_ref, wd_ref,
             act_ref, dup_ref, hn2t_ref, dmo_ref, dmix_ref, dh_ref, dmixed_ref, small_ref, up_ref):
        @pl.when(pl.program_id(0) == 0)
        def _():
            small_ref[...] = jnp.zeros_like(small_ref)

        g2, g3, g4 = g2_ref[...], g3_ref[...], g4_ref[...]
        mix_out = _mm(ma_ref[...], wo_ref[0:ATTN_W, :]) + _mm(mc_ref[...], wo_ref[ATTN_W:, :])
        r2 = _inv_rms(mix_out)
        mo_hat = mix_out * r2
        h = x_ref[...] + mo_hat * g2
        r3 = _inv_rms(h)
        h_hat = h * r3
        hn2 = (h_hat * g3).astype(BF16)
        hn2t_ref[...] = hn2.T
        for j in range(MID_CHUNKS):
            cols_j = slice(MID_CHUNK * j, MID_CHUNK * (j + 1))
            up = jnp.maximum(_mm(hn2, wu_ref[:, cols_j]), 0.0)
            up_ref[:, cols_j] = up.astype(BF16)
            act_ref[:, cols_j] = (up * up).astype(BF16)
        mlp = _mm(act_ref[...], wd_ref[...])
        r4 = _inv_rms(mlp)
        ml_hat = mlp * r4
        err = (h + ml_hat * g4) - t_ref[...]
        d_out = err * (1.0 / D_MODEL)
        d_mlp, dg4 = _rms_bwd(ml_hat, r4, g4, d_out)
        dmo = d_mlp.astype(BF16)
        dmo_ref[...] = dmo
        for j in range(MID_CHUNKS):
            cols_j = slice(MID_CHUNK * j, MID_CHUNK * (j + 1))
            dact = _mm_nt(dmo, wd_ref[cols_j, :])
            dup_ref[:, cols_j] = (dact * (2.0 * up_ref[:, cols_j].astype(F32))).astype(BF16)
        dhn2 = _mm_nt(dup_ref[...], wu_ref[...])
        dh_norm, dg3 = _rms_bwd(h_hat, r3, g3, dhn2)
        dh = d_out + dh_norm
        dh_ref[...] = dh
        d_mix, dg2 = _rms_bwd(mo_hat, r2, g2, dh)
        dmix = d_mix.astype(BF16)
        dmix_ref[...] = dmix
        dmixed_ref[...] = _mm_nt(dmix, wo_ref[...])
        small_ref[ROW_LOSS:ROW_LOSS + 1, :] += _colsum(err * err)
        small_ref[ROW_G2:ROW_G2 + 1, :] += _colsum(dg2)
        small_ref[ROW_G3:ROW_G3 + 1, :] += _colsum(dg3)
        small_ref[ROW_G4:ROW_G4 + 1, :] += _colsum(dg4)

    tile = lambda n: pl.BlockSpec((tm, n), lambda i: (i, 0))
    cols = lambda n: pl.BlockSpec((n, tm), lambda i: (0, i))
    gain = _full((1, D_MODEL))
    return pl.pallas_call(
        body, name="mid_fwd_bwd", grid=(t // tm,),
        in_specs=[tile(ATTN_W), tile(CONV_W), tile(D_MODEL), tile(D_MODEL), gain, gain, gain,
                  _resident((D_MODEL, D_MODEL)), _resident((D_MODEL, D_FF)), _resident((D_FF, D_MODEL))],
        out_specs=[tile(D_FF), tile(D_FF), cols(D_MODEL), tile(D_MODEL), tile(D_MODEL), tile(D_MODEL), tile(D_MODEL),
                   _full((SMALL_ROWS, D_MODEL))],
        out_shape=[jax.ShapeDtypeStruct((t, D_FF), BF16), jax.ShapeDtypeStruct((t, D_FF), BF16),
                   jax.ShapeDtypeStruct((D_MODEL, t), BF16), jax.ShapeDtypeStruct((t, D_MODEL), BF16),
                   jax.ShapeDtypeStruct((t, D_MODEL), BF16), jax.ShapeDtypeStruct((t, D_MODEL), F32),
                   jax.ShapeDtypeStruct((t, D_MODEL), F32), jax.ShapeDtypeStruct((SMALL_ROWS, D_MODEL), F32)],
        scratch_shapes=[pltpu.VMEM((tm, D_FF), BF16)],
        compiler_params=_params("arbitrary"),
    )(mattn, mconv, x, target, g2, g3, g4, w_out, w_up, w_down)


CHIP_FLIPS = ((1, 1), (1, 0), (0, 1))


def _block_order(dev):
    chip_masks = [4 * fx + 2 * fy for fx, fy in CHIP_FLIPS]
    masks = [m + 1 for m in chip_masks] + [1] + chip_masks + [0]
    return jnp.bitwise_xor(dev, jnp.asarray(masks, jnp.int32)).astype(jnp.int32)


def _other_chips(x, y, c):
    return [(1 - x if fx else x, 1 - y if fy else y, c) for fx, fy in CHIP_FLIPS]


def _dw_pair_sums(operands, order, which, name, barrier_id, ride=None):
    t = operands[-1].shape[0]
    n_far = len(CHIP_FLIPS)
    n_in = len(operands)
    n_ride = 0 if ride is None else 1
    out_chunk = D_MODEL // N_DEV
    if which == "up":
        rows, cols = D_MODEL, FF_CHUNK
        in_specs = [_resident((D_MODEL, t)), pl.BlockSpec((t, FF_CHUNK), lambda s, order_ref: (0, order_ref[s]))]
    elif which == "down":
        rows, cols = FF_CHUNK, D_MODEL
        in_specs = [pl.BlockSpec((t, FF_CHUNK), lambda s, order_ref: (0, order_ref[s])), _resident((t, D_MODEL))]
    else:
        rows, cols = out_chunk, D_MODEL
        half = pl.BlockSpec((t, out_chunk), lambda s, order_ref: (0, order_ref[s] % (N_DEV // 2)))
        in_specs = [half, half, _resident((t, D_MODEL))]

    def body(order_ref, *refs):
        own_ref, from_sib_ref, pair_ref = refs[n_in + n_ride:n_in + n_ride + 3]
        send_buf, land_buf, send_sems, recv_sems = refs[n_in + 2 * n_ride + 3:n_in + 2 * n_ride + 7]
        s_now = pl.program_id(0)
        x, y, c = _mesh_pos()
        sibling = (x, y, 1 - c)
        sems = (send_sems, recv_sems)

        @pl.when(s_now == 0)
        def _():
            _enter_with([sibling] + (_other_chips(x, y, c) if n_ride else []))

        if n_ride:
            _chip_exchange_beside(s_now == 0, s_now == N_DEV - 1, [refs[n_in]], [refs[n_in + 3 + n_ride]],
                                  refs[n_in + 2 * n_ride + 7:], enter=False)

        def hand_over(k):
            dst = land_buf.at[k] if k < n_far else from_sib_ref
            return _push(send_buf.at[k], dst, sems, k, sibling)

        if which == "out":
            ma_ref, mc_ref, b_ref = refs[:n_in]
            block = lax.cond(order_ref[s_now] < N_DEV // 2, lambda: _mm_tn(ma_ref[...], b_ref[...]),
                             lambda: _mm_tn(mc_ref[...], b_ref[...]))
        elif which == "down":
            block = _mm_tn(refs[0][...], refs[1][...])
        else:
            block = _mm(refs[0][...], refs[1][...])
        for k in range(n_far + 1):
            @pl.when(s_now == k)
            def _():
                send_buf[k] = block.astype(BF16)
                hand_over(k).start()

        for k in range(n_far):
            @pl.when(s_now == n_far + 1 + k)
            def _():
                hand_over(k).wait_recv()
                pair_ref[...] = (block + land_buf[k].astype(F32)).astype(BF16)

        @pl.when(s_now == N_DEV - 1)
        def _():
            own_ref[...] = block
            for k in range(n_far + 1):
                hand_over(k).wait_send()
            hand_over(n_far).wait_recv()

    rides = [] if ride is None else [ride]
    sems = lambda k: pltpu.SemaphoreType.DMA((k,))
    return pl.pallas_call(
        body, name=name,
        grid_spec=pltpu.PrefetchScalarGridSpec(
            num_scalar_prefetch=1, grid=(N_DEV,), in_specs=in_specs + [HBM_SPEC] * n_ride,
            out_specs=[pl.BlockSpec((rows, cols), lambda s, order_ref: (0, 0)), HBM_SPEC,
                       pl.BlockSpec((None, rows, cols), lambda s, order_ref: (jnp.clip(s - n_far - 1, 0, n_far - 1), 0, 0))]
            + [HBM_SPEC] * n_ride,
            scratch_shapes=[pltpu.VMEM((n_far + 1, rows, cols), BF16), pltpu.VMEM((n_far, rows, cols), BF16),
                            sems(n_far + 1), sems(n_far + 1)] + [sems(n_far), sems(n_far)] * n_ride),
        out_shape=[jax.ShapeDtypeStruct((rows, cols), F32), jax.ShapeDtypeStruct((rows, cols), BF16),
                   jax.ShapeDtypeStruct((n_far, rows, cols), BF16)]
        + [jax.ShapeDtypeStruct(r.shape, r.dtype) for r in rides],
        compiler_params=_params("arbitrary", barrier_id=barrier_id),
    )(order, *operands, *rides)


def _chip_exchange_beside(first, last, sums, outs, sems, enter=True):
    chips = _other_chips(*_mesh_pos())
    copies = [_push(sums[i].at[k], outs[i].at[k], sems, len(chips) * i + k, chip)
              for i in range(len(sums)) for k, chip in enumerate(chips)]

    @pl.when(first)
    def _():
        if enter:
            _enter_with(chips)
        for cp in copies:
            cp.start()

    @pl.when(last)
    def _():
        for cp in copies:
            cp.wait()


ROW_GCONV, ROW_CW0 = 1, 2


def _conv_bwd(dmixed, gates, g_conv, conv_w, tm):
    t = gates.shape[0]
    n = t // tm
    rev = lambda i: n - 1 - i

    def body(dm_ref, gates_ref, gprev_ref, gc_ref, cw_ref, dgates_ref, small_ref, carry_ref):
        i = pl.program_id(0)

        @pl.when(i == 0)
        def _():
            small_ref[...] = jnp.zeros_like(small_ref)
            carry_ref[...] = jnp.zeros_like(carry_ref)

        gates = gates_ref[...]
        gb, gcc, xin = gates[:, :CONV_W], gates[:, CONV_W:2 * CONV_W], gates[:, 2 * CONV_W:]
        u = gcc * xin
        gp = gprev_ref[...]
        uprev = jnp.where(rev(i) == 0, 0.0, gp[:, CONV_W:2 * CONV_W] * gp[:, 2 * CONV_W:])
        u1, u2 = _shift_rows_down(u, uprev, 1), _shift_rows_down(u, uprev, 2)
        w = cw_ref[...]
        c = _conv3(u, u1, u2, w)
        conv = gb * c
        rcv = _inv_rms(conv)
        c_hat = conv * rcv
        dconv, dgc = _rms_bwd(c_hat, rcv, gc_ref[...], dm_ref[...])
        dc = dconv * gb
        nxt = carry_ref[...]
        du = (w[2:3, :] * dc + w[1:2, :] * _shift_rows_up(dc, nxt, 1)) + w[0:1, :] * _shift_rows_up(dc, nxt, 2)
        carry_ref[...] = dc[0:8, :]
        dgates_ref[:, :CONV_W] = (dconv * c).astype(BF16)
        dgates_ref[:, CONV_W:2 * CONV_W] = (du * xin).astype(BF16)
        dgates_ref[:, 2 * CONV_W:] = (du * gcc).astype(BF16)
        small_ref[ROW_GCONV:ROW_GCONV + 1, :] += _colsum(dgc)
        small_ref[ROW_CW0:ROW_CW0 + 1, :] += _colsum(dc * u2)
        small_ref[ROW_CW0 + 1:ROW_CW0 + 2, :] += _colsum(dc * u1)
        small_ref[ROW_CW0 + 2:ROW_CW0 + 3, :] += _colsum(dc * u)

    tile = lambda w_: pl.BlockSpec((tm, w_), lambda i: (rev(i), 0))
    prev8 = pl.BlockSpec((8, GATES_W), lambda i: (jnp.maximum(rev(i) * (tm // 8) - 1, 0), 0))
    conv_half = pl.BlockSpec((tm, CONV_W), lambda i: (rev(i), ATTN_W // CONV_W))
    return pl.pallas_call(
        body, name="conv_bwd", grid=(n,),
        in_specs=[conv_half, tile(GATES_W), prev8, _full((1, CONV_W)), _full((3, CONV_W))],
        out_specs=[tile(GATES_W), _full((SMALL_ROWS, CONV_W))],
        out_shape=[jax.ShapeDtypeStruct((t, GATES_W), BF16), jax.ShapeDtypeStruct((SMALL_ROWS, CONV_W), F32)],
        scratch_shapes=[pltpu.VMEM((8, CONV_W), F32)],
        compiler_params=_params("arbitrary"),
    )(dmixed, gates, gates, g_conv, conv_w)


def _attn_bwd(qkv, dmixed, attn, g_attn, sinks, rope, sums):
    t = qkv.shape[0]
    n_steps = t // ATTN_STEP
    rev = lambda i: n_steps - 1 - i
    rc, rs1, rs2 = rope

    def body(sink_ref, q_ref, kp_ref, kc_ref, vp_ref, vc_ref, dm_ref, attn_ref, ga_ref, c_ref, s1_ref, s2_ref, sums_ref,
             dqkv_ref, dsink_ref, dgain_ref, arrived_ref, ck_ref, cv_ref, kacc_ref, vacc_ref, send_sems, recv_sems):
        i = pl.program_id(0)
        _chip_exchange_beside(i == 0, i == n_steps - 1, [sums_ref], [arrived_ref], (send_sems, recv_sems))

        @pl.when(i == 0)
        def _():
            dsink_ref[...] = jnp.zeros_like(dsink_ref)
            dgain_ref[...] = jnp.zeros_like(dgain_ref)
            ck_ref[...] = jnp.zeros_like(ck_ref)
            cv_ref[...] = jnp.zeros_like(cv_ref)

        kacc_ref[...] = jnp.zeros_like(kacc_ref)
        vacc_ref[...] = jnp.zeros_like(vacc_ref)
        a = attn_ref[...]
        ra = _inv_rms(a)
        dattn, dgain = _rms_bwd(a * ra, ra, ga_ref[...], dm_ref[...])
        dgain_ref[0:1, :] += _colsum(dgain)
        qt = (q_ref[...] * ATTN_SCALE).T
        dot = dattn.astype(BF16).T
        keys = jnp.concatenate([kp_ref[...], kc_ref[...]], axis=0)
        vals = jnp.concatenate([vp_ref[...], vc_ref[...]], axis=0)
        sink = [_group_sinks(sink_ref, g) for g in range(N_KV)]
        c, s1, s2 = c_ref[...], s1_ref[...], s2_ref[...]
        lane = lax.broadcasted_iota(jnp.int32, (1, 128), 1)
        dsink = jnp.zeros((1, 128), F32)
        masks = _attn_masks(rev(i) > 0)
        dq_parts = []
        for b in range(ATTN_STEP_BLOCKS):
            window = slice(BLOCK * b, BLOCK * (b + 2))
            valid = masks[b]
            dq_parts.append([])
            dk_parts, dv_parts = [], []
            for g in range(N_KV):
                gs = slice(HEAD_DIM * g, HEAD_DIM * (g + 1))
                kk, vv = keys[window, gs], vals[window, gs]
                qtg, dotg = _heads_side_by_side(qt, g, b), _heads_side_by_side(dot, g, b)
                probs, psink = _attn_probs(qtg, kk, sink[g], valid)
                dp = _mm(vv, dotg)
                delta = jnp.sum(probs * dp, axis=0, keepdims=True)
                ds = (probs * (dp - delta)).astype(BF16)
                sink_terms = psink * delta
                for hh in range(GROUP):
                    head_sum = jnp.sum(sink_terms[:, BLOCK * hh:BLOCK * (hh + 1)])
                    dsink = dsink + jnp.where(lane == GROUP * g + hh, -head_sum, 0.0)
                dq_parts[b].append(_mm_tn(kk * ATTN_SCALE, ds))
                dk_parts.append(_mm_nt(ds, qtg))
                dv_parts.append(_mm_nt(probs.astype(BF16), dotg))
            kacc_ref[window, :] += jnp.concatenate(dk_parts, axis=1)
            vacc_ref[window, :] += jnp.concatenate(dv_parts, axis=1)
        dq = _to_token_rows(dq_parts)
        for ci in range(ATTN_W // 128):
            sl = slice(128 * ci, 128 * (ci + 1))
            dqkv_ref[:, sl] = _rope_transpose(dq[:, sl], c, s1, s2).astype(BF16)
        kacc_ref[ATTN_STEP:, :] += ck_ref[...]
        vacc_ref[ATTN_STEP:, :] += cv_ref[...]
        ck_ref[...] = kacc_ref[:BLOCK, :]
        cv_ref[...] = vacc_ref[:BLOCK, :]
        dqkv_ref[:, ATTN_W:ATTN_W + KV_W] = _rope_transpose(kacc_ref[BLOCK:, :], c, s1, s2).astype(BF16)
        dqkv_ref[:, ATTN_W + KV_W:] = vacc_ref[BLOCK:, :].astype(BF16)
        dsink_ref[0:1, :] += dsink

    blk = lambda w_: pl.BlockSpec((ATTN_STEP, w_), lambda i: (rev(i), 0))
    return pl.pallas_call(
        body, name="attn_bwd", grid=(n_steps,),
        in_specs=[pl.BlockSpec(memory_space=pltpu.SMEM)] + _qkv_specs(rev)
        + [blk(ATTN_W), blk(ATTN_W), _full((1, ATTN_W)), blk(128), blk(128), blk(128), HBM_SPEC],
        out_specs=[blk(QKV_W), _full((8, 128)), _full((SMALL_ROWS, ATTN_W)), HBM_SPEC],
        out_shape=[jax.ShapeDtypeStruct((t, QKV_W), BF16), jax.ShapeDtypeStruct((8, 128), F32),
                   jax.ShapeDtypeStruct((SMALL_ROWS, ATTN_W), F32), jax.ShapeDtypeStruct(sums.shape, sums.dtype)],
        scratch_shapes=[pltpu.VMEM((BLOCK, KV_W), F32), pltpu.VMEM((BLOCK, KV_W), F32),
                        pltpu.VMEM((ATTN_KEYS, KV_W), F32), pltpu.VMEM((ATTN_KEYS, KV_W), F32),
                        pltpu.SemaphoreType.DMA((len(CHIP_FLIPS),)), pltpu.SemaphoreType.DMA((len(CHIP_FLIPS),))],
        compiler_params=_params("arbitrary", barrier_id=6),
    )(sinks, qkv, qkv, qkv, qkv, qkv, dmixed, attn, g_attn, rc, rs1, rs2, sums)


def _grad_x_tile(dq, dg, x_hat, r, g1, w_ref, dh):
    dhn = _mm_nt(dq, w_ref[:, :QKV_W]) + _mm_nt(dg, w_ref[:, QKV_W:])
    dx, dg1 = _rms_bwd(x_hat, r, g1, dhn)
    return dh + dx, _colsum(dg1)


def _in_proj_bwd(dqkv, dgates, x, dh, g1, w_in, tm, out_sums):
    t = x.shape[0]
    n = t // tm
    n_cover = max(n // 2, 1)
    n_steps = n + n_cover
    n_far = len(CHIP_FLIPS)
    shard = (D_MODEL, IN_SHARD)

    def body(dq_ref, dg_ref, x_ref, dh_ref, g1_ref, w_ref, osums_ref,
             dx_ref, own_ref, sib_ref, far_ref, dg1_ref, oarrived_ref,
             acc_ref, send_buf, land_buf, pair_buf, d2d_send, d2d_recv, ici_send, ici_recv, o_send, o_recv):
        i = pl.program_id(0)
        x_pos, y_pos, c = _mesh_pos()
        my_chip = 2 * x_pos + y_pos
        sibling = (x_pos, y_pos, 1 - c)
        @pl.when(i == 0)
        def _():
            _enter_with(_sibling_and_chips(x_pos, y_pos, c))

        _chip_exchange_beside(i == 0, i == n_steps - 1, [osums_ref], [oarrived_ref], (o_send, o_recv), enter=False)

        def cols(d):
            return slice(IN_SHARD * d, IN_SHARD * (d + 1))

        def hand_over(chip):
            return _push(send_buf.at[chip], land_buf.at[chip], (d2d_send, d2d_recv), chip, sibling)

        def to_chip(chip, rel):
            return pltpu.make_async_remote_copy(
                src_ref=pair_buf.at[chip], dst_ref=far_ref.at[rel - 1], send_sem=ici_send.at[rel - 1],
                recv_sem=ici_recv.at[rel - 1], device_id=(chip // 2, chip % 2, c), device_id_type=MESH)

        @pl.when(i == 0)
        def _():
            acc_ref[...] = jnp.zeros_like(acc_ref)
            dg1_ref[...] = jnp.zeros_like(dg1_ref)

        def normed_x():
            xv = x_ref[...]
            r = _inv_rms(xv)
            return xv * r, r

        @pl.when(i < n)
        def _():
            hn = (normed_x()[0] * g1_ref[...]).astype(BF16)
            acc_ref[:, :QKV_W] += _mm_tn(hn, dq_ref[...])
            acc_ref[:, QKV_W:] += _mm_tn(hn, dg_ref[...])

        @pl.when(i == n - 1)
        def _():
            for d in range(N_DEV):
                @pl.when(d % 2 != c)
                def _():
                    send_buf[d // 2] = acc_ref[:, cols(d)].astype(BF16)
                    hand_over(d // 2).start()
            for d in range(N_DEV):
                chip = d // 2

                @pl.when(d % 2 == c)
                def _():
                    hand_over(chip).wait_recv()

                    @pl.when(chip == my_chip)
                    def _():
                        own_ref[...] = acc_ref[:, cols(d)]
                        sib_ref[...] = land_buf[chip]

                    @pl.when(chip != my_chip)
                    def _():
                        pair_buf[chip] = (acc_ref[:, cols(d)] + land_buf[chip].astype(F32)).astype(BF16)
                        to_chip(chip, chip ^ my_chip).start()
            for chip in range(N_CHIPS):
                hand_over(chip).wait_send()

        @pl.when(i >= n)
        def _():
            x_hat, r = normed_x()
            dx_ref[...], dg1 = _grad_x_tile(dq_ref[...], dg_ref[...], x_hat, r, g1_ref[...], w_ref, dh_ref[...])
            dg1_ref[0:1, :] += dg1

        @pl.when(i == n_steps - 1)
        def _():
            for rel in range(1, n_far + 1):
                to_chip(0, rel).wait()

    both = lambda w_: pl.BlockSpec((tm, w_), lambda i: (i % n, 0))
    second = pl.BlockSpec((tm, D_MODEL), lambda i: (jnp.maximum(i - n, 0), 0))
    whole = lambda dtype: jax.ShapeDtypeStruct(shard, dtype)
    sems = lambda k: pltpu.SemaphoreType.DMA((k,))
    res = pl.pallas_call(
        body, name="in_proj_bwd", grid=(n_steps,),
        in_specs=[both(QKV_W), both(GATES_W), both(D_MODEL), second, _full((1, D_MODEL)), _resident((D_MODEL, IN_COLS)),
                  HBM_SPEC],
        out_specs=[second, _full(shard), _full(shard), HBM_SPEC, _full((SMALL_ROWS, D_MODEL)), HBM_SPEC],
        out_shape=[jax.ShapeDtypeStruct((n_cover * tm, D_MODEL), F32), whole(F32), whole(BF16),
                   jax.ShapeDtypeStruct((n_far,) + shard, BF16), jax.ShapeDtypeStruct((SMALL_ROWS, D_MODEL), F32),
                   jax.ShapeDtypeStruct(out_sums.shape, out_sums.dtype)],
        scratch_shapes=[pltpu.VMEM((D_MODEL, IN_COLS), F32), pltpu.VMEM((N_CHIPS,) + shard, BF16),
                        pltpu.VMEM((N_CHIPS,) + shard, BF16), pltpu.VMEM((N_CHIPS,) + shard, BF16),
                        sems(N_CHIPS), sems(N_CHIPS), sems(n_far), sems(n_far), sems(n_far), sems(n_far)],
        compiler_params=_params("arbitrary", barrier_id=7),
    )(dqkv, dgates, x, dh, g1, w_in, out_sums)
    return res[0], (res[1], res[2], res[3]), res[4], res[5]


def _grad_x_rest(dqkv, dgates, x, dh, g1, w_in, tm, head, dg1_rows):
    t = x.shape[0]
    first = head.shape[0] // tm
    n_rest = t // tm - first
    if n_rest == 0:
        return head, dg1_rows
    assert first <= n_rest

    def body(dq_ref, dg_ref, x_ref, dh_ref, g1_ref, w_ref, head_ref, rows_ref, gx_ref, dg1_ref, stage, sems):
        j = pl.program_id(0)

        def tile_out(step, kind):
            row0 = (step + first) * tm if kind == 0 else step * tm
            slot = 2 * kind + step % 2
            return pltpu.make_async_copy(stage.at[slot], gx_ref.at[pl.ds(pl.multiple_of(row0, tm), tm), :], sems.at[slot])

        @pl.when(j == 0)
        def _():
            dg1_ref[...] = rows_ref[...]

        @pl.when(j >= 2)
        def _():
            tile_out(j - 2, 0).wait()

        @pl.when((j >= 2) & (j - 2 < first))
        def _():
            tile_out(j - 2, 1).wait()

        @pl.when(j < first)
        def _():
            stage[2 + j % 2] = head_ref[...]
            tile_out(j, 1).start()

        xv = x_ref[...]
        r = _inv_rms(xv)
        dx, dg1 = _grad_x_tile(dq_ref[...], dg_ref[...], xv * r, r, g1_ref[...], w_ref, dh_ref[...])
        stage[j % 2] = dx
        dg1_ref[0:1, :] += dg1
        tile_out(j, 0).start()

        @pl.when(j == n_rest - 1)
        def _():
            for back in range(min(2, n_rest)):
                tile_out(j - back, 0).wait()

                @pl.when(j - back < first)
                def _():
                    tile_out(j - back, 1).wait()

    tile = lambda w_: pl.BlockSpec((tm, w_), lambda j: (j + first, 0))
    head_tile = pl.BlockSpec((tm, D_MODEL), lambda j: (jnp.minimum(j, first - 1), 0))
    return pl.pallas_call(
        body, name="grad_x_rest", grid=(n_rest,),
        in_specs=[tile(QKV_W), tile(GATES_W), tile(D_MODEL), tile(D_MODEL), _full((1, D_MODEL)),
                  _resident((D_MODEL, IN_COLS)), head_tile, _full((SMALL_ROWS, D_MODEL))],
        out_specs=[HBM_SPEC, _full((SMALL_ROWS, D_MODEL))],
        out_shape=[jax.ShapeDtypeStruct((t, D_MODEL), F32), jax.ShapeDtypeStruct((SMALL_ROWS, D_MODEL), F32)],
        scratch_shapes=[pltpu.VMEM((4, tm, D_MODEL), F32), pltpu.SemaphoreType.DMA((4,))],
        compiler_params=_params("arbitrary"),
    )(dqkv, dgates, x, dh, g1, w_in, head, dg1_rows)


def _all_gather(shards, name):
    n = len(shards)

    def body(*refs):
        _enter_with(_sibling_and_chips(*_mesh_pos()))
        start, finish = _gather_steps(refs[:n], refs[n:2 * n], *refs[2 * n:])
        start()
        finish()

    return pl.pallas_call(
        body, name=name,
        in_specs=[HBM_SPEC] * n, out_specs=[HBM_SPEC] * n,
        out_shape=[jax.ShapeDtypeStruct((N_DEV,) + s.shape, s.dtype) for s in shards],
        scratch_shapes=[pltpu.SemaphoreType.DMA((7 * n,)), pltpu.SemaphoreType.DMA((7 * n,)),
                        pltpu.SemaphoreType.DMA((n,))],
        compiler_params=_params(barrier_id=8),
    )(*shards)


def _adam_math(w, g, m, v):
    m = ADAM_B1 * m + (1.0 - ADAM_B1) * g
    v = ADAM_B2 * v + (1.0 - ADAM_B2) * (g * g)
    m_hat = m / (1.0 - ADAM_B1 ** ADAM_STEP)
    v_hat = v / (1.0 - ADAM_B2 ** ADAM_STEP)
    delta = -ADAM_LR * (m_hat / (jnp.sqrt(v_hat) + ADAM_EPS) + ADAM_WD * w)
    return delta, m, v


def _adamw_reduced(w, m, v, own, from_sibling, from_chips, tr):
    rows, cols = w.shape

    def body(w_ref, m_ref, v_ref, own_ref, sib_ref, far_ref, g_ref, d_ref, nm_ref, nv_ref):
        g = own_ref[...] + sib_ref[...].astype(F32)
        for k in range(len(CHIP_FLIPS)):
            g = g + far_ref[k].astype(F32)
        g_ref[...] = g
        d_ref[...], nm_ref[...], nv_ref[...] = _adam_math(w_ref[...], g, m_ref[...], v_ref[...])

    tile = pl.BlockSpec((tr, cols), lambda i: (i, 0))
    out = jax.ShapeDtypeStruct((rows, cols), F32)
    return pl.pallas_call(
        body, name="adamw_reduced", grid=(rows // tr,),
        in_specs=[tile] * 5 + [pl.BlockSpec((len(CHIP_FLIPS), tr, cols), lambda i: (0, i, 0))],
        out_specs=[tile] * 4, out_shape=[out] * 4,
        compiler_params=_params("parallel"),
    )(w, m, v, own, from_sibling, from_chips)


SMALL_PARAMS = ("pre_mix_norm", "post_mix_norm", "pre_mlp_norm", "post_mlp_norm", "attn_group_norm", "conv_group_norm",
                "conv_w", "attn_sinks")


SMALL_WIDTHS = (D_MODEL, CONV_W, ATTN_W, 128, D_MODEL)


def _small_tail(gathered, dev, weights, first_moments, second_moments):
    n = len(SMALL_PARAMS)
    conv_shard = CONV_W // N_DEV

    def body(dev_ref, sums_ref, *refs):
        w_refs, m_refs, v_refs = refs[:n], refs[n:2 * n], refs[2 * n:3 * n]
        loss_ref, outs = refs[3 * n], refs[3 * n + 1:]
        total = sums_ref[0]
        for d in range(1, N_DEV):
            total = total + sums_ref[d]
        starts = [sum(SMALL_WIDTHS[:i]) for i in range(len(SMALL_WIDTHS))]
        mid, conv, gain, sink, inp = (total[:, a:a + w_] for a, w_ in zip(starts, SMALL_WIDTHS))
        loss_ref[...] = (0.5 / D_MODEL) * jnp.sum(mid[ROW_LOSS:ROW_LOSS + 1, :], axis=1, keepdims=True)
        conv_rows = conv[ROW_CW0:ROW_CW0 + 3, :]
        conv_g = jnp.zeros((3, conv_shard), F32)
        for d in range(N_DEV):
            conv_g = conv_g + jnp.where(dev_ref[0] == d, conv_rows[:, conv_shard * d:conv_shard * (d + 1)], 0.0)
        grads = [inp[0:1, :], mid[ROW_G2:ROW_G2 + 1, :], mid[ROW_G3:ROW_G3 + 1, :], mid[ROW_G4:ROW_G4 + 1, :],
                 gain[0:1, :], conv[ROW_GCONV:ROW_GCONV + 1, :], conv_g, sink[0:1, :N_HEADS]]
        for i, g in enumerate(grads):
            delta, new_m, new_v = _adam_math(w_refs[i][...], g, m_refs[i][...], v_refs[i][...])
            outs[i][...], outs[n + i][...], outs[2 * n + i][...], outs[3 * n + i][...] = g, delta, new_m, new_v

    params = list(weights) + list(first_moments) + list(second_moments)
    shapes = [jax.ShapeDtypeStruct(w.shape, F32) for w in weights]
    res = pl.pallas_call(
        body, name="small_tail", grid=(1,),
        in_specs=[pl.BlockSpec(memory_space=pltpu.SMEM), _full(gathered.shape)] + [_full(p.shape) for p in params],
        out_specs=[_full((1, 1))] + [_full(sh.shape) for sh in shapes] * 4,
        out_shape=[jax.ShapeDtypeStruct((1, 1), F32)] + shapes * 4,
    )(dev, gathered, *params)
    return res[0], [res[1 + k * n:1 + (k + 1) * n] for k in range(4)]


TOKEN_TILE = 512
MID_TILE = 256
MID_CHUNK = 1024
MID_CHUNKS = D_FF // MID_CHUNK
ADAM_ROWS = 512


def _local_grads(x, target, g1, w_in_shard, conv_shard, sinks, g_attn, g_conv, g2, g3, g4, shards, order):
    t = x.shape[0]
    tm = min(TOKEN_TILE, t)
    rope = _rope_tables(t)
    qkv, gates, mconv, w_in, conv_w, gathered = _in_proj_fwd(x, g1, w_in_shard, conv_shard, g_conv, rope, tm, shards,
                                                             (False, True, False))
    attn, mattn, (w_out, w_up, w_down) = _attn_fwd(qkv, sinks, g_attn, shards, gathered)
    act, dup, hn2t, dmo, dmix, dh, dmixed, small_mid = _mid(
        mattn, mconv, x, target, g2, g3, g4, w_out.reshape(D_MODEL, D_MODEL),
        w_up, w_down.reshape(D_FF, D_MODEL), min(MID_TILE, t))
    up_own, up_sib, up_sums = _dw_pair_sums((hn2t, dup), order, "up", "dw_up", 2)
    down_own, down_sib, down_sums, up_far = _dw_pair_sums((act, dmo), order, "down", "dw_down", 3, ride=up_sums)
    out_own, out_sib, out_sums = _dw_pair_sums((mattn, mconv, dmix), order, "out", "dw_out", 4)
    dgates, small_conv = _conv_bwd(dmixed, gates, g_conv, conv_w, tm)
    dqkv, dsink, dg_attn, down_far = _attn_bwd(qkv, dmixed, attn, g_attn, sinks, rope, down_sums)
    grad_x_head, dw_in, small_in, out_far = _in_proj_bwd(dqkv, dgates, x, dh, g1, w_in, tm, out_sums)
    grad_x, small_in = _grad_x_rest(dqkv, dgates, x, dh, g1, w_in, tm, grad_x_head, small_in)
    dw_out, dw_up, dw_down = (out_own, out_sib, out_far), (up_own, up_sib, up_far), (down_own, down_sib, down_far)
    return grad_x, dw_in, dw_out, dw_up, dw_down, (small_mid, small_conv, dg_attn, dsink, small_in)


def kernel(x, pre_mix_norm, w_in, conv_w, attn_sinks, attn_group_norm, conv_group_norm, w_out, post_mix_norm, pre_mlp_norm, w_up, w_down, post_mlp_norm, loss_target, m_pre_mix_norm, m_w_in, m_conv_w, m_attn_sinks, m_attn_group_norm, m_conv_group_norm, m_w_out, m_post_mix_norm, m_pre_mlp_norm, m_w_up, m_w_down, m_post_mlp_norm, v_pre_mix_norm, v_w_in, v_conv_w, v_attn_sinks, v_attn_group_norm, v_conv_group_norm, v_w_out, v_post_mix_norm, v_pre_mlp_norm, v_w_up, v_w_down, v_post_mlp_norm):
    xi, yi, ci = _mesh_pos()
    chip = 2 * xi + yi
    dev = 2 * chip + ci

    order = _block_order(dev)

    shards = [w_out[0].astype(BF16), w_up[0].astype(BF16), w_down[0].astype(BF16)]

    grad_x, dw_in, dw_out, dw_up, dw_down, smalls = _local_grads(
        x[0], loss_target[0], pre_mix_norm, w_in[0].astype(BF16), conv_w[0], attn_sinks, attn_group_norm, conv_group_norm,
        post_mix_norm, pre_mlp_norm, post_mlp_norm, shards, order)

    big = {}
    for name, w, m, v, (own, sib, far) in zip(
            ("w_in", "w_out", "w_up", "w_down"), (w_in, w_out, w_up, w_down), (m_w_in, m_w_out, m_w_up, m_w_down),
            (v_w_in, v_w_out, v_w_up, v_w_down), (dw_in, dw_out, dw_up, dw_down)):
        big[name] = [a[None] for a in _adamw_reduced(w[0], m[0], v[0], own, sib, far, min(ADAM_ROWS, w.shape[1]))]

    flat = lambda a: a.reshape(-1, a.shape[-1])
    loss, small = _small_tail(
        _all_gather([jnp.concatenate(smalls, axis=1)], "gather_small")[0], dev.reshape(1).astype(jnp.int32),
        [flat(a) for a in (pre_mix_norm, post_mix_norm, pre_mlp_norm, post_mlp_norm, attn_group_norm, conv_group_norm,
                           conv_w, attn_sinks)],
        [flat(a) for a in (m_pre_mix_norm, m_post_mix_norm, m_pre_mlp_norm, m_post_mlp_norm, m_attn_group_norm,
                           m_conv_group_norm, m_conv_w, m_attn_sinks)],
        [flat(a) for a in (v_pre_mix_norm, v_post_mix_norm, v_pre_mlp_norm, v_post_mlp_norm, v_attn_group_norm,
                           v_conv_group_norm, v_conv_w, v_attn_sinks)])

    order = ("pre_mix_norm", "w_in", "conv_w", "attn_sinks", "attn_group_norm", "conv_group_norm", "w_out",
             "post_mix_norm", "pre_mlp_norm", "w_up", "w_down", "post_mlp_norm")
    shape_of = {"conv_w": conv_w.shape}
    outs = []
    for k in range(4):
        by_name = dict(zip(SMALL_PARAMS, small[k]))
        outs += [big[nm][k] if nm in big else by_name[nm].reshape(shape_of.get(nm, by_name[nm].shape)) for nm in order]
    loss = loss.reshape(())
    return (loss, grad_x[None], *outs)
```

```python
import jax
import jax.numpy as jnp
import numpy as np
from jax import lax
from jax.experimental import pallas as pl
from jax.experimental.pallas import tpu as pltpu

F32 = jnp.float32
BF16 = jnp.bfloat16

D_MODEL = 1024
HEAD_DIM = 64
ATTN_W = 512
CONV_W = 512
N_HEADS = 8
N_KV = 2
GROUP = 4
KV_W = 128
QKV_W = ATTN_W + 2 * KV_W
GATES_W = 3 * CONV_W
IN_COLS = QKV_W + GATES_W
D_FF = 4096
FF_CHUNK = 512
BLOCK = 128
ROT_HALF = 8
ROPE_THETA = 500000.0
NORM_EPS = 1e-6
NEG_INF = -1e30
ATTN_SCALE = 0.125
N_DEV = 8
N_CHIPS = 4
IN_SHARD = IN_COLS // N_DEV

ADAM_LR = 0.001
ADAM_B1 = 0.9
ADAM_B2 = 0.999
ADAM_EPS = 1e-08
ADAM_WD = 0.01
ADAM_STEP = 10

V7X_VMEM_BYTES = 64 * 1024 * 1024
VMEM_LIMIT = V7X_VMEM_BYTES - 2 * 1024 * 1024

MESH = pl.DeviceIdType.MESH
HBM_SPEC = pl.BlockSpec(memory_space=pltpu.HBM)


def _params(*sem, barrier_id=None):
    return pltpu.CompilerParams(dimension_semantics=sem or None, vmem_limit_bytes=VMEM_LIMIT, collective_id=barrier_id)


def _mm(a, b):
    return jnp.dot(a, b, preferred_element_type=F32)


def _mm_nt(a, b):
    return lax.dot_general(a, b, (((1,), (1,)), ((), ())), preferred_element_type=F32)


def _mm_tn(a, b):
    return lax.dot_general(a, b, (((0,), (0,)), ((), ())), preferred_element_type=F32)


def _inv_rms(x):
    return lax.rsqrt(jnp.mean(x * x, axis=-1, keepdims=True) + NORM_EPS)


def _rms_bwd(xhat, r, gain, dy):
    gy = dy * gain
    return r * (gy - xhat * jnp.mean(gy * xhat, axis=-1, keepdims=True)), dy * xhat


def _colsum(a):
    return jnp.sum(a, axis=0, keepdims=True)


def _full(shape):
    zeros = (0,) * len(shape)
    return pl.BlockSpec(shape, lambda *_: zeros)


def _resident(shape):
    zeros = (0,) * len(shape)
    return pl.BlockSpec(shape, lambda *_: zeros, pipeline_mode=pl.Buffered(1))


def _rope_tables(t):
    pos = np.arange(t, dtype=np.float32)
    inv_freq = (ROPE_THETA ** (-np.arange(0, 2 * ROT_HALF, 2, dtype=np.float64) / (2 * ROT_HALF))).astype(np.float32)
    ang = (pos[:, None] * inv_freq[None, :]).astype(np.float64)
    cos, sin = np.cos(ang).astype(np.float32), np.sin(ang).astype(np.float32)
    zeros8 = np.zeros((t, ROT_HALF), np.float32)
    rest = np.zeros((t, HEAD_DIM - 2 * ROT_HALF), np.float32)
    c_head = np.concatenate([cos, cos, rest + 1.0], axis=1)
    s1_head = np.concatenate([zeros8, sin, rest], axis=1)
    s2_head = np.concatenate([-sin, zeros8, rest], axis=1)
    two = lambda a: jnp.asarray(np.concatenate([a, a], axis=1))
    return two(c_head), two(s1_head), two(s2_head)


def _rope(v, c, s1, s2):
    return v * c + pltpu.roll(v, ROT_HALF, 1) * s1 + pltpu.roll(v, 128 - ROT_HALF, 1) * s2


def _rope_transpose(dv, c, s1, s2):
    return dv * c + pltpu.roll(dv * s1, 128 - ROT_HALF, 1) + pltpu.roll(dv * s2, ROT_HALF, 1)


def _shift_rows_down(u, prev, k):
    row = lax.broadcasted_iota(jnp.int32, u.shape, 0)
    out = pltpu.roll(u, k, 0)
    for r in range(k):
        out = jnp.where(row == r, prev[8 - k + r:8 - k + r + 1, :], out)
    return out


def _shift_rows_up(u, nxt, k):
    n = u.shape[0]
    row = lax.broadcasted_iota(jnp.int32, u.shape, 0)
    out = pltpu.roll(u, n - k, 0)
    for r in range(k):
        out = jnp.where(row == n - k + r, nxt[r:r + 1, :], out)
    return out


def _conv3(u, u1, u2, w):
    return (w[0:1, :] * u2 + w[1:2, :] * u1) + w[2:3, :] * u


def _mesh_pos():
    return lax.axis_index("x"), lax.axis_index("y"), lax.axis_index("c")


def _slot(ref, pos):
    dev = 4 * pos[0] + 2 * pos[1] + pos[2]
    if len(ref.shape) == 2:
        width = ref.shape[1] // N_DEV
        return ref.at[:, pl.ds(pl.multiple_of(dev * width, width), width)]
    return ref.at[dev]


def _gathered_shape(shard, by_cols):
    if by_cols:
        return jax.ShapeDtypeStruct((shard.shape[0], N_DEV * shard.shape[1]), shard.dtype)
    return jax.ShapeDtypeStruct((N_DEV,) + shard.shape, shard.dtype)


def _enter_with(peers):
    barrier = pltpu.get_barrier_semaphore()
    for peer in peers:
        pl.semaphore_signal(barrier, inc=1, device_id=peer, device_id_type=MESH)
    pl.semaphore_wait(barrier, len(peers))


def _sibling_and_chips(x, y, c):
    return [(x, y, 1 - c), (1 - x, y, c), (x, 1 - y, c), (1 - x, 1 - y, c)]


def _push(src, dst, sems, k, to):
    send_sems, recv_sems = sems
    return pltpu.make_async_remote_copy(src_ref=src, dst_ref=dst, send_sem=send_sems.at[k], recv_sem=recv_sems.at[k],
                                        device_id=to, device_id_type=MESH)


def _gather_steps(shards, outs, send_sems, recv_sems, local_sems):
    n = len(shards)
    x, y, c = _mesh_pos()
    me, sibling = (x, y, c), (x, y, 1 - c)
    chips = [(1 - x, y), (x, 1 - y), (1 - x, 1 - y)]

    def copy(i, k, block, to, src=None):
        dst = _slot(outs[i], block)
        return _push(dst if src is None else src, dst, (send_sems, recv_sems), 7 * i + k, to)

    mine = [pltpu.make_async_copy(shards[i], _slot(outs[i], me), local_sems.at[i]) for i in range(n)]
    first = []
    for i in range(n):
        first.append(copy(i, 0, me, sibling, src=shards[i]))
        first += [copy(i, 1 + j, me, (*chip, c), src=shards[i]) for j, chip in enumerate(chips)]

    def start():
        for cp in mine + first:
            cp.start()

    def finish():
        passed = []
        for j, chip in enumerate(chips):
            for i in range(n):
                copy(i, 1 + j, (*chip, c), me).wait_recv()
                cp = copy(i, 4 + j, (*chip, c), sibling)
                cp.start()
                passed.append(cp)
        for i in range(n):
            copy(i, 0, sibling, me).wait_recv()
            for j, chip in enumerate(chips):
                copy(i, 4 + j, (*chip, 1 - c), me).wait_recv()
        for cp in first + passed:
            cp.wait_send()
        for cp in mine:
            cp.wait()

    return start, finish


def _gather_near(first, last, shards, outs, sems, local_sems):
    x, y, c = _mesh_pos()
    me, peers = (x, y, c), [(x, y, 1 - c), (1 - x, y, c), (x, 1 - y, c)]
    n = len(shards)
    local = [pltpu.make_async_copy(shards[i], _slot(outs[i], me), local_sems.at[i]) for i in range(n)]
    sends = [_push(shards[i], _slot(outs[i], me), sems, 3 * i + k, peers[k]) for i in range(n) for k in range(3)]
    arrivals = [_push(shards[i], _slot(outs[i], peers[k]), sems, 3 * i + k, peers[k]) for i in range(n) for k in range(3)]

    def start():
        for cp in local + sends:
            cp.start()

    if first is not None:
        pl.when(first)(start)

    @pl.when(last)
    def _():
        for cp in sends:
            cp.wait_send()
        for cp in arrivals:
            cp.wait_recv()
        for cp in local:
            cp.wait()

    return start


def _relay_route(x, y, c):
    south = c == 0
    via = (jnp.where(south, 1 - x, x), jnp.where(south, y, 1 - y))
    to = (jnp.where(south, x, 1 - x), jnp.where(south, 1 - y, y))
    return via, to


def _gather_far(first, middle, last, shards, ins, outs, sems):
    x, y, c = _mesh_pos()
    sibling = (x, y, 1 - c)
    chips = [(1 - x, y), (x, 1 - y), (1 - x, 1 - y)]
    via, to = _relay_route(x, y, c)
    n = len(shards)
    diag_send = [_push(_slot(ins[i], (*via, c)), _slot(outs[i], (*via, c)), sems, 4 * i, (*to, c)) for i in range(n)]
    diag_arrival = [_push(shards[i], _slot(outs[i], (*chips[2], c)), sems, 4 * i, (*to, c)) for i in range(n)]
    passed = [[_push(_slot(ins[i], (*chips[j], c)), _slot(outs[i], (*chips[j], c)), sems, 4 * i + 1 + j, sibling)
               for i in range(n)] for j in range(3)]
    from_sibling = [_push(shards[i], _slot(outs[i], (*chips[j], 1 - c)), sems, 4 * i + 1 + j, sibling)
                    for i in range(n) for j in range(3)]

    @pl.when(first)
    def _():
        for cp in diag_send + passed[0] + passed[1]:
            cp.start()

    @pl.when(middle)
    def _():
        for cp in diag_arrival:
            cp.wait_recv()
        for cp in passed[2]:
            cp.start()

    @pl.when(last)
    def _():
        for cp in from_sibling:
            cp.wait_recv()
        for cp in diag_send + passed[0] + passed[1] + passed[2]:
            cp.wait_send()


def _in_proj_fwd(x, g1, w_in, conv_w, g_conv, rope, tm, shards, by_cols):
    t = x.shape[0]
    rc, rs1, rs2 = rope
    n = len(shards)
    n_tiles = t // tm

    def body(*refs):
        x_ref, g1_ref, w_ref, cw_ref, gc_ref, c_ref, s1_ref, s2_ref = refs[:8]
        shard_refs = refs[8:8 + n]
        qkv_ref, gates_ref, mconv_ref, w_full_ref, cw_full_ref = refs[8 + n:13 + n]
        gathered = refs[13 + n:13 + 2 * n]
        carry_ref, w_land, cw_land, hn_ref = refs[13 + 2 * n:17 + 2 * n]
        now_sems = refs[17 + 2 * n:20 + 2 * n]
        step = pl.program_id(0)
        start_later_weights = _gather_near(None, step == 2 * n_tiles - 1, shard_refs, gathered,
                                           refs[20 + 2 * n:22 + 2 * n], refs[22 + 2 * n]) if n else None
        start_w_in, finish_w_in = _gather_steps([w_ref, cw_ref], [w_land, cw_land], *now_sems)

        @pl.when(step == 0)
        def _():
            carry_ref[...] = jnp.zeros_like(carry_ref)
            _enter_with(_sibling_and_chips(*_mesh_pos()))
            start_w_in()

        @pl.when(step < n_tiles)
        def _():
            xv = x_ref[...]
            hn_ref[step] = ((xv * _inv_rms(xv)) * g1_ref[...]).astype(BF16)

        @pl.when(step == n_tiles)
        def _():
            finish_w_in()
            if start_later_weights is not None:
                start_later_weights()
            conv_shard = CONV_W // N_DEV
            for d in range(N_DEV):
                w_full_ref[:, IN_SHARD * d:IN_SHARD * (d + 1)] = w_land[d]
                cw_full_ref[:, conv_shard * d:conv_shard * (d + 1)] = cw_land[d]

        @pl.when(step >= n_tiles)
        def _():
            proj = _mm(hn_ref[step - n_tiles], w_full_ref[...])
            c, s1, s2 = c_ref[...], s1_ref[...], s2_ref[...]
            for ci in range((ATTN_W + KV_W) // 128):
                sl = slice(128 * ci, 128 * (ci + 1))
                qkv_ref[:, sl] = _rope(proj[:, sl], c, s1, s2).astype(BF16)
            qkv_ref[:, ATTN_W + KV_W:QKV_W] = proj[:, ATTN_W + KV_W:QKV_W].astype(BF16)
            gates = proj[:, QKV_W:]
            gates_ref[...] = gates
            gb, gcc, xin = gates[:, :CONV_W], gates[:, CONV_W:2 * CONV_W], gates[:, 2 * CONV_W:]
            u = gcc * xin
            prev = carry_ref[...]
            conv = gb * _conv3(u, _shift_rows_down(u, prev, 1), _shift_rows_down(u, prev, 2), cw_full_ref[...])
            carry_ref[...] = u[tm - 8:tm, :]
            mconv_ref[...] = ((conv * _inv_rms(conv)) * gc_ref[...]).astype(BF16)

    first_pass = pl.BlockSpec((tm, D_MODEL), lambda i: (jnp.minimum(i, n_tiles - 1), 0))
    tile = lambda w_: pl.BlockSpec((tm, w_), lambda i: (jnp.maximum(i - n_tiles, 0), 0))
    sems = lambda k: pltpu.SemaphoreType.DMA((k,))
    res = pl.pallas_call(
        body, name="in_proj_fwd", grid=(2 * n_tiles,),
        in_specs=[first_pass, _full((1, D_MODEL)), HBM_SPEC, HBM_SPEC, _full((1, CONV_W)), tile(128), tile(128),
                  tile(128)] + [HBM_SPEC] * n,
        out_specs=[tile(QKV_W), tile(GATES_W), tile(CONV_W), _full((D_MODEL, IN_COLS)), _full((3, CONV_W))]
        + [HBM_SPEC] * n,
        out_shape=[jax.ShapeDtypeStruct((t, QKV_W), BF16), jax.ShapeDtypeStruct((t, GATES_W), F32),
                   jax.ShapeDtypeStruct((t, CONV_W), BF16), jax.ShapeDtypeStruct((D_MODEL, IN_COLS), BF16),
                   jax.ShapeDtypeStruct((3, CONV_W), F32)]
        + [_gathered_shape(s, cols) for s, cols in zip(shards, by_cols)],
        scratch_shapes=[pltpu.VMEM((8, CONV_W), F32), pltpu.VMEM((N_DEV,) + w_in.shape, BF16),
                        pltpu.VMEM((N_DEV,) + conv_w.shape, F32), pltpu.VMEM((n_tiles, tm, D_MODEL), BF16),
                        sems(14), sems(14), sems(2)]
        + ([sems(3 * n), sems(3 * n), sems(n)] if n else []),
        compiler_params=_params("arbitrary", barrier_id=0),
    )(x, g1, w_in, conv_w, g_conv, rc, rs1, rs2, *shards)
    return res[0], res[1], res[2], res[3], res[4], list(res[5:])


GROUP_COLS = GROUP * BLOCK
ATTN_STEP_BLOCKS = 4


def _attn_masks(has_prev):
    key = lax.broadcasted_iota(jnp.int32, (2 * BLOCK, GROUP_COLS), 0)
    query = lax.broadcasted_iota(jnp.int32, (2 * BLOCK, GROUP_COLS), 1) & (BLOCK - 1)
    band = (key > query) & (key <= query + BLOCK)
    return [band & ((key >= BLOCK) | has_prev)] + [band] * (ATTN_STEP_BLOCKS - 1)


def _heads_side_by_side(at, g, b):
    heads = [at[HEAD_DIM * (GROUP * g + hh):HEAD_DIM * (GROUP * g + hh + 1), BLOCK * b:BLOCK * (b + 1)] for hh in range(GROUP)]
    return jnp.concatenate(heads, axis=1)


def _to_token_rows(parts):
    rows = [jnp.concatenate([parts[b][g][:, BLOCK * hh:BLOCK * (hh + 1)] for b in range(ATTN_STEP_BLOCKS)], axis=1)
            for g in range(N_KV) for hh in range(GROUP)]
    return jnp.concatenate(rows, axis=0).T


def _group_sinks(sink_ref, g):
    head = lax.broadcasted_iota(jnp.int32, (1, GROUP_COLS), 1) // BLOCK
    out = jnp.full((1, GROUP_COLS), sink_ref[0, GROUP * g], F32)
    for hh in range(1, GROUP):
        out = jnp.where(head == hh, sink_ref[0, GROUP * g + hh], out)
    return out


def _attn_probs(qt, kk, sink, valid):
    s = jnp.where(valid, _mm(kk, qt), NEG_INF)
    m = jnp.maximum(jnp.max(s, axis=0, keepdims=True), sink)
    p = jnp.exp(s - m)
    psink = jnp.exp(sink - m)
    inv_l = 1.0 / (jnp.sum(p, axis=0, keepdims=True) + psink)
    return p * inv_l, psink * inv_l


ATTN_STEP = ATTN_STEP_BLOCKS * BLOCK
ATTN_KEYS = ATTN_STEP + BLOCK


def _qkv_specs(order):
    prev = lambda i: jnp.maximum(ATTN_STEP_BLOCKS * order(i) - 1, 0)
    kcol, vcol = ATTN_W // KV_W, ATTN_W // KV_W + 1
    return [pl.BlockSpec((ATTN_STEP, ATTN_W), lambda i: (order(i), 0)),
            pl.BlockSpec((BLOCK, KV_W), lambda i: (prev(i), kcol)), pl.BlockSpec((ATTN_STEP, KV_W), lambda i: (order(i), kcol)),
            pl.BlockSpec((BLOCK, KV_W), lambda i: (prev(i), vcol)), pl.BlockSpec((ATTN_STEP, KV_W), lambda i: (order(i), vcol))]


def _attn_fwd(qkv, sinks, g_attn, shards, gathered):
    t = qkv.shape[0]
    n = len(shards)

    def body(*refs):
        sink_ref, q_ref, kp_ref, kc_ref, vp_ref, vc_ref, ga_ref = refs[:7]
        attn_ref, mattn_ref = refs[7 + 2 * n:9 + 2 * n]
        step = pl.program_id(0)
        if n:
            @pl.when(step == 0)
            def _():
                x, y, c = _mesh_pos()
                _enter_with([(x, y, 1 - c), (*_relay_route(x, y, c)[1], c)])

            n_steps = t // ATTN_STEP
            _gather_far(step == 0, step == n_steps // 2, step == n_steps - 1, refs[7:7 + n], refs[7 + n:7 + 2 * n],
                        refs[9 + 2 * n:9 + 3 * n], refs[9 + 3 * n:11 + 3 * n])
        qt = (q_ref[...] * ATTN_SCALE).T
        keys = jnp.concatenate([kp_ref[...], kc_ref[...]], axis=0)
        vals = jnp.concatenate([vp_ref[...], vc_ref[...]], axis=0)
        sink = [_group_sinks(sink_ref, g) for g in range(N_KV)]
        masks = _attn_masks(step > 0)
        parts = []
        for b in range(ATTN_STEP_BLOCKS):
            window = slice(BLOCK * b, BLOCK * (b + 2))
            valid = masks[b]
            parts.append([])
            for g in range(N_KV):
                gs = slice(HEAD_DIM * g, HEAD_DIM * (g + 1))
                probs, _ = _attn_probs(_heads_side_by_side(qt, g, b), keys[window, gs], sink[g], valid)
                parts[b].append(_mm_tn(vals[window, gs], probs.astype(BF16)))
        attn = _to_token_rows(parts)
        attn_ref[...] = attn
        mattn_ref[...] = ((attn * _inv_rms(attn)) * ga_ref[...]).astype(BF16)

    blk = pl.BlockSpec((ATTN_STEP, ATTN_W), lambda j: (j, 0))
    res = pl.pallas_call(
        body, name="attn_fwd", grid=(t // ATTN_STEP,),
        in_specs=[pl.BlockSpec(memory_space=pltpu.SMEM)] + _qkv_specs(lambda j: j) + [_full((1, ATTN_W))]
        + [HBM_SPEC] * (2 * n),
        out_specs=[blk, blk] + [HBM_SPEC] * n,
        out_shape=[jax.ShapeDtypeStruct((t, ATTN_W), F32), jax.ShapeDtypeStruct((t, ATTN_W), BF16)]
        + [jax.ShapeDtypeStruct(g.shape, g.dtype) for g in gathered],
        input_output_aliases={7 + n + i: 2 + i for i in range(n)},
        scratch_shapes=[pltpu.SemaphoreType.DMA((4 * n,)), pltpu.SemaphoreType.DMA((4 * n,))] if n else [],
        compiler_params=_params("arbitrary", barrier_id=1 if n else None),
    )(sinks, qkv, qkv, qkv, qkv, qkv, g_attn, *shards, *gathered)
    return res[0], res[1], list(res[2:])


SMALL_ROWS = 8
ROW_LOSS, ROW_G2, ROW_G3, ROW_G4 = 0, 1, 2, 3


def _mid(mattn, mconv, x, target, g2, g3, g4, w_out, w_up, w_down, tm):
    t = x.shape[0]

    def body(ma_ref, mc_ref, x_ref, t_ref, g2_ref, g3_ref, g4_ref, wo_ref, wu_ref, wd_ref,
             act_ref, dup_ref, hn2t_ref, dmo_ref, dmix_ref, dh_ref, dmixed_ref, small_ref, up_ref):
        @pl.when(pl.program_id(0) == 0)
        def _():
            small_ref[...] = jnp.zeros_like(small_ref)

        g2, g3, g4 = g2_ref[...], g3_ref[...], g4_ref[...]
        mix_out = _mm(ma_ref[...], wo_ref[0:ATTN_W, :]) + _mm(mc_ref[...], wo_ref[ATTN_W:, :])
        r2 = _inv_rms(mix_out)
        mo_hat = mix_out * r2
        h = x_ref[...] + mo_hat * g2
        r3 = _inv_rms(h)
        h_hat = h * r3
        hn2 = (h_hat * g3).astype(BF16)
        hn2t_ref[...] = hn2.T
        for j in range(MID_CHUNKS):
            cols_j = slice(MID_CHUNK * j, MID_CHUNK * (j + 1))
            up = jnp.maximum(_mm(hn2, wu_ref[:, cols_j]), 0.0)
            up_ref[:, cols_j] = up.astype(BF16)
            act_ref[:, cols_j] = (up * up).astype(BF16)
        mlp = _mm(act_ref[...], wd_ref[...])
        r4 = _inv_rms(mlp)
        ml_hat = mlp * r4
        err = (h + ml_hat * g4) - t_ref[...]
        d_out = err * (1.0 / D_MODEL)
        d_mlp, dg4 = _rms_bwd(ml_hat, r4, g4, d_out)
        dmo = d_mlp.astype(BF16)
        dmo_ref[...] = dmo
        for j in range(MID_CHUNKS):
            cols_j = slice(MID_CHUNK * j, MID_CHUNK * (j + 1))
            dact = _mm_nt(dmo, wd_ref[cols_j, :])
            dup_ref[:, cols_j] = (dact * (2.0 * up_ref[:, cols_j].astype(F32))).astype(BF16)
        dhn2 = _mm_nt(dup_ref[...], wu_ref[...])
        dh_norm, dg3 = _rms_bwd(h_hat, r3, g3, dhn2)
        dh = d_out + dh_norm
        dh_ref[...] = dh
        d_mix, dg2 = _rms_bwd(mo_hat, r2, g2, dh)
        dmix = d_mix.astype(BF16)
        dmix_ref[...] = dmix
        dmixed_ref[...] = _mm_nt(dmix, wo_ref[...])
        small_ref[ROW_LOSS:ROW_LOSS + 1, :] += _colsum(err * err)
        small_ref[ROW_G2:ROW_G2 + 1, :] += _colsum(dg2)
        small_ref[ROW_G3:ROW_G3 + 1, :] += _colsum(dg3)
        small_ref[ROW_G4:ROW_G4 + 1, :] += _colsum(dg4)

    tile = lambda n: pl.BlockSpec((tm, n), lambda i: (i, 0))
    cols = lambda n: pl.BlockSpec((n, tm), lambda i: (0, i))
    gain = _full((1, D_MODEL))
    return pl.pallas_call(
        body, name="mid_fwd_bwd", grid=(t // tm,),
        in_specs=[tile(ATTN_W), tile(CONV_W), tile(D_MODEL), tile(D_MODEL), gain, gain, gain,
                  _resident((D_MODEL, D_MODEL)), _resident((D_MODEL, D_FF)), _resident((D_FF, D_MODEL))],
        out_specs=[tile(D_FF), tile(D_FF), cols(D_MODEL), tile(D_MODEL), tile(D_MODEL), tile(D_MODEL), tile(D_MODEL),
                   _full((SMALL_ROWS, D_MODEL))],
        out_shape=[jax.ShapeDtypeStruct((t, D_FF), BF16), jax.ShapeDtypeStruct((t, D_FF), BF16),
                   jax.ShapeDtypeStruct((D_MODEL, t), BF16), jax.ShapeDtypeStruct((t, D_MODEL), BF16),
                   jax.ShapeDtypeStruct((t, D_MODEL), BF16), jax.ShapeDtypeStruct((t, D_MODEL), F32),
                   jax.ShapeDtypeStruct((t, D_MODEL), F32), jax.ShapeDtypeStruct((SMALL_ROWS, D_MODEL), F32)],
        scratch_shapes=[pltpu.VMEM((tm, D_FF), BF16)],
        compiler_params=_params("arbitrary"),
    )(mattn, mconv, x, target, g2, g3, g4, w_out, w_up, w_down)


CHIP_FLIPS = ((1, 1), (1, 0), (0, 1))


def _block_order(dev):
    chip_masks = [4 * fx + 2 * fy for fx, fy in CHIP_FLIPS]
    masks = [m + 1 for m in chip_masks] + [1] + chip_masks + [0]
    return jnp.bitwise_xor(dev, jnp.asarray(masks, jnp.int32)).astype(jnp.int32)


def _other_chips(x, y, c):
    return [(1 - x if fx else x, 1 - y if fy else y, c) for fx, fy in CHIP_FLIPS]


def _dw_pair_sums(operands, order, which, name, barrier_id, ride=None):
    t = operands[-1].shape[0]
    n_far = len(CHIP_FLIPS)
    n_in = len(operands)
    n_ride = 0 if ride is None else 1
    out_chunk = D_MODEL // N_DEV
    if which == "up":
        rows, cols = D_MODEL, FF_CHUNK
        in_specs = [_resident((D_MODEL, t)), pl.BlockSpec((t, FF_CHUNK), lambda s, order_ref: (0, order_ref[s]))]
    elif which == "down":
        rows, cols = FF_CHUNK, D_MODEL
        in_specs = [pl.BlockSpec((t, FF_CHUNK), lambda s, order_ref: (0, order_ref[s])), _resident((t, D_MODEL))]
    else:
        rows, cols = out_chunk, D_MODEL
        half = pl.BlockSpec((t, out_chunk), lambda s, order_ref: (0, order_ref[s] % (N_DEV // 2)))
        in_specs = [half, half, _resident((t, D_MODEL))]

    def body(order_ref, *refs):
        own_ref, from_sib_ref, pair_ref = refs[n_in + n_ride:n_in + n_ride + 3]
        send_buf, land_buf, send_sems, recv_sems = refs[n_in + 2 * n_ride + 3:n_in + 2 * n_ride + 7]
        s_now = pl.program_id(0)
        x, y, c = _mesh_pos()
        sibling = (x, y, 1 - c)
        sems = (send_sems, recv_sems)

        @pl.when(s_now == 0)
        def _():
            _enter_with([sibling] + (_other_chips(x, y, c) if n_ride else []))

        if n_ride:
            _chip_exchange_beside(s_now == 0, s_now == N_DEV - 1, [refs[n_in]], [refs[n_in + 3 + n_ride]],
                                  refs[n_in + 2 * n_ride + 7:], enter=False)

        def hand_over(k):
            dst = land_buf.at[k] if k < n_far else from_sib_ref
            return _push(send_buf.at[k], dst, sems, k, sibling)

        if which == "out":
            ma_ref, mc_ref, b_ref = refs[:n_in]
            block = lax.cond(order_ref[s_now] < N_DEV // 2, lambda: _mm_tn(ma_ref[...], b_ref[...]),
                             lambda: _mm_tn(mc_ref[...], b_ref[...]))
        elif which == "down":
            block = _mm_tn(refs[0][...], refs[1][...])
        else:
            block = _mm(refs[0][...], refs[1][...])
        for k in range(n_far + 1):
            @pl.when(s_now == k)
            def _():
                send_buf[k] = block.astype(BF16)
                hand_over(k).start()

        for k in range(n_far):
            @pl.when(s_now == n_far + 1 + k)
            def _():
                hand_over(k).wait_recv()
                pair_ref[...] = (block + land_buf[k].astype(F32)).astype(BF16)

        @pl.when(s_now == N_DEV - 1)
        def _():
            own_ref[...] = block
            for k in range(n_far + 1):
                hand_over(k).wait_send()
            hand_over(n_far).wait_recv()

    rides = [] if ride is None else [ride]
    sems = lambda k: pltpu.SemaphoreType.DMA((k,))
    return pl.pallas_call(
        body, name=name,
        grid_spec=pltpu.PrefetchScalarGridSpec(
            num_scalar_prefetch=1, grid=(N_DEV,), in_specs=in_specs + [HBM_SPEC] * n_ride,
            out_specs=[pl.BlockSpec((rows, cols), lambda s, order_ref: (0, 0)), HBM_SPEC,
                       pl.BlockSpec((None, rows, cols), lambda s, order_ref: (jnp.clip(s - n_far - 1, 0, n_far - 1), 0, 0))]
            + [HBM_SPEC] * n_ride,
            scratch_shapes=[pltpu.VMEM((n_far + 1, rows, cols), BF16), pltpu.VMEM((n_far, rows, cols), BF16),
                            sems(n_far + 1), sems(n_far + 1)] + [sems(n_far), sems(n_far)] * n_ride),
        out_shape=[jax.ShapeDtypeStruct((rows, cols), F32), jax.ShapeDtypeStruct((rows, cols), BF16),
                   jax.ShapeDtypeStruct((n_far, rows, cols), BF16)]
        + [jax.ShapeDtypeStruct(r.shape, r.dtype) for r in rides],
        compiler_params=_params("arbitrary", barrier_id=barrier_id),
    )(order, *operands, *rides)


def _chip_exchange_beside(first, last, sums, outs, sems, enter=True):
    chips = _other_chips(*_mesh_pos())
    copies = [_push(sums[i].at[k], outs[i].at[k], sems, len(chips) * i + k, chip)
              for i in range(len(sums)) for k, chip in enumerate(chips)]

    @pl.when(first)
    def _():
        if enter:
            _enter_with(chips)
        for cp in copies:
            cp.start()

    @pl.when(last)
    def _():
        for cp in copies:
            cp.wait()


ROW_GCONV, ROW_CW0 = 1, 2


def _conv_bwd(dmixed, gates, g_conv, conv_w, tm):
    t = gates.shape[0]
    n = t // tm
    rev = lambda i: n - 1 - i

    def body(dm_ref, gates_ref, gprev_ref, gc_ref, cw_ref, dgates_ref, small_ref, carry_ref):
        i = pl.program_id(0)

        @pl.when(i == 0)
        def _():
            small_ref[...] = jnp.zeros_like(small_ref)
            carry_ref[...] = jnp.zeros_like(carry_ref)

        gates = gates_ref[...]
        gb, gcc, xin = gates[:, :CONV_W], gates[:, CONV_W:2 * CONV_W], gates[:, 2 * CONV_W:]
        u = gcc * xin
        gp = gprev_ref[...]
        uprev = jnp.where(rev(i) == 0, 0.0, gp[:, CONV_W:2 * CONV_W] * gp[:, 2 * CONV_W:])
        u1, u2 = _shift_rows_down(u, uprev, 1), _shift_rows_down(u, uprev, 2)
        w = cw_ref[...]
        c = _conv3(u, u1, u2, w)
        conv = gb * c
        rcv = _inv_rms(conv)
        c_hat = conv * rcv
        dconv, dgc = _rms_bwd(c_hat, rcv, gc_ref[...], dm_ref[...])
        dc = dconv * gb
        nxt = carry_ref[...]
        du = (w[2:3, :] * dc + w[1:2, :] * _shift_rows_up(dc, nxt, 1)) + w[0:1, :] * _shift_rows_up(dc, nxt, 2)
        carry_ref[...] = dc[0:8, :]
        dgates_ref[:, :CONV_W] = (dconv * c).astype(BF16)
        dgates_ref[:, CONV_W:2 * CONV_W] = (du * xin).astype(BF16)
        dgates_ref[:, 2 * CONV_W:] = (du * gcc).astype(BF16)
        small_ref[ROW_GCONV:ROW_GCONV + 1, :] += _colsum(dgc)
        small_ref[ROW_CW0:ROW_CW0 + 1, :] += _colsum(dc * u2)
        small_ref[ROW_CW0 + 1:ROW_CW0 + 2, :] += _colsum(dc * u1)
        small_ref[ROW_CW0 + 2:ROW_CW0 + 3, :] += _colsum(dc * u)

    tile = lambda w_: pl.BlockSpec((tm, w_), lambda i: (rev(i), 0))
    prev8 = pl.BlockSpec((8, GATES_W), lambda i: (jnp.maximum(rev(i) * (tm // 8) - 1, 0), 0))
    conv_half = pl.BlockSpec((tm, CONV_W), lambda i: (rev(i), ATTN_W // CONV_W))
    return pl.pallas_call(
        body, name="conv_bwd", grid=(n,),
        in_specs=[conv_half, tile(GATES_W), prev8, _full((1, CONV_W)), _full((3, CONV_W))],
        out_specs=[tile(GATES_W), _full((SMALL_ROWS, CONV_W))],
        out_shape=[jax.ShapeDtypeStruct((t, GATES_W), BF16), jax.ShapeDtypeStruct((SMALL_ROWS, CONV_W), F32)],
        scratch_shapes=[pltpu.VMEM((8, CONV_W), F32)],
        compiler_params=_params("arbitrary"),
    )(dmixed, gates, gates, g_conv, conv_w)


def _attn_bwd(qkv, dmixed, attn, g_attn, sinks, rope, sums):
    t = qkv.shape[0]
    n_steps = t // ATTN_STEP
    rev = lambda i: n_steps - 1 - i
    rc, rs1, rs2 = rope

    def body(sink_ref, q_ref, kp_ref, kc_ref, vp_ref, vc_ref, dm_ref, attn_ref, ga_ref, c_ref, s1_ref, s2_ref, sums_ref,
             dqkv_ref, dsink_ref, dgain_ref, arrived_ref, ck_ref, cv_ref, kacc_ref, vacc_ref, send_sems, recv_sems):
        i = pl.program_id(0)
        _chip_exchange_beside(i == 0, i == n_steps - 1, [sums_ref], [arrived_ref], (send_sems, recv_sems))

        @pl.when(i == 0)
        def _():
            dsink_ref[...] = jnp.zeros_like(dsink_ref)
            dgain_ref[...] = jnp.zeros_like(dgain_ref)
            ck_ref[...] = jnp.zeros_like(ck_ref)
            cv_ref[...] = jnp.zeros_like(cv_ref)

        kacc_ref[...] = jnp.zeros_like(kacc_ref)
        vacc_ref[...] = jnp.zeros_like(vacc_ref)
        a = attn_ref[...]
        ra = _inv_rms(a)
        dattn, dgain = _rms_bwd(a * ra, ra, ga_ref[...], dm_ref[...])
        dgain_ref[0:1, :] += _colsum(dgain)
        qt = (q_ref[...] * ATTN_SCALE).T
        dot = dattn.astype(BF16).T
        keys = jnp.concatenate([kp_ref[...], kc_ref[...]], axis=0)
        vals = jnp.concatenate([vp_ref[...], vc_ref[...]], axis=0)
        sink = [_group_sinks(sink_ref, g) for g in range(N_KV)]
        c, s1, s2 = c_ref[...], s1_ref[...], s2_ref[...]
        lane = lax.broadcasted_iota(jnp.int32, (1, 128), 1)
        dsink = jnp.zeros((1, 128), F32)
        masks = _attn_masks(rev(i) > 0)
        dq_parts = []
        for b in range(ATTN_STEP_BLOCKS):
            window = slice(BLOCK * b, BLOCK * (b + 2))
            valid = masks[b]
            dq_parts.append([])
            dk_parts, dv_parts = [], []
            for g in range(N_KV):
                gs = slice(HEAD_DIM * g, HEAD_DIM * (g + 1))
                kk, vv = keys[window, gs], vals[window, gs]
                qtg, dotg = _heads_side_by_side(qt, g, b), _heads_side_by_side(dot, g, b)
                probs, psink = _attn_probs(qtg, kk, sink[g], valid)
                dp = _mm(vv, dotg)
                delta = jnp.sum(probs * dp, axis=0, keepdims=True)
                ds = (probs * (dp - delta)).astype(BF16)
                sink_terms = psink * delta
                for hh in range(GROUP):
                    head_sum = jnp.sum(sink_terms[:, BLOCK * hh:BLOCK * (hh + 1)])
                    dsink = dsink + jnp.where(lane == GROUP * g + hh, -head_sum, 0.0)
                dq_parts[b].append(_mm_tn(kk * ATTN_SCALE, ds))
                dk_parts.append(_mm_nt(ds, qtg))
                dv_parts.append(_mm_nt(probs.astype(BF16), dotg))
            kacc_ref[window, :] += jnp.concatenate(dk_parts, axis=1)
            vacc_ref[window, :] += jnp.concatenate(dv_parts, axis=1)
        dq = _to_token_rows(dq_parts)
        for ci in range(ATTN_W // 128):
            sl = slice(128 * ci, 128 * (ci + 1))
            dqkv_ref[:, sl] = _rope_transpose(dq[:, sl], c, s1, s2).astype(BF16)
        kacc_ref[ATTN_STEP:, :] += ck_ref[...]
        vacc_ref[ATTN_STEP:, :] += cv_ref[...]
        ck_ref[...] = kacc_ref[:BLOCK, :]
        cv_ref[...] = vacc_ref[:BLOCK, :]
        dqkv_ref[:, ATTN_W:ATTN_W + KV_W] = _rope_transpose(kacc_ref[BLOCK:, :], c, s1, s2).astype(BF16)
        dqkv_ref[:, ATTN_W + KV_W:] = vacc_ref[BLOCK:, :].astype(BF16)
        dsink_ref[0:1, :] += dsink

    blk = lambda w_: pl.BlockSpec((ATTN_STEP, w_), lambda i: (rev(i), 0))
    return pl.pallas_call(
        body, name="attn_bwd", grid=(n_steps,),
        in_specs=[pl.BlockSpec(memory_space=pltpu.SMEM)] + _qkv_specs(rev)
        + [blk(ATTN_W), blk(ATTN_W), _full((1, ATTN_W)), blk(128), blk(128), blk(128), HBM_SPEC],
        out_specs=[blk(QKV_W), _full((8, 128)), _full((SMALL_ROWS, ATTN_W)), HBM_SPEC],
        out_shape=[jax.ShapeDtypeStruct((t, QKV_W), BF16), jax.ShapeDtypeStruct((8, 128), F32),
                   jax.ShapeDtypeStruct((SMALL_ROWS, ATTN_W), F32), jax.ShapeDtypeStruct(sums.shape, sums.dtype)],
        scratch_shapes=[pltpu.VMEM((BLOCK, KV_W), F32), pltpu.VMEM((BLOCK, KV_W), F32),
                        pltpu.VMEM((ATTN_KEYS, KV_W), F32), pltpu.VMEM((ATTN_KEYS, KV_W), F32),
                        pltpu.SemaphoreType.DMA((len(CHIP_FLIPS),)), pltpu.SemaphoreType.DMA((len(CHIP_FLIPS),))],
        compiler_params=_params("arbitrary", barrier_id=6),
    )(sinks, qkv, qkv, qkv, qkv, qkv, dmixed, attn, g_attn, rc, rs1, rs2, sums)


def _grad_x_tile(dq, dg, x_hat, r, g1, w_ref, dh):
    dhn = _mm_nt(dq, w_ref[:, :QKV_W]) + _mm_nt(dg, w_ref[:, QKV_W:])
    dx, dg1 = _rms_bwd(x_hat, r, g1, dhn)
    return dh + dx, _colsum(dg1)


def _in_proj_bwd(dqkv, dgates, x, dh, g1, w_in, tm, out_sums):
    t = x.shape[0]
    n = t // tm
    n_cover = max(n // 2, 1)
    n_steps = n + n_cover
    n_far = len(CHIP_FLIPS)
    shard = (D_MODEL, IN_SHARD)

    def body(dq_ref, dg_ref, x_ref, dh_ref, g1_ref, w_ref, osums_ref,
             dx_ref, own_ref, sib_ref, far_ref, dg1_ref, oarrived_ref,
             acc_ref, send_buf, land_buf, pair_buf, d2d_send, d2d_recv, ici_send, ici_recv, o_send, o_recv):
        i = pl.program_id(0)
        x_pos, y_pos, c = _mesh_pos()
        my_chip = 2 * x_pos + y_pos
        sibling = (x_pos, y_pos, 1 - c)
        @pl.when(i == 0)
        def _():
            _enter_with(_sibling_and_chips(x_pos, y_pos, c))

        _chip_exchange_beside(i == 0, i == n_steps - 1, [osums_ref], [oarrived_ref], (o_send, o_recv), enter=False)

        def cols(d):
            return slice(IN_SHARD * d, IN_SHARD * (d + 1))

        def hand_over(chip):
            return _push(send_buf.at[chip], land_buf.at[chip], (d2d_send, d2d_recv), chip, sibling)

        def to_chip(chip, rel):
            return pltpu.make_async_remote_copy(
                src_ref=pair_buf.at[chip], dst_ref=far_ref.at[rel - 1], send_sem=ici_send.at[rel - 1],
                recv_sem=ici_recv.at[rel - 1], device_id=(chip // 2, chip % 2, c), device_id_type=MESH)

        @pl.when(i == 0)
        def _():
            acc_ref[...] = jnp.zeros_like(acc_ref)
            dg1_ref[...] = jnp.zeros_like(dg1_ref)

        def normed_x():
            xv = x_ref[...]
            r = _inv_rms(xv)
            return xv * r, r

        @pl.when(i < n)
        def _():
            hn = (normed_x()[0] * g1_ref[...]).astype(BF16)
            acc_ref[:, :QKV_W] += _mm_tn(hn, dq_ref[...])
            acc_ref[:, QKV_W:] += _mm_tn(hn, dg_ref[...])

        @pl.when(i == n - 1)
        def _():
            for d in range(N_DEV):
                @pl.when(d % 2 != c)
                def _():
                    send_buf[d // 2] = acc_ref[:, cols(d)].astype(BF16)
                    hand_over(d // 2).start()
            for d in range(N_DEV):
                chip = d // 2

                @pl.when(d % 2 == c)
                def _():
                    hand_over(chip).wait_recv()

                    @pl.when(chip == my_chip)
                    def _():
                        own_ref[...] = acc_ref[:, cols(d)]
                        sib_ref[...] = land_buf[chip]

                    @pl.when(chip != my_chip)
                    def _():
                        pair_buf[chip] = (acc_ref[:, cols(d)] + land_buf[chip].astype(F32)).astype(BF16)
                        to_chip(chip, chip ^ my_chip).start()
            for chip in range(N_CHIPS):
                hand_over(chip).wait_send()

        @pl.when(i >= n)
        def _():
            x_hat, r = normed_x()
            dx_ref[...], dg1 = _grad_x_tile(dq_ref[...], dg_ref[...], x_hat, r, g1_ref[...], w_ref, dh_ref[...])
            dg1_ref[0:1, :] += dg1

        @pl.when(i == n_steps - 1)
        def _():
            for rel in range(1, n_far + 1):
                to_chip(0, rel).wait()

    both = lambda w_: pl.BlockSpec((tm, w_), lambda i: (i % n, 0))
    second = pl.BlockSpec((tm, D_MODEL), lambda i: (jnp.maximum(i - n, 0), 0))
    whole = lambda dtype: jax.ShapeDtypeStruct(shard, dtype)
    sems = lambda k: pltpu.SemaphoreType.DMA((k,))
    res = pl.pallas_call(
        body, name="in_proj_bwd", grid=(n_steps,),
        in_specs=[both(QKV_W), both(GATES_W), both(D_MODEL), second, _full((1, D_MODEL)), _resident((D_MODEL, IN_COLS)),
                  HBM_SPEC],
        out_specs=[second, _full(shard), _full(shard), HBM_SPEC, _full((SMALL_ROWS, D_MODEL)), HBM_SPEC],
        out_shape=[jax.ShapeDtypeStruct((n_cover * tm, D_MODEL), F32), whole(F32), whole(BF16),
                   jax.ShapeDtypeStruct((n_far,) + shard, BF16), jax.ShapeDtypeStruct((SMALL_ROWS, D_MODEL), F32),
                   jax.ShapeDtypeStruct(out_sums.shape, out_sums.dtype)],
        scratch_shapes=[pltpu.VMEM((D_MODEL, IN_COLS), F32), pltpu.VMEM((N_CHIPS,) + shard, BF16),
                        pltpu.VMEM((N_CHIPS,) + shard, BF16), pltpu.VMEM((N_CHIPS,) + shard, BF16),
                        sems(N_CHIPS), sems(N_CHIPS), sems(n_far), sems(n_far), sems(n_far), sems(n_far)],
        compiler_params=_params("arbitrary", barrier_id=7),
    )(dqkv, dgates, x, dh, g1, w_in, out_sums)
    return res[0], (res[1], res[2], res[3]), res[4], res[5]


def _grad_x_rest(dqkv, dgates, x, dh, g1, w_in, tm, head, dg1_rows):
    t = x.shape[0]
    first = head.shape[0] // tm
    n_rest = t // tm - first
    if n_rest == 0:
        return head, dg1_rows
    assert first <= n_rest

    def body(dq_ref, dg_ref, x_ref, dh_ref, g1_ref, w_ref, head_ref, rows_ref, gx_ref, dg1_ref, stage, sems):
        j = pl.program_id(0)

        def tile_out(step, kind):
            row0 = (step + first) * tm if kind == 0 else step * tm
            slot = 2 * kind + step % 2
            return pltpu.make_async_copy(stage.at[slot], gx_ref.at[pl.ds(pl.multiple_of(row0, tm), tm), :], sems.at[slot])

        @pl.when(j == 0)
        def _():
            dg1_ref[...] = rows_ref[...]

        @pl.when(j >= 2)
        def _():
            tile_out(j - 2, 0).wait()

        @pl.when((j >= 2) & (j - 2 < first))
        def _():
            tile_out(j - 2, 1).wait()

        @pl.when(j < first)
        def _():
            stage[2 + j % 2] = head_ref[...]
            tile_out(j, 1).start()

        xv = x_ref[...]
        r = _inv_rms(xv)
        dx, dg1 = _grad_x_tile(dq_ref[...], dg_ref[...], xv * r, r, g1_ref[...], w_ref, dh_ref[...])
        stage[j % 2] = dx
        dg1_ref[0:1, :] += dg1
        tile_out(j, 0).start()

        @pl.when(j == n_rest - 1)
        def _():
            for back in range(min(2, n_rest)):
                tile_out(j - back, 0).wait()

                @pl.when(j - back < first)
                def _():
                    tile_out(j - back, 1).wait()

    tile = lambda w_: pl.BlockSpec((tm, w_), lambda j: (j + first, 0))
    head_tile = pl.BlockSpec((tm, D_MODEL), lambda j: (jnp.minimum(j, first - 1), 0))
    return pl.pallas_call(
        body, name="grad_x_rest", grid=(n_rest,),
        in_specs=[tile(QKV_W), tile(GATES_W), tile(D_MODEL), tile(D_MODEL), _full((1, D_MODEL)),
                  _resident((D_MODEL, IN_COLS)), head_tile, _full((SMALL_ROWS, D_MODEL))],
        out_specs=[HBM_SPEC, _full((SMALL_ROWS, D_MODEL))],
        out_shape=[jax.ShapeDtypeStruct((t, D_MODEL), F32), jax.ShapeDtypeStruct((SMALL_ROWS, D_MODEL), F32)],
        scratch_shapes=[pltpu.VMEM((4, tm, D_MODEL), F32), pltpu.SemaphoreType.DMA((4,))],
        compiler_params=_params("arbitrary"),
    )(dqkv, dgates, x, dh, g1, w_in, head, dg1_rows)


def _all_gather_small(block, name):
    def body(in_ref, out_ref, send_sems, recv_sems, local_sem):
        x, y, c = _mesh_pos()
        me = (x, y, c)
        peers = [(x ^ (k >> 2), y ^ ((k >> 1) & 1), c ^ (k & 1)) for k in range(1, N_DEV)]
        _enter_with(peers)
        mine = pltpu.make_async_copy(in_ref, _slot(out_ref, me), local_sem)
        mine.start()
        sends = [_push(in_ref, _slot(out_ref, me), (send_sems, recv_sems), k, peer) for k, peer in enumerate(peers)]
        for cp in sends:
            cp.start()
        for k, peer in enumerate(peers):
            _push(in_ref, _slot(out_ref, peer), (send_sems, recv_sems), k, peer).wait_recv()
        for cp in sends:
            cp.wait_send()
        mine.wait()

    return pl.pallas_call(
        body, name=name,
        in_specs=[HBM_SPEC], out_specs=HBM_SPEC,
        out_shape=jax.ShapeDtypeStruct((N_DEV,) + block.shape, block.dtype),
        scratch_shapes=[pltpu.SemaphoreType.DMA((N_DEV - 1,)), pltpu.SemaphoreType.DMA((N_DEV - 1,)),
                        pltpu.SemaphoreType.DMA],
        compiler_params=_params(barrier_id=8),
    )(block)


def _adam_math(w, g, m, v):
    m = ADAM_B1 * m + (1.0 - ADAM_B1) * g
    v = ADAM_B2 * v + (1.0 - ADAM_B2) * (g * g)
    m_hat = m / (1.0 - ADAM_B1 ** ADAM_STEP)
    v_hat = v / (1.0 - ADAM_B2 ** ADAM_STEP)
    delta = -ADAM_LR * (m_hat / (jnp.sqrt(v_hat) + ADAM_EPS) + ADAM_WD * w)
    return delta, m, v


def _adamw_reduced(w, m, v, own, from_sibling, from_chips, tr):
    rows, cols = w.shape

    def body(w_ref, m_ref, v_ref, own_ref, sib_ref, far_ref, g_ref, d_ref, nm_ref, nv_ref):
        g = own_ref[...] + sib_ref[...].astype(F32)
        for k in range(len(CHIP_FLIPS)):
            g = g + far_ref[k].astype(F32)
        g_ref[...] = g
        d_ref[...], nm_ref[...], nv_ref[...] = _adam_math(w_ref[...], g, m_ref[...], v_ref[...])

    tile = pl.BlockSpec((tr, cols), lambda i: (i, 0))
    out = jax.ShapeDtypeStruct((rows, cols), F32)
    return pl.pallas_call(
        body, name="adamw_reduced", grid=(rows // tr,),
        in_specs=[tile] * 5 + [pl.BlockSpec((len(CHIP_FLIPS), tr, cols), lambda i: (0, i, 0))],
        out_specs=[tile] * 4, out_shape=[out] * 4,
        compiler_params=_params("parallel"),
    )(w, m, v, own, from_sibling, from_chips)


SMALL_PARAMS = ("pre_mix_norm", "post_mix_norm", "pre_mlp_norm", "post_mlp_norm", "attn_group_norm", "conv_group_norm",
                "conv_w", "attn_sinks")


SMALL_WIDTHS = (D_MODEL, CONV_W, ATTN_W, 128, D_MODEL)


def _small_tail(gathered, dev, weights, first_moments, second_moments):
    n = len(SMALL_PARAMS)
    conv_shard = CONV_W // N_DEV

    def body(dev_ref, sums_ref, *refs):
        w_refs, m_refs, v_refs = refs[:n], refs[n:2 * n], refs[2 * n:3 * n]
        loss_ref, outs = refs[3 * n], refs[3 * n + 1:]
        total = sums_ref[0]
        for d in range(1, N_DEV):
            total = total + sums_ref[d]
        starts = [sum(SMALL_WIDTHS[:i]) for i in range(len(SMALL_WIDTHS))]
        mid, conv, gain, sink, inp = (total[:, a:a + w_] for a, w_ in zip(starts, SMALL_WIDTHS))
        loss_ref[...] = (0.5 / D_MODEL) * jnp.sum(mid[ROW_LOSS:ROW_LOSS + 1, :], axis=1, keepdims=True)
        conv_rows = conv[ROW_CW0:ROW_CW0 + 3, :]
        conv_g = jnp.zeros((3, conv_shard), F32)
        for d in range(N_DEV):
            conv_g = conv_g + jnp.where(dev_ref[0] == d, conv_rows[:, conv_shard * d:conv_shard * (d + 1)], 0.0)
        grads = [inp[0:1, :], mid[ROW_G2:ROW_G2 + 1, :], mid[ROW_G3:ROW_G3 + 1, :], mid[ROW_G4:ROW_G4 + 1, :],
                 gain[0:1, :], conv[ROW_GCONV:ROW_GCONV + 1, :], conv_g, sink[0:1, :N_HEADS]]
        for i, g in enumerate(grads):
            delta, new_m, new_v = _adam_math(w_refs[i][...], g, m_refs[i][...], v_refs[i][...])
            outs[i][...], outs[n + i][...], outs[2 * n + i][...], outs[3 * n + i][...] = g, delta, new_m, new_v

    params = list(weights) + list(first_moments) + list(second_moments)
    shapes = [jax.ShapeDtypeStruct(w.shape, F32) for w in weights]
    res = pl.pallas_call(
        body, name="small_tail", grid=(1,),
        in_specs=[pl.BlockSpec(memory_space=pltpu.SMEM), _full(gathered.shape)] + [_full(p.shape) for p in params],
        out_specs=[_full((1, 1))] + [_full(sh.shape) for sh in shapes] * 4,
        out_shape=[jax.ShapeDtypeStruct((1, 1), F32)] + shapes * 4,
    )(dev, gathered, *params)
    return res[0], [res[1 + k * n:1 + (k + 1) * n] for k in range(4)]


TOKEN_TILE = 512
MID_TILE = 256
MID_CHUNK = 1024
MID_CHUNKS = D_FF // MID_CHUNK
ADAM_ROWS = 512


def _local_grads(x, target, g1, w_in_shard, conv_shard, sinks, g_attn, g_conv, g2, g3, g4, shards, order):
    t = x.shape[0]
    tm = min(TOKEN_TILE, t)
    rope = _rope_tables(t)
    qkv, gates, mconv, w_in, conv_w, gathered = _in_proj_fwd(x, g1, w_in_shard, conv_shard, g_conv, rope, tm, shards,
                                                             (False, True, False))
    attn, mattn, (w_out, w_up, w_down) = _attn_fwd(qkv, sinks, g_attn, shards, gathered)
    act, dup, hn2t, dmo, dmix, dh, dmixed, small_mid = _mid(
        mattn, mconv, x, target, g2, g3, g4, w_out.reshape(D_MODEL, D_MODEL),
        w_up, w_down.reshape(D_FF, D_MODEL), min(MID_TILE, t))
    up_own, up_sib, up_sums = _dw_pair_sums((hn2t, dup), order, "up", "dw_up", 2)
    down_own, down_sib, down_sums, up_far = _dw_pair_sums((act, dmo), order, "down", "dw_down", 3, ride=up_sums)
    out_own, out_sib, out_sums = _dw_pair_sums((mattn, mconv, dmix), order, "out", "dw_out", 4)
    dgates, small_conv = _conv_bwd(dmixed, gates, g_conv, conv_w, tm)
    dqkv, dsink, dg_attn, down_far = _attn_bwd(qkv, dmixed, attn, g_attn, sinks, rope, down_sums)
    grad_x_head, dw_in, small_in, out_far = _in_proj_bwd(dqkv, dgates, x, dh, g1, w_in, tm, out_sums)
    grad_x, small_in = _grad_x_rest(dqkv, dgates, x, dh, g1, w_in, tm, grad_x_head, small_in)
    dw_out, dw_up, dw_down = (out_own, out_sib, out_far), (up_own, up_sib, up_far), (down_own, down_sib, down_far)
    return grad_x, dw_in, dw_out, dw_up, dw_down, (small_mid, small_conv, dg_attn, dsink, small_in)


def kernel(x, pre_mix_norm, w_in, conv_w, attn_sinks, attn_group_norm, conv_group_norm, w_out, post_mix_norm, pre_mlp_norm, w_up, w_down, post_mlp_norm, loss_target, m_pre_mix_norm, m_w_in, m_conv_w, m_attn_sinks, m_attn_group_norm, m_conv_group_norm, m_w_out, m_post_mix_norm, m_pre_mlp_norm, m_w_up, m_w_down, m_post_mlp_norm, v_pre_mix_norm, v_w_in, v_conv_w, v_attn_sinks, v_attn_group_norm, v_conv_group_norm, v_w_out, v_post_mix_norm, v_pre_mlp_norm, v_w_up, v_w_down, v_post_mlp_norm):
    xi, yi, ci = _mesh_pos()
    chip = 2 * xi + yi
    dev = 2 * chip + ci

    order = _block_order(dev)

    shards = [w_out[0].astype(BF16), w_up[0].astype(BF16), w_down[0].astype(BF16)]

    grad_x, dw_in, dw_out, dw_up, dw_down, smalls = _local_grads(
        x[0], loss_target[0], pre_mix_norm, w_in[0].astype(BF16), conv_w[0], attn_sinks, attn_group_norm, conv_group_norm,
        post_mix_norm, pre_mlp_norm, post_mlp_norm, shards, order)

    big = {}
    for name, w, m, v, (own, sib, far) in zip(
            ("w_in", "w_out", "w_up", "w_down"), (w_in, w_out, w_up, w_down), (m_w_in, m_w_out, m_w_up, m_w_down),
            (v_w_in, v_w_out, v_w_up, v_w_down), (dw_in, dw_out, dw_up, dw_down)):
        big[name] = [a[None] for a in _adamw_reduced(w[0], m[0], v[0], own, sib, far, min(ADAM_ROWS, w.shape[1]))]

    flat = lambda a: a.reshape(-1, a.shape[-1])
    loss, small = _small_tail(
        _all_gather_small(jnp.concatenate(smalls, axis=1), "gather_small"), dev.reshape(1).astype(jnp.int32),
        [flat(a) for a in (pre_mix_norm, post_mix_norm, pre_mlp_norm, post_mlp_norm, attn_group_norm, conv_group_norm,
                           conv_w, attn_sinks)],
        [flat(a) for a in (m_pre_mix_norm, m_post_mix_norm, m_pre_mlp_norm, m_post_mlp_norm, m_attn_group_norm,
                           m_conv_group_norm, m_conv_w, m_attn_sinks)],
        [flat(a) for a in (v_pre_mix_norm, v_post_mix_norm, v_pre_mlp_norm, v_post_mlp_norm, v_attn_group_norm,
                           v_conv_group_norm, v_conv_w, v_attn_sinks)])

    order = ("pre_mix_norm", "w_in", "conv_w", "attn_sinks", "attn_group_norm", "conv_group_norm", "w_out",
             "post_mix_norm", "pre_mlp_norm", "w_up", "w_down", "post_mlp_norm")
    shape_of = {"conv_w": conv_w.shape}
    outs = []
    for k in range(4):
        by_name = dict(zip(SMALL_PARAMS, small[k]))
        outs += [big[nm][k] if nm in big else by_name[nm].reshape(shape_of.get(nm, by_name[nm].shape)) for nm in order]
    loss = loss.reshape(())
    return (loss, grad_x[None], *outs)
```

```python
import jax
import jax.numpy as jnp
import numpy as np
from jax import lax
from jax.experimental import pallas as pl
from jax.experimental.pallas import tpu as pltpu

F32 = jnp.float32
BF16 = jnp.bfloat16

D_MODEL = 1024
HEAD_DIM = 64
ATTN_W = 512
CONV_W = 512
N_HEADS = 8
N_KV = 2
GROUP = 4
KV_W = 128
QKV_W = ATTN_W + 2 * KV_W
GATES_W = 3 * CONV_W
IN_COLS = QKV_W + GATES_W
D_FF = 4096
FF_CHUNK = 512
BLOCK = 128
ROT_HALF = 8
ROPE_THETA = 500000.0
NORM_EPS = 1e-6
NEG_INF = -1e30
ATTN_SCALE = 0.125
N_DEV = 8
N_CHIPS = 4
IN_SHARD = IN_COLS // N_DEV

ADAM_LR = 0.001
ADAM_B1 = 0.9
ADAM_B2 = 0.999
ADAM_EPS = 1e-08
ADAM_WD = 0.01
ADAM_STEP = 10

V7X_VMEM_BYTES = 64 * 1024 * 1024
VMEM_LIMIT = V7X_VMEM_BYTES - 2 * 1024 * 1024

MESH = pl.DeviceIdType.MESH
HBM_SPEC = pl.BlockSpec(memory_space=pltpu.HBM)


def _params(*sem, barrier_id=None):
    return pltpu.CompilerParams(dimension_semantics=sem or None, vmem_limit_bytes=VMEM_LIMIT, collective_id=barrier_id)


def _mm(a, b):
    return jnp.dot(a, b, preferred_element_type=F32)


def _mm_nt(a, b):
    return lax.dot_general(a, b, (((1,), (1,)), ((), ())), preferred_element_type=F32)


def _mm_tn(a, b):
    return lax.dot_general(a, b, (((0,), (0,)), ((), ())), preferred_element_type=F32)


def _inv_rms(x):
    return lax.rsqrt(jnp.mean(x * x, axis=-1, keepdims=True) + NORM_EPS)


def _rms_bwd(xhat, r, gain, dy):
    gy = dy * gain
    return r * (gy - xhat * jnp.mean(gy * xhat, axis=-1, keepdims=True)), dy * xhat


def _colsum(a):
    return jnp.sum(a, axis=0, keepdims=True)


def _full(shape):
    zeros = (0,) * len(shape)
    return pl.BlockSpec(shape, lambda *_: zeros)


def _resident(shape):
    zeros = (0,) * len(shape)
    return pl.BlockSpec(shape, lambda *_: zeros, pipeline_mode=pl.Buffered(1))


def _rope_tables(t):
    pos = np.arange(t, dtype=np.float32)
    inv_freq = (ROPE_THETA ** (-np.arange(0, 2 * ROT_HALF, 2, dtype=np.float64) / (2 * ROT_HALF))).astype(np.float32)
    ang = (pos[:, None] * inv_freq[None, :]).astype(np.float64)
    cos, sin = np.cos(ang).astype(np.float32), np.sin(ang).astype(np.float32)
    zeros8 = np.zeros((t, ROT_HALF), np.float32)
    rest = np.zeros((t, HEAD_DIM - 2 * ROT_HALF), np.float32)
    c_head = np.concatenate([cos, cos, rest + 1.0], axis=1)
    s1_head = np.concatenate([zeros8, sin, rest], axis=1)
    s2_head = np.concatenate([-sin, zeros8, rest], axis=1)
    two = lambda a: jnp.asarray(np.concatenate([a, a], axis=1))
    return two(c_head), two(s1_head), two(s2_head)


def _rope(v, c, s1, s2):
    return v * c + pltpu.roll(v, ROT_HALF, 1) * s1 + pltpu.roll(v, 128 - ROT_HALF, 1) * s2


def _rope_transpose(dv, c, s1, s2):
    return dv * c + pltpu.roll(dv * s1, 128 - ROT_HALF, 1) + pltpu.roll(dv * s2, ROT_HALF, 1)


def _shift_rows_down(u, prev, k):
    row = lax.broadcasted_iota(jnp.int32, u.shape, 0)
    out = pltpu.roll(u, k, 0)
    for r in range(k):
        out = jnp.where(row == r, prev[8 - k + r:8 - k + r + 1, :], out)
    return out


def _shift_rows_up(u, nxt, k):
    n = u.shape[0]
    row = lax.broadcasted_iota(jnp.int32, u.shape, 0)
    out = pltpu.roll(u, n - k, 0)
    for r in range(k):
        out = jnp.where(row == n - k + r, nxt[r:r + 1, :], out)
    return out


def _conv3(u, u1, u2, w):
    return (w[0:1, :] * u2 + w[1:2, :] * u1) + w[2:3, :] * u


def _mesh_pos():
    return lax.axis_index("x"), lax.axis_index("y"), lax.axis_index("c")


def _slot(ref, pos):
    dev = 4 * pos[0] + 2 * pos[1] + pos[2]
    if len(ref.shape) == 2:
        width = ref.shape[1] // N_DEV
        return ref.at[:, pl.ds(pl.multiple_of(dev * width, width), width)]
    return ref.at[dev]


def _gathered_shape(shard, by_cols):
    if by_cols:
        return jax.ShapeDtypeStruct((shard.shape[0], N_DEV * shard.shape[1]), shard.dtype)
    return jax.ShapeDtypeStruct((N_DEV,) + shard.shape, shard.dtype)


def _enter_with(peers):
    barrier = pltpu.get_barrier_semaphore()
    for peer in peers:
        pl.semaphore_signal(barrier, inc=1, device_id=peer, device_id_type=MESH)
    pl.semaphore_wait(barrier, len(peers))


def _sibling_and_chips(x, y, c):
    return [(x, y, 1 - c), (1 - x, y, c), (x, 1 - y, c), (1 - x, 1 - y, c)]


def _push(src, dst, sems, k, to):
    send_sems, recv_sems = sems
    return pltpu.make_async_remote_copy(src_ref=src, dst_ref=dst, send_sem=send_sems.at[k], recv_sem=recv_sems.at[k],
                                        device_id=to, device_id_type=MESH)


def _gather_steps(shards, outs, send_sems, recv_sems, local_sems):
    n = len(shards)
    x, y, c = _mesh_pos()
    me, sibling = (x, y, c), (x, y, 1 - c)
    chips = [(1 - x, y), (x, 1 - y), (1 - x, 1 - y)]

    def copy(i, k, block, to, src=None):
        dst = _slot(outs[i], block)
        return _push(dst if src is None else src, dst, (send_sems, recv_sems), 7 * i + k, to)

    mine = [pltpu.make_async_copy(shards[i], _slot(outs[i], me), local_sems.at[i]) for i in range(n)]
    first = []
    for i in range(n):
        first.append(copy(i, 0, me, sibling, src=shards[i]))
        first += [copy(i, 1 + j, me, (*chip, c), src=shards[i]) for j, chip in enumerate(chips)]

    def start():
        for cp in mine + first:
            cp.start()

    def finish():
        passed = []
        for j, chip in enumerate(chips):
            for i in range(n):
                copy(i, 1 + j, (*chip, c), me).wait_recv()
                cp = copy(i, 4 + j, (*chip, c), sibling)
                cp.start()
                passed.append(cp)
        for i in range(n):
            copy(i, 0, sibling, me).wait_recv()
            for j, chip in enumerate(chips):
                copy(i, 4 + j, (*chip, 1 - c), me).wait_recv()
        for cp in first + passed:
            cp.wait_send()
        for cp in mine:
            cp.wait()

    return start, finish


def _gather_near(first, last, shards, outs, sems, local_sems):
    x, y, c = _mesh_pos()
    me, peers = (x, y, c), [(x, y, 1 - c), (1 - x, y, c), (x, 1 - y, c)]
    n = len(shards)
    local = [pltpu.make_async_copy(shards[i], _slot(outs[i], me), local_sems.at[i]) for i in range(n)]
    sends = [_push(shards[i], _slot(outs[i], me), sems, 3 * i + k, peers[k]) for i in range(n) for k in range(3)]
    arrivals = [_push(shards[i], _slot(outs[i], peers[k]), sems, 3 * i + k, peers[k]) for i in range(n) for k in range(3)]

    def start():
        for cp in local + sends:
            cp.start()

    if first is not None:
        pl.when(first)(start)

    @pl.when(last)
    def _():
        for cp in sends:
            cp.wait_send()
        for cp in arrivals:
            cp.wait_recv()
        for cp in local:
            cp.wait()

    return start


def _relay_route(x, y, c):
    south = c == 0
    via = (jnp.where(south, 1 - x, x), jnp.where(south, y, 1 - y))
    to = (jnp.where(south, x, 1 - x), jnp.where(south, 1 - y, y))
    return via, to


def _gather_far(first, middle, last, shards, ins, outs, sems):
    x, y, c = _mesh_pos()
    sibling = (x, y, 1 - c)
    chips = [(1 - x, y), (x, 1 - y), (1 - x, 1 - y)]
    via, to = _relay_route(x, y, c)
    n = len(shards)
    diag_send = [_push(_slot(ins[i], (*via, c)), _slot(outs[i], (*via, c)), sems, 4 * i, (*to, c)) for i in range(n)]
    diag_arrival = [_push(shards[i], _slot(outs[i], (*chips[2], c)), sems, 4 * i, (*to, c)) for i in range(n)]
    passed = [[_push(_slot(ins[i], (*chips[j], c)), _slot(outs[i], (*chips[j], c)), sems, 4 * i + 1 + j, sibling)
               for i in range(n)] for j in range(3)]
    from_sibling = [_push(shards[i], _slot(outs[i], (*chips[j], 1 - c)), sems, 4 * i + 1 + j, sibling)
                    for i in range(n) for j in range(3)]

    @pl.when(first)
    def _():
        for cp in diag_send + passed[0] + passed[1]:
            cp.start()

    @pl.when(middle)
    def _():
        for cp in diag_arrival:
            cp.wait_recv()
        for cp in passed[2]:
            cp.start()

    @pl.when(last)
    def _():
        for cp in from_sibling:
            cp.wait_recv()
        for cp in diag_send + passed[0] + passed[1] + passed[2]:
            cp.wait_send()


def _in_proj_fwd(x, g1, w_in, conv_w, g_conv, rope, tm, shards, by_cols):
    t = x.shape[0]
    rc, rs1, rs2 = rope
    n = len(shards)
    n_tiles = t // tm

    def body(*refs):
        x_ref, g1_ref, w_ref, cw_ref, gc_ref, c_ref, s1_ref, s2_ref = refs[:8]
        shard_refs = refs[8:8 + n]
        qkv_ref, gates_ref, mconv_ref, w_full_ref, cw_full_ref = refs[8 + n:13 + n]
        gathered = refs[13 + n:13 + 2 * n]
        carry_ref, w_land, cw_land, hn_ref = refs[13 + 2 * n:17 + 2 * n]
        now_sems = refs[17 + 2 * n:20 + 2 * n]
        step = pl.program_id(0)
        start_later_weights = _gather_near(None, step == 2 * n_tiles - 1, shard_refs, gathered,
                                           refs[20 + 2 * n:22 + 2 * n], refs[22 + 2 * n]) if n else None
        start_w_in, finish_w_in = _gather_steps([w_ref, cw_ref], [w_land, cw_land], *now_sems)

        @pl.when(step == 0)
        def _():
            carry_ref[...] = jnp.zeros_like(carry_ref)
            _enter_with(_sibling_and_chips(*_mesh_pos()))
            start_w_in()
            if start_later_weights is not None:
                start_later_weights()

        @pl.when(step < n_tiles)
        def _():
            xv = x_ref[...]
            hn_ref[step] = ((xv * _inv_rms(xv)) * g1_ref[...]).astype(BF16)

        @pl.when(step == n_tiles)
        def _():
            finish_w_in()
            conv_shard = CONV_W // N_DEV
            for d in range(N_DEV):
                w_full_ref[:, IN_SHARD * d:IN_SHARD * (d + 1)] = w_land[d]
                cw_full_ref[:, conv_shard * d:conv_shard * (d + 1)] = cw_land[d]

        @pl.when(step >= n_tiles)
        def _():
            proj = _mm(hn_ref[step - n_tiles], w_full_ref[...])
            c, s1, s2 = c_ref[...], s1_ref[...], s2_ref[...]
            for ci in range((ATTN_W + KV_W) // 128):
                sl = slice(128 * ci, 128 * (ci + 1))
                qkv_ref[:, sl] = _rope(proj[:, sl], c, s1, s2).astype(BF16)
            qkv_ref[:, ATTN_W + KV_W:QKV_W] = proj[:, ATTN_W + KV_W:QKV_W].astype(BF16)
            gates = proj[:, QKV_W:]
            gates_ref[...] = gates
            gb, gcc, xin = gates[:, :CONV_W], gates[:, CONV_W:2 * CONV_W], gates[:, 2 * CONV_W:]
            u = gcc * xin
            prev = carry_ref[...]
            conv = gb * _conv3(u, _shift_rows_down(u, prev, 1), _shift_rows_down(u, prev, 2), cw_full_ref[...])
            carry_ref[...] = u[tm - 8:tm, :]
            mconv_ref[...] = ((conv * _inv_rms(conv)) * gc_ref[...]).astype(BF16)

    first_pass = pl.BlockSpec((tm, D_MODEL), lambda i: (jnp.minimum(i, n_tiles - 1), 0))
    tile = lambda w_: pl.BlockSpec((tm, w_), lambda i: (jnp.maximum(i - n_tiles, 0), 0))
    sems = lambda k: pltpu.SemaphoreType.DMA((k,))
    res = pl.pallas_call(
        body, name="in_proj_fwd", grid=(2 * n_tiles,),
        in_specs=[first_pass, _full((1, D_MODEL)), HBM_SPEC, HBM_SPEC, _full((1, CONV_W)), tile(128), tile(128),
                  tile(128)] + [HBM_SPEC] * n,
        out_specs=[tile(QKV_W), tile(GATES_W), tile(CONV_W), _full((D_MODEL, IN_COLS)), _full((3, CONV_W))]
        + [HBM_SPEC] * n,
        out_shape=[jax.ShapeDtypeStruct((t, QKV_W), BF16), jax.ShapeDtypeStruct((t, GATES_W), F32),
                   jax.ShapeDtypeStruct((t, CONV_W), BF16), jax.ShapeDtypeStruct((D_MODEL, IN_COLS), BF16),
                   jax.ShapeDtypeStruct((3, CONV_W), F32)]
        + [_gathered_shape(s, cols) for s, cols in zip(shards, by_cols)],
        scratch_shapes=[pltpu.VMEM((8, CONV_W), F32), pltpu.VMEM((N_DEV,) + w_in.shape, BF16),
                        pltpu.VMEM((N_DEV,) + conv_w.shape, F32), pltpu.VMEM((n_tiles, tm, D_MODEL), BF16),
                        sems(14), sems(14), sems(2)]
        + ([sems(3 * n), sems(3 * n), sems(n)] if n else []),
        compiler_params=_params("arbitrary", barrier_id=0),
    )(x, g1, w_in, conv_w, g_conv, rc, rs1, rs2, *shards)
    return res[0], res[1], res[2], res[3], res[4], list(res[5:])


GROUP_COLS = GROUP * BLOCK
ATTN_STEP_BLOCKS = 4


def _attn_masks(has_prev):
    key = lax.broadcasted_iota(jnp.int32, (2 * BLOCK, GROUP_COLS), 0)
    query = lax.broadcasted_iota(jnp.int32, (2 * BLOCK, GROUP_COLS), 1) & (BLOCK - 1)
    band = (key > query) & (key <= query + BLOCK)
    return [band & ((key >= BLOCK) | has_prev)] + [band] * (ATTN_STEP_BLOCKS - 1)


def _heads_side_by_side(at, g, b):
    heads = [at[HEAD_DIM * (GROUP * g + hh):HEAD_DIM * (GROUP * g + hh + 1), BLOCK * b:BLOCK * (b + 1)] for hh in range(GROUP)]
    return jnp.concatenate(heads, axis=1)


def _to_token_rows(parts):
    rows = [jnp.concatenate([parts[b][g][:, BLOCK * hh:BLOCK * (hh + 1)] for b in range(ATTN_STEP_BLOCKS)], axis=1)
            for g in range(N_KV) for hh in range(GROUP)]
    return jnp.concatenate(rows, axis=0).T


def _group_sinks(sink_ref, g):
    head = lax.broadcasted_iota(jnp.int32, (1, GROUP_COLS), 1) // BLOCK
    out = jnp.full((1, GROUP_COLS), sink_ref[0, GROUP * g], F32)
    for hh in range(1, GROUP):
        out = jnp.where(head == hh, sink_ref[0, GROUP * g + hh], out)
    return out


def _attn_probs(qt, kk, sink, valid):
    s = jnp.where(valid, _mm(kk, qt), NEG_INF)
    m = jnp.maximum(jnp.max(s, axis=0, keepdims=True), sink)
    p = jnp.exp(s - m)
    psink = jnp.exp(sink - m)
    inv_l = 1.0 / (jnp.sum(p, axis=0, keepdims=True) + psink)
    return p * inv_l, psink * inv_l


ATTN_STEP = ATTN_STEP_BLOCKS * BLOCK
ATTN_KEYS = ATTN_STEP + BLOCK


def _qkv_specs(order):
    prev = lambda i: jnp.maximum(ATTN_STEP_BLOCKS * order(i) - 1, 0)
    kcol, vcol = ATTN_W // KV_W, ATTN_W // KV_W + 1
    return [pl.BlockSpec((ATTN_STEP, ATTN_W), lambda i: (order(i), 0)),
            pl.BlockSpec((BLOCK, KV_W), lambda i: (prev(i), kcol)), pl.BlockSpec((ATTN_STEP, KV_W), lambda i: (order(i), kcol)),
            pl.BlockSpec((BLOCK, KV_W), lambda i: (prev(i), vcol)), pl.BlockSpec((ATTN_STEP, KV_W), lambda i: (order(i), vcol))]


def _attn_fwd(qkv, sinks, g_attn, shards, gathered):
    t = qkv.shape[0]
    n = len(shards)

    def body(*refs):
        sink_ref, q_ref, kp_ref, kc_ref, vp_ref, vc_ref, ga_ref = refs[:7]
        attn_ref, mattn_ref = refs[7 + 2 * n:9 + 2 * n]
        step = pl.program_id(0)
        if n:
            @pl.when(step == 0)
            def _():
                x, y, c = _mesh_pos()
                _enter_with([(x, y, 1 - c), (*_relay_route(x, y, c)[1], c)])

            n_steps = t // ATTN_STEP
            _gather_far(step == 0, step == n_steps // 2, step == n_steps - 1, refs[7:7 + n], refs[7 + n:7 + 2 * n],
                        refs[9 + 2 * n:9 + 3 * n], refs[9 + 3 * n:11 + 3 * n])
        qt = (q_ref[...] * ATTN_SCALE).T
        keys = jnp.concatenate([kp_ref[...], kc_ref[...]], axis=0)
        vals = jnp.concatenate([vp_ref[...], vc_ref[...]], axis=0)
        sink = [_group_sinks(sink_ref, g) for g in range(N_KV)]
        masks = _attn_masks(step > 0)
        parts = []
        for b in range(ATTN_STEP_BLOCKS):
            window = slice(BLOCK * b, BLOCK * (b + 2))
            valid = masks[b]
            parts.append([])
            for g in range(N_KV):
                gs = slice(HEAD_DIM * g, HEAD_DIM * (g + 1))
                probs, _ = _attn_probs(_heads_side_by_side(qt, g, b), keys[window, gs], sink[g], valid)
                parts[b].append(_mm_tn(vals[window, gs], probs.astype(BF16)))
        attn = _to_token_rows(parts)
        attn_ref[...] = attn
        mattn_ref[...] = ((attn * _inv_rms(attn)) * ga_ref[...]).astype(BF16)

    blk = pl.BlockSpec((ATTN_STEP, ATTN_W), lambda j: (j, 0))
    res = pl.pallas_call(
        body, name="attn_fwd", grid=(t // ATTN_STEP,),
        in_specs=[pl.BlockSpec(memory_space=pltpu.SMEM)] + _qkv_specs(lambda j: j) + [_full((1, ATTN_W))]
        + [HBM_SPEC] * (2 * n),
        out_specs=[blk, blk] + [HBM_SPEC] * n,
        out_shape=[jax.ShapeDtypeStruct((t, ATTN_W), F32), jax.ShapeDtypeStruct((t, ATTN_W), BF16)]
        + [jax.ShapeDtypeStruct(g.shape, g.dtype) for g in gathered],
        input_output_aliases={7 + n + i: 2 + i for i in range(n)},
        scratch_shapes=[pltpu.SemaphoreType.DMA((4 * n,)), pltpu.SemaphoreType.DMA((4 * n,))] if n else [],
        compiler_params=_params("arbitrary", barrier_id=1 if n else None),
    )(sinks, qkv, qkv, qkv, qkv, qkv, g_attn, *shards, *gathered)
    return res[0], res[1], list(res[2:])


SMALL_ROWS = 8
ROW_LOSS, ROW_G2, ROW_G3, ROW_G4 = 0, 1, 2, 3


def _mid(mattn, mconv, x, target, g2, g3, g4, w_out, w_up, w_down, tm):
    t = x.shape[0]

    def body(ma_ref, mc_ref, x_ref, t_ref, g2_ref, g3_ref, g4_ref, wo_ref, wu_ref, wd_ref,
             act_ref, dup_ref, hn2t_ref, dmo_ref, dmix_ref, dh_ref, dmixed_ref, small_ref, up_ref):
        @pl.when(pl.program_id(0) == 0)
        def _():
            small_ref[...] = jnp.zeros_like(small_ref)

        g2, g3, g4 = g2_ref[...], g3_ref[...], g4_ref[...]
        mix_out = _mm(ma_ref[...], wo_ref[0:ATTN_W, :]) + _mm(mc_ref[...], wo_ref[ATTN_W:, :])
        r2 = _inv_rms(mix_out)
        mo_hat = mix_out * r2
        h = x_ref[...] + mo_hat * g2
        r3 = _inv_rms(h)
        h_hat = h * r3
        hn2 = (h_hat * g3).astype(BF16)
        hn2t_ref[...] = hn2.T
        for j in range(MID_CHUNKS):
            cols_j = slice(MID_CHUNK * j, MID_CHUNK * (j + 1))
            up = jnp.maximum(_mm(hn2, wu_ref[:, cols_j]), 0.0)
            up_ref[:, cols_j] = up.astype(BF16)
            act_ref[:, cols_j] = (up * up).astype(BF16)
        mlp = _mm(act_ref[...], wd_ref[...])
        r4 = _inv_rms(mlp)
        ml_hat = mlp * r4
        err = (h + ml_hat * g4) - t_ref[...]
        d_out = err * (1.0 / D_MODEL)
        d_mlp, dg4 = _rms_bwd(ml_hat, r4, g4, d_out)
        dmo = d_mlp.astype(BF16)
        dmo_ref[...] = dmo
        for j in range(MID_CHUNKS):
            cols_j = slice(MID_CHUNK * j, MID_CHUNK * (j + 1))
            dact = _mm_nt(dmo, wd_ref[cols_j, :])
            dup_ref[:, cols_j] = (dact * (2.0 * up_ref[:, cols_j].astype(F32))).astype(BF16)
        dhn2 = _mm_nt(dup_ref[...], wu_ref[...])
        dh_norm, dg3 = _rms_bwd(h_hat, r3, g3, dhn2)
        dh = d_out + dh_norm
        dh_ref[...] = dh
        d_mix, dg2 = _rms_bwd(mo_hat, r2, g2, dh)
        dmix = d_mix.astype(BF16)
        dmix_ref[...] = dmix
        dmixed_ref[...] = _mm_nt(dmix, wo_ref[...])
        small_ref[ROW_LOSS:ROW_LOSS + 1, :] += _colsum(err * err)
        small_ref[ROW_G2:ROW_G2 + 1, :] += _colsum(dg2)
        small_ref[ROW_G3:ROW_G3 + 1, :] += _colsum(dg3)
        small_ref[ROW_G4:ROW_G4 + 1, :] += _colsum(dg4)

    tile = lambda n: pl.BlockSpec((tm, n), lambda i: (i, 0))
    cols = lambda n: pl.BlockSpec((n, tm), lambda i: (0, i))
    gain = _full((1, D_MODEL))
    return pl.pallas_call(
        body, name="mid_fwd_bwd", grid=(t // tm,),
        in_specs=[tile(ATTN_W), tile(CONV_W), tile(D_MODEL), tile(D_MODEL), gain, gain, gain,
                  _resident((D_MODEL, D_MODEL)), _resident((D_MODEL, D_FF)), _resident((D_FF, D_MODEL))],
        out_specs=[tile(D_FF), tile(D_FF), cols(D_MODEL), tile(D_MODEL), tile(D_MODEL), tile(D_MODEL), tile(D_MODEL),
                   _full((SMALL_ROWS, D_MODEL))],
        out_shape=[jax.ShapeDtypeStruct((t, D_FF), BF16), jax.ShapeDtypeStruct((t, D_FF), BF16),
                   jax.ShapeDtypeStruct((D_MODEL, t), BF16), jax.ShapeDtypeStruct((t, D_MODEL), BF16),
                   jax.ShapeDtypeStruct((t, D_MODEL), BF16), jax.ShapeDtypeStruct((t, D_MODEL), F32),
                   jax.ShapeDtypeStruct((t, D_MODEL), F32), jax.ShapeDtypeStruct((SMALL_ROWS, D_MODEL), F32)],
        scratch_shapes=[pltpu.VMEM((tm, D_FF), BF16)],
        compiler_params=_params("arbitrary"),
    )(mattn, mconv, x, target, g2, g3, g4, w_out, w_up, w_down)


CHIP_FLIPS = ((1, 1), (1, 0), (0, 1))


def _block_order(dev):
    chip_masks = [4 * fx + 2 * fy for fx, fy in CHIP_FLIPS]
    masks = [m + 1 for m in chip_masks] + [1] + chip_masks + [0]
    return jnp.bitwise_xor(dev, jnp.asarray(masks, jnp.int32)).astype(jnp.int32)


def _other_chips(x, y, c):
    return [(1 - x if fx else x, 1 - y if fy else y, c) for fx, fy in CHIP_FLIPS]


def _dw_pair_sums(operands, order, which, name, barrier_id, ride=None):
    t = operands[-1].shape[0]
    n_far = len(CHIP_FLIPS)
    n_in = len(operands)
    n_ride = 0 if ride is None else 1
    out_chunk = D_MODEL // N_DEV
    if which == "up":
        rows, cols = D_MODEL, FF_CHUNK
        in_specs = [_resident((D_MODEL, t)), pl.BlockSpec((t, FF_CHUNK), lambda s, order_ref: (0, order_ref[s]))]
    elif which == "down":
        rows, cols = FF_CHUNK, D_MODEL
        in_specs = [pl.BlockSpec((t, FF_CHUNK), lambda s, order_ref: (0, order_ref[s])), _resident((t, D_MODEL))]
    else:
        rows, cols = out_chunk, D_MODEL
        half = pl.BlockSpec((t, out_chunk), lambda s, order_ref: (0, order_ref[s] % (N_DEV // 2)))
        in_specs = [half, half, _resident((t, D_MODEL))]

    def body(order_ref, *refs):
        own_ref, from_sib_ref, pair_ref = refs[n_in + n_ride:n_in + n_ride + 3]
        send_buf, land_buf, send_sems, recv_sems = refs[n_in + 2 * n_ride + 3:n_in + 2 * n_ride + 7]
        s_now = pl.program_id(0)
        x, y, c = _mesh_pos()
        sibling = (x, y, 1 - c)
        sems = (send_sems, recv_sems)

        @pl.when(s_now == 0)
        def _():
            _enter_with([sibling] + (_other_chips(x, y, c) if n_ride else []))

        if n_ride:
            _chip_exchange_beside(s_now == 0, s_now == N_DEV - 1, [refs[n_in]], [refs[n_in + 3 + n_ride]],
                                  refs[n_in + 2 * n_ride + 7:], enter=False)

        def hand_over(k):
            dst = land_buf.at[k] if k < n_far else from_sib_ref
            return _push(send_buf.at[k], dst, sems, k, sibling)

        if which == "out":
            ma_ref, mc_ref, b_ref = refs[:n_in]
            block = lax.cond(order_ref[s_now] < N_DEV // 2, lambda: _mm_tn(ma_ref[...], b_ref[...]),
                             lambda: _mm_tn(mc_ref[...], b_ref[...]))
        elif which == "down":
            block = _mm_tn(refs[0][...], refs[1][...])
        else:
            block = _mm(refs[0][...], refs[1][...])
        for k in range(n_far + 1):
            @pl.when(s_now == k)
            def _():
                send_buf[k] = block.astype(BF16)
                hand_over(k).start()

        for k in range(n_far):
            @pl.when(s_now == n_far + 1 + k)
            def _():
                hand_over(k).wait_recv()
                pair_ref[...] = (block + land_buf[k].astype(F32)).astype(BF16)

        @pl.when(s_now == N_DEV - 1)
        def _():
            own_ref[...] = block
            for k in range(n_far + 1):
                hand_over(k).wait_send()
            hand_over(n_far).wait_recv()

    rides = [] if ride is None else [ride]
    sems = lambda k: pltpu.SemaphoreType.DMA((k,))
    return pl.pallas_call(
        body, name=name,
        grid_spec=pltpu.PrefetchScalarGridSpec(
            num_scalar_prefetch=1, grid=(N_DEV,), in_specs=in_specs + [HBM_SPEC] * n_ride,
            out_specs=[pl.BlockSpec((rows, cols), lambda s, order_ref: (0, 0)), HBM_SPEC,
                       pl.BlockSpec((None, rows, cols), lambda s, order_ref: (jnp.clip(s - n_far - 1, 0, n_far - 1), 0, 0))]
            + [HBM_SPEC] * n_ride,
            scratch_shapes=[pltpu.VMEM((n_far + 1, rows, cols), BF16), pltpu.VMEM((n_far, rows, cols), BF16),
                            sems(n_far + 1), sems(n_far + 1)] + [sems(n_far), sems(n_far)] * n_ride),
        out_shape=[jax.ShapeDtypeStruct((rows, cols), F32), jax.ShapeDtypeStruct((rows, cols), BF16),
                   jax.ShapeDtypeStruct((n_far, rows, cols), BF16)]
        + [jax.ShapeDtypeStruct(r.shape, r.dtype) for r in rides],
        compiler_params=_params("arbitrary", barrier_id=barrier_id),
    )(order, *operands, *rides)


def _chip_exchange_beside(first, last, sums, outs, sems, enter=True):
    chips = _other_chips(*_mesh_pos())
    copies = [_push(sums[i].at[k], outs[i].at[k], sems, len(chips) * i + k, chip)
              for i in range(len(sums)) for k, chip in enumerate(chips)]

    @pl.when(first)
    def _():
        if enter:
            _enter_with(chips)
        for cp in copies:
            cp.start()

    @pl.when(last)
    def _():
        for cp in copies:
            cp.wait()


ROW_GCONV, ROW_CW0 = 1, 2


def _conv_bwd(dmixed, gates, g_conv, conv_w, tm):
    t = gates.shape[0]
    n = t // tm
    rev = lambda i: n - 1 - i

    def body(dm_ref, gates_ref, gprev_ref, gc_ref, cw_ref, dgates_ref, small_ref, carry_ref):
        i = pl.program_id(0)

        @pl.when(i == 0)
        def _():
            small_ref[...] = jnp.zeros_like(small_ref)
            carry_ref[...] = jnp.zeros_like(carry_ref)

        gates = gates_ref[...]
        gb, gcc, xin = gates[:, :CONV_W], gates[:, CONV_W:2 * CONV_W], gates[:, 2 * CONV_W:]
        u = gcc * xin
        gp = gprev_ref[...]
        uprev = jnp.where(rev(i) == 0, 0.0, gp[:, CONV_W:2 * CONV_W] * gp[:, 2 * CONV_W:])
        u1, u2 = _shift_rows_down(u, uprev, 1), _shift_rows_down(u, uprev, 2)
        w = cw_ref[...]
        c = _conv3(u, u1, u2, w)
        conv = gb * c
        rcv = _inv_rms(conv)
        c_hat = conv * rcv
        dconv, dgc = _rms_bwd(c_hat, rcv, gc_ref[...], dm_ref[...])
        dc = dconv * gb
        nxt = carry_ref[...]
        du = (w[2:3, :] * dc + w[1:2, :] * _shift_rows_up(dc, nxt, 1)) + w[0:1, :] * _shift_rows_up(dc, nxt, 2)
        carry_ref[...] = dc[0:8, :]
        dgates_ref[:, :CONV_W] = (dconv * c).astype(BF16)
        dgates_ref[:, CONV_W:2 * CONV_W] = (du * xin).astype(BF16)
        dgates_ref[:, 2 * CONV_W:] = (du * gcc).astype(BF16)
        small_ref[ROW_GCONV:ROW_GCONV + 1, :] += _colsum(dgc)
        small_ref[ROW_CW0:ROW_CW0 + 1, :] += _colsum(dc * u2)
        small_ref[ROW_CW0 + 1:ROW_CW0 + 2, :] += _colsum(dc * u1)
        small_ref[ROW_CW0 + 2:ROW_CW0 + 3, :] += _colsum(dc * u)

    tile = lambda w_: pl.BlockSpec((tm, w_), lambda i: (rev(i), 0))
    prev8 = pl.BlockSpec((8, GATES_W), lambda i: (jnp.maximum(rev(i) * (tm // 8) - 1, 0), 0))
    conv_half = pl.BlockSpec((tm, CONV_W), lambda i: (rev(i), ATTN_W // CONV_W))
    return pl.pallas_call(
        body, name="conv_bwd", grid=(n,),
        in_specs=[conv_half, tile(GATES_W), prev8, _full((1, CONV_W)), _full((3, CONV_W))],
        out_specs=[tile(GATES_W), _full((SMALL_ROWS, CONV_W))],
        out_shape=[jax.ShapeDtypeStruct((t, GATES_W), BF16), jax.ShapeDtypeStruct((SMALL_ROWS, CONV_W), F32)],
        scratch_shapes=[pltpu.VMEM((8, CONV_W), F32)],
        compiler_params=_params("arbitrary"),
    )(dmixed, gates, gates, g_conv, conv_w)


def _attn_bwd(qkv, dmixed, attn, g_attn, sinks, rope, sums):
    t = qkv.shape[0]
    n_steps = t // ATTN_STEP
    rev = lambda i: n_steps - 1 - i
    rc, rs1, rs2 = rope

    def body(sink_ref, q_ref, kp_ref, kc_ref, vp_ref, vc_ref, dm_ref, attn_ref, ga_ref, c_ref, s1_ref, s2_ref, sums_ref,
             dqkv_ref, dsink_ref, dgain_ref, arrived_ref, ck_ref, cv_ref, kacc_ref, vacc_ref, send_sems, recv_sems):
        i = pl.program_id(0)
        _chip_exchange_beside(i == 0, i == n_steps - 1, [sums_ref], [arrived_ref], (send_sems, recv_sems))

        @pl.when(i == 0)
        def _():
            dsink_ref[...] = jnp.zeros_like(dsink_ref)
            dgain_ref[...] = jnp.zeros_like(dgain_ref)
            ck_ref[...] = jnp.zeros_like(ck_ref)
            cv_ref[...] = jnp.zeros_like(cv_ref)

        kacc_ref[...] = jnp.zeros_like(kacc_ref)
        vacc_ref[...] = jnp.zeros_like(vacc_ref)
        a = attn_ref[...]
        ra = _inv_rms(a)
        dattn, dgain = _rms_bwd(a * ra, ra, ga_ref[...], dm_ref[...])
        dgain_ref[0:1, :] += _colsum(dgain)
        qt = (q_ref[...] * ATTN_SCALE).T
        dot = dattn.astype(BF16).T
        keys = jnp.concatenate([kp_ref[...], kc_ref[...]], axis=0)
        vals = jnp.concatenate([vp_ref[...], vc_ref[...]], axis=0)
        sink = [_group_sinks(sink_ref, g) for g in range(N_KV)]
        c, s1, s2 = c_ref[...], s1_ref[...], s2_ref[...]
        lane = lax.broadcasted_iota(jnp.int32, (1, 128), 1)
        dsink = jnp.zeros((1, 128), F32)
        masks = _attn_masks(rev(i) > 0)
        dq_parts = []
        for b in range(ATTN_STEP_BLOCKS):
            window = slice(BLOCK * b, BLOCK * (b + 2))
            valid = masks[b]
            dq_parts.append([])
            dk_parts, dv_parts = [], []
            for g in range(N_KV):
                gs = slice(HEAD_DIM * g, HEAD_DIM * (g + 1))
                kk, vv = keys[window, gs], vals[window, gs]
                qtg, dotg = _heads_side_by_side(qt, g, b), _heads_side_by_side(dot, g, b)
                probs, psink = _attn_probs(qtg, kk, sink[g], valid)
                dp = _mm(vv, dotg)
                delta = jnp.sum(probs * dp, axis=0, keepdims=True)
                ds = (probs * (dp - delta)).astype(BF16)
                sink_terms = psink * delta
                for hh in range(GROUP):
                    head_sum = jnp.sum(sink_terms[:, BLOCK * hh:BLOCK * (hh + 1)])
                    dsink = dsink + jnp.where(lane == GROUP * g + hh, -head_sum, 0.0)
                dq_parts[b].append(_mm_tn(kk * ATTN_SCALE, ds))
                dk_parts.append(_mm_nt(ds, qtg))
                dv_parts.append(_mm_nt(probs.astype(BF16), dotg))
            kacc_ref[window, :] += jnp.concatenate(dk_parts, axis=1)
            vacc_ref[window, :] += jnp.concatenate(dv_parts, axis=1)
        dq = _to_token_rows(dq_parts)
        for ci in range(ATTN_W // 128):
            sl = slice(128 * ci, 128 * (ci + 1))
            dqkv_ref[:, sl] = _rope_transpose(dq[:, sl], c, s1, s2).astype(BF16)
        kacc_ref[ATTN_STEP:, :] += ck_ref[...]
        vacc_ref[ATTN_STEP:, :] += cv_ref[...]
        ck_ref[...] = kacc_ref[:BLOCK, :]
        cv_ref[...] = vacc_ref[:BLOCK, :]
        dqkv_ref[:, ATTN_W:ATTN_W + KV_W] = _rope_transpose(kacc_ref[BLOCK:, :], c, s1, s2).astype(BF16)
        dqkv_ref[:, ATTN_W + KV_W:] = vacc_ref[BLOCK:, :].astype(BF16)
        dsink_ref[0:1, :] += dsink

    blk = lambda w_: pl.BlockSpec((ATTN_STEP, w_), lambda i: (rev(i), 0))
    return pl.pallas_call(
        body, name="attn_bwd", grid=(n_steps,),
        in_specs=[pl.BlockSpec(memory_space=pltpu.SMEM)] + _qkv_specs(rev)
        + [blk(ATTN_W), blk(ATTN_W), _full((1, ATTN_W)), blk(128), blk(128), blk(128), HBM_SPEC],
        out_specs=[blk(QKV_W), _full((8, 128)), _full((SMALL_ROWS, ATTN_W)), HBM_SPEC],
        out_shape=[jax.ShapeDtypeStruct((t, QKV_W), BF16), jax.ShapeDtypeStruct((8, 128), F32),
                   jax.ShapeDtypeStruct((SMALL_ROWS, ATTN_W), F32), jax.ShapeDtypeStruct(sums.shape, sums.dtype)],
        scratch_shapes=[pltpu.VMEM((BLOCK, KV_W), F32), pltpu.VMEM((BLOCK, KV_W), F32),
                        pltpu.VMEM((ATTN_KEYS, KV_W), F32), pltpu.VMEM((ATTN_KEYS, KV_W), F32),
                        pltpu.SemaphoreType.DMA((len(CHIP_FLIPS),)), pltpu.SemaphoreType.DMA((len(CHIP_FLIPS),))],
        compiler_params=_params("arbitrary", barrier_id=6),
    )(sinks, qkv, qkv, qkv, qkv, qkv, dmixed, attn, g_attn, rc, rs1, rs2, sums)


def _grad_x_tile(dq, dg, x_hat, r, g1, w_ref, dh):
    dhn = _mm_nt(dq, w_ref[:, :QKV_W]) + _mm_nt(dg, w_ref[:, QKV_W:])
    dx, dg1 = _rms_bwd(x_hat, r, g1, dhn)
    return dh + dx, _colsum(dg1)


def _in_proj_bwd(dqkv, dgates, x, dh, g1, w_in, tm, out_sums):
    t = x.shape[0]
    n = t // tm
    n_cover = max(n // 2, 1)
    n_steps = n + n_cover
    n_far = len(CHIP_FLIPS)
    shard = (IN_SHARD, D_MODEL)

    def body(dq_ref, dg_ref, x_ref, dh_ref, g1_ref, w_ref, osums_ref,
             dx_ref, own_ref, sib_ref, far_ref, dg1_ref, oarrived_ref,
             acc_ref, send_buf, land_buf, pair_buf, d2d_send, d2d_recv, ici_send, ici_recv, o_send, o_recv):
        i = pl.program_id(0)
        x_pos, y_pos, c = _mesh_pos()
        my_chip = 2 * x_pos + y_pos
        sibling = (x_pos, y_pos, 1 - c)
        @pl.when(i == 0)
        def _():
            _enter_with(_sibling_and_chips(x_pos, y_pos, c))

        _chip_exchange_beside(i == 0, i == n_steps - 1, [osums_ref], [oarrived_ref], (o_send, o_recv), enter=False)

        def rows(d):
            return slice(IN_SHARD * d, IN_SHARD * (d + 1))

        def hand_over(chip):
            return _push(send_buf.at[chip], land_buf.at[chip], (d2d_send, d2d_recv), chip, sibling)

        def to_chip(chip, rel):
            return pltpu.make_async_remote_copy(
                src_ref=pair_buf.at[chip], dst_ref=far_ref.at[rel - 1], send_sem=ici_send.at[rel - 1],
                recv_sem=ici_recv.at[rel - 1], device_id=(chip // 2, chip % 2, c), device_id_type=MESH)

        @pl.when(i == 0)
        def _():
            acc_ref[...] = jnp.zeros_like(acc_ref)
            dg1_ref[...] = jnp.zeros_like(dg1_ref)

        def normed_x():
            xv = x_ref[...]
            r = _inv_rms(xv)
            return xv * r, r

        @pl.when(i < n)
        def _():
            hn = (normed_x()[0] * g1_ref[...]).astype(BF16)
            acc_ref[:QKV_W, :] += _mm_tn(dq_ref[...], hn)
            acc_ref[QKV_W:, :] += _mm_tn(dg_ref[...], hn)

        @pl.when(i == n - 1)
        def _():
            for d in range(N_DEV):
                @pl.when(d % 2 != c)
                def _():
                    send_buf[d // 2] = acc_ref[rows(d), :].astype(BF16)
                    hand_over(d // 2).start()
            for d in range(N_DEV):
                chip = d // 2

                @pl.when(d % 2 == c)
                def _():
                    hand_over(chip).wait_recv()

                    @pl.when(chip == my_chip)
                    def _():
                        own_ref[...] = acc_ref[rows(d), :]
                        sib_ref[...] = land_buf[chip]

                    @pl.when(chip != my_chip)
                    def _():
                        pair_buf[chip] = (acc_ref[rows(d), :] + land_buf[chip].astype(F32)).astype(BF16)
                        to_chip(chip, chip ^ my_chip).start()
            for chip in range(N_CHIPS):
                hand_over(chip).wait_send()

        @pl.when(i >= n)
        def _():
            x_hat, r = normed_x()
            dx_ref[...], dg1 = _grad_x_tile(dq_ref[...], dg_ref[...], x_hat, r, g1_ref[...], w_ref, dh_ref[...])
            dg1_ref[0:1, :] += dg1

        @pl.when(i == n_steps - 1)
        def _():
            for rel in range(1, n_far + 1):
                to_chip(0, rel).wait()

    both = lambda w_: pl.BlockSpec((tm, w_), lambda i: (i % n, 0))
    second = pl.BlockSpec((tm, D_MODEL), lambda i: (jnp.maximum(i - n, 0), 0))
    whole = lambda dtype: jax.ShapeDtypeStruct(shard, dtype)
    sems = lambda k: pltpu.SemaphoreType.DMA((k,))
    res = pl.pallas_call(
        body, name="in_proj_bwd", grid=(n_steps,),
        in_specs=[both(QKV_W), both(GATES_W), both(D_MODEL), second, _full((1, D_MODEL)), _resident((D_MODEL, IN_COLS)),
                  HBM_SPEC],
        out_specs=[second, _full(shard), _full(shard), HBM_SPEC, _full((SMALL_ROWS, D_MODEL)), HBM_SPEC],
        out_shape=[jax.ShapeDtypeStruct((n_cover * tm, D_MODEL), F32), whole(F32), whole(BF16),
                   jax.ShapeDtypeStruct((n_far,) + shard, BF16), jax.ShapeDtypeStruct((SMALL_ROWS, D_MODEL), F32),
                   jax.ShapeDtypeStruct(out_sums.shape, out_sums.dtype)],
        scratch_shapes=[pltpu.VMEM((IN_COLS, D_MODEL), F32), pltpu.VMEM((N_CHIPS,) + shard, BF16),
                        pltpu.VMEM((N_CHIPS,) + shard, BF16), pltpu.VMEM((N_CHIPS,) + shard, BF16),
                        sems(N_CHIPS), sems(N_CHIPS), sems(n_far), sems(n_far), sems(n_far), sems(n_far)],
        compiler_params=_params("arbitrary", barrier_id=7),
    )(dqkv, dgates, x, dh, g1, w_in, out_sums)
    return res[0], (res[1], res[2], res[3]), res[4], res[5]


def _grad_x_rest(dqkv, dgates, x, dh, g1, w_in, tm, head, dg1_rows):
    t = x.shape[0]
    first = head.shape[0] // tm
    n_rest = t // tm - first
    if n_rest == 0:
        return head, dg1_rows
    assert first <= n_rest

    def body(dq_ref, dg_ref, x_ref, dh_ref, g1_ref, w_ref, head_ref, rows_ref, gx_ref, dg1_ref, stage, sems):
        j = pl.program_id(0)

        def tile_out(step, kind):
            row0 = (step + first) * tm if kind == 0 else step * tm
            slot = 2 * kind + step % 2
            return pltpu.make_async_copy(stage.at[slot], gx_ref.at[pl.ds(pl.multiple_of(row0, tm), tm), :], sems.at[slot])

        @pl.when(j == 0)
        def _():
            dg1_ref[...] = rows_ref[...]

        @pl.when(j >= 2)
        def _():
            tile_out(j - 2, 0).wait()

        @pl.when((j >= 2) & (j - 2 < first))
        def _():
            tile_out(j - 2, 1).wait()

        @pl.when(j < first)
        def _():
            stage[2 + j % 2] = head_ref[...]
            tile_out(j, 1).start()

        xv = x_ref[...]
        r = _inv_rms(xv)
        dx, dg1 = _grad_x_tile(dq_ref[...], dg_ref[...], xv * r, r, g1_ref[...], w_ref, dh_ref[...])
        stage[j % 2] = dx
        dg1_ref[0:1, :] += dg1
        tile_out(j, 0).start()

        @pl.when(j == n_rest - 1)
        def _():
            for back in range(min(2, n_rest)):
                tile_out(j - back, 0).wait()

                @pl.when(j - back < first)
                def _():
                    tile_out(j - back, 1).wait()

    tile = lambda w_: pl.BlockSpec((tm, w_), lambda j: (j + first, 0))
    head_tile = pl.BlockSpec((tm, D_MODEL), lambda j: (jnp.minimum(j, first - 1), 0))
    return pl.pallas_call(
        body, name="grad_x_rest", grid=(n_rest,),
        in_specs=[tile(QKV_W), tile(GATES_W), tile(D_MODEL), tile(D_MODEL), _full((1, D_MODEL)),
                  _resident((D_MODEL, IN_COLS)), head_tile, _full((SMALL_ROWS, D_MODEL))],
        out_specs=[HBM_SPEC, _full((SMALL_ROWS, D_MODEL))],
        out_shape=[jax.ShapeDtypeStruct((t, D_MODEL), F32), jax.ShapeDtypeStruct((SMALL_ROWS, D_MODEL), F32)],
        scratch_shapes=[pltpu.VMEM((4, tm, D_MODEL), F32), pltpu.SemaphoreType.DMA((4,))],
        compiler_params=_params("arbitrary"),
    )(dqkv, dgates, x, dh, g1, w_in, head, dg1_rows)


def _all_gather(shards, name):
    n = len(shards)

    def body(*refs):
        _enter_with(_sibling_and_chips(*_mesh_pos()))
        start, finish = _gather_steps(refs[:n], refs[n:2 * n], *refs[2 * n:])
        start()
        finish()

    return pl.pallas_call(
        body, name=name,
        in_specs=[HBM_SPEC] * n, out_specs=[HBM_SPEC] * n,
        out_shape=[jax.ShapeDtypeStruct((N_DEV,) + s.shape, s.dtype) for s in shards],
        scratch_shapes=[pltpu.SemaphoreType.DMA((7 * n,)), pltpu.SemaphoreType.DMA((7 * n,)),
                        pltpu.SemaphoreType.DMA((n,))],
        compiler_params=_params(barrier_id=8),
    )(*shards)


def _adam_math(w, g, m, v):
    m = ADAM_B1 * m + (1.0 - ADAM_B1) * g
    v = ADAM_B2 * v + (1.0 - ADAM_B2) * (g * g)
    m_hat = m / (1.0 - ADAM_B1 ** ADAM_STEP)
    v_hat = v / (1.0 - ADAM_B2 ** ADAM_STEP)
    delta = -ADAM_LR * (m_hat / (jnp.sqrt(v_hat) + ADAM_EPS) + ADAM_WD * w)
    return delta, m, v


def _adamw_reduced(w, m, v, own, from_sibling, from_chips, tr):
    rows, cols = w.shape

    def body(w_ref, m_ref, v_ref, own_ref, sib_ref, far_ref, g_ref, d_ref, nm_ref, nv_ref):
        g = own_ref[...] + sib_ref[...].astype(F32)
        for k in range(len(CHIP_FLIPS)):
            g = g + far_ref[k].astype(F32)
        g_ref[...] = g
        d_ref[...], nm_ref[...], nv_ref[...] = _adam_math(w_ref[...], g, m_ref[...], v_ref[...])

    tile = pl.BlockSpec((tr, cols), lambda i: (i, 0))
    out = jax.ShapeDtypeStruct((rows, cols), F32)
    return pl.pallas_call(
        body, name="adamw_reduced", grid=(rows // tr,),
        in_specs=[tile] * 5 + [pl.BlockSpec((len(CHIP_FLIPS), tr, cols), lambda i: (0, i, 0))],
        out_specs=[tile] * 4, out_shape=[out] * 4,
        compiler_params=_params("parallel"),
    )(w, m, v, own, from_sibling, from_chips)


SMALL_PARAMS = ("pre_mix_norm", "post_mix_norm", "pre_mlp_norm", "post_mlp_norm", "attn_group_norm", "conv_group_norm",
                "conv_w", "attn_sinks")


SMALL_WIDTHS = (D_MODEL, CONV_W, ATTN_W, 128, D_MODEL)


def _small_tail(gathered, dev, weights, first_moments, second_moments):
    n = len(SMALL_PARAMS)
    conv_shard = CONV_W // N_DEV

    def body(dev_ref, sums_ref, *refs):
        w_refs, m_refs, v_refs = refs[:n], refs[n:2 * n], refs[2 * n:3 * n]
        loss_ref, outs = refs[3 * n], refs[3 * n + 1:]
        total = sums_ref[0]
        for d in range(1, N_DEV):
            total = total + sums_ref[d]
        starts = [sum(SMALL_WIDTHS[:i]) for i in range(len(SMALL_WIDTHS))]
        mid, conv, gain, sink, inp = (total[:, a:a + w_] for a, w_ in zip(starts, SMALL_WIDTHS))
        loss_ref[...] = (0.5 / D_MODEL) * jnp.sum(mid[ROW_LOSS:ROW_LOSS + 1, :], axis=1, keepdims=True)
        conv_rows = conv[ROW_CW0:ROW_CW0 + 3, :]
        conv_g = jnp.zeros((3, conv_shard), F32)
        for d in range(N_DEV):
            conv_g = conv_g + jnp.where(dev_ref[0] == d, conv_rows[:, conv_shard * d:conv_shard * (d + 1)], 0.0)
        grads = [inp[0:1, :], mid[ROW_G2:ROW_G2 + 1, :], mid[ROW_G3:ROW_G3 + 1, :], mid[ROW_G4:ROW_G4 + 1, :],
                 gain[0:1, :], conv[ROW_GCONV:ROW_GCONV + 1, :], conv_g, sink[0:1, :N_HEADS]]
        for i, g in enumerate(grads):
            parts = [(..., g)] if len(w_refs[i].shape) == 2 else [(r, g[r:r + 1, :]) for r in range(g.shape[0])]
            for at, g_at in parts:
                delta, new_m, new_v = _adam_math(w_refs[i][at], g_at, m_refs[i][at], v_refs[i][at])
                outs[i][at], outs[n + i][at], outs[2 * n + i][at], outs[3 * n + i][at] = g_at, delta, new_m, new_v

    params = list(weights) + list(first_moments) + list(second_moments)
    shapes = [jax.ShapeDtypeStruct(w.shape, F32) for w in weights]
    res = pl.pallas_call(
        body, name="small_tail", grid=(1,),
        in_specs=[pl.BlockSpec(memory_space=pltpu.SMEM), _full(gathered.shape)] + [_full(p.shape) for p in params],
        out_specs=[_full((1, 1))] + [_full(sh.shape) for sh in shapes] * 4,
        out_shape=[jax.ShapeDtypeStruct((1, 1), F32)] + shapes * 4,
    )(dev, gathered, *params)
    return res[0], [res[1 + k * n:1 + (k + 1) * n] for k in range(4)]


TOKEN_TILE = 512
MID_TILE = 256
MID_CHUNK = 1024
MID_CHUNKS = D_FF // MID_CHUNK
ADAM_ROWS = 512


def _local_grads(x, target, g1, w_in_shard, conv_shard, sinks, g_attn, g_conv, g2, g3, g4, shards, order):
    t = x.shape[0]
    tm = min(TOKEN_TILE, t)
    rope = _rope_tables(t)
    qkv, gates, mconv, w_in, conv_w, gathered = _in_proj_fwd(x, g1, w_in_shard, conv_shard, g_conv, rope, tm, shards,
                                                             (False, True, False))
    attn, mattn, (w_out, w_up, w_down) = _attn_fwd(qkv, sinks, g_attn, shards, gathered)
    act, dup, hn2t, dmo, dmix, dh, dmixed, small_mid = _mid(
        mattn, mconv, x, target, g2, g3, g4, w_out.reshape(D_MODEL, D_MODEL),
        w_up, w_down.reshape(D_FF, D_MODEL), min(MID_TILE, t))
    up_own, up_sib, up_sums = _dw_pair_sums((hn2t, dup), order, "up", "dw_up", 2)
    down_own, down_sib, down_sums, up_far = _dw_pair_sums((act, dmo), order, "down", "dw_down", 3, ride=up_sums)
    out_own, out_sib, out_sums = _dw_pair_sums((mattn, mconv, dmix), order, "out", "dw_out", 4)
    dgates, small_conv = _conv_bwd(dmixed, gates, g_conv, conv_w, tm)
    dqkv, dsink, dg_attn, down_far = _attn_bwd(qkv, dmixed, attn, g_attn, sinks, rope, down_sums)
    grad_x_head, dw_in, small_in, out_far = _in_proj_bwd(dqkv, dgates, x, dh, g1, w_in, tm, out_sums)
    grad_x, small_in = _grad_x_rest(dqkv, dgates, x, dh, g1, w_in, tm, grad_x_head, small_in)
    dw_out, dw_up, dw_down = (out_own, out_sib, out_far), (up_own, up_sib, up_far), (down_own, down_sib, down_far)
    return grad_x, dw_in, dw_out, dw_up, dw_down, (small_mid, small_conv, dg_attn, dsink, small_in)


def kernel(x, pre_mix_norm, w_in, conv_w, attn_sinks, attn_group_norm, conv_group_norm, w_out, post_mix_norm, pre_mlp_norm, w_up, w_down, post_mlp_norm, loss_target, m_pre_mix_norm, m_w_in, m_conv_w, m_attn_sinks, m_attn_group_norm, m_conv_group_norm, m_w_out, m_post_mix_norm, m_pre_mlp_norm, m_w_up, m_w_down, m_post_mlp_norm, v_pre_mix_norm, v_w_in, v_conv_w, v_attn_sinks, v_attn_group_norm, v_conv_group_norm, v_w_out, v_post_mix_norm, v_pre_mlp_norm, v_w_up, v_w_down, v_post_mlp_norm):
    xi, yi, ci = _mesh_pos()
    chip = 2 * xi + yi
    dev = 2 * chip + ci

    order = _block_order(dev)

    shards = [w_out[0].astype(BF16), w_up[0].astype(BF16), w_down[0].astype(BF16)]

    grad_x, dw_in, dw_out, dw_up, dw_down, smalls = _local_grads(
        x[0], loss_target[0], pre_mix_norm, w_in[0].astype(BF16), conv_w[0], attn_sinks, attn_group_norm, conv_group_norm,
        post_mix_norm, pre_mlp_norm, post_mlp_norm, shards, order)

    turned = lambda a: jnp.swapaxes(a, 1, 2)
    big = {}
    for name, w, m, v, (own, sib, far) in zip(
            ("w_in", "w_out", "w_up", "w_down"), (turned(w_in), w_out, w_up, w_down),
            (turned(m_w_in), m_w_out, m_w_up, m_w_down), (turned(v_w_in), v_w_out, v_w_up, v_w_down),
            (dw_in, dw_out, dw_up, dw_down)):
        big[name] = [a[None] for a in _adamw_reduced(w[0], m[0], v[0], own, sib, far, min(ADAM_ROWS, w.shape[1]))]
    big["w_in"] = [turned(a) for a in big["w_in"]]

    flat = lambda a: a.reshape(-1, a.shape[-1]) if a.ndim < 3 else a.reshape(a.shape[1], 1, a.shape[2])
    loss, small = _small_tail(
        _all_gather([jnp.concatenate(smalls, axis=1)], "gather_small")[0], dev.reshape(1).astype(jnp.int32),
        [flat(a) for a in (pre_mix_norm, post_mix_norm, pre_mlp_norm, post_mlp_norm, attn_group_norm, conv_group_norm,
                           conv_w, attn_sinks)],
        [flat(a) for a in (m_pre_mix_norm, m_post_mix_norm, m_pre_mlp_norm, m_post_mlp_norm, m_attn_group_norm,
                           m_conv_group_norm, m_conv_w, m_attn_sinks)],
        [flat(a) for a in (v_pre_mix_norm, v_post_mix_norm, v_pre_mlp_norm, v_post_mlp_norm, v_attn_group_norm,
                           v_conv_group_norm, v_conv_w, v_attn_sinks)])

    order = ("pre_mix_norm", "w_in", "conv_w", "attn_sinks", "attn_group_norm", "conv_group_norm", "w_out",
             "post_mix_norm", "pre_mlp_norm", "w_up", "w_down", "post_mlp_norm")
    shape_of = {"conv_w": conv_w.shape}
    outs = []
    for k in range(4):
        by_name = dict(zip(SMALL_PARAMS, small[k]))
        outs += [big[nm][k] if nm in big else by_name[nm].reshape(shape_of.get(nm, by_name[nm].shape)) for nm in order]
    loss = loss.reshape(())
    return (loss, grad_x[None], *outs)
```

```python
import jax
import jax.numpy as jnp
import numpy as np
from jax import lax
from jax.experimental import pallas as pl
from jax.experimental.pallas import tpu as pltpu

F32 = jnp.float32
BF16 = jnp.bfloat16

D_MODEL = 1024
HEAD_DIM = 64
ATTN_W = 512
CONV_W = 512
N_HEADS = 8
N_KV = 2
GROUP = 4
KV_W = 128
QKV_W = ATTN_W + 2 * KV_W
GATES_W = 3 * CONV_W
IN_COLS = QKV_W + GATES_W
D_FF = 4096
FF_CHUNK = 512
BLOCK = 128
ROT_HALF = 8
ROPE_THETA = 500000.0
NORM_EPS = 1e-6
NEG_INF = -1e30
ATTN_SCALE = 0.125
N_DEV = 8
N_CHIPS = 4
IN_SHARD = IN_COLS // N_DEV

ADAM_LR = 0.001
ADAM_B1 = 0.9
ADAM_B2 = 0.999
ADAM_EPS = 1e-08
ADAM_WD = 0.01
ADAM_STEP = 10

V7X_VMEM_BYTES = 64 * 1024 * 1024
VMEM_LIMIT = V7X_VMEM_BYTES - 2 * 1024 * 1024

MESH = pl.DeviceIdType.MESH
HBM_SPEC = pl.BlockSpec(memory_space=pltpu.HBM)


def _params(*sem, barrier_id=None):
    return pltpu.CompilerParams(dimension_semantics=sem or None, vmem_limit_bytes=VMEM_LIMIT, collective_id=barrier_id)


def _mm(a, b):
    return jnp.dot(a, b, preferred_element_type=F32)


def _mm_nt(a, b):
    return lax.dot_general(a, b, (((1,), (1,)), ((), ())), preferred_element_type=F32)


def _mm_tn(a, b):
    return lax.dot_general(a, b, (((0,), (0,)), ((), ())), preferred_element_type=F32)


def _inv_rms(x):
    return lax.rsqrt(jnp.mean(x * x, axis=-1, keepdims=True) + NORM_EPS)


def _rms_bwd(xhat, r, gain, dy):
    gy = dy * gain
    return r * (gy - xhat * jnp.mean(gy * xhat, axis=-1, keepdims=True)), dy * xhat


def _colsum(a):
    return jnp.sum(a, axis=0, keepdims=True)


def _full(shape):
    zeros = (0,) * len(shape)
    return pl.BlockSpec(shape, lambda *_: zeros)


def _resident(shape):
    zeros = (0,) * len(shape)
    return pl.BlockSpec(shape, lambda *_: zeros, pipeline_mode=pl.Buffered(1))


def _rope_tables(t):
    pos = np.arange(t, dtype=np.float32)
    inv_freq = (ROPE_THETA ** (-np.arange(0, 2 * ROT_HALF, 2, dtype=np.float64) / (2 * ROT_HALF))).astype(np.float32)
    ang = (pos[:, None] * inv_freq[None, :]).astype(np.float64)
    cos, sin = np.cos(ang).astype(np.float32), np.sin(ang).astype(np.float32)
    zeros8 = np.zeros((t, ROT_HALF), np.float32)
    rest = np.zeros((t, HEAD_DIM - 2 * ROT_HALF), np.float32)
    c_head = np.concatenate([cos, cos, rest + 1.0], axis=1)
    s1_head = np.concatenate([zeros8, sin, rest], axis=1)
    s2_head = np.concatenate([-sin, zeros8, rest], axis=1)
    two = lambda a: jnp.asarray(np.concatenate([a, a], axis=1))
    return two(c_head), two(s1_head), two(s2_head)


def _rope(v, c, s1, s2):
    return v * c + pltpu.roll(v, ROT_HALF, 1) * s1 + pltpu.roll(v, 128 - ROT_HALF, 1) * s2


def _rope_transpose(dv, c, s1, s2):
    return dv * c + pltpu.roll(dv * s1, 128 - ROT_HALF, 1) + pltpu.roll(dv * s2, ROT_HALF, 1)


def _shift_rows_down(u, prev, k):
    row = lax.broadcasted_iota(jnp.int32, u.shape, 0)
    out = pltpu.roll(u, k, 0)
    for r in range(k):
        out = jnp.where(row == r, prev[8 - k + r:8 - k + r + 1, :], out)
    return out


def _shift_rows_up(u, nxt, k):
    n = u.shape[0]
    row = lax.broadcasted_iota(jnp.int32, u.shape, 0)
    out = pltpu.roll(u, n - k, 0)
    for r in range(k):
        out = jnp.where(row == n - k + r, nxt[r:r + 1, :], out)
    return out


def _conv3(u, u1, u2, w):
    return (w[0:1, :] * u2 + w[1:2, :] * u1) + w[2:3, :] * u


def _mesh_pos():
    return lax.axis_index("x"), lax.axis_index("y"), lax.axis_index("c")


def _slot(ref, pos):
    dev = 4 * pos[0] + 2 * pos[1] + pos[2]
    if len(ref.shape) == 2:
        width = ref.shape[1] // N_DEV
        return ref.at[:, pl.ds(pl.multiple_of(dev * width, width), width)]
    return ref.at[dev]


def _gathered_shape(shard, by_cols):
    if by_cols:
        return jax.ShapeDtypeStruct((shard.shape[0], N_DEV * shard.shape[1]), shard.dtype)
    return jax.ShapeDtypeStruct((N_DEV,) + shard.shape, shard.dtype)


def _enter_with(peers):
    barrier = pltpu.get_barrier_semaphore()
    for peer in peers:
        pl.semaphore_signal(barrier, inc=1, device_id=peer, device_id_type=MESH)
    pl.semaphore_wait(barrier, len(peers))


def _sibling_and_chips(x, y, c):
    return [(x, y, 1 - c), (1 - x, y, c), (x, 1 - y, c), (1 - x, 1 - y, c)]


def _push(src, dst, sems, k, to):
    send_sems, recv_sems = sems
    return pltpu.make_async_remote_copy(src_ref=src, dst_ref=dst, send_sem=send_sems.at[k], recv_sem=recv_sems.at[k],
                                        device_id=to, device_id_type=MESH)


def _gather_steps(shards, outs, send_sems, recv_sems, local_sems):
    n = len(shards)
    x, y, c = _mesh_pos()
    me, sibling = (x, y, c), (x, y, 1 - c)
    chips = [(1 - x, y), (x, 1 - y), (1 - x, 1 - y)]

    def copy(i, k, block, to, src=None):
        dst = _slot(outs[i], block)
        return _push(dst if src is None else src, dst, (send_sems, recv_sems), 7 * i + k, to)

    mine = [pltpu.make_async_copy(shards[i], _slot(outs[i], me), local_sems.at[i]) for i in range(n)]
    first = []
    for i in range(n):
        first.append(copy(i, 0, me, sibling, src=shards[i]))
        first += [copy(i, 1 + j, me, (*chip, c), src=shards[i]) for j, chip in enumerate(chips)]

    def start():
        for cp in mine + first:
            cp.start()

    def finish():
        passed = []
        for j, chip in enumerate(chips):
            for i in range(n):
                copy(i, 1 + j, (*chip, c), me).wait_recv()
                cp = copy(i, 4 + j, (*chip, c), sibling)
                cp.start()
                passed.append(cp)
        for i in range(n):
            copy(i, 0, sibling, me).wait_recv()
            for j, chip in enumerate(chips):
                copy(i, 4 + j, (*chip, 1 - c), me).wait_recv()
        for cp in first + passed:
            cp.wait_send()
        for cp in mine:
            cp.wait()

    return start, finish


def _gather_near(first, last, shards, outs, sems, local_sems):
    x, y, c = _mesh_pos()
    me, peers = (x, y, c), [(x, y, 1 - c), (1 - x, y, c), (x, 1 - y, c)]
    n = len(shards)
    local = [pltpu.make_async_copy(shards[i], _slot(outs[i], me), local_sems.at[i]) for i in range(n)]
    sends = [_push(shards[i], _slot(outs[i], me), sems, 3 * i + k, peers[k]) for i in range(n) for k in range(3)]
    arrivals = [_push(shards[i], _slot(outs[i], peers[k]), sems, 3 * i + k, peers[k]) for i in range(n) for k in range(3)]

    def start():
        for cp in local + sends:
            cp.start()

    if first is not None:
        pl.when(first)(start)

    @pl.when(last)
    def _():
        for cp in sends:
            cp.wait_send()
        for cp in arrivals:
            cp.wait_recv()
        for cp in local:
            cp.wait()

    return start


def _relay_route(x, y, c):
    south = c == 0
    via = (jnp.where(south, 1 - x, x), jnp.where(south, y, 1 - y))
    to = (jnp.where(south, x, 1 - x), jnp.where(south, 1 - y, y))
    return via, to


def _gather_far(first, middle, last, shards, ins, outs, sems):
    x, y, c = _mesh_pos()
    sibling = (x, y, 1 - c)
    chips = [(1 - x, y), (x, 1 - y), (1 - x, 1 - y)]
    via, to = _relay_route(x, y, c)
    n = len(shards)
    diag_send = [_push(_slot(ins[i], (*via, c)), _slot(outs[i], (*via, c)), sems, 4 * i, (*to, c)) for i in range(n)]
    diag_arrival = [_push(shards[i], _slot(outs[i], (*chips[2], c)), sems, 4 * i, (*to, c)) for i in range(n)]
    passed = [[_push(_slot(ins[i], (*chips[j], c)), _slot(outs[i], (*chips[j], c)), sems, 4 * i + 1 + j, sibling)
               for i in range(n)] for j in range(3)]
    from_sibling = [_push(shards[i], _slot(outs[i], (*chips[j], 1 - c)), sems, 4 * i + 1 + j, sibling)
                    for i in range(n) for j in range(3)]

    @pl.when(first)
    def _():
        for cp in diag_send + passed[0] + passed[1]:
            cp.start()

    @pl.when(middle)
    def _():
        for cp in diag_arrival:
            cp.wait_recv()
        for cp in passed[2]:
            cp.start()

    @pl.when(last)
    def _():
        for cp in from_sibling:
            cp.wait_recv()
        for cp in diag_send + passed[0] + passed[1] + passed[2]:
            cp.wait_send()


def _in_proj_fwd(x, g1, w_in, conv_w, g_conv, rope, tm, shards, by_cols):
    t = x.shape[0]
    rc, rs1, rs2 = rope
    n = len(shards)
    n_tiles = t // tm

    def body(*refs):
        x_ref, g1_ref, w_ref, cw_ref, gc_ref, c_ref, s1_ref, s2_ref = refs[:8]
        shard_refs = refs[8:8 + n]
        qkv_ref, gates_ref, mconv_ref, w_full_ref, cw_full_ref = refs[8 + n:13 + n]
        gathered = refs[13 + n:13 + 2 * n]
        carry_ref, w_land, cw_land, hn_ref = refs[13 + 2 * n:17 + 2 * n]
        now_sems = refs[17 + 2 * n:20 + 2 * n]
        step = pl.program_id(0)
        start_later_weights = _gather_near(None, step == 2 * n_tiles - 1, shard_refs, gathered,
                                           refs[20 + 2 * n:22 + 2 * n], refs[22 + 2 * n]) if n else None
        start_w_in, finish_w_in = _gather_steps([w_ref, cw_ref], [w_land, cw_land], *now_sems)

        @pl.when(step == 0)
        def _():
            carry_ref[...] = jnp.zeros_like(carry_ref)
            _enter_with(_sibling_and_chips(*_mesh_pos()))
            start_w_in()
            if start_later_weights is not None:
                start_later_weights()

        @pl.when(step < n_tiles)
        def _():
            xv = x_ref[...]
            hn_ref[step] = ((xv * _inv_rms(xv)) * g1_ref[...]).astype(BF16)

        @pl.when(step == n_tiles)
        def _():
            finish_w_in()
            conv_shard = CONV_W // N_DEV
            for d in range(N_DEV):
                w_full_ref[IN_SHARD * d:IN_SHARD * (d + 1), :] = w_land[d]
                cw_full_ref[:, conv_shard * d:conv_shard * (d + 1)] = cw_land[d]

        @pl.when(step >= n_tiles)
        def _():
            proj = _mm_nt(hn_ref[step - n_tiles], w_full_ref[...])
            c, s1, s2 = c_ref[...], s1_ref[...], s2_ref[...]
            for ci in range((ATTN_W + KV_W) // 128):
                sl = slice(128 * ci, 128 * (ci + 1))
                qkv_ref[:, sl] = _rope(proj[:, sl], c, s1, s2).astype(BF16)
            qkv_ref[:, ATTN_W + KV_W:QKV_W] = proj[:, ATTN_W + KV_W:QKV_W].astype(BF16)
            gates = proj[:, QKV_W:]
            gates_ref[...] = gates
            gb, gcc, xin = gates[:, :CONV_W], gates[:, CONV_W:2 * CONV_W], gates[:, 2 * CONV_W:]
            u = gcc * xin
            prev = carry_ref[...]
            conv = gb * _conv3(u, _shift_rows_down(u, prev, 1), _shift_rows_down(u, prev, 2), cw_full_ref[...])
            carry_ref[...] = u[tm - 8:tm, :]
            mconv_ref[...] = ((conv * _inv_rms(conv)) * gc_ref[...]).astype(BF16)

    first_pass = pl.BlockSpec((tm, D_MODEL), lambda i: (jnp.minimum(i, n_tiles - 1), 0))
    tile = lambda w_: pl.BlockSpec((tm, w_), lambda i: (jnp.maximum(i - n_tiles, 0), 0))
    sems = lambda k: pltpu.SemaphoreType.DMA((k,))
    res = pl.pallas_call(
        body, name="in_proj_fwd", grid=(2 * n_tiles,),
        in_specs=[first_pass, _full((1, D_MODEL)), HBM_SPEC, HBM_SPEC, _full((1, CONV_W)), tile(128), tile(128),
                  tile(128)] + [HBM_SPEC] * n,
        out_specs=[tile(QKV_W), tile(GATES_W), tile(CONV_W), _full((IN_COLS, D_MODEL)), _full((3, CONV_W))]
        + [HBM_SPEC] * n,
        out_shape=[jax.ShapeDtypeStruct((t, QKV_W), BF16), jax.ShapeDtypeStruct((t, GATES_W), F32),
                   jax.ShapeDtypeStruct((t, CONV_W), BF16), jax.ShapeDtypeStruct((IN_COLS, D_MODEL), BF16),
                   jax.ShapeDtypeStruct((3, CONV_W), F32)]
        + [_gathered_shape(s, cols) for s, cols in zip(shards, by_cols)],
        scratch_shapes=[pltpu.VMEM((8, CONV_W), F32), pltpu.VMEM((N_DEV,) + w_in.shape, BF16),
                        pltpu.VMEM((N_DEV,) + conv_w.shape, F32), pltpu.VMEM((n_tiles, tm, D_MODEL), BF16),
                        sems(14), sems(14), sems(2)]
        + ([sems(3 * n), sems(3 * n), sems(n)] if n else []),
        compiler_params=_params("arbitrary", barrier_id=0),
    )(x, g1, w_in, conv_w, g_conv, rc, rs1, rs2, *shards)
    return res[0], res[1], res[2], res[3], res[4], list(res[5:])


GROUP_COLS = GROUP * BLOCK
ATTN_STEP_BLOCKS = 4


def _attn_masks(has_prev):
    key = lax.broadcasted_iota(jnp.int32, (2 * BLOCK, GROUP_COLS), 0)
    query = lax.broadcasted_iota(jnp.int32, (2 * BLOCK, GROUP_COLS), 1) & (BLOCK - 1)
    band = (key > query) & (key <= query + BLOCK)
    return [band & ((key >= BLOCK) | has_prev)] + [band] * (ATTN_STEP_BLOCKS - 1)


def _heads_side_by_side(at, g, b):
    heads = [at[HEAD_DIM * (GROUP * g + hh):HEAD_DIM * (GROUP * g + hh + 1), BLOCK * b:BLOCK * (b + 1)] for hh in range(GROUP)]
    return jnp.concatenate(heads, axis=1)


def _to_token_rows(parts):
    rows = [jnp.concatenate([parts[b][g][:, BLOCK * hh:BLOCK * (hh + 1)] for b in range(ATTN_STEP_BLOCKS)], axis=1)
            for g in range(N_KV) for hh in range(GROUP)]
    return jnp.concatenate(rows, axis=0).T


def _group_sinks(sink_ref, g):
    head = lax.broadcasted_iota(jnp.int32, (1, GROUP_COLS), 1) // BLOCK
    out = jnp.full((1, GROUP_COLS), sink_ref[0, GROUP * g], F32)
    for hh in range(1, GROUP):
        out = jnp.where(head == hh, sink_ref[0, GROUP * g + hh], out)
    return out


def _attn_probs(qt, kk, sink, valid):
    s = jnp.where(valid, _mm(kk, qt), NEG_INF)
    m = jnp.maximum(jnp.max(s, axis=0, keepdims=True), sink)
    p = jnp.exp(s - m)
    psink = jnp.exp(sink - m)
    inv_l = 1.0 / (jnp.sum(p, axis=0, keepdims=True) + psink)
    return p * inv_l, psink * inv_l


ATTN_STEP = ATTN_STEP_BLOCKS * BLOCK
ATTN_KEYS = ATTN_STEP + BLOCK


def _qkv_specs(order):
    prev = lambda i: jnp.maximum(ATTN_STEP_BLOCKS * order(i) - 1, 0)
    kcol, vcol = ATTN_W // KV_W, ATTN_W // KV_W + 1
    return [pl.BlockSpec((ATTN_STEP, ATTN_W), lambda i: (order(i), 0)),
            pl.BlockSpec((BLOCK, KV_W), lambda i: (prev(i), kcol)), pl.BlockSpec((ATTN_STEP, KV_W), lambda i: (order(i), kcol)),
            pl.BlockSpec((BLOCK, KV_W), lambda i: (prev(i), vcol)), pl.BlockSpec((ATTN_STEP, KV_W), lambda i: (order(i), vcol))]


def _attn_fwd(qkv, sinks, g_attn, shards, gathered):
    t = qkv.shape[0]
    n = len(shards)

    def body(*refs):
        sink_ref, q_ref, kp_ref, kc_ref, vp_ref, vc_ref, ga_ref = refs[:7]
        attn_ref, mattn_ref = refs[7 + 2 * n:9 + 2 * n]
        step = pl.program_id(0)
        if n:
            @pl.when(step == 0)
            def _():
                x, y, c = _mesh_pos()
                _enter_with([(x, y, 1 - c), (*_relay_route(x, y, c)[1], c)])

            n_steps = t // ATTN_STEP
            _gather_far(step == 0, step == n_steps // 2, step == n_steps - 1, refs[7:7 + n], refs[7 + n:7 + 2 * n],
                        refs[9 + 2 * n:9 + 3 * n], refs[9 + 3 * n:11 + 3 * n])
        qt = (q_ref[...] * ATTN_SCALE).T
        keys = jnp.concatenate([kp_ref[...], kc_ref[...]], axis=0)
        vals = jnp.concatenate([vp_ref[...], vc_ref[...]], axis=0)
        sink = [_group_sinks(sink_ref, g) for g in range(N_KV)]
        masks = _attn_masks(step > 0)
        parts = []
        for b in range(ATTN_STEP_BLOCKS):
            window = slice(BLOCK * b, BLOCK * (b + 2))
            valid = masks[b]
            parts.append([])
            for g in range(N_KV):
                gs = slice(HEAD_DIM * g, HEAD_DIM * (g + 1))
                probs, _ = _attn_probs(_heads_side_by_side(qt, g, b), keys[window, gs], sink[g], valid)
                parts[b].append(_mm_tn(vals[window, gs], probs.astype(BF16)))
        attn = _to_token_rows(parts)
        attn_ref[...] = attn
        mattn_ref[...] = ((attn * _inv_rms(attn)) * ga_ref[...]).astype(BF16)

    blk = pl.BlockSpec((ATTN_STEP, ATTN_W), lambda j: (j, 0))
    res = pl.pallas_call(
        body, name="attn_fwd", grid=(t // ATTN_STEP,),
        in_specs=[pl.BlockSpec(memory_space=pltpu.SMEM)] + _qkv_specs(lambda j: j) + [_full((1, ATTN_W))]
        + [HBM_SPEC] * (2 * n),
        out_specs=[blk, blk] + [HBM_SPEC] * n,
        out_shape=[jax.ShapeDtypeStruct((t, ATTN_W), F32), jax.ShapeDtypeStruct((t, ATTN_W), BF16)]
        + [jax.ShapeDtypeStruct(g.shape, g.dtype) for g in gathered],
        input_output_aliases={7 + n + i: 2 + i for i in range(n)},
        scratch_shapes=[pltpu.SemaphoreType.DMA((4 * n,)), pltpu.SemaphoreType.DMA((4 * n,))] if n else [],
        compiler_params=_params("arbitrary", barrier_id=1 if n else None),
    )(sinks, qkv, qkv, qkv, qkv, qkv, g_attn, *shards, *gathered)
    return res[0], res[1], list(res[2:])


SMALL_ROWS = 8
ROW_LOSS, ROW_G2, ROW_G3, ROW_G4 = 0, 1, 2, 3


def _mid(mattn, mconv, x, target, g2, g3, g4, w_out, w_up, w_down, tm):
    t = x.shape[0]

    def body(ma_ref, mc_ref, x_ref, t_ref, g2_ref, g3_ref, g4_ref, wo_ref, wu_ref, wd_ref,
             act_ref, dup_ref, hn2t_ref, dmo_ref, dmix_ref, dh_ref, dmixed_ref, small_ref, up_ref):
        @pl.when(pl.program_id(0) == 0)
        def _():
            small_ref[...] = jnp.zeros_like(small_ref)

        g2, g3, g4 = g2_ref[...], g3_ref[...], g4_ref[...]
        mix_out = _mm(ma_ref[...], wo_ref[0:ATTN_W, :]) + _mm(mc_ref[...], wo_ref[ATTN_W:, :])
        r2 = _inv_rms(mix_out)
        mo_hat = mix_out * r2
        h = x_ref[...] + mo_hat * g2
        r3 = _inv_rms(h)
        h_hat = h * r3
        hn2 = (h_hat * g3).astype(BF16)
        hn2t_ref[...] = hn2.T
        for j in range(MID_CHUNKS):
            cols_j = slice(MID_CHUNK * j, MID_CHUNK * (j + 1))
            up = jnp.maximum(_mm(hn2, wu_ref[:, cols_j]), 0.0)
            up_ref[:, cols_j] = up.astype(BF16)
            act_ref[:, cols_j] = (up * up).astype(BF16)
        mlp = _mm(act_ref[...], wd_ref[...])
        r4 = _inv_rms(mlp)
        ml_hat = mlp * r4
        err = (h + ml_hat * g4) - t_ref[...]
        d_out = err * (1.0 / D_MODEL)
        d_mlp, dg4 = _rms_bwd(ml_hat, r4, g4, d_out)
        dmo = d_mlp.astype(BF16)
        dmo_ref[...] = dmo
        for j in range(MID_CHUNKS):
            cols_j = slice(MID_CHUNK * j, MID_CHUNK * (j + 1))
            dact = _mm_nt(dmo, wd_ref[cols_j, :])
            dup_ref[:, cols_j] = (dact * (2.0 * up_ref[:, cols_j].astype(F32))).astype(BF16)
        dhn2 = _mm_nt(dup_ref[...], wu_ref[...])
        dh_norm, dg3 = _rms_bwd(h_hat, r3, g3, dhn2)
        dh = d_out + dh_norm
        dh_ref[...] = dh
        d_mix, dg2 = _rms_bwd(mo_hat, r2, g2, dh)
        dmix = d_mix.astype(BF16)
        dmix_ref[...] = dmix
        dmixed_ref[...] = _mm_nt(dmix, wo_ref[...])
        small_ref[ROW_LOSS:ROW_LOSS + 1, :] += _colsum(err * err)
        small_ref[ROW_G2:ROW_G2 + 1, :] += _colsum(dg2)
        small_ref[ROW_G3:ROW_G3 + 1, :] += _colsum(dg3)
        small_ref[ROW_G4:ROW_G4 + 1, :] += _colsum(dg4)

    tile = lambda n: pl.BlockSpec((tm, n), lambda i: (i, 0))
    cols = lambda n: pl.BlockSpec((n, tm), lambda i: (0, i))
    gain = _full((1, D_MODEL))
    return pl.pallas_call(
        body, name="mid_fwd_bwd", grid=(t // tm,),
        in_specs=[tile(ATTN_W), tile(CONV_W), tile(D_MODEL), tile(D_MODEL), gain, gain, gain,
                  _resident((D_MODEL, D_MODEL)), _resident((D_MODEL, D_FF)), _resident((D_FF, D_MODEL))],
        out_specs=[tile(D_FF), tile(D_FF), cols(D_MODEL), tile(D_MODEL), tile(D_MODEL), tile(D_MODEL), tile(D_MODEL),
                   _full((SMALL_ROWS, D_MODEL))],
        out_shape=[jax.ShapeDtypeStruct((t, D_FF), BF16), jax.ShapeDtypeStruct((t, D_FF), BF16),
                   jax.ShapeDtypeStruct((D_MODEL, t), BF16), jax.ShapeDtypeStruct((t, D_MODEL), BF16),
                   jax.ShapeDtypeStruct((t, D_MODEL), BF16), jax.ShapeDtypeStruct((t, D_MODEL), F32),
                   jax.ShapeDtypeStruct((t, D_MODEL), F32), jax.ShapeDtypeStruct((SMALL_ROWS, D_MODEL), F32)],
        scratch_shapes=[pltpu.VMEM((tm, D_FF), BF16)],
        compiler_params=_params("arbitrary"),
    )(mattn, mconv, x, target, g2, g3, g4, w_out, w_up, w_down)


CHIP_FLIPS = ((1, 1), (1, 0), (0, 1))


def _block_order(dev):
    chip_masks = [4 * fx + 2 * fy for fx, fy in CHIP_FLIPS]
    masks = [m + 1 for m in chip_masks] + [1] + chip_masks + [0]
    return jnp.bitwise_xor(dev, jnp.asarray(masks, jnp.int32)).astype(jnp.int32)


def _other_chips(x, y, c):
    return [(1 - x if fx else x, 1 - y if fy else y, c) for fx, fy in CHIP_FLIPS]


def _dw_pair_sums(operands, order, which, name, barrier_id, ride=None):
    t = operands[-1].shape[0]
    n_far = len(CHIP_FLIPS)
    n_in = len(operands)
    n_ride = 0 if ride is None else 1
    out_chunk = D_MODEL // N_DEV
    if which == "up":
        rows, cols = D_MODEL, FF_CHUNK
        in_specs = [_resident((D_MODEL, t)), pl.BlockSpec((t, FF_CHUNK), lambda s, order_ref: (0, order_ref[s]))]
    elif which == "down":
        rows, cols = FF_CHUNK, D_MODEL
        in_specs = [pl.BlockSpec((t, FF_CHUNK), lambda s, order_ref: (0, order_ref[s])), _resident((t, D_MODEL))]
    else:
        rows, cols = out_chunk, D_MODEL
        half = pl.BlockSpec((t, out_chunk), lambda s, order_ref: (0, order_ref[s] % (N_DEV // 2)))
        in_specs = [half, half, _resident((t, D_MODEL))]

    def body(order_ref, *refs):
        own_ref, from_sib_ref, pair_ref = refs[n_in + n_ride:n_in + n_ride + 3]
        send_buf, land_buf, send_sems, recv_sems = refs[n_in + 2 * n_ride + 3:n_in + 2 * n_ride + 7]
        s_now = pl.program_id(0)
        x, y, c = _mesh_pos()
        sibling = (x, y, 1 - c)
        sems = (send_sems, recv_sems)

        @pl.when(s_now == 0)
        def _():
            _enter_with([sibling] + (_other_chips(x, y, c) if n_ride else []))

        if n_ride:
            _chip_exchange_beside(s_now == 0, s_now == N_DEV - 1, [refs[n_in]], [refs[n_in + 3 + n_ride]],
                                  refs[n_in + 2 * n_ride + 7:], enter=False)

        def hand_over(k):
            dst = land_buf.at[k] if k < n_far else from_sib_ref
            return _push(send_buf.at[k], dst, sems, k, sibling)

        if which == "out":
            ma_ref, mc_ref, b_ref = refs[:n_in]
            block = lax.cond(order_ref[s_now] < N_DEV // 2, lambda: _mm_tn(ma_ref[...], b_ref[...]),
                             lambda: _mm_tn(mc_ref[...], b_ref[...]))
        elif which == "down":
            block = _mm_tn(refs[0][...], refs[1][...])
        else:
            block = _mm(refs[0][...], refs[1][...])
        for k in range(n_far + 1):
            @pl.when(s_now == k)
            def _():
                send_buf[k] = block.astype(BF16)
                hand_over(k).start()

        for k in range(n_far):
            @pl.when(s_now == n_far + 1 + k)
            def _():
                hand_over(k).wait_recv()
                pair_ref[...] = (block + land_buf[k].astype(F32)).astype(BF16)

        @pl.when(s_now == N_DEV - 1)
        def _():
            own_ref[...] = block
            for k in range(n_far + 1):
                hand_over(k).wait_send()
            hand_over(n_far).wait_recv()

    rides = [] if ride is None else [ride]
    sems = lambda k: pltpu.SemaphoreType.DMA((k,))
    return pl.pallas_call(
        body, name=name,
        grid_spec=pltpu.PrefetchScalarGridSpec(
            num_scalar_prefetch=1, grid=(N_DEV,), in_specs=in_specs + [HBM_SPEC] * n_ride,
            out_specs=[pl.BlockSpec((rows, cols), lambda s, order_ref: (0, 0)), HBM_SPEC,
                       pl.BlockSpec((None, rows, cols), lambda s, order_ref: (jnp.clip(s - n_far - 1, 0, n_far - 1), 0, 0))]
            + [HBM_SPEC] * n_ride,
            scratch_shapes=[pltpu.VMEM((n_far + 1, rows, cols), BF16), pltpu.VMEM((n_far, rows, cols), BF16),
                            sems(n_far + 1), sems(n_far + 1)] + [sems(n_far), sems(n_far)] * n_ride),
        out_shape=[jax.ShapeDtypeStruct((rows, cols), F32), jax.ShapeDtypeStruct((rows, cols), BF16),
                   jax.ShapeDtypeStruct((n_far, rows, cols), BF16)]
        + [jax.ShapeDtypeStruct(r.shape, r.dtype) for r in rides],
        compiler_params=_params("arbitrary", barrier_id=barrier_id),
    )(order, *operands, *rides)


def _chip_exchange_beside(first, last, sums, outs, sems, enter=True):
    chips = _other_chips(*_mesh_pos())
    copies = [_push(sums[i].at[k], outs[i].at[k], sems, len(chips) * i + k, chip)
              for i in range(len(sums)) for k, chip in enumerate(chips)]

    @pl.when(first)
    def _():
        if enter:
            _enter_with(chips)
        for cp in copies:
            cp.start()

    @pl.when(last)
    def _():
        for cp in copies:
            cp.wait()


ROW_GCONV, ROW_CW0 = 1, 2


def _conv_bwd(dmixed, gates, g_conv, conv_w, tm):
    t = gates.shape[0]
    n = t // tm
    rev = lambda i: n - 1 - i

    def body(dm_ref, gates_ref, gprev_ref, gc_ref, cw_ref, dgates_ref, small_ref, carry_ref):
        i = pl.program_id(0)

        @pl.when(i == 0)
        def _():
            small_ref[...] = jnp.zeros_like(small_ref)
            carry_ref[...] = jnp.zeros_like(carry_ref)

        gates = gates_ref[...]
        gb, gcc, xin = gates[:, :CONV_W], gates[:, CONV_W:2 * CONV_W], gates[:, 2 * CONV_W:]
        u = gcc * xin
        gp = gprev_ref[...]
        uprev = jnp.where(rev(i) == 0, 0.0, gp[:, CONV_W:2 * CONV_W] * gp[:, 2 * CONV_W:])
        u1, u2 = _shift_rows_down(u, uprev, 1), _shift_rows_down(u, uprev, 2)
        w = cw_ref[...]
        c = _conv3(u, u1, u2, w)
        conv = gb * c
        rcv = _inv_rms(conv)
        c_hat = conv * rcv
        dconv, dgc = _rms_bwd(c_hat, rcv, gc_ref[...], dm_ref[...])
        dc = dconv * gb
        nxt = carry_ref[...]
        du = (w[2:3, :] * dc + w[1:2, :] * _shift_rows_up(dc, nxt, 1)) + w[0:1, :] * _shift_rows_up(dc, nxt, 2)
        carry_ref[...] = dc[0:8, :]
        dgates_ref[:, :CONV_W] = (dconv * c).astype(BF16)
        dgates_ref[:, CONV_W:2 * CONV_W] = (du * xin).astype(BF16)
        dgates_ref[:, 2 * CONV_W:] = (du * gcc).astype(BF16)
        small_ref[ROW_GCONV:ROW_GCONV + 1, :] += _colsum(dgc)
        small_ref[ROW_CW0:ROW_CW0 + 1, :] += _colsum(dc * u2)
        small_ref[ROW_CW0 + 1:ROW_CW0 + 2, :] += _colsum(dc * u1)
        small_ref[ROW_CW0 + 2:ROW_CW0 + 3, :] += _colsum(dc * u)

    tile = lambda w_: pl.BlockSpec((tm, w_), lambda i: (rev(i), 0))
    prev8 = pl.BlockSpec((8, GATES_W), lambda i: (jnp.maximum(rev(i) * (tm // 8) - 1, 0), 0))
    conv_half = pl.BlockSpec((tm, CONV_W), lambda i: (rev(i), ATTN_W // CONV_W))
    return pl.pallas_call(
        body, name="conv_bwd", grid=(n,),
        in_specs=[conv_half, tile(GATES_W), prev8, _full((1, CONV_W)), _full((3, CONV_W))],
        out_specs=[tile(GATES_W), _full((SMALL_ROWS, CONV_W))],
        out_shape=[jax.ShapeDtypeStruct((t, GATES_W), BF16), jax.ShapeDtypeStruct((SMALL_ROWS, CONV_W), F32)],
        scratch_shapes=[pltpu.VMEM((8, CONV_W), F32)],
        compiler_params=_params("arbitrary"),
    )(dmixed, gates, gates, g_conv, conv_w)


def _attn_bwd(qkv, dmixed, attn, g_attn, sinks, rope, sums):
    t = qkv.shape[0]
    n_steps = t // ATTN_STEP
    rev = lambda i: n_steps - 1 - i
    rc, rs1, rs2 = rope

    def body(sink_ref, q_ref, kp_ref, kc_ref, vp_ref, vc_ref, dm_ref, attn_ref, ga_ref, c_ref, s1_ref, s2_ref, sums_ref,
             dqkv_ref, dsink_ref, dgain_ref, arrived_ref, ck_ref, cv_ref, kacc_ref, vacc_ref, send_sems, recv_sems):
        i = pl.program_id(0)
        _chip_exchange_beside(i == 0, i == n_steps - 1, [sums_ref], [arrived_ref], (send_sems, recv_sems))

        @pl.when(i == 0)
        def _():
            dsink_ref[...] = jnp.zeros_like(dsink_ref)
            dgain_ref[...] = jnp.zeros_like(dgain_ref)
            ck_ref[...] = jnp.zeros_like(ck_ref)
            cv_ref[...] = jnp.zeros_like(cv_ref)

        kacc_ref[...] = jnp.zeros_like(kacc_ref)
        vacc_ref[...] = jnp.zeros_like(vacc_ref)
        a = attn_ref[...]
        ra = _inv_rms(a)
        dattn, dgain = _rms_bwd(a * ra, ra, ga_ref[...], dm_ref[...])
        dgain_ref[0:1, :] += _colsum(dgain)
        qt = (q_ref[...] * ATTN_SCALE).T
        dot = dattn.astype(BF16).T
        keys = jnp.concatenate([kp_ref[...], kc_ref[...]], axis=0)
        vals = jnp.concatenate([vp_ref[...], vc_ref[...]], axis=0)
        sink = [_group_sinks(sink_ref, g) for g in range(N_KV)]
        c, s1, s2 = c_ref[...], s1_ref[...], s2_ref[...]
        lane = lax.broadcasted_iota(jnp.int32, (1, 128), 1)
        dsink = jnp.zeros((1, 128), F32)
        masks = _attn_masks(rev(i) > 0)
        dq_parts = []
        for b in range(ATTN_STEP_BLOCKS):
            window = slice(BLOCK * b, BLOCK * (b + 2))
            valid = masks[b]
            dq_parts.append([])
            dk_parts, dv_parts = [], []
            for g in range(N_KV):
                gs = slice(HEAD_DIM * g, HEAD_DIM * (g + 1))
                kk, vv = keys[window, gs], vals[window, gs]
                qtg, dotg = _heads_side_by_side(qt, g, b), _heads_side_by_side(dot, g, b)
                probs, psink = _attn_probs(qtg, kk, sink[g], valid)
                dp = _mm(vv, dotg)
                delta = jnp.sum(probs * dp, axis=0, keepdims=True)
                ds = (probs * (dp - delta)).astype(BF16)
                sink_terms = psink * delta
                for hh in range(GROUP):
                    head_sum = jnp.sum(sink_terms[:, BLOCK * hh:BLOCK * (hh + 1)])
                    dsink = dsink + jnp.where(lane == GROUP * g + hh, -head_sum, 0.0)
                dq_parts[b].append(_mm_tn(kk * ATTN_SCALE, ds))
                dk_parts.append(_mm_nt(ds, qtg))
                dv_parts.append(_mm_nt(probs.astype(BF16), dotg))
            kacc_ref[window, :] += jnp.concatenate(dk_parts, axis=1)
            vacc_ref[window, :] += jnp.concatenate(dv_parts, axis=1)
        dq = _to_token_rows(dq_parts)
        for ci in range(ATTN_W // 128):
            sl = slice(128 * ci, 128 * (ci + 1))
            dqkv_ref[:, sl] = _rope_transpose(dq[:, sl], c, s1, s2).astype(BF16)
        kacc_ref[ATTN_STEP:, :] += ck_ref[...]
        vacc_ref[ATTN_STEP:, :] += cv_ref[...]
        ck_ref[...] = kacc_ref[:BLOCK, :]
        cv_ref[...] = vacc_ref[:BLOCK, :]
        dqkv_ref[:, ATTN_W:ATTN_W + KV_W] = _rope_transpose(kacc_ref[BLOCK:, :], c, s1, s2).astype(BF16)
        dqkv_ref[:, ATTN_W + KV_W:] = vacc_ref[BLOCK:, :].astype(BF16)
        dsink_ref[0:1, :] += dsink

    blk = lambda w_: pl.BlockSpec((ATTN_STEP, w_), lambda i: (rev(i), 0))
    return pl.pallas_call(
        body, name="attn_bwd", grid=(n_steps,),
        in_specs=[pl.BlockSpec(memory_space=pltpu.SMEM)] + _qkv_specs(rev)
        + [blk(ATTN_W), blk(ATTN_W), _full((1, ATTN_W)), blk(128), blk(128), blk(128), HBM_SPEC],
        out_specs=[blk(QKV_W), _full((8, 128)), _full((SMALL_ROWS, ATTN_W)), HBM_SPEC],
        out_shape=[jax.ShapeDtypeStruct((t, QKV_W), BF16), jax.ShapeDtypeStruct((8, 128), F32),
                   jax.ShapeDtypeStruct((SMALL_ROWS, ATTN_W), F32), jax.ShapeDtypeStruct(sums.shape, sums.dtype)],
        scratch_shapes=[pltpu.VMEM((BLOCK, KV_W), F32), pltpu.VMEM((BLOCK, KV_W), F32),
                        pltpu.VMEM((ATTN_KEYS, KV_W), F32), pltpu.VMEM((ATTN_KEYS, KV_W), F32),
                        pltpu.SemaphoreType.DMA((len(CHIP_FLIPS),)), pltpu.SemaphoreType.DMA((len(CHIP_FLIPS),))],
        compiler_params=_params("arbitrary", barrier_id=6),
    )(sinks, qkv, qkv, qkv, qkv, qkv, dmixed, attn, g_attn, rc, rs1, rs2, sums)


def _grad_x_tile(dq, dg, x_hat, r, g1, w_ref, dh):
    dhn = _mm(dq, w_ref[:QKV_W, :]) + _mm(dg, w_ref[QKV_W:, :])
    dx, dg1 = _rms_bwd(x_hat, r, g1, dhn)
    return dh + dx, _colsum(dg1)


def _in_proj_bwd(dqkv, dgates, x, dh, g1, w_in, tm, out_sums):
    t = x.shape[0]
    n = t // tm
    n_cover = max(n // 2, 1)
    n_steps = n + n_cover
    n_far = len(CHIP_FLIPS)
    shard = (IN_SHARD, D_MODEL)

    def body(dq_ref, dg_ref, x_ref, dh_ref, g1_ref, w_ref, osums_ref,
             dx_ref, own_ref, sib_ref, far_ref, dg1_ref, oarrived_ref,
             acc_ref, send_buf, land_buf, pair_buf, d2d_send, d2d_recv, ici_send, ici_recv, o_send, o_recv):
        i = pl.program_id(0)
        x_pos, y_pos, c = _mesh_pos()
        my_chip = 2 * x_pos + y_pos
        sibling = (x_pos, y_pos, 1 - c)
        @pl.when(i == 0)
        def _():
            _enter_with(_sibling_and_chips(x_pos, y_pos, c))

        _chip_exchange_beside(i == 0, i == n_steps - 1, [osums_ref], [oarrived_ref], (o_send, o_recv), enter=False)

        def rows(d):
            return slice(IN_SHARD * d, IN_SHARD * (d + 1))

        def hand_over(chip):
            return _push(send_buf.at[chip], land_buf.at[chip], (d2d_send, d2d_recv), chip, sibling)

        def to_chip(chip, rel):
            return pltpu.make_async_remote_copy(
                src_ref=pair_buf.at[chip], dst_ref=far_ref.at[rel - 1], send_sem=ici_send.at[rel - 1],
                recv_sem=ici_recv.at[rel - 1], device_id=(chip // 2, chip % 2, c), device_id_type=MESH)

        @pl.when(i == 0)
        def _():
            acc_ref[...] = jnp.zeros_like(acc_ref)
            dg1_ref[...] = jnp.zeros_like(dg1_ref)

        def normed_x():
            xv = x_ref[...]
            r = _inv_rms(xv)
            return xv * r, r

        @pl.when(i < n)
        def _():
            hn = (normed_x()[0] * g1_ref[...]).astype(BF16)
            acc_ref[:QKV_W, :] += _mm_tn(dq_ref[...], hn)
            acc_ref[QKV_W:, :] += _mm_tn(dg_ref[...], hn)

        @pl.when(i == n - 1)
        def _():
            for d in range(N_DEV):
                @pl.when(d % 2 != c)
                def _():
                    send_buf[d // 2] = acc_ref[rows(d), :].astype(BF16)
                    hand_over(d // 2).start()
            for d in range(N_DEV):
                chip = d // 2

                @pl.when(d % 2 == c)
                def _():
                    hand_over(chip).wait_recv()

                    @pl.when(chip == my_chip)
                    def _():
                        own_ref[...] = acc_ref[rows(d), :]
                        sib_ref[...] = land_buf[chip]

                    @pl.when(chip != my_chip)
                    def _():
                        pair_buf[chip] = (acc_ref[rows(d), :] + land_buf[chip].astype(F32)).astype(BF16)
                        to_chip(chip, chip ^ my_chip).start()
            for chip in range(N_CHIPS):
                hand_over(chip).wait_send()

        @pl.when(i >= n)
        def _():
            x_hat, r = normed_x()
            dx_ref[...], dg1 = _grad_x_tile(dq_ref[...], dg_ref[...], x_hat, r, g1_ref[...], w_ref, dh_ref[...])
            dg1_ref[0:1, :] += dg1

        @pl.when(i == n_steps - 1)
        def _():
            for rel in range(1, n_far + 1):
                to_chip(0, rel).wait()

    both = lambda w_: pl.BlockSpec((tm, w_), lambda i: (i % n, 0))
    second = pl.BlockSpec((tm, D_MODEL), lambda i: (jnp.maximum(i - n, 0), 0))
    whole = lambda dtype: jax.ShapeDtypeStruct(shard, dtype)
    sems = lambda k: pltpu.SemaphoreType.DMA((k,))
    res = pl.pallas_call(
        body, name="in_proj_bwd", grid=(n_steps,),
        in_specs=[both(QKV_W), both(GATES_W), both(D_MODEL), second, _full((1, D_MODEL)), _resident((IN_COLS, D_MODEL)),
                  HBM_SPEC],
        out_specs=[second, _full(shard), _full(shard), HBM_SPEC, _full((SMALL_ROWS, D_MODEL)), HBM_SPEC],
        out_shape=[jax.ShapeDtypeStruct((n_cover * tm, D_MODEL), F32), whole(F32), whole(BF16),
                   jax.ShapeDtypeStruct((n_far,) + shard, BF16), jax.ShapeDtypeStruct((SMALL_ROWS, D_MODEL), F32),
                   jax.ShapeDtypeStruct(out_sums.shape, out_sums.dtype)],
        scratch_shapes=[pltpu.VMEM((IN_COLS, D_MODEL), F32), pltpu.VMEM((N_CHIPS,) + shard, BF16),
                        pltpu.VMEM((N_CHIPS,) + shard, BF16), pltpu.VMEM((N_CHIPS,) + shard, BF16),
                        sems(N_CHIPS), sems(N_CHIPS), sems(n_far), sems(n_far), sems(n_far), sems(n_far)],
        compiler_params=_params("arbitrary", barrier_id=7),
    )(dqkv, dgates, x, dh, g1, w_in, out_sums)
    return res[0], (res[1], res[2], res[3]), res[4], res[5]


def _grad_x_rest(dqkv, dgates, x, dh, g1, w_in, tm, head, dg1_rows):
    t = x.shape[0]
    first = head.shape[0] // tm
    n_rest = t // tm - first
    if n_rest == 0:
        return head, dg1_rows
    assert first <= n_rest

    def body(dq_ref, dg_ref, x_ref, dh_ref, g1_ref, w_ref, head_ref, rows_ref, gx_ref, dg1_ref, stage, sems):
        j = pl.program_id(0)

        def tile_out(step, kind):
            row0 = (step + first) * tm if kind == 0 else step * tm
            slot = 2 * kind + step % 2
            return pltpu.make_async_copy(stage.at[slot], gx_ref.at[pl.ds(pl.multiple_of(row0, tm), tm), :], sems.at[slot])

        @pl.when(j == 0)
        def _():
            dg1_ref[...] = rows_ref[...]

        @pl.when(j >= 2)
        def _():
            tile_out(j - 2, 0).wait()

        @pl.when((j >= 2) & (j - 2 < first))
        def _():
            tile_out(j - 2, 1).wait()

        @pl.when(j < first)
        def _():
            stage[2 + j % 2] = head_ref[...]
            tile_out(j, 1).start()

        xv = x_ref[...]
        r = _inv_rms(xv)
        dx, dg1 = _grad_x_tile(dq_ref[...], dg_ref[...], xv * r, r, g1_ref[...], w_ref, dh_ref[...])
        stage[j % 2] = dx
        dg1_ref[0:1, :] += dg1
        tile_out(j, 0).start()

        @pl.when(j == n_rest - 1)
        def _():
            for back in range(min(2, n_rest)):
                tile_out(j - back, 0).wait()

                @pl.when(j - back < first)
                def _():
                    tile_out(j - back, 1).wait()

    tile = lambda w_: pl.BlockSpec((tm, w_), lambda j: (j + first, 0))
    head_tile = pl.BlockSpec((tm, D_MODEL), lambda j: (jnp.minimum(j, first - 1), 0))
    return pl.pallas_call(
        body, name="grad_x_rest", grid=(n_rest,),
        in_specs=[tile(QKV_W), tile(GATES_W), tile(D_MODEL), tile(D_MODEL), _full((1, D_MODEL)),
                  _resident((IN_COLS, D_MODEL)), head_tile, _full((SMALL_ROWS, D_MODEL))],
        out_specs=[HBM_SPEC, _full((SMALL_ROWS, D_MODEL))],
        out_shape=[jax.ShapeDtypeStruct((t, D_MODEL), F32), jax.ShapeDtypeStruct((SMALL_ROWS, D_MODEL), F32)],
        scratch_shapes=[pltpu.VMEM((4, tm, D_MODEL), F32), pltpu.SemaphoreType.DMA((4,))],
        compiler_params=_params("arbitrary"),
    )(dqkv, dgates, x, dh, g1, w_in, head, dg1_rows)


def _all_gather(shards, name):
    n = len(shards)

    def body(*refs):
        _enter_with(_sibling_and_chips(*_mesh_pos()))
        start, finish = _gather_steps(refs[:n], refs[n:2 * n], *refs[2 * n:])
        start()
        finish()

    return pl.pallas_call(
        body, name=name,
        in_specs=[HBM_SPEC] * n, out_specs=[HBM_SPEC] * n,
        out_shape=[jax.ShapeDtypeStruct((N_DEV,) + s.shape, s.dtype) for s in shards],
        scratch_shapes=[pltpu.SemaphoreType.DMA((7 * n,)), pltpu.SemaphoreType.DMA((7 * n,)),
                        pltpu.SemaphoreType.DMA((n,))],
        compiler_params=_params(barrier_id=8),
    )(*shards)


def _adam_math(w, g, m, v):
    m = ADAM_B1 * m + (1.0 - ADAM_B1) * g
    v = ADAM_B2 * v + (1.0 - ADAM_B2) * (g * g)
    m_hat = m / (1.0 - ADAM_B1 ** ADAM_STEP)
    v_hat = v / (1.0 - ADAM_B2 ** ADAM_STEP)
    delta = -ADAM_LR * (m_hat / (jnp.sqrt(v_hat) + ADAM_EPS) + ADAM_WD * w)
    return delta, m, v


def _adamw_reduced(w, m, v, own, from_sibling, from_chips, tr):
    rows, cols = w.shape

    def body(w_ref, m_ref, v_ref, own_ref, sib_ref, far_ref, g_ref, d_ref, nm_ref, nv_ref):
        g = own_ref[...] + sib_ref[...].astype(F32)
        for k in range(len(CHIP_FLIPS)):
            g = g + far_ref[k].astype(F32)
        g_ref[...] = g
        d_ref[...], nm_ref[...], nv_ref[...] = _adam_math(w_ref[...], g, m_ref[...], v_ref[...])

    tile = pl.BlockSpec((tr, cols), lambda i: (i, 0))
    out = jax.ShapeDtypeStruct((rows, cols), F32)
    return pl.pallas_call(
        body, name="adamw_reduced", grid=(rows // tr,),
        in_specs=[tile] * 5 + [pl.BlockSpec((len(CHIP_FLIPS), tr, cols), lambda i: (0, i, 0))],
        out_specs=[tile] * 4, out_shape=[out] * 4,
        compiler_params=_params("parallel"),
    )(w, m, v, own, from_sibling, from_chips)


SMALL_PARAMS = ("pre_mix_norm", "post_mix_norm", "pre_mlp_norm", "post_mlp_norm", "attn_group_norm", "conv_group_norm",
                "conv_w", "attn_sinks")


SMALL_WIDTHS = (D_MODEL, CONV_W, ATTN_W, 128, D_MODEL)


def _small_tail(gathered, dev, weights, first_moments, second_moments):
    n = len(SMALL_PARAMS)
    conv_shard = CONV_W // N_DEV

    def body(dev_ref, sums_ref, *refs):
        w_refs, m_refs, v_refs = refs[:n], refs[n:2 * n], refs[2 * n:3 * n]
        loss_ref, outs = refs[3 * n], refs[3 * n + 1:]
        total = sums_ref[0]
        for d in range(1, N_DEV):
            total = total + sums_ref[d]
        starts = [sum(SMALL_WIDTHS[:i]) for i in range(len(SMALL_WIDTHS))]
        mid, conv, gain, sink, inp = (total[:, a:a + w_] for a, w_ in zip(starts, SMALL_WIDTHS))
        loss_ref[...] = (0.5 / D_MODEL) * jnp.sum(mid[ROW_LOSS:ROW_LOSS + 1, :], axis=1, keepdims=True)
        conv_rows = conv[ROW_CW0:ROW_CW0 + 3, :]
        conv_g = jnp.zeros((3, conv_shard), F32)
        for d in range(N_DEV):
            conv_g = conv_g + jnp.where(dev_ref[0] == d, conv_rows[:, conv_shard * d:conv_shard * (d + 1)], 0.0)
        grads = [inp[0:1, :], mid[ROW_G2:ROW_G2 + 1, :], mid[ROW_G3:ROW_G3 + 1, :], mid[ROW_G4:ROW_G4 + 1, :],
                 gain[0:1, :], conv[ROW_GCONV:ROW_GCONV + 1, :], conv_g, sink[0:1, :N_HEADS]]
        for i, g in enumerate(grads):
            parts = [(..., g)] if len(w_refs[i].shape) == 2 else [(r, g[r:r + 1, :]) for r in range(g.shape[0])]
            for at, g_at in parts:
                delta, new_m, new_v = _adam_math(w_refs[i][at], g_at, m_refs[i][at], v_refs[i][at])
                outs[i][at], outs[n + i][at], outs[2 * n + i][at], outs[3 * n + i][at] = g_at, delta, new_m, new_v

    params = list(weights) + list(first_moments) + list(second_moments)
    shapes = [jax.ShapeDtypeStruct(w.shape, F32) for w in weights]
    res = pl.pallas_call(
        body, name="small_tail", grid=(1,),
        in_specs=[pl.BlockSpec(memory_space=pltpu.SMEM), _full(gathered.shape)] + [_full(p.shape) for p in params],
        out_specs=[_full((1, 1))] + [_full(sh.shape) for sh in shapes] * 4,
        out_shape=[jax.ShapeDtypeStruct((1, 1), F32)] + shapes * 4,
    )(dev, gathered, *params)
    return res[0], [res[1 + k * n:1 + (k + 1) * n] for k in range(4)]


TOKEN_TILE = 512
MID_TILE = 256
MID_CHUNK = 1024
MID_CHUNKS = D_FF // MID_CHUNK
ADAM_ROWS = 512


def _local_grads(x, target, g1, w_in_shard, conv_shard, sinks, g_attn, g_conv, g2, g3, g4, shards, order):
    t = x.shape[0]
    tm = min(TOKEN_TILE, t)
    rope = _rope_tables(t)
    qkv, gates, mconv, w_in, conv_w, gathered = _in_proj_fwd(x, g1, w_in_shard, conv_shard, g_conv, rope, tm, shards,
                                                             (False, True, False))
    attn, mattn, (w_out, w_up, w_down) = _attn_fwd(qkv, sinks, g_attn, shards, gathered)
    act, dup, hn2t, dmo, dmix, dh, dmixed, small_mid = _mid(
        mattn, mconv, x, target, g2, g3, g4, w_out.reshape(D_MODEL, D_MODEL),
        w_up, w_down.reshape(D_FF, D_MODEL), min(MID_TILE, t))
    up_own, up_sib, up_sums = _dw_pair_sums((hn2t, dup), order, "up", "dw_up", 2)
    down_own, down_sib, down_sums, up_far = _dw_pair_sums((act, dmo), order, "down", "dw_down", 3, ride=up_sums)
    out_own, out_sib, out_sums = _dw_pair_sums((mattn, mconv, dmix), order, "out", "dw_out", 4)
    dgates, small_conv = _conv_bwd(dmixed, gates, g_conv, conv_w, tm)
    dqkv, dsink, dg_attn, down_far = _attn_bwd(qkv, dmixed, attn, g_attn, sinks, rope, down_sums)
    grad_x_head, dw_in, small_in, out_far = _in_proj_bwd(dqkv, dgates, x, dh, g1, w_in, tm, out_sums)
    grad_x, small_in = _grad_x_rest(dqkv, dgates, x, dh, g1, w_in, tm, grad_x_head, small_in)
    dw_out, dw_up, dw_down = (out_own, out_sib, out_far), (up_own, up_sib, up_far), (down_own, down_sib, down_far)
    return grad_x, dw_in, dw_out, dw_up, dw_down, (small_mid, small_conv, dg_attn, dsink, small_in)


def kernel(x, pre_mix_norm, w_in, conv_w, attn_sinks, attn_group_norm, conv_group_norm, w_out, post_mix_norm, pre_mlp_norm, w_up, w_down, post_mlp_norm, loss_target, m_pre_mix_norm, m_w_in, m_conv_w, m_attn_sinks, m_attn_group_norm, m_conv_group_norm, m_w_out, m_post_mix_norm, m_pre_mlp_norm, m_w_up, m_w_down, m_post_mlp_norm, v_pre_mix_norm, v_w_in, v_conv_w, v_attn_sinks, v_attn_group_norm, v_conv_group_norm, v_w_out, v_post_mix_norm, v_pre_mlp_norm, v_w_up, v_w_down, v_post_mlp_norm):
    xi, yi, ci = _mesh_pos()
    chip = 2 * xi + yi
    dev = 2 * chip + ci

    order = _block_order(dev)

    shards = [w_out[0].astype(BF16), w_up[0].astype(BF16), w_down[0].astype(BF16)]

    turned = lambda a: jnp.swapaxes(a, 1, 2)
    grad_x, dw_in, dw_out, dw_up, dw_down, smalls = _local_grads(
        x[0], loss_target[0], pre_mix_norm, turned(w_in)[0].astype(BF16), conv_w[0], attn_sinks, attn_group_norm, conv_group_norm,
        post_mix_norm, pre_mlp_norm, post_mlp_norm, shards, order)

    big = {}
    for name, w, m, v, (own, sib, far) in zip(
            ("w_in", "w_out", "w_up", "w_down"), (turned(w_in), w_out, w_up, w_down),
            (turned(m_w_in), m_w_out, m_w_up, m_w_down), (turned(v_w_in), v_w_out, v_w_up, v_w_down),
            (dw_in, dw_out, dw_up, dw_down)):
        big[name] = [a[None] for a in _adamw_reduced(w[0], m[0], v[0], own, sib, far, min(ADAM_ROWS, w.shape[1]))]
    big["w_in"] = [turned(a) for a in big["w_in"]]

    flat = lambda a: a.reshape(-1, a.shape[-1]) if a.ndim < 3 else a.reshape(a.shape[1], 1, a.shape[2])
    loss, small = _small_tail(
        _all_gather([jnp.concatenate(smalls, axis=1)], "gather_small")[0], dev.reshape(1).astype(jnp.int32),
        [flat(a) for a in (pre_mix_norm, post_mix_norm, pre_mlp_norm, post_mlp_norm, attn_group_norm, conv_group_norm,
                           conv_w, attn_sinks)],
        [flat(a) for a in (m_pre_mix_norm, m_post_mix_norm, m_pre_mlp_norm, m_post_mlp_norm, m_attn_group_norm,
                           m_conv_group_norm, m_conv_w, m_attn_sinks)],
        [flat(a) for a in (v_pre_mix_norm, v_post_mix_norm, v_pre_mlp_norm, v_post_mlp_norm, v_attn_group_norm,
                           v_conv_group_norm, v_conv_w, v_attn_sinks)])

    order = ("pre_mix_norm", "w_in", "conv_w", "attn_sinks", "attn_group_norm", "conv_group_norm", "w_out",
             "post_mix_norm", "pre_mlp_norm", "w_up", "w_down", "post_mlp_norm")
    shape_of = {"conv_w": conv_w.shape}
    outs = []
    for k in range(4):
        by_name = dict(zip(SMALL_PARAMS, small[k]))
        outs += [big[nm][k] if nm in big else by_name[nm].reshape(shape_of.get(nm, by_name[nm].shape)) for nm in order]
    loss = loss.reshape(())
    return (loss, grad_x[None], *outs)
```

```python
import jax
import jax.numpy as jnp
import numpy as np
from jax import lax
from jax.experimental import pallas as pl
from jax.experimental.pallas import tpu as pltpu

F32 = jnp.float32
BF16 = jnp.bfloat16

D_MODEL = 1024
HEAD_DIM = 64
ATTN_W = 512
CONV_W = 512
N_HEADS = 8
N_KV = 2
GROUP = 4
KV_W = 128
QKV_W = ATTN_W + 2 * KV_W
GATES_W = 3 * CONV_W
IN_COLS = QKV_W + GATES_W
D_FF = 4096
FF_CHUNK = 512
BLOCK = 128
ROT_HALF = 8
ROPE_THETA = 500000.0
NORM_EPS = 1e-6
NEG_INF = -1e30
ATTN_SCALE = 0.125
N_DEV = 8
N_CHIPS = 4
IN_SHARD = IN_COLS // N_DEV

ADAM_LR = 0.001
ADAM_B1 = 0.9
ADAM_B2 = 0.999
ADAM_EPS = 1e-08
ADAM_WD = 0.01
ADAM_STEP = 10

V7X_VMEM_BYTES = 64 * 1024 * 1024
VMEM_LIMIT = V7X_VMEM_BYTES - 2 * 1024 * 1024

MESH = pl.DeviceIdType.MESH
HBM_SPEC = pl.BlockSpec(memory_space=pltpu.HBM)


def _params(*sem, barrier_id=None):
    return pltpu.CompilerParams(dimension_semantics=sem or None, vmem_limit_bytes=VMEM_LIMIT, collective_id=barrier_id)


def _mm(a, b):
    return jnp.dot(a, b, preferred_element_type=F32)


def _mm_nt(a, b):
    return lax.dot_general(a, b, (((1,), (1,)), ((), ())), preferred_element_type=F32)


def _mm_tn(a, b):
    return lax.dot_general(a, b, (((0,), (0,)), ((), ())), preferred_element_type=F32)


def _inv_rms(x):
    return lax.rsqrt(jnp.mean(x * x, axis=-1, keepdims=True) + NORM_EPS)


def _rms_bwd(xhat, r, gain, dy):
    gy = dy * gain
    return r * (gy - xhat * jnp.mean(gy * xhat, axis=-1, keepdims=True)), dy * xhat


def _colsum(a):
    return jnp.sum(a, axis=0, keepdims=True)


def _full(shape):
    zeros = (0,) * len(shape)
    return pl.BlockSpec(shape, lambda *_: zeros)


def _resident(shape):
    zeros = (0,) * len(shape)
    return pl.BlockSpec(shape, lambda *_: zeros, pipeline_mode=pl.Buffered(1))


def _rope_tables(t):
    pos = np.arange(t, dtype=np.float32)
    inv_freq = (ROPE_THETA ** (-np.arange(0, 2 * ROT_HALF, 2, dtype=np.float64) / (2 * ROT_HALF))).astype(np.float32)
    ang = (pos[:, None] * inv_freq[None, :]).astype(np.float64)
    cos, sin = np.cos(ang).astype(np.float32), np.sin(ang).astype(np.float32)
    zeros8 = np.zeros((t, ROT_HALF), np.float32)
    rest = np.zeros((t, HEAD_DIM - 2 * ROT_HALF), np.float32)
    c_head = np.concatenate([cos, cos, rest + 1.0], axis=1)
    s1_head = np.concatenate([zeros8, sin, rest], axis=1)
    s2_head = np.concatenate([-sin, zeros8, rest], axis=1)
    two = lambda a: jnp.asarray(np.concatenate([a, a], axis=1))
    return two(c_head), two(s1_head), two(s2_head)


def _rope(v, c, s1, s2):
    return v * c + pltpu.roll(v, ROT_HALF, 1) * s1 + pltpu.roll(v, 128 - ROT_HALF, 1) * s2


def _rope_transpose(dv, c, s1, s2):
    return dv * c + pltpu.roll(dv * s1, 128 - ROT_HALF, 1) + pltpu.roll(dv * s2, ROT_HALF, 1)


def _shift_rows_down(u, prev, k):
    row = lax.broadcasted_iota(jnp.int32, u.shape, 0)
    out = pltpu.roll(u, k, 0)
    for r in range(k):
        out = jnp.where(row == r, prev[8 - k + r:8 - k + r + 1, :], out)
    return out


def _shift_rows_up(u, nxt, k):
    n = u.shape[0]
    row = lax.broadcasted_iota(jnp.int32, u.shape, 0)
    out = pltpu.roll(u, n - k, 0)
    for r in range(k):
        out = jnp.where(row == n - k + r, nxt[r:r + 1, :], out)
    return out


def _conv3(u, u1, u2, w):
    return (w[0:1, :] * u2 + w[1:2, :] * u1) + w[2:3, :] * u


def _mesh_pos():
    return lax.axis_index("x"), lax.axis_index("y"), lax.axis_index("c")


def _slot(ref, pos):
    dev = 4 * pos[0] + 2 * pos[1] + pos[2]
    if len(ref.shape) == 2:
        width = ref.shape[1] // N_DEV
        return ref.at[:, pl.ds(pl.multiple_of(dev * width, width), width)]
    return ref.at[dev]


def _gathered_shape(shard, by_cols):
    if by_cols:
        return jax.ShapeDtypeStruct((shard.shape[0], N_DEV * shard.shape[1]), shard.dtype)
    return jax.ShapeDtypeStruct((N_DEV,) + shard.shape, shard.dtype)


def _enter_with(peers):
    barrier = pltpu.get_barrier_semaphore()
    for peer in peers:
        pl.semaphore_signal(barrier, inc=1, device_id=peer, device_id_type=MESH)
    pl.semaphore_wait(barrier, len(peers))


def _sibling_and_chips(x, y, c):
    return [(x, y, 1 - c), (1 - x, y, c), (x, 1 - y, c), (1 - x, 1 - y, c)]


def _push(src, dst, sems, k, to):
    send_sems, recv_sems = sems
    return pltpu.make_async_remote_copy(src_ref=src, dst_ref=dst, send_sem=send_sems.at[k], recv_sem=recv_sems.at[k],
                                        device_id=to, device_id_type=MESH)


def _gather_steps(shards, outs, send_sems, recv_sems, local_sems, relay=False):
    n = len(shards)
    x, y, c = _mesh_pos()
    me, sibling = (x, y, c), (x, y, 1 - c)
    chips = [(1 - x, y), (x, 1 - y), (1 - x, 1 - y)]
    via, to_chip = _relay_route(x, y, c)

    def copy(i, k, block, to, src=None):
        dst = _slot(outs[i], block)
        return _push(dst if src is None else src, dst, (send_sems, recv_sems), 7 * i + k, to)

    mine = [pltpu.make_async_copy(shards[i], _slot(outs[i], me), local_sems.at[i]) for i in range(n)]
    first = []
    for i in range(n):
        first.append(copy(i, 0, me, sibling, src=shards[i]))
        first += [copy(i, 1 + j, me, (*chip, c), src=shards[i]) for j, chip in enumerate(chips[:2 if relay else 3])]

    def start():
        for cp in mine + first:
            cp.start()

    def finish():
        passed = []
        for j, chip in enumerate(chips):
            for i in range(n):
                copy(i, 1 + j, (*chip, c), me).wait_recv()
                cp = copy(i, 4 + j, (*chip, c), sibling)
                cp.start()
                passed.append(cp)
            if relay and j == 1:
                for i in range(n):
                    cp = copy(i, 3, (*via, c), (*to_chip, c))
                    cp.start()
                    passed.append(cp)
        for i in range(n):
            copy(i, 0, sibling, me).wait_recv()
            for j, chip in enumerate(chips):
                copy(i, 4 + j, (*chip, 1 - c), me).wait_recv()
        for cp in first + passed:
            cp.wait_send()
        for cp in mine:
            cp.wait()

    return start, finish


def _gather_near(first, last, shards, outs, sems, local_sems):
    x, y, c = _mesh_pos()
    me, peers = (x, y, c), [(x, y, 1 - c), (1 - x, y, c), (x, 1 - y, c)]
    n = len(shards)
    local = [pltpu.make_async_copy(shards[i], _slot(outs[i], me), local_sems.at[i]) for i in range(n)]
    sends = [_push(shards[i], _slot(outs[i], me), sems, 3 * i + k, peers[k]) for i in range(n) for k in range(3)]
    arrivals = [_push(shards[i], _slot(outs[i], peers[k]), sems, 3 * i + k, peers[k]) for i in range(n) for k in range(3)]

    def start():
        for cp in local + sends:
            cp.start()

    if first is not None:
        pl.when(first)(start)

    @pl.when(last)
    def _():
        for cp in sends:
            cp.wait_send()
        for cp in arrivals:
            cp.wait_recv()
        for cp in local:
            cp.wait()

    return start


def _relay_route(x, y, c):
    south = c == 0
    via = (jnp.where(south, 1 - x, x), jnp.where(south, y, 1 - y))
    to = (jnp.where(south, x, 1 - x), jnp.where(south, 1 - y, y))
    return via, to


def _gather_far(first, middle, last, shards, ins, outs, sems):
    x, y, c = _mesh_pos()
    sibling = (x, y, 1 - c)
    chips = [(1 - x, y), (x, 1 - y), (1 - x, 1 - y)]
    via, to = _relay_route(x, y, c)
    n = len(shards)
    diag_send = [_push(_slot(ins[i], (*via, c)), _slot(outs[i], (*via, c)), sems, 4 * i, (*to, c)) for i in range(n)]
    diag_arrival = [_push(shards[i], _slot(outs[i], (*chips[2], c)), sems, 4 * i, (*to, c)) for i in range(n)]
    passed = [[_push(_slot(ins[i], (*chips[j], c)), _slot(outs[i], (*chips[j], c)), sems, 4 * i + 1 + j, sibling)
               for i in range(n)] for j in range(3)]
    from_sibling = [_push(shards[i], _slot(outs[i], (*chips[j], 1 - c)), sems, 4 * i + 1 + j, sibling)
                    for i in range(n) for j in range(3)]

    @pl.when(first)
    def _():
        for cp in diag_send + passed[0] + passed[1]:
            cp.start()

    @pl.when(middle)
    def _():
        for cp in diag_arrival:
            cp.wait_recv()
        for cp in passed[2]:
            cp.start()

    @pl.when(last)
    def _():
        for cp in from_sibling:
            cp.wait_recv()
        for cp in diag_send + passed[0] + passed[1] + passed[2]:
            cp.wait_send()


def _in_proj_fwd(x, g1, w_in, conv_w, g_conv, rope, tm, shards, by_cols):
    t = x.shape[0]
    rc, rs1, rs2 = rope
    n = len(shards)
    n_tiles = t // tm

    def body(*refs):
        x_ref, g1_ref, w_ref, cw_ref, gc_ref, c_ref, s1_ref, s2_ref = refs[:8]
        shard_refs = refs[8:8 + n]
        qkv_ref, gates_ref, mconv_ref, w_full_ref, cw_full_ref = refs[8 + n:13 + n]
        gathered = refs[13 + n:13 + 2 * n]
        carry_ref, w_land, cw_land, hn_ref = refs[13 + 2 * n:17 + 2 * n]
        now_sems = refs[17 + 2 * n:20 + 2 * n]
        step = pl.program_id(0)
        start_later_weights = _gather_near(None, step == 2 * n_tiles - 1, shard_refs, gathered,
                                           refs[20 + 2 * n:22 + 2 * n], refs[22 + 2 * n]) if n else None
        start_w_in, finish_w_in = _gather_steps([w_ref, cw_ref], [w_land, cw_land], *now_sems, relay=True)

        @pl.when(step == 0)
        def _():
            carry_ref[...] = jnp.zeros_like(carry_ref)
            _enter_with(_sibling_and_chips(*_mesh_pos())[:3])
            start_w_in()
            if start_later_weights is not None:
                start_later_weights()

        @pl.when(step < n_tiles)
        def _():
            xv = x_ref[...]
            hn_ref[step] = ((xv * _inv_rms(xv)) * g1_ref[...]).astype(BF16)

        @pl.when(step == n_tiles)
        def _():
            finish_w_in()
            conv_shard = CONV_W // N_DEV
            for d in range(N_DEV):
                w_full_ref[IN_SHARD * d:IN_SHARD * (d + 1), :] = w_land[d]
                cw_full_ref[:, conv_shard * d:conv_shard * (d + 1)] = cw_land[d]

        @pl.when(step >= n_tiles)
        def _():
            proj = _mm_nt(hn_ref[step - n_tiles], w_full_ref[...])
            c, s1, s2 = c_ref[...], s1_ref[...], s2_ref[...]
            for ci in range((ATTN_W + KV_W) // 128):
                sl = slice(128 * ci, 128 * (ci + 1))
                qkv_ref[:, sl] = _rope(proj[:, sl], c, s1, s2).astype(BF16)
            qkv_ref[:, ATTN_W + KV_W:QKV_W] = proj[:, ATTN_W + KV_W:QKV_W].astype(BF16)
            gates = proj[:, QKV_W:]
            gates_ref[...] = gates
            gb, gcc, xin = gates[:, :CONV_W], gates[:, CONV_W:2 * CONV_W], gates[:, 2 * CONV_W:]
            u = gcc * xin
            prev = carry_ref[...]
            conv = gb * _conv3(u, _shift_rows_down(u, prev, 1), _shift_rows_down(u, prev, 2), cw_full_ref[...])
            carry_ref[...] = u[tm - 8:tm, :]
            mconv_ref[...] = ((conv * _inv_rms(conv)) * gc_ref[...]).astype(BF16)

    first_pass = pl.BlockSpec((tm, D_MODEL), lambda i: (jnp.minimum(i, n_tiles - 1), 0))
    tile = lambda w_: pl.BlockSpec((tm, w_), lambda i: (jnp.maximum(i - n_tiles, 0), 0))
    sems = lambda k: pltpu.SemaphoreType.DMA((k,))
    res = pl.pallas_call(
        body, name="in_proj_fwd", grid=(2 * n_tiles,),
        in_specs=[first_pass, _full((1, D_MODEL)), HBM_SPEC, HBM_SPEC, _full((1, CONV_W)), tile(128), tile(128),
                  tile(128)] + [HBM_SPEC] * n,
        out_specs=[tile(QKV_W), tile(GATES_W), tile(CONV_W), _full((IN_COLS, D_MODEL)), _full((3, CONV_W))]
        + [HBM_SPEC] * n,
        out_shape=[jax.ShapeDtypeStruct((t, QKV_W), BF16), jax.ShapeDtypeStruct((t, GATES_W), F32),
                   jax.ShapeDtypeStruct((t, CONV_W), BF16), jax.ShapeDtypeStruct((IN_COLS, D_MODEL), BF16),
                   jax.ShapeDtypeStruct((3, CONV_W), F32)]
        + [_gathered_shape(s, cols) for s, cols in zip(shards, by_cols)],
        scratch_shapes=[pltpu.VMEM((8, CONV_W), F32), pltpu.VMEM((N_DEV,) + w_in.shape, BF16),
                        pltpu.VMEM((N_DEV,) + conv_w.shape, F32), pltpu.VMEM((n_tiles, tm, D_MODEL), BF16),
                        sems(14), sems(14), sems(2)]
        + ([sems(3 * n), sems(3 * n), sems(n)] if n else []),
        compiler_params=_params("arbitrary", barrier_id=0),
    )(x, g1, w_in, conv_w, g_conv, rc, rs1, rs2, *shards)
    return res[0], res[1], res[2], res[3], res[4], list(res[5:])


GROUP_COLS = GROUP * BLOCK
ATTN_STEP_BLOCKS = 4


def _attn_masks(has_prev):
    key = lax.broadcasted_iota(jnp.int32, (2 * BLOCK, GROUP_COLS), 0)
    query = lax.broadcasted_iota(jnp.int32, (2 * BLOCK, GROUP_COLS), 1) & (BLOCK - 1)
    band = (key > query) & (key <= query + BLOCK)
    return [band & ((key >= BLOCK) | has_prev)] + [band] * (ATTN_STEP_BLOCKS - 1)


def _heads_side_by_side(at, g, b):
    heads = [at[HEAD_DIM * (GROUP * g + hh):HEAD_DIM * (GROUP * g + hh + 1), BLOCK * b:BLOCK * (b + 1)] for hh in range(GROUP)]
    return jnp.concatenate(heads, axis=1)


def _to_token_rows(parts):
    rows = [jnp.concatenate([parts[b][g][:, BLOCK * hh:BLOCK * (hh + 1)] for b in range(ATTN_STEP_BLOCKS)], axis=1)
            for g in range(N_KV) for hh in range(GROUP)]
    return jnp.concatenate(rows, axis=0).T


def _group_sinks(sink_ref, g):
    head = lax.broadcasted_iota(jnp.int32, (1, GROUP_COLS), 1) // BLOCK
    out = jnp.full((1, GROUP_COLS), sink_ref[0, GROUP * g], F32)
    for hh in range(1, GROUP):
        out = jnp.where(head == hh, sink_ref[0, GROUP * g + hh], out)
    return out


def _attn_probs(qt, kk, sink, valid):
    s = jnp.where(valid, _mm(kk, qt), NEG_INF)
    m = jnp.maximum(jnp.max(s, axis=0, keepdims=True), sink)
    p = jnp.exp(s - m)
    psink = jnp.exp(sink - m)
    inv_l = 1.0 / (jnp.sum(p, axis=0, keepdims=True) + psink)
    return p * inv_l, psink * inv_l


ATTN_STEP = ATTN_STEP_BLOCKS * BLOCK
ATTN_KEYS = ATTN_STEP + BLOCK


def _qkv_specs(order):
    prev = lambda i: jnp.maximum(ATTN_STEP_BLOCKS * order(i) - 1, 0)
    kcol, vcol = ATTN_W // KV_W, ATTN_W // KV_W + 1
    return [pl.BlockSpec((ATTN_STEP, ATTN_W), lambda i: (order(i), 0)),
            pl.BlockSpec((BLOCK, KV_W), lambda i: (prev(i), kcol)), pl.BlockSpec((ATTN_STEP, KV_W), lambda i: (order(i), kcol)),
            pl.BlockSpec((BLOCK, KV_W), lambda i: (prev(i), vcol)), pl.BlockSpec((ATTN_STEP, KV_W), lambda i: (order(i), vcol))]


def _attn_fwd(qkv, sinks, g_attn, shards, gathered):
    t = qkv.shape[0]
    n = len(shards)

    def body(*refs):
        sink_ref, q_ref, kp_ref, kc_ref, vp_ref, vc_ref, ga_ref = refs[:7]
        attn_ref, mattn_ref = refs[7 + 2 * n:9 + 2 * n]
        step = pl.program_id(0)
        if n:
            @pl.when(step == 0)
            def _():
                x, y, c = _mesh_pos()
                _enter_with([(x, y, 1 - c), (*_relay_route(x, y, c)[1], c)])

            n_steps = t // ATTN_STEP
            _gather_far(step == 0, step == n_steps // 2, step == n_steps - 1, refs[7:7 + n], refs[7 + n:7 + 2 * n],
                        refs[9 + 2 * n:9 + 3 * n], refs[9 + 3 * n:11 + 3 * n])
        qt = (q_ref[...] * ATTN_SCALE).T
        keys = jnp.concatenate([kp_ref[...], kc_ref[...]], axis=0)
        vals = jnp.concatenate([vp_ref[...], vc_ref[...]], axis=0)
        sink = [_group_sinks(sink_ref, g) for g in range(N_KV)]
        masks = _attn_masks(step > 0)
        parts = []
        for b in range(ATTN_STEP_BLOCKS):
            window = slice(BLOCK * b, BLOCK * (b + 2))
            valid = masks[b]
            parts.append([])
            for g in range(N_KV):
                gs = slice(HEAD_DIM * g, HEAD_DIM * (g + 1))
                probs, _ = _attn_probs(_heads_side_by_side(qt, g, b), keys[window, gs], sink[g], valid)
                parts[b].append(_mm_tn(vals[window, gs], probs.astype(BF16)))
        attn = _to_token_rows(parts)
        attn_ref[...] = attn
        mattn_ref[...] = ((attn * _inv_rms(attn)) * ga_ref[...]).astype(BF16)

    blk = pl.BlockSpec((ATTN_STEP, ATTN_W), lambda j: (j, 0))
    res = pl.pallas_call(
        body, name="attn_fwd", grid=(t // ATTN_STEP,),
        in_specs=[pl.BlockSpec(memory_space=pltpu.SMEM)] + _qkv_specs(lambda j: j) + [_full((1, ATTN_W))]
        + [HBM_SPEC] * (2 * n),
        out_specs=[blk, blk] + [HBM_SPEC] * n,
        out_shape=[jax.ShapeDtypeStruct((t, ATTN_W), F32), jax.ShapeDtypeStruct((t, ATTN_W), BF16)]
        + [jax.ShapeDtypeStruct(g.shape, g.dtype) for g in gathered],
        input_output_aliases={7 + n + i: 2 + i for i in range(n)},
        scratch_shapes=[pltpu.SemaphoreType.DMA((4 * n,)), pltpu.SemaphoreType.DMA((4 * n,))] if n else [],
        compiler_params=_params("arbitrary", barrier_id=1 if n else None),
    )(sinks, qkv, qkv, qkv, qkv, qkv, g_attn, *shards, *gathered)
    return res[0], res[1], list(res[2:])


SMALL_ROWS = 8
ROW_LOSS, ROW_G2, ROW_G3, ROW_G4 = 0, 1, 2, 3


def _mid(mattn, mconv, x, target, g2, g3, g4, w_out, w_up, w_down, tm):
    t = x.shape[0]

    def body(ma_ref, mc_ref, x_ref, t_ref, g2_ref, g3_ref, g4_ref, wo_ref, wu_ref, wd_ref,
             act_ref, dup_ref, hn2t_ref, dmo_ref, dmix_ref, dh_ref, dmixed_ref, small_ref, up_ref):
        @pl.when(pl.program_id(0) == 0)
        def _():
            small_ref[...] = jnp.zeros_like(small_ref)

        g2, g3, g4 = g2_ref[...], g3_ref[...], g4_ref[...]
        mix_out = _mm(ma_ref[...], wo_ref[0:ATTN_W, :]) + _mm(mc_ref[...], wo_ref[ATTN_W:, :])
        r2 = _inv_rms(mix_out)
        mo_hat = mix_out * r2
        h = x_ref[...] + mo_hat * g2
        r3 = _inv_rms(h)
        h_hat = h * r3
        hn2 = (h_hat * g3).astype(BF16)
        hn2t_ref[...] = hn2.T
        for j in range(MID_CHUNKS):
            cols_j = slice(MID_CHUNK * j, MID_CHUNK * (j + 1))
            up = jnp.maximum(_mm(hn2, wu_ref[:, cols_j]), 0.0)
            up_ref[:, cols_j] = up.astype(BF16)
            act_ref[:, cols_j] = (up * up).astype(BF16)
        mlp = _mm(act_ref[...], wd_ref[...])
        r4 = _inv_rms(mlp)
        ml_hat = mlp * r4
        err = (h + ml_hat * g4) - t_ref[...]
        d_out = err * (1.0 / D_MODEL)
        d_mlp, dg4 = _rms_bwd(ml_hat, r4, g4, d_out)
        dmo = d_mlp.astype(BF16)
        dmo_ref[...] = dmo
        for j in range(MID_CHUNKS):
            cols_j = slice(MID_CHUNK * j, MID_CHUNK * (j + 1))
            dact = _mm_nt(dmo, wd_ref[cols_j, :])
            dup_ref[:, cols_j] = (dact * (2.0 * up_ref[:, cols_j].astype(F32))).astype(BF16)
        dhn2 = _mm_nt(dup_ref[...], wu_ref[...])
        dh_norm, dg3 = _rms_bwd(h_hat, r3, g3, dhn2)
        dh = d_out + dh_norm
        dh_ref[...] = dh
        d_mix, dg2 = _rms_bwd(mo_hat, r2, g2, dh)
        dmix = d_mix.astype(BF16)
        dmix_ref[...] = dmix
        dmixed_ref[...] = _mm_nt(dmix, wo_ref[...])
        small_ref[ROW_LOSS:ROW_LOSS + 1, :] += _colsum(err * err)
        small_ref[ROW_G2:ROW_G2 + 1, :] += _colsum(dg2)
        small_ref[ROW_G3:ROW_G3 + 1, :] += _colsum(dg3)
        small_ref[ROW_G4:ROW_G4 + 1, :] += _colsum(dg4)

    tile = lambda n: pl.BlockSpec((tm, n), lambda i: (i, 0))
    cols = lambda n: pl.BlockSpec((n, tm), lambda i: (0, i))
    gain = _full((1, D_MODEL))
    return pl.pallas_call(
        body, name="mid_fwd_bwd", grid=(t // tm,),
        in_specs=[tile(ATTN_W), tile(CONV_W), tile(D_MODEL), tile(D_MODEL), gain, gain, gain,
                  _resident((D_MODEL, D_MODEL)), _resident((D_MODEL, D_FF)), _resident((D_FF, D_MODEL))],
        out_specs=[tile(D_FF), tile(D_FF), cols(D_MODEL), tile(D_MODEL), tile(D_MODEL), tile(D_MODEL), tile(D_MODEL),
                   _full((SMALL_ROWS, D_MODEL))],
        out_shape=[jax.ShapeDtypeStruct((t, D_FF), BF16), jax.ShapeDtypeStruct((t, D_FF), BF16),
                   jax.ShapeDtypeStruct((D_MODEL, t), BF16), jax.ShapeDtypeStruct((t, D_MODEL), BF16),
                   jax.ShapeDtypeStruct((t, D_MODEL), BF16), jax.ShapeDtypeStruct((t, D_MODEL), F32),
                   jax.ShapeDtypeStruct((t, D_MODEL), F32), jax.ShapeDtypeStruct((SMALL_ROWS, D_MODEL), F32)],
        scratch_shapes=[pltpu.VMEM((tm, D_FF), BF16)],
        compiler_params=_params("arbitrary"),
    )(mattn, mconv, x, target, g2, g3, g4, w_out, w_up, w_down)


CHIP_FLIPS = ((1, 1), (1, 0), (0, 1))


def _block_order(dev):
    chip_masks = [4 * fx + 2 * fy for fx, fy in CHIP_FLIPS]
    masks = [m + 1 for m in chip_masks] + [1] + chip_masks + [0]
    return jnp.bitwise_xor(dev, jnp.asarray(masks, jnp.int32)).astype(jnp.int32)


def _other_chips(x, y, c):
    return [(1 - x if fx else x, 1 - y if fy else y, c) for fx, fy in CHIP_FLIPS]


def _dw_pair_sums(operands, order, which, name, barrier_id, ride=None):
    t = operands[-1].shape[0]
    n_far = len(CHIP_FLIPS)
    n_in = len(operands)
    n_ride = 0 if ride is None else 1
    out_chunk = D_MODEL // N_DEV
    if which == "up":
        rows, cols = D_MODEL, FF_CHUNK
        in_specs = [_resident((D_MODEL, t)), pl.BlockSpec((t, FF_CHUNK), lambda s, order_ref: (0, order_ref[s]))]
    elif which == "down":
        rows, cols = FF_CHUNK, D_MODEL
        in_specs = [pl.BlockSpec((t, FF_CHUNK), lambda s, order_ref: (0, order_ref[s])), _resident((t, D_MODEL))]
    else:
        rows, cols = out_chunk, D_MODEL
        half = pl.BlockSpec((t, out_chunk), lambda s, order_ref: (0, order_ref[s] % (N_DEV // 2)))
        in_specs = [half, half, _resident((t, D_MODEL))]

    def body(order_ref, *refs):
        own_ref, from_sib_ref, pair_ref = refs[n_in + n_ride:n_in + n_ride + 3]
        send_buf, land_buf, send_sems, recv_sems = refs[n_in + 2 * n_ride + 3:n_in + 2 * n_ride + 7]
        s_now = pl.program_id(0)
        x, y, c = _mesh_pos()
        sibling = (x, y, 1 - c)
        sems = (send_sems, recv_sems)

        @pl.when(s_now == 0)
        def _():
            _enter_with([sibling] + (_other_chips(x, y, c) if n_ride else []))

        if n_ride:
            _chip_exchange_beside(s_now == 0, s_now == N_DEV - 1, [refs[n_in]], [refs[n_in + 3 + n_ride]],
                                  refs[n_in + 2 * n_ride + 7:], enter=False)

        def hand_over(k):
            dst = land_buf.at[k] if k < n_far else from_sib_ref
            return _push(send_buf.at[k], dst, sems, k, sibling)

        if which == "out":
            ma_ref, mc_ref, b_ref = refs[:n_in]
            block = lax.cond(order_ref[s_now] < N_DEV // 2, lambda: _mm_tn(ma_ref[...], b_ref[...]),
                             lambda: _mm_tn(mc_ref[...], b_ref[...]))
        elif which == "down":
            block = _mm_tn(refs[0][...], refs[1][...])
        else:
            block = _mm(refs[0][...], refs[1][...])
        for k in range(n_far + 1):
            @pl.when(s_now == k)
            def _():
                send_buf[k] = block.astype(BF16)
                hand_over(k).start()

        for k in range(n_far):
            @pl.when(s_now == n_far + 1 + k)
            def _():
                hand_over(k).wait_recv()
                pair_ref[...] = (block + land_buf[k].astype(F32)).astype(BF16)

        @pl.when(s_now == N_DEV - 1)
        def _():
            own_ref[...] = block
            for k in range(n_far + 1):
                hand_over(k).wait_send()
            hand_over(n_far).wait_recv()

    rides = [] if ride is None else [ride]
    sems = lambda k: pltpu.SemaphoreType.DMA((k,))
    return pl.pallas_call(
        body, name=name,
        grid_spec=pltpu.PrefetchScalarGridSpec(
            num_scalar_prefetch=1, grid=(N_DEV,), in_specs=in_specs + [HBM_SPEC] * n_ride,
            out_specs=[pl.BlockSpec((rows, cols), lambda s, order_ref: (0, 0)), HBM_SPEC,
                       pl.BlockSpec((None, rows, cols), lambda s, order_ref: (jnp.clip(s - n_far - 1, 0, n_far - 1), 0, 0))]
            + [HBM_SPEC] * n_ride,
            scratch_shapes=[pltpu.VMEM((n_far + 1, rows, cols), BF16), pltpu.VMEM((n_far, rows, cols), BF16),
                            sems(n_far + 1), sems(n_far + 1)] + [sems(n_far), sems(n_far)] * n_ride),
        out_shape=[jax.ShapeDtypeStruct((rows, cols), F32), jax.ShapeDtypeStruct((rows, cols), BF16),
                   jax.ShapeDtypeStruct((n_far, rows, cols), BF16)]
        + [jax.ShapeDtypeStruct(r.shape, r.dtype) for r in rides],
        compiler_params=_params("arbitrary", barrier_id=barrier_id),
    )(order, *operands, *rides)


def _chip_exchange_beside(first, last, sums, outs, sems, enter=True):
    chips = _other_chips(*_mesh_pos())
    copies = [_push(sums[i].at[k], outs[i].at[k], sems, len(chips) * i + k, chip)
              for i in range(len(sums)) for k, chip in enumerate(chips)]

    @pl.when(first)
    def _():
        if enter:
            _enter_with(chips)
        for cp in copies:
            cp.start()

    @pl.when(last)
    def _():
        for cp in copies:
            cp.wait()


ROW_GCONV, ROW_CW0 = 1, 2


def _conv_bwd(dmixed, gates, g_conv, conv_w, tm):
    t = gates.shape[0]
    n = t // tm
    rev = lambda i: n - 1 - i

    def body(dm_ref, gates_ref, gprev_ref, gc_ref, cw_ref, dgates_ref, small_ref, carry_ref):
        i = pl.program_id(0)

        @pl.when(i == 0)
        def _():
            small_ref[...] = jnp.zeros_like(small_ref)
            carry_ref[...] = jnp.zeros_like(carry_ref)

        gates = gates_ref[...]
        gb, gcc, xin = gates[:, :CONV_W], gates[:, CONV_W:2 * CONV_W], gates[:, 2 * CONV_W:]
        u = gcc * xin
        gp = gprev_ref[...]
        uprev = jnp.where(rev(i) == 0, 0.0, gp[:, CONV_W:2 * CONV_W] * gp[:, 2 * CONV_W:])
        u1, u2 = _shift_rows_down(u, uprev, 1), _shift_rows_down(u, uprev, 2)
        w = cw_ref[...]
        c = _conv3(u, u1, u2, w)
        conv = gb * c
        rcv = _inv_rms(conv)
        c_hat = conv * rcv
        dconv, dgc = _rms_bwd(c_hat, rcv, gc_ref[...], dm_ref[...])
        dc = dconv * gb
        nxt = carry_ref[...]
        du = (w[2:3, :] * dc + w[1:2, :] * _shift_rows_up(dc, nxt, 1)) + w[0:1, :] * _shift_rows_up(dc, nxt, 2)
        carry_ref[...] = dc[0:8, :]
        dgates_ref[:, :CONV_W] = (dconv * c).astype(BF16)
        dgates_ref[:, CONV_W:2 * CONV_W] = (du * xin).astype(BF16)
        dgates_ref[:, 2 * CONV_W:] = (du * gcc).astype(BF16)
        small_ref[ROW_GCONV:ROW_GCONV + 1, :] += _colsum(dgc)
        small_ref[ROW_CW0:ROW_CW0 + 1, :] += _colsum(dc * u2)
        small_ref[ROW_CW0 + 1:ROW_CW0 + 2, :] += _colsum(dc * u1)
        small_ref[ROW_CW0 + 2:ROW_CW0 + 3, :] += _colsum(dc * u)

    tile = lambda w_: pl.BlockSpec((tm, w_), lambda i: (rev(i), 0))
    prev8 = pl.BlockSpec((8, GATES_W), lambda i: (jnp.maximum(rev(i) * (tm // 8) - 1, 0), 0))
    conv_half = pl.BlockSpec((tm, CONV_W), lambda i: (rev(i), ATTN_W // CONV_W))
    return pl.pallas_call(
        body, name="conv_bwd", grid=(n,),
        in_specs=[conv_half, tile(GATES_W), prev8, _full((1, CONV_W)), _full((3, CONV_W))],
        out_specs=[tile(GATES_W), _full((SMALL_ROWS, CONV_W))],
        out_shape=[jax.ShapeDtypeStruct((t, GATES_W), BF16), jax.ShapeDtypeStruct((SMALL_ROWS, CONV_W), F32)],
        scratch_shapes=[pltpu.VMEM((8, CONV_W), F32)],
        compiler_params=_params("arbitrary"),
    )(dmixed, gates, gates, g_conv, conv_w)


def _attn_bwd(qkv, dmixed, attn, g_attn, sinks, rope, sums):
    t = qkv.shape[0]
    n_steps = t // ATTN_STEP
    rev = lambda i: n_steps - 1 - i
    rc, rs1, rs2 = rope

    def body(sink_ref, q_ref, kp_ref, kc_ref, vp_ref, vc_ref, dm_ref, attn_ref, ga_ref, c_ref, s1_ref, s2_ref, sums_ref,
             dqkv_ref, dsink_ref, dgain_ref, arrived_ref, ck_ref, cv_ref, kacc_ref, vacc_ref, send_sems, recv_sems):
        i = pl.program_id(0)
        _chip_exchange_beside(i == 0, i == n_steps - 1, [sums_ref], [arrived_ref], (send_sems, recv_sems))

        @pl.when(i == 0)
        def _():
            dsink_ref[...] = jnp.zeros_like(dsink_ref)
            dgain_ref[...] = jnp.zeros_like(dgain_ref)
            ck_ref[...] = jnp.zeros_like(ck_ref)
            cv_ref[...] = jnp.zeros_like(cv_ref)

        kacc_ref[...] = jnp.zeros_like(kacc_ref)
        vacc_ref[...] = jnp.zeros_like(vacc_ref)
        a = attn_ref[...]
        ra = _inv_rms(a)
        dattn, dgain = _rms_bwd(a * ra, ra, ga_ref[...], dm_ref[...])
        dgain_ref[0:1, :] += _colsum(dgain)
        qt = (q_ref[...] * ATTN_SCALE).T
        dot = dattn.astype(BF16).T
        keys = jnp.concatenate([kp_ref[...], kc_ref[...]], axis=0)
        vals = jnp.concatenate([vp_ref[...], vc_ref[...]], axis=0)
        sink = [_group_sinks(sink_ref, g) for g in range(N_KV)]
        c, s1, s2 = c_ref[...], s1_ref[...], s2_ref[...]
        lane = lax.broadcasted_iota(jnp.int32, (1, 128), 1)
        dsink = jnp.zeros((1, 128), F32)
        masks = _attn_masks(rev(i) > 0)
        dq_parts = []
        for b in range(ATTN_STEP_BLOCKS):
            window = slice(BLOCK * b, BLOCK * (b + 2))
            valid = masks[b]
            dq_parts.append([])
            dk_parts, dv_parts = [], []
            for g in range(N_KV):
                gs = slice(HEAD_DIM * g, HEAD_DIM * (g + 1))
                kk, vv = keys[window, gs], vals[window, gs]
                qtg, dotg = _heads_side_by_side(qt, g, b), _heads_side_by_side(dot, g, b)
                probs, psink = _attn_probs(qtg, kk, sink[g], valid)
                dp = _mm(vv, dotg)
                delta = jnp.sum(probs * dp, axis=0, keepdims=True)
                ds = (probs * (dp - delta)).astype(BF16)
                sink_terms = psink * delta
                for hh in range(GROUP):
                    head_sum = jnp.sum(sink_terms[:, BLOCK * hh:BLOCK * (hh + 1)])
                    dsink = dsink + jnp.where(lane == GROUP * g + hh, -head_sum, 0.0)
                dq_parts[b].append(_mm_tn(kk * ATTN_SCALE, ds))
                dk_parts.append(_mm_nt(ds, qtg))
                dv_parts.append(_mm_nt(probs.astype(BF16), dotg))
            kacc_ref[window, :] += jnp.concatenate(dk_parts, axis=1)
            vacc_ref[window, :] += jnp.concatenate(dv_parts, axis=1)
        dq = _to_token_rows(dq_parts)
        for ci in range(ATTN_W // 128):
            sl = slice(128 * ci, 128 * (ci + 1))
            dqkv_ref[:, sl] = _rope_transpose(dq[:, sl], c, s1, s2).astype(BF16)
        kacc_ref[ATTN_STEP:, :] += ck_ref[...]
        vacc_ref[ATTN_STEP:, :] += cv_ref[...]
        ck_ref[...] = kacc_ref[:BLOCK, :]
        cv_ref[...] = vacc_ref[:BLOCK, :]
        dqkv_ref[:, ATTN_W:ATTN_W + KV_W] = _rope_transpose(kacc_ref[BLOCK:, :], c, s1, s2).astype(BF16)
        dqkv_ref[:, ATTN_W + KV_W:] = vacc_ref[BLOCK:, :].astype(BF16)
        dsink_ref[0:1, :] += dsink

    blk = lambda w_: pl.BlockSpec((ATTN_STEP, w_), lambda i: (rev(i), 0))
    return pl.pallas_call(
        body, name="attn_bwd", grid=(n_steps,),
        in_specs=[pl.BlockSpec(memory_space=pltpu.SMEM)] + _qkv_specs(rev)
        + [blk(ATTN_W), blk(ATTN_W), _full((1, ATTN_W)), blk(128), blk(128), blk(128), HBM_SPEC],
        out_specs=[blk(QKV_W), _full((8, 128)), _full((SMALL_ROWS, ATTN_W)), HBM_SPEC],
        out_shape=[jax.ShapeDtypeStruct((t, QKV_W), BF16), jax.ShapeDtypeStruct((8, 128), F32),
                   jax.ShapeDtypeStruct((SMALL_ROWS, ATTN_W), F32), jax.ShapeDtypeStruct(sums.shape, sums.dtype)],
        scratch_shapes=[pltpu.VMEM((BLOCK, KV_W), F32), pltpu.VMEM((BLOCK, KV_W), F32),
                        pltpu.VMEM((ATTN_KEYS, KV_W), F32), pltpu.VMEM((ATTN_KEYS, KV_W), F32),
                        pltpu.SemaphoreType.DMA((len(CHIP_FLIPS),)), pltpu.SemaphoreType.DMA((len(CHIP_FLIPS),))],
        compiler_params=_params("arbitrary", barrier_id=6),
    )(sinks, qkv, qkv, qkv, qkv, qkv, dmixed, attn, g_attn, rc, rs1, rs2, sums)


def _grad_x_tile(dq, dg, x_hat, r, g1, w_ref, dh):
    dhn = _mm(dq, w_ref[:QKV_W, :]) + _mm(dg, w_ref[QKV_W:, :])
    dx, dg1 = _rms_bwd(x_hat, r, g1, dhn)
    return dh + dx, _colsum(dg1)


def _in_proj_bwd(dqkv, dgates, x, dh, g1, w_in, tm, out_sums):
    t = x.shape[0]
    n = t // tm
    n_cover = max(n // 2, 1)
    n_steps = n + n_cover
    n_far = len(CHIP_FLIPS)
    shard = (IN_SHARD, D_MODEL)

    def body(dq_ref, dg_ref, x_ref, dh_ref, g1_ref, w_ref, osums_ref,
             dx_ref, own_ref, sib_ref, far_ref, dg1_ref, oarrived_ref,
             acc_ref, send_buf, land_buf, pair_buf, d2d_send, d2d_recv, ici_send, ici_recv, o_send, o_recv):
        i = pl.program_id(0)
        x_pos, y_pos, c = _mesh_pos()
        my_chip = 2 * x_pos + y_pos
        sibling = (x_pos, y_pos, 1 - c)
        @pl.when(i == 0)
        def _():
            _enter_with(_sibling_and_chips(x_pos, y_pos, c))

        _chip_exchange_beside(i == 0, i == n_steps - 1, [osums_ref], [oarrived_ref], (o_send, o_recv), enter=False)

        def rows(d):
            return slice(IN_SHARD * d, IN_SHARD * (d + 1))

        def hand_over(chip):
            return _push(send_buf.at[chip], land_buf.at[chip], (d2d_send, d2d_recv), chip, sibling)

        def to_chip(chip, rel):
            return pltpu.make_async_remote_copy(
                src_ref=pair_buf.at[chip], dst_ref=far_ref.at[rel - 1], send_sem=ici_send.at[rel - 1],
                recv_sem=ici_recv.at[rel - 1], device_id=(chip // 2, chip % 2, c), device_id_type=MESH)

        @pl.when(i == 0)
        def _():
            acc_ref[...] = jnp.zeros_like(acc_ref)
            dg1_ref[...] = jnp.zeros_like(dg1_ref)

        def normed_x():
            xv = x_ref[...]
            r = _inv_rms(xv)
            return xv * r, r

        @pl.when(i < n)
        def _():
            hn = (normed_x()[0] * g1_ref[...]).astype(BF16)
            acc_ref[:QKV_W, :] += _mm_tn(dq_ref[...], hn)
            acc_ref[QKV_W:, :] += _mm_tn(dg_ref[...], hn)

        @pl.when(i == n - 1)
        def _():
            for d in range(N_DEV):
                @pl.when(d % 2 != c)
                def _():
                    send_buf[d // 2] = acc_ref[rows(d), :].astype(BF16)
                    hand_over(d // 2).start()
            for d in range(N_DEV):
                chip = d // 2

                @pl.when(d % 2 == c)
                def _():
                    hand_over(chip).wait_recv()

                    @pl.when(chip == my_chip)
                    def _():
                        own_ref[...] = acc_ref[rows(d), :]
                        sib_ref[...] = land_buf[chip]

                    @pl.when(chip != my_chip)
                    def _():
                        pair_buf[chip] = (acc_ref[rows(d), :] + land_buf[chip].astype(F32)).astype(BF16)
                        to_chip(chip, chip ^ my_chip).start()
            for chip in range(N_CHIPS):
                hand_over(chip).wait_send()

        @pl.when(i >= n)
        def _():
            x_hat, r = normed_x()
            dx_ref[...], dg1 = _grad_x_tile(dq_ref[...], dg_ref[...], x_hat, r, g1_ref[...], w_ref, dh_ref[...])
            dg1_ref[0:1, :] += dg1

        @pl.when(i == n_steps - 1)
        def _():
            for rel in range(1, n_far + 1):
                to_chip(0, rel).wait()

    both = lambda w_: pl.BlockSpec((tm, w_), lambda i: (i % n, 0))
    second = pl.BlockSpec((tm, D_MODEL), lambda i: (jnp.maximum(i - n, 0), 0))
    whole = lambda dtype: jax.ShapeDtypeStruct(shard, dtype)
    sems = lambda k: pltpu.SemaphoreType.DMA((k,))
    res = pl.pallas_call(
        body, name="in_proj_bwd", grid=(n_steps,),
        in_specs=[both(QKV_W), both(GATES_W), both(D_MODEL), second, _full((1, D_MODEL)), _resident((IN_COLS, D_MODEL)),
                  HBM_SPEC],
        out_specs=[second, _full(shard), _full(shard), HBM_SPEC, _full((SMALL_ROWS, D_MODEL)), HBM_SPEC],
        out_shape=[jax.ShapeDtypeStruct((n_cover * tm, D_MODEL), F32), whole(F32), whole(BF16),
                   jax.ShapeDtypeStruct((n_far,) + shard, BF16), jax.ShapeDtypeStruct((SMALL_ROWS, D_MODEL), F32),
                   jax.ShapeDtypeStruct(out_sums.shape, out_sums.dtype)],
        scratch_shapes=[pltpu.VMEM((IN_COLS, D_MODEL), F32), pltpu.VMEM((N_CHIPS,) + shard, BF16),
                        pltpu.VMEM((N_CHIPS,) + shard, BF16), pltpu.VMEM((N_CHIPS,) + shard, BF16),
                        sems(N_CHIPS), sems(N_CHIPS), sems(n_far), sems(n_far), sems(n_far), sems(n_far)],
        compiler_params=_params("arbitrary", barrier_id=7),
    )(dqkv, dgates, x, dh, g1, w_in, out_sums)
    return res[0], (res[1], res[2], res[3]), res[4], res[5]


def _grad_x_rest(dqkv, dgates, x, dh, g1, w_in, tm, head, dg1_rows):
    t = x.shape[0]
    first = head.shape[0] // tm
    n_rest = t // tm - first
    if n_rest == 0:
        return head, dg1_rows
    assert first <= n_rest

    def body(dq_ref, dg_ref, x_ref, dh_ref, g1_ref, w_ref, head_ref, rows_ref, gx_ref, dg1_ref, stage, sems):
        j = pl.program_id(0)

        def tile_out(step, kind):
            row0 = (step + first) * tm if kind == 0 else step * tm
            slot = 2 * kind + step % 2
            return pltpu.make_async_copy(stage.at[slot], gx_ref.at[pl.ds(pl.multiple_of(row0, tm), tm), :], sems.at[slot])

        @pl.when(j == 0)
        def _():
            dg1_ref[...] = rows_ref[...]

        @pl.when(j >= 2)
        def _():
            tile_out(j - 2, 0).wait()

        @pl.when((j >= 2) & (j - 2 < first))
        def _():
            tile_out(j - 2, 1).wait()

        @pl.when(j < first)
        def _():
            stage[2 + j % 2] = head_ref[...]
            tile_out(j, 1).start()

        xv = x_ref[...]
        r = _inv_rms(xv)
        dx, dg1 = _grad_x_tile(dq_ref[...], dg_ref[...], xv * r, r, g1_ref[...], w_ref, dh_ref[...])
        stage[j % 2] = dx
        dg1_ref[0:1, :] += dg1
        tile_out(j, 0).start()

        @pl.when(j == n_rest - 1)
        def _():
            for back in range(min(2, n_rest)):
                tile_out(j - back, 0).wait()

                @pl.when(j - back < first)
                def _():
                    tile_out(j - back, 1).wait()

    tile = lambda w_: pl.BlockSpec((tm, w_), lambda j: (j + first, 0))
    head_tile = pl.BlockSpec((tm, D_MODEL), lambda j: (jnp.minimum(j, first - 1), 0))
    return pl.pallas_call(
        body, name="grad_x_rest", grid=(n_rest,),
        in_specs=[tile(QKV_W), tile(GATES_W), tile(D_MODEL), tile(D_MODEL), _full((1, D_MODEL)),
                  _resident((IN_COLS, D_MODEL)), head_tile, _full((SMALL_ROWS, D_MODEL))],
        out_specs=[HBM_SPEC, _full((SMALL_ROWS, D_MODEL))],
        out_shape=[jax.ShapeDtypeStruct((t, D_MODEL), F32), jax.ShapeDtypeStruct((SMALL_ROWS, D_MODEL), F32)],
        scratch_shapes=[pltpu.VMEM((4, tm, D_MODEL), F32), pltpu.SemaphoreType.DMA((4,))],
        compiler_params=_params("arbitrary"),
    )(dqkv, dgates, x, dh, g1, w_in, head, dg1_rows)


def _all_gather(shards, name):
    n = len(shards)

    def body(*refs):
        _enter_with(_sibling_and_chips(*_mesh_pos()))
        start, finish = _gather_steps(refs[:n], refs[n:2 * n], *refs[2 * n:])
        start()
        finish()

    return pl.pallas_call(
        body, name=name,
        in_specs=[HBM_SPEC] * n, out_specs=[HBM_SPEC] * n,
        out_shape=[jax.ShapeDtypeStruct((N_DEV,) + s.shape, s.dtype) for s in shards],
        scratch_shapes=[pltpu.SemaphoreType.DMA((7 * n,)), pltpu.SemaphoreType.DMA((7 * n,)),
                        pltpu.SemaphoreType.DMA((n,))],
        compiler_params=_params(barrier_id=8),
    )(*shards)


def _adam_math(w, g, m, v):
    m = ADAM_B1 * m + (1.0 - ADAM_B1) * g
    v = ADAM_B2 * v + (1.0 - ADAM_B2) * (g * g)
    m_hat = m / (1.0 - ADAM_B1 ** ADAM_STEP)
    v_hat = v / (1.0 - ADAM_B2 ** ADAM_STEP)
    delta = -ADAM_LR * (m_hat / (jnp.sqrt(v_hat) + ADAM_EPS) + ADAM_WD * w)
    return delta, m, v


def _adamw_reduced(w, m, v, own, from_sibling, from_chips, tr):
    rows, cols = w.shape

    def body(w_ref, m_ref, v_ref, own_ref, sib_ref, far_ref, g_ref, d_ref, nm_ref, nv_ref):
        g = own_ref[...] + sib_ref[...].astype(F32)
        for k in range(len(CHIP_FLIPS)):
            g = g + far_ref[k].astype(F32)
        g_ref[...] = g
        d_ref[...], nm_ref[...], nv_ref[...] = _adam_math(w_ref[...], g, m_ref[...], v_ref[...])

    tile = pl.BlockSpec((tr, cols), lambda i: (i, 0))
    out = jax.ShapeDtypeStruct((rows, cols), F32)
    return pl.pallas_call(
        body, name="adamw_reduced", grid=(rows // tr,),
        in_specs=[tile] * 5 + [pl.BlockSpec((len(CHIP_FLIPS), tr, cols), lambda i: (0, i, 0))],
        out_specs=[tile] * 4, out_shape=[out] * 4,
        compiler_params=_params("parallel"),
    )(w, m, v, own, from_sibling, from_chips)


SMALL_PARAMS = ("pre_mix_norm", "post_mix_norm", "pre_mlp_norm", "post_mlp_norm", "attn_group_norm", "conv_group_norm",
                "conv_w", "attn_sinks")


SMALL_WIDTHS = (D_MODEL, CONV_W, ATTN_W, 128, D_MODEL)


def _small_tail(gathered, dev, weights, first_moments, second_moments):
    n = len(SMALL_PARAMS)
    conv_shard = CONV_W // N_DEV

    def body(dev_ref, sums_ref, *refs):
        w_refs, m_refs, v_refs = refs[:n], refs[n:2 * n], refs[2 * n:3 * n]
        loss_ref, outs = refs[3 * n], refs[3 * n + 1:]
        total = sums_ref[0]
        for d in range(1, N_DEV):
            total = total + sums_ref[d]
        starts = [sum(SMALL_WIDTHS[:i]) for i in range(len(SMALL_WIDTHS))]
        mid, conv, gain, sink, inp = (total[:, a:a + w_] for a, w_ in zip(starts, SMALL_WIDTHS))
        loss_ref[...] = (0.5 / D_MODEL) * jnp.sum(mid[ROW_LOSS:ROW_LOSS + 1, :], axis=1, keepdims=True)
        conv_rows = conv[ROW_CW0:ROW_CW0 + 3, :]
        conv_g = jnp.zeros((3, conv_shard), F32)
        for d in range(N_DEV):
            conv_g = conv_g + jnp.where(dev_ref[0] == d, conv_rows[:, conv_shard * d:conv_shard * (d + 1)], 0.0)
        grads = [inp[0:1, :], mid[ROW_G2:ROW_G2 + 1, :], mid[ROW_G3:ROW_G3 + 1, :], mid[ROW_G4:ROW_G4 + 1, :],
                 gain[0:1, :], conv[ROW_GCONV:ROW_GCONV + 1, :], conv_g, sink[0:1, :N_HEADS]]
        for i, g in enumerate(grads):
            parts = [(..., g)] if len(w_refs[i].shape) == 2 else [(r, g[r:r + 1, :]) for r in range(g.shape[0])]
            for at, g_at in parts:
                delta, new_m, new_v = _adam_math(w_refs[i][at], g_at, m_refs[i][at], v_refs[i][at])
                outs[i][at], outs[n + i][at], outs[2 * n + i][at], outs[3 * n + i][at] = g_at, delta, new_m, new_v

    params = list(weights) + list(first_moments) + list(second_moments)
    shapes = [jax.ShapeDtypeStruct(w.shape, F32) for w in weights]
    res = pl.pallas_call(
        body, name="small_tail", grid=(1,),
        in_specs=[pl.BlockSpec(memory_space=pltpu.SMEM), _full(gathered.shape)] + [_full(p.shape) for p in params],
        out_specs=[_full((1, 1))] + [_full(sh.shape) for sh in shapes] * 4,
        out_shape=[jax.ShapeDtypeStruct((1, 1), F32)] + shapes * 4,
    )(dev, gathered, *params)
    return res[0], [res[1 + k * n:1 + (k + 1) * n] for k in range(4)]


TOKEN_TILE = 512
MID_TILE = 256
MID_CHUNK = 1024
MID_CHUNKS = D_FF // MID_CHUNK
ADAM_ROWS = 512


def _local_grads(x, target, g1, w_in_shard, conv_shard, sinks, g_attn, g_conv, g2, g3, g4, shards, order):
    t = x.shape[0]
    tm = min(TOKEN_TILE, t)
    rope = _rope_tables(t)
    qkv, gates, mconv, w_in, conv_w, gathered = _in_proj_fwd(x, g1, w_in_shard, conv_shard, g_conv, rope, tm, shards,
                                                             (False, True, False))
    attn, mattn, (w_out, w_up, w_down) = _attn_fwd(qkv, sinks, g_attn, shards, gathered)
    act, dup, hn2t, dmo, dmix, dh, dmixed, small_mid = _mid(
        mattn, mconv, x, target, g2, g3, g4, w_out.reshape(D_MODEL, D_MODEL),
        w_up, w_down.reshape(D_FF, D_MODEL), min(MID_TILE, t))
    up_own, up_sib, up_sums = _dw_pair_sums((hn2t, dup), order, "up", "dw_up", 2)
    down_own, down_sib, down_sums, up_far = _dw_pair_sums((act, dmo), order, "down", "dw_down", 3, ride=up_sums)
    out_own, out_sib, out_sums = _dw_pair_sums((mattn, mconv, dmix), order, "out", "dw_out", 4)
    dgates, small_conv = _conv_bwd(dmixed, gates, g_conv, conv_w, tm)
    dqkv, dsink, dg_attn, down_far = _attn_bwd(qkv, dmixed, attn, g_attn, sinks, rope, down_sums)
    grad_x_head, dw_in, small_in, out_far = _in_proj_bwd(dqkv, dgates, x, dh, g1, w_in, tm, out_sums)
    grad_x, small_in = _grad_x_rest(dqkv, dgates, x, dh, g1, w_in, tm, grad_x_head, small_in)
    dw_out, dw_up, dw_down = (out_own, out_sib, out_far), (up_own, up_sib, up_far), (down_own, down_sib, down_far)
    return grad_x, dw_in, dw_out, dw_up, dw_down, (small_mid, small_conv, dg_attn, dsink, small_in)


def kernel(x, pre_mix_norm, w_in, conv_w, attn_sinks, attn_group_norm, conv_group_norm, w_out, post_mix_norm, pre_mlp_norm, w_up, w_down, post_mlp_norm, loss_target, m_pre_mix_norm, m_w_in, m_conv_w, m_attn_sinks, m_attn_group_norm, m_conv_group_norm, m_w_out, m_post_mix_norm, m_pre_mlp_norm, m_w_up, m_w_down, m_post_mlp_norm, v_pre_mix_norm, v_w_in, v_conv_w, v_attn_sinks, v_attn_group_norm, v_conv_group_norm, v_w_out, v_post_mix_norm, v_pre_mlp_norm, v_w_up, v_w_down, v_post_mlp_norm):
    xi, yi, ci = _mesh_pos()
    chip = 2 * xi + yi
    dev = 2 * chip + ci

    order = _block_order(dev)

    shards = [w_out[0].astype(BF16), w_up[0].astype(BF16), w_down[0].astype(BF16)]

    turned = lambda a: jnp.swapaxes(a, 1, 2)
    grad_x, dw_in, dw_out, dw_up, dw_down, smalls = _local_grads(
        x[0], loss_target[0], pre_mix_norm, turned(w_in)[0].astype(BF16), conv_w[0], attn_sinks, attn_group_norm, conv_group_norm,
        post_mix_norm, pre_mlp_norm, post_mlp_norm, shards, order)

    big = {}
    for name, w, m, v, (own, sib, far) in zip(
            ("w_in", "w_out", "w_up", "w_down"), (turned(w_in), w_out, w_up, w_down),
            (turned(m_w_in), m_w_out, m_w_up, m_w_down), (turned(v_w_in), v_w_out, v_w_up, v_w_down),
            (dw_in, dw_out, dw_up, dw_down)):
        big[name] = [a[None] for a in _adamw_reduced(w[0], m[0], v[0], own, sib, far, min(ADAM_ROWS, w.shape[1]))]
    big["w_in"] = [turned(a) for a in big["w_in"]]

    flat = lambda a: a.reshape(-1, a.shape[-1]) if a.ndim < 3 else a.reshape(a.shape[1], 1, a.shape[2])
    loss, small = _small_tail(
        _all_gather([jnp.concatenate(smalls, axis=1)], "gather_small")[0], dev.reshape(1).astype(jnp.int32),
        [flat(a) for a in (pre_mix_norm, post_mix_norm, pre_mlp_norm, post_mlp_norm, attn_group_norm, conv_group_norm,
                           conv_w, attn_sinks)],
        [flat(a) for a in (m_pre_mix_norm, m_post_mix_norm, m_pre_mlp_norm, m_post_mlp_norm, m_attn_group_norm,
                           m_conv_group_norm, m_conv_w, m_attn_sinks)],
        [flat(a) for a in (v_pre_mix_norm, v_post_mix_norm, v_pre_mlp_norm, v_post_mlp_norm, v_attn_group_norm,
                           v_conv_group_norm, v_conv_w, v_attn_sinks)])

    order = ("pre_mix_norm", "w_in", "conv_w", "attn_sinks", "attn_group_norm", "conv_group_norm", "w_out",
             "post_mix_norm", "pre_mlp_norm", "w_up", "w_down", "post_mlp_norm")
    shape_of = {"conv_w": conv_w.shape}
    outs = []
    for k in range(4):
        by_name = dict(zip(SMALL_PARAMS, small[k]))
        outs += [big[nm][k] if nm in big else by_name[nm].reshape(shape_of.get(nm, by_name[nm].shape)) for nm in order]
    loss = loss.reshape(())
    return (loss, grad_x[None], *outs)
```

```python
import jax
import jax.numpy as jnp
import numpy as np
from jax import lax
from jax.experimental import pallas as pl
from jax.experimental.pallas import tpu as pltpu

F32 = jnp.float32
BF16 = jnp.bfloat16

D_MODEL = 1024
HEAD_DIM = 64
ATTN_W = 512
CONV_W = 512
N_HEADS = 8
N_KV = 2
GROUP = 4
KV_W = 128
QKV_W = ATTN_W + 2 * KV_W
GATES_W = 3 * CONV_W
IN_COLS = QKV_W + GATES_W
D_FF = 4096
FF_CHUNK = 512
BLOCK = 128
ROT_HALF = 8
ROPE_THETA = 500000.0
NORM_EPS = 1e-6
NEG_INF = -1e30
ATTN_SCALE = 0.125
N_DEV = 8
N_CHIPS = 4
IN_SHARD = IN_COLS // N_DEV

ADAM_LR = 0.001
ADAM_B1 = 0.9
ADAM_B2 = 0.999
ADAM_EPS = 1e-08
ADAM_WD = 0.01
ADAM_STEP = 10

V7X_VMEM_BYTES = 64 * 1024 * 1024
VMEM_LIMIT = V7X_VMEM_BYTES - 2 * 1024 * 1024

MESH = pl.DeviceIdType.MESH
HBM_SPEC = pl.BlockSpec(memory_space=pltpu.HBM)


def _params(*sem, barrier_id=None):
    return pltpu.CompilerParams(dimension_semantics=sem or None, vmem_limit_bytes=VMEM_LIMIT, collective_id=barrier_id)


def _mm(a, b):
    return jnp.dot(a, b, preferred_element_type=F32)


def _mm_nt(a, b):
    return lax.dot_general(a, b, (((1,), (1,)), ((), ())), preferred_element_type=F32)


def _mm_tn(a, b):
    return lax.dot_general(a, b, (((0,), (0,)), ((), ())), preferred_element_type=F32)


def _inv_rms(x):
    return lax.rsqrt(jnp.mean(x * x, axis=-1, keepdims=True) + NORM_EPS)


def _rms_bwd(xhat, r, gain, dy):
    gy = dy * gain
    return r * (gy - xhat * jnp.mean(gy * xhat, axis=-1, keepdims=True)), dy * xhat


def _colsum(a):
    return jnp.sum(a, axis=0, keepdims=True)


def _full(shape):
    zeros = (0,) * len(shape)
    return pl.BlockSpec(shape, lambda *_: zeros)


def _resident(shape):
    zeros = (0,) * len(shape)
    return pl.BlockSpec(shape, lambda *_: zeros, pipeline_mode=pl.Buffered(1))


def _rope_tables(t):
    pos = np.arange(t, dtype=np.float32)
    inv_freq = (ROPE_THETA ** (-np.arange(0, 2 * ROT_HALF, 2, dtype=np.float64) / (2 * ROT_HALF))).astype(np.float32)
    ang = (pos[:, None] * inv_freq[None, :]).astype(np.float64)
    cos, sin = np.cos(ang).astype(np.float32), np.sin(ang).astype(np.float32)
    zeros8 = np.zeros((t, ROT_HALF), np.float32)
    rest = np.zeros((t, HEAD_DIM - 2 * ROT_HALF), np.float32)
    c_head = np.concatenate([cos, cos, rest + 1.0], axis=1)
    s1_head = np.concatenate([zeros8, sin, rest], axis=1)
    s2_head = np.concatenate([-sin, zeros8, rest], axis=1)
    two = lambda a: jnp.asarray(np.concatenate([a, a], axis=1))
    return two(c_head), two(s1_head), two(s2_head)


def _rope(v, c, s1, s2):
    return v * c + pltpu.roll(v, ROT_HALF, 1) * s1 + pltpu.roll(v, 128 - ROT_HALF, 1) * s2


def _rope_transpose(dv, c, s1, s2):
    return dv * c + pltpu.roll(dv * s1, 128 - ROT_HALF, 1) + pltpu.roll(dv * s2, ROT_HALF, 1)


def _shift_rows_down(u, prev, k):
    row = lax.broadcasted_iota(jnp.int32, u.shape, 0)
    out = pltpu.roll(u, k, 0)
    for r in range(k):
        out = jnp.where(row == r, prev[8 - k + r:8 - k + r + 1, :], out)
    return out


def _shift_rows_up(u, nxt, k):
    n = u.shape[0]
    row = lax.broadcasted_iota(jnp.int32, u.shape, 0)
    out = pltpu.roll(u, n - k, 0)
    for r in range(k):
        out = jnp.where(row == n - k + r, nxt[r:r + 1, :], out)
    return out


def _conv3(u, u1, u2, w):
    return (w[0:1, :] * u2 + w[1:2, :] * u1) + w[2:3, :] * u


def _mesh_pos():
    return lax.axis_index("x"), lax.axis_index("y"), lax.axis_index("c")


def _slot(ref, pos):
    dev = 4 * pos[0] + 2 * pos[1] + pos[2]
    if len(ref.shape) == 2:
        width = ref.shape[1] // N_DEV
        return ref.at[:, pl.ds(pl.multiple_of(dev * width, width), width)]
    return ref.at[dev]


def _gathered_shape(shard, by_cols):
    if by_cols:
        return jax.ShapeDtypeStruct((shard.shape[0], N_DEV * shard.shape[1]), shard.dtype)
    return jax.ShapeDtypeStruct((N_DEV,) + shard.shape, shard.dtype)


def _enter_with(peers):
    barrier = pltpu.get_barrier_semaphore()
    for peer in peers:
        pl.semaphore_signal(barrier, inc=1, device_id=peer, device_id_type=MESH)
    pl.semaphore_wait(barrier, len(peers))


def _sibling_and_chips(x, y, c):
    return [(x, y, 1 - c), (1 - x, y, c), (x, 1 - y, c), (1 - x, 1 - y, c)]


def _push(src, dst, sems, k, to):
    send_sems, recv_sems = sems
    return pltpu.make_async_remote_copy(src_ref=src, dst_ref=dst, send_sem=send_sems.at[k], recv_sem=recv_sems.at[k],
                                        device_id=to, device_id_type=MESH)


def _gather_steps(shards, outs, send_sems, recv_sems, local_sems, relay=False):
    n = len(shards)
    x, y, c = _mesh_pos()
    me, sibling = (x, y, c), (x, y, 1 - c)
    chips = [(1 - x, y), (x, 1 - y), (1 - x, 1 - y)]
    via, to_chip = _relay_route(x, y, c)

    def copy(i, k, block, to, src=None):
        dst = _slot(outs[i], block)
        return _push(dst if src is None else src, dst, (send_sems, recv_sems), 7 * i + k, to)

    mine = [pltpu.make_async_copy(shards[i], _slot(outs[i], me), local_sems.at[i]) for i in range(n)]
    first = []
    for i in range(n):
        first.append(copy(i, 0, me, sibling, src=shards[i]))
        first += [copy(i, 1 + j, me, (*chip, c), src=shards[i]) for j, chip in enumerate(chips[:2 if relay else 3])]

    def start():
        for cp in mine + first:
            cp.start()

    def finish(then=None):
        passed = []
        for j, chip in enumerate(chips):
            for i in range(n):
                copy(i, 1 + j, (*chip, c), me).wait_recv()
                cp = copy(i, 4 + j, (*chip, c), sibling)
                cp.start()
                passed.append(cp)
            if j == 1:
                if relay:
                    for i in range(n):
                        cp = copy(i, 3, (*via, c), (*to_chip, c))
                        cp.start()
                        passed.append(cp)
                if then is not None:
                    then()
        for i in range(n):
            copy(i, 0, sibling, me).wait_recv()
            for j, chip in enumerate(chips):
                copy(i, 4 + j, (*chip, 1 - c), me).wait_recv()
        for cp in first + passed:
            cp.wait_send()
        for cp in mine:
            cp.wait()

    return start, finish


def _gather_near(first, last, shards, outs, sems, local_sems):
    x, y, c = _mesh_pos()
    me, peers = (x, y, c), [(x, y, 1 - c), (1 - x, y, c), (x, 1 - y, c)]
    n = len(shards)
    local = [pltpu.make_async_copy(shards[i], _slot(outs[i], me), local_sems.at[i]) for i in range(n)]
    sends = [_push(shards[i], _slot(outs[i], me), sems, 3 * i + k, peers[k]) for i in range(n) for k in range(3)]
    arrivals = [_push(shards[i], _slot(outs[i], peers[k]), sems, 3 * i + k, peers[k]) for i in range(n) for k in range(3)]

    def start():
        for cp in local + sends:
            cp.start()

    if first is not None:
        pl.when(first)(start)

    @pl.when(last)
    def _():
        for cp in sends:
            cp.wait_send()
        for cp in arrivals:
            cp.wait_recv()
        for cp in local:
            cp.wait()

    return start


def _relay_route(x, y, c):
    south = c == 0
    via = (jnp.where(south, 1 - x, x), jnp.where(south, y, 1 - y))
    to = (jnp.where(south, x, 1 - x), jnp.where(south, 1 - y, y))
    return via, to


def _gather_far(first, middle, last, shards, ins, outs, sems):
    x, y, c = _mesh_pos()
    sibling = (x, y, 1 - c)
    chips = [(1 - x, y), (x, 1 - y), (1 - x, 1 - y)]
    via, to = _relay_route(x, y, c)
    n = len(shards)
    diag_send = [_push(_slot(ins[i], (*via, c)), _slot(outs[i], (*via, c)), sems, 4 * i, (*to, c)) for i in range(n)]
    diag_arrival = [_push(shards[i], _slot(outs[i], (*chips[2], c)), sems, 4 * i, (*to, c)) for i in range(n)]
    passed = [[_push(_slot(ins[i], (*chips[j], c)), _slot(outs[i], (*chips[j], c)), sems, 4 * i + 1 + j, sibling)
               for i in range(n)] for j in range(3)]
    from_sibling = [_push(shards[i], _slot(outs[i], (*chips[j], 1 - c)), sems, 4 * i + 1 + j, sibling)
                    for i in range(n) for j in range(3)]

    @pl.when(first)
    def _():
        for cp in diag_send + passed[0] + passed[1]:
            cp.start()

    @pl.when(middle)
    def _():
        for cp in diag_arrival:
            cp.wait_recv()
        for cp in passed[2]:
            cp.start()

    @pl.when(last)
    def _():
        for cp in from_sibling:
            cp.wait_recv()
        for cp in diag_send + passed[0] + passed[1] + passed[2]:
            cp.wait_send()


def _in_proj_fwd(x, g1, w_in, conv_w, g_conv, rope, tm, shards, by_cols):
    t = x.shape[0]
    rc, rs1, rs2 = rope
    n = len(shards)
    n_tiles = t // tm

    def body(*refs):
        x_ref, g1_ref, w_ref, cw_ref, gc_ref, c_ref, s1_ref, s2_ref = refs[:8]
        shard_refs = refs[8:8 + n]
        qkv_ref, gates_ref, mconv_ref, w_full_ref, cw_full_ref = refs[8 + n:13 + n]
        gathered = refs[13 + n:13 + 2 * n]
        carry_ref, w_land, cw_land, hn_ref = refs[13 + 2 * n:17 + 2 * n]
        now_sems = refs[17 + 2 * n:20 + 2 * n]
        step = pl.program_id(0)
        start_later_weights = _gather_near(None, step == 2 * n_tiles - 1, shard_refs, gathered,
                                           refs[20 + 2 * n:22 + 2 * n], refs[22 + 2 * n]) if n else None
        start_w_in, finish_w_in = _gather_steps([w_ref, cw_ref], [w_land, cw_land], *now_sems, relay=True)

        @pl.when(step == 0)
        def _():
            carry_ref[...] = jnp.zeros_like(carry_ref)
            _enter_with(_sibling_and_chips(*_mesh_pos())[:3])
            start_w_in()

        @pl.when(step < n_tiles)
        def _():
            xv = x_ref[...]
            hn_ref[step] = ((xv * _inv_rms(xv)) * g1_ref[...]).astype(BF16)

        @pl.when(step == n_tiles)
        def _():
            finish_w_in(start_later_weights)
            conv_shard = CONV_W // N_DEV
            for d in range(N_DEV):
                w_full_ref[IN_SHARD * d:IN_SHARD * (d + 1), :] = w_land[d]
                cw_full_ref[:, conv_shard * d:conv_shard * (d + 1)] = cw_land[d]

        @pl.when(step >= n_tiles)
        def _():
            proj = _mm_nt(hn_ref[step - n_tiles], w_full_ref[...])
            c, s1, s2 = c_ref[...], s1_ref[...], s2_ref[...]
            for ci in range((ATTN_W + KV_W) // 128):
                sl = slice(128 * ci, 128 * (ci + 1))
                qkv_ref[:, sl] = _rope(proj[:, sl], c, s1, s2).astype(BF16)
            qkv_ref[:, ATTN_W + KV_W:QKV_W] = proj[:, ATTN_W + KV_W:QKV_W].astype(BF16)
            gates = proj[:, QKV_W:]
            gates_ref[...] = gates
            gb, gcc, xin = gates[:, :CONV_W], gates[:, CONV_W:2 * CONV_W], gates[:, 2 * CONV_W:]
            u = gcc * xin
            prev = carry_ref[...]
            conv = gb * _conv3(u, _shift_rows_down(u, prev, 1), _shift_rows_down(u, prev, 2), cw_full_ref[...])
            carry_ref[...] = u[tm - 8:tm, :]
            mconv_ref[...] = ((conv * _inv_rms(conv)) * gc_ref[...]).astype(BF16)

    first_pass = pl.BlockSpec((tm, D_MODEL), lambda i: (jnp.minimum(i, n_tiles - 1), 0))
    tile = lambda w_: pl.BlockSpec((tm, w_), lambda i: (jnp.maximum(i - n_tiles, 0), 0))
    sems = lambda k: pltpu.SemaphoreType.DMA((k,))
    res = pl.pallas_call(
        body, name="in_proj_fwd", grid=(2 * n_tiles,),
        in_specs=[first_pass, _full((1, D_MODEL)), HBM_SPEC, HBM_SPEC, _full((1, CONV_W)), tile(128), tile(128),
                  tile(128)] + [HBM_SPEC] * n,
        out_specs=[tile(QKV_W), tile(GATES_W), tile(CONV_W), _full((IN_COLS, D_MODEL)), _full((3, CONV_W))]
        + [HBM_SPEC] * n,
        out_shape=[jax.ShapeDtypeStruct((t, QKV_W), BF16), jax.ShapeDtypeStruct((t, GATES_W), F32),
                   jax.ShapeDtypeStruct((t, CONV_W), BF16), jax.ShapeDtypeStruct((IN_COLS, D_MODEL), BF16),
                   jax.ShapeDtypeStruct((3, CONV_W), F32)]
        + [_gathered_shape(s, cols) for s, cols in zip(shards, by_cols)],
        scratch_shapes=[pltpu.VMEM((8, CONV_W), F32), pltpu.VMEM((N_DEV,) + w_in.shape, BF16),
                        pltpu.VMEM((N_DEV,) + conv_w.shape, F32), pltpu.VMEM((n_tiles, tm, D_MODEL), BF16),
                        sems(14), sems(14), sems(2)]
        + ([sems(3 * n), sems(3 * n), sems(n)] if n else []),
        compiler_params=_params("arbitrary", barrier_id=0),
    )(x, g1, w_in, conv_w, g_conv, rc, rs1, rs2, *shards)
    return res[0], res[1], res[2], res[3], res[4], list(res[5:])


GROUP_COLS = GROUP * BLOCK
ATTN_STEP_BLOCKS = 4


def _attn_masks(has_prev):
    key = lax.broadcasted_iota(jnp.int32, (2 * BLOCK, GROUP_COLS), 0)
    query = lax.broadcasted_iota(jnp.int32, (2 * BLOCK, GROUP_COLS), 1) & (BLOCK - 1)
    band = (key > query) & (key <= query + BLOCK)
    return [band & ((key >= BLOCK) | has_prev)] + [band] * (ATTN_STEP_BLOCKS - 1)


def _heads_side_by_side(at, g, b):
    heads = [at[HEAD_DIM * (GROUP * g + hh):HEAD_DIM * (GROUP * g + hh + 1), BLOCK * b:BLOCK * (b + 1)] for hh in range(GROUP)]
    return jnp.concatenate(heads, axis=1)


def _to_token_rows(parts):
    rows = [jnp.concatenate([parts[b][g][:, BLOCK * hh:BLOCK * (hh + 1)] for b in range(ATTN_STEP_BLOCKS)], axis=1)
            for g in range(N_KV) for hh in range(GROUP)]
    return jnp.concatenate(rows, axis=0).T


def _group_sinks(sink_ref, g):
    head = lax.broadcasted_iota(jnp.int32, (1, GROUP_COLS), 1) // BLOCK
    out = jnp.full((1, GROUP_COLS), sink_ref[0, GROUP * g], F32)
    for hh in range(1, GROUP):
        out = jnp.where(head == hh, sink_ref[0, GROUP * g + hh], out)
    return out


def _attn_probs(qt, kk, sink, valid):
    s = jnp.where(valid, _mm(kk, qt), NEG_INF)
    m = jnp.maximum(jnp.max(s, axis=0, keepdims=True), sink)
    p = jnp.exp(s - m)
    psink = jnp.exp(sink - m)
    inv_l = 1.0 / (jnp.sum(p, axis=0, keepdims=True) + psink)
    return p * inv_l, psink * inv_l


ATTN_STEP = ATTN_STEP_BLOCKS * BLOCK
ATTN_KEYS = ATTN_STEP + BLOCK


def _qkv_specs(order):
    prev = lambda i: jnp.maximum(ATTN_STEP_BLOCKS * order(i) - 1, 0)
    kcol, vcol = ATTN_W // KV_W, ATTN_W // KV_W + 1
    return [pl.BlockSpec((ATTN_STEP, ATTN_W), lambda i: (order(i), 0)),
            pl.BlockSpec((BLOCK, KV_W), lambda i: (prev(i), kcol)), pl.BlockSpec((ATTN_STEP, KV_W), lambda i: (order(i), kcol)),
            pl.BlockSpec((BLOCK, KV_W), lambda i: (prev(i), vcol)), pl.BlockSpec((ATTN_STEP, KV_W), lambda i: (order(i), vcol))]


def _attn_fwd(qkv, sinks, g_attn, shards, gathered):
    t = qkv.shape[0]
    n = len(shards)

    def body(*refs):
        sink_ref, q_ref, kp_ref, kc_ref, vp_ref, vc_ref, ga_ref = refs[:7]
        attn_ref, mattn_ref = refs[7 + 2 * n:9 + 2 * n]
        step = pl.program_id(0)
        if n:
            @pl.when(step == 0)
            def _():
                x, y, c = _mesh_pos()
                _enter_with([(x, y, 1 - c), (*_relay_route(x, y, c)[1], c)])

            n_steps = t // ATTN_STEP
            _gather_far(step == 0, step == n_steps // 2, step == n_steps - 1, refs[7:7 + n], refs[7 + n:7 + 2 * n],
                        refs[9 + 2 * n:9 + 3 * n], refs[9 + 3 * n:11 + 3 * n])
        qt = (q_ref[...] * ATTN_SCALE).T
        keys = jnp.concatenate([kp_ref[...], kc_ref[...]], axis=0)
        vals = jnp.concatenate([vp_ref[...], vc_ref[...]], axis=0)
        sink = [_group_sinks(sink_ref, g) for g in range(N_KV)]
        masks = _attn_masks(step > 0)
        parts = []
        for b in range(ATTN_STEP_BLOCKS):
            window = slice(BLOCK * b, BLOCK * (b + 2))
            valid = masks[b]
            parts.append([])
            for g in range(N_KV):
                gs = slice(HEAD_DIM * g, HEAD_DIM * (g + 1))
                probs, _ = _attn_probs(_heads_side_by_side(qt, g, b), keys[window, gs], sink[g], valid)
                parts[b].append(_mm_tn(vals[window, gs], probs.astype(BF16)))
        attn = _to_token_rows(parts)
        attn_ref[...] = attn
        mattn_ref[...] = ((attn * _inv_rms(attn)) * ga_ref[...]).astype(BF16)

    blk = pl.BlockSpec((ATTN_STEP, ATTN_W), lambda j: (j, 0))
    res = pl.pallas_call(
        body, name="attn_fwd", grid=(t // ATTN_STEP,),
        in_specs=[pl.BlockSpec(memory_space=pltpu.SMEM)] + _qkv_specs(lambda j: j) + [_full((1, ATTN_W))]
        + [HBM_SPEC] * (2 * n),
        out_specs=[blk, blk] + [HBM_SPEC] * n,
        out_shape=[jax.ShapeDtypeStruct((t, ATTN_W), F32), jax.ShapeDtypeStruct((t, ATTN_W), BF16)]
        + [jax.ShapeDtypeStruct(g.shape, g.dtype) for g in gathered],
        input_output_aliases={7 + n + i: 2 + i for i in range(n)},
        scratch_shapes=[pltpu.SemaphoreType.DMA((4 * n,)), pltpu.SemaphoreType.DMA((4 * n,))] if n else [],
        compiler_params=_params("arbitrary", barrier_id=1 if n else None),
    )(sinks, qkv, qkv, qkv, qkv, qkv, g_attn, *shards, *gathered)
    return res[0], res[1], list(res[2:])


SMALL_ROWS = 8
ROW_LOSS, ROW_G2, ROW_G3, ROW_G4 = 0, 1, 2, 3


def _mid(mattn, mconv, x, target, g2, g3, g4, w_out, w_up, w_down, tm):
    t = x.shape[0]

    def body(ma_ref, mc_ref, x_ref, t_ref, g2_ref, g3_ref, g4_ref, wo_ref, wu_ref, wd_ref,
             act_ref, dup_ref, hn2t_ref, dmo_ref, dmix_ref, dh_ref, dmixed_ref, small_ref, up_ref):
        @pl.when(pl.program_id(0) == 0)
        def _():
            small_ref[...] = jnp.zeros_like(small_ref)

        g2, g3, g4 = g2_ref[...], g3_ref[...], g4_ref[...]
        mix_out = _mm(ma_ref[...], wo_ref[0:ATTN_W, :]) + _mm(mc_ref[...], wo_ref[ATTN_W:, :])
        r2 = _inv_rms(mix_out)
        mo_hat = mix_out * r2
        h = x_ref[...] + mo_hat * g2
        r3 = _inv_rms(h)
        h_hat = h * r3
        hn2 = (h_hat * g3).astype(BF16)
        hn2t_ref[...] = hn2.T
        for j in range(MID_CHUNKS):
            cols_j = slice(MID_CHUNK * j, MID_CHUNK * (j + 1))
            up = jnp.maximum(_mm(hn2, wu_ref[:, cols_j]), 0.0)
            up_ref[:, cols_j] = up.astype(BF16)
            act_ref[:, cols_j] = (up * up).astype(BF16)
        mlp = _mm(act_ref[...], wd_ref[...])
        r4 = _inv_rms(mlp)
        ml_hat = mlp * r4
        err = (h + ml_hat * g4) - t_ref[...]
        d_out = err * (1.0 / D_MODEL)
        d_mlp, dg4 = _rms_bwd(ml_hat, r4, g4, d_out)
        dmo = d_mlp.astype(BF16)
        dmo_ref[...] = dmo
        for j in range(MID_CHUNKS):
            cols_j = slice(MID_CHUNK * j, MID_CHUNK * (j + 1))
            dact = _mm_nt(dmo, wd_ref[cols_j, :])
            dup_ref[:, cols_j] = (dact * (2.0 * up_ref[:, cols_j].astype(F32))).astype(BF16)
        dhn2 = _mm_nt(dup_ref[...], wu_ref[...])
        dh_norm, dg3 = _rms_bwd(h_hat, r3, g3, dhn2)
        dh = d_out + dh_norm
        dh_ref[...] = dh
        d_mix, dg2 = _rms_bwd(mo_hat, r2, g2, dh)
        dmix = d_mix.astype(BF16)
        dmix_ref[...] = dmix
        dmixed_ref[...] = _mm_nt(dmix, wo_ref[...])
        small_ref[ROW_LOSS:ROW_LOSS + 1, :] += _colsum(err * err)
        small_ref[ROW_G2:ROW_G2 + 1, :] += _colsum(dg2)
        small_ref[ROW_G3:ROW_G3 + 1, :] += _colsum(dg3)
        small_ref[ROW_G4:ROW_G4 + 1, :] += _colsum(dg4)

    tile = lambda n: pl.BlockSpec((tm, n), lambda i: (i, 0))
    cols = lambda n: pl.BlockSpec((n, tm), lambda i: (0, i))
    gain = _full((1, D_MODEL))
    return pl.pallas_call(
        body, name="mid_fwd_bwd", grid=(t // tm,),
        in_specs=[tile(ATTN_W), tile(CONV_W), tile(D_MODEL), tile(D_MODEL), gain, gain, gain,
                  _resident((D_MODEL, D_MODEL)), _resident((D_MODEL, D_FF)), _resident((D_FF, D_MODEL))],
        out_specs=[tile(D_FF), tile(D_FF), cols(D_MODEL), tile(D_MODEL), tile(D_MODEL), tile(D_MODEL), tile(D_MODEL),
                   _full((SMALL_ROWS, D_MODEL))],
        out_shape=[jax.ShapeDtypeStruct((t, D_FF), BF16), jax.ShapeDtypeStruct((t, D_FF), BF16),
                   jax.ShapeDtypeStruct((D_MODEL, t), BF16), jax.ShapeDtypeStruct((t, D_MODEL), BF16),
                   jax.ShapeDtypeStruct((t, D_MODEL), BF16), jax.ShapeDtypeStruct((t, D_MODEL), F32),
                   jax.ShapeDtypeStruct((t, D_MODEL), F32), jax.ShapeDtypeStruct((SMALL_ROWS, D_MODEL), F32)],
        scratch_shapes=[pltpu.VMEM((tm, D_FF), BF16)],
        compiler_params=_params("arbitrary"),
    )(mattn, mconv, x, target, g2, g3, g4, w_out, w_up, w_down)


CHIP_FLIPS = ((1, 1), (1, 0), (0, 1))


def _block_order(dev):
    chip_masks = [4 * fx + 2 * fy for fx, fy in CHIP_FLIPS]
    masks = [m + 1 for m in chip_masks] + [1] + chip_masks + [0]
    return jnp.bitwise_xor(dev, jnp.asarray(masks, jnp.int32)).astype(jnp.int32)


def _other_chips(x, y, c):
    return [(1 - x if fx else x, 1 - y if fy else y, c) for fx, fy in CHIP_FLIPS]


def _dw_pair_sums(operands, order, which, name, barrier_id, ride=None):
    t = operands[-1].shape[0]
    n_far = len(CHIP_FLIPS)
    n_in = len(operands)
    n_ride = 0 if ride is None else 1
    out_chunk = D_MODEL // N_DEV
    if which == "up":
        rows, cols = D_MODEL, FF_CHUNK
        in_specs = [_resident((D_MODEL, t)), pl.BlockSpec((t, FF_CHUNK), lambda s, order_ref: (0, order_ref[s]))]
    elif which == "down":
        rows, cols = FF_CHUNK, D_MODEL
        in_specs = [pl.BlockSpec((t, FF_CHUNK), lambda s, order_ref: (0, order_ref[s])), _resident((t, D_MODEL))]
    else:
        rows, cols = out_chunk, D_MODEL
        half = pl.BlockSpec((t, out_chunk), lambda s, order_ref: (0, order_ref[s] % (N_DEV // 2)))
        in_specs = [half, half, _resident((t, D_MODEL))]

    def body(order_ref, *refs):
        own_ref, from_sib_ref, pair_ref = refs[n_in + n_ride:n_in + n_ride + 3]
        send_buf, land_buf, send_sems, recv_sems = refs[n_in + 2 * n_ride + 3:n_in + 2 * n_ride + 7]
        s_now = pl.program_id(0)
        x, y, c = _mesh_pos()
        sibling = (x, y, 1 - c)
        sems = (send_sems, recv_sems)

        @pl.when(s_now == 0)
        def _():
            _enter_with([sibling] + (_other_chips(x, y, c) if n_ride else []))

        if n_ride:
            _chip_exchange_beside(s_now == 0, s_now == N_DEV - 1, [refs[n_in]], [refs[n_in + 3 + n_ride]],
                                  refs[n_in + 2 * n_ride + 7:], enter=False)

        def hand_over(k):
            dst = land_buf.at[k] if k < n_far else from_sib_ref
            return _push(send_buf.at[k], dst, sems, k, sibling)

        if which == "out":
            ma_ref, mc_ref, b_ref = refs[:n_in]
            block = lax.cond(order_ref[s_now] < N_DEV // 2, lambda: _mm_tn(ma_ref[...], b_ref[...]),
                             lambda: _mm_tn(mc_ref[...], b_ref[...]))
        elif which == "down":
            block = _mm_tn(refs[0][...], refs[1][...])
        else:
            block = _mm(refs[0][...], refs[1][...])
        for k in range(n_far + 1):
            @pl.when(s_now == k)
            def _():
                send_buf[k] = block.astype(BF16)
                hand_over(k).start()

        for k in range(n_far):
            @pl.when(s_now == n_far + 1 + k)
            def _():
                hand_over(k).wait_recv()
                pair_ref[...] = (block + land_buf[k].astype(F32)).astype(BF16)

        @pl.when(s_now == N_DEV - 1)
        def _():
            own_ref[...] = block
            for k in range(n_far + 1):
                hand_over(k).wait_send()
            hand_over(n_far).wait_recv()

    rides = [] if ride is None else [ride]
    sems = lambda k: pltpu.SemaphoreType.DMA((k,))
    return pl.pallas_call(
        body, name=name,
        grid_spec=pltpu.PrefetchScalarGridSpec(
            num_scalar_prefetch=1, grid=(N_DEV,), in_specs=in_specs + [HBM_SPEC] * n_ride,
            out_specs=[pl.BlockSpec((rows, cols), lambda s, order_ref: (0, 0)), HBM_SPEC,
                       pl.BlockSpec((None, rows, cols), lambda s, order_ref: (jnp.clip(s - n_far - 1, 0, n_far - 1), 0, 0))]
            + [HBM_SPEC] * n_ride,
            scratch_shapes=[pltpu.VMEM((n_far + 1, rows, cols), BF16), pltpu.VMEM((n_far, rows, cols), BF16),
                            sems(n_far + 1), sems(n_far + 1)] + [sems(n_far), sems(n_far)] * n_ride),
        out_shape=[jax.ShapeDtypeStruct((rows, cols), F32), jax.ShapeDtypeStruct((rows, cols), BF16),
                   jax.ShapeDtypeStruct((n_far, rows, cols), BF16)]
        + [jax.ShapeDtypeStruct(r.shape, r.dtype) for r in rides],
        compiler_params=_params("arbitrary", barrier_id=barrier_id),
    )(order, *operands, *rides)


def _chip_exchange_beside(first, last, sums, outs, sems, enter=True):
    chips = _other_chips(*_mesh_pos())
    copies = [_push(sums[i].at[k], outs[i].at[k], sems, len(chips) * i + k, chip)
              for i in range(len(sums)) for k, chip in enumerate(chips)]

    @pl.when(first)
    def _():
        if enter:
            _enter_with(chips)
        for cp in copies:
            cp.start()

    @pl.when(last)
    def _():
        for cp in copies:
            cp.wait()


ROW_GCONV, ROW_CW0 = 1, 2


def _conv_bwd(dmixed, gates, g_conv, conv_w, tm):
    t = gates.shape[0]
    n = t // tm
    rev = lambda i: n - 1 - i

    def body(dm_ref, gates_ref, gprev_ref, gc_ref, cw_ref, dgates_ref, small_ref, carry_ref):
        i = pl.program_id(0)

        @pl.when(i == 0)
        def _():
            small_ref[...] = jnp.zeros_like(small_ref)
            carry_ref[...] = jnp.zeros_like(carry_ref)

        gates = gates_ref[...]
        gb, gcc, xin = gates[:, :CONV_W], gates[:, CONV_W:2 * CONV_W], gates[:, 2 * CONV_W:]
        u = gcc * xin
        gp = gprev_ref[...]
        uprev = jnp.where(rev(i) == 0, 0.0, gp[:, CONV_W:2 * CONV_W] * gp[:, 2 * CONV_W:])
        u1, u2 = _shift_rows_down(u, uprev, 1), _shift_rows_down(u, uprev, 2)
        w = cw_ref[...]
        c = _conv3(u, u1, u2, w)
        conv = gb * c
        rcv = _inv_rms(conv)
        c_hat = conv * rcv
        dconv, dgc = _rms_bwd(c_hat, rcv, gc_ref[...], dm_ref[...])
        dc = dconv * gb
        nxt = carry_ref[...]
        du = (w[2:3, :] * dc + w[1:2, :] * _shift_rows_up(dc, nxt, 1)) + w[0:1, :] * _shift_rows_up(dc, nxt, 2)
        carry_ref[...] = dc[0:8, :]
        dgates_ref[:, :CONV_W] = (dconv * c).astype(BF16)
        dgates_ref[:, CONV_W:2 * CONV_W] = (du * xin).astype(BF16)
        dgates_ref[:, 2 * CONV_W:] = (du * gcc).astype(BF16)
        small_ref[ROW_GCONV:ROW_GCONV + 1, :] += _colsum(dgc)
        small_ref[ROW_CW0:ROW_CW0 + 1, :] += _colsum(dc * u2)
        small_ref[ROW_CW0 + 1:ROW_CW0 + 2, :] += _colsum(dc * u1)
        small_ref[ROW_CW0 + 2:ROW_CW0 + 3, :] += _colsum(dc * u)

    tile = lambda w_: pl.BlockSpec((tm, w_), lambda i: (rev(i), 0))
    prev8 = pl.BlockSpec((8, GATES_W), lambda i: (jnp.maximum(rev(i) * (tm // 8) - 1, 0), 0))
    conv_half = pl.BlockSpec((tm, CONV_W), lambda i: (rev(i), ATTN_W // CONV_W))
    return pl.pallas_call(
        body, name="conv_bwd", grid=(n,),
        in_specs=[conv_half, tile(GATES_W), prev8, _full((1, CONV_W)), _full((3, CONV_W))],
        out_specs=[tile(GATES_W), _full((SMALL_ROWS, CONV_W))],
        out_shape=[jax.ShapeDtypeStruct((t, GATES_W), BF16), jax.ShapeDtypeStruct((SMALL_ROWS, CONV_W), F32)],
        scratch_shapes=[pltpu.VMEM((8, CONV_W), F32)],
        compiler_params=_params("arbitrary"),
    )(dmixed, gates, gates, g_conv, conv_w)


def _attn_bwd(qkv, dmixed, attn, g_attn, sinks, rope, sums):
    t = qkv.shape[0]
    n_steps = t // ATTN_STEP
    rev = lambda i: n_steps - 1 - i
    rc, rs1, rs2 = rope

    def body(sink_ref, q_ref, kp_ref, kc_ref, vp_ref, vc_ref, dm_ref, attn_ref, ga_ref, c_ref, s1_ref, s2_ref, sums_ref,
             dqkv_ref, dsink_ref, dgain_ref, arrived_ref, ck_ref, cv_ref, kacc_ref, vacc_ref, send_sems, recv_sems):
        i = pl.program_id(0)
        _chip_exchange_beside(i == 0, i == n_steps - 1, [sums_ref], [arrived_ref], (send_sems, recv_sems))

        @pl.when(i == 0)
        def _():
            dsink_ref[...] = jnp.zeros_like(dsink_ref)
            dgain_ref[...] = jnp.zeros_like(dgain_ref)
            ck_ref[...] = jnp.zeros_like(ck_ref)
            cv_ref[...] = jnp.zeros_like(cv_ref)

        kacc_ref[...] = jnp.zeros_like(kacc_ref)
        vacc_ref[...] = jnp.zeros_like(vacc_ref)
        a = attn_ref[...]
        ra = _inv_rms(a)
        dattn, dgain = _rms_bwd(a * ra, ra, ga_ref[...], dm_ref[...])
        dgain_ref[0:1, :] += _colsum(dgain)
        qt = (q_ref[...] * ATTN_SCALE).T
        dot = dattn.astype(BF16).T
        keys = jnp.concatenate([kp_ref[...], kc_ref[...]], axis=0)
        vals = jnp.concatenate([vp_ref[...], vc_ref[...]], axis=0)
        sink = [_group_sinks(sink_ref, g) for g in range(N_KV)]
        c, s1, s2 = c_ref[...], s1_ref[...], s2_ref[...]
        lane = lax.broadcasted_iota(jnp.int32, (1, 128), 1)
        dsink = jnp.zeros((1, 128), F32)
        masks = _attn_masks(rev(i) > 0)
        dq_parts = []
        for b in range(ATTN_STEP_BLOCKS):
            window = slice(BLOCK * b, BLOCK * (b + 2))
            valid = masks[b]
            dq_parts.append([])
            dk_parts, dv_parts = [], []
            for g in range(N_KV):
                gs = slice(HEAD_DIM * g, HEAD_DIM * (g + 1))
                kk, vv = keys[window, gs], vals[window, gs]
                qtg, dotg = _heads_side_by_side(qt, g, b), _heads_side_by_side(dot, g, b)
                probs, psink = _attn_probs(qtg, kk, sink[g], valid)
                dp = _mm(vv, dotg)
                delta = jnp.sum(probs * dp, axis=0, keepdims=True)
                ds = (probs * (dp - delta)).astype(BF16)
                sink_terms = psink * delta
                for hh in range(GROUP):
                    head_sum = jnp.sum(sink_terms[:, BLOCK * hh:BLOCK * (hh + 1)])
                    dsink = dsink + jnp.where(lane == GROUP * g + hh, -head_sum, 0.0)
                dq_parts[b].append(_mm_tn(kk * ATTN_SCALE, ds))
                dk_parts.append(_mm_nt(ds, qtg))
                dv_parts.append(_mm_nt(probs.astype(BF16), dotg))
            kacc_ref[window, :] += jnp.concatenate(dk_parts, axis=1)
            vacc_ref[window, :] += jnp.concatenate(dv_parts, axis=1)
        dq = _to_token_rows(dq_parts)
        for ci in range(ATTN_W // 128):
            sl = slice(128 * ci, 128 * (ci + 1))
            dqkv_ref[:, sl] = _rope_transpose(dq[:, sl], c, s1, s2).astype(BF16)
        kacc_ref[ATTN_STEP:, :] += ck_ref[...]
        vacc_ref[ATTN_STEP:, :] += cv_ref[...]
        ck_ref[...] = kacc_ref[:BLOCK, :]
        cv_ref[...] = vacc_ref[:BLOCK, :]
        dqkv_ref[:, ATTN_W:ATTN_W + KV_W] = _rope_transpose(kacc_ref[BLOCK:, :], c, s1, s2).astype(BF16)
        dqkv_ref[:, ATTN_W + KV_W:] = vacc_ref[BLOCK:, :].astype(BF16)
        dsink_ref[0:1, :] += dsink

    blk = lambda w_: pl.BlockSpec((ATTN_STEP, w_), lambda i: (rev(i), 0))
    return pl.pallas_call(
        body, name="attn_bwd", grid=(n_steps,),
        in_specs=[pl.BlockSpec(memory_space=pltpu.SMEM)] + _qkv_specs(rev)
        + [blk(ATTN_W), blk(ATTN_W), _full((1, ATTN_W)), blk(128), blk(128), blk(128), HBM_SPEC],
        out_specs=[blk(QKV_W), _full((8, 128)), _full((SMALL_ROWS, ATTN_W)), HBM_SPEC],
        out_shape=[jax.ShapeDtypeStruct((t, QKV_W), BF16), jax.ShapeDtypeStruct((8, 128), F32),
                   jax.ShapeDtypeStruct((SMALL_ROWS, ATTN_W), F32), jax.ShapeDtypeStruct(sums.shape, sums.dtype)],
        scratch_shapes=[pltpu.VMEM((BLOCK, KV_W), F32), pltpu.VMEM((BLOCK, KV_W), F32),
                        pltpu.VMEM((ATTN_KEYS, KV_W), F32), pltpu.VMEM((ATTN_KEYS, KV_W), F32),
                        pltpu.SemaphoreType.DMA((len(CHIP_FLIPS),)), pltpu.SemaphoreType.DMA((len(CHIP_FLIPS),))],
        compiler_params=_params("arbitrary", barrier_id=6),
    )(sinks, qkv, qkv, qkv, qkv, qkv, dmixed, attn, g_attn, rc, rs1, rs2, sums)


def _grad_x_tile(dq, dg, x_hat, r, g1, w_ref, dh):
    dhn = _mm(dq, w_ref[:QKV_W, :]) + _mm(dg, w_ref[QKV_W:, :])
    dx, dg1 = _rms_bwd(x_hat, r, g1, dhn)
    return dh + dx, _colsum(dg1)


def _in_proj_bwd(dqkv, dgates, x, dh, g1, w_in, tm, out_sums):
    t = x.shape[0]
    n = t // tm
    n_cover = max(n // 2, 1)
    n_steps = n + n_cover
    n_far = len(CHIP_FLIPS)
    shard = (IN_SHARD, D_MODEL)

    def body(dq_ref, dg_ref, x_ref, dh_ref, g1_ref, w_ref, osums_ref,
             dx_ref, own_ref, sib_ref, far_ref, dg1_ref, oarrived_ref,
             acc_ref, send_buf, land_buf, pair_buf, d2d_send, d2d_recv, ici_send, ici_recv, o_send, o_recv):
        i = pl.program_id(0)
        x_pos, y_pos, c = _mesh_pos()
        my_chip = 2 * x_pos + y_pos
        sibling = (x_pos, y_pos, 1 - c)
        @pl.when(i == 0)
        def _():
            _enter_with(_sibling_and_chips(x_pos, y_pos, c))

        _chip_exchange_beside(i == 0, i == n_steps - 1, [osums_ref], [oarrived_ref], (o_send, o_recv), enter=False)

        def rows(d):
            return slice(IN_SHARD * d, IN_SHARD * (d + 1))

        def hand_over(chip):
            return _push(send_buf.at[chip], land_buf.at[chip], (d2d_send, d2d_recv), chip, sibling)

        def to_chip(chip, rel):
            return pltpu.make_async_remote_copy(
                src_ref=pair_buf.at[chip], dst_ref=far_ref.at[rel - 1], send_sem=ici_send.at[rel - 1],
                recv_sem=ici_recv.at[rel - 1], device_id=(chip // 2, chip % 2, c), device_id_type=MESH)

        @pl.when(i == 0)
        def _():
            acc_ref[...] = jnp.zeros_like(acc_ref)
            dg1_ref[...] = jnp.zeros_like(dg1_ref)

        def normed_x():
            xv = x_ref[...]
            r = _inv_rms(xv)
            return xv * r, r

        @pl.when(i < n)
        def _():
            hn = (normed_x()[0] * g1_ref[...]).astype(BF16)
            acc_ref[:QKV_W, :] += _mm_tn(dq_ref[...], hn)
            acc_ref[QKV_W:, :] += _mm_tn(dg_ref[...], hn)

        @pl.when(i == n - 1)
        def _():
            for d in range(N_DEV):
                @pl.when(d % 2 != c)
                def _():
                    send_buf[d // 2] = acc_ref[rows(d), :].astype(BF16)
                    hand_over(d // 2).start()
            for d in range(N_DEV):
                chip = d // 2

                @pl.when(d % 2 == c)
                def _():
                    hand_over(chip).wait_recv()

                    @pl.when(chip == my_chip)
                    def _():
                        own_ref[...] = acc_ref[rows(d), :]
                        sib_ref[...] = land_buf[chip]

                    @pl.when(chip != my_chip)
                    def _():
                        pair_buf[chip] = (acc_ref[rows(d), :] + land_buf[chip].astype(F32)).astype(BF16)
                        to_chip(chip, chip ^ my_chip).start()
            for chip in range(N_CHIPS):
                hand_over(chip).wait_send()

        @pl.when(i >= n)
        def _():
            x_hat, r = normed_x()
            dx_ref[...], dg1 = _grad_x_tile(dq_ref[...], dg_ref[...], x_hat, r, g1_ref[...], w_ref, dh_ref[...])
            dg1_ref[0:1, :] += dg1

        @pl.when(i == n_steps - 1)
        def _():
            for rel in range(1, n_far + 1):
                to_chip(0, rel).wait()

    both = lambda w_: pl.BlockSpec((tm, w_), lambda i: (i % n, 0))
    second = pl.BlockSpec((tm, D_MODEL), lambda i: (jnp.maximum(i - n, 0), 0))
    whole = lambda dtype: jax.ShapeDtypeStruct(shard, dtype)
    sems = lambda k: pltpu.SemaphoreType.DMA((k,))
    res = pl.pallas_call(
        body, name="in_proj_bwd", grid=(n_steps,),
        in_specs=[both(QKV_W), both(GATES_W), both(D_MODEL), second, _full((1, D_MODEL)), _resident((IN_COLS, D_MODEL)),
                  HBM_SPEC],
        out_specs=[second, _full(shard), _full(shard), HBM_SPEC, _full((SMALL_ROWS, D_MODEL)), HBM_SPEC],
        out_shape=[jax.ShapeDtypeStruct((n_cover * tm, D_MODEL), F32), whole(F32), whole(BF16),
                   jax.ShapeDtypeStruct((n_far,) + shard, BF16), jax.ShapeDtypeStruct((SMALL_ROWS, D_MODEL), F32),
                   jax.ShapeDtypeStruct(out_sums.shape, out_sums.dtype)],
        scratch_shapes=[pltpu.VMEM((IN_COLS, D_MODEL), F32), pltpu.VMEM((N_CHIPS,) + shard, BF16),
                        pltpu.VMEM((N_CHIPS,) + shard, BF16), pltpu.VMEM((N_CHIPS,) + shard, BF16),
                        sems(N_CHIPS), sems(N_CHIPS), sems(n_far), sems(n_far), sems(n_far), sems(n_far)],
        compiler_params=_params("arbitrary", barrier_id=7),
    )(dqkv, dgates, x, dh, g1, w_in, out_sums)
    return res[0], (res[1], res[2], res[3]), res[4], res[5]


def _grad_x_rest(dqkv, dgates, x, dh, g1, w_in, tm, head, dg1_rows):
    t = x.shape[0]
    first = head.shape[0] // tm
    n_rest = t // tm - first
    if n_rest == 0:
        return head, dg1_rows
    assert first <= n_rest

    def body(dq_ref, dg_ref, x_ref, dh_ref, g1_ref, w_ref, head_ref, rows_ref, gx_ref, dg1_ref, stage, sems):
        j = pl.program_id(0)

        def tile_out(step, kind):
            row0 = (step + first) * tm if kind == 0 else step * tm
            slot = 2 * kind + step % 2
            return pltpu.make_async_copy(stage.at[slot], gx_ref.at[pl.ds(pl.multiple_of(row0, tm), tm), :], sems.at[slot])

        @pl.when(j == 0)
        def _():
            dg1_ref[...] = rows_ref[...]

        @pl.when(j >= 2)
        def _():
            tile_out(j - 2, 0).wait()

        @pl.when((j >= 2) & (j - 2 < first))
        def _():
            tile_out(j - 2, 1).wait()

        @pl.when(j < first)
        def _():
            stage[2 + j % 2] = head_ref[...]
            tile_out(j, 1).start()

        xv = x_ref[...]
        r = _inv_rms(xv)
        dx, dg1 = _grad_x_tile(dq_ref[...], dg_ref[...], xv * r, r, g1_ref[...], w_ref, dh_ref[...])
        stage[j % 2] = dx
        dg1_ref[0:1, :] += dg1
        tile_out(j, 0).start()

        @pl.when(j == n_rest - 1)
        def _():
            for back in range(min(2, n_rest)):
                tile_out(j - back, 0).wait()

                @pl.when(j - back < first)
                def _():
                    tile_out(j - back, 1).wait()

    tile = lambda w_: pl.BlockSpec((tm, w_), lambda j: (j + first, 0))
    head_tile = pl.BlockSpec((tm, D_MODEL), lambda j: (jnp.minimum(j, first - 1), 0))
    return pl.pallas_call(
        body, name="grad_x_rest", grid=(n_rest,),
        in_specs=[tile(QKV_W), tile(GATES_W), tile(D_MODEL), tile(D_MODEL), _full((1, D_MODEL)),
                  _resident((IN_COLS, D_MODEL)), head_tile, _full((SMALL_ROWS, D_MODEL))],
        out_specs=[HBM_SPEC, _full((SMALL_ROWS, D_MODEL))],
        out_shape=[jax.ShapeDtypeStruct((t, D_MODEL), F32), jax.ShapeDtypeStruct((SMALL_ROWS, D_MODEL), F32)],
        scratch_shapes=[pltpu.VMEM((4, tm, D_MODEL), F32), pltpu.SemaphoreType.DMA((4,))],
        compiler_params=_params("arbitrary"),
    )(dqkv, dgates, x, dh, g1, w_in, head, dg1_rows)


def _all_gather(shards, name):
    n = len(shards)

    def body(*refs):
        _enter_with(_sibling_and_chips(*_mesh_pos()))
        start, finish = _gather_steps(refs[:n], refs[n:2 * n], *refs[2 * n:])
        start()
        finish()

    return pl.pallas_call(
        body, name=name,
        in_specs=[HBM_SPEC] * n, out_specs=[HBM_SPEC] * n,
        out_shape=[jax.ShapeDtypeStruct((N_DEV,) + s.shape, s.dtype) for s in shards],
        scratch_shapes=[pltpu.SemaphoreType.DMA((7 * n,)), pltpu.SemaphoreType.DMA((7 * n,)),
                        pltpu.SemaphoreType.DMA((n,))],
        compiler_params=_params(barrier_id=8),
    )(*shards)


def _adam_math(w, g, m, v):
    m = ADAM_B1 * m + (1.0 - ADAM_B1) * g
    v = ADAM_B2 * v + (1.0 - ADAM_B2) * (g * g)
    m_hat = m / (1.0 - ADAM_B1 ** ADAM_STEP)
    v_hat = v / (1.0 - ADAM_B2 ** ADAM_STEP)
    delta = -ADAM_LR * (m_hat / (jnp.sqrt(v_hat) + ADAM_EPS) + ADAM_WD * w)
    return delta, m, v


def _adamw_reduced(w, m, v, own, from_sibling, from_chips, tr):
    rows, cols = w.shape

    def body(w_ref, m_ref, v_ref, own_ref, sib_ref, far_ref, g_ref, d_ref, nm_ref, nv_ref):
        g = own_ref[...] + sib_ref[...].astype(F32)
        for k in range(len(CHIP_FLIPS)):
            g = g + far_ref[k].astype(F32)
        g_ref[...] = g
        d_ref[...], nm_ref[...], nv_ref[...] = _adam_math(w_ref[...], g, m_ref[...], v_ref[...])

    tile = pl.BlockSpec((tr, cols), lambda i: (i, 0))
    out = jax.ShapeDtypeStruct((rows, cols), F32)
    return pl.pallas_call(
        body, name="adamw_reduced", grid=(rows // tr,),
        in_specs=[tile] * 5 + [pl.BlockSpec((len(CHIP_FLIPS), tr, cols), lambda i: (0, i, 0))],
        out_specs=[tile] * 4, out_shape=[out] * 4,
        compiler_params=_params("parallel"),
    )(w, m, v, own, from_sibling, from_chips)


SMALL_PARAMS = ("pre_mix_norm", "post_mix_norm", "pre_mlp_norm", "post_mlp_norm", "attn_group_norm", "conv_group_norm",
                "conv_w", "attn_sinks")


SMALL_WIDTHS = (D_MODEL, CONV_W, ATTN_W, 128, D_MODEL)


def _small_tail(gathered, dev, weights, first_moments, second_moments):
    n = len(SMALL_PARAMS)
    conv_shard = CONV_W // N_DEV

    def body(dev_ref, sums_ref, *refs):
        w_refs, m_refs, v_refs = refs[:n], refs[n:2 * n], refs[2 * n:3 * n]
        loss_ref, outs = refs[3 * n], refs[3 * n + 1:]
        total = sums_ref[0]
        for d in range(1, N_DEV):
            total = total + sums_ref[d]
        starts = [sum(SMALL_WIDTHS[:i]) for i in range(len(SMALL_WIDTHS))]
        mid, conv, gain, sink, inp = (total[:, a:a + w_] for a, w_ in zip(starts, SMALL_WIDTHS))
        loss_ref[...] = (0.5 / D_MODEL) * jnp.sum(mid[ROW_LOSS:ROW_LOSS + 1, :], axis=1, keepdims=True)
        conv_rows = conv[ROW_CW0:ROW_CW0 + 3, :]
        conv_g = jnp.zeros((3, conv_shard), F32)
        for d in range(N_DEV):
            conv_g = conv_g + jnp.where(dev_ref[0] == d, conv_rows[:, conv_shard * d:conv_shard * (d + 1)], 0.0)
        grads = [inp[0:1, :], mid[ROW_G2:ROW_G2 + 1, :], mid[ROW_G3:ROW_G3 + 1, :], mid[ROW_G4:ROW_G4 + 1, :],
                 gain[0:1, :], conv[ROW_GCONV:ROW_GCONV + 1, :], conv_g, sink[0:1, :N_HEADS]]
        for i, g in enumerate(grads):
            parts = [(..., g)] if len(w_refs[i].shape) == 2 else [(r, g[r:r + 1, :]) for r in range(g.shape[0])]
            for at, g_at in parts:
                delta, new_m, new_v = _adam_math(w_refs[i][at], g_at, m_refs[i][at], v_refs[i][at])
                outs[i][at], outs[n + i][at], outs[2 * n + i][at], outs[3 * n + i][at] = g_at, delta, new_m, new_v

    params = list(weights) + list(first_moments) + list(second_moments)
    shapes = [jax.ShapeDtypeStruct(w.shape, F32) for w in weights]
    res = pl.pallas_call(
        body, name="small_tail", grid=(1,),
        in_specs=[pl.BlockSpec(memory_space=pltpu.SMEM), _full(gathered.shape)] + [_full(p.shape) for p in params],
        out_specs=[_full((1, 1))] + [_full(sh.shape) for sh in shapes] * 4,
        out_shape=[jax.ShapeDtypeStruct((1, 1), F32)] + shapes * 4,
    )(dev, gathered, *params)
    return res[0], [res[1 + k * n:1 + (k + 1) * n] for k in range(4)]


TOKEN_TILE = 512
MID_TILE = 256
MID_CHUNK = 1024
MID_CHUNKS = D_FF // MID_CHUNK
ADAM_ROWS = 512


def _local_grads(x, target, g1, w_in_shard, conv_shard, sinks, g_attn, g_conv, g2, g3, g4, shards, order):
    t = x.shape[0]
    tm = min(TOKEN_TILE, t)
    rope = _rope_tables(t)
    qkv, gates, mconv, w_in, conv_w, gathered = _in_proj_fwd(x, g1, w_in_shard, conv_shard, g_conv, rope, tm, shards,
                                                             (False, True, False))
    attn, mattn, (w_out, w_up, w_down) = _attn_fwd(qkv, sinks, g_attn, shards, gathered)
    act, dup, hn2t, dmo, dmix, dh, dmixed, small_mid = _mid(
        mattn, mconv, x, target, g2, g3, g4, w_out.reshape(D_MODEL, D_MODEL),
        w_up, w_down.reshape(D_FF, D_MODEL), min(MID_TILE, t))
    up_own, up_sib, up_sums = _dw_pair_sums((hn2t, dup), order, "up", "dw_up", 2)
    down_own, down_sib, down_sums, up_far = _dw_pair_sums((act, dmo), order, "down", "dw_down", 3, ride=up_sums)
    out_own, out_sib, out_sums = _dw_pair_sums((mattn, mconv, dmix), order, "out", "dw_out", 4)
    dgates, small_conv = _conv_bwd(dmixed, gates, g_conv, conv_w, tm)
    dqkv, dsink, dg_attn, down_far = _attn_bwd(qkv, dmixed, attn, g_attn, sinks, rope, down_sums)
    grad_x_head, dw_in, small_in, out_far = _in_proj_bwd(dqkv, dgates, x, dh, g1, w_in, tm, out_sums)
    grad_x, small_in = _grad_x_rest(dqkv, dgates, x, dh, g1, w_in, tm, grad_x_head, small_in)
    dw_out, dw_up, dw_down = (out_own, out_sib, out_far), (up_own, up_sib, up_far), (down_own, down_sib, down_far)
    return grad_x, dw_in, dw_out, dw_up, dw_down, (small_mid, small_conv, dg_attn, dsink, small_in)


def kernel(x, pre_mix_norm, w_in, conv_w, attn_sinks, attn_group_norm, conv_group_norm, w_out, post_mix_norm, pre_mlp_norm, w_up, w_down, post_mlp_norm, loss_target, m_pre_mix_norm, m_w_in, m_conv_w, m_attn_sinks, m_attn_group_norm, m_conv_group_norm, m_w_out, m_post_mix_norm, m_pre_mlp_norm, m_w_up, m_w_down, m_post_mlp_norm, v_pre_mix_norm, v_w_in, v_conv_w, v_attn_sinks, v_attn_group_norm, v_conv_group_norm, v_w_out, v_post_mix_norm, v_pre_mlp_norm, v_w_up, v_w_down, v_post_mlp_norm):
    xi, yi, ci = _mesh_pos()
    chip = 2 * xi + yi
    dev = 2 * chip + ci

    order = _block_order(dev)

    shards = [w_out[0].astype(BF16), w_up[0].astype(BF16), w_down[0].astype(BF16)]

    turned = lambda a: jnp.swapaxes(a, 1, 2)
    grad_x, dw_in, dw_out, dw_up, dw_down, smalls = _local_grads(
        x[0], loss_target[0], pre_mix_norm, turned(w_in)[0].astype(BF16), conv_w[0], attn_sinks, attn_group_norm, conv_group_norm,
        post_mix_norm, pre_mlp_norm, post_mlp_norm, shards, order)

    big = {}
    for name, w, m, v, (own, sib, far) in zip(
            ("w_in", "w_out", "w_up", "w_down"), (turned(w_in), w_out, w_up, w_down),
            (turned(m_w_in), m_w_out, m_w_up, m_w_down), (turned(v_w_in), v_w_out, v_w_up, v_w_down),
            (dw_in, dw_out, dw_up, dw_down)):
        big[name] = [a[None] for a in _adamw_reduced(w[0], m[0], v[0], own, sib, far, min(ADAM_ROWS, w.shape[1]))]
    big["w_in"] = [turned(a) for a in big["w_in"]]

    flat = lambda a: a.reshape(-1, a.shape[-1]) if a.ndim < 3 else a.reshape(a.shape[1], 1, a.shape[2])
    loss, small = _small_tail(
        _all_gather([jnp.concatenate(smalls, axis=1)], "gather_small")[0], dev.reshape(1).astype(jnp.int32),
        [flat(a) for a in (pre_mix_norm, post_mix_norm, pre_mlp_norm, post_mlp_norm, attn_group_norm, conv_group_norm,
                           conv_w, attn_sinks)],
        [flat(a) for a in (m_pre_mix_norm, m_post_mix_norm, m_pre_mlp_norm, m_post_mlp_norm, m_attn_group_norm,
                           m_conv_group_norm, m_conv_w, m_attn_sinks)],
        [flat(a) for a in (v_pre_mix_norm, v_post_mix_norm, v_pre_mlp_norm, v_post_mlp_norm, v_attn_group_norm,
                           v_conv_group_norm, v_conv_w, v_attn_sinks)])

    order = ("pre_mix_norm", "w_in", "conv_w", "attn_sinks", "attn_group_norm", "conv_group_norm", "w_out",
             "post_mix_norm", "pre_mlp_norm", "w_up", "w_down", "post_mlp_norm")
    shape_of = {"conv_w": conv_w.shape}
    outs = []
    for k in range(4):
        by_name = dict(zip(SMALL_PARAMS, small[k]))
        outs += [big[nm][k] if nm in big else by_name[nm].reshape(shape_of.get(nm, by_name[nm].shape)) for nm in order]
    loss = loss.reshape(())
    return (loss, grad_x[None], *outs)
```

```python
import jax
import jax.numpy as jnp
import numpy as np
from jax import lax
from jax.experimental import pallas as pl
from jax.experimental.pallas import tpu as pltpu

F32 = jnp.float32
BF16 = jnp.bfloat16

D_MODEL = 1024
HEAD_DIM = 64
ATTN_W = 512
CONV_W = 512
N_HEADS = 8
N_KV = 2
GROUP = 4
KV_W = 128
QKV_W = ATTN_W + 2 * KV_W
GATES_W = 3 * CONV_W
IN_COLS = QKV_W + GATES_W
D_FF = 4096
FF_CHUNK = 512
BLOCK = 128
ROT_HALF = 8
ROPE_THETA = 500000.0
NORM_EPS = 1e-6
NEG_INF = -1e30
ATTN_SCALE = 0.125
N_DEV = 8
N_CHIPS = 4
IN_SHARD = IN_COLS // N_DEV

ADAM_LR = 0.001
ADAM_B1 = 0.9
ADAM_B2 = 0.999
ADAM_EPS = 1e-08
ADAM_WD = 0.01
ADAM_STEP = 10

V7X_VMEM_BYTES = 64 * 1024 * 1024
VMEM_LIMIT = V7X_VMEM_BYTES - 2 * 1024 * 1024

MESH = pl.DeviceIdType.MESH
HBM_SPEC = pl.BlockSpec(memory_space=pltpu.HBM)


def _params(*sem, barrier_id=None):
    return pltpu.CompilerParams(dimension_semantics=sem or None, vmem_limit_bytes=VMEM_LIMIT, collective_id=barrier_id)


def _mm(a, b):
    return jnp.dot(a, b, preferred_element_type=F32)


def _mm_nt(a, b):
    return lax.dot_general(a, b, (((1,), (1,)), ((), ())), preferred_element_type=F32)


def _mm_tn(a, b):
    return lax.dot_general(a, b, (((0,), (0,)), ((), ())), preferred_element_type=F32)


def _inv_rms(x):
    return lax.rsqrt(jnp.mean(x * x, axis=-1, keepdims=True) + NORM_EPS)


def _rms_bwd(xhat, r, gain, dy):
    gy = dy * gain
    return r * (gy - xhat * jnp.mean(gy * xhat, axis=-1, keepdims=True)), dy * xhat


def _colsum(a):
    return jnp.sum(a, axis=0, keepdims=True)


def _full(shape):
    zeros = (0,) * len(shape)
    return pl.BlockSpec(shape, lambda *_: zeros)


def _resident(shape):
    zeros = (0,) * len(shape)
    return pl.BlockSpec(shape, lambda *_: zeros, pipeline_mode=pl.Buffered(1))


def _rope_tables(t):
    pos = np.arange(t, dtype=np.float32)
    inv_freq = (ROPE_THETA ** (-np.arange(0, 2 * ROT_HALF, 2, dtype=np.float64) / (2 * ROT_HALF))).astype(np.float32)
    ang = (pos[:, None] * inv_freq[None, :]).astype(np.float64)
    cos, sin = np.cos(ang).astype(np.float32), np.sin(ang).astype(np.float32)
    zeros8 = np.zeros((t, ROT_HALF), np.float32)
    rest = np.zeros((t, HEAD_DIM - 2 * ROT_HALF), np.float32)
    c_head = np.concatenate([cos, cos, rest + 1.0], axis=1)
    s1_head = np.concatenate([zeros8, sin, rest], axis=1)
    s2_head = np.concatenate([-sin, zeros8, rest], axis=1)
    two = lambda a: jnp.asarray(np.concatenate([a, a], axis=1))
    return two(c_head), two(s1_head), two(s2_head)


def _rope(v, c, s1, s2):
    return v * c + pltpu.roll(v, ROT_HALF, 1) * s1 + pltpu.roll(v, 128 - ROT_HALF, 1) * s2


def _rope_transpose(dv, c, s1, s2):
    return dv * c + pltpu.roll(dv * s1, 128 - ROT_HALF, 1) + pltpu.roll(dv * s2, ROT_HALF, 1)


def _shift_rows_down(u, prev, k):
    row = lax.broadcasted_iota(jnp.int32, u.shape, 0)
    out = pltpu.roll(u, k, 0)
    for r in range(k):
        out = jnp.where(row == r, prev[8 - k + r:8 - k + r + 1, :], out)
    return out


def _shift_rows_up(u, nxt, k):
    n = u.shape[0]
    row = lax.broadcasted_iota(jnp.int32, u.shape, 0)
    out = pltpu.roll(u, n - k, 0)
    for r in range(k):
        out = jnp.where(row == n - k + r, nxt[r:r + 1, :], out)
    return out


def _conv3(u, u1, u2, w):
    return (w[0:1, :] * u2 + w[1:2, :] * u1) + w[2:3, :] * u


def _mesh_pos():
    return lax.axis_index("x"), lax.axis_index("y"), lax.axis_index("c")


def _slot(ref, pos):
    dev = 4 * pos[0] + 2 * pos[1] + pos[2]
    if len(ref.shape) == 2:
        width = ref.shape[1] // N_DEV
        return ref.at[:, pl.ds(pl.multiple_of(dev * width, width), width)]
    return ref.at[dev]


def _gathered_shape(shard, by_cols):
    if by_cols:
        return jax.ShapeDtypeStruct((shard.shape[0], N_DEV * shard.shape[1]), shard.dtype)
    return jax.ShapeDtypeStruct((N_DEV,) + shard.shape, shard.dtype)


def _enter_with(peers):
    barrier = pltpu.get_barrier_semaphore()
    for peer in peers:
        pl.semaphore_signal(barrier, inc=1, device_id=peer, device_id_type=MESH)
    pl.semaphore_wait(barrier, len(peers))


def _sibling_and_chips(x, y, c):
    return [(x, y, 1 - c), (1 - x, y, c), (x, 1 - y, c), (1 - x, 1 - y, c)]


def _push(src, dst, sems, k, to):
    send_sems, recv_sems = sems
    return pltpu.make_async_remote_copy(src_ref=src, dst_ref=dst, send_sem=send_sems.at[k], recv_sem=recv_sems.at[k],
                                        device_id=to, device_id_type=MESH)


def _gather_steps(shards, outs, send_sems, recv_sems, local_sems):
    n = len(shards)
    x, y, c = _mesh_pos()
    me, sibling = (x, y, c), (x, y, 1 - c)
    chips = [(1 - x, y), (x, 1 - y), (1 - x, 1 - y)]

    def copy(i, k, block, to, src=None):
        dst = _slot(outs[i], block)
        return _push(dst if src is None else src, dst, (send_sems, recv_sems), 7 * i + k, to)

    mine = [pltpu.make_async_copy(shards[i], _slot(outs[i], me), local_sems.at[i]) for i in range(n)]
    first = []
    for i in range(n):
        first.append(copy(i, 0, me, sibling, src=shards[i]))
        first += [copy(i, 1 + j, me, (*chip, c), src=shards[i]) for j, chip in enumerate(chips)]

    def start():
        for cp in mine + first:
            cp.start()

    def finish():
        passed = []
        for j, chip in enumerate(chips):
            for i in range(n):
                copy(i, 1 + j, (*chip, c), me).wait_recv()
                cp = copy(i, 4 + j, (*chip, c), sibling)
                cp.start()
                passed.append(cp)
        for i in range(n):
            copy(i, 0, sibling, me).wait_recv()
            for j, chip in enumerate(chips):
                copy(i, 4 + j, (*chip, 1 - c), me).wait_recv()
        for cp in first + passed:
            cp.wait_send()
        for cp in mine:
            cp.wait()

    return start, finish


def _gather_near(first, last, shards, outs, sems, local_sems):
    x, y, c = _mesh_pos()
    me, peers = (x, y, c), [(x, y, 1 - c), (1 - x, y, c), (x, 1 - y, c)]
    n = len(shards)
    local = [pltpu.make_async_copy(shards[i], _slot(outs[i], me), local_sems.at[i]) for i in range(n)]
    sends = [_push(shards[i], _slot(outs[i], me), sems, 3 * i + k, peers[k]) for i in range(n) for k in range(3)]
    arrivals = [_push(shards[i], _slot(outs[i], peers[k]), sems, 3 * i + k, peers[k]) for i in range(n) for k in range(3)]

    def start():
        for cp in local + sends:
            cp.start()

    if first is not None:
        pl.when(first)(start)

    @pl.when(last)
    def _():
        for cp in sends:
            cp.wait_send()
        for cp in arrivals:
            cp.wait_recv()
        for cp in local:
            cp.wait()

    return start


def _relay_route(x, y, c):
    south = c == 0
    via = (jnp.where(south, 1 - x, x), jnp.where(south, y, 1 - y))
    to = (jnp.where(south, x, 1 - x), jnp.where(south, 1 - y, y))
    return via, to


def _gather_far(first, middle, last, shards, ins, outs, sems):
    x, y, c = _mesh_pos()
    sibling = (x, y, 1 - c)
    chips = [(1 - x, y), (x, 1 - y), (1 - x, 1 - y)]
    via, to = _relay_route(x, y, c)
    n = len(shards)
    diag_send = [_push(_slot(ins[i], (*via, c)), _slot(outs[i], (*via, c)), sems, 4 * i, (*to, c)) for i in range(n)]
    diag_arrival = [_push(shards[i], _slot(outs[i], (*chips[2], c)), sems, 4 * i, (*to, c)) for i in range(n)]
    passed = [[_push(_slot(ins[i], (*chips[j], c)), _slot(outs[i], (*chips[j], c)), sems, 4 * i + 1 + j, sibling)
               for i in range(n)] for j in range(3)]
    from_sibling = [_push(shards[i], _slot(outs[i], (*chips[j], 1 - c)), sems, 4 * i + 1 + j, sibling)
                    for i in range(n) for j in range(3)]

    @pl.when(first)
    def _():
        for cp in diag_send + passed[0] + passed[1]:
            cp.start()

    @pl.when(middle)
    def _():
        for cp in diag_arrival:
            cp.wait_recv()
        for cp in passed[2]:
            cp.start()

    @pl.when(last)
    def _():
        for cp in from_sibling:
            cp.wait_recv()
        for cp in diag_send + passed[0] + passed[1] + passed[2]:
            cp.wait_send()


def _in_proj_fwd(x, g1, w_in, conv_w, g_conv, rope, tm, shards, by_cols):
    t = x.shape[0]
    rc, rs1, rs2 = rope
    n = len(shards)
    n_tiles = t // tm

    def body(*refs):
        x_ref, g1_ref, w_ref, cw_ref, gc_ref, c_ref, s1_ref, s2_ref = refs[:8]
        shard_refs = refs[8:8 + n]
        qkv_ref, gates_ref, mconv_ref, w_full_ref, cw_full_ref = refs[8 + n:13 + n]
        gathered = refs[13 + n:13 + 2 * n]
        carry_ref, w_land, cw_land, hn_ref = refs[13 + 2 * n:17 + 2 * n]
        now_sems = refs[17 + 2 * n:20 + 2 * n]
        step = pl.program_id(0)
        start_later_weights = _gather_near(None, step == 2 * n_tiles - 1, shard_refs, gathered,
                                           refs[20 + 2 * n:22 + 2 * n], refs[22 + 2 * n]) if n else None
        start_w_in, finish_w_in = _gather_steps([w_ref, cw_ref], [w_land, cw_land], *now_sems)

        @pl.when(step == 0)
        def _():
            carry_ref[...] = jnp.zeros_like(carry_ref)
            _enter_with(_sibling_and_chips(*_mesh_pos()))
            start_w_in()
            if start_later_weights is not None:
                start_later_weights()

        @pl.when(step < n_tiles)
        def _():
            xv = x_ref[...]
            hn_ref[step] = ((xv * _inv_rms(xv)) * g1_ref[...]).astype(BF16)

        @pl.when(step == n_tiles)
        def _():
            finish_w_in()
            conv_shard = CONV_W // N_DEV
            for d in range(N_DEV):
                w_full_ref[IN_SHARD * d:IN_SHARD * (d + 1), :] = w_land[d]
                cw_full_ref[:, conv_shard * d:conv_shard * (d + 1)] = cw_land[d]

        @pl.when(step >= n_tiles)
        def _():
            proj = _mm_nt(hn_ref[step - n_tiles], w_full_ref[...])
            c, s1, s2 = c_ref[...], s1_ref[...], s2_ref[...]
            for ci in range((ATTN_W + KV_W) // 128):
                sl = slice(128 * ci, 128 * (ci + 1))
                qkv_ref[:, sl] = _rope(proj[:, sl], c, s1, s2).astype(BF16)
            qkv_ref[:, ATTN_W + KV_W:QKV_W] = proj[:, ATTN_W + KV_W:QKV_W].astype(BF16)
            gates = proj[:, QKV_W:]
            gates_ref[...] = gates
            gb, gcc, xin = gates[:, :CONV_W], gates[:, CONV_W:2 * CONV_W], gates[:, 2 * CONV_W:]
            u = gcc * xin
            prev = carry_ref[...]
            conv = gb * _conv3(u, _shift_rows_down(u, prev, 1), _shift_rows_down(u, prev, 2), cw_full_ref[...])
            carry_ref[...] = u[tm - 8:tm, :]
            mconv_ref[...] = ((conv * _inv_rms(conv)) * gc_ref[...]).astype(BF16)

    first_pass = pl.BlockSpec((tm, D_MODEL), lambda i: (jnp.minimum(i, n_tiles - 1), 0))
    tile = lambda w_: pl.BlockSpec((tm, w_), lambda i: (jnp.maximum(i - n_tiles, 0), 0))
    sems = lambda k: pltpu.SemaphoreType.DMA((k,))
    res = pl.pallas_call(
        body, name="in_proj_fwd", grid=(2 * n_tiles,),
        in_specs=[first_pass, _full((1, D_MODEL)), HBM_SPEC, HBM_SPEC, _full((1, CONV_W)), tile(128), tile(128),
                  tile(128)] + [HBM_SPEC] * n,
        out_specs=[tile(QKV_W), tile(GATES_W), tile(CONV_W), _full((IN_COLS, D_MODEL)), _full((3, CONV_W))]
        + [HBM_SPEC] * n,
        out_shape=[jax.ShapeDtypeStruct((t, QKV_W), BF16), jax.ShapeDtypeStruct((t, GATES_W), F32),
                   jax.ShapeDtypeStruct((t, CONV_W), BF16), jax.ShapeDtypeStruct((IN_COLS, D_MODEL), BF16),
                   jax.ShapeDtypeStruct((3, CONV_W), F32)]
        + [_gathered_shape(s, cols) for s, cols in zip(shards, by_cols)],
        scratch_shapes=[pltpu.VMEM((8, CONV_W), F32), pltpu.VMEM((N_DEV,) + w_in.shape, BF16),
                        pltpu.VMEM((N_DEV,) + conv_w.shape, F32), pltpu.VMEM((n_tiles, tm, D_MODEL), BF16),
                        sems(14), sems(14), sems(2)]
        + ([sems(3 * n), sems(3 * n), sems(n)] if n else []),
        compiler_params=_params("arbitrary", barrier_id=0),
    )(x, g1, w_in, conv_w, g_conv, rc, rs1, rs2, *shards)
    return res[0], res[1], res[2], res[3], res[4], list(res[5:])


GROUP_COLS = GROUP * BLOCK
ATTN_STEP_BLOCKS = 4


def _attn_masks(has_prev):
    key = lax.broadcasted_iota(jnp.int32, (2 * BLOCK, GROUP_COLS), 0)
    query = lax.broadcasted_iota(jnp.int32, (2 * BLOCK, GROUP_COLS), 1) & (BLOCK - 1)
    band = (key > query) & (key <= query + BLOCK)
    return [band & ((key >= BLOCK) | has_prev)] + [band] * (ATTN_STEP_BLOCKS - 1)


def _heads_side_by_side(at, g, b):
    heads = [at[HEAD_DIM * (GROUP * g + hh):HEAD_DIM * (GROUP * g + hh + 1), BLOCK * b:BLOCK * (b + 1)] for hh in range(GROUP)]
    return jnp.concatenate(heads, axis=1)


def _to_token_rows(parts):
    rows = [jnp.concatenate([parts[b][g][:, BLOCK * hh:BLOCK * (hh + 1)] for b in range(ATTN_STEP_BLOCKS)], axis=1)
            for g in range(N_KV) for hh in range(GROUP)]
    return jnp.concatenate(rows, axis=0).T


def _group_sinks(sink_ref, g):
    head = lax.broadcasted_iota(jnp.int32, (1, GROUP_COLS), 1) // BLOCK
    out = jnp.full((1, GROUP_COLS), sink_ref[0, GROUP * g], F32)
    for hh in range(1, GROUP):
        out = jnp.where(head == hh, sink_ref[0, GROUP * g + hh], out)
    return out


def _attn_probs(qt, kk, sink, valid):
    s = jnp.where(valid, _mm(kk, qt), NEG_INF)
    m = jnp.maximum(jnp.max(s, axis=0, keepdims=True), sink)
    p = jnp.exp(s - m)
    psink = jnp.exp(sink - m)
    inv_l = 1.0 / (jnp.sum(p, axis=0, keepdims=True) + psink)
    return p * inv_l, psink * inv_l


ATTN_STEP = ATTN_STEP_BLOCKS * BLOCK
ATTN_KEYS = ATTN_STEP + BLOCK


def _qkv_specs(order):
    prev = lambda i: jnp.maximum(ATTN_STEP_BLOCKS * order(i) - 1, 0)
    kcol, vcol = ATTN_W // KV_W, ATTN_W // KV_W + 1
    return [pl.BlockSpec((ATTN_STEP, ATTN_W), lambda i: (order(i), 0)),
            pl.BlockSpec((BLOCK, KV_W), lambda i: (prev(i), kcol)), pl.BlockSpec((ATTN_STEP, KV_W), lambda i: (order(i), kcol)),
            pl.BlockSpec((BLOCK, KV_W), lambda i: (prev(i), vcol)), pl.BlockSpec((ATTN_STEP, KV_W), lambda i: (order(i), vcol))]


def _attn_fwd(qkv, sinks, g_attn, shards, gathered):
    t = qkv.shape[0]
    n = len(shards)

    def body(*refs):
        sink_ref, q_ref, kp_ref, kc_ref, vp_ref, vc_ref, ga_ref = refs[:7]
        attn_ref, mattn_ref = refs[7 + 2 * n:9 + 2 * n]
        step = pl.program_id(0)
        if n:
            @pl.when(step == 0)
            def _():
                x, y, c = _mesh_pos()
                _enter_with([(x, y, 1 - c), (*_relay_route(x, y, c)[1], c)])

            n_steps = t // ATTN_STEP
            _gather_far(step == 0, step == n_steps // 2, step == n_steps - 1, refs[7:7 + n], refs[7 + n:7 + 2 * n],
                        refs[9 + 2 * n:9 + 3 * n], refs[9 + 3 * n:11 + 3 * n])
        qt = (q_ref[...] * ATTN_SCALE).T
        keys = jnp.concatenate([kp_ref[...], kc_ref[...]], axis=0)
        vals = jnp.concatenate([vp_ref[...], vc_ref[...]], axis=0)
        sink = [_group_sinks(sink_ref, g) for g in range(N_KV)]
        masks = _attn_masks(step > 0)
        parts = []
        for b in range(ATTN_STEP_BLOCKS):
            window = slice(BLOCK * b, BLOCK * (b + 2))
            valid = masks[b]
            parts.append([])
            for g in range(N_KV):
                gs = slice(HEAD_DIM * g, HEAD_DIM * (g + 1))
                probs, _ = _attn_probs(_heads_side_by_side(qt, g, b), keys[window, gs], sink[g], valid)
                parts[b].append(_mm_tn(vals[window, gs], probs.astype(BF16)))
        attn = _to_token_rows(parts)
        attn_ref[...] = attn
        mattn_ref[...] = ((attn * _inv_rms(attn)) * ga_ref[...]).astype(BF16)

    blk = pl.BlockSpec((ATTN_STEP, ATTN_W), lambda j: (j, 0))
    res = pl.pallas_call(
        body, name="attn_fwd", grid=(t // ATTN_STEP,),
        in_specs=[pl.BlockSpec(memory_space=pltpu.SMEM)] + _qkv_specs(lambda j: j) + [_full((1, ATTN_W))]
        + [HBM_SPEC] * (2 * n),
        out_specs=[blk, blk] + [HBM_SPEC] * n,
        out_shape=[jax.ShapeDtypeStruct((t, ATTN_W), F32), jax.ShapeDtypeStruct((t, ATTN_W), BF16)]
        + [jax.ShapeDtypeStruct(g.shape, g.dtype) for g in gathered],
        input_output_aliases={7 + n + i: 2 + i for i in range(n)},
        scratch_shapes=[pltpu.SemaphoreType.DMA((4 * n,)), pltpu.SemaphoreType.DMA((4 * n,))] if n else [],
        compiler_params=_params("arbitrary", barrier_id=1 if n else None),
    )(sinks, qkv, qkv, qkv, qkv, qkv, g_attn, *shards, *gathered)
    return res[0], res[1], list(res[2:])


SMALL_ROWS = 8
ROW_LOSS, ROW_G2, ROW_G3, ROW_G4 = 0, 1, 2, 3


def _mid(mattn, mconv, x, target, g2, g3, g4, w_out, w_up, w_down, tm):
    t = x.shape[0]

    def body(ma_ref, mc_ref, x_ref, t_ref, g2_ref, g3_ref, g4_ref, wo_ref, wu_ref, wd_ref,
             act_ref, dup_ref, hn2t_ref, dmo_ref, dmix_ref, dh_ref, dmixed_ref, small_ref, up_ref):
        @pl.when(pl.program_id(0) == 0)
        def _():
            small_ref[...] = jnp.zeros_like(small_ref)

        g2, g3, g4 = g2_ref[...], g3_ref[...], g4_ref[...]
        mix_out = _mm(ma_ref[...], wo_ref[0:ATTN_W, :]) + _mm(mc_ref[...], wo_ref[ATTN_W:, :])
        r2 = _inv_rms(mix_out)
        mo_hat = mix_out * r2
        h = x_ref[...] + mo_hat * g2
        r3 = _inv_rms(h)
        h_hat = h * r3
        hn2 = (h_hat * g3).astype(BF16)
        hn2t_ref[...] = hn2.T
        for j in range(MID_CHUNKS):
            cols_j = slice(MID_CHUNK * j, MID_CHUNK * (j + 1))
            up = jnp.maximum(_mm(hn2, wu_ref[:, cols_j]), 0.0)
            up_ref[:, cols_j] = up.astype(BF16)
            act_ref[:, cols_j] = (up * up).astype(BF16)
        mlp = _mm(act_ref[...], wd_ref[...])
        r4 = _inv_rms(mlp)
        ml_hat = mlp * r4
        err = (h + ml_hat * g4) - t_ref[...]
        d_out = err * (1.0 / D_MODEL)
        d_mlp, dg4 = _rms_bwd(ml_hat, r4, g4, d_out)
        dmo = d_mlp.astype(BF16)
        dmo_ref[...] = dmo
        for j in range(MID_CHUNKS):
            cols_j = slice(MID_CHUNK * j, MID_CHUNK * (j + 1))
            dact = _mm_nt(dmo, wd_ref[cols_j, :])
            dup_ref[:, cols_j] = (dact * (2.0 * up_ref[:, cols_j].astype(F32))).astype(BF16)
        dhn2 = _mm_nt(dup_ref[...], wu_ref[...])
        dh_norm, dg3 = _rms_bwd(h_hat, r3, g3, dhn2)
        dh = d_out + dh_norm
        dh_ref[...] = dh
        d_mix, dg2 = _rms_bwd(mo_hat, r2, g2, dh)
        dmix = d_mix.astype(BF16)
        dmix_ref[...] = dmix
        dmixed_ref[...] = _mm_nt(dmix, wo_ref[...])
        small_ref[ROW_LOSS:ROW_LOSS + 1, :] += _colsum(err * err)
        small_ref[ROW_G2:ROW_G2 + 1, :] += _colsum(dg2)
        small_ref[ROW_G3:ROW_G3 + 1, :] += _colsum(dg3)
        small_ref[ROW_G4:ROW_G4 + 1, :] += _colsum(dg4)

    tile = lambda n: pl.BlockSpec((tm, n), lambda i: (i, 0))
    cols = lambda n: pl.BlockSpec((n, tm), lambda i: (0, i))
    gain = _full((1, D_MODEL))
    return pl.pallas_call(
        body, name="mid_fwd_bwd", grid=(t // tm,),
        in_specs=[tile(ATTN_W), tile(CONV_W), tile(D_MODEL), tile(D_MODEL), gain, gain, gain,
                  _resident((D_MODEL, D_MODEL)), _resident((D_MODEL, D_FF)), _resident((D_FF, D_MODEL))],
        out_specs=[tile(D_FF), tile(D_FF), cols(D_MODEL), tile(D_MODEL), tile(D_MODEL), tile(D_MODEL), tile(D_MODEL),
                   _full((SMALL_ROWS, D_MODEL))],
        out_shape=[jax.ShapeDtypeStruct((t, D_FF), BF16), jax.ShapeDtypeStruct((t, D_FF), BF16),
                   jax.ShapeDtypeStruct((D_MODEL, t), BF16), jax.ShapeDtypeStruct((t, D_MODEL), BF16),
                   jax.ShapeDtypeStruct((t, D_MODEL), BF16), jax.ShapeDtypeStruct((t, D_MODEL), F32),
                   jax.ShapeDtypeStruct((t, D_MODEL), F32), jax.ShapeDtypeStruct((SMALL_ROWS, D_MODEL), F32)],
        scratch_shapes=[pltpu.VMEM((tm, D_FF), BF16)],
        compiler_params=_params("arbitrary"),
    )(mattn, mconv, x, target, g2, g3, g4, w_out, w_up, w_down)


CHIP_FLIPS = ((1, 1), (1, 0), (0, 1))


def _block_order(dev):
    chip_masks = [4 * fx + 2 * fy for fx, fy in CHIP_FLIPS]
    masks = [m + 1 for m in chip_masks] + [1] + chip_masks + [0]
    return jnp.bitwise_xor(dev, jnp.asarray(masks, jnp.int32)).astype(jnp.int32)


def _other_chips(x, y, c):
    return [(1 - x if fx else x, 1 - y if fy else y, c) for fx, fy in CHIP_FLIPS]


def _dw_pair_sums(operands, order, which, name, barrier_id, ride=None):
    t = operands[-1].shape[0]
    n_far = len(CHIP_FLIPS)
    n_in = len(operands)
    n_ride = 0 if ride is None else 1
    out_chunk = D_MODEL // N_DEV
    if which == "up":
        rows, cols = D_MODEL, FF_CHUNK
        in_specs = [_resident((D_MODEL, t)), pl.BlockSpec((t, FF_CHUNK), lambda s, order_ref: (0, order_ref[s]))]
    elif which == "down":
        rows, cols = FF_CHUNK, D_MODEL
        in_specs = [pl.BlockSpec((t, FF_CHUNK), lambda s, order_ref: (0, order_ref[s])), _resident((t, D_MODEL))]
    else:
        rows, cols = out_chunk, D_MODEL
        half = pl.BlockSpec((t, out_chunk), lambda s, order_ref: (0, order_ref[s] % (N_DEV // 2)))
        in_specs = [half, half, _resident((t, D_MODEL))]

    def body(order_ref, *refs):
        own_ref, from_sib_ref, pair_ref = refs[n_in + n_ride:n_in + n_ride + 3]
        send_buf, land_buf, send_sems, recv_sems = refs[n_in + 2 * n_ride + 3:n_in + 2 * n_ride + 7]
        s_now = pl.program_id(0)
        x, y, c = _mesh_pos()
        sibling = (x, y, 1 - c)
        sems = (send_sems, recv_sems)

        @pl.when(s_now == 0)
        def _():
            _enter_with([sibling] + (_other_chips(x, y, c) if n_ride else []))

        if n_ride:
            _chip_exchange_beside(s_now == 0, s_now == N_DEV - 1, [refs[n_in]], [refs[n_in + 3 + n_ride]],
                                  refs[n_in + 2 * n_ride + 7:], enter=False)

        def hand_over(k):
            dst = land_buf.at[k] if k < n_far else from_sib_ref
            return _push(send_buf.at[k], dst, sems, k, sibling)

        if which == "out":
            ma_ref, mc_ref, b_ref = refs[:n_in]
            block = lax.cond(order_ref[s_now] < N_DEV // 2, lambda: _mm_tn(ma_ref[...], b_ref[...]),
                             lambda: _mm_tn(mc_ref[...], b_ref[...]))
        elif which == "down":
            block = _mm_tn(refs[0][...], refs[1][...])
        else:
            block = _mm(refs[0][...], refs[1][...])
        for k in range(n_far + 1):
            @pl.when(s_now == k)
            def _():
                send_buf[k] = block.astype(BF16)
                hand_over(k).start()

        for k in range(n_far):
            @pl.when(s_now == n_far + 1 + k)
            def _():
                hand_over(k).wait_recv()
                pair_ref[...] = (block + land_buf[k].astype(F32)).astype(BF16)

        @pl.when(s_now == N_DEV - 1)
        def _():
            own_ref[...] = block
            for k in range(n_far + 1):
                hand_over(k).wait_send()
            hand_over(n_far).wait_recv()

    rides = [] if ride is None else [ride]
    sems = lambda k: pltpu.SemaphoreType.DMA((k,))
    return pl.pallas_call(
        body, name=name,
        grid_spec=pltpu.PrefetchScalarGridSpec(
            num_scalar_prefetch=1, grid=(N_DEV,), in_specs=in_specs + [HBM_SPEC] * n_ride,
            out_specs=[pl.BlockSpec((rows, cols), lambda s, order_ref: (0, 0)), HBM_SPEC,
                       pl.BlockSpec((None, rows, cols), lambda s, order_ref: (jnp.clip(s - n_far - 1, 0, n_far - 1), 0, 0))]
            + [HBM_SPEC] * n_ride,
            scratch_shapes=[pltpu.VMEM((n_far + 1, rows, cols), BF16), pltpu.VMEM((n_far, rows, cols), BF16),
                            sems(n_far + 1), sems(n_far + 1)] + [sems(n_far), sems(n_far)] * n_ride),
        out_shape=[jax.ShapeDtypeStruct((rows, cols), F32), jax.ShapeDtypeStruct((rows, cols), BF16),
                   jax.ShapeDtypeStruct((n_far, rows, cols), BF16)]
        + [jax.ShapeDtypeStruct(r.shape, r.dtype) for r in rides],
        compiler_params=_params("arbitrary", barrier_id=barrier_id),
    )(order, *operands, *rides)


def _chip_exchange_beside(first, last, sums, outs, sems, enter=True):
    chips = _other_chips(*_mesh_pos())
    copies = [_push(sums[i].at[k], outs[i].at[k], sems, len(chips) * i + k, chip)
              for i in range(len(sums)) for k, chip in enumerate(chips)]

    @pl.when(first)
    def _():
        if enter:
            _enter_with(chips)
        for cp in copies:
            cp.start()

    @pl.when(last)
    def _():
        for cp in copies:
            cp.wait()


ROW_GCONV, ROW_CW0 = 1, 2


def _conv_bwd(dmixed, gates, g_conv, conv_w, tm):
    t = gates.shape[0]
    n = t // tm
    rev = lambda i: n - 1 - i

    def body(dm_ref, gates_ref, gprev_ref, gc_ref, cw_ref, dgates_ref, small_ref, carry_ref):
        i = pl.program_id(0)

        @pl.when(i == 0)
        def _():
            small_ref[...] = jnp.zeros_like(small_ref)
            carry_ref[...] = jnp.zeros_like(carry_ref)

        gates = gates_ref[...]
        gb, gcc, xin = gates[:, :CONV_W], gates[:, CONV_W:2 * CONV_W], gates[:, 2 * CONV_W:]
        u = gcc * xin
        gp = gprev_ref[...]
        uprev = jnp.where(rev(i) == 0, 0.0, gp[:, CONV_W:2 * CONV_W] * gp[:, 2 * CONV_W:])
        u1, u2 = _shift_rows_down(u, uprev, 1), _shift_rows_down(u, uprev, 2)
        w = cw_ref[...]
        c = _conv3(u, u1, u2, w)
        conv = gb * c
        rcv = _inv_rms(conv)
        c_hat = conv * rcv
        dconv, dgc = _rms_bwd(c_hat, rcv, gc_ref[...], dm_ref[...])
        dc = dconv * gb
        nxt = carry_ref[...]
        du = (w[2:3, :] * dc + w[1:2, :] * _shift_rows_up(dc, nxt, 1)) + w[0:1, :] * _shift_rows_up(dc, nxt, 2)
        carry_ref[...] = dc[0:8, :]
        dgates_ref[:, :CONV_W] = (dconv * c).astype(BF16)
        dgates_ref[:, CONV_W:2 * CONV_W] = (du * xin).astype(BF16)
        dgates_ref[:, 2 * CONV_W:] = (du * gcc).astype(BF16)
        small_ref[ROW_GCONV:ROW_GCONV + 1, :] += _colsum(dgc)
        small_ref[ROW_CW0:ROW_CW0 + 1, :] += _colsum(dc * u2)
        small_ref[ROW_CW0 + 1:ROW_CW0 + 2, :] += _colsum(dc * u1)
        small_ref[ROW_CW0 + 2:ROW_CW0 + 3, :] += _colsum(dc * u)

    tile = lambda w_: pl.BlockSpec((tm, w_), lambda i: (rev(i), 0))
    prev8 = pl.BlockSpec((8, GATES_W), lambda i: (jnp.maximum(rev(i) * (tm // 8) - 1, 0), 0))
    conv_half = pl.BlockSpec((tm, CONV_W), lambda i: (rev(i), ATTN_W // CONV_W))
    return pl.pallas_call(
        body, name="conv_bwd", grid=(n,),
        in_specs=[conv_half, tile(GATES_W), prev8, _full((1, CONV_W)), _full((3, CONV_W))],
        out_specs=[tile(GATES_W), _full((SMALL_ROWS, CONV_W))],
        out_shape=[jax.ShapeDtypeStruct((t, GATES_W), BF16), jax.ShapeDtypeStruct((SMALL_ROWS, CONV_W), F32)],
        scratch_shapes=[pltpu.VMEM((8, CONV_W), F32)],
        compiler_params=_params("arbitrary"),
    )(dmixed, gates, gates, g_conv, conv_w)


def _attn_bwd(qkv, dmixed, attn, g_attn, sinks, rope, sums):
    t = qkv.shape[0]
    n_steps = t // ATTN_STEP
    rev = lambda i: n_steps - 1 - i
    rc, rs1, rs2 = rope

    def body(sink_ref, q_ref, kp_ref, kc_ref, vp_ref, vc_ref, dm_ref, attn_ref, ga_ref, c_ref, s1_ref, s2_ref, sums_ref,
             dqkv_ref, dsink_ref, dgain_ref, arrived_ref, ck_ref, cv_ref, kacc_ref, vacc_ref, send_sems, recv_sems):
        i = pl.program_id(0)
        _chip_exchange_beside(i == 0, i == n_steps - 1, [sums_ref], [arrived_ref], (send_sems, recv_sems))

        @pl.when(i == 0)
        def _():
            dsink_ref[...] = jnp.zeros_like(dsink_ref)
            dgain_ref[...] = jnp.zeros_like(dgain_ref)
            ck_ref[...] = jnp.zeros_like(ck_ref)
            cv_ref[...] = jnp.zeros_like(cv_ref)

        kacc_ref[...] = jnp.zeros_like(kacc_ref)
        vacc_ref[...] = jnp.zeros_like(vacc_ref)
        a = attn_ref[...]
        ra = _inv_rms(a)
        dattn, dgain = _rms_bwd(a * ra, ra, ga_ref[...], dm_ref[...])
        dgain_ref[0:1, :] += _colsum(dgain)
        qt = (q_ref[...] * ATTN_SCALE).T
        dot = dattn.astype(BF16).T
        keys = jnp.concatenate([kp_ref[...], kc_ref[...]], axis=0)
        vals = jnp.concatenate([vp_ref[...], vc_ref[...]], axis=0)
        sink = [_group_sinks(sink_ref, g) for g in range(N_KV)]
        c, s1, s2 = c_ref[...], s1_ref[...], s2_ref[...]
        lane = lax.broadcasted_iota(jnp.int32, (1, 128), 1)
        dsink = jnp.zeros((1, 128), F32)
        masks = _attn_masks(rev(i) > 0)
        dq_parts = []
        for b in range(ATTN_STEP_BLOCKS):
            window = slice(BLOCK * b, BLOCK * (b + 2))
            valid = masks[b]
            dq_parts.append([])
            dk_parts, dv_parts = [], []
            for g in range(N_KV):
                gs = slice(HEAD_DIM * g, HEAD_DIM * (g + 1))
                kk, vv = keys[window, gs], vals[window, gs]
                qtg, dotg = _heads_side_by_side(qt, g, b), _heads_side_by_side(dot, g, b)
                probs, psink = _attn_probs(qtg, kk, sink[g], valid)
                dp = _mm(vv, dotg)
                delta = jnp.sum(probs * dp, axis=0, keepdims=True)
                ds = (probs * (dp - delta)).astype(BF16)
                sink_terms = psink * delta
                for hh in range(GROUP):
                    head_sum = jnp.sum(sink_terms[:, BLOCK * hh:BLOCK * (hh + 1)])
                    dsink = dsink + jnp.where(lane == GROUP * g + hh, -head_sum, 0.0)
                dq_parts[b].append(_mm_tn(kk * ATTN_SCALE, ds))
                dk_parts.append(_mm_nt(ds, qtg))
                dv_parts.append(_mm_nt(probs.astype(BF16), dotg))
            kacc_ref[window, :] += jnp.concatenate(dk_parts, axis=1)
            vacc_ref[window, :] += jnp.concatenate(dv_parts, axis=1)
        dq = _to_token_rows(dq_parts)
        for ci in range(ATTN_W // 128):
            sl = slice(128 * ci, 128 * (ci + 1))
            dqkv_ref[:, sl] = _rope_transpose(dq[:, sl], c, s1, s2).astype(BF16)
        kacc_ref[ATTN_STEP:, :] += ck_ref[...]
        vacc_ref[ATTN_STEP:, :] += cv_ref[...]
        ck_ref[...] = kacc_ref[:BLOCK, :]
        cv_ref[...] = vacc_ref[:BLOCK, :]
        dqkv_ref[:, ATTN_W:ATTN_W + KV_W] = _rope_transpose(kacc_ref[BLOCK:, :], c, s1, s2).astype(BF16)
        dqkv_ref[:, ATTN_W + KV_W:] = vacc_ref[BLOCK:, :].astype(BF16)
        dsink_ref[0:1, :] += dsink

    blk = lambda w_: pl.BlockSpec((ATTN_STEP, w_), lambda i: (rev(i), 0))
    return pl.pallas_call(
        body, name="attn_bwd", grid=(n_steps,),
        in_specs=[pl.BlockSpec(memory_space=pltpu.SMEM)] + _qkv_specs(rev)
        + [blk(ATTN_W), blk(ATTN_W), _full((1, ATTN_W)), blk(128), blk(128), blk(128), HBM_SPEC],
        out_specs=[blk(QKV_W), _full((8, 128)), _full((SMALL_ROWS, ATTN_W)), HBM_SPEC],
        out_shape=[jax.ShapeDtypeStruct((t, QKV_W), BF16), jax.ShapeDtypeStruct((8, 128), F32),
                   jax.ShapeDtypeStruct((SMALL_ROWS, ATTN_W), F32), jax.ShapeDtypeStruct(sums.shape, sums.dtype)],
        scratch_shapes=[pltpu.VMEM((BLOCK, KV_W), F32), pltpu.VMEM((BLOCK, KV_W), F32),
                        pltpu.VMEM((ATTN_KEYS, KV_W), F32), pltpu.VMEM((ATTN_KEYS, KV_W), F32),
                        pltpu.SemaphoreType.DMA((len(CHIP_FLIPS),)), pltpu.SemaphoreType.DMA((len(CHIP_FLIPS),))],
        compiler_params=_params("arbitrary", barrier_id=6),
    )(sinks, qkv, qkv, qkv, qkv, qkv, dmixed, attn, g_attn, rc, rs1, rs2, sums)


def _grad_x_tile(dq, dg, x_hat, r, g1, w_ref, dh):
    dhn = _mm(dq, w_ref[:QKV_W, :]) + _mm(dg, w_ref[QKV_W:, :])
    dx, dg1 = _rms_bwd(x_hat, r, g1, dhn)
    return dh + dx, _colsum(dg1)


def _in_proj_bwd(dqkv, dgates, x, dh, g1, w_in, tm, out_sums):
    t = x.shape[0]
    n = t // tm
    n_cover = max(n // 2, 1)
    n_steps = n + n_cover
    n_far = len(CHIP_FLIPS)
    shard = (IN_SHARD, D_MODEL)

    def body(dq_ref, dg_ref, x_ref, dh_ref, g1_ref, w_ref, osums_ref,
             dx_ref, own_ref, sib_ref, far_ref, dg1_ref, oarrived_ref,
             acc_ref, send_buf, land_buf, pair_buf, d2d_send, d2d_recv, ici_send, ici_recv, o_send, o_recv):
        i = pl.program_id(0)
        x_pos, y_pos, c = _mesh_pos()
        my_chip = 2 * x_pos + y_pos
        sibling = (x_pos, y_pos, 1 - c)
        @pl.when(i == 0)
        def _():
            _enter_with(_sibling_and_chips(x_pos, y_pos, c))

        _chip_exchange_beside(i == 0, i == n_steps - 1, [osums_ref], [oarrived_ref], (o_send, o_recv), enter=False)

        def rows(d):
            return slice(IN_SHARD * d, IN_SHARD * (d + 1))

        def hand_over(chip):
            return _push(send_buf.at[chip], land_buf.at[chip], (d2d_send, d2d_recv), chip, sibling)

        def to_chip(chip, rel):
            return pltpu.make_async_remote_copy(
                src_ref=pair_buf.at[chip], dst_ref=far_ref.at[rel - 1], send_sem=ici_send.at[rel - 1],
                recv_sem=ici_recv.at[rel - 1], device_id=(chip // 2, chip % 2, c), device_id_type=MESH)

        @pl.when(i == 0)
        def _():
            acc_ref[...] = jnp.zeros_like(acc_ref)
            dg1_ref[...] = jnp.zeros_like(dg1_ref)

        def normed_x():
            xv = x_ref[...]
            r = _inv_rms(xv)
            return xv * r, r

        @pl.when(i < n)
        def _():
            hn = (normed_x()[0] * g1_ref[...]).astype(BF16)
            acc_ref[:QKV_W, :] += _mm_tn(dq_ref[...], hn)
            acc_ref[QKV_W:, :] += _mm_tn(dg_ref[...], hn)

        @pl.when(i == n - 1)
        def _():
            for d in range(N_DEV):
                @pl.when(d % 2 != c)
                def _():
                    send_buf[d // 2] = acc_ref[rows(d), :].astype(BF16)
                    hand_over(d // 2).start()
            for d in range(N_DEV):
                chip = d // 2

                @pl.when(d % 2 == c)
                def _():
                    hand_over(chip).wait_recv()

                    @pl.when(chip == my_chip)
                    def _():
                        own_ref[...] = acc_ref[rows(d), :]
                        sib_ref[...] = land_buf[chip]

                    @pl.when(chip != my_chip)
                    def _():
                        pair_buf[chip] = (acc_ref[rows(d), :] + land_buf[chip].astype(F32)).astype(BF16)
                        to_chip(chip, chip ^ my_chip).start()
            for chip in range(N_CHIPS):
                hand_over(chip).wait_send()

        @pl.when(i >= n)
        def _():
            x_hat, r = normed_x()
            dx_ref[...], dg1 = _grad_x_tile(dq_ref[...], dg_ref[...], x_hat, r, g1_ref[...], w_ref, dh_ref[...])
            dg1_ref[0:1, :] += dg1

        @pl.when(i == n_steps - 1)
        def _():
            for rel in range(1, n_far + 1):
                to_chip(0, rel).wait()

    both = lambda w_: pl.BlockSpec((tm, w_), lambda i: (i % n, 0))
    second = pl.BlockSpec((tm, D_MODEL), lambda i: (jnp.maximum(i - n, 0), 0))
    whole = lambda dtype: jax.ShapeDtypeStruct(shard, dtype)
    sems = lambda k: pltpu.SemaphoreType.DMA((k,))
    res = pl.pallas_call(
        body, name="in_proj_bwd", grid=(n_steps,),
        in_specs=[both(QKV_W), both(GATES_W), both(D_MODEL), second, _full((1, D_MODEL)), _resident((IN_COLS, D_MODEL)),
                  HBM_SPEC],
        out_specs=[second, _full(shard), _full(shard), HBM_SPEC, _full((SMALL_ROWS, D_MODEL)), HBM_SPEC],
        out_shape=[jax.ShapeDtypeStruct((n_cover * tm, D_MODEL), F32), whole(F32), whole(BF16),
                   jax.ShapeDtypeStruct((n_far,) + shard, BF16), jax.ShapeDtypeStruct((SMALL_ROWS, D_MODEL), F32),
                   jax.ShapeDtypeStruct(out_sums.shape, out_sums.dtype)],
        scratch_shapes=[pltpu.VMEM((IN_COLS, D_MODEL), F32), pltpu.VMEM((N_CHIPS,) + shard, BF16),
                        pltpu.VMEM((N_CHIPS,) + shard, BF16), pltpu.VMEM((N_CHIPS,) + shard, BF16),
                        sems(N_CHIPS), sems(N_CHIPS), sems(n_far), sems(n_far), sems(n_far), sems(n_far)],
        compiler_params=_params("arbitrary", barrier_id=7),
    )(dqkv, dgates, x, dh, g1, w_in, out_sums)
    return res[0], (res[1], res[2], res[3]), res[4], res[5]


def _grad_x_rest(dqkv, dgates, x, dh, g1, w_in, tm, head, dg1_rows):
    t = x.shape[0]
    first = head.shape[0] // tm
    n_rest = t // tm - first
    if n_rest == 0:
        return head, dg1_rows
    assert first <= n_rest

    def body(dq_ref, dg_ref, x_ref, dh_ref, g1_ref, w_ref, head_ref, rows_ref, gx_ref, dg1_ref, stage, sems):
        j = pl.program_id(0)

        def tile_out(step, kind):
            row0 = (step + first) * tm if kind == 0 else step * tm
            slot = 2 * kind + step % 2
            return pltpu.make_async_copy(stage.at[slot], gx_ref.at[pl.ds(pl.multiple_of(row0, tm), tm), :], sems.at[slot])

        @pl.when(j == 0)
        def _():
            dg1_ref[...] = rows_ref[...]

        @pl.when(j >= 2)
        def _():
            tile_out(j - 2, 0).wait()

        @pl.when((j >= 2) & (j - 2 < first))
        def _():
            tile_out(j - 2, 1).wait()

        @pl.when(j < first)
        def _():
            stage[2 + j % 2] = head_ref[...]
            tile_out(j, 1).start()

        xv = x_ref[...]
        r = _inv_rms(xv)
        dx, dg1 = _grad_x_tile(dq_ref[...], dg_ref[...], xv * r, r, g1_ref[...], w_ref, dh_ref[...])
        stage[j % 2] = dx
        dg1_ref[0:1, :] += dg1
        tile_out(j, 0).start()

        @pl.when(j == n_rest - 1)
        def _():
            for back in range(min(2, n_rest)):
                tile_out(j - back, 0).wait()

                @pl.when(j - back < first)
                def _():
                    tile_out(j - back, 1).wait()

    tile = lambda w_: pl.BlockSpec((tm, w_), lambda j: (j + first, 0))
    head_tile = pl.BlockSpec((tm, D_MODEL), lambda j: (jnp.minimum(j, first - 1), 0))
    return pl.pallas_call(
        body, name="grad_x_rest", grid=(n_rest,),
        in_specs=[tile(QKV_W), tile(GATES_W), tile(D_MODEL), tile(D_MODEL), _full((1, D_MODEL)),
                  _resident((IN_COLS, D_MODEL)), head_tile, _full((SMALL_ROWS, D_MODEL))],
        out_specs=[HBM_SPEC, _full((SMALL_ROWS, D_MODEL))],
        out_shape=[jax.ShapeDtypeStruct((t, D_MODEL), F32), jax.ShapeDtypeStruct((SMALL_ROWS, D_MODEL), F32)],
        scratch_shapes=[pltpu.VMEM((4, tm, D_MODEL), F32), pltpu.SemaphoreType.DMA((4,))],
        compiler_params=_params("arbitrary"),
    )(dqkv, dgates, x, dh, g1, w_in, head, dg1_rows)


def _all_gather(shards, name):
    n = len(shards)

    def body(*refs):
        _enter_with(_sibling_and_chips(*_mesh_pos()))
        start, finish = _gather_steps(refs[:n], refs[n:2 * n], *refs[2 * n:])
        start()
        finish()

    return pl.pallas_call(
        body, name=name,
        in_specs=[HBM_SPEC] * n, out_specs=[HBM_SPEC] * n,
        out_shape=[jax.ShapeDtypeStruct((N_DEV,) + s.shape, s.dtype) for s in shards],
        scratch_shapes=[pltpu.SemaphoreType.DMA((7 * n,)), pltpu.SemaphoreType.DMA((7 * n,)),
                        pltpu.SemaphoreType.DMA((n,))],
        compiler_params=_params(barrier_id=8),
    )(*shards)


def _adam_math(w, g, m, v):
    m = ADAM_B1 * m + (1.0 - ADAM_B1) * g
    v = ADAM_B2 * v + (1.0 - ADAM_B2) * (g * g)
    m_hat = m / (1.0 - ADAM_B1 ** ADAM_STEP)
    v_hat = v / (1.0 - ADAM_B2 ** ADAM_STEP)
    delta = -ADAM_LR * (m_hat / (jnp.sqrt(v_hat) + ADAM_EPS) + ADAM_WD * w)
    return delta, m, v


def _adamw_reduced(tensors, n_steps):
    n_far = len(CHIP_FLIPS)

    def body(*refs):
        ins, outs = refs[:6 * len(tensors)], refs[6 * len(tensors):]
        for k in range(len(tensors)):
            w_ref, m_ref, v_ref, own_ref, sib_ref, far_ref = ins[6 * k:6 * k + 6]
            g_ref, d_ref, nm_ref, nv_ref = outs[4 * k:4 * k + 4]
            g = own_ref[...] + sib_ref[...].astype(F32)
            for j in range(n_far):
                g = g + far_ref[j].astype(F32)
            g_ref[...] = g
            d_ref[...], nm_ref[...], nv_ref[...] = _adam_math(w_ref[...], g, m_ref[...], v_ref[...])

    in_specs, out_specs, out_shape = [], [], []
    for w, *_ in tensors:
        rows, cols = w.shape
        tile = pl.BlockSpec((rows // n_steps, cols), lambda i: (i, 0))
        in_specs += [tile] * 5 + [pl.BlockSpec((n_far, rows // n_steps, cols), lambda i: (0, i, 0))]
        out_specs += [tile] * 4
        out_shape += [jax.ShapeDtypeStruct((rows, cols), F32)] * 4
    res = pl.pallas_call(
        body, name="adamw_reduced", grid=(n_steps,), in_specs=in_specs, out_specs=out_specs, out_shape=out_shape,
        compiler_params=_params("parallel"),
    )(*[a for tensor in tensors for a in tensor])
    return [res[4 * k:4 * k + 4] for k in range(len(tensors))]


SMALL_PARAMS = ("pre_mix_norm", "post_mix_norm", "pre_mlp_norm", "post_mlp_norm", "attn_group_norm", "conv_group_norm",
                "conv_w", "attn_sinks")


SMALL_WIDTHS = (D_MODEL, CONV_W, ATTN_W, 128, D_MODEL)


def _small_tail(gathered, dev, weights, first_moments, second_moments):
    n = len(SMALL_PARAMS)
    conv_shard = CONV_W // N_DEV

    def body(dev_ref, sums_ref, *refs):
        w_refs, m_refs, v_refs = refs[:n], refs[n:2 * n], refs[2 * n:3 * n]
        loss_ref, outs = refs[3 * n], refs[3 * n + 1:]
        total = sums_ref[0]
        for d in range(1, N_DEV):
            total = total + sums_ref[d]
        starts = [sum(SMALL_WIDTHS[:i]) for i in range(len(SMALL_WIDTHS))]
        mid, conv, gain, sink, inp = (total[:, a:a + w_] for a, w_ in zip(starts, SMALL_WIDTHS))
        loss_ref[...] = (0.5 / D_MODEL) * jnp.sum(mid[ROW_LOSS:ROW_LOSS + 1, :], axis=1, keepdims=True)
        conv_rows = conv[ROW_CW0:ROW_CW0 + 3, :]
        conv_g = jnp.zeros((3, conv_shard), F32)
        for d in range(N_DEV):
            conv_g = conv_g + jnp.where(dev_ref[0] == d, conv_rows[:, conv_shard * d:conv_shard * (d + 1)], 0.0)
        grads = [inp[0:1, :], mid[ROW_G2:ROW_G2 + 1, :], mid[ROW_G3:ROW_G3 + 1, :], mid[ROW_G4:ROW_G4 + 1, :],
                 gain[0:1, :], conv[ROW_GCONV:ROW_GCONV + 1, :], conv_g, sink[0:1, :N_HEADS]]
        for i, g in enumerate(grads):
            parts = [(..., g)] if len(w_refs[i].shape) == 2 else [(r, g[r:r + 1, :]) for r in range(g.shape[0])]
            for at, g_at in parts:
                delta, new_m, new_v = _adam_math(w_refs[i][at], g_at, m_refs[i][at], v_refs[i][at])
                outs[i][at], outs[n + i][at], outs[2 * n + i][at], outs[3 * n + i][at] = g_at, delta, new_m, new_v

    params = list(weights) + list(first_moments) + list(second_moments)
    shapes = [jax.ShapeDtypeStruct(w.shape, F32) for w in weights]
    res = pl.pallas_call(
        body, name="small_tail", grid=(1,),
        in_specs=[pl.BlockSpec(memory_space=pltpu.SMEM), _full(gathered.shape)] + [_full(p.shape) for p in params],
        out_specs=[_full((1, 1))] + [_full(sh.shape) for sh in shapes] * 4,
        out_shape=[jax.ShapeDtypeStruct((1, 1), F32)] + shapes * 4,
    )(dev, gathered, *params)
    return res[0], [res[1 + k * n:1 + (k + 1) * n] for k in range(4)]


TOKEN_TILE = 512
MID_TILE = 256
MID_CHUNK = 1024
MID_CHUNKS = D_FF // MID_CHUNK
ADAM_STEPS = 2


def _local_grads(x, target, g1, w_in_shard, conv_shard, sinks, g_attn, g_conv, g2, g3, g4, shards, order):
    t = x.shape[0]
    tm = min(TOKEN_TILE, t)
    rope = _rope_tables(t)
    qkv, gates, mconv, w_in, conv_w, gathered = _in_proj_fwd(x, g1, w_in_shard, conv_shard, g_conv, rope, tm, shards,
                                                             (False, True, False))
    attn, mattn, (w_out, w_up, w_down) = _attn_fwd(qkv, sinks, g_attn, shards, gathered)
    act, dup, hn2t, dmo, dmix, dh, dmixed, small_mid = _mid(
        mattn, mconv, x, target, g2, g3, g4, w_out.reshape(D_MODEL, D_MODEL),
        w_up, w_down.reshape(D_FF, D_MODEL), min(MID_TILE, t))
    up_own, up_sib, up_sums = _dw_pair_sums((hn2t, dup), order, "up", "dw_up", 2)
    down_own, down_sib, down_sums, up_far = _dw_pair_sums((act, dmo), order, "down", "dw_down", 3, ride=up_sums)
    out_own, out_sib, out_sums = _dw_pair_sums((mattn, mconv, dmix), order, "out", "dw_out", 4)
    dgates, small_conv = _conv_bwd(dmixed, gates, g_conv, conv_w, tm)
    dqkv, dsink, dg_attn, down_far = _attn_bwd(qkv, dmixed, attn, g_attn, sinks, rope, down_sums)
    grad_x_head, dw_in, small_in, out_far = _in_proj_bwd(dqkv, dgates, x, dh, g1, w_in, tm, out_sums)
    grad_x, small_in = _grad_x_rest(dqkv, dgates, x, dh, g1, w_in, tm, grad_x_head, small_in)
    dw_out, dw_up, dw_down = (out_own, out_sib, out_far), (up_own, up_sib, up_far), (down_own, down_sib, down_far)
    return grad_x, dw_in, dw_out, dw_up, dw_down, (small_mid, small_conv, dg_attn, dsink, small_in)


def kernel(x, pre_mix_norm, w_in, conv_w, attn_sinks, attn_group_norm, conv_group_norm, w_out, post_mix_norm, pre_mlp_norm, w_up, w_down, post_mlp_norm, loss_target, m_pre_mix_norm, m_w_in, m_conv_w, m_attn_sinks, m_attn_group_norm, m_conv_group_norm, m_w_out, m_post_mix_norm, m_pre_mlp_norm, m_w_up, m_w_down, m_post_mlp_norm, v_pre_mix_norm, v_w_in, v_conv_w, v_attn_sinks, v_attn_group_norm, v_conv_group_norm, v_w_out, v_post_mix_norm, v_pre_mlp_norm, v_w_up, v_w_down, v_post_mlp_norm):
    xi, yi, ci = _mesh_pos()
    chip = 2 * xi + yi
    dev = 2 * chip + ci

    order = _block_order(dev)

    shards = [w_out[0].astype(BF16), w_up[0].astype(BF16), w_down[0].astype(BF16)]

    turned = lambda a: jnp.swapaxes(a, 1, 2)
    grad_x, dw_in, dw_out, dw_up, dw_down, smalls = _local_grads(
        x[0], loss_target[0], pre_mix_norm, turned(w_in)[0].astype(BF16), conv_w[0], attn_sinks, attn_group_norm, conv_group_norm,
        post_mix_norm, pre_mlp_norm, post_mlp_norm, shards, order)

    blocks = {"w_in": (turned(w_in), turned(m_w_in), turned(v_w_in), dw_in), "w_out": (w_out, m_w_out, v_w_out, dw_out),
              "w_up": (w_up, m_w_up, v_w_up, dw_up), "w_down": (w_down, m_w_down, v_w_down, dw_down)}
    big = {}
    for names in (("w_in", "w_out", "w_down"), ("w_up",)):
        stepped = _adamw_reduced([(w[0], m[0], v[0], *dw) for w, m, v, dw in (blocks[nm] for nm in names)], ADAM_STEPS)
        for nm, res in zip(names, stepped):
            big[nm] = [a[None] for a in res]
    big["w_in"] = [turned(a) for a in big["w_in"]]

    flat = lambda a: a.reshape(-1, a.shape[-1]) if a.ndim < 3 else a.reshape(a.shape[1], 1, a.shape[2])
    loss, small = _small_tail(
        _all_gather([jnp.concatenate(smalls, axis=1)], "gather_small")[0], dev.reshape(1).astype(jnp.int32),
        [flat(a) for a in (pre_mix_norm, post_mix_norm, pre_mlp_norm, post_mlp_norm, attn_group_norm, conv_group_norm,
                           conv_w, attn_sinks)],
        [flat(a) for a in (m_pre_mix_norm, m_post_mix_norm, m_pre_mlp_norm, m_post_mlp_norm, m_attn_group_norm,
                           m_conv_group_norm, m_conv_w, m_attn_sinks)],
        [flat(a) for a in (v_pre_mix_norm, v_post_mix_norm, v_pre_mlp_norm, v_post_mlp_norm, v_attn_group_norm,
                           v_conv_group_norm, v_conv_w, v_attn_sinks)])

    order = ("pre_mix_norm", "w_in", "conv_w", "attn_sinks", "attn_group_norm", "conv_group_norm", "w_out",
             "post_mix_norm", "pre_mlp_norm", "w_up", "w_down", "post_mlp_norm")
    shape_of = {"conv_w": conv_w.shape}
    outs = []
    for k in range(4):
        by_name = dict(zip(SMALL_PARAMS, small[k]))
        outs += [big[nm][k] if nm in big else by_name[nm].reshape(shape_of.get(nm, by_name[nm].shape)) for nm in order]
    loss = loss.reshape(())
    return (loss, grad_x[None], *outs)
```

```python
import jax
import jax.numpy as jnp
import numpy as np
from jax import lax
from jax.experimental import pallas as pl
from jax.experimental.pallas import tpu as pltpu

F32 = jnp.float32
BF16 = jnp.bfloat16

D_MODEL = 1024
HEAD_DIM = 64
ATTN_W = 512
CONV_W = 512
N_HEADS = 8
N_KV = 2
GROUP = 4
KV_W = 128
QKV_W = ATTN_W + 2 * KV_W
GATES_W = 3 * CONV_W
IN_COLS = QKV_W + GATES_W
D_FF = 4096
FF_CHUNK = 512
BLOCK = 128
ROT_HALF = 8
ROPE_THETA = 500000.0
NORM_EPS = 1e-6
NEG_INF = -1e30
ATTN_SCALE = 0.125
N_DEV = 8
N_CHIPS = 4
IN_SHARD = IN_COLS // N_DEV

ADAM_LR = 0.001
ADAM_B1 = 0.9
ADAM_B2 = 0.999
ADAM_EPS = 1e-08
ADAM_WD = 0.01
ADAM_STEP = 10

V7X_VMEM_BYTES = 64 * 1024 * 1024
VMEM_LIMIT = V7X_VMEM_BYTES - 2 * 1024 * 1024

MESH = pl.DeviceIdType.MESH
HBM_SPEC = pl.BlockSpec(memory_space=pltpu.HBM)


def _params(*sem, barrier_id=None):
    return pltpu.CompilerParams(dimension_semantics=sem or None, vmem_limit_bytes=VMEM_LIMIT, collective_id=barrier_id)


def _mm(a, b):
    return jnp.dot(a, b, preferred_element_type=F32)


def _mm_nt(a, b):
    return lax.dot_general(a, b, (((1,), (1,)), ((), ())), preferred_element_type=F32)


def _mm_tn(a, b):
    return lax.dot_general(a, b, (((0,), (0,)), ((), ())), preferred_element_type=F32)


def _inv_rms(x):
    return lax.rsqrt(jnp.mean(x * x, axis=-1, keepdims=True) + NORM_EPS)


def _rms_bwd(xhat, r, gain, dy):
    gy = dy * gain
    return r * (gy - xhat * jnp.mean(gy * xhat, axis=-1, keepdims=True)), dy * xhat


def _colsum(a):
    return jnp.sum(a, axis=0, keepdims=True)


def _full(shape):
    zeros = (0,) * len(shape)
    return pl.BlockSpec(shape, lambda *_: zeros)


def _resident(shape):
    zeros = (0,) * len(shape)
    return pl.BlockSpec(shape, lambda *_: zeros, pipeline_mode=pl.Buffered(1))


def _rope_tables(t):
    pos = np.arange(t, dtype=np.float32)
    inv_freq = (ROPE_THETA ** (-np.arange(0, 2 * ROT_HALF, 2, dtype=np.float64) / (2 * ROT_HALF))).astype(np.float32)
    ang = (pos[:, None] * inv_freq[None, :]).astype(np.float64)
    cos, sin = np.cos(ang).astype(np.float32), np.sin(ang).astype(np.float32)
    zeros8 = np.zeros((t, ROT_HALF), np.float32)
    rest = np.zeros((t, HEAD_DIM - 2 * ROT_HALF), np.float32)
    c_head = np.concatenate([cos, cos, rest + 1.0], axis=1)
    s1_head = np.concatenate([zeros8, sin, rest], axis=1)
    s2_head = np.concatenate([-sin, zeros8, rest], axis=1)
    two = lambda a: jnp.asarray(np.concatenate([a, a], axis=1))
    return two(c_head), two(s1_head), two(s2_head)


def _rope(v, c, s1, s2):
    return v * c + pltpu.roll(v, ROT_HALF, 1) * s1 + pltpu.roll(v, 128 - ROT_HALF, 1) * s2


def _rope_transpose(dv, c, s1, s2):
    return dv * c + pltpu.roll(dv * s1, 128 - ROT_HALF, 1) + pltpu.roll(dv * s2, ROT_HALF, 1)


def _shift_rows_down(u, prev, k):
    row = lax.broadcasted_iota(jnp.int32, u.shape, 0)
    out = pltpu.roll(u, k, 0)
    for r in range(k):
        out = jnp.where(row == r, prev[8 - k + r:8 - k + r + 1, :], out)
    return out


def _shift_rows_up(u, nxt, k):
    n = u.shape[0]
    row = lax.broadcasted_iota(jnp.int32, u.shape, 0)
    out = pltpu.roll(u, n - k, 0)
    for r in range(k):
        out = jnp.where(row == n - k + r, nxt[r:r + 1, :], out)
    return out


def _conv3(u, u1, u2, w):
    return (w[0:1, :] * u2 + w[1:2, :] * u1) + w[2:3, :] * u


def _mesh_pos():
    return lax.axis_index("x"), lax.axis_index("y"), lax.axis_index("c")


def _slot(ref, pos):
    dev = 4 * pos[0] + 2 * pos[1] + pos[2]
    if len(ref.shape) == 2:
        width = ref.shape[1] // N_DEV
        return ref.at[:, pl.ds(pl.multiple_of(dev * width, width), width)]
    return ref.at[dev]


def _gathered_shape(shard, by_cols):
    if by_cols:
        return jax.ShapeDtypeStruct((shard.shape[0], N_DEV * shard.shape[1]), shard.dtype)
    return jax.ShapeDtypeStruct((N_DEV,) + shard.shape, shard.dtype)


def _enter_with(peers):
    barrier = pltpu.get_barrier_semaphore()
    for peer in peers:
        pl.semaphore_signal(barrier, inc=1, device_id=peer, device_id_type=MESH)
    pl.semaphore_wait(barrier, len(peers))


def _sibling_and_chips(x, y, c):
    return [(x, y, 1 - c), (1 - x, y, c), (x, 1 - y, c), (1 - x, 1 - y, c)]


def _push(src, dst, sems, k, to):
    send_sems, recv_sems = sems
    return pltpu.make_async_remote_copy(src_ref=src, dst_ref=dst, send_sem=send_sems.at[k], recv_sem=recv_sems.at[k],
                                        device_id=to, device_id_type=MESH)


def _gather_steps(shards, outs, send_sems, recv_sems, local_sems):
    n = len(shards)
    x, y, c = _mesh_pos()
    me, sibling = (x, y, c), (x, y, 1 - c)
    chips = [(1 - x, y), (x, 1 - y), (1 - x, 1 - y)]

    def copy(i, k, block, to, src=None):
        dst = _slot(outs[i], block)
        return _push(dst if src is None else src, dst, (send_sems, recv_sems), 7 * i + k, to)

    mine = [pltpu.make_async_copy(shards[i], _slot(outs[i], me), local_sems.at[i]) for i in range(n)]
    first = []
    for i in range(n):
        first.append(copy(i, 0, me, sibling, src=shards[i]))
        first += [copy(i, 1 + j, me, (*chip, c), src=shards[i]) for j, chip in enumerate(chips)]

    def start():
        for cp in mine + first:
            cp.start()

    def finish():
        passed = []
        for j, chip in enumerate(chips):
            for i in range(n):
                copy(i, 1 + j, (*chip, c), me).wait_recv()
                cp = copy(i, 4 + j, (*chip, c), sibling)
                cp.start()
                passed.append(cp)
        for i in range(n):
            copy(i, 0, sibling, me).wait_recv()
            for j, chip in enumerate(chips):
                copy(i, 4 + j, (*chip, 1 - c), me).wait_recv()
        for cp in first + passed:
            cp.wait_send()
        for cp in mine:
            cp.wait()

    return start, finish


def _gather_near(first, last, shards, outs, sems, local_sems):
    x, y, c = _mesh_pos()
    me, peers = (x, y, c), [(x, y, 1 - c), (1 - x, y, c), (x, 1 - y, c)]
    n = len(shards)
    local = [pltpu.make_async_copy(shards[i], _slot(outs[i], me), local_sems.at[i]) for i in range(n)]
    sends = [_push(shards[i], _slot(outs[i], me), sems, 3 * i + k, peers[k]) for i in range(n) for k in range(3)]
    arrivals = [_push(shards[i], _slot(outs[i], peers[k]), sems, 3 * i + k, peers[k]) for i in range(n) for k in range(3)]

    def start():
        for cp in local + sends:
            cp.start()

    if first is not None:
        pl.when(first)(start)

    @pl.when(last)
    def _():
        for cp in sends:
            cp.wait_send()
        for cp in arrivals:
            cp.wait_recv()
        for cp in local:
            cp.wait()

    return start


def _relay_route(x, y, c):
    south = c == 0
    via = (jnp.where(south, 1 - x, x), jnp.where(south, y, 1 - y))
    to = (jnp.where(south, x, 1 - x), jnp.where(south, 1 - y, y))
    return via, to


def _gather_far(first, middle, last, shards, ins, outs, sems):
    x, y, c = _mesh_pos()
    sibling = (x, y, 1 - c)
    chips = [(1 - x, y), (x, 1 - y), (1 - x, 1 - y)]
    via, to = _relay_route(x, y, c)
    n = len(shards)
    diag_send = [_push(_slot(ins[i], (*via, c)), _slot(outs[i], (*via, c)), sems, 4 * i, (*to, c)) for i in range(n)]
    diag_arrival = [_push(shards[i], _slot(outs[i], (*chips[2], c)), sems, 4 * i, (*to, c)) for i in range(n)]
    passed = [[_push(_slot(ins[i], (*chips[j], c)), _slot(outs[i], (*chips[j], c)), sems, 4 * i + 1 + j, sibling)
               for i in range(n)] for j in range(3)]
    from_sibling = [_push(shards[i], _slot(outs[i], (*chips[j], 1 - c)), sems, 4 * i + 1 + j, sibling)
                    for i in range(n) for j in range(3)]

    @pl.when(first)
    def _():
        for cp in diag_send + passed[0] + passed[1]:
            cp.start()

    @pl.when(middle)
    def _():
        for cp in diag_arrival:
            cp.wait_recv()
        for cp in passed[2]:
            cp.start()

    @pl.when(last)
    def _():
        for cp in from_sibling:
            cp.wait_recv()
        for cp in diag_send + passed[0] + passed[1] + passed[2]:
            cp.wait_send()


def _in_proj_fwd(x, g1, w_in, conv_w, g_conv, rope, tm, shards, by_cols):
    t = x.shape[0]
    rc, rs1, rs2 = rope
    n = len(shards)
    n_tiles = t // tm

    def body(*refs):
        x_ref, g1_ref, w_ref, cw_ref, gc_ref, c_ref, s1_ref, s2_ref = refs[:8]
        shard_refs = refs[8:8 + n]
        qkv_ref, gates_ref, mconv_ref, w_full_ref, cw_full_ref = refs[8 + n:13 + n]
        gathered = refs[13 + n:13 + 2 * n]
        carry_ref, w_land, cw_land, hn_ref = refs[13 + 2 * n:17 + 2 * n]
        now_sems = refs[17 + 2 * n:20 + 2 * n]
        step = pl.program_id(0)
        start_later_weights = _gather_near(None, step == 2 * n_tiles - 1, shard_refs, gathered,
                                           refs[20 + 2 * n:22 + 2 * n], refs[22 + 2 * n]) if n else None
        start_w_in, finish_w_in = _gather_steps([w_ref, cw_ref], [w_land, cw_land], *now_sems)

        @pl.when(step == 0)
        def _():
            carry_ref[...] = jnp.zeros_like(carry_ref)
            _enter_with(_sibling_and_chips(*_mesh_pos()))
            start_w_in()
            if start_later_weights is not None:
                start_later_weights()

        @pl.when(step < n_tiles)
        def _():
            xv = x_ref[...]
            hn_ref[step] = ((xv * _inv_rms(xv)) * g1_ref[...]).astype(BF16)

        @pl.when(step == n_tiles)
        def _():
            finish_w_in()
            conv_shard = CONV_W // N_DEV
            for d in range(N_DEV):
                w_full_ref[IN_SHARD * d:IN_SHARD * (d + 1), :] = w_land[d]
                cw_full_ref[:, conv_shard * d:conv_shard * (d + 1)] = cw_land[d]

        @pl.when(step >= n_tiles)
        def _():
            proj = _mm_nt(hn_ref[step - n_tiles], w_full_ref[...])
            c, s1, s2 = c_ref[...], s1_ref[...], s2_ref[...]
            for ci in range((ATTN_W + KV_W) // 128):
                sl = slice(128 * ci, 128 * (ci + 1))
                qkv_ref[:, sl] = _rope(proj[:, sl], c, s1, s2).astype(BF16)
            qkv_ref[:, ATTN_W + KV_W:QKV_W] = proj[:, ATTN_W + KV_W:QKV_W].astype(BF16)
            gates = proj[:, QKV_W:]
            gates_ref[...] = gates
            gb, gcc, xin = gates[:, :CONV_W], gates[:, CONV_W:2 * CONV_W], gates[:, 2 * CONV_W:]
            u = gcc * xin
            prev = carry_ref[...]
            conv = gb * _conv3(u, _shift_rows_down(u, prev, 1), _shift_rows_down(u, prev, 2), cw_full_ref[...])
            carry_ref[...] = u[tm - 8:tm, :]
            mconv_ref[...] = ((conv * _inv_rms(conv)) * gc_ref[...]).astype(BF16)

    first_pass = pl.BlockSpec((tm, D_MODEL), lambda i: (jnp.minimum(i, n_tiles - 1), 0))
    tile = lambda w_: pl.BlockSpec((tm, w_), lambda i: (jnp.maximum(i - n_tiles, 0), 0))
    sems = lambda k: pltpu.SemaphoreType.DMA((k,))
    res = pl.pallas_call(
        body, name="in_proj_fwd", grid=(2 * n_tiles,),
        in_specs=[first_pass, _full((1, D_MODEL)), HBM_SPEC, HBM_SPEC, _full((1, CONV_W)), tile(128), tile(128),
                  tile(128)] + [HBM_SPEC] * n,
        out_specs=[tile(QKV_W), tile(GATES_W), tile(CONV_W), _full((IN_COLS, D_MODEL)), _full((3, CONV_W))]
        + [HBM_SPEC] * n,
        out_shape=[jax.ShapeDtypeStruct((t, QKV_W), BF16), jax.ShapeDtypeStruct((t, GATES_W), F32),
                   jax.ShapeDtypeStruct((t, CONV_W), BF16), jax.ShapeDtypeStruct((IN_COLS, D_MODEL), BF16),
                   jax.ShapeDtypeStruct((3, CONV_W), F32)]
        + [_gathered_shape(s, cols) for s, cols in zip(shards, by_cols)],
        scratch_shapes=[pltpu.VMEM((8, CONV_W), F32), pltpu.VMEM((N_DEV,) + w_in.shape, BF16),
                        pltpu.VMEM((N_DEV,) + conv_w.shape, F32), pltpu.VMEM((n_tiles, tm, D_MODEL), BF16),
                        sems(14), sems(14), sems(2)]
        + ([sems(3 * n), sems(3 * n), sems(n)] if n else []),
        compiler_params=_params("arbitrary", barrier_id=0),
    )(x, g1, w_in, conv_w, g_conv, rc, rs1, rs2, *shards)
    return res[0], res[1], res[2], res[3], res[4], list(res[5:])


GROUP_COLS = GROUP * BLOCK
ATTN_STEP_BLOCKS = 4


def _attn_masks(has_prev):
    key = lax.broadcasted_iota(jnp.int32, (2 * BLOCK, GROUP_COLS), 0)
    query = lax.broadcasted_iota(jnp.int32, (2 * BLOCK, GROUP_COLS), 1) & (BLOCK - 1)
    band = (key > query) & (key <= query + BLOCK)
    return [band & ((key >= BLOCK) | has_prev)] + [band] * (ATTN_STEP_BLOCKS - 1)


def _heads_side_by_side(at, g, b):
    heads = [at[HEAD_DIM * (GROUP * g + hh):HEAD_DIM * (GROUP * g + hh + 1), BLOCK * b:BLOCK * (b + 1)] for hh in range(GROUP)]
    return jnp.concatenate(heads, axis=1)


def _to_token_rows(parts):
    rows = [jnp.concatenate([parts[b][g][:, BLOCK * hh:BLOCK * (hh + 1)] for b in range(ATTN_STEP_BLOCKS)], axis=1)
            for g in range(N_KV) for hh in range(GROUP)]
    return jnp.concatenate(rows, axis=0).T


def _group_sinks(sink_ref, g):
    head = lax.broadcasted_iota(jnp.int32, (1, GROUP_COLS), 1) // BLOCK
    out = jnp.full((1, GROUP_COLS), sink_ref[0, GROUP * g], F32)
    for hh in range(1, GROUP):
        out = jnp.where(head == hh, sink_ref[0, GROUP * g + hh], out)
    return out


def _attn_probs(qt, kk, sink, valid):
    s = jnp.where(valid, _mm(kk, qt), NEG_INF)
    m = jnp.maximum(jnp.max(s, axis=0, keepdims=True), sink)
    p = jnp.exp(s - m)
    psink = jnp.exp(sink - m)
    inv_l = 1.0 / (jnp.sum(p, axis=0, keepdims=True) + psink)
    return p * inv_l, psink * inv_l


ATTN_STEP = ATTN_STEP_BLOCKS * BLOCK
ATTN_KEYS = ATTN_STEP + BLOCK


def _qkv_specs(order):
    prev = lambda i: jnp.maximum(ATTN_STEP_BLOCKS * order(i) - 1, 0)
    kcol, vcol = ATTN_W // KV_W, ATTN_W // KV_W + 1
    return [pl.BlockSpec((ATTN_STEP, ATTN_W), lambda i: (order(i), 0)),
            pl.BlockSpec((BLOCK, KV_W), lambda i: (prev(i), kcol)), pl.BlockSpec((ATTN_STEP, KV_W), lambda i: (order(i), kcol)),
            pl.BlockSpec((BLOCK, KV_W), lambda i: (prev(i), vcol)), pl.BlockSpec((ATTN_STEP, KV_W), lambda i: (order(i), vcol))]


def _attn_fwd(qkv, sinks, g_attn, shards, gathered):
    t = qkv.shape[0]
    n = len(shards)

    def body(*refs):
        sink_ref, q_ref, kp_ref, kc_ref, vp_ref, vc_ref, ga_ref = refs[:7]
        attn_ref, mattn_ref = refs[7 + 2 * n:9 + 2 * n]
        step = pl.program_id(0)
        if n:
            @pl.when(step == 0)
            def _():
                x, y, c = _mesh_pos()
                _enter_with([(x, y, 1 - c), (*_relay_route(x, y, c)[1], c)])

            n_steps = t // ATTN_STEP
            _gather_far(step == 0, step == n_steps // 2, step == n_steps - 1, refs[7:7 + n], refs[7 + n:7 + 2 * n],
                        refs[9 + 2 * n:9 + 3 * n], refs[9 + 3 * n:11 + 3 * n])
        qt = (q_ref[...] * ATTN_SCALE).T
        keys = jnp.concatenate([kp_ref[...], kc_ref[...]], axis=0)
        vals = jnp.concatenate([vp_ref[...], vc_ref[...]], axis=0)
        sink = [_group_sinks(sink_ref, g) for g in range(N_KV)]
        masks = _attn_masks(step > 0)
        parts = []
        for b in range(ATTN_STEP_BLOCKS):
            window = slice(BLOCK * b, BLOCK * (b + 2))
            valid = masks[b]
            parts.append([])
            for g in range(N_KV):
                gs = slice(HEAD_DIM * g, HEAD_DIM * (g + 1))
                probs, _ = _attn_probs(_heads_side_by_side(qt, g, b), keys[window, gs], sink[g], valid)
                parts[b].append(_mm_tn(vals[window, gs], probs.astype(BF16)))
        attn = _to_token_rows(parts)
        attn_ref[...] = attn
        mattn_ref[...] = ((attn * _inv_rms(attn)) * ga_ref[...]).astype(BF16)

    blk = pl.BlockSpec((ATTN_STEP, ATTN_W), lambda j: (j, 0))
    res = pl.pallas_call(
        body, name="attn_fwd", grid=(t // ATTN_STEP,),
        in_specs=[pl.BlockSpec(memory_space=pltpu.SMEM)] + _qkv_specs(lambda j: j) + [_full((1, ATTN_W))]
        + [HBM_SPEC] * (2 * n),
        out_specs=[blk, blk] + [HBM_SPEC] * n,
        out_shape=[jax.ShapeDtypeStruct((t, ATTN_W), F32), jax.ShapeDtypeStruct((t, ATTN_W), BF16)]
        + [jax.ShapeDtypeStruct(g.shape, g.dtype) for g in gathered],
        input_output_aliases={7 + n + i: 2 + i for i in range(n)},
        scratch_shapes=[pltpu.SemaphoreType.DMA((4 * n,)), pltpu.SemaphoreType.DMA((4 * n,))] if n else [],
        compiler_params=_params("arbitrary", barrier_id=1 if n else None),
    )(sinks, qkv, qkv, qkv, qkv, qkv, g_attn, *shards, *gathered)
    return res[0], res[1], list(res[2:])


SMALL_ROWS = 8
ROW_LOSS, ROW_G2, ROW_G3, ROW_G4 = 0, 1, 2, 3


def _mid(mattn, mconv, x, target, g2, g3, g4, w_out, w_up, w_down, tm):
    t = x.shape[0]

    def body(ma_ref, mc_ref, x_ref, t_ref, g2_ref, g3_ref, g4_ref, wo_ref, wu_ref, wd_ref,
             act_ref, dup_ref, hn2t_ref, dmo_ref, dmix_ref, dh_ref, dmixed_ref, small_ref, up_ref):
        @pl.when(pl.program_id(0) == 0)
        def _():
            small_ref[...] = jnp.zeros_like(small_ref)

        g2, g3, g4 = g2_ref[...], g3_ref[...], g4_ref[...]
        mix_out = _mm(ma_ref[...], wo_ref[0:ATTN_W, :]) + _mm(mc_ref[...], wo_ref[ATTN_W:, :])
        r2 = _inv_rms(mix_out)
        mo_hat = mix_out * r2
        h = x_ref[...] + mo_hat * g2
        r3 = _inv_rms(h)
        h_hat = h * r3
        hn2 = (h_hat * g3).astype(BF16)
        hn2t_ref[...] = hn2.T
        for j in range(MID_CHUNKS):
            cols_j = slice(MID_CHUNK * j, MID_CHUNK * (j + 1))
            up = jnp.maximum(_mm(hn2, wu_ref[:, cols_j]), 0.0)
            up_ref[:, cols_j] = up.astype(BF16)
            act_ref[:, cols_j] = (up * up).astype(BF16)
        mlp = _mm(act_ref[...], wd_ref[...])
        r4 = _inv_rms(mlp)
        ml_hat = mlp * r4
        err = (h + ml_hat * g4) - t_ref[...]
        d_out = err * (1.0 / D_MODEL)
        d_mlp, dg4 = _rms_bwd(ml_hat, r4, g4, d_out)
        dmo = d_mlp.astype(BF16)
        dmo_ref[...] = dmo
        for j in range(MID_CHUNKS):
            cols_j = slice(MID_CHUNK * j, MID_CHUNK * (j + 1))
            dact = _mm_nt(dmo, wd_ref[cols_j, :])
            dup_ref[:, cols_j] = (dact * (2.0 * up_ref[:, cols_j].astype(F32))).astype(BF16)
        dhn2 = _mm_nt(dup_ref[...], wu_ref[...])
        dh_norm, dg3 = _rms_bwd(h_hat, r3, g3, dhn2)
        dh = d_out + dh_norm
        dh_ref[...] = dh
        d_mix, dg2 = _rms_bwd(mo_hat, r2, g2, dh)
        dmix = d_mix.astype(BF16)
        dmix_ref[...] = dmix
        dmixed_ref[...] = _mm_nt(dmix, wo_ref[...])
        small_ref[ROW_LOSS:ROW_LOSS + 1, :] += _colsum(err * err)
        small_ref[ROW_G2:ROW_G2 + 1, :] += _colsum(dg2)
        small_ref[ROW_G3:ROW_G3 + 1, :] += _colsum(dg3)
        small_ref[ROW_G4:ROW_G4 + 1, :] += _colsum(dg4)

    tile = lambda n: pl.BlockSpec((tm, n), lambda i: (i, 0))
    cols = lambda n: pl.BlockSpec((n, tm), lambda i: (0, i))
    gain = _full((1, D_MODEL))
    return pl.pallas_call(
        body, name="mid_fwd_bwd", grid=(t // tm,),
        in_specs=[tile(ATTN_W), tile(CONV_W), tile(D_MODEL), tile(D_MODEL), gain, gain, gain,
                  _resident((D_MODEL, D_MODEL)), _resident((D_MODEL, D_FF)), _resident((D_FF, D_MODEL))],
        out_specs=[tile(D_FF), tile(D_FF), cols(D_MODEL), tile(D_MODEL), tile(D_MODEL), tile(D_MODEL), tile(D_MODEL),
                   _full((SMALL_ROWS, D_MODEL))],
        out_shape=[jax.ShapeDtypeStruct((t, D_FF), BF16), jax.ShapeDtypeStruct((t, D_FF), BF16),
                   jax.ShapeDtypeStruct((D_MODEL, t), BF16), jax.ShapeDtypeStruct((t, D_MODEL), BF16),
                   jax.ShapeDtypeStruct((t, D_MODEL), BF16), jax.ShapeDtypeStruct((t, D_MODEL), F32),
                   jax.ShapeDtypeStruct((t, D_MODEL), F32), jax.ShapeDtypeStruct((SMALL_ROWS, D_MODEL), F32)],
        scratch_shapes=[pltpu.VMEM((tm, D_FF), BF16)],
        compiler_params=_params("arbitrary"),
    )(mattn, mconv, x, target, g2, g3, g4, w_out, w_up, w_down)


CHIP_FLIPS = ((1, 1), (1, 0), (0, 1))


def _block_order(dev):
    chip_masks = [4 * fx + 2 * fy for fx, fy in CHIP_FLIPS]
    masks = [m + 1 for m in chip_masks] + [1] + chip_masks + [0]
    return jnp.bitwise_xor(dev, jnp.asarray(masks, jnp.int32)).astype(jnp.int32)


def _other_chips(x, y, c):
    return [(1 - x if fx else x, 1 - y if fy else y, c) for fx, fy in CHIP_FLIPS]


def _dw_pair_sums(operands, order, which, name, barrier_id, ride=None):
    t = operands[-1].shape[0]
    n_far = len(CHIP_FLIPS)
    n_in = len(operands)
    n_ride = 0 if ride is None else 1
    out_chunk = D_MODEL // N_DEV
    if which == "up":
        rows, cols = D_MODEL, FF_CHUNK
        in_specs = [_resident((D_MODEL, t)), pl.BlockSpec((t, FF_CHUNK), lambda s, order_ref: (0, order_ref[s]))]
    elif which == "down":
        rows, cols = FF_CHUNK, D_MODEL
        in_specs = [pl.BlockSpec((t, FF_CHUNK), lambda s, order_ref: (0, order_ref[s])), _resident((t, D_MODEL))]
    else:
        rows, cols = out_chunk, D_MODEL
        half = pl.BlockSpec((t, out_chunk), lambda s, order_ref: (0, order_ref[s] % (N_DEV // 2)))
        in_specs = [half, half, _resident((t, D_MODEL))]

    def body(order_ref, *refs):
        own_ref, from_sib_ref, pair_ref = refs[n_in + n_ride:n_in + n_ride + 3]
        send_buf, land_buf, send_sems, recv_sems = refs[n_in + 2 * n_ride + 3:n_in + 2 * n_ride + 7]
        s_now = pl.program_id(0)
        x, y, c = _mesh_pos()
        sibling = (x, y, 1 - c)
        sems = (send_sems, recv_sems)

        @pl.when(s_now == 0)
        def _():
            _enter_with([sibling] + (_other_chips(x, y, c) if n_ride else []))

        if n_ride:
            _chip_exchange_beside(s_now == 0, s_now == N_DEV - 1, [refs[n_in]], [refs[n_in + 3 + n_ride]],
                                  refs[n_in + 2 * n_ride + 7:], enter=False)

        def hand_over(k):
            dst = land_buf.at[k] if k < n_far else from_sib_ref
            return _push(send_buf.at[k], dst, sems, k, sibling)

        if which == "out":
            ma_ref, mc_ref, b_ref = refs[:n_in]
            block = lax.cond(order_ref[s_now] < N_DEV // 2, lambda: _mm_tn(ma_ref[...], b_ref[...]),
                             lambda: _mm_tn(mc_ref[...], b_ref[...]))
        elif which == "down":
            block = _mm_tn(refs[0][...], refs[1][...])
        else:
            block = _mm(refs[0][...], refs[1][...])
        for k in range(n_far + 1):
            @pl.when(s_now == k)
            def _():
                send_buf[k] = block.astype(BF16)
                hand_over(k).start()

        for k in range(n_far):
            @pl.when(s_now == n_far + 1 + k)
            def _():
                hand_over(k).wait_recv()
                pair_ref[...] = (block + land_buf[k].astype(F32)).astype(BF16)

        @pl.when(s_now == N_DEV - 1)
        def _():
            own_ref[...] = block
            for k in range(n_far + 1):
                hand_over(k).wait_send()
            hand_over(n_far).wait_recv()

    rides = [] if ride is None else [ride]
    sems = lambda k: pltpu.SemaphoreType.DMA((k,))
    return pl.pallas_call(
        body, name=name,
        grid_spec=pltpu.PrefetchScalarGridSpec(
            num_scalar_prefetch=1, grid=(N_DEV,), in_specs=in_specs + [HBM_SPEC] * n_ride,
            out_specs=[pl.BlockSpec((rows, cols), lambda s, order_ref: (0, 0)), HBM_SPEC,
                       pl.BlockSpec((None, rows, cols), lambda s, order_ref: (jnp.clip(s - n_far - 1, 0, n_far - 1), 0, 0))]
            + [HBM_SPEC] * n_ride,
            scratch_shapes=[pltpu.VMEM((n_far + 1, rows, cols), BF16), pltpu.VMEM((n_far, rows, cols), BF16),
                            sems(n_far + 1), sems(n_far + 1)] + [sems(n_far), sems(n_far)] * n_ride),
        out_shape=[jax.ShapeDtypeStruct((rows, cols), F32), jax.ShapeDtypeStruct((rows, cols), BF16),
                   jax.ShapeDtypeStruct((n_far, rows, cols), BF16)]
        + [jax.ShapeDtypeStruct(r.shape, r.dtype) for r in rides],
        compiler_params=_params("arbitrary", barrier_id=barrier_id),
    )(order, *operands, *rides)


def _chip_exchange_beside(first, last, sums, outs, sems, enter=True):
    chips = _other_chips(*_mesh_pos())
    copies = [_push(sums[i].at[k], outs[i].at[k], sems, len(chips) * i + k, chip)
              for i in range(len(sums)) for k, chip in enumerate(chips)]

    @pl.when(first)
    def _():
        if enter:
            _enter_with(chips)
        for cp in copies:
            cp.start()

    @pl.when(last)
    def _():
        for cp in copies:
            cp.wait()


ROW_GCONV, ROW_CW0 = 1, 2


def _conv_bwd(dmixed, gates, g_conv, conv_w, tm):
    t = gates.shape[0]
    n = t // tm
    rev = lambda i: n - 1 - i

    def body(dm_ref, gates_ref, gprev_ref, gc_ref, cw_ref, dgates_ref, small_ref, carry_ref):
        i = pl.program_id(0)

        @pl.when(i == 0)
        def _():
            small_ref[...] = jnp.zeros_like(small_ref)
            carry_ref[...] = jnp.zeros_like(carry_ref)

        gates = gates_ref[...]
        gb, gcc, xin = gates[:, :CONV_W], gates[:, CONV_W:2 * CONV_W], gates[:, 2 * CONV_W:]
        u = gcc * xin
        gp = gprev_ref[...]
        uprev = jnp.where(rev(i) == 0, 0.0, gp[:, CONV_W:2 * CONV_W] * gp[:, 2 * CONV_W:])
        u1, u2 = _shift_rows_down(u, uprev, 1), _shift_rows_down(u, uprev, 2)
        w = cw_ref[...]
        c = _conv3(u, u1, u2, w)
        conv = gb * c
        rcv = _inv_rms(conv)
        c_hat = conv * rcv
        dconv, dgc = _rms_bwd(c_hat, rcv, gc_ref[...], dm_ref[...])
        dc = dconv * gb
        nxt = carry_ref[...]
        du = (w[2:3, :] * dc + w[1:2, :] * _shift_rows_up(dc, nxt, 1)) + w[0:1, :] * _shift_rows_up(dc, nxt, 2)
        carry_ref[...] = dc[0:8, :]
        dgates_ref[:, :CONV_W] = (dconv * c).astype(BF16)
        dgates_ref[:, CONV_W:2 * CONV_W] = (du * xin).astype(BF16)
        dgates_ref[:, 2 * CONV_W:] = (du * gcc).astype(BF16)
        small_ref[ROW_GCONV:ROW_GCONV + 1, :] += _colsum(dgc)
        small_ref[ROW_CW0:ROW_CW0 + 1, :] += _colsum(dc * u2)
        small_ref[ROW_CW0 + 1:ROW_CW0 + 2, :] += _colsum(dc * u1)
        small_ref[ROW_CW0 + 2:ROW_CW0 + 3, :] += _colsum(dc * u)

    tile = lambda w_: pl.BlockSpec((tm, w_), lambda i: (rev(i), 0))
    prev8 = pl.BlockSpec((8, GATES_W), lambda i: (jnp.maximum(rev(i) * (tm // 8) - 1, 0), 0))
    conv_half = pl.BlockSpec((tm, CONV_W), lambda i: (rev(i), ATTN_W // CONV_W))
    return pl.pallas_call(
        body, name="conv_bwd", grid=(n,),
        in_specs=[conv_half, tile(GATES_W), prev8, _full((1, CONV_W)), _full((3, CONV_W))],
        out_specs=[tile(GATES_W), _full((SMALL_ROWS, CONV_W))],
        out_shape=[jax.ShapeDtypeStruct((t, GATES_W), BF16), jax.ShapeDtypeStruct((SMALL_ROWS, CONV_W), F32)],
        scratch_shapes=[pltpu.VMEM((8, CONV_W), F32)],
        compiler_params=_params("arbitrary"),
    )(dmixed, gates, gates, g_conv, conv_w)


def _attn_bwd(qkv, dmixed, attn, g_attn, sinks, rope, sums):
    t = qkv.shape[0]
    n_steps = t // ATTN_STEP
    rev = lambda i: n_steps - 1 - i
    rc, rs1, rs2 = rope

    def body(sink_ref, q_ref, kp_ref, kc_ref, vp_ref, vc_ref, dm_ref, attn_ref, ga_ref, c_ref, s1_ref, s2_ref, sums_ref,
             dqkv_ref, dsink_ref, dgain_ref, arrived_ref, ck_ref, cv_ref, kacc_ref, vacc_ref, send_sems, recv_sems):
        i = pl.program_id(0)
        _chip_exchange_beside(i == 0, i == n_steps - 1, [sums_ref], [arrived_ref], (send_sems, recv_sems))

        @pl.when(i == 0)
        def _():
            dsink_ref[...] = jnp.zeros_like(dsink_ref)
            dgain_ref[...] = jnp.zeros_like(dgain_ref)
            ck_ref[...] = jnp.zeros_like(ck_ref)
            cv_ref[...] = jnp.zeros_like(cv_ref)

        kacc_ref[...] = jnp.zeros_like(kacc_ref)
        vacc_ref[...] = jnp.zeros_like(vacc_ref)
        a = attn_ref[...]
        ra = _inv_rms(a)
        dattn, dgain = _rms_bwd(a * ra, ra, ga_ref[...], dm_ref[...])
        dgain_ref[0:1, :] += _colsum(dgain)
        qt = (q_ref[...] * ATTN_SCALE).T
        dot = dattn.astype(BF16).T
        keys = jnp.concatenate([kp_ref[...], kc_ref[...]], axis=0)
        vals = jnp.concatenate([vp_ref[...], vc_ref[...]], axis=0)
        sink = [_group_sinks(sink_ref, g) for g in range(N_KV)]
        c, s1, s2 = c_ref[...], s1_ref[...], s2_ref[...]
        lane = lax.broadcasted_iota(jnp.int32, (1, 128), 1)
        dsink = jnp.zeros((1, 128), F32)
        masks = _attn_masks(rev(i) > 0)
        dq_parts = []
        for b in range(ATTN_STEP_BLOCKS):
            window = slice(BLOCK * b, BLOCK * (b + 2))
            valid = masks[b]
            dq_parts.append([])
            dk_parts, dv_parts = [], []
            for g in range(N_KV):
                gs = slice(HEAD_DIM * g, HEAD_DIM * (g + 1))
                kk, vv = keys[window, gs], vals[window, gs]
                qtg, dotg = _heads_side_by_side(qt, g, b), _heads_side_by_side(dot, g, b)
                probs, psink = _attn_probs(qtg, kk, sink[g], valid)
                dp = _mm(vv, dotg)
                delta = jnp.sum(probs * dp, axis=0, keepdims=True)
                ds = (probs * (dp - delta)).astype(BF16)
                sink_terms = psink * delta
                for hh in range(GROUP):
                    head_sum = jnp.sum(sink_terms[:, BLOCK * hh:BLOCK * (hh + 1)])
                    dsink = dsink + jnp.where(lane == GROUP * g + hh, -head_sum, 0.0)
                dq_parts[b].append(_mm_tn(kk * ATTN_SCALE, ds))
                dk_parts.append(_mm_nt(ds, qtg))
                dv_parts.append(_mm_nt(probs.astype(BF16), dotg))
            kacc_ref[window, :] += jnp.concatenate(dk_parts, axis=1)
            vacc_ref[window, :] += jnp.concatenate(dv_parts, axis=1)
        dq = _to_token_rows(dq_parts)
        for ci in range(ATTN_W // 128):
            sl = slice(128 * ci, 128 * (ci + 1))
            dqkv_ref[:, sl] = _rope_transpose(dq[:, sl], c, s1, s2).astype(BF16)
        kacc_ref[ATTN_STEP:, :] += ck_ref[...]
        vacc_ref[ATTN_STEP:, :] += cv_ref[...]
        ck_ref[...] = kacc_ref[:BLOCK, :]
        cv_ref[...] = vacc_ref[:BLOCK, :]
        dqkv_ref[:, ATTN_W:ATTN_W + KV_W] = _rope_transpose(kacc_ref[BLOCK:, :], c, s1, s2).astype(BF16)
        dqkv_ref[:, ATTN_W + KV_W:] = vacc_ref[BLOCK:, :].astype(BF16)
        dsink_ref[0:1, :] += dsink

    blk = lambda w_: pl.BlockSpec((ATTN_STEP, w_), lambda i: (rev(i), 0))
    return pl.pallas_call(
        body, name="attn_bwd", grid=(n_steps,),
        in_specs=[pl.BlockSpec(memory_space=pltpu.SMEM)] + _qkv_specs(rev)
        + [blk(ATTN_W), blk(ATTN_W), _full((1, ATTN_W)), blk(128), blk(128), blk(128), HBM_SPEC],
        out_specs=[blk(QKV_W), _full((8, 128)), _full((SMALL_ROWS, ATTN_W)), HBM_SPEC],
        out_shape=[jax.ShapeDtypeStruct((t, QKV_W), BF16), jax.ShapeDtypeStruct((8, 128), F32),
                   jax.ShapeDtypeStruct((SMALL_ROWS, ATTN_W), F32), jax.ShapeDtypeStruct(sums.shape, sums.dtype)],
        scratch_shapes=[pltpu.VMEM((BLOCK, KV_W), F32), pltpu.VMEM((BLOCK, KV_W), F32),
                        pltpu.VMEM((ATTN_KEYS, KV_W), F32), pltpu.VMEM((ATTN_KEYS, KV_W), F32),
                        pltpu.SemaphoreType.DMA((len(CHIP_FLIPS),)), pltpu.SemaphoreType.DMA((len(CHIP_FLIPS),))],
        compiler_params=_params("arbitrary", barrier_id=6),
    )(sinks, qkv, qkv, qkv, qkv, qkv, dmixed, attn, g_attn, rc, rs1, rs2, sums)


def _grad_x_tile(dq, dg, x_hat, r, g1, w_ref, dh):
    dhn = _mm(dq, w_ref[:QKV_W, :]) + _mm(dg, w_ref[QKV_W:, :])
    dx, dg1 = _rms_bwd(x_hat, r, g1, dhn)
    return dh + dx, _colsum(dg1)


def _head_tiles(n):
    head = max(5 * n // 8, 1)
    rest = n - head
    if rest and head % -(-head // rest):
        head -= 1
    return head


def _in_proj_bwd(dqkv, dgates, x, dh, g1, w_in, tm, out_sums):
    t = x.shape[0]
    n = t // tm
    n_cover = _head_tiles(n)
    n_steps = n + n_cover
    n_far = len(CHIP_FLIPS)
    shard = (IN_SHARD, D_MODEL)

    def body(dq_ref, dg_ref, x_ref, dh_ref, g1_ref, w_ref, osums_ref,
             dx_ref, own_ref, sib_ref, far_ref, dg1_ref, oarrived_ref,
             acc_ref, send_buf, land_buf, pair_buf, d2d_send, d2d_recv, ici_send, ici_recv, o_send, o_recv):
        i = pl.program_id(0)
        x_pos, y_pos, c = _mesh_pos()
        my_chip = 2 * x_pos + y_pos
        sibling = (x_pos, y_pos, 1 - c)
        @pl.when(i == 0)
        def _():
            _enter_with(_sibling_and_chips(x_pos, y_pos, c))

        _chip_exchange_beside(i == 0, i == n_steps - 1, [osums_ref], [oarrived_ref], (o_send, o_recv), enter=False)

        def rows(d):
            return slice(IN_SHARD * d, IN_SHARD * (d + 1))

        def hand_over(chip):
            return _push(send_buf.at[chip], land_buf.at[chip], (d2d_send, d2d_recv), chip, sibling)

        def to_chip(chip, rel):
            return pltpu.make_async_remote_copy(
                src_ref=pair_buf.at[chip], dst_ref=far_ref.at[rel - 1], send_sem=ici_send.at[rel - 1],
                recv_sem=ici_recv.at[rel - 1], device_id=(chip // 2, chip % 2, c), device_id_type=MESH)

        @pl.when(i == 0)
        def _():
            acc_ref[...] = jnp.zeros_like(acc_ref)
            dg1_ref[...] = jnp.zeros_like(dg1_ref)

        def normed_x():
            xv = x_ref[...]
            r = _inv_rms(xv)
            return xv * r, r

        @pl.when(i < n)
        def _():
            hn = (normed_x()[0] * g1_ref[...]).astype(BF16)
            acc_ref[:QKV_W, :] += _mm_tn(dq_ref[...], hn)
            acc_ref[QKV_W:, :] += _mm_tn(dg_ref[...], hn)

        @pl.when(i == n - 1)
        def _():
            for d in range(N_DEV):
                @pl.when(d % 2 != c)
                def _():
                    send_buf[d // 2] = acc_ref[rows(d), :].astype(BF16)
                    hand_over(d // 2).start()
            for d in range(N_DEV):
                chip = d // 2

                @pl.when(d % 2 == c)
                def _():
                    hand_over(chip).wait_recv()

                    @pl.when(chip == my_chip)
                    def _():
                        own_ref[...] = acc_ref[rows(d), :]
                        sib_ref[...] = land_buf[chip]

                    @pl.when(chip != my_chip)
                    def _():
                        pair_buf[chip] = (acc_ref[rows(d), :] + land_buf[chip].astype(F32)).astype(BF16)
                        to_chip(chip, chip ^ my_chip).start()
            for chip in range(N_CHIPS):
                hand_over(chip).wait_send()

        @pl.when(i >= n)
        def _():
            x_hat, r = normed_x()
            dx_ref[...], dg1 = _grad_x_tile(dq_ref[...], dg_ref[...], x_hat, r, g1_ref[...], w_ref, dh_ref[...])
            dg1_ref[0:1, :] += dg1

        @pl.when(i == n_steps - 1)
        def _():
            for rel in range(1, n_far + 1):
                to_chip(0, rel).wait()

    both = lambda w_: pl.BlockSpec((tm, w_), lambda i: (i % n, 0))
    second = pl.BlockSpec((tm, D_MODEL), lambda i: (jnp.maximum(i - n, 0), 0))
    whole = lambda dtype: jax.ShapeDtypeStruct(shard, dtype)
    sems = lambda k: pltpu.SemaphoreType.DMA((k,))
    res = pl.pallas_call(
        body, name="in_proj_bwd", grid=(n_steps,),
        in_specs=[both(QKV_W), both(GATES_W), both(D_MODEL), second, _full((1, D_MODEL)), _resident((IN_COLS, D_MODEL)),
                  HBM_SPEC],
        out_specs=[second, _full(shard), _full(shard), HBM_SPEC, _full((SMALL_ROWS, D_MODEL)), HBM_SPEC],
        out_shape=[jax.ShapeDtypeStruct((n_cover * tm, D_MODEL), F32), whole(F32), whole(BF16),
                   jax.ShapeDtypeStruct((n_far,) + shard, BF16), jax.ShapeDtypeStruct((SMALL_ROWS, D_MODEL), F32),
                   jax.ShapeDtypeStruct(out_sums.shape, out_sums.dtype)],
        scratch_shapes=[pltpu.VMEM((IN_COLS, D_MODEL), F32), pltpu.VMEM((N_CHIPS,) + shard, BF16),
                        pltpu.VMEM((N_CHIPS,) + shard, BF16), pltpu.VMEM((N_CHIPS,) + shard, BF16),
                        sems(N_CHIPS), sems(N_CHIPS), sems(n_far), sems(n_far), sems(n_far), sems(n_far)],
        compiler_params=_params("arbitrary", barrier_id=7),
    )(dqkv, dgates, x, dh, g1, w_in, out_sums)
    return res[0], (res[1], res[2], res[3]), res[4], res[5]


def _grad_x_rest(dqkv, dgates, x, dh, g1, w_in, tm, head, dg1_rows):
    t = x.shape[0]
    first = head.shape[0] // tm
    n_rest = t // tm - first
    if n_rest == 0:
        return head, dg1_rows
    per = -(-first // n_rest)
    n_pass = first // per
    assert first == n_pass * per and n_pass <= n_rest

    def body(dq_ref, dg_ref, x_ref, dh_ref, g1_ref, w_ref, head_ref, rows_ref, gx_ref, dg1_ref, stage, head_stage, sems):
        j = pl.program_id(0)

        def tile_out(step, kind):
            src, row0, rows = (stage, (step + first) * tm, tm) if kind == 0 else (head_stage, step * per * tm, per * tm)
            slot = 2 * kind + step % 2
            return pltpu.make_async_copy(src.at[step % 2], gx_ref.at[pl.ds(pl.multiple_of(row0, tm), rows), :],
                                         sems.at[slot])

        @pl.when(j == 0)
        def _():
            dg1_ref[...] = rows_ref[...]

        @pl.when(j >= 2)
        def _():
            tile_out(j - 2, 0).wait()

        @pl.when((j >= 2) & (j - 2 < n_pass))
        def _():
            tile_out(j - 2, 1).wait()

        @pl.when(j < n_pass)
        def _():
            head_stage[j % 2] = head_ref[...]
            tile_out(j, 1).start()

        xv = x_ref[...]
        r = _inv_rms(xv)
        dx, dg1 = _grad_x_tile(dq_ref[...], dg_ref[...], xv * r, r, g1_ref[...], w_ref, dh_ref[...])
        stage[j % 2] = dx
        dg1_ref[0:1, :] += dg1
        tile_out(j, 0).start()

        @pl.when(j == n_rest - 1)
        def _():
            for back in range(min(2, n_rest)):
                tile_out(j - back, 0).wait()

                @pl.when(j - back < n_pass)
                def _():
                    tile_out(j - back, 1).wait()

    tile = lambda w_: pl.BlockSpec((tm, w_), lambda j: (j + first, 0))
    head_tile = pl.BlockSpec((per * tm, D_MODEL), lambda j: (jnp.minimum(j, n_pass - 1), 0))
    return pl.pallas_call(
        body, name="grad_x_rest", grid=(n_rest,),
        in_specs=[tile(QKV_W), tile(GATES_W), tile(D_MODEL), tile(D_MODEL), _full((1, D_MODEL)),
                  _resident((IN_COLS, D_MODEL)), head_tile, _full((SMALL_ROWS, D_MODEL))],
        out_specs=[HBM_SPEC, _full((SMALL_ROWS, D_MODEL))],
        out_shape=[jax.ShapeDtypeStruct((t, D_MODEL), F32), jax.ShapeDtypeStruct((SMALL_ROWS, D_MODEL), F32)],
        scratch_shapes=[pltpu.VMEM((2, tm, D_MODEL), F32), pltpu.VMEM((2, per * tm, D_MODEL), F32),
                        pltpu.SemaphoreType.DMA((4,))],
        compiler_params=_params("arbitrary"),
    )(dqkv, dgates, x, dh, g1, w_in, head, dg1_rows)


def _all_gather(shards, name):
    n = len(shards)

    def body(*refs):
        _enter_with(_sibling_and_chips(*_mesh_pos()))
        start, finish = _gather_steps(refs[:n], refs[n:2 * n], *refs[2 * n:])
        start()
        finish()

    return pl.pallas_call(
        body, name=name,
        in_specs=[HBM_SPEC] * n, out_specs=[HBM_SPEC] * n,
        out_shape=[jax.ShapeDtypeStruct((N_DEV,) + s.shape, s.dtype) for s in shards],
        scratch_shapes=[pltpu.SemaphoreType.DMA((7 * n,)), pltpu.SemaphoreType.DMA((7 * n,)),
                        pltpu.SemaphoreType.DMA((n,))],
        compiler_params=_params(barrier_id=8),
    )(*shards)


def _adam_math(w, g, m, v):
    m = ADAM_B1 * m + (1.0 - ADAM_B1) * g
    v = ADAM_B2 * v + (1.0 - ADAM_B2) * (g * g)
    m_hat = m / (1.0 - ADAM_B1 ** ADAM_STEP)
    v_hat = v / (1.0 - ADAM_B2 ** ADAM_STEP)
    delta = -ADAM_LR * (m_hat / (jnp.sqrt(v_hat) + ADAM_EPS) + ADAM_WD * w)
    return delta, m, v


def _adamw_reduced(tensors, n_steps):
    n_far = len(CHIP_FLIPS)

    def body(*refs):
        ins, outs = refs[:6 * len(tensors)], refs[6 * len(tensors):]
        for k in range(len(tensors)):
            w_ref, m_ref, v_ref, own_ref, sib_ref, far_ref = ins[6 * k:6 * k + 6]
            g_ref, d_ref, nm_ref, nv_ref = outs[4 * k:4 * k + 4]
            g = own_ref[...] + sib_ref[...].astype(F32)
            for j in range(n_far):
                g = g + far_ref[j].astype(F32)
            g_ref[...] = g
            d_ref[...], nm_ref[...], nv_ref[...] = _adam_math(w_ref[...], g, m_ref[...], v_ref[...])

    in_specs, out_specs, out_shape = [], [], []
    for w, *_ in tensors:
        rows, cols = w.shape
        tile = pl.BlockSpec((rows // n_steps, cols), lambda i: (i, 0))
        in_specs += [tile] * 5 + [pl.BlockSpec((n_far, rows // n_steps, cols), lambda i: (0, i, 0))]
        out_specs += [tile] * 4
        out_shape += [jax.ShapeDtypeStruct((rows, cols), F32)] * 4
    res = pl.pallas_call(
        body, name="adamw_reduced", grid=(n_steps,), in_specs=in_specs, out_specs=out_specs, out_shape=out_shape,
        compiler_params=_params("parallel"),
    )(*[a for tensor in tensors for a in tensor])
    return [res[4 * k:4 * k + 4] for k in range(len(tensors))]


SMALL_PARAMS = ("pre_mix_norm", "post_mix_norm", "pre_mlp_norm", "post_mlp_norm", "attn_group_norm", "conv_group_norm",
                "conv_w", "attn_sinks")


SMALL_WIDTHS = (D_MODEL, CONV_W, ATTN_W, 128, D_MODEL)


def _small_tail(gathered, dev, weights, first_moments, second_moments):
    n = len(SMALL_PARAMS)
    conv_shard = CONV_W // N_DEV

    def body(dev_ref, sums_ref, *refs):
        w_refs, m_refs, v_refs = refs[:n], refs[n:2 * n], refs[2 * n:3 * n]
        loss_ref, outs = refs[3 * n], refs[3 * n + 1:]
        total = sums_ref[0]
        for d in range(1, N_DEV):
            total = total + sums_ref[d]
        starts = [sum(SMALL_WIDTHS[:i]) for i in range(len(SMALL_WIDTHS))]
        mid, conv, gain, sink, inp = (total[:, a:a + w_] for a, w_ in zip(starts, SMALL_WIDTHS))
        loss_ref[...] = (0.5 / D_MODEL) * jnp.sum(mid[ROW_LOSS:ROW_LOSS + 1, :], axis=1, keepdims=True)
        conv_rows = conv[ROW_CW0:ROW_CW0 + 3, :]
        conv_g = jnp.zeros((3, conv_shard), F32)
        for d in range(N_DEV):
            conv_g = conv_g + jnp.where(dev_ref[0] == d, conv_rows[:, conv_shard * d:conv_shard * (d + 1)], 0.0)
        grads = [inp[0:1, :], mid[ROW_G2:ROW_G2 + 1, :], mid[ROW_G3:ROW_G3 + 1, :], mid[ROW_G4:ROW_G4 + 1, :],
                 gain[0:1, :], conv[ROW_GCONV:ROW_GCONV + 1, :], conv_g, sink[0:1, :N_HEADS]]
        for i, g in enumerate(grads):
            parts = [(..., g)] if len(w_refs[i].shape) == 2 else [(r, g[r:r + 1, :]) for r in range(g.shape[0])]
            for at, g_at in parts:
                delta, new_m, new_v = _adam_math(w_refs[i][at], g_at, m_refs[i][at], v_refs[i][at])
                outs[i][at], outs[n + i][at], outs[2 * n + i][at], outs[3 * n + i][at] = g_at, delta, new_m, new_v

    params = list(weights) + list(first_moments) + list(second_moments)
    shapes = [jax.ShapeDtypeStruct(w.shape, F32) for w in weights]
    res = pl.pallas_call(
        body, name="small_tail", grid=(1,),
        in_specs=[pl.BlockSpec(memory_space=pltpu.SMEM), _full(gathered.shape)] + [_full(p.shape) for p in params],
        out_specs=[_full((1, 1))] + [_full(sh.shape) for sh in shapes] * 4,
        out_shape=[jax.ShapeDtypeStruct((1, 1), F32)] + shapes * 4,
    )(dev, gathered, *params)
    return res[0], [res[1 + k * n:1 + (k + 1) * n] for k in range(4)]


TOKEN_TILE = 512
MID_TILE = 256
MID_CHUNK = 1024
MID_CHUNKS = D_FF // MID_CHUNK
ADAM_STEPS = 2


def _local_grads(x, target, g1, w_in_shard, conv_shard, sinks, g_attn, g_conv, g2, g3, g4, shards, order):
    t = x.shape[0]
    tm = min(TOKEN_TILE, t)
    rope = _rope_tables(t)
    qkv, gates, mconv, w_in, conv_w, gathered = _in_proj_fwd(x, g1, w_in_shard, conv_shard, g_conv, rope, tm, shards,
                                                             (False, True, False))
    attn, mattn, (w_out, w_up, w_down) = _attn_fwd(qkv, sinks, g_attn, shards, gathered)
    act, dup, hn2t, dmo, dmix, dh, dmixed, small_mid = _mid(
        mattn, mconv, x, target, g2, g3, g4, w_out.reshape(D_MODEL, D_MODEL),
        w_up, w_down.reshape(D_FF, D_MODEL), min(MID_TILE, t))
    up_own, up_sib, up_sums = _dw_pair_sums((hn2t, dup), order, "up", "dw_up", 2)
    down_own, down_sib, down_sums, up_far = _dw_pair_sums((act, dmo), order, "down", "dw_down", 3, ride=up_sums)
    out_own, out_sib, out_sums = _dw_pair_sums((mattn, mconv, dmix), order, "out", "dw_out", 4)
    dgates, small_conv = _conv_bwd(dmixed, gates, g_conv, conv_w, tm)
    dqkv, dsink, dg_attn, down_far = _attn_bwd(qkv, dmixed, attn, g_attn, sinks, rope, down_sums)
    grad_x_head, dw_in, small_in, out_far = _in_proj_bwd(dqkv, dgates, x, dh, g1, w_in, tm, out_sums)
    grad_x, small_in = _grad_x_rest(dqkv, dgates, x, dh, g1, w_in, tm, grad_x_head, small_in)
    dw_out, dw_up, dw_down = (out_own, out_sib, out_far), (up_own, up_sib, up_far), (down_own, down_sib, down_far)
    return grad_x, dw_in, dw_out, dw_up, dw_down, (small_mid, small_conv, dg_attn, dsink, small_in)


def kernel(x, pre_mix_norm, w_in, conv_w, attn_sinks, attn_group_norm, conv_group_norm, w_out, post_mix_norm, pre_mlp_norm, w_up, w_down, post_mlp_norm, loss_target, m_pre_mix_norm, m_w_in, m_conv_w, m_attn_sinks, m_attn_group_norm, m_conv_group_norm, m_w_out, m_post_mix_norm, m_pre_mlp_norm, m_w_up, m_w_down, m_post_mlp_norm, v_pre_mix_norm, v_w_in, v_conv_w, v_attn_sinks, v_attn_group_norm, v_conv_group_norm, v_w_out, v_post_mix_norm, v_pre_mlp_norm, v_w_up, v_w_down, v_post_mlp_norm):
    xi, yi, ci = _mesh_pos()
    chip = 2 * xi + yi
    dev = 2 * chip + ci

    order = _block_order(dev)

    shards = [w_out[0].astype(BF16), w_up[0].astype(BF16), w_down[0].astype(BF16)]

    turned = lambda a: jnp.swapaxes(a, 1, 2)
    grad_x, dw_in, dw_out, dw_up, dw_down, smalls = _local_grads(
        x[0], loss_target[0], pre_mix_norm, turned(w_in)[0].astype(BF16), conv_w[0], attn_sinks, attn_group_norm, conv_group_norm,
        post_mix_norm, pre_mlp_norm, post_mlp_norm, shards, order)

    blocks = {"w_in": (turned(w_in), turned(m_w_in), turned(v_w_in), dw_in), "w_out": (w_out, m_w_out, v_w_out, dw_out),
              "w_up": (w_up, m_w_up, v_w_up, dw_up), "w_down": (w_down, m_w_down, v_w_down, dw_down)}
    big = {}
    for names in (("w_in", "w_out", "w_down"), ("w_up",)):
        stepped = _adamw_reduced([(w[0], m[0], v[0], *dw) for w, m, v, dw in (blocks[nm] for nm in names)], ADAM_STEPS)
        for nm, res in zip(names, stepped):
            big[nm] = [a[None] for a in res]
    big["w_in"] = [turned(a) for a in big["w_in"]]

    flat = lambda a: a.reshape(-1, a.shape[-1]) if a.ndim < 3 else a.reshape(a.shape[1], 1, a.shape[2])
    loss, small = _small_tail(
        _all_gather([jnp.concatenate(smalls, axis=1)], "gather_small")[0], dev.reshape(1).astype(jnp.int32),
        [flat(a) for a in (pre_mix_norm, post_mix_norm, pre_mlp_norm, post_mlp_norm, attn_group_norm, conv_group_norm,
                           conv_w, attn_sinks)],
        [flat(a) for a in (m_pre_mix_norm, m_post_mix_norm, m_pre_mlp_norm, m_post_mlp_norm, m_attn_group_norm,
                           m_conv_group_norm, m_conv_w, m_attn_sinks)],
        [flat(a) for a in (v_pre_mix_norm, v_post_mix_norm, v_pre_mlp_norm, v_post_mlp_norm, v_attn_group_norm,
                           v_conv_group_norm, v_conv_w, v_attn_sinks)])

    order = ("pre_mix_norm", "w_in", "conv_w", "attn_sinks", "attn_group_norm", "conv_group_norm", "w_out",
             "post_mix_norm", "pre_mlp_norm", "w_up", "w_down", "post_mlp_norm")
    shape_of = {"conv_w": conv_w.shape}
    outs = []
    for k in range(4):
        by_name = dict(zip(SMALL_PARAMS, small[k]))
        outs += [big[nm][k] if nm in big else by_name[nm].reshape(shape_of.get(nm, by_name[nm].shape)) for nm in order]
    loss = loss.reshape(())
    return (loss, grad_x[None], *outs)
```

```python
import jax
import jax.numpy as jnp
import numpy as np
from jax import lax
from jax.experimental import pallas as pl
from jax.experimental.pallas import tpu as pltpu

F32 = jnp.float32
BF16 = jnp.bfloat16

D_MODEL = 1024
HEAD_DIM = 64
ATTN_W = 512
CONV_W = 512
N_HEADS = 8
N_KV = 2
GROUP = 4
KV_W = 128
QKV_W = ATTN_W + 2 * KV_W
GATES_W = 3 * CONV_W
IN_COLS = QKV_W + GATES_W
D_FF = 4096
FF_CHUNK = 512
BLOCK = 128
ROT_HALF = 8
ROPE_THETA = 500000.0
NORM_EPS = 1e-6
NEG_INF = -1e30
ATTN_SCALE = 0.125
N_DEV = 8
N_CHIPS = 4
IN_SHARD = IN_COLS // N_DEV

ADAM_LR = 0.001
ADAM_B1 = 0.9
ADAM_B2 = 0.999
ADAM_EPS = 1e-08
ADAM_WD = 0.01
ADAM_STEP = 10

V7X_VMEM_BYTES = 64 * 1024 * 1024
VMEM_LIMIT = V7X_VMEM_BYTES - 2 * 1024 * 1024

MESH = pl.DeviceIdType.MESH
HBM_SPEC = pl.BlockSpec(memory_space=pltpu.HBM)


def _params(*sem, barrier_id=None):
    return pltpu.CompilerParams(dimension_semantics=sem or None, vmem_limit_bytes=VMEM_LIMIT, collective_id=barrier_id)


def _mm(a, b):
    return jnp.dot(a, b, preferred_element_type=F32)


def _mm_nt(a, b):
    return lax.dot_general(a, b, (((1,), (1,)), ((), ())), preferred_element_type=F32)


def _mm_tn(a, b):
    return lax.dot_general(a, b, (((0,), (0,)), ((), ())), preferred_element_type=F32)


def _inv_rms(x):
    return lax.rsqrt(jnp.mean(x * x, axis=-1, keepdims=True) + NORM_EPS)


def _rms_bwd(xhat, r, gain, dy):
    gy = dy * gain
    return r * (gy - xhat * jnp.mean(gy * xhat, axis=-1, keepdims=True)), dy * xhat


def _colsum(a):
    return jnp.sum(a, axis=0, keepdims=True)


def _full(shape):
    zeros = (0,) * len(shape)
    return pl.BlockSpec(shape, lambda *_: zeros)


def _resident(shape):
    zeros = (0,) * len(shape)
    return pl.BlockSpec(shape, lambda *_: zeros, pipeline_mode=pl.Buffered(1))


def _rope_tables(t):
    pos = np.arange(t, dtype=np.float32)
    inv_freq = (ROPE_THETA ** (-np.arange(0, 2 * ROT_HALF, 2, dtype=np.float64) / (2 * ROT_HALF))).astype(np.float32)
    ang = (pos[:, None] * inv_freq[None, :]).astype(np.float64)
    cos, sin = np.cos(ang).astype(np.float32), np.sin(ang).astype(np.float32)
    zeros8 = np.zeros((t, ROT_HALF), np.float32)
    rest = np.zeros((t, HEAD_DIM - 2 * ROT_HALF), np.float32)
    c_head = np.concatenate([cos, cos, rest + 1.0], axis=1)
    s1_head = np.concatenate([zeros8, sin, rest], axis=1)
    s2_head = np.concatenate([-sin, zeros8, rest], axis=1)
    two = lambda a: jnp.asarray(np.concatenate([a, a], axis=1))
    return two(c_head), two(s1_head), two(s2_head)


def _rope(v, c, s1, s2):
    return v * c + pltpu.roll(v, ROT_HALF, 1) * s1 + pltpu.roll(v, 128 - ROT_HALF, 1) * s2


def _rope_transpose(dv, c, s1, s2):
    return dv * c + pltpu.roll(dv * s1, 128 - ROT_HALF, 1) + pltpu.roll(dv * s2, ROT_HALF, 1)


def _shift_rows_down(u, prev, k):
    row = lax.broadcasted_iota(jnp.int32, u.shape, 0)
    out = pltpu.roll(u, k, 0)
    for r in range(k):
        out = jnp.where(row == r, prev[8 - k + r:8 - k + r + 1, :], out)
    return out


def _shift_rows_up(u, nxt, k):
    n = u.shape[0]
    row = lax.broadcasted_iota(jnp.int32, u.shape, 0)
    out = pltpu.roll(u, n - k, 0)
    for r in range(k):
        out = jnp.where(row == n - k + r, nxt[r:r + 1, :], out)
    return out


def _conv3(u, u1, u2, w):
    return (w[0:1, :] * u2 + w[1:2, :] * u1) + w[2:3, :] * u


def _mesh_pos():
    return lax.axis_index("x"), lax.axis_index("y"), lax.axis_index("c")


def _slot(ref, pos):
    dev = 4 * pos[0] + 2 * pos[1] + pos[2]
    if len(ref.shape) == 2:
        width = ref.shape[1] // N_DEV
        return ref.at[:, pl.ds(pl.multiple_of(dev * width, width), width)]
    return ref.at[dev]


def _gathered_shape(shard, by_cols):
    if by_cols:
        return jax.ShapeDtypeStruct((shard.shape[0], N_DEV * shard.shape[1]), shard.dtype)
    return jax.ShapeDtypeStruct((N_DEV,) + shard.shape, shard.dtype)


def _enter_with(peers):
    barrier = pltpu.get_barrier_semaphore()
    for peer in peers:
        pl.semaphore_signal(barrier, inc=1, device_id=peer, device_id_type=MESH)
    pl.semaphore_wait(barrier, len(peers))


def _sibling_and_chips(x, y, c):
    return [(x, y, 1 - c), (1 - x, y, c), (x, 1 - y, c), (1 - x, 1 - y, c)]


def _push(src, dst, sems, k, to):
    send_sems, recv_sems = sems
    return pltpu.make_async_remote_copy(src_ref=src, dst_ref=dst, send_sem=send_sems.at[k], recv_sem=recv_sems.at[k],
                                        device_id=to, device_id_type=MESH)


def _gather_steps(shards, outs, send_sems, recv_sems, local_sems):
    n = len(shards)
    x, y, c = _mesh_pos()
    me, sibling = (x, y, c), (x, y, 1 - c)
    chips = [(1 - x, y), (x, 1 - y), (1 - x, 1 - y)]

    def copy(i, k, block, to, src=None):
        dst = _slot(outs[i], block)
        return _push(dst if src is None else src, dst, (send_sems, recv_sems), 7 * i + k, to)

    mine = [pltpu.make_async_copy(shards[i], _slot(outs[i], me), local_sems.at[i]) for i in range(n)]
    first = []
    for i in range(n):
        first.append(copy(i, 0, me, sibling, src=shards[i]))
        first += [copy(i, 1 + j, me, (*chip, c), src=shards[i]) for j, chip in enumerate(chips)]

    def start():
        for cp in mine + first:
            cp.start()

    def finish():
        passed = []
        for j, chip in enumerate(chips):
            for i in range(n):
                copy(i, 1 + j, (*chip, c), me).wait_recv()
                cp = copy(i, 4 + j, (*chip, c), sibling)
                cp.start()
                passed.append(cp)
        for i in range(n):
            copy(i, 0, sibling, me).wait_recv()
            for j, chip in enumerate(chips):
                copy(i, 4 + j, (*chip, 1 - c), me).wait_recv()
        for cp in first + passed:
            cp.wait_send()
        for cp in mine:
            cp.wait()

    return start, finish


def _gather_near(first, last, shards, outs, sems, local_sems):
    x, y, c = _mesh_pos()
    me, peers = (x, y, c), [(x, y, 1 - c), (1 - x, y, c), (x, 1 - y, c)]
    n = len(shards)
    local = [pltpu.make_async_copy(shards[i], _slot(outs[i], me), local_sems.at[i]) for i in range(n)]
    sends = [_push(shards[i], _slot(outs[i], me), sems, 3 * i + k, peers[k]) for i in range(n) for k in range(3)]
    arrivals = [_push(shards[i], _slot(outs[i], peers[k]), sems, 3 * i + k, peers[k]) for i in range(n) for k in range(3)]

    def start():
        for cp in local + sends:
            cp.start()

    if first is not None:
        pl.when(first)(start)

    @pl.when(last)
    def _():
        for cp in sends:
            cp.wait_send()
        for cp in arrivals:
            cp.wait_recv()
        for cp in local:
            cp.wait()

    return start


def _relay_route(x, y, c):
    south = c == 0
    via = (jnp.where(south, 1 - x, x), jnp.where(south, y, 1 - y))
    to = (jnp.where(south, x, 1 - x), jnp.where(south, 1 - y, y))
    return via, to


def _gather_far(first, middle, last, shards, ins, outs, sems):
    x, y, c = _mesh_pos()
    sibling = (x, y, 1 - c)
    chips = [(1 - x, y), (x, 1 - y), (1 - x, 1 - y)]
    via, to = _relay_route(x, y, c)
    n = len(shards)
    diag_send = [_push(_slot(ins[i], (*via, c)), _slot(outs[i], (*via, c)), sems, 4 * i, (*to, c)) for i in range(n)]
    diag_arrival = [_push(shards[i], _slot(outs[i], (*chips[2], c)), sems, 4 * i, (*to, c)) for i in range(n)]
    passed = [[_push(_slot(ins[i], (*chips[j], c)), _slot(outs[i], (*chips[j], c)), sems, 4 * i + 1 + j, sibling)
               for i in range(n)] for j in range(3)]
    from_sibling = [_push(shards[i], _slot(outs[i], (*chips[j], 1 - c)), sems, 4 * i + 1 + j, sibling)
                    for i in range(n) for j in range(3)]

    @pl.when(first)
    def _():
        for cp in diag_send + passed[0] + passed[1]:
            cp.start()

    @pl.when(middle)
    def _():
        for cp in diag_arrival:
            cp.wait_recv()
        for cp in passed[2]:
            cp.start()

    @pl.when(last)
    def _():
        for cp in from_sibling:
            cp.wait_recv()
        for cp in diag_send + passed[0] + passed[1] + passed[2]:
            cp.wait_send()


def _in_proj_fwd(x, g1, w_in, conv_w, g_conv, rope, tm, shards, by_cols):
    t = x.shape[0]
    rc, rs1, rs2 = rope
    n = len(shards)
    n_tiles = t // tm

    def body(*refs):
        x_ref, g1_ref, w_ref, cw_ref, gc_ref, c_ref, s1_ref, s2_ref = refs[:8]
        shard_refs = refs[8:8 + n]
        qkv_ref, gates_ref, mconv_ref, w_full_ref, cw_full_ref = refs[8 + n:13 + n]
        gathered = refs[13 + n:13 + 2 * n]
        carry_ref, w_land, cw_land, hn_ref = refs[13 + 2 * n:17 + 2 * n]
        now_sems = refs[17 + 2 * n:20 + 2 * n]
        step = pl.program_id(0)
        start_later_weights = _gather_near(None, step == 2 * n_tiles - 1, shard_refs, gathered,
                                           refs[20 + 2 * n:22 + 2 * n], refs[22 + 2 * n]) if n else None
        start_w_in, finish_w_in = _gather_steps([w_ref, cw_ref], [w_land, cw_land], *now_sems)

        @pl.when(step == 0)
        def _():
            carry_ref[...] = jnp.zeros_like(carry_ref)
            _enter_with(_sibling_and_chips(*_mesh_pos()))
            start_w_in()
            if start_later_weights is not None:
                start_later_weights()

        @pl.when(step < n_tiles)
        def _():
            xv = x_ref[...]
            hn_ref[step] = ((xv * _inv_rms(xv)) * g1_ref[...]).astype(BF16)

        @pl.when(step == n_tiles)
        def _():
            finish_w_in()
            conv_shard = CONV_W // N_DEV
            for d in range(N_DEV):
                w_full_ref[IN_SHARD * d:IN_SHARD * (d + 1), :] = w_land[d]
                cw_full_ref[:, conv_shard * d:conv_shard * (d + 1)] = cw_land[d]

        @pl.when(step >= n_tiles)
        def _():
            proj = _mm_nt(hn_ref[step - n_tiles], w_full_ref[...])
            c, s1, s2 = c_ref[...], s1_ref[...], s2_ref[...]
            for ci in range((ATTN_W + KV_W) // 128):
                sl = slice(128 * ci, 128 * (ci + 1))
                qkv_ref[:, sl] = _rope(proj[:, sl], c, s1, s2).astype(BF16)
            qkv_ref[:, ATTN_W + KV_W:QKV_W] = proj[:, ATTN_W + KV_W:QKV_W].astype(BF16)
            gates = proj[:, QKV_W:]
            gates_ref[...] = gates
            gb, gcc, xin = gates[:, :CONV_W], gates[:, CONV_W:2 * CONV_W], gates[:, 2 * CONV_W:]
            u = gcc * xin
            prev = carry_ref[...]
            conv = gb * _conv3(u, _shift_rows_down(u, prev, 1), _shift_rows_down(u, prev, 2), cw_full_ref[...])
            carry_ref[...] = u[tm - 8:tm, :]
            mconv_ref[...] = ((conv * _inv_rms(conv)) * gc_ref[...]).astype(BF16)

    first_pass = pl.BlockSpec((tm, D_MODEL), lambda i: (jnp.minimum(i, n_tiles - 1), 0))
    tile = lambda w_: pl.BlockSpec((tm, w_), lambda i: (jnp.maximum(i - n_tiles, 0), 0))
    sems = lambda k: pltpu.SemaphoreType.DMA((k,))
    res = pl.pallas_call(
        body, name="in_proj_fwd", grid=(2 * n_tiles,),
        in_specs=[first_pass, _full((1, D_MODEL)), HBM_SPEC, HBM_SPEC, _full((1, CONV_W)), tile(128), tile(128),
                  tile(128)] + [HBM_SPEC] * n,
        out_specs=[tile(QKV_W), tile(GATES_W), tile(CONV_W), _full((IN_COLS, D_MODEL)), _full((3, CONV_W))]
        + [HBM_SPEC] * n,
        out_shape=[jax.ShapeDtypeStruct((t, QKV_W), BF16), jax.ShapeDtypeStruct((t, GATES_W), F32),
                   jax.ShapeDtypeStruct((t, CONV_W), BF16), jax.ShapeDtypeStruct((IN_COLS, D_MODEL), BF16),
                   jax.ShapeDtypeStruct((3, CONV_W), F32)]
        + [_gathered_shape(s, cols) for s, cols in zip(shards, by_cols)],
        scratch_shapes=[pltpu.VMEM((8, CONV_W), F32), pltpu.VMEM((N_DEV,) + w_in.shape, BF16),
                        pltpu.VMEM((N_DEV,) + conv_w.shape, F32), pltpu.VMEM((n_tiles, tm, D_MODEL), BF16),
                        sems(14), sems(14), sems(2)]
        + ([sems(3 * n), sems(3 * n), sems(n)] if n else []),
        compiler_params=_params("arbitrary", barrier_id=0),
    )(x, g1, w_in, conv_w, g_conv, rc, rs1, rs2, *shards)
    return res[0], res[1], res[2], res[3], res[4], list(res[5:])


GROUP_COLS = GROUP * BLOCK
ATTN_STEP_BLOCKS = 8


def _attn_masks(has_prev):
    key = lax.broadcasted_iota(jnp.int32, (2 * BLOCK, GROUP_COLS), 0)
    query = lax.broadcasted_iota(jnp.int32, (2 * BLOCK, GROUP_COLS), 1) & (BLOCK - 1)
    band = (key > query) & (key <= query + BLOCK)
    return [band & ((key >= BLOCK) | has_prev)] + [band] * (ATTN_STEP_BLOCKS - 1)


def _heads_side_by_side(at, g, b):
    heads = [at[HEAD_DIM * (GROUP * g + hh):HEAD_DIM * (GROUP * g + hh + 1), BLOCK * b:BLOCK * (b + 1)] for hh in range(GROUP)]
    return jnp.concatenate(heads, axis=1)


def _to_token_rows(parts):
    rows = [jnp.concatenate([parts[b][g][:, BLOCK * hh:BLOCK * (hh + 1)] for b in range(ATTN_STEP_BLOCKS)], axis=1)
            for g in range(N_KV) for hh in range(GROUP)]
    return jnp.concatenate(rows, axis=0).T


def _group_sinks(sink_ref, g):
    head = lax.broadcasted_iota(jnp.int32, (1, GROUP_COLS), 1) // BLOCK
    out = jnp.full((1, GROUP_COLS), sink_ref[0, GROUP * g], F32)
    for hh in range(1, GROUP):
        out = jnp.where(head == hh, sink_ref[0, GROUP * g + hh], out)
    return out


def _attn_probs(qt, kk, sink, valid):
    s = jnp.where(valid, _mm(kk, qt), NEG_INF)
    m = jnp.maximum(jnp.max(s, axis=0, keepdims=True), sink)
    p = jnp.exp(s - m)
    psink = jnp.exp(sink - m)
    inv_l = 1.0 / (jnp.sum(p, axis=0, keepdims=True) + psink)
    return p * inv_l, psink * inv_l


ATTN_STEP = ATTN_STEP_BLOCKS * BLOCK
ATTN_KEYS = ATTN_STEP + BLOCK


def _qkv_specs(order):
    prev = lambda i: jnp.maximum(ATTN_STEP_BLOCKS * order(i) - 1, 0)
    kcol, vcol = ATTN_W // KV_W, ATTN_W // KV_W + 1
    return [pl.BlockSpec((ATTN_STEP, ATTN_W), lambda i: (order(i), 0)),
            pl.BlockSpec((BLOCK, KV_W), lambda i: (prev(i), kcol)), pl.BlockSpec((ATTN_STEP, KV_W), lambda i: (order(i), kcol)),
            pl.BlockSpec((BLOCK, KV_W), lambda i: (prev(i), vcol)), pl.BlockSpec((ATTN_STEP, KV_W), lambda i: (order(i), vcol))]


def _attn_fwd(qkv, sinks, g_attn, shards, gathered):
    t = qkv.shape[0]
    n = len(shards)

    def body(*refs):
        sink_ref, q_ref, kp_ref, kc_ref, vp_ref, vc_ref, ga_ref = refs[:7]
        attn_ref, mattn_ref = refs[7 + 2 * n:9 + 2 * n]
        step = pl.program_id(0)
        if n:
            @pl.when(step == 0)
            def _():
                x, y, c = _mesh_pos()
                _enter_with([(x, y, 1 - c), (*_relay_route(x, y, c)[1], c)])

            n_steps = t // ATTN_STEP
            _gather_far(step == 0, step == n_steps // 2, step == n_steps - 1, refs[7:7 + n], refs[7 + n:7 + 2 * n],
                        refs[9 + 2 * n:9 + 3 * n], refs[9 + 3 * n:11 + 3 * n])
        qt = (q_ref[...] * ATTN_SCALE).T
        keys = jnp.concatenate([kp_ref[...], kc_ref[...]], axis=0)
        vals = jnp.concatenate([vp_ref[...], vc_ref[...]], axis=0)
        sink = [_group_sinks(sink_ref, g) for g in range(N_KV)]
        masks = _attn_masks(step > 0)
        parts = []
        for b in range(ATTN_STEP_BLOCKS):
            window = slice(BLOCK * b, BLOCK * (b + 2))
            valid = masks[b]
            parts.append([])
            for g in range(N_KV):
                gs = slice(HEAD_DIM * g, HEAD_DIM * (g + 1))
                probs, _ = _attn_probs(_heads_side_by_side(qt, g, b), keys[window, gs], sink[g], valid)
                parts[b].append(_mm_tn(vals[window, gs], probs.astype(BF16)))
        attn = _to_token_rows(parts)
        attn_ref[...] = attn
        mattn_ref[...] = ((attn * _inv_rms(attn)) * ga_ref[...]).astype(BF16)

    blk = pl.BlockSpec((ATTN_STEP, ATTN_W), lambda j: (j, 0))
    res = pl.pallas_call(
        body, name="attn_fwd", grid=(t // ATTN_STEP,),
        in_specs=[pl.BlockSpec(memory_space=pltpu.SMEM)] + _qkv_specs(lambda j: j) + [_full((1, ATTN_W))]
        + [HBM_SPEC] * (2 * n),
        out_specs=[blk, blk] + [HBM_SPEC] * n,
        out_shape=[jax.ShapeDtypeStruct((t, ATTN_W), F32), jax.ShapeDtypeStruct((t, ATTN_W), BF16)]
        + [jax.ShapeDtypeStruct(g.shape, g.dtype) for g in gathered],
        input_output_aliases={7 + n + i: 2 + i for i in range(n)},
        scratch_shapes=[pltpu.SemaphoreType.DMA((4 * n,)), pltpu.SemaphoreType.DMA((4 * n,))] if n else [],
        compiler_params=_params("arbitrary", barrier_id=1 if n else None),
    )(sinks, qkv, qkv, qkv, qkv, qkv, g_attn, *shards, *gathered)
    return res[0], res[1], list(res[2:])


SMALL_ROWS = 8
ROW_LOSS, ROW_G2, ROW_G3, ROW_G4 = 0, 1, 2, 3


def _mid(mattn, mconv, x, target, g2, g3, g4, w_out, w_up, w_down, tm):
    t = x.shape[0]

    def body(ma_ref, mc_ref, x_ref, t_ref, g2_ref, g3_ref, g4_ref, wo_ref, wu_ref, wd_ref,
             act_ref, dup_ref, hn2t_ref, dmo_ref, dmix_ref, dh_ref, dmixed_ref, small_ref, up_ref):
        @pl.when(pl.program_id(0) == 0)
        def _():
            small_ref[...] = jnp.zeros_like(small_ref)

        g2, g3, g4 = g2_ref[...], g3_ref[...], g4_ref[...]
        mix_out = _mm(ma_ref[...], wo_ref[0:ATTN_W, :]) + _mm(mc_ref[...], wo_ref[ATTN_W:, :])
        r2 = _inv_rms(mix_out)
        mo_hat = mix_out * r2
        h = x_ref[...] + mo_hat * g2
        r3 = _inv_rms(h)
        h_hat = h * r3
        hn2 = (h_hat * g3).astype(BF16)
        hn2t_ref[...] = hn2.T
        for j in range(MID_CHUNKS):
            cols_j = slice(MID_CHUNK * j, MID_CHUNK * (j + 1))
            up = jnp.maximum(_mm(hn2, wu_ref[:, cols_j]), 0.0)
            up_ref[:, cols_j] = up.astype(BF16)
            act_ref[:, cols_j] = (up * up).astype(BF16)
        mlp = _mm(act_ref[...], wd_ref[...])
        r4 = _inv_rms(mlp)
        ml_hat = mlp * r4
        err = (h + ml_hat * g4) - t_ref[...]
        d_out = err * (1.0 / D_MODEL)
        d_mlp, dg4 = _rms_bwd(ml_hat, r4, g4, d_out)
        dmo = d_mlp.astype(BF16)
        dmo_ref[...] = dmo
        for j in range(MID_CHUNKS):
            cols_j = slice(MID_CHUNK * j, MID_CHUNK * (j + 1))
            dact = _mm_nt(dmo, wd_ref[cols_j, :])
            dup_ref[:, cols_j] = (dact * (2.0 * up_ref[:, cols_j].astype(F32))).astype(BF16)
        dhn2 = _mm_nt(dup_ref[...], wu_ref[...])
        dh_norm, dg3 = _rms_bwd(h_hat, r3, g3, dhn2)
        dh = d_out + dh_norm
        dh_ref[...] = dh
        d_mix, dg2 = _rms_bwd(mo_hat, r2, g2, dh)
        dmix = d_mix.astype(BF16)
        dmix_ref[...] = dmix
        dmixed_ref[...] = _mm_nt(dmix, wo_ref[...])
        small_ref[ROW_LOSS:ROW_LOSS + 1, :] += _colsum(err * err)
        small_ref[ROW_G2:ROW_G2 + 1, :] += _colsum(dg2)
        small_ref[ROW_G3:ROW_G3 + 1, :] += _colsum(dg3)
        small_ref[ROW_G4:ROW_G4 + 1, :] += _colsum(dg4)

    tile = lambda n: pl.BlockSpec((tm, n), lambda i: (i, 0))
    cols = lambda n: pl.BlockSpec((n, tm), lambda i: (0, i))
    gain = _full((1, D_MODEL))
    return pl.pallas_call(
        body, name="mid_fwd_bwd", grid=(t // tm,),
        in_specs=[tile(ATTN_W), tile(CONV_W), tile(D_MODEL), tile(D_MODEL), gain, gain, gain,
                  _resident((D_MODEL, D_MODEL)), _resident((D_MODEL, D_FF)), _resident((D_FF, D_MODEL))],
        out_specs=[tile(D_FF), tile(D_FF), cols(D_MODEL), tile(D_MODEL), tile(D_MODEL), tile(D_MODEL), tile(D_MODEL),
                   _full((SMALL_ROWS, D_MODEL))],
        out_shape=[jax.ShapeDtypeStruct((t, D_FF), BF16), jax.ShapeDtypeStruct((t, D_FF), BF16),
                   jax.ShapeDtypeStruct((D_MODEL, t), BF16), jax.ShapeDtypeStruct((t, D_MODEL), BF16),
                   jax.ShapeDtypeStruct((t, D_MODEL), BF16), jax.ShapeDtypeStruct((t, D_MODEL), F32),
                   jax.ShapeDtypeStruct((t, D_MODEL), F32), jax.ShapeDtypeStruct((SMALL_ROWS, D_MODEL), F32)],
        scratch_shapes=[pltpu.VMEM((tm, D_FF), BF16)],
        compiler_params=_params("arbitrary"),
    )(mattn, mconv, x, target, g2, g3, g4, w_out, w_up, w_down)


CHIP_FLIPS = ((1, 1), (1, 0), (0, 1))


def _block_order(dev):
    chip_masks = [4 * fx + 2 * fy for fx, fy in CHIP_FLIPS]
    masks = [m + 1 for m in chip_masks] + [1] + chip_masks + [0]
    return jnp.bitwise_xor(dev, jnp.asarray(masks, jnp.int32)).astype(jnp.int32)


def _other_chips(x, y, c):
    return [(1 - x if fx else x, 1 - y if fy else y, c) for fx, fy in CHIP_FLIPS]


def _dw_pair_sums(operands, order, which, name, barrier_id, ride=None):
    t = operands[-1].shape[0]
    n_far = len(CHIP_FLIPS)
    n_in = len(operands)
    n_ride = 0 if ride is None else 1
    out_chunk = D_MODEL // N_DEV
    if which == "up":
        rows, cols = D_MODEL, FF_CHUNK
        in_specs = [_resident((D_MODEL, t)), pl.BlockSpec((t, FF_CHUNK), lambda s, order_ref: (0, order_ref[s]))]
    elif which == "down":
        rows, cols = FF_CHUNK, D_MODEL
        in_specs = [pl.BlockSpec((t, FF_CHUNK), lambda s, order_ref: (0, order_ref[s])), _resident((t, D_MODEL))]
    else:
        rows, cols = out_chunk, D_MODEL
        half = pl.BlockSpec((t, out_chunk), lambda s, order_ref: (0, order_ref[s] % (N_DEV // 2)))
        in_specs = [half, half, _resident((t, D_MODEL))]

    def body(order_ref, *refs):
        own_ref, from_sib_ref, pair_ref = refs[n_in + n_ride:n_in + n_ride + 3]
        send_buf, land_buf, send_sems, recv_sems = refs[n_in + 2 * n_ride + 3:n_in + 2 * n_ride + 7]
        s_now = pl.program_id(0)
        x, y, c = _mesh_pos()
        sibling = (x, y, 1 - c)
        sems = (send_sems, recv_sems)

        @pl.when(s_now == 0)
        def _():
            _enter_with([sibling] + (_other_chips(x, y, c) if n_ride else []))

        if n_ride:
            _chip_exchange_beside(s_now == 0, s_now == N_DEV - 1, [refs[n_in]], [refs[n_in + 3 + n_ride]],
                                  refs[n_in + 2 * n_ride + 7:], enter=False)

        def hand_over(k):
            dst = land_buf.at[k] if k < n_far else from_sib_ref
            return _push(send_buf.at[k], dst, sems, k, sibling)

        if which == "out":
            ma_ref, mc_ref, b_ref = refs[:n_in]
            block = lax.cond(order_ref[s_now] < N_DEV // 2, lambda: _mm_tn(ma_ref[...], b_ref[...]),
                             lambda: _mm_tn(mc_ref[...], b_ref[...]))
        elif which == "down":
            block = _mm_tn(refs[0][...], refs[1][...])
        else:
            block = _mm(refs[0][...], refs[1][...])
        for k in range(n_far + 1):
            @pl.when(s_now == k)
            def _():
                send_buf[k] = block.astype(BF16)
                hand_over(k).start()

        for k in range(n_far):
            @pl.when(s_now == n_far + 1 + k)
            def _():
                hand_over(k).wait_recv()
                pair_ref[...] = (block + land_buf[k].astype(F32)).astype(BF16)

        @pl.when(s_now == N_DEV - 1)
        def _():
            own_ref[...] = block
            for k in range(n_far + 1):
                hand_over(k).wait_send()
            hand_over(n_far).wait_recv()

    rides = [] if ride is None else [ride]
    sems = lambda k: pltpu.SemaphoreType.DMA((k,))
    return pl.pallas_call(
        body, name=name,
        grid_spec=pltpu.PrefetchScalarGridSpec(
            num_scalar_prefetch=1, grid=(N_DEV,), in_specs=in_specs + [HBM_SPEC] * n_ride,
            out_specs=[pl.BlockSpec((rows, cols), lambda s, order_ref: (0, 0)), HBM_SPEC,
                       pl.BlockSpec((None, rows, cols), lambda s, order_ref: (jnp.clip(s - n_far - 1, 0, n_far - 1), 0, 0))]
            + [HBM_SPEC] * n_ride,
            scratch_shapes=[pltpu.VMEM((n_far + 1, rows, cols), BF16), pltpu.VMEM((n_far, rows, cols), BF16),
                            sems(n_far + 1), sems(n_far + 1)] + [sems(n_far), sems(n_far)] * n_ride),
        out_shape=[jax.ShapeDtypeStruct((rows, cols), F32), jax.ShapeDtypeStruct((rows, cols), BF16),
                   jax.ShapeDtypeStruct((n_far, rows, cols), BF16)]
        + [jax.ShapeDtypeStruct(r.shape, r.dtype) for r in rides],
        compiler_params=_params("arbitrary", barrier_id=barrier_id),
    )(order, *operands, *rides)


def _chip_exchange_beside(first, last, sums, outs, sems, enter=True):
    chips = _other_chips(*_mesh_pos())
    copies = [_push(sums[i].at[k], outs[i].at[k], sems, len(chips) * i + k, chip)
              for i in range(len(sums)) for k, chip in enumerate(chips)]

    @pl.when(first)
    def _():
        if enter:
            _enter_with(chips)
        for cp in copies:
            cp.start()

    @pl.when(last)
    def _():
        for cp in copies:
            cp.wait()


ROW_GCONV, ROW_CW0 = 1, 2


def _conv_bwd(dmixed, gates, g_conv, conv_w, tm):
    t = gates.shape[0]
    n = t // tm
    rev = lambda i: n - 1 - i

    def body(dm_ref, gates_ref, gprev_ref, gc_ref, cw_ref, dgates_ref, small_ref, carry_ref):
        i = pl.program_id(0)

        @pl.when(i == 0)
        def _():
            small_ref[...] = jnp.zeros_like(small_ref)
            carry_ref[...] = jnp.zeros_like(carry_ref)

        gates = gates_ref[...]
        gb, gcc, xin = gates[:, :CONV_W], gates[:, CONV_W:2 * CONV_W], gates[:, 2 * CONV_W:]
        u = gcc * xin
        gp = gprev_ref[...]
        uprev = jnp.where(rev(i) == 0, 0.0, gp[:, CONV_W:2 * CONV_W] * gp[:, 2 * CONV_W:])
        u1, u2 = _shift_rows_down(u, uprev, 1), _shift_rows_down(u, uprev, 2)
        w = cw_ref[...]
        c = _conv3(u, u1, u2, w)
        conv = gb * c
        rcv = _inv_rms(conv)
        c_hat = conv * rcv
        dconv, dgc = _rms_bwd(c_hat, rcv, gc_ref[...], dm_ref[...])
        dc = dconv * gb
        nxt = carry_ref[...]
        du = (w[2:3, :] * dc + w[1:2, :] * _shift_rows_up(dc, nxt, 1)) + w[0:1, :] * _shift_rows_up(dc, nxt, 2)
        carry_ref[...] = dc[0:8, :]
        dgates_ref[:, :CONV_W] = (dconv * c).astype(BF16)
        dgates_ref[:, CONV_W:2 * CONV_W] = (du * xin).astype(BF16)
        dgates_ref[:, 2 * CONV_W:] = (du * gcc).astype(BF16)
        small_ref[ROW_GCONV:ROW_GCONV + 1, :] += _colsum(dgc)
        small_ref[ROW_CW0:ROW_CW0 + 1, :] += _colsum(dc * u2)
        small_ref[ROW_CW0 + 1:ROW_CW0 + 2, :] += _colsum(dc * u1)
        small_ref[ROW_CW0 + 2:ROW_CW0 + 3, :] += _colsum(dc * u)

    tile = lambda w_: pl.BlockSpec((tm, w_), lambda i: (rev(i), 0))
    prev8 = pl.BlockSpec((8, GATES_W), lambda i: (jnp.maximum(rev(i) * (tm // 8) - 1, 0), 0))
    conv_half = pl.BlockSpec((tm, CONV_W), lambda i: (rev(i), ATTN_W // CONV_W))
    return pl.pallas_call(
        body, name="conv_bwd", grid=(n,),
        in_specs=[conv_half, tile(GATES_W), prev8, _full((1, CONV_W)), _full((3, CONV_W))],
        out_specs=[tile(GATES_W), _full((SMALL_ROWS, CONV_W))],
        out_shape=[jax.ShapeDtypeStruct((t, GATES_W), BF16), jax.ShapeDtypeStruct((SMALL_ROWS, CONV_W), F32)],
        scratch_shapes=[pltpu.VMEM((8, CONV_W), F32)],
        compiler_params=_params("arbitrary"),
    )(dmixed, gates, gates, g_conv, conv_w)


def _attn_bwd(qkv, dmixed, attn, g_attn, sinks, rope, sums):
    t = qkv.shape[0]
    n_steps = t // ATTN_STEP
    rev = lambda i: n_steps - 1 - i
    rc, rs1, rs2 = rope

    def body(sink_ref, q_ref, kp_ref, kc_ref, vp_ref, vc_ref, dm_ref, attn_ref, ga_ref, c_ref, s1_ref, s2_ref, sums_ref,
             dqkv_ref, dsink_ref, dgain_ref, arrived_ref, ck_ref, cv_ref, kacc_ref, vacc_ref, send_sems, recv_sems):
        i = pl.program_id(0)
        _chip_exchange_beside(i == 0, i == n_steps - 1, [sums_ref], [arrived_ref], (send_sems, recv_sems))

        @pl.when(i == 0)
        def _():
            dsink_ref[...] = jnp.zeros_like(dsink_ref)
            dgain_ref[...] = jnp.zeros_like(dgain_ref)
            ck_ref[...] = jnp.zeros_like(ck_ref)
            cv_ref[...] = jnp.zeros_like(cv_ref)

        kacc_ref[...] = jnp.zeros_like(kacc_ref)
        vacc_ref[...] = jnp.zeros_like(vacc_ref)
        a = attn_ref[...]
        ra = _inv_rms(a)
        dattn, dgain = _rms_bwd(a * ra, ra, ga_ref[...], dm_ref[...])
        dgain_ref[0:1, :] += _colsum(dgain)
        qt = (q_ref[...] * ATTN_SCALE).T
        dot = dattn.astype(BF16).T
        keys = jnp.concatenate([kp_ref[...], kc_ref[...]], axis=0)
        vals = jnp.concatenate([vp_ref[...], vc_ref[...]], axis=0)
        sink = [_group_sinks(sink_ref, g) for g in range(N_KV)]
        c, s1, s2 = c_ref[...], s1_ref[...], s2_ref[...]
        lane = lax.broadcasted_iota(jnp.int32, (1, 128), 1)
        dsink = jnp.zeros((1, 128), F32)
        masks = _attn_masks(rev(i) > 0)
        dq_parts = []
        for b in range(ATTN_STEP_BLOCKS):
            window = slice(BLOCK * b, BLOCK * (b + 2))
            valid = masks[b]
            dq_parts.append([])
            dk_parts, dv_parts = [], []
            for g in range(N_KV):
                gs = slice(HEAD_DIM * g, HEAD_DIM * (g + 1))
                kk, vv = keys[window, gs], vals[window, gs]
                qtg, dotg = _heads_side_by_side(qt, g, b), _heads_side_by_side(dot, g, b)
                probs, psink = _attn_probs(qtg, kk, sink[g], valid)
                dp = _mm(vv, dotg)
                delta = jnp.sum(probs * dp, axis=0, keepdims=True)
                ds = (probs * (dp - delta)).astype(BF16)
                sink_terms = psink * delta
                for hh in range(GROUP):
                    head_sum = jnp.sum(sink_terms[:, BLOCK * hh:BLOCK * (hh + 1)])
                    dsink = dsink + jnp.where(lane == GROUP * g + hh, -head_sum, 0.0)
                dq_parts[b].append(_mm_tn(kk * ATTN_SCALE, ds))
                dk_parts.append(_mm_nt(ds, qtg))
                dv_parts.append(_mm_nt(probs.astype(BF16), dotg))
            kacc_ref[window, :] += jnp.concatenate(dk_parts, axis=1)
            vacc_ref[window, :] += jnp.concatenate(dv_parts, axis=1)
        dq = _to_token_rows(dq_parts)
        for ci in range(ATTN_W // 128):
            sl = slice(128 * ci, 128 * (ci + 1))
            dqkv_ref[:, sl] = _rope_transpose(dq[:, sl], c, s1, s2).astype(BF16)
        kacc_ref[ATTN_STEP:, :] += ck_ref[...]
        vacc_ref[ATTN_STEP:, :] += cv_ref[...]
        ck_ref[...] = kacc_ref[:BLOCK, :]
        cv_ref[...] = vacc_ref[:BLOCK, :]
        dqkv_ref[:, ATTN_W:ATTN_W + KV_W] = _rope_transpose(kacc_ref[BLOCK:, :], c, s1, s2).astype(BF16)
        dqkv_ref[:, ATTN_W + KV_W:] = vacc_ref[BLOCK:, :].astype(BF16)
        dsink_ref[0:1, :] += dsink

    blk = lambda w_: pl.BlockSpec((ATTN_STEP, w_), lambda i: (rev(i), 0))
    return pl.pallas_call(
        body, name="attn_bwd", grid=(n_steps,),
        in_specs=[pl.BlockSpec(memory_space=pltpu.SMEM)] + _qkv_specs(rev)
        + [blk(ATTN_W), blk(ATTN_W), _full((1, ATTN_W)), blk(128), blk(128), blk(128), HBM_SPEC],
        out_specs=[blk(QKV_W), _full((8, 128)), _full((SMALL_ROWS, ATTN_W)), HBM_SPEC],
        out_shape=[jax.ShapeDtypeStruct((t, QKV_W), BF16), jax.ShapeDtypeStruct((8, 128), F32),
                   jax.ShapeDtypeStruct((SMALL_ROWS, ATTN_W), F32), jax.ShapeDtypeStruct(sums.shape, sums.dtype)],
        scratch_shapes=[pltpu.VMEM((BLOCK, KV_W), F32), pltpu.VMEM((BLOCK, KV_W), F32),
                        pltpu.VMEM((ATTN_KEYS, KV_W), F32), pltpu.VMEM((ATTN_KEYS, KV_W), F32),
                        pltpu.SemaphoreType.DMA((len(CHIP_FLIPS),)), pltpu.SemaphoreType.DMA((len(CHIP_FLIPS),))],
        compiler_params=_params("arbitrary", barrier_id=6),
    )(sinks, qkv, qkv, qkv, qkv, qkv, dmixed, attn, g_attn, rc, rs1, rs2, sums)


def _grad_x_tile(dq, dg, x_hat, r, g1, w_ref, dh):
    dhn = _mm(dq, w_ref[:QKV_W, :]) + _mm(dg, w_ref[QKV_W:, :])
    dx, dg1 = _rms_bwd(x_hat, r, g1, dhn)
    return dh + dx, _colsum(dg1)


def _in_proj_bwd(dqkv, dgates, x, dh, g1, w_in, tm, out_sums):
    t = x.shape[0]
    n = t // tm
    n_cover = max(n // 2, 1)
    n_steps = n + n_cover
    n_far = len(CHIP_FLIPS)
    shard = (IN_SHARD, D_MODEL)

    def body(dq_ref, dg_ref, x_ref, dh_ref, g1_ref, w_ref, osums_ref,
             dx_ref, own_ref, sib_ref, far_ref, dg1_ref, oarrived_ref,
             acc_ref, send_buf, land_buf, pair_buf, d2d_send, d2d_recv, ici_send, ici_recv, o_send, o_recv):
        i = pl.program_id(0)
        x_pos, y_pos, c = _mesh_pos()
        my_chip = 2 * x_pos + y_pos
        sibling = (x_pos, y_pos, 1 - c)
        @pl.when(i == 0)
        def _():
            _enter_with(_sibling_and_chips(x_pos, y_pos, c))

        _chip_exchange_beside(i == 0, i == n_steps - 1, [osums_ref], [oarrived_ref], (o_send, o_recv), enter=False)

        def rows(d):
            return slice(IN_SHARD * d, IN_SHARD * (d + 1))

        def hand_over(chip):
            return _push(send_buf.at[chip], land_buf.at[chip], (d2d_send, d2d_recv), chip, sibling)

        def to_chip(chip, rel):
            return pltpu.make_async_remote_copy(
                src_ref=pair_buf.at[chip], dst_ref=far_ref.at[rel - 1], send_sem=ici_send.at[rel - 1],
                recv_sem=ici_recv.at[rel - 1], device_id=(chip // 2, chip % 2, c), device_id_type=MESH)

        @pl.when(i == 0)
        def _():
            acc_ref[...] = jnp.zeros_like(acc_ref)
            dg1_ref[...] = jnp.zeros_like(dg1_ref)

        def normed_x():
            xv = x_ref[...]
            r = _inv_rms(xv)
            return xv * r, r

        @pl.when(i < n)
        def _():
            hn = (normed_x()[0] * g1_ref[...]).astype(BF16)
            acc_ref[:QKV_W, :] += _mm_tn(dq_ref[...], hn)
            acc_ref[QKV_W:, :] += _mm_tn(dg_ref[...], hn)

        @pl.when(i == n - 1)
        def _():
            for d in range(N_DEV):
                @pl.when(d % 2 != c)
                def _():
                    send_buf[d // 2] = acc_ref[rows(d), :].astype(BF16)
                    hand_over(d // 2).start()
            for d in range(N_DEV):
                chip = d // 2

                @pl.when(d % 2 == c)
                def _():
                    hand_over(chip).wait_recv()

                    @pl.when(chip == my_chip)
                    def _():
                        own_ref[...] = acc_ref[rows(d), :]
                        sib_ref[...] = land_buf[chip]

                    @pl.when(chip != my_chip)
                    def _():
                        pair_buf[chip] = (acc_ref[rows(d), :] + land_buf[chip].astype(F32)).astype(BF16)
                        to_chip(chip, chip ^ my_chip).start()
            for chip in range(N_CHIPS):
                hand_over(chip).wait_send()

        @pl.when(i >= n)
        def _():
            x_hat, r = normed_x()
            dx_ref[...], dg1 = _grad_x_tile(dq_ref[...], dg_ref[...], x_hat, r, g1_ref[...], w_ref, dh_ref[...])
            dg1_ref[0:1, :] += dg1

        @pl.when(i == n_steps - 1)
        def _():
            for rel in range(1, n_far + 1):
                to_chip(0, rel).wait()

    both = lambda w_: pl.BlockSpec((tm, w_), lambda i: (i % n, 0))
    second = pl.BlockSpec((tm, D_MODEL), lambda i: (jnp.maximum(i - n, 0), 0))
    whole = lambda dtype: jax.ShapeDtypeStruct(shard, dtype)
    sems = lambda k: pltpu.SemaphoreType.DMA((k,))
    res = pl.pallas_call(
        body, name="in_proj_bwd", grid=(n_steps,),
        in_specs=[both(QKV_W), both(GATES_W), both(D_MODEL), second, _full((1, D_MODEL)), _resident((IN_COLS, D_MODEL)),
                  HBM_SPEC],
        out_specs=[second, _full(shard), _full(shard), HBM_SPEC, _full((SMALL_ROWS, D_MODEL)), HBM_SPEC],
        out_shape=[jax.ShapeDtypeStruct((n_cover * tm, D_MODEL), F32), whole(F32), whole(BF16),
                   jax.ShapeDtypeStruct((n_far,) + shard, BF16), jax.ShapeDtypeStruct((SMALL_ROWS, D_MODEL), F32),
                   jax.ShapeDtypeStruct(out_sums.shape, out_sums.dtype)],
        scratch_shapes=[pltpu.VMEM((IN_COLS, D_MODEL), F32), pltpu.VMEM((N_CHIPS,) + shard, BF16),
                        pltpu.VMEM((N_CHIPS,) + shard, BF16), pltpu.VMEM((N_CHIPS,) + shard, BF16),
                        sems(N_CHIPS), sems(N_CHIPS), sems(n_far), sems(n_far), sems(n_far), sems(n_far)],
        compiler_params=_params("arbitrary", barrier_id=7),
    )(dqkv, dgates, x, dh, g1, w_in, out_sums)
    return res[0], (res[1], res[2], res[3]), res[4], res[5]


def _grad_x_rest(dqkv, dgates, x, dh, g1, w_in, tm, head, dg1_rows):
    t = x.shape[0]
    first = head.shape[0] // tm
    n_rest = t // tm - first
    if n_rest == 0:
        return head, dg1_rows
    assert first <= n_rest

    def body(dq_ref, dg_ref, x_ref, dh_ref, g1_ref, w_ref, head_ref, rows_ref, gx_ref, dg1_ref, stage, sems):
        j = pl.program_id(0)

        def tile_out(step, kind):
            row0 = (step + first) * tm if kind == 0 else step * tm
            slot = 2 * kind + step % 2
            return pltpu.make_async_copy(stage.at[slot], gx_ref.at[pl.ds(pl.multiple_of(row0, tm), tm), :], sems.at[slot])

        @pl.when(j == 0)
        def _():
            dg1_ref[...] = rows_ref[...]

        @pl.when(j >= 2)
        def _():
            tile_out(j - 2, 0).wait()

        @pl.when((j >= 2) & (j - 2 < first))
        def _():
            tile_out(j - 2, 1).wait()

        @pl.when(j < first)
        def _():
            stage[2 + j % 2] = head_ref[...]
            tile_out(j, 1).start()

        xv = x_ref[...]
        r = _inv_rms(xv)
        dx, dg1 = _grad_x_tile(dq_ref[...], dg_ref[...], xv * r, r, g1_ref[...], w_ref, dh_ref[...])
        stage[j % 2] = dx
        dg1_ref[0:1, :] += dg1
        tile_out(j, 0).start()

        @pl.when(j == n_rest - 1)
        def _():
            for back in range(min(2, n_rest)):
                tile_out(j - back, 0).wait()

                @pl.when(j - back < first)
                def _():
                    tile_out(j - back, 1).wait()

    tile = lambda w_: pl.BlockSpec((tm, w_), lambda j: (j + first, 0))
    head_tile = pl.BlockSpec((tm, D_MODEL), lambda j: (jnp.minimum(j, first - 1), 0))
    return pl.pallas_call(
        body, name="grad_x_rest", grid=(n_rest,),
        in_specs=[tile(QKV_W), tile(GATES_W), tile(D_MODEL), tile(D_MODEL), _full((1, D_MODEL)),
                  _resident((IN_COLS, D_MODEL)), head_tile, _full((SMALL_ROWS, D_MODEL))],
        out_specs=[HBM_SPEC, _full((SMALL_ROWS, D_MODEL))],
        out_shape=[jax.ShapeDtypeStruct((t, D_MODEL), F32), jax.ShapeDtypeStruct((SMALL_ROWS, D_MODEL), F32)],
        scratch_shapes=[pltpu.VMEM((4, tm, D_MODEL), F32), pltpu.SemaphoreType.DMA((4,))],
        compiler_params=_params("arbitrary"),
    )(dqkv, dgates, x, dh, g1, w_in, head, dg1_rows)


def _all_gather(shards, name):
    n = len(shards)

    def body(*refs):
        _enter_with(_sibling_and_chips(*_mesh_pos()))
        start, finish = _gather_steps(refs[:n], refs[n:2 * n], *refs[2 * n:])
        start()
        finish()

    return pl.pallas_call(
        body, name=name,
        in_specs=[HBM_SPEC] * n, out_specs=[HBM_SPEC] * n,
        out_shape=[jax.ShapeDtypeStruct((N_DEV,) + s.shape, s.dtype) for s in shards],
        scratch_shapes=[pltpu.SemaphoreType.DMA((7 * n,)), pltpu.SemaphoreType.DMA((7 * n,)),
                        pltpu.SemaphoreType.DMA((n,))],
        compiler_params=_params(barrier_id=8),
    )(*shards)


def _adam_math(w, g, m, v):
    m = ADAM_B1 * m + (1.0 - ADAM_B1) * g
    v = ADAM_B2 * v + (1.0 - ADAM_B2) * (g * g)
    m_hat = m / (1.0 - ADAM_B1 ** ADAM_STEP)
    v_hat = v / (1.0 - ADAM_B2 ** ADAM_STEP)
    delta = -ADAM_LR * (m_hat / (jnp.sqrt(v_hat) + ADAM_EPS) + ADAM_WD * w)
    return delta, m, v


def _adamw_reduced(tensors, n_steps):
    n_far = len(CHIP_FLIPS)

    def body(*refs):
        ins, outs = refs[:6 * len(tensors)], refs[6 * len(tensors):]
        for k in range(len(tensors)):
            w_ref, m_ref, v_ref, own_ref, sib_ref, far_ref = ins[6 * k:6 * k + 6]
            g_ref, d_ref, nm_ref, nv_ref = outs[4 * k:4 * k + 4]
            g = own_ref[...] + sib_ref[...].astype(F32)
            for j in range(n_far):
                g = g + far_ref[j].astype(F32)
            g_ref[...] = g
            d_ref[...], nm_ref[...], nv_ref[...] = _adam_math(w_ref[...], g, m_ref[...], v_ref[...])

    in_specs, out_specs, out_shape = [], [], []
    for w, *_ in tensors:
        rows, cols = w.shape
        tile = pl.BlockSpec((rows // n_steps, cols), lambda i: (i, 0))
        in_specs += [tile] * 5 + [pl.BlockSpec((n_far, rows // n_steps, cols), lambda i: (0, i, 0))]
        out_specs += [tile] * 4
        out_shape += [jax.ShapeDtypeStruct((rows, cols), F32)] * 4
    res = pl.pallas_call(
        body, name="adamw_reduced", grid=(n_steps,), in_specs=in_specs, out_specs=out_specs, out_shape=out_shape,
        compiler_params=_params("parallel"),
    )(*[a for tensor in tensors for a in tensor])
    return [res[4 * k:4 * k + 4] for k in range(len(tensors))]


SMALL_PARAMS = ("pre_mix_norm", "post_mix_norm", "pre_mlp_norm", "post_mlp_norm", "attn_group_norm", "conv_group_norm",
                "conv_w", "attn_sinks")


SMALL_WIDTHS = (D_MODEL, CONV_W, ATTN_W, 128, D_MODEL)


def _small_tail(gathered, dev, weights, first_moments, second_moments):
    n = len(SMALL_PARAMS)
    conv_shard = CONV_W // N_DEV

    def body(dev_ref, sums_ref, *refs):
        w_refs, m_refs, v_refs = refs[:n], refs[n:2 * n], refs[2 * n:3 * n]
        loss_ref, outs = refs[3 * n], refs[3 * n + 1:]
        total = sums_ref[0]
        for d in range(1, N_DEV):
            total = total + sums_ref[d]
        starts = [sum(SMALL_WIDTHS[:i]) for i in range(len(SMALL_WIDTHS))]
        mid, conv, gain, sink, inp = (total[:, a:a + w_] for a, w_ in zip(starts, SMALL_WIDTHS))
        loss_ref[...] = (0.5 / D_MODEL) * jnp.sum(mid[ROW_LOSS:ROW_LOSS + 1, :], axis=1, keepdims=True)
        conv_rows = conv[ROW_CW0:ROW_CW0 + 3, :]
        conv_g = jnp.zeros((3, conv_shard), F32)
        for d in range(N_DEV):
            conv_g = conv_g + jnp.where(dev_ref[0] == d, conv_rows[:, conv_shard * d:conv_shard * (d + 1)], 0.0)
        grads = [inp[0:1, :], mid[ROW_G2:ROW_G2 + 1, :], mid[ROW_G3:ROW_G3 + 1, :], mid[ROW_G4:ROW_G4 + 1, :],
                 gain[0:1, :], conv[ROW_GCONV:ROW_GCONV + 1, :], conv_g, sink[0:1, :N_HEADS]]
        for i, g in enumerate(grads):
            parts = [(..., g)] if len(w_refs[i].shape) == 2 else [(r, g[r:r + 1, :]) for r in range(g.shape[0])]
            for at, g_at in parts:
                delta, new_m, new_v = _adam_math(w_refs[i][at], g_at, m_refs[i][at], v_refs[i][at])
                outs[i][at], outs[n + i][at], outs[2 * n + i][at], outs[3 * n + i][at] = g_at, delta, new_m, new_v

    params = list(weights) + list(first_moments) + list(second_moments)
    shapes = [jax.ShapeDtypeStruct(w.shape, F32) for w in weights]
    res = pl.pallas_call(
        body, name="small_tail", grid=(1,),
        in_specs=[pl.BlockSpec(memory_space=pltpu.SMEM), _full(gathered.shape)] + [_full(p.shape) for p in params],
        out_specs=[_full((1, 1))] + [_full(sh.shape) for sh in shapes] * 4,
        out_shape=[jax.ShapeDtypeStruct((1, 1), F32)] + shapes * 4,
    )(dev, gathered, *params)
    return res[0], [res[1 + k * n:1 + (k + 1) * n] for k in range(4)]


TOKEN_TILE = 512
MID_TILE = 256
MID_CHUNK = 1024
MID_CHUNKS = D_FF // MID_CHUNK
ADAM_STEPS = 2


def _local_grads(x, target, g1, w_in_shard, conv_shard, sinks, g_attn, g_conv, g2, g3, g4, shards, order):
    t = x.shape[0]
    tm = min(TOKEN_TILE, t)
    rope = _rope_tables(t)
    qkv, gates, mconv, w_in, conv_w, gathered = _in_proj_fwd(x, g1, w_in_shard, conv_shard, g_conv, rope, tm, shards,
                                                             (False, True, False))
    attn, mattn, (w_out, w_up, w_down) = _attn_fwd(qkv, sinks, g_attn, shards, gathered)
    act, dup, hn2t, dmo, dmix, dh, dmixed, small_mid = _mid(
        mattn, mconv, x, target, g2, g3, g4, w_out.reshape(D_MODEL, D_MODEL),
        w_up, w_down.reshape(D_FF, D_MODEL), min(MID_TILE, t))
    up_own, up_sib, up_sums = _dw_pair_sums((hn2t, dup), order, "up", "dw_up", 2)
    down_own, down_sib, down_sums, up_far = _dw_pair_sums((act, dmo), order, "down", "dw_down", 3, ride=up_sums)
    out_own, out_sib, out_sums = _dw_pair_sums((mattn, mconv, dmix), order, "out", "dw_out", 4)
    dgates, small_conv = _conv_bwd(dmixed, gates, g_conv, conv_w, tm)
    dqkv, dsink, dg_attn, down_far = _attn_bwd(qkv, dmixed, attn, g_attn, sinks, rope, down_sums)
    grad_x_head, dw_in, small_in, out_far = _in_proj_bwd(dqkv, dgates, x, dh, g1, w_in, tm, out_sums)
    grad_x, small_in = _grad_x_rest(dqkv, dgates, x, dh, g1, w_in, tm, grad_x_head, small_in)
    dw_out, dw_up, dw_down = (out_own, out_sib, out_far), (up_own, up_sib, up_far), (down_own, down_sib, down_far)
    return grad_x, dw_in, dw_out, dw_up, dw_down, (small_mid, small_conv, dg_attn, dsink, small_in)


def kernel(x, pre_mix_norm, w_in, conv_w, attn_sinks, attn_group_norm, conv_group_norm, w_out, post_mix_norm, pre_mlp_norm, w_up, w_down, post_mlp_norm, loss_target, m_pre_mix_norm, m_w_in, m_conv_w, m_attn_sinks, m_attn_group_norm, m_conv_group_norm, m_w_out, m_post_mix_norm, m_pre_mlp_norm, m_w_up, m_w_down, m_post_mlp_norm, v_pre_mix_norm, v_w_in, v_conv_w, v_attn_sinks, v_attn_group_norm, v_conv_group_norm, v_w_out, v_post_mix_norm, v_pre_mlp_norm, v_w_up, v_w_down, v_post_mlp_norm):
    xi, yi, ci = _mesh_pos()
    chip = 2 * xi + yi
    dev = 2 * chip + ci

    order = _block_order(dev)

    shards = [w_out[0].astype(BF16), w_up[0].astype(BF16), w_down[0].astype(BF16)]

    turned = lambda a: jnp.swapaxes(a, 1, 2)
    grad_x, dw_in, dw_out, dw_up, dw_down, smalls = _local_grads(
        x[0], loss_target[0], pre_mix_norm, turned(w_in)[0].astype(BF16), conv_w[0], attn_sinks, attn_group_norm, conv_group_norm,
        post_mix_norm, pre_mlp_norm, post_mlp_norm, shards, order)

    blocks = {"w_in": (turned(w_in), turned(m_w_in), turned(v_w_in), dw_in), "w_out": (w_out, m_w_out, v_w_out, dw_out),
              "w_up": (w_up, m_w_up, v_w_up, dw_up), "w_down": (w_down, m_w_down, v_w_down, dw_down)}
    big = {}
    for names in (("w_in", "w_out", "w_down"), ("w_up",)):
        stepped = _adamw_reduced([(w[0], m[0], v[0], *dw) for w, m, v, dw in (blocks[nm] for nm in names)], ADAM_STEPS)
        for nm, res in zip(names, stepped):
            big[nm] = [a[None] for a in res]
    big["w_in"] = [turned(a) for a in big["w_in"]]

    flat = lambda a: a.reshape(-1, a.shape[-1]) if a.ndim < 3 else a.reshape(a.shape[1], 1, a.shape[2])
    loss, small = _small_tail(
        _all_gather([jnp.concatenate(smalls, axis=1)], "gather_small")[0], dev.reshape(1).astype(jnp.int32),
        [flat(a) for a in (pre_mix_norm, post_mix_norm, pre_mlp_norm, post_mlp_norm, attn_group_norm, conv_group_norm,
                           conv_w, attn_sinks)],
        [flat(a) for a in (m_pre_mix_norm, m_post_mix_norm, m_pre_mlp_norm, m_post_mlp_norm, m_attn_group_norm,
                           m_conv_group_norm, m_conv_w, m_attn_sinks)],
        [flat(a) for a in (v_pre_mix_norm, v_post_mix_norm, v_pre_mlp_norm, v_post_mlp_norm, v_attn_group_norm,
                           v_conv_group_norm, v_conv_w, v_attn_sinks)])

    order = ("pre_mix_norm", "w_in", "conv_w", "attn_sinks", "attn_group_norm", "conv_group_norm", "w_out",
             "post_mix_norm", "pre_mlp_norm", "w_up", "w_down", "post_mlp_norm")
    shape_of = {"conv_w": conv_w.shape}
    outs = []
    for k in range(4):
        by_name = dict(zip(SMALL_PARAMS, small[k]))
        outs += [big[nm][k] if nm in big else by_name[nm].reshape(shape_of.get(nm, by_name[nm].shape)) for nm in order]
    loss = loss.reshape(())
    return (loss, grad_x[None], *outs)
```

```python
import jax
import jax.numpy as jnp
import numpy as np
from jax import lax
from jax.experimental import pallas as pl
from jax.experimental.pallas import tpu as pltpu

F32 = jnp.float32
BF16 = jnp.bfloat16

D_MODEL = 1024
HEAD_DIM = 64
ATTN_W = 512
CONV_W = 512
N_HEADS = 8
N_KV = 2
GROUP = 4
KV_W = 128
QKV_W = ATTN_W + 2 * KV_W
GATES_W = 3 * CONV_W
IN_COLS = QKV_W + GATES_W
D_FF = 4096
FF_CHUNK = 512
BLOCK = 128
ROT_HALF = 8
ROPE_THETA = 500000.0
NORM_EPS = 1e-6
NEG_INF = -1e30
ATTN_SCALE = 0.125
N_DEV = 8
N_CHIPS = 4
IN_SHARD = IN_COLS // N_DEV

ADAM_LR = 0.001
ADAM_B1 = 0.9
ADAM_B2 = 0.999
ADAM_EPS = 1e-08
ADAM_WD = 0.01
ADAM_STEP = 10

V7X_VMEM_BYTES = 64 * 1024 * 1024
VMEM_LIMIT = V7X_VMEM_BYTES - 2 * 1024 * 1024

MESH = pl.DeviceIdType.MESH
HBM_SPEC = pl.BlockSpec(memory_space=pltpu.HBM)


def _params(*sem, barrier_id=None):
    return pltpu.CompilerParams(dimension_semantics=sem or None, vmem_limit_bytes=VMEM_LIMIT, collective_id=barrier_id)


def _mm(a, b):
    return jnp.dot(a, b, preferred_element_type=F32)


def _mm_nt(a, b):
    return lax.dot_general(a, b, (((1,), (1,)), ((), ())), preferred_element_type=F32)


def _mm_tn(a, b):
    return lax.dot_general(a, b, (((0,), (0,)), ((), ())), preferred_element_type=F32)


def _inv_rms(x):
    return lax.rsqrt(jnp.mean(x * x, axis=-1, keepdims=True) + NORM_EPS)


def _rms_bwd(xhat, r, gain, dy):
    gy = dy * gain
    return r * (gy - xhat * jnp.mean(gy * xhat, axis=-1, keepdims=True)), dy * xhat


def _colsum(a):
    return jnp.sum(a, axis=0, keepdims=True)


def _full(shape):
    zeros = (0,) * len(shape)
    return pl.BlockSpec(shape, lambda *_: zeros)


def _resident(shape):
    zeros = (0,) * len(shape)
    return pl.BlockSpec(shape, lambda *_: zeros, pipeline_mode=pl.Buffered(1))


def _rope_tables(t):
    pos = np.arange(t, dtype=np.float32)
    inv_freq = (ROPE_THETA ** (-np.arange(0, 2 * ROT_HALF, 2, dtype=np.float64) / (2 * ROT_HALF))).astype(np.float32)
    ang = (pos[:, None] * inv_freq[None, :]).astype(np.float64)
    cos, sin = np.cos(ang).astype(np.float32), np.sin(ang).astype(np.float32)
    zeros8 = np.zeros((t, ROT_HALF), np.float32)
    rest = np.zeros((t, HEAD_DIM - 2 * ROT_HALF), np.float32)
    c_head = np.concatenate([cos, cos, rest + 1.0], axis=1)
    s1_head = np.concatenate([zeros8, sin, rest], axis=1)
    s2_head = np.concatenate([-sin, zeros8, rest], axis=1)
    two = lambda a: jnp.asarray(np.concatenate([a, a], axis=1))
    return two(c_head), two(s1_head), two(s2_head)


def _rope(v, c, s1, s2):
    return v * c + pltpu.roll(v, ROT_HALF, 1) * s1 + pltpu.roll(v, 128 - ROT_HALF, 1) * s2


def _rope_transpose(dv, c, s1, s2):
    return dv * c + pltpu.roll(dv * s1, 128 - ROT_HALF, 1) + pltpu.roll(dv * s2, ROT_HALF, 1)


def _shift_rows_down(u, prev, k):
    row = lax.broadcasted_iota(jnp.int32, u.shape, 0)
    out = pltpu.roll(u, k, 0)
    for r in range(k):
        out = jnp.where(row == r, prev[8 - k + r:8 - k + r + 1, :], out)
    return out


def _shift_rows_up(u, nxt, k):
    n = u.shape[0]
    row = lax.broadcasted_iota(jnp.int32, u.shape, 0)
    out = pltpu.roll(u, n - k, 0)
    for r in range(k):
        out = jnp.where(row == n - k + r, nxt[r:r + 1, :], out)
    return out


def _conv3(u, u1, u2, w):
    return (w[0:1, :] * u2 + w[1:2, :] * u1) + w[2:3, :] * u


def _mesh_pos():
    return lax.axis_index("x"), lax.axis_index("y"), lax.axis_index("c")


def _slot(ref, pos):
    dev = 4 * pos[0] + 2 * pos[1] + pos[2]
    if len(ref.shape) == 2:
        width = ref.shape[1] // N_DEV
        return ref.at[:, pl.ds(pl.multiple_of(dev * width, width), width)]
    return ref.at[dev]


def _gathered_shape(shard, by_cols):
    if by_cols:
        return jax.ShapeDtypeStruct((shard.shape[0], N_DEV * shard.shape[1]), shard.dtype)
    return jax.ShapeDtypeStruct((N_DEV,) + shard.shape, shard.dtype)


def _enter_with(peers):
    barrier = pltpu.get_barrier_semaphore()
    for peer in peers:
        pl.semaphore_signal(barrier, inc=1, device_id=peer, device_id_type=MESH)
    pl.semaphore_wait(barrier, len(peers))


def _sibling_and_chips(x, y, c):
    return [(x, y, 1 - c), (1 - x, y, c), (x, 1 - y, c), (1 - x, 1 - y, c)]


def _push(src, dst, sems, k, to):
    send_sems, recv_sems = sems
    return pltpu.make_async_remote_copy(src_ref=src, dst_ref=dst, send_sem=send_sems.at[k], recv_sem=recv_sems.at[k],
                                        device_id=to, device_id_type=MESH)


def _gather_steps(shards, outs, send_sems, recv_sems, local_sems):
    n = len(shards)
    x, y, c = _mesh_pos()
    me, sibling = (x, y, c), (x, y, 1 - c)
    chips = [(1 - x, y), (x, 1 - y), (1 - x, 1 - y)]

    def copy(i, k, block, to, src=None):
        dst = _slot(outs[i], block)
        return _push(dst if src is None else src, dst, (send_sems, recv_sems), 7 * i + k, to)

    mine = [pltpu.make_async_copy(shards[i], _slot(outs[i], me), local_sems.at[i]) for i in range(n)]
    first = []
    for i in range(n):
        first.append(copy(i, 0, me, sibling, src=shards[i]))
        first += [copy(i, 1 + j, me, (*chip, c), src=shards[i]) for j, chip in enumerate(chips)]

    def start():
        for cp in mine + first:
            cp.start()

    def finish():
        passed = []
        for j, chip in enumerate(chips):
            for i in range(n):
                copy(i, 1 + j, (*chip, c), me).wait_recv()
                cp = copy(i, 4 + j, (*chip, c), sibling)
                cp.start()
                passed.append(cp)
        for i in range(n):
            copy(i, 0, sibling, me).wait_recv()
            for j, chip in enumerate(chips):
                copy(i, 4 + j, (*chip, 1 - c), me).wait_recv()
        for cp in first + passed:
            cp.wait_send()
        for cp in mine:
            cp.wait()

    return start, finish


def _gather_near(first, last, shards, outs, sems, local_sems):
    x, y, c = _mesh_pos()
    me, peers = (x, y, c), [(x, y, 1 - c), (1 - x, y, c), (x, 1 - y, c)]
    n = len(shards)
    local = [pltpu.make_async_copy(shards[i], _slot(outs[i], me), local_sems.at[i]) for i in range(n)]
    sends = [_push(shards[i], _slot(outs[i], me), sems, 3 * i + k, peers[k]) for i in range(n) for k in range(3)]
    arrivals = [_push(shards[i], _slot(outs[i], peers[k]), sems, 3 * i + k, peers[k]) for i in range(n) for k in range(3)]

    def start():
        for cp in local + sends:
            cp.start()

    if first is not None:
        pl.when(first)(start)

    @pl.when(last)
    def _():
        for cp in sends:
            cp.wait_send()
        for cp in arrivals:
            cp.wait_recv()
        for cp in local:
            cp.wait()

    return start


def _relay_route(x, y, c):
    south = c == 0
    via = (jnp.where(south, 1 - x, x), jnp.where(south, y, 1 - y))
    to = (jnp.where(south, x, 1 - x), jnp.where(south, 1 - y, y))
    return via, to


def _gather_far(first, middle, last, shards, ins, outs, sems):
    x, y, c = _mesh_pos()
    sibling = (x, y, 1 - c)
    chips = [(1 - x, y), (x, 1 - y), (1 - x, 1 - y)]
    via, to = _relay_route(x, y, c)
    n = len(shards)
    diag_send = [_push(_slot(ins[i], (*via, c)), _slot(outs[i], (*via, c)), sems, 4 * i, (*to, c)) for i in range(n)]
    diag_arrival = [_push(shards[i], _slot(outs[i], (*chips[2], c)), sems, 4 * i, (*to, c)) for i in range(n)]
    passed = [[_push(_slot(ins[i], (*chips[j], c)), _slot(outs[i], (*chips[j], c)), sems, 4 * i + 1 + j, sibling)
               for i in range(n)] for j in range(3)]
    from_sibling = [_push(shards[i], _slot(outs[i], (*chips[j], 1 - c)), sems, 4 * i + 1 + j, sibling)
                    for i in range(n) for j in range(3)]

    @pl.when(first)
    def _():
        for cp in diag_send + passed[0] + passed[1]:
            cp.start()

    @pl.when(middle)
    def _():
        for cp in diag_arrival:
            cp.wait_recv()
        for cp in passed[2]:
            cp.start()

    @pl.when(last)
    def _():
        for cp in from_sibling:
            cp.wait_recv()
        for cp in diag_send + passed[0] + passed[1] + passed[2]:
            cp.wait_send()


def _in_proj_fwd(x, g1, w_in, conv_w, g_conv, rope, tm, shards, by_cols):
    t = x.shape[0]
    rc, rs1, rs2 = rope
    n = len(shards)
    n_tiles = t // tm

    def body(*refs):
        x_ref, g1_ref, w_ref, cw_ref, gc_ref, c_ref, s1_ref, s2_ref = refs[:8]
        shard_refs = refs[8:8 + n]
        qkv_ref, gates_ref, mconv_ref, w_full_ref, cw_full_ref = refs[8 + n:13 + n]
        gathered = refs[13 + n:13 + 2 * n]
        carry_ref, w_land, cw_land, hn_ref = refs[13 + 2 * n:17 + 2 * n]
        now_sems = refs[17 + 2 * n:20 + 2 * n]
        step = pl.program_id(0)
        start_later_weights = _gather_near(None, step == 2 * n_tiles - 1, shard_refs, gathered,
                                           refs[20 + 2 * n:22 + 2 * n], refs[22 + 2 * n]) if n else None
        start_w_in, finish_w_in = _gather_steps([w_ref, cw_ref], [w_land, cw_land], *now_sems)

        @pl.when(step == 0)
        def _():
            carry_ref[...] = jnp.zeros_like(carry_ref)
            _enter_with(_sibling_and_chips(*_mesh_pos()))
            start_w_in()
            if start_later_weights is not None:
                start_later_weights()

        @pl.when(step < n_tiles)
        def _():
            xv = x_ref[...]
            hn_ref[step] = ((xv * _inv_rms(xv)) * g1_ref[...]).astype(BF16)

        @pl.when(step == n_tiles)
        def _():
            finish_w_in()
            conv_shard = CONV_W // N_DEV
            for d in range(N_DEV):
                w_full_ref[IN_SHARD * d:IN_SHARD * (d + 1), :] = w_land[d]
                cw_full_ref[:, conv_shard * d:conv_shard * (d + 1)] = cw_land[d]

        @pl.when(step >= n_tiles)
        def _():
            proj = _mm_nt(hn_ref[step - n_tiles], w_full_ref[...])
            c, s1, s2 = c_ref[...], s1_ref[...], s2_ref[...]
            for ci in range((ATTN_W + KV_W) // 128):
                sl = slice(128 * ci, 128 * (ci + 1))
                qkv_ref[:, sl] = _rope(proj[:, sl], c, s1, s2).astype(BF16)
            qkv_ref[:, ATTN_W + KV_W:QKV_W] = proj[:, ATTN_W + KV_W:QKV_W].astype(BF16)
            gates = proj[:, QKV_W:]
            gates_ref[...] = gates
            gb, gcc, xin = gates[:, :CONV_W], gates[:, CONV_W:2 * CONV_W], gates[:, 2 * CONV_W:]
            u = gcc * xin
            prev = carry_ref[...]
            conv = gb * _conv3(u, _shift_rows_down(u, prev, 1), _shift_rows_down(u, prev, 2), cw_full_ref[...])
            carry_ref[...] = u[tm - 8:tm, :]
            mconv_ref[...] = ((conv * _inv_rms(conv)) * gc_ref[...]).astype(BF16)

    first_pass = pl.BlockSpec((tm, D_MODEL), lambda i: (jnp.minimum(i, n_tiles - 1), 0))
    tile = lambda w_: pl.BlockSpec((tm, w_), lambda i: (jnp.maximum(i - n_tiles, 0), 0))
    sems = lambda k: pltpu.SemaphoreType.DMA((k,))
    res = pl.pallas_call(
        body, name="in_proj_fwd", grid=(2 * n_tiles,),
        in_specs=[first_pass, _full((1, D_MODEL)), HBM_SPEC, HBM_SPEC, _full((1, CONV_W)), tile(128), tile(128),
                  tile(128)] + [HBM_SPEC] * n,
        out_specs=[tile(QKV_W), tile(GATES_W), tile(CONV_W), _full((IN_COLS, D_MODEL)), _full((3, CONV_W))]
        + [HBM_SPEC] * n,
        out_shape=[jax.ShapeDtypeStruct((t, QKV_W), BF16), jax.ShapeDtypeStruct((t, GATES_W), F32),
                   jax.ShapeDtypeStruct((t, CONV_W), BF16), jax.ShapeDtypeStruct((IN_COLS, D_MODEL), BF16),
                   jax.ShapeDtypeStruct((3, CONV_W), F32)]
        + [_gathered_shape(s, cols) for s, cols in zip(shards, by_cols)],
        scratch_shapes=[pltpu.VMEM((8, CONV_W), F32), pltpu.VMEM((N_DEV,) + w_in.shape, BF16),
                        pltpu.VMEM((N_DEV,) + conv_w.shape, F32), pltpu.VMEM((n_tiles, tm, D_MODEL), BF16),
                        sems(14), sems(14), sems(2)]
        + ([sems(3 * n), sems(3 * n), sems(n)] if n else []),
        compiler_params=_params("arbitrary", barrier_id=0),
    )(x, g1, w_in, conv_w, g_conv, rc, rs1, rs2, *shards)
    return res[0], res[1], res[2], res[3], res[4], list(res[5:])


GROUP_COLS = GROUP * BLOCK
ATTN_STEP_BLOCKS = 16


def _attn_masks(has_prev):
    key = lax.broadcasted_iota(jnp.int32, (2 * BLOCK, GROUP_COLS), 0)
    query = lax.broadcasted_iota(jnp.int32, (2 * BLOCK, GROUP_COLS), 1) & (BLOCK - 1)
    band = (key > query) & (key <= query + BLOCK)
    return [band & ((key >= BLOCK) | has_prev)] + [band] * (ATTN_STEP_BLOCKS - 1)


def _heads_side_by_side(at, g, b):
    heads = [at[HEAD_DIM * (GROUP * g + hh):HEAD_DIM * (GROUP * g + hh + 1), BLOCK * b:BLOCK * (b + 1)] for hh in range(GROUP)]
    return jnp.concatenate(heads, axis=1)


def _to_token_rows(parts):
    rows = [jnp.concatenate([parts[b][g][:, BLOCK * hh:BLOCK * (hh + 1)] for b in range(ATTN_STEP_BLOCKS)], axis=1)
            for g in range(N_KV) for hh in range(GROUP)]
    return jnp.concatenate(rows, axis=0).T


def _group_sinks(sink_ref, g):
    head = lax.broadcasted_iota(jnp.int32, (1, GROUP_COLS), 1) // BLOCK
    out = jnp.full((1, GROUP_COLS), sink_ref[0, GROUP * g], F32)
    for hh in range(1, GROUP):
        out = jnp.where(head == hh, sink_ref[0, GROUP * g + hh], out)
    return out


def _attn_probs(qt, kk, sink, valid):
    s = jnp.where(valid, _mm(kk, qt), NEG_INF)
    m = jnp.maximum(jnp.max(s, axis=0, keepdims=True), sink)
    p = jnp.exp(s - m)
    psink = jnp.exp(sink - m)
    inv_l = 1.0 / (jnp.sum(p, axis=0, keepdims=True) + psink)
    return p * inv_l, psink * inv_l


ATTN_STEP = ATTN_STEP_BLOCKS * BLOCK
ATTN_KEYS = ATTN_STEP + BLOCK


def _qkv_specs(order):
    prev = lambda i: jnp.maximum(ATTN_STEP_BLOCKS * order(i) - 1, 0)
    kcol, vcol = ATTN_W // KV_W, ATTN_W // KV_W + 1
    return [pl.BlockSpec((ATTN_STEP, ATTN_W), lambda i: (order(i), 0)),
            pl.BlockSpec((BLOCK, KV_W), lambda i: (prev(i), kcol)), pl.BlockSpec((ATTN_STEP, KV_W), lambda i: (order(i), kcol)),
            pl.BlockSpec((BLOCK, KV_W), lambda i: (prev(i), vcol)), pl.BlockSpec((ATTN_STEP, KV_W), lambda i: (order(i), vcol))]


def _attn_fwd(qkv, sinks, g_attn, shards, gathered):
    t = qkv.shape[0]
    n = len(shards)

    def body(*refs):
        sink_ref, q_ref, kp_ref, kc_ref, vp_ref, vc_ref, ga_ref = refs[:7]
        attn_ref, mattn_ref = refs[7 + 2 * n:9 + 2 * n]
        step = pl.program_id(0)
        if n:
            @pl.when(step == 0)
            def _():
                x, y, c = _mesh_pos()
                _enter_with([(x, y, 1 - c), (*_relay_route(x, y, c)[1], c)])

            n_steps = t // ATTN_STEP
            _gather_far(step == 0, step == n_steps // 2, step == n_steps - 1, refs[7:7 + n], refs[7 + n:7 + 2 * n],
                        refs[9 + 2 * n:9 + 3 * n], refs[9 + 3 * n:11 + 3 * n])
        qt = (q_ref[...] * ATTN_SCALE).T
        keys = jnp.concatenate([kp_ref[...], kc_ref[...]], axis=0)
        vals = jnp.concatenate([vp_ref[...], vc_ref[...]], axis=0)
        sink = [_group_sinks(sink_ref, g) for g in range(N_KV)]
        masks = _attn_masks(step > 0)
        parts = []
        for b in range(ATTN_STEP_BLOCKS):
            window = slice(BLOCK * b, BLOCK * (b + 2))
            valid = masks[b]
            parts.append([])
            for g in range(N_KV):
                gs = slice(HEAD_DIM * g, HEAD_DIM * (g + 1))
                probs, _ = _attn_probs(_heads_side_by_side(qt, g, b), keys[window, gs], sink[g], valid)
                parts[b].append(_mm_tn(vals[window, gs], probs.astype(BF16)))
        attn = _to_token_rows(parts)
        attn_ref[...] = attn
        mattn_ref[...] = ((attn * _inv_rms(attn)) * ga_ref[...]).astype(BF16)

    blk = pl.BlockSpec((ATTN_STEP, ATTN_W), lambda j: (j, 0))
    res = pl.pallas_call(
        body, name="attn_fwd", grid=(t // ATTN_STEP,),
        in_specs=[pl.BlockSpec(memory_space=pltpu.SMEM)] + _qkv_specs(lambda j: j) + [_full((1, ATTN_W))]
        + [HBM_SPEC] * (2 * n),
        out_specs=[blk, blk] + [HBM_SPEC] * n,
        out_shape=[jax.ShapeDtypeStruct((t, ATTN_W), F32), jax.ShapeDtypeStruct((t, ATTN_W), BF16)]
        + [jax.ShapeDtypeStruct(g.shape, g.dtype) for g in gathered],
        input_output_aliases={7 + n + i: 2 + i for i in range(n)},
        scratch_shapes=[pltpu.SemaphoreType.DMA((4 * n,)), pltpu.SemaphoreType.DMA((4 * n,))] if n else [],
        compiler_params=_params("arbitrary", barrier_id=1 if n else None),
    )(sinks, qkv, qkv, qkv, qkv, qkv, g_attn, *shards, *gathered)
    return res[0], res[1], list(res[2:])


SMALL_ROWS = 8
ROW_LOSS, ROW_G2, ROW_G3, ROW_G4 = 0, 1, 2, 3


def _mid(mattn, mconv, x, target, g2, g3, g4, w_out, w_up, w_down, tm):
    t = x.shape[0]

    def body(ma_ref, mc_ref, x_ref, t_ref, g2_ref, g3_ref, g4_ref, wo_ref, wu_ref, wd_ref,
             act_ref, dup_ref, hn2t_ref, dmo_ref, dmix_ref, dh_ref, dmixed_ref, small_ref, up_ref):
        @pl.when(pl.program_id(0) == 0)
        def _():
            small_ref[...] = jnp.zeros_like(small_ref)

        g2, g3, g4 = g2_ref[...], g3_ref[...], g4_ref[...]
        mix_out = _mm(ma_ref[...], wo_ref[0:ATTN_W, :]) + _mm(mc_ref[...], wo_ref[ATTN_W:, :])
        r2 = _inv_rms(mix_out)
        mo_hat = mix_out * r2
        h = x_ref[...] + mo_hat * g2
        r3 = _inv_rms(h)
        h_hat = h * r3
        hn2 = (h_hat * g3).astype(BF16)
        hn2t_ref[...] = hn2.T
        for j in range(MID_CHUNKS):
            cols_j = slice(MID_CHUNK * j, MID_CHUNK * (j + 1))
            up = jnp.maximum(_mm(hn2, wu_ref[:, cols_j]), 0.0)
            up_ref[:, cols_j] = up.astype(BF16)
            act_ref[:, cols_j] = (up * up).astype(BF16)
        mlp = _mm(act_ref[...], wd_ref[...])
        r4 = _inv_rms(mlp)
        ml_hat = mlp * r4
        err = (h + ml_hat * g4) - t_ref[...]
        d_out = err * (1.0 / D_MODEL)
        d_mlp, dg4 = _rms_bwd(ml_hat, r4, g4, d_out)
        dmo = d_mlp.astype(BF16)
        dmo_ref[...] = dmo
        for j in range(MID_CHUNKS):
            cols_j = slice(MID_CHUNK * j, MID_CHUNK * (j + 1))
            dact = _mm_nt(dmo, wd_ref[cols_j, :])
            dup_ref[:, cols_j] = (dact * (2.0 * up_ref[:, cols_j].astype(F32))).astype(BF16)
        dhn2 = _mm_nt(dup_ref[...], wu_ref[...])
        dh_norm, dg3 = _rms_bwd(h_hat, r3, g3, dhn2)
        dh = d_out + dh_norm
        dh_ref[...] = dh
        d_mix, dg2 = _rms_bwd(mo_hat, r2, g2, dh)
        dmix = d_mix.astype(BF16)
        dmix_ref[...] = dmix
        dmixed_ref[...] = _mm_nt(dmix, wo_ref[...])
        small_ref[ROW_LOSS:ROW_LOSS + 1, :] += _colsum(err * err)
        small_ref[ROW_G2:ROW_G2 + 1, :] += _colsum(dg2)
        small_ref[ROW_G3:ROW_G3 + 1, :] += _colsum(dg3)
        small_ref[ROW_G4:ROW_G4 + 1, :] += _colsum(dg4)

    tile = lambda n: pl.BlockSpec((tm, n), lambda i: (i, 0))
    cols = lambda n: pl.BlockSpec((n, tm), lambda i: (0, i))
    gain = _full((1, D_MODEL))
    return pl.pallas_call(
        body, name="mid_fwd_bwd", grid=(t // tm,),
        in_specs=[tile(ATTN_W), tile(CONV_W), tile(D_MODEL), tile(D_MODEL), gain, gain, gain,
                  _resident((D_MODEL, D_MODEL)), _resident((D_MODEL, D_FF)), _resident((D_FF, D_MODEL))],
        out_specs=[tile(D_FF), tile(D_FF), cols(D_MODEL), tile(D_MODEL), tile(D_MODEL), tile(D_MODEL), tile(D_MODEL),
                   _full((SMALL_ROWS, D_MODEL))],
        out_shape=[jax.ShapeDtypeStruct((t, D_FF), BF16), jax.ShapeDtypeStruct((t, D_FF), BF16),
                   jax.ShapeDtypeStruct((D_MODEL, t), BF16), jax.ShapeDtypeStruct((t, D_MODEL), BF16),
                   jax.ShapeDtypeStruct((t, D_MODEL), BF16), jax.ShapeDtypeStruct((t, D_MODEL), F32),
                   jax.ShapeDtypeStruct((t, D_MODEL), F32), jax.ShapeDtypeStruct((SMALL_ROWS, D_MODEL), F32)],
        scratch_shapes=[pltpu.VMEM((tm, D_FF), BF16)],
        compiler_params=_params("arbitrary"),
    )(mattn, mconv, x, target, g2, g3, g4, w_out, w_up, w_down)


CHIP_FLIPS = ((1, 1), (1, 0), (0, 1))


def _block_order(dev):
    chip_masks = [4 * fx + 2 * fy for fx, fy in CHIP_FLIPS]
    masks = [m + 1 for m in chip_masks] + [1] + chip_masks + [0]
    return jnp.bitwise_xor(dev, jnp.asarray(masks, jnp.int32)).astype(jnp.int32)


def _other_chips(x, y, c):
    return [(1 - x if fx else x, 1 - y if fy else y, c) for fx, fy in CHIP_FLIPS]


def _dw_pair_sums(operands, order, which, name, barrier_id, ride=None):
    t = operands[-1].shape[0]
    n_far = len(CHIP_FLIPS)
    n_in = len(operands)
    n_ride = 0 if ride is None else 1
    out_chunk = D_MODEL // N_DEV
    if which == "up":
        rows, cols = D_MODEL, FF_CHUNK
        in_specs = [_resident((D_MODEL, t)), pl.BlockSpec((t, FF_CHUNK), lambda s, order_ref: (0, order_ref[s]))]
    elif which == "down":
        rows, cols = FF_CHUNK, D_MODEL
        in_specs = [pl.BlockSpec((t, FF_CHUNK), lambda s, order_ref: (0, order_ref[s])), _resident((t, D_MODEL))]
    else:
        rows, cols = out_chunk, D_MODEL
        half = pl.BlockSpec((t, out_chunk), lambda s, order_ref: (0, order_ref[s] % (N_DEV // 2)))
        in_specs = [half, half, _resident((t, D_MODEL))]

    def body(order_ref, *refs):
        own_ref, from_sib_ref, pair_ref = refs[n_in + n_ride:n_in + n_ride + 3]
        send_buf, land_buf, send_sems, recv_sems = refs[n_in + 2 * n_ride + 3:n_in + 2 * n_ride + 7]
        s_now = pl.program_id(0)
        x, y, c = _mesh_pos()
        sibling = (x, y, 1 - c)
        sems = (send_sems, recv_sems)

        @pl.when(s_now == 0)
        def _():
            _enter_with([sibling] + (_other_chips(x, y, c) if n_ride else []))

        if n_ride:
            _chip_exchange_beside(s_now == 0, s_now == N_DEV - 1, [refs[n_in]], [refs[n_in + 3 + n_ride]],
                                  refs[n_in + 2 * n_ride + 7:], enter=False)

        def hand_over(k):
            dst = land_buf.at[k] if k < n_far else from_sib_ref
            return _push(send_buf.at[k], dst, sems, k, sibling)

        if which == "out":
            ma_ref, mc_ref, b_ref = refs[:n_in]
            block = lax.cond(order_ref[s_now] < N_DEV // 2, lambda: _mm_tn(ma_ref[...], b_ref[...]),
                             lambda: _mm_tn(mc_ref[...], b_ref[...]))
        elif which == "down":
            block = _mm_tn(refs[0][...], refs[1][...])
        else:
            block = _mm(refs[0][...], refs[1][...])
        for k in range(n_far + 1):
            @pl.when(s_now == k)
            def _():
                send_buf[k] = block.astype(BF16)
                hand_over(k).start()

        for k in range(n_far):
            @pl.when(s_now == n_far + 1 + k)
            def _():
                hand_over(k).wait_recv()
                pair_ref[...] = (block + land_buf[k].astype(F32)).astype(BF16)

        @pl.when(s_now == N_DEV - 1)
        def _():
            own_ref[...] = block
            for k in range(n_far + 1):
                hand_over(k).wait_send()
            hand_over(n_far).wait_recv()

    rides = [] if ride is None else [ride]
    sems = lambda k: pltpu.SemaphoreType.DMA((k,))
    return pl.pallas_call(
        body, name=name,
        grid_spec=pltpu.PrefetchScalarGridSpec(
            num_scalar_prefetch=1, grid=(N_DEV,), in_specs=in_specs + [HBM_SPEC] * n_ride,
            out_specs=[pl.BlockSpec((rows, cols), lambda s, order_ref: (0, 0)), HBM_SPEC,
                       pl.BlockSpec((None, rows, cols), lambda s, order_ref: (jnp.clip(s - n_far - 1, 0, n_far - 1), 0, 0))]
            + [HBM_SPEC] * n_ride,
            scratch_shapes=[pltpu.VMEM((n_far + 1, rows, cols), BF16), pltpu.VMEM((n_far, rows, cols), BF16),
                            sems(n_far + 1), sems(n_far + 1)] + [sems(n_far), sems(n_far)] * n_ride),
        out_shape=[jax.ShapeDtypeStruct((rows, cols), F32), jax.ShapeDtypeStruct((rows, cols), BF16),
                   jax.ShapeDtypeStruct((n_far, rows, cols), BF16)]
        + [jax.ShapeDtypeStruct(r.shape, r.dtype) for r in rides],
        compiler_params=_params("arbitrary", barrier_id=barrier_id),
    )(order, *operands, *rides)


def _chip_exchange_beside(first, last, sums, outs, sems, enter=True):
    chips = _other_chips(*_mesh_pos())
    copies = [_push(sums[i].at[k], outs[i].at[k], sems, len(chips) * i + k, chip)
              for i in range(len(sums)) for k, chip in enumerate(chips)]

    @pl.when(first)
    def _():
        if enter:
            _enter_with(chips)
        for cp in copies:
            cp.start()

    @pl.when(last)
    def _():
        for cp in copies:
            cp.wait()


ROW_GCONV, ROW_CW0 = 1, 2


def _conv_bwd(dmixed, gates, g_conv, conv_w, tm):
    t = gates.shape[0]
    n = t // tm
    rev = lambda i: n - 1 - i

    def body(dm_ref, gates_ref, gprev_ref, gc_ref, cw_ref, dgates_ref, small_ref, carry_ref):
        i = pl.program_id(0)

        @pl.when(i == 0)
        def _():
            small_ref[...] = jnp.zeros_like(small_ref)
            carry_ref[...] = jnp.zeros_like(carry_ref)

        gates = gates_ref[...]
        gb, gcc, xin = gates[:, :CONV_W], gates[:, CONV_W:2 * CONV_W], gates[:, 2 * CONV_W:]
        u = gcc * xin
        gp = gprev_ref[...]
        uprev = jnp.where(rev(i) == 0, 0.0, gp[:, CONV_W:2 * CONV_W] * gp[:, 2 * CONV_W:])
        u1, u2 = _shift_rows_down(u, uprev, 1), _shift_rows_down(u, uprev, 2)
        w = cw_ref[...]
        c = _conv3(u, u1, u2, w)
        conv = gb * c
        rcv = _inv_rms(conv)
        c_hat = conv * rcv
        dconv, dgc = _rms_bwd(c_hat, rcv, gc_ref[...], dm_ref[...])
        dc = dconv * gb
        nxt = carry_ref[...]
        du = (w[2:3, :] * dc + w[1:2, :] * _shift_rows_up(dc, nxt, 1)) + w[0:1, :] * _shift_rows_up(dc, nxt, 2)
        carry_ref[...] = dc[0:8, :]
        dgates_ref[:, :CONV_W] = (dconv * c).astype(BF16)
        dgates_ref[:, CONV_W:2 * CONV_W] = (du * xin).astype(BF16)
        dgates_ref[:, 2 * CONV_W:] = (du * gcc).astype(BF16)
        small_ref[ROW_GCONV:ROW_GCONV + 1, :] += _colsum(dgc)
        small_ref[ROW_CW0:ROW_CW0 + 1, :] += _colsum(dc * u2)
        small_ref[ROW_CW0 + 1:ROW_CW0 + 2, :] += _colsum(dc * u1)
        small_ref[ROW_CW0 + 2:ROW_CW0 + 3, :] += _colsum(dc * u)

    tile = lambda w_: pl.BlockSpec((tm, w_), lambda i: (rev(i), 0))
    prev8 = pl.BlockSpec((8, GATES_W), lambda i: (jnp.maximum(rev(i) * (tm // 8) - 1, 0), 0))
    conv_half = pl.BlockSpec((tm, CONV_W), lambda i: (rev(i), ATTN_W // CONV_W))
    return pl.pallas_call(
        body, name="conv_bwd", grid=(n,),
        in_specs=[conv_half, tile(GATES_W), prev8, _full((1, CONV_W)), _full((3, CONV_W))],
        out_specs=[tile(GATES_W), _full((SMALL_ROWS, CONV_W))],
        out_shape=[jax.ShapeDtypeStruct((t, GATES_W), BF16), jax.ShapeDtypeStruct((SMALL_ROWS, CONV_W), F32)],
        scratch_shapes=[pltpu.VMEM((8, CONV_W), F32)],
        compiler_params=_params("arbitrary"),
    )(dmixed, gates, gates, g_conv, conv_w)


def _attn_bwd(qkv, dmixed, attn, g_attn, sinks, rope, sums):
    t = qkv.shape[0]
    n_steps = t // ATTN_STEP
    rev = lambda i: n_steps - 1 - i
    rc, rs1, rs2 = rope

    def body(sink_ref, q_ref, kp_ref, kc_ref, vp_ref, vc_ref, dm_ref, attn_ref, ga_ref, c_ref, s1_ref, s2_ref, sums_ref,
             dqkv_ref, dsink_ref, dgain_ref, arrived_ref, ck_ref, cv_ref, kacc_ref, vacc_ref, send_sems, recv_sems):
        i = pl.program_id(0)
        _chip_exchange_beside(i == 0, i == n_steps - 1, [sums_ref], [arrived_ref], (send_sems, recv_sems))

        @pl.when(i == 0)
        def _():
            dsink_ref[...] = jnp.zeros_like(dsink_ref)
            dgain_ref[...] = jnp.zeros_like(dgain_ref)
            ck_ref[...] = jnp.zeros_like(ck_ref)
            cv_ref[...] = jnp.zeros_like(cv_ref)

        kacc_ref[...] = jnp.zeros_like(kacc_ref)
        vacc_ref[...] = jnp.zeros_like(vacc_ref)
        a = attn_ref[...]
        ra = _inv_rms(a)
        dattn, dgain = _rms_bwd(a * ra, ra, ga_ref[...], dm_ref[...])
        dgain_ref[0:1, :] += _colsum(dgain)
        qt = (q_ref[...] * ATTN_SCALE).T
        dot = dattn.astype(BF16).T
        keys = jnp.concatenate([kp_ref[...], kc_ref[...]], axis=0)
        vals = jnp.concatenate([vp_ref[...], vc_ref[...]], axis=0)
        sink = [_group_sinks(sink_ref, g) for g in range(N_KV)]
        c, s1, s2 = c_ref[...], s1_ref[...], s2_ref[...]
        lane = lax.broadcasted_iota(jnp.int32, (1, 128), 1)
        dsink = jnp.zeros((1, 128), F32)
        masks = _attn_masks(rev(i) > 0)
        dq_parts = []
        for b in range(ATTN_STEP_BLOCKS):
            window = slice(BLOCK * b, BLOCK * (b + 2))
            valid = masks[b]
            dq_parts.append([])
            dk_parts, dv_parts = [], []
            for g in range(N_KV):
                gs = slice(HEAD_DIM * g, HEAD_DIM * (g + 1))
                kk, vv = keys[window, gs], vals[window, gs]
                qtg, dotg = _heads_side_by_side(qt, g, b), _heads_side_by_side(dot, g, b)
                probs, psink = _attn_probs(qtg, kk, sink[g], valid)
                dp = _mm(vv, dotg)
                delta = jnp.sum(probs * dp, axis=0, keepdims=True)
                ds = (probs * (dp - delta)).astype(BF16)
                sink_terms = psink * delta
                for hh in range(GROUP):
                    head_sum = jnp.sum(sink_terms[:, BLOCK * hh:BLOCK * (hh + 1)])
                    dsink = dsink + jnp.where(lane == GROUP * g + hh, -head_sum, 0.0)
                dq_parts[b].append(_mm_tn(kk * ATTN_SCALE, ds))
                dk_parts.append(_mm_nt(ds, qtg))
                dv_parts.append(_mm_nt(probs.astype(BF16), dotg))
            kacc_ref[window, :] += jnp.concatenate(dk_parts, axis=1)
            vacc_ref[window, :] += jnp.concatenate(dv_parts, axis=1)
        dq = _to_token_rows(dq_parts)
        for ci in range(ATTN_W // 128):
            sl = slice(128 * ci, 128 * (ci + 1))
            dqkv_ref[:, sl] = _rope_transpose(dq[:, sl], c, s1, s2).astype(BF16)
        kacc_ref[ATTN_STEP:, :] += ck_ref[...]
        vacc_ref[ATTN_STEP:, :] += cv_ref[...]
        ck_ref[...] = kacc_ref[:BLOCK, :]
        cv_ref[...] = vacc_ref[:BLOCK, :]
        dqkv_ref[:, ATTN_W:ATTN_W + KV_W] = _rope_transpose(kacc_ref[BLOCK:, :], c, s1, s2).astype(BF16)
        dqkv_ref[:, ATTN_W + KV_W:] = vacc_ref[BLOCK:, :].astype(BF16)
        dsink_ref[0:1, :] += dsink

    blk = lambda w_: pl.BlockSpec((ATTN_STEP, w_), lambda i: (rev(i), 0))
    return pl.pallas_call(
        body, name="attn_bwd", grid=(n_steps,),
        in_specs=[pl.BlockSpec(memory_space=pltpu.SMEM)] + _qkv_specs(rev)
        + [blk(ATTN_W), blk(ATTN_W), _full((1, ATTN_W)), blk(128), blk(128), blk(128), HBM_SPEC],
        out_specs=[blk(QKV_W), _full((8, 128)), _full((SMALL_ROWS, ATTN_W)), HBM_SPEC],
        out_shape=[jax.ShapeDtypeStruct((t, QKV_W), BF16), jax.ShapeDtypeStruct((8, 128), F32),
                   jax.ShapeDtypeStruct((SMALL_ROWS, ATTN_W), F32), jax.ShapeDtypeStruct(sums.shape, sums.dtype)],
        scratch_shapes=[pltpu.VMEM((BLOCK, KV_W), F32), pltpu.VMEM((BLOCK, KV_W), F32),
                        pltpu.VMEM((ATTN_KEYS, KV_W), F32), pltpu.VMEM((ATTN_KEYS, KV_W), F32),
                        pltpu.SemaphoreType.DMA((len(CHIP_FLIPS),)), pltpu.SemaphoreType.DMA((len(CHIP_FLIPS),))],
        compiler_params=_params("arbitrary", barrier_id=6),
    )(sinks, qkv, qkv, qkv, qkv, qkv, dmixed, attn, g_attn, rc, rs1, rs2, sums)


def _grad_x_tile(dq, dg, x_hat, r, g1, w_ref, dh):
    dhn = _mm(dq, w_ref[:QKV_W, :]) + _mm(dg, w_ref[QKV_W:, :])
    dx, dg1 = _rms_bwd(x_hat, r, g1, dhn)
    return dh + dx, _colsum(dg1)


def _in_proj_bwd(dqkv, dgates, x, dh, g1, w_in, tm, out_sums):
    t = x.shape[0]
    n = t // tm
    n_cover = max(n // 2, 1)
    n_steps = n + n_cover
    n_far = len(CHIP_FLIPS)
    shard = (IN_SHARD, D_MODEL)

    def body(dq_ref, dg_ref, x_ref, dh_ref, g1_ref, w_ref, osums_ref,
             dx_ref, own_ref, sib_ref, far_ref, dg1_ref, oarrived_ref,
             acc_ref, send_buf, land_buf, pair_buf, d2d_send, d2d_recv, ici_send, ici_recv, o_send, o_recv):
        i = pl.program_id(0)
        x_pos, y_pos, c = _mesh_pos()
        my_chip = 2 * x_pos + y_pos
        sibling = (x_pos, y_pos, 1 - c)
        @pl.when(i == 0)
        def _():
            _enter_with(_sibling_and_chips(x_pos, y_pos, c))

        _chip_exchange_beside(i == 0, i == n_steps - 1, [osums_ref], [oarrived_ref], (o_send, o_recv), enter=False)

        def rows(d):
            return slice(IN_SHARD * d, IN_SHARD * (d + 1))

        def hand_over(chip):
            return _push(send_buf.at[chip], land_buf.at[chip], (d2d_send, d2d_recv), chip, sibling)

        def to_chip(chip, rel):
            return pltpu.make_async_remote_copy(
                src_ref=pair_buf.at[chip], dst_ref=far_ref.at[rel - 1], send_sem=ici_send.at[rel - 1],
                recv_sem=ici_recv.at[rel - 1], device_id=(chip // 2, chip % 2, c), device_id_type=MESH)

        @pl.when(i == 0)
        def _():
            acc_ref[...] = jnp.zeros_like(acc_ref)
            dg1_ref[...] = jnp.zeros_like(dg1_ref)

        def normed_x():
            xv = x_ref[...]
            r = _inv_rms(xv)
            return xv * r, r

        @pl.when(i < n)
        def _():
            hn = (normed_x()[0] * g1_ref[...]).astype(BF16)
            acc_ref[:QKV_W, :] += _mm_tn(dq_ref[...], hn)
            acc_ref[QKV_W:, :] += _mm_tn(dg_ref[...], hn)

        @pl.when(i == n - 1)
        def _():
            for d in range(N_DEV):
                @pl.when(d % 2 != c)
                def _():
                    send_buf[d // 2] = acc_ref[rows(d), :].astype(BF16)
                    hand_over(d // 2).start()
            for d in range(N_DEV):
                chip = d // 2

                @pl.when(d % 2 == c)
                def _():
                    hand_over(chip).wait_recv()

                    @pl.when(chip == my_chip)
                    def _():
                        own_ref[...] = acc_ref[rows(d), :]
                        sib_ref[...] = land_buf[chip]

                    @pl.when(chip != my_chip)
                    def _():
                        pair_buf[chip] = (acc_ref[rows(d), :] + land_buf[chip].astype(F32)).astype(BF16)
                        to_chip(chip, chip ^ my_chip).start()
            for chip in range(N_CHIPS):
                hand_over(chip).wait_send()

        @pl.when(i >= n)
        def _():
            x_hat, r = normed_x()
            dx_ref[...], dg1 = _grad_x_tile(dq_ref[...], dg_ref[...], x_hat, r, g1_ref[...], w_ref, dh_ref[...])
            dg1_ref[0:1, :] += dg1

        @pl.when(i == n_steps - 1)
        def _():
            for rel in range(1, n_far + 1):
                to_chip(0, rel).wait()

    both = lambda w_: pl.BlockSpec((tm, w_), lambda i: (i % n, 0))
    second = pl.BlockSpec((tm, D_MODEL), lambda i: (jnp.maximum(i - n, 0), 0))
    whole = lambda dtype: jax.ShapeDtypeStruct(shard, dtype)
    sems = lambda k: pltpu.SemaphoreType.DMA((k,))
    res = pl.pallas_call(
        body, name="in_proj_bwd", grid=(n_steps,),
        in_specs=[both(QKV_W), both(GATES_W), both(D_MODEL), second, _full((1, D_MODEL)), _resident((IN_COLS, D_MODEL)),
                  HBM_SPEC],
        out_specs=[second, _full(shard), _full(shard), HBM_SPEC, _full((SMALL_ROWS, D_MODEL)), HBM_SPEC],
        out_shape=[jax.ShapeDtypeStruct((n_cover * tm, D_MODEL), F32), whole(F32), whole(BF16),
                   jax.ShapeDtypeStruct((n_far,) + shard, BF16), jax.ShapeDtypeStruct((SMALL_ROWS, D_MODEL), F32),
                   jax.ShapeDtypeStruct(out_sums.shape, out_sums.dtype)],
        scratch_shapes=[pltpu.VMEM((IN_COLS, D_MODEL), F32), pltpu.VMEM((N_CHIPS,) + shard, BF16),
                        pltpu.VMEM((N_CHIPS,) + shard, BF16), pltpu.VMEM((N_CHIPS,) + shard, BF16),
                        sems(N_CHIPS), sems(N_CHIPS), sems(n_far), sems(n_far), sems(n_far), sems(n_far)],
        compiler_params=_params("arbitrary", barrier_id=7),
    )(dqkv, dgates, x, dh, g1, w_in, out_sums)
    return res[0], (res[1], res[2], res[3]), res[4], res[5]


def _grad_x_rest(dqkv, dgates, x, dh, g1, w_in, tm, head, dg1_rows):
    t = x.shape[0]
    first = head.shape[0] // tm
    n_rest = t // tm - first
    if n_rest == 0:
        return head, dg1_rows
    assert first <= n_rest

    def body(dq_ref, dg_ref, x_ref, dh_ref, g1_ref, w_ref, head_ref, rows_ref, gx_ref, dg1_ref, stage, sems):
        j = pl.program_id(0)

        def tile_out(step, kind):
            row0 = (step + first) * tm if kind == 0 else step * tm
            slot = 2 * kind + step % 2
            return pltpu.make_async_copy(stage.at[slot], gx_ref.at[pl.ds(pl.multiple_of(row0, tm), tm), :], sems.at[slot])

        @pl.when(j == 0)
        def _():
            dg1_ref[...] = rows_ref[...]

        @pl.when(j >= 2)
        def _():
            tile_out(j - 2, 0).wait()

        @pl.when((j >= 2) & (j - 2 < first))
        def _():
            tile_out(j - 2, 1).wait()

        @pl.when(j < first)
        def _():
            stage[2 + j % 2] = head_ref[...]
            tile_out(j, 1).start()

        xv = x_ref[...]
        r = _inv_rms(xv)
        dx, dg1 = _grad_x_tile(dq_ref[...], dg_ref[...], xv * r, r, g1_ref[...], w_ref, dh_ref[...])
        stage[j % 2] = dx
        dg1_ref[0:1, :] += dg1
        tile_out(j, 0).start()

        @pl.when(j == n_rest - 1)
        def _():
            for back in range(min(2, n_rest)):
                tile_out(j - back, 0).wait()

                @pl.when(j - back < first)
                def _():
                    tile_out(j - back, 1).wait()

    tile = lambda w_: pl.BlockSpec((tm, w_), lambda j: (j + first, 0))
    head_tile = pl.BlockSpec((tm, D_MODEL), lambda j: (jnp.minimum(j, first - 1), 0))
    return pl.pallas_call(
        body, name="grad_x_rest", grid=(n_rest,),
        in_specs=[tile(QKV_W), tile(GATES_W), tile(D_MODEL), tile(D_MODEL), _full((1, D_MODEL)),
                  _resident((IN_COLS, D_MODEL)), head_tile, _full((SMALL_ROWS, D_MODEL))],
        out_specs=[HBM_SPEC, _full((SMALL_ROWS, D_MODEL))],
        out_shape=[jax.ShapeDtypeStruct((t, D_MODEL), F32), jax.ShapeDtypeStruct((SMALL_ROWS, D_MODEL), F32)],
        scratch_shapes=[pltpu.VMEM((4, tm, D_MODEL), F32), pltpu.SemaphoreType.DMA((4,))],
        compiler_params=_params("arbitrary"),
    )(dqkv, dgates, x, dh, g1, w_in, head, dg1_rows)


def _all_gather(shards, name):
    n = len(shards)

    def body(*refs):
        _enter_with(_sibling_and_chips(*_mesh_pos()))
        start, finish = _gather_steps(refs[:n], refs[n:2 * n], *refs[2 * n:])
        start()
        finish()

    return pl.pallas_call(
        body, name=name,
        in_specs=[HBM_SPEC] * n, out_specs=[HBM_SPEC] * n,
        out_shape=[jax.ShapeDtypeStruct((N_DEV,) + s.shape, s.dtype) for s in shards],
        scratch_shapes=[pltpu.SemaphoreType.DMA((7 * n,)), pltpu.SemaphoreType.DMA((7 * n,)),
                        pltpu.SemaphoreType.DMA((n,))],
        compiler_params=_params(barrier_id=8),
    )(*shards)


def _adam_math(w, g, m, v):
    m = ADAM_B1 * m + (1.0 - ADAM_B1) * g
    v = ADAM_B2 * v + (1.0 - ADAM_B2) * (g * g)
    m_hat = m / (1.0 - ADAM_B1 ** ADAM_STEP)
    v_hat = v / (1.0 - ADAM_B2 ** ADAM_STEP)
    delta = -ADAM_LR * (m_hat / (jnp.sqrt(v_hat) + ADAM_EPS) + ADAM_WD * w)
    return delta, m, v


def _adamw_reduced(tensors, n_steps):
    n_far = len(CHIP_FLIPS)

    def body(*refs):
        ins, outs = refs[:6 * len(tensors)], refs[6 * len(tensors):]
        for k in range(len(tensors)):
            w_ref, m_ref, v_ref, own_ref, sib_ref, far_ref = ins[6 * k:6 * k + 6]
            g_ref, d_ref, nm_ref, nv_ref = outs[4 * k:4 * k + 4]
            g = own_ref[...] + sib_ref[...].astype(F32)
            for j in range(n_far):
                g = g + far_ref[j].astype(F32)
            g_ref[...] = g
            d_ref[...], nm_ref[...], nv_ref[...] = _adam_math(w_ref[...], g, m_ref[...], v_ref[...])

    in_specs, out_specs, out_shape = [], [], []
    for w, *_ in tensors:
        rows, cols = w.shape
        tile = pl.BlockSpec((rows // n_steps, cols), lambda i: (i, 0))
        in_specs += [tile] * 5 + [pl.BlockSpec((n_far, rows // n_steps, cols), lambda i: (0, i, 0))]
        out_specs += [tile] * 4
        out_shape += [jax.ShapeDtypeStruct((rows, cols), F32)] * 4
    res = pl.pallas_call(
        body, name="adamw_reduced", grid=(n_steps,), in_specs=in_specs, out_specs=out_specs, out_shape=out_shape,
        compiler_params=_params("parallel"),
    )(*[a for tensor in tensors for a in tensor])
    return [res[4 * k:4 * k + 4] for k in range(len(tensors))]


SMALL_PARAMS = ("pre_mix_norm", "post_mix_norm", "pre_mlp_norm", "post_mlp_norm", "attn_group_norm", "conv_group_norm",
                "conv_w", "attn_sinks")


SMALL_WIDTHS = (D_MODEL, CONV_W, ATTN_W, 128, D_MODEL)


def _small_tail(gathered, dev, weights, first_moments, second_moments):
    n = len(SMALL_PARAMS)
    conv_shard = CONV_W // N_DEV

    def body(dev_ref, sums_ref, *refs):
        w_refs, m_refs, v_refs = refs[:n], refs[n:2 * n], refs[2 * n:3 * n]
        loss_ref, outs = refs[3 * n], refs[3 * n + 1:]
        total = sums_ref[0]
        for d in range(1, N_DEV):
            total = total + sums_ref[d]
        starts = [sum(SMALL_WIDTHS[:i]) for i in range(len(SMALL_WIDTHS))]
        mid, conv, gain, sink, inp = (total[:, a:a + w_] for a, w_ in zip(starts, SMALL_WIDTHS))
        loss_ref[...] = (0.5 / D_MODEL) * jnp.sum(mid[ROW_LOSS:ROW_LOSS + 1, :], axis=1, keepdims=True)
        conv_rows = conv[ROW_CW0:ROW_CW0 + 3, :]
        conv_g = jnp.zeros((3, conv_shard), F32)
        for d in range(N_DEV):
            conv_g = conv_g + jnp.where(dev_ref[0] == d, conv_rows[:, conv_shard * d:conv_shard * (d + 1)], 0.0)
        grads = [inp[0:1, :], mid[ROW_G2:ROW_G2 + 1, :], mid[ROW_G3:ROW_G3 + 1, :], mid[ROW_G4:ROW_G4 + 1, :],
                 gain[0:1, :], conv[ROW_GCONV:ROW_GCONV + 1, :], conv_g, sink[0:1, :N_HEADS]]
        for i, g in enumerate(grads):
            parts = [(..., g)] if len(w_refs[i].shape) == 2 else [(r, g[r:r + 1, :]) for r in range(g.shape[0])]
            for at, g_at in parts:
                delta, new_m, new_v = _adam_math(w_refs[i][at], g_at, m_refs[i][at], v_refs[i][at])
                outs[i][at], outs[n + i][at], outs[2 * n + i][at], outs[3 * n + i][at] = g_at, delta, new_m, new_v

    params = list(weights) + list(first_moments) + list(second_moments)
    shapes = [jax.ShapeDtypeStruct(w.shape, F32) for w in weights]
    res = pl.pallas_call(
        body, name="small_tail", grid=(1,),
        in_specs=[pl.BlockSpec(memory_space=pltpu.SMEM), _full(gathered.shape)] + [_full(p.shape) for p in params],
        out_specs=[_full((1, 1))] + [_full(sh.shape) for sh in shapes] * 4,
        out_shape=[jax.ShapeDtypeStruct((1, 1), F32)] + shapes * 4,
    )(dev, gathered, *params)
    return res[0], [res[1 + k * n:1 + (k + 1) * n] for k in range(4)]


TOKEN_TILE = 512
MID_TILE = 256
MID_CHUNK = 1024
MID_CHUNKS = D_FF // MID_CHUNK
ADAM_STEPS = 2


def _local_grads(x, target, g1, w_in_shard, conv_shard, sinks, g_attn, g_conv, g2, g3, g4, shards, order):
    t = x.shape[0]
    tm = min(TOKEN_TILE, t)
    rope = _rope_tables(t)
    qkv, gates, mconv, w_in, conv_w, gathered = _in_proj_fwd(x, g1, w_in_shard, conv_shard, g_conv, rope, tm, shards,
                                                             (False, True, False))
    attn, mattn, (w_out, w_up, w_down) = _attn_fwd(qkv, sinks, g_attn, shards, gathered)
    act, dup, hn2t, dmo, dmix, dh, dmixed, small_mid = _mid(
        mattn, mconv, x, target, g2, g3, g4, w_out.reshape(D_MODEL, D_MODEL),
        w_up, w_down.reshape(D_FF, D_MODEL), min(MID_TILE, t))
    up_own, up_sib, up_sums = _dw_pair_sums((hn2t, dup), order, "up", "dw_up", 2)
    down_own, down_sib, down_sums, up_far = _dw_pair_sums((act, dmo), order, "down", "dw_down", 3, ride=up_sums)
    out_own, out_sib, out_sums = _dw_pair_sums((mattn, mconv, dmix), order, "out", "dw_out", 4)
    dgates, small_conv = _conv_bwd(dmixed, gates, g_conv, conv_w, tm)
    dqkv, dsink, dg_attn, down_far = _attn_bwd(qkv, dmixed, attn, g_attn, sinks, rope, down_sums)
    grad_x_head, dw_in, small_in, out_far = _in_proj_bwd(dqkv, dgates, x, dh, g1, w_in, tm, out_sums)
    grad_x, small_in = _grad_x_rest(dqkv, dgates, x, dh, g1, w_in, tm, grad_x_head, small_in)
    dw_out, dw_up, dw_down = (out_own, out_sib, out_far), (up_own, up_sib, up_far), (down_own, down_sib, down_far)
    return grad_x, dw_in, dw_out, dw_up, dw_down, (small_mid, small_conv, dg_attn, dsink, small_in)


def kernel(x, pre_mix_norm, w_in, conv_w, attn_sinks, attn_group_norm, conv_group_norm, w_out, post_mix_norm, pre_mlp_norm, w_up, w_down, post_mlp_norm, loss_target, m_pre_mix_norm, m_w_in, m_conv_w, m_attn_sinks, m_attn_group_norm, m_conv_group_norm, m_w_out, m_post_mix_norm, m_pre_mlp_norm, m_w_up, m_w_down, m_post_mlp_norm, v_pre_mix_norm, v_w_in, v_conv_w, v_attn_sinks, v_attn_group_norm, v_conv_group_norm, v_w_out, v_post_mix_norm, v_pre_mlp_norm, v_w_up, v_w_down, v_post_mlp_norm):
    xi, yi, ci = _mesh_pos()
    chip = 2 * xi + yi
    dev = 2 * chip + ci

    order = _block_order(dev)

    shards = [w_out[0].astype(BF16), w_up[0].astype(BF16), w_down[0].astype(BF16)]

    turned = lambda a: jnp.swapaxes(a, 1, 2)
    grad_x, dw_in, dw_out, dw_up, dw_down, smalls = _local_grads(
        x[0], loss_target[0], pre_mix_norm, turned(w_in)[0].astype(BF16), conv_w[0], attn_sinks, attn_group_norm, conv_group_norm,
        post_mix_norm, pre_mlp_norm, post_mlp_norm, shards, order)

    blocks = {"w_in": (turned(w_in), turned(m_w_in), turned(v_w_in), dw_in), "w_out": (w_out, m_w_out, v_w_out, dw_out),
              "w_up": (w_up, m_w_up, v_w_up, dw_up), "w_down": (w_down, m_w_down, v_w_down, dw_down)}
    big = {}
    for names in (("w_in", "w_out", "w_down"), ("w_up",)):
        stepped = _adamw_reduced([(w[0], m[0], v[0], *dw) for w, m, v, dw in (blocks[nm] for nm in names)], ADAM_STEPS)
        for nm, res in zip(names, stepped):
            big[nm] = [a[None] for a in res]
    big["w_in"] = [turned(a) for a in big["w_in"]]

    flat = lambda a: a.reshape(-1, a.shape[-1]) if a.ndim < 3 else a.reshape(a.shape[1], 1, a.shape[2])
    loss, small = _small_tail(
        _all_gather([jnp.concatenate(smalls, axis=1)], "gather_small")[0], dev.reshape(1).astype(jnp.int32),
        [flat(a) for a in (pre_mix_norm, post_mix_norm, pre_mlp_norm, post_mlp_norm, attn_group_norm, conv_group_norm,
                           conv_w, attn_sinks)],
        [flat(a) for a in (m_pre_mix_norm, m_post_mix_norm, m_pre_mlp_norm, m_post_mlp_norm, m_attn_group_norm,
                           m_conv_group_norm, m_conv_w, m_attn_sinks)],
        [flat(a) for a in (v_pre_mix_norm, v_post_mix_norm, v_pre_mlp_norm, v_post_mlp_norm, v_attn_group_norm,
                           v_conv_group_norm, v_conv_w, v_attn_sinks)])

    order = ("pre_mix_norm", "w_in", "conv_w", "attn_sinks", "attn_group_norm", "conv_group_norm", "w_out",
             "post_mix_norm", "pre_mlp_norm", "w_up", "w_down", "post_mlp_norm")
    shape_of = {"conv_w": conv_w.shape}
    outs = []
    for k in range(4):
        by_name = dict(zip(SMALL_PARAMS, small[k]))
        outs += [big[nm][k] if nm in big else by_name[nm].reshape(shape_of.get(nm, by_name[nm].shape)) for nm in order]
    loss = loss.reshape(())
    return (loss, grad_x[None], *outs)
```

```python
import jax
import jax.numpy as jnp
import numpy as np
from jax import lax
from jax.experimental import pallas as pl
from jax.experimental.pallas import tpu as pltpu

F32 = jnp.float32
BF16 = jnp.bfloat16

D_MODEL = 1024
HEAD_DIM = 64
ATTN_W = 512
CONV_W = 512
N_HEADS = 8
N_KV = 2
GROUP = 4
KV_W = 128
QKV_W = ATTN_W + 2 * KV_W
GATES_W = 3 * CONV_W
IN_COLS = QKV_W + GATES_W
D_FF = 4096
FF_CHUNK = 512
BLOCK = 128
ROT_HALF = 8
ROPE_THETA = 500000.0
NORM_EPS = 1e-6
NEG_INF = -1e30
ATTN_SCALE = 0.125
N_DEV = 8
N_CHIPS = 4
IN_SHARD = IN_COLS // N_DEV

ADAM_LR = 0.001
ADAM_B1 = 0.9
ADAM_B2 = 0.999
ADAM_EPS = 1e-08
ADAM_WD = 0.01
ADAM_STEP = 10

V7X_VMEM_BYTES = 64 * 1024 * 1024
VMEM_LIMIT = V7X_VMEM_BYTES - 2 * 1024 * 1024

MESH = pl.DeviceIdType.MESH
HBM_SPEC = pl.BlockSpec(memory_space=pltpu.HBM)


def _params(*sem, barrier_id=None):
    return pltpu.CompilerParams(dimension_semantics=sem or None, vmem_limit_bytes=VMEM_LIMIT, collective_id=barrier_id)


def _mm(a, b):
    return jnp.dot(a, b, preferred_element_type=F32)


def _mm_nt(a, b):
    return lax.dot_general(a, b, (((1,), (1,)), ((), ())), preferred_element_type=F32)


def _mm_tn(a, b):
    return lax.dot_general(a, b, (((0,), (0,)), ((), ())), preferred_element_type=F32)


def _inv_rms(x):
    return lax.rsqrt(jnp.mean(x * x, axis=-1, keepdims=True) + NORM_EPS)


def _rms_bwd(xhat, r, gain, dy):
    gy = dy * gain
    return r * (gy - xhat * jnp.mean(gy * xhat, axis=-1, keepdims=True)), dy * xhat


def _colsum(a):
    return jnp.sum(a, axis=0, keepdims=True)


def _full(shape):
    zeros = (0,) * len(shape)
    return pl.BlockSpec(shape, lambda *_: zeros)


def _resident(shape):
    zeros = (0,) * len(shape)
    return pl.BlockSpec(shape, lambda *_: zeros, pipeline_mode=pl.Buffered(1))


def _rope_tables(t):
    pos = np.arange(t, dtype=np.float32)
    inv_freq = (ROPE_THETA ** (-np.arange(0, 2 * ROT_HALF, 2, dtype=np.float64) / (2 * ROT_HALF))).astype(np.float32)
    ang = (pos[:, None] * inv_freq[None, :]).astype(np.float64)
    cos, sin = np.cos(ang).astype(np.float32), np.sin(ang).astype(np.float32)
    zeros8 = np.zeros((t, ROT_HALF), np.float32)
    rest = np.zeros((t, HEAD_DIM - 2 * ROT_HALF), np.float32)
    c_head = np.concatenate([cos, cos, rest + 1.0], axis=1)
    s1_head = np.concatenate([zeros8, sin, rest], axis=1)
    s2_head = np.concatenate([-sin, zeros8, rest], axis=1)
    two = lambda a: jnp.asarray(np.concatenate([a, a], axis=1))
    return two(c_head), two(s1_head), two(s2_head)


def _rope(v, c, s1, s2):
    return v * c + pltpu.roll(v, ROT_HALF, 1) * s1 + pltpu.roll(v, 128 - ROT_HALF, 1) * s2


def _rope_transpose(dv, c, s1, s2):
    return dv * c + pltpu.roll(dv * s1, 128 - ROT_HALF, 1) + pltpu.roll(dv * s2, ROT_HALF, 1)


def _shift_rows_down(u, prev, k):
    row = lax.broadcasted_iota(jnp.int32, u.shape, 0)
    out = pltpu.roll(u, k, 0)
    for r in range(k):
        out = jnp.where(row == r, prev[8 - k + r:8 - k + r + 1, :], out)
    return out


def _shift_rows_up(u, nxt, k):
    n = u.shape[0]
    row = lax.broadcasted_iota(jnp.int32, u.shape, 0)
    out = pltpu.roll(u, n - k, 0)
    for r in range(k):
        out = jnp.where(row == n - k + r, nxt[r:r + 1, :], out)
    return out


def _conv3(u, u1, u2, w):
    return (w[0:1, :] * u2 + w[1:2, :] * u1) + w[2:3, :] * u


def _mesh_pos():
    return lax.axis_index("x"), lax.axis_index("y"), lax.axis_index("c")


def _slot(ref, pos):
    dev = 4 * pos[0] + 2 * pos[1] + pos[2]
    if len(ref.shape) == 2:
        width = ref.shape[1] // N_DEV
        return ref.at[:, pl.ds(pl.multiple_of(dev * width, width), width)]
    return ref.at[dev]


def _gathered_shape(shard, by_cols):
    if by_cols:
        return jax.ShapeDtypeStruct((shard.shape[0], N_DEV * shard.shape[1]), shard.dtype)
    return jax.ShapeDtypeStruct((N_DEV,) + shard.shape, shard.dtype)


def _enter_with(peers):
    barrier = pltpu.get_barrier_semaphore()
    for peer in peers:
        pl.semaphore_signal(barrier, inc=1, device_id=peer, device_id_type=MESH)
    pl.semaphore_wait(barrier, len(peers))


def _sibling_and_chips(x, y, c):
    return [(x, y, 1 - c), (1 - x, y, c), (x, 1 - y, c), (1 - x, 1 - y, c)]


def _push(src, dst, sems, k, to):
    send_sems, recv_sems = sems
    return pltpu.make_async_remote_copy(src_ref=src, dst_ref=dst, send_sem=send_sems.at[k], recv_sem=recv_sems.at[k],
                                        device_id=to, device_id_type=MESH)


def _gather_steps(shards, outs, send_sems, recv_sems, local_sems):
    n = len(shards)
    x, y, c = _mesh_pos()
    me, sibling = (x, y, c), (x, y, 1 - c)
    chips = [(1 - x, y), (x, 1 - y), (1 - x, 1 - y)]

    def copy(i, k, block, to, src=None):
        dst = _slot(outs[i], block)
        return _push(dst if src is None else src, dst, (send_sems, recv_sems), 7 * i + k, to)

    mine = [pltpu.make_async_copy(shards[i], _slot(outs[i], me), local_sems.at[i]) for i in range(n)]
    first = []
    for i in range(n):
        first.append(copy(i, 0, me, sibling, src=shards[i]))
        first += [copy(i, 1 + j, me, (*chip, c), src=shards[i]) for j, chip in enumerate(chips)]

    def start():
        for cp in mine + first:
            cp.start()

    def finish():
        passed = []
        for j, chip in enumerate(chips):
            for i in range(n):
                copy(i, 1 + j, (*chip, c), me).wait_recv()
                cp = copy(i, 4 + j, (*chip, c), sibling)
                cp.start()
                passed.append(cp)
        for i in range(n):
            copy(i, 0, sibling, me).wait_recv()
            for j, chip in enumerate(chips):
                copy(i, 4 + j, (*chip, 1 - c), me).wait_recv()
        for cp in first + passed:
            cp.wait_send()
        for cp in mine:
            cp.wait()

    return start, finish


def _gather_near(first, last, shards, outs, sems, local_sems):
    x, y, c = _mesh_pos()
    me, peers = (x, y, c), [(x, y, 1 - c), (1 - x, y, c), (x, 1 - y, c)]
    n = len(shards)
    local = [pltpu.make_async_copy(shards[i], _slot(outs[i], me), local_sems.at[i]) for i in range(n)]
    sends = [_push(shards[i], _slot(outs[i], me), sems, 3 * i + k, peers[k]) for i in range(n) for k in range(3)]
    arrivals = [_push(shards[i], _slot(outs[i], peers[k]), sems, 3 * i + k, peers[k]) for i in range(n) for k in range(3)]

    def start():
        for cp in local + sends:
            cp.start()

    if first is not None:
        pl.when(first)(start)

    @pl.when(last)
    def _():
        for cp in sends:
            cp.wait_send()
        for cp in arrivals:
            cp.wait_recv()
        for cp in local:
            cp.wait()

    return start


def _relay_route(x, y, c):
    south = c == 0
    via = (jnp.where(south, 1 - x, x), jnp.where(south, y, 1 - y))
    to = (jnp.where(south, x, 1 - x), jnp.where(south, 1 - y, y))
    return via, to


def _gather_far(first, middle, last, shards, ins, outs, sems):
    x, y, c = _mesh_pos()
    sibling = (x, y, 1 - c)
    chips = [(1 - x, y), (x, 1 - y), (1 - x, 1 - y)]
    via, to = _relay_route(x, y, c)
    n = len(shards)
    diag_send = [_push(_slot(ins[i], (*via, c)), _slot(outs[i], (*via, c)), sems, 4 * i, (*to, c)) for i in range(n)]
    diag_arrival = [_push(shards[i], _slot(outs[i], (*chips[2], c)), sems, 4 * i, (*to, c)) for i in range(n)]
    passed = [[_push(_slot(ins[i], (*chips[j], c)), _slot(outs[i], (*chips[j], c)), sems, 4 * i + 1 + j, sibling)
               for i in range(n)] for j in range(3)]
    from_sibling = [_push(shards[i], _slot(outs[i], (*chips[j], 1 - c)), sems, 4 * i + 1 + j, sibling)
                    for i in range(n) for j in range(3)]

    @pl.when(first)
    def _():
        for cp in diag_send + passed[0] + passed[1]:
            cp.start()

    @pl.when(middle)
    def _():
        for cp in diag_arrival:
            cp.wait_recv()
        for cp in passed[2]:
            cp.start()

    @pl.when(last)
    def _():
        for cp in from_sibling:
            cp.wait_recv()
        for cp in diag_send + passed[0] + passed[1] + passed[2]:
            cp.wait_send()


def _in_proj_fwd(x, g1, w_in, conv_w, g_conv, rope, tm, shards, by_cols):
    t = x.shape[0]
    rc, rs1, rs2 = rope
    n = len(shards)
    n_tiles = t // tm

    def body(*refs):
        x_ref, g1_ref, w_ref, cw_ref, gc_ref, c_ref, s1_ref, s2_ref = refs[:8]
        shard_refs = refs[8:8 + n]
        qkv_ref, gates_ref, mconv_ref, w_full_ref, cw_full_ref = refs[8 + n:13 + n]
        gathered = refs[13 + n:13 + 2 * n]
        carry_ref, w_land, cw_land, hn_ref = refs[13 + 2 * n:17 + 2 * n]
        now_sems = refs[17 + 2 * n:20 + 2 * n]
        step = pl.program_id(0)
        start_later_weights = _gather_near(None, step == 2 * n_tiles - 1, shard_refs, gathered,
                                           refs[20 + 2 * n:22 + 2 * n], refs[22 + 2 * n]) if n else None
        start_w_in, finish_w_in = _gather_steps([w_ref, cw_ref], [w_land, cw_land], *now_sems)

        @pl.when(step == 0)
        def _():
            carry_ref[...] = jnp.zeros_like(carry_ref)
            _enter_with(_sibling_and_chips(*_mesh_pos()))
            start_w_in()
            if start_later_weights is not None:
                start_later_weights()

        @pl.when(step < n_tiles)
        def _():
            xv = x_ref[...]
            hn_ref[step] = ((xv * _inv_rms(xv)) * g1_ref[...]).astype(BF16)

        @pl.when(step == n_tiles)
        def _():
            finish_w_in()
            conv_shard = CONV_W // N_DEV
            for d in range(N_DEV):
                w_full_ref[IN_SHARD * d:IN_SHARD * (d + 1), :] = w_land[d]
                cw_full_ref[:, conv_shard * d:conv_shard * (d + 1)] = cw_land[d]

        @pl.when(step >= n_tiles)
        def _():
            proj = _mm_nt(hn_ref[step - n_tiles], w_full_ref[...])
            c, s1, s2 = c_ref[...], s1_ref[...], s2_ref[...]
            for ci in range((ATTN_W + KV_W) // 128):
                sl = slice(128 * ci, 128 * (ci + 1))
                qkv_ref[:, sl] = _rope(proj[:, sl], c, s1, s2).astype(BF16)
            qkv_ref[:, ATTN_W + KV_W:QKV_W] = proj[:, ATTN_W + KV_W:QKV_W].astype(BF16)
            gates = proj[:, QKV_W:]
            gates_ref[...] = gates
            gb, gcc, xin = gates[:, :CONV_W], gates[:, CONV_W:2 * CONV_W], gates[:, 2 * CONV_W:]
            u = gcc * xin
            prev = carry_ref[...]
            conv = gb * _conv3(u, _shift_rows_down(u, prev, 1), _shift_rows_down(u, prev, 2), cw_full_ref[...])
            carry_ref[...] = u[tm - 8:tm, :]
            mconv_ref[...] = ((conv * _inv_rms(conv)) * gc_ref[...]).astype(BF16)

    first_pass = pl.BlockSpec((tm, D_MODEL), lambda i: (jnp.minimum(i, n_tiles - 1), 0))
    tile = lambda w_: pl.BlockSpec((tm, w_), lambda i: (jnp.maximum(i - n_tiles, 0), 0))
    sems = lambda k: pltpu.SemaphoreType.DMA((k,))
    res = pl.pallas_call(
        body, name="in_proj_fwd", grid=(2 * n_tiles,),
        in_specs=[first_pass, _full((1, D_MODEL)), HBM_SPEC, HBM_SPEC, _full((1, CONV_W)), tile(128), tile(128),
                  tile(128)] + [HBM_SPEC] * n,
        out_specs=[tile(QKV_W), tile(GATES_W), tile(CONV_W), _full((IN_COLS, D_MODEL)), _full((3, CONV_W))]
        + [HBM_SPEC] * n,
        out_shape=[jax.ShapeDtypeStruct((t, QKV_W), BF16), jax.ShapeDtypeStruct((t, GATES_W), F32),
                   jax.ShapeDtypeStruct((t, CONV_W), BF16), jax.ShapeDtypeStruct((IN_COLS, D_MODEL), BF16),
                   jax.ShapeDtypeStruct((3, CONV_W), F32)]
        + [_gathered_shape(s, cols) for s, cols in zip(shards, by_cols)],
        scratch_shapes=[pltpu.VMEM((8, CONV_W), F32), pltpu.VMEM((N_DEV,) + w_in.shape, BF16),
                        pltpu.VMEM((N_DEV,) + conv_w.shape, F32), pltpu.VMEM((n_tiles, tm, D_MODEL), BF16),
                        sems(14), sems(14), sems(2)]
        + ([sems(3 * n), sems(3 * n), sems(n)] if n else []),
        compiler_params=_params("arbitrary", barrier_id=0),
    )(x, g1, w_in, conv_w, g_conv, rc, rs1, rs2, *shards)
    return res[0], res[1], res[2], res[3], res[4], list(res[5:])


GROUP_COLS = GROUP * BLOCK
HEADS_PER_PIECE = 2
HEAD_PIECES = [slice(BLOCK * h, BLOCK * (h + HEADS_PER_PIECE)) for h in range(0, GROUP, HEADS_PER_PIECE)]
ATTN_STEP_BLOCKS = 8


def _attn_masks(has_prev):
    key = lax.broadcasted_iota(jnp.int32, (2 * BLOCK, GROUP_COLS), 0)
    query = lax.broadcasted_iota(jnp.int32, (2 * BLOCK, GROUP_COLS), 1) & (BLOCK - 1)
    band = (key > query) & (key <= query + BLOCK)
    return [band & ((key >= BLOCK) | has_prev)] + [band] * (ATTN_STEP_BLOCKS - 1)


def _heads_side_by_side(at, g, b):
    heads = [at[HEAD_DIM * (GROUP * g + hh):HEAD_DIM * (GROUP * g + hh + 1), BLOCK * b:BLOCK * (b + 1)] for hh in range(GROUP)]
    return jnp.concatenate(heads, axis=1)


def _to_token_rows(parts):
    rows = [jnp.concatenate([parts[b][g][:, BLOCK * hh:BLOCK * (hh + 1)] for b in range(ATTN_STEP_BLOCKS)], axis=1)
            for g in range(N_KV) for hh in range(GROUP)]
    return jnp.concatenate(rows, axis=0).T


def _group_sinks(sink_ref, g):
    head = lax.broadcasted_iota(jnp.int32, (1, GROUP_COLS), 1) // BLOCK
    out = jnp.full((1, GROUP_COLS), sink_ref[0, GROUP * g], F32)
    for hh in range(1, GROUP):
        out = jnp.where(head == hh, sink_ref[0, GROUP * g + hh], out)
    return out


def _attn_probs(qt, kk, sink, valid):
    scores = _mm(kk, qt)
    probs, psinks = [], []
    for piece in HEAD_PIECES:
        s = jnp.where(valid[:, piece], scores[:, piece], NEG_INF)
        m = jnp.maximum(jnp.max(s, axis=0, keepdims=True), sink[:, piece])
        p = jnp.exp(s - m)
        psink = jnp.exp(sink[:, piece] - m)
        inv_l = 1.0 / (jnp.sum(p, axis=0, keepdims=True) + psink)
        probs.append(p * inv_l)
        psinks.append(psink * inv_l)
    return jnp.concatenate(probs, axis=1), jnp.concatenate(psinks, axis=1)


ATTN_STEP = ATTN_STEP_BLOCKS * BLOCK
ATTN_KEYS = ATTN_STEP + BLOCK


def _qkv_specs(order):
    prev = lambda i: jnp.maximum(ATTN_STEP_BLOCKS * order(i) - 1, 0)
    kcol, vcol = ATTN_W // KV_W, ATTN_W // KV_W + 1
    return [pl.BlockSpec((ATTN_STEP, ATTN_W), lambda i: (order(i), 0)),
            pl.BlockSpec((BLOCK, KV_W), lambda i: (prev(i), kcol)), pl.BlockSpec((ATTN_STEP, KV_W), lambda i: (order(i), kcol)),
            pl.BlockSpec((BLOCK, KV_W), lambda i: (prev(i), vcol)), pl.BlockSpec((ATTN_STEP, KV_W), lambda i: (order(i), vcol))]


def _attn_fwd(qkv, sinks, g_attn, shards, gathered):
    t = qkv.shape[0]
    n = len(shards)

    def body(*refs):
        sink_ref, q_ref, kp_ref, kc_ref, vp_ref, vc_ref, ga_ref = refs[:7]
        attn_ref, mattn_ref = refs[7 + 2 * n:9 + 2 * n]
        step = pl.program_id(0)
        if n:
            @pl.when(step == 0)
            def _():
                x, y, c = _mesh_pos()
                _enter_with([(x, y, 1 - c), (*_relay_route(x, y, c)[1], c)])

            n_steps = t // ATTN_STEP
            _gather_far(step == 0, step == n_steps // 2, step == n_steps - 1, refs[7:7 + n], refs[7 + n:7 + 2 * n],
                        refs[9 + 2 * n:9 + 3 * n], refs[9 + 3 * n:11 + 3 * n])
        qt = (q_ref[...] * ATTN_SCALE).T
        keys = jnp.concatenate([kp_ref[...], kc_ref[...]], axis=0)
        vals = jnp.concatenate([vp_ref[...], vc_ref[...]], axis=0)
        sink = [_group_sinks(sink_ref, g) for g in range(N_KV)]
        masks = _attn_masks(step > 0)
        parts = []
        for b in range(ATTN_STEP_BLOCKS):
            window = slice(BLOCK * b, BLOCK * (b + 2))
            valid = masks[b]
            parts.append([])
            for g in range(N_KV):
                gs = slice(HEAD_DIM * g, HEAD_DIM * (g + 1))
                probs, _ = _attn_probs(_heads_side_by_side(qt, g, b), keys[window, gs], sink[g], valid)
                parts[b].append(_mm_tn(vals[window, gs], probs.astype(BF16)))
        attn = _to_token_rows(parts)
        attn_ref[...] = attn
        mattn_ref[...] = ((attn * _inv_rms(attn)) * ga_ref[...]).astype(BF16)

    blk = pl.BlockSpec((ATTN_STEP, ATTN_W), lambda j: (j, 0))
    res = pl.pallas_call(
        body, name="attn_fwd", grid=(t // ATTN_STEP,),
        in_specs=[pl.BlockSpec(memory_space=pltpu.SMEM)] + _qkv_specs(lambda j: j) + [_full((1, ATTN_W))]
        + [HBM_SPEC] * (2 * n),
        out_specs=[blk, blk] + [HBM_SPEC] * n,
        out_shape=[jax.ShapeDtypeStruct((t, ATTN_W), F32), jax.ShapeDtypeStruct((t, ATTN_W), BF16)]
        + [jax.ShapeDtypeStruct(g.shape, g.dtype) for g in gathered],
        input_output_aliases={7 + n + i: 2 + i for i in range(n)},
        scratch_shapes=[pltpu.SemaphoreType.DMA((4 * n,)), pltpu.SemaphoreType.DMA((4 * n,))] if n else [],
        compiler_params=_params("arbitrary", barrier_id=1 if n else None),
    )(sinks, qkv, qkv, qkv, qkv, qkv, g_attn, *shards, *gathered)
    return res[0], res[1], list(res[2:])


SMALL_ROWS = 8
ROW_LOSS, ROW_G2, ROW_G3, ROW_G4 = 0, 1, 2, 3


def _mid(mattn, mconv, x, target, g2, g3, g4, w_out, w_up, w_down, tm):
    t = x.shape[0]

    def body(ma_ref, mc_ref, x_ref, t_ref, g2_ref, g3_ref, g4_ref, wo_ref, wu_ref, wd_ref,
             act_ref, dup_ref, hn2t_ref, dmo_ref, dmix_ref, dh_ref, dmixed_ref, small_ref, up_ref):
        @pl.when(pl.program_id(0) == 0)
        def _():
            small_ref[...] = jnp.zeros_like(small_ref)

        g2, g3, g4 = g2_ref[...], g3_ref[...], g4_ref[...]
        mix_out = _mm(ma_ref[...], wo_ref[0:ATTN_W, :]) + _mm(mc_ref[...], wo_ref[ATTN_W:, :])
        r2 = _inv_rms(mix_out)
        mo_hat = mix_out * r2
        h = x_ref[...] + mo_hat * g2
        r3 = _inv_rms(h)
        h_hat = h * r3
        hn2 = (h_hat * g3).astype(BF16)
        hn2t_ref[...] = hn2.T
        for j in range(MID_CHUNKS):
            cols_j = slice(MID_CHUNK * j, MID_CHUNK * (j + 1))
            up = jnp.maximum(_mm(hn2, wu_ref[:, cols_j]), 0.0)
            up_ref[:, cols_j] = up.astype(BF16)
            act_ref[:, cols_j] = (up * up).astype(BF16)
        mlp = _mm(act_ref[...], wd_ref[...])
        r4 = _inv_rms(mlp)
        ml_hat = mlp * r4
        err = (h + ml_hat * g4) - t_ref[...]
        d_out = err * (1.0 / D_MODEL)
        d_mlp, dg4 = _rms_bwd(ml_hat, r4, g4, d_out)
        dmo = d_mlp.astype(BF16)
        dmo_ref[...] = dmo
        for j in range(MID_CHUNKS):
            cols_j = slice(MID_CHUNK * j, MID_CHUNK * (j + 1))
            dact = _mm_nt(dmo, wd_ref[cols_j, :])
            dup_ref[:, cols_j] = (dact * (2.0 * up_ref[:, cols_j].astype(F32))).astype(BF16)
        dhn2 = _mm_nt(dup_ref[...], wu_ref[...])
        dh_norm, dg3 = _rms_bwd(h_hat, r3, g3, dhn2)
        dh = d_out + dh_norm
        dh_ref[...] = dh
        d_mix, dg2 = _rms_bwd(mo_hat, r2, g2, dh)
        dmix = d_mix.astype(BF16)
        dmix_ref[...] = dmix
        dmixed_ref[...] = _mm_nt(dmix, wo_ref[...])
        small_ref[ROW_LOSS:ROW_LOSS + 1, :] += _colsum(err * err)
        small_ref[ROW_G2:ROW_G2 + 1, :] += _colsum(dg2)
        small_ref[ROW_G3:ROW_G3 + 1, :] += _colsum(dg3)
        small_ref[ROW_G4:ROW_G4 + 1, :] += _colsum(dg4)

    tile = lambda n: pl.BlockSpec((tm, n), lambda i: (i, 0))
    cols = lambda n: pl.BlockSpec((n, tm), lambda i: (0, i))
    gain = _full((1, D_MODEL))
    return pl.pallas_call(
        body, name="mid_fwd_bwd", grid=(t // tm,),
        in_specs=[tile(ATTN_W), tile(CONV_W), tile(D_MODEL), tile(D_MODEL), gain, gain, gain,
                  _resident((D_MODEL, D_MODEL)), _resident((D_MODEL, D_FF)), _resident((D_FF, D_MODEL))],
        out_specs=[tile(D_FF), tile(D_FF), cols(D_MODEL), tile(D_MODEL), tile(D_MODEL), tile(D_MODEL), tile(D_MODEL),
                   _full((SMALL_ROWS, D_MODEL))],
        out_shape=[jax.ShapeDtypeStruct((t, D_FF), BF16), jax.ShapeDtypeStruct((t, D_FF), BF16),
                   jax.ShapeDtypeStruct((D_MODEL, t), BF16), jax.ShapeDtypeStruct((t, D_MODEL), BF16),
                   jax.ShapeDtypeStruct((t, D_MODEL), BF16), jax.ShapeDtypeStruct((t, D_MODEL), F32),
                   jax.ShapeDtypeStruct((t, D_MODEL), F32), jax.ShapeDtypeStruct((SMALL_ROWS, D_MODEL), F32)],
        scratch_shapes=[pltpu.VMEM((tm, D_FF), BF16)],
        compiler_params=_params("arbitrary"),
    )(mattn, mconv, x, target, g2, g3, g4, w_out, w_up, w_down)


CHIP_FLIPS = ((1, 1), (1, 0), (0, 1))


def _block_order(dev):
    chip_masks = [4 * fx + 2 * fy for fx, fy in CHIP_FLIPS]
    masks = [m + 1 for m in chip_masks] + [1] + chip_masks + [0]
    return jnp.bitwise_xor(dev, jnp.asarray(masks, jnp.int32)).astype(jnp.int32)


def _other_chips(x, y, c):
    return [(1 - x if fx else x, 1 - y if fy else y, c) for fx, fy in CHIP_FLIPS]


def _dw_pair_sums(operands, order, which, name, barrier_id, ride=None):
    t = operands[-1].shape[0]
    n_far = len(CHIP_FLIPS)
    n_in = len(operands)
    n_ride = 0 if ride is None else 1
    out_chunk = D_MODEL // N_DEV
    if which == "up":
        rows, cols = D_MODEL, FF_CHUNK
        in_specs = [_resident((D_MODEL, t)), pl.BlockSpec((t, FF_CHUNK), lambda s, order_ref: (0, order_ref[s]))]
    elif which == "down":
        rows, cols = FF_CHUNK, D_MODEL
        in_specs = [pl.BlockSpec((t, FF_CHUNK), lambda s, order_ref: (0, order_ref[s])), _resident((t, D_MODEL))]
    else:
        rows, cols = out_chunk, D_MODEL
        half = pl.BlockSpec((t, out_chunk), lambda s, order_ref: (0, order_ref[s] % (N_DEV // 2)))
        in_specs = [half, half, _resident((t, D_MODEL))]

    def body(order_ref, *refs):
        own_ref, from_sib_ref, pair_ref = refs[n_in + n_ride:n_in + n_ride + 3]
        send_buf, land_buf, send_sems, recv_sems = refs[n_in + 2 * n_ride + 3:n_in + 2 * n_ride + 7]
        s_now = pl.program_id(0)
        x, y, c = _mesh_pos()
        sibling = (x, y, 1 - c)
        sems = (send_sems, recv_sems)

        @pl.when(s_now == 0)
        def _():
            _enter_with([sibling] + (_other_chips(x, y, c) if n_ride else []))

        if n_ride:
            _chip_exchange_beside(s_now == 0, s_now == N_DEV - 1, [refs[n_in]], [refs[n_in + 3 + n_ride]],
                                  refs[n_in + 2 * n_ride + 7:], enter=False)

        def hand_over(k):
            dst = land_buf.at[k] if k < n_far else from_sib_ref
            return _push(send_buf.at[k], dst, sems, k, sibling)

        if which == "out":
            ma_ref, mc_ref, b_ref = refs[:n_in]
            block = lax.cond(order_ref[s_now] < N_DEV // 2, lambda: _mm_tn(ma_ref[...], b_ref[...]),
                             lambda: _mm_tn(mc_ref[...], b_ref[...]))
        elif which == "down":
            block = _mm_tn(refs[0][...], refs[1][...])
        else:
            block = _mm(refs[0][...], refs[1][...])
        for k in range(n_far + 1):
            @pl.when(s_now == k)
            def _():
                send_buf[k] = block.astype(BF16)
                hand_over(k).start()

        for k in range(n_far):
            @pl.when(s_now == n_far + 1 + k)
            def _():
                hand_over(k).wait_recv()
                pair_ref[...] = (block + land_buf[k].astype(F32)).astype(BF16)

        @pl.when(s_now == N_DEV - 1)
        def _():
            own_ref[...] = block
            for k in range(n_far + 1):
                hand_over(k).wait_send()
            hand_over(n_far).wait_recv()

    rides = [] if ride is None else [ride]
    sems = lambda k: pltpu.SemaphoreType.DMA((k,))
    return pl.pallas_call(
        body, name=name,
        grid_spec=pltpu.PrefetchScalarGridSpec(
            num_scalar_prefetch=1, grid=(N_DEV,), in_specs=in_specs + [HBM_SPEC] * n_ride,
            out_specs=[pl.BlockSpec((rows, cols), lambda s, order_ref: (0, 0)), HBM_SPEC,
                       pl.BlockSpec((None, rows, cols), lambda s, order_ref: (jnp.clip(s - n_far - 1, 0, n_far - 1), 0, 0))]
            + [HBM_SPEC] * n_ride,
            scratch_shapes=[pltpu.VMEM((n_far + 1, rows, cols), BF16), pltpu.VMEM((n_far, rows, cols), BF16),
                            sems(n_far + 1), sems(n_far + 1)] + [sems(n_far), sems(n_far)] * n_ride),
        out_shape=[jax.ShapeDtypeStruct((rows, cols), F32), jax.ShapeDtypeStruct((rows, cols), BF16),
                   jax.ShapeDtypeStruct((n_far, rows, cols), BF16)]
        + [jax.ShapeDtypeStruct(r.shape, r.dtype) for r in rides],
        compiler_params=_params("arbitrary", barrier_id=barrier_id),
    )(order, *operands, *rides)


def _chip_exchange_beside(first, last, sums, outs, sems, enter=True):
    chips = _other_chips(*_mesh_pos())
    copies = [_push(sums[i].at[k], outs[i].at[k], sems, len(chips) * i + k, chip)
              for i in range(len(sums)) for k, chip in enumerate(chips)]

    @pl.when(first)
    def _():
        if enter:
            _enter_with(chips)
        for cp in copies:
            cp.start()

    @pl.when(last)
    def _():
        for cp in copies:
            cp.wait()


ROW_GCONV, ROW_CW0 = 1, 2


def _conv_bwd(dmixed, gates, g_conv, conv_w, tm):
    t = gates.shape[0]
    n = t // tm
    rev = lambda i: n - 1 - i

    def body(dm_ref, gates_ref, gprev_ref, gc_ref, cw_ref, dgates_ref, small_ref, carry_ref):
        i = pl.program_id(0)

        @pl.when(i == 0)
        def _():
            small_ref[...] = jnp.zeros_like(small_ref)
            carry_ref[...] = jnp.zeros_like(carry_ref)

        gates = gates_ref[...]
        gb, gcc, xin = gates[:, :CONV_W], gates[:, CONV_W:2 * CONV_W], gates[:, 2 * CONV_W:]
        u = gcc * xin
        gp = gprev_ref[...]
        uprev = jnp.where(rev(i) == 0, 0.0, gp[:, CONV_W:2 * CONV_W] * gp[:, 2 * CONV_W:])
        u1, u2 = _shift_rows_down(u, uprev, 1), _shift_rows_down(u, uprev, 2)
        w = cw_ref[...]
        c = _conv3(u, u1, u2, w)
        conv = gb * c
        rcv = _inv_rms(conv)
        c_hat = conv * rcv
        dconv, dgc = _rms_bwd(c_hat, rcv, gc_ref[...], dm_ref[...])
        dc = dconv * gb
        nxt = carry_ref[...]
        du = (w[2:3, :] * dc + w[1:2, :] * _shift_rows_up(dc, nxt, 1)) + w[0:1, :] * _shift_rows_up(dc, nxt, 2)
        carry_ref[...] = dc[0:8, :]
        dgates_ref[:, :CONV_W] = (dconv * c).astype(BF16)
        dgates_ref[:, CONV_W:2 * CONV_W] = (du * xin).astype(BF16)
        dgates_ref[:, 2 * CONV_W:] = (du * gcc).astype(BF16)
        small_ref[ROW_GCONV:ROW_GCONV + 1, :] += _colsum(dgc)
        small_ref[ROW_CW0:ROW_CW0 + 1, :] += _colsum(dc * u2)
        small_ref[ROW_CW0 + 1:ROW_CW0 + 2, :] += _colsum(dc * u1)
        small_ref[ROW_CW0 + 2:ROW_CW0 + 3, :] += _colsum(dc * u)

    tile = lambda w_: pl.BlockSpec((tm, w_), lambda i: (rev(i), 0))
    prev8 = pl.BlockSpec((8, GATES_W), lambda i: (jnp.maximum(rev(i) * (tm // 8) - 1, 0), 0))
    conv_half = pl.BlockSpec((tm, CONV_W), lambda i: (rev(i), ATTN_W // CONV_W))
    return pl.pallas_call(
        body, name="conv_bwd", grid=(n,),
        in_specs=[conv_half, tile(GATES_W), prev8, _full((1, CONV_W)), _full((3, CONV_W))],
        out_specs=[tile(GATES_W), _full((SMALL_ROWS, CONV_W))],
        out_shape=[jax.ShapeDtypeStruct((t, GATES_W), BF16), jax.ShapeDtypeStruct((SMALL_ROWS, CONV_W), F32)],
        scratch_shapes=[pltpu.VMEM((8, CONV_W), F32)],
        compiler_params=_params("arbitrary"),
    )(dmixed, gates, gates, g_conv, conv_w)


def _attn_bwd(qkv, dmixed, attn, g_attn, sinks, rope, sums):
    t = qkv.shape[0]
    n_steps = t // ATTN_STEP
    rev = lambda i: n_steps - 1 - i
    rc, rs1, rs2 = rope

    def body(sink_ref, q_ref, kp_ref, kc_ref, vp_ref, vc_ref, dm_ref, attn_ref, ga_ref, c_ref, s1_ref, s2_ref, sums_ref,
             dqkv_ref, dsink_ref, dgain_ref, arrived_ref, ck_ref, cv_ref, kacc_ref, vacc_ref, send_sems, recv_sems):
        i = pl.program_id(0)
        _chip_exchange_beside(i == 0, i == n_steps - 1, [sums_ref], [arrived_ref], (send_sems, recv_sems))

        @pl.when(i == 0)
        def _():
            dsink_ref[...] = jnp.zeros_like(dsink_ref)
            dgain_ref[...] = jnp.zeros_like(dgain_ref)
            ck_ref[...] = jnp.zeros_like(ck_ref)
            cv_ref[...] = jnp.zeros_like(cv_ref)

        kacc_ref[...] = jnp.zeros_like(kacc_ref)
        vacc_ref[...] = jnp.zeros_like(vacc_ref)
        a = attn_ref[...]
        ra = _inv_rms(a)
        dattn, dgain = _rms_bwd(a * ra, ra, ga_ref[...], dm_ref[...])
        dgain_ref[0:1, :] += _colsum(dgain)
        qt = (q_ref[...] * ATTN_SCALE).T
        dot = dattn.astype(BF16).T
        keys = jnp.concatenate([kp_ref[...], kc_ref[...]], axis=0)
        vals = jnp.concatenate([vp_ref[...], vc_ref[...]], axis=0)
        sink = [_group_sinks(sink_ref, g) for g in range(N_KV)]
        c, s1, s2 = c_ref[...], s1_ref[...], s2_ref[...]
        lane = lax.broadcasted_iota(jnp.int32, (1, 128), 1)
        dsink = jnp.zeros((1, 128), F32)
        masks = _attn_masks(rev(i) > 0)
        dq_parts = []
        for b in range(ATTN_STEP_BLOCKS):
            window = slice(BLOCK * b, BLOCK * (b + 2))
            valid = masks[b]
            dq_parts.append([])
            dk_parts, dv_parts = [], []
            for g in range(N_KV):
                gs = slice(HEAD_DIM * g, HEAD_DIM * (g + 1))
                kk, vv = keys[window, gs], vals[window, gs]
                qtg, dotg = _heads_side_by_side(qt, g, b), _heads_side_by_side(dot, g, b)
                probs, psink = _attn_probs(qtg, kk, sink[g], valid)
                dp = _mm(vv, dotg)
                ds_pieces, deltas = [], []
                for piece in HEAD_PIECES:
                    p_p, dp_p = probs[:, piece], dp[:, piece]
                    delta = jnp.sum(p_p * dp_p, axis=0, keepdims=True)
                    ds_pieces.append((p_p * (dp_p - delta)).astype(BF16))
                    deltas.append(delta)
                ds = jnp.concatenate(ds_pieces, axis=1)
                sink_terms = psink * jnp.concatenate(deltas, axis=1)
                for hh in range(GROUP):
                    head_sum = jnp.sum(sink_terms[:, BLOCK * hh:BLOCK * (hh + 1)])
                    dsink = dsink + jnp.where(lane == GROUP * g + hh, -head_sum, 0.0)
                dq_parts[b].append(_mm_tn(kk * ATTN_SCALE, ds))
                dk_parts.append(_mm_nt(ds, qtg))
                dv_parts.append(_mm_nt(probs.astype(BF16), dotg))
            kacc_ref[window, :] += jnp.concatenate(dk_parts, axis=1)
            vacc_ref[window, :] += jnp.concatenate(dv_parts, axis=1)
        dq = _to_token_rows(dq_parts)
        for ci in range(ATTN_W // 128):
            sl = slice(128 * ci, 128 * (ci + 1))
            dqkv_ref[:, sl] = _rope_transpose(dq[:, sl], c, s1, s2).astype(BF16)
        kacc_ref[ATTN_STEP:, :] += ck_ref[...]
        vacc_ref[ATTN_STEP:, :] += cv_ref[...]
        ck_ref[...] = kacc_ref[:BLOCK, :]
        cv_ref[...] = vacc_ref[:BLOCK, :]
        dqkv_ref[:, ATTN_W:ATTN_W + KV_W] = _rope_transpose(kacc_ref[BLOCK:, :], c, s1, s2).astype(BF16)
        dqkv_ref[:, ATTN_W + KV_W:] = vacc_ref[BLOCK:, :].astype(BF16)
        dsink_ref[0:1, :] += dsink

    blk = lambda w_: pl.BlockSpec((ATTN_STEP, w_), lambda i: (rev(i), 0))
    return pl.pallas_call(
        body, name="attn_bwd", grid=(n_steps,),
        in_specs=[pl.BlockSpec(memory_space=pltpu.SMEM)] + _qkv_specs(rev)
        + [blk(ATTN_W), blk(ATTN_W), _full((1, ATTN_W)), blk(128), blk(128), blk(128), HBM_SPEC],
        out_specs=[blk(QKV_W), _full((8, 128)), _full((SMALL_ROWS, ATTN_W)), HBM_SPEC],
        out_shape=[jax.ShapeDtypeStruct((t, QKV_W), BF16), jax.ShapeDtypeStruct((8, 128), F32),
                   jax.ShapeDtypeStruct((SMALL_ROWS, ATTN_W), F32), jax.ShapeDtypeStruct(sums.shape, sums.dtype)],
        scratch_shapes=[pltpu.VMEM((BLOCK, KV_W), F32), pltpu.VMEM((BLOCK, KV_W), F32),
                        pltpu.VMEM((ATTN_KEYS, KV_W), F32), pltpu.VMEM((ATTN_KEYS, KV_W), F32),
                        pltpu.SemaphoreType.DMA((len(CHIP_FLIPS),)), pltpu.SemaphoreType.DMA((len(CHIP_FLIPS),))],
        compiler_params=_params("arbitrary", barrier_id=6),
    )(sinks, qkv, qkv, qkv, qkv, qkv, dmixed, attn, g_attn, rc, rs1, rs2, sums)


def _grad_x_tile(dq, dg, x_hat, r, g1, w_ref, dh):
    dhn = _mm(dq, w_ref[:QKV_W, :]) + _mm(dg, w_ref[QKV_W:, :])
    dx, dg1 = _rms_bwd(x_hat, r, g1, dhn)
    return dh + dx, _colsum(dg1)


def _in_proj_bwd(dqkv, dgates, x, dh, g1, w_in, tm, out_sums):
    t = x.shape[0]
    n = t // tm
    n_cover = max(n // 2, 1)
    n_steps = n + n_cover
    n_far = len(CHIP_FLIPS)
    shard = (IN_SHARD, D_MODEL)

    def body(dq_ref, dg_ref, x_ref, dh_ref, g1_ref, w_ref, osums_ref,
             dx_ref, own_ref, sib_ref, far_ref, dg1_ref, oarrived_ref,
             acc_ref, send_buf, land_buf, pair_buf, d2d_send, d2d_recv, ici_send, ici_recv, o_send, o_recv):
        i = pl.program_id(0)
        x_pos, y_pos, c = _mesh_pos()
        my_chip = 2 * x_pos + y_pos
        sibling = (x_pos, y_pos, 1 - c)
        @pl.when(i == 0)
        def _():
            _enter_with(_sibling_and_chips(x_pos, y_pos, c))

        _chip_exchange_beside(i == 0, i == n_steps - 1, [osums_ref], [oarrived_ref], (o_send, o_recv), enter=False)

        def rows(d):
            return slice(IN_SHARD * d, IN_SHARD * (d + 1))

        def hand_over(chip):
            return _push(send_buf.at[chip], land_buf.at[chip], (d2d_send, d2d_recv), chip, sibling)

        def to_chip(chip, rel):
            return pltpu.make_async_remote_copy(
                src_ref=pair_buf.at[chip], dst_ref=far_ref.at[rel - 1], send_sem=ici_send.at[rel - 1],
                recv_sem=ici_recv.at[rel - 1], device_id=(chip // 2, chip % 2, c), device_id_type=MESH)

        @pl.when(i == 0)
        def _():
            acc_ref[...] = jnp.zeros_like(acc_ref)
            dg1_ref[...] = jnp.zeros_like(dg1_ref)

        def normed_x():
            xv = x_ref[...]
            r = _inv_rms(xv)
            return xv * r, r

        @pl.when(i < n)
        def _():
            hn = (normed_x()[0] * g1_ref[...]).astype(BF16)
            acc_ref[:QKV_W, :] += _mm_tn(dq_ref[...], hn)
            acc_ref[QKV_W:, :] += _mm_tn(dg_ref[...], hn)

        @pl.when(i == n - 1)
        def _():
            for d in range(N_DEV):
                @pl.when(d % 2 != c)
                def _():
                    send_buf[d // 2] = acc_ref[rows(d), :].astype(BF16)
                    hand_over(d // 2).start()
            for d in range(N_DEV):
                chip = d // 2

                @pl.when(d % 2 == c)
                def _():
                    hand_over(chip).wait_recv()

                    @pl.when(chip == my_chip)
                    def _():
                        own_ref[...] = acc_ref[rows(d), :]
                        sib_ref[...] = land_buf[chip]

                    @pl.when(chip != my_chip)
                    def _():
                        pair_buf[chip] = (acc_ref[rows(d), :] + land_buf[chip].astype(F32)).astype(BF16)
                        to_chip(chip, chip ^ my_chip).start()
            for chip in range(N_CHIPS):
                hand_over(chip).wait_send()

        @pl.when(i >= n)
        def _():
            x_hat, r = normed_x()
            dx_ref[...], dg1 = _grad_x_tile(dq_ref[...], dg_ref[...], x_hat, r, g1_ref[...], w_ref, dh_ref[...])
            dg1_ref[0:1, :] += dg1

        @pl.when(i == n_steps - 1)
        def _():
            for rel in range(1, n_far + 1):
                to_chip(0, rel).wait()

    both = lambda w_: pl.BlockSpec((tm, w_), lambda i: (i % n, 0))
    second = pl.BlockSpec((tm, D_MODEL), lambda i: (jnp.maximum(i - n, 0), 0))
    whole = lambda dtype: jax.ShapeDtypeStruct(shard, dtype)
    sems = lambda k: pltpu.SemaphoreType.DMA((k,))
    res = pl.pallas_call(
        body, name="in_proj_bwd", grid=(n_steps,),
        in_specs=[both(QKV_W), both(GATES_W), both(D_MODEL), second, _full((1, D_MODEL)), _resident((IN_COLS, D_MODEL)),
                  HBM_SPEC],
        out_specs=[second, _full(shard), _full(shard), HBM_SPEC, _full((SMALL_ROWS, D_MODEL)), HBM_SPEC],
        out_shape=[jax.ShapeDtypeStruct((n_cover * tm, D_MODEL), F32), whole(F32), whole(BF16),
                   jax.ShapeDtypeStruct((n_far,) + shard, BF16), jax.ShapeDtypeStruct((SMALL_ROWS, D_MODEL), F32),
                   jax.ShapeDtypeStruct(out_sums.shape, out_sums.dtype)],
        scratch_shapes=[pltpu.VMEM((IN_COLS, D_MODEL), F32), pltpu.VMEM((N_CHIPS,) + shard, BF16),
                        pltpu.VMEM((N_CHIPS,) + shard, BF16), pltpu.VMEM((N_CHIPS,) + shard, BF16),
                        sems(N_CHIPS), sems(N_CHIPS), sems(n_far), sems(n_far), sems(n_far), sems(n_far)],
        compiler_params=_params("arbitrary", barrier_id=7),
    )(dqkv, dgates, x, dh, g1, w_in, out_sums)
    return res[0], (res[1], res[2], res[3]), res[4], res[5]


def _grad_x_rest(dqkv, dgates, x, dh, g1, w_in, tm, head, dg1_rows):
    t = x.shape[0]
    first = head.shape[0] // tm
    n_rest = t // tm - first
    if n_rest == 0:
        return head, dg1_rows
    assert first <= n_rest

    def body(dq_ref, dg_ref, x_ref, dh_ref, g1_ref, w_ref, head_ref, rows_ref, gx_ref, dg1_ref, stage, sems):
        j = pl.program_id(0)

        def tile_out(step, kind):
            row0 = (step + first) * tm if kind == 0 else step * tm
            slot = 2 * kind + step % 2
            return pltpu.make_async_copy(stage.at[slot], gx_ref.at[pl.ds(pl.multiple_of(row0, tm), tm), :], sems.at[slot])

        @pl.when(j == 0)
        def _():
            dg1_ref[...] = rows_ref[...]

        @pl.when(j >= 2)
        def _():
            tile_out(j - 2, 0).wait()

        @pl.when((j >= 2) & (j - 2 < first))
        def _():
            tile_out(j - 2, 1).wait()

        @pl.when(j < first)
        def _():
            stage[2 + j % 2] = head_ref[...]
            tile_out(j, 1).start()

        xv = x_ref[...]
        r = _inv_rms(xv)
        dx, dg1 = _grad_x_tile(dq_ref[...], dg_ref[...], xv * r, r, g1_ref[...], w_ref, dh_ref[...])
        stage[j % 2] = dx
        dg1_ref[0:1, :] += dg1
        tile_out(j, 0).start()

        @pl.when(j == n_rest - 1)
        def _():
            for back in range(min(2, n_rest)):
                tile_out(j - back, 0).wait()

                @pl.when(j - back < first)
                def _():
                    tile_out(j - back, 1).wait()

    tile = lambda w_: pl.BlockSpec((tm, w_), lambda j: (j + first, 0))
    head_tile = pl.BlockSpec((tm, D_MODEL), lambda j: (jnp.minimum(j, first - 1), 0))
    return pl.pallas_call(
        body, name="grad_x_rest", grid=(n_rest,),
        in_specs=[tile(QKV_W), tile(GATES_W), tile(D_MODEL), tile(D_MODEL), _full((1, D_MODEL)),
                  _resident((IN_COLS, D_MODEL)), head_tile, _full((SMALL_ROWS, D_MODEL))],
        out_specs=[HBM_SPEC, _full((SMALL_ROWS, D_MODEL))],
        out_shape=[jax.ShapeDtypeStruct((t, D_MODEL), F32), jax.ShapeDtypeStruct((SMALL_ROWS, D_MODEL), F32)],
        scratch_shapes=[pltpu.VMEM((4, tm, D_MODEL), F32), pltpu.SemaphoreType.DMA((4,))],
        compiler_params=_params("arbitrary"),
    )(dqkv, dgates, x, dh, g1, w_in, head, dg1_rows)


def _all_gather(shards, name):
    n = len(shards)

    def body(*refs):
        _enter_with(_sibling_and_chips(*_mesh_pos()))
        start, finish = _gather_steps(refs[:n], refs[n:2 * n], *refs[2 * n:])
        start()
        finish()

    return pl.pallas_call(
        body, name=name,
        in_specs=[HBM_SPEC] * n, out_specs=[HBM_SPEC] * n,
        out_shape=[jax.ShapeDtypeStruct((N_DEV,) + s.shape, s.dtype) for s in shards],
        scratch_shapes=[pltpu.SemaphoreType.DMA((7 * n,)), pltpu.SemaphoreType.DMA((7 * n,)),
                        pltpu.SemaphoreType.DMA((n,))],
        compiler_params=_params(barrier_id=8),
    )(*shards)


def _adam_math(w, g, m, v):
    m = ADAM_B1 * m + (1.0 - ADAM_B1) * g
    v = ADAM_B2 * v + (1.0 - ADAM_B2) * (g * g)
    m_hat = m / (1.0 - ADAM_B1 ** ADAM_STEP)
    v_hat = v / (1.0 - ADAM_B2 ** ADAM_STEP)
    delta = -ADAM_LR * (m_hat / (jnp.sqrt(v_hat) + ADAM_EPS) + ADAM_WD * w)
    return delta, m, v


def _adamw_reduced(tensors, n_steps):
    n_far = len(CHIP_FLIPS)

    def body(*refs):
        ins, outs = refs[:6 * len(tensors)], refs[6 * len(tensors):]
        for k in range(len(tensors)):
            w_ref, m_ref, v_ref, own_ref, sib_ref, far_ref = ins[6 * k:6 * k + 6]
            g_ref, d_ref, nm_ref, nv_ref = outs[4 * k:4 * k + 4]
            g = own_ref[...] + sib_ref[...].astype(F32)
            for j in range(n_far):
                g = g + far_ref[j].astype(F32)
            g_ref[...] = g
            d_ref[...], nm_ref[...], nv_ref[...] = _adam_math(w_ref[...], g, m_ref[...], v_ref[...])

    in_specs, out_specs, out_shape = [], [], []
    for w, *_ in tensors:
        rows, cols = w.shape
        tile = pl.BlockSpec((rows // n_steps, cols), lambda i: (i, 0))
        in_specs += [tile] * 5 + [pl.BlockSpec((n_far, rows // n_steps, cols), lambda i: (0, i, 0))]
        out_specs += [tile] * 4
        out_shape += [jax.ShapeDtypeStruct((rows, cols), F32)] * 4
    res = pl.pallas_call(
        body, name="adamw_reduced", grid=(n_steps,), in_specs=in_specs, out_specs=out_specs, out_shape=out_shape,
        compiler_params=_params("parallel"),
    )(*[a for tensor in tensors for a in tensor])
    return [res[4 * k:4 * k + 4] for k in range(len(tensors))]


SMALL_PARAMS = ("pre_mix_norm", "post_mix_norm", "pre_mlp_norm", "post_mlp_norm", "attn_group_norm", "conv_group_norm",
                "conv_w", "attn_sinks")


SMALL_WIDTHS = (D_MODEL, CONV_W, ATTN_W, 128, D_MODEL)


def _small_tail(gathered, dev, weights, first_moments, second_moments):
    n = len(SMALL_PARAMS)
    conv_shard = CONV_W // N_DEV

    def body(dev_ref, sums_ref, *refs):
        w_refs, m_refs, v_refs = refs[:n], refs[n:2 * n], refs[2 * n:3 * n]
        loss_ref, outs = refs[3 * n], refs[3 * n + 1:]
        total = sums_ref[0]
        for d in range(1, N_DEV):
            total = total + sums_ref[d]
        starts = [sum(SMALL_WIDTHS[:i]) for i in range(len(SMALL_WIDTHS))]
        mid, conv, gain, sink, inp = (total[:, a:a + w_] for a, w_ in zip(starts, SMALL_WIDTHS))
        loss_ref[...] = (0.5 / D_MODEL) * jnp.sum(mid[ROW_LOSS:ROW_LOSS + 1, :], axis=1, keepdims=True)
        conv_rows = conv[ROW_CW0:ROW_CW0 + 3, :]
        conv_g = jnp.zeros((3, conv_shard), F32)
        for d in range(N_DEV):
            conv_g = conv_g + jnp.where(dev_ref[0] == d, conv_rows[:, conv_shard * d:conv_shard * (d + 1)], 0.0)
        grads = [inp[0:1, :], mid[ROW_G2:ROW_G2 + 1, :], mid[ROW_G3:ROW_G3 + 1, :], mid[ROW_G4:ROW_G4 + 1, :],
                 gain[0:1, :], conv[ROW_GCONV:ROW_GCONV + 1, :], conv_g, sink[0:1, :N_HEADS]]
        for i, g in enumerate(grads):
            parts = [(..., g)] if len(w_refs[i].shape) == 2 else [(r, g[r:r + 1, :]) for r in range(g.shape[0])]
            for at, g_at in parts:
                delta, new_m, new_v = _adam_math(w_refs[i][at], g_at, m_refs[i][at], v_refs[i][at])
                outs[i][at], outs[n + i][at], outs[2 * n + i][at], outs[3 * n + i][at] = g_at, delta, new_m, new_v

    params = list(weights) + list(first_moments) + list(second_moments)
    shapes = [jax.ShapeDtypeStruct(w.shape, F32) for w in weights]
    res = pl.pallas_call(
        body, name="small_tail", grid=(1,),
        in_specs=[pl.BlockSpec(memory_space=pltpu.SMEM), _full(gathered.shape)] + [_full(p.shape) for p in params],
        out_specs=[_full((1, 1))] + [_full(sh.shape) for sh in shapes] * 4,
        out_shape=[jax.ShapeDtypeStruct((1, 1), F32)] + shapes * 4,
    )(dev, gathered, *params)
    return res[0], [res[1 + k * n:1 + (k + 1) * n] for k in range(4)]


TOKEN_TILE = 512
MID_TILE = 256
MID_CHUNK = 1024
MID_CHUNKS = D_FF // MID_CHUNK
ADAM_STEPS = 2


def _local_grads(x, target, g1, w_in_shard, conv_shard, sinks, g_attn, g_conv, g2, g3, g4, shards, order):
    t = x.shape[0]
    tm = min(TOKEN_TILE, t)
    rope = _rope_tables(t)
    qkv, gates, mconv, w_in, conv_w, gathered = _in_proj_fwd(x, g1, w_in_shard, conv_shard, g_conv, rope, tm, shards,
                                                             (False, True, False))
    attn, mattn, (w_out, w_up, w_down) = _attn_fwd(qkv, sinks, g_attn, shards, gathered)
    act, dup, hn2t, dmo, dmix, dh, dmixed, small_mid = _mid(
        mattn, mconv, x, target, g2, g3, g4, w_out.reshape(D_MODEL, D_MODEL),
        w_up, w_down.reshape(D_FF, D_MODEL), min(MID_TILE, t))
    up_own, up_sib, up_sums = _dw_pair_sums((hn2t, dup), order, "up", "dw_up", 2)
    down_own, down_sib, down_sums, up_far = _dw_pair_sums((act, dmo), order, "down", "dw_down", 3, ride=up_sums)
    out_own, out_sib, out_sums = _dw_pair_sums((mattn, mconv, dmix), order, "out", "dw_out", 4)
    dgates, small_conv = _conv_bwd(dmixed, gates, g_conv, conv_w, tm)
    dqkv, dsink, dg_attn, down_far = _attn_bwd(qkv, dmixed, attn, g_attn, sinks, rope, down_sums)
    grad_x_head, dw_in, small_in, out_far = _in_proj_bwd(dqkv, dgates, x, dh, g1, w_in, tm, out_sums)
    grad_x, small_in = _grad_x_rest(dqkv, dgates, x, dh, g1, w_in, tm, grad_x_head, small_in)
    dw_out, dw_up, dw_down = (out_own, out_sib, out_far), (up_own, up_sib, up_far), (down_own, down_sib, down_far)
    return grad_x, dw_in, dw_out, dw_up, dw_down, (small_mid, small_conv, dg_attn, dsink, small_in)


def kernel(x, pre_mix_norm, w_in, conv_w, attn_sinks, attn_group_norm, conv_group_norm, w_out, post_mix_norm, pre_mlp_norm, w_up, w_down, post_mlp_norm, loss_target, m_pre_mix_norm, m_w_in, m_conv_w, m_attn_sinks, m_attn_group_norm, m_conv_group_norm, m_w_out, m_post_mix_norm, m_pre_mlp_norm, m_w_up, m_w_down, m_post_mlp_norm, v_pre_mix_norm, v_w_in, v_conv_w, v_attn_sinks, v_attn_group_norm, v_conv_group_norm, v_w_out, v_post_mix_norm, v_pre_mlp_norm, v_w_up, v_w_down, v_post_mlp_norm):
    xi, yi, ci = _mesh_pos()
    chip = 2 * xi + yi
    dev = 2 * chip + ci

    order = _block_order(dev)

    shards = [w_out[0].astype(BF16), w_up[0].astype(BF16), w_down[0].astype(BF16)]

    turned = lambda a: jnp.swapaxes(a, 1, 2)
    grad_x, dw_in, dw_out, dw_up, dw_down, smalls = _local_grads(
        x[0], loss_target[0], pre_mix_norm, turned(w_in)[0].astype(BF16), conv_w[0], attn_sinks, attn_group_norm, conv_group_norm,
        post_mix_norm, pre_mlp_norm, post_mlp_norm, shards, order)

    blocks = {"w_in": (turned(w_in), turned(m_w_in), turned(v_w_in), dw_in), "w_out": (w_out, m_w_out, v_w_out, dw_out),
              "w_up": (w_up, m_w_up, v_w_up, dw_up), "w_down": (w_down, m_w_down, v_w_down, dw_down)}
    big = {}
    for names in (("w_in", "w_out", "w_down"), ("w_up",)):
        stepped = _adamw_reduced([(w[0], m[0], v[0], *dw) for w, m, v, dw in (blocks[nm] for nm in names)], ADAM_STEPS)
        for nm, res in zip(names, stepped):
            big[nm] = [a[None] for a in res]
    big["w_in"] = [turned(a) for a in big["w_in"]]

    flat = lambda a: a.reshape(-1, a.shape[-1]) if a.ndim < 3 else a.reshape(a.shape[1], 1, a.shape[2])
    loss, small = _small_tail(
        _all_gather([jnp.concatenate(smalls, axis=1)], "gather_small")[0], dev.reshape(1).astype(jnp.int32),
        [flat(a) for a in (pre_mix_norm, post_mix_norm, pre_mlp_norm, post_mlp_norm, attn_group_norm, conv_group_norm,
                           conv_w, attn_sinks)],
        [flat(a) for a in (m_pre_mix_norm, m_post_mix_norm, m_pre_mlp_norm, m_post_mlp_norm, m_attn_group_norm,
                           m_conv_group_norm, m_conv_w, m_attn_sinks)],
        [flat(a) for a in (v_pre_mix_norm, v_post_mix_norm, v_pre_mlp_norm, v_post_mlp_norm, v_attn_group_norm,
                           v_conv_group_norm, v_conv_w, v_attn_sinks)])

    order = ("pre_mix_norm", "w_in", "conv_w", "attn_sinks", "attn_group_norm", "conv_group_norm", "w_out",
             "post_mix_norm", "pre_mlp_norm", "w_up", "w_down", "post_mlp_norm")
    shape_of = {"conv_w": conv_w.shape}
    outs = []
    for k in range(4):
        by_name = dict(zip(SMALL_PARAMS, small[k]))
        outs += [big[nm][k] if nm in big else by_name[nm].reshape(shape_of.get(nm, by_name[nm].shape)) for nm in order]
    loss = loss.reshape(())
    return (loss, grad_x[None], *outs)
```

```python
import jax
import jax.numpy as jnp
import numpy as np
from jax import lax
from jax.experimental import pallas as pl
from jax.experimental.pallas import tpu as pltpu

F32 = jnp.float32
BF16 = jnp.bfloat16

D_MODEL = 1024
HEAD_DIM = 64
ATTN_W = 512
CONV_W = 512
N_HEADS = 8
N_KV = 2
GROUP = 4
KV_W = 128
QKV_W = ATTN_W + 2 * KV_W
GATES_W = 3 * CONV_W
IN_COLS = QKV_W + GATES_W
D_FF = 4096
FF_CHUNK = 512
BLOCK = 128
ROT_HALF = 8
ROPE_THETA = 500000.0
NORM_EPS = 1e-6
NEG_INF = -1e30
ATTN_SCALE = 0.125
N_DEV = 8
N_CHIPS = 4
IN_SHARD = IN_COLS // N_DEV

ADAM_LR = 0.001
ADAM_B1 = 0.9
ADAM_B2 = 0.999
ADAM_EPS = 1e-08
ADAM_WD = 0.01
ADAM_STEP = 10

V7X_VMEM_BYTES = 64 * 1024 * 1024
VMEM_LIMIT = V7X_VMEM_BYTES - 2 * 1024 * 1024

MESH = pl.DeviceIdType.MESH
HBM_SPEC = pl.BlockSpec(memory_space=pltpu.HBM)


def _params(*sem, barrier_id=None):
    return pltpu.CompilerParams(dimension_semantics=sem or None, vmem_limit_bytes=VMEM_LIMIT, collective_id=barrier_id)


def _mm(a, b):
    return jnp.dot(a, b, preferred_element_type=F32)


def _mm_nt(a, b):
    return lax.dot_general(a, b, (((1,), (1,)), ((), ())), preferred_element_type=F32)


def _mm_tn(a, b):
    return lax.dot_general(a, b, (((0,), (0,)), ((), ())), preferred_element_type=F32)


def _inv_rms(x):
    return lax.rsqrt(jnp.mean(x * x, axis=-1, keepdims=True) + NORM_EPS)


def _rms_bwd(xhat, r, gain, dy):
    gy = dy * gain
    return r * (gy - xhat * jnp.mean(gy * xhat, axis=-1, keepdims=True)), dy * xhat


def _colsum(a):
    return jnp.sum(a, axis=0, keepdims=True)


def _full(shape):
    zeros = (0,) * len(shape)
    return pl.BlockSpec(shape, lambda *_: zeros)


def _resident(shape):
    zeros = (0,) * len(shape)
    return pl.BlockSpec(shape, lambda *_: zeros, pipeline_mode=pl.Buffered(1))


def _rope_tables(t):
    pos = np.arange(t, dtype=np.float32)
    inv_freq = (ROPE_THETA ** (-np.arange(0, 2 * ROT_HALF, 2, dtype=np.float64) / (2 * ROT_HALF))).astype(np.float32)
    ang = (pos[:, None] * inv_freq[None, :]).astype(np.float64)
    cos, sin = np.cos(ang).astype(np.float32), np.sin(ang).astype(np.float32)
    zeros8 = np.zeros((t, ROT_HALF), np.float32)
    rest = np.zeros((t, HEAD_DIM - 2 * ROT_HALF), np.float32)
    c_head = np.concatenate([cos, cos, rest + 1.0], axis=1)
    s1_head = np.concatenate([zeros8, sin, rest], axis=1)
    s2_head = np.concatenate([-sin, zeros8, rest], axis=1)
    two = lambda a: jnp.asarray(np.concatenate([a, a], axis=1))
    return two(c_head), two(s1_head), two(s2_head)


def _rope(v, c, s1, s2):
    return v * c + pltpu.roll(v, ROT_HALF, 1) * s1 + pltpu.roll(v, 128 - ROT_HALF, 1) * s2


def _rope_transpose(dv, c, s1, s2):
    return dv * c + pltpu.roll(dv * s1, 128 - ROT_HALF, 1) + pltpu.roll(dv * s2, ROT_HALF, 1)


def _shift_rows_down(u, prev, k):
    row = lax.broadcasted_iota(jnp.int32, u.shape, 0)
    out = pltpu.roll(u, k, 0)
    for r in range(k):
        out = jnp.where(row == r, prev[8 - k + r:8 - k + r + 1, :], out)
    return out


def _shift_rows_up(u, nxt, k):
    n = u.shape[0]
    row = lax.broadcasted_iota(jnp.int32, u.shape, 0)
    out = pltpu.roll(u, n - k, 0)
    for r in range(k):
        out = jnp.where(row == n - k + r, nxt[r:r + 1, :], out)
    return out


def _conv3(u, u1, u2, w):
    return (w[0:1, :] * u2 + w[1:2, :] * u1) + w[2:3, :] * u


def _mesh_pos():
    return lax.axis_index("x"), lax.axis_index("y"), lax.axis_index("c")


def _slot(ref, pos):
    dev = 4 * pos[0] + 2 * pos[1] + pos[2]
    if len(ref.shape) == 2:
        width = ref.shape[1] // N_DEV
        return ref.at[:, pl.ds(pl.multiple_of(dev * width, width), width)]
    return ref.at[dev]


def _gathered_shape(shard, by_cols):
    if by_cols:
        return jax.ShapeDtypeStruct((shard.shape[0], N_DEV * shard.shape[1]), shard.dtype)
    return jax.ShapeDtypeStruct((N_DEV,) + shard.shape, shard.dtype)


def _enter_with(peers):
    barrier = pltpu.get_barrier_semaphore()
    for peer in peers:
        pl.semaphore_signal(barrier, inc=1, device_id=peer, device_id_type=MESH)
    pl.semaphore_wait(barrier, len(peers))


def _sibling_and_chips(x, y, c):
    return [(x, y, 1 - c), (1 - x, y, c), (x, 1 - y, c), (1 - x, 1 - y, c)]


def _push(src, dst, sems, k, to):
    send_sems, recv_sems = sems
    return pltpu.make_async_remote_copy(src_ref=src, dst_ref=dst, send_sem=send_sems.at[k], recv_sem=recv_sems.at[k],
                                        device_id=to, device_id_type=MESH)


def _gather_steps(shards, outs, send_sems, recv_sems, local_sems):
    n = len(shards)
    x, y, c = _mesh_pos()
    me, sibling = (x, y, c), (x, y, 1 - c)
    chips = [(1 - x, y), (x, 1 - y), (1 - x, 1 - y)]

    def copy(i, k, block, to, src=None):
        dst = _slot(outs[i], block)
        return _push(dst if src is None else src, dst, (send_sems, recv_sems), 7 * i + k, to)

    mine = [pltpu.make_async_copy(shards[i], _slot(outs[i], me), local_sems.at[i]) for i in range(n)]
    first = []
    for i in range(n):
        first.append(copy(i, 0, me, sibling, src=shards[i]))
        first += [copy(i, 1 + j, me, (*chip, c), src=shards[i]) for j, chip in enumerate(chips)]

    def start():
        for cp in mine + first:
            cp.start()

    def finish():
        passed = []
        for j, chip in enumerate(chips):
            for i in range(n):
                copy(i, 1 + j, (*chip, c), me).wait_recv()
                cp = copy(i, 4 + j, (*chip, c), sibling)
                cp.start()
                passed.append(cp)
        for i in range(n):
            copy(i, 0, sibling, me).wait_recv()
            for j, chip in enumerate(chips):
                copy(i, 4 + j, (*chip, 1 - c), me).wait_recv()
        for cp in first + passed:
            cp.wait_send()
        for cp in mine:
            cp.wait()

    return start, finish


def _gather_near(first, last, shards, outs, sems, local_sems):
    x, y, c = _mesh_pos()
    me, peers = (x, y, c), [(x, y, 1 - c), (1 - x, y, c), (x, 1 - y, c)]
    n = len(shards)
    local = [pltpu.make_async_copy(shards[i], _slot(outs[i], me), local_sems.at[i]) for i in range(n)]
    sends = [_push(shards[i], _slot(outs[i], me), sems, 3 * i + k, peers[k]) for i in range(n) for k in range(3)]
    arrivals = [_push(shards[i], _slot(outs[i], peers[k]), sems, 3 * i + k, peers[k]) for i in range(n) for k in range(3)]

    def start():
        for cp in local + sends:
            cp.start()

    if first is not None:
        pl.when(first)(start)

    @pl.when(last)
    def _():
        for cp in sends:
            cp.wait_send()
        for cp in arrivals:
            cp.wait_recv()
        for cp in local:
            cp.wait()

    return start


def _relay_route(x, y, c):
    south = c == 0
    via = (jnp.where(south, 1 - x, x), jnp.where(south, y, 1 - y))
    to = (jnp.where(south, x, 1 - x), jnp.where(south, 1 - y, y))
    return via, to


def _gather_far(first, middle, last, shards, ins, outs, sems):
    x, y, c = _mesh_pos()
    sibling = (x, y, 1 - c)
    chips = [(1 - x, y), (x, 1 - y), (1 - x, 1 - y)]
    via, to = _relay_route(x, y, c)
    n = len(shards)
    diag_send = [_push(_slot(ins[i], (*via, c)), _slot(outs[i], (*via, c)), sems, 4 * i, (*to, c)) for i in range(n)]
    diag_arrival = [_push(shards[i], _slot(outs[i], (*chips[2], c)), sems, 4 * i, (*to, c)) for i in range(n)]
    passed = [[_push(_slot(ins[i], (*chips[j], c)), _slot(outs[i], (*chips[j], c)), sems, 4 * i + 1 + j, sibling)
               for i in range(n)] for j in range(3)]
    from_sibling = [_push(shards[i], _slot(outs[i], (*chips[j], 1 - c)), sems, 4 * i + 1 + j, sibling)
                    for i in range(n) for j in range(3)]

    @pl.when(first)
    def _():
        for cp in diag_send + passed[0] + passed[1]:
            cp.start()

    @pl.when(middle)
    def _():
        for cp in diag_arrival:
            cp.wait_recv()
        for cp in passed[2]:
            cp.start()

    @pl.when(last)
    def _():
        for cp in from_sibling:
            cp.wait_recv()
        for cp in diag_send + passed[0] + passed[1] + passed[2]:
            cp.wait_send()


def _in_proj_fwd(x, g1, w_in, conv_w, g_conv, rope, tm, shards, by_cols):
    t = x.shape[0]
    rc, rs1, rs2 = rope
    n = len(shards)
    n_tiles = t // tm

    def body(*refs):
        x_ref, g1_ref, w_ref, cw_ref, gc_ref, c_ref, s1_ref, s2_ref = refs[:8]
        shard_refs = refs[8:8 + n]
        qkv_ref, gates_ref, mconv_ref, w_full_ref, cw_full_ref = refs[8 + n:13 + n]
        gathered = refs[13 + n:13 + 2 * n]
        carry_ref, w_land, cw_land, hn_ref = refs[13 + 2 * n:17 + 2 * n]
        now_sems = refs[17 + 2 * n:20 + 2 * n]
        step = pl.program_id(0)
        start_later_weights = _gather_near(None, step == 2 * n_tiles - 1, shard_refs, gathered,
                                           refs[20 + 2 * n:22 + 2 * n], refs[22 + 2 * n]) if n else None
        start_w_in, finish_w_in = _gather_steps([w_ref, cw_ref], [w_land, cw_land], *now_sems)

        @pl.when(step == 0)
        def _():
            carry_ref[...] = jnp.zeros_like(carry_ref)
            _enter_with(_sibling_and_chips(*_mesh_pos()))
            start_w_in()
            if start_later_weights is not None:
                start_later_weights()

        @pl.when(step < n_tiles)
        def _():
            xv = x_ref[...]
            hn_ref[step] = ((xv * _inv_rms(xv)) * g1_ref[...]).astype(BF16)

        @pl.when(step == n_tiles)
        def _():
            finish_w_in()
            conv_shard = CONV_W // N_DEV
            for d in range(N_DEV):
                w_full_ref[IN_SHARD * d:IN_SHARD * (d + 1), :] = w_land[d]
                cw_full_ref[:, conv_shard * d:conv_shard * (d + 1)] = cw_land[d]

        @pl.when(step >= n_tiles)
        def _():
            proj = _mm_nt(hn_ref[step - n_tiles], w_full_ref[...])
            c, s1, s2 = c_ref[...], s1_ref[...], s2_ref[...]
            for ci in range((ATTN_W + KV_W) // 128):
                sl = slice(128 * ci, 128 * (ci + 1))
                qkv_ref[:, sl] = _rope(proj[:, sl], c, s1, s2).astype(BF16)
            qkv_ref[:, ATTN_W + KV_W:QKV_W] = proj[:, ATTN_W + KV_W:QKV_W].astype(BF16)
            gates = proj[:, QKV_W:]
            gates_ref[...] = gates
            gb, gcc, xin = gates[:, :CONV_W], gates[:, CONV_W:2 * CONV_W], gates[:, 2 * CONV_W:]
            u = gcc * xin
            prev = carry_ref[...]
            conv = gb * _conv3(u, _shift_rows_down(u, prev, 1), _shift_rows_down(u, prev, 2), cw_full_ref[...])
            carry_ref[...] = u[tm - 8:tm, :]
            mconv_ref[...] = ((conv * _inv_rms(conv)) * gc_ref[...]).astype(BF16)

    first_pass = pl.BlockSpec((tm, D_MODEL), lambda i: (jnp.minimum(i, n_tiles - 1), 0))
    tile = lambda w_: pl.BlockSpec((tm, w_), lambda i: (jnp.maximum(i - n_tiles, 0), 0))
    sems = lambda k: pltpu.SemaphoreType.DMA((k,))
    res = pl.pallas_call(
        body, name="in_proj_fwd", grid=(2 * n_tiles,),
        in_specs=[first_pass, _full((1, D_MODEL)), HBM_SPEC, HBM_SPEC, _full((1, CONV_W)), tile(128), tile(128),
                  tile(128)] + [HBM_SPEC] * n,
        out_specs=[tile(QKV_W), tile(GATES_W), tile(CONV_W), _full((IN_COLS, D_MODEL)), _full((3, CONV_W))]
        + [HBM_SPEC] * n,
        out_shape=[jax.ShapeDtypeStruct((t, QKV_W), BF16), jax.ShapeDtypeStruct((t, GATES_W), F32),
                   jax.ShapeDtypeStruct((t, CONV_W), BF16), jax.ShapeDtypeStruct((IN_COLS, D_MODEL), BF16),
                   jax.ShapeDtypeStruct((3, CONV_W), F32)]
        + [_gathered_shape(s, cols) for s, cols in zip(shards, by_cols)],
        scratch_shapes=[pltpu.VMEM((8, CONV_W), F32), pltpu.VMEM((N_DEV,) + w_in.shape, BF16),
                        pltpu.VMEM((N_DEV,) + conv_w.shape, F32), pltpu.VMEM((n_tiles, tm, D_MODEL), BF16),
                        sems(14), sems(14), sems(2)]
        + ([sems(3 * n), sems(3 * n), sems(n)] if n else []),
        compiler_params=_params("arbitrary", barrier_id=0),
    )(x, g1, w_in, conv_w, g_conv, rc, rs1, rs2, *shards)
    return res[0], res[1], res[2], res[3], res[4], list(res[5:])


GROUP_COLS = GROUP * BLOCK
ATTN_STEP_BLOCKS = 8


def _attn_masks(has_prev):
    key = lax.broadcasted_iota(jnp.int32, (2 * BLOCK, GROUP_COLS), 0)
    query = lax.broadcasted_iota(jnp.int32, (2 * BLOCK, GROUP_COLS), 1) & (BLOCK - 1)
    band = (key > query) & (key <= query + BLOCK)
    return [band & ((key >= BLOCK) | has_prev)] + [band] * (ATTN_STEP_BLOCKS - 1)


def _heads_side_by_side(at, g, b):
    heads = [at[HEAD_DIM * (GROUP * g + hh):HEAD_DIM * (GROUP * g + hh + 1), BLOCK * b:BLOCK * (b + 1)] for hh in range(GROUP)]
    return jnp.concatenate(heads, axis=1)


def _to_token_rows(parts):
    rows = [jnp.concatenate([parts[b][g][:, BLOCK * hh:BLOCK * (hh + 1)] for b in range(ATTN_STEP_BLOCKS)], axis=1)
            for g in range(N_KV) for hh in range(GROUP)]
    return jnp.concatenate(rows, axis=0).T


def _group_sinks(sink_ref, g):
    head = lax.broadcasted_iota(jnp.int32, (1, GROUP_COLS), 1) // BLOCK
    out = jnp.full((1, GROUP_COLS), sink_ref[0, GROUP * g], F32)
    for hh in range(1, GROUP):
        out = jnp.where(head == hh, sink_ref[0, GROUP * g + hh], out)
    return out


def _attn_probs(qt, kk, sink, valid):
    s = jnp.where(valid, _mm(kk, qt), NEG_INF)
    m = jnp.maximum(jnp.max(s, axis=0, keepdims=True), sink)
    p = jnp.exp(s - m)
    psink = jnp.exp(sink - m)
    inv_l = 1.0 / (jnp.sum(p, axis=0, keepdims=True) + psink)
    return p * inv_l, psink * inv_l


ATTN_STEP = ATTN_STEP_BLOCKS * BLOCK
ATTN_KEYS = ATTN_STEP + BLOCK


def _qkv_specs(order):
    prev = lambda i: jnp.maximum(ATTN_STEP_BLOCKS * order(i) - 1, 0)
    kcol, vcol = ATTN_W // KV_W, ATTN_W // KV_W + 1
    return [pl.BlockSpec((ATTN_STEP, ATTN_W), lambda i: (order(i), 0)),
            pl.BlockSpec((BLOCK, KV_W), lambda i: (prev(i), kcol)), pl.BlockSpec((ATTN_STEP, KV_W), lambda i: (order(i), kcol)),
            pl.BlockSpec((BLOCK, KV_W), lambda i: (prev(i), vcol)), pl.BlockSpec((ATTN_STEP, KV_W), lambda i: (order(i), vcol))]


def _attn_fwd(qkv, sinks, g_attn, shards, gathered):
    t = qkv.shape[0]
    n = len(shards)

    def body(*refs):
        sink_ref, q_ref, kp_ref, kc_ref, vp_ref, vc_ref, ga_ref = refs[:7]
        attn_ref, mattn_ref = refs[7 + 2 * n:9 + 2 * n]
        step = pl.program_id(0)
        if n:
            @pl.when(step == 0)
            def _():
                x, y, c = _mesh_pos()
                _enter_with([(x, y, 1 - c), (*_relay_route(x, y, c)[1], c)])

            n_steps = t // ATTN_STEP
            _gather_far(step == 0, step == n_steps // 2, step == n_steps - 1, refs[7:7 + n], refs[7 + n:7 + 2 * n],
                        refs[9 + 2 * n:9 + 3 * n], refs[9 + 3 * n:11 + 3 * n])
        qt = (q_ref[...] * ATTN_SCALE).T
        keys = jnp.concatenate([kp_ref[...], kc_ref[...]], axis=0)
        vals = jnp.concatenate([vp_ref[...], vc_ref[...]], axis=0)
        sink = [_group_sinks(sink_ref, g) for g in range(N_KV)]
        masks = _attn_masks(step > 0)
        parts = []
        for b in range(ATTN_STEP_BLOCKS):
            window = slice(BLOCK * b, BLOCK * (b + 2))
            valid = masks[b]
            parts.append([])
            for g in range(N_KV):
                gs = slice(HEAD_DIM * g, HEAD_DIM * (g + 1))
                probs, _ = _attn_probs(_heads_side_by_side(qt, g, b), keys[window, gs], sink[g], valid)
                parts[b].append(_mm_tn(vals[window, gs], probs.astype(BF16)))
        attn = _to_token_rows(parts)
        attn_ref[...] = attn
        mattn_ref[...] = ((attn * _inv_rms(attn)) * ga_ref[...]).astype(BF16)

    blk = pl.BlockSpec((ATTN_STEP, ATTN_W), lambda j: (j, 0))
    res = pl.pallas_call(
        body, name="attn_fwd", grid=(t // ATTN_STEP,),
        in_specs=[pl.BlockSpec(memory_space=pltpu.SMEM)] + _qkv_specs(lambda j: j) + [_full((1, ATTN_W))]
        + [HBM_SPEC] * (2 * n),
        out_specs=[blk, blk] + [HBM_SPEC] * n,
        out_shape=[jax.ShapeDtypeStruct((t, ATTN_W), F32), jax.ShapeDtypeStruct((t, ATTN_W), BF16)]
        + [jax.ShapeDtypeStruct(g.shape, g.dtype) for g in gathered],
        input_output_aliases={7 + n + i: 2 + i for i in range(n)},
        scratch_shapes=[pltpu.SemaphoreType.DMA((4 * n,)), pltpu.SemaphoreType.DMA((4 * n,))] if n else [],
        compiler_params=_params("arbitrary", barrier_id=1 if n else None),
    )(sinks, qkv, qkv, qkv, qkv, qkv, g_attn, *shards, *gathered)
    return res[0], res[1], list(res[2:])


SMALL_ROWS = 8
ROW_LOSS, ROW_G2, ROW_G3, ROW_G4 = 0, 1, 2, 3


def _mid(mattn, mconv, x, target, g2, g3, g4, w_out, w_up, w_down, tm):
    t = x.shape[0]

    def body(ma_ref, mc_ref, x_ref, t_ref, g2_ref, g3_ref, g4_ref, wo_ref, wu_ref, wd_ref,
             act_ref, dup_ref, hn2t_ref, dmo_ref, dmix_ref, dh_ref, dmixed_ref, small_ref, up_ref):
        @pl.when(pl.program_id(0) == 0)
        def _():
            small_ref[...] = jnp.zeros_like(small_ref)

        g2, g3, g4 = g2_ref[...], g3_ref[...], g4_ref[...]
        mix_out = _mm(ma_ref[...], wo_ref[0:ATTN_W, :]) + _mm(mc_ref[...], wo_ref[ATTN_W:, :])
        r2 = _inv_rms(mix_out)
        mo_hat = mix_out * r2
        h = x_ref[...] + mo_hat * g2
        r3 = _inv_rms(h)
        h_hat = h * r3
        hn2 = (h_hat * g3).astype(BF16)
        hn2t_ref[...] = hn2.T
        for j in range(MID_CHUNKS):
            cols_j = slice(MID_CHUNK * j, MID_CHUNK * (j + 1))
            up = jnp.maximum(_mm(hn2, wu_ref[:, cols_j]), 0.0)
            up_ref[:, cols_j] = up.astype(BF16)
            act_ref[:, cols_j] = (up * up).astype(BF16)
        mlp = _mm(act_ref[...], wd_ref[...])
        r4 = _inv_rms(mlp)
        ml_hat = mlp * r4
        err = (h + ml_hat * g4) - t_ref[...]
        d_out = err * (1.0 / D_MODEL)
        d_mlp, dg4 = _rms_bwd(ml_hat, r4, g4, d_out)
        dmo = d_mlp.astype(BF16)
        dmo_ref[...] = dmo
        for j in range(MID_CHUNKS):
            cols_j = slice(MID_CHUNK * j, MID_CHUNK * (j + 1))
            dact = _mm_nt(dmo, wd_ref[cols_j, :])
            dup_ref[:, cols_j] = (dact * (2.0 * up_ref[:, cols_j].astype(F32))).astype(BF16)
        dhn2 = _mm_nt(dup_ref[...], wu_ref[...])
        dh_norm, dg3 = _rms_bwd(h_hat, r3, g3, dhn2)
        dh = d_out + dh_norm
        dh_ref[...] = dh
        d_mix, dg2 = _rms_bwd(mo_hat, r2, g2, dh)
        dmix = d_mix.astype(BF16)
        dmix_ref[...] = dmix
        dmixed_ref[...] = _mm_nt(dmix, wo_ref[...])
        small_ref[ROW_LOSS:ROW_LOSS + 1, :] += _colsum(err * err)
        small_ref[ROW_G2:ROW_G2 + 1, :] += _colsum(dg2)
        small_ref[ROW_G3:ROW_G3 + 1, :] += _colsum(dg3)
        small_ref[ROW_G4:ROW_G4 + 1, :] += _colsum(dg4)

    tile = lambda n: pl.BlockSpec((tm, n), lambda i: (i, 0))
    cols = lambda n: pl.BlockSpec((n, tm), lambda i: (0, i))
    gain = _full((1, D_MODEL))
    return pl.pallas_call(
        body, name="mid_fwd_bwd", grid=(t // tm,),
        in_specs=[tile(ATTN_W), tile(CONV_W), tile(D_MODEL), tile(D_MODEL), gain, gain, gain,
                  _resident((D_MODEL, D_MODEL)), _resident((D_MODEL, D_FF)), _resident((D_FF, D_MODEL))],
        out_specs=[tile(D_FF), tile(D_FF), cols(D_MODEL), tile(D_MODEL), tile(D_MODEL), tile(D_MODEL), tile(D_MODEL),
                   _full((SMALL_ROWS, D_MODEL))],
        out_shape=[jax.ShapeDtypeStruct((t, D_FF), BF16), jax.ShapeDtypeStruct((t, D_FF), BF16),
                   jax.ShapeDtypeStruct((D_MODEL, t), BF16), jax.ShapeDtypeStruct((t, D_MODEL), BF16),
                   jax.ShapeDtypeStruct((t, D_MODEL), BF16), jax.ShapeDtypeStruct((t, D_MODEL), F32),
                   jax.ShapeDtypeStruct((t, D_MODEL), F32), jax.ShapeDtypeStruct((SMALL_ROWS, D_MODEL), F32)],
        scratch_shapes=[pltpu.VMEM((tm, D_FF), BF16)],
        compiler_params=_params("arbitrary"),
    )(mattn, mconv, x, target, g2, g3, g4, w_out, w_up, w_down)


CHIP_FLIPS = ((1, 1), (1, 0), (0, 1))


def _block_order(dev):
    chip_masks = [4 * fx + 2 * fy for fx, fy in CHIP_FLIPS]
    masks = [m + 1 for m in chip_masks] + [1] + chip_masks + [0]
    return jnp.bitwise_xor(dev, jnp.asarray(masks, jnp.int32)).astype(jnp.int32)


def _other_chips(x, y, c):
    return [(1 - x if fx else x, 1 - y if fy else y, c) for fx, fy in CHIP_FLIPS]


def _dw_pair_sums(operands, order, which, name, barrier_id, ride=None):
    t = operands[-1].shape[0]
    n_far = len(CHIP_FLIPS)
    n_in = len(operands)
    n_ride = 0 if ride is None else 1
    out_chunk = D_MODEL // N_DEV
    if which == "up":
        rows, cols = D_MODEL, FF_CHUNK
        in_specs = [_resident((D_MODEL, t)), pl.BlockSpec((t, FF_CHUNK), lambda s, order_ref: (0, order_ref[s]))]
    elif which == "down":
        rows, cols = FF_CHUNK, D_MODEL
        in_specs = [pl.BlockSpec((t, FF_CHUNK), lambda s, order_ref: (0, order_ref[s])), _resident((t, D_MODEL))]
    else:
        rows, cols = out_chunk, D_MODEL
        half = pl.BlockSpec((t, out_chunk), lambda s, order_ref: (0, order_ref[s] % (N_DEV // 2)))
        in_specs = [half, half, _resident((t, D_MODEL))]

    def body(order_ref, *refs):
        own_ref, from_sib_ref, pair_ref = refs[n_in + n_ride:n_in + n_ride + 3]
        send_buf, land_buf, send_sems, recv_sems = refs[n_in + 2 * n_ride + 3:n_in + 2 * n_ride + 7]
        s_now = pl.program_id(0)
        x, y, c = _mesh_pos()
        sibling = (x, y, 1 - c)
        sems = (send_sems, recv_sems)

        @pl.when(s_now == 0)
        def _():
            _enter_with([sibling] + (_other_chips(x, y, c) if n_ride else []))

        if n_ride:
            _chip_exchange_beside(s_now == 0, s_now == N_DEV - 1, [refs[n_in]], [refs[n_in + 3 + n_ride]],
                                  refs[n_in + 2 * n_ride + 7:], enter=False)

        def hand_over(k):
            dst = land_buf.at[k] if k < n_far else from_sib_ref
            return _push(send_buf.at[k], dst, sems, k, sibling)

        if which == "out":
            ma_ref, mc_ref, b_ref = refs[:n_in]
            block = lax.cond(order_ref[s_now] < N_DEV // 2, lambda: _mm_tn(ma_ref[...], b_ref[...]),
                             lambda: _mm_tn(mc_ref[...], b_ref[...]))
        elif which == "down":
            block = _mm_tn(refs[0][...], refs[1][...])
        else:
            block = _mm(refs[0][...], refs[1][...])
        for k in range(n_far + 1):
            @pl.when(s_now == k)
            def _():
                send_buf[k] = block.astype(BF16)
                hand_over(k).start()

        for k in range(n_far):
            @pl.when(s_now == n_far + 1 + k)
            def _():
                hand_over(k).wait_recv()
                pair_ref[...] = (block + land_buf[k].astype(F32)).astype(BF16)

        @pl.when(s_now == N_DEV - 1)
        def _():
            own_ref[...] = block
            for k in range(n_far + 1):
                hand_over(k).wait_send()
            hand_over(n_far).wait_recv()

    rides = [] if ride is None else [ride]
    sems = lambda k: pltpu.SemaphoreType.DMA((k,))
    return pl.pallas_call(
        body, name=name,
        grid_spec=pltpu.PrefetchScalarGridSpec(
            num_scalar_prefetch=1, grid=(N_DEV,), in_specs=in_specs + [HBM_SPEC] * n_ride,
            out_specs=[pl.BlockSpec((rows, cols), lambda s, order_ref: (0, 0)), HBM_SPEC,
                       pl.BlockSpec((None, rows, cols), lambda s, order_ref: (jnp.clip(s - n_far - 1, 0, n_far - 1), 0, 0))]
            + [HBM_SPEC] * n_ride,
            scratch_shapes=[pltpu.VMEM((n_far + 1, rows, cols), BF16), pltpu.VMEM((n_far, rows, cols), BF16),
                            sems(n_far + 1), sems(n_far + 1)] + [sems(n_far), sems(n_far)] * n_ride),
        out_shape=[jax.ShapeDtypeStruct((rows, cols), F32), jax.ShapeDtypeStruct((rows, cols), BF16),
                   jax.ShapeDtypeStruct((n_far, rows, cols), BF16)]
        + [jax.ShapeDtypeStruct(r.shape, r.dtype) for r in rides],
        compiler_params=_params("arbitrary", barrier_id=barrier_id),
    )(order, *operands, *rides)


def _chip_exchange_beside(first, last, sums, outs, sems, enter=True):
    chips = _other_chips(*_mesh_pos())
    copies = [_push(sums[i].at[k], outs[i].at[k], sems, len(chips) * i + k, chip)
              for i in range(len(sums)) for k, chip in enumerate(chips)]

    @pl.when(first)
    def _():
        if enter:
            _enter_with(chips)
        for cp in copies:
            cp.start()

    @pl.when(last)
    def _():
        for cp in copies:
            cp.wait()


ROW_GCONV, ROW_CW0 = 1, 2


def _conv_bwd(dmixed, gates, g_conv, conv_w, tm):
    t = gates.shape[0]
    n = t // tm
    rev = lambda i: n - 1 - i

    def body(dm_ref, gates_ref, gprev_ref, gc_ref, cw_ref, dgates_ref, small_ref, carry_ref):
        i = pl.program_id(0)

        @pl.when(i == 0)
        def _():
            small_ref[...] = jnp.zeros_like(small_ref)
            carry_ref[...] = jnp.zeros_like(carry_ref)

        gates = gates_ref[...]
        gb, gcc, xin = gates[:, :CONV_W], gates[:, CONV_W:2 * CONV_W], gates[:, 2 * CONV_W:]
        u = gcc * xin
        gp = gprev_ref[...]
        uprev = jnp.where(rev(i) == 0, 0.0, gp[:, CONV_W:2 * CONV_W] * gp[:, 2 * CONV_W:])
        u1, u2 = _shift_rows_down(u, uprev, 1), _shift_rows_down(u, uprev, 2)
        w = cw_ref[...]
        c = _conv3(u, u1, u2, w)
        conv = gb * c
        rcv = _inv_rms(conv)
        c_hat = conv * rcv
        dconv, dgc = _rms_bwd(c_hat, rcv, gc_ref[...], dm_ref[...])
        dc = dconv * gb
        nxt = carry_ref[...]
        du = (w[2:3, :] * dc + w[1:2, :] * _shift_rows_up(dc, nxt, 1)) + w[0:1, :] * _shift_rows_up(dc, nxt, 2)
        carry_ref[...] = dc[0:8, :]
        dgates_ref[:, :CONV_W] = (dconv * c).astype(BF16)
        dgates_ref[:, CONV_W:2 * CONV_W] = (du * xin).astype(BF16)
        dgates_ref[:, 2 * CONV_W:] = (du * gcc).astype(BF16)
        small_ref[ROW_GCONV:ROW_GCONV + 1, :] += _colsum(dgc)
        small_ref[ROW_CW0:ROW_CW0 + 1, :] += _colsum(dc * u2)
        small_ref[ROW_CW0 + 1:ROW_CW0 + 2, :] += _colsum(dc * u1)
        small_ref[ROW_CW0 + 2:ROW_CW0 + 3, :] += _colsum(dc * u)

    tile = lambda w_: pl.BlockSpec((tm, w_), lambda i: (rev(i), 0))
    prev8 = pl.BlockSpec((8, GATES_W), lambda i: (jnp.maximum(rev(i) * (tm // 8) - 1, 0), 0))
    conv_half = pl.BlockSpec((tm, CONV_W), lambda i: (rev(i), ATTN_W // CONV_W))
    return pl.pallas_call(
        body, name="conv_bwd", grid=(n,),
        in_specs=[conv_half, tile(GATES_W), prev8, _full((1, CONV_W)), _full((3, CONV_W))],
        out_specs=[tile(GATES_W), _full((SMALL_ROWS, CONV_W))],
        out_shape=[jax.ShapeDtypeStruct((t, GATES_W), BF16), jax.ShapeDtypeStruct((SMALL_ROWS, CONV_W), F32)],
        scratch_shapes=[pltpu.VMEM((8, CONV_W), F32)],
        compiler_params=_params("arbitrary"),
    )(dmixed, gates, gates, g_conv, conv_w)


def _attn_bwd(qkv, dmixed, attn, g_attn, sinks, rope, sums):
    t = qkv.shape[0]
    n_steps = t // ATTN_STEP
    rev = lambda i: n_steps - 1 - i
    rc, rs1, rs2 = rope

    def body(sink_ref, q_ref, kp_ref, kc_ref, vp_ref, vc_ref, dm_ref, attn_ref, ga_ref, c_ref, s1_ref, s2_ref, sums_ref,
             dqkv_ref, dsink_ref, dgain_ref, arrived_ref, ck_ref, cv_ref, kacc_ref, vacc_ref, send_sems, recv_sems):
        i = pl.program_id(0)
        _chip_exchange_beside(i == 0, i == n_steps - 1, [sums_ref], [arrived_ref], (send_sems, recv_sems))

        @pl.when(i == 0)
        def _():
            dsink_ref[...] = jnp.zeros_like(dsink_ref)
            dgain_ref[...] = jnp.zeros_like(dgain_ref)
            ck_ref[...] = jnp.zeros_like(ck_ref)
            cv_ref[...] = jnp.zeros_like(cv_ref)

        kacc_ref[...] = jnp.zeros_like(kacc_ref)
        vacc_ref[...] = jnp.zeros_like(vacc_ref)
        a = attn_ref[...]
        ra = _inv_rms(a)
        dattn, dgain = _rms_bwd(a * ra, ra, ga_ref[...], dm_ref[...])
        dgain_ref[0:1, :] += _colsum(dgain)
        qt = (q_ref[...] * ATTN_SCALE).T
        dot = dattn.astype(BF16).T
        keys = jnp.concatenate([kp_ref[...], kc_ref[...]], axis=0)
        vals = jnp.concatenate([vp_ref[...], vc_ref[...]], axis=0)
        sink = [_group_sinks(sink_ref, g) for g in range(N_KV)]
        c, s1, s2 = c_ref[...], s1_ref[...], s2_ref[...]
        lane = lax.broadcasted_iota(jnp.int32, (1, 128), 1)
        dsink = jnp.zeros((1, 128), F32)
        masks = _attn_masks(rev(i) > 0)
        dq_parts = []
        for b in range(ATTN_STEP_BLOCKS):
            window = slice(BLOCK * b, BLOCK * (b + 2))
            valid = masks[b]
            dq_parts.append([])
            dk_parts, dv_parts = [], []
            for g in range(N_KV):
                gs = slice(HEAD_DIM * g, HEAD_DIM * (g + 1))
                kk, vv = keys[window, gs], vals[window, gs]
                qtg, dotg = _heads_side_by_side(qt, g, b), _heads_side_by_side(dot, g, b)
                probs, psink = _attn_probs(qtg, kk, sink[g], valid)
                dp = _mm(vv, dotg)
                delta = jnp.sum(probs * dp, axis=0, keepdims=True)
                ds = (probs * (dp - delta)).astype(BF16)
                sink_terms = psink * delta
                for hh in range(GROUP):
                    head_sum = jnp.sum(sink_terms[:, BLOCK * hh:BLOCK * (hh + 1)])
                    dsink = dsink + jnp.where(lane == GROUP * g + hh, -head_sum, 0.0)
                dq_parts[b].append(_mm_tn(kk * ATTN_SCALE, ds))
                dk_parts.append(_mm_nt(ds, qtg))
                dv_parts.append(_mm_nt(probs.astype(BF16), dotg))
            kacc_ref[window, :] += jnp.concatenate(dk_parts, axis=1)
            vacc_ref[window, :] += jnp.concatenate(dv_parts, axis=1)
        dq = _to_token_rows(dq_parts)
        for ci in range(ATTN_W // 128):
            sl = slice(128 * ci, 128 * (ci + 1))
            dqkv_ref[:, sl] = _rope_transpose(dq[:, sl], c, s1, s2).astype(BF16)
        kacc_ref[ATTN_STEP:, :] += ck_ref[...]
        vacc_ref[ATTN_STEP:, :] += cv_ref[...]
        ck_ref[...] = kacc_ref[:BLOCK, :]
        cv_ref[...] = vacc_ref[:BLOCK, :]
        dqkv_ref[:, ATTN_W:ATTN_W + KV_W] = _rope_transpose(kacc_ref[BLOCK:, :], c, s1, s2).astype(BF16)
        dqkv_ref[:, ATTN_W + KV_W:] = vacc_ref[BLOCK:, :].astype(BF16)
        dsink_ref[0:1, :] += dsink

    blk = lambda w_: pl.BlockSpec((ATTN_STEP, w_), lambda i: (rev(i), 0))
    return pl.pallas_call(
        body, name="attn_bwd", grid=(n_steps,),
        in_specs=[pl.BlockSpec(memory_space=pltpu.SMEM)] + _qkv_specs(rev)
        + [blk(ATTN_W), blk(ATTN_W), _full((1, ATTN_W)), blk(128), blk(128), blk(128), HBM_SPEC],
        out_specs=[blk(QKV_W), _full((8, 128)), _full((SMALL_ROWS, ATTN_W)), HBM_SPEC],
        out_shape=[jax.ShapeDtypeStruct((t, QKV_W), BF16), jax.ShapeDtypeStruct((8, 128), F32),
                   jax.ShapeDtypeStruct((SMALL_ROWS, ATTN_W), F32), jax.ShapeDtypeStruct(sums.shape, sums.dtype)],
        scratch_shapes=[pltpu.VMEM((BLOCK, KV_W), F32), pltpu.VMEM((BLOCK, KV_W), F32),
                        pltpu.VMEM((ATTN_KEYS, KV_W), F32), pltpu.VMEM((ATTN_KEYS, KV_W), F32),
                        pltpu.SemaphoreType.DMA((len(CHIP_FLIPS),)), pltpu.SemaphoreType.DMA((len(CHIP_FLIPS),))],
        compiler_params=_params("arbitrary", barrier_id=6),
    )(sinks, qkv, qkv, qkv, qkv, qkv, dmixed, attn, g_attn, rc, rs1, rs2, sums)


def _grad_x_tile(dq, dg, x_hat, r, g1, w_ref, dh):
    dhn = _mm(dq, w_ref[:QKV_W, :]) + _mm(dg, w_ref[QKV_W:, :])
    dx, dg1 = _rms_bwd(x_hat, r, g1, dhn)
    return dh + dx, _colsum(dg1)


def _in_proj_bwd(dqkv, dgates, x, dh, g1, w_in, tm, out_sums):
    t = x.shape[0]
    n = t // tm
    n_cover = max(n // 2, 1)
    n_steps = n + n_cover
    n_far = len(CHIP_FLIPS)
    shard = (IN_SHARD, D_MODEL)

    def body(dq_ref, dg_ref, x_ref, dh_ref, g1_ref, w_ref, osums_ref,
             dx_ref, own_ref, sib_ref, far_ref, dg1_ref, oarrived_ref,
             acc_ref, send_buf, land_buf, pair_buf, d2d_send, d2d_recv, ici_send, ici_recv, o_send, o_recv):
        i = pl.program_id(0)
        x_pos, y_pos, c = _mesh_pos()
        my_chip = 2 * x_pos + y_pos
        sibling = (x_pos, y_pos, 1 - c)
        @pl.when(i == 0)
        def _():
            _enter_with(_sibling_and_chips(x_pos, y_pos, c))

        _chip_exchange_beside(i == 0, i == n_steps - 1, [osums_ref], [oarrived_ref], (o_send, o_recv), enter=False)

        def rows(d):
            return slice(IN_SHARD * d, IN_SHARD * (d + 1))

        def hand_over(chip):
            return _push(send_buf.at[chip], land_buf.at[chip], (d2d_send, d2d_recv), chip, sibling)

        def to_chip(chip, rel):
            return pltpu.make_async_remote_copy(
                src_ref=pair_buf.at[chip], dst_ref=far_ref.at[rel - 1], send_sem=ici_send.at[rel - 1],
                recv_sem=ici_recv.at[rel - 1], device_id=(chip // 2, chip % 2, c), device_id_type=MESH)

        @pl.when(i == 0)
        def _():
            acc_ref[...] = jnp.zeros_like(acc_ref)
            dg1_ref[...] = jnp.zeros_like(dg1_ref)

        def normed_x():
            xv = x_ref[...]
            r = _inv_rms(xv)
            return xv * r, r

        @pl.when(i < n)
        def _():
            hn = (normed_x()[0] * g1_ref[...]).astype(BF16)
            acc_ref[:QKV_W, :] += _mm_tn(dq_ref[...], hn)
            acc_ref[QKV_W:, :] += _mm_tn(dg_ref[...], hn)

        @pl.when(i == n - 1)
        def _():
            for d in range(N_DEV):
                @pl.when(d % 2 != c)
                def _():
                    send_buf[d // 2] = acc_ref[rows(d), :].astype(BF16)
                    hand_over(d // 2).start()
            for d in range(N_DEV):
                chip = d // 2

                @pl.when(d % 2 == c)
                def _():
                    hand_over(chip).wait_recv()

                    @pl.when(chip == my_chip)
                    def _():
                        own_ref[...] = acc_ref[rows(d), :]
                        sib_ref[...] = land_buf[chip]

                    @pl.when(chip != my_chip)
                    def _():
                        pair_buf[chip] = (acc_ref[rows(d), :] + land_buf[chip].astype(F32)).astype(BF16)
                        to_chip(chip, chip ^ my_chip).start()
            for chip in range(N_CHIPS):
                hand_over(chip).wait_send()

        @pl.when(i >= n)
        def _():
            x_hat, r = normed_x()
            dx_ref[...], dg1 = _grad_x_tile(dq_ref[...], dg_ref[...], x_hat, r, g1_ref[...], w_ref, dh_ref[...])
            dg1_ref[0:1, :] += dg1

        @pl.when(i == n_steps - 1)
        def _():
            for rel in range(1, n_far + 1):
                to_chip(0, rel).wait()

    both = lambda w_: pl.BlockSpec((tm, w_), lambda i: (i % n, 0))
    second = pl.BlockSpec((tm, D_MODEL), lambda i: (jnp.maximum(i - n, 0), 0))
    whole = lambda dtype: jax.ShapeDtypeStruct(shard, dtype)
    sems = lambda k: pltpu.SemaphoreType.DMA((k,))
    res = pl.pallas_call(
        body, name="in_proj_bwd", grid=(n_steps,),
        in_specs=[both(QKV_W), both(GATES_W), both(D_MODEL), second, _full((1, D_MODEL)), _resident((IN_COLS, D_MODEL)),
                  HBM_SPEC],
        out_specs=[second, _full(shard), _full(shard), HBM_SPEC, _full((SMALL_ROWS, D_MODEL)), HBM_SPEC],
        out_shape=[jax.ShapeDtypeStruct((n_cover * tm, D_MODEL), F32), whole(F32), whole(BF16),
                   jax.ShapeDtypeStruct((n_far,) + shard, BF16), jax.ShapeDtypeStruct((SMALL_ROWS, D_MODEL), F32),
                   jax.ShapeDtypeStruct(out_sums.shape, out_sums.dtype)],
        scratch_shapes=[pltpu.VMEM((IN_COLS, D_MODEL), F32), pltpu.VMEM((N_CHIPS,) + shard, BF16),
                        pltpu.VMEM((N_CHIPS,) + shard, BF16), pltpu.VMEM((N_CHIPS,) + shard, BF16),
                        sems(N_CHIPS), sems(N_CHIPS), sems(n_far), sems(n_far), sems(n_far), sems(n_far)],
        compiler_params=_params("arbitrary", barrier_id=7),
    )(dqkv, dgates, x, dh, g1, w_in, out_sums)
    return res[0], (res[1], res[2], res[3]), res[4], res[5]


def _grad_x_rest(dqkv, dgates, x, dh, g1, w_in, tm, head, dg1_rows):
    t = x.shape[0]
    first = head.shape[0] // tm
    n_rest = t // tm - first
    if n_rest == 0:
        return head, dg1_rows
    assert first <= n_rest

    def body(dq_ref, dg_ref, x_ref, dh_ref, g1_ref, w_ref, head_ref, rows_ref, gx_ref, dg1_ref, stage, sems):
        j = pl.program_id(0)

        def tile_out(step, kind):
            row0 = (step + first) * tm if kind == 0 else step * tm
            slot = 2 * kind + step % 2
            return pltpu.make_async_copy(stage.at[slot], gx_ref.at[pl.ds(pl.multiple_of(row0, tm), tm), :], sems.at[slot])

        @pl.when(j == 0)
        def _():
            dg1_ref[...] = rows_ref[...]

        @pl.when(j >= 2)
        def _():
            tile_out(j - 2, 0).wait()

        @pl.when((j >= 2) & (j - 2 < first))
        def _():
            tile_out(j - 2, 1).wait()

        @pl.when(j < first)
        def _():
            stage[2 + j % 2] = head_ref[...]
            tile_out(j, 1).start()

        xv = x_ref[...]
        r = _inv_rms(xv)
        dx, dg1 = _grad_x_tile(dq_ref[...], dg_ref[...], xv * r, r, g1_ref[...], w_ref, dh_ref[...])
        stage[j % 2] = dx
        dg1_ref[0:1, :] += dg1
        tile_out(j, 0).start()

        @pl.when(j == n_rest - 1)
        def _():
            for back in range(min(2, n_rest)):
                tile_out(j - back, 0).wait()

                @pl.when(j - back < first)
                def _():
                    tile_out(j - back, 1).wait()

    tile = lambda w_: pl.BlockSpec((tm, w_), lambda j: (j + first, 0))
    head_tile = pl.BlockSpec((tm, D_MODEL), lambda j: (jnp.minimum(j, first - 1), 0))
    return pl.pallas_call(
        body, name="grad_x_rest", grid=(n_rest,),
        in_specs=[tile(QKV_W), tile(GATES_W), tile(D_MODEL), tile(D_MODEL), _full((1, D_MODEL)),
                  _resident((IN_COLS, D_MODEL)), head_tile, _full((SMALL_ROWS, D_MODEL))],
        out_specs=[HBM_SPEC, _full((SMALL_ROWS, D_MODEL))],
        out_shape=[jax.ShapeDtypeStruct((t, D_MODEL), F32), jax.ShapeDtypeStruct((SMALL_ROWS, D_MODEL), F32)],
        scratch_shapes=[pltpu.VMEM((4, tm, D_MODEL), F32), pltpu.SemaphoreType.DMA((4,))],
        compiler_params=_params("arbitrary"),
    )(dqkv, dgates, x, dh, g1, w_in, head, dg1_rows)


def _all_gather(shards, name):
    n = len(shards)

    def body(*refs):
        _enter_with(_sibling_and_chips(*_mesh_pos()))
        start, finish = _gather_steps(refs[:n], refs[n:2 * n], *refs[2 * n:])
        start()
        finish()

    return pl.pallas_call(
        body, name=name,
        in_specs=[HBM_SPEC] * n, out_specs=[HBM_SPEC] * n,
        out_shape=[jax.ShapeDtypeStruct((N_DEV,) + s.shape, s.dtype) for s in shards],
        scratch_shapes=[pltpu.SemaphoreType.DMA((7 * n,)), pltpu.SemaphoreType.DMA((7 * n,)),
                        pltpu.SemaphoreType.DMA((n,))],
        compiler_params=_params(barrier_id=8),
    )(*shards)


def _adam_math(w, g, m, v):
    m = ADAM_B1 * m + (1.0 - ADAM_B1) * g
    v = ADAM_B2 * v + (1.0 - ADAM_B2) * (g * g)
    m_hat = m / (1.0 - ADAM_B1 ** ADAM_STEP)
    v_hat = v / (1.0 - ADAM_B2 ** ADAM_STEP)
    delta = -ADAM_LR * (m_hat / (jnp.sqrt(v_hat) + ADAM_EPS) + ADAM_WD * w)
    return delta, m, v


def _adamw_reduced(tensors, n_steps):
    n_far = len(CHIP_FLIPS)

    def body(*refs):
        ins, outs = refs[:6 * len(tensors)], refs[6 * len(tensors):]
        for k in range(len(tensors)):
            w_ref, m_ref, v_ref, own_ref, sib_ref, far_ref = ins[6 * k:6 * k + 6]
            g_ref, d_ref, nm_ref, nv_ref = outs[4 * k:4 * k + 4]
            g = own_ref[...] + sib_ref[...].astype(F32)
            for j in range(n_far):
                g = g + far_ref[j].astype(F32)
            g_ref[...] = g
            d_ref[...], nm_ref[...], nv_ref[...] = _adam_math(w_ref[...], g, m_ref[...], v_ref[...])

    in_specs, out_specs, out_shape = [], [], []
    for w, *_ in tensors:
        rows, cols = w.shape
        tile = pl.BlockSpec((rows // n_steps, cols), lambda i: (i, 0))
        in_specs += [tile] * 5 + [pl.BlockSpec((n_far, rows // n_steps, cols), lambda i: (0, i, 0))]
        out_specs += [tile] * 4
        out_shape += [jax.ShapeDtypeStruct((rows, cols), F32)] * 4
    res = pl.pallas_call(
        body, name="adamw_reduced", grid=(n_steps,), in_specs=in_specs, out_specs=out_specs, out_shape=out_shape,
        compiler_params=_params("parallel"),
    )(*[a for tensor in tensors for a in tensor])
    return [res[4 * k:4 * k + 4] for k in range(len(tensors))]


SMALL_PARAMS = ("pre_mix_norm", "post_mix_norm", "pre_mlp_norm", "post_mlp_norm", "attn_group_norm", "conv_group_norm",
                "conv_w", "attn_sinks")


SMALL_WIDTHS = (D_MODEL, CONV_W, ATTN_W, 128, D_MODEL)


def _small_tail(gathered, dev, weights, first_moments, second_moments):
    n = len(SMALL_PARAMS)
    conv_shard = CONV_W // N_DEV

    def body(dev_ref, sums_ref, *refs):
        w_refs, m_refs, v_refs = refs[:n], refs[n:2 * n], refs[2 * n:3 * n]
        loss_ref, outs = refs[3 * n], refs[3 * n + 1:]
        total = sums_ref[0]
        for d in range(1, N_DEV):
            total = total + sums_ref[d]
        starts = [sum(SMALL_WIDTHS[:i]) for i in range(len(SMALL_WIDTHS))]
        mid, conv, gain, sink, inp = (total[:, a:a + w_] for a, w_ in zip(starts, SMALL_WIDTHS))
        loss_ref[...] = (0.5 / D_MODEL) * jnp.sum(mid[ROW_LOSS:ROW_LOSS + 1, :], axis=1, keepdims=True)
        conv_rows = conv[ROW_CW0:ROW_CW0 + 3, :]
        conv_g = jnp.zeros((3, conv_shard), F32)
        for d in range(N_DEV):
            conv_g = conv_g + jnp.where(dev_ref[0] == d, conv_rows[:, conv_shard * d:conv_shard * (d + 1)], 0.0)
        grads = [inp[0:1, :], mid[ROW_G2:ROW_G2 + 1, :], mid[ROW_G3:ROW_G3 + 1, :], mid[ROW_G4:ROW_G4 + 1, :],
                 gain[0:1, :], conv[ROW_GCONV:ROW_GCONV + 1, :], conv_g, sink[0:1, :N_HEADS]]
        for i, g in enumerate(grads):
            parts = [(..., g)] if len(w_refs[i].shape) == 2 else [(r, g[r:r + 1, :]) for r in range(g.shape[0])]
            for at, g_at in parts:
                delta, new_m, new_v = _adam_math(w_refs[i][at], g_at, m_refs[i][at], v_refs[i][at])
                outs[i][at], outs[n + i][at], outs[2 * n + i][at], outs[3 * n + i][at] = g_at, delta, new_m, new_v

    params = list(weights) + list(first_moments) + list(second_moments)
    shapes = [jax.ShapeDtypeStruct(w.shape, F32) for w in weights]
    res = pl.pallas_call(
        body, name="small_tail", grid=(1,),
        in_specs=[pl.BlockSpec(memory_space=pltpu.SMEM), _full(gathered.shape)] + [_full(p.shape) for p in params],
        out_specs=[_full((1, 1))] + [_full(sh.shape) for sh in shapes] * 4,
        out_shape=[jax.ShapeDtypeStruct((1, 1), F32)] + shapes * 4,
    )(dev, gathered, *params)
    return res[0], [res[1 + k * n:1 + (k + 1) * n] for k in range(4)]


TOKEN_TILE = 512
MID_TILE = 256
MID_CHUNK = 1024
MID_CHUNKS = D_FF // MID_CHUNK
ADAM_STEPS = 2


def _local_grads(x, target, g1, w_in_shard, conv_shard, sinks, g_attn, g_conv, g2, g3, g4, shards, order):
    t = x.shape[0]
    tm = min(TOKEN_TILE, t)
    rope = _rope_tables(t)
    qkv, gates, mconv, w_in, conv_w, gathered = _in_proj_fwd(x, g1, w_in_shard, conv_shard, g_conv, rope, tm, shards,
                                                             (False, True, False))
    attn, mattn, (w_out, w_up, w_down) = _attn_fwd(qkv, sinks, g_attn, shards, gathered)
    act, dup, hn2t, dmo, dmix, dh, dmixed, small_mid = _mid(
        mattn, mconv, x, target, g2, g3, g4, w_out.reshape(D_MODEL, D_MODEL),
        w_up, w_down.reshape(D_FF, D_MODEL), min(MID_TILE, t))
    up_own, up_sib, up_sums = _dw_pair_sums((hn2t, dup), order, "up", "dw_up", 2)
    down_own, down_sib, down_sums, up_far = _dw_pair_sums((act, dmo), order, "down", "dw_down", 3, ride=up_sums)
    out_own, out_sib, out_sums = _dw_pair_sums((mattn, mconv, dmix), order, "out", "dw_out", 4)
    dgates, small_conv = _conv_bwd(dmixed, gates, g_conv, conv_w, min(2 * TOKEN_TILE, t))
    dqkv, dsink, dg_attn, down_far = _attn_bwd(qkv, dmixed, attn, g_attn, sinks, rope, down_sums)
    grad_x_head, dw_in, small_in, out_far = _in_proj_bwd(dqkv, dgates, x, dh, g1, w_in, tm, out_sums)
    grad_x, small_in = _grad_x_rest(dqkv, dgates, x, dh, g1, w_in, tm, grad_x_head, small_in)
    dw_out, dw_up, dw_down = (out_own, out_sib, out_far), (up_own, up_sib, up_far), (down_own, down_sib, down_far)
    return grad_x, dw_in, dw_out, dw_up, dw_down, (small_mid, small_conv, dg_attn, dsink, small_in)


def kernel(x, pre_mix_norm, w_in, conv_w, attn_sinks, attn_group_norm, conv_group_norm, w_out, post_mix_norm, pre_mlp_norm, w_up, w_down, post_mlp_norm, loss_target, m_pre_mix_norm, m_w_in, m_conv_w, m_attn_sinks, m_attn_group_norm, m_conv_group_norm, m_w_out, m_post_mix_norm, m_pre_mlp_norm, m_w_up, m_w_down, m_post_mlp_norm, v_pre_mix_norm, v_w_in, v_conv_w, v_attn_sinks, v_attn_group_norm, v_conv_group_norm, v_w_out, v_post_mix_norm, v_pre_mlp_norm, v_w_up, v_w_down, v_post_mlp_norm):
    xi, yi, ci = _mesh_pos()
    chip = 2 * xi + yi
    dev = 2 * chip + ci

    order = _block_order(dev)

    shards = [w_out[0].astype(BF16), w_up[0].astype(BF16), w_down[0].astype(BF16)]

    turned = lambda a: jnp.swapaxes(a, 1, 2)
    grad_x, dw_in, dw_out, dw_up, dw_down, smalls = _local_grads(
        x[0], loss_target[0], pre_mix_norm, turned(w_in)[0].astype(BF16), conv_w[0], attn_sinks, attn_group_norm, conv_group_norm,
        post_mix_norm, pre_mlp_norm, post_mlp_norm, shards, order)

    blocks = {"w_in": (turned(w_in), turned(m_w_in), turned(v_w_in), dw_in), "w_out": (w_out, m_w_out, v_w_out, dw_out),
              "w_up": (w_up, m_w_up, v_w_up, dw_up), "w_down": (w_down, m_w_down, v_w_down, dw_down)}
    big = {}
    for names in (("w_in", "w_out", "w_down"), ("w_up",)):
        stepped = _adamw_reduced([(w[0], m[0], v[0], *dw) for w, m, v, dw in (blocks[nm] for nm in names)], ADAM_STEPS)
        for nm, res in zip(names, stepped):
            big[nm] = [a[None] for a in res]
    big["w_in"] = [turned(a) for a in big["w_in"]]

    flat = lambda a: a.reshape(-1, a.shape[-1]) if a.ndim < 3 else a.reshape(a.shape[1], 1, a.shape[2])
    loss, small = _small_tail(
        _all_gather([jnp.concatenate(smalls, axis=1)], "gather_small")[0], dev.reshape(1).astype(jnp.int32),
        [flat(a) for a in (pre_mix_norm, post_mix_norm, pre_mlp_norm, post_mlp_norm, attn_group_norm, conv_group_norm,
                           conv_w, attn_sinks)],
        [flat(a) for a in (m_pre_mix_norm, m_post_mix_norm, m_pre_mlp_norm, m_post_mlp_norm, m_attn_group_norm,
                           m_conv_group_norm, m_conv_w, m_attn_sinks)],
        [flat(a) for a in (v_pre_mix_norm, v_post_mix_norm, v_pre_mlp_norm, v_post_mlp_norm, v_attn_group_norm,
                           v_conv_group_norm, v_conv_w, v_attn_sinks)])

    order = ("pre_mix_norm", "w_in", "conv_w", "attn_sinks", "attn_group_norm", "conv_group_norm", "w_out",
             "post_mix_norm", "pre_mlp_norm", "w_up", "w_down", "post_mlp_norm")
    shape_of = {"conv_w": conv_w.shape}
    outs = []
    for k in range(4):
        by_name = dict(zip(SMALL_PARAMS, small[k]))
        outs += [big[nm][k] if nm in big else by_name[nm].reshape(shape_of.get(nm, by_name[nm].shape)) for nm in order]
    loss = loss.reshape(())
    return (loss, grad_x[None], *outs)
```
